```python
import math
import jax, jax.numpy as jnp
from jax import lax
import numpy as np

D_MODEL = 1024
BATCH = 16
SEQ = 2048
DEPTH = 1

N_META = 16
D_MIX = D_MODEL
N_HEADS = 8
QK_NOPE = 64
QK_ROPE = 32
QK_HEAD = QK_NOPE + QK_ROPE
V_HEAD = 64
D_ATTN = N_HEADS * V_HEAD
Q_LORA = 384
KV_LORA = 256
D_RNN = D_MIX - D_ATTN
RNN_BLOCKS = 8
RNN_BW = D_RNN // RNN_BLOCKS
CONV_W = 4
CONV_PAD = (2, 1)
LRU_C = 8.0
ROPE_THETA = 10000.0
Q_BLOCK = 128
OFF_CQ = Q_LORA
OFF_CKV = OFF_CQ + KV_LORA
OFF_KR = OFF_CKV + QK_ROPE
OFF_XR = OFF_KR + D_RNN
IN_COLS = OFF_XR + D_RNN
D_FF = int(math.ceil(8 * D_MODEL / 3 / 256) * 256)
EPS = 1e-6

kernel_name = "hymba_mla_rglru_hybrid_encoder"


def rms_norm(x, g):
    xf = x.astype(jnp.float32)
    y = xf * lax.rsqrt(jnp.mean(xf * xf, axis=-1, keepdims=True) + EPS)
    return (y * g.astype(jnp.float32)).astype(x.dtype)


def rope(x, pos):
    half = x.shape[-1] // 2
    freqs = 1.0 / (ROPE_THETA ** (jnp.arange(half, dtype=jnp.float32) / half))
    ang = pos[:, None] * freqs[None, :]
    cos = jnp.cos(ang)[None, :, None, :]
    sin = jnp.sin(ang)[None, :, None, :]
    xf = x.astype(jnp.float32)
    x1, x2 = xf[..., :half], xf[..., half:]
    out = jnp.concatenate([x1 * cos - x2 * sin, x1 * sin + x2 * cos], axis=-1)
    return out.astype(x.dtype)


def attend_block(q_blk, k, v):
    s = jnp.einsum('bhqd,bhkd->bhqk', q_blk, k).astype(jnp.float32) * (QK_HEAD ** -0.5)
    p = jax.nn.softmax(s, axis=-1)
    return jnp.einsum('bhqk,bhkd->bhqd', p.astype(v.dtype), v)


def mla_group(c_q, c_kv, k_r, q_a_g, w_uq, kv_a_g, w_ukv, q_g, k_g, pos):
    B, T, _ = c_q.shape
    q = (rms_norm(c_q, q_a_g) @ w_uq).reshape(B, T, N_HEADS, QK_HEAD)
    kv = (rms_norm(c_kv, kv_a_g) @ w_ukv).reshape(B, T, N_HEADS, QK_NOPE + V_HEAD)
    k_nope, v = kv[..., :QK_NOPE], kv[..., QK_NOPE:]
    k = jnp.concatenate([k_nope, jnp.broadcast_to(k_r[:, :, None, :], (B, T, N_HEADS, QK_ROPE))], axis=-1)
    q = rms_norm(q, q_g)
    k = rms_norm(k, k_g)
    q = jnp.concatenate([q[..., :QK_NOPE], rope(q[..., QK_NOPE:], pos)], axis=-1)
    k = jnp.concatenate([k[..., :QK_NOPE], rope(k[..., QK_NOPE:], pos)], axis=-1)
    q = q.transpose(0, 2, 1, 3)
    k = k.transpose(0, 2, 1, 3)
    v = v.transpose(0, 2, 1, 3)
    o_meta = attend_block(q[:, :, :N_META], k, v)
    q_real = q[:, :, N_META:]
    n_blk = q_real.shape[2] // Q_BLOCK
    q_blocks = q_real.reshape(B, N_HEADS, n_blk, Q_BLOCK, QK_HEAD).transpose(2, 0, 1, 3, 4)
    o_blocks = lax.map(lambda qb: attend_block(qb, k, v), q_blocks)
    o_real = o_blocks.transpose(1, 2, 0, 3, 4).reshape(B, N_HEADS, n_blk * Q_BLOCK, V_HEAD)
    o = jnp.concatenate([o_meta, o_real], axis=2)
    return o.transpose(0, 2, 1, 3).reshape(B, T, D_ATTN)


def _linear_combine(c1, c2):
    a1, b1 = c1
    a2, b2 = c2
    return a1 * a2, a2 * b1 + b2


def rg_lru(xc, wa, ba, wi, bi, lam, reverse):
    B, T, _ = xc.shape
    xg = xc.reshape(B, T, RNN_BLOCKS, RNN_BW)
    r = jax.nn.sigmoid((jnp.einsum('btgi,gij->btgj', xg, wa).reshape(B, T, D_RNN) + ba).astype(jnp.float32))
    i = jax.nn.sigmoid((jnp.einsum('btgi,gij->btgj', xg, wi).reshape(B, T, D_RNN) + bi).astype(jnp.float32))
    log_a = -LRU_C * r * jax.nn.softplus(-lam.astype(jnp.float32))
    a = jnp.exp(log_a)
    b = jnp.sqrt(jnp.maximum(-jnp.expm1(2.0 * log_a), 0.0)) * (i * xc.astype(jnp.float32))
    _, h = lax.associative_scan(_linear_combine, (a, b), axis=1, reverse=reverse)
    return h.astype(xc.dtype)


def rglru_group(x_r, x_gate, conv_w, conv_b, wa, ba, wi, bi, lam):
    xc = lax.conv_general_dilated(
        x_r, conv_w[:, None, :], window_strides=(1,), padding=[CONV_PAD],
        dimension_numbers=('NWC', 'WIO', 'NWC'), feature_group_count=D_RNN) + conv_b
    y = rg_lru(xc, wa[0], ba[0], wi[0], bi[0], lam[0], reverse=False) \
        + rg_lru(xc, wa[1], ba[1], wi[1], bi[1], lam[1], reverse=True)
    return y * jax.nn.gelu(x_gate)


def _fwd_setup_inputs(seed: int = 0) -> dict:
    key = jax.random.key(seed)
    ks = iter(jax.random.split(key, 40))
    L = DEPTH
    f32 = jnp.float32

    def nrm(shape, fan_in):
        return jax.random.normal(next(ks), shape, f32) * (fan_in ** -0.5)

    def gain(shape):
        return 1.0 + 0.02 * jax.random.normal(next(ks), shape, f32)

    def bias(shape):
        return 0.01 * jax.random.normal(next(ks), shape, f32)

    x = jax.random.normal(next(ks), (BATCH, SEQ, D_MODEL), f32)
    meta_tokens = jax.random.normal(next(ks), (N_META, D_MODEL), f32)
    u = jax.random.uniform(next(ks), (L, 2, D_RNN), f32, 0.9, 0.999)
    s = u ** (1.0 / LRU_C)
    lru_lambda = jnp.log(s) - jnp.log1p(-s)
    return {
        "x": x,
        "meta_tokens": meta_tokens,
        "ln1_g": gain((L, D_MODEL)),
        "w_in": nrm((L, D_MODEL, IN_COLS), D_MODEL),
        "q_a_norm_g": gain((L, Q_LORA)),
        "w_uq": nrm((L, Q_LORA, N_HEADS * QK_HEAD), Q_LORA),
        "kv_a_norm_g": gain((L, KV_LORA)),
        "w_ukv": nrm((L, KV_LORA, N_HEADS * (QK_NOPE + V_HEAD)), KV_LORA),
        "q_norm_g": gain((L, QK_HEAD)),
        "k_norm_g": gain((L, QK_HEAD)),
        "conv_w": nrm((L, CONV_W, D_RNN), CONV_W),
        "conv_b": bias((L, D_RNN)),
        "lru_wa": nrm((L, 2, RNN_BLOCKS, RNN_BW, RNN_BW), RNN_BW),
        "lru_ba": bias((L, 2, D_RNN)),
        "lru_wi": nrm((L, 2, RNN_BLOCKS, RNN_BW, RNN_BW), RNN_BW),
        "lru_bi": bias((L, 2, D_RNN)),
        "lru_lambda": lru_lambda,
        "attn_out_g": gain((L, D_ATTN)),
        "rnn_out_g": gain((L, D_RNN)),
        "w_out": nrm((L, D_MIX, D_MODEL), D_MIX),
        "ln2_g": gain((L, D_MODEL)),
        "w_gate": nrm((L, D_MODEL, D_FF), D_MODEL),
        "w_up": nrm((L, D_MODEL, D_FF), D_MODEL),
        "w_down": nrm((L, D_FF, D_MODEL), D_FF),
    }


def _fwd_reference(x, meta_tokens, ln1_g, w_in, q_a_norm_g, w_uq, kv_a_norm_g, w_ukv,
              q_norm_g, k_norm_g, conv_w, conv_b, lru_wa, lru_ba, lru_wi, lru_bi,
              lru_lambda, attn_out_g, rnn_out_g, w_out, ln2_g, w_gate, w_up, w_down):
    B = x.shape[0]
    meta = jnp.broadcast_to(meta_tokens[None].astype(x.dtype), (B, N_META, x.shape[-1]))
    h = jnp.concatenate([meta, x], axis=1)
    T = h.shape[1]
    pos = jnp.arange(T, dtype=jnp.float32)
    for l in range(DEPTH):
        hn = rms_norm(h, ln1_g[l])
        p = hn @ w_in[l]
        c_q = p[..., :OFF_CQ]
        c_kv = p[..., OFF_CQ:OFF_CKV]
        k_r = p[..., OFF_CKV:OFF_KR]
        x_r = p[..., OFF_KR:OFF_XR]
        x_gate = p[..., OFF_XR:]
        o_attn = mla_group(c_q, c_kv, k_r, q_a_norm_g[l], w_uq[l], kv_a_norm_g[l],
                           w_ukv[l], q_norm_g[l], k_norm_g[l], pos)
        o_rnn = rglru_group(x_r, x_gate, conv_w[l], conv_b[l], lru_wa[l], lru_ba[l],
                            lru_wi[l], lru_bi[l], lru_lambda[l])
        mix = jnp.concatenate([rms_norm(o_attn, attn_out_g[l]), rms_norm(o_rnn, rnn_out_g[l])], axis=-1)
        h = h + mix @ w_out[l]
        hn = rms_norm(h, ln2_g[l])
        h = h + (jax.nn.silu(hn @ w_gate[l]) * (hn @ w_up[l])) @ w_down[l]
    return h[:, N_META:]


import jax as _jax
import jax.numpy as _jnp

TWIN_FORMAT = 'train_step'
FWD_PARAMS = ['x', 'meta_tokens', 'ln1_g', 'w_in', 'q_a_norm_g', 'w_uq', 'kv_a_norm_g', 'w_ukv', 'q_norm_g', 'k_norm_g', 'conv_w', 'conv_b', 'lru_wa', 'lru_ba', 'lru_wi', 'lru_bi', 'lru_lambda', 'attn_out_g', 'rnn_out_g', 'w_out', 'ln2_g', 'w_gate', 'w_up', 'w_down']
TWIN_WEIGHTS = ['meta_tokens', 'ln1_g', 'w_in', 'q_a_norm_g', 'w_uq', 'kv_a_norm_g', 'w_ukv', 'q_norm_g', 'k_norm_g', 'conv_w', 'conv_b', 'lru_wa', 'lru_ba', 'lru_wi', 'lru_bi', 'lru_lambda', 'attn_out_g', 'rnn_out_g', 'w_out', 'ln2_g', 'w_gate', 'w_up', 'w_down']
TWIN_DIFF_INPUT = 'x'
TWIN_INPUTS = ['x', 'meta_tokens', 'ln1_g', 'w_in', 'q_a_norm_g', 'w_uq', 'kv_a_norm_g', 'w_ukv', 'q_norm_g', 'k_norm_g', 'conv_w', 'conv_b', 'lru_wa', 'lru_ba', 'lru_wi', 'lru_bi', 'lru_lambda', 'attn_out_g', 'rnn_out_g', 'w_out', 'ln2_g', 'w_gate', 'w_up', 'w_down', 'loss_target', 'm_meta_tokens', 'm_ln1_g', 'm_w_in', 'm_q_a_norm_g', 'm_w_uq', 'm_kv_a_norm_g', 'm_w_ukv', 'm_q_norm_g', 'm_k_norm_g', 'm_conv_w', 'm_conv_b', 'm_lru_wa', 'm_lru_ba', 'm_lru_wi', 'm_lru_bi', 'm_lru_lambda', 'm_attn_out_g', 'm_rnn_out_g', 'm_w_out', 'm_ln2_g', 'm_w_gate', 'm_w_up', 'm_w_down', 'v_meta_tokens', 'v_ln1_g', 'v_w_in', 'v_q_a_norm_g', 'v_w_uq', 'v_kv_a_norm_g', 'v_w_ukv', 'v_q_norm_g', 'v_k_norm_g', 'v_conv_w', 'v_conv_b', 'v_lru_wa', 'v_lru_ba', 'v_lru_wi', 'v_lru_bi', 'v_lru_lambda', 'v_attn_out_g', 'v_rnn_out_g', 'v_w_out', 'v_ln2_g', 'v_w_gate', 'v_w_up', 'v_w_down']
TWIN_OUTPUTS = ['loss', 'grad_x', 'grad_meta_tokens', 'grad_ln1_g', 'grad_w_in', 'grad_q_a_norm_g', 'grad_w_uq', 'grad_kv_a_norm_g', 'grad_w_ukv', 'grad_q_norm_g', 'grad_k_norm_g', 'grad_conv_w', 'grad_conv_b', 'grad_lru_wa', 'grad_lru_ba', 'grad_lru_wi', 'grad_lru_bi', 'grad_lru_lambda', 'grad_attn_out_g', 'grad_rnn_out_g', 'grad_w_out', 'grad_ln2_g', 'grad_w_gate', 'grad_w_up', 'grad_w_down', 'delta_meta_tokens', 'delta_ln1_g', 'delta_w_in', 'delta_q_a_norm_g', 'delta_w_uq', 'delta_kv_a_norm_g', 'delta_w_ukv', 'delta_q_norm_g', 'delta_k_norm_g', 'delta_conv_w', 'delta_conv_b', 'delta_lru_wa', 'delta_lru_ba', 'delta_lru_wi', 'delta_lru_bi', 'delta_lru_lambda', 'delta_attn_out_g', 'delta_rnn_out_g', 'delta_w_out', 'delta_ln2_g', 'delta_w_gate', 'delta_w_up', 'delta_w_down', 'new_m_meta_tokens', 'new_m_ln1_g', 'new_m_w_in', 'new_m_q_a_norm_g', 'new_m_w_uq', 'new_m_kv_a_norm_g', 'new_m_w_ukv', 'new_m_q_norm_g', 'new_m_k_norm_g', 'new_m_conv_w', 'new_m_conv_b', 'new_m_lru_wa', 'new_m_lru_ba', 'new_m_lru_wi', 'new_m_lru_bi', 'new_m_lru_lambda', 'new_m_attn_out_g', 'new_m_rnn_out_g', 'new_m_w_out', 'new_m_ln2_g', 'new_m_w_gate', 'new_m_w_up', 'new_m_w_down', 'new_v_meta_tokens', 'new_v_ln1_g', 'new_v_w_in', 'new_v_q_a_norm_g', 'new_v_w_uq', 'new_v_kv_a_norm_g', 'new_v_w_ukv', 'new_v_q_norm_g', 'new_v_k_norm_g', 'new_v_conv_w', 'new_v_conv_b', 'new_v_lru_wa', 'new_v_lru_ba', 'new_v_lru_wi', 'new_v_lru_bi', 'new_v_lru_lambda', 'new_v_attn_out_g', 'new_v_rnn_out_g', 'new_v_w_out', 'new_v_ln2_g', 'new_v_w_gate', 'new_v_w_up', 'new_v_w_down']
TWIN_LEAF_KINDS = {'loss': 'loss', 'grad_x': 'grad_x', 'grad_meta_tokens': 'grad_w', 'grad_ln1_g': 'grad_w', 'grad_w_in': 'grad_w', 'grad_q_a_norm_g': 'grad_w', 'grad_w_uq': 'grad_w', 'grad_kv_a_norm_g': 'grad_w', 'grad_w_ukv': 'grad_w', 'grad_q_norm_g': 'grad_w', 'grad_k_norm_g': 'grad_w', 'grad_conv_w': 'grad_w', 'grad_conv_b': 'grad_w', 'grad_lru_wa': 'grad_w', 'grad_lru_ba': 'grad_w', 'grad_lru_wi': 'grad_w', 'grad_lru_bi': 'grad_w', 'grad_lru_lambda': 'grad_w', 'grad_attn_out_g': 'grad_w', 'grad_rnn_out_g': 'grad_w', 'grad_w_out': 'grad_w', 'grad_ln2_g': 'grad_w', 'grad_w_gate': 'grad_w', 'grad_w_up': 'grad_w', 'grad_w_down': 'grad_w', 'delta_meta_tokens': 'delta_w', 'delta_ln1_g': 'delta_w', 'delta_w_in': 'delta_w', 'delta_q_a_norm_g': 'delta_w', 'delta_w_uq': 'delta_w', 'delta_kv_a_norm_g': 'delta_w', 'delta_w_ukv': 'delta_w', 'delta_q_norm_g': 'delta_w', 'delta_k_norm_g': 'delta_w', 'delta_conv_w': 'delta_w', 'delta_conv_b': 'delta_w', 'delta_lru_wa': 'delta_w', 'delta_lru_ba': 'delta_w', 'delta_lru_wi': 'delta_w', 'delta_lru_bi': 'delta_w', 'delta_lru_lambda': 'delta_w', 'delta_attn_out_g': 'delta_w', 'delta_rnn_out_g': 'delta_w', 'delta_w_out': 'delta_w', 'delta_ln2_g': 'delta_w', 'delta_w_gate': 'delta_w', 'delta_w_up': 'delta_w', 'delta_w_down': 'delta_w', 'new_m_meta_tokens': 'new_m', 'new_m_ln1_g': 'new_m', 'new_m_w_in': 'new_m', 'new_m_q_a_norm_g': 'new_m', 'new_m_w_uq': 'new_m', 'new_m_kv_a_norm_g': 'new_m', 'new_m_w_ukv': 'new_m', 'new_m_q_norm_g': 'new_m', 'new_m_k_norm_g': 'new_m', 'new_m_conv_w': 'new_m', 'new_m_conv_b': 'new_m', 'new_m_lru_wa': 'new_m', 'new_m_lru_ba': 'new_m', 'new_m_lru_wi': 'new_m', 'new_m_lru_bi': 'new_m', 'new_m_lru_lambda': 'new_m', 'new_m_attn_out_g': 'new_m', 'new_m_rnn_out_g': 'new_m', 'new_m_w_out': 'new_m', 'new_m_ln2_g': 'new_m', 'new_m_w_gate': 'new_m', 'new_m_w_up': 'new_m', 'new_m_w_down': 'new_m', 'new_v_meta_tokens': 'new_v', 'new_v_ln1_g': 'new_v', 'new_v_w_in': 'new_v', 'new_v_q_a_norm_g': 'new_v', 'new_v_w_uq': 'new_v', 'new_v_kv_a_norm_g': 'new_v', 'new_v_w_ukv': 'new_v', 'new_v_q_norm_g': 'new_v', 'new_v_k_norm_g': 'new_v', 'new_v_conv_w': 'new_v', 'new_v_conv_b': 'new_v', 'new_v_lru_wa': 'new_v', 'new_v_lru_ba': 'new_v', 'new_v_lru_wi': 'new_v', 'new_v_lru_bi': 'new_v', 'new_v_lru_lambda': 'new_v', 'new_v_attn_out_g': 'new_v', 'new_v_rnn_out_g': 'new_v', 'new_v_w_out': 'new_v', 'new_v_ln2_g': 'new_v', 'new_v_w_gate': 'new_v', 'new_v_w_up': 'new_v', 'new_v_w_down': 'new_v'}


def _forward(args):
    return _fwd_reference(*[args[k] for k in FWD_PARAMS])


def _output_shape():
    out = _jax.eval_shape(lambda: _forward(_fwd_setup_inputs(0)))
    return out.shape, out.dtype

N_MICROBATCH = 1
ADAM_LR = 0.001
ADAM_B1 = 0.9
ADAM_B2 = 0.999
ADAM_EPS = 1e-08
ADAM_WD = 0.01
ADAM_STEP = 10
PER_EXAMPLE_BATCH_AXIS = {'x': 0, 'loss_target': 0}
SHARED_INPUTS = []
_WEIGHT_DTYPES = {'meta_tokens': _jnp.float32, 'ln1_g': _jnp.float32, 'w_in': _jnp.float32, 'q_a_norm_g': _jnp.float32, 'w_uq': _jnp.float32, 'kv_a_norm_g': _jnp.float32, 'w_ukv': _jnp.float32, 'q_norm_g': _jnp.float32, 'k_norm_g': _jnp.float32, 'conv_w': _jnp.float32, 'conv_b': _jnp.float32, 'lru_wa': _jnp.float32, 'lru_ba': _jnp.float32, 'lru_wi': _jnp.float32, 'lru_bi': _jnp.float32, 'lru_lambda': _jnp.float32, 'attn_out_g': _jnp.float32, 'rnn_out_g': _jnp.float32, 'w_out': _jnp.float32, 'ln2_g': _jnp.float32, 'w_gate': _jnp.float32, 'w_up': _jnp.float32, 'w_down': _jnp.float32}
MOMENT_SCALE = {'meta_tokens': 4.770893e-02, 'ln1_g': 1.492618e+00, 'w_in': 9.303246e-01, 'q_a_norm_g': 1.345861e+00, 'w_uq': 7.931128e-01, 'kv_a_norm_g': 3.351663e+00, 'w_ukv': 1.313895e+00, 'q_norm_g': 3.151155e+00, 'k_norm_g': 3.011075e+00, 'conv_w': 1.659703e+00, 'conv_b': 4.233726e+01, 'lru_wa': 6.863120e-01, 'lru_ba': 4.859869e-01, 'lru_wi': 1.276493e+00, 'lru_bi': 3.820616e-01, 'lru_lambda': 5.886733e-01, 'attn_out_g': 3.209115e+01, 'rnn_out_g': 4.893382e+01, 'w_out': 1.773317e+00, 'ln2_g': 2.488659e+01, 'w_gate': 3.718241e-01, 'w_up': 2.656580e-01, 'w_down': 3.988154e-01}


def _to_microbatches(a, axis):
    t = _jnp.moveaxis(a, axis, 0)
    t = t.reshape((N_MICROBATCH, t.shape[0] // N_MICROBATCH) + t.shape[1:])
    return _jnp.moveaxis(t, 1, axis + 1)


def setup_inputs(seed: int = 0) -> dict:
    inp = _fwd_setup_inputs(seed)
    key = _jax.random.fold_in(_jax.random.key(seed), 7919)
    shape, _ = _output_shape()
    out = dict(inp)
    out["loss_target"] = _jax.random.normal(_jax.random.fold_in(key, 0), shape, _jnp.float32)
    for i, name in enumerate(TWIN_WEIGHTS):
        w = inp[name].astype(_jnp.float32)
        if MOMENT_SCALE is None:
            s = _jnp.sqrt(_jnp.mean(_jnp.square(w)) + 1e-30)
        else:
            s = MOMENT_SCALE[name]
        km, kv = _jax.random.split(_jax.random.fold_in(key, i + 1))
        out[name] = w
        out["m_" + name] = s * _jax.random.normal(km, w.shape, _jnp.float32)
        out["v_" + name] = (s * s) * _jax.random.uniform(kv, w.shape, _jnp.float32, 0.5, 1.5)
    if N_MICROBATCH > 1:
        for name, axis in PER_EXAMPLE_BATCH_AXIS.items():
            out[name] = _to_microbatches(out[name], axis)
    return {'x': out['x'], 'meta_tokens': out['meta_tokens'], 'ln1_g': out['ln1_g'], 'w_in': out['w_in'], 'q_a_norm_g': out['q_a_norm_g'], 'w_uq': out['w_uq'], 'kv_a_norm_g': out['kv_a_norm_g'], 'w_ukv': out['w_ukv'], 'q_norm_g': out['q_norm_g'], 'k_norm_g': out['k_norm_g'], 'conv_w': out['conv_w'], 'conv_b': out['conv_b'], 'lru_wa': out['lru_wa'], 'lru_ba': out['lru_ba'], 'lru_wi': out['lru_wi'], 'lru_bi': out['lru_bi'], 'lru_lambda': out['lru_lambda'], 'attn_out_g': out['attn_out_g'], 'rnn_out_g': out['rnn_out_g'], 'w_out': out['w_out'], 'ln2_g': out['ln2_g'], 'w_gate': out['w_gate'], 'w_up': out['w_up'], 'w_down': out['w_down'], 'loss_target': out['loss_target'], 'm_meta_tokens': out['m_meta_tokens'], 'm_ln1_g': out['m_ln1_g'], 'm_w_in': out['m_w_in'], 'm_q_a_norm_g': out['m_q_a_norm_g'], 'm_w_uq': out['m_w_uq'], 'm_kv_a_norm_g': out['m_kv_a_norm_g'], 'm_w_ukv': out['m_w_ukv'], 'm_q_norm_g': out['m_q_norm_g'], 'm_k_norm_g': out['m_k_norm_g'], 'm_conv_w': out['m_conv_w'], 'm_conv_b': out['m_conv_b'], 'm_lru_wa': out['m_lru_wa'], 'm_lru_ba': out['m_lru_ba'], 'm_lru_wi': out['m_lru_wi'], 'm_lru_bi': out['m_lru_bi'], 'm_lru_lambda': out['m_lru_lambda'], 'm_attn_out_g': out['m_attn_out_g'], 'm_rnn_out_g': out['m_rnn_out_g'], 'm_w_out': out['m_w_out'], 'm_ln2_g': out['m_ln2_g'], 'm_w_gate': out['m_w_gate'], 'm_w_up': out['m_w_up'], 'm_w_down': out['m_w_down'], 'v_meta_tokens': out['v_meta_tokens'], 'v_ln1_g': out['v_ln1_g'], 'v_w_in': out['v_w_in'], 'v_q_a_norm_g': out['v_q_a_norm_g'], 'v_w_uq': out['v_w_uq'], 'v_kv_a_norm_g': out['v_kv_a_norm_g'], 'v_w_ukv': out['v_w_ukv'], 'v_q_norm_g': out['v_q_norm_g'], 'v_k_norm_g': out['v_k_norm_g'], 'v_conv_w': out['v_conv_w'], 'v_conv_b': out['v_conv_b'], 'v_lru_wa': out['v_lru_wa'], 'v_lru_ba': out['v_lru_ba'], 'v_lru_wi': out['v_lru_wi'], 'v_lru_bi': out['v_lru_bi'], 'v_lru_lambda': out['v_lru_lambda'], 'v_attn_out_g': out['v_attn_out_g'], 'v_rnn_out_g': out['v_rnn_out_g'], 'v_w_out': out['v_w_out'], 'v_ln2_g': out['v_ln2_g'], 'v_w_gate': out['v_w_gate'], 'v_w_up': out['v_w_up'], 'v_w_down': out['v_w_down']}


def _loss(weights, diff, rest, loss_target):
    with _jax.named_scope("forward"):
        args = {**rest, TWIN_DIFF_INPUT: diff, **{k: w.astype(_WEIGHT_DTYPES[k]) for k, w in weights.items()}}
        y = _forward(args)
    with _jax.named_scope("loss_head"):
        err = _jnp.square(y.astype(_jnp.float32) - loss_target)
        return 0.5 * _jnp.sum(_jnp.mean(err, axis=-1)) if err.ndim else 0.5 * err


def _adamw(w, g, m, v):
    m = ADAM_B1 * m + (1.0 - ADAM_B1) * g
    v = ADAM_B2 * v + (1.0 - ADAM_B2) * _jnp.square(g)
    m_hat = m / (1.0 - ADAM_B1 ** ADAM_STEP)
    v_hat = v / (1.0 - ADAM_B2 ** ADAM_STEP)
    delta = -ADAM_LR * (m_hat / (_jnp.sqrt(v_hat) + ADAM_EPS) + ADAM_WD * w)
    return delta, m, v


def reference(x, meta_tokens, ln1_g, w_in, q_a_norm_g, w_uq, kv_a_norm_g, w_ukv, q_norm_g, k_norm_g, conv_w, conv_b, lru_wa, lru_ba, lru_wi, lru_bi, lru_lambda, attn_out_g, rnn_out_g, w_out, ln2_g, w_gate, w_up, w_down, loss_target, m_meta_tokens, m_ln1_g, m_w_in, m_q_a_norm_g, m_w_uq, m_kv_a_norm_g, m_w_ukv, m_q_norm_g, m_k_norm_g, m_conv_w, m_conv_b, m_lru_wa, m_lru_ba, m_lru_wi, m_lru_bi, m_lru_lambda, m_attn_out_g, m_rnn_out_g, m_w_out, m_ln2_g, m_w_gate, m_w_up, m_w_down, v_meta_tokens, v_ln1_g, v_w_in, v_q_a_norm_g, v_w_uq, v_kv_a_norm_g, v_w_ukv, v_q_norm_g, v_k_norm_g, v_conv_w, v_conv_b, v_lru_wa, v_lru_ba, v_lru_wi, v_lru_bi, v_lru_lambda, v_attn_out_g, v_rnn_out_g, v_w_out, v_ln2_g, v_w_gate, v_w_up, v_w_down):
    given = dict(x=x, meta_tokens=meta_tokens, ln1_g=ln1_g, w_in=w_in, q_a_norm_g=q_a_norm_g, w_uq=w_uq, kv_a_norm_g=kv_a_norm_g, w_ukv=w_ukv, q_norm_g=q_norm_g, k_norm_g=k_norm_g, conv_w=conv_w, conv_b=conv_b, lru_wa=lru_wa, lru_ba=lru_ba, lru_wi=lru_wi, lru_bi=lru_bi, lru_lambda=lru_lambda, attn_out_g=attn_out_g, rnn_out_g=rnn_out_g, w_out=w_out, ln2_g=ln2_g, w_gate=w_gate, w_up=w_up, w_down=w_down, loss_target=loss_target, m_meta_tokens=m_meta_tokens, m_ln1_g=m_ln1_g, m_w_in=m_w_in, m_q_a_norm_g=m_q_a_norm_g, m_w_uq=m_w_uq, m_kv_a_norm_g=m_kv_a_norm_g, m_w_ukv=m_w_ukv, m_q_norm_g=m_q_norm_g, m_k_norm_g=m_k_norm_g, m_conv_w=m_conv_w, m_conv_b=m_conv_b, m_lru_wa=m_lru_wa, m_lru_ba=m_lru_ba, m_lru_wi=m_lru_wi, m_lru_bi=m_lru_bi, m_lru_lambda=m_lru_lambda, m_attn_out_g=m_attn_out_g, m_rnn_out_g=m_rnn_out_g, m_w_out=m_w_out, m_ln2_g=m_ln2_g, m_w_gate=m_w_gate, m_w_up=m_w_up, m_w_down=m_w_down, v_meta_tokens=v_meta_tokens, v_ln1_g=v_ln1_g, v_w_in=v_w_in, v_q_a_norm_g=v_q_a_norm_g, v_w_uq=v_w_uq, v_kv_a_norm_g=v_kv_a_norm_g, v_w_ukv=v_w_ukv, v_q_norm_g=v_q_norm_g, v_k_norm_g=v_k_norm_g, v_conv_w=v_conv_w, v_conv_b=v_conv_b, v_lru_wa=v_lru_wa, v_lru_ba=v_lru_ba, v_lru_wi=v_lru_wi, v_lru_bi=v_lru_bi, v_lru_lambda=v_lru_lambda, v_attn_out_g=v_attn_out_g, v_rnn_out_g=v_rnn_out_g, v_w_out=v_w_out, v_ln2_g=v_ln2_g, v_w_gate=v_w_gate, v_w_up=v_w_up, v_w_down=v_w_down)
    weights = {n: given[n] for n in TWIN_WEIGHTS}
    shared = {n: given[n] for n in SHARED_INPUTS}
    per_example = {n: given[n] for n in ['x']}
    grad_fn = _jax.value_and_grad(_loss, argnums=(0, 1))

    def one_microbatch(ex, loss_target):
        ex = dict(ex)
        diff = ex.pop(TWIN_DIFF_INPUT)
        return grad_fn(weights, diff, {**shared, **ex}, loss_target)

    if N_MICROBATCH == 1:
        loss, (grad_w, grad_x) = one_microbatch(per_example, given["loss_target"])
    else:
        def body(carry, xs):
            loss_sum, grad_sum = carry
            l_k, (gw_k, gx_k) = one_microbatch(xs[0], xs[1])
            with _jax.named_scope("update"):
                return (loss_sum + l_k, _jax.tree.map(_jnp.add, grad_sum, gw_k)), gx_k

        init = (_jnp.zeros((), _jnp.float32), _jax.tree.map(_jnp.zeros_like, weights))
        (loss, grad_w), grad_x = _jax.lax.scan(body, init, (per_example, given["loss_target"]))
    with _jax.named_scope("update"):
        delta_w, new_m, new_v = {}, {}, {}
        for n in TWIN_WEIGHTS:
            delta_w[n], new_m[n], new_v[n] = _adamw(weights[n], grad_w[n], given["m_" + n], given["v_" + n])
    return (loss, grad_x, *[grad_w[n] for n in TWIN_WEIGHTS], *[delta_w[n] for n in TWIN_WEIGHTS],
            *[new_m[n] for n in TWIN_WEIGHTS], *[new_v[n] for n in TWIN_WEIGHTS])
```

```python
import functools
import math

import jax
import jax.numpy as jnp
from jax import lax
from jax.experimental import pallas as pl
from jax.experimental.pallas import tpu as pltpu

F32 = jnp.float32
BF16 = jnp.bfloat16

D_MODEL = 1024
N_META = 16
SEQ = 2048
N_HEADS = 8
QK_NOPE = 64
QK_ROPE = 32
QK_HEAD = QK_NOPE + QK_ROPE
V_HEAD = 64
D_ATTN = N_HEADS * V_HEAD
Q_LORA = 384
KV_LORA = 256
D_RNN = 512
RNN_BW = 64
D_FF = 2816
EPS = 1e-6
LRU_C = 8.0
ROPE_THETA = 10000.0
OFF_CKV = Q_LORA + KV_LORA
OFF_KR = OFF_CKV + QK_ROPE
IN_COLS = OFF_KR + 2 * D_RNN

ADAM_LR = 0.001
ADAM_B1 = 0.9
ADAM_B2 = 0.999
ADAM_EPS = 1e-08
ADAM_WD = 0.01
ADAM_STEP = 10

N_DEV = 8
LANES = 128
HEAD_PAD = LANES
PAD_ROWS = LANES - N_META
QP_COLS = N_HEADS * HEAD_PAD
P_COLS = OFF_CKV + 2 * D_RNN + LANES
FF_CHUNK = D_FF // 2
VMEM_LIMIT = 56 * 1024 * 1024
MESH = pl.DeviceIdType.MESH


def _t_pad():
    return PAD_ROWS + N_META + SEQ


def _row_tile(n):
    return 256 if n % 256 == 0 else 128


def _const_spec(shape):
    nd = len(shape)
    return pl.BlockSpec(shape, lambda *_: (0,) * nd, pipeline_mode=pl.Buffered(1))


def _rms(x, d):
    r = lax.rsqrt(jnp.sum(x * x, axis=-1, keepdims=True) * (1.0 / d) + EPS)
    return x * r, r


def _rms_bwd(dy, xhat, r, g, d):
    dxh = dy * g
    return r * (dxh - xhat * (jnp.sum(dxh * xhat, axis=-1, keepdims=True) * (1.0 / d)))


def _colsum(x):
    return jnp.sum(x, axis=0, keepdims=True)


def _dot(a, b):
    return jnp.dot(a, b, preferred_element_type=F32)


def _dot_nt(a, b):
    return lax.dot_general(a, b, (((1,), (1,)), ((), ())), preferred_element_type=F32)


def _dot_tn(a, b):
    return lax.dot_general(a, b, (((0,), (0,)), ((), ())), preferred_element_type=F32)


def _rope(x, c, s1, s2):
    return x * c + pltpu.roll(x, 16, 1) * s1 + pltpu.roll(x, HEAD_PAD - 16, 1) * s2


def _rope_bwd(dy, c, s1, s2):
    return dy * c + pltpu.roll(dy * s1, HEAD_PAD - 16, 1) + pltpu.roll(dy * s2, 16, 1)


def _acc(ref, first, val):
    @pl.when(first)
    def _():
        ref[...] = val

    @pl.when(jnp.logical_not(first))
    def _():
        ref[...] += val


def _in_proj(h0, ln1_g, w_in_p):
    n = h0.shape[0]
    tm = _row_tile(n)

    def body(h_ref, g_ref, w_ref, hn_ref, cq_ref, ckv_ref, xr_ref, xg_ref, kr_ref):
        xhat, _ = _rms(h_ref[...], D_MODEL)
        hn = (xhat * g_ref[...]).astype(BF16)
        hn_ref[...] = hn
        p = _dot(hn, w_ref[...])
        cq_ref[...] = p[:, :Q_LORA]
        ckv_ref[...] = p[:, Q_LORA:OFF_CKV]
        xr_ref[...] = p[:, OFF_CKV:OFF_CKV + D_RNN]
        xg_ref[...] = p[:, OFF_CKV + D_RNN:OFF_CKV + 2 * D_RNN]
        kr_ref[...] = p[:, OFF_CKV + 2 * D_RNN:]

    def row(w):
        return pl.BlockSpec((tm, w), lambda i: (i, 0))

    widths = (D_MODEL, Q_LORA, KV_LORA, D_RNN, D_RNN, LANES)
    return pl.pallas_call(
        body, name="in_proj", grid=(n // tm,),
        in_specs=[row(D_MODEL), _const_spec((1, D_MODEL)), _const_spec((D_MODEL, P_COLS))],
        out_specs=[row(w) for w in widths],
        out_shape=[jax.ShapeDtypeStruct((n, w), BF16 if k == 0 else F32) for k, w in enumerate(widths)],
        compiler_params=pltpu.CompilerParams(dimension_semantics=("parallel",), vmem_limit_bytes=VMEM_LIMIT),
    )(h0, ln1_g, w_in_p)


def _qkv_fwd(cq, ckv, kr, gqa, gkva, w_uq_p, w_uk_p, w_v, qg, kg, rc, rs1, rs2):
    n = cq.shape[0]
    tm = _row_tile(n)

    def body(cq_ref, ckv_ref, kr_ref, gqa_ref, gkva_ref, wuq_ref, wuk_ref, wv_ref, qg_ref, kg_ref,
             c_ref, s1_ref, s2_ref, q_ref, k_ref, v_ref):
        xq, _ = _rms(cq_ref[...], Q_LORA)
        qa = (xq * gqa_ref[...]).astype(BF16)
        q = _dot(qa, wuq_ref[...])
        xkv, _ = _rms(ckv_ref[...], KV_LORA)
        kva = (xkv * gkva_ref[...]).astype(BF16)
        kn = _dot(kva, wuk_ref[...])
        v_ref[...] = _dot(kva, wv_ref[...]).astype(BF16)
        krp = kr_ref[...]
        c, s1, s2 = c_ref[...], s1_ref[...], s2_ref[...]
        for h in range(N_HEADS):
            sl = slice(h * HEAD_PAD, (h + 1) * HEAD_PAD)
            qh, _ = _rms(q[:, sl], QK_HEAD)
            q_ref[:, sl] = _rope(qh * qg_ref[...], c, s1, s2).astype(BF16)
            kh, _ = _rms(kn[:, sl] + krp, QK_HEAD)
            k_ref[:, sl] = _rope(kh * kg_ref[...], c, s1, s2).astype(BF16)

    def row(w):
        return pl.BlockSpec((tm, w), lambda i: (i, 0))

    return pl.pallas_call(
        body, name="qkv_fwd", grid=(n // tm,),
        in_specs=[row(Q_LORA), row(KV_LORA), row(LANES), _const_spec((1, Q_LORA)), _const_spec((1, KV_LORA)),
                  _const_spec((Q_LORA, QP_COLS)), _const_spec((KV_LORA, QP_COLS)), _const_spec((KV_LORA, D_ATTN)),
                  _const_spec((1, LANES)), _const_spec((1, LANES)), row(LANES), row(LANES), row(LANES)],
        out_specs=[row(QP_COLS), row(QP_COLS), row(D_ATTN)],
        out_shape=[jax.ShapeDtypeStruct((n, QP_COLS), BF16), jax.ShapeDtypeStruct((n, QP_COLS), BF16),
                   jax.ShapeDtypeStruct((n, D_ATTN), BF16)],
        compiler_params=pltpu.CompilerParams(dimension_semantics=("parallel",), vmem_limit_bytes=VMEM_LIMIT),
    )(cq, ckv, kr, gqa, gkva, w_uq_p, w_uk_p, w_v, qg, kg, rc, rs1, rs2)


def _qkv_bwd(cq, ckv, kr, dq_r, dk_r, dv, gqa, gkva, w_uq_p, w_uk_p, w_v, qg, kg, rc, rs1, rs2):
    n = cq.shape[0]
    tm = _row_tile(n)

    def body(cq_ref, ckv_ref, kr_ref, dq_ref, dk_ref, dv_ref, gqa_ref, gkva_ref, wuq_ref, wuk_ref, wv_ref,
             qg_ref, kg_ref, c_ref, s1_ref, s2_ref,
             dcq_ref, dckv_ref, dkr_ref, qa_ref, kva_ref, dqp_ref, dkv_ref, dqg_ref, dkg_ref, dgqa_ref, dgkva_ref):
        first = pl.program_id(0) == 0
        xq, rq = _rms(cq_ref[...], Q_LORA)
        qa = (xq * gqa_ref[...]).astype(BF16)
        qa_ref[...] = qa
        q = _dot(qa, wuq_ref[...])
        xkv, rkv = _rms(ckv_ref[...], KV_LORA)
        kva = (xkv * gkva_ref[...]).astype(BF16)
        kva_ref[...] = kva
        kn = _dot(kva, wuk_ref[...])
        krp = kr_ref[...]
        c, s1, s2 = c_ref[...], s1_ref[...], s2_ref[...]
        lane = lax.broadcasted_iota(jnp.int32, (tm, HEAD_PAD), 1)
        rope_lanes = jnp.logical_and(lane >= QK_NOPE, lane < QK_HEAD)
        dqg = jnp.zeros((1, HEAD_PAD), F32)
        dkg = jnp.zeros((1, HEAD_PAD), F32)
        dkr = jnp.zeros((tm, HEAD_PAD), F32)
        for h in range(N_HEADS):
            sl = slice(h * HEAD_PAD, (h + 1) * HEAD_PAD)
            qh, rqh = _rms(q[:, sl], QK_HEAD)
            dy = _rope_bwd(dq_ref[:, sl], c, s1, s2)
            dqg = dqg + _colsum(dy * qh)
            dqp_ref[:, sl] = _rms_bwd(dy, qh, rqh, qg_ref[...], QK_HEAD).astype(BF16)
            kh, rkh = _rms(kn[:, sl] + krp, QK_HEAD)
            dyk = _rope_bwd(dk_ref[:, sl], c, s1, s2)
            dkg = dkg + _colsum(dyk * kh)
            dkh = _rms_bwd(dyk, kh, rkh, kg_ref[...], QK_HEAD)
            dkv_ref[:, sl] = dkh.astype(BF16)
            dkr = dkr + jnp.where(rope_lanes, dkh, 0.0)
        dkv_ref[:, QP_COLS:] = dv_ref[...].astype(BF16)
        dkr_ref[...] = dkr.astype(BF16)
        dqa = _dot_nt(dqp_ref[...], wuq_ref[...])
        dcq_ref[...] = _rms_bwd(dqa, xq, rq, gqa_ref[...], Q_LORA).astype(BF16)
        dkva = _dot_nt(dkv_ref[:, :QP_COLS], wuk_ref[...]) + _dot_nt(dkv_ref[:, QP_COLS:], wv_ref[...])
        dckv_ref[...] = _rms_bwd(dkva, xkv, rkv, gkva_ref[...], KV_LORA).astype(BF16)
        _acc(dqg_ref, first, dqg)
        _acc(dkg_ref, first, dkg)
        _acc(dgqa_ref, first, _colsum(dqa * xq))
        _acc(dgkva_ref, first, _colsum(dkva * xkv))

    def row(w):
        return pl.BlockSpec((tm, w), lambda i: (i, 0))

    def acc(w):
        return pl.BlockSpec((1, w), lambda i: (0, 0))

    return pl.pallas_call(
        body, name="qkv_bwd", grid=(n // tm,),
        in_specs=[row(Q_LORA), row(KV_LORA), row(LANES), row(QP_COLS), row(QP_COLS), row(D_ATTN),
                  _const_spec((1, Q_LORA)), _const_spec((1, KV_LORA)),
                  _const_spec((Q_LORA, QP_COLS)), _const_spec((KV_LORA, QP_COLS)), _const_spec((KV_LORA, D_ATTN)),
                  _const_spec((1, LANES)), _const_spec((1, LANES)), row(LANES), row(LANES), row(LANES)],
        out_specs=[row(Q_LORA), row(KV_LORA), row(LANES), row(Q_LORA), row(KV_LORA), row(QP_COLS),
                   row(QP_COLS + D_ATTN), acc(LANES), acc(LANES), acc(Q_LORA), acc(KV_LORA)],
        out_shape=[jax.ShapeDtypeStruct((n, Q_LORA), BF16), jax.ShapeDtypeStruct((n, KV_LORA), BF16),
                   jax.ShapeDtypeStruct((n, LANES), BF16), jax.ShapeDtypeStruct((n, Q_LORA), BF16),
                   jax.ShapeDtypeStruct((n, KV_LORA), BF16), jax.ShapeDtypeStruct((n, QP_COLS), BF16),
                   jax.ShapeDtypeStruct((n, QP_COLS + D_ATTN), BF16),
                   jax.ShapeDtypeStruct((1, LANES), F32), jax.ShapeDtypeStruct((1, LANES), F32),
                   jax.ShapeDtypeStruct((1, Q_LORA), F32), jax.ShapeDtypeStruct((1, KV_LORA), F32)],
        compiler_params=pltpu.CompilerParams(dimension_semantics=("arbitrary",), vmem_limit_bytes=VMEM_LIMIT),
    )(cq, ckv, kr, dq_r, dk_r, dv, gqa, gkva, w_uq_p, w_uk_p, w_v, qg, kg, rc, rs1, rs2)


def _softmax_parts(qh, kh, tq, t):
    s = _dot_nt(qh, kh) * (QK_HEAD ** -0.5)
    key = lax.broadcasted_iota(jnp.int32, (tq, t), 1)
    s = jnp.where(key >= PAD_ROWS, s, -jnp.inf)
    e = jnp.exp(s - jnp.max(s, axis=-1, keepdims=True))
    return e, jnp.sum(e, axis=-1, keepdims=True)


def _attn_specs(t, tq):
    nq = t // tq
    qspec = pl.BlockSpec((tq, 2 * HEAD_PAD), lambda b, hp, i: (b * nq + i, hp))
    kspec = pl.BlockSpec((t, 2 * HEAD_PAD), lambda b, hp, i: (b, hp))
    vspec = pl.BlockSpec((t, 2 * V_HEAD), lambda b, hp, i: (b, hp))
    ospec = pl.BlockSpec((tq, 2 * V_HEAD), lambda b, hp, i: (b * nq + i, hp))
    return nq, qspec, kspec, vspec, ospec


def _attn_fwd(q, k, v):
    n = q.shape[0]
    t = _t_pad()
    tq = t // 8
    nq, qspec, kspec, vspec, ospec = _attn_specs(t, tq)

    def body(q_ref, k_ref, v_ref, o_ref):
        lane = lax.broadcasted_iota(jnp.int32, (tq, 2 * V_HEAD), 1)
        outs = []
        for j in range(2):
            sl = slice(j * HEAD_PAD, (j + 1) * HEAD_PAD)
            e, l = _softmax_parts(q_ref[:, sl], k_ref[:, sl], tq, t)
            outs.append(_dot(e.astype(BF16), v_ref[...]) / l)
        o_ref[...] = jnp.where(lane < V_HEAD, outs[0], outs[1])

    return pl.pallas_call(
        body, name="attn_fwd", grid=(n // t, N_HEADS // 2, nq),
        in_specs=[qspec, kspec, vspec], out_specs=ospec,
        out_shape=jax.ShapeDtypeStruct((n, D_ATTN), F32),
        compiler_params=pltpu.CompilerParams(dimension_semantics=("parallel", "parallel", "parallel"),
                                             vmem_limit_bytes=VMEM_LIMIT),
    )(q, k, v)


def _attn_bwd(q, k, v, do):
    n = q.shape[0]
    t = _t_pad()
    tq = t // 8
    nq, qspec, kspec, vspec, ospec = _attn_specs(t, tq)

    def body(q_ref, k_ref, v_ref, do_ref, dq_ref, dk_ref, dv_ref):
        first = pl.program_id(2) == 0
        lane = lax.broadcasted_iota(jnp.int32, (tq, 2 * V_HEAD), 1)
        do = do_ref[...]
        dv = jnp.zeros((t, 2 * V_HEAD), F32)
        for j in range(2):
            sl = slice(j * HEAD_PAD, (j + 1) * HEAD_PAD)
            qh, kh = q_ref[:, sl], k_ref[:, sl]
            e, l = _softmax_parts(qh, kh, tq, t)
            p = e / l
            in_head = (lane < V_HEAD) if j == 0 else (lane >= V_HEAD)
            doh = jnp.where(in_head, do, 0.0).astype(BF16)
            dp = _dot_nt(doh, v_ref[...])
            delta = jnp.sum(p * dp, axis=-1, keepdims=True)
            ds = (p * (dp - delta) * (QK_HEAD ** -0.5)).astype(BF16)
            dq_ref[:, sl] = _dot(ds, kh)
            dkh = _dot_tn(ds, qh)

            @pl.when(first)
            def _():
                dk_ref[:, sl] = dkh

            @pl.when(jnp.logical_not(first))
            def _():
                dk_ref[:, sl] += dkh

            dv = dv + _dot_tn(p.astype(BF16), doh)
        _acc(dv_ref, first, dv)

    return pl.pallas_call(
        body, name="attn_bwd", grid=(n // t, N_HEADS // 2, nq),
        in_specs=[qspec, kspec, vspec, ospec], out_specs=[qspec, kspec, vspec],
        out_shape=[jax.ShapeDtypeStruct((n, QP_COLS), F32), jax.ShapeDtypeStruct((n, QP_COLS), F32),
                   jax.ShapeDtypeStruct((n, D_ATTN), F32)],
        compiler_params=pltpu.CompilerParams(dimension_semantics=("parallel", "parallel", "arbitrary"),
                                             vmem_limit_bytes=VMEM_LIMIT),
    )(q, k, v, do)


def _scan(a_ref, b_ref, h_ref, t, reverse):
    groups = t // 8
    rows = lax.broadcasted_iota(jnp.int32, (8, LANES), 0)

    def step(g, carry):
        off = pl.multiple_of((groups - 1 - g if reverse else g) * 8, 8)
        a = a_ref[pl.ds(off, 8), :]
        b = b_ref[pl.ds(off, 8), :]
        for d in (1, 2, 4):
            if reverse:
                keep = rows < 8 - d
                a_n, b_n = pltpu.roll(a, 8 - d, 0), pltpu.roll(b, 8 - d, 0)
            else:
                keep = rows >= d
                a_n, b_n = pltpu.roll(a, d, 0), pltpu.roll(b, d, 0)
            b = a * jnp.where(keep, b_n, 0.0) + b
            a = a * jnp.where(keep, a_n, 1.0)
        h = b + a * carry
        h_ref[pl.ds(off, 8), :] = h
        return h[0:1] if reverse else h[7:8]

    lax.fori_loop(0, groups, step, jnp.zeros((1, LANES), F32))


def _shift_rows(x, s, rows, t):
    if s == 0:
        return x
    rolled = pltpu.roll(x, s % t, 0)
    return jnp.where(rows >= s, rolled, 0.0) if s > 0 else jnp.where(rows < t + s, rolled, 0.0)


def _neg_expm1(x):
    series = -x * (1.0 + x * (0.5 + x * (1.0 / 6 + x * (1.0 / 24 + x * (1.0 / 120 + x * (1.0 / 720))))))
    return jnp.where(x > -0.3, series, 1.0 - jnp.exp(x))


def _gelu_parts(x):
    k = math.sqrt(2.0 / math.pi)
    th = jnp.tanh(k * (x + 0.044715 * x * x * x))
    g = 0.5 * x * (1.0 + th)
    dg = 0.5 * (1.0 + th) + 0.5 * x * (1.0 - th * th) * k * (1.0 + 3 * 0.044715 * x * x)
    return g, dg


def _lru_gates(xc, gates, lam_ref, valid, d):
    r = jax.nn.sigmoid(gates[:, (2 * d) * LANES:(2 * d + 1) * LANES])
    i = jax.nn.sigmoid(gates[:, (2 * d + 1) * LANES:(2 * d + 2) * LANES])
    neg_lam = -lam_ref[d:d + 1, :]
    sp = jnp.maximum(neg_lam, 0.0) + jnp.log1p(jnp.exp(-jnp.abs(neg_lam)))
    log_a = -LRU_C * r * sp
    a = jnp.exp(log_a)
    m = jnp.maximum(_neg_expm1(2.0 * log_a), 0.0)
    sq = jnp.sqrt(m)
    b = jnp.where(valid, sq * (i * xc), 0.0)
    return r, i, sp, a, m, sq, b


def _conv(xr, cw_ref, cb_ref, rows, t):
    return (cw_ref[0:1, :] * _shift_rows(xr, 2, rows, t) + cw_ref[1:2, :] * _shift_rows(xr, 1, rows, t)
            + cw_ref[2:3, :] * xr + cw_ref[3:4, :] * _shift_rows(xr, -1, rows, t) + cb_ref[...])


def _rnn_specs(t):
    seq = pl.BlockSpec((t, LANES), lambda cb, b: (b, cb))
    cw = pl.BlockSpec((4, LANES), lambda cb, b: (0, cb))
    vec1 = pl.BlockSpec((1, LANES), lambda cb, b: (0, cb))
    vec2 = pl.BlockSpec((2, LANES), lambda cb, b: (0, cb))
    wblk = pl.BlockSpec((1, LANES, 4 * LANES), lambda cb, b: (cb, 0, 0))
    gbias = pl.BlockSpec((1, 1, 4 * LANES), lambda cb, b: (cb, 0, 0))
    return seq, cw, vec1, vec2, wblk, gbias


def _rnn_fwd(xr, xg, conv_w, conv_b, wblk, gbias, lam):
    n = xr.shape[0]
    t = _t_pad()
    seq, cw, vec1, vec2, wspec, gspec = _rnn_specs(t)

    def body(xr_ref, xg_ref, cw_ref, cb_ref, w_ref, gb_ref, lam_ref, o_ref, a_s, b_s, h_s):
        rows = lax.broadcasted_iota(jnp.int32, (t, LANES), 0)
        valid = rows >= PAD_ROWS
        xc = _conv(xr_ref[...], cw_ref, cb_ref, rows, t)
        gates = _dot(xc.astype(BF16), w_ref[0]) + gb_ref[0]
        for d in range(2):
            _, _, _, a, _, _, b = _lru_gates(xc, gates, lam_ref, valid, d)
            a_s[...] = a
            b_s[...] = b
            _scan(a_s, b_s, h_s.at[d], t, reverse=(d == 1))
        g, _ = _gelu_parts(xg_ref[...])
        o_ref[...] = (h_s[0] + h_s[1]) * g

    return pl.pallas_call(
        body, name="rnn_fwd", grid=(D_RNN // LANES, n // t),
        in_specs=[seq, seq, cw, vec1, wspec, gspec, vec2], out_specs=seq,
        out_shape=jax.ShapeDtypeStruct((n, D_RNN), F32),
        scratch_shapes=[pltpu.VMEM((t, LANES), F32), pltpu.VMEM((t, LANES), F32), pltpu.VMEM((2, t, LANES), F32)],
        compiler_params=pltpu.CompilerParams(dimension_semantics=("parallel", "parallel"), vmem_limit_bytes=VMEM_LIMIT),
    )(xr, xg, conv_w, conv_b, wblk, gbias, lam)


def _rnn_bwd(xr, xg, do, conv_w, conv_b, wblk, gbias, lam):
    n = xr.shape[0]
    t = _t_pad()
    seq, cw, vec1, vec2, wspec, gspec = _rnn_specs(t)

    def body(xr_ref, xg_ref, do_ref, cw_ref, cb_ref, w_ref, gb_ref, lam_ref,
             dxr_ref, dxg_ref, dcw_ref, dcb_ref, dw_ref, dgb_ref, dlam_ref, a_s, b_s, h_s, l_s, dg_s):
        first = pl.program_id(1) == 0
        rows = lax.broadcasted_iota(jnp.int32, (t, LANES), 0)
        valid = rows >= PAD_ROWS
        xr = xr_ref[...]
        xc = _conv(xr, cw_ref, cb_ref, rows, t)
        xcb = xc.astype(BF16)
        gates = _dot(xcb, w_ref[0]) + gb_ref[0]
        for d in range(2):
            _, _, _, a, _, _, b = _lru_gates(xc, gates, lam_ref, valid, d)
            a_s[...] = a
            b_s[...] = b
            _scan(a_s, b_s, h_s.at[d], t, reverse=(d == 1))
        g, dg = _gelu_parts(xg_ref[...])
        do = do_ref[...]
        dxg_ref[...] = do * (h_s[0] + h_s[1]) * dg
        b_s[...] = do * g
        dxc = jnp.zeros((t, LANES), F32)
        dlams = []
        for d in range(2):
            r, i, sp, a, m, sq, _ = _lru_gates(xc, gates, lam_ref, valid, d)
            toward = -1 if d == 0 else 1
            a_s[...] = _shift_rows(a, toward, rows, t)
            _scan(a_s, b_s, l_s, t, reverse=(d == 0))
            lam_t = l_s[...]
            da = lam_t * _shift_rows(h_s[d], -toward, rows, t)
            lam_v = jnp.where(valid, lam_t, 0.0)
            dsq = lam_v * (i * xc)
            di = lam_v * sq * xc
            dxc = dxc + lam_v * sq * i
            dm = jnp.where(m > 0.0, dsq * 0.5 / jnp.where(m > 0.0, sq, 1.0), 0.0)
            dla = da * a - 2.0 * dm * a * a
            dr = dla * (-LRU_C) * sp
            dsp = _colsum(dla * (-LRU_C) * r)
            dlams.append(dsp * -jax.nn.sigmoid(-lam_ref[d:d + 1, :]))
            dg_s[:, (2 * d) * LANES:(2 * d + 1) * LANES] = (dr * r * (1.0 - r)).astype(BF16)
            dg_s[:, (2 * d + 1) * LANES:(2 * d + 2) * LANES] = (di * i * (1.0 - i)).astype(BF16)
        dgates = dg_s[...]
        dxc = dxc + _dot_nt(dgates, w_ref[0])
        dxr_ref[...] = (cw_ref[0:1, :] * _shift_rows(dxc, -2, rows, t) + cw_ref[1:2, :] * _shift_rows(dxc, -1, rows, t)
                        + cw_ref[2:3, :] * dxc + cw_ref[3:4, :] * _shift_rows(dxc, 1, rows, t))
        dcw = jnp.concatenate([_colsum(dxc * _shift_rows(xr, 2 - j, rows, t)) for j in range(4)], axis=0)
        _acc(dcw_ref, first, dcw)
        _acc(dcb_ref, first, _colsum(dxc))
        _acc(dw_ref, first, _dot_tn(xcb, dgates)[None])
        _acc(dgb_ref, first, _colsum(dgates.astype(F32))[None])
        _acc(dlam_ref, first, jnp.concatenate(dlams, axis=0))

    return pl.pallas_call(
        body, name="rnn_bwd", grid=(D_RNN // LANES, n // t),
        in_specs=[seq, seq, seq, cw, vec1, wspec, gspec, vec2],
        out_specs=[seq, seq, cw, vec1, wspec, gspec, vec2],
        out_shape=[jax.ShapeDtypeStruct((n, D_RNN), F32), jax.ShapeDtypeStruct((n, D_RNN), F32),
                   jax.ShapeDtypeStruct((4, D_RNN), F32), jax.ShapeDtypeStruct((1, D_RNN), F32),
                   jax.ShapeDtypeStruct((D_RNN // LANES, LANES, 4 * LANES), F32),
                   jax.ShapeDtypeStruct((D_RNN // LANES, 1, 4 * LANES), F32), jax.ShapeDtypeStruct((2, D_RNN), F32)],
        scratch_shapes=[pltpu.VMEM((t, LANES), F32), pltpu.VMEM((t, LANES), F32), pltpu.VMEM((2, t, LANES), F32),
                        pltpu.VMEM((t, LANES), F32), pltpu.VMEM((t, 4 * LANES), BF16)],
        compiler_params=pltpu.CompilerParams(dimension_semantics=("parallel", "arbitrary"), vmem_limit_bytes=VMEM_LIMIT),
    )(xr, xg, do, conv_w, conv_b, wblk, gbias, lam)


def _post(oa, orn, h0, tgt, ga, gr, g2, w_out, w_gate, w_up, w_down):
    n = oa.shape[0]
    tm = _row_tile(n)
    t = _t_pad()

    def body(oa_ref, or_ref, h0_ref, tgt_ref, ga_ref, gr_ref, g2_ref, wo_ref, wg_ref, wu_ref, wd_ref,
             doa_ref, dor_ref, dh1_ref, mix_ref, h1n_ref, act_ref, dgate_ref, dup_ref, dy_ref,
             loss_ref, dga_ref, dgr_ref, dg2_ref, gate_s, up_s):
        first = pl.program_id(0) == 0
        xa, ra = _rms(oa_ref[...], D_ATTN)
        xr, rr = _rms(or_ref[...], D_RNN)
        mix_ref[:, :D_ATTN] = (xa * ga_ref[...]).astype(BF16)
        mix_ref[:, D_ATTN:] = (xr * gr_ref[...]).astype(BF16)
        h1 = h0_ref[...] + _dot(mix_ref[...], wo_ref[...])
        x2, r2 = _rms(h1, D_MODEL)
        h1n = (x2 * g2_ref[...]).astype(BF16)
        h1n_ref[...] = h1n
        y = h1
        for cs in range(0, D_FF, FF_CHUNK):
            sl = slice(cs, cs + FF_CHUNK)
            gate = _dot(h1n, wg_ref[:, sl])
            up = _dot(h1n, wu_ref[:, sl])
            gate_s[:, sl] = gate
            up_s[:, sl] = up
            act = (gate * jax.nn.sigmoid(gate) * up).astype(BF16)
            act_ref[:, sl] = act
            y = y + _dot(act, wd_ref[sl, :])
        row = pl.program_id(0) * tm + lax.broadcasted_iota(jnp.int32, (tm, 1), 0)
        for _ in range(1, n // t):
            row = jnp.where(row >= t, row - t, row)
        err = jnp.where(row >= PAD_ROWS + N_META, y - tgt_ref[...], 0.0)
        _acc(loss_ref, first, jnp.full((1, LANES), 0.5 / D_MODEL, F32) * jnp.sum(err * err))
        dy = err * (1.0 / D_MODEL)
        dyb = dy.astype(BF16)
        dy_ref[...] = dyb
        dh1n = jnp.zeros((tm, D_MODEL), F32)
        for cs in range(0, D_FF, FF_CHUNK):
            sl = slice(cs, cs + FF_CHUNK)
            dact = _dot_nt(dyb, wd_ref[sl, :])
            gate, up = gate_s[:, sl], up_s[:, sl]
            sg = jax.nn.sigmoid(gate)
            dgate = (dact * up * sg * (1.0 + gate * (1.0 - sg))).astype(BF16)
            dup = (dact * gate * sg).astype(BF16)
            dgate_ref[:, sl] = dgate
            dup_ref[:, sl] = dup
            dh1n = dh1n + _dot_nt(dgate, wg_ref[:, sl]) + _dot_nt(dup, wu_ref[:, sl])
        _acc(dg2_ref, first, _colsum(dh1n * x2))
        dh1 = dy + _rms_bwd(dh1n, x2, r2, g2_ref[...], D_MODEL)
        dh1_ref[...] = dh1
        dmix = _dot_nt(dh1.astype(BF16), wo_ref[...])
        dma, dmr = dmix[:, :D_ATTN], dmix[:, D_ATTN:]
        _acc(dga_ref, first, _colsum(dma * xa))
        _acc(dgr_ref, first, _colsum(dmr * xr))
        doa_ref[...] = _rms_bwd(dma, xa, ra, ga_ref[...], D_ATTN)
        dor_ref[...] = _rms_bwd(dmr, xr, rr, gr_ref[...], D_RNN)

    def row(w):
        return pl.BlockSpec((tm, w), lambda i: (i, 0))

    def acc(w):
        return pl.BlockSpec((1, w), lambda i: (0, 0))

    outs = [(D_ATTN, F32), (D_RNN, F32), (D_MODEL, F32), (D_MODEL, BF16), (D_MODEL, BF16), (D_FF, BF16),
            (D_FF, BF16), (D_FF, BF16), (D_MODEL, BF16)]
    accs = [LANES, D_ATTN, D_RNN, D_MODEL]
    return pl.pallas_call(
        body, name="post", grid=(n // tm,),
        in_specs=[row(D_ATTN), row(D_RNN), row(D_MODEL), row(D_MODEL),
                  _const_spec((1, D_ATTN)), _const_spec((1, D_RNN)), _const_spec((1, D_MODEL)),
                  _const_spec((D_MODEL, D_MODEL)), _const_spec((D_MODEL, D_FF)), _const_spec((D_MODEL, D_FF)),
                  _const_spec((D_FF, D_MODEL))],
        out_specs=[row(w) for w, _ in outs] + [acc(w) for w in accs],
        out_shape=[jax.ShapeDtypeStruct((n, w), dt) for w, dt in outs]
        + [jax.ShapeDtypeStruct((1, w), F32) for w in accs],
        scratch_shapes=[pltpu.VMEM((tm, D_FF), F32), pltpu.VMEM((tm, D_FF), F32)],
        compiler_params=pltpu.CompilerParams(dimension_semantics=("arbitrary",), vmem_limit_bytes=VMEM_LIMIT),
    )(oa, orn, h0, tgt, ga, gr, g2, w_out, w_gate, w_up, w_down)


def _in_bwd(dcq, dckv, dxr, dxg, dkr, h0, dh1, ln1_g, w_in_p):
    n = h0.shape[0]
    tm = _row_tile(n)

    def body(dcq_ref, dckv_ref, dxr_ref, dxg_ref, dkr_ref, h0_ref, dh1_ref, g_ref, w_ref, dp_ref, dh0_ref, dg_ref):
        first = pl.program_id(0) == 0
        dp_ref[:, :Q_LORA] = dcq_ref[...]
        dp_ref[:, Q_LORA:OFF_CKV] = dckv_ref[...]
        dp_ref[:, OFF_CKV:OFF_CKV + D_RNN] = dxr_ref[...].astype(BF16)
        dp_ref[:, OFF_CKV + D_RNN:OFF_CKV + 2 * D_RNN] = dxg_ref[...].astype(BF16)
        dp_ref[:, OFF_CKV + 2 * D_RNN:] = dkr_ref[...]
        dhn = _dot_nt(dp_ref[...], w_ref[...])
        xhat, r = _rms(h0_ref[...], D_MODEL)
        _acc(dg_ref, first, _colsum(dhn * xhat))
        dh0_ref[...] = dh1_ref[...] + _rms_bwd(dhn, xhat, r, g_ref[...], D_MODEL)

    def row(w):
        return pl.BlockSpec((tm, w), lambda i: (i, 0))

    return pl.pallas_call(
        body, name="in_bwd", grid=(n // tm,),
        in_specs=[row(Q_LORA), row(KV_LORA), row(D_RNN), row(D_RNN), row(LANES), row(D_MODEL), row(D_MODEL),
                  _const_spec((1, D_MODEL)), _const_spec((D_MODEL, P_COLS))],
        out_specs=[row(P_COLS), row(D_MODEL), pl.BlockSpec((1, D_MODEL), lambda i: (0, 0))],
        out_shape=[jax.ShapeDtypeStruct((n, P_COLS), BF16), jax.ShapeDtypeStruct((n, D_MODEL), F32),
                   jax.ShapeDtypeStruct((1, D_MODEL), F32)],
        compiler_params=pltpu.CompilerParams(dimension_semantics=("arbitrary",), vmem_limit_bytes=VMEM_LIMIT),
    )(dcq, dckv, dxr, dxg, dkr, h0, dh1, ln1_g, w_in_p)


def _pick_tile(width, cap):
    best = LANES
    for mult in range(1, width // LANES + 1):
        cand = mult * LANES
        if width % cand == 0 and cand <= cap:
            best = cand
    return best


def _matmul_tn(name, a, b):
    n, ka = a.shape
    kb = b.shape[1]
    ta, tb = _pick_tile(ka, 1408), _pick_tile(kb, 1408)
    tk = n // 4

    def body(a_ref, b_ref, o_ref):
        _acc(o_ref, pl.program_id(2) == 0, _dot_tn(a_ref[...].astype(BF16), b_ref[...].astype(BF16)))

    return pl.pallas_call(
        body, name=name, grid=(ka // ta, kb // tb, n // tk),
        in_specs=[pl.BlockSpec((tk, ta), lambda i, j, k: (k, i)), pl.BlockSpec((tk, tb), lambda i, j, k: (k, j))],
        out_specs=pl.BlockSpec((ta, tb), lambda i, j, k: (i, j)),
        out_shape=jax.ShapeDtypeStruct((ka, kb), F32),
        compiler_params=pltpu.CompilerParams(dimension_semantics=("parallel", "parallel", "arbitrary"),
                                             vmem_limit_bytes=VMEM_LIMIT),
    )(a, b)


def _adamw(name, g8, w, m, v):
    rows, cols = w.shape
    tr = rows
    for cand in (256, 176, 128, 64):
        if rows % cand == 0 and rows > cand:
            tr = cand
            break

    def body(g8_ref, w_ref, m_ref, v_ref, g_ref, d_ref, nm_ref, nv_ref):
        g = g8_ref[0].astype(F32)
        for s in range(1, N_DEV):
            g = g + g8_ref[s].astype(F32)
        g_ref[...] = g
        nm = ADAM_B1 * m_ref[...] + (1.0 - ADAM_B1) * g
        nv = ADAM_B2 * v_ref[...] + (1.0 - ADAM_B2) * (g * g)
        nm_ref[...] = nm
        nv_ref[...] = nv
        m_hat = nm / (1.0 - ADAM_B1 ** ADAM_STEP)
        v_hat = nv / (1.0 - ADAM_B2 ** ADAM_STEP)
        d_ref[...] = -ADAM_LR * (m_hat / (jnp.sqrt(v_hat) + ADAM_EPS) + ADAM_WD * w_ref[...])

    blk = pl.BlockSpec((tr, cols), lambda i: (i, 0))
    return pl.pallas_call(
        body, name=name, grid=(rows // tr,),
        in_specs=[pl.BlockSpec((N_DEV, tr, cols), lambda i: (0, i, 0)), blk, blk, blk],
        out_specs=[blk] * 4, out_shape=[jax.ShapeDtypeStruct((rows, cols), F32)] * 4,
        compiler_params=pltpu.CompilerParams(dimension_semantics=("parallel",), vmem_limit_bytes=VMEM_LIMIT),
    )(g8, w, m, v)


def _exchange(name, srcs, scatter):
    nk = len(srcs)

    def body(*refs):
        src_refs, out_refs = refs[:nk], refs[nk:2 * nk]
        send_sems, recv_sems, local_sems = refs[2 * nk:]
        x, y, c = lax.axis_index("x"), lax.axis_index("y"), lax.axis_index("c")
        me = 4 * x + 2 * y + c
        flips = [(f >> 2 & 1, f >> 1 & 1, f & 1) for f in range(1, N_DEV)]
        peers = [(1 - x if fx else x, 1 - y if fy else y, 1 - c if fc else c) for fx, fy, fc in flips]
        pids = [4 * px + 2 * py + pc for px, py, pc in peers]

        def src_for(k, dest):
            return src_refs[k].at[dest] if scatter[k] else src_refs[k]

        def remote(k, j, dst_slot, dest):
            return pltpu.make_async_remote_copy(
                src_ref=src_for(k, dest), dst_ref=out_refs[k].at[dst_slot],
                send_sem=send_sems.at[k, j], recv_sem=recv_sems.at[k, j],
                device_id=peers[j], device_id_type=MESH)

        local = [pltpu.make_async_copy(src_for(k, me), out_refs[k].at[me], local_sems.at[k]) for k in range(nk)]
        sends = [remote(k, j, me, pids[j]) for k in range(nk) for j in range(N_DEV - 1)]
        for cp in local + sends:
            cp.start()
        for k in range(nk):
            for j in range(N_DEV - 1):
                remote(k, j, pids[j], pids[j]).wait_recv()
        for cp in sends:
            cp.wait_send()
        for cp in local:
            cp.wait()

    any_spec = pl.BlockSpec(memory_space=pl.ANY)
    out_shape = [jax.ShapeDtypeStruct(s.shape if sc else (N_DEV,) + s.shape, s.dtype) for s, sc in zip(srcs, scatter)]
    return pl.pallas_call(
        body, name=name, in_specs=[any_spec] * nk, out_specs=[any_spec] * nk, out_shape=out_shape,
        scratch_shapes=[pltpu.SemaphoreType.DMA((nk, N_DEV - 1)), pltpu.SemaphoreType.DMA((nk, N_DEV - 1)),
                        pltpu.SemaphoreType.DMA((nk,))],
    )(*srcs)


def _cols_from_shards(g):
    return jnp.transpose(g, (1, 0, 2)).reshape(g.shape[1], -1)


def _cols_to_shards(w):
    return jnp.transpose(w.reshape(w.shape[0], N_DEV, -1), (1, 0, 2))


def _rope_tables(n):
    t = _t_pad()
    pos = (jnp.arange(t, dtype=F32) - PAD_ROWS)
    half = QK_ROPE // 2
    freqs = 1.0 / (ROPE_THETA ** (jnp.arange(half, dtype=F32) / half))
    ang = pos[:, None] * freqs[None, :]
    cos, sin = jnp.cos(ang), jnp.sin(ang)
    z = lambda w: jnp.zeros((t, w), F32)
    c = jnp.concatenate([jnp.ones((t, QK_NOPE), F32), cos, cos, z(HEAD_PAD - QK_HEAD)], axis=1)
    s1 = jnp.concatenate([z(QK_NOPE + half), sin, z(HEAD_PAD - QK_HEAD)], axis=1)
    s2 = jnp.concatenate([z(QK_NOPE), -sin, z(HEAD_PAD - QK_NOPE - half)], axis=1)
    reps = n // t
    return tuple(jnp.tile(a, (reps, 1)) for a in (c, s1, s2))


def _block_diag_gates(lru_wa, lru_wi):
    eye = jnp.eye(2, dtype=lru_wa.dtype)

    def bd(w):
        w = w.reshape(2, D_RNN // LANES, 2, RNN_BW, RNN_BW)
        full = w[:, :, :, :, None, :] * eye[None, None, :, None, :, None]
        return full.reshape(2, D_RNN // LANES, LANES, LANES)

    a, i = bd(lru_wa), bd(lru_wi)
    return jnp.concatenate([a[0], i[0], a[1], i[1]], axis=-1)


def _unblock_gates(dw):
    nb = D_RNN // LANES
    parts = dw.reshape(nb, 2, RNN_BW, 4, 2, RNN_BW)
    diag = jnp.stack([parts[:, k, :, :, k, :] for k in range(2)], axis=1)
    diag = jnp.transpose(diag, (3, 0, 1, 2, 4)).reshape(4, 2 * nb, RNN_BW, RNN_BW)
    return jnp.stack([diag[0], diag[2]]), jnp.stack([diag[1], diag[3]])


def _local_step(x, tgt, meta, w):
    nb = x.shape[0]
    t = _t_pad()
    n = nb * t
    lead = jnp.zeros((nb, PAD_ROWS, D_MODEL), F32)
    h0 = jnp.concatenate([lead, jnp.broadcast_to(meta[None], (nb, N_META, D_MODEL)), x], axis=1).reshape(n, D_MODEL)
    tgt_p = jnp.concatenate([jnp.zeros((nb, PAD_ROWS + N_META, D_MODEL), F32), tgt], axis=1).reshape(n, D_MODEL)

    w_in = w["w_in"]
    zc = lambda c: jnp.zeros((D_MODEL, c), w_in.dtype)
    w_in_p = jnp.concatenate([w_in[:, :OFF_CKV], w_in[:, OFF_KR:], zc(QK_NOPE), w_in[:, OFF_CKV:OFF_KR],
                              zc(HEAD_PAD - QK_HEAD)], axis=1)
    w_uq_p = jnp.pad(w["w_uq"].reshape(Q_LORA, N_HEADS, QK_HEAD), ((0, 0), (0, 0), (0, HEAD_PAD - QK_HEAD))
                     ).reshape(Q_LORA, QP_COLS)
    ukv = w["w_ukv"].reshape(KV_LORA, N_HEADS, QK_NOPE + V_HEAD)
    w_uk_p = jnp.pad(ukv[:, :, :QK_NOPE], ((0, 0), (0, 0), (0, HEAD_PAD - QK_NOPE))).reshape(KV_LORA, QP_COLS)
    w_v = ukv[:, :, QK_NOPE:].reshape(KV_LORA, D_ATTN)
    pad_g = lambda g: jnp.pad(g, ((0, 0), (0, HEAD_PAD - QK_HEAD)))
    qg, kg = pad_g(w["q_norm_g"]), pad_g(w["k_norm_g"])
    rc, rs1, rs2 = _rope_tables(n)
    wblk = _block_diag_gates(w["lru_wa"], w["lru_wi"]).astype(BF16)
    nblk = D_RNN // LANES
    gbias = jnp.stack([w["lru_ba"][0], w["lru_bi"][0], w["lru_ba"][1], w["lru_bi"][1]], axis=0)
    gbias = jnp.transpose(gbias.reshape(4, nblk, LANES), (1, 0, 2)).reshape(nblk, 1, 4 * LANES)

    hn, cq, ckv, xr, xg, kr = _in_proj(h0, w["ln1_g"], w_in_p)
    q, k, v = _qkv_fwd(cq, ckv, kr, w["q_a_norm_g"], w["kv_a_norm_g"], w_uq_p, w_uk_p, w_v, qg, kg, rc, rs1, rs2)
    oa = _attn_fwd(q, k, v)
    orn = _rnn_fwd(xr, xg, w["conv_w"], w["conv_b"], wblk, gbias, w["lru_lambda"])
    (doa, dor, dh1, mix, h1n, act, dgate, dup, dyb, loss, dga, dgr, dg2) = _post(
        oa, orn, h0, tgt_p, w["attn_out_g"], w["rnn_out_g"], w["ln2_g"], w["w_out"], w["w_gate"], w["w_up"], w["w_down"])
    dq_r, dk_r, dv = _attn_bwd(q, k, v, doa)
    dxr, dxg, dcw, dcb, dwblk, dgb, dlam = _rnn_bwd(xr, xg, dor, w["conv_w"], w["conv_b"], wblk, gbias, w["lru_lambda"])
    (dcq, dckv, dkr, qa, kva, dqp, dkv, dqg, dkg, dgqa, dgkva) = _qkv_bwd(
        cq, ckv, kr, dq_r, dk_r, dv, w["q_a_norm_g"], w["kv_a_norm_g"], w_uq_p, w_uk_p, w_v, qg, kg, rc, rs1, rs2)
    dp, dh0, dg1 = _in_bwd(dcq, dckv, dxr, dxg, dkr, h0, dh1, w["ln1_g"], w_in_p)

    dw_in_p = _matmul_tn("dw_in", hn, dp)
    dw_uq_p = _matmul_tn("dw_uq", qa, dqp)
    dw_kv = _matmul_tn("dw_ukv", kva, dkv)
    dw_out = _matmul_tn("dw_out", mix, dh1)
    dw_gate = _matmul_tn("dw_gate", h1n, dgate)
    dw_up = _matmul_tn("dw_up", h1n, dup)
    dw_down = _matmul_tn("dw_down", act, dyb)

    dh0 = dh0.reshape(nb, t, D_MODEL)
    kr0 = OFF_CKV + 2 * D_RNN + QK_NOPE
    dw_in = jnp.concatenate([dw_in_p[:, :OFF_CKV], dw_in_p[:, kr0:kr0 + QK_ROPE], dw_in_p[:, OFF_CKV:OFF_CKV + 2 * D_RNN]],
                            axis=1)
    dw_uq = dw_uq_p.reshape(Q_LORA, N_HEADS, HEAD_PAD)[:, :, :QK_HEAD].reshape(Q_LORA, N_HEADS * QK_HEAD)
    dw_ukv = jnp.concatenate([dw_kv[:, :QP_COLS].reshape(KV_LORA, N_HEADS, HEAD_PAD)[:, :, :QK_NOPE],
                              dw_kv[:, QP_COLS:].reshape(KV_LORA, N_HEADS, V_HEAD)], axis=2).reshape(KV_LORA, -1)
    dwa, dwi = _unblock_gates(dwblk)
    dgb = jnp.transpose(dgb.reshape(nblk, 4, LANES), (1, 0, 2)).reshape(4, D_RNN)
    grads = {
        "meta_tokens": jnp.sum(dh0[:, PAD_ROWS:PAD_ROWS + N_META], axis=0),
        "ln1_g": dg1, "w_in": dw_in, "q_a_norm_g": dgqa, "w_uq": dw_uq, "kv_a_norm_g": dgkva, "w_ukv": dw_ukv,
        "q_norm_g": dqg[:, :QK_HEAD], "k_norm_g": dkg[:, :QK_HEAD], "conv_w": dcw, "conv_b": dcb,
        "lru_wa": dwa, "lru_ba": jnp.stack([dgb[0], dgb[2]]), "lru_wi": dwi, "lru_bi": jnp.stack([dgb[1], dgb[3]]),
        "lru_lambda": dlam, "attn_out_g": dga, "rnn_out_g": dgr, "w_out": dw_out, "ln2_g": dg2,
        "w_gate": dw_gate, "w_up": dw_up, "w_down": dw_down,
    }
    return loss[0, 0], dh0[:, PAD_ROWS + N_META:], grads


WEIGHTS = ("meta_tokens", "ln1_g", "w_in", "q_a_norm_g", "w_uq", "kv_a_norm_g", "w_ukv", "q_norm_g", "k_norm_g",
           "conv_w", "conv_b", "lru_wa", "lru_ba", "lru_wi", "lru_bi", "lru_lambda", "attn_out_g", "rnn_out_g",
           "w_out", "ln2_g", "w_gate", "w_up", "w_down")
BIG = ("w_in", "w_uq", "w_ukv", "w_out", "w_gate", "w_up", "w_down")
ROW_SHARDED = ("w_out", "w_down")
SMALL_SHARDED = ("meta_tokens", "conv_w", "lru_ba", "lru_bi", "lru_lambda")
REPLICATED = ("ln1_g", "q_a_norm_g", "kv_a_norm_g", "q_norm_g", "k_norm_g", "conv_b", "lru_wa", "lru_wi",
              "attn_out_g", "rnn_out_g", "ln2_g")
SMALL_ROWS = 32


def _pack_sharded(parts):
    padded = [jnp.pad(parts[k], [(0, 0)] * (parts[k].ndim - 1) + [(0, LANES - parts[k].shape[-1])]) for k in SMALL_SHARDED]
    cat = jnp.concatenate(padded, axis=-2)
    return jnp.pad(cat, [(0, 0)] * (cat.ndim - 2) + [(0, SMALL_ROWS - cat.shape[-2]), (0, 0)])


def _unpack_sharded(buf, like):
    out, r0 = {}, 0
    for k in SMALL_SHARDED:
        r, c = like[k].shape[-2:]
        out[k] = buf[..., r0:r0 + r, :c]
        r0 += r
    return out


def _pack_replicated(parts):
    rows = []
    for k in REPLICATED:
        flat = parts[k].reshape(-1)
        rows.append(jnp.pad(flat, (0, -flat.shape[0] % LANES)).reshape(-1, LANES))
    cat = jnp.concatenate(rows, axis=0)
    return jnp.pad(cat, ((0, -cat.shape[0] % 8), (0, 0)))


def _unpack_replicated(buf, like):
    out, r0 = {}, 0
    for k in REPLICATED:
        size = math.prod(like[k].shape)
        r = -(-size // LANES)
        out[k] = buf[r0:r0 + r].reshape(-1)[:size].reshape(like[k].shape)
        r0 += r
    return out


def kernel(x, meta_tokens, ln1_g, w_in, q_a_norm_g, w_uq, kv_a_norm_g, w_ukv, q_norm_g, k_norm_g, conv_w, conv_b, lru_wa, lru_ba, lru_wi, lru_bi, lru_lambda, attn_out_g, rnn_out_g, w_out, ln2_g, w_gate, w_up, w_down, loss_target, m_meta_tokens, m_ln1_g, m_w_in, m_q_a_norm_g, m_w_uq, m_kv_a_norm_g, m_w_ukv, m_q_norm_g, m_k_norm_g, m_conv_w, m_conv_b, m_lru_wa, m_lru_ba, m_lru_wi, m_lru_bi, m_lru_lambda, m_attn_out_g, m_rnn_out_g, m_w_out, m_ln2_g, m_w_gate, m_w_up, m_w_down, v_meta_tokens, v_ln1_g, v_w_in, v_q_a_norm_g, v_w_uq, v_kv_a_norm_g, v_w_ukv, v_q_norm_g, v_k_norm_g, v_conv_w, v_conv_b, v_lru_wa, v_lru_ba, v_lru_wi, v_lru_bi, v_lru_lambda, v_attn_out_g, v_rnn_out_g, v_w_out, v_ln2_g, v_w_gate, v_w_up, v_w_down):
    given = (meta_tokens, ln1_g, w_in, q_a_norm_g, w_uq, kv_a_norm_g, w_ukv, q_norm_g, k_norm_g, conv_w, conv_b,
             lru_wa, lru_ba, lru_wi, lru_bi, lru_lambda, attn_out_g, rnn_out_g, w_out, ln2_g, w_gate, w_up, w_down)
    moments_m = (m_meta_tokens, m_ln1_g, m_w_in, m_q_a_norm_g, m_w_uq, m_kv_a_norm_g, m_w_ukv, m_q_norm_g, m_k_norm_g,
                 m_conv_w, m_conv_b, m_lru_wa, m_lru_ba, m_lru_wi, m_lru_bi, m_lru_lambda, m_attn_out_g, m_rnn_out_g,
                 m_w_out, m_ln2_g, m_w_gate, m_w_up, m_w_down)
    moments_v = (v_meta_tokens, v_ln1_g, v_w_in, v_q_a_norm_g, v_w_uq, v_kv_a_norm_g, v_w_ukv, v_q_norm_g, v_k_norm_g,
                 v_conv_w, v_conv_b, v_lru_wa, v_lru_ba, v_lru_wi, v_lru_bi, v_lru_lambda, v_attn_out_g, v_rnn_out_g,
                 v_w_out, v_ln2_g, v_w_gate, v_w_up, v_w_down)
    shapes = {k: a.shape for k, a in zip(WEIGHTS, given)}

    def strip(a, k):
        return a if k == "meta_tokens" or a.ndim == 2 else a.reshape(a.shape[1:])

    w = {k: strip(a, k) for k, a in zip(WEIGHTS, given)}
    m = {k: strip(a, k) for k, a in zip(WEIGHTS, moments_m)}
    v = {k: strip(a, k) for k, a in zip(WEIGHTS, moments_v)}

    gathered = _exchange("gather_weights", [w[k].astype(BF16) for k in BIG] + [_pack_sharded(w)],
                         [False] * (len(BIG) + 1))
    full = {k: w[k] for k in REPLICATED}
    for k, g in zip(BIG, gathered):
        full[k] = g.reshape(-1, g.shape[-1]) if k in ROW_SHARDED else _cols_from_shards(g)
    for k, g in _unpack_sharded(gathered[-1], w).items():
        full[k] = _cols_from_shards(g)

    loss_part, grad_x, grads = _local_step(x, loss_target, full["meta_tokens"], full)

    send = []
    for k in BIG:
        g = grads[k]
        send.append((g.reshape(N_DEV, -1, g.shape[-1]) if k in ROW_SHARDED else _cols_to_shards(g)).astype(BF16))
    send.append(_pack_sharded({k: _cols_to_shards(grads[k]) for k in SMALL_SHARDED}))
    send.append(_pack_replicated({k: grads[k] for k in REPLICATED}))
    parts = _exchange("reduce_grads", send, [True] * (len(BIG) + 1) + [False])

    new = {}
    for k, g8 in zip(BIG, parts):
        new[k] = _adamw("adamw_" + k, g8, w[k], m[k], v[k])
    packed = _adamw("adamw_small_sharded", parts[len(BIG)], _pack_sharded(w), _pack_sharded(m), _pack_sharded(v))
    for k, vals in zip(SMALL_SHARDED, zip(*[_unpack_sharded(a, w).values() for a in packed])):
        new[k] = vals
    packed = _adamw("adamw_replicated", parts[len(BIG) + 1], _pack_replicated(w), _pack_replicated(m), _pack_replicated(v))
    for k, vals in zip(REPLICATED, zip(*[_unpack_replicated(a, w).values() for a in packed])):
        new[k] = vals

    loss = lax.psum(loss_part, ("x", "y", "c"))
    outs = [loss, grad_x]
    for idx in range(4):
        outs += [new[k][idx].reshape(shapes[k]) for k in WEIGHTS]
    return tuple(outs)
```

```python
import functools
import math

import jax
import jax.numpy as jnp
from jax import lax
from jax.experimental import pallas as pl
from jax.experimental.pallas import tpu as pltpu

F32 = jnp.float32
BF16 = jnp.bfloat16

D_MODEL = 1024
N_META = 16
SEQ = 2048
N_HEADS = 8
QK_NOPE = 64
QK_ROPE = 32
QK_HEAD = QK_NOPE + QK_ROPE
V_HEAD = 64
D_ATTN = N_HEADS * V_HEAD
Q_LORA = 384
KV_LORA = 256
D_RNN = 512
RNN_BW = 64
D_FF = 2816
EPS = 1e-6
LRU_C = 8.0
ROPE_THETA = 10000.0
OFF_CKV = Q_LORA + KV_LORA
OFF_KR = OFF_CKV + QK_ROPE
IN_COLS = OFF_KR + 2 * D_RNN

ADAM_LR = 0.001
ADAM_B1 = 0.9
ADAM_B2 = 0.999
ADAM_EPS = 1e-08
ADAM_WD = 0.01
ADAM_STEP = 10

N_DEV = 8
LANES = 128
HEAD_PAD = LANES
PAD_ROWS = LANES - N_META
QP_COLS = N_HEADS * HEAD_PAD
P_COLS = OFF_CKV + 2 * D_RNN + LANES
FF_CHUNK = D_FF // 2
VMEM_LIMIT = 56 * 1024 * 1024
MESH = pl.DeviceIdType.MESH


def _t_pad():
    return PAD_ROWS + N_META + SEQ


def _row_tile(n):
    return 256 if n % 256 == 0 else 128


def _const_spec(shape):
    nd = len(shape)
    return pl.BlockSpec(shape, lambda *_: (0,) * nd, pipeline_mode=pl.Buffered(1))


def _rms(x, d):
    r = lax.rsqrt(jnp.sum(x * x, axis=-1, keepdims=True) * (1.0 / d) + EPS)
    return x * r, r


def _rms_bwd(dy, xhat, r, g, d):
    dxh = dy * g
    return r * (dxh - xhat * (jnp.sum(dxh * xhat, axis=-1, keepdims=True) * (1.0 / d)))


def _colsum(x):
    return jnp.sum(x, axis=0, keepdims=True)


def _dot(a, b):
    return jnp.dot(a, b, preferred_element_type=F32)


def _dot_nt(a, b):
    return lax.dot_general(a, b, (((1,), (1,)), ((), ())), preferred_element_type=F32)


def _dot_tn(a, b):
    return lax.dot_general(a, b, (((0,), (0,)), ((), ())), preferred_element_type=F32)


def _rope(x, c, s1, s2):
    return x * c + pltpu.roll(x, 16, 1) * s1 + pltpu.roll(x, HEAD_PAD - 16, 1) * s2


def _rope_bwd(dy, c, s1, s2):
    return dy * c + pltpu.roll(dy * s1, HEAD_PAD - 16, 1) + pltpu.roll(dy * s2, 16, 1)


def _acc(ref, first, val):
    @pl.when(first)
    def _():
        ref[...] = val

    @pl.when(jnp.logical_not(first))
    def _():
        ref[...] += val


def _in_proj(h0, ln1_g, w_in_p):
    n = h0.shape[0]
    tm = _row_tile(n)

    def body(h_ref, g_ref, w_ref, hn_ref, cq_ref, ckv_ref, xr_ref, xg_ref, kr_ref):
        xhat, _ = _rms(h_ref[...], D_MODEL)
        hn = (xhat * g_ref[...]).astype(BF16)
        hn_ref[...] = hn
        p = _dot(hn, w_ref[...])
        cq_ref[...] = p[:, :Q_LORA]
        ckv_ref[...] = p[:, Q_LORA:OFF_CKV]
        xr_ref[...] = p[:, OFF_CKV:OFF_CKV + D_RNN]
        xg_ref[...] = p[:, OFF_CKV + D_RNN:OFF_CKV + 2 * D_RNN]
        kr_ref[...] = p[:, OFF_CKV + 2 * D_RNN:]

    def row(w):
        return pl.BlockSpec((tm, w), lambda i: (i, 0))

    widths = (D_MODEL, Q_LORA, KV_LORA, D_RNN, D_RNN, LANES)
    return pl.pallas_call(
        body, name="in_proj", grid=(n // tm,),
        in_specs=[row(D_MODEL), _const_spec((1, D_MODEL)), _const_spec((D_MODEL, P_COLS))],
        out_specs=[row(w) for w in widths],
        out_shape=[jax.ShapeDtypeStruct((n, w), BF16 if k == 0 else F32) for k, w in enumerate(widths)],
        compiler_params=pltpu.CompilerParams(dimension_semantics=("parallel",), vmem_limit_bytes=VMEM_LIMIT),
    )(h0, ln1_g, w_in_p)


def _qkv_fwd(cq, ckv, kr, gqa, gkva, w_uq_p, w_uk_p, w_v, qg, kg, rc, rs1, rs2):
    n = cq.shape[0]
    tm = _row_tile(n)

    def body(cq_ref, ckv_ref, kr_ref, gqa_ref, gkva_ref, wuq_ref, wuk_ref, wv_ref, qg_ref, kg_ref,
             c_ref, s1_ref, s2_ref, q_ref, k_ref, v_ref):
        xq, _ = _rms(cq_ref[...], Q_LORA)
        qa = (xq * gqa_ref[...]).astype(BF16)
        q = _dot(qa, wuq_ref[...])
        xkv, _ = _rms(ckv_ref[...], KV_LORA)
        kva = (xkv * gkva_ref[...]).astype(BF16)
        kn = _dot(kva, wuk_ref[...])
        v_ref[...] = _dot(kva, wv_ref[...]).astype(BF16)
        krp = kr_ref[...]
        c, s1, s2 = c_ref[...], s1_ref[...], s2_ref[...]
        for h in range(N_HEADS):
            sl = slice(h * HEAD_PAD, (h + 1) * HEAD_PAD)
            qh, _ = _rms(q[:, sl], QK_HEAD)
            q_ref[:, sl] = _rope(qh * qg_ref[...], c, s1, s2).astype(BF16)
            kh, _ = _rms(kn[:, sl] + krp, QK_HEAD)
            k_ref[:, sl] = _rope(kh * kg_ref[...], c, s1, s2).astype(BF16)

    def row(w):
        return pl.BlockSpec((tm, w), lambda i: (i, 0))

    return pl.pallas_call(
        body, name="qkv_fwd", grid=(n // tm,),
        in_specs=[row(Q_LORA), row(KV_LORA), row(LANES), _const_spec((1, Q_LORA)), _const_spec((1, KV_LORA)),
                  _const_spec((Q_LORA, QP_COLS)), _const_spec((KV_LORA, QP_COLS)), _const_spec((KV_LORA, D_ATTN)),
                  _const_spec((1, LANES)), _const_spec((1, LANES)), row(LANES), row(LANES), row(LANES)],
        out_specs=[row(QP_COLS), row(QP_COLS), row(D_ATTN)],
        out_shape=[jax.ShapeDtypeStruct((n, QP_COLS), BF16), jax.ShapeDtypeStruct((n, QP_COLS), BF16),
                   jax.ShapeDtypeStruct((n, D_ATTN), BF16)],
        compiler_params=pltpu.CompilerParams(dimension_semantics=("parallel",), vmem_limit_bytes=VMEM_LIMIT),
    )(cq, ckv, kr, gqa, gkva, w_uq_p, w_uk_p, w_v, qg, kg, rc, rs1, rs2)


def _qkv_bwd(cq, ckv, kr, dq_r, dk_r, dv, gqa, gkva, w_uq_p, w_uk_p, w_v, qg, kg, rc, rs1, rs2):
    n = cq.shape[0]
    tm = _row_tile(n)

    def body(cq_ref, ckv_ref, kr_ref, dq_ref, dk_ref, dv_ref, gqa_ref, gkva_ref, wuq_ref, wuk_ref, wv_ref,
             qg_ref, kg_ref, c_ref, s1_ref, s2_ref,
             dcq_ref, dckv_ref, dkr_ref, qa_ref, kva_ref, dqp_ref, dkv_ref, dqg_ref, dkg_ref, dgqa_ref, dgkva_ref):
        first = pl.program_id(0) == 0
        xq, rq = _rms(cq_ref[...], Q_LORA)
        qa = (xq * gqa_ref[...]).astype(BF16)
        qa_ref[...] = qa
        q = _dot(qa, wuq_ref[...])
        xkv, rkv = _rms(ckv_ref[...], KV_LORA)
        kva = (xkv * gkva_ref[...]).astype(BF16)
        kva_ref[...] = kva
        kn = _dot(kva, wuk_ref[...])
        krp = kr_ref[...]
        c, s1, s2 = c_ref[...], s1_ref[...], s2_ref[...]
        lane = lax.broadcasted_iota(jnp.int32, (tm, HEAD_PAD), 1)
        rope_lanes = jnp.logical_and(lane >= QK_NOPE, lane < QK_HEAD)
        dqg = jnp.zeros((1, HEAD_PAD), F32)
        dkg = jnp.zeros((1, HEAD_PAD), F32)
        dkr = jnp.zeros((tm, HEAD_PAD), F32)
        for h in range(N_HEADS):
            sl = slice(h * HEAD_PAD, (h + 1) * HEAD_PAD)
            qh, rqh = _rms(q[:, sl], QK_HEAD)
            dy = _rope_bwd(dq_ref[:, sl], c, s1, s2)
            dqg = dqg + _colsum(dy * qh)
            dqp_ref[:, sl] = _rms_bwd(dy, qh, rqh, qg_ref[...], QK_HEAD).astype(BF16)
            kh, rkh = _rms(kn[:, sl] + krp, QK_HEAD)
            dyk = _rope_bwd(dk_ref[:, sl], c, s1, s2)
            dkg = dkg + _colsum(dyk * kh)
            dkh = _rms_bwd(dyk, kh, rkh, kg_ref[...], QK_HEAD)
            dkv_ref[:, sl] = dkh.astype(BF16)
            dkr = dkr + jnp.where(rope_lanes, dkh, 0.0)
        dkv_ref[:, QP_COLS:] = dv_ref[...].astype(BF16)
        dkr_ref[...] = dkr.astype(BF16)
        dqa = _dot_nt(dqp_ref[...], wuq_ref[...])
        dcq_ref[...] = _rms_bwd(dqa, xq, rq, gqa_ref[...], Q_LORA).astype(BF16)
        dkva = _dot_nt(dkv_ref[:, :QP_COLS], wuk_ref[...]) + _dot_nt(dkv_ref[:, QP_COLS:], wv_ref[...])
        dckv_ref[...] = _rms_bwd(dkva, xkv, rkv, gkva_ref[...], KV_LORA).astype(BF16)
        _acc(dqg_ref, first, dqg)
        _acc(dkg_ref, first, dkg)
        _acc(dgqa_ref, first, _colsum(dqa * xq))
        _acc(dgkva_ref, first, _colsum(dkva * xkv))

    def row(w):
        return pl.BlockSpec((tm, w), lambda i: (i, 0))

    def acc(w):
        return pl.BlockSpec((1, w), lambda i: (0, 0))

    return pl.pallas_call(
        body, name="qkv_bwd", grid=(n // tm,),
        in_specs=[row(Q_LORA), row(KV_LORA), row(LANES), row(QP_COLS), row(QP_COLS), row(D_ATTN),
                  _const_spec((1, Q_LORA)), _const_spec((1, KV_LORA)),
                  _const_spec((Q_LORA, QP_COLS)), _const_spec((KV_LORA, QP_COLS)), _const_spec((KV_LORA, D_ATTN)),
                  _const_spec((1, LANES)), _const_spec((1, LANES)), row(LANES), row(LANES), row(LANES)],
        out_specs=[row(Q_LORA), row(KV_LORA), row(LANES), row(Q_LORA), row(KV_LORA), row(QP_COLS),
                   row(QP_COLS + D_ATTN), acc(LANES), acc(LANES), acc(Q_LORA), acc(KV_LORA)],
        out_shape=[jax.ShapeDtypeStruct((n, Q_LORA), BF16), jax.ShapeDtypeStruct((n, KV_LORA), BF16),
                   jax.ShapeDtypeStruct((n, LANES), BF16), jax.ShapeDtypeStruct((n, Q_LORA), BF16),
                   jax.ShapeDtypeStruct((n, KV_LORA), BF16), jax.ShapeDtypeStruct((n, QP_COLS), BF16),
                   jax.ShapeDtypeStruct((n, QP_COLS + D_ATTN), BF16),
                   jax.ShapeDtypeStruct((1, LANES), F32), jax.ShapeDtypeStruct((1, LANES), F32),
                   jax.ShapeDtypeStruct((1, Q_LORA), F32), jax.ShapeDtypeStruct((1, KV_LORA), F32)],
        compiler_params=pltpu.CompilerParams(dimension_semantics=("arbitrary",), vmem_limit_bytes=VMEM_LIMIT),
    )(cq, ckv, kr, dq_r, dk_r, dv, gqa, gkva, w_uq_p, w_uk_p, w_v, qg, kg, rc, rs1, rs2)


def _softmax_parts(qh, kh, tq, t):
    s = _dot_nt(qh, kh) * (QK_HEAD ** -0.5)
    key = lax.broadcasted_iota(jnp.int32, (tq, t), 1)
    s = jnp.where(key >= PAD_ROWS, s, -jnp.inf)
    e = jnp.exp(s - jnp.max(s, axis=-1, keepdims=True))
    return e, jnp.sum(e, axis=-1, keepdims=True)


def _attn_specs(t, tq):
    nq = t // tq
    qspec = pl.BlockSpec((tq, 2 * HEAD_PAD), lambda b, hp, i: (b * nq + i, hp))
    kspec = pl.BlockSpec((t, 2 * HEAD_PAD), lambda b, hp, i: (b, hp))
    vspec = pl.BlockSpec((t, 2 * V_HEAD), lambda b, hp, i: (b, hp))
    ospec = pl.BlockSpec((tq, 2 * V_HEAD), lambda b, hp, i: (b * nq + i, hp))
    return nq, qspec, kspec, vspec, ospec


def _grid_ends():
    ids = [pl.program_id(a) for a in range(3)]
    first = jnp.logical_and(jnp.logical_and(ids[0] == 0, ids[1] == 0), ids[2] == 0)
    last = jnp.logical_and(jnp.logical_and(ids[0] == pl.num_programs(0) - 1, ids[1] == pl.num_programs(1) - 1),
                           ids[2] == pl.num_programs(2) - 1)
    return first, last


def _attn_fwd(q, k, v, srcs=(), scatter=()):
    n = q.shape[0]
    t = _t_pad()
    tq = t // 8
    nq, qspec, kspec, vspec, ospec = _attn_specs(t, tq)
    nk = len(srcs)
    c_in, c_out, c_shape, c_sems = _exchange_specs(srcs, scatter)

    def body(q_ref, k_ref, v_ref, *rest):
        o_ref = rest[nk]
        start, wait = _exchange_fns(rest[:nk], rest[nk + 1:2 * nk + 1], rest[2 * nk + 1:], scatter)
        first_step, last_step = _grid_ends()
        pl.when(first_step)(start)
        lane = lax.broadcasted_iota(jnp.int32, (tq, 2 * V_HEAD), 1)
        outs = []
        for j in range(2):
            sl = slice(j * HEAD_PAD, (j + 1) * HEAD_PAD)
            e, l = _softmax_parts(q_ref[:, sl], k_ref[:, sl], tq, t)
            outs.append(_dot(e.astype(BF16), v_ref[...]) / l)
        o_ref[...] = jnp.where(lane < V_HEAD, outs[0], outs[1])
        pl.when(last_step)(wait)

    res = pl.pallas_call(
        body, name="attn_fwd", grid=(n // t, N_HEADS // 2, nq),
        in_specs=[qspec, kspec, vspec] + c_in, out_specs=[ospec] + c_out,
        out_shape=[jax.ShapeDtypeStruct((n, D_ATTN), F32)] + c_shape, scratch_shapes=c_sems,
        compiler_params=pltpu.CompilerParams(dimension_semantics=("arbitrary", "arbitrary", "arbitrary"),
                                             vmem_limit_bytes=VMEM_LIMIT),
    )(q, k, v, *srcs)
    return res[0], res[1:]


def _attn_bwd(q, k, v, do, srcs=(), scatter=()):
    n = q.shape[0]
    t = _t_pad()
    tq = t // 8
    nq, qspec, kspec, vspec, ospec = _attn_specs(t, tq)
    nk = len(srcs)
    c_in, c_out, c_shape, c_sems = _exchange_specs(srcs, scatter)

    def body(q_ref, k_ref, v_ref, do_ref, *rest):
        dq_ref, dk_ref, dv_ref = rest[nk:nk + 3]
        start, wait = _exchange_fns(rest[:nk], rest[nk + 3:2 * nk + 3], rest[2 * nk + 3:], scatter)
        first_step, last_step = _grid_ends()
        pl.when(first_step)(start)
        first = pl.program_id(2) == 0
        lane = lax.broadcasted_iota(jnp.int32, (tq, 2 * V_HEAD), 1)
        do = do_ref[...]
        dv = jnp.zeros((t, 2 * V_HEAD), F32)
        for j in range(2):
            sl = slice(j * HEAD_PAD, (j + 1) * HEAD_PAD)
            qh, kh = q_ref[:, sl], k_ref[:, sl]
            e, l = _softmax_parts(qh, kh, tq, t)
            p = e / l
            in_head = (lane < V_HEAD) if j == 0 else (lane >= V_HEAD)
            doh = jnp.where(in_head, do, 0.0).astype(BF16)
            dp = _dot_nt(doh, v_ref[...])
            delta = jnp.sum(p * dp, axis=-1, keepdims=True)
            ds = (p * (dp - delta) * (QK_HEAD ** -0.5)).astype(BF16)
            dq_ref[:, sl] = _dot(ds, kh)
            dkh = _dot_tn(ds, qh)

            @pl.when(first)
            def _():
                dk_ref[:, sl] = dkh

            @pl.when(jnp.logical_not(first))
            def _():
                dk_ref[:, sl] += dkh

            dv = dv + _dot_tn(p.astype(BF16), doh)
        _acc(dv_ref, first, dv)
        pl.when(last_step)(wait)

    res = pl.pallas_call(
        body, name="attn_bwd", grid=(n // t, N_HEADS // 2, nq),
        in_specs=[qspec, kspec, vspec, ospec] + c_in, out_specs=[qspec, kspec, vspec] + c_out,
        out_shape=[jax.ShapeDtypeStruct((n, QP_COLS), F32), jax.ShapeDtypeStruct((n, QP_COLS), F32),
                   jax.ShapeDtypeStruct((n, D_ATTN), F32)] + c_shape, scratch_shapes=c_sems,
        compiler_params=pltpu.CompilerParams(dimension_semantics=("arbitrary", "arbitrary", "arbitrary"),
                                             vmem_limit_bytes=VMEM_LIMIT),
    )(q, k, v, do, *srcs)
    return res[:3], res[3:]


SCAN_UNROLL = 4


def _scan(chains, t):
    groups = t // 8
    rows = lax.broadcasted_iota(jnp.int32, (8, LANES), 0)

    def one(g, carry, a_ref, b_ref, h_ref, reverse):
        off = pl.multiple_of((groups - 1 - g if reverse else g) * 8, 8)
        a = a_ref[pl.ds(off, 8), :]
        b = b_ref[pl.ds(off, 8), :]
        for d in (1, 2, 4):
            if reverse:
                keep = rows < 8 - d
                a_n, b_n = pltpu.roll(a, 8 - d, 0), pltpu.roll(b, 8 - d, 0)
            else:
                keep = rows >= d
                a_n, b_n = pltpu.roll(a, d, 0), pltpu.roll(b, d, 0)
            b = a * jnp.where(keep, b_n, 0.0) + b
            a = a * jnp.where(keep, a_n, 1.0)
        h = b + a * carry
        h_ref[pl.ds(off, 8), :] = h
        return h[0:1] if reverse else h[7:8]

    def step(g, carries):
        return tuple(one(g, c, *chain) for c, chain in zip(carries, chains))

    lax.fori_loop(0, groups, step, tuple(jnp.zeros((1, LANES), F32) for _ in chains), unroll=SCAN_UNROLL)


def _shift_rows(x, s, rows, t):
    if s == 0:
        return x
    rolled = pltpu.roll(x, s % t, 0)
    return jnp.where(rows >= s, rolled, 0.0) if s > 0 else jnp.where(rows < t + s, rolled, 0.0)


def _neg_expm1(x):
    series = -x * (1.0 + x * (0.5 + x * (1.0 / 6 + x * (1.0 / 24 + x * (1.0 / 120 + x * (1.0 / 720))))))
    return jnp.where(x > -0.3, series, 1.0 - jnp.exp(x))


def _gelu_parts(x):
    k = math.sqrt(2.0 / math.pi)
    th = jnp.tanh(k * (x + 0.044715 * x * x * x))
    g = 0.5 * x * (1.0 + th)
    dg = 0.5 * (1.0 + th) + 0.5 * x * (1.0 - th * th) * k * (1.0 + 3 * 0.044715 * x * x)
    return g, dg


def _lru_gates(xc, gates, lam_ref, valid, d):
    r = jax.nn.sigmoid(gates[:, (2 * d) * LANES:(2 * d + 1) * LANES])
    i = jax.nn.sigmoid(gates[:, (2 * d + 1) * LANES:(2 * d + 2) * LANES])
    neg_lam = -lam_ref[d:d + 1, :]
    sp = jnp.maximum(neg_lam, 0.0) + jnp.log1p(jnp.exp(-jnp.abs(neg_lam)))
    log_a = -LRU_C * r * sp
    a = jnp.exp(log_a)
    m = jnp.maximum(_neg_expm1(2.0 * log_a), 0.0)
    sq = jnp.sqrt(m)
    b = jnp.where(valid, sq * (i * xc), 0.0)
    return r, i, sp, a, m, sq, b


def _conv(xr, cw_ref, cb_ref, rows, t):
    return (cw_ref[0:1, :] * _shift_rows(xr, 2, rows, t) + cw_ref[1:2, :] * _shift_rows(xr, 1, rows, t)
            + cw_ref[2:3, :] * xr + cw_ref[3:4, :] * _shift_rows(xr, -1, rows, t) + cb_ref[...])


def _rnn_specs(t):
    seq = pl.BlockSpec((t, LANES), lambda cb, b: (b, cb))
    cw = pl.BlockSpec((4, LANES), lambda cb, b: (0, cb))
    vec1 = pl.BlockSpec((1, LANES), lambda cb, b: (0, cb))
    vec2 = pl.BlockSpec((2, LANES), lambda cb, b: (0, cb))
    wblk = pl.BlockSpec((1, LANES, 4 * LANES), lambda cb, b: (cb, 0, 0))
    gbias = pl.BlockSpec((1, 1, 4 * LANES), lambda cb, b: (cb, 0, 0))
    return seq, cw, vec1, vec2, wblk, gbias


def _rnn_fwd(xr, xg, conv_w, conv_b, wblk, gbias, lam):
    n = xr.shape[0]
    t = _t_pad()
    seq, cw, vec1, vec2, wspec, gspec = _rnn_specs(t)

    def body(xr_ref, xg_ref, cw_ref, cb_ref, w_ref, gb_ref, lam_ref, o_ref, a_s, b_s, h_s):
        rows = lax.broadcasted_iota(jnp.int32, (t, LANES), 0)
        valid = rows >= PAD_ROWS
        xc = _conv(xr_ref[...], cw_ref, cb_ref, rows, t)
        gates = _dot(xc.astype(BF16), w_ref[0]) + gb_ref[0]
        for d in range(2):
            _, _, _, a, _, _, b = _lru_gates(xc, gates, lam_ref, valid, d)
            a_s[d] = a
            b_s[d] = b
        _scan([(a_s.at[d], b_s.at[d], h_s.at[d], d == 1) for d in range(2)], t)
        g, _ = _gelu_parts(xg_ref[...])
        o_ref[...] = (h_s[0] + h_s[1]) * g

    return pl.pallas_call(
        body, name="rnn_fwd", grid=(D_RNN // LANES, n // t),
        in_specs=[seq, seq, cw, vec1, wspec, gspec, vec2], out_specs=seq,
        out_shape=jax.ShapeDtypeStruct((n, D_RNN), F32),
        scratch_shapes=[pltpu.VMEM((2, t, LANES), F32), pltpu.VMEM((2, t, LANES), F32), pltpu.VMEM((2, t, LANES), F32)],
        compiler_params=pltpu.CompilerParams(dimension_semantics=("parallel", "parallel"), vmem_limit_bytes=VMEM_LIMIT),
    )(xr, xg, conv_w, conv_b, wblk, gbias, lam)


def _rnn_bwd(xr, xg, do, conv_w, conv_b, wblk, gbias, lam):
    n = xr.shape[0]
    t = _t_pad()
    seq, cw, vec1, vec2, wspec, gspec = _rnn_specs(t)

    def body(xr_ref, xg_ref, do_ref, cw_ref, cb_ref, w_ref, gb_ref, lam_ref,
             dxr_ref, dxg_ref, dcw_ref, dcb_ref, dw_ref, dgb_ref, dlam_ref, a_s, b_s, h_s, l_s, dg_s):
        first = pl.program_id(1) == 0
        rows = lax.broadcasted_iota(jnp.int32, (t, LANES), 0)
        valid = rows >= PAD_ROWS
        xr = xr_ref[...]
        xc = _conv(xr, cw_ref, cb_ref, rows, t)
        xcb = xc.astype(BF16)
        gates = _dot(xcb, w_ref[0]) + gb_ref[0]
        for d in range(2):
            _, _, _, a, _, _, b = _lru_gates(xc, gates, lam_ref, valid, d)
            a_s[d] = a
            b_s[d] = b
        _scan([(a_s.at[d], b_s.at[d], h_s.at[d], d == 1) for d in range(2)], t)
        g, dg = _gelu_parts(xg_ref[...])
        do = do_ref[...]
        dxg_ref[...] = do * (h_s[0] + h_s[1]) * dg
        b_s[0] = do * g
        for d in range(2):
            a_s[d] = _shift_rows(a_s[d], -1 if d == 0 else 1, rows, t)
        _scan([(a_s.at[d], b_s.at[0], l_s.at[d], d == 0) for d in range(2)], t)
        dxc = jnp.zeros((t, LANES), F32)
        dlams = []
        for d in range(2):
            r, i, sp, a, m, sq, _ = _lru_gates(xc, gates, lam_ref, valid, d)
            lam_t = l_s[d]
            da = lam_t * _shift_rows(h_s[d], 1 if d == 0 else -1, rows, t)
            lam_v = jnp.where(valid, lam_t, 0.0)
            dsq = lam_v * (i * xc)
            di = lam_v * sq * xc
            dxc = dxc + lam_v * sq * i
            dm = jnp.where(m > 0.0, dsq * 0.5 / jnp.where(m > 0.0, sq, 1.0), 0.0)
            dla = da * a - 2.0 * dm * a * a
            dr = dla * (-LRU_C) * sp
            dsp = _colsum(dla * (-LRU_C) * r)
            dlams.append(dsp * -jax.nn.sigmoid(-lam_ref[d:d + 1, :]))
            dg_s[:, (2 * d) * LANES:(2 * d + 1) * LANES] = (dr * r * (1.0 - r)).astype(BF16)
            dg_s[:, (2 * d + 1) * LANES:(2 * d + 2) * LANES] = (di * i * (1.0 - i)).astype(BF16)
        dgates = dg_s[...]
        dxc = dxc + _dot_nt(dgates, w_ref[0])
        dxr_ref[...] = (cw_ref[0:1, :] * _shift_rows(dxc, -2, rows, t) + cw_ref[1:2, :] * _shift_rows(dxc, -1, rows, t)
                        + cw_ref[2:3, :] * dxc + cw_ref[3:4, :] * _shift_rows(dxc, 1, rows, t))
        dcw = jnp.concatenate([_colsum(dxc * _shift_rows(xr, 2 - j, rows, t)) for j in range(4)], axis=0)
        _acc(dcw_ref, first, dcw)
        _acc(dcb_ref, first, _colsum(dxc))
        _acc(dw_ref, first, _dot_tn(xcb, dgates)[None])
        _acc(dgb_ref, first, _colsum(dgates.astype(F32))[None])
        _acc(dlam_ref, first, jnp.concatenate(dlams, axis=0))

    return pl.pallas_call(
        body, name="rnn_bwd", grid=(D_RNN // LANES, n // t),
        in_specs=[seq, seq, seq, cw, vec1, wspec, gspec, vec2],
        out_specs=[seq, seq, cw, vec1, wspec, gspec, vec2],
        out_shape=[jax.ShapeDtypeStruct((n, D_RNN), F32), jax.ShapeDtypeStruct((n, D_RNN), F32),
                   jax.ShapeDtypeStruct((4, D_RNN), F32), jax.ShapeDtypeStruct((1, D_RNN), F32),
                   jax.ShapeDtypeStruct((D_RNN // LANES, LANES, 4 * LANES), F32),
                   jax.ShapeDtypeStruct((D_RNN // LANES, 1, 4 * LANES), F32), jax.ShapeDtypeStruct((2, D_RNN), F32)],
        scratch_shapes=[pltpu.VMEM((2, t, LANES), F32), pltpu.VMEM((2, t, LANES), F32), pltpu.VMEM((2, t, LANES), F32),
                        pltpu.VMEM((2, t, LANES), F32), pltpu.VMEM((t, 4 * LANES), BF16)],
        compiler_params=pltpu.CompilerParams(dimension_semantics=("parallel", "arbitrary"), vmem_limit_bytes=VMEM_LIMIT),
    )(xr, xg, do, conv_w, conv_b, wblk, gbias, lam)


def _post(oa, orn, h0, tgt, ga, gr, g2, w_out, w_gate, w_up, w_down):
    n = oa.shape[0]
    tm = _row_tile(n)
    t = _t_pad()

    def body(oa_ref, or_ref, h0_ref, tgt_ref, ga_ref, gr_ref, g2_ref, wo_ref, wg_ref, wu_ref, wd_ref,
             doa_ref, dor_ref, dh1_ref, mix_ref, h1n_ref, act_ref, dgate_ref, dup_ref, dy_ref,
             loss_ref, dga_ref, dgr_ref, dg2_ref, gate_s, up_s):
        first = pl.program_id(0) == 0
        xa, ra = _rms(oa_ref[...], D_ATTN)
        xr, rr = _rms(or_ref[...], D_RNN)
        mix_ref[:, :D_ATTN] = (xa * ga_ref[...]).astype(BF16)
        mix_ref[:, D_ATTN:] = (xr * gr_ref[...]).astype(BF16)
        h1 = h0_ref[...] + _dot(mix_ref[...], wo_ref[...])
        x2, r2 = _rms(h1, D_MODEL)
        h1n = (x2 * g2_ref[...]).astype(BF16)
        h1n_ref[...] = h1n
        y = h1
        for cs in range(0, D_FF, FF_CHUNK):
            sl = slice(cs, cs + FF_CHUNK)
            gate = _dot(h1n, wg_ref[:, sl])
            up = _dot(h1n, wu_ref[:, sl])
            gate_s[:, sl] = gate
            up_s[:, sl] = up
            act = (gate * jax.nn.sigmoid(gate) * up).astype(BF16)
            act_ref[:, sl] = act
            y = y + _dot(act, wd_ref[sl, :])
        row = pl.program_id(0) * tm + lax.broadcasted_iota(jnp.int32, (tm, 1), 0)
        for _ in range(1, n // t):
            row = jnp.where(row >= t, row - t, row)
        err = jnp.where(row >= PAD_ROWS + N_META, y - tgt_ref[...], 0.0)
        _acc(loss_ref, first, jnp.full((1, LANES), 0.5 / D_MODEL, F32) * jnp.sum(err * err))
        dy = err * (1.0 / D_MODEL)
        dyb = dy.astype(BF16)
        dy_ref[...] = dyb
        dh1n = jnp.zeros((tm, D_MODEL), F32)
        for cs in range(0, D_FF, FF_CHUNK):
            sl = slice(cs, cs + FF_CHUNK)
            dact = _dot_nt(dyb, wd_ref[sl, :])
            gate, up = gate_s[:, sl], up_s[:, sl]
            sg = jax.nn.sigmoid(gate)
            dgate = (dact * up * sg * (1.0 + gate * (1.0 - sg))).astype(BF16)
            dup = (dact * gate * sg).astype(BF16)
            dgate_ref[:, sl] = dgate
            dup_ref[:, sl] = dup
            dh1n = dh1n + _dot_nt(dgate, wg_ref[:, sl]) + _dot_nt(dup, wu_ref[:, sl])
        _acc(dg2_ref, first, _colsum(dh1n * x2))
        dh1 = dy + _rms_bwd(dh1n, x2, r2, g2_ref[...], D_MODEL)
        dh1_ref[...] = dh1
        dmix = _dot_nt(dh1.astype(BF16), wo_ref[...])
        dma, dmr = dmix[:, :D_ATTN], dmix[:, D_ATTN:]
        _acc(dga_ref, first, _colsum(dma * xa))
        _acc(dgr_ref, first, _colsum(dmr * xr))
        doa_ref[...] = _rms_bwd(dma, xa, ra, ga_ref[...], D_ATTN)
        dor_ref[...] = _rms_bwd(dmr, xr, rr, gr_ref[...], D_RNN)

    def row(w):
        return pl.BlockSpec((tm, w), lambda i: (i, 0))

    def acc(w):
        return pl.BlockSpec((1, w), lambda i: (0, 0))

    outs = [(D_ATTN, F32), (D_RNN, F32), (D_MODEL, F32), (D_MODEL, BF16), (D_MODEL, BF16), (D_FF, BF16),
            (D_FF, BF16), (D_FF, BF16), (D_MODEL, BF16)]
    accs = [LANES, D_ATTN, D_RNN, D_MODEL]
    return pl.pallas_call(
        body, name="post", grid=(n // tm,),
        in_specs=[row(D_ATTN), row(D_RNN), row(D_MODEL), row(D_MODEL),
                  _const_spec((1, D_ATTN)), _const_spec((1, D_RNN)), _const_spec((1, D_MODEL)),
                  _const_spec((D_MODEL, D_MODEL)), _const_spec((D_MODEL, D_FF)), _const_spec((D_MODEL, D_FF)),
                  _const_spec((D_FF, D_MODEL))],
        out_specs=[row(w) for w, _ in outs] + [acc(w) for w in accs],
        out_shape=[jax.ShapeDtypeStruct((n, w), dt) for w, dt in outs]
        + [jax.ShapeDtypeStruct((1, w), F32) for w in accs],
        scratch_shapes=[pltpu.VMEM((tm, D_FF), F32), pltpu.VMEM((tm, D_FF), F32)],
        compiler_params=pltpu.CompilerParams(dimension_semantics=("arbitrary",), vmem_limit_bytes=VMEM_LIMIT),
    )(oa, orn, h0, tgt, ga, gr, g2, w_out, w_gate, w_up, w_down)


def _in_bwd(dcq, dckv, dxr, dxg, dkr, h0, dh1, ln1_g, w_in_p):
    n = h0.shape[0]
    tm = _row_tile(n)

    def body(dcq_ref, dckv_ref, dxr_ref, dxg_ref, dkr_ref, h0_ref, dh1_ref, g_ref, w_ref, dp_ref, dh0_ref, dg_ref):
        first = pl.program_id(0) == 0
        dp_ref[:, :Q_LORA] = dcq_ref[...]
        dp_ref[:, Q_LORA:OFF_CKV] = dckv_ref[...]
        dp_ref[:, OFF_CKV:OFF_CKV + D_RNN] = dxr_ref[...].astype(BF16)
        dp_ref[:, OFF_CKV + D_RNN:OFF_CKV + 2 * D_RNN] = dxg_ref[...].astype(BF16)
        dp_ref[:, OFF_CKV + 2 * D_RNN:] = dkr_ref[...]
        dhn = _dot_nt(dp_ref[...], w_ref[...])
        xhat, r = _rms(h0_ref[...], D_MODEL)
        _acc(dg_ref, first, _colsum(dhn * xhat))
        dh0_ref[...] = dh1_ref[...] + _rms_bwd(dhn, xhat, r, g_ref[...], D_MODEL)

    def row(w):
        return pl.BlockSpec((tm, w), lambda i: (i, 0))

    return pl.pallas_call(
        body, name="in_bwd", grid=(n // tm,),
        in_specs=[row(Q_LORA), row(KV_LORA), row(D_RNN), row(D_RNN), row(LANES), row(D_MODEL), row(D_MODEL),
                  _const_spec((1, D_MODEL)), _const_spec((D_MODEL, P_COLS))],
        out_specs=[row(P_COLS), row(D_MODEL), pl.BlockSpec((1, D_MODEL), lambda i: (0, 0))],
        out_shape=[jax.ShapeDtypeStruct((n, P_COLS), BF16), jax.ShapeDtypeStruct((n, D_MODEL), F32),
                   jax.ShapeDtypeStruct((1, D_MODEL), F32)],
        compiler_params=pltpu.CompilerParams(dimension_semantics=("arbitrary",), vmem_limit_bytes=VMEM_LIMIT),
    )(dcq, dckv, dxr, dxg, dkr, h0, dh1, ln1_g, w_in_p)


def _pick_tile(width, cap):
    best = LANES
    for mult in range(1, width // LANES + 1):
        cand = mult * LANES
        if width % cand == 0 and cand <= cap:
            best = cand
    return best


def _matmul_tn(name, a, b):
    n, ka = a.shape
    kb = b.shape[1]
    ta, tb = _pick_tile(ka, 1408), _pick_tile(kb, 1408)
    tk = n // 4

    def body(a_ref, b_ref, o_ref):
        _acc(o_ref, pl.program_id(2) == 0, _dot_tn(a_ref[...].astype(BF16), b_ref[...].astype(BF16)))

    return pl.pallas_call(
        body, name=name, grid=(ka // ta, kb // tb, n // tk),
        in_specs=[pl.BlockSpec((tk, ta), lambda i, j, k: (k, i)), pl.BlockSpec((tk, tb), lambda i, j, k: (k, j))],
        out_specs=pl.BlockSpec((ta, tb), lambda i, j, k: (i, j)),
        out_shape=jax.ShapeDtypeStruct((ka, kb), F32),
        compiler_params=pltpu.CompilerParams(dimension_semantics=("parallel", "parallel", "arbitrary"),
                                             vmem_limit_bytes=VMEM_LIMIT),
    )(a, b)


def _adamw(name, g8, w, m, v):
    rows, cols = w.shape
    tr = rows
    for cand in (256, 176, 128, 64):
        if rows % cand == 0 and rows > cand:
            tr = cand
            break

    def body(g8_ref, w_ref, m_ref, v_ref, g_ref, d_ref, nm_ref, nv_ref):
        g = g8_ref[0].astype(F32)
        for s in range(1, N_DEV):
            g = g + g8_ref[s].astype(F32)
        g_ref[...] = g
        nm = ADAM_B1 * m_ref[...] + (1.0 - ADAM_B1) * g
        nv = ADAM_B2 * v_ref[...] + (1.0 - ADAM_B2) * (g * g)
        nm_ref[...] = nm
        nv_ref[...] = nv
        m_hat = nm / (1.0 - ADAM_B1 ** ADAM_STEP)
        v_hat = nv / (1.0 - ADAM_B2 ** ADAM_STEP)
        d_ref[...] = -ADAM_LR * (m_hat / (jnp.sqrt(v_hat) + ADAM_EPS) + ADAM_WD * w_ref[...])

    blk = pl.BlockSpec((tr, cols), lambda i: (i, 0))
    return pl.pallas_call(
        body, name=name, grid=(rows // tr,),
        in_specs=[pl.BlockSpec((N_DEV, tr, cols), lambda i: (0, i, 0)), blk, blk, blk],
        out_specs=[blk] * 4, out_shape=[jax.ShapeDtypeStruct((rows, cols), F32)] * 4,
        compiler_params=pltpu.CompilerParams(dimension_semantics=("parallel",), vmem_limit_bytes=VMEM_LIMIT),
    )(g8, w, m, v)


def _exchange_specs(srcs, scatter):
    nk = len(srcs)
    if not nk:
        return [], [], [], []
    any_spec = pl.BlockSpec(memory_space=pl.ANY)
    out_shape = [jax.ShapeDtypeStruct(s.shape if sc else (N_DEV,) + s.shape, s.dtype) for s, sc in zip(srcs, scatter)]
    sems = [pltpu.SemaphoreType.DMA((nk, N_DEV - 1)), pltpu.SemaphoreType.DMA((nk, N_DEV - 1)),
            pltpu.SemaphoreType.DMA((nk,))]
    return [any_spec] * nk, [any_spec] * nk, out_shape, sems


def _exchange_fns(src_refs, out_refs, sems, scatter):
    nk = len(src_refs)
    if not nk:
        return (lambda: None), (lambda: None)
    send_sems, recv_sems, local_sems = sems

    def plan():
        x, y, c = lax.axis_index("x"), lax.axis_index("y"), lax.axis_index("c")
        me = 4 * x + 2 * y + c
        flips = [(f >> 2 & 1, f >> 1 & 1, f & 1) for f in range(1, N_DEV)]
        peers = [(1 - x if fx else x, 1 - y if fy else y, 1 - c if fc else c) for fx, fy, fc in flips]
        pids = [4 * px + 2 * py + pc for px, py, pc in peers]

        def src_for(k, dest):
            return src_refs[k].at[dest] if scatter[k] else src_refs[k]

        def remote(k, j, dst_slot, dest):
            return pltpu.make_async_remote_copy(
                src_ref=src_for(k, dest), dst_ref=out_refs[k].at[dst_slot],
                send_sem=send_sems.at[k, j], recv_sem=recv_sems.at[k, j],
                device_id=peers[j], device_id_type=MESH)

        local = [pltpu.make_async_copy(src_for(k, me), out_refs[k].at[me], local_sems.at[k]) for k in range(nk)]
        sends = [remote(k, j, me, pids[j]) for k in range(nk) for j in range(N_DEV - 1)]
        recvs = [remote(k, j, pids[j], pids[j]) for k in range(nk) for j in range(N_DEV - 1)]
        return local, sends, recvs

    def start():
        local, sends, _ = plan()
        for cp in local + sends:
            cp.start()

    def wait():
        local, sends, recvs = plan()
        for cp in recvs:
            cp.wait_recv()
        for cp in sends:
            cp.wait_send()
        for cp in local:
            cp.wait()

    return start, wait


def _exchange(name, srcs, scatter):
    nk = len(srcs)
    c_in, c_out, c_shape, c_sems = _exchange_specs(srcs, scatter)

    def body(*refs):
        start, wait = _exchange_fns(refs[:nk], refs[nk:2 * nk], refs[2 * nk:], scatter)
        start()
        wait()

    return pl.pallas_call(body, name=name, in_specs=c_in, out_specs=c_out, out_shape=c_shape, scratch_shapes=c_sems)(*srcs)


def _cols_from_shards(g):
    return jnp.transpose(g, (1, 0, 2)).reshape(g.shape[1], -1)


def _cols_to_shards(w):
    return jnp.transpose(w.reshape(w.shape[0], N_DEV, -1), (1, 0, 2))


def _rope_tables(n):
    t = _t_pad()
    pos = (jnp.arange(t, dtype=F32) - PAD_ROWS)
    half = QK_ROPE // 2
    freqs = 1.0 / (ROPE_THETA ** (jnp.arange(half, dtype=F32) / half))
    ang = pos[:, None] * freqs[None, :]
    cos, sin = jnp.cos(ang), jnp.sin(ang)
    z = lambda w: jnp.zeros((t, w), F32)
    c = jnp.concatenate([jnp.ones((t, QK_NOPE), F32), cos, cos, z(HEAD_PAD - QK_HEAD)], axis=1)
    s1 = jnp.concatenate([z(QK_NOPE + half), sin, z(HEAD_PAD - QK_HEAD)], axis=1)
    s2 = jnp.concatenate([z(QK_NOPE), -sin, z(HEAD_PAD - QK_NOPE - half)], axis=1)
    reps = n // t
    return tuple(jnp.tile(a, (reps, 1)) for a in (c, s1, s2))


def _block_diag_gates(lru_wa, lru_wi):
    eye = jnp.eye(2, dtype=lru_wa.dtype)

    def bd(w):
        w = w.reshape(2, D_RNN // LANES, 2, RNN_BW, RNN_BW)
        full = w[:, :, :, :, None, :] * eye[None, None, :, None, :, None]
        return full.reshape(2, D_RNN // LANES, LANES, LANES)

    a, i = bd(lru_wa), bd(lru_wi)
    return jnp.concatenate([a[0], i[0], a[1], i[1]], axis=-1)


def _unblock_gates(dw):
    nb = D_RNN // LANES
    parts = dw.reshape(nb, 2, RNN_BW, 4, 2, RNN_BW)
    diag = jnp.stack([parts[:, k, :, :, k, :] for k in range(2)], axis=1)
    diag = jnp.transpose(diag, (3, 0, 1, 2, 4)).reshape(4, 2 * nb, RNN_BW, RNN_BW)
    return jnp.stack([diag[0], diag[2]]), jnp.stack([diag[1], diag[3]])


LATE = ("w_out", "w_gate", "w_up", "w_down")


def _local_step(x, tgt, meta, w, attn_fwd, attn_bwd):
    nb = x.shape[0]
    t = _t_pad()
    n = nb * t
    lead = jnp.zeros((nb, PAD_ROWS, D_MODEL), F32)
    h0 = jnp.concatenate([lead, jnp.broadcast_to(meta[None], (nb, N_META, D_MODEL)), x], axis=1).reshape(n, D_MODEL)
    tgt_p = jnp.concatenate([jnp.zeros((nb, PAD_ROWS + N_META, D_MODEL), F32), tgt], axis=1).reshape(n, D_MODEL)

    w_in = w["w_in"]
    zc = lambda c: jnp.zeros((D_MODEL, c), w_in.dtype)
    w_in_p = jnp.concatenate([w_in[:, :OFF_CKV], w_in[:, OFF_KR:], zc(QK_NOPE), w_in[:, OFF_CKV:OFF_KR],
                              zc(HEAD_PAD - QK_HEAD)], axis=1)
    w_uq_p = jnp.pad(w["w_uq"].reshape(Q_LORA, N_HEADS, QK_HEAD), ((0, 0), (0, 0), (0, HEAD_PAD - QK_HEAD))
                     ).reshape(Q_LORA, QP_COLS)
    ukv = w["w_ukv"].reshape(KV_LORA, N_HEADS, QK_NOPE + V_HEAD)
    w_uk_p = jnp.pad(ukv[:, :, :QK_NOPE], ((0, 0), (0, 0), (0, HEAD_PAD - QK_NOPE))).reshape(KV_LORA, QP_COLS)
    w_v = ukv[:, :, QK_NOPE:].reshape(KV_LORA, D_ATTN)
    pad_g = lambda g: jnp.pad(g, ((0, 0), (0, HEAD_PAD - QK_HEAD)))
    qg, kg = pad_g(w["q_norm_g"]), pad_g(w["k_norm_g"])
    rc, rs1, rs2 = _rope_tables(n)
    wblk = _block_diag_gates(w["lru_wa"], w["lru_wi"]).astype(BF16)
    nblk = D_RNN // LANES
    gbias = jnp.stack([w["lru_ba"][0], w["lru_bi"][0], w["lru_ba"][1], w["lru_bi"][1]], axis=0)
    gbias = jnp.transpose(gbias.reshape(4, nblk, LANES), (1, 0, 2)).reshape(nblk, 1, 4 * LANES)

    hn, cq, ckv, xr, xg, kr = _in_proj(h0, w["ln1_g"], w_in_p)
    q, k, v = _qkv_fwd(cq, ckv, kr, w["q_a_norm_g"], w["kv_a_norm_g"], w_uq_p, w_uk_p, w_v, qg, kg, rc, rs1, rs2)
    oa, late = attn_fwd(q, k, v)
    orn = _rnn_fwd(xr, xg, w["conv_w"], w["conv_b"], wblk, gbias, w["lru_lambda"])
    (doa, dor, dh1, mix, h1n, act, dgate, dup, dyb, loss, dga, dgr, dg2) = _post(
        oa, orn, h0, tgt_p, w["attn_out_g"], w["rnn_out_g"], w["ln2_g"], late["w_out"], late["w_gate"], late["w_up"],
        late["w_down"])
    dw_out = _matmul_tn("dw_out", mix, dh1)
    dw_gate = _matmul_tn("dw_gate", h1n, dgate)
    dw_up = _matmul_tn("dw_up", h1n, dup)
    dw_down = _matmul_tn("dw_down", act, dyb)
    (dq_r, dk_r, dv), late_result = attn_bwd(q, k, v, doa, {"w_out": dw_out, "w_gate": dw_gate, "w_up": dw_up,
                                                         "w_down": dw_down})
    dxr, dxg, dcw, dcb, dwblk, dgb, dlam = _rnn_bwd(xr, xg, dor, w["conv_w"], w["conv_b"], wblk, gbias, w["lru_lambda"])
    (dcq, dckv, dkr, qa, kva, dqp, dkv, dqg, dkg, dgqa, dgkva) = _qkv_bwd(
        cq, ckv, kr, dq_r, dk_r, dv, w["q_a_norm_g"], w["kv_a_norm_g"], w_uq_p, w_uk_p, w_v, qg, kg, rc, rs1, rs2)
    dp, dh0, dg1 = _in_bwd(dcq, dckv, dxr, dxg, dkr, h0, dh1, w["ln1_g"], w_in_p)
    dw_in_p = _matmul_tn("dw_in", hn, dp)
    dw_uq_p = _matmul_tn("dw_uq", qa, dqp)
    dw_kv = _matmul_tn("dw_ukv", kva, dkv)

    dh0 = dh0.reshape(nb, t, D_MODEL)
    kr0 = OFF_CKV + 2 * D_RNN + QK_NOPE
    dw_in = jnp.concatenate([dw_in_p[:, :OFF_CKV], dw_in_p[:, kr0:kr0 + QK_ROPE], dw_in_p[:, OFF_CKV:OFF_CKV + 2 * D_RNN]],
                            axis=1)
    dw_uq = dw_uq_p.reshape(Q_LORA, N_HEADS, HEAD_PAD)[:, :, :QK_HEAD].reshape(Q_LORA, N_HEADS * QK_HEAD)
    dw_ukv = jnp.concatenate([dw_kv[:, :QP_COLS].reshape(KV_LORA, N_HEADS, HEAD_PAD)[:, :, :QK_NOPE],
                              dw_kv[:, QP_COLS:].reshape(KV_LORA, N_HEADS, V_HEAD)], axis=2).reshape(KV_LORA, -1)
    dwa, dwi = _unblock_gates(dwblk)
    dgb = jnp.transpose(dgb.reshape(nblk, 4, LANES), (1, 0, 2)).reshape(4, D_RNN)
    grads = {
        "meta_tokens": jnp.sum(dh0[:, PAD_ROWS:PAD_ROWS + N_META], axis=0),
        "ln1_g": dg1, "w_in": dw_in, "q_a_norm_g": dgqa, "w_uq": dw_uq, "kv_a_norm_g": dgkva, "w_ukv": dw_ukv,
        "q_norm_g": dqg[:, :QK_HEAD], "k_norm_g": dkg[:, :QK_HEAD], "conv_w": dcw, "conv_b": dcb,
        "lru_wa": dwa, "lru_ba": jnp.stack([dgb[0], dgb[2]]), "lru_wi": dwi, "lru_bi": jnp.stack([dgb[1], dgb[3]]),
        "lru_lambda": dlam, "attn_out_g": dga, "rnn_out_g": dgr, "w_out": dw_out, "ln2_g": dg2,
        "w_gate": dw_gate, "w_up": dw_up, "w_down": dw_down,
    }
    return loss[0, 0], dh0[:, PAD_ROWS + N_META:], grads, late_result


WEIGHTS = ("meta_tokens", "ln1_g", "w_in", "q_a_norm_g", "w_uq", "kv_a_norm_g", "w_ukv", "q_norm_g", "k_norm_g",
           "conv_w", "conv_b", "lru_wa", "lru_ba", "lru_wi", "lru_bi", "lru_lambda", "attn_out_g", "rnn_out_g",
           "w_out", "ln2_g", "w_gate", "w_up", "w_down")
BIG = ("w_in", "w_uq", "w_ukv", "w_out", "w_gate", "w_up", "w_down")
ROW_SHARDED = ("w_out", "w_down")
SMALL_SHARDED = ("meta_tokens", "conv_w", "lru_ba", "lru_bi", "lru_lambda")
REPLICATED = ("ln1_g", "q_a_norm_g", "kv_a_norm_g", "q_norm_g", "k_norm_g", "conv_b", "lru_wa", "lru_wi",
              "attn_out_g", "rnn_out_g", "ln2_g")
SMALL_ROWS = 32


def _pack_sharded(parts):
    padded = [jnp.pad(parts[k], [(0, 0)] * (parts[k].ndim - 1) + [(0, LANES - parts[k].shape[-1])]) for k in SMALL_SHARDED]
    cat = jnp.concatenate(padded, axis=-2)
    return jnp.pad(cat, [(0, 0)] * (cat.ndim - 2) + [(0, SMALL_ROWS - cat.shape[-2]), (0, 0)])


def _unpack_sharded(buf, like):
    out, r0 = {}, 0
    for k in SMALL_SHARDED:
        r, c = like[k].shape[-2:]
        out[k] = buf[..., r0:r0 + r, :c]
        r0 += r
    return out


def _pack_replicated(parts):
    rows = []
    for k in REPLICATED:
        flat = parts[k].reshape(-1)
        rows.append(jnp.pad(flat, (0, -flat.shape[0] % LANES)).reshape(-1, LANES))
    cat = jnp.concatenate(rows, axis=0)
    return jnp.pad(cat, ((0, -cat.shape[0] % 8), (0, 0)))


def _unpack_replicated(buf, like):
    out, r0 = {}, 0
    for k in REPLICATED:
        size = math.prod(like[k].shape)
        r = -(-size // LANES)
        out[k] = buf[r0:r0 + r].reshape(-1)[:size].reshape(like[k].shape)
        r0 += r
    return out


def kernel(x, meta_tokens, ln1_g, w_in, q_a_norm_g, w_uq, kv_a_norm_g, w_ukv, q_norm_g, k_norm_g, conv_w, conv_b, lru_wa, lru_ba, lru_wi, lru_bi, lru_lambda, attn_out_g, rnn_out_g, w_out, ln2_g, w_gate, w_up, w_down, loss_target, m_meta_tokens, m_ln1_g, m_w_in, m_q_a_norm_g, m_w_uq, m_kv_a_norm_g, m_w_ukv, m_q_norm_g, m_k_norm_g, m_conv_w, m_conv_b, m_lru_wa, m_lru_ba, m_lru_wi, m_lru_bi, m_lru_lambda, m_attn_out_g, m_rnn_out_g, m_w_out, m_ln2_g, m_w_gate, m_w_up, m_w_down, v_meta_tokens, v_ln1_g, v_w_in, v_q_a_norm_g, v_w_uq, v_kv_a_norm_g, v_w_ukv, v_q_norm_g, v_k_norm_g, v_conv_w, v_conv_b, v_lru_wa, v_lru_ba, v_lru_wi, v_lru_bi, v_lru_lambda, v_attn_out_g, v_rnn_out_g, v_w_out, v_ln2_g, v_w_gate, v_w_up, v_w_down):
    given = (meta_tokens, ln1_g, w_in, q_a_norm_g, w_uq, kv_a_norm_g, w_ukv, q_norm_g, k_norm_g, conv_w, conv_b,
             lru_wa, lru_ba, lru_wi, lru_bi, lru_lambda, attn_out_g, rnn_out_g, w_out, ln2_g, w_gate, w_up, w_down)
    moments_m = (m_meta_tokens, m_ln1_g, m_w_in, m_q_a_norm_g, m_w_uq, m_kv_a_norm_g, m_w_ukv, m_q_norm_g, m_k_norm_g,
                 m_conv_w, m_conv_b, m_lru_wa, m_lru_ba, m_lru_wi, m_lru_bi, m_lru_lambda, m_attn_out_g, m_rnn_out_g,
                 m_w_out, m_ln2_g, m_w_gate, m_w_up, m_w_down)
    moments_v = (v_meta_tokens, v_ln1_g, v_w_in, v_q_a_norm_g, v_w_uq, v_kv_a_norm_g, v_w_ukv, v_q_norm_g, v_k_norm_g,
                 v_conv_w, v_conv_b, v_lru_wa, v_lru_ba, v_lru_wi, v_lru_bi, v_lru_lambda, v_attn_out_g, v_rnn_out_g,
                 v_w_out, v_ln2_g, v_w_gate, v_w_up, v_w_down)
    shapes = {k: a.shape for k, a in zip(WEIGHTS, given)}

    def strip(a, k):
        return a if k == "meta_tokens" or a.ndim == 2 else a.reshape(a.shape[1:])

    w = {k: strip(a, k) for k, a in zip(WEIGHTS, given)}
    m = {k: strip(a, k) for k, a in zip(WEIGHTS, moments_m)}
    v = {k: strip(a, k) for k, a in zip(WEIGHTS, moments_v)}

    def from_shards(k, g):
        return g.reshape(-1, g.shape[-1]) if k in ROW_SHARDED else _cols_from_shards(g)

    def to_shards(k, g):
        return (g.reshape(N_DEV, -1, g.shape[-1]) if k in ROW_SHARDED else _cols_to_shards(g)).astype(BF16)

    early = [k for k in BIG if k not in LATE]
    gathered = _exchange("gather_early", [w[k].astype(BF16) for k in early] + [_pack_sharded(w)],
                         [False] * (len(early) + 1))
    full = {k: w[k] for k in REPLICATED}
    for k, g in zip(early, gathered):
        full[k] = from_shards(k, g)
    for k, g in _unpack_sharded(gathered[-1], w).items():
        full[k] = _cols_from_shards(g)

    def attn_fwd(q, k_, v_):
        oa, got = _attn_fwd(q, k_, v_, [w[k].astype(BF16) for k in LATE], [False] * len(LATE))
        return oa, {k: from_shards(k, g) for k, g in zip(LATE, got)}

    def attn_bwd(q, k_, v_, doa, late_grads):
        return _attn_bwd(q, k_, v_, doa, [to_shards(k, late_grads[k]) for k in LATE], [True] * len(LATE))

    loss_part, grad_x, grads, late_parts = _local_step(x, loss_target, full["meta_tokens"], full, attn_fwd, attn_bwd)

    send = [to_shards(k, grads[k]) for k in early]
    send.append(_pack_sharded({k: _cols_to_shards(grads[k]) for k in SMALL_SHARDED}))
    send.append(_pack_replicated({k: grads[k] for k in REPLICATED}))
    parts = _exchange("reduce_rest", send, [True] * (len(early) + 1) + [False])

    new = {}
    for k, g8 in list(zip(LATE, late_parts)) + list(zip(early, parts)):
        new[k] = _adamw("adamw_" + k, g8, w[k], m[k], v[k])
    packed = _adamw("adamw_small_sharded", parts[len(early)], _pack_sharded(w), _pack_sharded(m), _pack_sharded(v))
    for k, vals in zip(SMALL_SHARDED, zip(*[_unpack_sharded(a, w).values() for a in packed])):
        new[k] = vals
    packed = _adamw("adamw_replicated", parts[len(early) + 1], _pack_replicated(w), _pack_replicated(m), _pack_replicated(v))
    for k, vals in zip(REPLICATED, zip(*[_unpack_replicated(a, w).values() for a in packed])):
        new[k] = vals

    loss = lax.psum(loss_part, ("x", "y", "c"))
    outs = [loss, grad_x]
    for idx in range(4):
        outs += [new[k][idx].reshape(shapes[k]) for k in WEIGHTS]
    return tuple(outs)
```

```python
import functools
import math

import jax
import jax.numpy as jnp
from jax import lax
from jax.experimental import pallas as pl
from jax.experimental.pallas import tpu as pltpu

F32 = jnp.float32
BF16 = jnp.bfloat16

D_MODEL = 1024
N_META = 16
SEQ = 2048
N_HEADS = 8
QK_NOPE = 64
QK_ROPE = 32
QK_HEAD = QK_NOPE + QK_ROPE
V_HEAD = 64
D_ATTN = N_HEADS * V_HEAD
Q_LORA = 384
KV_LORA = 256
D_RNN = 512
RNN_BW = 64
D_FF = 2816
EPS = 1e-6
LRU_C = 8.0
ROPE_THETA = 10000.0
OFF_CKV = Q_LORA + KV_LORA
OFF_KR = OFF_CKV + QK_ROPE
IN_COLS = OFF_KR + 2 * D_RNN

ADAM_LR = 0.001
ADAM_B1 = 0.9
ADAM_B2 = 0.999
ADAM_EPS = 1e-08
ADAM_WD = 0.01
ADAM_STEP = 10

N_DEV = 8
LANES = 128
HEAD_PAD = LANES
PAD_ROWS = LANES - N_META
QP_COLS = N_HEADS * HEAD_PAD
P_COLS = OFF_CKV + 2 * D_RNN + LANES
FF_CHUNK = D_FF // 2
VMEM_LIMIT = 56 * 1024 * 1024
MESH = pl.DeviceIdType.MESH


def _t_pad():
    return PAD_ROWS + N_META + SEQ


def _row_tile(n):
    return 256 if n % 256 == 0 else 128


def _const_spec(shape):
    nd = len(shape)
    return pl.BlockSpec(shape, lambda *_: (0,) * nd, pipeline_mode=pl.Buffered(1))


def _rms(x, d):
    r = lax.rsqrt(jnp.sum(x * x, axis=-1, keepdims=True) * (1.0 / d) + EPS)
    return x * r, r


def _rms_bwd(dy, xhat, r, g, d):
    dxh = dy * g
    return r * (dxh - xhat * (jnp.sum(dxh * xhat, axis=-1, keepdims=True) * (1.0 / d)))


def _colsum(x):
    return jnp.sum(x, axis=0, keepdims=True)


def _dot(a, b):
    return jnp.dot(a, b, preferred_element_type=F32)


def _dot_nt(a, b):
    return lax.dot_general(a, b, (((1,), (1,)), ((), ())), preferred_element_type=F32)


def _dot_tn(a, b):
    return lax.dot_general(a, b, (((0,), (0,)), ((), ())), preferred_element_type=F32)


def _rope(x, c, s1, s2):
    return x * c + pltpu.roll(x, 16, 1) * s1 + pltpu.roll(x, HEAD_PAD - 16, 1) * s2


def _rope_bwd(dy, c, s1, s2):
    return dy * c + pltpu.roll(dy * s1, HEAD_PAD - 16, 1) + pltpu.roll(dy * s2, 16, 1)


def _acc(ref, first, val):
    @pl.when(first)
    def _():
        ref[...] = val

    @pl.when(jnp.logical_not(first))
    def _():
        ref[...] += val


def _grid_ends(rank):
    first = last = None
    for axis in range(rank):
        pid, size = pl.program_id(axis), pl.num_programs(axis)
        first = pid == 0 if first is None else jnp.logical_and(first, pid == 0)
        last = pid == size - 1 if last is None else jnp.logical_and(last, pid == size - 1)
    return first, last


def _in_proj(h0, ln1_g, w_in_p, srcs=(), scatter=()):
    n = h0.shape[0]
    tm = _row_tile(n)
    nk = len(srcs)
    c_in, c_out, c_shape, c_sems = _exchange_specs(srcs, scatter)

    def body(h_ref, g_ref, w_ref, *rest):
        hn_ref, cq_ref, ckv_ref, xr_ref, xg_ref, kr_ref = rest[nk:nk + 6]
        start, wait = _exchange_fns(rest[:nk], rest[nk + 6:2 * nk + 6], rest[2 * nk + 6:], scatter)
        first_step, last_step = _grid_ends(1)
        pl.when(first_step)(start)
        xhat, _ = _rms(h_ref[...], D_MODEL)
        hn = (xhat * g_ref[...]).astype(BF16)
        hn_ref[...] = hn
        p = _dot(hn, w_ref[...])
        cq_ref[...] = p[:, :Q_LORA]
        ckv_ref[...] = p[:, Q_LORA:OFF_CKV]
        xr_ref[...] = p[:, OFF_CKV:OFF_CKV + D_RNN]
        xg_ref[...] = p[:, OFF_CKV + D_RNN:OFF_CKV + 2 * D_RNN]
        kr_ref[...] = p[:, OFF_CKV + 2 * D_RNN:]
        pl.when(last_step)(wait)

    def row(w):
        return pl.BlockSpec((tm, w), lambda i: (i, 0))

    widths = (D_MODEL, Q_LORA, KV_LORA, D_RNN, D_RNN, LANES)
    res = pl.pallas_call(
        body, name="in_proj", grid=(n // tm,),
        in_specs=[row(D_MODEL), _const_spec((1, D_MODEL)), _const_spec((D_MODEL, P_COLS))] + c_in,
        out_specs=[row(w) for w in widths] + c_out,
        out_shape=[jax.ShapeDtypeStruct((n, w), BF16 if k == 0 else F32) for k, w in enumerate(widths)] + c_shape,
        scratch_shapes=c_sems,
        compiler_params=pltpu.CompilerParams(dimension_semantics=("arbitrary",), vmem_limit_bytes=VMEM_LIMIT),
    )(h0, ln1_g, w_in_p, *srcs)
    return res[:6], res[6:]


def _qkv_fwd(cq, ckv, kr, gqa, gkva, w_uq_p, w_uk_p, w_v, qg, kg, rc, rs1, rs2):
    n = cq.shape[0]
    tm = _row_tile(n)

    def body(cq_ref, ckv_ref, kr_ref, gqa_ref, gkva_ref, wuq_ref, wuk_ref, wv_ref, qg_ref, kg_ref,
             c_ref, s1_ref, s2_ref, q_ref, k_ref, v_ref):
        xq, _ = _rms(cq_ref[...], Q_LORA)
        qa = (xq * gqa_ref[...]).astype(BF16)
        q = _dot(qa, wuq_ref[...])
        xkv, _ = _rms(ckv_ref[...], KV_LORA)
        kva = (xkv * gkva_ref[...]).astype(BF16)
        kn = _dot(kva, wuk_ref[...])
        v_ref[...] = _dot(kva, wv_ref[...]).astype(BF16)
        krp = kr_ref[...]
        c, s1, s2 = c_ref[...], s1_ref[...], s2_ref[...]
        for h in range(N_HEADS):
            sl = slice(h * HEAD_PAD, (h + 1) * HEAD_PAD)
            qh, _ = _rms(q[:, sl], QK_HEAD)
            q_ref[:, sl] = _rope(qh * qg_ref[...], c, s1, s2).astype(BF16)
            kh, _ = _rms(kn[:, sl] + krp, QK_HEAD)
            k_ref[:, sl] = _rope(kh * kg_ref[...], c, s1, s2).astype(BF16)

    def row(w):
        return pl.BlockSpec((tm, w), lambda i: (i, 0))

    return pl.pallas_call(
        body, name="qkv_fwd", grid=(n // tm,),
        in_specs=[row(Q_LORA), row(KV_LORA), row(LANES), _const_spec((1, Q_LORA)), _const_spec((1, KV_LORA)),
                  _const_spec((Q_LORA, QP_COLS)), _const_spec((KV_LORA, QP_COLS)), _const_spec((KV_LORA, D_ATTN)),
                  _const_spec((1, LANES)), _const_spec((1, LANES)), row(LANES), row(LANES), row(LANES)],
        out_specs=[row(QP_COLS), row(QP_COLS), row(D_ATTN)],
        out_shape=[jax.ShapeDtypeStruct((n, QP_COLS), BF16), jax.ShapeDtypeStruct((n, QP_COLS), BF16),
                   jax.ShapeDtypeStruct((n, D_ATTN), BF16)],
        compiler_params=pltpu.CompilerParams(dimension_semantics=("parallel",), vmem_limit_bytes=VMEM_LIMIT),
    )(cq, ckv, kr, gqa, gkva, w_uq_p, w_uk_p, w_v, qg, kg, rc, rs1, rs2)


def _qkv_bwd(cq, ckv, kr, dq_r, dk_r, dv, dxr, dxg, gqa, gkva, w_uq_p, w_uk_p, w_v, qg, kg, rc, rs1, rs2):
    n = cq.shape[0]
    tm = _row_tile(n)

    def body(cq_ref, ckv_ref, kr_ref, dq_ref, dk_ref, dv_ref, dxr_ref, dxg_ref, gqa_ref, gkva_ref, wuq_ref, wuk_ref,
             wv_ref, qg_ref, kg_ref, c_ref, s1_ref, s2_ref,
             dp_ref, qa_ref, kva_ref, dqp_ref, dkv_ref, dqg_ref, dkg_ref, dgqa_ref, dgkva_ref):
        first = pl.program_id(0) == 0
        dp_ref[:, OFF_CKV:OFF_CKV + D_RNN] = dxr_ref[...].astype(BF16)
        dp_ref[:, OFF_CKV + D_RNN:OFF_CKV + 2 * D_RNN] = dxg_ref[...].astype(BF16)
        xq, rq = _rms(cq_ref[...], Q_LORA)
        qa = (xq * gqa_ref[...]).astype(BF16)
        qa_ref[...] = qa
        q = _dot(qa, wuq_ref[...])
        xkv, rkv = _rms(ckv_ref[...], KV_LORA)
        kva = (xkv * gkva_ref[...]).astype(BF16)
        kva_ref[...] = kva
        kn = _dot(kva, wuk_ref[...])
        krp = kr_ref[...]
        c, s1, s2 = c_ref[...], s1_ref[...], s2_ref[...]
        lane = lax.broadcasted_iota(jnp.int32, (tm, HEAD_PAD), 1)
        rope_lanes = jnp.logical_and(lane >= QK_NOPE, lane < QK_HEAD)
        dqg = jnp.zeros((1, HEAD_PAD), F32)
        dkg = jnp.zeros((1, HEAD_PAD), F32)
        dkr = jnp.zeros((tm, HEAD_PAD), F32)
        for h in range(N_HEADS):
            sl = slice(h * HEAD_PAD, (h + 1) * HEAD_PAD)
            qh, rqh = _rms(q[:, sl], QK_HEAD)
            dy = _rope_bwd(dq_ref[:, sl], c, s1, s2)
            dqg = dqg + _colsum(dy * qh)
            dqp_ref[:, sl] = _rms_bwd(dy, qh, rqh, qg_ref[...], QK_HEAD).astype(BF16)
            kh, rkh = _rms(kn[:, sl] + krp, QK_HEAD)
            dyk = _rope_bwd(dk_ref[:, sl], c, s1, s2)
            dkg = dkg + _colsum(dyk * kh)
            dkh = _rms_bwd(dyk, kh, rkh, kg_ref[...], QK_HEAD)
            dkv_ref[:, sl] = dkh.astype(BF16)
            dkr = dkr + jnp.where(rope_lanes, dkh, 0.0)
        dkv_ref[:, QP_COLS:] = dv_ref[...].astype(BF16)
        dp_ref[:, OFF_CKV + 2 * D_RNN:] = dkr.astype(BF16)
        dqa = _dot_nt(dqp_ref[...], wuq_ref[...])
        dp_ref[:, :Q_LORA] = _rms_bwd(dqa, xq, rq, gqa_ref[...], Q_LORA).astype(BF16)
        dkva = _dot_nt(dkv_ref[:, :QP_COLS], wuk_ref[...]) + _dot_nt(dkv_ref[:, QP_COLS:], wv_ref[...])
        dp_ref[:, Q_LORA:OFF_CKV] = _rms_bwd(dkva, xkv, rkv, gkva_ref[...], KV_LORA).astype(BF16)
        _acc(dqg_ref, first, dqg)
        _acc(dkg_ref, first, dkg)
        _acc(dgqa_ref, first, _colsum(dqa * xq))
        _acc(dgkva_ref, first, _colsum(dkva * xkv))

    def row(w):
        return pl.BlockSpec((tm, w), lambda i: (i, 0))

    def acc(w):
        return pl.BlockSpec((1, w), lambda i: (0, 0))

    return pl.pallas_call(
        body, name="qkv_bwd", grid=(n // tm,),
        in_specs=[row(Q_LORA), row(KV_LORA), row(LANES), row(QP_COLS), row(QP_COLS), row(D_ATTN), row(D_RNN), row(D_RNN),
                  _const_spec((1, Q_LORA)), _const_spec((1, KV_LORA)),
                  _const_spec((Q_LORA, QP_COLS)), _const_spec((KV_LORA, QP_COLS)), _const_spec((KV_LORA, D_ATTN)),
                  _const_spec((1, LANES)), _const_spec((1, LANES)), row(LANES), row(LANES), row(LANES)],
        out_specs=[row(P_COLS), row(Q_LORA), row(KV_LORA), row(QP_COLS),
                   row(QP_COLS + D_ATTN), acc(LANES), acc(LANES), acc(Q_LORA), acc(KV_LORA)],
        out_shape=[jax.ShapeDtypeStruct((n, P_COLS), BF16), jax.ShapeDtypeStruct((n, Q_LORA), BF16),
                   jax.ShapeDtypeStruct((n, KV_LORA), BF16), jax.ShapeDtypeStruct((n, QP_COLS), BF16),
                   jax.ShapeDtypeStruct((n, QP_COLS + D_ATTN), BF16),
                   jax.ShapeDtypeStruct((1, LANES), F32), jax.ShapeDtypeStruct((1, LANES), F32),
                   jax.ShapeDtypeStruct((1, Q_LORA), F32), jax.ShapeDtypeStruct((1, KV_LORA), F32)],
        compiler_params=pltpu.CompilerParams(dimension_semantics=("arbitrary",), vmem_limit_bytes=VMEM_LIMIT),
    )(cq, ckv, kr, dq_r, dk_r, dv, dxr, dxg, gqa, gkva, w_uq_p, w_uk_p, w_v, qg, kg, rc, rs1, rs2)


def _softmax_parts(qh, kh, tq, t):
    s = _dot_nt(qh, kh) * (QK_HEAD ** -0.5)
    key = lax.broadcasted_iota(jnp.int32, (tq, t), 1)
    s = jnp.where(key >= PAD_ROWS, s, -jnp.inf)
    e = jnp.exp(s - jnp.max(s, axis=-1, keepdims=True))
    return e, jnp.sum(e, axis=-1, keepdims=True)


def _attn_specs(t, tq):
    nq = t // tq
    qspec = pl.BlockSpec((tq, 2 * HEAD_PAD), lambda b, hp, i: (b * nq + i, hp))
    kspec = pl.BlockSpec((t, 2 * HEAD_PAD), lambda b, hp, i: (b, hp))
    vspec = pl.BlockSpec((t, 2 * V_HEAD), lambda b, hp, i: (b, hp))
    ospec = pl.BlockSpec((tq, 2 * V_HEAD), lambda b, hp, i: (b * nq + i, hp))
    return nq, qspec, kspec, vspec, ospec


def _attn_fwd(q, k, v, srcs=(), scatter=()):
    n = q.shape[0]
    t = _t_pad()
    tq = t // 8
    nq, qspec, kspec, vspec, ospec = _attn_specs(t, tq)
    nk = len(srcs)
    c_in, c_out, c_shape, c_sems = _exchange_specs(srcs, scatter)

    def body(q_ref, k_ref, v_ref, *rest):
        o_ref = rest[nk]
        start, wait = _exchange_fns(rest[:nk], rest[nk + 1:2 * nk + 1], rest[2 * nk + 1:], scatter)
        first_step, last_step = _grid_ends(3)
        pl.when(first_step)(start)
        lane = lax.broadcasted_iota(jnp.int32, (tq, 2 * V_HEAD), 1)
        outs = []
        for j in range(2):
            sl = slice(j * HEAD_PAD, (j + 1) * HEAD_PAD)
            e, l = _softmax_parts(q_ref[:, sl], k_ref[:, sl], tq, t)
            outs.append(_dot(e.astype(BF16), v_ref[...]) / l)
        o_ref[...] = jnp.where(lane < V_HEAD, outs[0], outs[1])
        pl.when(last_step)(wait)

    res = pl.pallas_call(
        body, name="attn_fwd", grid=(n // t, N_HEADS // 2, nq),
        in_specs=[qspec, kspec, vspec] + c_in, out_specs=[ospec] + c_out,
        out_shape=[jax.ShapeDtypeStruct((n, D_ATTN), F32)] + c_shape, scratch_shapes=c_sems,
        compiler_params=pltpu.CompilerParams(dimension_semantics=("arbitrary", "arbitrary", "arbitrary"),
                                             vmem_limit_bytes=VMEM_LIMIT),
    )(q, k, v, *srcs)
    return res[0], res[1:]


def _attn_bwd(q, k, v, do, srcs=(), scatter=()):
    n = q.shape[0]
    t = _t_pad()
    tq = t // 4
    nq, qspec, kspec, vspec, ospec = _attn_specs(t, tq)
    nk = len(srcs)
    c_in, c_out, c_shape, c_sems = _exchange_specs(srcs, scatter)

    def body(q_ref, k_ref, v_ref, do_ref, *rest):
        dq_ref, dk_ref, dv_ref = rest[nk:nk + 3]
        start, wait = _exchange_fns(rest[:nk], rest[nk + 3:2 * nk + 3], rest[2 * nk + 3:], scatter)
        first_step, last_step = _grid_ends(3)
        pl.when(first_step)(start)
        first = pl.program_id(2) == 0
        lane = lax.broadcasted_iota(jnp.int32, (tq, 2 * V_HEAD), 1)
        do = do_ref[...]
        dv = jnp.zeros((t, 2 * V_HEAD), F32)
        for j in range(2):
            sl = slice(j * HEAD_PAD, (j + 1) * HEAD_PAD)
            qh, kh = q_ref[:, sl], k_ref[:, sl]
            e, l = _softmax_parts(qh, kh, tq, t)
            p = e / l
            in_head = (lane < V_HEAD) if j == 0 else (lane >= V_HEAD)
            doh = jnp.where(in_head, do, 0.0).astype(BF16)
            dp = _dot_nt(doh, v_ref[...])
            delta = jnp.sum(p * dp, axis=-1, keepdims=True)
            ds = (p * (dp - delta) * (QK_HEAD ** -0.5)).astype(BF16)
            dq_ref[:, sl] = _dot(ds, kh)
            dkh = _dot_tn(ds, qh)

            @pl.when(first)
            def _():
                dk_ref[:, sl] = dkh

            @pl.when(jnp.logical_not(first))
            def _():
                dk_ref[:, sl] += dkh

            dv = dv + _dot_tn(p.astype(BF16), doh)
        _acc(dv_ref, first, dv)
        pl.when(last_step)(wait)

    res = pl.pallas_call(
        body, name="attn_bwd", grid=(n // t, N_HEADS // 2, nq),
        in_specs=[qspec, kspec, vspec, ospec] + c_in, out_specs=[qspec, kspec, vspec] + c_out,
        out_shape=[jax.ShapeDtypeStruct((n, QP_COLS), F32), jax.ShapeDtypeStruct((n, QP_COLS), F32),
                   jax.ShapeDtypeStruct((n, D_ATTN), F32)] + c_shape, scratch_shapes=c_sems,
        compiler_params=pltpu.CompilerParams(dimension_semantics=("arbitrary", "arbitrary", "arbitrary"),
                                             vmem_limit_bytes=VMEM_LIMIT),
    )(q, k, v, do, *srcs)
    return res[:3], res[3:]


SCAN_UNROLL = 4


def _scan(chains, t):
    groups = t // 8
    rows = lax.broadcasted_iota(jnp.int32, (8, LANES), 0)

    def one(g, carry, a_ref, b_ref, h_ref, reverse):
        off = pl.multiple_of((groups - 1 - g if reverse else g) * 8, 8)
        a = a_ref[pl.ds(off, 8), :]
        b = b_ref[pl.ds(off, 8), :]
        for d in (1, 2, 4):
            if reverse:
                keep = rows < 8 - d
                a_n, b_n = pltpu.roll(a, 8 - d, 0), pltpu.roll(b, 8 - d, 0)
            else:
                keep = rows >= d
                a_n, b_n = pltpu.roll(a, d, 0), pltpu.roll(b, d, 0)
            b = a * jnp.where(keep, b_n, 0.0) + b
            a = a * jnp.where(keep, a_n, 1.0)
        h = b + a * carry
        h_ref[pl.ds(off, 8), :] = h
        return h[0:1] if reverse else h[7:8]

    def step(g, carries):
        return tuple(one(g, c, *chain) for c, chain in zip(carries, chains))

    lax.fori_loop(0, groups, step, tuple(jnp.zeros((1, LANES), F32) for _ in chains), unroll=SCAN_UNROLL)


def _shift_rows(x, s, rows, t):
    if s == 0:
        return x
    rolled = pltpu.roll(x, s % t, 0)
    return jnp.where(rows >= s, rolled, 0.0) if s > 0 else jnp.where(rows < t + s, rolled, 0.0)


def _neg_expm1(x):
    series = -x * (1.0 + x * (0.5 + x * (1.0 / 6 + x * (1.0 / 24 + x * (1.0 / 120 + x * (1.0 / 720))))))
    return jnp.where(x > -0.3, series, 1.0 - jnp.exp(x))


def _gelu_parts(x):
    k = math.sqrt(2.0 / math.pi)
    th = jnp.tanh(k * (x + 0.044715 * x * x * x))
    g = 0.5 * x * (1.0 + th)
    dg = 0.5 * (1.0 + th) + 0.5 * x * (1.0 - th * th) * k * (1.0 + 3 * 0.044715 * x * x)
    return g, dg


def _lru_gates(xc, gates, lam_ref, valid, d):
    r = jax.nn.sigmoid(gates[:, (2 * d) * LANES:(2 * d + 1) * LANES])
    i = jax.nn.sigmoid(gates[:, (2 * d + 1) * LANES:(2 * d + 2) * LANES])
    neg_lam = -lam_ref[d:d + 1, :]
    sp = jnp.maximum(neg_lam, 0.0) + jnp.log1p(jnp.exp(-jnp.abs(neg_lam)))
    log_a = -LRU_C * r * sp
    a = jnp.exp(log_a)
    m = jnp.maximum(_neg_expm1(2.0 * log_a), 0.0)
    sq = jnp.sqrt(m)
    b = jnp.where(valid, sq * (i * xc), 0.0)
    return r, i, sp, a, m, sq, b


def _conv(xr, cw_ref, cb_ref, rows, t):
    return (cw_ref[0:1, :] * _shift_rows(xr, 2, rows, t) + cw_ref[1:2, :] * _shift_rows(xr, 1, rows, t)
            + cw_ref[2:3, :] * xr + cw_ref[3:4, :] * _shift_rows(xr, -1, rows, t) + cb_ref[...])


def _rnn_specs(t):
    seq = pl.BlockSpec((t, LANES), lambda cb, b: (b, cb))
    cw = pl.BlockSpec((4, LANES), lambda cb, b: (0, cb))
    vec1 = pl.BlockSpec((1, LANES), lambda cb, b: (0, cb))
    vec2 = pl.BlockSpec((2, LANES), lambda cb, b: (0, cb))
    wblk = pl.BlockSpec((1, LANES, 4 * LANES), lambda cb, b: (cb, 0, 0))
    gbias = pl.BlockSpec((1, 1, 4 * LANES), lambda cb, b: (cb, 0, 0))
    return seq, cw, vec1, vec2, wblk, gbias


def _rnn_fwd(xr, xg, conv_w, conv_b, wblk, gbias, lam):
    n = xr.shape[0]
    t = _t_pad()
    seq, cw, vec1, vec2, wspec, gspec = _rnn_specs(t)

    def body(xr_ref, xg_ref, cw_ref, cb_ref, w_ref, gb_ref, lam_ref, o_ref, a_s, b_s, h_s):
        rows = lax.broadcasted_iota(jnp.int32, (t, LANES), 0)
        valid = rows >= PAD_ROWS
        xc = _conv(xr_ref[...], cw_ref, cb_ref, rows, t)
        gates = _dot(xc.astype(BF16), w_ref[0]) + gb_ref[0]
        for d in range(2):
            _, _, _, a, _, _, b = _lru_gates(xc, gates, lam_ref, valid, d)
            a_s[d] = a
            b_s[d] = b
        _scan([(a_s.at[d], b_s.at[d], h_s.at[d], d == 1) for d in range(2)], t)
        g, _ = _gelu_parts(xg_ref[...])
        o_ref[...] = (h_s[0] + h_s[1]) * g

    return pl.pallas_call(
        body, name="rnn_fwd", grid=(D_RNN // LANES, n // t),
        in_specs=[seq, seq, cw, vec1, wspec, gspec, vec2], out_specs=seq,
        out_shape=jax.ShapeDtypeStruct((n, D_RNN), F32),
        scratch_shapes=[pltpu.VMEM((2, t, LANES), F32), pltpu.VMEM((2, t, LANES), F32), pltpu.VMEM((2, t, LANES), F32)],
        compiler_params=pltpu.CompilerParams(dimension_semantics=("parallel", "parallel"), vmem_limit_bytes=VMEM_LIMIT),
    )(xr, xg, conv_w, conv_b, wblk, gbias, lam)


def _rnn_bwd(xr, xg, do, conv_w, conv_b, wblk, gbias, lam):
    n = xr.shape[0]
    t = _t_pad()
    seq, cw, vec1, vec2, wspec, gspec = _rnn_specs(t)

    def body(xr_ref, xg_ref, do_ref, cw_ref, cb_ref, w_ref, gb_ref, lam_ref,
             dxr_ref, dxg_ref, dcw_ref, dcb_ref, dw_ref, dgb_ref, dlam_ref, a_s, b_s, h_s, l_s, dg_s):
        first = pl.program_id(1) == 0
        rows = lax.broadcasted_iota(jnp.int32, (t, LANES), 0)
        valid = rows >= PAD_ROWS
        xr = xr_ref[...]
        xc = _conv(xr, cw_ref, cb_ref, rows, t)
        xcb = xc.astype(BF16)
        gates = _dot(xcb, w_ref[0]) + gb_ref[0]
        for d in range(2):
            _, _, _, a, _, _, b = _lru_gates(xc, gates, lam_ref, valid, d)
            a_s[d] = a
            b_s[d] = b
        _scan([(a_s.at[d], b_s.at[d], h_s.at[d], d == 1) for d in range(2)], t)
        g, dg = _gelu_parts(xg_ref[...])
        do = do_ref[...]
        dxg_ref[...] = do * (h_s[0] + h_s[1]) * dg
        b_s[0] = do * g
        for d in range(2):
            a_s[d] = _shift_rows(a_s[d], -1 if d == 0 else 1, rows, t)
        _scan([(a_s.at[d], b_s.at[0], l_s.at[d], d == 0) for d in range(2)], t)
        dxc = jnp.zeros((t, LANES), F32)
        dlams = []
        for d in range(2):
            r, i, sp, a, m, sq, _ = _lru_gates(xc, gates, lam_ref, valid, d)
            lam_t = l_s[d]
            da = lam_t * _shift_rows(h_s[d], 1 if d == 0 else -1, rows, t)
            lam_v = jnp.where(valid, lam_t, 0.0)
            dsq = lam_v * (i * xc)
            di = lam_v * sq * xc
            dxc = dxc + lam_v * sq * i
            dm = jnp.where(m > 0.0, dsq * 0.5 / jnp.where(m > 0.0, sq, 1.0), 0.0)
            dla = da * a - 2.0 * dm * a * a
            dr = dla * (-LRU_C) * sp
            dsp = _colsum(dla * (-LRU_C) * r)
            dlams.append(dsp * -jax.nn.sigmoid(-lam_ref[d:d + 1, :]))
            dg_s[:, (2 * d) * LANES:(2 * d + 1) * LANES] = (dr * r * (1.0 - r)).astype(BF16)
            dg_s[:, (2 * d + 1) * LANES:(2 * d + 2) * LANES] = (di * i * (1.0 - i)).astype(BF16)
        dgates = dg_s[...]
        dxc = dxc + _dot_nt(dgates, w_ref[0])
        dxr_ref[...] = (cw_ref[0:1, :] * _shift_rows(dxc, -2, rows, t) + cw_ref[1:2, :] * _shift_rows(dxc, -1, rows, t)
                        + cw_ref[2:3, :] * dxc + cw_ref[3:4, :] * _shift_rows(dxc, 1, rows, t))
        dcw = jnp.concatenate([_colsum(dxc * _shift_rows(xr, 2 - j, rows, t)) for j in range(4)], axis=0)
        _acc(dcw_ref, first, dcw)
        _acc(dcb_ref, first, _colsum(dxc))
        _acc(dw_ref, first, _dot_tn(xcb, dgates)[None])
        _acc(dgb_ref, first, _colsum(dgates.astype(F32))[None])
        _acc(dlam_ref, first, jnp.concatenate(dlams, axis=0))

    return pl.pallas_call(
        body, name="rnn_bwd", grid=(D_RNN // LANES, n // t),
        in_specs=[seq, seq, seq, cw, vec1, wspec, gspec, vec2],
        out_specs=[seq, seq, cw, vec1, wspec, gspec, vec2],
        out_shape=[jax.ShapeDtypeStruct((n, D_RNN), F32), jax.ShapeDtypeStruct((n, D_RNN), F32),
                   jax.ShapeDtypeStruct((4, D_RNN), F32), jax.ShapeDtypeStruct((1, D_RNN), F32),
                   jax.ShapeDtypeStruct((D_RNN // LANES, LANES, 4 * LANES), F32),
                   jax.ShapeDtypeStruct((D_RNN // LANES, 1, 4 * LANES), F32), jax.ShapeDtypeStruct((2, D_RNN), F32)],
        scratch_shapes=[pltpu.VMEM((2, t, LANES), F32), pltpu.VMEM((2, t, LANES), F32), pltpu.VMEM((2, t, LANES), F32),
                        pltpu.VMEM((2, t, LANES), F32), pltpu.VMEM((t, 4 * LANES), BF16)],
        compiler_params=pltpu.CompilerParams(dimension_semantics=("parallel", "arbitrary"), vmem_limit_bytes=VMEM_LIMIT),
    )(xr, xg, do, conv_w, conv_b, wblk, gbias, lam)


def _post(oa, orn, h0, tgt, ga, gr, g2, w_out, w_gate, w_up, w_down):
    n = oa.shape[0]
    tm = _row_tile(n)
    t = _t_pad()

    def body(oa_ref, or_ref, h0_ref, tgt_ref, ga_ref, gr_ref, g2_ref, wo_ref, wg_ref, wu_ref, wd_ref,
             doa_ref, dor_ref, dh1_ref, mix_ref, h1n_ref, act_ref, dgate_ref, dup_ref, dy_ref,
             loss_ref, dga_ref, dgr_ref, dg2_ref, gate_s, up_s):
        first = pl.program_id(0) == 0
        xa, ra = _rms(oa_ref[...], D_ATTN)
        xr, rr = _rms(or_ref[...], D_RNN)
        mix_ref[:, :D_ATTN] = (xa * ga_ref[...]).astype(BF16)
        mix_ref[:, D_ATTN:] = (xr * gr_ref[...]).astype(BF16)
        h1 = h0_ref[...] + _dot(mix_ref[...], wo_ref[...])
        x2, r2 = _rms(h1, D_MODEL)
        h1n = (x2 * g2_ref[...]).astype(BF16)
        h1n_ref[...] = h1n
        y = h1
        for cs in range(0, D_FF, FF_CHUNK):
            sl = slice(cs, cs + FF_CHUNK)
            gate = _dot(h1n, wg_ref[:, sl])
            up = _dot(h1n, wu_ref[:, sl])
            gate_s[:, sl] = gate
            up_s[:, sl] = up
            act = (gate * jax.nn.sigmoid(gate) * up).astype(BF16)
            act_ref[:, sl] = act
            y = y + _dot(act, wd_ref[sl, :])
        row = pl.program_id(0) * tm + lax.broadcasted_iota(jnp.int32, (tm, 1), 0)
        for _ in range(1, n // t):
            row = jnp.where(row >= t, row - t, row)
        err = jnp.where(row >= PAD_ROWS + N_META, y - tgt_ref[...], 0.0)
        _acc(loss_ref, first, jnp.full((1, LANES), 0.5 / D_MODEL, F32) * jnp.sum(err * err))
        dy = err * (1.0 / D_MODEL)
        dyb = dy.astype(BF16)
        dy_ref[...] = dyb
        dh1n = jnp.zeros((tm, D_MODEL), F32)
        for cs in range(0, D_FF, FF_CHUNK):
            sl = slice(cs, cs + FF_CHUNK)
            dact = _dot_nt(dyb, wd_ref[sl, :])
            gate, up = gate_s[:, sl], up_s[:, sl]
            sg = jax.nn.sigmoid(gate)
            dgate = (dact * up * sg * (1.0 + gate * (1.0 - sg))).astype(BF16)
            dup = (dact * gate * sg).astype(BF16)
            dgate_ref[:, sl] = dgate
            dup_ref[:, sl] = dup
            dh1n = dh1n + _dot_nt(dgate, wg_ref[:, sl]) + _dot_nt(dup, wu_ref[:, sl])
        _acc(dg2_ref, first, _colsum(dh1n * x2))
        dh1 = dy + _rms_bwd(dh1n, x2, r2, g2_ref[...], D_MODEL)
        dh1_ref[...] = dh1
        dmix = _dot_nt(dh1.astype(BF16), wo_ref[...])
        dma, dmr = dmix[:, :D_ATTN], dmix[:, D_ATTN:]
        _acc(dga_ref, first, _colsum(dma * xa))
        _acc(dgr_ref, first, _colsum(dmr * xr))
        doa_ref[...] = _rms_bwd(dma, xa, ra, ga_ref[...], D_ATTN)
        dor_ref[...] = _rms_bwd(dmr, xr, rr, gr_ref[...], D_RNN)

    def row(w):
        return pl.BlockSpec((tm, w), lambda i: (i, 0))

    def acc(w):
        return pl.BlockSpec((1, w), lambda i: (0, 0))

    outs = [(D_ATTN, F32), (D_RNN, F32), (D_MODEL, F32), (D_MODEL, BF16), (D_MODEL, BF16), (D_FF, BF16),
            (D_FF, BF16), (D_FF, BF16), (D_MODEL, BF16)]
    accs = [LANES, D_ATTN, D_RNN, D_MODEL]
    return pl.pallas_call(
        body, name="post", grid=(n // tm,),
        in_specs=[row(D_ATTN), row(D_RNN), row(D_MODEL), row(D_MODEL),
                  _const_spec((1, D_ATTN)), _const_spec((1, D_RNN)), _const_spec((1, D_MODEL)),
                  _const_spec((D_MODEL, D_MODEL)), _const_spec((D_MODEL, D_FF)), _const_spec((D_MODEL, D_FF)),
                  _const_spec((D_FF, D_MODEL))],
        out_specs=[row(w) for w, _ in outs] + [acc(w) for w in accs],
        out_shape=[jax.ShapeDtypeStruct((n, w), dt) for w, dt in outs]
        + [jax.ShapeDtypeStruct((1, w), F32) for w in accs],
        scratch_shapes=[pltpu.VMEM((tm, D_FF), F32), pltpu.VMEM((tm, D_FF), F32)],
        compiler_params=pltpu.CompilerParams(dimension_semantics=("arbitrary",), vmem_limit_bytes=VMEM_LIMIT),
    )(oa, orn, h0, tgt, ga, gr, g2, w_out, w_gate, w_up, w_down)


def _in_bwd(dp, h0, dh1, ln1_g, w_in_p, srcs=(), scatter=()):
    n = h0.shape[0]
    tm = _row_tile(n)
    nk = len(srcs)
    c_in, c_out, c_shape, c_sems = _exchange_specs(srcs, scatter)

    def body(dp_ref, h0_ref, dh1_ref, g_ref, w_ref, *rest):
        dh0_ref, dg_ref = rest[nk:nk + 2]
        start, wait = _exchange_fns(rest[:nk], rest[nk + 2:2 * nk + 2], rest[2 * nk + 2:], scatter)
        first_step, last_step = _grid_ends(1)
        pl.when(first_step)(start)
        dhn = _dot_nt(dp_ref[...], w_ref[...])
        xhat, r = _rms(h0_ref[...], D_MODEL)
        _acc(dg_ref, first_step, _colsum(dhn * xhat))
        dh0_ref[...] = dh1_ref[...] + _rms_bwd(dhn, xhat, r, g_ref[...], D_MODEL)
        pl.when(last_step)(wait)

    def row(w):
        return pl.BlockSpec((tm, w), lambda i: (i, 0))

    res = pl.pallas_call(
        body, name="in_bwd", grid=(n // tm,),
        in_specs=[row(P_COLS), row(D_MODEL), row(D_MODEL), _const_spec((1, D_MODEL)), _const_spec((D_MODEL, P_COLS))] + c_in,
        out_specs=[row(D_MODEL), pl.BlockSpec((1, D_MODEL), lambda i: (0, 0))] + c_out,
        out_shape=[jax.ShapeDtypeStruct((n, D_MODEL), F32), jax.ShapeDtypeStruct((1, D_MODEL), F32)] + c_shape,
        scratch_shapes=c_sems,
        compiler_params=pltpu.CompilerParams(dimension_semantics=("arbitrary",), vmem_limit_bytes=VMEM_LIMIT),
    )(dp, h0, dh1, ln1_g, w_in_p, *srcs)
    return res[:2], res[2:]


def _pick_tile(width, cap):
    best = LANES
    for mult in range(1, width // LANES + 1):
        cand = mult * LANES
        if width % cand == 0 and cand <= cap:
            best = cand
    return best


def _matmul_tn(name, a, b):
    n, ka = a.shape
    kb = b.shape[1]
    ta, tb = _pick_tile(ka, 1408), _pick_tile(kb, 1408)
    tk = n // 4

    def body(a_ref, b_ref, o_ref):
        _acc(o_ref, pl.program_id(2) == 0, _dot_tn(a_ref[...].astype(BF16), b_ref[...].astype(BF16)))

    return pl.pallas_call(
        body, name=name, grid=(ka // ta, kb // tb, n // tk),
        in_specs=[pl.BlockSpec((tk, ta), lambda i, j, k: (k, i)), pl.BlockSpec((tk, tb), lambda i, j, k: (k, j))],
        out_specs=pl.BlockSpec((ta, tb), lambda i, j, k: (i, j)),
        out_shape=jax.ShapeDtypeStruct((ka, kb), F32),
        compiler_params=pltpu.CompilerParams(dimension_semantics=("parallel", "parallel", "arbitrary"),
                                             vmem_limit_bytes=VMEM_LIMIT),
    )(a, b)


def _matmul_tn_shards(name, a, b, row_sharded):
    n, ka = a.shape
    kb = b.shape[1]
    ta, tb = _pick_tile(ka, 1408), _pick_tile(kb, 1408)
    tk = n // 4
    if row_sharded:
        width = ka // N_DEV
        per = ta // width
        out_shape, out_block = (N_DEV, width, kb), (per, width, tb)
        out_map = lambda i, j, k: (i, 0, j)
    else:
        width = kb // N_DEV
        per = tb // width
        out_shape, out_block = (N_DEV, ka, width), (per, ta, width)
        out_map = lambda i, j, k: (j, i, 0)

    def body(a_ref, b_ref, o_ref, acc_ref):
        _acc(acc_ref, pl.program_id(2) == 0, _dot_tn(a_ref[...].astype(BF16), b_ref[...].astype(BF16)))

        @pl.when(pl.program_id(2) == pl.num_programs(2) - 1)
        def _():
            for s in range(per):
                sl = slice(s * width, (s + 1) * width)
                o_ref[s] = (acc_ref[sl, :] if row_sharded else acc_ref[:, sl]).astype(BF16)

    return pl.pallas_call(
        body, name=name, grid=(ka // ta, kb // tb, n // tk),
        in_specs=[pl.BlockSpec((tk, ta), lambda i, j, k: (k, i)), pl.BlockSpec((tk, tb), lambda i, j, k: (k, j))],
        out_specs=pl.BlockSpec(out_block, out_map), out_shape=jax.ShapeDtypeStruct(out_shape, BF16),
        scratch_shapes=[pltpu.VMEM((ta, tb), F32)],
        compiler_params=pltpu.CompilerParams(dimension_semantics=("parallel", "parallel", "arbitrary"),
                                             vmem_limit_bytes=VMEM_LIMIT),
    )(a, b)


def _adamw_math(g8_ref, w_ref, m_ref, v_ref, g_ref, d_ref, nm_ref, nv_ref):
    g = g8_ref[0].astype(F32)
    for s in range(1, N_DEV):
        g = g + g8_ref[s].astype(F32)
    g_ref[...] = g
    nm = ADAM_B1 * m_ref[...] + (1.0 - ADAM_B1) * g
    nv = ADAM_B2 * v_ref[...] + (1.0 - ADAM_B2) * (g * g)
    nm_ref[...] = nm
    nv_ref[...] = nv
    m_hat = nm / (1.0 - ADAM_B1 ** ADAM_STEP)
    v_hat = nv / (1.0 - ADAM_B2 ** ADAM_STEP)
    d_ref[...] = -ADAM_LR * (m_hat / (jnp.sqrt(v_hat) + ADAM_EPS) + ADAM_WD * w_ref[...])


def _adamw_many(name, items):
    count = len(items)

    def body(*refs):
        ins, outs = refs[:4 * count], refs[4 * count:]
        for i in range(count):
            _adamw_math(*ins[4 * i:4 * i + 4], *outs[4 * i:4 * i + 4])

    flat = [a for item in items for a in item]
    res = pl.pallas_call(
        body, name=name,
        out_shape=[jax.ShapeDtypeStruct(item[1].shape, F32) for item in items for _ in range(4)],
        compiler_params=pltpu.CompilerParams(vmem_limit_bytes=VMEM_LIMIT),
    )(*flat)
    return [tuple(res[4 * i:4 * i + 4]) for i in range(count)]


def _adamw(name, g8, w, m, v):
    rows, cols = w.shape
    tr = rows
    for cand in (256, 176, 128, 64):
        if rows % cand == 0 and rows > cand:
            tr = cand
            break

    def body(*refs):
        _adamw_math(*refs)

    blk = pl.BlockSpec((tr, cols), lambda i: (i, 0))
    return pl.pallas_call(
        body, name=name, grid=(rows // tr,),
        in_specs=[pl.BlockSpec((N_DEV, tr, cols), lambda i: (0, i, 0)), blk, blk, blk],
        out_specs=[blk] * 4, out_shape=[jax.ShapeDtypeStruct((rows, cols), F32)] * 4,
        compiler_params=pltpu.CompilerParams(dimension_semantics=("parallel",), vmem_limit_bytes=VMEM_LIMIT),
    )(g8, w, m, v)


def _exchange_specs(srcs, scatter):
    nk = len(srcs)
    if not nk:
        return [], [], [], []
    any_spec = pl.BlockSpec(memory_space=pl.ANY)
    out_shape = [jax.ShapeDtypeStruct(s.shape if sc else (N_DEV,) + s.shape, s.dtype) for s, sc in zip(srcs, scatter)]
    sems = [pltpu.SemaphoreType.DMA((nk, N_DEV - 1)), pltpu.SemaphoreType.DMA((nk, N_DEV - 1)),
            pltpu.SemaphoreType.DMA((nk,))]
    return [any_spec] * nk, [any_spec] * nk, out_shape, sems


def _exchange_fns(src_refs, out_refs, sems, scatter):
    nk = len(src_refs)
    if not nk:
        return (lambda: None), (lambda: None)
    send_sems, recv_sems, local_sems = sems

    def plan():
        x, y, c = lax.axis_index("x"), lax.axis_index("y"), lax.axis_index("c")
        me = 4 * x + 2 * y + c
        flips = [(f >> 2 & 1, f >> 1 & 1, f & 1) for f in range(1, N_DEV)]
        peers = [(1 - x if fx else x, 1 - y if fy else y, 1 - c if fc else c) for fx, fy, fc in flips]
        pids = [4 * px + 2 * py + pc for px, py, pc in peers]

        def src_for(k, dest):
            return src_refs[k].at[dest] if scatter[k] else src_refs[k]

        def remote(k, j, dst_slot, dest):
            return pltpu.make_async_remote_copy(
                src_ref=src_for(k, dest), dst_ref=out_refs[k].at[dst_slot],
                send_sem=send_sems.at[k, j], recv_sem=recv_sems.at[k, j],
                device_id=peers[j], device_id_type=MESH)

        local = [pltpu.make_async_copy(src_for(k, me), out_refs[k].at[me], local_sems.at[k]) for k in range(nk)]
        sends = [remote(k, j, me, pids[j]) for k in range(nk) for j in range(N_DEV - 1)]
        recvs = [remote(k, j, pids[j], pids[j]) for k in range(nk) for j in range(N_DEV - 1)]
        return local, sends, recvs

    def start():
        local, sends, _ = plan()
        for cp in local + sends:
            cp.start()

    def wait():
        local, sends, recvs = plan()
        for cp in recvs:
            cp.wait_recv()
        for cp in sends:
            cp.wait_send()
        for cp in local:
            cp.wait()

    return start, wait


def _exchange(name, srcs, scatter):
    nk = len(srcs)
    c_in, c_out, c_shape, c_sems = _exchange_specs(srcs, scatter)

    def body(*refs):
        start, wait = _exchange_fns(refs[:nk], refs[nk:2 * nk], refs[2 * nk:], scatter)
        start()
        wait()

    return pl.pallas_call(body, name=name, in_specs=c_in, out_specs=c_out, out_shape=c_shape, scratch_shapes=c_sems)(*srcs)


def _cols_from_shards(g):
    return jnp.transpose(g, (1, 0, 2)).reshape(g.shape[1], -1)


def _cols_to_shards(w):
    return jnp.transpose(w.reshape(w.shape[0], N_DEV, -1), (1, 0, 2))


def _rope_tables(n):
    t = _t_pad()
    pos = (jnp.arange(t, dtype=F32) - PAD_ROWS)
    half = QK_ROPE // 2
    freqs = 1.0 / (ROPE_THETA ** (jnp.arange(half, dtype=F32) / half))
    ang = pos[:, None] * freqs[None, :]
    cos, sin = jnp.cos(ang), jnp.sin(ang)
    z = lambda w: jnp.zeros((t, w), F32)
    c = jnp.concatenate([jnp.ones((t, QK_NOPE), F32), cos, cos, z(HEAD_PAD - QK_HEAD)], axis=1)
    s1 = jnp.concatenate([z(QK_NOPE + half), sin, z(HEAD_PAD - QK_HEAD)], axis=1)
    s2 = jnp.concatenate([z(QK_NOPE), -sin, z(HEAD_PAD - QK_NOPE - half)], axis=1)
    reps = n // t
    return tuple(jnp.tile(a, (reps, 1)) for a in (c, s1, s2))


def _block_diag_gates(lru_wa, lru_wi):
    eye = jnp.eye(2, dtype=lru_wa.dtype)

    def bd(w):
        w = w.reshape(2, D_RNN // LANES, 2, RNN_BW, RNN_BW)
        full = w[:, :, :, :, None, :] * eye[None, None, :, None, :, None]
        return full.reshape(2, D_RNN // LANES, LANES, LANES)

    a, i = bd(lru_wa), bd(lru_wi)
    return jnp.concatenate([a[0], i[0], a[1], i[1]], axis=-1)


def _unblock_gates(dw):
    nb = D_RNN // LANES
    parts = dw.reshape(nb, 2, RNN_BW, 4, 2, RNN_BW)
    diag = jnp.stack([parts[:, k, :, :, k, :] for k in range(2)], axis=1)
    diag = jnp.transpose(diag, (3, 0, 1, 2, 4)).reshape(4, 2 * nb, RNN_BW, RNN_BW)
    return jnp.stack([diag[0], diag[2]]), jnp.stack([diag[1], diag[3]])


WEIGHTS = ("meta_tokens", "ln1_g", "w_in", "q_a_norm_g", "w_uq", "kv_a_norm_g", "w_ukv", "q_norm_g", "k_norm_g",
           "conv_w", "conv_b", "lru_wa", "lru_ba", "lru_wi", "lru_bi", "lru_lambda", "attn_out_g", "rnn_out_g",
           "w_out", "ln2_g", "w_gate", "w_up", "w_down")
BIG = ("w_in", "w_uq", "w_ukv", "w_out", "w_gate", "w_up", "w_down")
ROW_SHARDED = ("w_out", "w_down")
REPLICATED = ("ln1_g", "q_a_norm_g", "kv_a_norm_g", "q_norm_g", "k_norm_g", "conv_b", "lru_wa", "lru_wi",
              "attn_out_g", "rnn_out_g", "ln2_g")
G_FIRST = ("w_in", "meta_tokens")
G_MID = ("w_uq", "w_ukv", "conv_w", "lru_ba", "lru_bi", "lru_lambda")
LATE = ("w_out", "w_gate", "w_up", "w_down")
G_LAST = ("meta_tokens", "ln1_g")


def _local_step(x, tgt, ex):
    nb = x.shape[0]
    t = _t_pad()
    n = nb * t
    local = ex.local
    first = ex.gathered(G_FIRST, ex.run("gather_first", *ex.gather_srcs(G_FIRST)))
    meta, w_in = first["meta_tokens"], first["w_in"]
    lead = jnp.zeros((nb, PAD_ROWS, D_MODEL), F32)
    h0 = jnp.concatenate([lead, jnp.broadcast_to(meta[None], (nb, N_META, D_MODEL)), x], axis=1).reshape(n, D_MODEL)
    tgt_p = jnp.concatenate([jnp.zeros((nb, PAD_ROWS + N_META, D_MODEL), F32), tgt], axis=1).reshape(n, D_MODEL)

    zc = lambda c: jnp.zeros((D_MODEL, c), w_in.dtype)
    w_in_p = jnp.concatenate([w_in[:, :OFF_CKV], w_in[:, OFF_KR:], zc(QK_NOPE), w_in[:, OFF_CKV:OFF_KR],
                              zc(HEAD_PAD - QK_HEAD)], axis=1)
    pad_g = lambda g: jnp.pad(g, ((0, 0), (0, HEAD_PAD - QK_HEAD)))
    qg, kg = pad_g(local["q_norm_g"]), pad_g(local["k_norm_g"])
    rc, rs1, rs2 = _rope_tables(n)
    wblk = _block_diag_gates(local["lru_wa"].reshape(2, -1, RNN_BW, RNN_BW),
                             local["lru_wi"].reshape(2, -1, RNN_BW, RNN_BW)).astype(BF16)
    nblk = D_RNN // LANES

    (hn, cq, ckv, xr, xg, kr), got = _in_proj(h0, local["ln1_g"], w_in_p, *ex.gather_srcs(G_MID))
    w = ex.gathered(G_MID, got)
    w_uq_p = jnp.pad(w["w_uq"].reshape(Q_LORA, N_HEADS, QK_HEAD), ((0, 0), (0, 0), (0, HEAD_PAD - QK_HEAD))
                     ).reshape(Q_LORA, QP_COLS)
    ukv = w["w_ukv"].reshape(KV_LORA, N_HEADS, QK_NOPE + V_HEAD)
    w_uk_p = jnp.pad(ukv[:, :, :QK_NOPE], ((0, 0), (0, 0), (0, HEAD_PAD - QK_NOPE))).reshape(KV_LORA, QP_COLS)
    w_v = ukv[:, :, QK_NOPE:].reshape(KV_LORA, D_ATTN)
    gbias = jnp.stack([w["lru_ba"][0], w["lru_bi"][0], w["lru_ba"][1], w["lru_bi"][1]], axis=0)
    gbias = jnp.transpose(gbias.reshape(4, nblk, LANES), (1, 0, 2)).reshape(nblk, 1, 4 * LANES)

    q, k, v = _qkv_fwd(cq, ckv, kr, local["q_a_norm_g"], local["kv_a_norm_g"], w_uq_p, w_uk_p, w_v, qg, kg, rc, rs1, rs2)
    oa, got = _attn_fwd(q, k, v, *ex.gather_srcs(LATE))
    late = ex.gathered(LATE, got)
    orn = _rnn_fwd(xr, xg, w["conv_w"], local["conv_b"], wblk, gbias, w["lru_lambda"])
    (doa, dor, dh1, mix, h1n, act, dgate, dup, dyb, loss, dga, dgr, dg2) = _post(
        oa, orn, h0, tgt_p, local["attn_out_g"], local["rnn_out_g"], local["ln2_g"], late["w_out"], late["w_gate"],
        late["w_up"], late["w_down"])
    wire = {"w_out": _matmul_tn_shards("dw_out", mix, dh1, True), "w_gate": _matmul_tn_shards("dw_gate", h1n, dgate, False),
            "w_up": _matmul_tn_shards("dw_up", h1n, dup, False), "w_down": _matmul_tn_shards("dw_down", act, dyb, True)}
    (dq_r, dk_r, dv), got = _attn_bwd(q, k, v, doa, *ex.scatter_srcs(LATE, wire))
    summed = ex.scattered(LATE, wire, got)
    dxr, dxg, dcw, dcb, dwblk, dgb, dlam = _rnn_bwd(xr, xg, dor, w["conv_w"], local["conv_b"], wblk, gbias, w["lru_lambda"])
    (dp, qa, kva, dqp, dkv, dqg, dkg, dgqa, dgkva) = _qkv_bwd(
        cq, ckv, kr, dq_r, dk_r, dv, dxr, dxg, local["q_a_norm_g"], local["kv_a_norm_g"], w_uq_p, w_uk_p, w_v, qg, kg,
        rc, rs1, rs2)
    dw_in_p = _matmul_tn("dw_in", hn, dp)
    dw_uq_p = _matmul_tn("dw_uq", qa, dqp)
    dw_kv = _matmul_tn("dw_ukv", kva, dkv)
    kr0 = OFF_CKV + 2 * D_RNN + QK_NOPE
    dw_in = jnp.concatenate([dw_in_p[:, :OFF_CKV], dw_in_p[:, kr0:kr0 + QK_ROPE], dw_in_p[:, OFF_CKV:OFF_CKV + 2 * D_RNN]],
                            axis=1)
    dw_uq = dw_uq_p.reshape(Q_LORA, N_HEADS, HEAD_PAD)[:, :, :QK_HEAD].reshape(Q_LORA, N_HEADS * QK_HEAD)
    dw_ukv = jnp.concatenate([dw_kv[:, :QP_COLS].reshape(KV_LORA, N_HEADS, HEAD_PAD)[:, :, :QK_NOPE],
                              dw_kv[:, QP_COLS:].reshape(KV_LORA, N_HEADS, V_HEAD)], axis=2).reshape(KV_LORA, -1)
    dwa, dwi = _unblock_gates(dwblk)
    dgb = jnp.transpose(dgb.reshape(nblk, 4, LANES), (1, 0, 2)).reshape(4, D_RNN)
    grads = {
        "w_in": dw_in, "q_a_norm_g": dgqa, "w_uq": dw_uq, "kv_a_norm_g": dgkva, "w_ukv": dw_ukv,
        "q_norm_g": dqg[:, :QK_HEAD], "k_norm_g": dkg[:, :QK_HEAD], "conv_w": dcw, "conv_b": dcb,
        "lru_wa": dwa.reshape(-1, RNN_BW), "lru_ba": jnp.stack([dgb[0], dgb[2]]), "lru_wi": dwi.reshape(-1, RNN_BW),
        "lru_bi": jnp.stack([dgb[1], dgb[3]]), "lru_lambda": dlam, "attn_out_g": dga, "rnn_out_g": dgr, "ln2_g": dg2,
    }
    names = tuple(grads)
    wire = ex.to_wire(grads)
    (dh0, dg1), got = _in_bwd(dp, h0, dh1, local["ln1_g"], w_in_p, *ex.scatter_srcs(names, wire))
    summed.update(ex.scattered(names, wire, got))

    dh0 = dh0.reshape(nb, t, D_MODEL)
    wire = ex.to_wire({"meta_tokens": jnp.sum(dh0[:, PAD_ROWS:PAD_ROWS + N_META], axis=0), "ln1_g": dg1})
    got = ex.run("reduce_last", *ex.scatter_srcs(G_LAST, wire))
    summed.update(ex.scattered(G_LAST, wire, got))
    return loss[0, 0], dh0[:, PAD_ROWS + N_META:], summed


class _MeshExchange:
    def __init__(self, shards):
        self.local = shards

    @staticmethod
    def run(name, srcs, scatter):
        return _exchange(name, srcs, scatter)

    def gather_srcs(self, names):
        return [self.local[k].astype(BF16) if k in BIG else self.local[k] for k in names], [False] * len(names)

    @staticmethod
    def gathered(names, outs):
        return {k: g.reshape(-1, g.shape[-1]) if k in ROW_SHARDED else _cols_from_shards(g) for k, g in zip(names, outs)}

    @staticmethod
    def to_wire(grads):
        wire = {}
        for k, g in grads.items():
            if k in REPLICATED:
                wire[k] = g
            elif k in ROW_SHARDED:
                wire[k] = g.reshape(N_DEV, -1, g.shape[-1]).astype(BF16)
            else:
                wire[k] = _cols_to_shards(g).astype(BF16) if k in BIG else _cols_to_shards(g)
        return wire

    @staticmethod
    def scatter_srcs(names, wire):
        return [wire[k] for k in names], [k not in REPLICATED for k in names]

    @staticmethod
    def scattered(names, wire, outs):
        return dict(zip(names, outs))


def kernel(x, meta_tokens, ln1_g, w_in, q_a_norm_g, w_uq, kv_a_norm_g, w_ukv, q_norm_g, k_norm_g, conv_w, conv_b, lru_wa, lru_ba, lru_wi, lru_bi, lru_lambda, attn_out_g, rnn_out_g, w_out, ln2_g, w_gate, w_up, w_down, loss_target, m_meta_tokens, m_ln1_g, m_w_in, m_q_a_norm_g, m_w_uq, m_kv_a_norm_g, m_w_ukv, m_q_norm_g, m_k_norm_g, m_conv_w, m_conv_b, m_lru_wa, m_lru_ba, m_lru_wi, m_lru_bi, m_lru_lambda, m_attn_out_g, m_rnn_out_g, m_w_out, m_ln2_g, m_w_gate, m_w_up, m_w_down, v_meta_tokens, v_ln1_g, v_w_in, v_q_a_norm_g, v_w_uq, v_kv_a_norm_g, v_w_ukv, v_q_norm_g, v_k_norm_g, v_conv_w, v_conv_b, v_lru_wa, v_lru_ba, v_lru_wi, v_lru_bi, v_lru_lambda, v_attn_out_g, v_rnn_out_g, v_w_out, v_ln2_g, v_w_gate, v_w_up, v_w_down):
    given = (meta_tokens, ln1_g, w_in, q_a_norm_g, w_uq, kv_a_norm_g, w_ukv, q_norm_g, k_norm_g, conv_w, conv_b,
             lru_wa, lru_ba, lru_wi, lru_bi, lru_lambda, attn_out_g, rnn_out_g, w_out, ln2_g, w_gate, w_up, w_down)
    moments_m = (m_meta_tokens, m_ln1_g, m_w_in, m_q_a_norm_g, m_w_uq, m_kv_a_norm_g, m_w_ukv, m_q_norm_g, m_k_norm_g,
                 m_conv_w, m_conv_b, m_lru_wa, m_lru_ba, m_lru_wi, m_lru_bi, m_lru_lambda, m_attn_out_g, m_rnn_out_g,
                 m_w_out, m_ln2_g, m_w_gate, m_w_up, m_w_down)
    moments_v = (v_meta_tokens, v_ln1_g, v_w_in, v_q_a_norm_g, v_w_uq, v_kv_a_norm_g, v_w_ukv, v_q_norm_g, v_k_norm_g,
                 v_conv_w, v_conv_b, v_lru_wa, v_lru_ba, v_lru_wi, v_lru_bi, v_lru_lambda, v_attn_out_g, v_rnn_out_g,
                 v_w_out, v_ln2_g, v_w_gate, v_w_up, v_w_down)
    shapes = {k: a.shape for k, a in zip(WEIGHTS, given)}

    def two_d(a):
        return a.reshape(-1, a.shape[-1])

    w = {k: two_d(a) for k, a in zip(WEIGHTS, given)}
    m = {k: two_d(a) for k, a in zip(WEIGHTS, moments_m)}
    v = {k: two_d(a) for k, a in zip(WEIGHTS, moments_v)}

    loss_part, grad_x, parts = _local_step(x, loss_target, _MeshExchange(w))

    new = {k: _adamw("adamw_" + k, parts[k], w[k], m[k], v[k]) for k in BIG}
    small = [k for k in WEIGHTS if k not in BIG]
    new.update(zip(small, _adamw_many("adamw_small", [(parts[k], w[k], m[k], v[k]) for k in small])))

    loss = lax.psum(loss_part, ("x", "y", "c"))
    outs = [loss, grad_x]
    for idx in range(4):
        outs += [new[k][idx].reshape(shapes[k]) for k in WEIGHTS]
    return tuple(outs)
```

```python
import functools
import math

import jax
import jax.numpy as jnp
from jax import lax
from jax.experimental import pallas as pl
from jax.experimental.pallas import tpu as pltpu

F32 = jnp.float32
BF16 = jnp.bfloat16

D_MODEL = 1024
N_META = 16
SEQ = 2048
N_HEADS = 8
QK_NOPE = 64
QK_ROPE = 32
QK_HEAD = QK_NOPE + QK_ROPE
V_HEAD = 64
D_ATTN = N_HEADS * V_HEAD
Q_LORA = 384
KV_LORA = 256
D_RNN = 512
RNN_BW = 64
D_FF = 2816
EPS = 1e-6
LRU_C = 8.0
ROPE_THETA = 10000.0
OFF_CKV = Q_LORA + KV_LORA
OFF_KR = OFF_CKV + QK_ROPE
IN_COLS = OFF_KR + 2 * D_RNN

ADAM_LR = 0.001
ADAM_B1 = 0.9
ADAM_B2 = 0.999
ADAM_EPS = 1e-08
ADAM_WD = 0.01
ADAM_STEP = 10

N_DEV = 8
LANES = 128
HEAD_PAD = LANES
PAD_ROWS = LANES - N_META
QP_COLS = N_HEADS * HEAD_PAD
P_COLS = OFF_CKV + 2 * D_RNN + LANES
FF_CHUNK = D_FF // 2
VMEM_LIMIT = 56 * 1024 * 1024
MESH = pl.DeviceIdType.MESH


def _t_pad():
    return PAD_ROWS + N_META + SEQ


def _row_tile(n):
    return 256 if n % 256 == 0 else 128


def _const_spec(shape):
    nd = len(shape)
    return pl.BlockSpec(shape, lambda *_: (0,) * nd, pipeline_mode=pl.Buffered(1))


def _rms(x, d):
    r = lax.rsqrt(jnp.sum(x * x, axis=-1, keepdims=True) * (1.0 / d) + EPS)
    return x * r, r


def _rms_bwd(dy, xhat, r, g, d):
    dxh = dy * g
    return r * (dxh - xhat * (jnp.sum(dxh * xhat, axis=-1, keepdims=True) * (1.0 / d)))


def _colsum(x):
    return jnp.sum(x, axis=0, keepdims=True)


def _dot(a, b):
    return jnp.dot(a, b, preferred_element_type=F32)


def _dot_nt(a, b):
    return lax.dot_general(a, b, (((1,), (1,)), ((), ())), preferred_element_type=F32)


def _dot_tn(a, b):
    return lax.dot_general(a, b, (((0,), (0,)), ((), ())), preferred_element_type=F32)


def _rope(x, c, s1, s2):
    return x * c + pltpu.roll(x, 16, 1) * s1 + pltpu.roll(x, HEAD_PAD - 16, 1) * s2


def _rope_bwd(dy, c, s1, s2):
    return dy * c + pltpu.roll(dy * s1, HEAD_PAD - 16, 1) + pltpu.roll(dy * s2, 16, 1)


def _acc(ref, first, val):
    @pl.when(first)
    def _():
        ref[...] = val

    @pl.when(jnp.logical_not(first))
    def _():
        ref[...] += val


def _grid_ends(rank):
    first = last = None
    for axis in range(rank):
        pid, size = pl.program_id(axis), pl.num_programs(axis)
        first = pid == 0 if first is None else jnp.logical_and(first, pid == 0)
        last = pid == size - 1 if last is None else jnp.logical_and(last, pid == size - 1)
    return first, last


def _in_proj(h0, ln1_g, w_in_p, srcs=(), scatter=()):
    n = h0.shape[0]
    tm = _row_tile(n)
    nk = len(srcs)
    c_in, c_out, c_shape, c_sems = _exchange_specs(srcs, scatter)

    def body(h_ref, g_ref, w_ref, *rest):
        hn_ref, cq_ref, ckv_ref, xr_ref, xg_ref, kr_ref = rest[nk:nk + 6]
        start, wait = _exchange_fns(rest[:nk], rest[nk + 6:2 * nk + 6], rest[2 * nk + 6:], scatter)
        first_step, last_step = _grid_ends(1)
        pl.when(first_step)(start)
        xhat, _ = _rms(h_ref[...], D_MODEL)
        hn = (xhat * g_ref[...]).astype(BF16)
        hn_ref[...] = hn
        p = _dot(hn, w_ref[...])
        cq_ref[...] = p[:, :Q_LORA]
        ckv_ref[...] = p[:, Q_LORA:OFF_CKV]
        xr_ref[...] = p[:, OFF_CKV:OFF_CKV + D_RNN]
        xg_ref[...] = p[:, OFF_CKV + D_RNN:OFF_CKV + 2 * D_RNN]
        kr_ref[...] = p[:, OFF_CKV + 2 * D_RNN:]
        pl.when(last_step)(wait)

    def row(w):
        return pl.BlockSpec((tm, w), lambda i: (i, 0))

    widths = (D_MODEL, Q_LORA, KV_LORA, D_RNN, D_RNN, LANES)
    res = pl.pallas_call(
        body, name="in_proj", grid=(n // tm,),
        in_specs=[row(D_MODEL), _const_spec((1, D_MODEL)), _const_spec((D_MODEL, P_COLS))] + c_in,
        out_specs=[row(w) for w in widths] + c_out,
        out_shape=[jax.ShapeDtypeStruct((n, w), BF16 if k == 0 else F32) for k, w in enumerate(widths)] + c_shape,
        scratch_shapes=c_sems,
        compiler_params=pltpu.CompilerParams(dimension_semantics=("arbitrary",), vmem_limit_bytes=VMEM_LIMIT),
    )(h0, ln1_g, w_in_p, *srcs)
    return res[:6], res[6:]


def _qkv_fwd(cq, ckv, kr, gqa, gkva, w_uq_p, w_uk_p, w_v, qg, kg, rc, rs1, rs2):
    n = cq.shape[0]
    tm = _row_tile(n)

    def body(cq_ref, ckv_ref, kr_ref, gqa_ref, gkva_ref, wuq_ref, wuk_ref, wv_ref, qg_ref, kg_ref,
             c_ref, s1_ref, s2_ref, q_ref, k_ref, v_ref):
        xq, _ = _rms(cq_ref[...], Q_LORA)
        qa = (xq * gqa_ref[...]).astype(BF16)
        q = _dot(qa, wuq_ref[...])
        xkv, _ = _rms(ckv_ref[...], KV_LORA)
        kva = (xkv * gkva_ref[...]).astype(BF16)
        kn = _dot(kva, wuk_ref[...])
        v_ref[...] = _dot(kva, wv_ref[...]).astype(BF16)
        krp = kr_ref[...]
        c, s1, s2 = c_ref[...], s1_ref[...], s2_ref[...]
        for h in range(N_HEADS):
            sl = slice(h * HEAD_PAD, (h + 1) * HEAD_PAD)
            qh, _ = _rms(q[:, sl], QK_HEAD)
            q_ref[:, sl] = _rope(qh * qg_ref[...], c, s1, s2).astype(BF16)
            kh, _ = _rms(kn[:, sl] + krp, QK_HEAD)
            k_ref[:, sl] = _rope(kh * kg_ref[...], c, s1, s2).astype(BF16)

    def row(w):
        return pl.BlockSpec((tm, w), lambda i: (i, 0))

    return pl.pallas_call(
        body, name="qkv_fwd", grid=(n // tm,),
        in_specs=[row(Q_LORA), row(KV_LORA), row(LANES), _const_spec((1, Q_LORA)), _const_spec((1, KV_LORA)),
                  _const_spec((Q_LORA, QP_COLS)), _const_spec((KV_LORA, QP_COLS)), _const_spec((KV_LORA, D_ATTN)),
                  _const_spec((1, LANES)), _const_spec((1, LANES)), row(LANES), row(LANES), row(LANES)],
        out_specs=[row(QP_COLS), row(QP_COLS), row(D_ATTN)],
        out_shape=[jax.ShapeDtypeStruct((n, QP_COLS), BF16), jax.ShapeDtypeStruct((n, QP_COLS), BF16),
                   jax.ShapeDtypeStruct((n, D_ATTN), BF16)],
        compiler_params=pltpu.CompilerParams(dimension_semantics=("parallel",), vmem_limit_bytes=VMEM_LIMIT),
    )(cq, ckv, kr, gqa, gkva, w_uq_p, w_uk_p, w_v, qg, kg, rc, rs1, rs2)


def _qkv_bwd(cq, ckv, kr, dq_r, dk_r, dv, dxr, dxg, gqa, gkva, w_uq_p, w_uk_p, w_v, qg, kg, rc, rs1, rs2):
    n = cq.shape[0]
    tm = _row_tile(n)

    def body(cq_ref, ckv_ref, kr_ref, dq_ref, dk_ref, dv_ref, dxr_ref, dxg_ref, gqa_ref, gkva_ref, wuq_ref, wuk_ref,
             wv_ref, qg_ref, kg_ref, c_ref, s1_ref, s2_ref,
             dp_ref, qa_ref, kva_ref, dqp_ref, dkv_ref, dqg_ref, dkg_ref, dgqa_ref, dgkva_ref):
        first = pl.program_id(0) == 0
        dp_ref[:, OFF_CKV:OFF_CKV + D_RNN] = dxr_ref[...].astype(BF16)
        dp_ref[:, OFF_CKV + D_RNN:OFF_CKV + 2 * D_RNN] = dxg_ref[...].astype(BF16)
        xq, rq = _rms(cq_ref[...], Q_LORA)
        qa = (xq * gqa_ref[...]).astype(BF16)
        qa_ref[...] = qa
        q = _dot(qa, wuq_ref[...])
        xkv, rkv = _rms(ckv_ref[...], KV_LORA)
        kva = (xkv * gkva_ref[...]).astype(BF16)
        kva_ref[...] = kva
        kn = _dot(kva, wuk_ref[...])
        krp = kr_ref[...]
        c, s1, s2 = c_ref[...], s1_ref[...], s2_ref[...]
        lane = lax.broadcasted_iota(jnp.int32, (tm, HEAD_PAD), 1)
        rope_lanes = jnp.logical_and(lane >= QK_NOPE, lane < QK_HEAD)
        dqg = jnp.zeros((1, HEAD_PAD), F32)
        dkg = jnp.zeros((1, HEAD_PAD), F32)
        dkr = jnp.zeros((tm, HEAD_PAD), F32)
        for h in range(N_HEADS):
            sl = slice(h * HEAD_PAD, (h + 1) * HEAD_PAD)
            qh, rqh = _rms(q[:, sl], QK_HEAD)
            dy = _rope_bwd(dq_ref[:, sl], c, s1, s2)
            dqg = dqg + _colsum(dy * qh)
            dqp_ref[:, sl] = _rms_bwd(dy, qh, rqh, qg_ref[...], QK_HEAD).astype(BF16)
            kh, rkh = _rms(kn[:, sl] + krp, QK_HEAD)
            dyk = _rope_bwd(dk_ref[:, sl], c, s1, s2)
            dkg = dkg + _colsum(dyk * kh)
            dkh = _rms_bwd(dyk, kh, rkh, kg_ref[...], QK_HEAD)
            dkv_ref[:, sl] = dkh.astype(BF16)
            dkr = dkr + jnp.where(rope_lanes, dkh, 0.0)
        dkv_ref[:, QP_COLS:] = dv_ref[...].astype(BF16)
        dp_ref[:, OFF_CKV + 2 * D_RNN:] = dkr.astype(BF16)
        dqa = _dot_nt(dqp_ref[...], wuq_ref[...])
        dp_ref[:, :Q_LORA] = _rms_bwd(dqa, xq, rq, gqa_ref[...], Q_LORA).astype(BF16)
        dkva = _dot_nt(dkv_ref[:, :QP_COLS], wuk_ref[...]) + _dot_nt(dkv_ref[:, QP_COLS:], wv_ref[...])
        dp_ref[:, Q_LORA:OFF_CKV] = _rms_bwd(dkva, xkv, rkv, gkva_ref[...], KV_LORA).astype(BF16)
        _acc(dqg_ref, first, dqg)
        _acc(dkg_ref, first, dkg)
        _acc(dgqa_ref, first, _colsum(dqa * xq))
        _acc(dgkva_ref, first, _colsum(dkva * xkv))

    def row(w):
        return pl.BlockSpec((tm, w), lambda i: (i, 0))

    def acc(w):
        return pl.BlockSpec((1, w), lambda i: (0, 0))

    return pl.pallas_call(
        body, name="qkv_bwd", grid=(n // tm,),
        in_specs=[row(Q_LORA), row(KV_LORA), row(LANES), row(QP_COLS), row(QP_COLS), row(D_ATTN), row(D_RNN), row(D_RNN),
                  _const_spec((1, Q_LORA)), _const_spec((1, KV_LORA)),
                  _const_spec((Q_LORA, QP_COLS)), _const_spec((KV_LORA, QP_COLS)), _const_spec((KV_LORA, D_ATTN)),
                  _const_spec((1, LANES)), _const_spec((1, LANES)), row(LANES), row(LANES), row(LANES)],
        out_specs=[row(P_COLS), row(Q_LORA), row(KV_LORA), row(QP_COLS),
                   row(QP_COLS + D_ATTN), acc(LANES), acc(LANES), acc(Q_LORA), acc(KV_LORA)],
        out_shape=[jax.ShapeDtypeStruct((n, P_COLS), BF16), jax.ShapeDtypeStruct((n, Q_LORA), BF16),
                   jax.ShapeDtypeStruct((n, KV_LORA), BF16), jax.ShapeDtypeStruct((n, QP_COLS), BF16),
                   jax.ShapeDtypeStruct((n, QP_COLS + D_ATTN), BF16),
                   jax.ShapeDtypeStruct((1, LANES), F32), jax.ShapeDtypeStruct((1, LANES), F32),
                   jax.ShapeDtypeStruct((1, Q_LORA), F32), jax.ShapeDtypeStruct((1, KV_LORA), F32)],
        compiler_params=pltpu.CompilerParams(dimension_semantics=("arbitrary",), vmem_limit_bytes=VMEM_LIMIT),
    )(cq, ckv, kr, dq_r, dk_r, dv, dxr, dxg, gqa, gkva, w_uq_p, w_uk_p, w_v, qg, kg, rc, rs1, rs2)


def _softmax_parts(qh, kh, tq, t):
    s = _dot_nt(qh, kh) * (QK_HEAD ** -0.5)
    key = lax.broadcasted_iota(jnp.int32, (tq, t), 1)
    s = jnp.where(key >= PAD_ROWS, s, -jnp.inf)
    e = jnp.exp(s - jnp.max(s, axis=-1, keepdims=True))
    return e, jnp.sum(e, axis=-1, keepdims=True)


def _attn_specs(t, tq):
    nq = t // tq
    qspec = pl.BlockSpec((tq, 2 * HEAD_PAD), lambda b, hp, i: (b * nq + i, hp))
    kspec = pl.BlockSpec((t, 2 * HEAD_PAD), lambda b, hp, i: (b, hp))
    vspec = pl.BlockSpec((t, 2 * V_HEAD), lambda b, hp, i: (b, hp))
    ospec = pl.BlockSpec((tq, 2 * V_HEAD), lambda b, hp, i: (b * nq + i, hp))
    return nq, qspec, kspec, vspec, ospec


def _attn_fwd(q, k, v, srcs=(), scatter=()):
    n = q.shape[0]
    t = _t_pad()
    tq = t // 8
    nq, qspec, kspec, vspec, ospec = _attn_specs(t, tq)
    nk = len(srcs)
    c_in, c_out, c_shape, c_sems = _exchange_specs(srcs, scatter)

    def body(q_ref, k_ref, v_ref, *rest):
        o_ref = rest[nk]
        start, wait = _exchange_fns(rest[:nk], rest[nk + 1:2 * nk + 1], rest[2 * nk + 1:], scatter)
        first_step, last_step = _grid_ends(3)
        pl.when(first_step)(start)
        lane = lax.broadcasted_iota(jnp.int32, (tq, 2 * V_HEAD), 1)
        outs = []
        for j in range(2):
            sl = slice(j * HEAD_PAD, (j + 1) * HEAD_PAD)
            e, l = _softmax_parts(q_ref[:, sl], k_ref[:, sl], tq, t)
            outs.append(_dot(e.astype(BF16), v_ref[...]) / l)
        o_ref[...] = jnp.where(lane < V_HEAD, outs[0], outs[1])
        pl.when(last_step)(wait)

    res = pl.pallas_call(
        body, name="attn_fwd", grid=(n // t, N_HEADS // 2, nq),
        in_specs=[qspec, kspec, vspec] + c_in, out_specs=[ospec] + c_out,
        out_shape=[jax.ShapeDtypeStruct((n, D_ATTN), F32)] + c_shape, scratch_shapes=c_sems,
        compiler_params=pltpu.CompilerParams(dimension_semantics=("arbitrary", "arbitrary", "arbitrary"),
                                             vmem_limit_bytes=VMEM_LIMIT),
    )(q, k, v, *srcs)
    return res[0], res[1:]


def _attn_bwd(q, k, v, do, srcs=(), scatter=()):
    n = q.shape[0]
    t = _t_pad()
    tq = t // 4
    nq, qspec, kspec, vspec, ospec = _attn_specs(t, tq)
    nk = len(srcs)
    c_in, c_out, c_shape, c_sems = _exchange_specs(srcs, scatter)

    def body(q_ref, k_ref, v_ref, do_ref, *rest):
        dq_ref, dk_ref, dv_ref = rest[nk:nk + 3]
        start, wait = _exchange_fns(rest[:nk], rest[nk + 3:2 * nk + 3], rest[2 * nk + 3:], scatter)
        first_step, last_step = _grid_ends(3)
        pl.when(first_step)(start)
        first = pl.program_id(2) == 0
        lane = lax.broadcasted_iota(jnp.int32, (tq, 2 * V_HEAD), 1)
        do = do_ref[...]
        dv = jnp.zeros((t, 2 * V_HEAD), F32)
        for j in range(2):
            sl = slice(j * HEAD_PAD, (j + 1) * HEAD_PAD)
            qh, kh = q_ref[:, sl], k_ref[:, sl]
            e, l = _softmax_parts(qh, kh, tq, t)
            p = e / l
            in_head = (lane < V_HEAD) if j == 0 else (lane >= V_HEAD)
            doh = jnp.where(in_head, do, 0.0).astype(BF16)
            dp = _dot_nt(doh, v_ref[...])
            delta = jnp.sum(p * dp, axis=-1, keepdims=True)
            ds = (p * (dp - delta) * (QK_HEAD ** -0.5)).astype(BF16)
            dq_ref[:, sl] = _dot(ds, kh)
            dkh = _dot_tn(ds, qh)

            @pl.when(first)
            def _():
                dk_ref[:, sl] = dkh

            @pl.when(jnp.logical_not(first))
            def _():
                dk_ref[:, sl] += dkh

            dv = dv + _dot_tn(p.astype(BF16), doh)
        _acc(dv_ref, first, dv)
        pl.when(last_step)(wait)

    res = pl.pallas_call(
        body, name="attn_bwd", grid=(n // t, N_HEADS // 2, nq),
        in_specs=[qspec, kspec, vspec, ospec] + c_in, out_specs=[qspec, kspec, vspec] + c_out,
        out_shape=[jax.ShapeDtypeStruct((n, QP_COLS), F32), jax.ShapeDtypeStruct((n, QP_COLS), F32),
                   jax.ShapeDtypeStruct((n, D_ATTN), F32)] + c_shape, scratch_shapes=c_sems,
        compiler_params=pltpu.CompilerParams(dimension_semantics=("arbitrary", "arbitrary", "arbitrary"),
                                             vmem_limit_bytes=VMEM_LIMIT),
    )(q, k, v, do, *srcs)
    return res[:3], res[3:]


def _scan(chains, t):
    seg = t // 8
    rows = lax.broadcasted_iota(jnp.int32, (8, LANES), 0)

    def step(j, carry):
        new = []
        for (a_ref, b_ref, h_ref, p_ref, reverse), (h, p) in zip(chains, carry):
            idx = pl.ds(seg - 1 - j if reverse else j, 8, stride=seg)
            a = a_ref[idx, :]
            h = a * h + b_ref[idx, :]
            p = a * p
            h_ref[idx, :] = h
            p_ref[idx, :] = p
            new.append((h, p))
        return tuple(new)

    init = tuple((jnp.zeros((8, LANES), F32), jnp.ones((8, LANES), F32)) for _ in chains)
    ends = lax.fori_loop(0, seg, step, init)
    for (_, _, h_ref, p_ref, reverse), (b, a) in zip(chains, ends):
        for d in (1, 2, 4):
            if reverse:
                keep = rows < 8 - d
                a_n, b_n = pltpu.roll(a, 8 - d, 0), pltpu.roll(b, 8 - d, 0)
            else:
                keep = rows >= d
                a_n, b_n = pltpu.roll(a, d, 0), pltpu.roll(b, d, 0)
            b = a * jnp.where(keep, b_n, 0.0) + b
            a = a * jnp.where(keep, a_n, 1.0)
        for s in (range(7) if reverse else range(1, 8)):
            sl = slice(s * seg, (s + 1) * seg)
            carry_in = b[s + 1:s + 2, :] if reverse else b[s - 1:s, :]
            h_ref[sl, :] = h_ref[sl, :] + p_ref[sl, :] * carry_in


def _shift_rows(x, s, rows, t):
    if s == 0:
        return x
    rolled = pltpu.roll(x, s % t, 0)
    return jnp.where(rows >= s, rolled, 0.0) if s > 0 else jnp.where(rows < t + s, rolled, 0.0)


def _neg_expm1(x):
    series = -x * (1.0 + x * (0.5 + x * (1.0 / 6 + x * (1.0 / 24 + x * (1.0 / 120 + x * (1.0 / 720))))))
    return jnp.where(x > -0.3, series, 1.0 - jnp.exp(x))


def _gelu_parts(x):
    k = math.sqrt(2.0 / math.pi)
    th = jnp.tanh(k * (x + 0.044715 * x * x * x))
    g = 0.5 * x * (1.0 + th)
    dg = 0.5 * (1.0 + th) + 0.5 * x * (1.0 - th * th) * k * (1.0 + 3 * 0.044715 * x * x)
    return g, dg


def _lru_gates(xc, gates, lam_ref, valid, d):
    r = jax.nn.sigmoid(gates[:, (2 * d) * LANES:(2 * d + 1) * LANES])
    i = jax.nn.sigmoid(gates[:, (2 * d + 1) * LANES:(2 * d + 2) * LANES])
    neg_lam = -lam_ref[d:d + 1, :]
    sp = jnp.maximum(neg_lam, 0.0) + jnp.log1p(jnp.exp(-jnp.abs(neg_lam)))
    log_a = -LRU_C * r * sp
    a = jnp.exp(log_a)
    m = jnp.maximum(_neg_expm1(2.0 * log_a), 0.0)
    sq = jnp.sqrt(m)
    b = jnp.where(valid, sq * (i * xc), 0.0)
    return r, i, sp, a, m, sq, b


def _conv(xr, cw_ref, cb_ref, rows, t):
    return (cw_ref[0:1, :] * _shift_rows(xr, 2, rows, t) + cw_ref[1:2, :] * _shift_rows(xr, 1, rows, t)
            + cw_ref[2:3, :] * xr + cw_ref[3:4, :] * _shift_rows(xr, -1, rows, t) + cb_ref[...])


def _rnn_specs(t):
    seq = pl.BlockSpec((t, LANES), lambda cb, b: (b, cb))
    cw = pl.BlockSpec((4, LANES), lambda cb, b: (0, cb))
    vec1 = pl.BlockSpec((1, LANES), lambda cb, b: (0, cb))
    vec2 = pl.BlockSpec((2, LANES), lambda cb, b: (0, cb))
    wblk = pl.BlockSpec((1, LANES, 4 * LANES), lambda cb, b: (cb, 0, 0))
    gbias = pl.BlockSpec((1, 1, 4 * LANES), lambda cb, b: (cb, 0, 0))
    return seq, cw, vec1, vec2, wblk, gbias


def _rnn_fwd(xr, xg, conv_w, conv_b, wblk, gbias, lam):
    n = xr.shape[0]
    t = _t_pad()
    seq, cw, vec1, vec2, wspec, gspec = _rnn_specs(t)

    def body(xr_ref, xg_ref, cw_ref, cb_ref, w_ref, gb_ref, lam_ref, o_ref, a_s, b_s, h_s, p_s):
        rows = lax.broadcasted_iota(jnp.int32, (t, LANES), 0)
        valid = rows >= PAD_ROWS
        xc = _conv(xr_ref[...], cw_ref, cb_ref, rows, t)
        gates = _dot(xc.astype(BF16), w_ref[0]) + gb_ref[0]
        for d in range(2):
            _, _, _, a, _, _, b = _lru_gates(xc, gates, lam_ref, valid, d)
            a_s[d] = a
            b_s[d] = b
        _scan([(a_s.at[d], b_s.at[d], h_s.at[d], p_s.at[d], d == 1) for d in range(2)], t)
        g, _ = _gelu_parts(xg_ref[...])
        o_ref[...] = (h_s[0] + h_s[1]) * g

    return pl.pallas_call(
        body, name="rnn_fwd", grid=(D_RNN // LANES, n // t),
        in_specs=[seq, seq, cw, vec1, wspec, gspec, vec2], out_specs=seq,
        out_shape=jax.ShapeDtypeStruct((n, D_RNN), F32),
        scratch_shapes=[pltpu.VMEM((2, t, LANES), F32)] * 4,
        compiler_params=pltpu.CompilerParams(dimension_semantics=("parallel", "parallel"), vmem_limit_bytes=VMEM_LIMIT),
    )(xr, xg, conv_w, conv_b, wblk, gbias, lam)


def _rnn_bwd(xr, xg, do, conv_w, conv_b, wblk, gbias, lam):
    n = xr.shape[0]
    t = _t_pad()
    seq, cw, vec1, vec2, wspec, gspec = _rnn_specs(t)

    def body(xr_ref, xg_ref, do_ref, cw_ref, cb_ref, w_ref, gb_ref, lam_ref,
             dxr_ref, dxg_ref, dcw_ref, dcb_ref, dw_ref, dgb_ref, dlam_ref, a_s, b_s, h_s, l_s, p_s, dg_s):
        first = pl.program_id(1) == 0
        rows = lax.broadcasted_iota(jnp.int32, (t, LANES), 0)
        valid = rows >= PAD_ROWS
        xr = xr_ref[...]
        xc = _conv(xr, cw_ref, cb_ref, rows, t)
        xcb = xc.astype(BF16)
        gates = _dot(xcb, w_ref[0]) + gb_ref[0]
        for d in range(2):
            _, _, _, a, _, _, b = _lru_gates(xc, gates, lam_ref, valid, d)
            a_s[d] = a
            b_s[d] = b
        _scan([(a_s.at[d], b_s.at[d], h_s.at[d], p_s.at[d], d == 1) for d in range(2)], t)
        g, dg = _gelu_parts(xg_ref[...])
        do = do_ref[...]
        dxg_ref[...] = do * (h_s[0] + h_s[1]) * dg
        b_s[0] = do * g
        for d in range(2):
            a_s[d] = _shift_rows(a_s[d], -1 if d == 0 else 1, rows, t)
        _scan([(a_s.at[d], b_s.at[0], l_s.at[d], p_s.at[d], d == 0) for d in range(2)], t)
        dxc = jnp.zeros((t, LANES), F32)
        dlams = []
        for d in range(2):
            r, i, sp, a, m, sq, _ = _lru_gates(xc, gates, lam_ref, valid, d)
            lam_t = l_s[d]
            da = lam_t * _shift_rows(h_s[d], 1 if d == 0 else -1, rows, t)
            lam_v = jnp.where(valid, lam_t, 0.0)
            dsq = lam_v * (i * xc)
            di = lam_v * sq * xc
            dxc = dxc + lam_v * sq * i
            dm = jnp.where(m > 0.0, dsq * 0.5 / jnp.where(m > 0.0, sq, 1.0), 0.0)
            dla = da * a - 2.0 * dm * a * a
            dr = dla * (-LRU_C) * sp
            dsp = _colsum(dla * (-LRU_C) * r)
            dlams.append(dsp * -jax.nn.sigmoid(-lam_ref[d:d + 1, :]))
            dg_s[:, (2 * d) * LANES:(2 * d + 1) * LANES] = (dr * r * (1.0 - r)).astype(BF16)
            dg_s[:, (2 * d + 1) * LANES:(2 * d + 2) * LANES] = (di * i * (1.0 - i)).astype(BF16)
        dgates = dg_s[...]
        dxc = dxc + _dot_nt(dgates, w_ref[0])
        dxr_ref[...] = (cw_ref[0:1, :] * _shift_rows(dxc, -2, rows, t) + cw_ref[1:2, :] * _shift_rows(dxc, -1, rows, t)
                        + cw_ref[2:3, :] * dxc + cw_ref[3:4, :] * _shift_rows(dxc, 1, rows, t))
        dcw = jnp.concatenate([_colsum(dxc * _shift_rows(xr, 2 - j, rows, t)) for j in range(4)], axis=0)
        _acc(dcw_ref, first, dcw)
        _acc(dcb_ref, first, _colsum(dxc))
        _acc(dw_ref, first, _dot_tn(xcb, dgates)[None])
        _acc(dgb_ref, first, _colsum(dgates.astype(F32))[None])
        _acc(dlam_ref, first, jnp.concatenate(dlams, axis=0))

    return pl.pallas_call(
        body, name="rnn_bwd", grid=(D_RNN // LANES, n // t),
        in_specs=[seq, seq, seq, cw, vec1, wspec, gspec, vec2],
        out_specs=[seq, seq, cw, vec1, wspec, gspec, vec2],
        out_shape=[jax.ShapeDtypeStruct((n, D_RNN), F32), jax.ShapeDtypeStruct((n, D_RNN), F32),
                   jax.ShapeDtypeStruct((4, D_RNN), F32), jax.ShapeDtypeStruct((1, D_RNN), F32),
                   jax.ShapeDtypeStruct((D_RNN // LANES, LANES, 4 * LANES), F32),
                   jax.ShapeDtypeStruct((D_RNN // LANES, 1, 4 * LANES), F32), jax.ShapeDtypeStruct((2, D_RNN), F32)],
        scratch_shapes=[pltpu.VMEM((2, t, LANES), F32)] * 5 + [pltpu.VMEM((t, 4 * LANES), BF16)],
        compiler_params=pltpu.CompilerParams(dimension_semantics=("parallel", "arbitrary"), vmem_limit_bytes=VMEM_LIMIT),
    )(xr, xg, do, conv_w, conv_b, wblk, gbias, lam)


def _post(oa, orn, h0, tgt, ga, gr, g2, w_out, w_gate, w_up, w_down):
    n = oa.shape[0]
    tm = _row_tile(n)
    t = _t_pad()

    def body(oa_ref, or_ref, h0_ref, tgt_ref, ga_ref, gr_ref, g2_ref, wo_ref, wg_ref, wu_ref, wd_ref,
             doa_ref, dor_ref, dh1_ref, mix_ref, h1n_ref, act_ref, dgate_ref, dup_ref, dy_ref,
             loss_ref, dga_ref, dgr_ref, dg2_ref, gate_s, up_s):
        first = pl.program_id(0) == 0
        xa, ra = _rms(oa_ref[...], D_ATTN)
        xr, rr = _rms(or_ref[...], D_RNN)
        mix_ref[:, :D_ATTN] = (xa * ga_ref[...]).astype(BF16)
        mix_ref[:, D_ATTN:] = (xr * gr_ref[...]).astype(BF16)
        h1 = h0_ref[...] + _dot(mix_ref[...], wo_ref[...])
        x2, r2 = _rms(h1, D_MODEL)
        h1n = (x2 * g2_ref[...]).astype(BF16)
        h1n_ref[...] = h1n
        y = h1
        for cs in range(0, D_FF, FF_CHUNK):
            sl = slice(cs, cs + FF_CHUNK)
            gate = _dot(h1n, wg_ref[:, sl])
            up = _dot(h1n, wu_ref[:, sl])
            gate_s[:, sl] = gate
            up_s[:, sl] = up
            act = (gate * jax.nn.sigmoid(gate) * up).astype(BF16)
            act_ref[:, sl] = act
            y = y + _dot(act, wd_ref[sl, :])
        row = pl.program_id(0) * tm + lax.broadcasted_iota(jnp.int32, (tm, 1), 0)
        for _ in range(1, n // t):
            row = jnp.where(row >= t, row - t, row)
        err = jnp.where(row >= PAD_ROWS + N_META, y - tgt_ref[...], 0.0)
        _acc(loss_ref, first, jnp.full((1, LANES), 0.5 / D_MODEL, F32) * jnp.sum(err * err))
        dy = err * (1.0 / D_MODEL)
        dyb = dy.astype(BF16)
        dy_ref[...] = dyb
        dh1n = jnp.zeros((tm, D_MODEL), F32)
        for cs in range(0, D_FF, FF_CHUNK):
            sl = slice(cs, cs + FF_CHUNK)
            dact = _dot_nt(dyb, wd_ref[sl, :])
            gate, up = gate_s[:, sl], up_s[:, sl]
            sg = jax.nn.sigmoid(gate)
            dgate = (dact * up * sg * (1.0 + gate * (1.0 - sg))).astype(BF16)
            dup = (dact * gate * sg).astype(BF16)
            dgate_ref[:, sl] = dgate
            dup_ref[:, sl] = dup
            dh1n = dh1n + _dot_nt(dgate, wg_ref[:, sl]) + _dot_nt(dup, wu_ref[:, sl])
        _acc(dg2_ref, first, _colsum(dh1n * x2))
        dh1 = dy + _rms_bwd(dh1n, x2, r2, g2_ref[...], D_MODEL)
        dh1_ref[...] = dh1
        dmix = _dot_nt(dh1.astype(BF16), wo_ref[...])
        dma, dmr = dmix[:, :D_ATTN], dmix[:, D_ATTN:]
        _acc(dga_ref, first, _colsum(dma * xa))
        _acc(dgr_ref, first, _colsum(dmr * xr))
        doa_ref[...] = _rms_bwd(dma, xa, ra, ga_ref[...], D_ATTN)
        dor_ref[...] = _rms_bwd(dmr, xr, rr, gr_ref[...], D_RNN)

    def row(w):
        return pl.BlockSpec((tm, w), lambda i: (i, 0))

    def acc(w):
        return pl.BlockSpec((1, w), lambda i: (0, 0))

    outs = [(D_ATTN, F32), (D_RNN, F32), (D_MODEL, F32), (D_MODEL, BF16), (D_MODEL, BF16), (D_FF, BF16),
            (D_FF, BF16), (D_FF, BF16), (D_MODEL, BF16)]
    accs = [LANES, D_ATTN, D_RNN, D_MODEL]
    return pl.pallas_call(
        body, name="post", grid=(n // tm,),
        in_specs=[row(D_ATTN), row(D_RNN), row(D_MODEL), row(D_MODEL),
                  _const_spec((1, D_ATTN)), _const_spec((1, D_RNN)), _const_spec((1, D_MODEL)),
                  _const_spec((D_MODEL, D_MODEL)), _const_spec((D_MODEL, D_FF)), _const_spec((D_MODEL, D_FF)),
                  _const_spec((D_FF, D_MODEL))],
        out_specs=[row(w) for w, _ in outs] + [acc(w) for w in accs],
        out_shape=[jax.ShapeDtypeStruct((n, w), dt) for w, dt in outs]
        + [jax.ShapeDtypeStruct((1, w), F32) for w in accs],
        scratch_shapes=[pltpu.VMEM((tm, D_FF), F32), pltpu.VMEM((tm, D_FF), F32)],
        compiler_params=pltpu.CompilerParams(dimension_semantics=("arbitrary",), vmem_limit_bytes=VMEM_LIMIT),
    )(oa, orn, h0, tgt, ga, gr, g2, w_out, w_gate, w_up, w_down)


def _in_bwd(dp, h0, dh1, ln1_g, w_in_p, srcs=(), scatter=()):
    n = h0.shape[0]
    tm = _row_tile(n)
    nk = len(srcs)
    c_in, c_out, c_shape, c_sems = _exchange_specs(srcs, scatter)

    def body(dp_ref, h0_ref, dh1_ref, g_ref, w_ref, *rest):
        dh0_ref, dg_ref = rest[nk:nk + 2]
        start, wait = _exchange_fns(rest[:nk], rest[nk + 2:2 * nk + 2], rest[2 * nk + 2:], scatter)
        first_step, last_step = _grid_ends(1)
        pl.when(first_step)(start)
        dhn = _dot_nt(dp_ref[...], w_ref[...])
        xhat, r = _rms(h0_ref[...], D_MODEL)
        _acc(dg_ref, first_step, _colsum(dhn * xhat))
        dh0_ref[...] = dh1_ref[...] + _rms_bwd(dhn, xhat, r, g_ref[...], D_MODEL)
        pl.when(last_step)(wait)

    def row(w):
        return pl.BlockSpec((tm, w), lambda i: (i, 0))

    res = pl.pallas_call(
        body, name="in_bwd", grid=(n // tm,),
        in_specs=[row(P_COLS), row(D_MODEL), row(D_MODEL), _const_spec((1, D_MODEL)), _const_spec((D_MODEL, P_COLS))] + c_in,
        out_specs=[row(D_MODEL), pl.BlockSpec((1, D_MODEL), lambda i: (0, 0))] + c_out,
        out_shape=[jax.ShapeDtypeStruct((n, D_MODEL), F32), jax.ShapeDtypeStruct((1, D_MODEL), F32)] + c_shape,
        scratch_shapes=c_sems,
        compiler_params=pltpu.CompilerParams(dimension_semantics=("arbitrary",), vmem_limit_bytes=VMEM_LIMIT),
    )(dp, h0, dh1, ln1_g, w_in_p, *srcs)
    return res[:2], res[2:]


def _pick_tile(width, cap):
    best = LANES
    for mult in range(1, width // LANES + 1):
        cand = mult * LANES
        if width % cand == 0 and cand <= cap:
            best = cand
    return best


def _matmul_tn(name, a, b):
    n, ka = a.shape
    kb = b.shape[1]
    ta, tb = _pick_tile(ka, 1408), _pick_tile(kb, 1408)
    tk = n // 4

    def body(a_ref, b_ref, o_ref):
        _acc(o_ref, pl.program_id(2) == 0, _dot_tn(a_ref[...].astype(BF16), b_ref[...].astype(BF16)))

    return pl.pallas_call(
        body, name=name, grid=(ka // ta, kb // tb, n // tk),
        in_specs=[pl.BlockSpec((tk, ta), lambda i, j, k: (k, i)), pl.BlockSpec((tk, tb), lambda i, j, k: (k, j))],
        out_specs=pl.BlockSpec((ta, tb), lambda i, j, k: (i, j)),
        out_shape=jax.ShapeDtypeStruct((ka, kb), F32),
        compiler_params=pltpu.CompilerParams(dimension_semantics=("parallel", "parallel", "arbitrary"),
                                             vmem_limit_bytes=VMEM_LIMIT),
    )(a, b)


def _matmul_tn_shards(name, a, b, row_sharded):
    n, ka = a.shape
    kb = b.shape[1]
    ta, tb = _pick_tile(ka, 1408), _pick_tile(kb, 1408)
    tk = n // 4
    if row_sharded:
        width = ka // N_DEV
        per = ta // width
        out_shape, out_block = (N_DEV, width, kb), (per, width, tb)
        out_map = lambda i, j, k: (i, 0, j)
    else:
        width = kb // N_DEV
        per = tb // width
        out_shape, out_block = (N_DEV, ka, width), (per, ta, width)
        out_map = lambda i, j, k: (j, i, 0)

    def body(a_ref, b_ref, o_ref, acc_ref):
        _acc(acc_ref, pl.program_id(2) == 0, _dot_tn(a_ref[...].astype(BF16), b_ref[...].astype(BF16)))

        @pl.when(pl.program_id(2) == pl.num_programs(2) - 1)
        def _():
            for s in range(per):
                sl = slice(s * width, (s + 1) * width)
                o_ref[s] = (acc_ref[sl, :] if row_sharded else acc_ref[:, sl]).astype(BF16)

    return pl.pallas_call(
        body, name=name, grid=(ka // ta, kb // tb, n // tk),
        in_specs=[pl.BlockSpec((tk, ta), lambda i, j, k: (k, i)), pl.BlockSpec((tk, tb), lambda i, j, k: (k, j))],
        out_specs=pl.BlockSpec(out_block, out_map), out_shape=jax.ShapeDtypeStruct(out_shape, BF16),
        scratch_shapes=[pltpu.VMEM((ta, tb), F32)],
        compiler_params=pltpu.CompilerParams(dimension_semantics=("parallel", "parallel", "arbitrary"),
                                             vmem_limit_bytes=VMEM_LIMIT),
    )(a, b)


def _adamw_math(g8_ref, w_ref, m_ref, v_ref, g_ref, d_ref, nm_ref, nv_ref):
    g = g8_ref[0].astype(F32)
    for s in range(1, N_DEV):
        g = g + g8_ref[s].astype(F32)
    g_ref[...] = g
    nm = ADAM_B1 * m_ref[...] + (1.0 - ADAM_B1) * g
    nv = ADAM_B2 * v_ref[...] + (1.0 - ADAM_B2) * (g * g)
    nm_ref[...] = nm
    nv_ref[...] = nv
    m_hat = nm / (1.0 - ADAM_B1 ** ADAM_STEP)
    v_hat = nv / (1.0 - ADAM_B2 ** ADAM_STEP)
    d_ref[...] = -ADAM_LR * (m_hat / (jnp.sqrt(v_hat) + ADAM_EPS) + ADAM_WD * w_ref[...])


def _adamw_many(name, items):
    count = len(items)

    def body(*refs):
        ins, outs = refs[:4 * count], refs[4 * count:]
        for i in range(count):
            _adamw_math(*ins[4 * i:4 * i + 4], *outs[4 * i:4 * i + 4])

    flat = [a for item in items for a in item]
    res = pl.pallas_call(
        body, name=name,
        out_shape=[jax.ShapeDtypeStruct(item[1].shape, F32) for item in items for _ in range(4)],
        compiler_params=pltpu.CompilerParams(vmem_limit_bytes=VMEM_LIMIT),
    )(*flat)
    return [tuple(res[4 * i:4 * i + 4]) for i in range(count)]


def _adamw(name, g8, w, m, v):
    rows, cols = w.shape
    tr = rows
    for cand in (256, 176, 128, 64):
        if rows % cand == 0 and rows > cand:
            tr = cand
            break

    def body(*refs):
        _adamw_math(*refs)

    blk = pl.BlockSpec((tr, cols), lambda i: (i, 0))
    return pl.pallas_call(
        body, name=name, grid=(rows // tr,),
        in_specs=[pl.BlockSpec((N_DEV, tr, cols), lambda i: (0, i, 0)), blk, blk, blk],
        out_specs=[blk] * 4, out_shape=[jax.ShapeDtypeStruct((rows, cols), F32)] * 4,
        compiler_params=pltpu.CompilerParams(dimension_semantics=("parallel",), vmem_limit_bytes=VMEM_LIMIT),
    )(g8, w, m, v)


def _exchange_specs(srcs, scatter):
    nk = len(srcs)
    if not nk:
        return [], [], [], []
    any_spec = pl.BlockSpec(memory_space=pl.ANY)
    out_shape = [jax.ShapeDtypeStruct(s.shape if sc else (N_DEV,) + s.shape, s.dtype) for s, sc in zip(srcs, scatter)]
    sems = [pltpu.SemaphoreType.DMA((nk, N_DEV - 1)), pltpu.SemaphoreType.DMA((nk, N_DEV - 1)),
            pltpu.SemaphoreType.DMA((nk,))]
    return [any_spec] * nk, [any_spec] * nk, out_shape, sems


def _exchange_fns(src_refs, out_refs, sems, scatter):
    nk = len(src_refs)
    if not nk:
        return (lambda: None), (lambda: None)
    send_sems, recv_sems, local_sems = sems

    def plan():
        x, y, c = lax.axis_index("x"), lax.axis_index("y"), lax.axis_index("c")
        me = 4 * x + 2 * y + c
        flips = [(f >> 2 & 1, f >> 1 & 1, f & 1) for f in range(1, N_DEV)]
        peers = [(1 - x if fx else x, 1 - y if fy else y, 1 - c if fc else c) for fx, fy, fc in flips]
        pids = [4 * px + 2 * py + pc for px, py, pc in peers]

        def src_for(k, dest):
            return src_refs[k].at[dest] if scatter[k] else src_refs[k]

        def remote(k, j, dst_slot, dest):
            return pltpu.make_async_remote_copy(
                src_ref=src_for(k, dest), dst_ref=out_refs[k].at[dst_slot],
                send_sem=send_sems.at[k, j], recv_sem=recv_sems.at[k, j],
                device_id=peers[j], device_id_type=MESH)

        local = [pltpu.make_async_copy(src_for(k, me), out_refs[k].at[me], local_sems.at[k]) for k in range(nk)]
        sends = [remote(k, j, me, pids[j]) for k in range(nk) for j in range(N_DEV - 1)]
        recvs = [remote(k, j, pids[j], pids[j]) for k in range(nk) for j in range(N_DEV - 1)]
        return local, sends, recvs

    def start():
        local, sends, _ = plan()
        for cp in local + sends:
            cp.start()

    def wait():
        local, sends, recvs = plan()
        for cp in recvs:
            cp.wait_recv()
        for cp in sends:
            cp.wait_send()
        for cp in local:
            cp.wait()

    return start, wait


def _exchange(name, srcs, scatter):
    nk = len(srcs)
    c_in, c_out, c_shape, c_sems = _exchange_specs(srcs, scatter)

    def body(*refs):
        start, wait = _exchange_fns(refs[:nk], refs[nk:2 * nk], refs[2 * nk:], scatter)
        start()
        wait()

    return pl.pallas_call(body, name=name, in_specs=c_in, out_specs=c_out, out_shape=c_shape, scratch_shapes=c_sems)(*srcs)


def _cols_from_shards(g):
    return jnp.transpose(g, (1, 0, 2)).reshape(g.shape[1], -1)


def _cols_to_shards(w):
    return jnp.transpose(w.reshape(w.shape[0], N_DEV, -1), (1, 0, 2))


def _rope_tables(n):
    t = _t_pad()
    pos = (jnp.arange(t, dtype=F32) - PAD_ROWS)
    half = QK_ROPE // 2
    freqs = 1.0 / (ROPE_THETA ** (jnp.arange(half, dtype=F32) / half))
    ang = pos[:, None] * freqs[None, :]
    cos, sin = jnp.cos(ang), jnp.sin(ang)
    z = lambda w: jnp.zeros((t, w), F32)
    c = jnp.concatenate([jnp.ones((t, QK_NOPE), F32), cos, cos, z(HEAD_PAD - QK_HEAD)], axis=1)
    s1 = jnp.concatenate([z(QK_NOPE + half), sin, z(HEAD_PAD - QK_HEAD)], axis=1)
    s2 = jnp.concatenate([z(QK_NOPE), -sin, z(HEAD_PAD - QK_NOPE - half)], axis=1)
    reps = n // t
    return tuple(jnp.tile(a, (reps, 1)) for a in (c, s1, s2))


def _block_diag_gates(lru_wa, lru_wi):
    eye = jnp.eye(2, dtype=lru_wa.dtype)

    def bd(w):
        w = w.reshape(2, D_RNN // LANES, 2, RNN_BW, RNN_BW)
        full = w[:, :, :, :, None, :] * eye[None, None, :, None, :, None]
        return full.reshape(2, D_RNN // LANES, LANES, LANES)

    a, i = bd(lru_wa), bd(lru_wi)
    return jnp.concatenate([a[0], i[0], a[1], i[1]], axis=-1)


def _unblock_gates(dw):
    nb = D_RNN // LANES
    parts = dw.reshape(nb, 2, RNN_BW, 4, 2, RNN_BW)
    diag = jnp.stack([parts[:, k, :, :, k, :] for k in range(2)], axis=1)
    diag = jnp.transpose(diag, (3, 0, 1, 2, 4)).reshape(4, 2 * nb, RNN_BW, RNN_BW)
    return jnp.stack([diag[0], diag[2]]), jnp.stack([diag[1], diag[3]])


WEIGHTS = ("meta_tokens", "ln1_g", "w_in", "q_a_norm_g", "w_uq", "kv_a_norm_g", "w_ukv", "q_norm_g", "k_norm_g",
           "conv_w", "conv_b", "lru_wa", "lru_ba", "lru_wi", "lru_bi", "lru_lambda", "attn_out_g", "rnn_out_g",
           "w_out", "ln2_g", "w_gate", "w_up", "w_down")
BIG = ("w_in", "w_uq", "w_ukv", "w_out", "w_gate", "w_up", "w_down")
ROW_SHARDED = ("w_out", "w_down")
REPLICATED = ("ln1_g", "q_a_norm_g", "kv_a_norm_g", "q_norm_g", "k_norm_g", "conv_b", "lru_wa", "lru_wi",
              "attn_out_g", "rnn_out_g", "ln2_g")
G_FIRST = ("w_in", "meta_tokens")
G_MID = ("w_uq", "w_ukv", "conv_w", "lru_ba", "lru_bi", "lru_lambda")
LATE = ("w_out", "w_gate", "w_up", "w_down")
G_LAST = ("meta_tokens", "ln1_g")


def _local_step(x, tgt, ex):
    nb = x.shape[0]
    t = _t_pad()
    n = nb * t
    local = ex.local
    first = ex.gathered(G_FIRST, ex.run("gather_first", *ex.gather_srcs(G_FIRST)))
    meta, w_in = first["meta_tokens"], first["w_in"]
    lead = jnp.zeros((nb, PAD_ROWS, D_MODEL), F32)
    h0 = jnp.concatenate([lead, jnp.broadcast_to(meta[None], (nb, N_META, D_MODEL)), x], axis=1).reshape(n, D_MODEL)
    tgt_p = jnp.concatenate([jnp.zeros((nb, PAD_ROWS + N_META, D_MODEL), F32), tgt], axis=1).reshape(n, D_MODEL)

    zc = lambda c: jnp.zeros((D_MODEL, c), w_in.dtype)
    w_in_p = jnp.concatenate([w_in[:, :OFF_CKV], w_in[:, OFF_KR:], zc(QK_NOPE), w_in[:, OFF_CKV:OFF_KR],
                              zc(HEAD_PAD - QK_HEAD)], axis=1)
    pad_g = lambda g: jnp.pad(g, ((0, 0), (0, HEAD_PAD - QK_HEAD)))
    qg, kg = pad_g(local["q_norm_g"]), pad_g(local["k_norm_g"])
    rc, rs1, rs2 = _rope_tables(n)
    wblk = _block_diag_gates(local["lru_wa"].reshape(2, -1, RNN_BW, RNN_BW),
                             local["lru_wi"].reshape(2, -1, RNN_BW, RNN_BW)).astype(BF16)
    nblk = D_RNN // LANES

    (hn, cq, ckv, xr, xg, kr), got = _in_proj(h0, local["ln1_g"], w_in_p, *ex.gather_srcs(G_MID))
    w = ex.gathered(G_MID, got)
    w_uq_p = jnp.pad(w["w_uq"].reshape(Q_LORA, N_HEADS, QK_HEAD), ((0, 0), (0, 0), (0, HEAD_PAD - QK_HEAD))
                     ).reshape(Q_LORA, QP_COLS)
    ukv = w["w_ukv"].reshape(KV_LORA, N_HEADS, QK_NOPE + V_HEAD)
    w_uk_p = jnp.pad(ukv[:, :, :QK_NOPE], ((0, 0), (0, 0), (0, HEAD_PAD - QK_NOPE))).reshape(KV_LORA, QP_COLS)
    w_v = ukv[:, :, QK_NOPE:].reshape(KV_LORA, D_ATTN)
    gbias = jnp.stack([w["lru_ba"][0], w["lru_bi"][0], w["lru_ba"][1], w["lru_bi"][1]], axis=0)
    gbias = jnp.transpose(gbias.reshape(4, nblk, LANES), (1, 0, 2)).reshape(nblk, 1, 4 * LANES)

    q, k, v = _qkv_fwd(cq, ckv, kr, local["q_a_norm_g"], local["kv_a_norm_g"], w_uq_p, w_uk_p, w_v, qg, kg, rc, rs1, rs2)
    oa, got = _attn_fwd(q, k, v, *ex.gather_srcs(LATE))
    late = ex.gathered(LATE, got)
    orn = _rnn_fwd(xr, xg, w["conv_w"], local["conv_b"], wblk, gbias, w["lru_lambda"])
    (doa, dor, dh1, mix, h1n, act, dgate, dup, dyb, loss, dga, dgr, dg2) = _post(
        oa, orn, h0, tgt_p, local["attn_out_g"], local["rnn_out_g"], local["ln2_g"], late["w_out"], late["w_gate"],
        late["w_up"], late["w_down"])
    wire = {"w_out": _matmul_tn_shards("dw_out", mix, dh1, True), "w_gate": _matmul_tn_shards("dw_gate", h1n, dgate, False),
            "w_up": _matmul_tn_shards("dw_up", h1n, dup, False), "w_down": _matmul_tn_shards("dw_down", act, dyb, True)}
    dxr, dxg, dcw, dcb, dwblk, dgb, dlam = _rnn_bwd(xr, xg, dor, w["conv_w"], local["conv_b"], wblk, gbias, w["lru_lambda"])
    dwa, dwi = _unblock_gates(dwblk)
    dgb = jnp.transpose(dgb.reshape(nblk, 4, LANES), (1, 0, 2)).reshape(4, D_RNN)
    wire.update(ex.to_wire({
        "conv_w": dcw, "conv_b": dcb, "lru_wa": dwa.reshape(-1, RNN_BW), "lru_ba": jnp.stack([dgb[0], dgb[2]]),
        "lru_wi": dwi.reshape(-1, RNN_BW), "lru_bi": jnp.stack([dgb[1], dgb[3]]), "lru_lambda": dlam,
        "attn_out_g": dga, "rnn_out_g": dgr, "ln2_g": dg2}))
    names = tuple(wire)
    (dq_r, dk_r, dv), got = _attn_bwd(q, k, v, doa, *ex.scatter_srcs(names, wire))
    summed = ex.scattered(names, wire, got)
    (dp, qa, kva, dqp, dkv, dqg, dkg, dgqa, dgkva) = _qkv_bwd(
        cq, ckv, kr, dq_r, dk_r, dv, dxr, dxg, local["q_a_norm_g"], local["kv_a_norm_g"], w_uq_p, w_uk_p, w_v, qg, kg,
        rc, rs1, rs2)
    dw_in_p = _matmul_tn("dw_in", hn, dp)
    dw_uq_p = _matmul_tn("dw_uq", qa, dqp)
    dw_kv = _matmul_tn("dw_ukv", kva, dkv)
    kr0 = OFF_CKV + 2 * D_RNN + QK_NOPE
    dw_in = jnp.concatenate([dw_in_p[:, :OFF_CKV], dw_in_p[:, kr0:kr0 + QK_ROPE], dw_in_p[:, OFF_CKV:OFF_CKV + 2 * D_RNN]],
                            axis=1)
    dw_uq = dw_uq_p.reshape(Q_LORA, N_HEADS, HEAD_PAD)[:, :, :QK_HEAD].reshape(Q_LORA, N_HEADS * QK_HEAD)
    dw_ukv = jnp.concatenate([dw_kv[:, :QP_COLS].reshape(KV_LORA, N_HEADS, HEAD_PAD)[:, :, :QK_NOPE],
                              dw_kv[:, QP_COLS:].reshape(KV_LORA, N_HEADS, V_HEAD)], axis=2).reshape(KV_LORA, -1)
    wire = ex.to_wire({"w_in": dw_in, "q_a_norm_g": dgqa, "w_uq": dw_uq, "kv_a_norm_g": dgkva, "w_ukv": dw_ukv,
                       "q_norm_g": dqg[:, :QK_HEAD], "k_norm_g": dkg[:, :QK_HEAD]})
    names = tuple(wire)
    (dh0, dg1), got = _in_bwd(dp, h0, dh1, local["ln1_g"], w_in_p, *ex.scatter_srcs(names, wire))
    summed.update(ex.scattered(names, wire, got))

    dh0 = dh0.reshape(nb, t, D_MODEL)
    wire = ex.to_wire({"meta_tokens": jnp.sum(dh0[:, PAD_ROWS:PAD_ROWS + N_META], axis=0), "ln1_g": dg1})
    got = ex.run("reduce_last", *ex.scatter_srcs(G_LAST, wire))
    summed.update(ex.scattered(G_LAST, wire, got))
    return loss[0, 0], dh0[:, PAD_ROWS + N_META:], summed


class _MeshExchange:
    def __init__(self, shards):
        self.local = shards

    @staticmethod
    def run(name, srcs, scatter):
        return _exchange(name, srcs, scatter)

    def gather_srcs(self, names):
        return [self.local[k].astype(BF16) if k in BIG else self.local[k] for k in names], [False] * len(names)

    @staticmethod
    def gathered(names, outs):
        return {k: g.reshape(-1, g.shape[-1]) if k in ROW_SHARDED else _cols_from_shards(g) for k, g in zip(names, outs)}

    @staticmethod
    def to_wire(grads):
        wire = {}
        for k, g in grads.items():
            if k in REPLICATED:
                wire[k] = g
            elif k in ROW_SHARDED:
                wire[k] = g.reshape(N_DEV, -1, g.shape[-1]).astype(BF16)
            else:
                wire[k] = _cols_to_shards(g).astype(BF16) if k in BIG else _cols_to_shards(g)
        return wire

    @staticmethod
    def scatter_srcs(names, wire):
        return [wire[k] for k in names], [k not in REPLICATED for k in names]

    @staticmethod
    def scattered(names, wire, outs):
        return dict(zip(names, outs))


def kernel(x, meta_tokens, ln1_g, w_in, q_a_norm_g, w_uq, kv_a_norm_g, w_ukv, q_norm_g, k_norm_g, conv_w, conv_b, lru_wa, lru_ba, lru_wi, lru_bi, lru_lambda, attn_out_g, rnn_out_g, w_out, ln2_g, w_gate, w_up, w_down, loss_target, m_meta_tokens, m_ln1_g, m_w_in, m_q_a_norm_g, m_w_uq, m_kv_a_norm_g, m_w_ukv, m_q_norm_g, m_k_norm_g, m_conv_w, m_conv_b, m_lru_wa, m_lru_ba, m_lru_wi, m_lru_bi, m_lru_lambda, m_attn_out_g, m_rnn_out_g, m_w_out, m_ln2_g, m_w_gate, m_w_up, m_w_down, v_meta_tokens, v_ln1_g, v_w_in, v_q_a_norm_g, v_w_uq, v_kv_a_norm_g, v_w_ukv, v_q_norm_g, v_k_norm_g, v_conv_w, v_conv_b, v_lru_wa, v_lru_ba, v_lru_wi, v_lru_bi, v_lru_lambda, v_attn_out_g, v_rnn_out_g, v_w_out, v_ln2_g, v_w_gate, v_w_up, v_w_down):
    given = (meta_tokens, ln1_g, w_in, q_a_norm_g, w_uq, kv_a_norm_g, w_ukv, q_norm_g, k_norm_g, conv_w, conv_b,
             lru_wa, lru_ba, lru_wi, lru_bi, lru_lambda, attn_out_g, rnn_out_g, w_out, ln2_g, w_gate, w_up, w_down)
    moments_m = (m_meta_tokens, m_ln1_g, m_w_in, m_q_a_norm_g, m_w_uq, m_kv_a_norm_g, m_w_ukv, m_q_norm_g, m_k_norm_g,
                 m_conv_w, m_conv_b, m_lru_wa, m_lru_ba, m_lru_wi, m_lru_bi, m_lru_lambda, m_attn_out_g, m_rnn_out_g,
                 m_w_out, m_ln2_g, m_w_gate, m_w_up, m_w_down)
    moments_v = (v_meta_tokens, v_ln1_g, v_w_in, v_q_a_norm_g, v_w_uq, v_kv_a_norm_g, v_w_ukv, v_q_norm_g, v_k_norm_g,
                 v_conv_w, v_conv_b, v_lru_wa, v_lru_ba, v_lru_wi, v_lru_bi, v_lru_lambda, v_attn_out_g, v_rnn_out_g,
                 v_w_out, v_ln2_g, v_w_gate, v_w_up, v_w_down)
    shapes = {k: a.shape for k, a in zip(WEIGHTS, given)}

    def two_d(a):
        return a.reshape(-1, a.shape[-1])

    w = {k: two_d(a) for k, a in zip(WEIGHTS, given)}
    m = {k: two_d(a) for k, a in zip(WEIGHTS, moments_m)}
    v = {k: two_d(a) for k, a in zip(WEIGHTS, moments_v)}

    loss_part, grad_x, parts = _local_step(x, loss_target, _MeshExchange(w))

    new = {k: _adamw("adamw_" + k, parts[k], w[k], m[k], v[k]) for k in BIG}
    small = [k for k in WEIGHTS if k not in BIG]
    new.update(zip(small, _adamw_many("adamw_small", [(parts[k], w[k], m[k], v[k]) for k in small])))

    loss = lax.psum(loss_part, ("x", "y", "c"))
    outs = [loss, grad_x]
    for idx in range(4):
        outs += [new[k][idx].reshape(shapes[k]) for k in WEIGHTS]
    return tuple(outs)
```

```python
import functools
import math

import jax
import jax.numpy as jnp
from jax import lax
from jax.experimental import pallas as pl
from jax.experimental.pallas import tpu as pltpu

F32 = jnp.float32
BF16 = jnp.bfloat16

D_MODEL = 1024
N_META = 16
SEQ = 2048
N_HEADS = 8
QK_NOPE = 64
QK_ROPE = 32
QK_HEAD = QK_NOPE + QK_ROPE
V_HEAD = 64
D_ATTN = N_HEADS * V_HEAD
Q_LORA = 384
KV_LORA = 256
D_RNN = 512
RNN_BW = 64
D_FF = 2816
EPS = 1e-6
LRU_C = 8.0
ROPE_THETA = 10000.0
OFF_CKV = Q_LORA + KV_LORA
OFF_KR = OFF_CKV + QK_ROPE
IN_COLS = OFF_KR + 2 * D_RNN

ADAM_LR = 0.001
ADAM_B1 = 0.9
ADAM_B2 = 0.999
ADAM_EPS = 1e-08
ADAM_WD = 0.01
ADAM_STEP = 10

N_DEV = 8
LANES = 128
HEAD_PAD = LANES
PAD_ROWS = LANES - N_META
QP_COLS = N_HEADS * HEAD_PAD
P_COLS = OFF_CKV + 2 * D_RNN + LANES
FF_CHUNK = D_FF // 2
VMEM_LIMIT = 56 * 1024 * 1024
MESH = pl.DeviceIdType.MESH


def _t_pad():
    return PAD_ROWS + N_META + SEQ


def _row_tile(n):
    return 256 if n % 256 == 0 else 128


def _const_spec(shape):
    nd = len(shape)
    return pl.BlockSpec(shape, lambda *_: (0,) * nd, pipeline_mode=pl.Buffered(1))


def _rms(x, d):
    r = lax.rsqrt(jnp.sum(x * x, axis=-1, keepdims=True) * (1.0 / d) + EPS)
    return x * r, r


def _rms_bwd(dy, xhat, r, g, d):
    dxh = dy * g
    return r * (dxh - xhat * (jnp.sum(dxh * xhat, axis=-1, keepdims=True) * (1.0 / d)))


def _colsum(x):
    return jnp.sum(x, axis=0, keepdims=True)


def _dot(a, b):
    return jnp.dot(a, b, preferred_element_type=F32)


def _dot_nt(a, b):
    return lax.dot_general(a, b, (((1,), (1,)), ((), ())), preferred_element_type=F32)


def _dot_tn(a, b):
    return lax.dot_general(a, b, (((0,), (0,)), ((), ())), preferred_element_type=F32)


def _rope(x, c, s1, s2):
    return x * c + pltpu.roll(x, 16, 1) * s1 + pltpu.roll(x, HEAD_PAD - 16, 1) * s2


def _rope_bwd(dy, c, s1, s2):
    return dy * c + pltpu.roll(dy * s1, HEAD_PAD - 16, 1) + pltpu.roll(dy * s2, 16, 1)


def _acc(ref, first, val):
    @pl.when(first)
    def _():
        ref[...] = val

    @pl.when(jnp.logical_not(first))
    def _():
        ref[...] += val


def _in_proj(h0, ln1_g, w_in_p, srcs=(), scatter=()):
    n = h0.shape[0]
    tm = _row_tile(n)
    nk = len(srcs)
    c_in, c_out, c_shape, c_sems = _exchange_specs(srcs, scatter)

    def body(h_ref, g_ref, w_ref, *rest):
        hn_ref, cq_ref, ckv_ref, xr_ref, xg_ref, kr_ref = rest[nk:nk + 6]
        finish = _ride(1, *_exchange_fns(rest[:nk], rest[nk + 6:2 * nk + 6], rest[2 * nk + 6:], scatter))
        xhat, _ = _rms(h_ref[...], D_MODEL)
        hn = (xhat * g_ref[...]).astype(BF16)
        hn_ref[...] = hn
        p = _dot(hn, w_ref[...])
        cq_ref[...] = p[:, :Q_LORA]
        ckv_ref[...] = p[:, Q_LORA:OFF_CKV]
        xr_ref[...] = p[:, OFF_CKV:OFF_CKV + D_RNN]
        xg_ref[...] = p[:, OFF_CKV + D_RNN:OFF_CKV + 2 * D_RNN]
        kr_ref[...] = p[:, OFF_CKV + 2 * D_RNN:]
        finish()

    def row(w):
        return pl.BlockSpec((tm, w), lambda i: (i, 0))

    widths = (D_MODEL, Q_LORA, KV_LORA, D_RNN, D_RNN, LANES)
    res = pl.pallas_call(
        body, name="in_proj", grid=(n // tm,),
        in_specs=[row(D_MODEL), _const_spec((1, D_MODEL)), _const_spec((D_MODEL, P_COLS))] + c_in,
        out_specs=[row(w) for w in widths] + c_out,
        out_shape=[jax.ShapeDtypeStruct((n, w), BF16 if k == 0 else F32) for k, w in enumerate(widths)] + c_shape,
        scratch_shapes=c_sems,
        compiler_params=pltpu.CompilerParams(dimension_semantics=("arbitrary",), vmem_limit_bytes=VMEM_LIMIT),
    )(h0, ln1_g, w_in_p, *srcs)
    return res[:6], res[6:]


def _qkv_fwd(cq, ckv, kr, gqa, gkva, w_uq_p, w_uk_p, w_v, qg, kg, rc, rs1, rs2):
    n = cq.shape[0]
    tm = _row_tile(n)

    def body(cq_ref, ckv_ref, kr_ref, gqa_ref, gkva_ref, wuq_ref, wuk_ref, wv_ref, qg_ref, kg_ref,
             c_ref, s1_ref, s2_ref, q_ref, k_ref, v_ref):
        xq, _ = _rms(cq_ref[...], Q_LORA)
        qa = (xq * gqa_ref[...]).astype(BF16)
        q = _dot(qa, wuq_ref[...])
        xkv, _ = _rms(ckv_ref[...], KV_LORA)
        kva = (xkv * gkva_ref[...]).astype(BF16)
        kn = _dot(kva, wuk_ref[...])
        v_ref[...] = _dot(kva, wv_ref[...]).astype(BF16)
        krp = kr_ref[...]
        c, s1, s2 = c_ref[...], s1_ref[...], s2_ref[...]
        for h in range(N_HEADS):
            sl = slice(h * HEAD_PAD, (h + 1) * HEAD_PAD)
            qh, _ = _rms(q[:, sl], QK_HEAD)
            q_ref[:, sl] = _rope(qh * qg_ref[...], c, s1, s2).astype(BF16)
            kh, _ = _rms(kn[:, sl] + krp, QK_HEAD)
            k_ref[:, sl] = _rope(kh * kg_ref[...], c, s1, s2).astype(BF16)

    def row(w):
        return pl.BlockSpec((tm, w), lambda i: (i, 0))

    return pl.pallas_call(
        body, name="qkv_fwd", grid=(n // tm,),
        in_specs=[row(Q_LORA), row(KV_LORA), row(LANES), _const_spec((1, Q_LORA)), _const_spec((1, KV_LORA)),
                  _const_spec((Q_LORA, QP_COLS)), _const_spec((KV_LORA, QP_COLS)), _const_spec((KV_LORA, D_ATTN)),
                  _const_spec((1, LANES)), _const_spec((1, LANES)), row(LANES), row(LANES), row(LANES)],
        out_specs=[row(QP_COLS), row(QP_COLS), row(D_ATTN)],
        out_shape=[jax.ShapeDtypeStruct((n, QP_COLS), BF16), jax.ShapeDtypeStruct((n, QP_COLS), BF16),
                   jax.ShapeDtypeStruct((n, D_ATTN), BF16)],
        compiler_params=pltpu.CompilerParams(dimension_semantics=("parallel",), vmem_limit_bytes=VMEM_LIMIT),
    )(cq, ckv, kr, gqa, gkva, w_uq_p, w_uk_p, w_v, qg, kg, rc, rs1, rs2)


def _qkv_bwd(cq, ckv, kr, dq_r, dk_r, dv, dxr, dxg, gqa, gkva, w_uq_p, w_uk_p, w_v, qg, kg, rc, rs1, rs2):
    n = cq.shape[0]
    tm = _row_tile(n)

    def body(cq_ref, ckv_ref, kr_ref, dq_ref, dk_ref, dv_ref, dxr_ref, dxg_ref, gqa_ref, gkva_ref, wuq_ref, wuk_ref,
             wv_ref, qg_ref, kg_ref, c_ref, s1_ref, s2_ref,
             dp_ref, qa_ref, kva_ref, dqp_ref, dkv_ref, dqg_ref, dkg_ref, dgqa_ref, dgkva_ref):
        first = pl.program_id(0) == 0
        dp_ref[:, OFF_CKV:OFF_CKV + D_RNN] = dxr_ref[...].astype(BF16)
        dp_ref[:, OFF_CKV + D_RNN:OFF_CKV + 2 * D_RNN] = dxg_ref[...].astype(BF16)
        xq, rq = _rms(cq_ref[...], Q_LORA)
        qa = (xq * gqa_ref[...]).astype(BF16)
        qa_ref[...] = qa
        q = _dot(qa, wuq_ref[...])
        xkv, rkv = _rms(ckv_ref[...], KV_LORA)
        kva = (xkv * gkva_ref[...]).astype(BF16)
        kva_ref[...] = kva
        kn = _dot(kva, wuk_ref[...])
        krp = kr_ref[...]
        c, s1, s2 = c_ref[...], s1_ref[...], s2_ref[...]
        lane = lax.broadcasted_iota(jnp.int32, (tm, HEAD_PAD), 1)
        rope_lanes = jnp.logical_and(lane >= QK_NOPE, lane < QK_HEAD)
        dqg = jnp.zeros((1, HEAD_PAD), F32)
        dkg = jnp.zeros((1, HEAD_PAD), F32)
        dkr = jnp.zeros((tm, HEAD_PAD), F32)
        for h in range(N_HEADS):
            sl = slice(h * HEAD_PAD, (h + 1) * HEAD_PAD)
            qh, rqh = _rms(q[:, sl], QK_HEAD)
            dy = _rope_bwd(dq_ref[:, sl], c, s1, s2)
            dqg = dqg + _colsum(dy * qh)
            dqp_ref[:, sl] = _rms_bwd(dy, qh, rqh, qg_ref[...], QK_HEAD).astype(BF16)
            kh, rkh = _rms(kn[:, sl] + krp, QK_HEAD)
            dyk = _rope_bwd(dk_ref[:, sl], c, s1, s2)
            dkg = dkg + _colsum(dyk * kh)
            dkh = _rms_bwd(dyk, kh, rkh, kg_ref[...], QK_HEAD)
            dkv_ref[:, sl] = dkh.astype(BF16)
            dkr = dkr + jnp.where(rope_lanes, dkh, 0.0)
        dkv_ref[:, QP_COLS:] = dv_ref[...].astype(BF16)
        dp_ref[:, OFF_CKV + 2 * D_RNN:] = dkr.astype(BF16)
        dqa = _dot_nt(dqp_ref[...], wuq_ref[...])
        dp_ref[:, :Q_LORA] = _rms_bwd(dqa, xq, rq, gqa_ref[...], Q_LORA).astype(BF16)
        dkva = _dot_nt(dkv_ref[:, :QP_COLS], wuk_ref[...]) + _dot_nt(dkv_ref[:, QP_COLS:], wv_ref[...])
        dp_ref[:, Q_LORA:OFF_CKV] = _rms_bwd(dkva, xkv, rkv, gkva_ref[...], KV_LORA).astype(BF16)
        _acc(dqg_ref, first, dqg)
        _acc(dkg_ref, first, dkg)
        _acc(dgqa_ref, first, _colsum(dqa * xq))
        _acc(dgkva_ref, first, _colsum(dkva * xkv))

    def row(w):
        return pl.BlockSpec((tm, w), lambda i: (i, 0))

    def acc(w):
        return pl.BlockSpec((1, w), lambda i: (0, 0))

    return pl.pallas_call(
        body, name="qkv_bwd", grid=(n // tm,),
        in_specs=[row(Q_LORA), row(KV_LORA), row(LANES), row(QP_COLS), row(QP_COLS), row(D_ATTN), row(D_RNN), row(D_RNN),
                  _const_spec((1, Q_LORA)), _const_spec((1, KV_LORA)),
                  _const_spec((Q_LORA, QP_COLS)), _const_spec((KV_LORA, QP_COLS)), _const_spec((KV_LORA, D_ATTN)),
                  _const_spec((1, LANES)), _const_spec((1, LANES)), row(LANES), row(LANES), row(LANES)],
        out_specs=[row(P_COLS), row(Q_LORA), row(KV_LORA), row(QP_COLS),
                   row(QP_COLS + D_ATTN), acc(LANES), acc(LANES), acc(Q_LORA), acc(KV_LORA)],
        out_shape=[jax.ShapeDtypeStruct((n, P_COLS), BF16), jax.ShapeDtypeStruct((n, Q_LORA), BF16),
                   jax.ShapeDtypeStruct((n, KV_LORA), BF16), jax.ShapeDtypeStruct((n, QP_COLS), BF16),
                   jax.ShapeDtypeStruct((n, QP_COLS + D_ATTN), BF16),
                   jax.ShapeDtypeStruct((1, LANES), F32), jax.ShapeDtypeStruct((1, LANES), F32),
                   jax.ShapeDtypeStruct((1, Q_LORA), F32), jax.ShapeDtypeStruct((1, KV_LORA), F32)],
        compiler_params=pltpu.CompilerParams(dimension_semantics=("arbitrary",), vmem_limit_bytes=VMEM_LIMIT),
    )(cq, ckv, kr, dq_r, dk_r, dv, dxr, dxg, gqa, gkva, w_uq_p, w_uk_p, w_v, qg, kg, rc, rs1, rs2)


def _softmax_parts(qh, kh, tq, t):
    s = _dot_nt(qh, kh) * (QK_HEAD ** -0.5)
    key = lax.broadcasted_iota(jnp.int32, (tq, t), 1)
    s = jnp.where(key >= PAD_ROWS, s, -jnp.inf)
    e = jnp.exp(s - jnp.max(s, axis=-1, keepdims=True))
    return e, jnp.sum(e, axis=-1, keepdims=True)


def _attn_specs(t, tq):
    nq = t // tq
    qspec = pl.BlockSpec((tq, 2 * HEAD_PAD), lambda b, hp, i: (b * nq + i, hp))
    kspec = pl.BlockSpec((t, 2 * HEAD_PAD), lambda b, hp, i: (b, hp))
    vspec = pl.BlockSpec((t, 2 * V_HEAD), lambda b, hp, i: (b, hp))
    ospec = pl.BlockSpec((tq, 2 * V_HEAD), lambda b, hp, i: (b * nq + i, hp))
    return nq, qspec, kspec, vspec, ospec


def _attn_fwd(q, k, v, srcs=(), scatter=()):
    n = q.shape[0]
    t = _t_pad()
    tq = t // 8
    nq, qspec, kspec, vspec, ospec = _attn_specs(t, tq)
    nk = len(srcs)
    c_in, c_out, c_shape, c_sems = _exchange_specs(srcs, scatter)

    def body(q_ref, k_ref, v_ref, *rest):
        o_ref = rest[nk]
        finish = _ride(3, *_exchange_fns(rest[:nk], rest[nk + 1:2 * nk + 1], rest[2 * nk + 1:], scatter))
        lane = lax.broadcasted_iota(jnp.int32, (tq, 2 * V_HEAD), 1)
        outs = []
        for j in range(2):
            sl = slice(j * HEAD_PAD, (j + 1) * HEAD_PAD)
            e, l = _softmax_parts(q_ref[:, sl], k_ref[:, sl], tq, t)
            outs.append(_dot(e.astype(BF16), v_ref[...]) / l)
        o_ref[...] = jnp.where(lane < V_HEAD, outs[0], outs[1])
        finish()

    res = pl.pallas_call(
        body, name="attn_fwd", grid=(n // t, N_HEADS // 2, nq),
        in_specs=[qspec, kspec, vspec] + c_in, out_specs=[ospec] + c_out,
        out_shape=[jax.ShapeDtypeStruct((n, D_ATTN), F32)] + c_shape, scratch_shapes=c_sems,
        compiler_params=pltpu.CompilerParams(dimension_semantics=("arbitrary", "arbitrary", "arbitrary"),
                                             vmem_limit_bytes=VMEM_LIMIT),
    )(q, k, v, *srcs)
    return res[0], res[1:]


def _attn_bwd(q, k, v, do, srcs=(), scatter=()):
    n = q.shape[0]
    t = _t_pad()
    tq = t // 4
    nq, qspec, kspec, vspec, ospec = _attn_specs(t, tq)
    nk = len(srcs)
    c_in, c_out, c_shape, c_sems = _exchange_specs(srcs, scatter)

    def body(q_ref, k_ref, v_ref, do_ref, *rest):
        dq_ref, dk_ref, dv_ref = rest[nk:nk + 3]
        finish = _ride(3, *_exchange_fns(rest[:nk], rest[nk + 3:2 * nk + 3], rest[2 * nk + 3:], scatter))
        first = pl.program_id(2) == 0
        lane = lax.broadcasted_iota(jnp.int32, (tq, 2 * V_HEAD), 1)
        do = do_ref[...]
        dv = jnp.zeros((t, 2 * V_HEAD), F32)
        for j in range(2):
            sl = slice(j * HEAD_PAD, (j + 1) * HEAD_PAD)
            qh, kh = q_ref[:, sl], k_ref[:, sl]
            e, l = _softmax_parts(qh, kh, tq, t)
            p = e / l
            in_head = (lane < V_HEAD) if j == 0 else (lane >= V_HEAD)
            doh = jnp.where(in_head, do, 0.0).astype(BF16)
            dp = _dot_nt(doh, v_ref[...])
            delta = jnp.sum(p * dp, axis=-1, keepdims=True)
            ds = (p * (dp - delta) * (QK_HEAD ** -0.5)).astype(BF16)
            dq_ref[:, sl] = _dot(ds, kh)
            dkh = _dot_tn(ds, qh)

            @pl.when(first)
            def _():
                dk_ref[:, sl] = dkh

            @pl.when(jnp.logical_not(first))
            def _():
                dk_ref[:, sl] += dkh

            dv = dv + _dot_tn(p.astype(BF16), doh)
        _acc(dv_ref, first, dv)
        finish()

    res = pl.pallas_call(
        body, name="attn_bwd", grid=(n // t, N_HEADS // 2, nq),
        in_specs=[qspec, kspec, vspec, ospec] + c_in, out_specs=[qspec, kspec, vspec] + c_out,
        out_shape=[jax.ShapeDtypeStruct((n, QP_COLS), F32), jax.ShapeDtypeStruct((n, QP_COLS), F32),
                   jax.ShapeDtypeStruct((n, D_ATTN), F32)] + c_shape, scratch_shapes=c_sems,
        compiler_params=pltpu.CompilerParams(dimension_semantics=("arbitrary", "arbitrary", "arbitrary"),
                                             vmem_limit_bytes=VMEM_LIMIT),
    )(q, k, v, do, *srcs)
    return res[:3], res[3:]


SCAN_STEPS = 8


def _scan(chains, t):
    seg = t // 8
    rows = lax.broadcasted_iota(jnp.int32, (8, LANES), 0)

    def step(i, carry):
        carry = list(carry)
        for u in range(SCAN_STEPS):
            j = i * SCAN_STEPS + u
            for n, (a_ref, b_ref, h_ref, p_ref, reverse) in enumerate(chains):
                h, p = carry[n]
                idx = pl.ds(seg - 1 - j if reverse else j, 8, stride=seg)
                a = a_ref[idx, :]
                h = a * h + b_ref[idx, :]
                p = a * p
                h_ref[idx, :] = h
                p_ref[idx, :] = p
                carry[n] = (h, p)
        return tuple(carry)

    init = tuple((jnp.zeros((8, LANES), F32), jnp.ones((8, LANES), F32)) for _ in chains)
    ends = lax.fori_loop(0, seg // SCAN_STEPS, step, init)
    for (_, _, h_ref, p_ref, reverse), (b, a) in zip(chains, ends):
        for d in (1, 2, 4):
            if reverse:
                keep = rows < 8 - d
                a_n, b_n = pltpu.roll(a, 8 - d, 0), pltpu.roll(b, 8 - d, 0)
            else:
                keep = rows >= d
                a_n, b_n = pltpu.roll(a, d, 0), pltpu.roll(b, d, 0)
            b = a * jnp.where(keep, b_n, 0.0) + b
            a = a * jnp.where(keep, a_n, 1.0)
        for s in (range(7) if reverse else range(1, 8)):
            sl = slice(s * seg, (s + 1) * seg)
            carry_in = b[s + 1:s + 2, :] if reverse else b[s - 1:s, :]
            h_ref[sl, :] = h_ref[sl, :] + p_ref[sl, :] * carry_in


def _shift_rows(x, s, rows, t):
    if s == 0:
        return x
    rolled = pltpu.roll(x, s % t, 0)
    return jnp.where(rows >= s, rolled, 0.0) if s > 0 else jnp.where(rows < t + s, rolled, 0.0)


def _neg_expm1(x):
    series = -x * (1.0 + x * (0.5 + x * (1.0 / 6 + x * (1.0 / 24 + x * (1.0 / 120 + x * (1.0 / 720))))))
    return jnp.where(x > -0.3, series, 1.0 - jnp.exp(x))


def _gelu_parts(x):
    k = math.sqrt(2.0 / math.pi)
    th = jnp.tanh(k * (x + 0.044715 * x * x * x))
    g = 0.5 * x * (1.0 + th)
    dg = 0.5 * (1.0 + th) + 0.5 * x * (1.0 - th * th) * k * (1.0 + 3 * 0.044715 * x * x)
    return g, dg


def _lru_gates(xc, gates, lam_ref, valid, d):
    r = jax.nn.sigmoid(gates[:, (2 * d) * LANES:(2 * d + 1) * LANES])
    i = jax.nn.sigmoid(gates[:, (2 * d + 1) * LANES:(2 * d + 2) * LANES])
    neg_lam = -lam_ref[d:d + 1, :]
    sp = jnp.maximum(neg_lam, 0.0) + jnp.log1p(jnp.exp(-jnp.abs(neg_lam)))
    log_a = -LRU_C * r * sp
    a = jnp.exp(log_a)
    m = jnp.maximum(_neg_expm1(2.0 * log_a), 0.0)
    sq = jnp.sqrt(m)
    b = jnp.where(valid, sq * (i * xc), 0.0)
    return r, i, sp, a, m, sq, b


def _conv(xr, cw_ref, cb_ref, rows, t):
    return (cw_ref[0:1, :] * _shift_rows(xr, 2, rows, t) + cw_ref[1:2, :] * _shift_rows(xr, 1, rows, t)
            + cw_ref[2:3, :] * xr + cw_ref[3:4, :] * _shift_rows(xr, -1, rows, t) + cb_ref[...])


def _rnn_specs(t):
    seq = pl.BlockSpec((t, LANES), lambda cb, b: (b, cb))
    cw = pl.BlockSpec((4, LANES), lambda cb, b: (0, cb))
    vec1 = pl.BlockSpec((1, LANES), lambda cb, b: (0, cb))
    vec2 = pl.BlockSpec((2, LANES), lambda cb, b: (0, cb))
    wblk = pl.BlockSpec((1, LANES, 4 * LANES), lambda cb, b: (cb, 0, 0))
    gbias = pl.BlockSpec((1, 1, 4 * LANES), lambda cb, b: (cb, 0, 0))
    return seq, cw, vec1, vec2, wblk, gbias


def _rnn_fwd(xr, xg, conv_w, conv_b, wblk, gbias, lam):
    n = xr.shape[0]
    t = _t_pad()
    seq, cw, vec1, vec2, wspec, gspec = _rnn_specs(t)

    def body(xr_ref, xg_ref, cw_ref, cb_ref, w_ref, gb_ref, lam_ref, o_ref, a_s, b_s, h_s, p_s):
        rows = lax.broadcasted_iota(jnp.int32, (t, LANES), 0)
        valid = rows >= PAD_ROWS
        xc = _conv(xr_ref[...], cw_ref, cb_ref, rows, t)
        gates = _dot(xc.astype(BF16), w_ref[0]) + gb_ref[0]
        for d in range(2):
            _, _, _, a, _, _, b = _lru_gates(xc, gates, lam_ref, valid, d)
            a_s[d] = a
            b_s[d] = b
        _scan([(a_s.at[d], b_s.at[d], h_s.at[d], p_s.at[d], d == 1) for d in range(2)], t)
        g, _ = _gelu_parts(xg_ref[...])
        o_ref[...] = (h_s[0] + h_s[1]) * g

    return pl.pallas_call(
        body, name="rnn_fwd", grid=(D_RNN // LANES, n // t),
        in_specs=[seq, seq, cw, vec1, wspec, gspec, vec2], out_specs=seq,
        out_shape=jax.ShapeDtypeStruct((n, D_RNN), F32),
        scratch_shapes=[pltpu.VMEM((2, t, LANES), F32)] * 4,
        compiler_params=pltpu.CompilerParams(dimension_semantics=("parallel", "parallel"), vmem_limit_bytes=VMEM_LIMIT),
    )(xr, xg, conv_w, conv_b, wblk, gbias, lam)


def _rnn_bwd(xr, xg, do, conv_w, conv_b, wblk, gbias, lam):
    n = xr.shape[0]
    t = _t_pad()
    seq, cw, vec1, vec2, wspec, gspec = _rnn_specs(t)

    def body(xr_ref, xg_ref, do_ref, cw_ref, cb_ref, w_ref, gb_ref, lam_ref,
             dxr_ref, dxg_ref, dcw_ref, dcb_ref, dw_ref, dgb_ref, dlam_ref, a_s, b_s, h_s, l_s, p_s, dg_s):
        first = pl.program_id(1) == 0
        rows = lax.broadcasted_iota(jnp.int32, (t, LANES), 0)
        valid = rows >= PAD_ROWS
        xr = xr_ref[...]
        xc = _conv(xr, cw_ref, cb_ref, rows, t)
        xcb = xc.astype(BF16)
        gates = _dot(xcb, w_ref[0]) + gb_ref[0]
        for d in range(2):
            _, _, _, a, _, _, b = _lru_gates(xc, gates, lam_ref, valid, d)
            a_s[d] = a
            b_s[d] = b
        _scan([(a_s.at[d], b_s.at[d], h_s.at[d], p_s.at[d], d == 1) for d in range(2)], t)
        g, dg = _gelu_parts(xg_ref[...])
        do = do_ref[...]
        dxg_ref[...] = do * (h_s[0] + h_s[1]) * dg
        b_s[0] = do * g
        for d in range(2):
            a_s[d] = _shift_rows(a_s[d], -1 if d == 0 else 1, rows, t)
        _scan([(a_s.at[d], b_s.at[0], l_s.at[d], p_s.at[d], d == 0) for d in range(2)], t)
        dxc = jnp.zeros((t, LANES), F32)
        dlams = []
        for d in range(2):
            r, i, sp, a, m, sq, _ = _lru_gates(xc, gates, lam_ref, valid, d)
            lam_t = l_s[d]
            da = lam_t * _shift_rows(h_s[d], 1 if d == 0 else -1, rows, t)
            lam_v = jnp.where(valid, lam_t, 0.0)
            dsq = lam_v * (i * xc)
            di = lam_v * sq * xc
            dxc = dxc + lam_v * sq * i
            dm = jnp.where(m > 0.0, dsq * 0.5 / jnp.where(m > 0.0, sq, 1.0), 0.0)
            dla = da * a - 2.0 * dm * a * a
            dr = dla * (-LRU_C) * sp
            dsp = _colsum(dla * (-LRU_C) * r)
            dlams.append(dsp * -jax.nn.sigmoid(-lam_ref[d:d + 1, :]))
            dg_s[:, (2 * d) * LANES:(2 * d + 1) * LANES] = (dr * r * (1.0 - r)).astype(BF16)
            dg_s[:, (2 * d + 1) * LANES:(2 * d + 2) * LANES] = (di * i * (1.0 - i)).astype(BF16)
        dgates = dg_s[...]
        dxc = dxc + _dot_nt(dgates, w_ref[0])
        dxr_ref[...] = (cw_ref[0:1, :] * _shift_rows(dxc, -2, rows, t) + cw_ref[1:2, :] * _shift_rows(dxc, -1, rows, t)
                        + cw_ref[2:3, :] * dxc + cw_ref[3:4, :] * _shift_rows(dxc, 1, rows, t))
        dcw = jnp.concatenate([_colsum(dxc * _shift_rows(xr, 2 - j, rows, t)) for j in range(4)], axis=0)
        _acc(dcw_ref, first, dcw)
        _acc(dcb_ref, first, _colsum(dxc))
        _acc(dw_ref, first, _dot_tn(xcb, dgates)[None])
        _acc(dgb_ref, first, _colsum(dgates.astype(F32))[None])
        _acc(dlam_ref, first, jnp.concatenate(dlams, axis=0))

    return pl.pallas_call(
        body, name="rnn_bwd", grid=(D_RNN // LANES, n // t),
        in_specs=[seq, seq, seq, cw, vec1, wspec, gspec, vec2],
        out_specs=[seq, seq, cw, vec1, wspec, gspec, vec2],
        out_shape=[jax.ShapeDtypeStruct((n, D_RNN), F32), jax.ShapeDtypeStruct((n, D_RNN), F32),
                   jax.ShapeDtypeStruct((4, D_RNN), F32), jax.ShapeDtypeStruct((1, D_RNN), F32),
                   jax.ShapeDtypeStruct((D_RNN // LANES, LANES, 4 * LANES), F32),
                   jax.ShapeDtypeStruct((D_RNN // LANES, 1, 4 * LANES), F32), jax.ShapeDtypeStruct((2, D_RNN), F32)],
        scratch_shapes=[pltpu.VMEM((2, t, LANES), F32)] * 5 + [pltpu.VMEM((t, 4 * LANES), BF16)],
        compiler_params=pltpu.CompilerParams(dimension_semantics=("parallel", "arbitrary"), vmem_limit_bytes=VMEM_LIMIT),
    )(xr, xg, do, conv_w, conv_b, wblk, gbias, lam)


def _post(oa, orn, h0, tgt, ga, gr, g2, w_out, w_gate, w_up, w_down):
    n = oa.shape[0]
    tm = _row_tile(n)
    t = _t_pad()

    def body(oa_ref, or_ref, h0_ref, tgt_ref, ga_ref, gr_ref, g2_ref, wo_ref, wg_ref, wu_ref, wd_ref,
             doa_ref, dor_ref, dh1_ref, mix_ref, h1n_ref, act_ref, dgate_ref, dup_ref, dy_ref,
             loss_ref, dga_ref, dgr_ref, dg2_ref, gate_s, up_s):
        first = pl.program_id(0) == 0
        xa, ra = _rms(oa_ref[...], D_ATTN)
        xr, rr = _rms(or_ref[...], D_RNN)
        mix_ref[:, :D_ATTN] = (xa * ga_ref[...]).astype(BF16)
        mix_ref[:, D_ATTN:] = (xr * gr_ref[...]).astype(BF16)
        h1 = h0_ref[...] + _dot(mix_ref[...], wo_ref[...])
        x2, r2 = _rms(h1, D_MODEL)
        h1n = (x2 * g2_ref[...]).astype(BF16)
        h1n_ref[...] = h1n
        y = h1
        for cs in range(0, D_FF, FF_CHUNK):
            sl = slice(cs, cs + FF_CHUNK)
            gate = _dot(h1n, wg_ref[:, sl])
            up = _dot(h1n, wu_ref[:, sl])
            gate_s[:, sl] = gate
            up_s[:, sl] = up
            act = (gate * jax.nn.sigmoid(gate) * up).astype(BF16)
            act_ref[:, sl] = act
            y = y + _dot(act, wd_ref[sl, :])
        row = pl.program_id(0) * tm + lax.broadcasted_iota(jnp.int32, (tm, 1), 0)
        for _ in range(1, n // t):
            row = jnp.where(row >= t, row - t, row)
        err = jnp.where(row >= PAD_ROWS + N_META, y - tgt_ref[...], 0.0)
        _acc(loss_ref, first, jnp.full((1, LANES), 0.5 / D_MODEL, F32) * jnp.sum(err * err))
        dy = err * (1.0 / D_MODEL)
        dyb = dy.astype(BF16)
        dy_ref[...] = dyb
        dh1n = jnp.zeros((tm, D_MODEL), F32)
        for cs in range(0, D_FF, FF_CHUNK):
            sl = slice(cs, cs + FF_CHUNK)
            dact = _dot_nt(dyb, wd_ref[sl, :])
            gate, up = gate_s[:, sl], up_s[:, sl]
            sg = jax.nn.sigmoid(gate)
            dgate = (dact * up * sg * (1.0 + gate * (1.0 - sg))).astype(BF16)
            dup = (dact * gate * sg).astype(BF16)
            dgate_ref[:, sl] = dgate
            dup_ref[:, sl] = dup
            dh1n = dh1n + _dot_nt(dgate, wg_ref[:, sl]) + _dot_nt(dup, wu_ref[:, sl])
        _acc(dg2_ref, first, _colsum(dh1n * x2))
        dh1 = dy + _rms_bwd(dh1n, x2, r2, g2_ref[...], D_MODEL)
        dh1_ref[...] = dh1
        dmix = _dot_nt(dh1.astype(BF16), wo_ref[...])
        dma, dmr = dmix[:, :D_ATTN], dmix[:, D_ATTN:]
        _acc(dga_ref, first, _colsum(dma * xa))
        _acc(dgr_ref, first, _colsum(dmr * xr))
        doa_ref[...] = _rms_bwd(dma, xa, ra, ga_ref[...], D_ATTN)
        dor_ref[...] = _rms_bwd(dmr, xr, rr, gr_ref[...], D_RNN)

    def row(w):
        return pl.BlockSpec((tm, w), lambda i: (i, 0))

    def acc(w):
        return pl.BlockSpec((1, w), lambda i: (0, 0))

    outs = [(D_ATTN, F32), (D_RNN, F32), (D_MODEL, F32), (D_MODEL, BF16), (D_MODEL, BF16), (D_FF, BF16),
            (D_FF, BF16), (D_FF, BF16), (D_MODEL, BF16)]
    accs = [LANES, D_ATTN, D_RNN, D_MODEL]
    return pl.pallas_call(
        body, name="post", grid=(n // tm,),
        in_specs=[row(D_ATTN), row(D_RNN), row(D_MODEL), row(D_MODEL),
                  _const_spec((1, D_ATTN)), _const_spec((1, D_RNN)), _const_spec((1, D_MODEL)),
                  _const_spec((D_MODEL, D_MODEL)), _const_spec((D_MODEL, D_FF)), _const_spec((D_MODEL, D_FF)),
                  _const_spec((D_FF, D_MODEL))],
        out_specs=[row(w) for w, _ in outs] + [acc(w) for w in accs],
        out_shape=[jax.ShapeDtypeStruct((n, w), dt) for w, dt in outs]
        + [jax.ShapeDtypeStruct((1, w), F32) for w in accs],
        scratch_shapes=[pltpu.VMEM((tm, D_FF), F32), pltpu.VMEM((tm, D_FF), F32)],
        compiler_params=pltpu.CompilerParams(dimension_semantics=("arbitrary",), vmem_limit_bytes=VMEM_LIMIT),
    )(oa, orn, h0, tgt, ga, gr, g2, w_out, w_gate, w_up, w_down)


def _in_bwd(dp, h0, dh1, ln1_g, w_in_p, srcs=(), scatter=()):
    n = h0.shape[0]
    tm = _row_tile(n)
    nk = len(srcs)
    c_in, c_out, c_shape, c_sems = _exchange_specs(srcs, scatter)

    def body(dp_ref, h0_ref, dh1_ref, g_ref, w_ref, *rest):
        dh0_ref, dg_ref = rest[nk:nk + 2]
        finish = _ride(1, *_exchange_fns(rest[:nk], rest[nk + 2:2 * nk + 2], rest[2 * nk + 2:], scatter))
        dhn = _dot_nt(dp_ref[...], w_ref[...])
        xhat, r = _rms(h0_ref[...], D_MODEL)
        _acc(dg_ref, pl.program_id(0) == 0, _colsum(dhn * xhat))
        dh0_ref[...] = dh1_ref[...] + _rms_bwd(dhn, xhat, r, g_ref[...], D_MODEL)
        finish()

    def row(w):
        return pl.BlockSpec((tm, w), lambda i: (i, 0))

    res = pl.pallas_call(
        body, name="in_bwd", grid=(n // tm,),
        in_specs=[row(P_COLS), row(D_MODEL), row(D_MODEL), _const_spec((1, D_MODEL)), _const_spec((D_MODEL, P_COLS))] + c_in,
        out_specs=[row(D_MODEL), pl.BlockSpec((1, D_MODEL), lambda i: (0, 0))] + c_out,
        out_shape=[jax.ShapeDtypeStruct((n, D_MODEL), F32), jax.ShapeDtypeStruct((1, D_MODEL), F32)] + c_shape,
        scratch_shapes=c_sems,
        compiler_params=pltpu.CompilerParams(dimension_semantics=("arbitrary",), vmem_limit_bytes=VMEM_LIMIT),
    )(dp, h0, dh1, ln1_g, w_in_p, *srcs)
    return res[:2], res[2:]


def _pick_tile(width, cap):
    best = LANES
    for mult in range(1, width // LANES + 1):
        cand = mult * LANES
        if width % cand == 0 and cand <= cap:
            best = cand
    return best


def _matmul_tn(name, a, b):
    n, ka = a.shape
    kb = b.shape[1]
    ta, tb = _pick_tile(ka, 1408), _pick_tile(kb, 1408)
    tk = n // 4

    def body(a_ref, b_ref, o_ref):
        _acc(o_ref, pl.program_id(2) == 0, _dot_tn(a_ref[...].astype(BF16), b_ref[...].astype(BF16)))

    return pl.pallas_call(
        body, name=name, grid=(ka // ta, kb // tb, n // tk),
        in_specs=[pl.BlockSpec((tk, ta), lambda i, j, k: (k, i)), pl.BlockSpec((tk, tb), lambda i, j, k: (k, j))],
        out_specs=pl.BlockSpec((ta, tb), lambda i, j, k: (i, j)),
        out_shape=jax.ShapeDtypeStruct((ka, kb), F32),
        compiler_params=pltpu.CompilerParams(dimension_semantics=("parallel", "parallel", "arbitrary"),
                                             vmem_limit_bytes=VMEM_LIMIT),
    )(a, b)


def _matmul_tn_shards(name, a, b, row_sharded):
    n, ka = a.shape
    kb = b.shape[1]
    ta, tb = _pick_tile(ka, 1408), _pick_tile(kb, 1408)
    tk = n // 4
    if row_sharded:
        width = ka // N_DEV
        per = ta // width
        out_shape, out_block = (N_DEV, width, kb), (per, width, tb)
        out_map = lambda i, j, k: (i, 0, j)
    else:
        width = kb // N_DEV
        per = tb // width
        out_shape, out_block = (N_DEV, ka, width), (per, ta, width)
        out_map = lambda i, j, k: (j, i, 0)

    def body(a_ref, b_ref, o_ref, acc_ref):
        _acc(acc_ref, pl.program_id(2) == 0, _dot_tn(a_ref[...].astype(BF16), b_ref[...].astype(BF16)))

        @pl.when(pl.program_id(2) == pl.num_programs(2) - 1)
        def _():
            for s in range(per):
                sl = slice(s * width, (s + 1) * width)
                o_ref[s] = (acc_ref[sl, :] if row_sharded else acc_ref[:, sl]).astype(BF16)

    return pl.pallas_call(
        body, name=name, grid=(ka // ta, kb // tb, n // tk),
        in_specs=[pl.BlockSpec((tk, ta), lambda i, j, k: (k, i)), pl.BlockSpec((tk, tb), lambda i, j, k: (k, j))],
        out_specs=pl.BlockSpec(out_block, out_map), out_shape=jax.ShapeDtypeStruct(out_shape, BF16),
        scratch_shapes=[pltpu.VMEM((ta, tb), F32)],
        compiler_params=pltpu.CompilerParams(dimension_semantics=("parallel", "parallel", "arbitrary"),
                                             vmem_limit_bytes=VMEM_LIMIT),
    )(a, b)


def _adamw_math(g8_ref, w_ref, m_ref, v_ref, g_ref, d_ref, nm_ref, nv_ref):
    g = g8_ref[0].astype(F32)
    for s in range(1, N_DEV):
        g = g + g8_ref[s].astype(F32)
    g_ref[...] = g
    nm = ADAM_B1 * m_ref[...] + (1.0 - ADAM_B1) * g
    nv = ADAM_B2 * v_ref[...] + (1.0 - ADAM_B2) * (g * g)
    nm_ref[...] = nm
    nv_ref[...] = nv
    m_hat = nm / (1.0 - ADAM_B1 ** ADAM_STEP)
    v_hat = nv / (1.0 - ADAM_B2 ** ADAM_STEP)
    d_ref[...] = -ADAM_LR * (m_hat / (jnp.sqrt(v_hat) + ADAM_EPS) + ADAM_WD * w_ref[...])


def _adamw_many(name, items):
    count = len(items)

    def body(*refs):
        ins, outs = refs[:4 * count], refs[4 * count:]
        for i in range(count):
            _adamw_math(*ins[4 * i:4 * i + 4], *outs[4 * i:4 * i + 4])

    flat = [a for item in items for a in item]
    res = pl.pallas_call(
        body, name=name,
        out_shape=[jax.ShapeDtypeStruct(item[1].shape, F32) for item in items for _ in range(4)],
        compiler_params=pltpu.CompilerParams(vmem_limit_bytes=VMEM_LIMIT),
    )(*flat)
    return [tuple(res[4 * i:4 * i + 4]) for i in range(count)]


def _adamw(name, g8, w, m, v):
    rows, cols = w.shape
    tr = rows
    for cand in (256, 176, 128, 64):
        if rows % cand == 0 and rows > cand:
            tr = cand
            break

    def body(*refs):
        _adamw_math(*refs)

    blk = pl.BlockSpec((tr, cols), lambda i: (i, 0))
    return pl.pallas_call(
        body, name=name, grid=(rows // tr,),
        in_specs=[pl.BlockSpec((N_DEV, tr, cols), lambda i: (0, i, 0)), blk, blk, blk],
        out_specs=[blk] * 4, out_shape=[jax.ShapeDtypeStruct((rows, cols), F32)] * 4,
        compiler_params=pltpu.CompilerParams(dimension_semantics=("parallel",), vmem_limit_bytes=VMEM_LIMIT),
    )(g8, w, m, v)


def _exchange_specs(srcs, scatter):
    nk = len(srcs)
    if not nk:
        return [], [], [], []
    any_spec = pl.BlockSpec(memory_space=pl.ANY)
    out_shape = [jax.ShapeDtypeStruct(s.shape if sc else (N_DEV,) + s.shape, s.dtype) for s, sc in zip(srcs, scatter)]
    sems = [pltpu.SemaphoreType.DMA((nk, N_DEV - 1)), pltpu.SemaphoreType.DMA((nk, N_DEV - 1)),
            pltpu.SemaphoreType.DMA((nk,))]
    return [any_spec] * nk, [any_spec] * nk, out_shape, sems


FLIPS = ((0, 0, 1), (1, 0, 0), (0, 1, 0), (1, 1, 0), (1, 0, 1), (0, 1, 1), (1, 1, 1))
N_CHIP_PEERS = 3


def _exchange_fns(src_refs, out_refs, sems, scatter):
    nk = len(src_refs)
    if not nk:
        return (lambda: None), (lambda: None), (lambda: None)
    send_sems, recv_sems, local_sems = sems
    first = 1 + N_CHIP_PEERS

    def plan():
        x, y, c = lax.axis_index("x"), lax.axis_index("y"), lax.axis_index("c")
        me = 4 * x + 2 * y + c
        peers = [(1 - x if fx else x, 1 - y if fy else y, 1 - c if fc else c) for fx, fy, fc in FLIPS]
        pids = [4 * px + 2 * py + pc for px, py, pc in peers]

        def remote(k, j, src, dst, to):
            return pltpu.make_async_remote_copy(src_ref=src, dst_ref=dst, send_sem=send_sems.at[k, j],
                                                recv_sem=recv_sems.at[k, j], device_id=to, device_id_type=MESH)

        def mine(k, dest):
            return src_refs[k].at[dest] if scatter[k] else src_refs[k]

        local = [pltpu.make_async_copy(mine(k, me), out_refs[k].at[me], local_sems.at[k]) for k in range(nk)]
        direct = [remote(k, j, mine(k, pids[j]), out_refs[k].at[me], peers[j])
                  for k in range(nk) for j in range(len(FLIPS) if scatter[k] else first)]
        relays = {(k, j): remote(k, j, out_refs[k].at[pids[j - N_CHIP_PEERS]], out_refs[k].at[pids[j - N_CHIP_PEERS]], peers[0])
                  for k in range(nk) if not scatter[k] for j in range(first, len(FLIPS))}
        arrivals = {(k, j): remote(k, j, out_refs[k].at[pids[j]], out_refs[k].at[pids[j]], peers[j])
                    for k in range(nk) for j in range(len(FLIPS))}
        return local, direct, relays, arrivals

    def start():
        local, direct, _, _ = plan()
        for cp in local + direct:
            cp.start()

    def relay():
        _, _, relays, arrivals = plan()
        for (k, j), cp in relays.items():
            arrivals[k, j - N_CHIP_PEERS].wait_recv()
            cp.start()

    def wait():
        local, direct, relays, arrivals = plan()
        for (k, j), cp in arrivals.items():
            if (k, j + N_CHIP_PEERS) not in relays:
                cp.wait_recv()
        for cp in direct + list(relays.values()):
            cp.wait_send()
        for cp in local:
            cp.wait()

    return start, relay, wait


def _grid_step(rank):
    step, total = 0, 1
    for axis in range(rank):
        step = step * pl.num_programs(axis) + pl.program_id(axis)
        total = total * pl.num_programs(axis)
    return step, total


def _ride(rank, start, relay, wait):
    step, total = _grid_step(rank)
    pl.when(step == 0)(start)
    pl.when(step == (3 * total) // 4)(relay)
    return lambda: pl.when(step == total - 1)(wait)


def _exchange(name, srcs, scatter):
    nk = len(srcs)
    c_in, c_out, c_shape, c_sems = _exchange_specs(srcs, scatter)

    def body(*refs):
        start, relay, wait = _exchange_fns(refs[:nk], refs[nk:2 * nk], refs[2 * nk:], scatter)
        start()
        relay()
        wait()

    return pl.pallas_call(body, name=name, in_specs=c_in, out_specs=c_out, out_shape=c_shape, scratch_shapes=c_sems)(*srcs)


def _cols_from_shards(g):
    return jnp.transpose(g, (1, 0, 2)).reshape(g.shape[1], -1)


def _cols_to_shards(w):
    return jnp.transpose(w.reshape(w.shape[0], N_DEV, -1), (1, 0, 2))


def _rope_tables(n):
    t = _t_pad()
    pos = (jnp.arange(t, dtype=F32) - PAD_ROWS)
    half = QK_ROPE // 2
    freqs = 1.0 / (ROPE_THETA ** (jnp.arange(half, dtype=F32) / half))
    ang = pos[:, None] * freqs[None, :]
    cos, sin = jnp.cos(ang), jnp.sin(ang)
    z = lambda w: jnp.zeros((t, w), F32)
    c = jnp.concatenate([jnp.ones((t, QK_NOPE), F32), cos, cos, z(HEAD_PAD - QK_HEAD)], axis=1)
    s1 = jnp.concatenate([z(QK_NOPE + half), sin, z(HEAD_PAD - QK_HEAD)], axis=1)
    s2 = jnp.concatenate([z(QK_NOPE), -sin, z(HEAD_PAD - QK_NOPE - half)], axis=1)
    reps = n // t
    return tuple(jnp.tile(a, (reps, 1)) for a in (c, s1, s2))


def _block_diag_gates(lru_wa, lru_wi):
    eye = jnp.eye(2, dtype=lru_wa.dtype)

    def bd(w):
        w = w.reshape(2, D_RNN // LANES, 2, RNN_BW, RNN_BW)
        full = w[:, :, :, :, None, :] * eye[None, None, :, None, :, None]
        return full.reshape(2, D_RNN // LANES, LANES, LANES)

    a, i = bd(lru_wa), bd(lru_wi)
    return jnp.concatenate([a[0], i[0], a[1], i[1]], axis=-1)


def _unblock_gates(dw):
    nb = D_RNN // LANES
    parts = dw.reshape(nb, 2, RNN_BW, 4, 2, RNN_BW)
    diag = jnp.stack([parts[:, k, :, :, k, :] for k in range(2)], axis=1)
    diag = jnp.transpose(diag, (3, 0, 1, 2, 4)).reshape(4, 2 * nb, RNN_BW, RNN_BW)
    return jnp.stack([diag[0], diag[2]]), jnp.stack([diag[1], diag[3]])


WEIGHTS = ("meta_tokens", "ln1_g", "w_in", "q_a_norm_g", "w_uq", "kv_a_norm_g", "w_ukv", "q_norm_g", "k_norm_g",
           "conv_w", "conv_b", "lru_wa", "lru_ba", "lru_wi", "lru_bi", "lru_lambda", "attn_out_g", "rnn_out_g",
           "w_out", "ln2_g", "w_gate", "w_up", "w_down")
BIG = ("w_in", "w_uq", "w_ukv", "w_out", "w_gate", "w_up", "w_down")
ROW_SHARDED = ("w_out", "w_down")
REPLICATED = ("ln1_g", "q_a_norm_g", "kv_a_norm_g", "q_norm_g", "k_norm_g", "conv_b", "lru_wa", "lru_wi",
              "attn_out_g", "rnn_out_g", "ln2_g")
G_FIRST = ("w_in", "meta_tokens")
G_MID = ("w_uq", "w_ukv", "conv_w", "lru_ba", "lru_bi", "lru_lambda")
LATE = ("w_out", "w_gate", "w_up", "w_down")
G_LAST = ("meta_tokens", "ln1_g")


def _local_step(x, tgt, ex):
    nb = x.shape[0]
    t = _t_pad()
    n = nb * t
    local = ex.local
    first = ex.gathered(G_FIRST, ex.run("gather_first", *ex.gather_srcs(G_FIRST)))
    meta, w_in = first["meta_tokens"], first["w_in"]
    lead = jnp.zeros((nb, PAD_ROWS, D_MODEL), F32)
    h0 = jnp.concatenate([lead, jnp.broadcast_to(meta[None], (nb, N_META, D_MODEL)), x], axis=1).reshape(n, D_MODEL)
    tgt_p = jnp.concatenate([jnp.zeros((nb, PAD_ROWS + N_META, D_MODEL), F32), tgt], axis=1).reshape(n, D_MODEL)

    zc = lambda c: jnp.zeros((D_MODEL, c), w_in.dtype)
    w_in_p = jnp.concatenate([w_in[:, :OFF_CKV], w_in[:, OFF_KR:], zc(QK_NOPE), w_in[:, OFF_CKV:OFF_KR],
                              zc(HEAD_PAD - QK_HEAD)], axis=1)
    pad_g = lambda g: jnp.pad(g, ((0, 0), (0, HEAD_PAD - QK_HEAD)))
    qg, kg = pad_g(local["q_norm_g"]), pad_g(local["k_norm_g"])
    rc, rs1, rs2 = _rope_tables(n)
    wblk = _block_diag_gates(local["lru_wa"].reshape(2, -1, RNN_BW, RNN_BW),
                             local["lru_wi"].reshape(2, -1, RNN_BW, RNN_BW)).astype(BF16)
    nblk = D_RNN // LANES

    (hn, cq, ckv, xr, xg, kr), got = _in_proj(h0, local["ln1_g"], w_in_p, *ex.gather_srcs(G_MID))
    w = ex.gathered(G_MID, got)
    w_uq_p = jnp.pad(w["w_uq"].reshape(Q_LORA, N_HEADS, QK_HEAD), ((0, 0), (0, 0), (0, HEAD_PAD - QK_HEAD))
                     ).reshape(Q_LORA, QP_COLS)
    ukv = w["w_ukv"].reshape(KV_LORA, N_HEADS, QK_NOPE + V_HEAD)
    w_uk_p = jnp.pad(ukv[:, :, :QK_NOPE], ((0, 0), (0, 0), (0, HEAD_PAD - QK_NOPE))).reshape(KV_LORA, QP_COLS)
    w_v = ukv[:, :, QK_NOPE:].reshape(KV_LORA, D_ATTN)
    gbias = jnp.stack([w["lru_ba"][0], w["lru_bi"][0], w["lru_ba"][1], w["lru_bi"][1]], axis=0)
    gbias = jnp.transpose(gbias.reshape(4, nblk, LANES), (1, 0, 2)).reshape(nblk, 1, 4 * LANES)

    q, k, v = _qkv_fwd(cq, ckv, kr, local["q_a_norm_g"], local["kv_a_norm_g"], w_uq_p, w_uk_p, w_v, qg, kg, rc, rs1, rs2)
    oa, got = _attn_fwd(q, k, v, *ex.gather_srcs(LATE))
    late = ex.gathered(LATE, got)
    orn = _rnn_fwd(xr, xg, w["conv_w"], local["conv_b"], wblk, gbias, w["lru_lambda"])
    (doa, dor, dh1, mix, h1n, act, dgate, dup, dyb, loss, dga, dgr, dg2) = _post(
        oa, orn, h0, tgt_p, local["attn_out_g"], local["rnn_out_g"], local["ln2_g"], late["w_out"], late["w_gate"],
        late["w_up"], late["w_down"])
    wire = {"w_out": _matmul_tn_shards("dw_out", mix, dh1, True), "w_gate": _matmul_tn_shards("dw_gate", h1n, dgate, False),
            "w_up": _matmul_tn_shards("dw_up", h1n, dup, False), "w_down": _matmul_tn_shards("dw_down", act, dyb, True)}
    dxr, dxg, dcw, dcb, dwblk, dgb, dlam = _rnn_bwd(xr, xg, dor, w["conv_w"], local["conv_b"], wblk, gbias, w["lru_lambda"])
    dwa, dwi = _unblock_gates(dwblk)
    dgb = jnp.transpose(dgb.reshape(nblk, 4, LANES), (1, 0, 2)).reshape(4, D_RNN)
    wire.update(ex.to_wire({
        "conv_w": dcw, "conv_b": dcb, "lru_wa": dwa.reshape(-1, RNN_BW), "lru_ba": jnp.stack([dgb[0], dgb[2]]),
        "lru_wi": dwi.reshape(-1, RNN_BW), "lru_bi": jnp.stack([dgb[1], dgb[3]]), "lru_lambda": dlam,
        "attn_out_g": dga, "rnn_out_g": dgr, "ln2_g": dg2}))
    names = tuple(wire)
    (dq_r, dk_r, dv), got = _attn_bwd(q, k, v, doa, *ex.scatter_srcs(names, wire))
    summed = ex.scattered(names, wire, got)
    (dp, qa, kva, dqp, dkv, dqg, dkg, dgqa, dgkva) = _qkv_bwd(
        cq, ckv, kr, dq_r, dk_r, dv, dxr, dxg, local["q_a_norm_g"], local["kv_a_norm_g"], w_uq_p, w_uk_p, w_v, qg, kg,
        rc, rs1, rs2)
    dw_in_p = _matmul_tn("dw_in", hn, dp)
    dw_uq_p = _matmul_tn("dw_uq", qa, dqp)
    dw_kv = _matmul_tn("dw_ukv", kva, dkv)
    kr0 = OFF_CKV + 2 * D_RNN + QK_NOPE
    dw_in = jnp.concatenate([dw_in_p[:, :OFF_CKV], dw_in_p[:, kr0:kr0 + QK_ROPE], dw_in_p[:, OFF_CKV:OFF_CKV + 2 * D_RNN]],
                            axis=1)
    dw_uq = dw_uq_p.reshape(Q_LORA, N_HEADS, HEAD_PAD)[:, :, :QK_HEAD].reshape(Q_LORA, N_HEADS * QK_HEAD)
    dw_ukv = jnp.concatenate([dw_kv[:, :QP_COLS].reshape(KV_LORA, N_HEADS, HEAD_PAD)[:, :, :QK_NOPE],
                              dw_kv[:, QP_COLS:].reshape(KV_LORA, N_HEADS, V_HEAD)], axis=2).reshape(KV_LORA, -1)
    wire = ex.to_wire({"w_in": dw_in, "q_a_norm_g": dgqa, "w_uq": dw_uq, "kv_a_norm_g": dgkva, "w_ukv": dw_ukv,
                       "q_norm_g": dqg[:, :QK_HEAD], "k_norm_g": dkg[:, :QK_HEAD]})
    names = tuple(wire)
    (dh0, dg1), got = _in_bwd(dp, h0, dh1, local["ln1_g"], w_in_p, *ex.scatter_srcs(names, wire))
    summed.update(ex.scattered(names, wire, got))

    dh0 = dh0.reshape(nb, t, D_MODEL)
    wire = ex.to_wire({"meta_tokens": jnp.sum(dh0[:, PAD_ROWS:PAD_ROWS + N_META], axis=0), "ln1_g": dg1})
    got = ex.run("reduce_last", *ex.scatter_srcs(G_LAST, wire))
    summed.update(ex.scattered(G_LAST, wire, got))
    return loss[0, 0], dh0[:, PAD_ROWS + N_META:], summed


class _MeshExchange:
    def __init__(self, shards):
        self.local = shards

    @staticmethod
    def run(name, srcs, scatter):
        return _exchange(name, srcs, scatter)

    def gather_srcs(self, names):
        return [self.local[k].astype(BF16) if k in BIG else self.local[k] for k in names], [False] * len(names)

    @staticmethod
    def gathered(names, outs):
        return {k: g.reshape(-1, g.shape[-1]) if k in ROW_SHARDED else _cols_from_shards(g) for k, g in zip(names, outs)}

    @staticmethod
    def to_wire(grads):
        wire = {}
        for k, g in grads.items():
            if k in REPLICATED:
                wire[k] = g
            elif k in ROW_SHARDED:
                wire[k] = g.reshape(N_DEV, -1, g.shape[-1]).astype(BF16)
            else:
                wire[k] = _cols_to_shards(g).astype(BF16) if k in BIG else _cols_to_shards(g)
        return wire

    @staticmethod
    def scatter_srcs(names, wire):
        return [wire[k] for k in names], [k not in REPLICATED for k in names]

    @staticmethod
    def scattered(names, wire, outs):
        return dict(zip(names, outs))


def kernel(x, meta_tokens, ln1_g, w_in, q_a_norm_g, w_uq, kv_a_norm_g, w_ukv, q_norm_g, k_norm_g, conv_w, conv_b, lru_wa, lru_ba, lru_wi, lru_bi, lru_lambda, attn_out_g, rnn_out_g, w_out, ln2_g, w_gate, w_up, w_down, loss_target, m_meta_tokens, m_ln1_g, m_w_in, m_q_a_norm_g, m_w_uq, m_kv_a_norm_g, m_w_ukv, m_q_norm_g, m_k_norm_g, m_conv_w, m_conv_b, m_lru_wa, m_lru_ba, m_lru_wi, m_lru_bi, m_lru_lambda, m_attn_out_g, m_rnn_out_g, m_w_out, m_ln2_g, m_w_gate, m_w_up, m_w_down, v_meta_tokens, v_ln1_g, v_w_in, v_q_a_norm_g, v_w_uq, v_kv_a_norm_g, v_w_ukv, v_q_norm_g, v_k_norm_g, v_conv_w, v_conv_b, v_lru_wa, v_lru_ba, v_lru_wi, v_lru_bi, v_lru_lambda, v_attn_out_g, v_rnn_out_g, v_w_out, v_ln2_g, v_w_gate, v_w_up, v_w_down):
    given = (meta_tokens, ln1_g, w_in, q_a_norm_g, w_uq, kv_a_norm_g, w_ukv, q_norm_g, k_norm_g, conv_w, conv_b,
             lru_wa, lru_ba, lru_wi, lru_bi, lru_lambda, attn_out_g, rnn_out_g, w_out, ln2_g, w_gate, w_up, w_down)
    moments_m = (m_meta_tokens, m_ln1_g, m_w_in, m_q_a_norm_g, m_w_uq, m_kv_a_norm_g, m_w_ukv, m_q_norm_g, m_k_norm_g,
                 m_conv_w, m_conv_b, m_lru_wa, m_lru_ba, m_lru_wi, m_lru_bi, m_lru_lambda, m_attn_out_g, m_rnn_out_g,
                 m_w_out, m_ln2_g, m_w_gate, m_w_up, m_w_down)
    moments_v = (v_meta_tokens, v_ln1_g, v_w_in, v_q_a_norm_g, v_w_uq, v_kv_a_norm_g, v_w_ukv, v_q_norm_g, v_k_norm_g,
                 v_conv_w, v_conv_b, v_lru_wa, v_lru_ba, v_lru_wi, v_lru_bi, v_lru_lambda, v_attn_out_g, v_rnn_out_g,
                 v_w_out, v_ln2_g, v_w_gate, v_w_up, v_w_down)
    shapes = {k: a.shape for k, a in zip(WEIGHTS, given)}

    def two_d(a):
        return a.reshape(-1, a.shape[-1])

    w = {k: two_d(a) for k, a in zip(WEIGHTS, given)}
    m = {k: two_d(a) for k, a in zip(WEIGHTS, moments_m)}
    v = {k: two_d(a) for k, a in zip(WEIGHTS, moments_v)}

    loss_part, grad_x, parts = _local_step(x, loss_target, _MeshExchange(w))

    new = {k: _adamw("adamw_" + k, parts[k], w[k], m[k], v[k]) for k in BIG}
    small = [k for k in WEIGHTS if k not in BIG]
    new.update(zip(small, _adamw_many("adamw_small", [(parts[k], w[k], m[k], v[k]) for k in small])))

    loss = lax.psum(loss_part, ("x", "y", "c"))
    outs = [loss, grad_x]
    for idx in range(4):
        outs += [new[k][idx].reshape(shapes[k]) for k in WEIGHTS]
    return tuple(outs)
```

```python
import functools
import math

import jax
import jax.numpy as jnp
from jax import lax
from jax.experimental import pallas as pl
from jax.experimental.pallas import tpu as pltpu

F32 = jnp.float32
BF16 = jnp.bfloat16

D_MODEL = 1024
N_META = 16
SEQ = 2048
N_HEADS = 8
QK_NOPE = 64
QK_ROPE = 32
QK_HEAD = QK_NOPE + QK_ROPE
V_HEAD = 64
D_ATTN = N_HEADS * V_HEAD
Q_LORA = 384
KV_LORA = 256
D_RNN = 512
RNN_BW = 64
D_FF = 2816
EPS = 1e-6
LRU_C = 8.0
ROPE_THETA = 10000.0
OFF_CKV = Q_LORA + KV_LORA
OFF_KR = OFF_CKV + QK_ROPE
IN_COLS = OFF_KR + 2 * D_RNN

ADAM_LR = 0.001
ADAM_B1 = 0.9
ADAM_B2 = 0.999
ADAM_EPS = 1e-08
ADAM_WD = 0.01
ADAM_STEP = 10

N_DEV = 8
LANES = 128
HEAD_PAD = LANES
PAD_ROWS = LANES - N_META
QP_COLS = N_HEADS * HEAD_PAD
P_COLS = OFF_CKV + 2 * D_RNN + LANES
FF_CHUNK = D_FF // 2
VMEM_LIMIT = 56 * 1024 * 1024
MESH = pl.DeviceIdType.MESH


def _t_pad():
    return PAD_ROWS + N_META + SEQ


def _row_tile(n):
    return 256 if n % 256 == 0 else 128


def _const_spec(shape):
    nd = len(shape)
    return pl.BlockSpec(shape, lambda *_: (0,) * nd, pipeline_mode=pl.Buffered(1))


def _rms(x, d):
    r = lax.rsqrt(jnp.sum(x * x, axis=-1, keepdims=True) * (1.0 / d) + EPS)
    return x * r, r


def _rms_bwd(dy, xhat, r, g, d):
    dxh = dy * g
    return r * (dxh - xhat * (jnp.sum(dxh * xhat, axis=-1, keepdims=True) * (1.0 / d)))


def _colsum(x):
    return jnp.sum(x, axis=0, keepdims=True)


def _dot(a, b):
    return jnp.dot(a, b, preferred_element_type=F32)


def _dot_nt(a, b):
    return lax.dot_general(a, b, (((1,), (1,)), ((), ())), preferred_element_type=F32)


def _dot_tn(a, b):
    return lax.dot_general(a, b, (((0,), (0,)), ((), ())), preferred_element_type=F32)


def _rope(x, c, s1, s2):
    return x * c + pltpu.roll(x, 16, 1) * s1 + pltpu.roll(x, HEAD_PAD - 16, 1) * s2


def _rope_bwd(dy, c, s1, s2):
    return dy * c + pltpu.roll(dy * s1, HEAD_PAD - 16, 1) + pltpu.roll(dy * s2, 16, 1)


def _acc(ref, first, val):
    @pl.when(first)
    def _():
        ref[...] = val

    @pl.when(jnp.logical_not(first))
    def _():
        ref[...] += val


def _in_proj(h0, ln1_g, w_in_p, srcs=(), scatter=()):
    n = h0.shape[0]
    tm = _row_tile(n)
    nk = len(srcs)
    c_in, c_out, c_shape, c_sems = _exchange_specs(srcs, scatter)

    def body(h_ref, g_ref, w_ref, *rest):
        hn_ref, cq_ref, ckv_ref, xr_ref, xg_ref, kr_ref = rest[nk:nk + 6]
        finish = _ride(1, *_exchange_fns(rest[:nk], rest[nk + 6:2 * nk + 6], rest[2 * nk + 6:], scatter))
        xhat, _ = _rms(h_ref[...], D_MODEL)
        hn = (xhat * g_ref[...]).astype(BF16)
        hn_ref[...] = hn
        p = _dot_nt(hn, w_ref[...])
        cq_ref[...] = p[:, :Q_LORA]
        ckv_ref[...] = p[:, Q_LORA:OFF_CKV]
        xr_ref[...] = p[:, OFF_CKV:OFF_CKV + D_RNN]
        xg_ref[...] = p[:, OFF_CKV + D_RNN:OFF_CKV + 2 * D_RNN]
        kr_ref[...] = p[:, OFF_CKV + 2 * D_RNN:]
        finish()

    def row(w):
        return pl.BlockSpec((tm, w), lambda i: (i, 0))

    widths = (D_MODEL, Q_LORA, KV_LORA, D_RNN, D_RNN, LANES)
    res = pl.pallas_call(
        body, name="in_proj", grid=(n // tm,),
        in_specs=[row(D_MODEL), _const_spec((1, D_MODEL)), _const_spec((P_COLS, D_MODEL))] + c_in,
        out_specs=[row(w) for w in widths] + c_out,
        out_shape=[jax.ShapeDtypeStruct((n, w), BF16 if k == 0 else F32) for k, w in enumerate(widths)] + c_shape,
        scratch_shapes=c_sems,
        compiler_params=pltpu.CompilerParams(dimension_semantics=("arbitrary",), vmem_limit_bytes=VMEM_LIMIT),
    )(h0, ln1_g, w_in_p, *srcs)
    return res[:6], res[6:]


def _qkv_fwd(cq, ckv, kr, gqa, gkva, w_uq_p, w_uk_p, w_v, qg, kg, rc, rs1, rs2):
    n = cq.shape[0]
    tm = _row_tile(n)

    def body(cq_ref, ckv_ref, kr_ref, gqa_ref, gkva_ref, wuq_ref, wuk_ref, wv_ref, qg_ref, kg_ref,
             c_ref, s1_ref, s2_ref, q_ref, k_ref, v_ref):
        xq, _ = _rms(cq_ref[...], Q_LORA)
        qa = (xq * gqa_ref[...]).astype(BF16)
        q = _dot_nt(qa, wuq_ref[...])
        xkv, _ = _rms(ckv_ref[...], KV_LORA)
        kva = (xkv * gkva_ref[...]).astype(BF16)
        kn = _dot(kva, wuk_ref[...])
        v_ref[...] = _dot(kva, wv_ref[...]).astype(BF16)
        krp = kr_ref[...]
        c, s1, s2 = c_ref[...], s1_ref[...], s2_ref[...]
        for h in range(N_HEADS):
            sl = slice(h * HEAD_PAD, (h + 1) * HEAD_PAD)
            qh, _ = _rms(q[:, sl], QK_HEAD)
            q_ref[:, sl] = _rope(qh * qg_ref[...], c, s1, s2).astype(BF16)
            kh, _ = _rms(kn[:, sl] + krp, QK_HEAD)
            k_ref[:, sl] = _rope(kh * kg_ref[...], c, s1, s2).astype(BF16)

    def row(w):
        return pl.BlockSpec((tm, w), lambda i: (i, 0))

    return pl.pallas_call(
        body, name="qkv_fwd", grid=(n // tm,),
        in_specs=[row(Q_LORA), row(KV_LORA), row(LANES), _const_spec((1, Q_LORA)), _const_spec((1, KV_LORA)),
                  _const_spec((QP_COLS, Q_LORA)), _const_spec((KV_LORA, QP_COLS)), _const_spec((KV_LORA, D_ATTN)),
                  _const_spec((1, LANES)), _const_spec((1, LANES)), row(LANES), row(LANES), row(LANES)],
        out_specs=[row(QP_COLS), row(QP_COLS), row(D_ATTN)],
        out_shape=[jax.ShapeDtypeStruct((n, QP_COLS), BF16), jax.ShapeDtypeStruct((n, QP_COLS), BF16),
                   jax.ShapeDtypeStruct((n, D_ATTN), BF16)],
        compiler_params=pltpu.CompilerParams(dimension_semantics=("parallel",), vmem_limit_bytes=VMEM_LIMIT),
    )(cq, ckv, kr, gqa, gkva, w_uq_p, w_uk_p, w_v, qg, kg, rc, rs1, rs2)


def _qkv_bwd(cq, ckv, kr, dq_r, dk_r, dv, dxr, dxg, gqa, gkva, w_uq_p, w_uk_p, w_v, qg, kg, rc, rs1, rs2):
    n = cq.shape[0]
    tm = _row_tile(n)

    def body(cq_ref, ckv_ref, kr_ref, dq_ref, dk_ref, dv_ref, dxr_ref, dxg_ref, gqa_ref, gkva_ref, wuq_ref, wuk_ref,
             wv_ref, qg_ref, kg_ref, c_ref, s1_ref, s2_ref,
             dp_ref, qa_ref, kva_ref, dqp_ref, dkv_ref, dqg_ref, dkg_ref, dgqa_ref, dgkva_ref):
        first = pl.program_id(0) == 0
        dp_ref[:, OFF_CKV:OFF_CKV + D_RNN] = dxr_ref[...].astype(BF16)
        dp_ref[:, OFF_CKV + D_RNN:OFF_CKV + 2 * D_RNN] = dxg_ref[...].astype(BF16)
        xq, rq = _rms(cq_ref[...], Q_LORA)
        qa = (xq * gqa_ref[...]).astype(BF16)
        qa_ref[...] = qa
        q = _dot_nt(qa, wuq_ref[...])
        xkv, rkv = _rms(ckv_ref[...], KV_LORA)
        kva = (xkv * gkva_ref[...]).astype(BF16)
        kva_ref[...] = kva
        kn = _dot(kva, wuk_ref[...])
        krp = kr_ref[...]
        c, s1, s2 = c_ref[...], s1_ref[...], s2_ref[...]
        lane = lax.broadcasted_iota(jnp.int32, (tm, HEAD_PAD), 1)
        rope_lanes = jnp.logical_and(lane >= QK_NOPE, lane < QK_HEAD)
        dqg = jnp.zeros((1, HEAD_PAD), F32)
        dkg = jnp.zeros((1, HEAD_PAD), F32)
        dkr = jnp.zeros((tm, HEAD_PAD), F32)
        for h in range(N_HEADS):
            sl = slice(h * HEAD_PAD, (h + 1) * HEAD_PAD)
            qh, rqh = _rms(q[:, sl], QK_HEAD)
            dy = _rope_bwd(dq_ref[:, sl], c, s1, s2)
            dqg = dqg + _colsum(dy * qh)
            dqp_ref[:, sl] = _rms_bwd(dy, qh, rqh, qg_ref[...], QK_HEAD).astype(BF16)
            kh, rkh = _rms(kn[:, sl] + krp, QK_HEAD)
            dyk = _rope_bwd(dk_ref[:, sl], c, s1, s2)
            dkg = dkg + _colsum(dyk * kh)
            dkh = _rms_bwd(dyk, kh, rkh, kg_ref[...], QK_HEAD)
            dkv_ref[:, sl] = dkh.astype(BF16)
            dkr = dkr + jnp.where(rope_lanes, dkh, 0.0)
        dkv_ref[:, QP_COLS:] = dv_ref[...].astype(BF16)
        dp_ref[:, OFF_CKV + 2 * D_RNN:] = dkr.astype(BF16)
        dqa = _dot(dqp_ref[...], wuq_ref[...])
        dp_ref[:, :Q_LORA] = _rms_bwd(dqa, xq, rq, gqa_ref[...], Q_LORA).astype(BF16)
        dkva = _dot_nt(dkv_ref[:, :QP_COLS], wuk_ref[...]) + _dot_nt(dkv_ref[:, QP_COLS:], wv_ref[...])
        dp_ref[:, Q_LORA:OFF_CKV] = _rms_bwd(dkva, xkv, rkv, gkva_ref[...], KV_LORA).astype(BF16)
        _acc(dqg_ref, first, dqg)
        _acc(dkg_ref, first, dkg)
        _acc(dgqa_ref, first, _colsum(dqa * xq))
        _acc(dgkva_ref, first, _colsum(dkva * xkv))

    def row(w):
        return pl.BlockSpec((tm, w), lambda i: (i, 0))

    def acc(w):
        return pl.BlockSpec((1, w), lambda i: (0, 0))

    return pl.pallas_call(
        body, name="qkv_bwd", grid=(n // tm,),
        in_specs=[row(Q_LORA), row(KV_LORA), row(LANES), row(QP_COLS), row(QP_COLS), row(D_ATTN), row(D_RNN), row(D_RNN),
                  _const_spec((1, Q_LORA)), _const_spec((1, KV_LORA)),
                  _const_spec((QP_COLS, Q_LORA)), _const_spec((KV_LORA, QP_COLS)), _const_spec((KV_LORA, D_ATTN)),
                  _const_spec((1, LANES)), _const_spec((1, LANES)), row(LANES), row(LANES), row(LANES)],
        out_specs=[row(P_COLS), row(Q_LORA), row(KV_LORA), row(QP_COLS),
                   row(QP_COLS + D_ATTN), acc(LANES), acc(LANES), acc(Q_LORA), acc(KV_LORA)],
        out_shape=[jax.ShapeDtypeStruct((n, P_COLS), BF16), jax.ShapeDtypeStruct((n, Q_LORA), BF16),
                   jax.ShapeDtypeStruct((n, KV_LORA), BF16), jax.ShapeDtypeStruct((n, QP_COLS), BF16),
                   jax.ShapeDtypeStruct((n, QP_COLS + D_ATTN), BF16),
                   jax.ShapeDtypeStruct((1, LANES), F32), jax.ShapeDtypeStruct((1, LANES), F32),
                   jax.ShapeDtypeStruct((1, Q_LORA), F32), jax.ShapeDtypeStruct((1, KV_LORA), F32)],
        compiler_params=pltpu.CompilerParams(dimension_semantics=("arbitrary",), vmem_limit_bytes=VMEM_LIMIT),
    )(cq, ckv, kr, dq_r, dk_r, dv, dxr, dxg, gqa, gkva, w_uq_p, w_uk_p, w_v, qg, kg, rc, rs1, rs2)


def _softmax_parts(qh, kh, tq, t):
    s = _dot_nt(qh, kh) * (QK_HEAD ** -0.5)
    key = lax.broadcasted_iota(jnp.int32, (tq, t), 1)
    s = jnp.where(key >= PAD_ROWS, s, -jnp.inf)
    e = jnp.exp(s - jnp.max(s, axis=-1, keepdims=True))
    return e, jnp.sum(e, axis=-1, keepdims=True)


def _attn_specs(t, tq):
    nq = t // tq
    qspec = pl.BlockSpec((tq, 2 * HEAD_PAD), lambda b, hp, i: (b * nq + i, hp))
    kspec = pl.BlockSpec((t, 2 * HEAD_PAD), lambda b, hp, i: (b, hp))
    vspec = pl.BlockSpec((t, 2 * V_HEAD), lambda b, hp, i: (b, hp))
    ospec = pl.BlockSpec((tq, 2 * V_HEAD), lambda b, hp, i: (b * nq + i, hp))
    return nq, qspec, kspec, vspec, ospec


def _attn_fwd(q, k, v, srcs=(), scatter=()):
    n = q.shape[0]
    t = _t_pad()
    tq = t // 8
    nq, qspec, kspec, vspec, ospec = _attn_specs(t, tq)
    nk = len(srcs)
    c_in, c_out, c_shape, c_sems = _exchange_specs(srcs, scatter)

    def body(q_ref, k_ref, v_ref, *rest):
        o_ref = rest[nk]
        finish = _ride(3, *_exchange_fns(rest[:nk], rest[nk + 1:2 * nk + 1], rest[2 * nk + 1:], scatter))
        lane = lax.broadcasted_iota(jnp.int32, (tq, 2 * V_HEAD), 1)
        outs = []
        for j in range(2):
            sl = slice(j * HEAD_PAD, (j + 1) * HEAD_PAD)
            e, l = _softmax_parts(q_ref[:, sl], k_ref[:, sl], tq, t)
            outs.append(_dot(e.astype(BF16), v_ref[...]) / l)
        o_ref[...] = jnp.where(lane < V_HEAD, outs[0], outs[1])
        finish()

    res = pl.pallas_call(
        body, name="attn_fwd", grid=(n // t, N_HEADS // 2, nq),
        in_specs=[qspec, kspec, vspec] + c_in, out_specs=[ospec] + c_out,
        out_shape=[jax.ShapeDtypeStruct((n, D_ATTN), F32)] + c_shape, scratch_shapes=c_sems,
        compiler_params=pltpu.CompilerParams(dimension_semantics=("arbitrary", "arbitrary", "arbitrary"),
                                             vmem_limit_bytes=VMEM_LIMIT),
    )(q, k, v, *srcs)
    return res[0], res[1:]


def _attn_bwd(q, k, v, do, srcs=(), scatter=()):
    n = q.shape[0]
    t = _t_pad()
    tq = t // 4
    nq, qspec, kspec, vspec, ospec = _attn_specs(t, tq)
    nk = len(srcs)
    c_in, c_out, c_shape, c_sems = _exchange_specs(srcs, scatter)

    def body(q_ref, k_ref, v_ref, do_ref, *rest):
        dq_ref, dk_ref, dv_ref = rest[nk:nk + 3]
        finish = _ride(3, *_exchange_fns(rest[:nk], rest[nk + 3:2 * nk + 3], rest[2 * nk + 3:], scatter))
        first = pl.program_id(2) == 0
        lane = lax.broadcasted_iota(jnp.int32, (tq, 2 * V_HEAD), 1)
        do = do_ref[...]
        dv = jnp.zeros((t, 2 * V_HEAD), F32)
        for j in range(2):
            sl = slice(j * HEAD_PAD, (j + 1) * HEAD_PAD)
            qh, kh = q_ref[:, sl], k_ref[:, sl]
            e, l = _softmax_parts(qh, kh, tq, t)
            p = e / l
            in_head = (lane < V_HEAD) if j == 0 else (lane >= V_HEAD)
            doh = jnp.where(in_head, do, 0.0).astype(BF16)
            dp = _dot_nt(doh, v_ref[...])
            delta = jnp.sum(p * dp, axis=-1, keepdims=True)
            ds = (p * (dp - delta) * (QK_HEAD ** -0.5)).astype(BF16)
            dq_ref[:, sl] = _dot(ds, kh)
            dkh = _dot_tn(ds, qh)

            @pl.when(first)
            def _():
                dk_ref[:, sl] = dkh

            @pl.when(jnp.logical_not(first))
            def _():
                dk_ref[:, sl] += dkh

            dv = dv + _dot_tn(p.astype(BF16), doh)
        _acc(dv_ref, first, dv)
        finish()

    res = pl.pallas_call(
        body, name="attn_bwd", grid=(n // t, N_HEADS // 2, nq),
        in_specs=[qspec, kspec, vspec, ospec] + c_in, out_specs=[qspec, kspec, vspec] + c_out,
        out_shape=[jax.ShapeDtypeStruct((n, QP_COLS), F32), jax.ShapeDtypeStruct((n, QP_COLS), F32),
                   jax.ShapeDtypeStruct((n, D_ATTN), F32)] + c_shape, scratch_shapes=c_sems,
        compiler_params=pltpu.CompilerParams(dimension_semantics=("arbitrary", "arbitrary", "arbitrary"),
                                             vmem_limit_bytes=VMEM_LIMIT),
    )(q, k, v, do, *srcs)
    return res[:3], res[3:]


SCAN_STEPS = 8


def _scan(chains, t):
    seg = t // 8
    rows = lax.broadcasted_iota(jnp.int32, (8, LANES), 0)

    def step(i, carry):
        carry = list(carry)
        for u in range(SCAN_STEPS):
            j = i * SCAN_STEPS + u
            for n, (a_ref, b_ref, h_ref, p_ref, reverse) in enumerate(chains):
                h, p = carry[n]
                idx = pl.ds(seg - 1 - j if reverse else j, 8, stride=seg)
                a = a_ref[idx, :]
                h = a * h + b_ref[idx, :]
                p = a * p
                h_ref[idx, :] = h
                p_ref[idx, :] = p
                carry[n] = (h, p)
        return tuple(carry)

    init = tuple((jnp.zeros((8, LANES), F32), jnp.ones((8, LANES), F32)) for _ in chains)
    ends = lax.fori_loop(0, seg // SCAN_STEPS, step, init)
    for (_, _, h_ref, p_ref, reverse), (b, a) in zip(chains, ends):
        for d in (1, 2, 4):
            if reverse:
                keep = rows < 8 - d
                a_n, b_n = pltpu.roll(a, 8 - d, 0), pltpu.roll(b, 8 - d, 0)
            else:
                keep = rows >= d
                a_n, b_n = pltpu.roll(a, d, 0), pltpu.roll(b, d, 0)
            b = a * jnp.where(keep, b_n, 0.0) + b
            a = a * jnp.where(keep, a_n, 1.0)
        for s in (range(7) if reverse else range(1, 8)):
            sl = slice(s * seg, (s + 1) * seg)
            carry_in = b[s + 1:s + 2, :] if reverse else b[s - 1:s, :]
            h_ref[sl, :] = h_ref[sl, :] + p_ref[sl, :] * carry_in


def _shift_rows(x, s, rows, t):
    if s == 0:
        return x
    rolled = pltpu.roll(x, s % t, 0)
    return jnp.where(rows >= s, rolled, 0.0) if s > 0 else jnp.where(rows < t + s, rolled, 0.0)


def _neg_expm1(x):
    series = -x * (1.0 + x * (0.5 + x * (1.0 / 6 + x * (1.0 / 24 + x * (1.0 / 120 + x * (1.0 / 720))))))
    return jnp.where(x > -0.3, series, 1.0 - jnp.exp(x))


def _gelu_parts(x):
    k = math.sqrt(2.0 / math.pi)
    th = jnp.tanh(k * (x + 0.044715 * x * x * x))
    g = 0.5 * x * (1.0 + th)
    dg = 0.5 * (1.0 + th) + 0.5 * x * (1.0 - th * th) * k * (1.0 + 3 * 0.044715 * x * x)
    return g, dg


def _lru_gates(xc, gates, lam_ref, valid, d):
    r = jax.nn.sigmoid(gates[:, (2 * d) * LANES:(2 * d + 1) * LANES])
    i = jax.nn.sigmoid(gates[:, (2 * d + 1) * LANES:(2 * d + 2) * LANES])
    neg_lam = -lam_ref[d:d + 1, :]
    sp = jnp.maximum(neg_lam, 0.0) + jnp.log1p(jnp.exp(-jnp.abs(neg_lam)))
    log_a = -LRU_C * r * sp
    a = jnp.exp(log_a)
    m = jnp.maximum(_neg_expm1(2.0 * log_a), 0.0)
    sq = jnp.sqrt(m)
    b = jnp.where(valid, sq * (i * xc), 0.0)
    return r, i, sp, a, m, sq, b


def _conv(xr, cw_ref, cb_ref, rows, t):
    return (cw_ref[0:1, :] * _shift_rows(xr, 2, rows, t) + cw_ref[1:2, :] * _shift_rows(xr, 1, rows, t)
            + cw_ref[2:3, :] * xr + cw_ref[3:4, :] * _shift_rows(xr, -1, rows, t) + cb_ref[...])


def _rnn_specs(t):
    seq = pl.BlockSpec((t, LANES), lambda cb, b: (b, cb))
    cw = pl.BlockSpec((4, LANES), lambda cb, b: (0, cb))
    vec1 = pl.BlockSpec((1, LANES), lambda cb, b: (0, cb))
    vec2 = pl.BlockSpec((2, LANES), lambda cb, b: (0, cb))
    wblk = pl.BlockSpec((1, LANES, 4 * LANES), lambda cb, b: (cb, 0, 0))
    gbias = pl.BlockSpec((1, 1, 4 * LANES), lambda cb, b: (cb, 0, 0))
    return seq, cw, vec1, vec2, wblk, gbias


def _rnn_fwd(xr, xg, conv_w, conv_b, wblk, gbias, lam):
    n = xr.shape[0]
    t = _t_pad()
    seq, cw, vec1, vec2, wspec, gspec = _rnn_specs(t)

    def body(xr_ref, xg_ref, cw_ref, cb_ref, w_ref, gb_ref, lam_ref, o_ref, a_s, b_s, h_s, p_s):
        rows = lax.broadcasted_iota(jnp.int32, (t, LANES), 0)
        valid = rows >= PAD_ROWS
        xc = _conv(xr_ref[...], cw_ref, cb_ref, rows, t)
        gates = _dot(xc.astype(BF16), w_ref[0]) + gb_ref[0]
        for d in range(2):
            _, _, _, a, _, _, b = _lru_gates(xc, gates, lam_ref, valid, d)
            a_s[d] = a
            b_s[d] = b
        _scan([(a_s.at[d], b_s.at[d], h_s.at[d], p_s.at[d], d == 1) for d in range(2)], t)
        g, _ = _gelu_parts(xg_ref[...])
        o_ref[...] = (h_s[0] + h_s[1]) * g

    return pl.pallas_call(
        body, name="rnn_fwd", grid=(D_RNN // LANES, n // t),
        in_specs=[seq, seq, cw, vec1, wspec, gspec, vec2], out_specs=seq,
        out_shape=jax.ShapeDtypeStruct((n, D_RNN), F32),
        scratch_shapes=[pltpu.VMEM((2, t, LANES), F32)] * 4,
        compiler_params=pltpu.CompilerParams(dimension_semantics=("parallel", "parallel"), vmem_limit_bytes=VMEM_LIMIT),
    )(xr, xg, conv_w, conv_b, wblk, gbias, lam)


def _rnn_bwd(xr, xg, do, conv_w, conv_b, wblk, gbias, lam):
    n = xr.shape[0]
    t = _t_pad()
    seq, cw, vec1, vec2, wspec, gspec = _rnn_specs(t)

    def body(xr_ref, xg_ref, do_ref, cw_ref, cb_ref, w_ref, gb_ref, lam_ref,
             dxr_ref, dxg_ref, dcw_ref, dcb_ref, dw_ref, dgb_ref, dlam_ref, a_s, b_s, h_s, l_s, p_s, dg_s):
        first = pl.program_id(1) == 0
        rows = lax.broadcasted_iota(jnp.int32, (t, LANES), 0)
        valid = rows >= PAD_ROWS
        xr = xr_ref[...]
        xc = _conv(xr, cw_ref, cb_ref, rows, t)
        xcb = xc.astype(BF16)
        gates = _dot(xcb, w_ref[0]) + gb_ref[0]
        for d in range(2):
            _, _, _, a, _, _, b = _lru_gates(xc, gates, lam_ref, valid, d)
            a_s[d] = a
            b_s[d] = b
        _scan([(a_s.at[d], b_s.at[d], h_s.at[d], p_s.at[d], d == 1) for d in range(2)], t)
        g, dg = _gelu_parts(xg_ref[...])
        do = do_ref[...]
        dxg_ref[...] = do * (h_s[0] + h_s[1]) * dg
        b_s[0] = do * g
        for d in range(2):
            a_s[d] = _shift_rows(a_s[d], -1 if d == 0 else 1, rows, t)
        _scan([(a_s.at[d], b_s.at[0], l_s.at[d], p_s.at[d], d == 0) for d in range(2)], t)
        dxc = jnp.zeros((t, LANES), F32)
        dlams = []
        for d in range(2):
            r, i, sp, a, m, sq, _ = _lru_gates(xc, gates, lam_ref, valid, d)
            lam_t = l_s[d]
            da = lam_t * _shift_rows(h_s[d], 1 if d == 0 else -1, rows, t)
            lam_v = jnp.where(valid, lam_t, 0.0)
            dsq = lam_v * (i * xc)
            di = lam_v * sq * xc
            dxc = dxc + lam_v * sq * i
            dm = jnp.where(m > 0.0, dsq * 0.5 / jnp.where(m > 0.0, sq, 1.0), 0.0)
            dla = da * a - 2.0 * dm * a * a
            dr = dla * (-LRU_C) * sp
            dsp = _colsum(dla * (-LRU_C) * r)
            dlams.append(dsp * -jax.nn.sigmoid(-lam_ref[d:d + 1, :]))
            dg_s[:, (2 * d) * LANES:(2 * d + 1) * LANES] = (dr * r * (1.0 - r)).astype(BF16)
            dg_s[:, (2 * d + 1) * LANES:(2 * d + 2) * LANES] = (di * i * (1.0 - i)).astype(BF16)
        dgates = dg_s[...]
        dxc = dxc + _dot_nt(dgates, w_ref[0])
        dxr_ref[...] = (cw_ref[0:1, :] * _shift_rows(dxc, -2, rows, t) + cw_ref[1:2, :] * _shift_rows(dxc, -1, rows, t)
                        + cw_ref[2:3, :] * dxc + cw_ref[3:4, :] * _shift_rows(dxc, 1, rows, t))
        dcw = jnp.concatenate([_colsum(dxc * _shift_rows(xr, 2 - j, rows, t)) for j in range(4)], axis=0)
        _acc(dcw_ref, first, dcw)
        _acc(dcb_ref, first, _colsum(dxc))
        _acc(dw_ref, first, _dot_tn(xcb, dgates)[None])
        _acc(dgb_ref, first, _colsum(dgates.astype(F32))[None])
        _acc(dlam_ref, first, jnp.concatenate(dlams, axis=0))

    return pl.pallas_call(
        body, name="rnn_bwd", grid=(D_RNN // LANES, n // t),
        in_specs=[seq, seq, seq, cw, vec1, wspec, gspec, vec2],
        out_specs=[seq, seq, cw, vec1, wspec, gspec, vec2],
        out_shape=[jax.ShapeDtypeStruct((n, D_RNN), F32), jax.ShapeDtypeStruct((n, D_RNN), F32),
                   jax.ShapeDtypeStruct((4, D_RNN), F32), jax.ShapeDtypeStruct((1, D_RNN), F32),
                   jax.ShapeDtypeStruct((D_RNN // LANES, LANES, 4 * LANES), F32),
                   jax.ShapeDtypeStruct((D_RNN // LANES, 1, 4 * LANES), F32), jax.ShapeDtypeStruct((2, D_RNN), F32)],
        scratch_shapes=[pltpu.VMEM((2, t, LANES), F32)] * 5 + [pltpu.VMEM((t, 4 * LANES), BF16)],
        compiler_params=pltpu.CompilerParams(dimension_semantics=("parallel", "arbitrary"), vmem_limit_bytes=VMEM_LIMIT),
    )(xr, xg, do, conv_w, conv_b, wblk, gbias, lam)


def _post(oa, orn, h0, tgt, ga, gr, g2, w_out, w_gate, w_up, w_down):
    n = oa.shape[0]
    tm = _row_tile(n)
    t = _t_pad()

    def body(oa_ref, or_ref, h0_ref, tgt_ref, ga_ref, gr_ref, g2_ref, wo_ref, wg_ref, wu_ref, wd_ref,
             doa_ref, dor_ref, dh1_ref, mix_ref, h1n_ref, act_ref, dgate_ref, dup_ref, dy_ref,
             loss_ref, dga_ref, dgr_ref, dg2_ref, gate_s, up_s):
        first = pl.program_id(0) == 0
        xa, ra = _rms(oa_ref[...], D_ATTN)
        xr, rr = _rms(or_ref[...], D_RNN)
        mix_ref[:, :D_ATTN] = (xa * ga_ref[...]).astype(BF16)
        mix_ref[:, D_ATTN:] = (xr * gr_ref[...]).astype(BF16)
        h1 = h0_ref[...] + _dot(mix_ref[...], wo_ref[...])
        x2, r2 = _rms(h1, D_MODEL)
        h1n = (x2 * g2_ref[...]).astype(BF16)
        h1n_ref[...] = h1n
        y = h1
        for cs in range(0, D_FF, FF_CHUNK):
            sl = slice(cs, cs + FF_CHUNK)
            gate = _dot_nt(h1n, wg_ref[sl, :])
            up = _dot_nt(h1n, wu_ref[sl, :])
            gate_s[:, sl] = gate
            up_s[:, sl] = up
            act = (gate * jax.nn.sigmoid(gate) * up).astype(BF16)
            act_ref[:, sl] = act
            y = y + _dot(act, wd_ref[sl, :])
        row = pl.program_id(0) * tm + lax.broadcasted_iota(jnp.int32, (tm, 1), 0)
        for _ in range(1, n // t):
            row = jnp.where(row >= t, row - t, row)
        err = jnp.where(row >= PAD_ROWS + N_META, y - tgt_ref[...], 0.0)
        _acc(loss_ref, first, jnp.full((1, LANES), 0.5 / D_MODEL, F32) * jnp.sum(err * err))
        dy = err * (1.0 / D_MODEL)
        dyb = dy.astype(BF16)
        dy_ref[...] = dyb
        dh1n = jnp.zeros((tm, D_MODEL), F32)
        for cs in range(0, D_FF, FF_CHUNK):
            sl = slice(cs, cs + FF_CHUNK)
            dact = _dot_nt(dyb, wd_ref[sl, :])
            gate, up = gate_s[:, sl], up_s[:, sl]
            sg = jax.nn.sigmoid(gate)
            dgate = (dact * up * sg * (1.0 + gate * (1.0 - sg))).astype(BF16)
            dup = (dact * gate * sg).astype(BF16)
            dgate_ref[:, sl] = dgate
            dup_ref[:, sl] = dup
            dh1n = dh1n + _dot(dgate, wg_ref[sl, :]) + _dot(dup, wu_ref[sl, :])
        _acc(dg2_ref, first, _colsum(dh1n * x2))
        dh1 = dy + _rms_bwd(dh1n, x2, r2, g2_ref[...], D_MODEL)
        dh1_ref[...] = dh1
        dmix = _dot_nt(dh1.astype(BF16), wo_ref[...])
        dma, dmr = dmix[:, :D_ATTN], dmix[:, D_ATTN:]
        _acc(dga_ref, first, _colsum(dma * xa))
        _acc(dgr_ref, first, _colsum(dmr * xr))
        doa_ref[...] = _rms_bwd(dma, xa, ra, ga_ref[...], D_ATTN)
        dor_ref[...] = _rms_bwd(dmr, xr, rr, gr_ref[...], D_RNN)

    def row(w):
        return pl.BlockSpec((tm, w), lambda i: (i, 0))

    def acc(w):
        return pl.BlockSpec((1, w), lambda i: (0, 0))

    outs = [(D_ATTN, F32), (D_RNN, F32), (D_MODEL, F32), (D_MODEL, BF16), (D_MODEL, BF16), (D_FF, BF16),
            (D_FF, BF16), (D_FF, BF16), (D_MODEL, BF16)]
    accs = [LANES, D_ATTN, D_RNN, D_MODEL]
    return pl.pallas_call(
        body, name="post", grid=(n // tm,),
        in_specs=[row(D_ATTN), row(D_RNN), row(D_MODEL), row(D_MODEL),
                  _const_spec((1, D_ATTN)), _const_spec((1, D_RNN)), _const_spec((1, D_MODEL)),
                  _const_spec((D_MODEL, D_MODEL)), _const_spec((D_FF, D_MODEL)), _const_spec((D_FF, D_MODEL)),
                  _const_spec((D_FF, D_MODEL))],
        out_specs=[row(w) for w, _ in outs] + [acc(w) for w in accs],
        out_shape=[jax.ShapeDtypeStruct((n, w), dt) for w, dt in outs]
        + [jax.ShapeDtypeStruct((1, w), F32) for w in accs],
        scratch_shapes=[pltpu.VMEM((tm, D_FF), F32), pltpu.VMEM((tm, D_FF), F32)],
        compiler_params=pltpu.CompilerParams(dimension_semantics=("arbitrary",), vmem_limit_bytes=VMEM_LIMIT),
    )(oa, orn, h0, tgt, ga, gr, g2, w_out, w_gate, w_up, w_down)


def _in_bwd(dp, h0, dh1, ln1_g, w_in_p, srcs=(), scatter=()):
    n = h0.shape[0]
    tm = _row_tile(n)
    nk = len(srcs)
    c_in, c_out, c_shape, c_sems = _exchange_specs(srcs, scatter)

    def body(dp_ref, h0_ref, dh1_ref, g_ref, w_ref, *rest):
        dh0_ref, dg_ref = rest[nk:nk + 2]
        finish = _ride(1, *_exchange_fns(rest[:nk], rest[nk + 2:2 * nk + 2], rest[2 * nk + 2:], scatter))
        dhn = _dot(dp_ref[...], w_ref[...])
        xhat, r = _rms(h0_ref[...], D_MODEL)
        _acc(dg_ref, pl.program_id(0) == 0, _colsum(dhn * xhat))
        dh0_ref[...] = dh1_ref[...] + _rms_bwd(dhn, xhat, r, g_ref[...], D_MODEL)
        finish()

    def row(w):
        return pl.BlockSpec((tm, w), lambda i: (i, 0))

    res = pl.pallas_call(
        body, name="in_bwd", grid=(n // tm,),
        in_specs=[row(P_COLS), row(D_MODEL), row(D_MODEL), _const_spec((1, D_MODEL)), _const_spec((P_COLS, D_MODEL))] + c_in,
        out_specs=[row(D_MODEL), pl.BlockSpec((1, D_MODEL), lambda i: (0, 0))] + c_out,
        out_shape=[jax.ShapeDtypeStruct((n, D_MODEL), F32), jax.ShapeDtypeStruct((1, D_MODEL), F32)] + c_shape,
        scratch_shapes=c_sems,
        compiler_params=pltpu.CompilerParams(dimension_semantics=("arbitrary",), vmem_limit_bytes=VMEM_LIMIT),
    )(dp, h0, dh1, ln1_g, w_in_p, *srcs)
    return res[:2], res[2:]


def _pick_tile(width, cap):
    best = LANES
    for mult in range(1, width // LANES + 1):
        cand = mult * LANES
        if width % cand == 0 and cand <= cap:
            best = cand
    return best


def _matmul_tn(name, a, b):
    n, ka = a.shape
    kb = b.shape[1]
    ta, tb = _pick_tile(ka, 1408), _pick_tile(kb, 1408)
    tk = n // 4

    def body(a_ref, b_ref, o_ref):
        _acc(o_ref, pl.program_id(2) == 0, _dot_tn(a_ref[...].astype(BF16), b_ref[...].astype(BF16)))

    return pl.pallas_call(
        body, name=name, grid=(ka // ta, kb // tb, n // tk),
        in_specs=[pl.BlockSpec((tk, ta), lambda i, j, k: (k, i)), pl.BlockSpec((tk, tb), lambda i, j, k: (k, j))],
        out_specs=pl.BlockSpec((ta, tb), lambda i, j, k: (i, j)),
        out_shape=jax.ShapeDtypeStruct((ka, kb), F32),
        compiler_params=pltpu.CompilerParams(dimension_semantics=("parallel", "parallel", "arbitrary"),
                                             vmem_limit_bytes=VMEM_LIMIT),
    )(a, b)


def _matmul_tn_shards(name, a, b):
    n, ka = a.shape
    kb = b.shape[1]
    ta, tb = _pick_tile(ka, 1408), _pick_tile(kb, 1408)
    tk = n // 4
    width = ka // N_DEV
    per = ta // width

    def body(a_ref, b_ref, o_ref, acc_ref):
        _acc(acc_ref, pl.program_id(2) == 0, _dot_tn(a_ref[...].astype(BF16), b_ref[...].astype(BF16)))

        @pl.when(pl.program_id(2) == pl.num_programs(2) - 1)
        def _():
            for s in range(per):
                o_ref[s] = acc_ref[s * width:(s + 1) * width, :].astype(BF16)

    return pl.pallas_call(
        body, name=name, grid=(ka // ta, kb // tb, n // tk),
        in_specs=[pl.BlockSpec((tk, ta), lambda i, j, k: (k, i)), pl.BlockSpec((tk, tb), lambda i, j, k: (k, j))],
        out_specs=pl.BlockSpec((per, width, tb), lambda i, j, k: (i, 0, j)),
        out_shape=jax.ShapeDtypeStruct((N_DEV, width, kb), BF16),
        scratch_shapes=[pltpu.VMEM((ta, tb), F32)],
        compiler_params=pltpu.CompilerParams(dimension_semantics=("parallel", "parallel", "arbitrary"),
                                             vmem_limit_bytes=VMEM_LIMIT),
    )(a, b)


def _adamw_math(g8_ref, w_ref, m_ref, v_ref, g_ref, d_ref, nm_ref, nv_ref):
    g = g8_ref[0].astype(F32)
    for s in range(1, N_DEV):
        g = g + g8_ref[s].astype(F32)
    g_ref[...] = g
    nm = ADAM_B1 * m_ref[...] + (1.0 - ADAM_B1) * g
    nv = ADAM_B2 * v_ref[...] + (1.0 - ADAM_B2) * (g * g)
    nm_ref[...] = nm
    nv_ref[...] = nv
    m_hat = nm / (1.0 - ADAM_B1 ** ADAM_STEP)
    v_hat = nv / (1.0 - ADAM_B2 ** ADAM_STEP)
    d_ref[...] = -ADAM_LR * (m_hat / (jnp.sqrt(v_hat) + ADAM_EPS) + ADAM_WD * w_ref[...])


def _adamw_many(name, items):
    count = len(items)

    def body(*refs):
        ins, outs = refs[:4 * count], refs[4 * count:]
        for i in range(count):
            _adamw_math(*ins[4 * i:4 * i + 4], *outs[4 * i:4 * i + 4])

    flat = [a for item in items for a in item]
    res = pl.pallas_call(
        body, name=name,
        out_shape=[jax.ShapeDtypeStruct(item[1].shape, F32) for item in items for _ in range(4)],
        compiler_params=pltpu.CompilerParams(vmem_limit_bytes=VMEM_LIMIT),
    )(*flat)
    return [tuple(res[4 * i:4 * i + 4]) for i in range(count)]


def _adamw(name, g8, w, m, v):
    rows, cols = w.shape
    tr = rows
    for cand in (256, 176, 128, 64):
        if rows % cand == 0 and rows > cand:
            tr = cand
            break

    def body(*refs):
        _adamw_math(*refs)

    blk = pl.BlockSpec((tr, cols), lambda i: (i, 0))
    return pl.pallas_call(
        body, name=name, grid=(rows // tr,),
        in_specs=[pl.BlockSpec((N_DEV, tr, cols), lambda i: (0, i, 0)), blk, blk, blk],
        out_specs=[blk] * 4, out_shape=[jax.ShapeDtypeStruct((rows, cols), F32)] * 4,
        compiler_params=pltpu.CompilerParams(dimension_semantics=("parallel",), vmem_limit_bytes=VMEM_LIMIT),
    )(g8, w, m, v)


def _exchange_specs(srcs, scatter):
    nk = len(srcs)
    if not nk:
        return [], [], [], []
    any_spec = pl.BlockSpec(memory_space=pl.ANY)
    out_shape = [jax.ShapeDtypeStruct(s.shape if sc else (N_DEV,) + s.shape, s.dtype) for s, sc in zip(srcs, scatter)]
    sems = [pltpu.SemaphoreType.DMA((nk, N_DEV - 1)), pltpu.SemaphoreType.DMA((nk, N_DEV - 1)),
            pltpu.SemaphoreType.DMA((nk,))]
    return [any_spec] * nk, [any_spec] * nk, out_shape, sems


FLIPS = ((0, 0, 1), (1, 0, 0), (0, 1, 0), (1, 1, 0), (1, 0, 1), (0, 1, 1), (1, 1, 1))
N_CHIP_PEERS = 3


def _exchange_fns(src_refs, out_refs, sems, scatter):
    nk = len(src_refs)
    if not nk:
        return (lambda: None), (lambda: None), (lambda: None)
    send_sems, recv_sems, local_sems = sems
    first = 1 + N_CHIP_PEERS

    def plan():
        x, y, c = lax.axis_index("x"), lax.axis_index("y"), lax.axis_index("c")
        me = 4 * x + 2 * y + c
        peers = [(1 - x if fx else x, 1 - y if fy else y, 1 - c if fc else c) for fx, fy, fc in FLIPS]
        pids = [4 * px + 2 * py + pc for px, py, pc in peers]

        def remote(k, j, src, dst, to):
            return pltpu.make_async_remote_copy(src_ref=src, dst_ref=dst, send_sem=send_sems.at[k, j],
                                                recv_sem=recv_sems.at[k, j], device_id=to, device_id_type=MESH)

        def mine(k, dest):
            return src_refs[k].at[dest] if scatter[k] else src_refs[k]

        local = [pltpu.make_async_copy(mine(k, me), out_refs[k].at[me], local_sems.at[k]) for k in range(nk)]
        direct = [remote(k, j, mine(k, pids[j]), out_refs[k].at[me], peers[j])
                  for k in range(nk) for j in range(len(FLIPS) if scatter[k] else first)]
        relays = {(k, j): remote(k, j, out_refs[k].at[pids[j - N_CHIP_PEERS]], out_refs[k].at[pids[j - N_CHIP_PEERS]], peers[0])
                  for k in range(nk) if not scatter[k] for j in range(first, len(FLIPS))}
        arrivals = {(k, j): remote(k, j, out_refs[k].at[pids[j]], out_refs[k].at[pids[j]], peers[j])
                    for k in range(nk) for j in range(len(FLIPS))}
        return local, direct, relays, arrivals

    def start():
        local, direct, _, _ = plan()
        for cp in local + direct:
            cp.start()

    def relay():
        _, _, relays, arrivals = plan()
        for (k, j), cp in relays.items():
            arrivals[k, j - N_CHIP_PEERS].wait_recv()
            cp.start()

    def wait():
        local, direct, relays, arrivals = plan()
        for (k, j), cp in arrivals.items():
            if (k, j + N_CHIP_PEERS) not in relays:
                cp.wait_recv()
        for cp in direct + list(relays.values()):
            cp.wait_send()
        for cp in local:
            cp.wait()

    return start, relay, wait


def _grid_step(rank):
    step, total = 0, 1
    for axis in range(rank):
        step = step * pl.num_programs(axis) + pl.program_id(axis)
        total = total * pl.num_programs(axis)
    return step, total


def _ride(rank, start, relay, wait):
    step, total = _grid_step(rank)
    pl.when(step == 0)(start)
    pl.when(step == (3 * total) // 4)(relay)
    return lambda: pl.when(step == total - 1)(wait)


def _exchange(name, srcs, scatter):
    nk = len(srcs)
    c_in, c_out, c_shape, c_sems = _exchange_specs(srcs, scatter)

    def body(*refs):
        start, relay, wait = _exchange_fns(refs[:nk], refs[nk:2 * nk], refs[2 * nk:], scatter)
        start()
        relay()
        wait()

    return pl.pallas_call(body, name=name, in_specs=c_in, out_specs=c_out, out_shape=c_shape, scratch_shapes=c_sems)(*srcs)


def _cols_from_shards(g):
    return jnp.transpose(g, (1, 0, 2)).reshape(g.shape[1], -1)


def _cols_to_shards(w):
    return jnp.transpose(w.reshape(w.shape[0], N_DEV, -1), (1, 0, 2))


def _rope_tables(n):
    t = _t_pad()
    pos = (jnp.arange(t, dtype=F32) - PAD_ROWS)
    half = QK_ROPE // 2
    freqs = 1.0 / (ROPE_THETA ** (jnp.arange(half, dtype=F32) / half))
    ang = pos[:, None] * freqs[None, :]
    cos, sin = jnp.cos(ang), jnp.sin(ang)
    z = lambda w: jnp.zeros((t, w), F32)
    c = jnp.concatenate([jnp.ones((t, QK_NOPE), F32), cos, cos, z(HEAD_PAD - QK_HEAD)], axis=1)
    s1 = jnp.concatenate([z(QK_NOPE + half), sin, z(HEAD_PAD - QK_HEAD)], axis=1)
    s2 = jnp.concatenate([z(QK_NOPE), -sin, z(HEAD_PAD - QK_NOPE - half)], axis=1)
    reps = n // t
    return tuple(jnp.tile(a, (reps, 1)) for a in (c, s1, s2))


def _block_diag_gates(lru_wa, lru_wi):
    eye = jnp.eye(2, dtype=lru_wa.dtype)

    def bd(w):
        w = w.reshape(2, D_RNN // LANES, 2, RNN_BW, RNN_BW)
        full = w[:, :, :, :, None, :] * eye[None, None, :, None, :, None]
        return full.reshape(2, D_RNN // LANES, LANES, LANES)

    a, i = bd(lru_wa), bd(lru_wi)
    return jnp.concatenate([a[0], i[0], a[1], i[1]], axis=-1)


def _unblock_gates(dw):
    nb = D_RNN // LANES
    parts = dw.reshape(nb, 2, RNN_BW, 4, 2, RNN_BW)
    diag = jnp.stack([parts[:, k, :, :, k, :] for k in range(2)], axis=1)
    diag = jnp.transpose(diag, (3, 0, 1, 2, 4)).reshape(4, 2 * nb, RNN_BW, RNN_BW)
    return jnp.stack([diag[0], diag[2]]), jnp.stack([diag[1], diag[3]])


WEIGHTS = ("meta_tokens", "ln1_g", "w_in", "q_a_norm_g", "w_uq", "kv_a_norm_g", "w_ukv", "q_norm_g", "k_norm_g",
           "conv_w", "conv_b", "lru_wa", "lru_ba", "lru_wi", "lru_bi", "lru_lambda", "attn_out_g", "rnn_out_g",
           "w_out", "ln2_g", "w_gate", "w_up", "w_down")
BIG = ("w_in", "w_uq", "w_ukv", "w_out", "w_gate", "w_up", "w_down")
TRANSPOSED = ("w_in", "w_uq", "w_gate", "w_up")
ROW_SHARDED = ("w_out", "w_down") + TRANSPOSED
REPLICATED = ("ln1_g", "q_a_norm_g", "kv_a_norm_g", "q_norm_g", "k_norm_g", "conv_b", "lru_wa", "lru_wi",
              "attn_out_g", "rnn_out_g", "ln2_g")
G_FIRST = ("w_in", "meta_tokens")
G_MID = ("w_uq", "w_ukv", "conv_w", "lru_ba", "lru_bi", "lru_lambda")
LATE = ("w_out", "w_gate", "w_up", "w_down")
G_LAST = ("meta_tokens", "ln1_g")


def _local_step(x, tgt, ex):
    nb = x.shape[0]
    t = _t_pad()
    n = nb * t
    local = ex.local
    first = ex.gathered(G_FIRST, ex.run("gather_first", *ex.gather_srcs(G_FIRST)))
    meta, w_in = first["meta_tokens"], first["w_in"]
    lead = jnp.zeros((nb, PAD_ROWS, D_MODEL), F32)
    h0 = jnp.concatenate([lead, jnp.broadcast_to(meta[None], (nb, N_META, D_MODEL)), x], axis=1).reshape(n, D_MODEL)
    tgt_p = jnp.concatenate([jnp.zeros((nb, PAD_ROWS + N_META, D_MODEL), F32), tgt], axis=1).reshape(n, D_MODEL)

    zr = lambda r: jnp.zeros((r, D_MODEL), w_in.dtype)
    w_in_p = jnp.concatenate([w_in[:OFF_CKV], w_in[OFF_KR:], zr(QK_NOPE), w_in[OFF_CKV:OFF_KR], zr(HEAD_PAD - QK_HEAD)],
                             axis=0)
    pad_g = lambda g: jnp.pad(g, ((0, 0), (0, HEAD_PAD - QK_HEAD)))
    qg, kg = pad_g(local["q_norm_g"]), pad_g(local["k_norm_g"])
    rc, rs1, rs2 = _rope_tables(n)
    wblk = _block_diag_gates(local["lru_wa"].reshape(2, -1, RNN_BW, RNN_BW),
                             local["lru_wi"].reshape(2, -1, RNN_BW, RNN_BW)).astype(BF16)
    nblk = D_RNN // LANES

    (hn, cq, ckv, xr, xg, kr), got = _in_proj(h0, local["ln1_g"], w_in_p, *ex.gather_srcs(G_MID))
    w = ex.gathered(G_MID, got)
    w_uq_p = jnp.pad(w["w_uq"].reshape(N_HEADS, QK_HEAD, Q_LORA), ((0, 0), (0, HEAD_PAD - QK_HEAD), (0, 0))
                     ).reshape(QP_COLS, Q_LORA)
    ukv = w["w_ukv"].reshape(KV_LORA, N_HEADS, QK_NOPE + V_HEAD)
    w_uk_p = jnp.pad(ukv[:, :, :QK_NOPE], ((0, 0), (0, 0), (0, HEAD_PAD - QK_NOPE))).reshape(KV_LORA, QP_COLS)
    w_v = ukv[:, :, QK_NOPE:].reshape(KV_LORA, D_ATTN)
    gbias = jnp.stack([w["lru_ba"][0], w["lru_bi"][0], w["lru_ba"][1], w["lru_bi"][1]], axis=0)
    gbias = jnp.transpose(gbias.reshape(4, nblk, LANES), (1, 0, 2)).reshape(nblk, 1, 4 * LANES)

    q, k, v = _qkv_fwd(cq, ckv, kr, local["q_a_norm_g"], local["kv_a_norm_g"], w_uq_p, w_uk_p, w_v, qg, kg, rc, rs1, rs2)
    oa, got = _attn_fwd(q, k, v, *ex.gather_srcs(LATE))
    late = ex.gathered(LATE, got)
    orn = _rnn_fwd(xr, xg, w["conv_w"], local["conv_b"], wblk, gbias, w["lru_lambda"])
    (doa, dor, dh1, mix, h1n, act, dgate, dup, dyb, loss, dga, dgr, dg2) = _post(
        oa, orn, h0, tgt_p, local["attn_out_g"], local["rnn_out_g"], local["ln2_g"], late["w_out"], late["w_gate"],
        late["w_up"], late["w_down"])
    wire = {"w_out": _matmul_tn_shards("dw_out", mix, dh1), "w_gate": _matmul_tn_shards("dw_gate", dgate, h1n),
            "w_up": _matmul_tn_shards("dw_up", dup, h1n), "w_down": _matmul_tn_shards("dw_down", act, dyb)}
    dxr, dxg, dcw, dcb, dwblk, dgb, dlam = _rnn_bwd(xr, xg, dor, w["conv_w"], local["conv_b"], wblk, gbias, w["lru_lambda"])
    dwa, dwi = _unblock_gates(dwblk)
    dgb = jnp.transpose(dgb.reshape(nblk, 4, LANES), (1, 0, 2)).reshape(4, D_RNN)
    wire.update(ex.to_wire({
        "conv_w": dcw, "conv_b": dcb, "lru_wa": dwa.reshape(-1, RNN_BW), "lru_ba": jnp.stack([dgb[0], dgb[2]]),
        "lru_wi": dwi.reshape(-1, RNN_BW), "lru_bi": jnp.stack([dgb[1], dgb[3]]), "lru_lambda": dlam,
        "attn_out_g": dga, "rnn_out_g": dgr, "ln2_g": dg2}))
    names = tuple(wire)
    (dq_r, dk_r, dv), got = _attn_bwd(q, k, v, doa, *ex.scatter_srcs(names, wire))
    summed = ex.scattered(names, wire, got)
    (dp, qa, kva, dqp, dkv, dqg, dkg, dgqa, dgkva) = _qkv_bwd(
        cq, ckv, kr, dq_r, dk_r, dv, dxr, dxg, local["q_a_norm_g"], local["kv_a_norm_g"], w_uq_p, w_uk_p, w_v, qg, kg,
        rc, rs1, rs2)
    dw_in_p = _matmul_tn("dw_in", dp, hn)
    dw_uq_p = _matmul_tn("dw_uq", dqp, qa)
    dw_kv = _matmul_tn("dw_ukv", kva, dkv)
    kr0 = OFF_CKV + 2 * D_RNN + QK_NOPE
    dw_in = jnp.concatenate([dw_in_p[:OFF_CKV], dw_in_p[kr0:kr0 + QK_ROPE], dw_in_p[OFF_CKV:OFF_CKV + 2 * D_RNN]], axis=0)
    dw_uq = dw_uq_p.reshape(N_HEADS, HEAD_PAD, Q_LORA)[:, :QK_HEAD].reshape(N_HEADS * QK_HEAD, Q_LORA)
    dw_ukv = jnp.concatenate([dw_kv[:, :QP_COLS].reshape(KV_LORA, N_HEADS, HEAD_PAD)[:, :, :QK_NOPE],
                              dw_kv[:, QP_COLS:].reshape(KV_LORA, N_HEADS, V_HEAD)], axis=2).reshape(KV_LORA, -1)
    wire = ex.to_wire({"w_in": dw_in, "q_a_norm_g": dgqa, "w_uq": dw_uq, "kv_a_norm_g": dgkva, "w_ukv": dw_ukv,
                       "q_norm_g": dqg[:, :QK_HEAD], "k_norm_g": dkg[:, :QK_HEAD]})
    names = tuple(wire)
    (dh0, dg1), got = _in_bwd(dp, h0, dh1, local["ln1_g"], w_in_p, *ex.scatter_srcs(names, wire))
    summed.update(ex.scattered(names, wire, got))

    dh0 = dh0.reshape(nb, t, D_MODEL)
    wire = ex.to_wire({"meta_tokens": jnp.sum(dh0[:, PAD_ROWS:PAD_ROWS + N_META], axis=0), "ln1_g": dg1})
    got = ex.run("reduce_last", *ex.scatter_srcs(G_LAST, wire))
    summed.update(ex.scattered(G_LAST, wire, got))
    return loss[0, 0], dh0[:, PAD_ROWS + N_META:], summed


class _MeshExchange:
    def __init__(self, shards):
        self.local = shards

    @staticmethod
    def run(name, srcs, scatter):
        return _exchange(name, srcs, scatter)

    def gather_srcs(self, names):
        return [self.local[k].astype(BF16) if k in BIG else self.local[k] for k in names], [False] * len(names)

    @staticmethod
    def gathered(names, outs):
        return {k: g.reshape(-1, g.shape[-1]) if k in ROW_SHARDED else _cols_from_shards(g) for k, g in zip(names, outs)}

    @staticmethod
    def to_wire(grads):
        wire = {}
        for k, g in grads.items():
            if k in REPLICATED:
                wire[k] = g
            elif k in ROW_SHARDED:
                wire[k] = g.reshape(N_DEV, -1, g.shape[-1]).astype(BF16)
            else:
                wire[k] = _cols_to_shards(g).astype(BF16) if k in BIG else _cols_to_shards(g)
        return wire

    @staticmethod
    def scatter_srcs(names, wire):
        return [wire[k] for k in names], [k not in REPLICATED for k in names]

    @staticmethod
    def scattered(names, wire, outs):
        return dict(zip(names, outs))


def kernel(x, meta_tokens, ln1_g, w_in, q_a_norm_g, w_uq, kv_a_norm_g, w_ukv, q_norm_g, k_norm_g, conv_w, conv_b, lru_wa, lru_ba, lru_wi, lru_bi, lru_lambda, attn_out_g, rnn_out_g, w_out, ln2_g, w_gate, w_up, w_down, loss_target, m_meta_tokens, m_ln1_g, m_w_in, m_q_a_norm_g, m_w_uq, m_kv_a_norm_g, m_w_ukv, m_q_norm_g, m_k_norm_g, m_conv_w, m_conv_b, m_lru_wa, m_lru_ba, m_lru_wi, m_lru_bi, m_lru_lambda, m_attn_out_g, m_rnn_out_g, m_w_out, m_ln2_g, m_w_gate, m_w_up, m_w_down, v_meta_tokens, v_ln1_g, v_w_in, v_q_a_norm_g, v_w_uq, v_kv_a_norm_g, v_w_ukv, v_q_norm_g, v_k_norm_g, v_conv_w, v_conv_b, v_lru_wa, v_lru_ba, v_lru_wi, v_lru_bi, v_lru_lambda, v_attn_out_g, v_rnn_out_g, v_w_out, v_ln2_g, v_w_gate, v_w_up, v_w_down):
    given = (meta_tokens, ln1_g, w_in, q_a_norm_g, w_uq, kv_a_norm_g, w_ukv, q_norm_g, k_norm_g, conv_w, conv_b,
             lru_wa, lru_ba, lru_wi, lru_bi, lru_lambda, attn_out_g, rnn_out_g, w_out, ln2_g, w_gate, w_up, w_down)
    moments_m = (m_meta_tokens, m_ln1_g, m_w_in, m_q_a_norm_g, m_w_uq, m_kv_a_norm_g, m_w_ukv, m_q_norm_g, m_k_norm_g,
                 m_conv_w, m_conv_b, m_lru_wa, m_lru_ba, m_lru_wi, m_lru_bi, m_lru_lambda, m_attn_out_g, m_rnn_out_g,
                 m_w_out, m_ln2_g, m_w_gate, m_w_up, m_w_down)
    moments_v = (v_meta_tokens, v_ln1_g, v_w_in, v_q_a_norm_g, v_w_uq, v_kv_a_norm_g, v_w_ukv, v_q_norm_g, v_k_norm_g,
                 v_conv_w, v_conv_b, v_lru_wa, v_lru_ba, v_lru_wi, v_lru_bi, v_lru_lambda, v_attn_out_g, v_rnn_out_g,
                 v_w_out, v_ln2_g, v_w_gate, v_w_up, v_w_down)
    shapes = {k: a.shape for k, a in zip(WEIGHTS, given)}

    def two_d(k, a):
        a = a.reshape(-1, a.shape[-1])
        return a.T if k in TRANSPOSED else a

    w = {k: two_d(k, a) for k, a in zip(WEIGHTS, given)}
    m = {k: two_d(k, a) for k, a in zip(WEIGHTS, moments_m)}
    v = {k: two_d(k, a) for k, a in zip(WEIGHTS, moments_v)}

    loss_part, grad_x, parts = _local_step(x, loss_target, _MeshExchange(w))

    new = {k: _adamw("adamw_" + k, parts[k], w[k], m[k], v[k]) for k in BIG}
    small = [k for k in WEIGHTS if k not in BIG]
    new.update(zip(small, _adamw_many("adamw_small", [(parts[k], w[k], m[k], v[k]) for k in small])))

    loss = lax.psum(loss_part, ("x", "y", "c"))
    outs = [loss, grad_x]
    for idx in range(4):
        outs += [(new[k][idx].T if k in TRANSPOSED else new[k][idx]).reshape(shapes[k]) for k in WEIGHTS]
    return tuple(outs)
```

```python
import functools
import math

import jax
import jax.numpy as jnp
from jax import lax
from jax.experimental import pallas as pl
from jax.experimental.pallas import tpu as pltpu

F32 = jnp.float32
BF16 = jnp.bfloat16

D_MODEL = 1024
N_META = 16
SEQ = 2048
N_HEADS = 8
QK_NOPE = 64
QK_ROPE = 32
QK_HEAD = QK_NOPE + QK_ROPE
V_HEAD = 64
D_ATTN = N_HEADS * V_HEAD
Q_LORA = 384
KV_LORA = 256
D_RNN = 512
RNN_BW = 64
D_FF = 2816
EPS = 1e-6
LRU_C = 8.0
ROPE_THETA = 10000.0
OFF_CKV = Q_LORA + KV_LORA
OFF_KR = OFF_CKV + QK_ROPE
IN_COLS = OFF_KR + 2 * D_RNN

ADAM_LR = 0.001
ADAM_B1 = 0.9
ADAM_B2 = 0.999
ADAM_EPS = 1e-08
ADAM_WD = 0.01
ADAM_STEP = 10

N_DEV = 8
LANES = 128
HEAD_PAD = LANES
PAD_ROWS = LANES - N_META
QP_COLS = N_HEADS * HEAD_PAD
P_COLS = OFF_CKV + 2 * D_RNN + LANES
FF_CHUNK = D_FF // 2
VMEM_LIMIT = 56 * 1024 * 1024
MESH = pl.DeviceIdType.MESH


def _t_pad():
    return PAD_ROWS + N_META + SEQ


def _row_tile(n):
    return 256 if n % 256 == 0 else 128


def _const_spec(shape):
    nd = len(shape)
    return pl.BlockSpec(shape, lambda *_: (0,) * nd, pipeline_mode=pl.Buffered(1))


def _rms(x, d):
    r = lax.rsqrt(jnp.sum(x * x, axis=-1, keepdims=True) * (1.0 / d) + EPS)
    return x * r, r


def _rms_bwd(dy, xhat, r, g, d):
    dxh = dy * g
    return r * (dxh - xhat * (jnp.sum(dxh * xhat, axis=-1, keepdims=True) * (1.0 / d)))


def _colsum(x):
    return jnp.sum(x, axis=0, keepdims=True)


def _dot(a, b):
    return jnp.dot(a, b, preferred_element_type=F32)


def _dot_nt(a, b):
    return lax.dot_general(a, b, (((1,), (1,)), ((), ())), preferred_element_type=F32)


def _dot_tn(a, b):
    return lax.dot_general(a, b, (((0,), (0,)), ((), ())), preferred_element_type=F32)


def _rope(x, c, s1, s2):
    return x * c + pltpu.roll(x, 16, 1) * s1 + pltpu.roll(x, HEAD_PAD - 16, 1) * s2


def _rope_bwd(dy, c, s1, s2):
    return dy * c + pltpu.roll(dy * s1, HEAD_PAD - 16, 1) + pltpu.roll(dy * s2, 16, 1)


def _acc(ref, first, val):
    @pl.when(first)
    def _():
        ref[...] = val

    @pl.when(jnp.logical_not(first))
    def _():
        ref[...] += val


def _in_proj(h0, ln1_g, w_in_p, srcs=(), scatter=()):
    n = h0.shape[0]
    tm = _row_tile(n)
    nk = len(srcs)
    c_in, c_out, c_shape, c_sems = _exchange_specs(srcs, scatter)

    def body(h_ref, g_ref, w_ref, *rest):
        hn_ref, cq_ref, ckv_ref, xr_ref, xg_ref, kr_ref = rest[nk:nk + 6]
        finish = _ride(1, *_exchange_fns(rest[:nk], rest[nk + 6:2 * nk + 6], rest[2 * nk + 6:], scatter))
        xhat, _ = _rms(h_ref[...], D_MODEL)
        hn = (xhat * g_ref[...]).astype(BF16)
        hn_ref[...] = hn
        p = _dot_nt(hn, w_ref[...])
        cq_ref[...] = p[:, :Q_LORA]
        ckv_ref[...] = p[:, Q_LORA:OFF_CKV]
        xr_ref[...] = p[:, OFF_CKV:OFF_CKV + D_RNN]
        xg_ref[...] = p[:, OFF_CKV + D_RNN:OFF_CKV + 2 * D_RNN]
        kr_ref[...] = p[:, OFF_CKV + 2 * D_RNN:]
        finish()

    def row(w):
        return pl.BlockSpec((tm, w), lambda i: (i, 0))

    widths = (D_MODEL, Q_LORA, KV_LORA, D_RNN, D_RNN, LANES)
    res = pl.pallas_call(
        body, name="in_proj", grid=(n // tm,),
        in_specs=[row(D_MODEL), _const_spec((1, D_MODEL)), _const_spec((P_COLS, D_MODEL))] + c_in,
        out_specs=[row(w) for w in widths] + c_out,
        out_shape=[jax.ShapeDtypeStruct((n, w), BF16 if k == 0 else F32) for k, w in enumerate(widths)] + c_shape,
        scratch_shapes=c_sems,
        compiler_params=pltpu.CompilerParams(dimension_semantics=("arbitrary",), vmem_limit_bytes=VMEM_LIMIT),
    )(h0, ln1_g, w_in_p, *srcs)
    return res[:6], res[6:]


def _qkv_fwd(cq, ckv, kr, gqa, gkva, w_uq_p, w_uk_p, w_v, qg, kg, rc, rs1, rs2):
    n = cq.shape[0]
    tm = _row_tile(n)

    def body(cq_ref, ckv_ref, kr_ref, gqa_ref, gkva_ref, wuq_ref, wuk_ref, wv_ref, qg_ref, kg_ref,
             c_ref, s1_ref, s2_ref, q_ref, k_ref, v_ref):
        xq, _ = _rms(cq_ref[...], Q_LORA)
        qa = (xq * gqa_ref[...]).astype(BF16)
        q = _dot_nt(qa, wuq_ref[...])
        xkv, _ = _rms(ckv_ref[...], KV_LORA)
        kva = (xkv * gkva_ref[...]).astype(BF16)
        kn = _dot(kva, wuk_ref[...])
        v_ref[...] = _dot(kva, wv_ref[...]).astype(BF16)
        krp = kr_ref[...]
        c, s1, s2 = c_ref[...], s1_ref[...], s2_ref[...]
        for h in range(N_HEADS):
            sl = slice(h * HEAD_PAD, (h + 1) * HEAD_PAD)
            qh, _ = _rms(q[:, sl], QK_HEAD)
            q_ref[:, sl] = _rope(qh * qg_ref[...], c, s1, s2).astype(BF16)
            kh, _ = _rms(kn[:, sl] + krp, QK_HEAD)
            k_ref[:, sl] = _rope(kh * kg_ref[...], c, s1, s2).astype(BF16)

    def row(w):
        return pl.BlockSpec((tm, w), lambda i: (i, 0))

    return pl.pallas_call(
        body, name="qkv_fwd", grid=(n // tm,),
        in_specs=[row(Q_LORA), row(KV_LORA), row(LANES), _const_spec((1, Q_LORA)), _const_spec((1, KV_LORA)),
                  _const_spec((QP_COLS, Q_LORA)), _const_spec((KV_LORA, QP_COLS)), _const_spec((KV_LORA, D_ATTN)),
                  _const_spec((1, LANES)), _const_spec((1, LANES)), row(LANES), row(LANES), row(LANES)],
        out_specs=[row(QP_COLS), row(QP_COLS), row(D_ATTN)],
        out_shape=[jax.ShapeDtypeStruct((n, QP_COLS), BF16), jax.ShapeDtypeStruct((n, QP_COLS), BF16),
                   jax.ShapeDtypeStruct((n, D_ATTN), BF16)],
        compiler_params=pltpu.CompilerParams(dimension_semantics=("parallel",), vmem_limit_bytes=VMEM_LIMIT),
    )(cq, ckv, kr, gqa, gkva, w_uq_p, w_uk_p, w_v, qg, kg, rc, rs1, rs2)


def _qkv_bwd(cq, ckv, kr, dq_r, dk_r, dv, dxr, dxg, gqa, gkva, w_uq_p, w_uk_p, w_v, qg, kg, rc, rs1, rs2):
    n = cq.shape[0]
    tm = _row_tile(n)

    def body(cq_ref, ckv_ref, kr_ref, dq_ref, dk_ref, dv_ref, dxr_ref, dxg_ref, gqa_ref, gkva_ref, wuq_ref, wuk_ref,
             wv_ref, qg_ref, kg_ref, c_ref, s1_ref, s2_ref,
             dp_ref, qa_ref, kva_ref, dqp_ref, dkv_ref, dqg_ref, dkg_ref, dgqa_ref, dgkva_ref):
        first = pl.program_id(0) == 0
        dp_ref[:, OFF_CKV:OFF_CKV + D_RNN] = dxr_ref[...].astype(BF16)
        dp_ref[:, OFF_CKV + D_RNN:OFF_CKV + 2 * D_RNN] = dxg_ref[...].astype(BF16)
        xq, rq = _rms(cq_ref[...], Q_LORA)
        qa = (xq * gqa_ref[...]).astype(BF16)
        qa_ref[...] = qa
        q = _dot_nt(qa, wuq_ref[...])
        xkv, rkv = _rms(ckv_ref[...], KV_LORA)
        kva = (xkv * gkva_ref[...]).astype(BF16)
        kva_ref[...] = kva
        kn = _dot(kva, wuk_ref[...])
        krp = kr_ref[...]
        c, s1, s2 = c_ref[...], s1_ref[...], s2_ref[...]
        lane = lax.broadcasted_iota(jnp.int32, (tm, HEAD_PAD), 1)
        rope_lanes = jnp.logical_and(lane >= QK_NOPE, lane < QK_HEAD)
        dqg = jnp.zeros((1, HEAD_PAD), F32)
        dkg = jnp.zeros((1, HEAD_PAD), F32)
        dkr = jnp.zeros((tm, HEAD_PAD), F32)
        for h in range(N_HEADS):
            sl = slice(h * HEAD_PAD, (h + 1) * HEAD_PAD)
            qh, rqh = _rms(q[:, sl], QK_HEAD)
            dy = _rope_bwd(dq_ref[:, sl], c, s1, s2)
            dqg = dqg + _colsum(dy * qh)
            dqp_ref[:, sl] = _rms_bwd(dy, qh, rqh, qg_ref[...], QK_HEAD).astype(BF16)
            kh, rkh = _rms(kn[:, sl] + krp, QK_HEAD)
            dyk = _rope_bwd(dk_ref[:, sl], c, s1, s2)
            dkg = dkg + _colsum(dyk * kh)
            dkh = _rms_bwd(dyk, kh, rkh, kg_ref[...], QK_HEAD)
            dkv_ref[:, sl] = dkh.astype(BF16)
            dkr = dkr + jnp.where(rope_lanes, dkh, 0.0)
        dkv_ref[:, QP_COLS:] = dv_ref[...].astype(BF16)
        dp_ref[:, OFF_CKV + 2 * D_RNN:] = dkr.astype(BF16)
        dqa = _dot(dqp_ref[...], wuq_ref[...])
        dp_ref[:, :Q_LORA] = _rms_bwd(dqa, xq, rq, gqa_ref[...], Q_LORA).astype(BF16)
        dkva = _dot_nt(dkv_ref[:, :QP_COLS], wuk_ref[...]) + _dot_nt(dkv_ref[:, QP_COLS:], wv_ref[...])
        dp_ref[:, Q_LORA:OFF_CKV] = _rms_bwd(dkva, xkv, rkv, gkva_ref[...], KV_LORA).astype(BF16)
        _acc(dqg_ref, first, dqg)
        _acc(dkg_ref, first, dkg)
        _acc(dgqa_ref, first, _colsum(dqa * xq))
        _acc(dgkva_ref, first, _colsum(dkva * xkv))

    def row(w):
        return pl.BlockSpec((tm, w), lambda i: (i, 0))

    def acc(w):
        return pl.BlockSpec((1, w), lambda i: (0, 0))

    return pl.pallas_call(
        body, name="qkv_bwd", grid=(n // tm,),
        in_specs=[row(Q_LORA), row(KV_LORA), row(LANES), row(QP_COLS), row(QP_COLS), row(D_ATTN), row(D_RNN), row(D_RNN),
                  _const_spec((1, Q_LORA)), _const_spec((1, KV_LORA)),
                  _const_spec((QP_COLS, Q_LORA)), _const_spec((KV_LORA, QP_COLS)), _const_spec((KV_LORA, D_ATTN)),
                  _const_spec((1, LANES)), _const_spec((1, LANES)), row(LANES), row(LANES), row(LANES)],
        out_specs=[row(P_COLS), row(Q_LORA), row(KV_LORA), row(QP_COLS),
                   row(QP_COLS + D_ATTN), acc(LANES), acc(LANES), acc(Q_LORA), acc(KV_LORA)],
        out_shape=[jax.ShapeDtypeStruct((n, P_COLS), BF16), jax.ShapeDtypeStruct((n, Q_LORA), BF16),
                   jax.ShapeDtypeStruct((n, KV_LORA), BF16), jax.ShapeDtypeStruct((n, QP_COLS), BF16),
                   jax.ShapeDtypeStruct((n, QP_COLS + D_ATTN), BF16),
                   jax.ShapeDtypeStruct((1, LANES), F32), jax.ShapeDtypeStruct((1, LANES), F32),
                   jax.ShapeDtypeStruct((1, Q_LORA), F32), jax.ShapeDtypeStruct((1, KV_LORA), F32)],
        compiler_params=pltpu.CompilerParams(dimension_semantics=("arbitrary",), vmem_limit_bytes=VMEM_LIMIT),
    )(cq, ckv, kr, dq_r, dk_r, dv, dxr, dxg, gqa, gkva, w_uq_p, w_uk_p, w_v, qg, kg, rc, rs1, rs2)


def _softmax_parts(qh, kh, tq, t):
    s = _dot_nt(qh, kh) * (QK_HEAD ** -0.5)
    key = lax.broadcasted_iota(jnp.int32, (tq, t), 1)
    s = jnp.where(key >= PAD_ROWS, s, -jnp.inf)
    e = jnp.exp(s - jnp.max(s, axis=-1, keepdims=True))
    return e, jnp.sum(e, axis=-1, keepdims=True)


def _attn_specs(t, tq):
    nq = t // tq
    qspec = pl.BlockSpec((tq, 2 * HEAD_PAD), lambda b, hp, i: (b * nq + i, hp))
    kspec = pl.BlockSpec((t, 2 * HEAD_PAD), lambda b, hp, i: (b, hp))
    vspec = pl.BlockSpec((t, 2 * V_HEAD), lambda b, hp, i: (b, hp))
    ospec = pl.BlockSpec((tq, 2 * V_HEAD), lambda b, hp, i: (b * nq + i, hp))
    return nq, qspec, kspec, vspec, ospec


def _attn_fwd(q, k, v, srcs=(), scatter=()):
    n = q.shape[0]
    t = _t_pad()
    tq = t // 8
    nq, qspec, kspec, vspec, ospec = _attn_specs(t, tq)
    nk = len(srcs)
    c_in, c_out, c_shape, c_sems = _exchange_specs(srcs, scatter)

    def body(q_ref, k_ref, v_ref, *rest):
        o_ref = rest[nk]
        finish = _ride(3, *_exchange_fns(rest[:nk], rest[nk + 1:2 * nk + 1], rest[2 * nk + 1:], scatter))
        lane = lax.broadcasted_iota(jnp.int32, (tq, 2 * V_HEAD), 1)
        outs = []
        for j in range(2):
            sl = slice(j * HEAD_PAD, (j + 1) * HEAD_PAD)
            e, l = _softmax_parts(q_ref[:, sl], k_ref[:, sl], tq, t)
            outs.append(_dot(e.astype(BF16), v_ref[...]) / l)
        o_ref[...] = jnp.where(lane < V_HEAD, outs[0], outs[1])
        finish()

    res = pl.pallas_call(
        body, name="attn_fwd", grid=(n // t, N_HEADS // 2, nq),
        in_specs=[qspec, kspec, vspec] + c_in, out_specs=[ospec] + c_out,
        out_shape=[jax.ShapeDtypeStruct((n, D_ATTN), F32)] + c_shape, scratch_shapes=c_sems,
        compiler_params=pltpu.CompilerParams(dimension_semantics=("arbitrary", "arbitrary", "arbitrary"),
                                             vmem_limit_bytes=VMEM_LIMIT),
    )(q, k, v, *srcs)
    return res[0], res[1:]


def _attn_bwd(q, k, v, do, srcs=(), scatter=()):
    n = q.shape[0]
    t = _t_pad()
    tq = t // 4
    nq, qspec, kspec, vspec, ospec = _attn_specs(t, tq)
    nk = len(srcs)
    c_in, c_out, c_shape, c_sems = _exchange_specs(srcs, scatter)

    def body(q_ref, k_ref, v_ref, do_ref, *rest):
        dq_ref, dk_ref, dv_ref = rest[nk:nk + 3]
        finish = _ride(3, *_exchange_fns(rest[:nk], rest[nk + 3:2 * nk + 3], rest[2 * nk + 3:], scatter))
        first = pl.program_id(2) == 0
        lane = lax.broadcasted_iota(jnp.int32, (tq, 2 * V_HEAD), 1)
        do = do_ref[...]
        dv = jnp.zeros((t, 2 * V_HEAD), F32)
        for j in range(2):
            sl = slice(j * HEAD_PAD, (j + 1) * HEAD_PAD)
            qh, kh = q_ref[:, sl], k_ref[:, sl]
            e, l = _softmax_parts(qh, kh, tq, t)
            p = e / l
            in_head = (lane < V_HEAD) if j == 0 else (lane >= V_HEAD)
            doh = jnp.where(in_head, do, 0.0).astype(BF16)
            dp = _dot_nt(doh, v_ref[...])
            delta = jnp.sum(p * dp, axis=-1, keepdims=True)
            ds = (p * (dp - delta) * (QK_HEAD ** -0.5)).astype(BF16)
            dq_ref[:, sl] = _dot(ds, kh)
            dkh = _dot_tn(ds, qh)

            @pl.when(first)
            def _():
                dk_ref[:, sl] = dkh

            @pl.when(jnp.logical_not(first))
            def _():
                dk_ref[:, sl] += dkh

            dv = dv + _dot_tn(p.astype(BF16), doh)
        _acc(dv_ref, first, dv)
        finish()

    res = pl.pallas_call(
        body, name="attn_bwd", grid=(n // t, N_HEADS // 2, nq),
        in_specs=[qspec, kspec, vspec, ospec] + c_in, out_specs=[qspec, kspec, vspec] + c_out,
        out_shape=[jax.ShapeDtypeStruct((n, QP_COLS), F32), jax.ShapeDtypeStruct((n, QP_COLS), F32),
                   jax.ShapeDtypeStruct((n, D_ATTN), F32)] + c_shape, scratch_shapes=c_sems,
        compiler_params=pltpu.CompilerParams(dimension_semantics=("arbitrary", "arbitrary", "arbitrary"),
                                             vmem_limit_bytes=VMEM_LIMIT),
    )(q, k, v, do, *srcs)
    return res[:3], res[3:]


SCAN_STEPS = 8


def _scan(chains, t):
    seg = t // 8
    rows = lax.broadcasted_iota(jnp.int32, (8, LANES), 0)

    def step(i, carry):
        carry = list(carry)
        for u in range(SCAN_STEPS):
            j = i * SCAN_STEPS + u
            for n, (a_ref, b_ref, h_ref, p_ref, reverse) in enumerate(chains):
                h, p = carry[n]
                idx = pl.ds(seg - 1 - j if reverse else j, 8, stride=seg)
                a = a_ref[idx, :]
                h = a * h + b_ref[idx, :]
                p = a * p
                h_ref[idx, :] = h
                p_ref[idx, :] = p
                carry[n] = (h, p)
        return tuple(carry)

    init = tuple((jnp.zeros((8, LANES), F32), jnp.ones((8, LANES), F32)) for _ in chains)
    ends = lax.fori_loop(0, seg // SCAN_STEPS, step, init)
    for (_, _, h_ref, p_ref, reverse), (b, a) in zip(chains, ends):
        for d in (1, 2, 4):
            if reverse:
                keep = rows < 8 - d
                a_n, b_n = pltpu.roll(a, 8 - d, 0), pltpu.roll(b, 8 - d, 0)
            else:
                keep = rows >= d
                a_n, b_n = pltpu.roll(a, d, 0), pltpu.roll(b, d, 0)
            b = a * jnp.where(keep, b_n, 0.0) + b
            a = a * jnp.where(keep, a_n, 1.0)
        for s in (range(7) if reverse else range(1, 8)):
            sl = slice(s * seg, (s + 1) * seg)
            carry_in = b[s + 1:s + 2, :] if reverse else b[s - 1:s, :]
            h_ref[sl, :] = h_ref[sl, :] + p_ref[sl, :] * carry_in


def _shift_rows(x, s, rows, t):
    if s == 0:
        return x
    rolled = pltpu.roll(x, s % t, 0)
    return jnp.where(rows >= s, rolled, 0.0) if s > 0 else jnp.where(rows < t + s, rolled, 0.0)


def _neg_expm1(x):
    series = -x * (1.0 + x * (0.5 + x * (1.0 / 6 + x * (1.0 / 24 + x * (1.0 / 120 + x * (1.0 / 720))))))
    return jnp.where(x > -0.3, series, 1.0 - jnp.exp(x))


def _gelu_parts(x):
    k = math.sqrt(2.0 / math.pi)
    th = jnp.tanh(k * (x + 0.044715 * x * x * x))
    g = 0.5 * x * (1.0 + th)
    dg = 0.5 * (1.0 + th) + 0.5 * x * (1.0 - th * th) * k * (1.0 + 3 * 0.044715 * x * x)
    return g, dg


def _lru_gates(xc, gates, lam_ref, valid, d):
    r = jax.nn.sigmoid(gates[:, (2 * d) * LANES:(2 * d + 1) * LANES])
    i = jax.nn.sigmoid(gates[:, (2 * d + 1) * LANES:(2 * d + 2) * LANES])
    neg_lam = -lam_ref[d:d + 1, :]
    sp = jnp.maximum(neg_lam, 0.0) + jnp.log1p(jnp.exp(-jnp.abs(neg_lam)))
    log_a = -LRU_C * r * sp
    a = jnp.exp(log_a)
    m = jnp.maximum(_neg_expm1(2.0 * log_a), 0.0)
    sq = jnp.sqrt(m)
    b = jnp.where(valid, sq * (i * xc), 0.0)
    return r, i, sp, a, m, sq, b


def _conv(xr, cw_ref, cb_ref, rows, t):
    return (cw_ref[0:1, :] * _shift_rows(xr, 2, rows, t) + cw_ref[1:2, :] * _shift_rows(xr, 1, rows, t)
            + cw_ref[2:3, :] * xr + cw_ref[3:4, :] * _shift_rows(xr, -1, rows, t) + cb_ref[...])


def _rnn_specs(t):
    seq = pl.BlockSpec((t, LANES), lambda cb, b: (b, cb))
    cw = pl.BlockSpec((4, LANES), lambda cb, b: (0, cb))
    vec1 = pl.BlockSpec((1, LANES), lambda cb, b: (0, cb))
    vec2 = pl.BlockSpec((2, LANES), lambda cb, b: (0, cb))
    wblk = pl.BlockSpec((1, LANES, 4 * LANES), lambda cb, b: (cb, 0, 0))
    gbias = pl.BlockSpec((1, 1, 4 * LANES), lambda cb, b: (cb, 0, 0))
    return seq, cw, vec1, vec2, wblk, gbias


def _rnn_fwd(xr, xg, conv_w, conv_b, wblk, gbias, lam):
    n = xr.shape[0]
    t = _t_pad()
    seq, cw, vec1, vec2, wspec, gspec = _rnn_specs(t)

    def body(xr_ref, xg_ref, cw_ref, cb_ref, w_ref, gb_ref, lam_ref, o_ref, a_s, b_s, h_s, p_s):
        rows = lax.broadcasted_iota(jnp.int32, (t, LANES), 0)
        valid = rows >= PAD_ROWS
        xc = _conv(xr_ref[...], cw_ref, cb_ref, rows, t)
        gates = _dot(xc.astype(BF16), w_ref[0]) + gb_ref[0]
        for d in range(2):
            _, _, _, a, _, _, b = _lru_gates(xc, gates, lam_ref, valid, d)
            a_s[d] = a
            b_s[d] = b
        _scan([(a_s.at[d], b_s.at[d], h_s.at[d], p_s.at[d], d == 1) for d in range(2)], t)
        g, _ = _gelu_parts(xg_ref[...])
        o_ref[...] = (h_s[0] + h_s[1]) * g

    return pl.pallas_call(
        body, name="rnn_fwd", grid=(D_RNN // LANES, n // t),
        in_specs=[seq, seq, cw, vec1, wspec, gspec, vec2], out_specs=seq,
        out_shape=jax.ShapeDtypeStruct((n, D_RNN), F32),
        scratch_shapes=[pltpu.VMEM((2, t, LANES), F32)] * 4,
        compiler_params=pltpu.CompilerParams(dimension_semantics=("parallel", "parallel"), vmem_limit_bytes=VMEM_LIMIT),
    )(xr, xg, conv_w, conv_b, wblk, gbias, lam)


def _rnn_bwd(xr, xg, do, conv_w, conv_b, wblk, gbias, lam):
    n = xr.shape[0]
    t = _t_pad()
    seq, cw, vec1, vec2, wspec, gspec = _rnn_specs(t)

    def body(xr_ref, xg_ref, do_ref, cw_ref, cb_ref, w_ref, gb_ref, lam_ref,
             dxr_ref, dxg_ref, dcw_ref, dcb_ref, dw_ref, dgb_ref, dlam_ref, a_s, b_s, h_s, l_s, p_s, dg_s):
        first = pl.program_id(1) == 0
        rows = lax.broadcasted_iota(jnp.int32, (t, LANES), 0)
        valid = rows >= PAD_ROWS
        xr = xr_ref[...]
        xc = _conv(xr, cw_ref, cb_ref, rows, t)
        xcb = xc.astype(BF16)
        gates = _dot(xcb, w_ref[0]) + gb_ref[0]
        for d in range(2):
            _, _, _, a, _, _, b = _lru_gates(xc, gates, lam_ref, valid, d)
            a_s[d] = a
            b_s[d] = b
        _scan([(a_s.at[d], b_s.at[d], h_s.at[d], p_s.at[d], d == 1) for d in range(2)], t)
        g, dg = _gelu_parts(xg_ref[...])
        do = do_ref[...]
        dxg_ref[...] = do * (h_s[0] + h_s[1]) * dg
        b_s[0] = do * g
        for d in range(2):
            a_s[d] = _shift_rows(a_s[d], -1 if d == 0 else 1, rows, t)
        _scan([(a_s.at[d], b_s.at[0], l_s.at[d], p_s.at[d], d == 0) for d in range(2)], t)
        dxc = jnp.zeros((t, LANES), F32)
        dlams = []
        for d in range(2):
            r, i, sp, a, m, sq, _ = _lru_gates(xc, gates, lam_ref, valid, d)
            lam_t = l_s[d]
            da = lam_t * _shift_rows(h_s[d], 1 if d == 0 else -1, rows, t)
            lam_v = jnp.where(valid, lam_t, 0.0)
            dsq = lam_v * (i * xc)
            di = lam_v * sq * xc
            dxc = dxc + lam_v * sq * i
            dm = jnp.where(m > 0.0, dsq * 0.5 / jnp.where(m > 0.0, sq, 1.0), 0.0)
            dla = da * a - 2.0 * dm * a * a
            dr = dla * (-LRU_C) * sp
            dsp = _colsum(dla * (-LRU_C) * r)
            dlams.append(dsp * -jax.nn.sigmoid(-lam_ref[d:d + 1, :]))
            dg_s[:, (2 * d) * LANES:(2 * d + 1) * LANES] = (dr * r * (1.0 - r)).astype(BF16)
            dg_s[:, (2 * d + 1) * LANES:(2 * d + 2) * LANES] = (di * i * (1.0 - i)).astype(BF16)
        dgates = dg_s[...]
        dxc = dxc + _dot_nt(dgates, w_ref[0])
        dxr_ref[...] = (cw_ref[0:1, :] * _shift_rows(dxc, -2, rows, t) + cw_ref[1:2, :] * _shift_rows(dxc, -1, rows, t)
                        + cw_ref[2:3, :] * dxc + cw_ref[3:4, :] * _shift_rows(dxc, 1, rows, t))
        dcw = jnp.concatenate([_colsum(dxc * _shift_rows(xr, 2 - j, rows, t)) for j in range(4)], axis=0)
        _acc(dcw_ref, first, dcw)
        _acc(dcb_ref, first, _colsum(dxc))
        _acc(dw_ref, first, _dot_tn(xcb, dgates)[None])
        _acc(dgb_ref, first, _colsum(dgates.astype(F32))[None])
        _acc(dlam_ref, first, jnp.concatenate(dlams, axis=0))

    return pl.pallas_call(
        body, name="rnn_bwd", grid=(D_RNN // LANES, n // t),
        in_specs=[seq, seq, seq, cw, vec1, wspec, gspec, vec2],
        out_specs=[seq, seq, cw, vec1, wspec, gspec, vec2],
        out_shape=[jax.ShapeDtypeStruct((n, D_RNN), F32), jax.ShapeDtypeStruct((n, D_RNN), F32),
                   jax.ShapeDtypeStruct((4, D_RNN), F32), jax.ShapeDtypeStruct((1, D_RNN), F32),
                   jax.ShapeDtypeStruct((D_RNN // LANES, LANES, 4 * LANES), F32),
                   jax.ShapeDtypeStruct((D_RNN // LANES, 1, 4 * LANES), F32), jax.ShapeDtypeStruct((2, D_RNN), F32)],
        scratch_shapes=[pltpu.VMEM((2, t, LANES), F32)] * 5 + [pltpu.VMEM((t, 4 * LANES), BF16)],
        compiler_params=pltpu.CompilerParams(dimension_semantics=("parallel", "arbitrary"), vmem_limit_bytes=VMEM_LIMIT),
    )(xr, xg, do, conv_w, conv_b, wblk, gbias, lam)


def _post(oa, orn, h0, tgt, ga, gr, g2, w_out, w_gate, w_up, w_down):
    n = oa.shape[0]
    tm = _row_tile(n)
    t = _t_pad()

    def body(oa_ref, or_ref, h0_ref, tgt_ref, ga_ref, gr_ref, g2_ref, wo_ref, wg_ref, wu_ref, wd_ref,
             doa_ref, dor_ref, dh1_ref, mix_ref, h1n_ref, act_ref, dgate_ref, dup_ref, dy_ref,
             loss_ref, dga_ref, dgr_ref, dg2_ref, gate_s, up_s):
        first = pl.program_id(0) == 0
        xa, ra = _rms(oa_ref[...], D_ATTN)
        xr, rr = _rms(or_ref[...], D_RNN)
        mix = jnp.concatenate([(xa * ga_ref[...]).astype(BF16), (xr * gr_ref[...]).astype(BF16)], axis=-1)
        mix_ref[...] = mix.T
        h1 = h0_ref[...] + _dot(mix, wo_ref[...])
        x2, r2 = _rms(h1, D_MODEL)
        h1n = (x2 * g2_ref[...]).astype(BF16)
        h1n_ref[...] = h1n
        y = h1
        for cs in range(0, D_FF, FF_CHUNK):
            sl = slice(cs, cs + FF_CHUNK)
            gate = _dot_nt(h1n, wg_ref[sl, :])
            up = _dot_nt(h1n, wu_ref[sl, :])
            gate_s[:, sl] = gate
            up_s[:, sl] = up
            act = (gate * jax.nn.sigmoid(gate) * up).astype(BF16)
            act_ref[sl, :] = act.T
            y = y + _dot(act, wd_ref[sl, :])
        row = pl.program_id(0) * tm + lax.broadcasted_iota(jnp.int32, (tm, 1), 0)
        for _ in range(1, n // t):
            row = jnp.where(row >= t, row - t, row)
        err = jnp.where(row >= PAD_ROWS + N_META, y - tgt_ref[...], 0.0)
        _acc(loss_ref, first, jnp.full((1, LANES), 0.5 / D_MODEL, F32) * jnp.sum(err * err))
        dy = err * (1.0 / D_MODEL)
        dyb = dy.astype(BF16)
        dy_ref[...] = dyb
        dh1n = jnp.zeros((tm, D_MODEL), F32)
        for cs in range(0, D_FF, FF_CHUNK):
            sl = slice(cs, cs + FF_CHUNK)
            dact = _dot_nt(dyb, wd_ref[sl, :])
            gate, up = gate_s[:, sl], up_s[:, sl]
            sg = jax.nn.sigmoid(gate)
            dgate = (dact * up * sg * (1.0 + gate * (1.0 - sg))).astype(BF16)
            dup = (dact * gate * sg).astype(BF16)
            dgate_ref[sl, :] = dgate.T
            dup_ref[sl, :] = dup.T
            dh1n = dh1n + _dot(dgate, wg_ref[sl, :]) + _dot(dup, wu_ref[sl, :])
        _acc(dg2_ref, first, _colsum(dh1n * x2))
        dh1 = dy + _rms_bwd(dh1n, x2, r2, g2_ref[...], D_MODEL)
        dh1_ref[...] = dh1
        dmix = _dot_nt(dh1.astype(BF16), wo_ref[...])
        dma, dmr = dmix[:, :D_ATTN], dmix[:, D_ATTN:]
        _acc(dga_ref, first, _colsum(dma * xa))
        _acc(dgr_ref, first, _colsum(dmr * xr))
        doa_ref[...] = _rms_bwd(dma, xa, ra, ga_ref[...], D_ATTN)
        dor_ref[...] = _rms_bwd(dmr, xr, rr, gr_ref[...], D_RNN)

    def row(w):
        return pl.BlockSpec((tm, w), lambda i: (i, 0))

    def acc(w):
        return pl.BlockSpec((1, w), lambda i: (0, 0))

    def col(w):
        return pl.BlockSpec((w, tm), lambda i: (0, i))

    outs = [(D_ATTN, F32, row), (D_RNN, F32, row), (D_MODEL, F32, row), (D_MODEL, BF16, col), (D_MODEL, BF16, row),
            (D_FF, BF16, col), (D_FF, BF16, col), (D_FF, BF16, col), (D_MODEL, BF16, row)]
    accs = [LANES, D_ATTN, D_RNN, D_MODEL]
    return pl.pallas_call(
        body, name="post", grid=(n // tm,),
        in_specs=[row(D_ATTN), row(D_RNN), row(D_MODEL), row(D_MODEL),
                  _const_spec((1, D_ATTN)), _const_spec((1, D_RNN)), _const_spec((1, D_MODEL)),
                  _const_spec((D_MODEL, D_MODEL)), _const_spec((D_FF, D_MODEL)), _const_spec((D_FF, D_MODEL)),
                  _const_spec((D_FF, D_MODEL))],
        out_specs=[spec(w) for w, _, spec in outs] + [acc(w) for w in accs],
        out_shape=[jax.ShapeDtypeStruct((n, w) if spec is row else (w, n), dt) for w, dt, spec in outs]
        + [jax.ShapeDtypeStruct((1, w), F32) for w in accs],
        scratch_shapes=[pltpu.VMEM((tm, D_FF), F32), pltpu.VMEM((tm, D_FF), F32)],
        compiler_params=pltpu.CompilerParams(dimension_semantics=("arbitrary",), vmem_limit_bytes=VMEM_LIMIT),
    )(oa, orn, h0, tgt, ga, gr, g2, w_out, w_gate, w_up, w_down)


def _in_bwd(dp, h0, dh1, ln1_g, w_in_p, srcs=(), scatter=()):
    n = h0.shape[0]
    tm = _row_tile(n)
    nk = len(srcs)
    c_in, c_out, c_shape, c_sems = _exchange_specs(srcs, scatter)

    def body(dp_ref, h0_ref, dh1_ref, g_ref, w_ref, *rest):
        dh0_ref, dg_ref = rest[nk:nk + 2]
        finish = _ride(1, *_exchange_fns(rest[:nk], rest[nk + 2:2 * nk + 2], rest[2 * nk + 2:], scatter))
        dhn = _dot(dp_ref[...], w_ref[...])
        xhat, r = _rms(h0_ref[...], D_MODEL)
        _acc(dg_ref, pl.program_id(0) == 0, _colsum(dhn * xhat))
        dh0_ref[...] = dh1_ref[...] + _rms_bwd(dhn, xhat, r, g_ref[...], D_MODEL)
        finish()

    def row(w):
        return pl.BlockSpec((tm, w), lambda i: (i, 0))

    res = pl.pallas_call(
        body, name="in_bwd", grid=(n // tm,),
        in_specs=[row(P_COLS), row(D_MODEL), row(D_MODEL), _const_spec((1, D_MODEL)), _const_spec((P_COLS, D_MODEL))] + c_in,
        out_specs=[row(D_MODEL), pl.BlockSpec((1, D_MODEL), lambda i: (0, 0))] + c_out,
        out_shape=[jax.ShapeDtypeStruct((n, D_MODEL), F32), jax.ShapeDtypeStruct((1, D_MODEL), F32)] + c_shape,
        scratch_shapes=c_sems,
        compiler_params=pltpu.CompilerParams(dimension_semantics=("arbitrary",), vmem_limit_bytes=VMEM_LIMIT),
    )(dp, h0, dh1, ln1_g, w_in_p, *srcs)
    return res[:2], res[2:]


def _pick_tile(width, cap):
    best = LANES
    for mult in range(1, width // LANES + 1):
        cand = mult * LANES
        if width % cand == 0 and cand <= cap:
            best = cand
    return best


def _matmul_tn(name, a, b):
    n, ka = a.shape
    kb = b.shape[1]
    ta, tb = _pick_tile(ka, 1408), _pick_tile(kb, 1408)
    tk = n // 4

    def body(a_ref, b_ref, o_ref):
        _acc(o_ref, pl.program_id(2) == 0, _dot_tn(a_ref[...].astype(BF16), b_ref[...].astype(BF16)))

    return pl.pallas_call(
        body, name=name, grid=(ka // ta, kb // tb, n // tk),
        in_specs=[pl.BlockSpec((tk, ta), lambda i, j, k: (k, i)), pl.BlockSpec((tk, tb), lambda i, j, k: (k, j))],
        out_specs=pl.BlockSpec((ta, tb), lambda i, j, k: (i, j)),
        out_shape=jax.ShapeDtypeStruct((ka, kb), F32),
        compiler_params=pltpu.CompilerParams(dimension_semantics=("parallel", "parallel", "arbitrary"),
                                             vmem_limit_bytes=VMEM_LIMIT),
    )(a, b)


def _matmul_shards(name, at, b):
    ka, n = at.shape
    kb = b.shape[1]
    ta, tb = _pick_tile(ka, 1408), _pick_tile(kb, 1408)
    tk = n // 2
    width = ka // N_DEV
    per = ta // width

    def body(a_ref, b_ref, o_ref, acc_ref):
        _acc(acc_ref, pl.program_id(2) == 0, _dot(a_ref[...], b_ref[...].astype(BF16)))

        @pl.when(pl.program_id(2) == pl.num_programs(2) - 1)
        def _():
            for s in range(per):
                o_ref[s] = acc_ref[s * width:(s + 1) * width, :].astype(BF16)

    return pl.pallas_call(
        body, name=name, grid=(ka // ta, kb // tb, n // tk),
        in_specs=[pl.BlockSpec((ta, tk), lambda i, j, k: (i, k)), pl.BlockSpec((tk, tb), lambda i, j, k: (k, j))],
        out_specs=pl.BlockSpec((per, width, tb), lambda i, j, k: (i, 0, j)),
        out_shape=jax.ShapeDtypeStruct((N_DEV, width, kb), BF16),
        scratch_shapes=[pltpu.VMEM((ta, tb), F32)],
        compiler_params=pltpu.CompilerParams(dimension_semantics=("parallel", "parallel", "arbitrary"),
                                             vmem_limit_bytes=VMEM_LIMIT),
    )(at, b)


def _adamw_math(g8_ref, w_ref, m_ref, v_ref, g_ref, d_ref, nm_ref, nv_ref):
    g = g8_ref[0].astype(F32)
    for s in range(1, N_DEV):
        g = g + g8_ref[s].astype(F32)
    g_ref[...] = g
    nm = ADAM_B1 * m_ref[...] + (1.0 - ADAM_B1) * g
    nv = ADAM_B2 * v_ref[...] + (1.0 - ADAM_B2) * (g * g)
    nm_ref[...] = nm
    nv_ref[...] = nv
    m_hat = nm / (1.0 - ADAM_B1 ** ADAM_STEP)
    v_hat = nv / (1.0 - ADAM_B2 ** ADAM_STEP)
    d_ref[...] = -ADAM_LR * (m_hat / (jnp.sqrt(v_hat) + ADAM_EPS) + ADAM_WD * w_ref[...])


def _adamw_many(name, items):
    count = len(items)

    def body(*refs):
        ins, outs = refs[:4 * count], refs[4 * count:]
        for i in range(count):
            _adamw_math(*ins[4 * i:4 * i + 4], *outs[4 * i:4 * i + 4])

    flat = [a for item in items for a in item]
    res = pl.pallas_call(
        body, name=name,
        out_shape=[jax.ShapeDtypeStruct(item[1].shape, F32) for item in items for _ in range(4)],
        compiler_params=pltpu.CompilerParams(vmem_limit_bytes=VMEM_LIMIT),
    )(*flat)
    return [tuple(res[4 * i:4 * i + 4]) for i in range(count)]


def _adamw(name, g8, w, m, v):
    rows, cols = w.shape
    tr = rows
    for cand in (256, 176, 128, 64):
        if rows % cand == 0 and rows > cand:
            tr = cand
            break

    def body(*refs):
        _adamw_math(*refs)

    blk = pl.BlockSpec((tr, cols), lambda i: (i, 0))
    return pl.pallas_call(
        body, name=name, grid=(rows // tr,),
        in_specs=[pl.BlockSpec((N_DEV, tr, cols), lambda i: (0, i, 0)), blk, blk, blk],
        out_specs=[blk] * 4, out_shape=[jax.ShapeDtypeStruct((rows, cols), F32)] * 4,
        compiler_params=pltpu.CompilerParams(dimension_semantics=("parallel",), vmem_limit_bytes=VMEM_LIMIT),
    )(g8, w, m, v)


def _exchange_specs(srcs, scatter):
    nk = len(srcs)
    if not nk:
        return [], [], [], []
    any_spec = pl.BlockSpec(memory_space=pl.ANY)
    out_shape = [jax.ShapeDtypeStruct(s.shape if sc else (N_DEV,) + s.shape, s.dtype) for s, sc in zip(srcs, scatter)]
    sems = [pltpu.SemaphoreType.DMA((nk, N_DEV - 1)), pltpu.SemaphoreType.DMA((nk, N_DEV - 1)),
            pltpu.SemaphoreType.DMA((nk,))]
    return [any_spec] * nk, [any_spec] * nk, out_shape, sems


FLIPS = ((0, 0, 1), (1, 0, 0), (0, 1, 0), (1, 1, 0), (1, 0, 1), (0, 1, 1), (1, 1, 1))
N_CHIP_PEERS = 3


def _exchange_fns(src_refs, out_refs, sems, scatter):
    nk = len(src_refs)
    if not nk:
        return (lambda: None), (lambda: None), (lambda: None)
    send_sems, recv_sems, local_sems = sems
    first = 1 + N_CHIP_PEERS

    def plan():
        x, y, c = lax.axis_index("x"), lax.axis_index("y"), lax.axis_index("c")
        me = 4 * x + 2 * y + c
        peers = [(1 - x if fx else x, 1 - y if fy else y, 1 - c if fc else c) for fx, fy, fc in FLIPS]
        pids = [4 * px + 2 * py + pc for px, py, pc in peers]

        def remote(k, j, src, dst, to):
            return pltpu.make_async_remote_copy(src_ref=src, dst_ref=dst, send_sem=send_sems.at[k, j],
                                                recv_sem=recv_sems.at[k, j], device_id=to, device_id_type=MESH)

        def mine(k, dest):
            return src_refs[k].at[dest] if scatter[k] else src_refs[k]

        local = [pltpu.make_async_copy(mine(k, me), out_refs[k].at[me], local_sems.at[k]) for k in range(nk)]
        direct = [remote(k, j, mine(k, pids[j]), out_refs[k].at[me], peers[j])
                  for k in range(nk) for j in range(len(FLIPS) if scatter[k] else first)]
        relays = {(k, j): remote(k, j, out_refs[k].at[pids[j - N_CHIP_PEERS]], out_refs[k].at[pids[j - N_CHIP_PEERS]], peers[0])
                  for k in range(nk) if not scatter[k] for j in range(first, len(FLIPS))}
        arrivals = {(k, j): remote(k, j, out_refs[k].at[pids[j]], out_refs[k].at[pids[j]], peers[j])
                    for k in range(nk) for j in range(len(FLIPS))}
        return local, direct, relays, arrivals

    def start():
        local, direct, _, _ = plan()
        for cp in local + direct:
            cp.start()

    def relay():
        _, _, relays, arrivals = plan()
        for (k, j), cp in relays.items():
            arrivals[k, j - N_CHIP_PEERS].wait_recv()
            cp.start()

    def wait():
        local, direct, relays, arrivals = plan()
        for (k, j), cp in arrivals.items():
            if (k, j + N_CHIP_PEERS) not in relays:
                cp.wait_recv()
        for cp in direct + list(relays.values()):
            cp.wait_send()
        for cp in local:
            cp.wait()

    return start, relay, wait


def _grid_step(rank):
    step, total = 0, 1
    for axis in range(rank):
        step = step * pl.num_programs(axis) + pl.program_id(axis)
        total = total * pl.num_programs(axis)
    return step, total


def _ride(rank, start, relay, wait):
    step, total = _grid_step(rank)
    pl.when(step == 0)(start)
    pl.when(step == (3 * total) // 4)(relay)
    return lambda: pl.when(step == total - 1)(wait)


def _exchange(name, srcs, scatter):
    nk = len(srcs)
    c_in, c_out, c_shape, c_sems = _exchange_specs(srcs, scatter)

    def body(*refs):
        start, relay, wait = _exchange_fns(refs[:nk], refs[nk:2 * nk], refs[2 * nk:], scatter)
        start()
        relay()
        wait()

    return pl.pallas_call(body, name=name, in_specs=c_in, out_specs=c_out, out_shape=c_shape, scratch_shapes=c_sems)(*srcs)


def _cols_from_shards(g):
    return jnp.transpose(g, (1, 0, 2)).reshape(g.shape[1], -1)


def _cols_to_shards(w):
    return jnp.transpose(w.reshape(w.shape[0], N_DEV, -1), (1, 0, 2))


def _rope_tables(n):
    t = _t_pad()
    pos = (jnp.arange(t, dtype=F32) - PAD_ROWS)
    half = QK_ROPE // 2
    freqs = 1.0 / (ROPE_THETA ** (jnp.arange(half, dtype=F32) / half))
    ang = pos[:, None] * freqs[None, :]
    cos, sin = jnp.cos(ang), jnp.sin(ang)
    z = lambda w: jnp.zeros((t, w), F32)
    c = jnp.concatenate([jnp.ones((t, QK_NOPE), F32), cos, cos, z(HEAD_PAD - QK_HEAD)], axis=1)
    s1 = jnp.concatenate([z(QK_NOPE + half), sin, z(HEAD_PAD - QK_HEAD)], axis=1)
    s2 = jnp.concatenate([z(QK_NOPE), -sin, z(HEAD_PAD - QK_NOPE - half)], axis=1)
    reps = n // t
    return tuple(jnp.tile(a, (reps, 1)) for a in (c, s1, s2))


def _block_diag_gates(lru_wa, lru_wi):
    eye = jnp.eye(2, dtype=lru_wa.dtype)

    def bd(w):
        w = w.reshape(2, D_RNN // LANES, 2, RNN_BW, RNN_BW)
        full = w[:, :, :, :, None, :] * eye[None, None, :, None, :, None]
        return full.reshape(2, D_RNN // LANES, LANES, LANES)

    a, i = bd(lru_wa), bd(lru_wi)
    return jnp.concatenate([a[0], i[0], a[1], i[1]], axis=-1)


def _unblock_gates(dw):
    nb = D_RNN // LANES
    parts = dw.reshape(nb, 2, RNN_BW, 4, 2, RNN_BW)
    diag = jnp.stack([parts[:, k, :, :, k, :] for k in range(2)], axis=1)
    diag = jnp.transpose(diag, (3, 0, 1, 2, 4)).reshape(4, 2 * nb, RNN_BW, RNN_BW)
    return jnp.stack([diag[0], diag[2]]), jnp.stack([diag[1], diag[3]])


WEIGHTS = ("meta_tokens", "ln1_g", "w_in", "q_a_norm_g", "w_uq", "kv_a_norm_g", "w_ukv", "q_norm_g", "k_norm_g",
           "conv_w", "conv_b", "lru_wa", "lru_ba", "lru_wi", "lru_bi", "lru_lambda", "attn_out_g", "rnn_out_g",
           "w_out", "ln2_g", "w_gate", "w_up", "w_down")
BIG = ("w_in", "w_uq", "w_ukv", "w_out", "w_gate", "w_up", "w_down")
TRANSPOSED = ("w_in", "w_uq", "w_gate", "w_up")
ROW_SHARDED = ("w_out", "w_down") + TRANSPOSED
REPLICATED = ("ln1_g", "q_a_norm_g", "kv_a_norm_g", "q_norm_g", "k_norm_g", "conv_b", "lru_wa", "lru_wi",
              "attn_out_g", "rnn_out_g", "ln2_g")
G_FIRST = ("w_in", "meta_tokens")
G_MID = ("w_uq", "w_ukv", "conv_w", "lru_ba", "lru_bi", "lru_lambda")
LATE = ("w_out", "w_gate", "w_up", "w_down")
G_LAST = ("meta_tokens", "ln1_g")


def _local_step(x, tgt, ex):
    nb = x.shape[0]
    t = _t_pad()
    n = nb * t
    local = ex.local
    first = ex.gathered(G_FIRST, ex.run("gather_first", *ex.gather_srcs(G_FIRST)))
    meta, w_in = first["meta_tokens"], first["w_in"]
    lead = jnp.zeros((nb, PAD_ROWS, D_MODEL), F32)
    h0 = jnp.concatenate([lead, jnp.broadcast_to(meta[None], (nb, N_META, D_MODEL)), x], axis=1).reshape(n, D_MODEL)
    tgt_p = jnp.concatenate([jnp.zeros((nb, PAD_ROWS + N_META, D_MODEL), F32), tgt], axis=1).reshape(n, D_MODEL)

    zr = lambda r: jnp.zeros((r, D_MODEL), w_in.dtype)
    w_in_p = jnp.concatenate([w_in[:OFF_CKV], w_in[OFF_KR:], zr(QK_NOPE), w_in[OFF_CKV:OFF_KR], zr(HEAD_PAD - QK_HEAD)],
                             axis=0)
    pad_g = lambda g: jnp.pad(g, ((0, 0), (0, HEAD_PAD - QK_HEAD)))
    qg, kg = pad_g(local["q_norm_g"]), pad_g(local["k_norm_g"])
    rc, rs1, rs2 = _rope_tables(n)
    wblk = _block_diag_gates(local["lru_wa"].reshape(2, -1, RNN_BW, RNN_BW),
                             local["lru_wi"].reshape(2, -1, RNN_BW, RNN_BW)).astype(BF16)
    nblk = D_RNN // LANES

    (hn, cq, ckv, xr, xg, kr), got = _in_proj(h0, local["ln1_g"], w_in_p, *ex.gather_srcs(G_MID))
    w = ex.gathered(G_MID, got)
    w_uq_p = jnp.pad(w["w_uq"].reshape(N_HEADS, QK_HEAD, Q_LORA), ((0, 0), (0, HEAD_PAD - QK_HEAD), (0, 0))
                     ).reshape(QP_COLS, Q_LORA)
    ukv = w["w_ukv"].reshape(KV_LORA, N_HEADS, QK_NOPE + V_HEAD)
    w_uk_p = jnp.pad(ukv[:, :, :QK_NOPE], ((0, 0), (0, 0), (0, HEAD_PAD - QK_NOPE))).reshape(KV_LORA, QP_COLS)
    w_v = ukv[:, :, QK_NOPE:].reshape(KV_LORA, D_ATTN)
    gbias = jnp.stack([w["lru_ba"][0], w["lru_bi"][0], w["lru_ba"][1], w["lru_bi"][1]], axis=0)
    gbias = jnp.transpose(gbias.reshape(4, nblk, LANES), (1, 0, 2)).reshape(nblk, 1, 4 * LANES)

    q, k, v = _qkv_fwd(cq, ckv, kr, local["q_a_norm_g"], local["kv_a_norm_g"], w_uq_p, w_uk_p, w_v, qg, kg, rc, rs1, rs2)
    oa, got = _attn_fwd(q, k, v, *ex.gather_srcs(LATE))
    late = ex.gathered(LATE, got)
    orn = _rnn_fwd(xr, xg, w["conv_w"], local["conv_b"], wblk, gbias, w["lru_lambda"])
    (doa, dor, dh1, mix_t, h1n, act_t, dgate_t, dup_t, dyb, loss, dga, dgr, dg2) = _post(
        oa, orn, h0, tgt_p, local["attn_out_g"], local["rnn_out_g"], local["ln2_g"], late["w_out"], late["w_gate"],
        late["w_up"], late["w_down"])
    wire = {"w_out": _matmul_shards("dw_out", mix_t, dh1), "w_gate": _matmul_shards("dw_gate", dgate_t, h1n),
            "w_up": _matmul_shards("dw_up", dup_t, h1n), "w_down": _matmul_shards("dw_down", act_t, dyb)}
    dxr, dxg, dcw, dcb, dwblk, dgb, dlam = _rnn_bwd(xr, xg, dor, w["conv_w"], local["conv_b"], wblk, gbias, w["lru_lambda"])
    dwa, dwi = _unblock_gates(dwblk)
    dgb = jnp.transpose(dgb.reshape(nblk, 4, LANES), (1, 0, 2)).reshape(4, D_RNN)
    wire.update(ex.to_wire({
        "conv_w": dcw, "conv_b": dcb, "lru_wa": dwa.reshape(-1, RNN_BW), "lru_ba": jnp.stack([dgb[0], dgb[2]]),
        "lru_wi": dwi.reshape(-1, RNN_BW), "lru_bi": jnp.stack([dgb[1], dgb[3]]), "lru_lambda": dlam,
        "attn_out_g": dga, "rnn_out_g": dgr, "ln2_g": dg2}))
    names = tuple(wire)
    (dq_r, dk_r, dv), got = _attn_bwd(q, k, v, doa, *ex.scatter_srcs(names, wire))
    summed = ex.scattered(names, wire, got)
    (dp, qa, kva, dqp, dkv, dqg, dkg, dgqa, dgkva) = _qkv_bwd(
        cq, ckv, kr, dq_r, dk_r, dv, dxr, dxg, local["q_a_norm_g"], local["kv_a_norm_g"], w_uq_p, w_uk_p, w_v, qg, kg,
        rc, rs1, rs2)
    dw_in_p = _matmul_tn("dw_in", dp, hn)
    dw_uq_p = _matmul_tn("dw_uq", dqp, qa)
    dw_kv = _matmul_tn("dw_ukv", kva, dkv)
    kr0 = OFF_CKV + 2 * D_RNN + QK_NOPE
    dw_in = jnp.concatenate([dw_in_p[:OFF_CKV], dw_in_p[kr0:kr0 + QK_ROPE], dw_in_p[OFF_CKV:OFF_CKV + 2 * D_RNN]], axis=0)
    dw_uq = dw_uq_p.reshape(N_HEADS, HEAD_PAD, Q_LORA)[:, :QK_HEAD].reshape(N_HEADS * QK_HEAD, Q_LORA)
    dw_ukv = jnp.concatenate([dw_kv[:, :QP_COLS].reshape(KV_LORA, N_HEADS, HEAD_PAD)[:, :, :QK_NOPE],
                              dw_kv[:, QP_COLS:].reshape(KV_LORA, N_HEADS, V_HEAD)], axis=2).reshape(KV_LORA, -1)
    wire = ex.to_wire({"w_in": dw_in, "q_a_norm_g": dgqa, "w_uq": dw_uq, "kv_a_norm_g": dgkva, "w_ukv": dw_ukv,
                       "q_norm_g": dqg[:, :QK_HEAD], "k_norm_g": dkg[:, :QK_HEAD]})
    names = tuple(wire)
    (dh0, dg1), got = _in_bwd(dp, h0, dh1, local["ln1_g"], w_in_p, *ex.scatter_srcs(names, wire))
    summed.update(ex.scattered(names, wire, got))

    dh0 = dh0.reshape(nb, t, D_MODEL)
    wire = ex.to_wire({"meta_tokens": jnp.sum(dh0[:, PAD_ROWS:PAD_ROWS + N_META], axis=0), "ln1_g": dg1})
    got = ex.run("reduce_last", *ex.scatter_srcs(G_LAST, wire))
    summed.update(ex.scattered(G_LAST, wire, got))
    return loss[0, 0], dh0[:, PAD_ROWS + N_META:], summed


class _MeshExchange:
    def __init__(self, shards):
        self.local = shards

    @staticmethod
    def run(name, srcs, scatter):
        return _exchange(name, srcs, scatter)

    def gather_srcs(self, names):
        return [self.local[k].astype(BF16) if k in BIG else self.local[k] for k in names], [False] * len(names)

    @staticmethod
    def gathered(names, outs):
        return {k: g.reshape(-1, g.shape[-1]) if k in ROW_SHARDED else _cols_from_shards(g) for k, g in zip(names, outs)}

    @staticmethod
    def to_wire(grads):
        wire = {}
        for k, g in grads.items():
            if k in REPLICATED:
                wire[k] = g
            elif k in ROW_SHARDED:
                wire[k] = g.reshape(N_DEV, -1, g.shape[-1]).astype(BF16)
            else:
                wire[k] = _cols_to_shards(g).astype(BF16) if k in BIG else _cols_to_shards(g)
        return wire

    @staticmethod
    def scatter_srcs(names, wire):
        return [wire[k] for k in names], [k not in REPLICATED for k in names]

    @staticmethod
    def scattered(names, wire, outs):
        return dict(zip(names, outs))


def kernel(x, meta_tokens, ln1_g, w_in, q_a_norm_g, w_uq, kv_a_norm_g, w_ukv, q_norm_g, k_norm_g, conv_w, conv_b, lru_wa, lru_ba, lru_wi, lru_bi, lru_lambda, attn_out_g, rnn_out_g, w_out, ln2_g, w_gate, w_up, w_down, loss_target, m_meta_tokens, m_ln1_g, m_w_in, m_q_a_norm_g, m_w_uq, m_kv_a_norm_g, m_w_ukv, m_q_norm_g, m_k_norm_g, m_conv_w, m_conv_b, m_lru_wa, m_lru_ba, m_lru_wi, m_lru_bi, m_lru_lambda, m_attn_out_g, m_rnn_out_g, m_w_out, m_ln2_g, m_w_gate, m_w_up, m_w_down, v_meta_tokens, v_ln1_g, v_w_in, v_q_a_norm_g, v_w_uq, v_kv_a_norm_g, v_w_ukv, v_q_norm_g, v_k_norm_g, v_conv_w, v_conv_b, v_lru_wa, v_lru_ba, v_lru_wi, v_lru_bi, v_lru_lambda, v_attn_out_g, v_rnn_out_g, v_w_out, v_ln2_g, v_w_gate, v_w_up, v_w_down):
    given = (meta_tokens, ln1_g, w_in, q_a_norm_g, w_uq, kv_a_norm_g, w_ukv, q_norm_g, k_norm_g, conv_w, conv_b,
             lru_wa, lru_ba, lru_wi, lru_bi, lru_lambda, attn_out_g, rnn_out_g, w_out, ln2_g, w_gate, w_up, w_down)
    moments_m = (m_meta_tokens, m_ln1_g, m_w_in, m_q_a_norm_g, m_w_uq, m_kv_a_norm_g, m_w_ukv, m_q_norm_g, m_k_norm_g,
                 m_conv_w, m_conv_b, m_lru_wa, m_lru_ba, m_lru_wi, m_lru_bi, m_lru_lambda, m_attn_out_g, m_rnn_out_g,
                 m_w_out, m_ln2_g, m_w_gate, m_w_up, m_w_down)
    moments_v = (v_meta_tokens, v_ln1_g, v_w_in, v_q_a_norm_g, v_w_uq, v_kv_a_norm_g, v_w_ukv, v_q_norm_g, v_k_norm_g,
                 v_conv_w, v_conv_b, v_lru_wa, v_lru_ba, v_lru_wi, v_lru_bi, v_lru_lambda, v_attn_out_g, v_rnn_out_g,
                 v_w_out, v_ln2_g, v_w_gate, v_w_up, v_w_down)
    shapes = {k: a.shape for k, a in zip(WEIGHTS, given)}

    def two_d(k, a):
        a = a.reshape(-1, a.shape[-1])
        return a.T if k in TRANSPOSED else a

    w = {k: two_d(k, a) for k, a in zip(WEIGHTS, given)}
    m = {k: two_d(k, a) for k, a in zip(WEIGHTS, moments_m)}
    v = {k: two_d(k, a) for k, a in zip(WEIGHTS, moments_v)}

    loss_part, grad_x, parts = _local_step(x, loss_target, _MeshExchange(w))

    new = {k: _adamw("adamw_" + k, parts[k], w[k], m[k], v[k]) for k in BIG}
    small = [k for k in WEIGHTS if k not in BIG]
    new.update(zip(small, _adamw_many("adamw_small", [(parts[k], w[k], m[k], v[k]) for k in small])))

    loss = lax.psum(loss_part, ("x", "y", "c"))
    outs = [loss, grad_x]
    for idx in range(4):
        outs += [(new[k][idx].T if k in TRANSPOSED else new[k][idx]).reshape(shapes[k]) for k in WEIGHTS]
    return tuple(outs)
```

```python
import functools
import math

import jax
import jax.numpy as jnp
from jax import lax
from jax.experimental import pallas as pl
from jax.experimental.pallas import tpu as pltpu

F32 = jnp.float32
BF16 = jnp.bfloat16

D_MODEL = 1024
N_META = 16
SEQ = 2048
N_HEADS = 8
QK_NOPE = 64
QK_ROPE = 32
QK_HEAD = QK_NOPE + QK_ROPE
V_HEAD = 64
D_ATTN = N_HEADS * V_HEAD
Q_LORA = 384
KV_LORA = 256
D_RNN = 512
RNN_BW = 64
D_FF = 2816
EPS = 1e-6
LRU_C = 8.0
ROPE_THETA = 10000.0
OFF_CKV = Q_LORA + KV_LORA
OFF_KR = OFF_CKV + QK_ROPE
IN_COLS = OFF_KR + 2 * D_RNN

ADAM_LR = 0.001
ADAM_B1 = 0.9
ADAM_B2 = 0.999
ADAM_EPS = 1e-08
ADAM_WD = 0.01
ADAM_STEP = 10

N_DEV = 8
LANES = 128
HEAD_PAD = LANES
PAD_ROWS = LANES - N_META
QP_COLS = N_HEADS * HEAD_PAD
P_COLS = OFF_CKV + 2 * D_RNN + LANES
FF_CHUNK = D_FF // 2
VMEM_LIMIT = 56 * 1024 * 1024
MESH = pl.DeviceIdType.MESH


def _t_pad():
    return PAD_ROWS + N_META + SEQ


def _row_tile(n):
    return 256 if n % 256 == 0 else 128


def _const_spec(shape):
    nd = len(shape)
    return pl.BlockSpec(shape, lambda *_: (0,) * nd, pipeline_mode=pl.Buffered(1))


def _rms(x, d):
    r = lax.rsqrt(jnp.sum(x * x, axis=-1, keepdims=True) * (1.0 / d) + EPS)
    return x * r, r


def _rms_bwd(dy, xhat, r, g, d):
    dxh = dy * g
    return r * (dxh - xhat * (jnp.sum(dxh * xhat, axis=-1, keepdims=True) * (1.0 / d)))


def _colsum(x):
    return jnp.sum(x, axis=0, keepdims=True)


def _dot(a, b):
    return jnp.dot(a, b, preferred_element_type=F32)


def _dot_nt(a, b):
    return lax.dot_general(a, b, (((1,), (1,)), ((), ())), preferred_element_type=F32)


def _dot_tn(a, b):
    return lax.dot_general(a, b, (((0,), (0,)), ((), ())), preferred_element_type=F32)


def _rope(x, c, s1, s2):
    return x * c + pltpu.roll(x, 16, 1) * s1 + pltpu.roll(x, HEAD_PAD - 16, 1) * s2


def _rope_bwd(dy, c, s1, s2):
    return dy * c + pltpu.roll(dy * s1, HEAD_PAD - 16, 1) + pltpu.roll(dy * s2, 16, 1)


def _acc(ref, first, val):
    @pl.when(first)
    def _():
        ref[...] = val

    @pl.when(jnp.logical_not(first))
    def _():
        ref[...] += val


def _in_proj(h0, ln1_g, w_in_p, srcs=(), scatter=()):
    n = h0.shape[0]
    tm = _row_tile(n)
    nk = len(srcs)
    c_in, c_out, c_shape, c_sems = _exchange_specs(srcs, scatter)

    def body(h_ref, g_ref, w_ref, *rest):
        hn_ref, cq_ref, ckv_ref, xr_ref, xg_ref, kr_ref = rest[nk:nk + 6]
        finish = _ride(1, *_exchange_fns(rest[:nk], rest[nk + 6:2 * nk + 6], rest[2 * nk + 6:], scatter))
        xhat, _ = _rms(h_ref[...], D_MODEL)
        hn = (xhat * g_ref[...]).astype(BF16)
        hn_ref[...] = hn
        p = _dot_nt(hn, w_ref[...])
        cq_ref[...] = p[:, :Q_LORA]
        ckv_ref[...] = p[:, Q_LORA:OFF_CKV]
        xr_ref[...] = p[:, OFF_CKV:OFF_CKV + D_RNN]
        xg_ref[...] = p[:, OFF_CKV + D_RNN:OFF_CKV + 2 * D_RNN]
        kr_ref[...] = p[:, OFF_CKV + 2 * D_RNN:]
        finish()

    def row(w):
        return pl.BlockSpec((tm, w), lambda i: (i, 0))

    widths = (D_MODEL, Q_LORA, KV_LORA, D_RNN, D_RNN, LANES)
    res = pl.pallas_call(
        body, name="in_proj", grid=(n // tm,),
        in_specs=[row(D_MODEL), _const_spec((1, D_MODEL)), _const_spec((P_COLS, D_MODEL))] + c_in,
        out_specs=[row(w) for w in widths] + c_out,
        out_shape=[jax.ShapeDtypeStruct((n, w), BF16 if k == 0 else F32) for k, w in enumerate(widths)] + c_shape,
        scratch_shapes=c_sems,
        compiler_params=pltpu.CompilerParams(dimension_semantics=("arbitrary",), vmem_limit_bytes=VMEM_LIMIT),
    )(h0, ln1_g, w_in_p, *srcs)
    return res[:6], res[6:]


def _qkv_fwd(cq, ckv, kr, gqa, gkva, w_uq_p, w_uk_p, w_v, qg, kg, rc, rs1, rs2):
    n = cq.shape[0]
    tm = _row_tile(n)

    def body(cq_ref, ckv_ref, kr_ref, gqa_ref, gkva_ref, wuq_ref, wuk_ref, wv_ref, qg_ref, kg_ref,
             c_ref, s1_ref, s2_ref, q_ref, k_ref, v_ref):
        xq, _ = _rms(cq_ref[...], Q_LORA)
        qa = (xq * gqa_ref[...]).astype(BF16)
        q = _dot_nt(qa, wuq_ref[...])
        xkv, _ = _rms(ckv_ref[...], KV_LORA)
        kva = (xkv * gkva_ref[...]).astype(BF16)
        kn = _dot(kva, wuk_ref[...])
        v_ref[...] = _dot(kva, wv_ref[...]).astype(BF16)
        krp = kr_ref[...]
        c, s1, s2 = c_ref[...], s1_ref[...], s2_ref[...]
        for h in range(N_HEADS):
            sl = slice(h * HEAD_PAD, (h + 1) * HEAD_PAD)
            qh, _ = _rms(q[:, sl], QK_HEAD)
            q_ref[:, sl] = _rope(qh * qg_ref[...], c, s1, s2).astype(BF16)
            kh, _ = _rms(kn[:, sl] + krp, QK_HEAD)
            k_ref[:, sl] = _rope(kh * kg_ref[...], c, s1, s2).astype(BF16)

    def row(w):
        return pl.BlockSpec((tm, w), lambda i: (i, 0))

    return pl.pallas_call(
        body, name="qkv_fwd", grid=(n // tm,),
        in_specs=[row(Q_LORA), row(KV_LORA), row(LANES), _const_spec((1, Q_LORA)), _const_spec((1, KV_LORA)),
                  _const_spec((QP_COLS, Q_LORA)), _const_spec((KV_LORA, QP_COLS)), _const_spec((KV_LORA, D_ATTN)),
                  _const_spec((1, LANES)), _const_spec((1, LANES)), row(LANES), row(LANES), row(LANES)],
        out_specs=[row(QP_COLS), row(QP_COLS), row(D_ATTN)],
        out_shape=[jax.ShapeDtypeStruct((n, QP_COLS), BF16), jax.ShapeDtypeStruct((n, QP_COLS), BF16),
                   jax.ShapeDtypeStruct((n, D_ATTN), BF16)],
        compiler_params=pltpu.CompilerParams(dimension_semantics=("parallel",), vmem_limit_bytes=VMEM_LIMIT),
    )(cq, ckv, kr, gqa, gkva, w_uq_p, w_uk_p, w_v, qg, kg, rc, rs1, rs2)


def _qkv_bwd(cq, ckv, kr, dq_r, dk_r, dv, dxr, dxg, gqa, gkva, w_uq_p, w_uk_p, w_v, qg, kg, rc, rs1, rs2):
    n = cq.shape[0]
    tm = _row_tile(n)

    def body(cq_ref, ckv_ref, kr_ref, dq_ref, dk_ref, dv_ref, dxr_ref, dxg_ref, gqa_ref, gkva_ref, wuq_ref, wuk_ref,
             wv_ref, qg_ref, kg_ref, c_ref, s1_ref, s2_ref,
             dp_ref, qa_ref, kva_ref, dqp_ref, dkv_ref, dqg_ref, dkg_ref, dgqa_ref, dgkva_ref):
        first = pl.program_id(0) == 0
        dp_ref[:, OFF_CKV:OFF_CKV + D_RNN] = dxr_ref[...].astype(BF16)
        dp_ref[:, OFF_CKV + D_RNN:OFF_CKV + 2 * D_RNN] = dxg_ref[...].astype(BF16)
        xq, rq = _rms(cq_ref[...], Q_LORA)
        qa = (xq * gqa_ref[...]).astype(BF16)
        qa_ref[...] = qa
        q = _dot_nt(qa, wuq_ref[...])
        xkv, rkv = _rms(ckv_ref[...], KV_LORA)
        kva = (xkv * gkva_ref[...]).astype(BF16)
        kva_ref[...] = kva
        kn = _dot(kva, wuk_ref[...])
        krp = kr_ref[...]
        c, s1, s2 = c_ref[...], s1_ref[...], s2_ref[...]
        lane = lax.broadcasted_iota(jnp.int32, (tm, HEAD_PAD), 1)
        rope_lanes = jnp.logical_and(lane >= QK_NOPE, lane < QK_HEAD)
        dqg = jnp.zeros((1, HEAD_PAD), F32)
        dkg = jnp.zeros((1, HEAD_PAD), F32)
        dkr = jnp.zeros((tm, HEAD_PAD), F32)
        for h in range(N_HEADS):
            sl = slice(h * HEAD_PAD, (h + 1) * HEAD_PAD)
            qh, rqh = _rms(q[:, sl], QK_HEAD)
            dy = _rope_bwd(dq_ref[:, sl], c, s1, s2)
            dqg = dqg + _colsum(dy * qh)
            dqp_ref[:, sl] = _rms_bwd(dy, qh, rqh, qg_ref[...], QK_HEAD).astype(BF16)
            kh, rkh = _rms(kn[:, sl] + krp, QK_HEAD)
            dyk = _rope_bwd(dk_ref[:, sl], c, s1, s2)
            dkg = dkg + _colsum(dyk * kh)
            dkh = _rms_bwd(dyk, kh, rkh, kg_ref[...], QK_HEAD)
            dkv_ref[:, sl] = dkh.astype(BF16)
            dkr = dkr + jnp.where(rope_lanes, dkh, 0.0)
        dkv_ref[:, QP_COLS:] = dv_ref[...].astype(BF16)
        dp_ref[:, OFF_CKV + 2 * D_RNN:] = dkr.astype(BF16)
        dqa = _dot(dqp_ref[...], wuq_ref[...])
        dp_ref[:, :Q_LORA] = _rms_bwd(dqa, xq, rq, gqa_ref[...], Q_LORA).astype(BF16)
        dkva = _dot_nt(dkv_ref[:, :QP_COLS], wuk_ref[...]) + _dot_nt(dkv_ref[:, QP_COLS:], wv_ref[...])
        dp_ref[:, Q_LORA:OFF_CKV] = _rms_bwd(dkva, xkv, rkv, gkva_ref[...], KV_LORA).astype(BF16)
        _acc(dqg_ref, first, dqg)
        _acc(dkg_ref, first, dkg)
        _acc(dgqa_ref, first, _colsum(dqa * xq))
        _acc(dgkva_ref, first, _colsum(dkva * xkv))

    def row(w):
        return pl.BlockSpec((tm, w), lambda i: (i, 0))

    def acc(w):
        return pl.BlockSpec((1, w), lambda i: (0, 0))

    return pl.pallas_call(
        body, name="qkv_bwd", grid=(n // tm,),
        in_specs=[row(Q_LORA), row(KV_LORA), row(LANES), row(QP_COLS), row(QP_COLS), row(D_ATTN), row(D_RNN), row(D_RNN),
                  _const_spec((1, Q_LORA)), _const_spec((1, KV_LORA)),
                  _const_spec((QP_COLS, Q_LORA)), _const_spec((KV_LORA, QP_COLS)), _const_spec((KV_LORA, D_ATTN)),
                  _const_spec((1, LANES)), _const_spec((1, LANES)), row(LANES), row(LANES), row(LANES)],
        out_specs=[row(P_COLS), row(Q_LORA), row(KV_LORA), row(QP_COLS),
                   row(QP_COLS + D_ATTN), acc(LANES), acc(LANES), acc(Q_LORA), acc(KV_LORA)],
        out_shape=[jax.ShapeDtypeStruct((n, P_COLS), BF16), jax.ShapeDtypeStruct((n, Q_LORA), BF16),
                   jax.ShapeDtypeStruct((n, KV_LORA), BF16), jax.ShapeDtypeStruct((n, QP_COLS), BF16),
                   jax.ShapeDtypeStruct((n, QP_COLS + D_ATTN), BF16),
                   jax.ShapeDtypeStruct((1, LANES), F32), jax.ShapeDtypeStruct((1, LANES), F32),
                   jax.ShapeDtypeStruct((1, Q_LORA), F32), jax.ShapeDtypeStruct((1, KV_LORA), F32)],
        compiler_params=pltpu.CompilerParams(dimension_semantics=("arbitrary",), vmem_limit_bytes=VMEM_LIMIT),
    )(cq, ckv, kr, dq_r, dk_r, dv, dxr, dxg, gqa, gkva, w_uq_p, w_uk_p, w_v, qg, kg, rc, rs1, rs2)


def _softmax_parts(qh, kh, tq, t):
    s = _dot_nt(qh, kh) * (QK_HEAD ** -0.5)
    key = lax.broadcasted_iota(jnp.int32, (tq, t), 1)
    s = jnp.where(key >= PAD_ROWS, s, -jnp.inf)
    e = jnp.exp(s - jnp.max(s, axis=-1, keepdims=True))
    return e, jnp.sum(e, axis=-1, keepdims=True)


def _attn_specs(t, tq):
    nq = t // tq
    qspec = pl.BlockSpec((tq, 2 * HEAD_PAD), lambda b, hp, i: (b * nq + i, hp))
    kspec = pl.BlockSpec((t, 2 * HEAD_PAD), lambda b, hp, i: (b, hp))
    vspec = pl.BlockSpec((t, 2 * V_HEAD), lambda b, hp, i: (b, hp))
    ospec = pl.BlockSpec((tq, 2 * V_HEAD), lambda b, hp, i: (b * nq + i, hp))
    return nq, qspec, kspec, vspec, ospec


def _attn_fwd(q, k, v, srcs=(), scatter=()):
    n = q.shape[0]
    t = _t_pad()
    tq = t // 8
    nq, qspec, kspec, vspec, ospec = _attn_specs(t, tq)
    nk = len(srcs)
    c_in, c_out, c_shape, c_sems = _exchange_specs(srcs, scatter)

    def body(q_ref, k_ref, v_ref, *rest):
        o_ref = rest[nk]
        finish = _ride(3, *_exchange_fns(rest[:nk], rest[nk + 1:2 * nk + 1], rest[2 * nk + 1:], scatter))
        lane = lax.broadcasted_iota(jnp.int32, (tq, 2 * V_HEAD), 1)
        outs = []
        for j in range(2):
            sl = slice(j * HEAD_PAD, (j + 1) * HEAD_PAD)
            e, l = _softmax_parts(q_ref[:, sl], k_ref[:, sl], tq, t)
            outs.append(_dot(e.astype(BF16), v_ref[...]) / l)
        o_ref[...] = jnp.where(lane < V_HEAD, outs[0], outs[1])
        finish()

    res = pl.pallas_call(
        body, name="attn_fwd", grid=(n // t, N_HEADS // 2, nq),
        in_specs=[qspec, kspec, vspec] + c_in, out_specs=[ospec] + c_out,
        out_shape=[jax.ShapeDtypeStruct((n, D_ATTN), F32)] + c_shape, scratch_shapes=c_sems,
        compiler_params=pltpu.CompilerParams(dimension_semantics=("arbitrary", "arbitrary", "arbitrary"),
                                             vmem_limit_bytes=VMEM_LIMIT),
    )(q, k, v, *srcs)
    return res[0], res[1:]


def _attn_bwd(q, k, v, do, srcs=(), scatter=()):
    n = q.shape[0]
    t = _t_pad()
    tq = t // 4
    nq, qspec, kspec, vspec, ospec = _attn_specs(t, tq)
    nk = len(srcs)
    c_in, c_out, c_shape, c_sems = _exchange_specs(srcs, scatter)

    def body(q_ref, k_ref, v_ref, do_ref, *rest):
        dq_ref, dk_ref, dv_ref = rest[nk:nk + 3]
        finish = _ride(3, *_exchange_fns(rest[:nk], rest[nk + 3:2 * nk + 3], rest[2 * nk + 3:], scatter))
        first = pl.program_id(2) == 0
        lane = lax.broadcasted_iota(jnp.int32, (tq, 2 * V_HEAD), 1)
        do = do_ref[...]
        dv = jnp.zeros((t, 2 * V_HEAD), F32)
        for j in range(2):
            sl = slice(j * HEAD_PAD, (j + 1) * HEAD_PAD)
            qh, kh = q_ref[:, sl], k_ref[:, sl]
            e, l = _softmax_parts(qh, kh, tq, t)
            p = e / l
            in_head = (lane < V_HEAD) if j == 0 else (lane >= V_HEAD)
            doh = jnp.where(in_head, do, 0.0).astype(BF16)
            dp = _dot_nt(doh, v_ref[...])
            delta = jnp.sum(p * dp, axis=-1, keepdims=True)
            ds = (p * (dp - delta) * (QK_HEAD ** -0.5)).astype(BF16)
            dq_ref[:, sl] = _dot(ds, kh)
            dkh = _dot_tn(ds, qh)

            @pl.when(first)
            def _():
                dk_ref[:, sl] = dkh

            @pl.when(jnp.logical_not(first))
            def _():
                dk_ref[:, sl] += dkh

            dv = dv + _dot_tn(p.astype(BF16), doh)
        _acc(dv_ref, first, dv)
        finish()

    res = pl.pallas_call(
        body, name="attn_bwd", grid=(n // t, N_HEADS // 2, nq),
        in_specs=[qspec, kspec, vspec, ospec] + c_in, out_specs=[qspec, kspec, vspec] + c_out,
        out_shape=[jax.ShapeDtypeStruct((n, QP_COLS), F32), jax.ShapeDtypeStruct((n, QP_COLS), F32),
                   jax.ShapeDtypeStruct((n, D_ATTN), F32)] + c_shape, scratch_shapes=c_sems,
        compiler_params=pltpu.CompilerParams(dimension_semantics=("arbitrary", "arbitrary", "arbitrary"),
                                             vmem_limit_bytes=VMEM_LIMIT),
    )(q, k, v, do, *srcs)
    return res[:3], res[3:]


SCAN_STEPS = 8


def _scan(chains, t):
    seg = t // 8
    rows = lax.broadcasted_iota(jnp.int32, (8, LANES), 0)

    def step(i, carry):
        carry = list(carry)
        for u in range(SCAN_STEPS):
            j = i * SCAN_STEPS + u
            for n, (a_ref, b_ref, h_ref, p_ref, reverse) in enumerate(chains):
                h, p = carry[n]
                idx = pl.ds(seg - 1 - j if reverse else j, 8, stride=seg)
                a = a_ref[idx, :]
                h = a * h + b_ref[idx, :]
                p = a * p
                h_ref[idx, :] = h
                p_ref[idx, :] = p
                carry[n] = (h, p)
        return tuple(carry)

    init = tuple((jnp.zeros((8, LANES), F32), jnp.ones((8, LANES), F32)) for _ in chains)
    ends = lax.fori_loop(0, seg // SCAN_STEPS, step, init)
    for (_, _, h_ref, p_ref, reverse), (b, a) in zip(chains, ends):
        for d in (1, 2, 4):
            if reverse:
                keep = rows < 8 - d
                a_n, b_n = pltpu.roll(a, 8 - d, 0), pltpu.roll(b, 8 - d, 0)
            else:
                keep = rows >= d
                a_n, b_n = pltpu.roll(a, d, 0), pltpu.roll(b, d, 0)
            b = a * jnp.where(keep, b_n, 0.0) + b
            a = a * jnp.where(keep, a_n, 1.0)
        for s in (range(7) if reverse else range(1, 8)):
            sl = slice(s * seg, (s + 1) * seg)
            carry_in = b[s + 1:s + 2, :] if reverse else b[s - 1:s, :]
            h_ref[sl, :] = h_ref[sl, :] + p_ref[sl, :] * carry_in


def _shift_rows(x, s, rows, t):
    if s == 0:
        return x
    rolled = pltpu.roll(x, s % t, 0)
    return jnp.where(rows >= s, rolled, 0.0) if s > 0 else jnp.where(rows < t + s, rolled, 0.0)


def _neg_expm1(x, exp_x):
    series = -x * (1.0 + x * (0.5 + x * (1.0 / 6 + x * (1.0 / 24))))
    return jnp.where(x > -0.1, series, 1.0 - exp_x)


def _sigmoid(x):
    return 0.5 * jnp.tanh(0.5 * x) + 0.5


def _gelu_parts(x):
    k = math.sqrt(2.0 / math.pi)
    th = jnp.tanh(k * (x + 0.044715 * x * x * x))
    g = 0.5 * x * (1.0 + th)
    dg = 0.5 * (1.0 + th) + 0.5 * x * (1.0 - th * th) * k * (1.0 + 3 * 0.044715 * x * x)
    return g, dg


def _lru_gates(xc, gates, lam_ref, valid, d):
    r = _sigmoid(gates[:, (2 * d) * LANES:(2 * d + 1) * LANES])
    i = _sigmoid(gates[:, (2 * d + 1) * LANES:(2 * d + 2) * LANES])
    neg_lam = -lam_ref[d:d + 1, :]
    sp = jnp.maximum(neg_lam, 0.0) + jnp.log1p(jnp.exp(-jnp.abs(neg_lam)))
    log_a = -LRU_C * r * sp
    a = jnp.exp(log_a)
    m = jnp.maximum(_neg_expm1(2.0 * log_a, a * a), 0.0)
    sq = jnp.sqrt(m)
    b = jnp.where(valid, sq * (i * xc), 0.0)
    return r, i, sp, a, m, sq, b


def _conv(xr, cw_ref, cb_ref, rows, t):
    return (cw_ref[0:1, :] * _shift_rows(xr, 2, rows, t) + cw_ref[1:2, :] * _shift_rows(xr, 1, rows, t)
            + cw_ref[2:3, :] * xr + cw_ref[3:4, :] * _shift_rows(xr, -1, rows, t) + cb_ref[...])


def _rnn_specs(t):
    seq = pl.BlockSpec((t, LANES), lambda cb, b: (b, cb))
    cw = pl.BlockSpec((4, LANES), lambda cb, b: (0, cb))
    vec1 = pl.BlockSpec((1, LANES), lambda cb, b: (0, cb))
    vec2 = pl.BlockSpec((2, LANES), lambda cb, b: (0, cb))
    wblk = pl.BlockSpec((1, LANES, 4 * LANES), lambda cb, b: (cb, 0, 0))
    gbias = pl.BlockSpec((1, 1, 4 * LANES), lambda cb, b: (cb, 0, 0))
    return seq, cw, vec1, vec2, wblk, gbias


def _rnn_fwd(xr, xg, conv_w, conv_b, wblk, gbias, lam):
    n = xr.shape[0]
    t = _t_pad()
    seq, cw, vec1, vec2, wspec, gspec = _rnn_specs(t)

    def body(xr_ref, xg_ref, cw_ref, cb_ref, w_ref, gb_ref, lam_ref, o_ref, a_s, b_s, h_s, p_s):
        rows = lax.broadcasted_iota(jnp.int32, (t, LANES), 0)
        valid = rows >= PAD_ROWS
        xc = _conv(xr_ref[...], cw_ref, cb_ref, rows, t)
        gates = _dot(xc.astype(BF16), w_ref[0]) + gb_ref[0]
        for d in range(2):
            _, _, _, a, _, _, b = _lru_gates(xc, gates, lam_ref, valid, d)
            a_s[d] = a
            b_s[d] = b
        _scan([(a_s.at[d], b_s.at[d], h_s.at[d], p_s.at[d], d == 1) for d in range(2)], t)
        g, _ = _gelu_parts(xg_ref[...])
        o_ref[...] = (h_s[0] + h_s[1]) * g

    return pl.pallas_call(
        body, name="rnn_fwd", grid=(D_RNN // LANES, n // t),
        in_specs=[seq, seq, cw, vec1, wspec, gspec, vec2], out_specs=seq,
        out_shape=jax.ShapeDtypeStruct((n, D_RNN), F32),
        scratch_shapes=[pltpu.VMEM((2, t, LANES), F32)] * 4,
        compiler_params=pltpu.CompilerParams(dimension_semantics=("parallel", "parallel"), vmem_limit_bytes=VMEM_LIMIT),
    )(xr, xg, conv_w, conv_b, wblk, gbias, lam)


def _rnn_bwd(xr, xg, do, conv_w, conv_b, wblk, gbias, lam):
    n = xr.shape[0]
    t = _t_pad()
    seq, cw, vec1, vec2, wspec, gspec = _rnn_specs(t)

    def body(xr_ref, xg_ref, do_ref, cw_ref, cb_ref, w_ref, gb_ref, lam_ref,
             dxr_ref, dxg_ref, dcw_ref, dcb_ref, dw_ref, dgb_ref, dlam_ref,
             a_s, b_s, h_s, l_s, p_s, back_s, r_s, i_s, q_s, dg_s):
        first = pl.program_id(1) == 0
        rows = lax.broadcasted_iota(jnp.int32, (t, LANES), 0)
        valid = rows >= PAD_ROWS
        xr = xr_ref[...]
        xc = _conv(xr, cw_ref, cb_ref, rows, t)
        xcb = xc.astype(BF16)
        gates = _dot(xcb, w_ref[0]) + gb_ref[0]
        sps = []
        for d in range(2):
            r_s[d], i_s[d], sp, a_s[d], _, q_s[d], b_s[d] = _lru_gates(xc, gates, lam_ref, valid, d)
            sps.append(sp)
        _scan([(a_s.at[d], b_s.at[d], h_s.at[d], p_s.at[d], d == 1) for d in range(2)], t)
        g, dg = _gelu_parts(xg_ref[...])
        do = do_ref[...]
        dxg_ref[...] = do * (h_s[0] + h_s[1]) * dg
        b_s[0] = do * g
        for d in range(2):
            back_s[d] = _shift_rows(a_s[d], -1 if d == 0 else 1, rows, t)
        _scan([(back_s.at[d], b_s.at[0], l_s.at[d], p_s.at[d], d == 0) for d in range(2)], t)
        dxc = jnp.zeros((t, LANES), F32)
        dlams = []
        for d in range(2):
            r, i, sp, a, sq = r_s[d], i_s[d], sps[d], a_s[d], q_s[d]
            lam_t = l_s[d]
            da = lam_t * _shift_rows(h_s[d], 1 if d == 0 else -1, rows, t)
            lam_v = jnp.where(valid, lam_t, 0.0)
            dsq = lam_v * (i * xc)
            di = lam_v * sq * xc
            dxc = dxc + lam_v * sq * i
            dm = jnp.where(sq > 0.0, dsq * 0.5 / jnp.where(sq > 0.0, sq, 1.0), 0.0)
            dla = da * a - 2.0 * dm * a * a
            dr = dla * (-LRU_C) * sp
            dsp = _colsum(dla * (-LRU_C) * r)
            dlams.append(dsp * -jax.nn.sigmoid(-lam_ref[d:d + 1, :]))
            dg_s[:, (2 * d) * LANES:(2 * d + 1) * LANES] = (dr * r * (1.0 - r)).astype(BF16)
            dg_s[:, (2 * d + 1) * LANES:(2 * d + 2) * LANES] = (di * i * (1.0 - i)).astype(BF16)
        dgates = dg_s[...]
        dxc = dxc + _dot_nt(dgates, w_ref[0])
        taps = [_shift_rows(dxc, j - 2, rows, t) for j in range(4)]
        dxr_ref[...] = (cw_ref[0:1, :] * taps[0] + cw_ref[1:2, :] * taps[1] + cw_ref[2:3, :] * taps[2]
                        + cw_ref[3:4, :] * taps[3])
        dcw = jnp.concatenate([_colsum(tap * xr) for tap in taps], axis=0)
        _acc(dcw_ref, first, dcw)
        _acc(dcb_ref, first, _colsum(dxc))
        _acc(dw_ref, first, _dot_tn(xcb, dgates)[None])
        _acc(dgb_ref, first, _colsum(dgates.astype(F32))[None])
        _acc(dlam_ref, first, jnp.concatenate(dlams, axis=0))

    return pl.pallas_call(
        body, name="rnn_bwd", grid=(D_RNN // LANES, n // t),
        in_specs=[seq, seq, seq, cw, vec1, wspec, gspec, vec2],
        out_specs=[seq, seq, cw, vec1, wspec, gspec, vec2],
        out_shape=[jax.ShapeDtypeStruct((n, D_RNN), F32), jax.ShapeDtypeStruct((n, D_RNN), F32),
                   jax.ShapeDtypeStruct((4, D_RNN), F32), jax.ShapeDtypeStruct((1, D_RNN), F32),
                   jax.ShapeDtypeStruct((D_RNN // LANES, LANES, 4 * LANES), F32),
                   jax.ShapeDtypeStruct((D_RNN // LANES, 1, 4 * LANES), F32), jax.ShapeDtypeStruct((2, D_RNN), F32)],
        scratch_shapes=[pltpu.VMEM((2, t, LANES), F32)] * 9 + [pltpu.VMEM((t, 4 * LANES), BF16)],
        compiler_params=pltpu.CompilerParams(dimension_semantics=("parallel", "arbitrary"), vmem_limit_bytes=VMEM_LIMIT),
    )(xr, xg, do, conv_w, conv_b, wblk, gbias, lam)


def _post(oa, orn, h0, tgt, ga, gr, g2, w_out, w_gate, w_up, w_down):
    n = oa.shape[0]
    tm = _row_tile(n)
    t = _t_pad()

    def body(oa_ref, or_ref, h0_ref, tgt_ref, ga_ref, gr_ref, g2_ref, wo_ref, wg_ref, wu_ref, wd_ref,
             doa_ref, dor_ref, dh1_ref, mix_ref, h1n_ref, act_ref, dgate_ref, dup_ref, dy_ref,
             loss_ref, dga_ref, dgr_ref, dg2_ref, gate_s, up_s):
        first = pl.program_id(0) == 0
        xa, ra = _rms(oa_ref[...], D_ATTN)
        xr, rr = _rms(or_ref[...], D_RNN)
        mix = jnp.concatenate([(xa * ga_ref[...]).astype(BF16), (xr * gr_ref[...]).astype(BF16)], axis=-1)
        mix_ref[...] = mix.T
        h1 = h0_ref[...] + _dot(mix, wo_ref[...])
        x2, r2 = _rms(h1, D_MODEL)
        h1n = (x2 * g2_ref[...]).astype(BF16)
        h1n_ref[...] = h1n
        y = h1
        for cs in range(0, D_FF, FF_CHUNK):
            sl = slice(cs, cs + FF_CHUNK)
            gate = _dot_nt(h1n, wg_ref[sl, :])
            up = _dot_nt(h1n, wu_ref[sl, :])
            gate_s[:, sl] = gate
            up_s[:, sl] = up
            act = (gate * _sigmoid(gate) * up).astype(BF16)
            act_ref[sl, :] = act.T
            y = y + _dot(act, wd_ref[sl, :])
        row = pl.program_id(0) * tm + lax.broadcasted_iota(jnp.int32, (tm, 1), 0)
        for _ in range(1, n // t):
            row = jnp.where(row >= t, row - t, row)
        err = jnp.where(row >= PAD_ROWS + N_META, y - tgt_ref[...], 0.0)
        _acc(loss_ref, first, jnp.full((1, LANES), 0.5 / D_MODEL, F32) * jnp.sum(err * err))
        dy = err * (1.0 / D_MODEL)
        dyb = dy.astype(BF16)
        dy_ref[...] = dyb
        dh1n = jnp.zeros((tm, D_MODEL), F32)
        for cs in range(0, D_FF, FF_CHUNK):
            sl = slice(cs, cs + FF_CHUNK)
            dact = _dot_nt(dyb, wd_ref[sl, :])
            gate, up = gate_s[:, sl], up_s[:, sl]
            sg = _sigmoid(gate)
            dgate = (dact * up * sg * (1.0 + gate * (1.0 - sg))).astype(BF16)
            dup = (dact * gate * sg).astype(BF16)
            dgate_ref[sl, :] = dgate.T
            dup_ref[sl, :] = dup.T
            dh1n = dh1n + _dot(dgate, wg_ref[sl, :]) + _dot(dup, wu_ref[sl, :])
        _acc(dg2_ref, first, _colsum(dh1n * x2))
        dh1 = dy + _rms_bwd(dh1n, x2, r2, g2_ref[...], D_MODEL)
        dh1_ref[...] = dh1
        dmix = _dot_nt(dh1.astype(BF16), wo_ref[...])
        dma, dmr = dmix[:, :D_ATTN], dmix[:, D_ATTN:]
        _acc(dga_ref, first, _colsum(dma * xa))
        _acc(dgr_ref, first, _colsum(dmr * xr))
        doa_ref[...] = _rms_bwd(dma, xa, ra, ga_ref[...], D_ATTN)
        dor_ref[...] = _rms_bwd(dmr, xr, rr, gr_ref[...], D_RNN)

    def row(w):
        return pl.BlockSpec((tm, w), lambda i: (i, 0))

    def acc(w):
        return pl.BlockSpec((1, w), lambda i: (0, 0))

    def col(w):
        return pl.BlockSpec((w, tm), lambda i: (0, i))

    outs = [(D_ATTN, F32, row), (D_RNN, F32, row), (D_MODEL, F32, row), (D_MODEL, BF16, col), (D_MODEL, BF16, row),
            (D_FF, BF16, col), (D_FF, BF16, col), (D_FF, BF16, col), (D_MODEL, BF16, row)]
    accs = [LANES, D_ATTN, D_RNN, D_MODEL]
    return pl.pallas_call(
        body, name="post", grid=(n // tm,),
        in_specs=[row(D_ATTN), row(D_RNN), row(D_MODEL), row(D_MODEL),
                  _const_spec((1, D_ATTN)), _const_spec((1, D_RNN)), _const_spec((1, D_MODEL)),
                  _const_spec((D_MODEL, D_MODEL)), _const_spec((D_FF, D_MODEL)), _const_spec((D_FF, D_MODEL)),
                  _const_spec((D_FF, D_MODEL))],
        out_specs=[spec(w) for w, _, spec in outs] + [acc(w) for w in accs],
        out_shape=[jax.ShapeDtypeStruct((n, w) if spec is row else (w, n), dt) for w, dt, spec in outs]
        + [jax.ShapeDtypeStruct((1, w), F32) for w in accs],
        scratch_shapes=[pltpu.VMEM((tm, D_FF), F32), pltpu.VMEM((tm, D_FF), F32)],
        compiler_params=pltpu.CompilerParams(dimension_semantics=("arbitrary",), vmem_limit_bytes=VMEM_LIMIT),
    )(oa, orn, h0, tgt, ga, gr, g2, w_out, w_gate, w_up, w_down)


def _in_bwd(dp, h0, dh1, ln1_g, w_in_p, srcs=(), scatter=()):
    n = h0.shape[0]
    tm = _row_tile(n)
    nk = len(srcs)
    c_in, c_out, c_shape, c_sems = _exchange_specs(srcs, scatter)

    def body(dp_ref, h0_ref, dh1_ref, g_ref, w_ref, *rest):
        dh0_ref, dg_ref = rest[nk:nk + 2]
        finish = _ride(1, *_exchange_fns(rest[:nk], rest[nk + 2:2 * nk + 2], rest[2 * nk + 2:], scatter))
        dhn = _dot(dp_ref[...], w_ref[...])
        xhat, r = _rms(h0_ref[...], D_MODEL)
        _acc(dg_ref, pl.program_id(0) == 0, _colsum(dhn * xhat))
        dh0_ref[...] = dh1_ref[...] + _rms_bwd(dhn, xhat, r, g_ref[...], D_MODEL)
        finish()

    def row(w):
        return pl.BlockSpec((tm, w), lambda i: (i, 0))

    res = pl.pallas_call(
        body, name="in_bwd", grid=(n // tm,),
        in_specs=[row(P_COLS), row(D_MODEL), row(D_MODEL), _const_spec((1, D_MODEL)), _const_spec((P_COLS, D_MODEL))] + c_in,
        out_specs=[row(D_MODEL), pl.BlockSpec((1, D_MODEL), lambda i: (0, 0))] + c_out,
        out_shape=[jax.ShapeDtypeStruct((n, D_MODEL), F32), jax.ShapeDtypeStruct((1, D_MODEL), F32)] + c_shape,
        scratch_shapes=c_sems,
        compiler_params=pltpu.CompilerParams(dimension_semantics=("arbitrary",), vmem_limit_bytes=VMEM_LIMIT),
    )(dp, h0, dh1, ln1_g, w_in_p, *srcs)
    return res[:2], res[2:]


def _pick_tile(width, cap):
    best = LANES
    for mult in range(1, width // LANES + 1):
        cand = mult * LANES
        if width % cand == 0 and cand <= cap:
            best = cand
    return best


def _matmul_tn(name, a, b):
    n, ka = a.shape
    kb = b.shape[1]
    ta, tb = _pick_tile(ka, 1408), _pick_tile(kb, 1408)
    tk = n // 4

    def body(a_ref, b_ref, o_ref):
        _acc(o_ref, pl.program_id(2) == 0, _dot_tn(a_ref[...].astype(BF16), b_ref[...].astype(BF16)))

    return pl.pallas_call(
        body, name=name, grid=(ka // ta, kb // tb, n // tk),
        in_specs=[pl.BlockSpec((tk, ta), lambda i, j, k: (k, i)), pl.BlockSpec((tk, tb), lambda i, j, k: (k, j))],
        out_specs=pl.BlockSpec((ta, tb), lambda i, j, k: (i, j)),
        out_shape=jax.ShapeDtypeStruct((ka, kb), F32),
        compiler_params=pltpu.CompilerParams(dimension_semantics=("parallel", "parallel", "arbitrary"),
                                             vmem_limit_bytes=VMEM_LIMIT),
    )(a, b)


def _matmul_shards(name, at, b):
    ka, n = at.shape
    kb = b.shape[1]
    ta, tb = _pick_tile(ka, 1408), _pick_tile(kb, 1408)
    tk = n // 2
    width = ka // N_DEV
    per = ta // width

    def body(a_ref, b_ref, o_ref, acc_ref):
        _acc(acc_ref, pl.program_id(2) == 0, _dot(a_ref[...], b_ref[...].astype(BF16)))

        @pl.when(pl.program_id(2) == pl.num_programs(2) - 1)
        def _():
            for s in range(per):
                o_ref[s] = acc_ref[s * width:(s + 1) * width, :].astype(BF16)

    return pl.pallas_call(
        body, name=name, grid=(ka // ta, kb // tb, n // tk),
        in_specs=[pl.BlockSpec((ta, tk), lambda i, j, k: (i, k)), pl.BlockSpec((tk, tb), lambda i, j, k: (k, j))],
        out_specs=pl.BlockSpec((per, width, tb), lambda i, j, k: (i, 0, j)),
        out_shape=jax.ShapeDtypeStruct((N_DEV, width, kb), BF16),
        scratch_shapes=[pltpu.VMEM((ta, tb), F32)],
        compiler_params=pltpu.CompilerParams(dimension_semantics=("parallel", "parallel", "arbitrary"),
                                             vmem_limit_bytes=VMEM_LIMIT),
    )(at, b)


def _adamw_math(g8_ref, w_ref, m_ref, v_ref, g_ref, d_ref, nm_ref, nv_ref):
    g = g8_ref[0].astype(F32)
    for s in range(1, N_DEV):
        g = g + g8_ref[s].astype(F32)
    g_ref[...] = g
    nm = ADAM_B1 * m_ref[...] + (1.0 - ADAM_B1) * g
    nv = ADAM_B2 * v_ref[...] + (1.0 - ADAM_B2) * (g * g)
    nm_ref[...] = nm
    nv_ref[...] = nv
    m_hat = nm / (1.0 - ADAM_B1 ** ADAM_STEP)
    v_hat = nv / (1.0 - ADAM_B2 ** ADAM_STEP)
    d_ref[...] = -ADAM_LR * (m_hat / (jnp.sqrt(v_hat) + ADAM_EPS) + ADAM_WD * w_ref[...])


def _adamw_many(name, items):
    count = len(items)

    def body(*refs):
        ins, outs = refs[:4 * count], refs[4 * count:]
        for i in range(count):
            _adamw_math(*ins[4 * i:4 * i + 4], *outs[4 * i:4 * i + 4])

    flat = [a for item in items for a in item]
    res = pl.pallas_call(
        body, name=name,
        out_shape=[jax.ShapeDtypeStruct(item[1].shape, F32) for item in items for _ in range(4)],
        compiler_params=pltpu.CompilerParams(vmem_limit_bytes=VMEM_LIMIT),
    )(*flat)
    return [tuple(res[4 * i:4 * i + 4]) for i in range(count)]


def _adamw(name, g8, w, m, v):
    rows, cols = w.shape
    tr = rows
    for cand in (256, 176, 128, 64):
        if rows % cand == 0 and rows > cand:
            tr = cand
            break

    def body(*refs):
        _adamw_math(*refs)

    blk = pl.BlockSpec((tr, cols), lambda i: (i, 0))
    return pl.pallas_call(
        body, name=name, grid=(rows // tr,),
        in_specs=[pl.BlockSpec((N_DEV, tr, cols), lambda i: (0, i, 0)), blk, blk, blk],
        out_specs=[blk] * 4, out_shape=[jax.ShapeDtypeStruct((rows, cols), F32)] * 4,
        compiler_params=pltpu.CompilerParams(dimension_semantics=("parallel",), vmem_limit_bytes=VMEM_LIMIT),
    )(g8, w, m, v)


def _exchange_specs(srcs, scatter):
    nk = len(srcs)
    if not nk:
        return [], [], [], []
    any_spec = pl.BlockSpec(memory_space=pl.ANY)
    out_shape = [jax.ShapeDtypeStruct(s.shape if sc else (N_DEV,) + s.shape, s.dtype) for s, sc in zip(srcs, scatter)]
    sems = [pltpu.SemaphoreType.DMA((nk, N_DEV - 1)), pltpu.SemaphoreType.DMA((nk, N_DEV - 1)),
            pltpu.SemaphoreType.DMA((nk,))]
    return [any_spec] * nk, [any_spec] * nk, out_shape, sems


FLIPS = ((0, 0, 1), (1, 0, 0), (0, 1, 0), (1, 1, 0), (1, 0, 1), (0, 1, 1), (1, 1, 1))
N_CHIP_PEERS = 3


def _exchange_fns(src_refs, out_refs, sems, scatter):
    nk = len(src_refs)
    if not nk:
        return (lambda: None), (lambda: None), (lambda: None)
    send_sems, recv_sems, local_sems = sems
    first = 1 + N_CHIP_PEERS

    def plan():
        x, y, c = lax.axis_index("x"), lax.axis_index("y"), lax.axis_index("c")
        me = 4 * x + 2 * y + c
        peers = [(1 - x if fx else x, 1 - y if fy else y, 1 - c if fc else c) for fx, fy, fc in FLIPS]
        pids = [4 * px + 2 * py + pc for px, py, pc in peers]

        def remote(k, j, src, dst, to):
            return pltpu.make_async_remote_copy(src_ref=src, dst_ref=dst, send_sem=send_sems.at[k, j],
                                                recv_sem=recv_sems.at[k, j], device_id=to, device_id_type=MESH)

        def mine(k, dest):
            return src_refs[k].at[dest] if scatter[k] else src_refs[k]

        local = [pltpu.make_async_copy(mine(k, me), out_refs[k].at[me], local_sems.at[k]) for k in range(nk)]
        direct = [remote(k, j, mine(k, pids[j]), out_refs[k].at[me], peers[j])
                  for k in range(nk) for j in range(len(FLIPS) if scatter[k] else first)]
        relays = {(k, j): remote(k, j, out_refs[k].at[pids[j - N_CHIP_PEERS]], out_refs[k].at[pids[j - N_CHIP_PEERS]], peers[0])
                  for k in range(nk) if not scatter[k] for j in range(first, len(FLIPS))}
        arrivals = {(k, j): remote(k, j, out_refs[k].at[pids[j]], out_refs[k].at[pids[j]], peers[j])
                    for k in range(nk) for j in range(len(FLIPS))}
        return local, direct, relays, arrivals

    def start():
        local, direct, _, _ = plan()
        for cp in local + direct:
            cp.start()

    def relay():
        _, _, relays, arrivals = plan()
        for (k, j), cp in relays.items():
            arrivals[k, j - N_CHIP_PEERS].wait_recv()
            cp.start()

    def wait():
        local, direct, relays, arrivals = plan()
        for (k, j), cp in arrivals.items():
            if (k, j + N_CHIP_PEERS) not in relays:
                cp.wait_recv()
        for cp in direct + list(relays.values()):
            cp.wait_send()
        for cp in local:
            cp.wait()

    return start, relay, wait


def _grid_step(rank):
    step, total = 0, 1
    for axis in range(rank):
        step = step * pl.num_programs(axis) + pl.program_id(axis)
        total = total * pl.num_programs(axis)
    return step, total


def _ride(rank, start, relay, wait):
    step, total = _grid_step(rank)
    pl.when(step == 0)(start)
    pl.when(step == (3 * total) // 4)(relay)
    return lambda: pl.when(step == total - 1)(wait)


def _exchange(name, srcs, scatter):
    nk = len(srcs)
    c_in, c_out, c_shape, c_sems = _exchange_specs(srcs, scatter)

    def body(*refs):
        start, relay, wait = _exchange_fns(refs[:nk], refs[nk:2 * nk], refs[2 * nk:], scatter)
        start()
        relay()
        wait()

    return pl.pallas_call(body, name=name, in_specs=c_in, out_specs=c_out, out_shape=c_shape, scratch_shapes=c_sems)(*srcs)


def _cols_from_shards(g):
    return jnp.transpose(g, (1, 0, 2)).reshape(g.shape[1], -1)


def _cols_to_shards(w):
    return jnp.transpose(w.reshape(w.shape[0], N_DEV, -1), (1, 0, 2))


def _rope_tables(n):
    t = _t_pad()
    pos = (jnp.arange(t, dtype=F32) - PAD_ROWS)
    half = QK_ROPE // 2
    freqs = 1.0 / (ROPE_THETA ** (jnp.arange(half, dtype=F32) / half))
    ang = pos[:, None] * freqs[None, :]
    cos, sin = jnp.cos(ang), jnp.sin(ang)
    z = lambda w: jnp.zeros((t, w), F32)
    c = jnp.concatenate([jnp.ones((t, QK_NOPE), F32), cos, cos, z(HEAD_PAD - QK_HEAD)], axis=1)
    s1 = jnp.concatenate([z(QK_NOPE + half), sin, z(HEAD_PAD - QK_HEAD)], axis=1)
    s2 = jnp.concatenate([z(QK_NOPE), -sin, z(HEAD_PAD - QK_NOPE - half)], axis=1)
    reps = n // t
    return tuple(jnp.tile(a, (reps, 1)) for a in (c, s1, s2))


def _block_diag_gates(lru_wa, lru_wi):
    eye = jnp.eye(2, dtype=lru_wa.dtype)

    def bd(w):
        w = w.reshape(2, D_RNN // LANES, 2, RNN_BW, RNN_BW)
        full = w[:, :, :, :, None, :] * eye[None, None, :, None, :, None]
        return full.reshape(2, D_RNN // LANES, LANES, LANES)

    a, i = bd(lru_wa), bd(lru_wi)
    return jnp.concatenate([a[0], i[0], a[1], i[1]], axis=-1)


def _unblock_gates(dw):
    nb = D_RNN // LANES
    parts = dw.reshape(nb, 2, RNN_BW, 4, 2, RNN_BW)
    diag = jnp.stack([parts[:, k, :, :, k, :] for k in range(2)], axis=1)
    diag = jnp.transpose(diag, (3, 0, 1, 2, 4)).reshape(4, 2 * nb, RNN_BW, RNN_BW)
    return jnp.stack([diag[0], diag[2]]), jnp.stack([diag[1], diag[3]])


WEIGHTS = ("meta_tokens", "ln1_g", "w_in", "q_a_norm_g", "w_uq", "kv_a_norm_g", "w_ukv", "q_norm_g", "k_norm_g",
           "conv_w", "conv_b", "lru_wa", "lru_ba", "lru_wi", "lru_bi", "lru_lambda", "attn_out_g", "rnn_out_g",
           "w_out", "ln2_g", "w_gate", "w_up", "w_down")
BIG = ("w_in", "w_uq", "w_ukv", "w_out", "w_gate", "w_up", "w_down")
TRANSPOSED = ("w_in", "w_uq", "w_gate", "w_up")
ROW_SHARDED = ("w_out", "w_down") + TRANSPOSED
REPLICATED = ("ln1_g", "q_a_norm_g", "kv_a_norm_g", "q_norm_g", "k_norm_g", "conv_b", "lru_wa", "lru_wi",
              "attn_out_g", "rnn_out_g", "ln2_g")
G_FIRST = ("w_in", "meta_tokens")
G_MID = ("w_uq", "w_ukv", "conv_w", "lru_ba", "lru_bi", "lru_lambda")
LATE = ("w_out", "w_gate", "w_up", "w_down")
G_LAST = ("meta_tokens", "ln1_g")


def _local_step(x, tgt, ex):
    nb = x.shape[0]
    t = _t_pad()
    n = nb * t
    local = ex.local
    first = ex.gathered(G_FIRST, ex.run("gather_first", *ex.gather_srcs(G_FIRST)))
    meta, w_in = first["meta_tokens"], first["w_in"]
    lead = jnp.zeros((nb, PAD_ROWS, D_MODEL), F32)
    h0 = jnp.concatenate([lead, jnp.broadcast_to(meta[None], (nb, N_META, D_MODEL)), x], axis=1).reshape(n, D_MODEL)
    tgt_p = jnp.concatenate([jnp.zeros((nb, PAD_ROWS + N_META, D_MODEL), F32), tgt], axis=1).reshape(n, D_MODEL)

    zr = lambda r: jnp.zeros((r, D_MODEL), w_in.dtype)
    w_in_p = jnp.concatenate([w_in[:OFF_CKV], w_in[OFF_KR:], zr(QK_NOPE), w_in[OFF_CKV:OFF_KR], zr(HEAD_PAD - QK_HEAD)],
                             axis=0)
    pad_g = lambda g: jnp.pad(g, ((0, 0), (0, HEAD_PAD - QK_HEAD)))
    qg, kg = pad_g(local["q_norm_g"]), pad_g(local["k_norm_g"])
    rc, rs1, rs2 = _rope_tables(n)
    wblk = _block_diag_gates(local["lru_wa"].reshape(2, -1, RNN_BW, RNN_BW),
                             local["lru_wi"].reshape(2, -1, RNN_BW, RNN_BW)).astype(BF16)
    nblk = D_RNN // LANES

    (hn, cq, ckv, xr, xg, kr), got = _in_proj(h0, local["ln1_g"], w_in_p, *ex.gather_srcs(G_MID))
    w = ex.gathered(G_MID, got)
    w_uq_p = jnp.pad(w["w_uq"].reshape(N_HEADS, QK_HEAD, Q_LORA), ((0, 0), (0, HEAD_PAD - QK_HEAD), (0, 0))
                     ).reshape(QP_COLS, Q_LORA)
    ukv = w["w_ukv"].reshape(KV_LORA, N_HEADS, QK_NOPE + V_HEAD)
    w_uk_p = jnp.pad(ukv[:, :, :QK_NOPE], ((0, 0), (0, 0), (0, HEAD_PAD - QK_NOPE))).reshape(KV_LORA, QP_COLS)
    w_v = ukv[:, :, QK_NOPE:].reshape(KV_LORA, D_ATTN)
    gbias = jnp.stack([w["lru_ba"][0], w["lru_bi"][0], w["lru_ba"][1], w["lru_bi"][1]], axis=0)
    gbias = jnp.transpose(gbias.reshape(4, nblk, LANES), (1, 0, 2)).reshape(nblk, 1, 4 * LANES)

    q, k, v = _qkv_fwd(cq, ckv, kr, local["q_a_norm_g"], local["kv_a_norm_g"], w_uq_p, w_uk_p, w_v, qg, kg, rc, rs1, rs2)
    oa, got = _attn_fwd(q, k, v, *ex.gather_srcs(LATE))
    late = ex.gathered(LATE, got)
    orn = _rnn_fwd(xr, xg, w["conv_w"], local["conv_b"], wblk, gbias, w["lru_lambda"])
    (doa, dor, dh1, mix_t, h1n, act_t, dgate_t, dup_t, dyb, loss, dga, dgr, dg2) = _post(
        oa, orn, h0, tgt_p, local["attn_out_g"], local["rnn_out_g"], local["ln2_g"], late["w_out"], late["w_gate"],
        late["w_up"], late["w_down"])
    wire = {"w_out": _matmul_shards("dw_out", mix_t, dh1), "w_gate": _matmul_shards("dw_gate", dgate_t, h1n),
            "w_up": _matmul_shards("dw_up", dup_t, h1n), "w_down": _matmul_shards("dw_down", act_t, dyb)}
    dxr, dxg, dcw, dcb, dwblk, dgb, dlam = _rnn_bwd(xr, xg, dor, w["conv_w"], local["conv_b"], wblk, gbias, w["lru_lambda"])
    dwa, dwi = _unblock_gates(dwblk)
    dgb = jnp.transpose(dgb.reshape(nblk, 4, LANES), (1, 0, 2)).reshape(4, D_RNN)
    wire.update(ex.to_wire({
        "conv_w": dcw, "conv_b": dcb, "lru_wa": dwa.reshape(-1, RNN_BW), "lru_ba": jnp.stack([dgb[0], dgb[2]]),
        "lru_wi": dwi.reshape(-1, RNN_BW), "lru_bi": jnp.stack([dgb[1], dgb[3]]), "lru_lambda": dlam,
        "attn_out_g": dga, "rnn_out_g": dgr, "ln2_g": dg2}))
    names = tuple(wire)
    (dq_r, dk_r, dv), got = _attn_bwd(q, k, v, doa, *ex.scatter_srcs(names, wire))
    summed = ex.scattered(names, wire, got)
    (dp, qa, kva, dqp, dkv, dqg, dkg, dgqa, dgkva) = _qkv_bwd(
        cq, ckv, kr, dq_r, dk_r, dv, dxr, dxg, local["q_a_norm_g"], local["kv_a_norm_g"], w_uq_p, w_uk_p, w_v, qg, kg,
        rc, rs1, rs2)
    dw_in_p = _matmul_tn("dw_in", dp, hn)
    dw_uq_p = _matmul_tn("dw_uq", dqp, qa)
    dw_kv = _matmul_tn("dw_ukv", kva, dkv)
    kr0 = OFF_CKV + 2 * D_RNN + QK_NOPE
    dw_in = jnp.concatenate([dw_in_p[:OFF_CKV], dw_in_p[kr0:kr0 + QK_ROPE], dw_in_p[OFF_CKV:OFF_CKV + 2 * D_RNN]], axis=0)
    dw_uq = dw_uq_p.reshape(N_HEADS, HEAD_PAD, Q_LORA)[:, :QK_HEAD].reshape(N_HEADS * QK_HEAD, Q_LORA)
    dw_ukv = jnp.concatenate([dw_kv[:, :QP_COLS].reshape(KV_LORA, N_HEADS, HEAD_PAD)[:, :, :QK_NOPE],
                              dw_kv[:, QP_COLS:].reshape(KV_LORA, N_HEADS, V_HEAD)], axis=2).reshape(KV_LORA, -1)
    wire = ex.to_wire({"w_in": dw_in, "q_a_norm_g": dgqa, "w_uq": dw_uq, "kv_a_norm_g": dgkva, "w_ukv": dw_ukv,
                       "q_norm_g": dqg[:, :QK_HEAD], "k_norm_g": dkg[:, :QK_HEAD]})
    names = tuple(wire)
    (dh0, dg1), got = _in_bwd(dp, h0, dh1, local["ln1_g"], w_in_p, *ex.scatter_srcs(names, wire))
    summed.update(ex.scattered(names, wire, got))

    dh0 = dh0.reshape(nb, t, D_MODEL)
    wire = ex.to_wire({"meta_tokens": jnp.sum(dh0[:, PAD_ROWS:PAD_ROWS + N_META], axis=0), "ln1_g": dg1})
    got = ex.run("reduce_last", *ex.scatter_srcs(G_LAST, wire))
    summed.update(ex.scattered(G_LAST, wire, got))
    return loss[0, 0], dh0[:, PAD_ROWS + N_META:], summed


class _MeshExchange:
    def __init__(self, shards):
        self.local = shards

    @staticmethod
    def run(name, srcs, scatter):
        return _exchange(name, srcs, scatter)

    def gather_srcs(self, names):
        return [self.local[k].astype(BF16) if k in BIG else self.local[k] for k in names], [False] * len(names)

    @staticmethod
    def gathered(names, outs):
        return {k: g.reshape(-1, g.shape[-1]) if k in ROW_SHARDED else _cols_from_shards(g) for k, g in zip(names, outs)}

    @staticmethod
    def to_wire(grads):
        wire = {}
        for k, g in grads.items():
            if k in REPLICATED:
                wire[k] = g
            elif k in ROW_SHARDED:
                wire[k] = g.reshape(N_DEV, -1, g.shape[-1]).astype(BF16)
            else:
                wire[k] = _cols_to_shards(g).astype(BF16) if k in BIG else _cols_to_shards(g)
        return wire

    @staticmethod
    def scatter_srcs(names, wire):
        return [wire[k] for k in names], [k not in REPLICATED for k in names]

    @staticmethod
    def scattered(names, wire, outs):
        return dict(zip(names, outs))


def kernel(x, meta_tokens, ln1_g, w_in, q_a_norm_g, w_uq, kv_a_norm_g, w_ukv, q_norm_g, k_norm_g, conv_w, conv_b, lru_wa, lru_ba, lru_wi, lru_bi, lru_lambda, attn_out_g, rnn_out_g, w_out, ln2_g, w_gate, w_up, w_down, loss_target, m_meta_tokens, m_ln1_g, m_w_in, m_q_a_norm_g, m_w_uq, m_kv_a_norm_g, m_w_ukv, m_q_norm_g, m_k_norm_g, m_conv_w, m_conv_b, m_lru_wa, m_lru_ba, m_lru_wi, m_lru_bi, m_lru_lambda, m_attn_out_g, m_rnn_out_g, m_w_out, m_ln2_g, m_w_gate, m_w_up, m_w_down, v_meta_tokens, v_ln1_g, v_w_in, v_q_a_norm_g, v_w_uq, v_kv_a_norm_g, v_w_ukv, v_q_norm_g, v_k_norm_g, v_conv_w, v_conv_b, v_lru_wa, v_lru_ba, v_lru_wi, v_lru_bi, v_lru_lambda, v_attn_out_g, v_rnn_out_g, v_w_out, v_ln2_g, v_w_gate, v_w_up, v_w_down):
    given = (meta_tokens, ln1_g, w_in, q_a_norm_g, w_uq, kv_a_norm_g, w_ukv, q_norm_g, k_norm_g, conv_w, conv_b,
             lru_wa, lru_ba, lru_wi, lru_bi, lru_lambda, attn_out_g, rnn_out_g, w_out, ln2_g, w_gate, w_up, w_down)
    moments_m = (m_meta_tokens, m_ln1_g, m_w_in, m_q_a_norm_g, m_w_uq, m_kv_a_norm_g, m_w_ukv, m_q_norm_g, m_k_norm_g,
                 m_conv_w, m_conv_b, m_lru_wa, m_lru_ba, m_lru_wi, m_lru_bi, m_lru_lambda, m_attn_out_g, m_rnn_out_g,
                 m_w_out, m_ln2_g, m_w_gate, m_w_up, m_w_down)
    moments_v = (v_meta_tokens, v_ln1_g, v_w_in, v_q_a_norm_g, v_w_uq, v_kv_a_norm_g, v_w_ukv, v_q_norm_g, v_k_norm_g,
                 v_conv_w, v_conv_b, v_lru_wa, v_lru_ba, v_lru_wi, v_lru_bi, v_lru_lambda, v_attn_out_g, v_rnn_out_g,
                 v_w_out, v_ln2_g, v_w_gate, v_w_up, v_w_down)
    shapes = {k: a.shape for k, a in zip(WEIGHTS, given)}

    def two_d(k, a):
        a = a.reshape(-1, a.shape[-1])
        return a.T if k in TRANSPOSED else a

    w = {k: two_d(k, a) for k, a in zip(WEIGHTS, given)}
    m = {k: two_d(k, a) for k, a in zip(WEIGHTS, moments_m)}
    v = {k: two_d(k, a) for k, a in zip(WEIGHTS, moments_v)}

    loss_part, grad_x, parts = _local_step(x, loss_target, _MeshExchange(w))

    new = {k: _adamw("adamw_" + k, parts[k], w[k], m[k], v[k]) for k in BIG}
    small = [k for k in WEIGHTS if k not in BIG]
    new.update(zip(small, _adamw_many("adamw_small", [(parts[k], w[k], m[k], v[k]) for k in small])))

    loss = lax.psum(loss_part, ("x", "y", "c"))
    outs = [loss, grad_x]
    for idx in range(4):
        outs += [(new[k][idx].T if k in TRANSPOSED else new[k][idx]).reshape(shapes[k]) for k in WEIGHTS]
    return tuple(outs)
```

```python
import functools
import math

import jax
import jax.numpy as jnp
from jax import lax
from jax.experimental import pallas as pl
from jax.experimental.pallas import tpu as pltpu

F32 = jnp.float32
BF16 = jnp.bfloat16

D_MODEL = 1024
N_META = 16
SEQ = 2048
N_HEADS = 8
QK_NOPE = 64
QK_ROPE = 32
QK_HEAD = QK_NOPE + QK_ROPE
V_HEAD = 64
D_ATTN = N_HEADS * V_HEAD
Q_LORA = 384
KV_LORA = 256
D_RNN = 512
RNN_BW = 64
D_FF = 2816
EPS = 1e-6
LRU_C = 8.0
ROPE_THETA = 10000.0
OFF_CKV = Q_LORA + KV_LORA
OFF_KR = OFF_CKV + QK_ROPE
IN_COLS = OFF_KR + 2 * D_RNN

ADAM_LR = 0.001
ADAM_B1 = 0.9
ADAM_B2 = 0.999
ADAM_EPS = 1e-08
ADAM_WD = 0.01
ADAM_STEP = 10

N_DEV = 8
LANES = 128
HEAD_PAD = LANES
PAD_ROWS = LANES - N_META
QP_COLS = N_HEADS * HEAD_PAD
P_COLS = OFF_CKV + 2 * D_RNN + LANES
FF_CHUNK = D_FF // 2
VMEM_LIMIT = 56 * 1024 * 1024
MESH = pl.DeviceIdType.MESH


def _t_pad():
    return PAD_ROWS + N_META + SEQ


def _row_tile(n):
    return 256 if n % 256 == 0 else 128


def _const_spec(shape):
    nd = len(shape)
    return pl.BlockSpec(shape, lambda *_: (0,) * nd, pipeline_mode=pl.Buffered(1))


def _rms(x, d):
    r = lax.rsqrt(jnp.sum(x * x, axis=-1, keepdims=True) * (1.0 / d) + EPS)
    return x * r, r


def _rms_bwd(dy, xhat, r, g, d):
    dxh = dy * g
    return r * (dxh - xhat * (jnp.sum(dxh * xhat, axis=-1, keepdims=True) * (1.0 / d)))


def _colsum(x):
    return jnp.sum(x, axis=0, keepdims=True)


def _dot(a, b):
    return jnp.dot(a, b, preferred_element_type=F32)


def _dot_nt(a, b):
    return lax.dot_general(a, b, (((1,), (1,)), ((), ())), preferred_element_type=F32)


def _dot_tn(a, b):
    return lax.dot_general(a, b, (((0,), (0,)), ((), ())), preferred_element_type=F32)


def _rope(x, c, s1, s2):
    return x * c + pltpu.roll(x, 16, 1) * s1 + pltpu.roll(x, HEAD_PAD - 16, 1) * s2


def _rope_bwd(dy, c, s1, s2):
    return dy * c + pltpu.roll(dy * s1, HEAD_PAD - 16, 1) + pltpu.roll(dy * s2, 16, 1)


def _acc(ref, first, val):
    @pl.when(first)
    def _():
        ref[...] = val

    @pl.when(jnp.logical_not(first))
    def _():
        ref[...] += val


def _in_proj(h0, ln1_g, w_in_p, srcs=(), scatter=()):
    n = h0.shape[0]
    tm = _row_tile(n)
    nk = len(srcs)
    c_in, c_out, c_shape, c_sems = _exchange_specs(srcs, scatter)

    def body(h_ref, g_ref, w_ref, *rest):
        hn_ref, cq_ref, ckv_ref, xr_ref, xg_ref, kr_ref = rest[nk:nk + 6]
        finish = _ride(1, *_exchange_fns(rest[:nk], rest[nk + 6:2 * nk + 6], rest[2 * nk + 6:], scatter))
        xhat, _ = _rms(h_ref[...], D_MODEL)
        hn = (xhat * g_ref[...]).astype(BF16)
        hn_ref[...] = hn
        p = _dot_nt(hn, w_ref[...])
        cq_ref[...] = p[:, :Q_LORA]
        ckv_ref[...] = p[:, Q_LORA:OFF_CKV]
        xr_ref[...] = p[:, OFF_CKV:OFF_CKV + D_RNN]
        xg_ref[...] = p[:, OFF_CKV + D_RNN:OFF_CKV + 2 * D_RNN]
        kr_ref[...] = p[:, OFF_CKV + 2 * D_RNN:]
        finish()

    def row(w):
        return pl.BlockSpec((tm, w), lambda i: (i, 0))

    widths = (D_MODEL, Q_LORA, KV_LORA, D_RNN, D_RNN, LANES)
    res = pl.pallas_call(
        body, name="in_proj", grid=(n // tm,),
        in_specs=[row(D_MODEL), _const_spec((1, D_MODEL)), _const_spec((P_COLS, D_MODEL))] + c_in,
        out_specs=[row(w) for w in widths] + c_out,
        out_shape=[jax.ShapeDtypeStruct((n, w), BF16 if k == 0 else F32) for k, w in enumerate(widths)] + c_shape,
        scratch_shapes=c_sems,
        compiler_params=pltpu.CompilerParams(dimension_semantics=("arbitrary",), vmem_limit_bytes=VMEM_LIMIT),
    )(h0, ln1_g, w_in_p, *srcs)
    return res[:6], res[6:]


def _qkv_fwd(cq, ckv, kr, gqa, gkva, w_uq_p, w_uk_p, w_v, qg, kg, rc, rs1, rs2):
    n = cq.shape[0]
    tm = _row_tile(n)

    def body(cq_ref, ckv_ref, kr_ref, gqa_ref, gkva_ref, wuq_ref, wuk_ref, wv_ref, qg_ref, kg_ref,
             c_ref, s1_ref, s2_ref, q_ref, k_ref, v_ref):
        xq, _ = _rms(cq_ref[...], Q_LORA)
        qa = (xq * gqa_ref[...]).astype(BF16)
        q = _dot_nt(qa, wuq_ref[...])
        xkv, _ = _rms(ckv_ref[...], KV_LORA)
        kva = (xkv * gkva_ref[...]).astype(BF16)
        kn = _dot(kva, wuk_ref[...])
        v_ref[...] = _dot(kva, wv_ref[...]).astype(BF16)
        krp = kr_ref[...]
        c, s1, s2 = c_ref[...], s1_ref[...], s2_ref[...]
        for h in range(N_HEADS):
            sl = slice(h * HEAD_PAD, (h + 1) * HEAD_PAD)
            qh, _ = _rms(q[:, sl], QK_HEAD)
            q_ref[:, sl] = _rope(qh * qg_ref[...], c, s1, s2).astype(BF16)
            kh, _ = _rms(kn[:, sl] + krp, QK_HEAD)
            k_ref[:, sl] = _rope(kh * kg_ref[...], c, s1, s2).astype(BF16)

    def row(w):
        return pl.BlockSpec((tm, w), lambda i: (i, 0))

    return pl.pallas_call(
        body, name="qkv_fwd", grid=(n // tm,),
        in_specs=[row(Q_LORA), row(KV_LORA), row(LANES), _const_spec((1, Q_LORA)), _const_spec((1, KV_LORA)),
                  _const_spec((QP_COLS, Q_LORA)), _const_spec((KV_LORA, QP_COLS)), _const_spec((KV_LORA, D_ATTN)),
                  _const_spec((1, LANES)), _const_spec((1, LANES)), row(LANES), row(LANES), row(LANES)],
        out_specs=[row(QP_COLS), row(QP_COLS), row(D_ATTN)],
        out_shape=[jax.ShapeDtypeStruct((n, QP_COLS), BF16), jax.ShapeDtypeStruct((n, QP_COLS), BF16),
                   jax.ShapeDtypeStruct((n, D_ATTN), BF16)],
        compiler_params=pltpu.CompilerParams(dimension_semantics=("parallel",), vmem_limit_bytes=VMEM_LIMIT),
    )(cq, ckv, kr, gqa, gkva, w_uq_p, w_uk_p, w_v, qg, kg, rc, rs1, rs2)


def _qkv_bwd(cq, ckv, kr, dq_r, dk_r, dv, dxr, dxg, gqa, gkva, w_uq_p, w_uk_p, w_v, qg, kg, rc, rs1, rs2):
    n = cq.shape[0]
    tm = _row_tile(n)

    def body(cq_ref, ckv_ref, kr_ref, dq_ref, dk_ref, dv_ref, dxr_ref, dxg_ref, gqa_ref, gkva_ref, wuq_ref, wuk_ref,
             wv_ref, qg_ref, kg_ref, c_ref, s1_ref, s2_ref,
             dp_ref, qa_ref, kva_ref, dqp_ref, dkv_ref, dqg_ref, dkg_ref, dgqa_ref, dgkva_ref):
        first = pl.program_id(0) == 0
        dp_ref[:, OFF_CKV:OFF_CKV + D_RNN] = dxr_ref[...].astype(BF16)
        dp_ref[:, OFF_CKV + D_RNN:OFF_CKV + 2 * D_RNN] = dxg_ref[...].astype(BF16)
        xq, rq = _rms(cq_ref[...], Q_LORA)
        qa = (xq * gqa_ref[...]).astype(BF16)
        qa_ref[...] = qa
        q = _dot_nt(qa, wuq_ref[...])
        xkv, rkv = _rms(ckv_ref[...], KV_LORA)
        kva = (xkv * gkva_ref[...]).astype(BF16)
        kva_ref[...] = kva
        kn = _dot(kva, wuk_ref[...])
        krp = kr_ref[...]
        c, s1, s2 = c_ref[...], s1_ref[...], s2_ref[...]
        lane = lax.broadcasted_iota(jnp.int32, (tm, HEAD_PAD), 1)
        rope_lanes = jnp.logical_and(lane >= QK_NOPE, lane < QK_HEAD)
        dqg = jnp.zeros((1, HEAD_PAD), F32)
        dkg = jnp.zeros((1, HEAD_PAD), F32)
        dkr = jnp.zeros((tm, HEAD_PAD), F32)
        for h in range(N_HEADS):
            sl = slice(h * HEAD_PAD, (h + 1) * HEAD_PAD)
            qh, rqh = _rms(q[:, sl], QK_HEAD)
            dy = _rope_bwd(dq_ref[:, sl], c, s1, s2)
            dqg = dqg + _colsum(dy * qh)
            dqp_ref[:, sl] = _rms_bwd(dy, qh, rqh, qg_ref[...], QK_HEAD).astype(BF16)
            kh, rkh = _rms(kn[:, sl] + krp, QK_HEAD)
            dyk = _rope_bwd(dk_ref[:, sl], c, s1, s2)
            dkg = dkg + _colsum(dyk * kh)
            dkh = _rms_bwd(dyk, kh, rkh, kg_ref[...], QK_HEAD)
            dkv_ref[:, sl] = dkh.astype(BF16)
            dkr = dkr + jnp.where(rope_lanes, dkh, 0.0)
        dkv_ref[:, QP_COLS:] = dv_ref[...].astype(BF16)
        dp_ref[:, OFF_CKV + 2 * D_RNN:] = dkr.astype(BF16)
        dqa = _dot(dqp_ref[...], wuq_ref[...])
        dp_ref[:, :Q_LORA] = _rms_bwd(dqa, xq, rq, gqa_ref[...], Q_LORA).astype(BF16)
        dkva = _dot_nt(dkv_ref[:, :QP_COLS], wuk_ref[...]) + _dot_nt(dkv_ref[:, QP_COLS:], wv_ref[...])
        dp_ref[:, Q_LORA:OFF_CKV] = _rms_bwd(dkva, xkv, rkv, gkva_ref[...], KV_LORA).astype(BF16)
        _acc(dqg_ref, first, dqg)
        _acc(dkg_ref, first, dkg)
        _acc(dgqa_ref, first, _colsum(dqa * xq))
        _acc(dgkva_ref, first, _colsum(dkva * xkv))

    def row(w):
        return pl.BlockSpec((tm, w), lambda i: (i, 0))

    def acc(w):
        return pl.BlockSpec((1, w), lambda i: (0, 0))

    return pl.pallas_call(
        body, name="qkv_bwd", grid=(n // tm,),
        in_specs=[row(Q_LORA), row(KV_LORA), row(LANES), row(QP_COLS), row(QP_COLS), row(D_ATTN), row(D_RNN), row(D_RNN),
                  _const_spec((1, Q_LORA)), _const_spec((1, KV_LORA)),
                  _const_spec((QP_COLS, Q_LORA)), _const_spec((KV_LORA, QP_COLS)), _const_spec((KV_LORA, D_ATTN)),
                  _const_spec((1, LANES)), _const_spec((1, LANES)), row(LANES), row(LANES), row(LANES)],
        out_specs=[row(P_COLS), row(Q_LORA), row(KV_LORA), row(QP_COLS),
                   row(QP_COLS + D_ATTN), acc(LANES), acc(LANES), acc(Q_LORA), acc(KV_LORA)],
        out_shape=[jax.ShapeDtypeStruct((n, P_COLS), BF16), jax.ShapeDtypeStruct((n, Q_LORA), BF16),
                   jax.ShapeDtypeStruct((n, KV_LORA), BF16), jax.ShapeDtypeStruct((n, QP_COLS), BF16),
                   jax.ShapeDtypeStruct((n, QP_COLS + D_ATTN), BF16),
                   jax.ShapeDtypeStruct((1, LANES), F32), jax.ShapeDtypeStruct((1, LANES), F32),
                   jax.ShapeDtypeStruct((1, Q_LORA), F32), jax.ShapeDtypeStruct((1, KV_LORA), F32)],
        compiler_params=pltpu.CompilerParams(dimension_semantics=("arbitrary",), vmem_limit_bytes=VMEM_LIMIT),
    )(cq, ckv, kr, dq_r, dk_r, dv, dxr, dxg, gqa, gkva, w_uq_p, w_uk_p, w_v, qg, kg, rc, rs1, rs2)


KEY_CHUNK = 4 * LANES


def _key_chunks(t):
    count = max(t // KEY_CHUNK, 1)
    first = t - KEY_CHUNK * (count - 1)
    return [(0, first)] + [(first + KEY_CHUNK * c, KEY_CHUNK) for c in range(count - 1)]


def _softmax_parts(qh, k_ref, sl, tq, t):
    scores = []
    for start, size in _key_chunks(t):
        s = _dot_nt(qh, k_ref[start:start + size, sl]) * (QK_HEAD ** -0.5)
        if start < PAD_ROWS:
            key = lax.broadcasted_iota(jnp.int32, (tq, size), 1) + start
            s = jnp.where(key >= PAD_ROWS, s, -jnp.inf)
        scores.append(s)
    top = functools.reduce(jnp.maximum, [jnp.max(s, axis=-1, keepdims=True) for s in scores])
    es = [jnp.exp(s - top) for s in scores]
    return es, functools.reduce(jnp.add, [jnp.sum(e, axis=-1, keepdims=True) for e in es])


def _attn_specs(t, tq):
    nq = t // tq
    qspec = pl.BlockSpec((tq, 2 * HEAD_PAD), lambda b, hp, i: (b * nq + i, hp))
    kspec = pl.BlockSpec((t, 2 * HEAD_PAD), lambda b, hp, i: (b, hp))
    vspec = pl.BlockSpec((t, 2 * V_HEAD), lambda b, hp, i: (b, hp))
    ospec = pl.BlockSpec((tq, 2 * V_HEAD), lambda b, hp, i: (b * nq + i, hp))
    return nq, qspec, kspec, vspec, ospec


def _attn_fwd(q, k, v, srcs=(), scatter=()):
    n = q.shape[0]
    t = _t_pad()
    tq = t // 2
    nq, qspec, kspec, vspec, ospec = _attn_specs(t, tq)
    nk = len(srcs)
    c_in, c_out, c_shape, c_sems = _exchange_specs(srcs, scatter)

    def body(q_ref, k_ref, v_ref, *rest):
        o_ref = rest[nk]
        finish = _ride(3, *_exchange_fns(rest[:nk], rest[nk + 1:2 * nk + 1], rest[2 * nk + 1:], scatter))
        lane = lax.broadcasted_iota(jnp.int32, (tq, 2 * V_HEAD), 1)
        outs = []
        for j in range(2):
            sl = slice(j * HEAD_PAD, (j + 1) * HEAD_PAD)
            es, l = _softmax_parts(q_ref[:, sl], k_ref, sl, tq, t)
            pv = [_dot(e.astype(BF16), v_ref[start:start + size, :]) for e, (start, size) in zip(es, _key_chunks(t))]
            outs.append(functools.reduce(jnp.add, pv) / l)
        o_ref[...] = jnp.where(lane < V_HEAD, outs[0], outs[1])
        finish()

    res = pl.pallas_call(
        body, name="attn_fwd", grid=(n // t, N_HEADS // 2, nq),
        in_specs=[qspec, kspec, vspec] + c_in, out_specs=[ospec] + c_out,
        out_shape=[jax.ShapeDtypeStruct((n, D_ATTN), F32)] + c_shape, scratch_shapes=c_sems,
        compiler_params=pltpu.CompilerParams(dimension_semantics=("arbitrary", "arbitrary", "arbitrary"),
                                             vmem_limit_bytes=VMEM_LIMIT),
    )(q, k, v, *srcs)
    return res[0], res[1:]


def _attn_bwd(q, k, v, do, o, srcs=(), scatter=()):
    n = q.shape[0]
    t = _t_pad()
    tq = t // 2
    nq, qspec, kspec, vspec, ospec = _attn_specs(t, tq)
    nk = len(srcs)
    c_in, c_out, c_shape, c_sems = _exchange_specs(srcs, scatter)

    def body(q_ref, k_ref, v_ref, do_ref, o_ref, *rest):
        dq_ref, dk_ref, dv_ref = rest[nk:nk + 3]
        finish = _ride(3, *_exchange_fns(rest[:nk], rest[nk + 3:2 * nk + 3], rest[2 * nk + 3:], scatter))

        @pl.when(pl.program_id(2) == 0)
        def _():
            dk_ref[...] = jnp.zeros_like(dk_ref)
            dv_ref[...] = jnp.zeros_like(dv_ref)

        lane = lax.broadcasted_iota(jnp.int32, (tq, 2 * V_HEAD), 1)
        do = do_ref[...]
        do_o = do * o_ref[...]
        chunks = _key_chunks(t)
        dvs = [None] * len(chunks)
        for j in range(2):
            sl = slice(j * HEAD_PAD, (j + 1) * HEAD_PAD)
            qh = q_ref[:, sl]
            es, l = _softmax_parts(qh, k_ref, sl, tq, t)
            inv_l = 1.0 / l
            in_head = (lane < V_HEAD) if j == 0 else (lane >= V_HEAD)
            doh = jnp.where(in_head, do, 0.0).astype(BF16)
            delta = jnp.sum(jnp.where(in_head, do_o, 0.0), axis=-1, keepdims=True)
            dq = jnp.zeros((tq, HEAD_PAD), F32)
            for c, (start, size) in enumerate(chunks):
                rows = slice(start, start + size)
                p = es[c] * inv_l
                dp = _dot_nt(doh, v_ref[rows, :])
                ds = (p * (dp - delta) * (QK_HEAD ** -0.5)).astype(BF16)
                dq = dq + _dot(ds, k_ref[rows, sl])
                dk_ref[rows, sl] += _dot_tn(ds, qh)
                dvc = _dot_tn(p.astype(BF16), doh)
                dvs[c] = dvc if dvs[c] is None else dvs[c] + dvc
            dq_ref[:, sl] = dq
        for (start, size), dvc in zip(chunks, dvs):
            dv_ref[start:start + size, :] += dvc
        finish()

    res = pl.pallas_call(
        body, name="attn_bwd", grid=(n // t, N_HEADS // 2, nq),
        in_specs=[qspec, kspec, vspec, ospec, ospec] + c_in, out_specs=[qspec, kspec, vspec] + c_out,
        out_shape=[jax.ShapeDtypeStruct((n, QP_COLS), F32), jax.ShapeDtypeStruct((n, QP_COLS), F32),
                   jax.ShapeDtypeStruct((n, D_ATTN), F32)] + c_shape, scratch_shapes=c_sems,
        compiler_params=pltpu.CompilerParams(dimension_semantics=("arbitrary", "arbitrary", "arbitrary"),
                                             vmem_limit_bytes=VMEM_LIMIT),
    )(q, k, v, do, o, *srcs)
    return res[:3], res[3:]


SCAN_STEPS = 8


def _scan(chains, t):
    seg = t // 8
    rows = lax.broadcasted_iota(jnp.int32, (8, LANES), 0)

    def step(i, carry):
        carry = list(carry)
        for u in range(SCAN_STEPS):
            j = i * SCAN_STEPS + u
            for n, (a_ref, b_ref, h_ref, p_ref, reverse) in enumerate(chains):
                h, p = carry[n]
                idx = pl.ds(seg - 1 - j if reverse else j, 8, stride=seg)
                a = a_ref[idx, :]
                h = a * h + b_ref[idx, :]
                p = a * p
                h_ref[idx, :] = h
                p_ref[idx, :] = p
                carry[n] = (h, p)
        return tuple(carry)

    init = tuple((jnp.zeros((8, LANES), F32), jnp.ones((8, LANES), F32)) for _ in chains)
    ends = lax.fori_loop(0, seg // SCAN_STEPS, step, init)
    for (_, _, h_ref, p_ref, reverse), (b, a) in zip(chains, ends):
        for d in (1, 2, 4):
            if reverse:
                keep = rows < 8 - d
                a_n, b_n = pltpu.roll(a, 8 - d, 0), pltpu.roll(b, 8 - d, 0)
            else:
                keep = rows >= d
                a_n, b_n = pltpu.roll(a, d, 0), pltpu.roll(b, d, 0)
            b = a * jnp.where(keep, b_n, 0.0) + b
            a = a * jnp.where(keep, a_n, 1.0)
        for s in (range(7) if reverse else range(1, 8)):
            sl = slice(s * seg, (s + 1) * seg)
            carry_in = b[s + 1:s + 2, :] if reverse else b[s - 1:s, :]
            h_ref[sl, :] = h_ref[sl, :] + p_ref[sl, :] * carry_in


def _shift_rows(x, s, rows, t):
    if s == 0:
        return x
    rolled = pltpu.roll(x, s % t, 0)
    return jnp.where(rows >= s, rolled, 0.0) if s > 0 else jnp.where(rows < t + s, rolled, 0.0)


def _neg_expm1(x, exp_x):
    series = -x * (1.0 + x * (0.5 + x * (1.0 / 6 + x * (1.0 / 24))))
    return jnp.where(x > -0.1, series, 1.0 - exp_x)


def _sigmoid(x):
    return 0.5 * jnp.tanh(0.5 * x) + 0.5


def _gelu_parts(x):
    k = math.sqrt(2.0 / math.pi)
    th = jnp.tanh(k * (x + 0.044715 * x * x * x))
    g = 0.5 * x * (1.0 + th)
    dg = 0.5 * (1.0 + th) + 0.5 * x * (1.0 - th * th) * k * (1.0 + 3 * 0.044715 * x * x)
    return g, dg


def _lru_gates(xc, gates, lam_ref, valid, d):
    r = _sigmoid(gates[:, (2 * d) * LANES:(2 * d + 1) * LANES])
    i = _sigmoid(gates[:, (2 * d + 1) * LANES:(2 * d + 2) * LANES])
    neg_lam = -lam_ref[d:d + 1, :]
    sp = jnp.maximum(neg_lam, 0.0) + jnp.log1p(jnp.exp(-jnp.abs(neg_lam)))
    log_a = -LRU_C * r * sp
    a = jnp.exp(log_a)
    m = jnp.maximum(_neg_expm1(2.0 * log_a, a * a), 0.0)
    sq = jnp.sqrt(m)
    b = jnp.where(valid, sq * (i * xc), 0.0)
    return r, i, sp, a, m, sq, b


def _conv(xr, cw_ref, cb_ref, rows, t):
    return (cw_ref[0:1, :] * _shift_rows(xr, 2, rows, t) + cw_ref[1:2, :] * _shift_rows(xr, 1, rows, t)
            + cw_ref[2:3, :] * xr + cw_ref[3:4, :] * _shift_rows(xr, -1, rows, t) + cb_ref[...])


def _rnn_specs(t):
    seq = pl.BlockSpec((t, LANES), lambda cb, b: (b, cb))
    cw = pl.BlockSpec((4, LANES), lambda cb, b: (0, cb))
    vec1 = pl.BlockSpec((1, LANES), lambda cb, b: (0, cb))
    vec2 = pl.BlockSpec((2, LANES), lambda cb, b: (0, cb))
    wblk = pl.BlockSpec((1, LANES, 4 * LANES), lambda cb, b: (cb, 0, 0))
    gbias = pl.BlockSpec((1, 1, 4 * LANES), lambda cb, b: (cb, 0, 0))
    return seq, cw, vec1, vec2, wblk, gbias


def _rnn_fwd(xr, xg, conv_w, conv_b, wblk, gbias, lam):
    n = xr.shape[0]
    t = _t_pad()
    seq, cw, vec1, vec2, wspec, gspec = _rnn_specs(t)

    def body(xr_ref, xg_ref, cw_ref, cb_ref, w_ref, gb_ref, lam_ref, o_ref, a_s, b_s, h_s, p_s):
        rows = lax.broadcasted_iota(jnp.int32, (t, LANES), 0)
        valid = rows >= PAD_ROWS
        xc = _conv(xr_ref[...], cw_ref, cb_ref, rows, t)
        gates = _dot(xc.astype(BF16), w_ref[0]) + gb_ref[0]
        for d in range(2):
            _, _, _, a, _, _, b = _lru_gates(xc, gates, lam_ref, valid, d)
            a_s[d] = a
            b_s[d] = b
        _scan([(a_s.at[d], b_s.at[d], h_s.at[d], p_s.at[d], d == 1) for d in range(2)], t)
        g, _ = _gelu_parts(xg_ref[...])
        o_ref[...] = (h_s[0] + h_s[1]) * g

    return pl.pallas_call(
        body, name="rnn_fwd", grid=(D_RNN // LANES, n // t),
        in_specs=[seq, seq, cw, vec1, wspec, gspec, vec2], out_specs=seq,
        out_shape=jax.ShapeDtypeStruct((n, D_RNN), F32),
        scratch_shapes=[pltpu.VMEM((2, t, LANES), F32)] * 4,
        compiler_params=pltpu.CompilerParams(dimension_semantics=("parallel", "parallel"), vmem_limit_bytes=VMEM_LIMIT),
    )(xr, xg, conv_w, conv_b, wblk, gbias, lam)


def _rnn_bwd(xr, xg, do, conv_w, conv_b, wblk, gbias, lam):
    n = xr.shape[0]
    t = _t_pad()
    seq, cw, vec1, vec2, wspec, gspec = _rnn_specs(t)

    def body(xr_ref, xg_ref, do_ref, cw_ref, cb_ref, w_ref, gb_ref, lam_ref,
             dxr_ref, dxg_ref, dcw_ref, dcb_ref, dw_ref, dgb_ref, dlam_ref,
             a_s, b_s, h_s, l_s, p_s, back_s, r_s, i_s, q_s, dg_s):
        first = pl.program_id(1) == 0
        rows = lax.broadcasted_iota(jnp.int32, (t, LANES), 0)
        valid = rows >= PAD_ROWS
        xr = xr_ref[...]
        xc = _conv(xr, cw_ref, cb_ref, rows, t)
        xcb = xc.astype(BF16)
        gates = _dot(xcb, w_ref[0]) + gb_ref[0]
        sps = []
        for d in range(2):
            r_s[d], i_s[d], sp, a_s[d], _, q_s[d], b_s[d] = _lru_gates(xc, gates, lam_ref, valid, d)
            sps.append(sp)
        _scan([(a_s.at[d], b_s.at[d], h_s.at[d], p_s.at[d], d == 1) for d in range(2)], t)
        g, dg = _gelu_parts(xg_ref[...])
        do = do_ref[...]
        dxg_ref[...] = do * (h_s[0] + h_s[1]) * dg
        b_s[0] = do * g
        for d in range(2):
            back_s[d] = _shift_rows(a_s[d], -1 if d == 0 else 1, rows, t)
        _scan([(back_s.at[d], b_s.at[0], l_s.at[d], p_s.at[d], d == 0) for d in range(2)], t)
        dxc = jnp.zeros((t, LANES), F32)
        dlams = []
        for d in range(2):
            r, i, sp, a, sq = r_s[d], i_s[d], sps[d], a_s[d], q_s[d]
            lam_t = l_s[d]
            da = lam_t * _shift_rows(h_s[d], 1 if d == 0 else -1, rows, t)
            lam_v = jnp.where(valid, lam_t, 0.0)
            dsq = lam_v * (i * xc)
            di = lam_v * sq * xc
            dxc = dxc + lam_v * sq * i
            dm = jnp.where(sq > 0.0, dsq * 0.5 / jnp.where(sq > 0.0, sq, 1.0), 0.0)
            dla = da * a - 2.0 * dm * a * a
            dr = dla * (-LRU_C) * sp
            dsp = _colsum(dla * (-LRU_C) * r)
            dlams.append(dsp * -jax.nn.sigmoid(-lam_ref[d:d + 1, :]))
            dg_s[:, (2 * d) * LANES:(2 * d + 1) * LANES] = (dr * r * (1.0 - r)).astype(BF16)
            dg_s[:, (2 * d + 1) * LANES:(2 * d + 2) * LANES] = (di * i * (1.0 - i)).astype(BF16)
        dgates = dg_s[...]
        dxc = dxc + _dot_nt(dgates, w_ref[0])
        taps = [_shift_rows(dxc, j - 2, rows, t) for j in range(4)]
        dxr_ref[...] = (cw_ref[0:1, :] * taps[0] + cw_ref[1:2, :] * taps[1] + cw_ref[2:3, :] * taps[2]
                        + cw_ref[3:4, :] * taps[3])
        dcw = jnp.concatenate([_colsum(tap * xr) for tap in taps], axis=0)
        _acc(dcw_ref, first, dcw)
        _acc(dcb_ref, first, _colsum(dxc))
        _acc(dw_ref, first, _dot_tn(xcb, dgates)[None])
        _acc(dgb_ref, first, _colsum(dgates.astype(F32))[None])
        _acc(dlam_ref, first, jnp.concatenate(dlams, axis=0))

    return pl.pallas_call(
        body, name="rnn_bwd", grid=(D_RNN // LANES, n // t),
        in_specs=[seq, seq, seq, cw, vec1, wspec, gspec, vec2],
        out_specs=[seq, seq, cw, vec1, wspec, gspec, vec2],
        out_shape=[jax.ShapeDtypeStruct((n, D_RNN), F32), jax.ShapeDtypeStruct((n, D_RNN), F32),
                   jax.ShapeDtypeStruct((4, D_RNN), F32), jax.ShapeDtypeStruct((1, D_RNN), F32),
                   jax.ShapeDtypeStruct((D_RNN // LANES, LANES, 4 * LANES), F32),
                   jax.ShapeDtypeStruct((D_RNN // LANES, 1, 4 * LANES), F32), jax.ShapeDtypeStruct((2, D_RNN), F32)],
        scratch_shapes=[pltpu.VMEM((2, t, LANES), F32)] * 9 + [pltpu.VMEM((t, 4 * LANES), BF16)],
        compiler_params=pltpu.CompilerParams(dimension_semantics=("parallel", "arbitrary"), vmem_limit_bytes=VMEM_LIMIT),
    )(xr, xg, do, conv_w, conv_b, wblk, gbias, lam)


def _post(oa, orn, h0, tgt, ga, gr, g2, w_out, w_gate, w_up, w_down):
    n = oa.shape[0]
    tm = _row_tile(n)
    t = _t_pad()

    def body(oa_ref, or_ref, h0_ref, tgt_ref, ga_ref, gr_ref, g2_ref, wo_ref, wg_ref, wu_ref, wd_ref,
             doa_ref, dor_ref, dh1_ref, mix_ref, h1n_ref, act_ref, dgate_ref, dup_ref, dy_ref,
             loss_ref, dga_ref, dgr_ref, dg2_ref, gate_s, up_s):
        first = pl.program_id(0) == 0
        xa, ra = _rms(oa_ref[...], D_ATTN)
        xr, rr = _rms(or_ref[...], D_RNN)
        mix = jnp.concatenate([(xa * ga_ref[...]).astype(BF16), (xr * gr_ref[...]).astype(BF16)], axis=-1)
        mix_ref[...] = mix.T
        h1 = h0_ref[...] + _dot(mix, wo_ref[...])
        x2, r2 = _rms(h1, D_MODEL)
        h1n = (x2 * g2_ref[...]).astype(BF16)
        h1n_ref[...] = h1n
        y = h1
        for cs in range(0, D_FF, FF_CHUNK):
            sl = slice(cs, cs + FF_CHUNK)
            gate = _dot_nt(h1n, wg_ref[sl, :])
            up = _dot_nt(h1n, wu_ref[sl, :])
            gate_s[:, sl] = gate
            up_s[:, sl] = up
            act = (gate * _sigmoid(gate) * up).astype(BF16)
            act_ref[sl, :] = act.T
            y = y + _dot(act, wd_ref[sl, :])
        row = pl.program_id(0) * tm + lax.broadcasted_iota(jnp.int32, (tm, 1), 0)
        for _ in range(1, n // t):
            row = jnp.where(row >= t, row - t, row)
        err = jnp.where(row >= PAD_ROWS + N_META, y - tgt_ref[...], 0.0)
        _acc(loss_ref, first, jnp.full((1, LANES), 0.5 / D_MODEL, F32) * jnp.sum(err * err))
        dy = err * (1.0 / D_MODEL)
        dyb = dy.astype(BF16)
        dy_ref[...] = dyb
        dh1n = jnp.zeros((tm, D_MODEL), F32)
        for cs in range(0, D_FF, FF_CHUNK):
            sl = slice(cs, cs + FF_CHUNK)
            dact = _dot_nt(dyb, wd_ref[sl, :])
            gate, up = gate_s[:, sl], up_s[:, sl]
            sg = _sigmoid(gate)
            dgate = (dact * up * sg * (1.0 + gate * (1.0 - sg))).astype(BF16)
            dup = (dact * gate * sg).astype(BF16)
            dgate_ref[sl, :] = dgate.T
            dup_ref[sl, :] = dup.T
            dh1n = dh1n + _dot(dgate, wg_ref[sl, :]) + _dot(dup, wu_ref[sl, :])
        _acc(dg2_ref, first, _colsum(dh1n * x2))
        dh1 = dy + _rms_bwd(dh1n, x2, r2, g2_ref[...], D_MODEL)
        dh1_ref[...] = dh1
        dmix = _dot_nt(dh1.astype(BF16), wo_ref[...])
        dma, dmr = dmix[:, :D_ATTN], dmix[:, D_ATTN:]
        _acc(dga_ref, first, _colsum(dma * xa))
        _acc(dgr_ref, first, _colsum(dmr * xr))
        doa_ref[...] = _rms_bwd(dma, xa, ra, ga_ref[...], D_ATTN)
        dor_ref[...] = _rms_bwd(dmr, xr, rr, gr_ref[...], D_RNN)

    def row(w):
        return pl.BlockSpec((tm, w), lambda i: (i, 0))

    def acc(w):
        return pl.BlockSpec((1, w), lambda i: (0, 0))

    def col(w):
        return pl.BlockSpec((w, tm), lambda i: (0, i))

    outs = [(D_ATTN, F32, row), (D_RNN, F32, row), (D_MODEL, F32, row), (D_MODEL, BF16, col), (D_MODEL, BF16, row),
            (D_FF, BF16, col), (D_FF, BF16, col), (D_FF, BF16, col), (D_MODEL, BF16, row)]
    accs = [LANES, D_ATTN, D_RNN, D_MODEL]
    return pl.pallas_call(
        body, name="post", grid=(n // tm,),
        in_specs=[row(D_ATTN), row(D_RNN), row(D_MODEL), row(D_MODEL),
                  _const_spec((1, D_ATTN)), _const_spec((1, D_RNN)), _const_spec((1, D_MODEL)),
                  _const_spec((D_MODEL, D_MODEL)), _const_spec((D_FF, D_MODEL)), _const_spec((D_FF, D_MODEL)),
                  _const_spec((D_FF, D_MODEL))],
        out_specs=[spec(w) for w, _, spec in outs] + [acc(w) for w in accs],
        out_shape=[jax.ShapeDtypeStruct((n, w) if spec is row else (w, n), dt) for w, dt, spec in outs]
        + [jax.ShapeDtypeStruct((1, w), F32) for w in accs],
        scratch_shapes=[pltpu.VMEM((tm, D_FF), F32), pltpu.VMEM((tm, D_FF), F32)],
        compiler_params=pltpu.CompilerParams(dimension_semantics=("arbitrary",), vmem_limit_bytes=VMEM_LIMIT),
    )(oa, orn, h0, tgt, ga, gr, g2, w_out, w_gate, w_up, w_down)


def _in_bwd(dp, h0, dh1, ln1_g, w_in_p, srcs=(), scatter=()):
    n = h0.shape[0]
    tm = _row_tile(n)
    nk = len(srcs)
    c_in, c_out, c_shape, c_sems = _exchange_specs(srcs, scatter)

    def body(dp_ref, h0_ref, dh1_ref, g_ref, w_ref, *rest):
        dh0_ref, dg_ref = rest[nk:nk + 2]
        finish = _ride(1, *_exchange_fns(rest[:nk], rest[nk + 2:2 * nk + 2], rest[2 * nk + 2:], scatter))
        dhn = _dot(dp_ref[...], w_ref[...])
        xhat, r = _rms(h0_ref[...], D_MODEL)
        _acc(dg_ref, pl.program_id(0) == 0, _colsum(dhn * xhat))
        dh0_ref[...] = dh1_ref[...] + _rms_bwd(dhn, xhat, r, g_ref[...], D_MODEL)
        finish()

    def row(w):
        return pl.BlockSpec((tm, w), lambda i: (i, 0))

    res = pl.pallas_call(
        body, name="in_bwd", grid=(n // tm,),
        in_specs=[row(P_COLS), row(D_MODEL), row(D_MODEL), _const_spec((1, D_MODEL)), _const_spec((P_COLS, D_MODEL))] + c_in,
        out_specs=[row(D_MODEL), pl.BlockSpec((1, D_MODEL), lambda i: (0, 0))] + c_out,
        out_shape=[jax.ShapeDtypeStruct((n, D_MODEL), F32), jax.ShapeDtypeStruct((1, D_MODEL), F32)] + c_shape,
        scratch_shapes=c_sems,
        compiler_params=pltpu.CompilerParams(dimension_semantics=("arbitrary",), vmem_limit_bytes=VMEM_LIMIT),
    )(dp, h0, dh1, ln1_g, w_in_p, *srcs)
    return res[:2], res[2:]


def _pick_tile(width, cap):
    best = LANES
    for mult in range(1, width // LANES + 1):
        cand = mult * LANES
        if width % cand == 0 and cand <= cap:
            best = cand
    return best


def _matmul_tn(name, a, b):
    n, ka = a.shape
    kb = b.shape[1]
    ta, tb = _pick_tile(ka, 1408), _pick_tile(kb, 1408)
    tk = n // 4

    def body(a_ref, b_ref, o_ref):
        _acc(o_ref, pl.program_id(2) == 0, _dot_tn(a_ref[...].astype(BF16), b_ref[...].astype(BF16)))

    return pl.pallas_call(
        body, name=name, grid=(ka // ta, kb // tb, n // tk),
        in_specs=[pl.BlockSpec((tk, ta), lambda i, j, k: (k, i)), pl.BlockSpec((tk, tb), lambda i, j, k: (k, j))],
        out_specs=pl.BlockSpec((ta, tb), lambda i, j, k: (i, j)),
        out_shape=jax.ShapeDtypeStruct((ka, kb), F32),
        compiler_params=pltpu.CompilerParams(dimension_semantics=("parallel", "parallel", "arbitrary"),
                                             vmem_limit_bytes=VMEM_LIMIT),
    )(a, b)


def _matmul_shards(name, at, b):
    ka, n = at.shape
    kb = b.shape[1]
    ta, tb = _pick_tile(ka, 1408), _pick_tile(kb, 1408)
    tk = n // 2
    width = ka // N_DEV
    per = ta // width

    def body(a_ref, b_ref, o_ref, acc_ref):
        _acc(acc_ref, pl.program_id(2) == 0, _dot(a_ref[...], b_ref[...].astype(BF16)))

        @pl.when(pl.program_id(2) == pl.num_programs(2) - 1)
        def _():
            for s in range(per):
                o_ref[s] = acc_ref[s * width:(s + 1) * width, :].astype(BF16)

    return pl.pallas_call(
        body, name=name, grid=(ka // ta, kb // tb, n // tk),
        in_specs=[pl.BlockSpec((ta, tk), lambda i, j, k: (i, k)), pl.BlockSpec((tk, tb), lambda i, j, k: (k, j))],
        out_specs=pl.BlockSpec((per, width, tb), lambda i, j, k: (i, 0, j)),
        out_shape=jax.ShapeDtypeStruct((N_DEV, width, kb), BF16),
        scratch_shapes=[pltpu.VMEM((ta, tb), F32)],
        compiler_params=pltpu.CompilerParams(dimension_semantics=("parallel", "parallel", "arbitrary"),
                                             vmem_limit_bytes=VMEM_LIMIT),
    )(at, b)


def _adamw_math(g8_ref, w_ref, m_ref, v_ref, g_ref, d_ref, nm_ref, nv_ref):
    g = g8_ref[0].astype(F32)
    for s in range(1, N_DEV):
        g = g + g8_ref[s].astype(F32)
    g_ref[...] = g
    nm = ADAM_B1 * m_ref[...] + (1.0 - ADAM_B1) * g
    nv = ADAM_B2 * v_ref[...] + (1.0 - ADAM_B2) * (g * g)
    nm_ref[...] = nm
    nv_ref[...] = nv
    m_hat = nm / (1.0 - ADAM_B1 ** ADAM_STEP)
    v_hat = nv / (1.0 - ADAM_B2 ** ADAM_STEP)
    d_ref[...] = -ADAM_LR * (m_hat / (jnp.sqrt(v_hat) + ADAM_EPS) + ADAM_WD * w_ref[...])


def _adamw_many(name, items):
    count = len(items)

    def body(*refs):
        ins, outs = refs[:4 * count], refs[4 * count:]
        for i in range(count):
            _adamw_math(*ins[4 * i:4 * i + 4], *outs[4 * i:4 * i + 4])

    flat = [a for item in items for a in item]
    res = pl.pallas_call(
        body, name=name,
        out_shape=[jax.ShapeDtypeStruct(item[1].shape, F32) for item in items for _ in range(4)],
        compiler_params=pltpu.CompilerParams(vmem_limit_bytes=VMEM_LIMIT),
    )(*flat)
    return [tuple(res[4 * i:4 * i + 4]) for i in range(count)]


def _adamw(name, g8, w, m, v):
    rows, cols = w.shape
    tr = rows
    for cand in (256, 176, 128, 64):
        if rows % cand == 0 and rows > cand:
            tr = cand
            break

    def body(*refs):
        _adamw_math(*refs)

    blk = pl.BlockSpec((tr, cols), lambda i: (i, 0))
    return pl.pallas_call(
        body, name=name, grid=(rows // tr,),
        in_specs=[pl.BlockSpec((N_DEV, tr, cols), lambda i: (0, i, 0)), blk, blk, blk],
        out_specs=[blk] * 4, out_shape=[jax.ShapeDtypeStruct((rows, cols), F32)] * 4,
        compiler_params=pltpu.CompilerParams(dimension_semantics=("parallel",), vmem_limit_bytes=VMEM_LIMIT),
    )(g8, w, m, v)


def _exchange_specs(srcs, scatter):
    nk = len(srcs)
    if not nk:
        return [], [], [], []
    any_spec = pl.BlockSpec(memory_space=pl.ANY)
    out_shape = [jax.ShapeDtypeStruct(s.shape if sc else (N_DEV,) + s.shape, s.dtype) for s, sc in zip(srcs, scatter)]
    sems = [pltpu.SemaphoreType.DMA((nk, N_DEV - 1)), pltpu.SemaphoreType.DMA((nk, N_DEV - 1)),
            pltpu.SemaphoreType.DMA((nk,))]
    return [any_spec] * nk, [any_spec] * nk, out_shape, sems


FLIPS = ((0, 0, 1), (1, 0, 0), (0, 1, 0), (1, 1, 0), (1, 0, 1), (0, 1, 1), (1, 1, 1))
N_CHIP_PEERS = 3


def _exchange_fns(src_refs, out_refs, sems, scatter):
    nk = len(src_refs)
    if not nk:
        return (lambda: None), (lambda: None), (lambda: None)
    send_sems, recv_sems, local_sems = sems
    first = 1 + N_CHIP_PEERS

    def plan():
        x, y, c = lax.axis_index("x"), lax.axis_index("y"), lax.axis_index("c")
        me = 4 * x + 2 * y + c
        peers = [(1 - x if fx else x, 1 - y if fy else y, 1 - c if fc else c) for fx, fy, fc in FLIPS]
        pids = [4 * px + 2 * py + pc for px, py, pc in peers]

        def remote(k, j, src, dst, to):
            return pltpu.make_async_remote_copy(src_ref=src, dst_ref=dst, send_sem=send_sems.at[k, j],
                                                recv_sem=recv_sems.at[k, j], device_id=to, device_id_type=MESH)

        def mine(k, dest):
            return src_refs[k].at[dest] if scatter[k] else src_refs[k]

        local = [pltpu.make_async_copy(mine(k, me), out_refs[k].at[me], local_sems.at[k]) for k in range(nk)]
        direct = [remote(k, j, mine(k, pids[j]), out_refs[k].at[me], peers[j])
                  for k in range(nk) for j in range(len(FLIPS) if scatter[k] else first)]
        relays = {(k, j): remote(k, j, out_refs[k].at[pids[j - N_CHIP_PEERS]], out_refs[k].at[pids[j - N_CHIP_PEERS]], peers[0])
                  for k in range(nk) if not scatter[k] for j in range(first, len(FLIPS))}
        arrivals = {(k, j): remote(k, j, out_refs[k].at[pids[j]], out_refs[k].at[pids[j]], peers[j])
                    for k in range(nk) for j in range(len(FLIPS))}
        return local, direct, relays, arrivals

    def start():
        local, direct, _, _ = plan()
        for cp in local + direct:
            cp.start()

    def relay():
        _, _, relays, arrivals = plan()
        for (k, j), cp in relays.items():
            arrivals[k, j - N_CHIP_PEERS].wait_recv()
            cp.start()

    def wait():
        local, direct, relays, arrivals = plan()
        for (k, j), cp in arrivals.items():
            if (k, j + N_CHIP_PEERS) not in relays:
                cp.wait_recv()
        for cp in direct + list(relays.values()):
            cp.wait_send()
        for cp in local:
            cp.wait()

    return start, relay, wait


def _grid_step(rank):
    step, total = 0, 1
    for axis in range(rank):
        step = step * pl.num_programs(axis) + pl.program_id(axis)
        total = total * pl.num_programs(axis)
    return step, total


def _ride(rank, start, relay, wait):
    step, total = _grid_step(rank)
    pl.when(step == 0)(start)
    pl.when(step == (3 * total) // 4)(relay)
    return lambda: pl.when(step == total - 1)(wait)


def _exchange(name, srcs, scatter):
    nk = len(srcs)
    c_in, c_out, c_shape, c_sems = _exchange_specs(srcs, scatter)

    def body(*refs):
        start, relay, wait = _exchange_fns(refs[:nk], refs[nk:2 * nk], refs[2 * nk:], scatter)
        start()
        relay()
        wait()

    return pl.pallas_call(body, name=name, in_specs=c_in, out_specs=c_out, out_shape=c_shape, scratch_shapes=c_sems)(*srcs)


def _cols_from_shards(g):
    return jnp.transpose(g, (1, 0, 2)).reshape(g.shape[1], -1)


def _cols_to_shards(w):
    return jnp.transpose(w.reshape(w.shape[0], N_DEV, -1), (1, 0, 2))


def _rope_tables(n):
    t = _t_pad()
    pos = (jnp.arange(t, dtype=F32) - PAD_ROWS)
    half = QK_ROPE // 2
    freqs = 1.0 / (ROPE_THETA ** (jnp.arange(half, dtype=F32) / half))
    ang = pos[:, None] * freqs[None, :]
    cos, sin = jnp.cos(ang), jnp.sin(ang)
    z = lambda w: jnp.zeros((t, w), F32)
    c = jnp.concatenate([jnp.ones((t, QK_NOPE), F32), cos, cos, z(HEAD_PAD - QK_HEAD)], axis=1)
    s1 = jnp.concatenate([z(QK_NOPE + half), sin, z(HEAD_PAD - QK_HEAD)], axis=1)
    s2 = jnp.concatenate([z(QK_NOPE), -sin, z(HEAD_PAD - QK_NOPE - half)], axis=1)
    reps = n // t
    return tuple(jnp.tile(a, (reps, 1)) for a in (c, s1, s2))


def _block_diag_gates(lru_wa, lru_wi):
    eye = jnp.eye(2, dtype=lru_wa.dtype)

    def bd(w):
        w = w.reshape(2, D_RNN // LANES, 2, RNN_BW, RNN_BW)
        full = w[:, :, :, :, None, :] * eye[None, None, :, None, :, None]
        return full.reshape(2, D_RNN // LANES, LANES, LANES)

    a, i = bd(lru_wa), bd(lru_wi)
    return jnp.concatenate([a[0], i[0], a[1], i[1]], axis=-1)


def _unblock_gates(dw):
    nb = D_RNN // LANES
    parts = dw.reshape(nb, 2, RNN_BW, 4, 2, RNN_BW)
    diag = jnp.stack([parts[:, k, :, :, k, :] for k in range(2)], axis=1)
    diag = jnp.transpose(diag, (3, 0, 1, 2, 4)).reshape(4, 2 * nb, RNN_BW, RNN_BW)
    return jnp.stack([diag[0], diag[2]]), jnp.stack([diag[1], diag[3]])


WEIGHTS = ("meta_tokens", "ln1_g", "w_in", "q_a_norm_g", "w_uq", "kv_a_norm_g", "w_ukv", "q_norm_g", "k_norm_g",
           "conv_w", "conv_b", "lru_wa", "lru_ba", "lru_wi", "lru_bi", "lru_lambda", "attn_out_g", "rnn_out_g",
           "w_out", "ln2_g", "w_gate", "w_up", "w_down")
BIG = ("w_in", "w_uq", "w_ukv", "w_out", "w_gate", "w_up", "w_down")
TRANSPOSED = ("w_in", "w_uq", "w_gate", "w_up")
ROW_SHARDED = ("w_out", "w_down") + TRANSPOSED
REPLICATED = ("ln1_g", "q_a_norm_g", "kv_a_norm_g", "q_norm_g", "k_norm_g", "conv_b", "lru_wa", "lru_wi",
              "attn_out_g", "rnn_out_g", "ln2_g")
G_FIRST = ("w_in", "meta_tokens")
G_MID = ("w_uq", "w_ukv", "conv_w", "lru_ba", "lru_bi", "lru_lambda")
LATE = ("w_out", "w_gate", "w_up", "w_down")
G_LAST = ("meta_tokens", "ln1_g")


def _local_step(x, tgt, ex):
    nb = x.shape[0]
    t = _t_pad()
    n = nb * t
    local = ex.local
    first = ex.gathered(G_FIRST, ex.run("gather_first", *ex.gather_srcs(G_FIRST)))
    meta, w_in = first["meta_tokens"], first["w_in"]
    lead = jnp.zeros((nb, PAD_ROWS, D_MODEL), F32)
    h0 = jnp.concatenate([lead, jnp.broadcast_to(meta[None], (nb, N_META, D_MODEL)), x], axis=1).reshape(n, D_MODEL)
    tgt_p = jnp.concatenate([jnp.zeros((nb, PAD_ROWS + N_META, D_MODEL), F32), tgt], axis=1).reshape(n, D_MODEL)

    zr = lambda r: jnp.zeros((r, D_MODEL), w_in.dtype)
    w_in_p = jnp.concatenate([w_in[:OFF_CKV], w_in[OFF_KR:], zr(QK_NOPE), w_in[OFF_CKV:OFF_KR], zr(HEAD_PAD - QK_HEAD)],
                             axis=0)
    pad_g = lambda g: jnp.pad(g, ((0, 0), (0, HEAD_PAD - QK_HEAD)))
    qg, kg = pad_g(local["q_norm_g"]), pad_g(local["k_norm_g"])
    rc, rs1, rs2 = _rope_tables(n)
    wblk = _block_diag_gates(local["lru_wa"].reshape(2, -1, RNN_BW, RNN_BW),
                             local["lru_wi"].reshape(2, -1, RNN_BW, RNN_BW)).astype(BF16)
    nblk = D_RNN // LANES

    (hn, cq, ckv, xr, xg, kr), got = _in_proj(h0, local["ln1_g"], w_in_p, *ex.gather_srcs(G_MID))
    w = ex.gathered(G_MID, got)
    w_uq_p = jnp.pad(w["w_uq"].reshape(N_HEADS, QK_HEAD, Q_LORA), ((0, 0), (0, HEAD_PAD - QK_HEAD), (0, 0))
                     ).reshape(QP_COLS, Q_LORA)
    ukv = w["w_ukv"].reshape(KV_LORA, N_HEADS, QK_NOPE + V_HEAD)
    w_uk_p = jnp.pad(ukv[:, :, :QK_NOPE], ((0, 0), (0, 0), (0, HEAD_PAD - QK_NOPE))).reshape(KV_LORA, QP_COLS)
    w_v = ukv[:, :, QK_NOPE:].reshape(KV_LORA, D_ATTN)
    gbias = jnp.stack([w["lru_ba"][0], w["lru_bi"][0], w["lru_ba"][1], w["lru_bi"][1]], axis=0)
    gbias = jnp.transpose(gbias.reshape(4, nblk, LANES), (1, 0, 2)).reshape(nblk, 1, 4 * LANES)

    q, k, v = _qkv_fwd(cq, ckv, kr, local["q_a_norm_g"], local["kv_a_norm_g"], w_uq_p, w_uk_p, w_v, qg, kg, rc, rs1, rs2)
    oa, got = _attn_fwd(q, k, v, *ex.gather_srcs(LATE))
    late = ex.gathered(LATE, got)
    orn = _rnn_fwd(xr, xg, w["conv_w"], local["conv_b"], wblk, gbias, w["lru_lambda"])
    (doa, dor, dh1, mix_t, h1n, act_t, dgate_t, dup_t, dyb, loss, dga, dgr, dg2) = _post(
        oa, orn, h0, tgt_p, local["attn_out_g"], local["rnn_out_g"], local["ln2_g"], late["w_out"], late["w_gate"],
        late["w_up"], late["w_down"])
    wire = {"w_out": _matmul_shards("dw_out", mix_t, dh1), "w_gate": _matmul_shards("dw_gate", dgate_t, h1n),
            "w_up": _matmul_shards("dw_up", dup_t, h1n), "w_down": _matmul_shards("dw_down", act_t, dyb)}
    dxr, dxg, dcw, dcb, dwblk, dgb, dlam = _rnn_bwd(xr, xg, dor, w["conv_w"], local["conv_b"], wblk, gbias, w["lru_lambda"])
    dwa, dwi = _unblock_gates(dwblk)
    dgb = jnp.transpose(dgb.reshape(nblk, 4, LANES), (1, 0, 2)).reshape(4, D_RNN)
    wire.update(ex.to_wire({
        "conv_w": dcw, "conv_b": dcb, "lru_wa": dwa.reshape(-1, RNN_BW), "lru_ba": jnp.stack([dgb[0], dgb[2]]),
        "lru_wi": dwi.reshape(-1, RNN_BW), "lru_bi": jnp.stack([dgb[1], dgb[3]]), "lru_lambda": dlam,
        "attn_out_g": dga, "rnn_out_g": dgr, "ln2_g": dg2}))
    names = tuple(wire)
    (dq_r, dk_r, dv), got = _attn_bwd(q, k, v, doa, oa, *ex.scatter_srcs(names, wire))
    summed = ex.scattered(names, wire, got)
    (dp, qa, kva, dqp, dkv, dqg, dkg, dgqa, dgkva) = _qkv_bwd(
        cq, ckv, kr, dq_r, dk_r, dv, dxr, dxg, local["q_a_norm_g"], local["kv_a_norm_g"], w_uq_p, w_uk_p, w_v, qg, kg,
        rc, rs1, rs2)
    dw_in_p = _matmul_tn("dw_in", dp, hn)
    dw_uq_p = _matmul_tn("dw_uq", dqp, qa)
    dw_kv = _matmul_tn("dw_ukv", kva, dkv)
    kr0 = OFF_CKV + 2 * D_RNN + QK_NOPE
    dw_in = jnp.concatenate([dw_in_p[:OFF_CKV], dw_in_p[kr0:kr0 + QK_ROPE], dw_in_p[OFF_CKV:OFF_CKV + 2 * D_RNN]], axis=0)
    dw_uq = dw_uq_p.reshape(N_HEADS, HEAD_PAD, Q_LORA)[:, :QK_HEAD].reshape(N_HEADS * QK_HEAD, Q_LORA)
    dw_ukv = jnp.concatenate([dw_kv[:, :QP_COLS].reshape(KV_LORA, N_HEADS, HEAD_PAD)[:, :, :QK_NOPE],
                              dw_kv[:, QP_COLS:].reshape(KV_LORA, N_HEADS, V_HEAD)], axis=2).reshape(KV_LORA, -1)
    wire = ex.to_wire({"w_in": dw_in, "q_a_norm_g": dgqa, "w_uq": dw_uq, "kv_a_norm_g": dgkva, "w_ukv": dw_ukv,
                       "q_norm_g": dqg[:, :QK_HEAD], "k_norm_g": dkg[:, :QK_HEAD]})
    names = tuple(wire)
    (dh0, dg1), got = _in_bwd(dp, h0, dh1, local["ln1_g"], w_in_p, *ex.scatter_srcs(names, wire))
    summed.update(ex.scattered(names, wire, got))

    dh0 = dh0.reshape(nb, t, D_MODEL)
    wire = ex.to_wire({"meta_tokens": jnp.sum(dh0[:, PAD_ROWS:PAD_ROWS + N_META], axis=0), "ln1_g": dg1})
    got = ex.run("reduce_last", *ex.scatter_srcs(G_LAST, wire))
    summed.update(ex.scattered(G_LAST, wire, got))
    return loss[0, 0], dh0[:, PAD_ROWS + N_META:], summed


class _MeshExchange:
    def __init__(self, shards):
        self.local = shards

    @staticmethod
    def run(name, srcs, scatter):
        return _exchange(name, srcs, scatter)

    def gather_srcs(self, names):
        return [self.local[k].astype(BF16) if k in BIG else self.local[k] for k in names], [False] * len(names)

    @staticmethod
    def gathered(names, outs):
        return {k: g.reshape(-1, g.shape[-1]) if k in ROW_SHARDED else _cols_from_shards(g) for k, g in zip(names, outs)}

    @staticmethod
    def to_wire(grads):
        wire = {}
        for k, g in grads.items():
            if k in REPLICATED:
                wire[k] = g
            elif k in ROW_SHARDED:
                wire[k] = g.reshape(N_DEV, -1, g.shape[-1]).astype(BF16)
            else:
                wire[k] = _cols_to_shards(g).astype(BF16) if k in BIG else _cols_to_shards(g)
        return wire

    @staticmethod
    def scatter_srcs(names, wire):
        return [wire[k] for k in names], [k not in REPLICATED for k in names]

    @staticmethod
    def scattered(names, wire, outs):
        return dict(zip(names, outs))


def kernel(x, meta_tokens, ln1_g, w_in, q_a_norm_g, w_uq, kv_a_norm_g, w_ukv, q_norm_g, k_norm_g, conv_w, conv_b, lru_wa, lru_ba, lru_wi, lru_bi, lru_lambda, attn_out_g, rnn_out_g, w_out, ln2_g, w_gate, w_up, w_down, loss_target, m_meta_tokens, m_ln1_g, m_w_in, m_q_a_norm_g, m_w_uq, m_kv_a_norm_g, m_w_ukv, m_q_norm_g, m_k_norm_g, m_conv_w, m_conv_b, m_lru_wa, m_lru_ba, m_lru_wi, m_lru_bi, m_lru_lambda, m_attn_out_g, m_rnn_out_g, m_w_out, m_ln2_g, m_w_gate, m_w_up, m_w_down, v_meta_tokens, v_ln1_g, v_w_in, v_q_a_norm_g, v_w_uq, v_kv_a_norm_g, v_w_ukv, v_q_norm_g, v_k_norm_g, v_conv_w, v_conv_b, v_lru_wa, v_lru_ba, v_lru_wi, v_lru_bi, v_lru_lambda, v_attn_out_g, v_rnn_out_g, v_w_out, v_ln2_g, v_w_gate, v_w_up, v_w_down):
    given = (meta_tokens, ln1_g, w_in, q_a_norm_g, w_uq, kv_a_norm_g, w_ukv, q_norm_g, k_norm_g, conv_w, conv_b,
             lru_wa, lru_ba, lru_wi, lru_bi, lru_lambda, attn_out_g, rnn_out_g, w_out, ln2_g, w_gate, w_up, w_down)
    moments_m = (m_meta_tokens, m_ln1_g, m_w_in, m_q_a_norm_g, m_w_uq, m_kv_a_norm_g, m_w_ukv, m_q_norm_g, m_k_norm_g,
                 m_conv_w, m_conv_b, m_lru_wa, m_lru_ba, m_lru_wi, m_lru_bi, m_lru_lambda, m_attn_out_g, m_rnn_out_g,
                 m_w_out, m_ln2_g, m_w_gate, m_w_up, m_w_down)
    moments_v = (v_meta_tokens, v_ln1_g, v_w_in, v_q_a_norm_g, v_w_uq, v_kv_a_norm_g, v_w_ukv, v_q_norm_g, v_k_norm_g,
                 v_conv_w, v_conv_b, v_lru_wa, v_lru_ba, v_lru_wi, v_lru_bi, v_lru_lambda, v_attn_out_g, v_rnn_out_g,
                 v_w_out, v_ln2_g, v_w_gate, v_w_up, v_w_down)
    shapes = {k: a.shape for k, a in zip(WEIGHTS, given)}

    def two_d(k, a):
        a = a.reshape(-1, a.shape[-1])
        return a.T if k in TRANSPOSED else a

    w = {k: two_d(k, a) for k, a in zip(WEIGHTS, given)}
    m = {k: two_d(k, a) for k, a in zip(WEIGHTS, moments_m)}
    v = {k: two_d(k, a) for k, a in zip(WEIGHTS, moments_v)}

    loss_part, grad_x, parts = _local_step(x, loss_target, _MeshExchange(w))

    new = {k: _adamw("adamw_" + k, parts[k], w[k], m[k], v[k]) for k in BIG}
    small = [k for k in WEIGHTS if k not in BIG]
    new.update(zip(small, _adamw_many("adamw_small", [(parts[k], w[k], m[k], v[k]) for k in small])))

    loss = lax.psum(loss_part, ("x", "y", "c"))
    outs = [loss, grad_x]
    for idx in range(4):
        outs += [(new[k][idx].T if k in TRANSPOSED else new[k][idx]).reshape(shapes[k]) for k in WEIGHTS]
    return tuple(outs)
```

```python
import functools
import math

import numpy as np
import jax
import jax.numpy as jnp
from jax import lax
from jax.experimental import pallas as pl
from jax.experimental.pallas import tpu as pltpu

F32 = jnp.float32
BF16 = jnp.bfloat16

D_MODEL = 1024
N_META = 16
SEQ = 2048
N_HEADS = 8
QK_NOPE = 64
QK_ROPE = 32
QK_HEAD = QK_NOPE + QK_ROPE
V_HEAD = 64
D_ATTN = N_HEADS * V_HEAD
Q_LORA = 384
KV_LORA = 256
D_RNN = 512
RNN_BW = 64
D_FF = 2816
EPS = 1e-6
LRU_C = 8.0
ROPE_THETA = 10000.0
OFF_CKV = Q_LORA + KV_LORA
OFF_KR = OFF_CKV + QK_ROPE
IN_COLS = OFF_KR + 2 * D_RNN

ADAM_LR = 0.001
ADAM_B1 = 0.9
ADAM_B2 = 0.999
ADAM_EPS = 1e-08
ADAM_WD = 0.01
ADAM_STEP = 10

N_DEV = 8
LANES = 128
HEAD_PAD = LANES
PAD_ROWS = LANES - N_META
QP_COLS = N_HEADS * HEAD_PAD
P_COLS = OFF_CKV + 2 * D_RNN + LANES
FF_CHUNK = D_FF // 2
VMEM_LIMIT = 56 * 1024 * 1024
MESH = pl.DeviceIdType.MESH


def _t_pad():
    return PAD_ROWS + N_META + SEQ


def _row_tile(n):
    return 256 if n % 256 == 0 else 128


def _const_spec(shape):
    nd = len(shape)
    return pl.BlockSpec(shape, lambda *_: (0,) * nd, pipeline_mode=pl.Buffered(1))


def _rms(x, d):
    r = lax.rsqrt(jnp.sum(x * x, axis=-1, keepdims=True) * (1.0 / d) + EPS)
    return x * r, r


def _rms_bwd(dy, xhat, r, g, d):
    dxh = dy * g
    return r * (dxh - xhat * (jnp.sum(dxh * xhat, axis=-1, keepdims=True) * (1.0 / d)))


def _colsum(x):
    return jnp.sum(x, axis=0, keepdims=True)


def _dot(a, b):
    return jnp.dot(a, b, preferred_element_type=F32)


def _dot_nt(a, b):
    return lax.dot_general(a, b, (((1,), (1,)), ((), ())), preferred_element_type=F32)


def _dot_tn(a, b):
    return lax.dot_general(a, b, (((0,), (0,)), ((), ())), preferred_element_type=F32)


def _rope(x, c, s1, s2):
    return x * c + pltpu.roll(x, 16, 1) * s1 + pltpu.roll(x, HEAD_PAD - 16, 1) * s2


def _rope_bwd(dy, c, s1, s2):
    return dy * c + pltpu.roll(dy * s1, HEAD_PAD - 16, 1) + pltpu.roll(dy * s2, 16, 1)


def _acc(ref, first, val):
    @pl.when(first)
    def _():
        ref[...] = val

    @pl.when(jnp.logical_not(first))
    def _():
        ref[...] += val


def _in_proj(h0, ln1_g, w_in_p, srcs=(), scatter=()):
    n = h0.shape[0]
    tm = _row_tile(n)
    nk = len(srcs)
    c_in, c_out, c_shape, c_sems = _exchange_specs(srcs, scatter)

    def body(h_ref, g_ref, w_ref, *rest):
        hn_ref, cq_ref, ckv_ref, xr_ref, xg_ref, kr_ref = rest[nk:nk + 6]
        finish = _ride(1, *_exchange_fns(rest[:nk], rest[nk + 6:2 * nk + 6], rest[2 * nk + 6:], scatter))
        xhat, _ = _rms(h_ref[...], D_MODEL)
        hn = (xhat * g_ref[...]).astype(BF16)
        hn_ref[...] = hn
        p = _dot_nt(hn, w_ref[...])
        cq_ref[...] = p[:, :Q_LORA]
        ckv_ref[...] = p[:, Q_LORA:OFF_CKV]
        xr_ref[...] = p[:, OFF_CKV:OFF_CKV + D_RNN]
        xg_ref[...] = p[:, OFF_CKV + D_RNN:OFF_CKV + 2 * D_RNN]
        kr_ref[...] = p[:, OFF_CKV + 2 * D_RNN:]
        finish()

    def row(w):
        return pl.BlockSpec((tm, w), lambda i: (i, 0))

    widths = (D_MODEL, Q_LORA, KV_LORA, D_RNN, D_RNN, LANES)
    res = pl.pallas_call(
        body, name="in_proj", grid=(n // tm,),
        in_specs=[row(D_MODEL), _const_spec((1, D_MODEL)), _const_spec((P_COLS, D_MODEL))] + c_in,
        out_specs=[row(w) for w in widths] + c_out,
        out_shape=[jax.ShapeDtypeStruct((n, w), BF16 if k == 0 else F32) for k, w in enumerate(widths)] + c_shape,
        scratch_shapes=c_sems,
        compiler_params=pltpu.CompilerParams(dimension_semantics=("arbitrary",), vmem_limit_bytes=VMEM_LIMIT),
    )(h0, ln1_g, w_in_p, *srcs)
    return res[:6], res[6:]


def _qkv_fwd(cq, ckv, kr, gqa, gkva, w_uq_p, w_uk_p, w_v, qg, kg, rc, rs1, rs2):
    n = cq.shape[0]
    tm = _row_tile(n)

    def body(cq_ref, ckv_ref, kr_ref, gqa_ref, gkva_ref, wuq_ref, wuk_ref, wv_ref, qg_ref, kg_ref,
             c_ref, s1_ref, s2_ref, q_ref, k_ref, v_ref):
        xq, _ = _rms(cq_ref[...], Q_LORA)
        qa = (xq * gqa_ref[...]).astype(BF16)
        q = _dot_nt(qa, wuq_ref[...])
        xkv, _ = _rms(ckv_ref[...], KV_LORA)
        kva = (xkv * gkva_ref[...]).astype(BF16)
        kn = _dot(kva, wuk_ref[...])
        v_ref[...] = _dot(kva, wv_ref[...]).astype(BF16)
        krp = kr_ref[...]
        c, s1, s2 = c_ref[...], s1_ref[...], s2_ref[...]
        for h in range(N_HEADS):
            sl = slice(h * HEAD_PAD, (h + 1) * HEAD_PAD)
            qh, _ = _rms(q[:, sl], QK_HEAD)
            q_ref[:, sl] = _rope(qh * qg_ref[...], c, s1, s2).astype(BF16)
            kh, _ = _rms(kn[:, sl] + krp, QK_HEAD)
            k_ref[:, sl] = _rope(kh * kg_ref[...], c, s1, s2).astype(BF16)

    def row(w):
        return pl.BlockSpec((tm, w), lambda i: (i, 0))

    return pl.pallas_call(
        body, name="qkv_fwd", grid=(n // tm,),
        in_specs=[row(Q_LORA), row(KV_LORA), row(LANES), _const_spec((1, Q_LORA)), _const_spec((1, KV_LORA)),
                  _const_spec((QP_COLS, Q_LORA)), _const_spec((KV_LORA, QP_COLS)), _const_spec((KV_LORA, D_ATTN)),
                  _const_spec((1, LANES)), _const_spec((1, LANES)), row(LANES), row(LANES), row(LANES)],
        out_specs=[row(QP_COLS), row(QP_COLS), row(D_ATTN)],
        out_shape=[jax.ShapeDtypeStruct((n, QP_COLS), BF16), jax.ShapeDtypeStruct((n, QP_COLS), BF16),
                   jax.ShapeDtypeStruct((n, D_ATTN), BF16)],
        compiler_params=pltpu.CompilerParams(dimension_semantics=("parallel",), vmem_limit_bytes=VMEM_LIMIT),
    )(cq, ckv, kr, gqa, gkva, w_uq_p, w_uk_p, w_v, qg, kg, rc, rs1, rs2)


def _qkv_bwd(cq, ckv, kr, dq_r, dk_r, dv, dxr, dxg, gqa, gkva, w_uq_p, w_uk_p, w_v, qg, kg, rc, rs1, rs2):
    n = cq.shape[0]
    tm = _row_tile(n)

    def body(cq_ref, ckv_ref, kr_ref, dq_ref, dk_ref, dv_ref, dxr_ref, dxg_ref, gqa_ref, gkva_ref, wuq_ref, wuk_ref,
             wv_ref, qg_ref, kg_ref, c_ref, s1_ref, s2_ref,
             dp_ref, qa_ref, kva_ref, dqp_ref, dkv_ref, dqg_ref, dkg_ref, dgqa_ref, dgkva_ref):
        first = pl.program_id(0) == 0
        dp_ref[:, OFF_CKV:OFF_CKV + D_RNN] = dxr_ref[...].astype(BF16)
        dp_ref[:, OFF_CKV + D_RNN:OFF_CKV + 2 * D_RNN] = dxg_ref[...].astype(BF16)
        xq, rq = _rms(cq_ref[...], Q_LORA)
        qa = (xq * gqa_ref[...]).astype(BF16)
        qa_ref[...] = qa
        q = _dot_nt(qa, wuq_ref[...])
        xkv, rkv = _rms(ckv_ref[...], KV_LORA)
        kva = (xkv * gkva_ref[...]).astype(BF16)
        kva_ref[...] = kva
        kn = _dot(kva, wuk_ref[...])
        krp = kr_ref[...]
        c, s1, s2 = c_ref[...], s1_ref[...], s2_ref[...]
        lane = lax.broadcasted_iota(jnp.int32, (tm, HEAD_PAD), 1)
        rope_lanes = jnp.logical_and(lane >= QK_NOPE, lane < QK_HEAD)
        dqg = jnp.zeros((1, HEAD_PAD), F32)
        dkg = jnp.zeros((1, HEAD_PAD), F32)
        dkr = jnp.zeros((tm, HEAD_PAD), F32)
        for h in range(N_HEADS):
            sl = slice(h * HEAD_PAD, (h + 1) * HEAD_PAD)
            qh, rqh = _rms(q[:, sl], QK_HEAD)
            dy = _rope_bwd(dq_ref[:, sl], c, s1, s2)
            dqg = dqg + _colsum(dy * qh)
            dqp_ref[:, sl] = _rms_bwd(dy, qh, rqh, qg_ref[...], QK_HEAD).astype(BF16)
            kh, rkh = _rms(kn[:, sl] + krp, QK_HEAD)
            dyk = _rope_bwd(dk_ref[:, sl], c, s1, s2)
            dkg = dkg + _colsum(dyk * kh)
            dkh = _rms_bwd(dyk, kh, rkh, kg_ref[...], QK_HEAD)
            dkv_ref[:, sl] = dkh.astype(BF16)
            dkr = dkr + jnp.where(rope_lanes, dkh, 0.0)
        dkv_ref[:, QP_COLS:] = dv_ref[...].astype(BF16)
        dp_ref[:, OFF_CKV + 2 * D_RNN:] = dkr.astype(BF16)
        dqa = _dot(dqp_ref[...], wuq_ref[...])
        dp_ref[:, :Q_LORA] = _rms_bwd(dqa, xq, rq, gqa_ref[...], Q_LORA).astype(BF16)
        dkva = _dot_nt(dkv_ref[:, :QP_COLS], wuk_ref[...]) + _dot_nt(dkv_ref[:, QP_COLS:], wv_ref[...])
        dp_ref[:, Q_LORA:OFF_CKV] = _rms_bwd(dkva, xkv, rkv, gkva_ref[...], KV_LORA).astype(BF16)
        _acc(dqg_ref, first, dqg)
        _acc(dkg_ref, first, dkg)
        _acc(dgqa_ref, first, _colsum(dqa * xq))
        _acc(dgkva_ref, first, _colsum(dkva * xkv))

    def row(w):
        return pl.BlockSpec((tm, w), lambda i: (i, 0))

    def acc(w):
        return pl.BlockSpec((1, w), lambda i: (0, 0))

    return pl.pallas_call(
        body, name="qkv_bwd", grid=(n // tm,),
        in_specs=[row(Q_LORA), row(KV_LORA), row(LANES), row(QP_COLS), row(QP_COLS), row(D_ATTN), row(D_RNN), row(D_RNN),
                  _const_spec((1, Q_LORA)), _const_spec((1, KV_LORA)),
                  _const_spec((QP_COLS, Q_LORA)), _const_spec((KV_LORA, QP_COLS)), _const_spec((KV_LORA, D_ATTN)),
                  _const_spec((1, LANES)), _const_spec((1, LANES)), row(LANES), row(LANES), row(LANES)],
        out_specs=[row(P_COLS), row(Q_LORA), row(KV_LORA), row(QP_COLS),
                   row(QP_COLS + D_ATTN), acc(LANES), acc(LANES), acc(Q_LORA), acc(KV_LORA)],
        out_shape=[jax.ShapeDtypeStruct((n, P_COLS), BF16), jax.ShapeDtypeStruct((n, Q_LORA), BF16),
                   jax.ShapeDtypeStruct((n, KV_LORA), BF16), jax.ShapeDtypeStruct((n, QP_COLS), BF16),
                   jax.ShapeDtypeStruct((n, QP_COLS + D_ATTN), BF16),
                   jax.ShapeDtypeStruct((1, LANES), F32), jax.ShapeDtypeStruct((1, LANES), F32),
                   jax.ShapeDtypeStruct((1, Q_LORA), F32), jax.ShapeDtypeStruct((1, KV_LORA), F32)],
        compiler_params=pltpu.CompilerParams(dimension_semantics=("arbitrary",), vmem_limit_bytes=VMEM_LIMIT),
    )(cq, ckv, kr, dq_r, dk_r, dv, dxr, dxg, gqa, gkva, w_uq_p, w_uk_p, w_v, qg, kg, rc, rs1, rs2)


KEY_CHUNK = 4 * LANES


def _key_chunks(t):
    count = max(t // KEY_CHUNK, 1)
    first = t - KEY_CHUNK * (count - 1)
    return [(0, first)] + [(first + KEY_CHUNK * c, KEY_CHUNK) for c in range(count - 1)]


def _softmax_parts(qh, k_ref, sl, tq, t):
    scores = []
    for start, size in _key_chunks(t):
        s = _dot_nt(qh, k_ref[start:start + size, sl]) * (QK_HEAD ** -0.5)
        if start < PAD_ROWS:
            key = lax.broadcasted_iota(jnp.int32, (tq, size), 1) + start
            s = jnp.where(key >= PAD_ROWS, s, -jnp.inf)
        scores.append(s)
    top = functools.reduce(jnp.maximum, [jnp.max(s, axis=-1, keepdims=True) for s in scores])
    es = [jnp.exp(s - top) for s in scores]
    return es, functools.reduce(jnp.add, [jnp.sum(e, axis=-1, keepdims=True) for e in es])


def _attn_specs(t, tq):
    nq = t // tq
    qspec = pl.BlockSpec((tq, 2 * HEAD_PAD), lambda b, hp, i: (b * nq + i, hp))
    kspec = pl.BlockSpec((t, 2 * HEAD_PAD), lambda b, hp, i: (b, hp))
    vspec = pl.BlockSpec((t, 2 * V_HEAD), lambda b, hp, i: (b, hp))
    ospec = pl.BlockSpec((tq, 2 * V_HEAD), lambda b, hp, i: (b * nq + i, hp))
    return nq, qspec, kspec, vspec, ospec


def _attn_fwd(q, k, v, srcs=(), scatter=()):
    n = q.shape[0]
    t = _t_pad()
    tq = t // 2
    nq, qspec, kspec, vspec, ospec = _attn_specs(t, tq)
    nk = len(srcs)
    c_in, c_out, c_shape, c_sems = _exchange_specs(srcs, scatter)

    def body(q_ref, k_ref, v_ref, *rest):
        o_ref = rest[nk]
        finish = _ride(3, *_exchange_fns(rest[:nk], rest[nk + 1:2 * nk + 1], rest[2 * nk + 1:], scatter))
        lane = lax.broadcasted_iota(jnp.int32, (tq, 2 * V_HEAD), 1)
        outs = []
        for j in range(2):
            sl = slice(j * HEAD_PAD, (j + 1) * HEAD_PAD)
            es, l = _softmax_parts(q_ref[:, sl], k_ref, sl, tq, t)
            pv = [_dot(e.astype(BF16), v_ref[start:start + size, :]) for e, (start, size) in zip(es, _key_chunks(t))]
            outs.append(functools.reduce(jnp.add, pv) / l)
        o_ref[...] = jnp.where(lane < V_HEAD, outs[0], outs[1])
        finish()

    res = pl.pallas_call(
        body, name="attn_fwd", grid=(n // t, N_HEADS // 2, nq),
        in_specs=[qspec, kspec, vspec] + c_in, out_specs=[ospec] + c_out,
        out_shape=[jax.ShapeDtypeStruct((n, D_ATTN), F32)] + c_shape, scratch_shapes=c_sems,
        compiler_params=pltpu.CompilerParams(dimension_semantics=("arbitrary", "arbitrary", "arbitrary"),
                                             vmem_limit_bytes=VMEM_LIMIT),
    )(q, k, v, *srcs)
    return res[0], res[1:]


def _attn_bwd(q, k, v, do, o, srcs=(), scatter=()):
    n = q.shape[0]
    t = _t_pad()
    tq = t // 2
    nq, qspec, kspec, vspec, ospec = _attn_specs(t, tq)
    nk = len(srcs)
    c_in, c_out, c_shape, c_sems = _exchange_specs(srcs, scatter)

    def body(q_ref, k_ref, v_ref, do_ref, o_ref, *rest):
        dq_ref, dk_ref, dv_ref = rest[nk:nk + 3]
        finish = _ride(3, *_exchange_fns(rest[:nk], rest[nk + 3:2 * nk + 3], rest[2 * nk + 3:], scatter))

        @pl.when(pl.program_id(2) == 0)
        def _():
            dk_ref[...] = jnp.zeros_like(dk_ref)
            dv_ref[...] = jnp.zeros_like(dv_ref)

        lane = lax.broadcasted_iota(jnp.int32, (tq, 2 * V_HEAD), 1)
        do = do_ref[...]
        do_o = do * o_ref[...]
        chunks = _key_chunks(t)
        dvs = [None] * len(chunks)
        for j in range(2):
            sl = slice(j * HEAD_PAD, (j + 1) * HEAD_PAD)
            qh = q_ref[:, sl]
            es, l = _softmax_parts(qh, k_ref, sl, tq, t)
            inv_l = 1.0 / l
            in_head = (lane < V_HEAD) if j == 0 else (lane >= V_HEAD)
            doh = jnp.where(in_head, do, 0.0).astype(BF16)
            delta = jnp.sum(jnp.where(in_head, do_o, 0.0), axis=-1, keepdims=True)
            dq = jnp.zeros((tq, HEAD_PAD), F32)
            for c, (start, size) in enumerate(chunks):
                rows = slice(start, start + size)
                p = es[c] * inv_l
                dp = _dot_nt(doh, v_ref[rows, :])
                ds = (p * (dp - delta) * (QK_HEAD ** -0.5)).astype(BF16)
                dq = dq + _dot(ds, k_ref[rows, sl])
                dk_ref[rows, sl] += _dot_tn(ds, qh)
                dvc = _dot_tn(p.astype(BF16), doh)
                dvs[c] = dvc if dvs[c] is None else dvs[c] + dvc
            dq_ref[:, sl] = dq
        for (start, size), dvc in zip(chunks, dvs):
            dv_ref[start:start + size, :] += dvc
        finish()

    res = pl.pallas_call(
        body, name="attn_bwd", grid=(n // t, N_HEADS // 2, nq),
        in_specs=[qspec, kspec, vspec, ospec, ospec] + c_in, out_specs=[qspec, kspec, vspec] + c_out,
        out_shape=[jax.ShapeDtypeStruct((n, QP_COLS), F32), jax.ShapeDtypeStruct((n, QP_COLS), F32),
                   jax.ShapeDtypeStruct((n, D_ATTN), F32)] + c_shape, scratch_shapes=c_sems,
        compiler_params=pltpu.CompilerParams(dimension_semantics=("arbitrary", "arbitrary", "arbitrary"),
                                             vmem_limit_bytes=VMEM_LIMIT),
    )(q, k, v, do, o, *srcs)
    return res[:3], res[3:]


SCAN_STEPS = 8


def _scan(chains, t):
    seg = t // 8
    rows = lax.broadcasted_iota(jnp.int32, (8, LANES), 0)

    def step(i, carry):
        carry = list(carry)
        for u in range(SCAN_STEPS):
            j = i * SCAN_STEPS + u
            for n, (a_ref, b_ref, h_ref, p_ref, reverse) in enumerate(chains):
                h, p = carry[n]
                idx = pl.ds(seg - 1 - j if reverse else j, 8, stride=seg)
                a = a_ref[idx, :]
                h = a * h + b_ref[idx, :]
                p = a * p
                h_ref[idx, :] = h
                p_ref[idx, :] = p
                carry[n] = (h, p)
        return tuple(carry)

    init = tuple((jnp.zeros((8, LANES), F32), jnp.ones((8, LANES), F32)) for _ in chains)
    ends = lax.fori_loop(0, seg // SCAN_STEPS, step, init)
    for (_, _, h_ref, p_ref, reverse), (b, a) in zip(chains, ends):
        for d in (1, 2, 4):
            if reverse:
                keep = rows < 8 - d
                a_n, b_n = pltpu.roll(a, 8 - d, 0), pltpu.roll(b, 8 - d, 0)
            else:
                keep = rows >= d
                a_n, b_n = pltpu.roll(a, d, 0), pltpu.roll(b, d, 0)
            b = a * jnp.where(keep, b_n, 0.0) + b
            a = a * jnp.where(keep, a_n, 1.0)
        for s in (range(7) if reverse else range(1, 8)):
            sl = slice(s * seg, (s + 1) * seg)
            carry_in = b[s + 1:s + 2, :] if reverse else b[s - 1:s, :]
            h_ref[sl, :] = h_ref[sl, :] + p_ref[sl, :] * carry_in


def _shift_rows(x, s, rows, t):
    if s == 0:
        return x
    rolled = pltpu.roll(x, s % t, 0)
    return jnp.where(rows >= s, rolled, 0.0) if s > 0 else jnp.where(rows < t + s, rolled, 0.0)


def _neg_expm1(x, exp_x):
    series = -x * (1.0 + x * (0.5 + x * (1.0 / 6 + x * (1.0 / 24))))
    return jnp.where(x > -0.1, series, 1.0 - exp_x)


def _sigmoid(x):
    return 0.5 * jnp.tanh(0.5 * x) + 0.5


def _gelu_parts(x):
    k = math.sqrt(2.0 / math.pi)
    th = jnp.tanh(k * (x + 0.044715 * x * x * x))
    g = 0.5 * x * (1.0 + th)
    dg = 0.5 * (1.0 + th) + 0.5 * x * (1.0 - th * th) * k * (1.0 + 3 * 0.044715 * x * x)
    return g, dg


def _lru_gates(xc, gates, lam_ref, valid, d):
    r = _sigmoid(gates[:, (2 * d) * LANES:(2 * d + 1) * LANES])
    i = _sigmoid(gates[:, (2 * d + 1) * LANES:(2 * d + 2) * LANES])
    neg_lam = -lam_ref[d:d + 1, :]
    sp = jnp.maximum(neg_lam, 0.0) + jnp.log1p(jnp.exp(-jnp.abs(neg_lam)))
    log_a = -LRU_C * r * sp
    a = jnp.exp(log_a)
    m = jnp.maximum(_neg_expm1(2.0 * log_a, a * a), 0.0)
    sq = jnp.sqrt(m)
    b = jnp.where(valid, sq * (i * xc), 0.0)
    return r, i, sp, a, m, sq, b


def _conv(xr, cw_ref, cb_ref, rows, t):
    return (cw_ref[0:1, :] * _shift_rows(xr, 2, rows, t) + cw_ref[1:2, :] * _shift_rows(xr, 1, rows, t)
            + cw_ref[2:3, :] * xr + cw_ref[3:4, :] * _shift_rows(xr, -1, rows, t) + cb_ref[...])


def _rnn_specs(t):
    seq = pl.BlockSpec((t, LANES), lambda cb, b: (b, cb))
    cw = pl.BlockSpec((4, LANES), lambda cb, b: (0, cb))
    vec1 = pl.BlockSpec((1, LANES), lambda cb, b: (0, cb))
    vec2 = pl.BlockSpec((2, LANES), lambda cb, b: (0, cb))
    wblk = pl.BlockSpec((1, LANES, 4 * LANES), lambda cb, b: (cb, 0, 0))
    gbias = pl.BlockSpec((1, 1, 4 * LANES), lambda cb, b: (cb, 0, 0))
    return seq, cw, vec1, vec2, wblk, gbias


def _rnn_fwd(xr, xg, conv_w, conv_b, wblk, gbias, lam):
    n = xr.shape[0]
    t = _t_pad()
    seq, cw, vec1, vec2, wspec, gspec = _rnn_specs(t)

    def body(xr_ref, xg_ref, cw_ref, cb_ref, w_ref, gb_ref, lam_ref, o_ref, a_s, b_s, h_s, p_s):
        rows = lax.broadcasted_iota(jnp.int32, (t, LANES), 0)
        valid = rows >= PAD_ROWS
        xc = _conv(xr_ref[...], cw_ref, cb_ref, rows, t)
        gates = _dot(xc.astype(BF16), w_ref[0]) + gb_ref[0]
        for d in range(2):
            _, _, _, a, _, _, b = _lru_gates(xc, gates, lam_ref, valid, d)
            a_s[d] = a
            b_s[d] = b
        _scan([(a_s.at[d], b_s.at[d], h_s.at[d], p_s.at[d], d == 1) for d in range(2)], t)
        g, _ = _gelu_parts(xg_ref[...])
        o_ref[...] = (h_s[0] + h_s[1]) * g

    return pl.pallas_call(
        body, name="rnn_fwd", grid=(D_RNN // LANES, n // t),
        in_specs=[seq, seq, cw, vec1, wspec, gspec, vec2], out_specs=seq,
        out_shape=jax.ShapeDtypeStruct((n, D_RNN), F32),
        scratch_shapes=[pltpu.VMEM((2, t, LANES), F32)] * 4,
        compiler_params=pltpu.CompilerParams(dimension_semantics=("parallel", "parallel"), vmem_limit_bytes=VMEM_LIMIT),
    )(xr, xg, conv_w, conv_b, wblk, gbias, lam)


def _rnn_bwd(xr, xg, do, conv_w, conv_b, wblk, gbias, lam):
    n = xr.shape[0]
    t = _t_pad()
    seq, cw, vec1, vec2, wspec, gspec = _rnn_specs(t)

    def body(xr_ref, xg_ref, do_ref, cw_ref, cb_ref, w_ref, gb_ref, lam_ref,
             dxr_ref, dxg_ref, dcw_ref, dcb_ref, dw_ref, dgb_ref, dlam_ref,
             a_s, b_s, h_s, l_s, p_s, back_s, r_s, i_s, q_s, dg_s):
        first = pl.program_id(1) == 0
        rows = lax.broadcasted_iota(jnp.int32, (t, LANES), 0)
        valid = rows >= PAD_ROWS
        xr = xr_ref[...]
        xc = _conv(xr, cw_ref, cb_ref, rows, t)
        xcb = xc.astype(BF16)
        gates = _dot(xcb, w_ref[0]) + gb_ref[0]
        sps = []
        for d in range(2):
            r_s[d], i_s[d], sp, a_s[d], _, q_s[d], b_s[d] = _lru_gates(xc, gates, lam_ref, valid, d)
            sps.append(sp)
        _scan([(a_s.at[d], b_s.at[d], h_s.at[d], p_s.at[d], d == 1) for d in range(2)], t)
        g, dg = _gelu_parts(xg_ref[...])
        do = do_ref[...]
        dxg_ref[...] = do * (h_s[0] + h_s[1]) * dg
        b_s[0] = do * g
        for d in range(2):
            back_s[d] = _shift_rows(a_s[d], -1 if d == 0 else 1, rows, t)
        _scan([(back_s.at[d], b_s.at[0], l_s.at[d], p_s.at[d], d == 0) for d in range(2)], t)
        dxc = jnp.zeros((t, LANES), F32)
        dlams = []
        for d in range(2):
            r, i, sp, a, sq = r_s[d], i_s[d], sps[d], a_s[d], q_s[d]
            lam_t = l_s[d]
            da = lam_t * _shift_rows(h_s[d], 1 if d == 0 else -1, rows, t)
            lam_v = jnp.where(valid, lam_t, 0.0)
            dsq = lam_v * (i * xc)
            di = lam_v * sq * xc
            dxc = dxc + lam_v * sq * i
            dm = jnp.where(sq > 0.0, dsq * 0.5 / jnp.where(sq > 0.0, sq, 1.0), 0.0)
            dla = da * a - 2.0 * dm * a * a
            dr = dla * (-LRU_C) * sp
            dsp = _colsum(dla * (-LRU_C) * r)
            dlams.append(dsp * -jax.nn.sigmoid(-lam_ref[d:d + 1, :]))
            dg_s[:, (2 * d) * LANES:(2 * d + 1) * LANES] = (dr * r * (1.0 - r)).astype(BF16)
            dg_s[:, (2 * d + 1) * LANES:(2 * d + 2) * LANES] = (di * i * (1.0 - i)).astype(BF16)
        dgates = dg_s[...]
        dxc = dxc + _dot_nt(dgates, w_ref[0])
        taps = [_shift_rows(dxc, j - 2, rows, t) for j in range(4)]
        dxr_ref[...] = (cw_ref[0:1, :] * taps[0] + cw_ref[1:2, :] * taps[1] + cw_ref[2:3, :] * taps[2]
                        + cw_ref[3:4, :] * taps[3])
        dcw = jnp.concatenate([_colsum(tap * xr) for tap in taps], axis=0)
        _acc(dcw_ref, first, dcw)
        _acc(dcb_ref, first, _colsum(dxc))
        _acc(dw_ref, first, _dot_tn(xcb, dgates)[None])
        _acc(dgb_ref, first, _colsum(dgates.astype(F32))[None])
        _acc(dlam_ref, first, jnp.concatenate(dlams, axis=0))

    return pl.pallas_call(
        body, name="rnn_bwd", grid=(D_RNN // LANES, n // t),
        in_specs=[seq, seq, seq, cw, vec1, wspec, gspec, vec2],
        out_specs=[seq, seq, cw, vec1, wspec, gspec, vec2],
        out_shape=[jax.ShapeDtypeStruct((n, D_RNN), F32), jax.ShapeDtypeStruct((n, D_RNN), F32),
                   jax.ShapeDtypeStruct((4, D_RNN), F32), jax.ShapeDtypeStruct((1, D_RNN), F32),
                   jax.ShapeDtypeStruct((D_RNN // LANES, LANES, 4 * LANES), F32),
                   jax.ShapeDtypeStruct((D_RNN // LANES, 1, 4 * LANES), F32), jax.ShapeDtypeStruct((2, D_RNN), F32)],
        scratch_shapes=[pltpu.VMEM((2, t, LANES), F32)] * 9 + [pltpu.VMEM((t, 4 * LANES), BF16)],
        compiler_params=pltpu.CompilerParams(dimension_semantics=("parallel", "arbitrary"), vmem_limit_bytes=VMEM_LIMIT),
    )(xr, xg, do, conv_w, conv_b, wblk, gbias, lam)


def _post(oa, orn, h0, tgt, ga, gr, g2, w_out, w_gate, w_up, w_down):
    n = oa.shape[0]
    tm = _row_tile(n)
    t = _t_pad()

    def body(oa_ref, or_ref, h0_ref, tgt_ref, ga_ref, gr_ref, g2_ref, wo_ref, wg_ref, wu_ref, wd_ref,
             doa_ref, dor_ref, dh1_ref, mix_ref, h1n_ref, act_ref, dgate_ref, dup_ref, dy_ref,
             loss_ref, dga_ref, dgr_ref, dg2_ref, gate_s, up_s):
        first = pl.program_id(0) == 0
        xa, ra = _rms(oa_ref[...], D_ATTN)
        xr, rr = _rms(or_ref[...], D_RNN)
        mix = jnp.concatenate([(xa * ga_ref[...]).astype(BF16), (xr * gr_ref[...]).astype(BF16)], axis=-1)
        mix_ref[...] = mix.T
        h1 = h0_ref[...] + _dot(mix, wo_ref[...])
        x2, r2 = _rms(h1, D_MODEL)
        h1n = (x2 * g2_ref[...]).astype(BF16)
        h1n_ref[...] = h1n
        y = h1
        for cs in range(0, D_FF, FF_CHUNK):
            sl = slice(cs, cs + FF_CHUNK)
            gate = _dot_nt(h1n, wg_ref[sl, :])
            up = _dot_nt(h1n, wu_ref[sl, :])
            gate_s[:, sl] = gate
            up_s[:, sl] = up
            act = (gate * _sigmoid(gate) * up).astype(BF16)
            act_ref[sl, :] = act.T
            y = y + _dot(act, wd_ref[sl, :])
        row = pl.program_id(0) * tm + lax.broadcasted_iota(jnp.int32, (tm, 1), 0)
        for _ in range(1, n // t):
            row = jnp.where(row >= t, row - t, row)
        err = jnp.where(row >= PAD_ROWS + N_META, y - tgt_ref[...], 0.0)
        _acc(loss_ref, first, jnp.full((1, LANES), 0.5 / D_MODEL, F32) * jnp.sum(err * err))
        dy = err * (1.0 / D_MODEL)
        dyb = dy.astype(BF16)
        dy_ref[...] = dyb
        dh1n = jnp.zeros((tm, D_MODEL), F32)
        for cs in range(0, D_FF, FF_CHUNK):
            sl = slice(cs, cs + FF_CHUNK)
            dact = _dot_nt(dyb, wd_ref[sl, :])
            gate, up = gate_s[:, sl], up_s[:, sl]
            sg = _sigmoid(gate)
            dgate = (dact * up * sg * (1.0 + gate * (1.0 - sg))).astype(BF16)
            dup = (dact * gate * sg).astype(BF16)
            dgate_ref[sl, :] = dgate.T
            dup_ref[sl, :] = dup.T
            dh1n = dh1n + _dot(dgate, wg_ref[sl, :]) + _dot(dup, wu_ref[sl, :])
        _acc(dg2_ref, first, _colsum(dh1n * x2))
        dh1 = dy + _rms_bwd(dh1n, x2, r2, g2_ref[...], D_MODEL)
        dh1_ref[...] = dh1
        dmix = _dot_nt(dh1.astype(BF16), wo_ref[...])
        dma, dmr = dmix[:, :D_ATTN], dmix[:, D_ATTN:]
        _acc(dga_ref, first, _colsum(dma * xa))
        _acc(dgr_ref, first, _colsum(dmr * xr))
        doa_ref[...] = _rms_bwd(dma, xa, ra, ga_ref[...], D_ATTN)
        dor_ref[...] = _rms_bwd(dmr, xr, rr, gr_ref[...], D_RNN)

    def row(w):
        return pl.BlockSpec((tm, w), lambda i: (i, 0))

    def acc(w):
        return pl.BlockSpec((1, w), lambda i: (0, 0))

    def col(w):
        return pl.BlockSpec((w, tm), lambda i: (0, i))

    outs = [(D_ATTN, F32, row), (D_RNN, F32, row), (D_MODEL, F32, row), (D_MODEL, BF16, col), (D_MODEL, BF16, row),
            (D_FF, BF16, col), (D_FF, BF16, col), (D_FF, BF16, col), (D_MODEL, BF16, row)]
    accs = [LANES, D_ATTN, D_RNN, D_MODEL]
    return pl.pallas_call(
        body, name="post", grid=(n // tm,),
        in_specs=[row(D_ATTN), row(D_RNN), row(D_MODEL), row(D_MODEL),
                  _const_spec((1, D_ATTN)), _const_spec((1, D_RNN)), _const_spec((1, D_MODEL)),
                  _const_spec((D_MODEL, D_MODEL)), _const_spec((D_FF, D_MODEL)), _const_spec((D_FF, D_MODEL)),
                  _const_spec((D_FF, D_MODEL))],
        out_specs=[spec(w) for w, _, spec in outs] + [acc(w) for w in accs],
        out_shape=[jax.ShapeDtypeStruct((n, w) if spec is row else (w, n), dt) for w, dt, spec in outs]
        + [jax.ShapeDtypeStruct((1, w), F32) for w in accs],
        scratch_shapes=[pltpu.VMEM((tm, D_FF), F32), pltpu.VMEM((tm, D_FF), F32)],
        compiler_params=pltpu.CompilerParams(dimension_semantics=("arbitrary",), vmem_limit_bytes=VMEM_LIMIT),
    )(oa, orn, h0, tgt, ga, gr, g2, w_out, w_gate, w_up, w_down)


def _in_bwd(dp, h0, dh1, ln1_g, w_in_p, srcs=(), scatter=()):
    n = h0.shape[0]
    tm = _row_tile(n)
    nk = len(srcs)
    c_in, c_out, c_shape, c_sems = _exchange_specs(srcs, scatter)

    def body(dp_ref, h0_ref, dh1_ref, g_ref, w_ref, *rest):
        dh0_ref, dg_ref = rest[nk:nk + 2]
        finish = _ride(1, *_exchange_fns(rest[:nk], rest[nk + 2:2 * nk + 2], rest[2 * nk + 2:], scatter))
        dhn = _dot(dp_ref[...], w_ref[...])
        xhat, r = _rms(h0_ref[...], D_MODEL)
        _acc(dg_ref, pl.program_id(0) == 0, _colsum(dhn * xhat))
        dh0_ref[...] = dh1_ref[...] + _rms_bwd(dhn, xhat, r, g_ref[...], D_MODEL)
        finish()

    def row(w):
        return pl.BlockSpec((tm, w), lambda i: (i, 0))

    res = pl.pallas_call(
        body, name="in_bwd", grid=(n // tm,),
        in_specs=[row(P_COLS), row(D_MODEL), row(D_MODEL), _const_spec((1, D_MODEL)), _const_spec((P_COLS, D_MODEL))] + c_in,
        out_specs=[row(D_MODEL), pl.BlockSpec((1, D_MODEL), lambda i: (0, 0))] + c_out,
        out_shape=[jax.ShapeDtypeStruct((n, D_MODEL), F32), jax.ShapeDtypeStruct((1, D_MODEL), F32)] + c_shape,
        scratch_shapes=c_sems,
        compiler_params=pltpu.CompilerParams(dimension_semantics=("arbitrary",), vmem_limit_bytes=VMEM_LIMIT),
    )(dp, h0, dh1, ln1_g, w_in_p, *srcs)
    return res[:2], res[2:]


def _pick_tile(width, cap):
    best = LANES
    for mult in range(1, width // LANES + 1):
        cand = mult * LANES
        if width % cand == 0 and cand <= cap:
            best = cand
    return best


def _matmul_tn(name, a, b):
    n, ka = a.shape
    kb = b.shape[1]
    ta, tb = _pick_tile(ka, 1408), _pick_tile(kb, 1408)
    tk = n // 4

    def body(a_ref, b_ref, o_ref):
        _acc(o_ref, pl.program_id(2) == 0, _dot_tn(a_ref[...].astype(BF16), b_ref[...].astype(BF16)))

    return pl.pallas_call(
        body, name=name, grid=(ka // ta, kb // tb, n // tk),
        in_specs=[pl.BlockSpec((tk, ta), lambda i, j, k: (k, i)), pl.BlockSpec((tk, tb), lambda i, j, k: (k, j))],
        out_specs=pl.BlockSpec((ta, tb), lambda i, j, k: (i, j)),
        out_shape=jax.ShapeDtypeStruct((ka, kb), F32),
        compiler_params=pltpu.CompilerParams(dimension_semantics=("parallel", "parallel", "arbitrary"),
                                             vmem_limit_bytes=VMEM_LIMIT),
    )(a, b)


def _matmul_shards(name, at, b):
    ka, n = at.shape
    kb = b.shape[1]
    ta, tb = _pick_tile(ka, 1408), _pick_tile(kb, 1408)
    tk = n // 2
    width = ka // N_DEV
    per = ta // width

    def body(a_ref, b_ref, o_ref, acc_ref):
        _acc(acc_ref, pl.program_id(2) == 0, _dot(a_ref[...], b_ref[...].astype(BF16)))

        @pl.when(pl.program_id(2) == pl.num_programs(2) - 1)
        def _():
            for s in range(per):
                o_ref[s] = acc_ref[s * width:(s + 1) * width, :].astype(BF16)

    return pl.pallas_call(
        body, name=name, grid=(ka // ta, kb // tb, n // tk),
        in_specs=[pl.BlockSpec((ta, tk), lambda i, j, k: (i, k)), pl.BlockSpec((tk, tb), lambda i, j, k: (k, j))],
        out_specs=pl.BlockSpec((per, width, tb), lambda i, j, k: (i, 0, j)),
        out_shape=jax.ShapeDtypeStruct((N_DEV, width, kb), BF16),
        scratch_shapes=[pltpu.VMEM((ta, tb), F32)],
        compiler_params=pltpu.CompilerParams(dimension_semantics=("parallel", "parallel", "arbitrary"),
                                             vmem_limit_bytes=VMEM_LIMIT),
    )(at, b)


def _adamw_math(g8_ref, w_ref, m_ref, v_ref, g_ref, d_ref, nm_ref, nv_ref):
    g = g8_ref[0].astype(F32)
    for s in range(1, N_DEV):
        g = g + g8_ref[s].astype(F32)
    g_ref[...] = g
    nm = ADAM_B1 * m_ref[...] + (1.0 - ADAM_B1) * g
    nv = ADAM_B2 * v_ref[...] + (1.0 - ADAM_B2) * (g * g)
    nm_ref[...] = nm
    nv_ref[...] = nv
    m_hat = nm / (1.0 - ADAM_B1 ** ADAM_STEP)
    v_hat = nv / (1.0 - ADAM_B2 ** ADAM_STEP)
    d_ref[...] = -ADAM_LR * (m_hat / (jnp.sqrt(v_hat) + ADAM_EPS) + ADAM_WD * w_ref[...])


def _adamw_many(name, items):
    count = len(items)

    def body(*refs):
        ins, outs = refs[:4 * count], refs[4 * count:]
        for i in range(count):
            _adamw_math(*ins[4 * i:4 * i + 4], *outs[4 * i:4 * i + 4])

    flat = [a for item in items for a in item]
    res = pl.pallas_call(
        body, name=name,
        out_shape=[jax.ShapeDtypeStruct(item[1].shape, F32) for item in items for _ in range(4)],
        compiler_params=pltpu.CompilerParams(vmem_limit_bytes=VMEM_LIMIT),
    )(*flat)
    return [tuple(res[4 * i:4 * i + 4]) for i in range(count)]


def _adamw(name, g8, w, m, v):
    rows, cols = w.shape
    tr = rows
    for cand in (256, 176, 128, 64):
        if rows % cand == 0 and rows > cand:
            tr = cand
            break

    def body(*refs):
        _adamw_math(*refs)

    blk = pl.BlockSpec((tr, cols), lambda i: (i, 0))
    return pl.pallas_call(
        body, name=name, grid=(rows // tr,),
        in_specs=[pl.BlockSpec((N_DEV, tr, cols), lambda i: (0, i, 0)), blk, blk, blk],
        out_specs=[blk] * 4, out_shape=[jax.ShapeDtypeStruct((rows, cols), F32)] * 4,
        compiler_params=pltpu.CompilerParams(dimension_semantics=("parallel",), vmem_limit_bytes=VMEM_LIMIT),
    )(g8, w, m, v)


def _exchange_specs(srcs, scatter):
    nk = len(srcs)
    if not nk:
        return [], [], [], []
    any_spec = pl.BlockSpec(memory_space=pl.ANY)
    out_shape = [jax.ShapeDtypeStruct(s.shape if sc else (N_DEV,) + s.shape, s.dtype) for s, sc in zip(srcs, scatter)]
    sems = [pltpu.SemaphoreType.DMA((nk, N_DEV - 1)), pltpu.SemaphoreType.DMA((nk, N_DEV - 1)),
            pltpu.SemaphoreType.DMA((nk,))]
    return [any_spec] * nk, [any_spec] * nk, out_shape, sems


FLIPS = ((0, 0, 1), (1, 0, 0), (0, 1, 0), (1, 1, 0), (1, 0, 1), (0, 1, 1), (1, 1, 1))
N_CHIP_PEERS = 3


def _exchange_fns(src_refs, out_refs, sems, scatter):
    nk = len(src_refs)
    if not nk:
        return (lambda: None), (lambda: None), (lambda: None)
    send_sems, recv_sems, local_sems = sems
    first = 1 + N_CHIP_PEERS

    def plan():
        x, y, c = lax.axis_index("x"), lax.axis_index("y"), lax.axis_index("c")
        me = 4 * x + 2 * y + c
        peers = [(1 - x if fx else x, 1 - y if fy else y, 1 - c if fc else c) for fx, fy, fc in FLIPS]
        pids = [4 * px + 2 * py + pc for px, py, pc in peers]

        def remote(k, j, src, dst, to):
            return pltpu.make_async_remote_copy(src_ref=src, dst_ref=dst, send_sem=send_sems.at[k, j],
                                                recv_sem=recv_sems.at[k, j], device_id=to, device_id_type=MESH)

        def mine(k, dest):
            return src_refs[k].at[dest] if scatter[k] else src_refs[k]

        local = [pltpu.make_async_copy(mine(k, me), out_refs[k].at[me], local_sems.at[k]) for k in range(nk)]
        direct = [remote(k, j, mine(k, pids[j]), out_refs[k].at[me], peers[j])
                  for k in range(nk) for j in range(len(FLIPS) if scatter[k] else first)]
        relays = {(k, j): remote(k, j, out_refs[k].at[pids[j - N_CHIP_PEERS]], out_refs[k].at[pids[j - N_CHIP_PEERS]], peers[0])
                  for k in range(nk) if not scatter[k] for j in range(first, len(FLIPS))}
        arrivals = {(k, j): remote(k, j, out_refs[k].at[pids[j]], out_refs[k].at[pids[j]], peers[j])
                    for k in range(nk) for j in range(len(FLIPS))}
        return local, direct, relays, arrivals

    def start():
        local, direct, _, _ = plan()
        for cp in local + direct:
            cp.start()

    def relay():
        _, _, relays, arrivals = plan()
        for (k, j), cp in relays.items():
            arrivals[k, j - N_CHIP_PEERS].wait_recv()
            cp.start()

    def wait():
        local, direct, relays, arrivals = plan()
        for (k, j), cp in arrivals.items():
            if (k, j + N_CHIP_PEERS) not in relays:
                cp.wait_recv()
        for cp in direct + list(relays.values()):
            cp.wait_send()
        for cp in local:
            cp.wait()

    return start, relay, wait


def _grid_step(rank):
    step, total = 0, 1
    for axis in range(rank):
        step = step * pl.num_programs(axis) + pl.program_id(axis)
        total = total * pl.num_programs(axis)
    return step, total


def _ride(rank, start, relay, wait):
    step, total = _grid_step(rank)
    pl.when(step == 0)(start)
    pl.when(step == (3 * total) // 4)(relay)
    return lambda: pl.when(step == total - 1)(wait)


def _exchange(name, srcs, scatter):
    nk = len(srcs)
    c_in, c_out, c_shape, c_sems = _exchange_specs(srcs, scatter)

    def body(*refs):
        start, relay, wait = _exchange_fns(refs[:nk], refs[nk:2 * nk], refs[2 * nk:], scatter)
        start()
        relay()
        wait()

    return pl.pallas_call(body, name=name, in_specs=c_in, out_specs=c_out, out_shape=c_shape, scratch_shapes=c_sems)(*srcs)


def _cols_from_shards(g):
    return jnp.transpose(g, (1, 0, 2)).reshape(g.shape[1], -1)


def _cols_to_shards(w):
    return jnp.transpose(w.reshape(w.shape[0], N_DEV, -1), (1, 0, 2))


def _prep(x, tgt, srcs, scatter):
    nb = x.shape[0]
    t = _t_pad()
    head = PAD_ROWS + N_META
    nk = len(srcs)
    c_in, c_out, c_shape, c_sems = _exchange_specs(srcs, scatter)

    def body(x_ref, tgt_ref, *rest):
        h0_ref, tp_ref = rest[nk:nk + 2]
        sems = rest[2 * nk + 2:]
        start, relay, wait = _exchange_fns(rest[:nk], rest[nk + 2:2 * nk + 2], sems[:len(c_sems)], scatter)
        copy_sems, zeros = sems[len(c_sems):]
        start()
        zeros[...] = jnp.zeros_like(zeros)
        copies = []
        for b in range(nb):
            for k, (src, dst) in enumerate(((x_ref, h0_ref), (tgt_ref, tp_ref))):
                copies.append(pltpu.make_async_copy(src.at[b], dst.at[b, pl.ds(head, t - head)], copy_sems.at[4 * b + k]))
                copies.append(pltpu.make_async_copy(zeros, dst.at[b, pl.ds(0, head)], copy_sems.at[4 * b + 2 + k]))
        for cp in copies:
            cp.start()
        relay()
        for cp in copies:
            cp.wait()
        wait()

    any_spec = pl.BlockSpec(memory_space=pl.ANY)
    padded = jax.ShapeDtypeStruct((nb, t, D_MODEL), F32)
    res = pl.pallas_call(
        body, name="prep", in_specs=[any_spec, any_spec] + c_in, out_specs=[any_spec, any_spec] + c_out,
        out_shape=[padded, padded] + c_shape,
        scratch_shapes=c_sems + [pltpu.SemaphoreType.DMA((4 * nb,)), pltpu.VMEM((head, D_MODEL), F32)],
    )(x, tgt, *srcs)
    return res[0], res[1], res[2:]


def _rope_tables(n):
    t = _t_pad()
    pos = np.arange(t, dtype=np.float32) - np.float32(PAD_ROWS)
    half = QK_ROPE // 2
    freqs = (1.0 / (ROPE_THETA ** (np.arange(half, dtype=np.float32) / half))).astype(np.float32)
    ang = pos[:, None] * freqs[None, :]
    cos, sin = np.cos(ang), np.sin(ang)
    z = lambda w: np.zeros((t, w), np.float32)
    c = np.concatenate([np.ones((t, QK_NOPE), np.float32), cos, cos, z(HEAD_PAD - QK_HEAD)], axis=1)
    s1 = np.concatenate([z(QK_NOPE + half), sin, z(HEAD_PAD - QK_HEAD)], axis=1)
    s2 = np.concatenate([z(QK_NOPE), -sin, z(HEAD_PAD - QK_NOPE - half)], axis=1)
    return tuple(jnp.asarray(np.tile(a, (n // t, 1))) for a in (c, s1, s2))


def _block_diag_gates(lru_wa, lru_wi):
    eye = jnp.eye(2, dtype=lru_wa.dtype)

    def bd(w):
        w = w.reshape(2, D_RNN // LANES, 2, RNN_BW, RNN_BW)
        full = w[:, :, :, :, None, :] * eye[None, None, :, None, :, None]
        return full.reshape(2, D_RNN // LANES, LANES, LANES)

    a, i = bd(lru_wa), bd(lru_wi)
    return jnp.concatenate([a[0], i[0], a[1], i[1]], axis=-1)


def _unblock_gates(dw):
    nb = D_RNN // LANES
    parts = dw.reshape(nb, 2, RNN_BW, 4, 2, RNN_BW)
    diag = jnp.stack([parts[:, k, :, :, k, :] for k in range(2)], axis=1)
    diag = jnp.transpose(diag, (3, 0, 1, 2, 4)).reshape(4, 2 * nb, RNN_BW, RNN_BW)
    return jnp.stack([diag[0], diag[2]]), jnp.stack([diag[1], diag[3]])


WEIGHTS = ("meta_tokens", "ln1_g", "w_in", "q_a_norm_g", "w_uq", "kv_a_norm_g", "w_ukv", "q_norm_g", "k_norm_g",
           "conv_w", "conv_b", "lru_wa", "lru_ba", "lru_wi", "lru_bi", "lru_lambda", "attn_out_g", "rnn_out_g",
           "w_out", "ln2_g", "w_gate", "w_up", "w_down")
BIG = ("w_in", "w_uq", "w_ukv", "w_out", "w_gate", "w_up", "w_down")
TRANSPOSED = ("w_in", "w_uq", "w_gate", "w_up")
ROW_SHARDED = ("w_out", "w_down") + TRANSPOSED
REPLICATED = ("ln1_g", "q_a_norm_g", "kv_a_norm_g", "q_norm_g", "k_norm_g", "conv_b", "lru_wa", "lru_wi",
              "attn_out_g", "rnn_out_g", "ln2_g")
WHOLE = REPLICATED + ("loss",)
G_FIRST = ("w_in", "meta_tokens")
G_MID = ("w_uq", "w_ukv", "conv_w", "lru_ba", "lru_bi", "lru_lambda")
LATE = ("w_out", "w_gate", "w_up", "w_down")
G_LAST = ("meta_tokens", "ln1_g")


def _local_step(x, tgt, ex):
    nb = x.shape[0]
    t = _t_pad()
    n = nb * t
    local = ex.local
    h0, tgt_p, got = _prep(x, tgt, *ex.gather_srcs(G_FIRST))
    first = ex.gathered(G_FIRST, got)
    meta, w_in = first["meta_tokens"], first["w_in"]
    h0 = h0.at[:, PAD_ROWS:PAD_ROWS + N_META].set(jnp.broadcast_to(meta[None], (nb, N_META, D_MODEL))).reshape(n, D_MODEL)
    tgt_p = tgt_p.reshape(n, D_MODEL)

    zr = lambda r: jnp.zeros((r, D_MODEL), w_in.dtype)
    w_in_p = jnp.concatenate([w_in[:OFF_CKV], w_in[OFF_KR:], zr(QK_NOPE), w_in[OFF_CKV:OFF_KR], zr(HEAD_PAD - QK_HEAD)],
                             axis=0)
    pad_g = lambda g: jnp.pad(g, ((0, 0), (0, HEAD_PAD - QK_HEAD)))
    qg, kg = pad_g(local["q_norm_g"]), pad_g(local["k_norm_g"])
    rc, rs1, rs2 = _rope_tables(n)
    wblk = _block_diag_gates(local["lru_wa"].reshape(2, -1, RNN_BW, RNN_BW),
                             local["lru_wi"].reshape(2, -1, RNN_BW, RNN_BW)).astype(BF16)
    nblk = D_RNN // LANES

    (hn, cq, ckv, xr, xg, kr), got = _in_proj(h0, local["ln1_g"], w_in_p, *ex.gather_srcs(G_MID))
    w = ex.gathered(G_MID, got)
    w_uq_p = jnp.pad(w["w_uq"].reshape(N_HEADS, QK_HEAD, Q_LORA), ((0, 0), (0, HEAD_PAD - QK_HEAD), (0, 0))
                     ).reshape(QP_COLS, Q_LORA)
    ukv = w["w_ukv"].reshape(KV_LORA, N_HEADS, QK_NOPE + V_HEAD)
    w_uk_p = jnp.pad(ukv[:, :, :QK_NOPE], ((0, 0), (0, 0), (0, HEAD_PAD - QK_NOPE))).reshape(KV_LORA, QP_COLS)
    w_v = ukv[:, :, QK_NOPE:].reshape(KV_LORA, D_ATTN)
    gbias = jnp.stack([w["lru_ba"][0], w["lru_bi"][0], w["lru_ba"][1], w["lru_bi"][1]], axis=0)
    gbias = jnp.transpose(gbias.reshape(4, nblk, LANES), (1, 0, 2)).reshape(nblk, 1, 4 * LANES)

    q, k, v = _qkv_fwd(cq, ckv, kr, local["q_a_norm_g"], local["kv_a_norm_g"], w_uq_p, w_uk_p, w_v, qg, kg, rc, rs1, rs2)
    oa, got = _attn_fwd(q, k, v, *ex.gather_srcs(LATE))
    late = ex.gathered(LATE, got)
    orn = _rnn_fwd(xr, xg, w["conv_w"], local["conv_b"], wblk, gbias, w["lru_lambda"])
    (doa, dor, dh1, mix_t, h1n, act_t, dgate_t, dup_t, dyb, loss, dga, dgr, dg2) = _post(
        oa, orn, h0, tgt_p, local["attn_out_g"], local["rnn_out_g"], local["ln2_g"], late["w_out"], late["w_gate"],
        late["w_up"], late["w_down"])
    wire = {"w_out": _matmul_shards("dw_out", mix_t, dh1), "w_gate": _matmul_shards("dw_gate", dgate_t, h1n),
            "w_up": _matmul_shards("dw_up", dup_t, h1n), "w_down": _matmul_shards("dw_down", act_t, dyb)}
    dxr, dxg, dcw, dcb, dwblk, dgb, dlam = _rnn_bwd(xr, xg, dor, w["conv_w"], local["conv_b"], wblk, gbias, w["lru_lambda"])
    dwa, dwi = _unblock_gates(dwblk)
    dgb = jnp.transpose(dgb.reshape(nblk, 4, LANES), (1, 0, 2)).reshape(4, D_RNN)
    wire.update(ex.to_wire({
        "conv_w": dcw, "conv_b": dcb, "lru_wa": dwa.reshape(-1, RNN_BW), "lru_ba": jnp.stack([dgb[0], dgb[2]]),
        "lru_wi": dwi.reshape(-1, RNN_BW), "lru_bi": jnp.stack([dgb[1], dgb[3]]), "lru_lambda": dlam,
        "attn_out_g": dga, "rnn_out_g": dgr, "ln2_g": dg2, "loss": loss}))
    names = tuple(wire)
    (dq_r, dk_r, dv), got = _attn_bwd(q, k, v, doa, oa, *ex.scatter_srcs(names, wire))
    summed = ex.scattered(names, wire, got)
    (dp, qa, kva, dqp, dkv, dqg, dkg, dgqa, dgkva) = _qkv_bwd(
        cq, ckv, kr, dq_r, dk_r, dv, dxr, dxg, local["q_a_norm_g"], local["kv_a_norm_g"], w_uq_p, w_uk_p, w_v, qg, kg,
        rc, rs1, rs2)
    dw_in_p = _matmul_tn("dw_in", dp, hn)
    dw_uq_p = _matmul_tn("dw_uq", dqp, qa)
    dw_kv = _matmul_tn("dw_ukv", kva, dkv)
    kr0 = OFF_CKV + 2 * D_RNN + QK_NOPE
    dw_in = jnp.concatenate([dw_in_p[:OFF_CKV], dw_in_p[kr0:kr0 + QK_ROPE], dw_in_p[OFF_CKV:OFF_CKV + 2 * D_RNN]], axis=0)
    dw_uq = dw_uq_p.reshape(N_HEADS, HEAD_PAD, Q_LORA)[:, :QK_HEAD].reshape(N_HEADS * QK_HEAD, Q_LORA)
    dw_ukv = jnp.concatenate([dw_kv[:, :QP_COLS].reshape(KV_LORA, N_HEADS, HEAD_PAD)[:, :, :QK_NOPE],
                              dw_kv[:, QP_COLS:].reshape(KV_LORA, N_HEADS, V_HEAD)], axis=2).reshape(KV_LORA, -1)
    wire = ex.to_wire({"w_in": dw_in, "q_a_norm_g": dgqa, "w_uq": dw_uq, "kv_a_norm_g": dgkva, "w_ukv": dw_ukv,
                       "q_norm_g": dqg[:, :QK_HEAD], "k_norm_g": dkg[:, :QK_HEAD]})
    names = tuple(wire)
    (dh0, dg1), got = _in_bwd(dp, h0, dh1, local["ln1_g"], w_in_p, *ex.scatter_srcs(names, wire))
    summed.update(ex.scattered(names, wire, got))

    dh0 = dh0.reshape(nb, t, D_MODEL)
    wire = ex.to_wire({"meta_tokens": jnp.sum(dh0[:, PAD_ROWS:PAD_ROWS + N_META], axis=0), "ln1_g": dg1})
    got = ex.run("reduce_last", *ex.scatter_srcs(G_LAST, wire))
    summed.update(ex.scattered(G_LAST, wire, got))
    return dh0[:, PAD_ROWS + N_META:], summed


class _MeshExchange:
    def __init__(self, shards):
        self.local = shards

    @staticmethod
    def run(name, srcs, scatter):
        return _exchange(name, srcs, scatter)

    def gather_srcs(self, names):
        return [self.local[k].astype(BF16) if k in BIG else self.local[k] for k in names], [False] * len(names)

    @staticmethod
    def gathered(names, outs):
        return {k: g.reshape(-1, g.shape[-1]) if k in ROW_SHARDED else _cols_from_shards(g) for k, g in zip(names, outs)}

    @staticmethod
    def to_wire(grads):
        wire = {}
        for k, g in grads.items():
            if k in WHOLE:
                wire[k] = g
            elif k in ROW_SHARDED:
                wire[k] = g.reshape(N_DEV, -1, g.shape[-1]).astype(BF16)
            else:
                wire[k] = _cols_to_shards(g).astype(BF16) if k in BIG else _cols_to_shards(g)
        return wire

    @staticmethod
    def scatter_srcs(names, wire):
        return [wire[k] for k in names], [k not in WHOLE for k in names]

    @staticmethod
    def scattered(names, wire, outs):
        return dict(zip(names, outs))


def kernel(x, meta_tokens, ln1_g, w_in, q_a_norm_g, w_uq, kv_a_norm_g, w_ukv, q_norm_g, k_norm_g, conv_w, conv_b, lru_wa, lru_ba, lru_wi, lru_bi, lru_lambda, attn_out_g, rnn_out_g, w_out, ln2_g, w_gate, w_up, w_down, loss_target, m_meta_tokens, m_ln1_g, m_w_in, m_q_a_norm_g, m_w_uq, m_kv_a_norm_g, m_w_ukv, m_q_norm_g, m_k_norm_g, m_conv_w, m_conv_b, m_lru_wa, m_lru_ba, m_lru_wi, m_lru_bi, m_lru_lambda, m_attn_out_g, m_rnn_out_g, m_w_out, m_ln2_g, m_w_gate, m_w_up, m_w_down, v_meta_tokens, v_ln1_g, v_w_in, v_q_a_norm_g, v_w_uq, v_kv_a_norm_g, v_w_ukv, v_q_norm_g, v_k_norm_g, v_conv_w, v_conv_b, v_lru_wa, v_lru_ba, v_lru_wi, v_lru_bi, v_lru_lambda, v_attn_out_g, v_rnn_out_g, v_w_out, v_ln2_g, v_w_gate, v_w_up, v_w_down):
    given = (meta_tokens, ln1_g, w_in, q_a_norm_g, w_uq, kv_a_norm_g, w_ukv, q_norm_g, k_norm_g, conv_w, conv_b,
             lru_wa, lru_ba, lru_wi, lru_bi, lru_lambda, attn_out_g, rnn_out_g, w_out, ln2_g, w_gate, w_up, w_down)
    moments_m = (m_meta_tokens, m_ln1_g, m_w_in, m_q_a_norm_g, m_w_uq, m_kv_a_norm_g, m_w_ukv, m_q_norm_g, m_k_norm_g,
                 m_conv_w, m_conv_b, m_lru_wa, m_lru_ba, m_lru_wi, m_lru_bi, m_lru_lambda, m_attn_out_g, m_rnn_out_g,
                 m_w_out, m_ln2_g, m_w_gate, m_w_up, m_w_down)
    moments_v = (v_meta_tokens, v_ln1_g, v_w_in, v_q_a_norm_g, v_w_uq, v_kv_a_norm_g, v_w_ukv, v_q_norm_g, v_k_norm_g,
                 v_conv_w, v_conv_b, v_lru_wa, v_lru_ba, v_lru_wi, v_lru_bi, v_lru_lambda, v_attn_out_g, v_rnn_out_g,
                 v_w_out, v_ln2_g, v_w_gate, v_w_up, v_w_down)
    shapes = {k: a.shape for k, a in zip(WEIGHTS, given)}

    def two_d(k, a):
        a = a.reshape(-1, a.shape[-1])
        return a.T if k in TRANSPOSED else a

    w = {k: two_d(k, a) for k, a in zip(WEIGHTS, given)}
    m = {k: two_d(k, a) for k, a in zip(WEIGHTS, moments_m)}
    v = {k: two_d(k, a) for k, a in zip(WEIGHTS, moments_v)}

    grad_x, parts = _local_step(x, loss_target, _MeshExchange(w))

    new = {k: _adamw("adamw_" + k, parts[k], w[k], m[k], v[k]) for k in BIG}
    small = [k for k in WEIGHTS if k not in BIG]
    new.update(zip(small, _adamw_many("adamw_small", [(parts[k], w[k], m[k], v[k]) for k in small])))

    loss = jnp.sum(parts["loss"][:, 0, 0])
    outs = [loss, grad_x]
    for idx in range(4):
        outs += [(new[k][idx].T if k in TRANSPOSED else new[k][idx]).reshape(shapes[k]) for k in WEIGHTS]
    return tuple(outs)
```

```python
import functools
import math

import numpy as np
import jax
import jax.numpy as jnp
from jax import lax
from jax.experimental import pallas as pl
from jax.experimental.pallas import tpu as pltpu

F32 = jnp.float32
BF16 = jnp.bfloat16

D_MODEL = 1024
N_META = 16
SEQ = 2048
N_HEADS = 8
QK_NOPE = 64
QK_ROPE = 32
QK_HEAD = QK_NOPE + QK_ROPE
V_HEAD = 64
D_ATTN = N_HEADS * V_HEAD
Q_LORA = 384
KV_LORA = 256
D_RNN = 512
RNN_BW = 64
D_FF = 2816
EPS = 1e-6
LRU_C = 8.0
ROPE_THETA = 10000.0
OFF_CKV = Q_LORA + KV_LORA
OFF_KR = OFF_CKV + QK_ROPE
IN_COLS = OFF_KR + 2 * D_RNN

ADAM_LR = 0.001
ADAM_B1 = 0.9
ADAM_B2 = 0.999
ADAM_EPS = 1e-08
ADAM_WD = 0.01
ADAM_STEP = 10

N_DEV = 8
LANES = 128
HEAD_PAD = LANES
PAD_ROWS = LANES - N_META
QP_COLS = N_HEADS * HEAD_PAD
P_COLS = OFF_CKV + 2 * D_RNN + LANES
FF_CHUNK = D_FF // 2
VMEM_LIMIT = 56 * 1024 * 1024
MESH = pl.DeviceIdType.MESH


def _t_pad():
    return PAD_ROWS + N_META + SEQ


def _row_tile(n):
    return 256 if n % 256 == 0 else 128


def _const_spec(shape):
    nd = len(shape)
    return pl.BlockSpec(shape, lambda *_: (0,) * nd, pipeline_mode=pl.Buffered(1))


def _rms(x, d):
    r = lax.rsqrt(jnp.sum(x * x, axis=-1, keepdims=True) * (1.0 / d) + EPS)
    return x * r, r


def _rms_bwd(dy, xhat, r, g, d):
    dxh = dy * g
    return r * (dxh - xhat * (jnp.sum(dxh * xhat, axis=-1, keepdims=True) * (1.0 / d)))


def _colsum(x):
    return jnp.sum(x, axis=0, keepdims=True)


def _dot(a, b):
    return jnp.dot(a, b, preferred_element_type=F32)


def _dot_nt(a, b):
    return lax.dot_general(a, b, (((1,), (1,)), ((), ())), preferred_element_type=F32)


def _dot_tn(a, b):
    return lax.dot_general(a, b, (((0,), (0,)), ((), ())), preferred_element_type=F32)


def _rope(x, c, s1, s2):
    return x * c + pltpu.roll(x, 16, 1) * s1 + pltpu.roll(x, HEAD_PAD - 16, 1) * s2


def _rope_bwd(dy, c, s1, s2):
    return dy * c + pltpu.roll(dy * s1, HEAD_PAD - 16, 1) + pltpu.roll(dy * s2, 16, 1)


def _acc(ref, first, val):
    @pl.when(first)
    def _():
        ref[...] = val

    @pl.when(jnp.logical_not(first))
    def _():
        ref[...] += val


def _in_proj(h0, ln1_g, w_in_p, srcs=(), scatter=()):
    n = h0.shape[0]
    tm = _row_tile(n)
    nk = len(srcs)
    c_in, c_out, c_shape, c_sems = _exchange_specs(srcs, scatter)

    def body(h_ref, g_ref, w_ref, *rest):
        hn_ref, cq_ref, ckv_ref, xr_ref, xg_ref, kr_ref = rest[nk:nk + 6]
        finish = _ride(1, *_exchange_fns(rest[:nk], rest[nk + 6:2 * nk + 6], rest[2 * nk + 6:], scatter))
        xhat, _ = _rms(h_ref[...], D_MODEL)
        hn = (xhat * g_ref[...]).astype(BF16)
        hn_ref[...] = hn
        p = _dot_nt(hn, w_ref[...])
        cq_ref[...] = p[:, :Q_LORA]
        ckv_ref[...] = p[:, Q_LORA:OFF_CKV]
        xr_ref[...] = p[:, OFF_CKV:OFF_CKV + D_RNN]
        xg_ref[...] = p[:, OFF_CKV + D_RNN:OFF_CKV + 2 * D_RNN]
        kr_ref[...] = p[:, OFF_CKV + 2 * D_RNN:]
        finish()

    def row(w):
        return pl.BlockSpec((tm, w), lambda i: (i, 0))

    widths = (D_MODEL, Q_LORA, KV_LORA, D_RNN, D_RNN, LANES)
    res = pl.pallas_call(
        body, name="in_proj", grid=(n // tm,),
        in_specs=[row(D_MODEL), _const_spec((1, D_MODEL)), _const_spec((P_COLS, D_MODEL))] + c_in,
        out_specs=[row(w) for w in widths] + c_out,
        out_shape=[jax.ShapeDtypeStruct((n, w), BF16 if k == 0 else F32) for k, w in enumerate(widths)] + c_shape,
        scratch_shapes=c_sems,
        compiler_params=pltpu.CompilerParams(dimension_semantics=("arbitrary",), vmem_limit_bytes=VMEM_LIMIT),
    )(h0, ln1_g, w_in_p, *srcs)
    return res[:6], res[6:]


def _qkv_fwd(cq, ckv, kr, gqa, gkva, w_uq_p, w_uk_p, w_v, qg, kg, rc, rs1, rs2):
    n = cq.shape[0]
    tm = _row_tile(n)

    def body(cq_ref, ckv_ref, kr_ref, gqa_ref, gkva_ref, wuq_ref, wuk_ref, wv_ref, qg_ref, kg_ref,
             c_ref, s1_ref, s2_ref, q_ref, k_ref, v_ref):
        xq, _ = _rms(cq_ref[...], Q_LORA)
        qa = (xq * gqa_ref[...]).astype(BF16)
        q = _dot_nt(qa, wuq_ref[...])
        xkv, _ = _rms(ckv_ref[...], KV_LORA)
        kva = (xkv * gkva_ref[...]).astype(BF16)
        kn = _dot(kva, wuk_ref[...])
        v_ref[...] = _dot(kva, wv_ref[...]).astype(BF16)
        krp = kr_ref[...]
        c, s1, s2 = c_ref[...], s1_ref[...], s2_ref[...]
        for h in range(N_HEADS):
            sl = slice(h * HEAD_PAD, (h + 1) * HEAD_PAD)
            qh, _ = _rms(q[:, sl], QK_HEAD)
            q_ref[:, sl] = _rope(qh * qg_ref[...], c, s1, s2).astype(BF16)
            kh, _ = _rms(kn[:, sl] + krp, QK_HEAD)
            k_ref[:, sl] = _rope(kh * kg_ref[...], c, s1, s2).astype(BF16)

    def row(w):
        return pl.BlockSpec((tm, w), lambda i: (i, 0))

    return pl.pallas_call(
        body, name="qkv_fwd", grid=(n // tm,),
        in_specs=[row(Q_LORA), row(KV_LORA), row(LANES), _const_spec((1, Q_LORA)), _const_spec((1, KV_LORA)),
                  _const_spec((QP_COLS, Q_LORA)), _const_spec((KV_LORA, QP_COLS)), _const_spec((KV_LORA, D_ATTN)),
                  _const_spec((1, LANES)), _const_spec((1, LANES)), row(LANES), row(LANES), row(LANES)],
        out_specs=[row(QP_COLS), row(QP_COLS), row(D_ATTN)],
        out_shape=[jax.ShapeDtypeStruct((n, QP_COLS), BF16), jax.ShapeDtypeStruct((n, QP_COLS), BF16),
                   jax.ShapeDtypeStruct((n, D_ATTN), BF16)],
        compiler_params=pltpu.CompilerParams(dimension_semantics=("parallel",), vmem_limit_bytes=VMEM_LIMIT),
    )(cq, ckv, kr, gqa, gkva, w_uq_p, w_uk_p, w_v, qg, kg, rc, rs1, rs2)


def _qkv_bwd(cq, ckv, kr, dq_r, dk_r, dv, dxr, dxg, gqa, gkva, w_uq_p, w_uk_p, w_v, qg, kg, rc, rs1, rs2):
    n = cq.shape[0]
    tm = _row_tile(n)

    def body(cq_ref, ckv_ref, kr_ref, dq_ref, dk_ref, dv_ref, dxr_ref, dxg_ref, gqa_ref, gkva_ref, wuq_ref, wuk_ref,
             wv_ref, qg_ref, kg_ref, c_ref, s1_ref, s2_ref,
             dp_ref, qa_ref, kva_ref, dqp_ref, dkv_ref, dqg_ref, dkg_ref, dgqa_ref, dgkva_ref):
        first = pl.program_id(0) == 0
        dp_ref[:, OFF_CKV:OFF_CKV + D_RNN] = dxr_ref[...].astype(BF16)
        dp_ref[:, OFF_CKV + D_RNN:OFF_CKV + 2 * D_RNN] = dxg_ref[...].astype(BF16)
        xq, rq = _rms(cq_ref[...], Q_LORA)
        qa = (xq * gqa_ref[...]).astype(BF16)
        qa_ref[...] = qa
        q = _dot_nt(qa, wuq_ref[...])
        xkv, rkv = _rms(ckv_ref[...], KV_LORA)
        kva = (xkv * gkva_ref[...]).astype(BF16)
        kva_ref[...] = kva
        kn = _dot(kva, wuk_ref[...])
        krp = kr_ref[...]
        c, s1, s2 = c_ref[...], s1_ref[...], s2_ref[...]
        lane = lax.broadcasted_iota(jnp.int32, (tm, HEAD_PAD), 1)
        rope_lanes = jnp.logical_and(lane >= QK_NOPE, lane < QK_HEAD)
        dqg = jnp.zeros((1, HEAD_PAD), F32)
        dkg = jnp.zeros((1, HEAD_PAD), F32)
        dkr = jnp.zeros((tm, HEAD_PAD), F32)
        for h in range(N_HEADS):
            sl = slice(h * HEAD_PAD, (h + 1) * HEAD_PAD)
            qh, rqh = _rms(q[:, sl], QK_HEAD)
            dy = _rope_bwd(dq_ref[:, sl], c, s1, s2)
            dqg = dqg + _colsum(dy * qh)
            dqp_ref[:, sl] = _rms_bwd(dy, qh, rqh, qg_ref[...], QK_HEAD).astype(BF16)
            kh, rkh = _rms(kn[:, sl] + krp, QK_HEAD)
            dyk = _rope_bwd(dk_ref[:, sl], c, s1, s2)
            dkg = dkg + _colsum(dyk * kh)
            dkh = _rms_bwd(dyk, kh, rkh, kg_ref[...], QK_HEAD)
            dkv_ref[:, sl] = dkh.astype(BF16)
            dkr = dkr + jnp.where(rope_lanes, dkh, 0.0)
        dkv_ref[:, QP_COLS:] = dv_ref[...].astype(BF16)
        dp_ref[:, OFF_CKV + 2 * D_RNN:] = dkr.astype(BF16)
        dqa = _dot(dqp_ref[...], wuq_ref[...])
        dp_ref[:, :Q_LORA] = _rms_bwd(dqa, xq, rq, gqa_ref[...], Q_LORA).astype(BF16)
        dkva = _dot_nt(dkv_ref[:, :QP_COLS], wuk_ref[...]) + _dot_nt(dkv_ref[:, QP_COLS:], wv_ref[...])
        dp_ref[:, Q_LORA:OFF_CKV] = _rms_bwd(dkva, xkv, rkv, gkva_ref[...], KV_LORA).astype(BF16)
        _acc(dqg_ref, first, dqg)
        _acc(dkg_ref, first, dkg)
        _acc(dgqa_ref, first, _colsum(dqa * xq))
        _acc(dgkva_ref, first, _colsum(dkva * xkv))

    def row(w):
        return pl.BlockSpec((tm, w), lambda i: (i, 0))

    def acc(w):
        return pl.BlockSpec((1, w), lambda i: (0, 0))

    return pl.pallas_call(
        body, name="qkv_bwd", grid=(n // tm,),
        in_specs=[row(Q_LORA), row(KV_LORA), row(LANES), row(QP_COLS), row(QP_COLS), row(D_ATTN), row(D_RNN), row(D_RNN),
                  _const_spec((1, Q_LORA)), _const_spec((1, KV_LORA)),
                  _const_spec((QP_COLS, Q_LORA)), _const_spec((KV_LORA, QP_COLS)), _const_spec((KV_LORA, D_ATTN)),
                  _const_spec((1, LANES)), _const_spec((1, LANES)), row(LANES), row(LANES), row(LANES)],
        out_specs=[row(P_COLS), row(Q_LORA), row(KV_LORA), row(QP_COLS),
                   row(QP_COLS + D_ATTN), acc(LANES), acc(LANES), acc(Q_LORA), acc(KV_LORA)],
        out_shape=[jax.ShapeDtypeStruct((n, P_COLS), BF16), jax.ShapeDtypeStruct((n, Q_LORA), BF16),
                   jax.ShapeDtypeStruct((n, KV_LORA), BF16), jax.ShapeDtypeStruct((n, QP_COLS), BF16),
                   jax.ShapeDtypeStruct((n, QP_COLS + D_ATTN), BF16),
                   jax.ShapeDtypeStruct((1, LANES), F32), jax.ShapeDtypeStruct((1, LANES), F32),
                   jax.ShapeDtypeStruct((1, Q_LORA), F32), jax.ShapeDtypeStruct((1, KV_LORA), F32)],
        compiler_params=pltpu.CompilerParams(dimension_semantics=("arbitrary",), vmem_limit_bytes=VMEM_LIMIT),
    )(cq, ckv, kr, dq_r, dk_r, dv, dxr, dxg, gqa, gkva, w_uq_p, w_uk_p, w_v, qg, kg, rc, rs1, rs2)


KEY_CHUNK = 4 * LANES


def _key_chunks(t):
    count = max(t // KEY_CHUNK, 1)
    first = t - KEY_CHUNK * (count - 1)
    return [(0, first)] + [(first + KEY_CHUNK * c, KEY_CHUNK) for c in range(count - 1)]


def _softmax_parts(qh, k_ref, sl, tq, t):
    scores = []
    for start, size in _key_chunks(t):
        s = _dot_nt(qh, k_ref[start:start + size, sl]) * (QK_HEAD ** -0.5)
        if start < PAD_ROWS:
            key = lax.broadcasted_iota(jnp.int32, (tq, size), 1) + start
            s = jnp.where(key >= PAD_ROWS, s, -jnp.inf)
        scores.append(s)
    top = functools.reduce(jnp.maximum, [jnp.max(s, axis=-1, keepdims=True) for s in scores])
    es = [jnp.exp(s - top) for s in scores]
    return es, functools.reduce(jnp.add, [jnp.sum(e, axis=-1, keepdims=True) for e in es])


def _attn_specs(t, tq):
    nq = t // tq
    qspec = pl.BlockSpec((tq, 2 * HEAD_PAD), lambda b, hp, i: (b * nq + i, hp))
    kspec = pl.BlockSpec((t, 2 * HEAD_PAD), lambda b, hp, i: (b, hp))
    vspec = pl.BlockSpec((t, 2 * V_HEAD), lambda b, hp, i: (b, hp))
    ospec = pl.BlockSpec((tq, 2 * V_HEAD), lambda b, hp, i: (b * nq + i, hp))
    return nq, qspec, kspec, vspec, ospec


def _attn_fwd(q, k, v, srcs=(), scatter=()):
    n = q.shape[0]
    t = _t_pad()
    tq = t // 2
    nq, qspec, kspec, vspec, ospec = _attn_specs(t, tq)
    nk = len(srcs)
    c_in, c_out, c_shape, c_sems = _exchange_specs(srcs, scatter)

    def body(q_ref, k_ref, v_ref, *rest):
        o_ref = rest[nk]
        finish = _ride(3, *_exchange_fns(rest[:nk], rest[nk + 1:2 * nk + 1], rest[2 * nk + 1:], scatter))
        lane = lax.broadcasted_iota(jnp.int32, (tq, 2 * V_HEAD), 1)
        outs = []
        for j in range(2):
            sl = slice(j * HEAD_PAD, (j + 1) * HEAD_PAD)
            es, l = _softmax_parts(q_ref[:, sl], k_ref, sl, tq, t)
            pv = [_dot(e.astype(BF16), v_ref[start:start + size, :]) for e, (start, size) in zip(es, _key_chunks(t))]
            outs.append(functools.reduce(jnp.add, pv) / l)
        o_ref[...] = jnp.where(lane < V_HEAD, outs[0], outs[1])
        finish()

    res = pl.pallas_call(
        body, name="attn_fwd", grid=(n // t, N_HEADS // 2, nq),
        in_specs=[qspec, kspec, vspec] + c_in, out_specs=[ospec] + c_out,
        out_shape=[jax.ShapeDtypeStruct((n, D_ATTN), F32)] + c_shape, scratch_shapes=c_sems,
        compiler_params=pltpu.CompilerParams(dimension_semantics=("arbitrary", "arbitrary", "arbitrary"),
                                             vmem_limit_bytes=VMEM_LIMIT),
    )(q, k, v, *srcs)
    return res[0], res[1:]


def _attn_bwd(q, k, v, do, o, srcs=(), scatter=()):
    n = q.shape[0]
    t = _t_pad()
    tq = t // 2
    nq, qspec, kspec, vspec, ospec = _attn_specs(t, tq)
    nk = len(srcs)
    c_in, c_out, c_shape, c_sems = _exchange_specs(srcs, scatter)

    def body(q_ref, k_ref, v_ref, do_ref, o_ref, *rest):
        dq_ref, dk_ref, dv_ref = rest[nk:nk + 3]
        finish = _ride(3, *_exchange_fns(rest[:nk], rest[nk + 3:2 * nk + 3], rest[2 * nk + 3:], scatter))

        @pl.when(pl.program_id(2) == 0)
        def _():
            dk_ref[...] = jnp.zeros_like(dk_ref)
            dv_ref[...] = jnp.zeros_like(dv_ref)

        lane = lax.broadcasted_iota(jnp.int32, (tq, 2 * V_HEAD), 1)
        do = do_ref[...]
        do_o = do * o_ref[...]
        chunks = _key_chunks(t)
        dvs = [None] * len(chunks)
        for j in range(2):
            sl = slice(j * HEAD_PAD, (j + 1) * HEAD_PAD)
            qh = q_ref[:, sl]
            es, l = _softmax_parts(qh, k_ref, sl, tq, t)
            inv_l = 1.0 / l
            in_head = (lane < V_HEAD) if j == 0 else (lane >= V_HEAD)
            doh = jnp.where(in_head, do, 0.0).astype(BF16)
            delta = jnp.sum(jnp.where(in_head, do_o, 0.0), axis=-1, keepdims=True)
            dq = jnp.zeros((tq, HEAD_PAD), F32)
            for c, (start, size) in enumerate(chunks):
                rows = slice(start, start + size)
                p = es[c] * inv_l
                dp = _dot_nt(doh, v_ref[rows, :])
                ds = (p * (dp - delta) * (QK_HEAD ** -0.5)).astype(BF16)
                dq = dq + _dot(ds, k_ref[rows, sl])
                dk_ref[rows, sl] += _dot_tn(ds, qh)
                dvc = _dot_tn(p.astype(BF16), doh)
                dvs[c] = dvc if dvs[c] is None else dvs[c] + dvc
            dq_ref[:, sl] = dq
        for (start, size), dvc in zip(chunks, dvs):
            dv_ref[start:start + size, :] += dvc
        finish()

    res = pl.pallas_call(
        body, name="attn_bwd", grid=(n // t, N_HEADS // 2, nq),
        in_specs=[qspec, kspec, vspec, ospec, ospec] + c_in, out_specs=[qspec, kspec, vspec] + c_out,
        out_shape=[jax.ShapeDtypeStruct((n, QP_COLS), F32), jax.ShapeDtypeStruct((n, QP_COLS), F32),
                   jax.ShapeDtypeStruct((n, D_ATTN), F32)] + c_shape, scratch_shapes=c_sems,
        compiler_params=pltpu.CompilerParams(dimension_semantics=("arbitrary", "arbitrary", "arbitrary"),
                                             vmem_limit_bytes=VMEM_LIMIT),
    )(q, k, v, do, o, *srcs)
    return res[:3], res[3:]


SCAN_STEPS = 8


def _scan(chains, t):
    seg = t // 8
    rows = lax.broadcasted_iota(jnp.int32, (8, LANES), 0)

    def step(i, carry):
        carry = list(carry)
        for u in range(SCAN_STEPS):
            j = i * SCAN_STEPS + u
            for n, (a_ref, b_ref, h_ref, p_ref, reverse) in enumerate(chains):
                h, p = carry[n]
                idx = pl.ds(seg - 1 - j if reverse else j, 8, stride=seg)
                a = a_ref[idx, :]
                h = a * h + b_ref[idx, :]
                p = a * p
                h_ref[idx, :] = h
                p_ref[idx, :] = p
                carry[n] = (h, p)
        return tuple(carry)

    init = tuple((jnp.zeros((8, LANES), F32), jnp.ones((8, LANES), F32)) for _ in chains)
    ends = lax.fori_loop(0, seg // SCAN_STEPS, step, init)
    for (_, _, h_ref, p_ref, reverse), (b, a) in zip(chains, ends):
        for d in (1, 2, 4):
            if reverse:
                keep = rows < 8 - d
                a_n, b_n = pltpu.roll(a, 8 - d, 0), pltpu.roll(b, 8 - d, 0)
            else:
                keep = rows >= d
                a_n, b_n = pltpu.roll(a, d, 0), pltpu.roll(b, d, 0)
            b = a * jnp.where(keep, b_n, 0.0) + b
            a = a * jnp.where(keep, a_n, 1.0)
        for s in (range(7) if reverse else range(1, 8)):
            sl = slice(s * seg, (s + 1) * seg)
            carry_in = b[s + 1:s + 2, :] if reverse else b[s - 1:s, :]
            h_ref[sl, :] = h_ref[sl, :] + p_ref[sl, :] * carry_in


def _shift_rows(x, s, rows, t):
    if s == 0:
        return x
    rolled = pltpu.roll(x, s % t, 0)
    return jnp.where(rows >= s, rolled, 0.0) if s > 0 else jnp.where(rows < t + s, rolled, 0.0)


def _neg_expm1(x, exp_x):
    series = -x * (1.0 + x * (0.5 + x * (1.0 / 6 + x * (1.0 / 24))))
    return jnp.where(x > -0.1, series, 1.0 - exp_x)


def _sigmoid(x):
    return 0.5 * jnp.tanh(0.5 * x) + 0.5


def _gelu_parts(x):
    k = math.sqrt(2.0 / math.pi)
    th = jnp.tanh(k * (x + 0.044715 * x * x * x))
    g = 0.5 * x * (1.0 + th)
    dg = 0.5 * (1.0 + th) + 0.5 * x * (1.0 - th * th) * k * (1.0 + 3 * 0.044715 * x * x)
    return g, dg


def _lru_gates(xc, gates, lam_ref, valid, d):
    r = _sigmoid(gates[:, (2 * d) * LANES:(2 * d + 1) * LANES])
    i = _sigmoid(gates[:, (2 * d + 1) * LANES:(2 * d + 2) * LANES])
    neg_lam = -lam_ref[d:d + 1, :]
    sp = jnp.maximum(neg_lam, 0.0) + jnp.log1p(jnp.exp(-jnp.abs(neg_lam)))
    log_a = -LRU_C * r * sp
    a = jnp.exp(log_a)
    m = jnp.maximum(_neg_expm1(2.0 * log_a, a * a), 0.0)
    sq = jnp.sqrt(m)
    b = jnp.where(valid, sq * (i * xc), 0.0)
    return r, i, sp, a, m, sq, b


def _conv(xr, cw_ref, cb_ref, rows, t):
    return (cw_ref[0:1, :] * _shift_rows(xr, 2, rows, t) + cw_ref[1:2, :] * _shift_rows(xr, 1, rows, t)
            + cw_ref[2:3, :] * xr + cw_ref[3:4, :] * _shift_rows(xr, -1, rows, t) + cb_ref[...])


def _rnn_specs(t):
    seq = pl.BlockSpec((t, LANES), lambda cb, b: (b, cb))
    cw = pl.BlockSpec((4, LANES), lambda cb, b: (0, cb))
    vec1 = pl.BlockSpec((1, LANES), lambda cb, b: (0, cb))
    vec2 = pl.BlockSpec((2, LANES), lambda cb, b: (0, cb))
    wblk = pl.BlockSpec((1, LANES, 4 * LANES), lambda cb, b: (cb, 0, 0))
    gbias = pl.BlockSpec((1, 1, 4 * LANES), lambda cb, b: (cb, 0, 0))
    return seq, cw, vec1, vec2, wblk, gbias


def _rnn_fwd(xr, xg, conv_w, conv_b, wblk, gbias, lam):
    n = xr.shape[0]
    t = _t_pad()
    seq, cw, vec1, vec2, wspec, gspec = _rnn_specs(t)

    def body(xr_ref, xg_ref, cw_ref, cb_ref, w_ref, gb_ref, lam_ref, o_ref, a_s, b_s, h_s, p_s):
        rows = lax.broadcasted_iota(jnp.int32, (t, LANES), 0)
        valid = rows >= PAD_ROWS
        xc = _conv(xr_ref[...], cw_ref, cb_ref, rows, t)
        gates = _dot(xc.astype(BF16), w_ref[0]) + gb_ref[0]
        for d in range(2):
            _, _, _, a, _, _, b = _lru_gates(xc, gates, lam_ref, valid, d)
            a_s[d] = a
            b_s[d] = b
        _scan([(a_s.at[d], b_s.at[d], h_s.at[d], p_s.at[d], d == 1) for d in range(2)], t)
        g, _ = _gelu_parts(xg_ref[...])
        o_ref[...] = (h_s[0] + h_s[1]) * g

    return pl.pallas_call(
        body, name="rnn_fwd", grid=(D_RNN // LANES, n // t),
        in_specs=[seq, seq, cw, vec1, wspec, gspec, vec2], out_specs=seq,
        out_shape=jax.ShapeDtypeStruct((n, D_RNN), F32),
        scratch_shapes=[pltpu.VMEM((2, t, LANES), F32)] * 4,
        compiler_params=pltpu.CompilerParams(dimension_semantics=("parallel", "parallel"), vmem_limit_bytes=VMEM_LIMIT),
    )(xr, xg, conv_w, conv_b, wblk, gbias, lam)


def _rnn_bwd(xr, xg, do, conv_w, conv_b, wblk, gbias, lam):
    n = xr.shape[0]
    t = _t_pad()
    seq, cw, vec1, vec2, wspec, gspec = _rnn_specs(t)

    def body(xr_ref, xg_ref, do_ref, cw_ref, cb_ref, w_ref, gb_ref, lam_ref,
             dxr_ref, dxg_ref, dcw_ref, dcb_ref, dw_ref, dgb_ref, dlam_ref,
             a_s, b_s, h_s, l_s, p_s, back_s, r_s, i_s, q_s, dg_s):
        first = pl.program_id(1) == 0
        rows = lax.broadcasted_iota(jnp.int32, (t, LANES), 0)
        valid = rows >= PAD_ROWS
        xr = xr_ref[...]
        xc = _conv(xr, cw_ref, cb_ref, rows, t)
        xcb = xc.astype(BF16)
        gates = _dot(xcb, w_ref[0]) + gb_ref[0]
        sps = []
        for d in range(2):
            r_s[d], i_s[d], sp, a_s[d], _, q_s[d], b_s[d] = _lru_gates(xc, gates, lam_ref, valid, d)
            sps.append(sp)
        _scan([(a_s.at[d], b_s.at[d], h_s.at[d], p_s.at[d], d == 1) for d in range(2)], t)
        g, dg = _gelu_parts(xg_ref[...])
        do = do_ref[...]
        dxg_ref[...] = do * (h_s[0] + h_s[1]) * dg
        b_s[0] = do * g
        for d in range(2):
            back_s[d] = _shift_rows(a_s[d], -1 if d == 0 else 1, rows, t)
        _scan([(back_s.at[d], b_s.at[0], l_s.at[d], p_s.at[d], d == 0) for d in range(2)], t)
        dxc = jnp.zeros((t, LANES), F32)
        dlams = []
        for d in range(2):
            r, i, sp, a, sq = r_s[d], i_s[d], sps[d], a_s[d], q_s[d]
            lam_t = l_s[d]
            da = lam_t * _shift_rows(h_s[d], 1 if d == 0 else -1, rows, t)
            lam_v = jnp.where(valid, lam_t, 0.0)
            dsq = lam_v * (i * xc)
            di = lam_v * sq * xc
            dxc = dxc + lam_v * sq * i
            dm = jnp.where(sq > 0.0, dsq * 0.5 / jnp.where(sq > 0.0, sq, 1.0), 0.0)
            dla = da * a - 2.0 * dm * a * a
            dr = dla * (-LRU_C) * sp
            dsp = _colsum(dla * (-LRU_C) * r)
            dlams.append(dsp * -jax.nn.sigmoid(-lam_ref[d:d + 1, :]))
            dg_s[:, (2 * d) * LANES:(2 * d + 1) * LANES] = (dr * r * (1.0 - r)).astype(BF16)
            dg_s[:, (2 * d + 1) * LANES:(2 * d + 2) * LANES] = (di * i * (1.0 - i)).astype(BF16)
        dgates = dg_s[...]
        dxc = dxc + _dot_nt(dgates, w_ref[0])
        taps = [_shift_rows(dxc, j - 2, rows, t) for j in range(4)]
        dxr_ref[...] = (cw_ref[0:1, :] * taps[0] + cw_ref[1:2, :] * taps[1] + cw_ref[2:3, :] * taps[2]
                        + cw_ref[3:4, :] * taps[3])
        dcw = jnp.concatenate([_colsum(tap * xr) for tap in taps], axis=0)
        _acc(dcw_ref, first, dcw)
        _acc(dcb_ref, first, _colsum(dxc))
        _acc(dw_ref, first, _dot_tn(xcb, dgates)[None])
        _acc(dgb_ref, first, _colsum(dgates.astype(F32))[None])
        _acc(dlam_ref, first, jnp.concatenate(dlams, axis=0))

    return pl.pallas_call(
        body, name="rnn_bwd", grid=(D_RNN // LANES, n // t),
        in_specs=[seq, seq, seq, cw, vec1, wspec, gspec, vec2],
        out_specs=[seq, seq, cw, vec1, wspec, gspec, vec2],
        out_shape=[jax.ShapeDtypeStruct((n, D_RNN), F32), jax.ShapeDtypeStruct((n, D_RNN), F32),
                   jax.ShapeDtypeStruct((4, D_RNN), F32), jax.ShapeDtypeStruct((1, D_RNN), F32),
                   jax.ShapeDtypeStruct((D_RNN // LANES, LANES, 4 * LANES), F32),
                   jax.ShapeDtypeStruct((D_RNN // LANES, 1, 4 * LANES), F32), jax.ShapeDtypeStruct((2, D_RNN), F32)],
        scratch_shapes=[pltpu.VMEM((2, t, LANES), F32)] * 9 + [pltpu.VMEM((t, 4 * LANES), BF16)],
        compiler_params=pltpu.CompilerParams(dimension_semantics=("parallel", "arbitrary"), vmem_limit_bytes=VMEM_LIMIT),
    )(xr, xg, do, conv_w, conv_b, wblk, gbias, lam)


def _post(oa, orn, h0, tgt, ga, gr, g2, w_out, w_gate, w_up, w_down):
    n = oa.shape[0]
    tm = _row_tile(n)
    t = _t_pad()

    def body(oa_ref, or_ref, h0_ref, tgt_ref, ga_ref, gr_ref, g2_ref, wo_ref, wg_ref, wu_ref, wd_ref,
             doa_ref, dor_ref, dh1_ref, mix_ref, h1n_ref, act_ref, dgate_ref, dup_ref, dy_ref,
             loss_ref, dga_ref, dgr_ref, dg2_ref, gate_s, up_s):
        first = pl.program_id(0) == 0
        xa, ra = _rms(oa_ref[...], D_ATTN)
        xr, rr = _rms(or_ref[...], D_RNN)
        mix = jnp.concatenate([(xa * ga_ref[...]).astype(BF16), (xr * gr_ref[...]).astype(BF16)], axis=-1)
        mix_ref[...] = mix.T
        h1 = h0_ref[...] + _dot(mix, wo_ref[...])
        x2, r2 = _rms(h1, D_MODEL)
        h1n = (x2 * g2_ref[...]).astype(BF16)
        h1n_ref[...] = h1n
        y = h1
        for cs in range(0, D_FF, FF_CHUNK):
            sl = slice(cs, cs + FF_CHUNK)
            gate = _dot_nt(h1n, wg_ref[sl, :])
            up = _dot_nt(h1n, wu_ref[sl, :])
            gate_s[:, sl] = gate
            up_s[:, sl] = up
            act = (gate * _sigmoid(gate) * up).astype(BF16)
            act_ref[sl, :] = act.T
            y = y + _dot(act, wd_ref[sl, :])
        row = pl.program_id(0) * tm + lax.broadcasted_iota(jnp.int32, (tm, 1), 0)
        for _ in range(1, n // t):
            row = jnp.where(row >= t, row - t, row)
        err = jnp.where(row >= PAD_ROWS + N_META, y - tgt_ref[...], 0.0)
        _acc(loss_ref, first, jnp.full((1, LANES), 0.5 / D_MODEL, F32) * jnp.sum(err * err))
        dy = err * (1.0 / D_MODEL)
        dyb = dy.astype(BF16)
        dy_ref[...] = dyb
        dh1n = jnp.zeros((tm, D_MODEL), F32)
        for cs in range(0, D_FF, FF_CHUNK):
            sl = slice(cs, cs + FF_CHUNK)
            dact = _dot_nt(dyb, wd_ref[sl, :])
            gate, up = gate_s[:, sl], up_s[:, sl]
            sg = _sigmoid(gate)
            dgate = (dact * up * sg * (1.0 + gate * (1.0 - sg))).astype(BF16)
            dup = (dact * gate * sg).astype(BF16)
            dgate_ref[sl, :] = dgate.T
            dup_ref[sl, :] = dup.T
            dh1n = dh1n + _dot(dgate, wg_ref[sl, :]) + _dot(dup, wu_ref[sl, :])
        _acc(dg2_ref, first, _colsum(dh1n * x2))
        dh1 = dy + _rms_bwd(dh1n, x2, r2, g2_ref[...], D_MODEL)
        dh1_ref[...] = dh1
        dmix = _dot_nt(dh1.astype(BF16), wo_ref[...])
        dma, dmr = dmix[:, :D_ATTN], dmix[:, D_ATTN:]
        _acc(dga_ref, first, _colsum(dma * xa))
        _acc(dgr_ref, first, _colsum(dmr * xr))
        doa_ref[...] = _rms_bwd(dma, xa, ra, ga_ref[...], D_ATTN)
        dor_ref[...] = _rms_bwd(dmr, xr, rr, gr_ref[...], D_RNN)

    def row(w):
        return pl.BlockSpec((tm, w), lambda i: (i, 0))

    def acc(w):
        return pl.BlockSpec((1, w), lambda i: (0, 0))

    def col(w):
        return pl.BlockSpec((w, tm), lambda i: (0, i))

    outs = [(D_ATTN, F32, row), (D_RNN, F32, row), (D_MODEL, F32, row), (D_MODEL, BF16, col), (D_MODEL, BF16, row),
            (D_FF, BF16, col), (D_FF, BF16, col), (D_FF, BF16, col), (D_MODEL, BF16, row)]
    accs = [LANES, D_ATTN, D_RNN, D_MODEL]
    return pl.pallas_call(
        body, name="post", grid=(n // tm,),
        in_specs=[row(D_ATTN), row(D_RNN), row(D_MODEL), row(D_MODEL),
                  _const_spec((1, D_ATTN)), _const_spec((1, D_RNN)), _const_spec((1, D_MODEL)),
                  _const_spec((D_MODEL, D_MODEL)), _const_spec((D_FF, D_MODEL)), _const_spec((D_FF, D_MODEL)),
                  _const_spec((D_FF, D_MODEL))],
        out_specs=[spec(w) for w, _, spec in outs] + [acc(w) for w in accs],
        out_shape=[jax.ShapeDtypeStruct((n, w) if spec is row else (w, n), dt) for w, dt, spec in outs]
        + [jax.ShapeDtypeStruct((1, w), F32) for w in accs],
        scratch_shapes=[pltpu.VMEM((tm, D_FF), F32), pltpu.VMEM((tm, D_FF), F32)],
        compiler_params=pltpu.CompilerParams(dimension_semantics=("arbitrary",), vmem_limit_bytes=VMEM_LIMIT),
    )(oa, orn, h0, tgt, ga, gr, g2, w_out, w_gate, w_up, w_down)


def _in_bwd(dp, h0, dh1, ln1_g, w_in_p, srcs=(), scatter=()):
    n = h0.shape[0]
    tm = _row_tile(n)
    nk = len(srcs)
    c_in, c_out, c_shape, c_sems = _exchange_specs(srcs, scatter)

    def body(dp_ref, h0_ref, dh1_ref, g_ref, w_ref, *rest):
        dh0_ref, dg_ref = rest[nk:nk + 2]
        finish = _ride(1, *_exchange_fns(rest[:nk], rest[nk + 2:2 * nk + 2], rest[2 * nk + 2:], scatter))
        dhn = _dot(dp_ref[...], w_ref[...])
        xhat, r = _rms(h0_ref[...], D_MODEL)
        _acc(dg_ref, pl.program_id(0) == 0, _colsum(dhn * xhat))
        dh0_ref[...] = dh1_ref[...] + _rms_bwd(dhn, xhat, r, g_ref[...], D_MODEL)
        finish()

    def row(w):
        return pl.BlockSpec((tm, w), lambda i: (i, 0))

    res = pl.pallas_call(
        body, name="in_bwd", grid=(n // tm,),
        in_specs=[row(P_COLS), row(D_MODEL), row(D_MODEL), _const_spec((1, D_MODEL)), _const_spec((P_COLS, D_MODEL))] + c_in,
        out_specs=[row(D_MODEL), pl.BlockSpec((1, D_MODEL), lambda i: (0, 0))] + c_out,
        out_shape=[jax.ShapeDtypeStruct((n, D_MODEL), F32), jax.ShapeDtypeStruct((1, D_MODEL), F32)] + c_shape,
        scratch_shapes=c_sems,
        compiler_params=pltpu.CompilerParams(dimension_semantics=("arbitrary",), vmem_limit_bytes=VMEM_LIMIT),
    )(dp, h0, dh1, ln1_g, w_in_p, *srcs)
    return res[:2], res[2:]


def _pick_tile(width, cap):
    best = LANES
    for mult in range(1, width // LANES + 1):
        cand = mult * LANES
        if width % cand == 0 and cand <= cap:
            best = cand
    return best


def _matmul_tn(name, a, b):
    n, ka = a.shape
    kb = b.shape[1]
    ta, tb = _pick_tile(ka, 1408), _pick_tile(kb, 1408)
    tk = n // 4

    def body(a_ref, b_ref, o_ref):
        _acc(o_ref, pl.program_id(2) == 0, _dot_tn(a_ref[...].astype(BF16), b_ref[...].astype(BF16)))

    return pl.pallas_call(
        body, name=name, grid=(ka // ta, kb // tb, n // tk),
        in_specs=[pl.BlockSpec((tk, ta), lambda i, j, k: (k, i)), pl.BlockSpec((tk, tb), lambda i, j, k: (k, j))],
        out_specs=pl.BlockSpec((ta, tb), lambda i, j, k: (i, j)),
        out_shape=jax.ShapeDtypeStruct((ka, kb), F32),
        compiler_params=pltpu.CompilerParams(dimension_semantics=("parallel", "parallel", "arbitrary"),
                                             vmem_limit_bytes=VMEM_LIMIT),
    )(a, b)


def _matmul_shards(name, at, b):
    ka, n = at.shape
    kb = b.shape[1]
    ta, tb = _pick_tile(ka, 1408), _pick_tile(kb, 1408)
    tk = n // 2
    width = ka // N_DEV
    per = ta // width

    def body(a_ref, b_ref, o_ref, acc_ref):
        _acc(acc_ref, pl.program_id(2) == 0, _dot(a_ref[...], b_ref[...].astype(BF16)))

        @pl.when(pl.program_id(2) == pl.num_programs(2) - 1)
        def _():
            for s in range(per):
                o_ref[s] = acc_ref[s * width:(s + 1) * width, :].astype(BF16)

    return pl.pallas_call(
        body, name=name, grid=(ka // ta, kb // tb, n // tk),
        in_specs=[pl.BlockSpec((ta, tk), lambda i, j, k: (i, k)), pl.BlockSpec((tk, tb), lambda i, j, k: (k, j))],
        out_specs=pl.BlockSpec((per, width, tb), lambda i, j, k: (i, 0, j)),
        out_shape=jax.ShapeDtypeStruct((N_DEV, width, kb), BF16),
        scratch_shapes=[pltpu.VMEM((ta, tb), F32)],
        compiler_params=pltpu.CompilerParams(dimension_semantics=("parallel", "parallel", "arbitrary"),
                                             vmem_limit_bytes=VMEM_LIMIT),
    )(at, b)


def _adamw_math(g8_ref, w_ref, m_ref, v_ref, g_ref, d_ref, nm_ref, nv_ref):
    g = g8_ref[0].astype(F32)
    for s in range(1, N_DEV):
        g = g + g8_ref[s].astype(F32)
    g_ref[...] = g
    nm = ADAM_B1 * m_ref[...] + (1.0 - ADAM_B1) * g
    nv = ADAM_B2 * v_ref[...] + (1.0 - ADAM_B2) * (g * g)
    nm_ref[...] = nm
    nv_ref[...] = nv
    m_hat = nm / (1.0 - ADAM_B1 ** ADAM_STEP)
    v_hat = nv / (1.0 - ADAM_B2 ** ADAM_STEP)
    d_ref[...] = -ADAM_LR * (m_hat / (jnp.sqrt(v_hat) + ADAM_EPS) + ADAM_WD * w_ref[...])


def _adamw_many(name, items):
    count = len(items)

    def body(*refs):
        ins, outs = refs[:4 * count], refs[4 * count:]
        for i in range(count):
            _adamw_math(*ins[4 * i:4 * i + 4], *outs[4 * i:4 * i + 4])

    flat = [a for item in items for a in item]
    res = pl.pallas_call(
        body, name=name,
        out_shape=[jax.ShapeDtypeStruct(item[1].shape, F32) for item in items for _ in range(4)],
        compiler_params=pltpu.CompilerParams(vmem_limit_bytes=VMEM_LIMIT),
    )(*flat)
    return [tuple(res[4 * i:4 * i + 4]) for i in range(count)]


def _adamw(name, g8, w, m, v):
    rows, cols = w.shape
    tr = rows
    for cand in (256, 176, 128, 64):
        if rows % cand == 0 and rows > cand:
            tr = cand
            break

    def body(*refs):
        _adamw_math(*refs)

    blk = pl.BlockSpec((tr, cols), lambda i: (i, 0))
    return pl.pallas_call(
        body, name=name, grid=(rows // tr,),
        in_specs=[pl.BlockSpec((N_DEV, tr, cols), lambda i: (0, i, 0)), blk, blk, blk],
        out_specs=[blk] * 4, out_shape=[jax.ShapeDtypeStruct((rows, cols), F32)] * 4,
        compiler_params=pltpu.CompilerParams(dimension_semantics=("parallel",), vmem_limit_bytes=VMEM_LIMIT),
    )(g8, w, m, v)


def _exchange_specs(srcs, scatter):
    nk = len(srcs)
    if not nk:
        return [], [], [], []
    any_spec = pl.BlockSpec(memory_space=pl.ANY)
    out_shape = [jax.ShapeDtypeStruct(s.shape if sc else (N_DEV,) + s.shape, s.dtype) for s, sc in zip(srcs, scatter)]
    sems = [pltpu.SemaphoreType.DMA((nk, N_DEV - 1)), pltpu.SemaphoreType.DMA((nk, N_DEV - 1)),
            pltpu.SemaphoreType.DMA((nk,))]
    return [any_spec] * nk, [any_spec] * nk, out_shape, sems


FLIPS = ((0, 0, 1), (1, 0, 0), (0, 1, 0), (1, 1, 0), (1, 0, 1), (0, 1, 1), (1, 1, 1))
N_CHIP_PEERS = 3


def _exchange_fns(src_refs, out_refs, sems, scatter):
    nk = len(src_refs)
    if not nk:
        return (lambda: None), (lambda: None), (lambda: None)
    send_sems, recv_sems, local_sems = sems
    first = 1 + N_CHIP_PEERS

    def plan():
        x, y, c = lax.axis_index("x"), lax.axis_index("y"), lax.axis_index("c")
        me = 4 * x + 2 * y + c
        peers = [(1 - x if fx else x, 1 - y if fy else y, 1 - c if fc else c) for fx, fy, fc in FLIPS]
        pids = [4 * px + 2 * py + pc for px, py, pc in peers]

        def remote(k, j, src, dst, to):
            return pltpu.make_async_remote_copy(src_ref=src, dst_ref=dst, send_sem=send_sems.at[k, j],
                                                recv_sem=recv_sems.at[k, j], device_id=to, device_id_type=MESH)

        def mine(k, dest):
            return src_refs[k].at[dest] if scatter[k] else src_refs[k]

        local = [pltpu.make_async_copy(mine(k, me), out_refs[k].at[me], local_sems.at[k]) for k in range(nk)]
        direct = [remote(k, j, mine(k, pids[j]), out_refs[k].at[me], peers[j])
                  for k in range(nk) for j in range(len(FLIPS) if scatter[k] else first)]
        relays = {(k, j): remote(k, j, out_refs[k].at[pids[j - N_CHIP_PEERS]], out_refs[k].at[pids[j - N_CHIP_PEERS]], peers[0])
                  for k in range(nk) if not scatter[k] for j in range(first, len(FLIPS))}
        arrivals = {(k, j): remote(k, j, out_refs[k].at[pids[j]], out_refs[k].at[pids[j]], peers[j])
                    for k in range(nk) for j in range(len(FLIPS))}
        return local, direct, relays, arrivals

    def start():
        local, direct, _, _ = plan()
        for cp in local + direct:
            cp.start()

    def relay():
        _, _, relays, arrivals = plan()
        for (k, j), cp in relays.items():
            arrivals[k, j - N_CHIP_PEERS].wait_recv()
            cp.start()

    def wait():
        local, direct, relays, arrivals = plan()
        for (k, j), cp in arrivals.items():
            if (k, j + N_CHIP_PEERS) not in relays:
                cp.wait_recv()
        for cp in direct + list(relays.values()):
            cp.wait_send()
        for cp in local:
            cp.wait()

    return start, relay, wait


def _grid_step(rank):
    step, total = 0, 1
    for axis in range(rank):
        step = step * pl.num_programs(axis) + pl.program_id(axis)
        total = total * pl.num_programs(axis)
    return step, total


def _ride(rank, start, relay, wait):
    step, total = _grid_step(rank)
    pl.when(step == 0)(start)
    pl.when(step == (3 * total) // 4)(relay)
    return lambda: pl.when(step == total - 1)(wait)


def _exchange(name, srcs, scatter):
    nk = len(srcs)
    c_in, c_out, c_shape, c_sems = _exchange_specs(srcs, scatter)

    def body(*refs):
        start, relay, wait = _exchange_fns(refs[:nk], refs[nk:2 * nk], refs[2 * nk:], scatter)
        start()
        relay()
        wait()

    return pl.pallas_call(body, name=name, in_specs=c_in, out_specs=c_out, out_shape=c_shape, scratch_shapes=c_sems)(*srcs)


def _cols_from_shards(g):
    return jnp.transpose(g, (1, 0, 2)).reshape(g.shape[1], -1)


def _cols_to_shards(w):
    return jnp.transpose(w.reshape(w.shape[0], N_DEV, -1), (1, 0, 2))


def _prep(x, tgt, srcs, scatter):
    nb = x.shape[0]
    t = _t_pad()
    head = PAD_ROWS + N_META
    nk = len(srcs)
    c_in, c_out, c_shape, c_sems = _exchange_specs(srcs, scatter)

    def body(x_ref, tgt_ref, *rest):
        h0_ref, tp_ref = rest[nk:nk + 2]
        finish = _ride(2, *_exchange_fns(rest[:nk], rest[nk + 2:2 * nk + 2], rest[2 * nk + 2:], scatter))
        lead = pl.program_id(1) == 0

        @pl.when(lead)
        def _():
            h0_ref[...] = jnp.zeros_like(h0_ref)
            tp_ref[...] = jnp.zeros_like(tp_ref)

        @pl.when(jnp.logical_not(lead))
        def _():
            h0_ref[...] = x_ref[...]
            tp_ref[...] = tgt_ref[...]

        finish()

    src = pl.BlockSpec((1, head, D_MODEL), lambda b, j: (b, jnp.maximum(j - 1, 0), 0))
    dst = pl.BlockSpec((1, head, D_MODEL), lambda b, j: (b, j, 0))
    padded = jax.ShapeDtypeStruct((nb, t, D_MODEL), F32)
    res = pl.pallas_call(
        body, name="prep", grid=(nb, t // head), in_specs=[src, src] + c_in, out_specs=[dst, dst] + c_out,
        out_shape=[padded, padded] + c_shape, scratch_shapes=c_sems,
        compiler_params=pltpu.CompilerParams(dimension_semantics=("arbitrary", "arbitrary")),
    )(x, tgt, *srcs)
    return res[0], res[1], res[2:]


def _rope_tables(n):
    t = _t_pad()
    pos = np.arange(t, dtype=np.float32) - np.float32(PAD_ROWS)
    half = QK_ROPE // 2
    freqs = (1.0 / (ROPE_THETA ** (np.arange(half, dtype=np.float32) / half))).astype(np.float32)
    ang = pos[:, None] * freqs[None, :]
    cos, sin = np.cos(ang), np.sin(ang)
    z = lambda w: np.zeros((t, w), np.float32)
    c = np.concatenate([np.ones((t, QK_NOPE), np.float32), cos, cos, z(HEAD_PAD - QK_HEAD)], axis=1)
    s1 = np.concatenate([z(QK_NOPE + half), sin, z(HEAD_PAD - QK_HEAD)], axis=1)
    s2 = np.concatenate([z(QK_NOPE), -sin, z(HEAD_PAD - QK_NOPE - half)], axis=1)
    return tuple(jnp.asarray(np.tile(a, (n // t, 1))) for a in (c, s1, s2))


def _block_diag_gates(lru_wa, lru_wi):
    eye = jnp.eye(2, dtype=lru_wa.dtype)

    def bd(w):
        w = w.reshape(2, D_RNN // LANES, 2, RNN_BW, RNN_BW)
        full = w[:, :, :, :, None, :] * eye[None, None, :, None, :, None]
        return full.reshape(2, D_RNN // LANES, LANES, LANES)

    a, i = bd(lru_wa), bd(lru_wi)
    return jnp.concatenate([a[0], i[0], a[1], i[1]], axis=-1)


def _unblock_gates(dw):
    nb = D_RNN // LANES
    parts = dw.reshape(nb, 2, RNN_BW, 4, 2, RNN_BW)
    diag = jnp.stack([parts[:, k, :, :, k, :] for k in range(2)], axis=1)
    diag = jnp.transpose(diag, (3, 0, 1, 2, 4)).reshape(4, 2 * nb, RNN_BW, RNN_BW)
    return jnp.stack([diag[0], diag[2]]), jnp.stack([diag[1], diag[3]])


WEIGHTS = ("meta_tokens", "ln1_g", "w_in", "q_a_norm_g", "w_uq", "kv_a_norm_g", "w_ukv", "q_norm_g", "k_norm_g",
           "conv_w", "conv_b", "lru_wa", "lru_ba", "lru_wi", "lru_bi", "lru_lambda", "attn_out_g", "rnn_out_g",
           "w_out", "ln2_g", "w_gate", "w_up", "w_down")
BIG = ("w_in", "w_uq", "w_ukv", "w_out", "w_gate", "w_up", "w_down")
TRANSPOSED = ("w_in", "w_uq", "w_gate", "w_up")
ROW_SHARDED = ("w_out", "w_down") + TRANSPOSED
REPLICATED = ("ln1_g", "q_a_norm_g", "kv_a_norm_g", "q_norm_g", "k_norm_g", "conv_b", "lru_wa", "lru_wi",
              "attn_out_g", "rnn_out_g", "ln2_g")
WHOLE = REPLICATED + ("loss",)
G_FIRST = ("w_in", "meta_tokens")
G_MID = ("w_uq", "w_ukv", "conv_w", "lru_ba", "lru_bi", "lru_lambda")
LATE = ("w_out", "w_gate", "w_up", "w_down")
G_LAST = ("meta_tokens", "ln1_g")


def _local_step(x, tgt, ex):
    nb = x.shape[0]
    t = _t_pad()
    n = nb * t
    local = ex.local
    h0, tgt_p, got = _prep(x, tgt, *ex.gather_srcs(G_FIRST))
    first = ex.gathered(G_FIRST, got)
    meta, w_in = first["meta_tokens"], first["w_in"]
    h0 = h0.at[:, PAD_ROWS:PAD_ROWS + N_META].set(jnp.broadcast_to(meta[None], (nb, N_META, D_MODEL))).reshape(n, D_MODEL)
    tgt_p = tgt_p.reshape(n, D_MODEL)

    zr = lambda r: jnp.zeros((r, D_MODEL), w_in.dtype)
    w_in_p = jnp.concatenate([w_in[:OFF_CKV], w_in[OFF_KR:], zr(QK_NOPE), w_in[OFF_CKV:OFF_KR], zr(HEAD_PAD - QK_HEAD)],
                             axis=0)
    pad_g = lambda g: jnp.pad(g, ((0, 0), (0, HEAD_PAD - QK_HEAD)))
    qg, kg = pad_g(local["q_norm_g"]), pad_g(local["k_norm_g"])
    rc, rs1, rs2 = _rope_tables(n)
    wblk = _block_diag_gates(local["lru_wa"].reshape(2, -1, RNN_BW, RNN_BW),
                             local["lru_wi"].reshape(2, -1, RNN_BW, RNN_BW)).astype(BF16)
    nblk = D_RNN // LANES

    (hn, cq, ckv, xr, xg, kr), got = _in_proj(h0, local["ln1_g"], w_in_p, *ex.gather_srcs(G_MID))
    w = ex.gathered(G_MID, got)
    w_uq_p = jnp.pad(w["w_uq"].reshape(N_HEADS, QK_HEAD, Q_LORA), ((0, 0), (0, HEAD_PAD - QK_HEAD), (0, 0))
                     ).reshape(QP_COLS, Q_LORA)
    ukv = w["w_ukv"].reshape(KV_LORA, N_HEADS, QK_NOPE + V_HEAD)
    w_uk_p = jnp.pad(ukv[:, :, :QK_NOPE], ((0, 0), (0, 0), (0, HEAD_PAD - QK_NOPE))).reshape(KV_LORA, QP_COLS)
    w_v = ukv[:, :, QK_NOPE:].reshape(KV_LORA, D_ATTN)
    gbias = jnp.stack([w["lru_ba"][0], w["lru_bi"][0], w["lru_ba"][1], w["lru_bi"][1]], axis=0)
    gbias = jnp.transpose(gbias.reshape(4, nblk, LANES), (1, 0, 2)).reshape(nblk, 1, 4 * LANES)

    q, k, v = _qkv_fwd(cq, ckv, kr, local["q_a_norm_g"], local["kv_a_norm_g"], w_uq_p, w_uk_p, w_v, qg, kg, rc, rs1, rs2)
    oa, got = _attn_fwd(q, k, v, *ex.gather_srcs(LATE))
    late = ex.gathered(LATE, got)
    orn = _rnn_fwd(xr, xg, w["conv_w"], local["conv_b"], wblk, gbias, w["lru_lambda"])
    (doa, dor, dh1, mix_t, h1n, act_t, dgate_t, dup_t, dyb, loss, dga, dgr, dg2) = _post(
        oa, orn, h0, tgt_p, local["attn_out_g"], local["rnn_out_g"], local["ln2_g"], late["w_out"], late["w_gate"],
        late["w_up"], late["w_down"])
    wire = {"w_out": _matmul_shards("dw_out", mix_t, dh1), "w_gate": _matmul_shards("dw_gate", dgate_t, h1n),
            "w_up": _matmul_shards("dw_up", dup_t, h1n), "w_down": _matmul_shards("dw_down", act_t, dyb)}
    dxr, dxg, dcw, dcb, dwblk, dgb, dlam = _rnn_bwd(xr, xg, dor, w["conv_w"], local["conv_b"], wblk, gbias, w["lru_lambda"])
    dwa, dwi = _unblock_gates(dwblk)
    dgb = jnp.transpose(dgb.reshape(nblk, 4, LANES), (1, 0, 2)).reshape(4, D_RNN)
    wire.update(ex.to_wire({
        "conv_w": dcw, "conv_b": dcb, "lru_wa": dwa.reshape(-1, RNN_BW), "lru_ba": jnp.stack([dgb[0], dgb[2]]),
        "lru_wi": dwi.reshape(-1, RNN_BW), "lru_bi": jnp.stack([dgb[1], dgb[3]]), "lru_lambda": dlam,
        "attn_out_g": dga, "rnn_out_g": dgr, "ln2_g": dg2, "loss": loss}))
    names = tuple(wire)
    (dq_r, dk_r, dv), got = _attn_bwd(q, k, v, doa, oa, *ex.scatter_srcs(names, wire))
    summed = ex.scattered(names, wire, got)
    (dp, qa, kva, dqp, dkv, dqg, dkg, dgqa, dgkva) = _qkv_bwd(
        cq, ckv, kr, dq_r, dk_r, dv, dxr, dxg, local["q_a_norm_g"], local["kv_a_norm_g"], w_uq_p, w_uk_p, w_v, qg, kg,
        rc, rs1, rs2)
    dw_in_p = _matmul_tn("dw_in", dp, hn)
    dw_uq_p = _matmul_tn("dw_uq", dqp, qa)
    dw_kv = _matmul_tn("dw_ukv", kva, dkv)
    kr0 = OFF_CKV + 2 * D_RNN + QK_NOPE
    dw_in = jnp.concatenate([dw_in_p[:OFF_CKV], dw_in_p[kr0:kr0 + QK_ROPE], dw_in_p[OFF_CKV:OFF_CKV + 2 * D_RNN]], axis=0)
    dw_uq = dw_uq_p.reshape(N_HEADS, HEAD_PAD, Q_LORA)[:, :QK_HEAD].reshape(N_HEADS * QK_HEAD, Q_LORA)
    dw_ukv = jnp.concatenate([dw_kv[:, :QP_COLS].reshape(KV_LORA, N_HEADS, HEAD_PAD)[:, :, :QK_NOPE],
                              dw_kv[:, QP_COLS:].reshape(KV_LORA, N_HEADS, V_HEAD)], axis=2).reshape(KV_LORA, -1)
    wire = ex.to_wire({"w_in": dw_in, "q_a_norm_g": dgqa, "w_uq": dw_uq, "kv_a_norm_g": dgkva, "w_ukv": dw_ukv,
                       "q_norm_g": dqg[:, :QK_HEAD], "k_norm_g": dkg[:, :QK_HEAD]})
    names = tuple(wire)
    (dh0, dg1), got = _in_bwd(dp, h0, dh1, local["ln1_g"], w_in_p, *ex.scatter_srcs(names, wire))
    summed.update(ex.scattered(names, wire, got))

    dh0 = dh0.reshape(nb, t, D_MODEL)
    wire = ex.to_wire({"meta_tokens": jnp.sum(dh0[:, PAD_ROWS:PAD_ROWS + N_META], axis=0), "ln1_g": dg1})
    got = ex.run("reduce_last", *ex.scatter_srcs(G_LAST, wire))
    summed.update(ex.scattered(G_LAST, wire, got))
    return dh0[:, PAD_ROWS + N_META:], summed


class _MeshExchange:
    def __init__(self, shards):
        self.local = shards

    @staticmethod
    def run(name, srcs, scatter):
        return _exchange(name, srcs, scatter)

    def gather_srcs(self, names):
        return [self.local[k].astype(BF16) if k in BIG else self.local[k] for k in names], [False] * len(names)

    @staticmethod
    def gathered(names, outs):
        return {k: g.reshape(-1, g.shape[-1]) if k in ROW_SHARDED else _cols_from_shards(g) for k, g in zip(names, outs)}

    @staticmethod
    def to_wire(grads):
        wire = {}
        for k, g in grads.items():
            if k in WHOLE:
                wire[k] = g
            elif k in ROW_SHARDED:
                wire[k] = g.reshape(N_DEV, -1, g.shape[-1]).astype(BF16)
            else:
                wire[k] = _cols_to_shards(g).astype(BF16) if k in BIG else _cols_to_shards(g)
        return wire

    @staticmethod
    def scatter_srcs(names, wire):
        return [wire[k] for k in names], [k not in WHOLE for k in names]

    @staticmethod
    def scattered(names, wire, outs):
        return dict(zip(names, outs))


def kernel(x, meta_tokens, ln1_g, w_in, q_a_norm_g, w_uq, kv_a_norm_g, w_ukv, q_norm_g, k_norm_g, conv_w, conv_b, lru_wa, lru_ba, lru_wi, lru_bi, lru_lambda, attn_out_g, rnn_out_g, w_out, ln2_g, w_gate, w_up, w_down, loss_target, m_meta_tokens, m_ln1_g, m_w_in, m_q_a_norm_g, m_w_uq, m_kv_a_norm_g, m_w_ukv, m_q_norm_g, m_k_norm_g, m_conv_w, m_conv_b, m_lru_wa, m_lru_ba, m_lru_wi, m_lru_bi, m_lru_lambda, m_attn_out_g, m_rnn_out_g, m_w_out, m_ln2_g, m_w_gate, m_w_up, m_w_down, v_meta_tokens, v_ln1_g, v_w_in, v_q_a_norm_g, v_w_uq, v_kv_a_norm_g, v_w_ukv, v_q_norm_g, v_k_norm_g, v_conv_w, v_conv_b, v_lru_wa, v_lru_ba, v_lru_wi, v_lru_bi, v_lru_lambda, v_attn_out_g, v_rnn_out_g, v_w_out, v_ln2_g, v_w_gate, v_w_up, v_w_down):
    given = (meta_tokens, ln1_g, w_in, q_a_norm_g, w_uq, kv_a_norm_g, w_ukv, q_norm_g, k_norm_g, conv_w, conv_b,
             lru_wa, lru_ba, lru_wi, lru_bi, lru_lambda, attn_out_g, rnn_out_g, w_out, ln2_g, w_gate, w_up, w_down)
    moments_m = (m_meta_tokens, m_ln1_g, m_w_in, m_q_a_norm_g, m_w_uq, m_kv_a_norm_g, m_w_ukv, m_q_norm_g, m_k_norm_g,
                 m_conv_w, m_conv_b, m_lru_wa, m_lru_ba, m_lru_wi, m_lru_bi, m_lru_lambda, m_attn_out_g, m_rnn_out_g,
                 m_w_out, m_ln2_g, m_w_gate, m_w_up, m_w_down)
    moments_v = (v_meta_tokens, v_ln1_g, v_w_in, v_q_a_norm_g, v_w_uq, v_kv_a_norm_g, v_w_ukv, v_q_norm_g, v_k_norm_g,
                 v_conv_w, v_conv_b, v_lru_wa, v_lru_ba, v_lru_wi, v_lru_bi, v_lru_lambda, v_attn_out_g, v_rnn_out_g,
                 v_w_out, v_ln2_g, v_w_gate, v_w_up, v_w_down)
    shapes = {k: a.shape for k, a in zip(WEIGHTS, given)}

    def two_d(k, a):
        a = a.reshape(-1, a.shape[-1])
        return a.T if k in TRANSPOSED else a

    w = {k: two_d(k, a) for k, a in zip(WEIGHTS, given)}
    m = {k: two_d(k, a) for k, a in zip(WEIGHTS, moments_m)}
    v = {k: two_d(k, a) for k, a in zip(WEIGHTS, moments_v)}

    grad_x, parts = _local_step(x, loss_target, _MeshExchange(w))

    new = {k: _adamw("adamw_" + k, parts[k], w[k], m[k], v[k]) for k in BIG}
    small = [k for k in WEIGHTS if k not in BIG]
    new.update(zip(small, _adamw_many("adamw_small", [(parts[k], w[k], m[k], v[k]) for k in small])))

    loss = jnp.sum(parts["loss"][:, 0, 0])
    outs = [loss, grad_x]
    for idx in range(4):
        outs += [(new[k][idx].T if k in TRANSPOSED else new[k][idx]).reshape(shapes[k]) for k in WEIGHTS]
    return tuple(outs)
```

```python
import functools
import math

import numpy as np
import jax
import jax.numpy as jnp
from jax import lax
from jax.experimental import pallas as pl
from jax.experimental.pallas import tpu as pltpu

F32 = jnp.float32
BF16 = jnp.bfloat16

D_MODEL = 1024
N_META = 16
SEQ = 2048
N_HEADS = 8
QK_NOPE = 64
QK_ROPE = 32
QK_HEAD = QK_NOPE + QK_ROPE
V_HEAD = 64
D_ATTN = N_HEADS * V_HEAD
Q_LORA = 384
KV_LORA = 256
D_RNN = 512
RNN_BW = 64
D_FF = 2816
EPS = 1e-6
LRU_C = 8.0
ROPE_THETA = 10000.0
OFF_CKV = Q_LORA + KV_LORA
OFF_KR = OFF_CKV + QK_ROPE
IN_COLS = OFF_KR + 2 * D_RNN

ADAM_LR = 0.001
ADAM_B1 = 0.9
ADAM_B2 = 0.999
ADAM_EPS = 1e-08
ADAM_WD = 0.01
ADAM_STEP = 10

N_DEV = 8
LANES = 128
HEAD_PAD = LANES
PAD_ROWS = LANES - N_META
QP_COLS = N_HEADS * HEAD_PAD
P_COLS = OFF_CKV + 2 * D_RNN + LANES
FF_CHUNK = D_FF // 2
VMEM_LIMIT = 56 * 1024 * 1024
MESH = pl.DeviceIdType.MESH


def _t_pad():
    return PAD_ROWS + N_META + SEQ


def _row_tile(n):
    return 256 if n % 256 == 0 else 128


def _const_spec(shape):
    nd = len(shape)
    return pl.BlockSpec(shape, lambda *_: (0,) * nd, pipeline_mode=pl.Buffered(1))


def _rms(x, d):
    r = lax.rsqrt(jnp.sum(x * x, axis=-1, keepdims=True) * (1.0 / d) + EPS)
    return x * r, r


def _rms_bwd(dy, xhat, r, g, d):
    dxh = dy * g
    return r * (dxh - xhat * (jnp.sum(dxh * xhat, axis=-1, keepdims=True) * (1.0 / d)))


def _colsum(x):
    return jnp.sum(x, axis=0, keepdims=True)


def _dot(a, b):
    return jnp.dot(a, b, preferred_element_type=F32)


def _dot_nt(a, b):
    return lax.dot_general(a, b, (((1,), (1,)), ((), ())), preferred_element_type=F32)


def _dot_tn(a, b):
    return lax.dot_general(a, b, (((0,), (0,)), ((), ())), preferred_element_type=F32)


def _rope(x, c, s1, s2):
    return x * c + pltpu.roll(x, 16, 1) * s1 + pltpu.roll(x, HEAD_PAD - 16, 1) * s2


def _rope_bwd(dy, c, s1, s2):
    return dy * c + pltpu.roll(dy * s1, HEAD_PAD - 16, 1) + pltpu.roll(dy * s2, 16, 1)


def _acc(ref, first, val):
    @pl.when(first)
    def _():
        ref[...] = val

    @pl.when(jnp.logical_not(first))
    def _():
        ref[...] += val


def _in_proj(h0, ln1_g, w_in_p, srcs=(), scatter=()):
    n = h0.shape[0]
    tm = _row_tile(n)
    nk = len(srcs)
    c_in, c_out, c_shape, c_sems = _exchange_specs(srcs, scatter)

    def body(h_ref, g_ref, w_ref, *rest):
        hn_ref, cq_ref, ckv_ref, xr_ref, xg_ref, kr_ref = rest[nk:nk + 6]
        finish = _ride(1, *_exchange_fns(rest[:nk], rest[nk + 6:2 * nk + 6], rest[2 * nk + 6:], scatter))
        xhat, _ = _rms(h_ref[...], D_MODEL)
        hn = (xhat * g_ref[...]).astype(BF16)
        hn_ref[...] = hn
        p = _dot_nt(hn, w_ref[...])
        cq_ref[...] = p[:, :Q_LORA]
        ckv_ref[...] = p[:, Q_LORA:OFF_CKV]
        xr_ref[...] = p[:, OFF_CKV:OFF_CKV + D_RNN]
        xg_ref[...] = p[:, OFF_CKV + D_RNN:OFF_CKV + 2 * D_RNN]
        kr_ref[...] = p[:, OFF_CKV + 2 * D_RNN:]
        finish()

    def row(w):
        return pl.BlockSpec((tm, w), lambda i: (i, 0))

    widths = (D_MODEL, Q_LORA, KV_LORA, D_RNN, D_RNN, LANES)
    res = pl.pallas_call(
        body, name="in_proj", grid=(n // tm,),
        in_specs=[row(D_MODEL), _const_spec((1, D_MODEL)), _const_spec((P_COLS, D_MODEL))] + c_in,
        out_specs=[row(w) for w in widths] + c_out,
        out_shape=[jax.ShapeDtypeStruct((n, w), BF16 if k == 0 else F32) for k, w in enumerate(widths)] + c_shape,
        scratch_shapes=c_sems,
        compiler_params=pltpu.CompilerParams(dimension_semantics=("arbitrary",), vmem_limit_bytes=VMEM_LIMIT),
    )(h0, ln1_g, w_in_p, *srcs)
    return res[:6], res[6:]


def _qkv_fwd(cq, ckv, kr, gqa, gkva, w_uq_p, w_uk_p, w_v, qg, kg, rc, rs1, rs2):
    n = cq.shape[0]
    tm = _row_tile(n)

    def body(cq_ref, ckv_ref, kr_ref, gqa_ref, gkva_ref, wuq_ref, wuk_ref, wv_ref, qg_ref, kg_ref,
             c_ref, s1_ref, s2_ref, q_ref, k_ref, v_ref):
        xq, _ = _rms(cq_ref[...], Q_LORA)
        qa = (xq * gqa_ref[...]).astype(BF16)
        q = _dot_nt(qa, wuq_ref[...])
        xkv, _ = _rms(ckv_ref[...], KV_LORA)
        kva = (xkv * gkva_ref[...]).astype(BF16)
        kn = _dot(kva, wuk_ref[...])
        v_ref[...] = _dot(kva, wv_ref[...]).astype(BF16)
        krp = kr_ref[...]
        c, s1, s2 = c_ref[...], s1_ref[...], s2_ref[...]
        for h in range(N_HEADS):
            sl = slice(h * HEAD_PAD, (h + 1) * HEAD_PAD)
            qh, _ = _rms(q[:, sl], QK_HEAD)
            q_ref[:, sl] = _rope(qh * qg_ref[...], c, s1, s2).astype(BF16)
            kh, _ = _rms(kn[:, sl] + krp, QK_HEAD)
            k_ref[:, sl] = _rope(kh * kg_ref[...], c, s1, s2).astype(BF16)

    def row(w):
        return pl.BlockSpec((tm, w), lambda i: (i, 0))

    return pl.pallas_call(
        body, name="qkv_fwd", grid=(n // tm,),
        in_specs=[row(Q_LORA), row(KV_LORA), row(LANES), _const_spec((1, Q_LORA)), _const_spec((1, KV_LORA)),
                  _const_spec((QP_COLS, Q_LORA)), _const_spec((KV_LORA, QP_COLS)), _const_spec((KV_LORA, D_ATTN)),
                  _const_spec((1, LANES)), _const_spec((1, LANES)), row(LANES), row(LANES), row(LANES)],
        out_specs=[row(QP_COLS), row(QP_COLS), row(D_ATTN)],
        out_shape=[jax.ShapeDtypeStruct((n, QP_COLS), BF16), jax.ShapeDtypeStruct((n, QP_COLS), BF16),
                   jax.ShapeDtypeStruct((n, D_ATTN), BF16)],
        compiler_params=pltpu.CompilerParams(dimension_semantics=("parallel",), vmem_limit_bytes=VMEM_LIMIT),
    )(cq, ckv, kr, gqa, gkva, w_uq_p, w_uk_p, w_v, qg, kg, rc, rs1, rs2)


def _qkv_bwd(cq, ckv, kr, dq_r, dk_r, dv, dxr, dxg, gqa, gkva, w_uq_p, w_uk_p, w_v, qg, kg, rc, rs1, rs2):
    n = cq.shape[0]
    tm = _row_tile(n)

    def body(cq_ref, ckv_ref, kr_ref, dq_ref, dk_ref, dv_ref, dxr_ref, dxg_ref, gqa_ref, gkva_ref, wuq_ref, wuk_ref,
             wv_ref, qg_ref, kg_ref, c_ref, s1_ref, s2_ref,
             dp_ref, qa_ref, kva_ref, dqp_ref, dkv_ref, dqg_ref, dkg_ref, dgqa_ref, dgkva_ref):
        first = pl.program_id(0) == 0
        dp_ref[:, OFF_CKV:OFF_CKV + D_RNN] = dxr_ref[...].astype(BF16)
        dp_ref[:, OFF_CKV + D_RNN:OFF_CKV + 2 * D_RNN] = dxg_ref[...].astype(BF16)
        xq, rq = _rms(cq_ref[...], Q_LORA)
        qa = (xq * gqa_ref[...]).astype(BF16)
        qa_ref[...] = qa
        q = _dot_nt(qa, wuq_ref[...])
        xkv, rkv = _rms(ckv_ref[...], KV_LORA)
        kva = (xkv * gkva_ref[...]).astype(BF16)
        kva_ref[...] = kva
        kn = _dot(kva, wuk_ref[...])
        krp = kr_ref[...]
        c, s1, s2 = c_ref[...], s1_ref[...], s2_ref[...]
        lane = lax.broadcasted_iota(jnp.int32, (tm, HEAD_PAD), 1)
        rope_lanes = jnp.logical_and(lane >= QK_NOPE, lane < QK_HEAD)
        dqg = jnp.zeros((1, HEAD_PAD), F32)
        dkg = jnp.zeros((1, HEAD_PAD), F32)
        dkr = jnp.zeros((tm, HEAD_PAD), F32)
        for h in range(N_HEADS):
            sl = slice(h * HEAD_PAD, (h + 1) * HEAD_PAD)
            qh, rqh = _rms(q[:, sl], QK_HEAD)
            dy = _rope_bwd(dq_ref[:, sl], c, s1, s2)
            dqg = dqg + _colsum(dy * qh)
            dqp_ref[:, sl] = _rms_bwd(dy, qh, rqh, qg_ref[...], QK_HEAD).astype(BF16)
            kh, rkh = _rms(kn[:, sl] + krp, QK_HEAD)
            dyk = _rope_bwd(dk_ref[:, sl], c, s1, s2)
            dkg = dkg + _colsum(dyk * kh)
            dkh = _rms_bwd(dyk, kh, rkh, kg_ref[...], QK_HEAD)
            dkv_ref[:, sl] = dkh.astype(BF16)
            dkr = dkr + jnp.where(rope_lanes, dkh, 0.0)
        dkv_ref[:, QP_COLS:] = dv_ref[...].astype(BF16)
        dp_ref[:, OFF_CKV + 2 * D_RNN:] = dkr.astype(BF16)
        dqa = _dot(dqp_ref[...], wuq_ref[...])
        dp_ref[:, :Q_LORA] = _rms_bwd(dqa, xq, rq, gqa_ref[...], Q_LORA).astype(BF16)
        dkva = _dot_nt(dkv_ref[:, :QP_COLS], wuk_ref[...]) + _dot_nt(dkv_ref[:, QP_COLS:], wv_ref[...])
        dp_ref[:, Q_LORA:OFF_CKV] = _rms_bwd(dkva, xkv, rkv, gkva_ref[...], KV_LORA).astype(BF16)
        _acc(dqg_ref, first, dqg)
        _acc(dkg_ref, first, dkg)
        _acc(dgqa_ref, first, _colsum(dqa * xq))
        _acc(dgkva_ref, first, _colsum(dkva * xkv))

    def row(w):
        return pl.BlockSpec((tm, w), lambda i: (i, 0))

    def acc(w):
        return pl.BlockSpec((1, w), lambda i: (0, 0))

    return pl.pallas_call(
        body, name="qkv_bwd", grid=(n // tm,),
        in_specs=[row(Q_LORA), row(KV_LORA), row(LANES), row(QP_COLS), row(QP_COLS), row(D_ATTN), row(D_RNN), row(D_RNN),
                  _const_spec((1, Q_LORA)), _const_spec((1, KV_LORA)),
                  _const_spec((QP_COLS, Q_LORA)), _const_spec((KV_LORA, QP_COLS)), _const_spec((KV_LORA, D_ATTN)),
                  _const_spec((1, LANES)), _const_spec((1, LANES)), row(LANES), row(LANES), row(LANES)],
        out_specs=[row(P_COLS), row(Q_LORA), row(KV_LORA), row(QP_COLS),
                   row(QP_COLS + D_ATTN), acc(LANES), acc(LANES), acc(Q_LORA), acc(KV_LORA)],
        out_shape=[jax.ShapeDtypeStruct((n, P_COLS), BF16), jax.ShapeDtypeStruct((n, Q_LORA), BF16),
                   jax.ShapeDtypeStruct((n, KV_LORA), BF16), jax.ShapeDtypeStruct((n, QP_COLS), BF16),
                   jax.ShapeDtypeStruct((n, QP_COLS + D_ATTN), BF16),
                   jax.ShapeDtypeStruct((1, LANES), F32), jax.ShapeDtypeStruct((1, LANES), F32),
                   jax.ShapeDtypeStruct((1, Q_LORA), F32), jax.ShapeDtypeStruct((1, KV_LORA), F32)],
        compiler_params=pltpu.CompilerParams(dimension_semantics=("arbitrary",), vmem_limit_bytes=VMEM_LIMIT),
    )(cq, ckv, kr, dq_r, dk_r, dv, dxr, dxg, gqa, gkva, w_uq_p, w_uk_p, w_v, qg, kg, rc, rs1, rs2)


KEY_CHUNK = 4 * LANES


def _key_chunks(t):
    count = max(t // KEY_CHUNK, 1)
    first = t - KEY_CHUNK * (count - 1)
    return [(0, first)] + [(first + KEY_CHUNK * c, KEY_CHUNK) for c in range(count - 1)]


def _softmax_parts(qh, k_ref, sl, tq, t):
    scores = []
    for start, size in _key_chunks(t):
        s = _dot_nt(qh, k_ref[start:start + size, sl]) * (QK_HEAD ** -0.5)
        if start < PAD_ROWS:
            key = lax.broadcasted_iota(jnp.int32, (tq, size), 1) + start
            s = jnp.where(key >= PAD_ROWS, s, -jnp.inf)
        scores.append(s)
    top = functools.reduce(jnp.maximum, [jnp.max(s, axis=-1, keepdims=True) for s in scores])
    es = [jnp.exp(s - top) for s in scores]
    return es, functools.reduce(jnp.add, [jnp.sum(e, axis=-1, keepdims=True) for e in es])


def _attn_specs(t, tq):
    nq = t // tq
    qspec = pl.BlockSpec((tq, 2 * HEAD_PAD), lambda b, hp, i: (b * nq + i, hp))
    kspec = pl.BlockSpec((t, 2 * HEAD_PAD), lambda b, hp, i: (b, hp))
    vspec = pl.BlockSpec((t, 2 * V_HEAD), lambda b, hp, i: (b, hp))
    ospec = pl.BlockSpec((tq, 2 * V_HEAD), lambda b, hp, i: (b * nq + i, hp))
    return nq, qspec, kspec, vspec, ospec


def _attn_fwd(q, k, v, srcs=(), scatter=()):
    n = q.shape[0]
    t = _t_pad()
    tq = t // 2
    nq, qspec, kspec, vspec, ospec = _attn_specs(t, tq)
    nk = len(srcs)
    c_in, c_out, c_shape, c_sems = _exchange_specs(srcs, scatter)

    def body(q_ref, k_ref, v_ref, *rest):
        o_ref = rest[nk]
        finish = _ride(3, *_exchange_fns(rest[:nk], rest[nk + 1:2 * nk + 1], rest[2 * nk + 1:], scatter))
        lane = lax.broadcasted_iota(jnp.int32, (tq, 2 * V_HEAD), 1)
        outs = []
        for j in range(2):
            sl = slice(j * HEAD_PAD, (j + 1) * HEAD_PAD)
            es, l = _softmax_parts(q_ref[:, sl], k_ref, sl, tq, t)
            pv = [_dot(e.astype(BF16), v_ref[start:start + size, :]) for e, (start, size) in zip(es, _key_chunks(t))]
            outs.append(functools.reduce(jnp.add, pv) / l)
        o_ref[...] = jnp.where(lane < V_HEAD, outs[0], outs[1])
        finish()

    res = pl.pallas_call(
        body, name="attn_fwd", grid=(n // t, N_HEADS // 2, nq),
        in_specs=[qspec, kspec, vspec] + c_in, out_specs=[ospec] + c_out,
        out_shape=[jax.ShapeDtypeStruct((n, D_ATTN), F32)] + c_shape, scratch_shapes=c_sems,
        compiler_params=pltpu.CompilerParams(dimension_semantics=("arbitrary", "arbitrary", "arbitrary"),
                                             vmem_limit_bytes=VMEM_LIMIT),
    )(q, k, v, *srcs)
    return res[0], res[1:]


def _attn_bwd(q, k, v, do, o, srcs=(), scatter=()):
    n = q.shape[0]
    t = _t_pad()
    tq = t // 2
    nq, qspec, kspec, vspec, ospec = _attn_specs(t, tq)
    nk = len(srcs)
    c_in, c_out, c_shape, c_sems = _exchange_specs(srcs, scatter)

    def body(q_ref, k_ref, v_ref, do_ref, o_ref, *rest):
        dq_ref, dk_ref, dv_ref = rest[nk:nk + 3]
        finish = _ride(3, *_exchange_fns(rest[:nk], rest[nk + 3:2 * nk + 3], rest[2 * nk + 3:], scatter))

        @pl.when(pl.program_id(2) == 0)
        def _():
            dk_ref[...] = jnp.zeros_like(dk_ref)
            dv_ref[...] = jnp.zeros_like(dv_ref)

        lane = lax.broadcasted_iota(jnp.int32, (tq, 2 * V_HEAD), 1)
        do = do_ref[...]
        do_o = do * o_ref[...]
        chunks = _key_chunks(t)
        dvs = [None] * len(chunks)
        for j in range(2):
            sl = slice(j * HEAD_PAD, (j + 1) * HEAD_PAD)
            qh = q_ref[:, sl]
            es, l = _softmax_parts(qh, k_ref, sl, tq, t)
            inv_l = 1.0 / l
            in_head = (lane < V_HEAD) if j == 0 else (lane >= V_HEAD)
            doh = jnp.where(in_head, do, 0.0).astype(BF16)
            delta = jnp.sum(jnp.where(in_head, do_o, 0.0), axis=-1, keepdims=True)
            dq = jnp.zeros((tq, HEAD_PAD), F32)
            for c, (start, size) in enumerate(chunks):
                rows = slice(start, start + size)
                p = es[c] * inv_l
                dp = _dot_nt(doh, v_ref[rows, :])
                ds = (p * (dp - delta) * (QK_HEAD ** -0.5)).astype(BF16)
                dq = dq + _dot(ds, k_ref[rows, sl])
                dk_ref[rows, sl] += _dot_tn(ds, qh)
                dvc = _dot_tn(p.astype(BF16), doh)
                dvs[c] = dvc if dvs[c] is None else dvs[c] + dvc
            dq_ref[:, sl] = dq
        for (start, size), dvc in zip(chunks, dvs):
            dv_ref[start:start + size, :] += dvc
        finish()

    res = pl.pallas_call(
        body, name="attn_bwd", grid=(n // t, N_HEADS // 2, nq),
        in_specs=[qspec, kspec, vspec, ospec, ospec] + c_in, out_specs=[qspec, kspec, vspec] + c_out,
        out_shape=[jax.ShapeDtypeStruct((n, QP_COLS), F32), jax.ShapeDtypeStruct((n, QP_COLS), F32),
                   jax.ShapeDtypeStruct((n, D_ATTN), F32)] + c_shape, scratch_shapes=c_sems,
        compiler_params=pltpu.CompilerParams(dimension_semantics=("arbitrary", "arbitrary", "arbitrary"),
                                             vmem_limit_bytes=VMEM_LIMIT),
    )(q, k, v, do, o, *srcs)
    return res[:3], res[3:]


SCAN_STEPS = 8


def _scan(chains, t):
    seg = t // 8
    rows = lax.broadcasted_iota(jnp.int32, (8, LANES), 0)

    def step(i, carry):
        carry = list(carry)
        for u in range(SCAN_STEPS):
            j = i * SCAN_STEPS + u
            for n, (a_ref, b_ref, h_ref, p_ref, reverse) in enumerate(chains):
                h, p = carry[n]
                idx = pl.ds(seg - 1 - j if reverse else j, 8, stride=seg)
                a = a_ref[idx, :]
                h = a * h + b_ref[idx, :]
                p = a * p
                h_ref[idx, :] = h
                p_ref[idx, :] = p
                carry[n] = (h, p)
        return tuple(carry)

    init = tuple((jnp.zeros((8, LANES), F32), jnp.ones((8, LANES), F32)) for _ in chains)
    ends = lax.fori_loop(0, seg // SCAN_STEPS, step, init)
    for (_, _, h_ref, p_ref, reverse), (b, a) in zip(chains, ends):
        for d in (1, 2, 4):
            if reverse:
                keep = rows < 8 - d
                a_n, b_n = pltpu.roll(a, 8 - d, 0), pltpu.roll(b, 8 - d, 0)
            else:
                keep = rows >= d
                a_n, b_n = pltpu.roll(a, d, 0), pltpu.roll(b, d, 0)
            b = a * jnp.where(keep, b_n, 0.0) + b
            a = a * jnp.where(keep, a_n, 1.0)
        for s in (range(7) if reverse else range(1, 8)):
            sl = slice(s * seg, (s + 1) * seg)
            carry_in = b[s + 1:s + 2, :] if reverse else b[s - 1:s, :]
            h_ref[sl, :] = h_ref[sl, :] + p_ref[sl, :] * carry_in


def _shift_rows(x, s, rows, t):
    if s == 0:
        return x
    rolled = pltpu.roll(x, s % t, 0)
    return jnp.where(rows >= s, rolled, 0.0) if s > 0 else jnp.where(rows < t + s, rolled, 0.0)


def _neg_expm1(x, exp_x):
    series = -x * (1.0 + x * (0.5 + x * (1.0 / 6 + x * (1.0 / 24))))
    return jnp.where(x > -0.1, series, 1.0 - exp_x)


def _sigmoid(x):
    return 0.5 * jnp.tanh(0.5 * x) + 0.5


def _gelu_parts(x):
    k = math.sqrt(2.0 / math.pi)
    th = jnp.tanh(k * (x + 0.044715 * x * x * x))
    g = 0.5 * x * (1.0 + th)
    dg = 0.5 * (1.0 + th) + 0.5 * x * (1.0 - th * th) * k * (1.0 + 3 * 0.044715 * x * x)
    return g, dg


def _lru_gates(xc, gates, lam_ref, valid, d):
    r = _sigmoid(gates[:, (2 * d) * LANES:(2 * d + 1) * LANES])
    i = _sigmoid(gates[:, (2 * d + 1) * LANES:(2 * d + 2) * LANES])
    neg_lam = -lam_ref[d:d + 1, :]
    sp = jnp.maximum(neg_lam, 0.0) + jnp.log1p(jnp.exp(-jnp.abs(neg_lam)))
    log_a = -LRU_C * r * sp
    a = jnp.exp(log_a)
    m = jnp.maximum(_neg_expm1(2.0 * log_a, a * a), 0.0)
    sq = jnp.sqrt(m)
    b = jnp.where(valid, sq * (i * xc), 0.0)
    return r, i, sp, a, m, sq, b


def _conv(xr, cw_ref, cb_ref, rows, t):
    return (cw_ref[0:1, :] * _shift_rows(xr, 2, rows, t) + cw_ref[1:2, :] * _shift_rows(xr, 1, rows, t)
            + cw_ref[2:3, :] * xr + cw_ref[3:4, :] * _shift_rows(xr, -1, rows, t) + cb_ref[...])


def _rnn_specs(t):
    seq = pl.BlockSpec((t, LANES), lambda cb, b: (b, cb))
    cw = pl.BlockSpec((4, LANES), lambda cb, b: (0, cb))
    vec1 = pl.BlockSpec((1, LANES), lambda cb, b: (0, cb))
    vec2 = pl.BlockSpec((2, LANES), lambda cb, b: (0, cb))
    wblk = pl.BlockSpec((1, LANES, 4 * LANES), lambda cb, b: (cb, 0, 0))
    gbias = pl.BlockSpec((1, 1, 4 * LANES), lambda cb, b: (cb, 0, 0))
    return seq, cw, vec1, vec2, wblk, gbias


def _rnn_fwd(xr, xg, conv_w, conv_b, wblk, gbias, lam):
    n = xr.shape[0]
    t = _t_pad()
    seq, cw, vec1, vec2, wspec, gspec = _rnn_specs(t)

    def body(xr_ref, xg_ref, cw_ref, cb_ref, w_ref, gb_ref, lam_ref, o_ref, a_s, b_s, h_s, p_s):
        rows = lax.broadcasted_iota(jnp.int32, (t, LANES), 0)
        valid = rows >= PAD_ROWS
        xc = _conv(xr_ref[...], cw_ref, cb_ref, rows, t)
        gates = _dot(xc.astype(BF16), w_ref[0]) + gb_ref[0]
        for d in range(2):
            _, _, _, a, _, _, b = _lru_gates(xc, gates, lam_ref, valid, d)
            a_s[d] = a
            b_s[d] = b
        _scan([(a_s.at[d], b_s.at[d], h_s.at[d], p_s.at[d], d == 1) for d in range(2)], t)
        g, _ = _gelu_parts(xg_ref[...])
        o_ref[...] = (h_s[0] + h_s[1]) * g

    return pl.pallas_call(
        body, name="rnn_fwd", grid=(D_RNN // LANES, n // t),
        in_specs=[seq, seq, cw, vec1, wspec, gspec, vec2], out_specs=seq,
        out_shape=jax.ShapeDtypeStruct((n, D_RNN), F32),
        scratch_shapes=[pltpu.VMEM((2, t, LANES), F32)] * 4,
        compiler_params=pltpu.CompilerParams(dimension_semantics=("parallel", "parallel"), vmem_limit_bytes=VMEM_LIMIT),
    )(xr, xg, conv_w, conv_b, wblk, gbias, lam)


def _rnn_bwd(xr, xg, do, conv_w, conv_b, wblk, gbias, lam):
    n = xr.shape[0]
    t = _t_pad()
    seq, cw, vec1, vec2, wspec, gspec = _rnn_specs(t)

    def body(xr_ref, xg_ref, do_ref, cw_ref, cb_ref, w_ref, gb_ref, lam_ref,
             dxr_ref, dxg_ref, dcw_ref, dcb_ref, dw_ref, dgb_ref, dlam_ref,
             a_s, b_s, h_s, l_s, p_s, back_s, r_s, i_s, q_s, dg_s):
        first = pl.program_id(1) == 0
        rows = lax.broadcasted_iota(jnp.int32, (t, LANES), 0)
        valid = rows >= PAD_ROWS
        xr = xr_ref[...]
        xc = _conv(xr, cw_ref, cb_ref, rows, t)
        xcb = xc.astype(BF16)
        gates = _dot(xcb, w_ref[0]) + gb_ref[0]
        sps = []
        for d in range(2):
            r_s[d], i_s[d], sp, a_s[d], _, q_s[d], b_s[d] = _lru_gates(xc, gates, lam_ref, valid, d)
            sps.append(sp)
        _scan([(a_s.at[d], b_s.at[d], h_s.at[d], p_s.at[d], d == 1) for d in range(2)], t)
        g, dg = _gelu_parts(xg_ref[...])
        do = do_ref[...]
        dxg_ref[...] = do * (h_s[0] + h_s[1]) * dg
        b_s[0] = do * g
        for d in range(2):
            back_s[d] = _shift_rows(a_s[d], -1 if d == 0 else 1, rows, t)
        _scan([(back_s.at[d], b_s.at[0], l_s.at[d], p_s.at[d], d == 0) for d in range(2)], t)
        dxc = jnp.zeros((t, LANES), F32)
        dlams = []
        for d in range(2):
            r, i, sp, a, sq = r_s[d], i_s[d], sps[d], a_s[d], q_s[d]
            lam_t = l_s[d]
            da = lam_t * _shift_rows(h_s[d], 1 if d == 0 else -1, rows, t)
            lam_v = jnp.where(valid, lam_t, 0.0)
            dsq = lam_v * (i * xc)
            di = lam_v * sq * xc
            dxc = dxc + lam_v * sq * i
            dm = jnp.where(sq > 0.0, dsq * 0.5 / jnp.where(sq > 0.0, sq, 1.0), 0.0)
            dla = da * a - 2.0 * dm * a * a
            dr = dla * (-LRU_C) * sp
            dsp = _colsum(dla * (-LRU_C) * r)
            dlams.append(dsp * -jax.nn.sigmoid(-lam_ref[d:d + 1, :]))
            dg_s[:, (2 * d) * LANES:(2 * d + 1) * LANES] = (dr * r * (1.0 - r)).astype(BF16)
            dg_s[:, (2 * d + 1) * LANES:(2 * d + 2) * LANES] = (di * i * (1.0 - i)).astype(BF16)
        dgates = dg_s[...]
        dxc = dxc + _dot_nt(dgates, w_ref[0])
        taps = [_shift_rows(dxc, j - 2, rows, t) for j in range(4)]
        dxr_ref[...] = (cw_ref[0:1, :] * taps[0] + cw_ref[1:2, :] * taps[1] + cw_ref[2:3, :] * taps[2]
                        + cw_ref[3:4, :] * taps[3])
        dcw = jnp.concatenate([_colsum(tap * xr) for tap in taps], axis=0)
        _acc(dcw_ref, first, dcw)
        _acc(dcb_ref, first, _colsum(dxc))
        _acc(dw_ref, first, _dot_tn(xcb, dgates)[None])
        _acc(dgb_ref, first, _colsum(dgates.astype(F32))[None])
        _acc(dlam_ref, first, jnp.concatenate(dlams, axis=0))

    return pl.pallas_call(
        body, name="rnn_bwd", grid=(D_RNN // LANES, n // t),
        in_specs=[seq, seq, seq, cw, vec1, wspec, gspec, vec2],
        out_specs=[seq, seq, cw, vec1, wspec, gspec, vec2],
        out_shape=[jax.ShapeDtypeStruct((n, D_RNN), F32), jax.ShapeDtypeStruct((n, D_RNN), F32),
                   jax.ShapeDtypeStruct((4, D_RNN), F32), jax.ShapeDtypeStruct((1, D_RNN), F32),
                   jax.ShapeDtypeStruct((D_RNN // LANES, LANES, 4 * LANES), F32),
                   jax.ShapeDtypeStruct((D_RNN // LANES, 1, 4 * LANES), F32), jax.ShapeDtypeStruct((2, D_RNN), F32)],
        scratch_shapes=[pltpu.VMEM((2, t, LANES), F32)] * 9 + [pltpu.VMEM((t, 4 * LANES), BF16)],
        compiler_params=pltpu.CompilerParams(dimension_semantics=("parallel", "arbitrary"), vmem_limit_bytes=VMEM_LIMIT),
    )(xr, xg, do, conv_w, conv_b, wblk, gbias, lam)


def _post(oa, orn, h0, tgt, ga, gr, g2, w_out, w_gate, w_up, w_down):
    n = oa.shape[0]
    tm = _row_tile(n)
    t = _t_pad()

    def body(oa_ref, or_ref, h0_ref, tgt_ref, ga_ref, gr_ref, g2_ref, wo_ref, wg_ref, wu_ref, wd_ref,
             doa_ref, dor_ref, dh1_ref, mix_ref, h1n_ref, act_ref, dgate_ref, dup_ref, dy_ref,
             loss_ref, dga_ref, dgr_ref, dg2_ref, gate_s, up_s):
        first = pl.program_id(0) == 0
        xa, ra = _rms(oa_ref[...], D_ATTN)
        xr, rr = _rms(or_ref[...], D_RNN)
        mix = jnp.concatenate([(xa * ga_ref[...]).astype(BF16), (xr * gr_ref[...]).astype(BF16)], axis=-1)
        mix_ref[...] = mix.T
        h1 = h0_ref[...] + _dot(mix, wo_ref[...])
        x2, r2 = _rms(h1, D_MODEL)
        h1n = (x2 * g2_ref[...]).astype(BF16)
        h1n_ref[...] = h1n
        y = h1
        for cs in range(0, D_FF, FF_CHUNK):
            sl = slice(cs, cs + FF_CHUNK)
            gate = _dot_nt(h1n, wg_ref[sl, :])
            up = _dot_nt(h1n, wu_ref[sl, :])
            gate_s[:, sl] = gate
            up_s[:, sl] = up
            act = (gate * _sigmoid(gate) * up).astype(BF16)
            act_ref[sl, :] = act.T
            y = y + _dot(act, wd_ref[sl, :])
        row = pl.program_id(0) * tm + lax.broadcasted_iota(jnp.int32, (tm, 1), 0)
        for _ in range(1, n // t):
            row = jnp.where(row >= t, row - t, row)
        err = jnp.where(row >= PAD_ROWS + N_META, y - tgt_ref[...], 0.0)
        _acc(loss_ref, first, jnp.full((1, LANES), 0.5 / D_MODEL, F32) * jnp.sum(err * err))
        dy = err * (1.0 / D_MODEL)
        dyb = dy.astype(BF16)
        dy_ref[...] = dyb
        dh1n = jnp.zeros((tm, D_MODEL), F32)
        for cs in range(0, D_FF, FF_CHUNK):
            sl = slice(cs, cs + FF_CHUNK)
            dact = _dot_nt(dyb, wd_ref[sl, :])
            gate, up = gate_s[:, sl], up_s[:, sl]
            sg = _sigmoid(gate)
            dgate = (dact * up * sg * (1.0 + gate * (1.0 - sg))).astype(BF16)
            dup = (dact * gate * sg).astype(BF16)
            dgate_ref[sl, :] = dgate.T
            dup_ref[sl, :] = dup.T
            dh1n = dh1n + _dot(dgate, wg_ref[sl, :]) + _dot(dup, wu_ref[sl, :])
        _acc(dg2_ref, first, _colsum(dh1n * x2))
        dh1 = dy + _rms_bwd(dh1n, x2, r2, g2_ref[...], D_MODEL)
        dh1_ref[...] = dh1
        dmix = _dot_nt(dh1.astype(BF16), wo_ref[...])
        dma, dmr = dmix[:, :D_ATTN], dmix[:, D_ATTN:]
        _acc(dga_ref, first, _colsum(dma * xa))
        _acc(dgr_ref, first, _colsum(dmr * xr))
        doa_ref[...] = _rms_bwd(dma, xa, ra, ga_ref[...], D_ATTN)
        dor_ref[...] = _rms_bwd(dmr, xr, rr, gr_ref[...], D_RNN)

    def row(w):
        return pl.BlockSpec((tm, w), lambda i: (i, 0))

    def acc(w):
        return pl.BlockSpec((1, w), lambda i: (0, 0))

    def col(w):
        return pl.BlockSpec((w, tm), lambda i: (0, i))

    outs = [(D_ATTN, F32, row), (D_RNN, F32, row), (D_MODEL, F32, row), (D_MODEL, BF16, col), (D_MODEL, BF16, row),
            (D_FF, BF16, col), (D_FF, BF16, col), (D_FF, BF16, col), (D_MODEL, BF16, row)]
    accs = [LANES, D_ATTN, D_RNN, D_MODEL]
    return pl.pallas_call(
        body, name="post", grid=(n // tm,),
        in_specs=[row(D_ATTN), row(D_RNN), row(D_MODEL), row(D_MODEL),
                  _const_spec((1, D_ATTN)), _const_spec((1, D_RNN)), _const_spec((1, D_MODEL)),
                  _const_spec((D_MODEL, D_MODEL)), _const_spec((D_FF, D_MODEL)), _const_spec((D_FF, D_MODEL)),
                  _const_spec((D_FF, D_MODEL))],
        out_specs=[spec(w) for w, _, spec in outs] + [acc(w) for w in accs],
        out_shape=[jax.ShapeDtypeStruct((n, w) if spec is row else (w, n), dt) for w, dt, spec in outs]
        + [jax.ShapeDtypeStruct((1, w), F32) for w in accs],
        scratch_shapes=[pltpu.VMEM((tm, D_FF), F32), pltpu.VMEM((tm, D_FF), F32)],
        compiler_params=pltpu.CompilerParams(dimension_semantics=("arbitrary",), vmem_limit_bytes=VMEM_LIMIT),
    )(oa, orn, h0, tgt, ga, gr, g2, w_out, w_gate, w_up, w_down)


def _in_bwd(dp, h0, dh1, ln1_g, w_in_p, srcs=(), scatter=()):
    n = h0.shape[0]
    t = _t_pad()
    head = PAD_ROWS + N_META
    per = t // head
    nk = len(srcs)
    c_in, c_out, c_shape, c_sems = _exchange_specs(srcs, scatter)

    def body(dp_ref, h0_ref, dh1_ref, g_ref, w_ref, *rest):
        dx_ref, dlead_ref, dg_ref = rest[nk:nk + 3]
        finish = _ride(2, *_exchange_fns(rest[:nk], rest[nk + 3:2 * nk + 3], rest[2 * nk + 3:], scatter))
        lead = pl.program_id(1) == 0
        dhn = _dot(dp_ref[...], w_ref[...])
        xhat, r = _rms(h0_ref[...], D_MODEL)
        _acc(dg_ref, jnp.logical_and(pl.program_id(0) == 0, lead), _colsum(dhn * xhat))
        dh0 = dh1_ref[...] + _rms_bwd(dhn, xhat, r, g_ref[...], D_MODEL)

        @pl.when(lead)
        def _():
            dlead_ref[0] = dh0

        @pl.when(jnp.logical_not(lead))
        def _():
            dx_ref[0] = dh0

        finish()

    def row(w):
        return pl.BlockSpec((head, w), lambda b, j: (b * per + j, 0))

    res = pl.pallas_call(
        body, name="in_bwd", grid=(n // t, per),
        in_specs=[row(P_COLS), row(D_MODEL), row(D_MODEL), _const_spec((1, D_MODEL)), _const_spec((P_COLS, D_MODEL))] + c_in,
        out_specs=[pl.BlockSpec((1, head, D_MODEL), lambda b, j: (b, jnp.maximum(j - 1, 0), 0)),
                   pl.BlockSpec((1, head, D_MODEL), lambda b, j: (b, 0, 0)),
                   pl.BlockSpec((1, D_MODEL), lambda b, j: (0, 0))] + c_out,
        out_shape=[jax.ShapeDtypeStruct((n // t, t - head, D_MODEL), F32), jax.ShapeDtypeStruct((n // t, head, D_MODEL), F32),
                   jax.ShapeDtypeStruct((1, D_MODEL), F32)] + c_shape,
        scratch_shapes=c_sems,
        compiler_params=pltpu.CompilerParams(dimension_semantics=("arbitrary", "arbitrary"), vmem_limit_bytes=VMEM_LIMIT),
    )(dp, h0, dh1, ln1_g, w_in_p, *srcs)
    return res[:3], res[3:]


def _pick_tile(width, cap):
    best = LANES
    for mult in range(1, width // LANES + 1):
        cand = mult * LANES
        if width % cand == 0 and cand <= cap:
            best = cand
    return best


def _matmul_tn(name, a, b):
    n, ka = a.shape
    kb = b.shape[1]
    ta, tb = _pick_tile(ka, 1408), _pick_tile(kb, 1408)
    tk = n // 4

    def body(a_ref, b_ref, o_ref):
        _acc(o_ref, pl.program_id(2) == 0, _dot_tn(a_ref[...].astype(BF16), b_ref[...].astype(BF16)))

    return pl.pallas_call(
        body, name=name, grid=(ka // ta, kb // tb, n // tk),
        in_specs=[pl.BlockSpec((tk, ta), lambda i, j, k: (k, i)), pl.BlockSpec((tk, tb), lambda i, j, k: (k, j))],
        out_specs=pl.BlockSpec((ta, tb), lambda i, j, k: (i, j)),
        out_shape=jax.ShapeDtypeStruct((ka, kb), F32),
        compiler_params=pltpu.CompilerParams(dimension_semantics=("parallel", "parallel", "arbitrary"),
                                             vmem_limit_bytes=VMEM_LIMIT),
    )(a, b)


def _matmul_shards(name, at, b):
    ka, n = at.shape
    kb = b.shape[1]
    ta, tb = _pick_tile(ka, 1408), _pick_tile(kb, 1408)
    tk = n // 2
    width = ka // N_DEV
    per = ta // width

    def body(a_ref, b_ref, o_ref, acc_ref):
        _acc(acc_ref, pl.program_id(2) == 0, _dot(a_ref[...], b_ref[...].astype(BF16)))

        @pl.when(pl.program_id(2) == pl.num_programs(2) - 1)
        def _():
            for s in range(per):
                o_ref[s] = acc_ref[s * width:(s + 1) * width, :].astype(BF16)

    return pl.pallas_call(
        body, name=name, grid=(ka // ta, kb // tb, n // tk),
        in_specs=[pl.BlockSpec((ta, tk), lambda i, j, k: (i, k)), pl.BlockSpec((tk, tb), lambda i, j, k: (k, j))],
        out_specs=pl.BlockSpec((per, width, tb), lambda i, j, k: (i, 0, j)),
        out_shape=jax.ShapeDtypeStruct((N_DEV, width, kb), BF16),
        scratch_shapes=[pltpu.VMEM((ta, tb), F32)],
        compiler_params=pltpu.CompilerParams(dimension_semantics=("parallel", "parallel", "arbitrary"),
                                             vmem_limit_bytes=VMEM_LIMIT),
    )(at, b)


def _adamw_math(g8_ref, w_ref, m_ref, v_ref, g_ref, d_ref, nm_ref, nv_ref):
    g = g8_ref[0].astype(F32)
    for s in range(1, N_DEV):
        g = g + g8_ref[s].astype(F32)
    g_ref[...] = g
    nm = ADAM_B1 * m_ref[...] + (1.0 - ADAM_B1) * g
    nv = ADAM_B2 * v_ref[...] + (1.0 - ADAM_B2) * (g * g)
    nm_ref[...] = nm
    nv_ref[...] = nv
    m_hat = nm / (1.0 - ADAM_B1 ** ADAM_STEP)
    v_hat = nv / (1.0 - ADAM_B2 ** ADAM_STEP)
    d_ref[...] = -ADAM_LR * (m_hat / (jnp.sqrt(v_hat) + ADAM_EPS) + ADAM_WD * w_ref[...])


def _adamw_many(name, items):
    count = len(items)

    def body(*refs):
        ins, outs = refs[:4 * count], refs[4 * count:]
        for i in range(count):
            _adamw_math(*ins[4 * i:4 * i + 4], *outs[4 * i:4 * i + 4])

    flat = [a for item in items for a in item]
    res = pl.pallas_call(
        body, name=name,
        out_shape=[jax.ShapeDtypeStruct(item[1].shape, F32) for item in items for _ in range(4)],
        compiler_params=pltpu.CompilerParams(vmem_limit_bytes=VMEM_LIMIT),
    )(*flat)
    return [tuple(res[4 * i:4 * i + 4]) for i in range(count)]


def _adamw(name, g8, w, m, v):
    rows, cols = w.shape
    tr = rows
    for cand in (256, 176, 128, 64):
        if rows % cand == 0 and rows > cand:
            tr = cand
            break

    def body(*refs):
        _adamw_math(*refs)

    blk = pl.BlockSpec((tr, cols), lambda i: (i, 0))
    return pl.pallas_call(
        body, name=name, grid=(rows // tr,),
        in_specs=[pl.BlockSpec((N_DEV, tr, cols), lambda i: (0, i, 0)), blk, blk, blk],
        out_specs=[blk] * 4, out_shape=[jax.ShapeDtypeStruct((rows, cols), F32)] * 4,
        compiler_params=pltpu.CompilerParams(dimension_semantics=("parallel",), vmem_limit_bytes=VMEM_LIMIT),
    )(g8, w, m, v)


def _exchange_specs(srcs, scatter):
    nk = len(srcs)
    if not nk:
        return [], [], [], []
    any_spec = pl.BlockSpec(memory_space=pl.ANY)
    out_shape = [jax.ShapeDtypeStruct(s.shape if sc else (N_DEV,) + s.shape, s.dtype) for s, sc in zip(srcs, scatter)]
    sems = [pltpu.SemaphoreType.DMA((nk, N_DEV - 1)), pltpu.SemaphoreType.DMA((nk, N_DEV - 1)),
            pltpu.SemaphoreType.DMA((nk,))]
    return [any_spec] * nk, [any_spec] * nk, out_shape, sems


FLIPS = ((0, 0, 1), (1, 0, 0), (0, 1, 0), (1, 1, 0), (1, 0, 1), (0, 1, 1), (1, 1, 1))
N_CHIP_PEERS = 3


def _exchange_fns(src_refs, out_refs, sems, scatter):
    nk = len(src_refs)
    if not nk:
        return (lambda: None), (lambda: None), (lambda: None)
    send_sems, recv_sems, local_sems = sems
    first = 1 + N_CHIP_PEERS

    def plan():
        x, y, c = lax.axis_index("x"), lax.axis_index("y"), lax.axis_index("c")
        me = 4 * x + 2 * y + c
        peers = [(1 - x if fx else x, 1 - y if fy else y, 1 - c if fc else c) for fx, fy, fc in FLIPS]
        pids = [4 * px + 2 * py + pc for px, py, pc in peers]

        def remote(k, j, src, dst, to):
            return pltpu.make_async_remote_copy(src_ref=src, dst_ref=dst, send_sem=send_sems.at[k, j],
                                                recv_sem=recv_sems.at[k, j], device_id=to, device_id_type=MESH)

        def mine(k, dest):
            return src_refs[k].at[dest] if scatter[k] else src_refs[k]

        local = [pltpu.make_async_copy(mine(k, me), out_refs[k].at[me], local_sems.at[k]) for k in range(nk)]
        direct = [remote(k, j, mine(k, pids[j]), out_refs[k].at[me], peers[j])
                  for k in range(nk) for j in range(len(FLIPS) if scatter[k] else first)]
        relays = {(k, j): remote(k, j, out_refs[k].at[pids[j - N_CHIP_PEERS]], out_refs[k].at[pids[j - N_CHIP_PEERS]], peers[0])
                  for k in range(nk) if not scatter[k] for j in range(first, len(FLIPS))}
        arrivals = {(k, j): remote(k, j, out_refs[k].at[pids[j]], out_refs[k].at[pids[j]], peers[j])
                    for k in range(nk) for j in range(len(FLIPS))}
        return local, direct, relays, arrivals

    def start():
        local, direct, _, _ = plan()
        for cp in local + direct:
            cp.start()

    def relay():
        _, _, relays, arrivals = plan()
        for (k, j), cp in relays.items():
            arrivals[k, j - N_CHIP_PEERS].wait_recv()
            cp.start()

    def wait():
        local, direct, relays, arrivals = plan()
        for (k, j), cp in arrivals.items():
            if (k, j + N_CHIP_PEERS) not in relays:
                cp.wait_recv()
        for cp in direct + list(relays.values()):
            cp.wait_send()
        for cp in local:
            cp.wait()

    return start, relay, wait


def _grid_step(rank):
    step, total = 0, 1
    for axis in range(rank):
        step = step * pl.num_programs(axis) + pl.program_id(axis)
        total = total * pl.num_programs(axis)
    return step, total


def _ride(rank, start, relay, wait):
    step, total = _grid_step(rank)
    pl.when(step == 0)(start)
    pl.when(step == (3 * total) // 4)(relay)
    return lambda: pl.when(step == total - 1)(wait)


def _exchange(name, srcs, scatter):
    nk = len(srcs)
    c_in, c_out, c_shape, c_sems = _exchange_specs(srcs, scatter)

    def body(*refs):
        start, relay, wait = _exchange_fns(refs[:nk], refs[nk:2 * nk], refs[2 * nk:], scatter)
        start()
        relay()
        wait()

    return pl.pallas_call(body, name=name, in_specs=c_in, out_specs=c_out, out_shape=c_shape, scratch_shapes=c_sems)(*srcs)


def _cols_from_shards(g):
    return jnp.transpose(g, (1, 0, 2)).reshape(g.shape[1], -1)


def _cols_to_shards(w):
    return jnp.transpose(w.reshape(w.shape[0], N_DEV, -1), (1, 0, 2))


def _prep(x, tgt, srcs, scatter):
    nb = x.shape[0]
    t = _t_pad()
    head = PAD_ROWS + N_META
    nk = len(srcs)
    c_in, c_out, c_shape, c_sems = _exchange_specs(srcs, scatter)

    def body(x_ref, tgt_ref, *rest):
        h0_ref, tp_ref = rest[nk:nk + 2]
        finish = _ride(2, *_exchange_fns(rest[:nk], rest[nk + 2:2 * nk + 2], rest[2 * nk + 2:], scatter))
        lead = pl.program_id(1) == 0

        @pl.when(lead)
        def _():
            h0_ref[...] = jnp.zeros_like(h0_ref)
            tp_ref[...] = jnp.zeros_like(tp_ref)

        @pl.when(jnp.logical_not(lead))
        def _():
            h0_ref[...] = x_ref[...]
            tp_ref[...] = tgt_ref[...]

        finish()

    src = pl.BlockSpec((1, head, D_MODEL), lambda b, j: (b, jnp.maximum(j - 1, 0), 0))
    dst = pl.BlockSpec((1, head, D_MODEL), lambda b, j: (b, j, 0))
    padded = jax.ShapeDtypeStruct((nb, t, D_MODEL), F32)
    res = pl.pallas_call(
        body, name="prep", grid=(nb, t // head), in_specs=[src, src] + c_in, out_specs=[dst, dst] + c_out,
        out_shape=[padded, padded] + c_shape, scratch_shapes=c_sems,
        compiler_params=pltpu.CompilerParams(dimension_semantics=("arbitrary", "arbitrary")),
    )(x, tgt, *srcs)
    return res[0], res[1], res[2:]


def _rope_tables(n):
    t = _t_pad()
    pos = np.arange(t, dtype=np.float32) - np.float32(PAD_ROWS)
    half = QK_ROPE // 2
    freqs = (1.0 / (ROPE_THETA ** (np.arange(half, dtype=np.float32) / half))).astype(np.float32)
    ang = pos[:, None] * freqs[None, :]
    cos, sin = np.cos(ang), np.sin(ang)
    z = lambda w: np.zeros((t, w), np.float32)
    c = np.concatenate([np.ones((t, QK_NOPE), np.float32), cos, cos, z(HEAD_PAD - QK_HEAD)], axis=1)
    s1 = np.concatenate([z(QK_NOPE + half), sin, z(HEAD_PAD - QK_HEAD)], axis=1)
    s2 = np.concatenate([z(QK_NOPE), -sin, z(HEAD_PAD - QK_NOPE - half)], axis=1)
    return tuple(jnp.asarray(np.tile(a, (n // t, 1))) for a in (c, s1, s2))


def _block_diag_gates(lru_wa, lru_wi):
    eye = jnp.eye(2, dtype=lru_wa.dtype)

    def bd(w):
        w = w.reshape(2, D_RNN // LANES, 2, RNN_BW, RNN_BW)
        full = w[:, :, :, :, None, :] * eye[None, None, :, None, :, None]
        return full.reshape(2, D_RNN // LANES, LANES, LANES)

    a, i = bd(lru_wa), bd(lru_wi)
    return jnp.concatenate([a[0], i[0], a[1], i[1]], axis=-1)


def _unblock_gates(dw):
    nb = D_RNN // LANES
    parts = dw.reshape(nb, 2, RNN_BW, 4, 2, RNN_BW)
    diag = jnp.stack([parts[:, k, :, :, k, :] for k in range(2)], axis=1)
    diag = jnp.transpose(diag, (3, 0, 1, 2, 4)).reshape(4, 2 * nb, RNN_BW, RNN_BW)
    return jnp.stack([diag[0], diag[2]]), jnp.stack([diag[1], diag[3]])


WEIGHTS = ("meta_tokens", "ln1_g", "w_in", "q_a_norm_g", "w_uq", "kv_a_norm_g", "w_ukv", "q_norm_g", "k_norm_g",
           "conv_w", "conv_b", "lru_wa", "lru_ba", "lru_wi", "lru_bi", "lru_lambda", "attn_out_g", "rnn_out_g",
           "w_out", "ln2_g", "w_gate", "w_up", "w_down")
BIG = ("w_in", "w_uq", "w_ukv", "w_out", "w_gate", "w_up", "w_down")
TRANSPOSED = ("w_in", "w_uq", "w_gate", "w_up")
ROW_SHARDED = ("w_out", "w_down") + TRANSPOSED
REPLICATED = ("ln1_g", "q_a_norm_g", "kv_a_norm_g", "q_norm_g", "k_norm_g", "conv_b", "lru_wa", "lru_wi",
              "attn_out_g", "rnn_out_g", "ln2_g")
WHOLE = REPLICATED + ("loss",)
G_FIRST = ("w_in", "meta_tokens")
G_MID = ("w_uq", "w_ukv", "conv_w", "lru_ba", "lru_bi", "lru_lambda")
LATE = ("w_out", "w_gate", "w_up", "w_down")
G_LAST = ("meta_tokens", "ln1_g")


def _local_step(x, tgt, ex):
    nb = x.shape[0]
    t = _t_pad()
    n = nb * t
    local = ex.local
    h0, tgt_p, got = _prep(x, tgt, *ex.gather_srcs(G_FIRST))
    first = ex.gathered(G_FIRST, got)
    meta, w_in = first["meta_tokens"], first["w_in"]
    h0 = h0.at[:, PAD_ROWS:PAD_ROWS + N_META].set(jnp.broadcast_to(meta[None], (nb, N_META, D_MODEL))).reshape(n, D_MODEL)
    tgt_p = tgt_p.reshape(n, D_MODEL)

    zr = lambda r: jnp.zeros((r, D_MODEL), w_in.dtype)
    w_in_p = jnp.concatenate([w_in[:OFF_CKV], w_in[OFF_KR:], zr(QK_NOPE), w_in[OFF_CKV:OFF_KR], zr(HEAD_PAD - QK_HEAD)],
                             axis=0)
    pad_g = lambda g: jnp.pad(g, ((0, 0), (0, HEAD_PAD - QK_HEAD)))
    qg, kg = pad_g(local["q_norm_g"]), pad_g(local["k_norm_g"])
    rc, rs1, rs2 = _rope_tables(n)
    wblk = _block_diag_gates(local["lru_wa"].reshape(2, -1, RNN_BW, RNN_BW),
                             local["lru_wi"].reshape(2, -1, RNN_BW, RNN_BW)).astype(BF16)
    nblk = D_RNN // LANES

    (hn, cq, ckv, xr, xg, kr), got = _in_proj(h0, local["ln1_g"], w_in_p, *ex.gather_srcs(G_MID))
    w = ex.gathered(G_MID, got)
    w_uq_p = jnp.pad(w["w_uq"].reshape(N_HEADS, QK_HEAD, Q_LORA), ((0, 0), (0, HEAD_PAD - QK_HEAD), (0, 0))
                     ).reshape(QP_COLS, Q_LORA)
    ukv = w["w_ukv"].reshape(KV_LORA, N_HEADS, QK_NOPE + V_HEAD)
    w_uk_p = jnp.pad(ukv[:, :, :QK_NOPE], ((0, 0), (0, 0), (0, HEAD_PAD - QK_NOPE))).reshape(KV_LORA, QP_COLS)
    w_v = ukv[:, :, QK_NOPE:].reshape(KV_LORA, D_ATTN)
    gbias = jnp.stack([w["lru_ba"][0], w["lru_bi"][0], w["lru_ba"][1], w["lru_bi"][1]], axis=0)
    gbias = jnp.transpose(gbias.reshape(4, nblk, LANES), (1, 0, 2)).reshape(nblk, 1, 4 * LANES)

    q, k, v = _qkv_fwd(cq, ckv, kr, local["q_a_norm_g"], local["kv_a_norm_g"], w_uq_p, w_uk_p, w_v, qg, kg, rc, rs1, rs2)
    oa, got = _attn_fwd(q, k, v, *ex.gather_srcs(LATE))
    late = ex.gathered(LATE, got)
    orn = _rnn_fwd(xr, xg, w["conv_w"], local["conv_b"], wblk, gbias, w["lru_lambda"])
    (doa, dor, dh1, mix_t, h1n, act_t, dgate_t, dup_t, dyb, loss, dga, dgr, dg2) = _post(
        oa, orn, h0, tgt_p, local["attn_out_g"], local["rnn_out_g"], local["ln2_g"], late["w_out"], late["w_gate"],
        late["w_up"], late["w_down"])
    wire = {"w_out": _matmul_shards("dw_out", mix_t, dh1), "w_gate": _matmul_shards("dw_gate", dgate_t, h1n),
            "w_up": _matmul_shards("dw_up", dup_t, h1n), "w_down": _matmul_shards("dw_down", act_t, dyb)}
    dxr, dxg, dcw, dcb, dwblk, dgb, dlam = _rnn_bwd(xr, xg, dor, w["conv_w"], local["conv_b"], wblk, gbias, w["lru_lambda"])
    dwa, dwi = _unblock_gates(dwblk)
    dgb = jnp.transpose(dgb.reshape(nblk, 4, LANES), (1, 0, 2)).reshape(4, D_RNN)
    wire.update(ex.to_wire({
        "conv_w": dcw, "conv_b": dcb, "lru_wa": dwa.reshape(-1, RNN_BW), "lru_ba": jnp.stack([dgb[0], dgb[2]]),
        "lru_wi": dwi.reshape(-1, RNN_BW), "lru_bi": jnp.stack([dgb[1], dgb[3]]), "lru_lambda": dlam,
        "attn_out_g": dga, "rnn_out_g": dgr, "ln2_g": dg2, "loss": loss}))
    names = tuple(wire)
    (dq_r, dk_r, dv), got = _attn_bwd(q, k, v, doa, oa, *ex.scatter_srcs(names, wire))
    summed = ex.scattered(names, wire, got)
    (dp, qa, kva, dqp, dkv, dqg, dkg, dgqa, dgkva) = _qkv_bwd(
        cq, ckv, kr, dq_r, dk_r, dv, dxr, dxg, local["q_a_norm_g"], local["kv_a_norm_g"], w_uq_p, w_uk_p, w_v, qg, kg,
        rc, rs1, rs2)
    dw_in_p = _matmul_tn("dw_in", dp, hn)
    dw_uq_p = _matmul_tn("dw_uq", dqp, qa)
    dw_kv = _matmul_tn("dw_ukv", kva, dkv)
    kr0 = OFF_CKV + 2 * D_RNN + QK_NOPE
    dw_in = jnp.concatenate([dw_in_p[:OFF_CKV], dw_in_p[kr0:kr0 + QK_ROPE], dw_in_p[OFF_CKV:OFF_CKV + 2 * D_RNN]], axis=0)
    dw_uq = dw_uq_p.reshape(N_HEADS, HEAD_PAD, Q_LORA)[:, :QK_HEAD].reshape(N_HEADS * QK_HEAD, Q_LORA)
    dw_ukv = jnp.concatenate([dw_kv[:, :QP_COLS].reshape(KV_LORA, N_HEADS, HEAD_PAD)[:, :, :QK_NOPE],
                              dw_kv[:, QP_COLS:].reshape(KV_LORA, N_HEADS, V_HEAD)], axis=2).reshape(KV_LORA, -1)
    wire = ex.to_wire({"w_in": dw_in, "q_a_norm_g": dgqa, "w_uq": dw_uq, "kv_a_norm_g": dgkva, "w_ukv": dw_ukv,
                       "q_norm_g": dqg[:, :QK_HEAD], "k_norm_g": dkg[:, :QK_HEAD]})
    names = tuple(wire)
    (grad_x, dlead, dg1), got = _in_bwd(dp, h0, dh1, local["ln1_g"], w_in_p, *ex.scatter_srcs(names, wire))
    summed.update(ex.scattered(names, wire, got))

    wire = ex.to_wire({"meta_tokens": jnp.sum(dlead[:, PAD_ROWS:], axis=0), "ln1_g": dg1})
    got = ex.run("reduce_last", *ex.scatter_srcs(G_LAST, wire))
    summed.update(ex.scattered(G_LAST, wire, got))
    return grad_x, summed


class _MeshExchange:
    def __init__(self, shards):
        self.local = shards

    @staticmethod
    def run(name, srcs, scatter):
        return _exchange(name, srcs, scatter)

    def gather_srcs(self, names):
        return [self.local[k].astype(BF16) if k in BIG else self.local[k] for k in names], [False] * len(names)

    @staticmethod
    def gathered(names, outs):
        return {k: g.reshape(-1, g.shape[-1]) if k in ROW_SHARDED else _cols_from_shards(g) for k, g in zip(names, outs)}

    @staticmethod
    def to_wire(grads):
        wire = {}
        for k, g in grads.items():
            if k in WHOLE:
                wire[k] = g
            elif k in ROW_SHARDED:
                wire[k] = g.reshape(N_DEV, -1, g.shape[-1]).astype(BF16)
            else:
                wire[k] = _cols_to_shards(g).astype(BF16) if k in BIG else _cols_to_shards(g)
        return wire

    @staticmethod
    def scatter_srcs(names, wire):
        return [wire[k] for k in names], [k not in WHOLE for k in names]

    @staticmethod
    def scattered(names, wire, outs):
        return dict(zip(names, outs))


def kernel(x, meta_tokens, ln1_g, w_in, q_a_norm_g, w_uq, kv_a_norm_g, w_ukv, q_norm_g, k_norm_g, conv_w, conv_b, lru_wa, lru_ba, lru_wi, lru_bi, lru_lambda, attn_out_g, rnn_out_g, w_out, ln2_g, w_gate, w_up, w_down, loss_target, m_meta_tokens, m_ln1_g, m_w_in, m_q_a_norm_g, m_w_uq, m_kv_a_norm_g, m_w_ukv, m_q_norm_g, m_k_norm_g, m_conv_w, m_conv_b, m_lru_wa, m_lru_ba, m_lru_wi, m_lru_bi, m_lru_lambda, m_attn_out_g, m_rnn_out_g, m_w_out, m_ln2_g, m_w_gate, m_w_up, m_w_down, v_meta_tokens, v_ln1_g, v_w_in, v_q_a_norm_g, v_w_uq, v_kv_a_norm_g, v_w_ukv, v_q_norm_g, v_k_norm_g, v_conv_w, v_conv_b, v_lru_wa, v_lru_ba, v_lru_wi, v_lru_bi, v_lru_lambda, v_attn_out_g, v_rnn_out_g, v_w_out, v_ln2_g, v_w_gate, v_w_up, v_w_down):
    given = (meta_tokens, ln1_g, w_in, q_a_norm_g, w_uq, kv_a_norm_g, w_ukv, q_norm_g, k_norm_g, conv_w, conv_b,
             lru_wa, lru_ba, lru_wi, lru_bi, lru_lambda, attn_out_g, rnn_out_g, w_out, ln2_g, w_gate, w_up, w_down)
    moments_m = (m_meta_tokens, m_ln1_g, m_w_in, m_q_a_norm_g, m_w_uq, m_kv_a_norm_g, m_w_ukv, m_q_norm_g, m_k_norm_g,
                 m_conv_w, m_conv_b, m_lru_wa, m_lru_ba, m_lru_wi, m_lru_bi, m_lru_lambda, m_attn_out_g, m_rnn_out_g,
                 m_w_out, m_ln2_g, m_w_gate, m_w_up, m_w_down)
    moments_v = (v_meta_tokens, v_ln1_g, v_w_in, v_q_a_norm_g, v_w_uq, v_kv_a_norm_g, v_w_ukv, v_q_norm_g, v_k_norm_g,
                 v_conv_w, v_conv_b, v_lru_wa, v_lru_ba, v_lru_wi, v_lru_bi, v_lru_lambda, v_attn_out_g, v_rnn_out_g,
                 v_w_out, v_ln2_g, v_w_gate, v_w_up, v_w_down)
    shapes = {k: a.shape for k, a in zip(WEIGHTS, given)}

    def two_d(k, a):
        a = a.reshape(-1, a.shape[-1])
        return a.T if k in TRANSPOSED else a

    w = {k: two_d(k, a) for k, a in zip(WEIGHTS, given)}
    m = {k: two_d(k, a) for k, a in zip(WEIGHTS, moments_m)}
    v = {k: two_d(k, a) for k, a in zip(WEIGHTS, moments_v)}

    grad_x, parts = _local_step(x, loss_target, _MeshExchange(w))

    new = {k: _adamw("adamw_" + k, parts[k], w[k], m[k], v[k]) for k in BIG}
    small = [k for k in WEIGHTS if k not in BIG]
    new.update(zip(small, _adamw_many("adamw_small", [(parts[k], w[k], m[k], v[k]) for k in small])))

    loss = jnp.sum(parts["loss"][:, 0, 0])
    outs = [loss, grad_x]
    for idx in range(4):
        outs += [(new[k][idx].T if k in TRANSPOSED else new[k][idx]).reshape(shapes[k]) for k in WEIGHTS]
    return tuple(outs)
```

```python
import functools
import math

import numpy as np
import jax
import jax.numpy as jnp
from jax import lax
from jax.experimental import pallas as pl
from jax.experimental.pallas import tpu as pltpu

F32 = jnp.float32
BF16 = jnp.bfloat16

D_MODEL = 1024
N_META = 16
SEQ = 2048
N_HEADS = 8
QK_NOPE = 64
QK_ROPE = 32
QK_HEAD = QK_NOPE + QK_ROPE
V_HEAD = 64
D_ATTN = N_HEADS * V_HEAD
Q_LORA = 384
KV_LORA = 256
D_RNN = 512
RNN_BW = 64
D_FF = 2816
EPS = 1e-6
LRU_C = 8.0
ROPE_THETA = 10000.0
OFF_CKV = Q_LORA + KV_LORA
OFF_KR = OFF_CKV + QK_ROPE
IN_COLS = OFF_KR + 2 * D_RNN

ADAM_LR = 0.001
ADAM_B1 = 0.9
ADAM_B2 = 0.999
ADAM_EPS = 1e-08
ADAM_WD = 0.01
ADAM_STEP = 10

N_DEV = 8
LANES = 128
HEAD_PAD = LANES
PAD_ROWS = LANES - N_META
QP_COLS = N_HEADS * HEAD_PAD
P_COLS = OFF_CKV + 2 * D_RNN + LANES
FF_CHUNK = D_FF // 2
VMEM_LIMIT = 56 * 1024 * 1024
MESH = pl.DeviceIdType.MESH


def _t_pad():
    return PAD_ROWS + N_META + SEQ


def _row_tile(n):
    return 256 if n % 256 == 0 else 128


def _const_spec(shape):
    nd = len(shape)
    return pl.BlockSpec(shape, lambda *_: (0,) * nd, pipeline_mode=pl.Buffered(1))


def _rms(x, d):
    r = lax.rsqrt(jnp.sum(x * x, axis=-1, keepdims=True) * (1.0 / d) + EPS)
    return x * r, r


def _rms_bwd(dy, xhat, r, g, d):
    dxh = dy * g
    return r * (dxh - xhat * (jnp.sum(dxh * xhat, axis=-1, keepdims=True) * (1.0 / d)))


def _colsum(x):
    return jnp.sum(x, axis=0, keepdims=True)


def _dot(a, b):
    return jnp.dot(a, b, preferred_element_type=F32)


def _dot_nt(a, b):
    return lax.dot_general(a, b, (((1,), (1,)), ((), ())), preferred_element_type=F32)


def _dot_tn(a, b):
    return lax.dot_general(a, b, (((0,), (0,)), ((), ())), preferred_element_type=F32)


def _rope(x, c, s1, s2):
    return x * c + pltpu.roll(x, 16, 1) * s1 + pltpu.roll(x, HEAD_PAD - 16, 1) * s2


def _rope_bwd(dy, c, s1, s2):
    return dy * c + pltpu.roll(dy * s1, HEAD_PAD - 16, 1) + pltpu.roll(dy * s2, 16, 1)


def _acc(ref, first, val):
    @pl.when(first)
    def _():
        ref[...] = val

    @pl.when(jnp.logical_not(first))
    def _():
        ref[...] += val


def _in_proj(h0, ln1_g, w_in_p, srcs=(), scatter=()):
    n = h0.shape[0]
    tm = _row_tile(n)
    nk = len(srcs)
    c_in, c_out, c_shape, c_sems = _exchange_specs(srcs, scatter)

    def body(h_ref, g_ref, w_ref, *rest):
        hn_ref, cq_ref, ckv_ref, xr_ref, xg_ref, kr_ref = rest[nk:nk + 6]
        finish = _ride(1, *_exchange_fns(rest[:nk], rest[nk + 6:2 * nk + 6], rest[2 * nk + 6:], scatter))
        xhat, _ = _rms(h_ref[...], D_MODEL)
        hn = (xhat * g_ref[...]).astype(BF16)
        hn_ref[...] = hn
        p = _dot_nt(hn, w_ref[...])
        cq_ref[...] = p[:, :Q_LORA]
        ckv_ref[...] = p[:, Q_LORA:OFF_CKV]
        xr_ref[...] = p[:, OFF_CKV:OFF_CKV + D_RNN]
        xg_ref[...] = p[:, OFF_CKV + D_RNN:OFF_CKV + 2 * D_RNN]
        kr_ref[...] = p[:, OFF_CKV + 2 * D_RNN:]
        finish()

    def row(w):
        return pl.BlockSpec((tm, w), lambda i: (i, 0))

    widths = (D_MODEL, Q_LORA, KV_LORA, D_RNN, D_RNN, LANES)
    res = pl.pallas_call(
        body, name="in_proj", grid=(n // tm,),
        in_specs=[row(D_MODEL), _const_spec((1, D_MODEL)), _const_spec((P_COLS, D_MODEL))] + c_in,
        out_specs=[row(w) for w in widths] + c_out,
        out_shape=[jax.ShapeDtypeStruct((n, w), BF16 if k == 0 else F32) for k, w in enumerate(widths)] + c_shape,
        scratch_shapes=c_sems,
        compiler_params=pltpu.CompilerParams(dimension_semantics=("arbitrary",), vmem_limit_bytes=VMEM_LIMIT),
    )(h0, ln1_g, w_in_p, *srcs)
    return res[:6], res[6:]


def _qkv_fwd(cq, ckv, kr, gqa, gkva, w_uq_p, w_uk_p, w_v, qg, kg, rc, rs1, rs2):
    n = cq.shape[0]
    tm = _row_tile(n)

    def body(cq_ref, ckv_ref, kr_ref, gqa_ref, gkva_ref, wuq_ref, wuk_ref, wv_ref, qg_ref, kg_ref,
             c_ref, s1_ref, s2_ref, q_ref, k_ref, v_ref):
        xq, _ = _rms(cq_ref[...], Q_LORA)
        qa = (xq * gqa_ref[...]).astype(BF16)
        q = _dot_nt(qa, wuq_ref[...])
        xkv, _ = _rms(ckv_ref[...], KV_LORA)
        kva = (xkv * gkva_ref[...]).astype(BF16)
        kn = _dot(kva, wuk_ref[...])
        v_ref[...] = _dot(kva, wv_ref[...]).astype(BF16)
        krp = kr_ref[...]
        c, s1, s2 = c_ref[...], s1_ref[...], s2_ref[...]
        for h in range(N_HEADS):
            sl = slice(h * HEAD_PAD, (h + 1) * HEAD_PAD)
            qh, _ = _rms(q[:, sl], QK_HEAD)
            q_ref[:, sl] = _rope(qh * qg_ref[...], c, s1, s2).astype(BF16)
            kh, _ = _rms(kn[:, sl] + krp, QK_HEAD)
            k_ref[:, sl] = _rope(kh * kg_ref[...], c, s1, s2).astype(BF16)

    def row(w):
        return pl.BlockSpec((tm, w), lambda i: (i, 0))

    return pl.pallas_call(
        body, name="qkv_fwd", grid=(n // tm,),
        in_specs=[row(Q_LORA), row(KV_LORA), row(LANES), _const_spec((1, Q_LORA)), _const_spec((1, KV_LORA)),
                  _const_spec((QP_COLS, Q_LORA)), _const_spec((KV_LORA, QP_COLS)), _const_spec((KV_LORA, D_ATTN)),
                  _const_spec((1, LANES)), _const_spec((1, LANES)), row(LANES), row(LANES), row(LANES)],
        out_specs=[row(QP_COLS), row(QP_COLS), row(D_ATTN)],
        out_shape=[jax.ShapeDtypeStruct((n, QP_COLS), BF16), jax.ShapeDtypeStruct((n, QP_COLS), BF16),
                   jax.ShapeDtypeStruct((n, D_ATTN), BF16)],
        compiler_params=pltpu.CompilerParams(dimension_semantics=("parallel",), vmem_limit_bytes=VMEM_LIMIT),
    )(cq, ckv, kr, gqa, gkva, w_uq_p, w_uk_p, w_v, qg, kg, rc, rs1, rs2)


def _qkv_bwd(cq, ckv, kr, dq_r, dk_r, dv, dxr, dxg, gqa, gkva, w_uq_p, w_uk_p, w_v, qg, kg, rc, rs1, rs2):
    n = cq.shape[0]
    tm = _row_tile(n)

    def body(cq_ref, ckv_ref, kr_ref, dq_ref, dk_ref, dv_ref, dxr_ref, dxg_ref, gqa_ref, gkva_ref, wuq_ref, wuk_ref,
             wv_ref, qg_ref, kg_ref, c_ref, s1_ref, s2_ref,
             dp_ref, qa_ref, kva_ref, dqp_ref, dkv_ref, dqg_ref, dkg_ref, dgqa_ref, dgkva_ref):
        first = pl.program_id(0) == 0
        dp_ref[:, OFF_CKV:OFF_CKV + D_RNN] = dxr_ref[...].astype(BF16)
        dp_ref[:, OFF_CKV + D_RNN:OFF_CKV + 2 * D_RNN] = dxg_ref[...].astype(BF16)
        xq, rq = _rms(cq_ref[...], Q_LORA)
        qa = (xq * gqa_ref[...]).astype(BF16)
        qa_ref[...] = qa
        q = _dot_nt(qa, wuq_ref[...])
        xkv, rkv = _rms(ckv_ref[...], KV_LORA)
        kva = (xkv * gkva_ref[...]).astype(BF16)
        kva_ref[...] = kva
        kn = _dot(kva, wuk_ref[...])
        krp = kr_ref[...]
        c, s1, s2 = c_ref[...], s1_ref[...], s2_ref[...]
        lane = lax.broadcasted_iota(jnp.int32, (tm, HEAD_PAD), 1)
        rope_lanes = jnp.logical_and(lane >= QK_NOPE, lane < QK_HEAD)
        dqg = jnp.zeros((1, HEAD_PAD), F32)
        dkg = jnp.zeros((1, HEAD_PAD), F32)
        dkr = jnp.zeros((tm, HEAD_PAD), F32)
        for h in range(N_HEADS):
            sl = slice(h * HEAD_PAD, (h + 1) * HEAD_PAD)
            qh, rqh = _rms(q[:, sl], QK_HEAD)
            dy = _rope_bwd(dq_ref[:, sl], c, s1, s2)
            dqg = dqg + _colsum(dy * qh)
            dqp_ref[:, sl] = _rms_bwd(dy, qh, rqh, qg_ref[...], QK_HEAD).astype(BF16)
            kh, rkh = _rms(kn[:, sl] + krp, QK_HEAD)
            dyk = _rope_bwd(dk_ref[:, sl], c, s1, s2)
            dkg = dkg + _colsum(dyk * kh)
            dkh = _rms_bwd(dyk, kh, rkh, kg_ref[...], QK_HEAD)
            dkv_ref[:, sl] = dkh.astype(BF16)
            dkr = dkr + jnp.where(rope_lanes, dkh, 0.0)
        dkv_ref[:, QP_COLS:] = dv_ref[...].astype(BF16)
        dp_ref[:, OFF_CKV + 2 * D_RNN:] = dkr.astype(BF16)
        dqa = _dot(dqp_ref[...], wuq_ref[...])
        dp_ref[:, :Q_LORA] = _rms_bwd(dqa, xq, rq, gqa_ref[...], Q_LORA).astype(BF16)
        dkva = _dot_nt(dkv_ref[:, :QP_COLS], wuk_ref[...]) + _dot_nt(dkv_ref[:, QP_COLS:], wv_ref[...])
        dp_ref[:, Q_LORA:OFF_CKV] = _rms_bwd(dkva, xkv, rkv, gkva_ref[...], KV_LORA).astype(BF16)
        _acc(dqg_ref, first, dqg)
        _acc(dkg_ref, first, dkg)
        _acc(dgqa_ref, first, _colsum(dqa * xq))
        _acc(dgkva_ref, first, _colsum(dkva * xkv))

    def row(w):
        return pl.BlockSpec((tm, w), lambda i: (i, 0))

    def acc(w):
        return pl.BlockSpec((1, w), lambda i: (0, 0))

    return pl.pallas_call(
        body, name="qkv_bwd", grid=(n // tm,),
        in_specs=[row(Q_LORA), row(KV_LORA), row(LANES), row(QP_COLS), row(QP_COLS), row(D_ATTN), row(D_RNN), row(D_RNN),
                  _const_spec((1, Q_LORA)), _const_spec((1, KV_LORA)),
                  _const_spec((QP_COLS, Q_LORA)), _const_spec((KV_LORA, QP_COLS)), _const_spec((KV_LORA, D_ATTN)),
                  _const_spec((1, LANES)), _const_spec((1, LANES)), row(LANES), row(LANES), row(LANES)],
        out_specs=[row(P_COLS), row(Q_LORA), row(KV_LORA), row(QP_COLS),
                   row(QP_COLS + D_ATTN), acc(LANES), acc(LANES), acc(Q_LORA), acc(KV_LORA)],
        out_shape=[jax.ShapeDtypeStruct((n, P_COLS), BF16), jax.ShapeDtypeStruct((n, Q_LORA), BF16),
                   jax.ShapeDtypeStruct((n, KV_LORA), BF16), jax.ShapeDtypeStruct((n, QP_COLS), BF16),
                   jax.ShapeDtypeStruct((n, QP_COLS + D_ATTN), BF16),
                   jax.ShapeDtypeStruct((1, LANES), F32), jax.ShapeDtypeStruct((1, LANES), F32),
                   jax.ShapeDtypeStruct((1, Q_LORA), F32), jax.ShapeDtypeStruct((1, KV_LORA), F32)],
        compiler_params=pltpu.CompilerParams(dimension_semantics=("arbitrary",), vmem_limit_bytes=VMEM_LIMIT),
    )(cq, ckv, kr, dq_r, dk_r, dv, dxr, dxg, gqa, gkva, w_uq_p, w_uk_p, w_v, qg, kg, rc, rs1, rs2)


KEY_CHUNK = 4 * LANES


def _key_chunks(t):
    count = max(t // KEY_CHUNK, 1)
    first = t - KEY_CHUNK * (count - 1)
    return [(0, first)] + [(first + KEY_CHUNK * c, KEY_CHUNK) for c in range(count - 1)]


def _softmax_parts(qh, k_ref, sl, tq, t):
    scores = []
    for start, size in _key_chunks(t):
        s = _dot_nt(qh, k_ref[start:start + size, sl]) * (QK_HEAD ** -0.5)
        if start < PAD_ROWS:
            key = lax.broadcasted_iota(jnp.int32, (tq, size), 1) + start
            s = jnp.where(key >= PAD_ROWS, s, -jnp.inf)
        scores.append(s)
    top = functools.reduce(jnp.maximum, [jnp.max(s, axis=-1, keepdims=True) for s in scores])
    es = [jnp.exp(s - top) for s in scores]
    return es, functools.reduce(jnp.add, [jnp.sum(e, axis=-1, keepdims=True) for e in es])


def _attn_specs(t, tq):
    nq = t // tq
    qspec = pl.BlockSpec((tq, 2 * HEAD_PAD), lambda b, hp, i: (b * nq + i, hp))
    kspec = pl.BlockSpec((t, 2 * HEAD_PAD), lambda b, hp, i: (b, hp))
    vspec = pl.BlockSpec((t, 2 * V_HEAD), lambda b, hp, i: (b, hp))
    ospec = pl.BlockSpec((tq, 2 * V_HEAD), lambda b, hp, i: (b * nq + i, hp))
    return nq, qspec, kspec, vspec, ospec


def _attn_fwd(q, k, v, srcs=(), scatter=()):
    n = q.shape[0]
    t = _t_pad()
    tq = t // 2
    nq, qspec, kspec, vspec, ospec = _attn_specs(t, tq)
    nk = len(srcs)
    c_in, c_out, c_shape, c_sems = _exchange_specs(srcs, scatter)

    def body(q_ref, k_ref, v_ref, *rest):
        o_ref = rest[nk]
        finish = _ride(3, *_exchange_fns(rest[:nk], rest[nk + 1:2 * nk + 1], rest[2 * nk + 1:], scatter))
        lane = lax.broadcasted_iota(jnp.int32, (tq, 2 * V_HEAD), 1)
        outs = []
        for j in range(2):
            sl = slice(j * HEAD_PAD, (j + 1) * HEAD_PAD)
            es, l = _softmax_parts(q_ref[:, sl], k_ref, sl, tq, t)
            pv = [_dot(e.astype(BF16), v_ref[start:start + size, :]) for e, (start, size) in zip(es, _key_chunks(t))]
            outs.append(functools.reduce(jnp.add, pv) / l)
        o_ref[...] = jnp.where(lane < V_HEAD, outs[0], outs[1])
        finish()

    res = pl.pallas_call(
        body, name="attn_fwd", grid=(n // t, N_HEADS // 2, nq),
        in_specs=[qspec, kspec, vspec] + c_in, out_specs=[ospec] + c_out,
        out_shape=[jax.ShapeDtypeStruct((n, D_ATTN), F32)] + c_shape, scratch_shapes=c_sems,
        compiler_params=pltpu.CompilerParams(dimension_semantics=("arbitrary", "arbitrary", "arbitrary"),
                                             vmem_limit_bytes=VMEM_LIMIT),
    )(q, k, v, *srcs)
    return res[0], res[1:]


def _attn_bwd(q, k, v, do, o, srcs=(), scatter=()):
    n = q.shape[0]
    t = _t_pad()
    tq = t // 2
    nq, qspec, kspec, vspec, ospec = _attn_specs(t, tq)
    nk = len(srcs)
    c_in, c_out, c_shape, c_sems = _exchange_specs(srcs, scatter)

    def body(q_ref, k_ref, v_ref, do_ref, o_ref, *rest):
        dq_ref, dk_ref, dv_ref = rest[nk:nk + 3]
        finish = _ride(3, *_exchange_fns(rest[:nk], rest[nk + 3:2 * nk + 3], rest[2 * nk + 3:], scatter))

        @pl.when(pl.program_id(2) == 0)
        def _():
            dk_ref[...] = jnp.zeros_like(dk_ref)
            dv_ref[...] = jnp.zeros_like(dv_ref)

        lane = lax.broadcasted_iota(jnp.int32, (tq, 2 * V_HEAD), 1)
        do = do_ref[...]
        do_o = do * o_ref[...]
        chunks = _key_chunks(t)
        dvs = [None] * len(chunks)
        for j in range(2):
            sl = slice(j * HEAD_PAD, (j + 1) * HEAD_PAD)
            qh = q_ref[:, sl]
            es, l = _softmax_parts(qh, k_ref, sl, tq, t)
            inv_l = 1.0 / l
            in_head = (lane < V_HEAD) if j == 0 else (lane >= V_HEAD)
            doh = jnp.where(in_head, do, 0.0).astype(BF16)
            delta = jnp.sum(jnp.where(in_head, do_o, 0.0), axis=-1, keepdims=True)
            dq = jnp.zeros((tq, HEAD_PAD), F32)
            for c, (start, size) in enumerate(chunks):
                rows = slice(start, start + size)
                p = es[c] * inv_l
                dp = _dot_nt(doh, v_ref[rows, :])
                ds = (p * (dp - delta) * (QK_HEAD ** -0.5)).astype(BF16)
                dq = dq + _dot(ds, k_ref[rows, sl])
                dk_ref[rows, sl] += _dot_tn(ds, qh)
                dvc = _dot_tn(p.astype(BF16), doh)
                dvs[c] = dvc if dvs[c] is None else dvs[c] + dvc
            dq_ref[:, sl] = dq
        for (start, size), dvc in zip(chunks, dvs):
            dv_ref[start:start + size, :] += dvc
        finish()

    res = pl.pallas_call(
        body, name="attn_bwd", grid=(n // t, N_HEADS // 2, nq),
        in_specs=[qspec, kspec, vspec, ospec, ospec] + c_in, out_specs=[qspec, kspec, vspec] + c_out,
        out_shape=[jax.ShapeDtypeStruct((n, QP_COLS), F32), jax.ShapeDtypeStruct((n, QP_COLS), F32),
                   jax.ShapeDtypeStruct((n, D_ATTN), F32)] + c_shape, scratch_shapes=c_sems,
        compiler_params=pltpu.CompilerParams(dimension_semantics=("arbitrary", "arbitrary", "arbitrary"),
                                             vmem_limit_bytes=VMEM_LIMIT),
    )(q, k, v, do, o, *srcs)
    return res[:3], res[3:]


SCAN_STEPS = 8


def _scan(chains, t):
    seg = t // 8
    rows = lax.broadcasted_iota(jnp.int32, (8, LANES), 0)

    def step(i, carry):
        carry = list(carry)
        for u in range(SCAN_STEPS):
            j = i * SCAN_STEPS + u
            for n, (a_ref, b_ref, h_ref, p_ref, reverse) in enumerate(chains):
                h, p = carry[n]
                idx = pl.ds(seg - 1 - j if reverse else j, 8, stride=seg)
                a = a_ref[idx, :]
                h = a * h + b_ref[idx, :]
                p = a * p
                h_ref[idx, :] = h
                p_ref[idx, :] = p
                carry[n] = (h, p)
        return tuple(carry)

    init = tuple((jnp.zeros((8, LANES), F32), jnp.ones((8, LANES), F32)) for _ in chains)
    ends = lax.fori_loop(0, seg // SCAN_STEPS, step, init)
    for (_, _, h_ref, p_ref, reverse), (b, a) in zip(chains, ends):
        for d in (1, 2, 4):
            if reverse:
                keep = rows < 8 - d
                a_n, b_n = pltpu.roll(a, 8 - d, 0), pltpu.roll(b, 8 - d, 0)
            else:
                keep = rows >= d
                a_n, b_n = pltpu.roll(a, d, 0), pltpu.roll(b, d, 0)
            b = a * jnp.where(keep, b_n, 0.0) + b
            a = a * jnp.where(keep, a_n, 1.0)
        for s in (range(7) if reverse else range(1, 8)):
            sl = slice(s * seg, (s + 1) * seg)
            carry_in = b[s + 1:s + 2, :] if reverse else b[s - 1:s, :]
            h_ref[sl, :] = h_ref[sl, :] + p_ref[sl, :] * carry_in


def _shift_rows(x, s, rows, t):
    if s == 0:
        return x
    rolled = pltpu.roll(x, s % t, 0)
    return jnp.where(rows >= s, rolled, 0.0) if s > 0 else jnp.where(rows < t + s, rolled, 0.0)


def _neg_expm1(x, exp_x):
    series = -x * (1.0 + x * (0.5 + x * (1.0 / 6 + x * (1.0 / 24))))
    return jnp.where(x > -0.1, series, 1.0 - exp_x)


def _sigmoid(x):
    return 0.5 * jnp.tanh(0.5 * x) + 0.5


def _gelu_parts(x):
    k = math.sqrt(2.0 / math.pi)
    th = jnp.tanh(k * (x + 0.044715 * x * x * x))
    g = 0.5 * x * (1.0 + th)
    dg = 0.5 * (1.0 + th) + 0.5 * x * (1.0 - th * th) * k * (1.0 + 3 * 0.044715 * x * x)
    return g, dg


def _lru_gates(xc, gates, lam_ref, valid, d):
    r = _sigmoid(gates[:, (2 * d) * LANES:(2 * d + 1) * LANES])
    i = _sigmoid(gates[:, (2 * d + 1) * LANES:(2 * d + 2) * LANES])
    neg_lam = -lam_ref[d:d + 1, :]
    sp = jnp.maximum(neg_lam, 0.0) + jnp.log1p(jnp.exp(-jnp.abs(neg_lam)))
    log_a = -LRU_C * r * sp
    a = jnp.exp(log_a)
    m = jnp.maximum(_neg_expm1(2.0 * log_a, a * a), 0.0)
    sq = jnp.sqrt(m)
    b = jnp.where(valid, sq * (i * xc), 0.0)
    return r, i, sp, a, m, sq, b


def _conv(xr, cw_ref, cb_ref, rows, t):
    return (cw_ref[0:1, :] * _shift_rows(xr, 2, rows, t) + cw_ref[1:2, :] * _shift_rows(xr, 1, rows, t)
            + cw_ref[2:3, :] * xr + cw_ref[3:4, :] * _shift_rows(xr, -1, rows, t) + cb_ref[...])


def _rnn_specs(t):
    seq = pl.BlockSpec((t, LANES), lambda cb, b: (b, cb))
    cw = pl.BlockSpec((4, LANES), lambda cb, b: (0, cb))
    vec1 = pl.BlockSpec((1, LANES), lambda cb, b: (0, cb))
    vec2 = pl.BlockSpec((2, LANES), lambda cb, b: (0, cb))
    wblk = pl.BlockSpec((1, LANES, 4 * LANES), lambda cb, b: (cb, 0, 0))
    gbias = pl.BlockSpec((1, 1, 4 * LANES), lambda cb, b: (cb, 0, 0))
    return seq, cw, vec1, vec2, wblk, gbias


def _rnn_fwd(xr, xg, conv_w, conv_b, wblk, gbias, lam):
    n = xr.shape[0]
    t = _t_pad()
    seq, cw, vec1, vec2, wspec, gspec = _rnn_specs(t)

    def body(xr_ref, xg_ref, cw_ref, cb_ref, w_ref, gb_ref, lam_ref, o_ref, a_s, b_s, h_s, p_s):
        rows = lax.broadcasted_iota(jnp.int32, (t, LANES), 0)
        valid = rows >= PAD_ROWS
        xc = _conv(xr_ref[...], cw_ref, cb_ref, rows, t)
        gates = _dot(xc.astype(BF16), w_ref[0]) + gb_ref[0]
        for d in range(2):
            _, _, _, a, _, _, b = _lru_gates(xc, gates, lam_ref, valid, d)
            a_s[d] = a
            b_s[d] = b
        _scan([(a_s.at[d], b_s.at[d], h_s.at[d], p_s.at[d], d == 1) for d in range(2)], t)
        g, _ = _gelu_parts(xg_ref[...])
        o_ref[...] = (h_s[0] + h_s[1]) * g

    return pl.pallas_call(
        body, name="rnn_fwd", grid=(D_RNN // LANES, n // t),
        in_specs=[seq, seq, cw, vec1, wspec, gspec, vec2], out_specs=seq,
        out_shape=jax.ShapeDtypeStruct((n, D_RNN), F32),
        scratch_shapes=[pltpu.VMEM((2, t, LANES), F32)] * 4,
        compiler_params=pltpu.CompilerParams(dimension_semantics=("parallel", "parallel"), vmem_limit_bytes=VMEM_LIMIT),
    )(xr, xg, conv_w, conv_b, wblk, gbias, lam)


def _rnn_bwd(xr, xg, do, conv_w, conv_b, wblk, gbias, lam):
    n = xr.shape[0]
    t = _t_pad()
    seq, cw, vec1, vec2, wspec, gspec = _rnn_specs(t)

    def body(xr_ref, xg_ref, do_ref, cw_ref, cb_ref, w_ref, gb_ref, lam_ref,
             dxr_ref, dxg_ref, dcw_ref, dcb_ref, dw_ref, dgb_ref, dlam_ref,
             a_s, b_s, h_s, l_s, p_s, back_s, r_s, i_s, q_s, dg_s):
        first = pl.program_id(1) == 0
        rows = lax.broadcasted_iota(jnp.int32, (t, LANES), 0)
        valid = rows >= PAD_ROWS
        xr = xr_ref[...]
        xc = _conv(xr, cw_ref, cb_ref, rows, t)
        xcb = xc.astype(BF16)
        gates = _dot(xcb, w_ref[0]) + gb_ref[0]
        sps = []
        for d in range(2):
            r_s[d], i_s[d], sp, a_s[d], _, q_s[d], b_s[d] = _lru_gates(xc, gates, lam_ref, valid, d)
            sps.append(sp)
        _scan([(a_s.at[d], b_s.at[d], h_s.at[d], p_s.at[d], d == 1) for d in range(2)], t)
        g, dg = _gelu_parts(xg_ref[...])
        do = do_ref[...]
        dxg_ref[...] = do * (h_s[0] + h_s[1]) * dg
        b_s[0] = do * g
        for d in range(2):
            back_s[d] = _shift_rows(a_s[d], -1 if d == 0 else 1, rows, t)
        _scan([(back_s.at[d], b_s.at[0], l_s.at[d], p_s.at[d], d == 0) for d in range(2)], t)
        dxc = jnp.zeros((t, LANES), F32)
        dlams = []
        for d in range(2):
            r, i, sp, a, sq = r_s[d], i_s[d], sps[d], a_s[d], q_s[d]
            lam_t = l_s[d]
            da = lam_t * _shift_rows(h_s[d], 1 if d == 0 else -1, rows, t)
            lam_v = jnp.where(valid, lam_t, 0.0)
            dsq = lam_v * (i * xc)
            di = lam_v * sq * xc
            dxc = dxc + lam_v * sq * i
            dm = jnp.where(sq > 0.0, dsq * 0.5 / jnp.where(sq > 0.0, sq, 1.0), 0.0)
            dla = da * a - 2.0 * dm * a * a
            dr = dla * (-LRU_C) * sp
            dsp = _colsum(dla * (-LRU_C) * r)
            dlams.append(dsp * -jax.nn.sigmoid(-lam_ref[d:d + 1, :]))
            dg_s[:, (2 * d) * LANES:(2 * d + 1) * LANES] = (dr * r * (1.0 - r)).astype(BF16)
            dg_s[:, (2 * d + 1) * LANES:(2 * d + 2) * LANES] = (di * i * (1.0 - i)).astype(BF16)
        dgates = dg_s[...]
        dxc = dxc + _dot_nt(dgates, w_ref[0])
        taps = [_shift_rows(dxc, j - 2, rows, t) for j in range(4)]
        dxr_ref[...] = (cw_ref[0:1, :] * taps[0] + cw_ref[1:2, :] * taps[1] + cw_ref[2:3, :] * taps[2]
                        + cw_ref[3:4, :] * taps[3])
        dcw = jnp.concatenate([_colsum(tap * xr) for tap in taps], axis=0)
        _acc(dcw_ref, first, dcw)
        _acc(dcb_ref, first, _colsum(dxc))
        _acc(dw_ref, first, _dot_tn(xcb, dgates)[None])
        _acc(dgb_ref, first, _colsum(dgates.astype(F32))[None])
        _acc(dlam_ref, first, jnp.concatenate(dlams, axis=0))

    return pl.pallas_call(
        body, name="rnn_bwd", grid=(D_RNN // LANES, n // t),
        in_specs=[seq, seq, seq, cw, vec1, wspec, gspec, vec2],
        out_specs=[seq, seq, cw, vec1, wspec, gspec, vec2],
        out_shape=[jax.ShapeDtypeStruct((n, D_RNN), F32), jax.ShapeDtypeStruct((n, D_RNN), F32),
                   jax.ShapeDtypeStruct((4, D_RNN), F32), jax.ShapeDtypeStruct((1, D_RNN), F32),
                   jax.ShapeDtypeStruct((D_RNN // LANES, LANES, 4 * LANES), F32),
                   jax.ShapeDtypeStruct((D_RNN // LANES, 1, 4 * LANES), F32), jax.ShapeDtypeStruct((2, D_RNN), F32)],
        scratch_shapes=[pltpu.VMEM((2, t, LANES), F32)] * 9 + [pltpu.VMEM((t, 4 * LANES), BF16)],
        compiler_params=pltpu.CompilerParams(dimension_semantics=("parallel", "arbitrary"), vmem_limit_bytes=VMEM_LIMIT),
    )(xr, xg, do, conv_w, conv_b, wblk, gbias, lam)


def _post(oa, orn, h0, tgt, ga, gr, g2, w_out, w_gate, w_up, w_down):
    n = oa.shape[0]
    tm = _row_tile(n)
    t = _t_pad()

    def body(oa_ref, or_ref, h0_ref, tgt_ref, ga_ref, gr_ref, g2_ref, wo_ref, wg_ref, wu_ref, wd_ref,
             doa_ref, dor_ref, dh1_ref, mix_ref, h1n_ref, act_ref, dgate_ref, dup_ref, dy_ref,
             loss_ref, dga_ref, dgr_ref, dg2_ref, gate_s, up_s):
        first = pl.program_id(0) == 0
        xa, ra = _rms(oa_ref[...], D_ATTN)
        xr, rr = _rms(or_ref[...], D_RNN)
        mix = jnp.concatenate([(xa * ga_ref[...]).astype(BF16), (xr * gr_ref[...]).astype(BF16)], axis=-1)
        mix_ref[...] = mix.T
        h1 = h0_ref[...] + _dot(mix, wo_ref[...])
        x2, r2 = _rms(h1, D_MODEL)
        h1n = (x2 * g2_ref[...]).astype(BF16)
        h1n_ref[...] = h1n
        y = h1
        for cs in range(0, D_FF, FF_CHUNK):
            sl = slice(cs, cs + FF_CHUNK)
            gate = _dot_nt(h1n, wg_ref[sl, :])
            up = _dot_nt(h1n, wu_ref[sl, :])
            gate_s[:, sl] = gate
            up_s[:, sl] = up
            act = (gate * _sigmoid(gate) * up).astype(BF16)
            act_ref[sl, :] = act.T
            y = y + _dot(act, wd_ref[sl, :])
        row = pl.program_id(0) * tm + lax.broadcasted_iota(jnp.int32, (tm, 1), 0)
        for _ in range(1, n // t):
            row = jnp.where(row >= t, row - t, row)
        err = jnp.where(row >= PAD_ROWS + N_META, y - tgt_ref[...], 0.0)
        _acc(loss_ref, first, jnp.full((1, LANES), 0.5 / D_MODEL, F32) * jnp.sum(err * err))
        dy = err * (1.0 / D_MODEL)
        dyb = dy.astype(BF16)
        dy_ref[...] = dyb
        dh1n = jnp.zeros((tm, D_MODEL), F32)
        for cs in range(0, D_FF, FF_CHUNK):
            sl = slice(cs, cs + FF_CHUNK)
            dact = _dot_nt(dyb, wd_ref[sl, :])
            gate, up = gate_s[:, sl], up_s[:, sl]
            sg = _sigmoid(gate)
            dgate = (dact * up * sg * (1.0 + gate * (1.0 - sg))).astype(BF16)
            dup = (dact * gate * sg).astype(BF16)
            dgate_ref[sl, :] = dgate.T
            dup_ref[sl, :] = dup.T
            dh1n = dh1n + _dot(dgate, wg_ref[sl, :]) + _dot(dup, wu_ref[sl, :])
        _acc(dg2_ref, first, _colsum(dh1n * x2))
        dh1 = dy + _rms_bwd(dh1n, x2, r2, g2_ref[...], D_MODEL)
        dh1_ref[...] = dh1
        dmix = _dot_nt(dh1.astype(BF16), wo_ref[...])
        dma, dmr = dmix[:, :D_ATTN], dmix[:, D_ATTN:]
        _acc(dga_ref, first, _colsum(dma * xa))
        _acc(dgr_ref, first, _colsum(dmr * xr))
        doa_ref[...] = _rms_bwd(dma, xa, ra, ga_ref[...], D_ATTN)
        dor_ref[...] = _rms_bwd(dmr, xr, rr, gr_ref[...], D_RNN)

    def row(w):
        return pl.BlockSpec((tm, w), lambda i: (i, 0))

    def acc(w):
        return pl.BlockSpec((1, w), lambda i: (0, 0))

    def col(w):
        return pl.BlockSpec((w, tm), lambda i: (0, i))

    outs = [(D_ATTN, F32, row), (D_RNN, F32, row), (D_MODEL, F32, row), (D_MODEL, BF16, col), (D_MODEL, BF16, row),
            (D_FF, BF16, col), (D_FF, BF16, col), (D_FF, BF16, col), (D_MODEL, BF16, row)]
    accs = [LANES, D_ATTN, D_RNN, D_MODEL]
    return pl.pallas_call(
        body, name="post", grid=(n // tm,),
        in_specs=[row(D_ATTN), row(D_RNN), row(D_MODEL), row(D_MODEL),
                  _const_spec((1, D_ATTN)), _const_spec((1, D_RNN)), _const_spec((1, D_MODEL)),
                  _const_spec((D_MODEL, D_MODEL)), _const_spec((D_FF, D_MODEL)), _const_spec((D_FF, D_MODEL)),
                  _const_spec((D_FF, D_MODEL))],
        out_specs=[spec(w) for w, _, spec in outs] + [acc(w) for w in accs],
        out_shape=[jax.ShapeDtypeStruct((n, w) if spec is row else (w, n), dt) for w, dt, spec in outs]
        + [jax.ShapeDtypeStruct((1, w), F32) for w in accs],
        scratch_shapes=[pltpu.VMEM((tm, D_FF), F32), pltpu.VMEM((tm, D_FF), F32)],
        compiler_params=pltpu.CompilerParams(dimension_semantics=("arbitrary",), vmem_limit_bytes=VMEM_LIMIT),
    )(oa, orn, h0, tgt, ga, gr, g2, w_out, w_gate, w_up, w_down)


def _in_bwd(dp, h0, dh1, ln1_g, w_in_p, srcs=(), scatter=()):
    n = h0.shape[0]
    tm = _row_tile(n)
    nk = len(srcs)
    c_in, c_out, c_shape, c_sems = _exchange_specs(srcs, scatter)

    def body(dp_ref, h0_ref, dh1_ref, g_ref, w_ref, *rest):
        dh0_ref, dg_ref = rest[nk:nk + 2]
        finish = _ride(1, *_exchange_fns(rest[:nk], rest[nk + 2:2 * nk + 2], rest[2 * nk + 2:], scatter))
        dhn = _dot(dp_ref[...], w_ref[...])
        xhat, r = _rms(h0_ref[...], D_MODEL)
        _acc(dg_ref, pl.program_id(0) == 0, _colsum(dhn * xhat))
        dh0_ref[...] = dh1_ref[...] + _rms_bwd(dhn, xhat, r, g_ref[...], D_MODEL)
        finish()

    def row(w):
        return pl.BlockSpec((tm, w), lambda i: (i, 0))

    res = pl.pallas_call(
        body, name="in_bwd", grid=(n // tm,),
        in_specs=[row(P_COLS), row(D_MODEL), row(D_MODEL), _const_spec((1, D_MODEL)), _const_spec((P_COLS, D_MODEL))] + c_in,
        out_specs=[row(D_MODEL), pl.BlockSpec((1, D_MODEL), lambda i: (0, 0))] + c_out,
        out_shape=[jax.ShapeDtypeStruct((n, D_MODEL), F32), jax.ShapeDtypeStruct((1, D_MODEL), F32)] + c_shape,
        scratch_shapes=c_sems,
        compiler_params=pltpu.CompilerParams(dimension_semantics=("arbitrary",), vmem_limit_bytes=VMEM_LIMIT),
    )(dp, h0, dh1, ln1_g, w_in_p, *srcs)
    return res[:2], res[2:]


MAX_TILE = FF_CHUNK


def _pick_tile(width, cap):
    best = LANES
    for mult in range(1, width // LANES + 1):
        cand = mult * LANES
        if width % cand == 0 and cand <= cap:
            best = cand
    return best


def _matmul_tn(name, a, b, srcs=(), scatter=()):
    n, ka = a.shape
    kb = b.shape[1]
    ta, tb = _pick_tile(ka, MAX_TILE), _pick_tile(kb, MAX_TILE)
    tk = n // 4
    nk = len(srcs)
    c_in, c_out, c_shape, c_sems = _exchange_specs(srcs, scatter)

    def body(a_ref, b_ref, *rest):
        o_ref = rest[nk]
        finish = _ride(3, *_exchange_fns(rest[:nk], rest[nk + 1:2 * nk + 1], rest[2 * nk + 1:], scatter))
        _acc(o_ref, pl.program_id(2) == 0, _dot_tn(a_ref[...].astype(BF16), b_ref[...].astype(BF16)))
        finish()

    res = pl.pallas_call(
        body, name=name, grid=(ka // ta, kb // tb, n // tk),
        in_specs=[pl.BlockSpec((tk, ta), lambda i, j, k: (k, i)), pl.BlockSpec((tk, tb), lambda i, j, k: (k, j))] + c_in,
        out_specs=[pl.BlockSpec((ta, tb), lambda i, j, k: (i, j))] + c_out,
        out_shape=[jax.ShapeDtypeStruct((ka, kb), F32)] + c_shape, scratch_shapes=c_sems,
        compiler_params=pltpu.CompilerParams(dimension_semantics=("arbitrary", "arbitrary", "arbitrary"),
                                             vmem_limit_bytes=VMEM_LIMIT),
    )(a, b, *srcs)
    return res[0], res[1:]


def _matmul_shards(name, at, b):
    ka, n = at.shape
    kb = b.shape[1]
    ta, tb = _pick_tile(ka, MAX_TILE), _pick_tile(kb, MAX_TILE)
    tk = n // 2
    width = ka // N_DEV
    per = ta // width

    def body(a_ref, b_ref, o_ref, acc_ref):
        _acc(acc_ref, pl.program_id(2) == 0, _dot(a_ref[...], b_ref[...].astype(BF16)))

        @pl.when(pl.program_id(2) == pl.num_programs(2) - 1)
        def _():
            for s in range(per):
                o_ref[s] = acc_ref[s * width:(s + 1) * width, :].astype(BF16)

    return pl.pallas_call(
        body, name=name, grid=(ka // ta, kb // tb, n // tk),
        in_specs=[pl.BlockSpec((ta, tk), lambda i, j, k: (i, k)), pl.BlockSpec((tk, tb), lambda i, j, k: (k, j))],
        out_specs=pl.BlockSpec((per, width, tb), lambda i, j, k: (i, 0, j)),
        out_shape=jax.ShapeDtypeStruct((N_DEV, width, kb), BF16),
        scratch_shapes=[pltpu.VMEM((ta, tb), F32)],
        compiler_params=pltpu.CompilerParams(dimension_semantics=("parallel", "parallel", "arbitrary"),
                                             vmem_limit_bytes=VMEM_LIMIT),
    )(at, b)


def _adamw_math(g8_ref, w_ref, m_ref, v_ref, g_ref, d_ref, nm_ref, nv_ref):
    g = g8_ref[0].astype(F32)
    for s in range(1, N_DEV):
        g = g + g8_ref[s].astype(F32)
    g_ref[...] = g
    nm = ADAM_B1 * m_ref[...] + (1.0 - ADAM_B1) * g
    nv = ADAM_B2 * v_ref[...] + (1.0 - ADAM_B2) * (g * g)
    nm_ref[...] = nm
    nv_ref[...] = nv
    m_hat = nm / (1.0 - ADAM_B1 ** ADAM_STEP)
    v_hat = nv / (1.0 - ADAM_B2 ** ADAM_STEP)
    d_ref[...] = -ADAM_LR * (m_hat / (jnp.sqrt(v_hat) + ADAM_EPS) + ADAM_WD * w_ref[...])


def _adamw_many(name, items):
    count = len(items)

    def body(*refs):
        ins, outs = refs[:4 * count], refs[4 * count:]
        for i in range(count):
            _adamw_math(*ins[4 * i:4 * i + 4], *outs[4 * i:4 * i + 4])

    flat = [a for item in items for a in item]
    res = pl.pallas_call(
        body, name=name,
        out_shape=[jax.ShapeDtypeStruct(item[1].shape, F32) for item in items for _ in range(4)],
        compiler_params=pltpu.CompilerParams(vmem_limit_bytes=VMEM_LIMIT),
    )(*flat)
    return [tuple(res[4 * i:4 * i + 4]) for i in range(count)]


def _adamw(name, g8, w, m, v):
    rows, cols = w.shape
    tr = rows
    for cand in (256, 176, 128, 64):
        if rows % cand == 0 and rows > cand:
            tr = cand
            break

    def body(*refs):
        _adamw_math(*refs)

    blk = pl.BlockSpec((tr, cols), lambda i: (i, 0))
    return pl.pallas_call(
        body, name=name, grid=(rows // tr,),
        in_specs=[pl.BlockSpec((N_DEV, tr, cols), lambda i: (0, i, 0)), blk, blk, blk],
        out_specs=[blk] * 4, out_shape=[jax.ShapeDtypeStruct((rows, cols), F32)] * 4,
        compiler_params=pltpu.CompilerParams(dimension_semantics=("parallel",), vmem_limit_bytes=VMEM_LIMIT),
    )(g8, w, m, v)


def _exchange_specs(srcs, scatter):
    nk = len(srcs)
    if not nk:
        return [], [], [], []
    any_spec = pl.BlockSpec(memory_space=pl.ANY)
    out_shape = [jax.ShapeDtypeStruct(s.shape if sc else (N_DEV,) + s.shape, s.dtype) for s, sc in zip(srcs, scatter)]
    sems = [pltpu.SemaphoreType.DMA((nk, N_DEV - 1)), pltpu.SemaphoreType.DMA((nk, N_DEV - 1)),
            pltpu.SemaphoreType.DMA((nk,))]
    return [any_spec] * nk, [any_spec] * nk, out_shape, sems


FLIPS = ((0, 0, 1), (1, 0, 0), (0, 1, 0), (1, 1, 0), (1, 0, 1), (0, 1, 1), (1, 1, 1))
N_CHIP_PEERS = 3


def _exchange_fns(src_refs, out_refs, sems, scatter):
    nk = len(src_refs)
    if not nk:
        return (lambda: None), (lambda: None), (lambda: None)
    send_sems, recv_sems, local_sems = sems
    first = 1 + N_CHIP_PEERS

    def plan():
        x, y, c = lax.axis_index("x"), lax.axis_index("y"), lax.axis_index("c")
        me = 4 * x + 2 * y + c
        peers = [(1 - x if fx else x, 1 - y if fy else y, 1 - c if fc else c) for fx, fy, fc in FLIPS]
        pids = [4 * px + 2 * py + pc for px, py, pc in peers]

        def remote(k, j, src, dst, to):
            return pltpu.make_async_remote_copy(src_ref=src, dst_ref=dst, send_sem=send_sems.at[k, j],
                                                recv_sem=recv_sems.at[k, j], device_id=to, device_id_type=MESH)

        def mine(k, dest):
            return src_refs[k].at[dest] if scatter[k] else src_refs[k]

        local = [pltpu.make_async_copy(mine(k, me), out_refs[k].at[me], local_sems.at[k]) for k in range(nk)]
        direct = [remote(k, j, mine(k, pids[j]), out_refs[k].at[me], peers[j])
                  for k in range(nk) for j in range(len(FLIPS) if scatter[k] else first)]
        relays = {(k, j): remote(k, j, out_refs[k].at[pids[j - N_CHIP_PEERS]], out_refs[k].at[pids[j - N_CHIP_PEERS]], peers[0])
                  for k in range(nk) if not scatter[k] for j in range(first, len(FLIPS))}
        arrivals = {(k, j): remote(k, j, out_refs[k].at[pids[j]], out_refs[k].at[pids[j]], peers[j])
                    for k in range(nk) for j in range(len(FLIPS))}
        return local, direct, relays, arrivals

    def start():
        local, direct, _, _ = plan()
        for cp in local + direct:
            cp.start()

    def relay():
        _, _, relays, arrivals = plan()
        for (k, j), cp in relays.items():
            arrivals[k, j - N_CHIP_PEERS].wait_recv()
            cp.start()

    def wait():
        local, direct, relays, arrivals = plan()
        for (k, j), cp in arrivals.items():
            if (k, j + N_CHIP_PEERS) not in relays:
                cp.wait_recv()
        for cp in direct + list(relays.values()):
            cp.wait_send()
        for cp in local:
            cp.wait()

    return start, relay, wait


def _grid_step(rank):
    step, total = 0, 1
    for axis in range(rank):
        step = step * pl.num_programs(axis) + pl.program_id(axis)
        total = total * pl.num_programs(axis)
    return step, total


def _ride(rank, start, relay, wait):
    step, total = _grid_step(rank)
    pl.when(step == 0)(start)
    pl.when(step == (3 * total) // 4)(relay)
    return lambda: pl.when(step == total - 1)(wait)


def _exchange(name, srcs, scatter):
    nk = len(srcs)
    c_in, c_out, c_shape, c_sems = _exchange_specs(srcs, scatter)

    def body(*refs):
        start, relay, wait = _exchange_fns(refs[:nk], refs[nk:2 * nk], refs[2 * nk:], scatter)
        start()
        relay()
        wait()

    return pl.pallas_call(body, name=name, in_specs=c_in, out_specs=c_out, out_shape=c_shape, scratch_shapes=c_sems)(*srcs)


def _cols_from_shards(g):
    return jnp.transpose(g, (1, 0, 2)).reshape(g.shape[1], -1)


def _cols_to_shards(w):
    return jnp.transpose(w.reshape(w.shape[0], N_DEV, -1), (1, 0, 2))


def _prep(x, tgt, srcs, scatter):
    nb = x.shape[0]
    t = _t_pad()
    head = PAD_ROWS + N_META
    nk = len(srcs)
    c_in, c_out, c_shape, c_sems = _exchange_specs(srcs, scatter)

    def body(x_ref, tgt_ref, *rest):
        h0_ref, tp_ref = rest[nk:nk + 2]
        finish = _ride(1, *_exchange_fns(rest[:nk], rest[nk + 2:2 * nk + 2], rest[2 * nk + 2:], scatter))
        lead = pl.program_id(0) == 0

        @pl.when(lead)
        def _():
            h0_ref[...] = jnp.zeros_like(h0_ref)
            tp_ref[...] = jnp.zeros_like(tp_ref)

        @pl.when(jnp.logical_not(lead))
        def _():
            h0_ref[...] = x_ref[...]
            tp_ref[...] = tgt_ref[...]

        finish()

    src = pl.BlockSpec((nb, head, D_MODEL), lambda j: (0, jnp.maximum(j - 1, 0), 0))
    dst = pl.BlockSpec((nb, head, D_MODEL), lambda j: (0, j, 0))
    padded = jax.ShapeDtypeStruct((nb, t, D_MODEL), F32)
    res = pl.pallas_call(
        body, name="prep", grid=(t // head,), in_specs=[src, src] + c_in, out_specs=[dst, dst] + c_out,
        out_shape=[padded, padded] + c_shape, scratch_shapes=c_sems,
        compiler_params=pltpu.CompilerParams(dimension_semantics=("arbitrary",)),
    )(x, tgt, *srcs)
    return res[0], res[1], res[2:]


def _rope_tables(n):
    t = _t_pad()
    pos = np.arange(t, dtype=np.float32) - np.float32(PAD_ROWS)
    half = QK_ROPE // 2
    freqs = (1.0 / (ROPE_THETA ** (np.arange(half, dtype=np.float32) / half))).astype(np.float32)
    ang = pos[:, None] * freqs[None, :]
    cos, sin = np.cos(ang), np.sin(ang)
    z = lambda w: np.zeros((t, w), np.float32)
    c = np.concatenate([np.ones((t, QK_NOPE), np.float32), cos, cos, z(HEAD_PAD - QK_HEAD)], axis=1)
    s1 = np.concatenate([z(QK_NOPE + half), sin, z(HEAD_PAD - QK_HEAD)], axis=1)
    s2 = np.concatenate([z(QK_NOPE), -sin, z(HEAD_PAD - QK_NOPE - half)], axis=1)
    return tuple(jnp.asarray(np.tile(a, (n // t, 1))) for a in (c, s1, s2))


def _block_diag_gates(lru_wa, lru_wi):
    eye = jnp.eye(2, dtype=lru_wa.dtype)

    def bd(w):
        w = w.reshape(2, D_RNN // LANES, 2, RNN_BW, RNN_BW)
        full = w[:, :, :, :, None, :] * eye[None, None, :, None, :, None]
        return full.reshape(2, D_RNN // LANES, LANES, LANES)

    a, i = bd(lru_wa), bd(lru_wi)
    return jnp.concatenate([a[0], i[0], a[1], i[1]], axis=-1)


def _unblock_gates(dw):
    nb = D_RNN // LANES
    parts = dw.reshape(nb, 2, RNN_BW, 4, 2, RNN_BW)
    diag = jnp.stack([parts[:, k, :, :, k, :] for k in range(2)], axis=1)
    diag = jnp.transpose(diag, (3, 0, 1, 2, 4)).reshape(4, 2 * nb, RNN_BW, RNN_BW)
    return jnp.stack([diag[0], diag[2]]), jnp.stack([diag[1], diag[3]])


WEIGHTS = ("meta_tokens", "ln1_g", "w_in", "q_a_norm_g", "w_uq", "kv_a_norm_g", "w_ukv", "q_norm_g", "k_norm_g",
           "conv_w", "conv_b", "lru_wa", "lru_ba", "lru_wi", "lru_bi", "lru_lambda", "attn_out_g", "rnn_out_g",
           "w_out", "ln2_g", "w_gate", "w_up", "w_down")
BIG = ("w_in", "w_uq", "w_ukv", "w_out", "w_gate", "w_up", "w_down")
TRANSPOSED = ("w_in", "w_uq", "w_gate", "w_up")
ROW_SHARDED = ("w_out", "w_down") + TRANSPOSED
REPLICATED = ("ln1_g", "q_a_norm_g", "kv_a_norm_g", "q_norm_g", "k_norm_g", "conv_b", "lru_wa", "lru_wi",
              "attn_out_g", "rnn_out_g", "ln2_g")
WHOLE = REPLICATED + ("loss",)
G_FIRST = ("w_in", "meta_tokens")
G_MID = ("w_uq", "w_ukv", "conv_w", "lru_ba", "lru_bi", "lru_lambda")
LATE = ("w_out", "w_gate", "w_up", "w_down")
G_LAST = ("meta_tokens", "ln1_g")


def _local_step(x, tgt, ex):
    nb = x.shape[0]
    t = _t_pad()
    n = nb * t
    local = ex.local
    h0, tgt_p, got = _prep(x, tgt, *ex.gather_srcs(G_FIRST))
    first = ex.gathered(G_FIRST, got)
    meta, w_in = first["meta_tokens"], first["w_in"]
    h0 = h0.at[:, PAD_ROWS:PAD_ROWS + N_META].set(jnp.broadcast_to(meta[None], (nb, N_META, D_MODEL))).reshape(n, D_MODEL)
    tgt_p = tgt_p.reshape(n, D_MODEL)

    zr = lambda r: jnp.zeros((r, D_MODEL), w_in.dtype)
    w_in_p = jnp.concatenate([w_in[:OFF_CKV], w_in[OFF_KR:], zr(QK_NOPE), w_in[OFF_CKV:OFF_KR], zr(HEAD_PAD - QK_HEAD)],
                             axis=0)
    pad_g = lambda g: jnp.pad(g, ((0, 0), (0, HEAD_PAD - QK_HEAD)))
    qg, kg = pad_g(local["q_norm_g"]), pad_g(local["k_norm_g"])
    rc, rs1, rs2 = _rope_tables(n)
    wblk = _block_diag_gates(local["lru_wa"].reshape(2, -1, RNN_BW, RNN_BW),
                             local["lru_wi"].reshape(2, -1, RNN_BW, RNN_BW)).astype(BF16)
    nblk = D_RNN // LANES

    (hn, cq, ckv, xr, xg, kr), got = _in_proj(h0, local["ln1_g"], w_in_p, *ex.gather_srcs(G_MID))
    w = ex.gathered(G_MID, got)
    w_uq_p = jnp.pad(w["w_uq"].reshape(N_HEADS, QK_HEAD, Q_LORA), ((0, 0), (0, HEAD_PAD - QK_HEAD), (0, 0))
                     ).reshape(QP_COLS, Q_LORA)
    ukv = w["w_ukv"].reshape(KV_LORA, N_HEADS, QK_NOPE + V_HEAD)
    w_uk_p = jnp.pad(ukv[:, :, :QK_NOPE], ((0, 0), (0, 0), (0, HEAD_PAD - QK_NOPE))).reshape(KV_LORA, QP_COLS)
    w_v = ukv[:, :, QK_NOPE:].reshape(KV_LORA, D_ATTN)
    gbias = jnp.stack([w["lru_ba"][0], w["lru_bi"][0], w["lru_ba"][1], w["lru_bi"][1]], axis=0)
    gbias = jnp.transpose(gbias.reshape(4, nblk, LANES), (1, 0, 2)).reshape(nblk, 1, 4 * LANES)

    q, k, v = _qkv_fwd(cq, ckv, kr, local["q_a_norm_g"], local["kv_a_norm_g"], w_uq_p, w_uk_p, w_v, qg, kg, rc, rs1, rs2)
    oa, got = _attn_fwd(q, k, v, *ex.gather_srcs(LATE))
    late = ex.gathered(LATE, got)
    orn = _rnn_fwd(xr, xg, w["conv_w"], local["conv_b"], wblk, gbias, w["lru_lambda"])
    (doa, dor, dh1, mix_t, h1n, act_t, dgate_t, dup_t, dyb, loss, dga, dgr, dg2) = _post(
        oa, orn, h0, tgt_p, local["attn_out_g"], local["rnn_out_g"], local["ln2_g"], late["w_out"], late["w_gate"],
        late["w_up"], late["w_down"])
    wire = {"w_out": _matmul_shards("dw_out", mix_t, dh1), "w_gate": _matmul_shards("dw_gate", dgate_t, h1n),
            "w_up": _matmul_shards("dw_up", dup_t, h1n), "w_down": _matmul_shards("dw_down", act_t, dyb)}
    dxr, dxg, dcw, dcb, dwblk, dgb, dlam = _rnn_bwd(xr, xg, dor, w["conv_w"], local["conv_b"], wblk, gbias, w["lru_lambda"])
    dwa, dwi = _unblock_gates(dwblk)
    dgb = jnp.transpose(dgb.reshape(nblk, 4, LANES), (1, 0, 2)).reshape(4, D_RNN)
    wire.update(ex.to_wire({
        "conv_w": dcw, "conv_b": dcb, "lru_wa": dwa.reshape(-1, RNN_BW), "lru_ba": jnp.stack([dgb[0], dgb[2]]),
        "lru_wi": dwi.reshape(-1, RNN_BW), "lru_bi": jnp.stack([dgb[1], dgb[3]]), "lru_lambda": dlam,
        "attn_out_g": dga, "rnn_out_g": dgr, "ln2_g": dg2, "loss": loss}))
    names = tuple(wire)
    (dq_r, dk_r, dv), got = _attn_bwd(q, k, v, doa, oa, *ex.scatter_srcs(names, wire))
    summed = ex.scattered(names, wire, got)
    (dp, qa, kva, dqp, dkv, dqg, dkg, dgqa, dgkva) = _qkv_bwd(
        cq, ckv, kr, dq_r, dk_r, dv, dxr, dxg, local["q_a_norm_g"], local["kv_a_norm_g"], w_uq_p, w_uk_p, w_v, qg, kg,
        rc, rs1, rs2)
    dw_uq_p, _ = _matmul_tn("dw_uq", dqp, qa)
    dw_kv, _ = _matmul_tn("dw_ukv", kva, dkv)
    dw_uq = dw_uq_p.reshape(N_HEADS, HEAD_PAD, Q_LORA)[:, :QK_HEAD].reshape(N_HEADS * QK_HEAD, Q_LORA)
    dw_ukv = jnp.concatenate([dw_kv[:, :QP_COLS].reshape(KV_LORA, N_HEADS, HEAD_PAD)[:, :, :QK_NOPE],
                              dw_kv[:, QP_COLS:].reshape(KV_LORA, N_HEADS, V_HEAD)], axis=2).reshape(KV_LORA, -1)
    wire = ex.to_wire({"q_a_norm_g": dgqa, "w_uq": dw_uq, "kv_a_norm_g": dgkva, "w_ukv": dw_ukv,
                       "q_norm_g": dqg[:, :QK_HEAD], "k_norm_g": dkg[:, :QK_HEAD]})
    names = tuple(wire)
    dw_in_p, got = _matmul_tn("dw_in", dp, hn, *ex.scatter_srcs(names, wire))
    summed.update(ex.scattered(names, wire, got))
    kr0 = OFF_CKV + 2 * D_RNN + QK_NOPE
    dw_in = jnp.concatenate([dw_in_p[:OFF_CKV], dw_in_p[kr0:kr0 + QK_ROPE], dw_in_p[OFF_CKV:OFF_CKV + 2 * D_RNN]], axis=0)
    wire = ex.to_wire({"w_in": dw_in})
    (dh0, dg1), got = _in_bwd(dp, h0, dh1, local["ln1_g"], w_in_p, *ex.scatter_srcs(("w_in",), wire))
    summed.update(ex.scattered(("w_in",), wire, got))

    dh0 = dh0.reshape(nb, t, D_MODEL)
    wire = ex.to_wire({"meta_tokens": jnp.sum(dh0[:, PAD_ROWS:PAD_ROWS + N_META], axis=0), "ln1_g": dg1})
    got = ex.run("reduce_last", *ex.scatter_srcs(G_LAST, wire))
    summed.update(ex.scattered(G_LAST, wire, got))
    return dh0[:, PAD_ROWS + N_META:], summed


class _MeshExchange:
    def __init__(self, shards):
        self.local = shards

    @staticmethod
    def run(name, srcs, scatter):
        return _exchange(name, srcs, scatter)

    def gather_srcs(self, names):
        return [self.local[k].astype(BF16) if k in BIG else self.local[k] for k in names], [False] * len(names)

    @staticmethod
    def gathered(names, outs):
        return {k: g.reshape(-1, g.shape[-1]) if k in ROW_SHARDED else _cols_from_shards(g) for k, g in zip(names, outs)}

    @staticmethod
    def to_wire(grads):
        wire = {}
        for k, g in grads.items():
            if k in WHOLE:
                wire[k] = g
            elif k in ROW_SHARDED:
                wire[k] = g.reshape(N_DEV, -1, g.shape[-1]).astype(BF16)
            else:
                wire[k] = _cols_to_shards(g).astype(BF16) if k in BIG else _cols_to_shards(g)
        return wire

    @staticmethod
    def scatter_srcs(names, wire):
        return [wire[k] for k in names], [k not in WHOLE for k in names]

    @staticmethod
    def scattered(names, wire, outs):
        return dict(zip(names, outs))


def kernel(x, meta_tokens, ln1_g, w_in, q_a_norm_g, w_uq, kv_a_norm_g, w_ukv, q_norm_g, k_norm_g, conv_w, conv_b, lru_wa, lru_ba, lru_wi, lru_bi, lru_lambda, attn_out_g, rnn_out_g, w_out, ln2_g, w_gate, w_up, w_down, loss_target, m_meta_tokens, m_ln1_g, m_w_in, m_q_a_norm_g, m_w_uq, m_kv_a_norm_g, m_w_ukv, m_q_norm_g, m_k_norm_g, m_conv_w, m_conv_b, m_lru_wa, m_lru_ba, m_lru_wi, m_lru_bi, m_lru_lambda, m_attn_out_g, m_rnn_out_g, m_w_out, m_ln2_g, m_w_gate, m_w_up, m_w_down, v_meta_tokens, v_ln1_g, v_w_in, v_q_a_norm_g, v_w_uq, v_kv_a_norm_g, v_w_ukv, v_q_norm_g, v_k_norm_g, v_conv_w, v_conv_b, v_lru_wa, v_lru_ba, v_lru_wi, v_lru_bi, v_lru_lambda, v_attn_out_g, v_rnn_out_g, v_w_out, v_ln2_g, v_w_gate, v_w_up, v_w_down):
    given = (meta_tokens, ln1_g, w_in, q_a_norm_g, w_uq, kv_a_norm_g, w_ukv, q_norm_g, k_norm_g, conv_w, conv_b,
             lru_wa, lru_ba, lru_wi, lru_bi, lru_lambda, attn_out_g, rnn_out_g, w_out, ln2_g, w_gate, w_up, w_down)
    moments_m = (m_meta_tokens, m_ln1_g, m_w_in, m_q_a_norm_g, m_w_uq, m_kv_a_norm_g, m_w_ukv, m_q_norm_g, m_k_norm_g,
                 m_conv_w, m_conv_b, m_lru_wa, m_lru_ba, m_lru_wi, m_lru_bi, m_lru_lambda, m_attn_out_g, m_rnn_out_g,
                 m_w_out, m_ln2_g, m_w_gate, m_w_up, m_w_down)
    moments_v = (v_meta_tokens, v_ln1_g, v_w_in, v_q_a_norm_g, v_w_uq, v_kv_a_norm_g, v_w_ukv, v_q_norm_g, v_k_norm_g,
                 v_conv_w, v_conv_b, v_lru_wa, v_lru_ba, v_lru_wi, v_lru_bi, v_lru_lambda, v_attn_out_g, v_rnn_out_g,
                 v_w_out, v_ln2_g, v_w_gate, v_w_up, v_w_down)
    shapes = {k: a.shape for k, a in zip(WEIGHTS, given)}

    def two_d(k, a):
        a = a.reshape(-1, a.shape[-1])
        return a.T if k in TRANSPOSED else a

    w = {k: two_d(k, a) for k, a in zip(WEIGHTS, given)}
    m = {k: two_d(k, a) for k, a in zip(WEIGHTS, moments_m)}
    v = {k: two_d(k, a) for k, a in zip(WEIGHTS, moments_v)}

    grad_x, parts = _local_step(x, loss_target, _MeshExchange(w))

    new = {k: _adamw("adamw_" + k, parts[k], w[k], m[k], v[k]) for k in BIG}
    small = [k for k in WEIGHTS if k not in BIG]
    new.update(zip(small, _adamw_many("adamw_small", [(parts[k], w[k], m[k], v[k]) for k in small])))

    loss = jnp.sum(parts["loss"][:, 0, 0])
    outs = [loss, grad_x]
    for idx in range(4):
        outs += [(new[k][idx].T if k in TRANSPOSED else new[k][idx]).reshape(shapes[k]) for k in WEIGHTS]
    return tuple(outs)
```

```python
import functools
import math

import numpy as np
import jax
import jax.numpy as jnp
from jax import lax
from jax.experimental import pallas as pl
from jax.experimental.pallas import tpu as pltpu

F32 = jnp.float32
BF16 = jnp.bfloat16

D_MODEL = 1024
N_META = 16
SEQ = 2048
N_HEADS = 8
QK_NOPE = 64
QK_ROPE = 32
QK_HEAD = QK_NOPE + QK_ROPE
V_HEAD = 64
D_ATTN = N_HEADS * V_HEAD
Q_LORA = 384
KV_LORA = 256
D_RNN = 512
RNN_BW = 64
D_FF = 2816
EPS = 1e-6
LRU_C = 8.0
ROPE_THETA = 10000.0
OFF_CKV = Q_LORA + KV_LORA
OFF_KR = OFF_CKV + QK_ROPE
IN_COLS = OFF_KR + 2 * D_RNN

ADAM_LR = 0.001
ADAM_B1 = 0.9
ADAM_B2 = 0.999
ADAM_EPS = 1e-08
ADAM_WD = 0.01
ADAM_STEP = 10

N_DEV = 8
LANES = 128
HEAD_PAD = LANES
PAD_ROWS = LANES - N_META
QP_COLS = N_HEADS * HEAD_PAD
P_COLS = OFF_CKV + 2 * D_RNN + LANES
FF_CHUNK = D_FF // 2
VMEM_LIMIT = 56 * 1024 * 1024
MESH = pl.DeviceIdType.MESH


def _t_pad():
    return PAD_ROWS + N_META + SEQ


def _row_tile(n):
    return 256 if n % 256 == 0 else 128


def _const_spec(shape):
    nd = len(shape)
    return pl.BlockSpec(shape, lambda *_: (0,) * nd, pipeline_mode=pl.Buffered(1))


def _rms(x, d):
    r = lax.rsqrt(jnp.sum(x * x, axis=-1, keepdims=True) * (1.0 / d) + EPS)
    return x * r, r


def _rms_bwd(dy, xhat, r, g, d):
    dxh = dy * g
    return r * (dxh - xhat * (jnp.sum(dxh * xhat, axis=-1, keepdims=True) * (1.0 / d)))


def _colsum(x):
    return jnp.sum(x, axis=0, keepdims=True)


def _dot(a, b):
    return jnp.dot(a, b, preferred_element_type=F32)


def _dot_nt(a, b):
    return lax.dot_general(a, b, (((1,), (1,)), ((), ())), preferred_element_type=F32)


def _dot_tn(a, b):
    return lax.dot_general(a, b, (((0,), (0,)), ((), ())), preferred_element_type=F32)


def _rope(x, c, s1, s2):
    return x * c + pltpu.roll(x, 16, 1) * s1 + pltpu.roll(x, HEAD_PAD - 16, 1) * s2


def _rope_bwd(dy, c, s1, s2):
    return dy * c + pltpu.roll(dy * s1, HEAD_PAD - 16, 1) + pltpu.roll(dy * s2, 16, 1)


def _acc(ref, first, val):
    @pl.when(first)
    def _():
        ref[...] = val

    @pl.when(jnp.logical_not(first))
    def _():
        ref[...] += val


def _in_proj(h0, ln1_g, w_in_p, srcs=(), scatter=()):
    n = h0.shape[0]
    tm = _row_tile(n)
    nk = len(srcs)
    c_in, c_out, c_shape, c_sems = _exchange_specs(srcs, scatter)

    def body(h_ref, g_ref, w_ref, *rest):
        hn_ref, cq_ref, ckv_ref, xr_ref, xg_ref, kr_ref = rest[nk:nk + 6]
        finish = _ride(1, *_exchange_fns(rest[:nk], rest[nk + 6:2 * nk + 6], rest[2 * nk + 6:], scatter))
        xhat, _ = _rms(h_ref[...], D_MODEL)
        hn = (xhat * g_ref[...]).astype(BF16)
        hn_ref[...] = hn
        p = _dot_nt(hn, w_ref[...])
        cq_ref[...] = p[:, :Q_LORA]
        ckv_ref[...] = p[:, Q_LORA:OFF_CKV]
        xr_ref[...] = p[:, OFF_CKV:OFF_CKV + D_RNN]
        xg_ref[...] = p[:, OFF_CKV + D_RNN:OFF_CKV + 2 * D_RNN]
        kr_ref[...] = p[:, OFF_CKV + 2 * D_RNN:]
        finish()

    def row(w):
        return pl.BlockSpec((tm, w), lambda i: (i, 0))

    widths = (D_MODEL, Q_LORA, KV_LORA, D_RNN, D_RNN, LANES)
    res = pl.pallas_call(
        body, name="in_proj", grid=(n // tm,),
        in_specs=[row(D_MODEL), _const_spec((1, D_MODEL)), _const_spec((P_COLS, D_MODEL))] + c_in,
        out_specs=[row(w) for w in widths] + c_out,
        out_shape=[jax.ShapeDtypeStruct((n, w), BF16 if k == 0 else F32) for k, w in enumerate(widths)] + c_shape,
        scratch_shapes=c_sems,
        compiler_params=pltpu.CompilerParams(dimension_semantics=("arbitrary",), vmem_limit_bytes=VMEM_LIMIT),
    )(h0, ln1_g, w_in_p, *srcs)
    return res[:6], res[6:]


def _qkv_fwd(cq, ckv, kr, gqa, gkva, w_uq_p, w_uk_p, w_v, qg, kg, rc, rs1, rs2):
    n = cq.shape[0]
    tm = _row_tile(n)

    def body(cq_ref, ckv_ref, kr_ref, gqa_ref, gkva_ref, wuq_ref, wuk_ref, wv_ref, qg_ref, kg_ref,
             c_ref, s1_ref, s2_ref, q_ref, k_ref, v_ref):
        xq, _ = _rms(cq_ref[...], Q_LORA)
        qa = (xq * gqa_ref[...]).astype(BF16)
        q = _dot_nt(qa, wuq_ref[...])
        xkv, _ = _rms(ckv_ref[...], KV_LORA)
        kva = (xkv * gkva_ref[...]).astype(BF16)
        kn = _dot(kva, wuk_ref[...])
        v_ref[...] = _dot(kva, wv_ref[...]).astype(BF16)
        krp = kr_ref[...]
        c, s1, s2 = c_ref[...], s1_ref[...], s2_ref[...]
        for h in range(N_HEADS):
            sl = slice(h * HEAD_PAD, (h + 1) * HEAD_PAD)
            qh, _ = _rms(q[:, sl], QK_HEAD)
            q_ref[:, sl] = _rope(qh * qg_ref[...], c, s1, s2).astype(BF16)
            kh, _ = _rms(kn[:, sl] + krp, QK_HEAD)
            k_ref[:, sl] = _rope(kh * kg_ref[...], c, s1, s2).astype(BF16)

    def row(w):
        return pl.BlockSpec((tm, w), lambda i: (i, 0))

    return pl.pallas_call(
        body, name="qkv_fwd", grid=(n // tm,),
        in_specs=[row(Q_LORA), row(KV_LORA), row(LANES), _const_spec((1, Q_LORA)), _const_spec((1, KV_LORA)),
                  _const_spec((QP_COLS, Q_LORA)), _const_spec((KV_LORA, QP_COLS)), _const_spec((KV_LORA, D_ATTN)),
                  _const_spec((1, LANES)), _const_spec((1, LANES)), row(LANES), row(LANES), row(LANES)],
        out_specs=[row(QP_COLS), row(QP_COLS), row(D_ATTN)],
        out_shape=[jax.ShapeDtypeStruct((n, QP_COLS), BF16), jax.ShapeDtypeStruct((n, QP_COLS), BF16),
                   jax.ShapeDtypeStruct((n, D_ATTN), BF16)],
        compiler_params=pltpu.CompilerParams(dimension_semantics=("parallel",), vmem_limit_bytes=VMEM_LIMIT),
    )(cq, ckv, kr, gqa, gkva, w_uq_p, w_uk_p, w_v, qg, kg, rc, rs1, rs2)


def _qkv_bwd(cq, ckv, kr, dq_r, dk_r, dv, dxr, dxg, gqa, gkva, w_uq_p, w_uk_p, w_v, qg, kg, rc, rs1, rs2):
    n = cq.shape[0]
    tm = _row_tile(n)

    def body(cq_ref, ckv_ref, kr_ref, dq_ref, dk_ref, dv_ref, dxr_ref, dxg_ref, gqa_ref, gkva_ref, wuq_ref, wuk_ref,
             wv_ref, qg_ref, kg_ref, c_ref, s1_ref, s2_ref,
             dp_ref, qa_ref, kva_ref, dqp_ref, dkv_ref, dqg_ref, dkg_ref, dgqa_ref, dgkva_ref):
        first = pl.program_id(0) == 0
        dp_ref[:, OFF_CKV:OFF_CKV + D_RNN] = dxr_ref[...].astype(BF16)
        dp_ref[:, OFF_CKV + D_RNN:OFF_CKV + 2 * D_RNN] = dxg_ref[...].astype(BF16)
        xq, rq = _rms(cq_ref[...], Q_LORA)
        qa = (xq * gqa_ref[...]).astype(BF16)
        qa_ref[...] = qa
        q = _dot_nt(qa, wuq_ref[...])
        xkv, rkv = _rms(ckv_ref[...], KV_LORA)
        kva = (xkv * gkva_ref[...]).astype(BF16)
        kva_ref[...] = kva
        kn = _dot(kva, wuk_ref[...])
        krp = kr_ref[...]
        c, s1, s2 = c_ref[...], s1_ref[...], s2_ref[...]
        lane = lax.broadcasted_iota(jnp.int32, (tm, HEAD_PAD), 1)
        rope_lanes = jnp.logical_and(lane >= QK_NOPE, lane < QK_HEAD)
        dqg = jnp.zeros((1, HEAD_PAD), F32)
        dkg = jnp.zeros((1, HEAD_PAD), F32)
        dkr = jnp.zeros((tm, HEAD_PAD), F32)
        for h in range(N_HEADS):
            sl = slice(h * HEAD_PAD, (h + 1) * HEAD_PAD)
            qh, rqh = _rms(q[:, sl], QK_HEAD)
            dy = _rope_bwd(dq_ref[:, sl], c, s1, s2)
            dqg = dqg + _colsum(dy * qh)
            dqp_ref[:, sl] = _rms_bwd(dy, qh, rqh, qg_ref[...], QK_HEAD).astype(BF16)
            kh, rkh = _rms(kn[:, sl] + krp, QK_HEAD)
            dyk = _rope_bwd(dk_ref[:, sl], c, s1, s2)
            dkg = dkg + _colsum(dyk * kh)
            dkh = _rms_bwd(dyk, kh, rkh, kg_ref[...], QK_HEAD)
            dkv_ref[:, sl] = dkh.astype(BF16)
            dkr = dkr + jnp.where(rope_lanes, dkh, 0.0)
        dkv_ref[:, QP_COLS:] = dv_ref[...].astype(BF16)
        dp_ref[:, OFF_CKV + 2 * D_RNN:] = dkr.astype(BF16)
        dqa = _dot(dqp_ref[...], wuq_ref[...])
        dp_ref[:, :Q_LORA] = _rms_bwd(dqa, xq, rq, gqa_ref[...], Q_LORA).astype(BF16)
        dkva = _dot_nt(dkv_ref[:, :QP_COLS], wuk_ref[...]) + _dot_nt(dkv_ref[:, QP_COLS:], wv_ref[...])
        dp_ref[:, Q_LORA:OFF_CKV] = _rms_bwd(dkva, xkv, rkv, gkva_ref[...], KV_LORA).astype(BF16)
        _acc(dqg_ref, first, dqg)
        _acc(dkg_ref, first, dkg)
        _acc(dgqa_ref, first, _colsum(dqa * xq))
        _acc(dgkva_ref, first, _colsum(dkva * xkv))

    def row(w):
        return pl.BlockSpec((tm, w), lambda i: (i, 0))

    def acc(w):
        return pl.BlockSpec((1, w), lambda i: (0, 0))

    return pl.pallas_call(
        body, name="qkv_bwd", grid=(n // tm,),
        in_specs=[row(Q_LORA), row(KV_LORA), row(LANES), row(QP_COLS), row(QP_COLS), row(D_ATTN), row(D_RNN), row(D_RNN),
                  _const_spec((1, Q_LORA)), _const_spec((1, KV_LORA)),
                  _const_spec((QP_COLS, Q_LORA)), _const_spec((KV_LORA, QP_COLS)), _const_spec((KV_LORA, D_ATTN)),
                  _const_spec((1, LANES)), _const_spec((1, LANES)), row(LANES), row(LANES), row(LANES)],
        out_specs=[row(P_COLS), row(Q_LORA), row(KV_LORA), row(QP_COLS),
                   row(QP_COLS + D_ATTN), acc(LANES), acc(LANES), acc(Q_LORA), acc(KV_LORA)],
        out_shape=[jax.ShapeDtypeStruct((n, P_COLS), BF16), jax.ShapeDtypeStruct((n, Q_LORA), BF16),
                   jax.ShapeDtypeStruct((n, KV_LORA), BF16), jax.ShapeDtypeStruct((n, QP_COLS), BF16),
                   jax.ShapeDtypeStruct((n, QP_COLS + D_ATTN), BF16),
                   jax.ShapeDtypeStruct((1, LANES), F32), jax.ShapeDtypeStruct((1, LANES), F32),
                   jax.ShapeDtypeStruct((1, Q_LORA), F32), jax.ShapeDtypeStruct((1, KV_LORA), F32)],
        compiler_params=pltpu.CompilerParams(dimension_semantics=("arbitrary",), vmem_limit_bytes=VMEM_LIMIT),
    )(cq, ckv, kr, dq_r, dk_r, dv, dxr, dxg, gqa, gkva, w_uq_p, w_uk_p, w_v, qg, kg, rc, rs1, rs2)


KEY_CHUNK = 4 * LANES


def _key_chunks(t):
    count = max(t // KEY_CHUNK, 1)
    first = t - KEY_CHUNK * (count - 1)
    return [(0, first)] + [(first + KEY_CHUNK * c, KEY_CHUNK) for c in range(count - 1)]


def _softmax_parts(qh, k_ref, sl, tq, t):
    scores = []
    for start, size in _key_chunks(t):
        s = _dot_nt(qh, k_ref[start:start + size, sl]) * (QK_HEAD ** -0.5)
        if start < PAD_ROWS:
            key = lax.broadcasted_iota(jnp.int32, (tq, size), 1) + start
            s = jnp.where(key >= PAD_ROWS, s, -jnp.inf)
        scores.append(s)
    top = functools.reduce(jnp.maximum, [jnp.max(s, axis=-1, keepdims=True) for s in scores])
    es = [jnp.exp(s - top) for s in scores]
    return es, functools.reduce(jnp.add, [jnp.sum(e, axis=-1, keepdims=True) for e in es])


def _attn_specs(t, tq):
    nq = t // tq
    qspec = pl.BlockSpec((tq, 2 * HEAD_PAD), lambda b, hp, i: (b * nq + i, hp))
    kspec = pl.BlockSpec((t, 2 * HEAD_PAD), lambda b, hp, i: (b, hp))
    vspec = pl.BlockSpec((t, 2 * V_HEAD), lambda b, hp, i: (b, hp))
    ospec = pl.BlockSpec((tq, 2 * V_HEAD), lambda b, hp, i: (b * nq + i, hp))
    return nq, qspec, kspec, vspec, ospec


def _probs_spec(t, tq):
    return pl.BlockSpec((1, 2, tq, t), lambda b, hp, i: (b, hp, i, 0))


def _attn_fwd(q, k, v, srcs=(), scatter=()):
    n = q.shape[0]
    t = _t_pad()
    tq = t // 2
    nq, qspec, kspec, vspec, ospec = _attn_specs(t, tq)
    nk = len(srcs)
    c_in, c_out, c_shape, c_sems = _exchange_specs(srcs, scatter)

    def body(q_ref, k_ref, v_ref, *rest):
        o_ref, p_ref = rest[nk:nk + 2]
        finish = _ride(3, *_exchange_fns(rest[:nk], rest[nk + 2:2 * nk + 2], rest[2 * nk + 2:], scatter))
        lane = lax.broadcasted_iota(jnp.int32, (tq, 2 * V_HEAD), 1)
        outs = []
        for j in range(2):
            sl = slice(j * HEAD_PAD, (j + 1) * HEAD_PAD)
            es, l = _softmax_parts(q_ref[:, sl], k_ref, sl, tq, t)
            inv_l = 1.0 / l
            pv = []
            for e, (start, size) in zip(es, _key_chunks(t)):
                p = (e * inv_l).astype(BF16)
                p_ref[0, j, :, start:start + size] = p
                pv.append(_dot(p, v_ref[start:start + size, :]))
            outs.append(functools.reduce(jnp.add, pv))
        o_ref[...] = jnp.where(lane < V_HEAD, outs[0], outs[1])
        finish()

    res = pl.pallas_call(
        body, name="attn_fwd", grid=(n // t, N_HEADS // 2, nq),
        in_specs=[qspec, kspec, vspec] + c_in, out_specs=[ospec, _probs_spec(t, tq)] + c_out,
        out_shape=[jax.ShapeDtypeStruct((n, D_ATTN), F32), jax.ShapeDtypeStruct((n // t, N_HEADS, t, t), BF16)] + c_shape,
        scratch_shapes=c_sems,
        compiler_params=pltpu.CompilerParams(dimension_semantics=("arbitrary", "arbitrary", "arbitrary"),
                                             vmem_limit_bytes=VMEM_LIMIT),
    )(q, k, v, *srcs)
    return res[0], res[1], res[2:]


def _attn_bwd(q, k, v, do, o, probs, srcs=(), scatter=()):
    n = q.shape[0]
    t = _t_pad()
    tq = t // 2
    nq, qspec, kspec, vspec, ospec = _attn_specs(t, tq)
    nk = len(srcs)
    c_in, c_out, c_shape, c_sems = _exchange_specs(srcs, scatter)

    def body(q_ref, k_ref, v_ref, do_ref, o_ref, p_ref, *rest):
        dq_ref, dk_ref, dv_ref = rest[nk:nk + 3]
        finish = _ride(3, *_exchange_fns(rest[:nk], rest[nk + 3:2 * nk + 3], rest[2 * nk + 3:], scatter))

        @pl.when(pl.program_id(2) == 0)
        def _():
            dk_ref[...] = jnp.zeros_like(dk_ref)
            dv_ref[...] = jnp.zeros_like(dv_ref)

        lane = lax.broadcasted_iota(jnp.int32, (tq, 2 * V_HEAD), 1)
        do = do_ref[...]
        do_o = do * o_ref[...]
        chunks = _key_chunks(t)
        dvs = [None] * len(chunks)
        for j in range(2):
            sl = slice(j * HEAD_PAD, (j + 1) * HEAD_PAD)
            qh = q_ref[:, sl]
            in_head = (lane < V_HEAD) if j == 0 else (lane >= V_HEAD)
            doh = jnp.where(in_head, do, 0.0).astype(BF16)
            delta = jnp.sum(jnp.where(in_head, do_o, 0.0), axis=-1, keepdims=True)
            dq = jnp.zeros((tq, HEAD_PAD), F32)
            for c, (start, size) in enumerate(chunks):
                rows = slice(start, start + size)
                p = p_ref[0, j, :, rows]
                dp = _dot_nt(doh, v_ref[rows, :])
                ds = (p.astype(F32) * (dp - delta) * (QK_HEAD ** -0.5)).astype(BF16)
                dq = dq + _dot(ds, k_ref[rows, sl])
                dk_ref[rows, sl] += _dot_tn(ds, qh)
                dvc = _dot_tn(p, doh)
                dvs[c] = dvc if dvs[c] is None else dvs[c] + dvc
            dq_ref[:, sl] = dq
        for (start, size), dvc in zip(chunks, dvs):
            dv_ref[start:start + size, :] += dvc
        finish()

    res = pl.pallas_call(
        body, name="attn_bwd", grid=(n // t, N_HEADS // 2, nq),
        in_specs=[qspec, kspec, vspec, ospec, ospec, _probs_spec(t, tq)] + c_in, out_specs=[qspec, kspec, vspec] + c_out,
        out_shape=[jax.ShapeDtypeStruct((n, QP_COLS), F32), jax.ShapeDtypeStruct((n, QP_COLS), F32),
                   jax.ShapeDtypeStruct((n, D_ATTN), F32)] + c_shape, scratch_shapes=c_sems,
        compiler_params=pltpu.CompilerParams(dimension_semantics=("arbitrary", "arbitrary", "arbitrary"),
                                             vmem_limit_bytes=VMEM_LIMIT),
    )(q, k, v, do, o, probs, *srcs)
    return res[:3], res[3:]


SCAN_STEPS = 8


def _scan(chains, t):
    seg = t // 8
    rows = lax.broadcasted_iota(jnp.int32, (8, LANES), 0)

    def step(i, carry):
        carry = list(carry)
        for u in range(SCAN_STEPS):
            j = i * SCAN_STEPS + u
            for n, (a_ref, b_ref, h_ref, p_ref, reverse) in enumerate(chains):
                h, p = carry[n]
                idx = pl.ds(seg - 1 - j if reverse else j, 8, stride=seg)
                a = a_ref[idx, :]
                h = a * h + b_ref[idx, :]
                p = a * p
                h_ref[idx, :] = h
                p_ref[idx, :] = p
                carry[n] = (h, p)
        return tuple(carry)

    init = tuple((jnp.zeros((8, LANES), F32), jnp.ones((8, LANES), F32)) for _ in chains)
    ends = lax.fori_loop(0, seg // SCAN_STEPS, step, init)
    for (_, _, h_ref, p_ref, reverse), (b, a) in zip(chains, ends):
        for d in (1, 2, 4):
            if reverse:
                keep = rows < 8 - d
                a_n, b_n = pltpu.roll(a, 8 - d, 0), pltpu.roll(b, 8 - d, 0)
            else:
                keep = rows >= d
                a_n, b_n = pltpu.roll(a, d, 0), pltpu.roll(b, d, 0)
            b = a * jnp.where(keep, b_n, 0.0) + b
            a = a * jnp.where(keep, a_n, 1.0)
        for s in (range(7) if reverse else range(1, 8)):
            sl = slice(s * seg, (s + 1) * seg)
            carry_in = b[s + 1:s + 2, :] if reverse else b[s - 1:s, :]
            h_ref[sl, :] = h_ref[sl, :] + p_ref[sl, :] * carry_in


def _shift_rows(x, s, rows, t):
    if s == 0:
        return x
    rolled = pltpu.roll(x, s % t, 0)
    return jnp.where(rows >= s, rolled, 0.0) if s > 0 else jnp.where(rows < t + s, rolled, 0.0)


def _neg_expm1(x, exp_x):
    series = -x * (1.0 + x * (0.5 + x * (1.0 / 6 + x * (1.0 / 24))))
    return jnp.where(x > -0.1, series, 1.0 - exp_x)


def _sigmoid(x):
    return 0.5 * jnp.tanh(0.5 * x) + 0.5


def _gelu_parts(x):
    k = math.sqrt(2.0 / math.pi)
    th = jnp.tanh(k * (x + 0.044715 * x * x * x))
    g = 0.5 * x * (1.0 + th)
    dg = 0.5 * (1.0 + th) + 0.5 * x * (1.0 - th * th) * k * (1.0 + 3 * 0.044715 * x * x)
    return g, dg


def _lru_gates(xc, gates, lam_ref, valid, d):
    r = _sigmoid(gates[:, (2 * d) * LANES:(2 * d + 1) * LANES])
    i = _sigmoid(gates[:, (2 * d + 1) * LANES:(2 * d + 2) * LANES])
    neg_lam = -lam_ref[d:d + 1, :]
    sp = jnp.maximum(neg_lam, 0.0) + jnp.log1p(jnp.exp(-jnp.abs(neg_lam)))
    log_a = -LRU_C * r * sp
    a = jnp.exp(log_a)
    m = jnp.maximum(_neg_expm1(2.0 * log_a, a * a), 0.0)
    sq = jnp.sqrt(m)
    b = jnp.where(valid, sq * (i * xc), 0.0)
    return r, i, sp, a, m, sq, b


def _conv(xr, cw_ref, cb_ref, rows, t):
    return (cw_ref[0:1, :] * _shift_rows(xr, 2, rows, t) + cw_ref[1:2, :] * _shift_rows(xr, 1, rows, t)
            + cw_ref[2:3, :] * xr + cw_ref[3:4, :] * _shift_rows(xr, -1, rows, t) + cb_ref[...])


def _rnn_specs(t):
    seq = pl.BlockSpec((t, LANES), lambda cb, b: (b, cb))
    cw = pl.BlockSpec((4, LANES), lambda cb, b: (0, cb))
    vec1 = pl.BlockSpec((1, LANES), lambda cb, b: (0, cb))
    vec2 = pl.BlockSpec((2, LANES), lambda cb, b: (0, cb))
    wblk = pl.BlockSpec((1, LANES, 4 * LANES), lambda cb, b: (cb, 0, 0))
    gbias = pl.BlockSpec((1, 1, 4 * LANES), lambda cb, b: (cb, 0, 0))
    return seq, cw, vec1, vec2, wblk, gbias


def _rnn_fwd(xr, xg, conv_w, conv_b, wblk, gbias, lam):
    n = xr.shape[0]
    t = _t_pad()
    seq, cw, vec1, vec2, wspec, gspec = _rnn_specs(t)

    def body(xr_ref, xg_ref, cw_ref, cb_ref, w_ref, gb_ref, lam_ref, o_ref, a_s, b_s, h_s, p_s):
        rows = lax.broadcasted_iota(jnp.int32, (t, LANES), 0)
        valid = rows >= PAD_ROWS
        xc = _conv(xr_ref[...], cw_ref, cb_ref, rows, t)
        gates = _dot(xc.astype(BF16), w_ref[0]) + gb_ref[0]
        for d in range(2):
            _, _, _, a, _, _, b = _lru_gates(xc, gates, lam_ref, valid, d)
            a_s[d] = a
            b_s[d] = b
        _scan([(a_s.at[d], b_s.at[d], h_s.at[d], p_s.at[d], d == 1) for d in range(2)], t)
        g, _ = _gelu_parts(xg_ref[...])
        o_ref[...] = (h_s[0] + h_s[1]) * g

    return pl.pallas_call(
        body, name="rnn_fwd", grid=(D_RNN // LANES, n // t),
        in_specs=[seq, seq, cw, vec1, wspec, gspec, vec2], out_specs=seq,
        out_shape=jax.ShapeDtypeStruct((n, D_RNN), F32),
        scratch_shapes=[pltpu.VMEM((2, t, LANES), F32)] * 4,
        compiler_params=pltpu.CompilerParams(dimension_semantics=("parallel", "parallel"), vmem_limit_bytes=VMEM_LIMIT),
    )(xr, xg, conv_w, conv_b, wblk, gbias, lam)


def _rnn_bwd(xr, xg, do, conv_w, conv_b, wblk, gbias, lam):
    n = xr.shape[0]
    t = _t_pad()
    seq, cw, vec1, vec2, wspec, gspec = _rnn_specs(t)

    def body(xr_ref, xg_ref, do_ref, cw_ref, cb_ref, w_ref, gb_ref, lam_ref,
             dxr_ref, dxg_ref, dcw_ref, dcb_ref, dw_ref, dgb_ref, dlam_ref,
             a_s, b_s, h_s, l_s, p_s, back_s, r_s, i_s, q_s, dg_s):
        first = pl.program_id(1) == 0
        rows = lax.broadcasted_iota(jnp.int32, (t, LANES), 0)
        valid = rows >= PAD_ROWS
        xr = xr_ref[...]
        xc = _conv(xr, cw_ref, cb_ref, rows, t)
        xcb = xc.astype(BF16)
        gates = _dot(xcb, w_ref[0]) + gb_ref[0]
        sps = []
        for d in range(2):
            r_s[d], i_s[d], sp, a_s[d], _, q_s[d], b_s[d] = _lru_gates(xc, gates, lam_ref, valid, d)
            sps.append(sp)
        _scan([(a_s.at[d], b_s.at[d], h_s.at[d], p_s.at[d], d == 1) for d in range(2)], t)
        g, dg = _gelu_parts(xg_ref[...])
        do = do_ref[...]
        dxg_ref[...] = do * (h_s[0] + h_s[1]) * dg
        b_s[0] = do * g
        for d in range(2):
            back_s[d] = _shift_rows(a_s[d], -1 if d == 0 else 1, rows, t)
        _scan([(back_s.at[d], b_s.at[0], l_s.at[d], p_s.at[d], d == 0) for d in range(2)], t)
        dxc = jnp.zeros((t, LANES), F32)
        dlams = []
        for d in range(2):
            r, i, sp, a, sq = r_s[d], i_s[d], sps[d], a_s[d], q_s[d]
            lam_t = l_s[d]
            da = lam_t * _shift_rows(h_s[d], 1 if d == 0 else -1, rows, t)
            lam_v = jnp.where(valid, lam_t, 0.0)
            dsq = lam_v * (i * xc)
            di = lam_v * sq * xc
            dxc = dxc + lam_v * sq * i
            dm = jnp.where(sq > 0.0, dsq * 0.5 / jnp.where(sq > 0.0, sq, 1.0), 0.0)
            dla = da * a - 2.0 * dm * a * a
            dr = dla * (-LRU_C) * sp
            dsp = _colsum(dla * (-LRU_C) * r)
            dlams.append(dsp * -jax.nn.sigmoid(-lam_ref[d:d + 1, :]))
            dg_s[:, (2 * d) * LANES:(2 * d + 1) * LANES] = (dr * r * (1.0 - r)).astype(BF16)
            dg_s[:, (2 * d + 1) * LANES:(2 * d + 2) * LANES] = (di * i * (1.0 - i)).astype(BF16)
        dgates = dg_s[...]
        dxc = dxc + _dot_nt(dgates, w_ref[0])
        taps = [_shift_rows(dxc, j - 2, rows, t) for j in range(4)]
        dxr_ref[...] = (cw_ref[0:1, :] * taps[0] + cw_ref[1:2, :] * taps[1] + cw_ref[2:3, :] * taps[2]
                        + cw_ref[3:4, :] * taps[3])
        dcw = jnp.concatenate([_colsum(tap * xr) for tap in taps], axis=0)
        _acc(dcw_ref, first, dcw)
        _acc(dcb_ref, first, _colsum(dxc))
        _acc(dw_ref, first, _dot_tn(xcb, dgates)[None])
        _acc(dgb_ref, first, _colsum(dgates.astype(F32))[None])
        _acc(dlam_ref, first, jnp.concatenate(dlams, axis=0))

    return pl.pallas_call(
        body, name="rnn_bwd", grid=(D_RNN // LANES, n // t),
        in_specs=[seq, seq, seq, cw, vec1, wspec, gspec, vec2],
        out_specs=[seq, seq, cw, vec1, wspec, gspec, vec2],
        out_shape=[jax.ShapeDtypeStruct((n, D_RNN), F32), jax.ShapeDtypeStruct((n, D_RNN), F32),
                   jax.ShapeDtypeStruct((4, D_RNN), F32), jax.ShapeDtypeStruct((1, D_RNN), F32),
                   jax.ShapeDtypeStruct((D_RNN // LANES, LANES, 4 * LANES), F32),
                   jax.ShapeDtypeStruct((D_RNN // LANES, 1, 4 * LANES), F32), jax.ShapeDtypeStruct((2, D_RNN), F32)],
        scratch_shapes=[pltpu.VMEM((2, t, LANES), F32)] * 9 + [pltpu.VMEM((t, 4 * LANES), BF16)],
        compiler_params=pltpu.CompilerParams(dimension_semantics=("parallel", "arbitrary"), vmem_limit_bytes=VMEM_LIMIT),
    )(xr, xg, do, conv_w, conv_b, wblk, gbias, lam)


def _post(oa, orn, h0, tgt, ga, gr, g2, w_out, w_gate, w_up, w_down):
    n = oa.shape[0]
    tm = _row_tile(n)
    t = _t_pad()

    def body(oa_ref, or_ref, h0_ref, tgt_ref, ga_ref, gr_ref, g2_ref, wo_ref, wg_ref, wu_ref, wd_ref,
             doa_ref, dor_ref, dh1_ref, mix_ref, h1n_ref, act_ref, dgate_ref, dup_ref, dy_ref,
             loss_ref, dga_ref, dgr_ref, dg2_ref, gate_s, up_s):
        first = pl.program_id(0) == 0
        xa, ra = _rms(oa_ref[...], D_ATTN)
        xr, rr = _rms(or_ref[...], D_RNN)
        mix = jnp.concatenate([(xa * ga_ref[...]).astype(BF16), (xr * gr_ref[...]).astype(BF16)], axis=-1)
        mix_ref[...] = mix.T
        h1 = h0_ref[...] + _dot(mix, wo_ref[...])
        x2, r2 = _rms(h1, D_MODEL)
        h1n = (x2 * g2_ref[...]).astype(BF16)
        h1n_ref[...] = h1n
        y = h1
        for cs in range(0, D_FF, FF_CHUNK):
            sl = slice(cs, cs + FF_CHUNK)
            gate = _dot_nt(h1n, wg_ref[sl, :])
            up = _dot_nt(h1n, wu_ref[sl, :])
            gate_s[:, sl] = gate
            up_s[:, sl] = up
            act = (gate * _sigmoid(gate) * up).astype(BF16)
            act_ref[sl, :] = act.T
            y = y + _dot(act, wd_ref[sl, :])
        row = pl.program_id(0) * tm + lax.broadcasted_iota(jnp.int32, (tm, 1), 0)
        for _ in range(1, n // t):
            row = jnp.where(row >= t, row - t, row)
        err = jnp.where(row >= PAD_ROWS + N_META, y - tgt_ref[...], 0.0)
        _acc(loss_ref, first, jnp.full((1, LANES), 0.5 / D_MODEL, F32) * jnp.sum(err * err))
        dy = err * (1.0 / D_MODEL)
        dyb = dy.astype(BF16)
        dy_ref[...] = dyb
        dh1n = jnp.zeros((tm, D_MODEL), F32)
        for cs in range(0, D_FF, FF_CHUNK):
            sl = slice(cs, cs + FF_CHUNK)
            dact = _dot_nt(dyb, wd_ref[sl, :])
            gate, up = gate_s[:, sl], up_s[:, sl]
            sg = _sigmoid(gate)
            dgate = (dact * up * sg * (1.0 + gate * (1.0 - sg))).astype(BF16)
            dup = (dact * gate * sg).astype(BF16)
            dgate_ref[sl, :] = dgate.T
            dup_ref[sl, :] = dup.T
            dh1n = dh1n + _dot(dgate, wg_ref[sl, :]) + _dot(dup, wu_ref[sl, :])
        _acc(dg2_ref, first, _colsum(dh1n * x2))
        dh1 = dy + _rms_bwd(dh1n, x2, r2, g2_ref[...], D_MODEL)
        dh1_ref[...] = dh1
        dmix = _dot_nt(dh1.astype(BF16), wo_ref[...])
        dma, dmr = dmix[:, :D_ATTN], dmix[:, D_ATTN:]
        _acc(dga_ref, first, _colsum(dma * xa))
        _acc(dgr_ref, first, _colsum(dmr * xr))
        doa_ref[...] = _rms_bwd(dma, xa, ra, ga_ref[...], D_ATTN)
        dor_ref[...] = _rms_bwd(dmr, xr, rr, gr_ref[...], D_RNN)

    def row(w):
        return pl.BlockSpec((tm, w), lambda i: (i, 0))

    def acc(w):
        return pl.BlockSpec((1, w), lambda i: (0, 0))

    def col(w):
        return pl.BlockSpec((w, tm), lambda i: (0, i))

    outs = [(D_ATTN, F32, row), (D_RNN, F32, row), (D_MODEL, F32, row), (D_MODEL, BF16, col), (D_MODEL, BF16, row),
            (D_FF, BF16, col), (D_FF, BF16, col), (D_FF, BF16, col), (D_MODEL, BF16, row)]
    accs = [LANES, D_ATTN, D_RNN, D_MODEL]
    return pl.pallas_call(
        body, name="post", grid=(n // tm,),
        in_specs=[row(D_ATTN), row(D_RNN), row(D_MODEL), row(D_MODEL),
                  _const_spec((1, D_ATTN)), _const_spec((1, D_RNN)), _const_spec((1, D_MODEL)),
                  _const_spec((D_MODEL, D_MODEL)), _const_spec((D_FF, D_MODEL)), _const_spec((D_FF, D_MODEL)),
                  _const_spec((D_FF, D_MODEL))],
        out_specs=[spec(w) for w, _, spec in outs] + [acc(w) for w in accs],
        out_shape=[jax.ShapeDtypeStruct((n, w) if spec is row else (w, n), dt) for w, dt, spec in outs]
        + [jax.ShapeDtypeStruct((1, w), F32) for w in accs],
        scratch_shapes=[pltpu.VMEM((tm, D_FF), F32), pltpu.VMEM((tm, D_FF), F32)],
        compiler_params=pltpu.CompilerParams(dimension_semantics=("arbitrary",), vmem_limit_bytes=VMEM_LIMIT),
    )(oa, orn, h0, tgt, ga, gr, g2, w_out, w_gate, w_up, w_down)


def _in_bwd(dp, h0, dh1, ln1_g, w_in_p, srcs=(), scatter=()):
    n = h0.shape[0]
    tm = _row_tile(n)
    nk = len(srcs)
    c_in, c_out, c_shape, c_sems = _exchange_specs(srcs, scatter)

    def body(dp_ref, h0_ref, dh1_ref, g_ref, w_ref, *rest):
        dh0_ref, dg_ref = rest[nk:nk + 2]
        finish = _ride(1, *_exchange_fns(rest[:nk], rest[nk + 2:2 * nk + 2], rest[2 * nk + 2:], scatter))
        dhn = _dot(dp_ref[...], w_ref[...])
        xhat, r = _rms(h0_ref[...], D_MODEL)
        _acc(dg_ref, pl.program_id(0) == 0, _colsum(dhn * xhat))
        dh0_ref[...] = dh1_ref[...] + _rms_bwd(dhn, xhat, r, g_ref[...], D_MODEL)
        finish()

    def row(w):
        return pl.BlockSpec((tm, w), lambda i: (i, 0))

    res = pl.pallas_call(
        body, name="in_bwd", grid=(n // tm,),
        in_specs=[row(P_COLS), row(D_MODEL), row(D_MODEL), _const_spec((1, D_MODEL)), _const_spec((P_COLS, D_MODEL))] + c_in,
        out_specs=[row(D_MODEL), pl.BlockSpec((1, D_MODEL), lambda i: (0, 0))] + c_out,
        out_shape=[jax.ShapeDtypeStruct((n, D_MODEL), F32), jax.ShapeDtypeStruct((1, D_MODEL), F32)] + c_shape,
        scratch_shapes=c_sems,
        compiler_params=pltpu.CompilerParams(dimension_semantics=("arbitrary",), vmem_limit_bytes=VMEM_LIMIT),
    )(dp, h0, dh1, ln1_g, w_in_p, *srcs)
    return res[:2], res[2:]


MAX_TILE = FF_CHUNK


def _pick_tile(width, cap):
    best = LANES
    for mult in range(1, width // LANES + 1):
        cand = mult * LANES
        if width % cand == 0 and cand <= cap:
            best = cand
    return best


def _matmul_tn(name, a, b, srcs=(), scatter=()):
    n, ka = a.shape
    kb = b.shape[1]
    ta, tb = _pick_tile(ka, MAX_TILE), _pick_tile(kb, MAX_TILE)
    tk = n // 4
    nk = len(srcs)
    c_in, c_out, c_shape, c_sems = _exchange_specs(srcs, scatter)

    def body(a_ref, b_ref, *rest):
        o_ref = rest[nk]
        finish = _ride(3, *_exchange_fns(rest[:nk], rest[nk + 1:2 * nk + 1], rest[2 * nk + 1:], scatter))
        _acc(o_ref, pl.program_id(2) == 0, _dot_tn(a_ref[...].astype(BF16), b_ref[...].astype(BF16)))
        finish()

    res = pl.pallas_call(
        body, name=name, grid=(ka // ta, kb // tb, n // tk),
        in_specs=[pl.BlockSpec((tk, ta), lambda i, j, k: (k, i)), pl.BlockSpec((tk, tb), lambda i, j, k: (k, j))] + c_in,
        out_specs=[pl.BlockSpec((ta, tb), lambda i, j, k: (i, j))] + c_out,
        out_shape=[jax.ShapeDtypeStruct((ka, kb), F32)] + c_shape, scratch_shapes=c_sems,
        compiler_params=pltpu.CompilerParams(dimension_semantics=("arbitrary", "arbitrary", "arbitrary"),
                                             vmem_limit_bytes=VMEM_LIMIT),
    )(a, b, *srcs)
    return res[0], res[1:]


def _matmul_shards(name, at, b):
    ka, n = at.shape
    kb = b.shape[1]
    ta, tb = _pick_tile(ka, MAX_TILE), _pick_tile(kb, MAX_TILE)
    tk = n // 2
    width = ka // N_DEV
    per = ta // width

    def body(a_ref, b_ref, o_ref, acc_ref):
        _acc(acc_ref, pl.program_id(2) == 0, _dot(a_ref[...], b_ref[...].astype(BF16)))

        @pl.when(pl.program_id(2) == pl.num_programs(2) - 1)
        def _():
            for s in range(per):
                o_ref[s] = acc_ref[s * width:(s + 1) * width, :].astype(BF16)

    return pl.pallas_call(
        body, name=name, grid=(ka // ta, kb // tb, n // tk),
        in_specs=[pl.BlockSpec((ta, tk), lambda i, j, k: (i, k)), pl.BlockSpec((tk, tb), lambda i, j, k: (k, j))],
        out_specs=pl.BlockSpec((per, width, tb), lambda i, j, k: (i, 0, j)),
        out_shape=jax.ShapeDtypeStruct((N_DEV, width, kb), BF16),
        scratch_shapes=[pltpu.VMEM((ta, tb), F32)],
        compiler_params=pltpu.CompilerParams(dimension_semantics=("parallel", "parallel", "arbitrary"),
                                             vmem_limit_bytes=VMEM_LIMIT),
    )(at, b)


def _adamw_math(g8_ref, w_ref, m_ref, v_ref, g_ref, d_ref, nm_ref, nv_ref):
    g = g8_ref[0].astype(F32)
    for s in range(1, N_DEV):
        g = g + g8_ref[s].astype(F32)
    g_ref[...] = g
    nm = ADAM_B1 * m_ref[...] + (1.0 - ADAM_B1) * g
    nv = ADAM_B2 * v_ref[...] + (1.0 - ADAM_B2) * (g * g)
    nm_ref[...] = nm
    nv_ref[...] = nv
    m_hat = nm / (1.0 - ADAM_B1 ** ADAM_STEP)
    v_hat = nv / (1.0 - ADAM_B2 ** ADAM_STEP)
    d_ref[...] = -ADAM_LR * (m_hat / (jnp.sqrt(v_hat) + ADAM_EPS) + ADAM_WD * w_ref[...])


def _adamw_many(name, items):
    count = len(items)

    def body(*refs):
        ins, outs = refs[:4 * count], refs[4 * count:]
        for i in range(count):
            _adamw_math(*ins[4 * i:4 * i + 4], *outs[4 * i:4 * i + 4])

    flat = [a for item in items for a in item]
    res = pl.pallas_call(
        body, name=name,
        out_shape=[jax.ShapeDtypeStruct(item[1].shape, F32) for item in items for _ in range(4)],
        compiler_params=pltpu.CompilerParams(vmem_limit_bytes=VMEM_LIMIT),
    )(*flat)
    return [tuple(res[4 * i:4 * i + 4]) for i in range(count)]


def _adamw(name, g8, w, m, v):
    rows, cols = w.shape
    tr = rows
    for cand in (256, 176, 128, 64):
        if rows % cand == 0 and rows > cand:
            tr = cand
            break

    def body(*refs):
        _adamw_math(*refs)

    blk = pl.BlockSpec((tr, cols), lambda i: (i, 0))
    return pl.pallas_call(
        body, name=name, grid=(rows // tr,),
        in_specs=[pl.BlockSpec((N_DEV, tr, cols), lambda i: (0, i, 0)), blk, blk, blk],
        out_specs=[blk] * 4, out_shape=[jax.ShapeDtypeStruct((rows, cols), F32)] * 4,
        compiler_params=pltpu.CompilerParams(dimension_semantics=("parallel",), vmem_limit_bytes=VMEM_LIMIT),
    )(g8, w, m, v)


def _exchange_specs(srcs, scatter):
    nk = len(srcs)
    if not nk:
        return [], [], [], []
    any_spec = pl.BlockSpec(memory_space=pl.ANY)
    out_shape = [jax.ShapeDtypeStruct(s.shape if sc else (N_DEV,) + s.shape, s.dtype) for s, sc in zip(srcs, scatter)]
    sems = [pltpu.SemaphoreType.DMA((nk, N_DEV - 1)), pltpu.SemaphoreType.DMA((nk, N_DEV - 1)),
            pltpu.SemaphoreType.DMA((nk,))]
    return [any_spec] * nk, [any_spec] * nk, out_shape, sems


FLIPS = ((0, 0, 1), (1, 0, 0), (0, 1, 0), (1, 1, 0), (1, 0, 1), (0, 1, 1), (1, 1, 1))
N_CHIP_PEERS = 3


def _exchange_fns(src_refs, out_refs, sems, scatter):
    nk = len(src_refs)
    if not nk:
        return (lambda: None), (lambda: None), (lambda: None)
    send_sems, recv_sems, local_sems = sems
    first = 1 + N_CHIP_PEERS

    def plan():
        x, y, c = lax.axis_index("x"), lax.axis_index("y"), lax.axis_index("c")
        me = 4 * x + 2 * y + c
        peers = [(1 - x if fx else x, 1 - y if fy else y, 1 - c if fc else c) for fx, fy, fc in FLIPS]
        pids = [4 * px + 2 * py + pc for px, py, pc in peers]

        def remote(k, j, src, dst, to):
            return pltpu.make_async_remote_copy(src_ref=src, dst_ref=dst, send_sem=send_sems.at[k, j],
                                                recv_sem=recv_sems.at[k, j], device_id=to, device_id_type=MESH)

        def mine(k, dest):
            return src_refs[k].at[dest] if scatter[k] else src_refs[k]

        local = [pltpu.make_async_copy(mine(k, me), out_refs[k].at[me], local_sems.at[k]) for k in range(nk)]
        direct = [remote(k, j, mine(k, pids[j]), out_refs[k].at[me], peers[j])
                  for k in range(nk) for j in range(len(FLIPS) if scatter[k] else first)]
        relays = {(k, j): remote(k, j, out_refs[k].at[pids[j - N_CHIP_PEERS]], out_refs[k].at[pids[j - N_CHIP_PEERS]], peers[0])
                  for k in range(nk) if not scatter[k] for j in range(first, len(FLIPS))}
        arrivals = {(k, j): remote(k, j, out_refs[k].at[pids[j]], out_refs[k].at[pids[j]], peers[j])
                    for k in range(nk) for j in range(len(FLIPS))}
        return local, direct, relays, arrivals

    def start():
        local, direct, _, _ = plan()
        for cp in local + direct:
            cp.start()

    def relay():
        _, _, relays, arrivals = plan()
        for (k, j), cp in relays.items():
            arrivals[k, j - N_CHIP_PEERS].wait_recv()
            cp.start()

    def wait():
        local, direct, relays, arrivals = plan()
        for (k, j), cp in arrivals.items():
            if (k, j + N_CHIP_PEERS) not in relays:
                cp.wait_recv()
        for cp in direct + list(relays.values()):
            cp.wait_send()
        for cp in local:
            cp.wait()

    return start, relay, wait


def _grid_step(rank):
    step, total = 0, 1
    for axis in range(rank):
        step = step * pl.num_programs(axis) + pl.program_id(axis)
        total = total * pl.num_programs(axis)
    return step, total


def _ride(rank, start, relay, wait):
    step, total = _grid_step(rank)
    pl.when(step == 0)(start)
    pl.when(step == (3 * total) // 4)(relay)
    return lambda: pl.when(step == total - 1)(wait)


def _exchange(name, srcs, scatter):
    nk = len(srcs)
    c_in, c_out, c_shape, c_sems = _exchange_specs(srcs, scatter)

    def body(*refs):
        start, relay, wait = _exchange_fns(refs[:nk], refs[nk:2 * nk], refs[2 * nk:], scatter)
        start()
        relay()
        wait()

    return pl.pallas_call(body, name=name, in_specs=c_in, out_specs=c_out, out_shape=c_shape, scratch_shapes=c_sems)(*srcs)


def _cols_from_shards(g):
    return jnp.transpose(g, (1, 0, 2)).reshape(g.shape[1], -1)


def _cols_to_shards(w):
    return jnp.transpose(w.reshape(w.shape[0], N_DEV, -1), (1, 0, 2))


def _prep(x, tgt, srcs, scatter):
    nb = x.shape[0]
    t = _t_pad()
    head = PAD_ROWS + N_META
    nk = len(srcs)
    c_in, c_out, c_shape, c_sems = _exchange_specs(srcs, scatter)

    def body(x_ref, tgt_ref, *rest):
        h0_ref, tp_ref = rest[nk:nk + 2]
        finish = _ride(1, *_exchange_fns(rest[:nk], rest[nk + 2:2 * nk + 2], rest[2 * nk + 2:], scatter))
        lead = pl.program_id(0) == 0

        @pl.when(lead)
        def _():
            h0_ref[...] = jnp.zeros_like(h0_ref)
            tp_ref[...] = jnp.zeros_like(tp_ref)

        @pl.when(jnp.logical_not(lead))
        def _():
            h0_ref[...] = x_ref[...]
            tp_ref[...] = tgt_ref[...]

        finish()

    src = pl.BlockSpec((nb, head, D_MODEL), lambda j: (0, jnp.maximum(j - 1, 0), 0))
    dst = pl.BlockSpec((nb, head, D_MODEL), lambda j: (0, j, 0))
    padded = jax.ShapeDtypeStruct((nb, t, D_MODEL), F32)
    res = pl.pallas_call(
        body, name="prep", grid=(t // head,), in_specs=[src, src] + c_in, out_specs=[dst, dst] + c_out,
        out_shape=[padded, padded] + c_shape, scratch_shapes=c_sems,
        compiler_params=pltpu.CompilerParams(dimension_semantics=("arbitrary",)),
    )(x, tgt, *srcs)
    return res[0], res[1], res[2:]


def _rope_tables(n):
    t = _t_pad()
    pos = np.arange(t, dtype=np.float32) - np.float32(PAD_ROWS)
    half = QK_ROPE // 2
    freqs = (1.0 / (ROPE_THETA ** (np.arange(half, dtype=np.float32) / half))).astype(np.float32)
    ang = pos[:, None] * freqs[None, :]
    cos, sin = np.cos(ang), np.sin(ang)
    z = lambda w: np.zeros((t, w), np.float32)
    c = np.concatenate([np.ones((t, QK_NOPE), np.float32), cos, cos, z(HEAD_PAD - QK_HEAD)], axis=1)
    s1 = np.concatenate([z(QK_NOPE + half), sin, z(HEAD_PAD - QK_HEAD)], axis=1)
    s2 = np.concatenate([z(QK_NOPE), -sin, z(HEAD_PAD - QK_NOPE - half)], axis=1)
    return tuple(jnp.asarray(np.tile(a, (n // t, 1))) for a in (c, s1, s2))


def _block_diag_gates(lru_wa, lru_wi):
    eye = jnp.eye(2, dtype=lru_wa.dtype)

    def bd(w):
        w = w.reshape(2, D_RNN // LANES, 2, RNN_BW, RNN_BW)
        full = w[:, :, :, :, None, :] * eye[None, None, :, None, :, None]
        return full.reshape(2, D_RNN // LANES, LANES, LANES)

    a, i = bd(lru_wa), bd(lru_wi)
    return jnp.concatenate([a[0], i[0], a[1], i[1]], axis=-1)


def _unblock_gates(dw):
    nb = D_RNN // LANES
    parts = dw.reshape(nb, 2, RNN_BW, 4, 2, RNN_BW)
    diag = jnp.stack([parts[:, k, :, :, k, :] for k in range(2)], axis=1)
    diag = jnp.transpose(diag, (3, 0, 1, 2, 4)).reshape(4, 2 * nb, RNN_BW, RNN_BW)
    return jnp.stack([diag[0], diag[2]]), jnp.stack([diag[1], diag[3]])


WEIGHTS = ("meta_tokens", "ln1_g", "w_in", "q_a_norm_g", "w_uq", "kv_a_norm_g", "w_ukv", "q_norm_g", "k_norm_g",
           "conv_w", "conv_b", "lru_wa", "lru_ba", "lru_wi", "lru_bi", "lru_lambda", "attn_out_g", "rnn_out_g",
           "w_out", "ln2_g", "w_gate", "w_up", "w_down")
BIG = ("w_in", "w_uq", "w_ukv", "w_out", "w_gate", "w_up", "w_down")
TRANSPOSED = ("w_in", "w_uq", "w_gate", "w_up")
ROW_SHARDED = ("w_out", "w_down") + TRANSPOSED
REPLICATED = ("ln1_g", "q_a_norm_g", "kv_a_norm_g", "q_norm_g", "k_norm_g", "conv_b", "lru_wa", "lru_wi",
              "attn_out_g", "rnn_out_g", "ln2_g")
WHOLE = REPLICATED + ("loss",)
G_FIRST = ("w_in", "meta_tokens")
G_MID = ("w_uq", "w_ukv", "conv_w", "lru_ba", "lru_bi", "lru_lambda")
LATE = ("w_out", "w_gate", "w_up", "w_down")
G_LAST = ("meta_tokens", "ln1_g")


def _local_step(x, tgt, ex):
    nb = x.shape[0]
    t = _t_pad()
    n = nb * t
    local = ex.local
    h0, tgt_p, got = _prep(x, tgt, *ex.gather_srcs(G_FIRST))
    first = ex.gathered(G_FIRST, got)
    meta, w_in = first["meta_tokens"], first["w_in"]
    h0 = h0.at[:, PAD_ROWS:PAD_ROWS + N_META].set(jnp.broadcast_to(meta[None], (nb, N_META, D_MODEL))).reshape(n, D_MODEL)
    tgt_p = tgt_p.reshape(n, D_MODEL)

    zr = lambda r: jnp.zeros((r, D_MODEL), w_in.dtype)
    w_in_p = jnp.concatenate([w_in[:OFF_CKV], w_in[OFF_KR:], zr(QK_NOPE), w_in[OFF_CKV:OFF_KR], zr(HEAD_PAD - QK_HEAD)],
                             axis=0)
    pad_g = lambda g: jnp.pad(g, ((0, 0), (0, HEAD_PAD - QK_HEAD)))
    qg, kg = pad_g(local["q_norm_g"]), pad_g(local["k_norm_g"])
    rc, rs1, rs2 = _rope_tables(n)
    wblk = _block_diag_gates(local["lru_wa"].reshape(2, -1, RNN_BW, RNN_BW),
                             local["lru_wi"].reshape(2, -1, RNN_BW, RNN_BW)).astype(BF16)
    nblk = D_RNN // LANES

    (hn, cq, ckv, xr, xg, kr), got = _in_proj(h0, local["ln1_g"], w_in_p, *ex.gather_srcs(G_MID))
    w = ex.gathered(G_MID, got)
    w_uq_p = jnp.pad(w["w_uq"].reshape(N_HEADS, QK_HEAD, Q_LORA), ((0, 0), (0, HEAD_PAD - QK_HEAD), (0, 0))
                     ).reshape(QP_COLS, Q_LORA)
    ukv = w["w_ukv"].reshape(KV_LORA, N_HEADS, QK_NOPE + V_HEAD)
    w_uk_p = jnp.pad(ukv[:, :, :QK_NOPE], ((0, 0), (0, 0), (0, HEAD_PAD - QK_NOPE))).reshape(KV_LORA, QP_COLS)
    w_v = ukv[:, :, QK_NOPE:].reshape(KV_LORA, D_ATTN)
    gbias = jnp.stack([w["lru_ba"][0], w["lru_bi"][0], w["lru_ba"][1], w["lru_bi"][1]], axis=0)
    gbias = jnp.transpose(gbias.reshape(4, nblk, LANES), (1, 0, 2)).reshape(nblk, 1, 4 * LANES)

    q, k, v = _qkv_fwd(cq, ckv, kr, local["q_a_norm_g"], local["kv_a_norm_g"], w_uq_p, w_uk_p, w_v, qg, kg, rc, rs1, rs2)
    oa, probs, got = _attn_fwd(q, k, v, *ex.gather_srcs(LATE))
    late = ex.gathered(LATE, got)
    orn = _rnn_fwd(xr, xg, w["conv_w"], local["conv_b"], wblk, gbias, w["lru_lambda"])
    (doa, dor, dh1, mix_t, h1n, act_t, dgate_t, dup_t, dyb, loss, dga, dgr, dg2) = _post(
        oa, orn, h0, tgt_p, local["attn_out_g"], local["rnn_out_g"], local["ln2_g"], late["w_out"], late["w_gate"],
        late["w_up"], late["w_down"])
    wire = {"w_out": _matmul_shards("dw_out", mix_t, dh1), "w_gate": _matmul_shards("dw_gate", dgate_t, h1n),
            "w_up": _matmul_shards("dw_up", dup_t, h1n), "w_down": _matmul_shards("dw_down", act_t, dyb)}
    dxr, dxg, dcw, dcb, dwblk, dgb, dlam = _rnn_bwd(xr, xg, dor, w["conv_w"], local["conv_b"], wblk, gbias, w["lru_lambda"])
    dwa, dwi = _unblock_gates(dwblk)
    dgb = jnp.transpose(dgb.reshape(nblk, 4, LANES), (1, 0, 2)).reshape(4, D_RNN)
    wire.update(ex.to_wire({
        "conv_w": dcw, "conv_b": dcb, "lru_wa": dwa.reshape(-1, RNN_BW), "lru_ba": jnp.stack([dgb[0], dgb[2]]),
        "lru_wi": dwi.reshape(-1, RNN_BW), "lru_bi": jnp.stack([dgb[1], dgb[3]]), "lru_lambda": dlam,
        "attn_out_g": dga, "rnn_out_g": dgr, "ln2_g": dg2, "loss": loss}))
    names = tuple(wire)
    (dq_r, dk_r, dv), got = _attn_bwd(q, k, v, doa, oa, probs, *ex.scatter_srcs(names, wire))
    summed = ex.scattered(names, wire, got)
    (dp, qa, kva, dqp, dkv, dqg, dkg, dgqa, dgkva) = _qkv_bwd(
        cq, ckv, kr, dq_r, dk_r, dv, dxr, dxg, local["q_a_norm_g"], local["kv_a_norm_g"], w_uq_p, w_uk_p, w_v, qg, kg,
        rc, rs1, rs2)
    dw_uq_p, _ = _matmul_tn("dw_uq", dqp, qa)
    dw_kv, _ = _matmul_tn("dw_ukv", kva, dkv)
    dw_uq = dw_uq_p.reshape(N_HEADS, HEAD_PAD, Q_LORA)[:, :QK_HEAD].reshape(N_HEADS * QK_HEAD, Q_LORA)
    dw_ukv = jnp.concatenate([dw_kv[:, :QP_COLS].reshape(KV_LORA, N_HEADS, HEAD_PAD)[:, :, :QK_NOPE],
                              dw_kv[:, QP_COLS:].reshape(KV_LORA, N_HEADS, V_HEAD)], axis=2).reshape(KV_LORA, -1)
    wire = ex.to_wire({"q_a_norm_g": dgqa, "w_uq": dw_uq, "kv_a_norm_g": dgkva, "w_ukv": dw_ukv,
                       "q_norm_g": dqg[:, :QK_HEAD], "k_norm_g": dkg[:, :QK_HEAD]})
    names = tuple(wire)
    dw_in_p, got = _matmul_tn("dw_in", dp, hn, *ex.scatter_srcs(names, wire))
    summed.update(ex.scattered(names, wire, got))
    kr0 = OFF_CKV + 2 * D_RNN + QK_NOPE
    dw_in = jnp.concatenate([dw_in_p[:OFF_CKV], dw_in_p[kr0:kr0 + QK_ROPE], dw_in_p[OFF_CKV:OFF_CKV + 2 * D_RNN]], axis=0)
    wire = ex.to_wire({"w_in": dw_in})
    (dh0, dg1), got = _in_bwd(dp, h0, dh1, local["ln1_g"], w_in_p, *ex.scatter_srcs(("w_in",), wire))
    summed.update(ex.scattered(("w_in",), wire, got))

    dh0 = dh0.reshape(nb, t, D_MODEL)
    wire = ex.to_wire({"meta_tokens": jnp.sum(dh0[:, PAD_ROWS:PAD_ROWS + N_META], axis=0), "ln1_g": dg1})
    got = ex.run("reduce_last", *ex.scatter_srcs(G_LAST, wire))
    summed.update(ex.scattered(G_LAST, wire, got))
    return dh0[:, PAD_ROWS + N_META:], summed


class _MeshExchange:
    def __init__(self, shards):
        self.local = shards

    @staticmethod
    def run(name, srcs, scatter):
        return _exchange(name, srcs, scatter)

    def gather_srcs(self, names):
        return [self.local[k].astype(BF16) if k in BIG else self.local[k] for k in names], [False] * len(names)

    @staticmethod
    def gathered(names, outs):
        return {k: g.reshape(-1, g.shape[-1]) if k in ROW_SHARDED else _cols_from_shards(g) for k, g in zip(names, outs)}

    @staticmethod
    def to_wire(grads):
        wire = {}
        for k, g in grads.items():
            if k in WHOLE:
                wire[k] = g
            elif k in ROW_SHARDED:
                wire[k] = g.reshape(N_DEV, -1, g.shape[-1]).astype(BF16)
            else:
                wire[k] = _cols_to_shards(g).astype(BF16) if k in BIG else _cols_to_shards(g)
        return wire

    @staticmethod
    def scatter_srcs(names, wire):
        return [wire[k] for k in names], [k not in WHOLE for k in names]

    @staticmethod
    def scattered(names, wire, outs):
        return dict(zip(names, outs))


def kernel(x, meta_tokens, ln1_g, w_in, q_a_norm_g, w_uq, kv_a_norm_g, w_ukv, q_norm_g, k_norm_g, conv_w, conv_b, lru_wa, lru_ba, lru_wi, lru_bi, lru_lambda, attn_out_g, rnn_out_g, w_out, ln2_g, w_gate, w_up, w_down, loss_target, m_meta_tokens, m_ln1_g, m_w_in, m_q_a_norm_g, m_w_uq, m_kv_a_norm_g, m_w_ukv, m_q_norm_g, m_k_norm_g, m_conv_w, m_conv_b, m_lru_wa, m_lru_ba, m_lru_wi, m_lru_bi, m_lru_lambda, m_attn_out_g, m_rnn_out_g, m_w_out, m_ln2_g, m_w_gate, m_w_up, m_w_down, v_meta_tokens, v_ln1_g, v_w_in, v_q_a_norm_g, v_w_uq, v_kv_a_norm_g, v_w_ukv, v_q_norm_g, v_k_norm_g, v_conv_w, v_conv_b, v_lru_wa, v_lru_ba, v_lru_wi, v_lru_bi, v_lru_lambda, v_attn_out_g, v_rnn_out_g, v_w_out, v_ln2_g, v_w_gate, v_w_up, v_w_down):
    given = (meta_tokens, ln1_g, w_in, q_a_norm_g, w_uq, kv_a_norm_g, w_ukv, q_norm_g, k_norm_g, conv_w, conv_b,
             lru_wa, lru_ba, lru_wi, lru_bi, lru_lambda, attn_out_g, rnn_out_g, w_out, ln2_g, w_gate, w_up, w_down)
    moments_m = (m_meta_tokens, m_ln1_g, m_w_in, m_q_a_norm_g, m_w_uq, m_kv_a_norm_g, m_w_ukv, m_q_norm_g, m_k_norm_g,
                 m_conv_w, m_conv_b, m_lru_wa, m_lru_ba, m_lru_wi, m_lru_bi, m_lru_lambda, m_attn_out_g, m_rnn_out_g,
                 m_w_out, m_ln2_g, m_w_gate, m_w_up, m_w_down)
    moments_v = (v_meta_tokens, v_ln1_g, v_w_in, v_q_a_norm_g, v_w_uq, v_kv_a_norm_g, v_w_ukv, v_q_norm_g, v_k_norm_g,
                 v_conv_w, v_conv_b, v_lru_wa, v_lru_ba, v_lru_wi, v_lru_bi, v_lru_lambda, v_attn_out_g, v_rnn_out_g,
                 v_w_out, v_ln2_g, v_w_gate, v_w_up, v_w_down)
    shapes = {k: a.shape for k, a in zip(WEIGHTS, given)}

    def two_d(k, a):
        a = a.reshape(-1, a.shape[-1])
        return a.T if k in TRANSPOSED else a

    w = {k: two_d(k, a) for k, a in zip(WEIGHTS, given)}
    m = {k: two_d(k, a) for k, a in zip(WEIGHTS, moments_m)}
    v = {k: two_d(k, a) for k, a in zip(WEIGHTS, moments_v)}

    grad_x, parts = _local_step(x, loss_target, _MeshExchange(w))

    new = {k: _adamw("adamw_" + k, parts[k], w[k], m[k], v[k]) for k in BIG}
    small = [k for k in WEIGHTS if k not in BIG]
    new.update(zip(small, _adamw_many("adamw_small", [(parts[k], w[k], m[k], v[k]) for k in small])))

    loss = jnp.sum(parts["loss"][:, 0, 0])
    outs = [loss, grad_x]
    for idx in range(4):
        outs += [(new[k][idx].T if k in TRANSPOSED else new[k][idx]).reshape(shapes[k]) for k in WEIGHTS]
    return tuple(outs)
```

```python
import functools
import math

import numpy as np
import jax
import jax.numpy as jnp
from jax import lax
from jax.experimental import pallas as pl
from jax.experimental.pallas import tpu as pltpu

F32 = jnp.float32
BF16 = jnp.bfloat16

D_MODEL = 1024
N_META = 16
SEQ = 2048
N_HEADS = 8
QK_NOPE = 64
QK_ROPE = 32
QK_HEAD = QK_NOPE + QK_ROPE
V_HEAD = 64
D_ATTN = N_HEADS * V_HEAD
Q_LORA = 384
KV_LORA = 256
D_RNN = 512
RNN_BW = 64
D_FF = 2816
EPS = 1e-6
LRU_C = 8.0
ROPE_THETA = 10000.0
OFF_CKV = Q_LORA + KV_LORA
OFF_KR = OFF_CKV + QK_ROPE
IN_COLS = OFF_KR + 2 * D_RNN

ADAM_LR = 0.001
ADAM_B1 = 0.9
ADAM_B2 = 0.999
ADAM_EPS = 1e-08
ADAM_WD = 0.01
ADAM_STEP = 10

N_DEV = 8
LANES = 128
HEAD_PAD = LANES
PAD_ROWS = LANES - N_META
QP_COLS = N_HEADS * HEAD_PAD
P_COLS = OFF_CKV + 2 * D_RNN + LANES
FF_CHUNK = D_FF // 2
VMEM_LIMIT = 56 * 1024 * 1024
MESH = pl.DeviceIdType.MESH


def _t_pad():
    return PAD_ROWS + N_META + SEQ


def _row_tile(n):
    return 256 if n % 256 == 0 else 128


def _const_spec(shape):
    nd = len(shape)
    return pl.BlockSpec(shape, lambda *_: (0,) * nd, pipeline_mode=pl.Buffered(1))


def _rms(x, d):
    r = lax.rsqrt(jnp.sum(x * x, axis=-1, keepdims=True) * (1.0 / d) + EPS)
    return x * r, r


def _rms_bwd(dy, xhat, r, g, d):
    dxh = dy * g
    return r * (dxh - xhat * (jnp.sum(dxh * xhat, axis=-1, keepdims=True) * (1.0 / d)))


def _colsum(x):
    return jnp.sum(x, axis=0, keepdims=True)


def _dot(a, b):
    return jnp.dot(a, b, preferred_element_type=F32)


def _dot_nt(a, b):
    return lax.dot_general(a, b, (((1,), (1,)), ((), ())), preferred_element_type=F32)


def _dot_tn(a, b):
    return lax.dot_general(a, b, (((0,), (0,)), ((), ())), preferred_element_type=F32)


def _rope(x, c, s1, s2):
    return x * c + pltpu.roll(x, 16, 1) * s1 + pltpu.roll(x, HEAD_PAD - 16, 1) * s2


def _rope_bwd(dy, c, s1, s2):
    return dy * c + pltpu.roll(dy * s1, HEAD_PAD - 16, 1) + pltpu.roll(dy * s2, 16, 1)


def _acc(ref, first, val):
    @pl.when(first)
    def _():
        ref[...] = val

    @pl.when(jnp.logical_not(first))
    def _():
        ref[...] += val


def _in_proj(h0, ln1_g, w_in_p, srcs=(), scatter=()):
    n = h0.shape[0]
    tm = _row_tile(n)
    nk = len(srcs)
    c_in, c_out, c_shape, c_sems = _exchange_specs(srcs, scatter)

    def body(h_ref, g_ref, w_ref, *rest):
        hn_ref, cq_ref, ckv_ref, xr_ref, xg_ref, kr_ref = rest[nk:nk + 6]
        finish = _ride(1, *_exchange_fns(rest[:nk], rest[nk + 6:2 * nk + 6], rest[2 * nk + 6:], scatter))
        xhat, _ = _rms(h_ref[...], D_MODEL)
        hn = (xhat * g_ref[...]).astype(BF16)
        hn_ref[...] = hn
        p = _dot_nt(hn, w_ref[...])
        cq_ref[...] = p[:, :Q_LORA]
        ckv_ref[...] = p[:, Q_LORA:OFF_CKV]
        xr_ref[...] = p[:, OFF_CKV:OFF_CKV + D_RNN]
        xg_ref[...] = p[:, OFF_CKV + D_RNN:OFF_CKV + 2 * D_RNN]
        kr_ref[...] = p[:, OFF_CKV + 2 * D_RNN:]
        finish()

    def row(w):
        return pl.BlockSpec((tm, w), lambda i: (i, 0))

    widths = (D_MODEL, Q_LORA, KV_LORA, D_RNN, D_RNN, LANES)
    res = pl.pallas_call(
        body, name="in_proj", grid=(n // tm,),
        in_specs=[row(D_MODEL), _const_spec((1, D_MODEL)), _const_spec((P_COLS, D_MODEL))] + c_in,
        out_specs=[row(w) for w in widths] + c_out,
        out_shape=[jax.ShapeDtypeStruct((n, w), BF16 if k == 0 else F32) for k, w in enumerate(widths)] + c_shape,
        scratch_shapes=c_sems,
        compiler_params=pltpu.CompilerParams(dimension_semantics=("arbitrary",), vmem_limit_bytes=VMEM_LIMIT),
    )(h0, ln1_g, w_in_p, *srcs)
    return res[:6], res[6:]


def _qkv_fwd(cq, ckv, kr, gqa, gkva, w_uq_p, w_uk_p, w_v, qg, kg, rc, rs1, rs2):
    n = cq.shape[0]
    tm = _row_tile(n)

    def body(cq_ref, ckv_ref, kr_ref, gqa_ref, gkva_ref, wuq_ref, wuk_ref, wv_ref, qg_ref, kg_ref,
             c_ref, s1_ref, s2_ref, q_ref, k_ref, v_ref):
        xq, _ = _rms(cq_ref[...], Q_LORA)
        qa = (xq * gqa_ref[...]).astype(BF16)
        q = _dot_nt(qa, wuq_ref[...])
        xkv, _ = _rms(ckv_ref[...], KV_LORA)
        kva = (xkv * gkva_ref[...]).astype(BF16)
        kn = _dot(kva, wuk_ref[...])
        v_ref[...] = _dot(kva, wv_ref[...]).astype(BF16)
        krp = kr_ref[...]
        c, s1, s2 = c_ref[...], s1_ref[...], s2_ref[...]
        for h in range(N_HEADS):
            sl = slice(h * HEAD_PAD, (h + 1) * HEAD_PAD)
            qh, _ = _rms(q[:, sl], QK_HEAD)
            q_ref[:, sl] = _rope(qh * qg_ref[...], c, s1, s2).astype(BF16)
            kh, _ = _rms(kn[:, sl] + krp, QK_HEAD)
            k_ref[:, sl] = _rope(kh * kg_ref[...], c, s1, s2).astype(BF16)

    def row(w):
        return pl.BlockSpec((tm, w), lambda i: (i, 0))

    return pl.pallas_call(
        body, name="qkv_fwd", grid=(n // tm,),
        in_specs=[row(Q_LORA), row(KV_LORA), row(LANES), _const_spec((1, Q_LORA)), _const_spec((1, KV_LORA)),
                  _const_spec((QP_COLS, Q_LORA)), _const_spec((KV_LORA, QP_COLS)), _const_spec((KV_LORA, D_ATTN)),
                  _const_spec((1, LANES)), _const_spec((1, LANES)), row(LANES), row(LANES), row(LANES)],
        out_specs=[row(QP_COLS), row(QP_COLS), row(D_ATTN)],
        out_shape=[jax.ShapeDtypeStruct((n, QP_COLS), BF16), jax.ShapeDtypeStruct((n, QP_COLS), BF16),
                   jax.ShapeDtypeStruct((n, D_ATTN), BF16)],
        compiler_params=pltpu.CompilerParams(dimension_semantics=("parallel",), vmem_limit_bytes=VMEM_LIMIT),
    )(cq, ckv, kr, gqa, gkva, w_uq_p, w_uk_p, w_v, qg, kg, rc, rs1, rs2)


def _qkv_bwd(cq, ckv, kr, dq_r, dk_r, dv, dxr, dxg, gqa, gkva, w_uq_p, w_uk_p, w_v, qg, kg, rc, rs1, rs2):
    n = cq.shape[0]
    tm = _row_tile(n)

    def body(cq_ref, ckv_ref, kr_ref, dq_ref, dk_ref, dv_ref, dxr_ref, dxg_ref, gqa_ref, gkva_ref, wuq_ref, wuk_ref,
             wv_ref, qg_ref, kg_ref, c_ref, s1_ref, s2_ref,
             dp_ref, qa_ref, kva_ref, dqp_ref, dkv_ref, dqg_ref, dkg_ref, dgqa_ref, dgkva_ref):
        first = pl.program_id(0) == 0
        dp_ref[:, OFF_CKV:OFF_CKV + D_RNN] = dxr_ref[...].astype(BF16)
        dp_ref[:, OFF_CKV + D_RNN:OFF_CKV + 2 * D_RNN] = dxg_ref[...].astype(BF16)
        xq, rq = _rms(cq_ref[...], Q_LORA)
        qa = (xq * gqa_ref[...]).astype(BF16)
        qa_ref[...] = qa
        q = _dot_nt(qa, wuq_ref[...])
        xkv, rkv = _rms(ckv_ref[...], KV_LORA)
        kva = (xkv * gkva_ref[...]).astype(BF16)
        kva_ref[...] = kva
        kn = _dot(kva, wuk_ref[...])
        krp = kr_ref[...]
        c, s1, s2 = c_ref[...], s1_ref[...], s2_ref[...]
        lane = lax.broadcasted_iota(jnp.int32, (tm, HEAD_PAD), 1)
        rope_lanes = jnp.logical_and(lane >= QK_NOPE, lane < QK_HEAD)
        dqg = jnp.zeros((1, HEAD_PAD), F32)
        dkg = jnp.zeros((1, HEAD_PAD), F32)
        dkr = jnp.zeros((tm, HEAD_PAD), F32)
        for h in range(N_HEADS):
            sl = slice(h * HEAD_PAD, (h + 1) * HEAD_PAD)
            qh, rqh = _rms(q[:, sl], QK_HEAD)
            dy = _rope_bwd(dq_ref[:, sl], c, s1, s2)
            dqg = dqg + _colsum(dy * qh)
            dqp_ref[:, sl] = _rms_bwd(dy, qh, rqh, qg_ref[...], QK_HEAD).astype(BF16)
            kh, rkh = _rms(kn[:, sl] + krp, QK_HEAD)
            dyk = _rope_bwd(dk_ref[:, sl], c, s1, s2)
            dkg = dkg + _colsum(dyk * kh)
            dkh = _rms_bwd(dyk, kh, rkh, kg_ref[...], QK_HEAD)
            dkv_ref[:, sl] = dkh.astype(BF16)
            dkr = dkr + jnp.where(rope_lanes, dkh, 0.0)
        dkv_ref[:, QP_COLS:] = dv_ref[...].astype(BF16)
        dp_ref[:, OFF_CKV + 2 * D_RNN:] = dkr.astype(BF16)
        dqa = _dot(dqp_ref[...], wuq_ref[...])
        dp_ref[:, :Q_LORA] = _rms_bwd(dqa, xq, rq, gqa_ref[...], Q_LORA).astype(BF16)
        dkva = _dot_nt(dkv_ref[:, :QP_COLS], wuk_ref[...]) + _dot_nt(dkv_ref[:, QP_COLS:], wv_ref[...])
        dp_ref[:, Q_LORA:OFF_CKV] = _rms_bwd(dkva, xkv, rkv, gkva_ref[...], KV_LORA).astype(BF16)
        _acc(dqg_ref, first, dqg)
        _acc(dkg_ref, first, dkg)
        _acc(dgqa_ref, first, _colsum(dqa * xq))
        _acc(dgkva_ref, first, _colsum(dkva * xkv))

    def row(w):
        return pl.BlockSpec((tm, w), lambda i: (i, 0))

    def acc(w):
        return pl.BlockSpec((1, w), lambda i: (0, 0))

    return pl.pallas_call(
        body, name="qkv_bwd", grid=(n // tm,),
        in_specs=[row(Q_LORA), row(KV_LORA), row(LANES), row(QP_COLS), row(QP_COLS), row(D_ATTN), row(D_RNN), row(D_RNN),
                  _const_spec((1, Q_LORA)), _const_spec((1, KV_LORA)),
                  _const_spec((QP_COLS, Q_LORA)), _const_spec((KV_LORA, QP_COLS)), _const_spec((KV_LORA, D_ATTN)),
                  _const_spec((1, LANES)), _const_spec((1, LANES)), row(LANES), row(LANES), row(LANES)],
        out_specs=[row(P_COLS), row(Q_LORA), row(KV_LORA), row(QP_COLS),
                   row(QP_COLS + D_ATTN), acc(LANES), acc(LANES), acc(Q_LORA), acc(KV_LORA)],
        out_shape=[jax.ShapeDtypeStruct((n, P_COLS), BF16), jax.ShapeDtypeStruct((n, Q_LORA), BF16),
                   jax.ShapeDtypeStruct((n, KV_LORA), BF16), jax.ShapeDtypeStruct((n, QP_COLS), BF16),
                   jax.ShapeDtypeStruct((n, QP_COLS + D_ATTN), BF16),
                   jax.ShapeDtypeStruct((1, LANES), F32), jax.ShapeDtypeStruct((1, LANES), F32),
                   jax.ShapeDtypeStruct((1, Q_LORA), F32), jax.ShapeDtypeStruct((1, KV_LORA), F32)],
        compiler_params=pltpu.CompilerParams(dimension_semantics=("arbitrary",), vmem_limit_bytes=VMEM_LIMIT),
    )(cq, ckv, kr, dq_r, dk_r, dv, dxr, dxg, gqa, gkva, w_uq_p, w_uk_p, w_v, qg, kg, rc, rs1, rs2)


KEY_CHUNK = 4 * LANES


def _key_chunks(t):
    count = max(t // KEY_CHUNK, 1)
    first = t - KEY_CHUNK * (count - 1)
    return [(0, first)] + [(first + KEY_CHUNK * c, KEY_CHUNK) for c in range(count - 1)]


def _softmax_parts(qh, k_ref, sl, tq, t):
    scores = []
    for start, size in _key_chunks(t):
        s = _dot_nt(qh, k_ref[start:start + size, sl]) * (QK_HEAD ** -0.5)
        if start < PAD_ROWS:
            key = lax.broadcasted_iota(jnp.int32, (tq, size), 1) + start
            s = jnp.where(key >= PAD_ROWS, s, -jnp.inf)
        scores.append(s)
    top = functools.reduce(jnp.maximum, [jnp.max(s, axis=-1, keepdims=True) for s in scores])
    es = [jnp.exp(s - top) for s in scores]
    return es, functools.reduce(jnp.add, [jnp.sum(e, axis=-1, keepdims=True) for e in es])


def _attn_specs(t, tq):
    nq = t // tq
    qspec = pl.BlockSpec((tq, 2 * HEAD_PAD), lambda b, hp, i: (b * nq + i, hp))
    kspec = pl.BlockSpec((t, 2 * HEAD_PAD), lambda b, hp, i: (b, hp))
    vspec = pl.BlockSpec((t, 2 * V_HEAD), lambda b, hp, i: (b, hp))
    ospec = pl.BlockSpec((tq, 2 * V_HEAD), lambda b, hp, i: (b * nq + i, hp))
    return nq, qspec, kspec, vspec, ospec


def _probs_spec(t, tq):
    return pl.BlockSpec((1, 2, tq, t), lambda b, hp, i: (b, hp, i, 0))


def _attn_fwd(q, k, v, srcs=(), scatter=()):
    n = q.shape[0]
    t = _t_pad()
    tq = t // 2
    nq, qspec, kspec, vspec, ospec = _attn_specs(t, tq)
    nk = len(srcs)
    c_in, c_out, c_shape, c_sems = _exchange_specs(srcs, scatter)

    def body(q_ref, k_ref, v_ref, *rest):
        o_ref, p_ref = rest[nk:nk + 2]
        finish = _ride(3, *_exchange_fns(rest[:nk], rest[nk + 2:2 * nk + 2], rest[2 * nk + 2:], scatter))
        lane = lax.broadcasted_iota(jnp.int32, (tq, 2 * V_HEAD), 1)
        outs = []
        for j in range(2):
            sl = slice(j * HEAD_PAD, (j + 1) * HEAD_PAD)
            es, l = _softmax_parts(q_ref[:, sl], k_ref, sl, tq, t)
            inv_l = 1.0 / l
            pv = []
            for e, (start, size) in zip(es, _key_chunks(t)):
                p = (e * inv_l).astype(BF16)
                p_ref[0, j, :, start:start + size] = p
                pv.append(_dot(p, v_ref[start:start + size, :]))
            outs.append(functools.reduce(jnp.add, pv))
        o_ref[...] = jnp.where(lane < V_HEAD, outs[0], outs[1])
        finish()

    res = pl.pallas_call(
        body, name="attn_fwd", grid=(n // t, N_HEADS // 2, nq),
        in_specs=[qspec, kspec, vspec] + c_in, out_specs=[ospec, _probs_spec(t, tq)] + c_out,
        out_shape=[jax.ShapeDtypeStruct((n, D_ATTN), F32), jax.ShapeDtypeStruct((n // t, N_HEADS, t, t), BF16)] + c_shape,
        scratch_shapes=c_sems,
        compiler_params=pltpu.CompilerParams(dimension_semantics=("arbitrary", "arbitrary", "arbitrary"),
                                             vmem_limit_bytes=VMEM_LIMIT),
    )(q, k, v, *srcs)
    return res[0], res[1], res[2:]


def _attn_bwd(q, k, v, do, o, probs, srcs=(), scatter=()):
    n = q.shape[0]
    t = _t_pad()
    tq = t // 2
    nq, qspec, kspec, vspec, ospec = _attn_specs(t, tq)
    nk = len(srcs)
    c_in, c_out, c_shape, c_sems = _exchange_specs(srcs, scatter)

    def body(q_ref, k_ref, v_ref, do_ref, o_ref, p_ref, *rest):
        dq_ref, dk_ref, dv_ref = rest[nk:nk + 3]
        finish = _ride(3, *_exchange_fns(rest[:nk], rest[nk + 3:2 * nk + 3], rest[2 * nk + 3:], scatter))

        @pl.when(pl.program_id(2) == 0)
        def _():
            dk_ref[...] = jnp.zeros_like(dk_ref)
            dv_ref[...] = jnp.zeros_like(dv_ref)

        lane = lax.broadcasted_iota(jnp.int32, (tq, 2 * V_HEAD), 1)
        do = do_ref[...]
        do_o = do * o_ref[...]
        chunks = _key_chunks(t)
        dvs = [None] * len(chunks)
        for j in range(2):
            sl = slice(j * HEAD_PAD, (j + 1) * HEAD_PAD)
            qh = q_ref[:, sl]
            in_head = (lane < V_HEAD) if j == 0 else (lane >= V_HEAD)
            doh = jnp.where(in_head, do, 0.0).astype(BF16)
            delta = jnp.sum(jnp.where(in_head, do_o, 0.0), axis=-1, keepdims=True)
            dq = jnp.zeros((tq, HEAD_PAD), F32)
            for c, (start, size) in enumerate(chunks):
                rows = slice(start, start + size)
                p = p_ref[0, j, :, rows]
                dp = _dot_nt(doh, v_ref[rows, :])
                ds = (p.astype(F32) * (dp - delta) * (QK_HEAD ** -0.5)).astype(BF16)
                dq = dq + _dot(ds, k_ref[rows, sl])
                dk_ref[rows, sl] += _dot_tn(ds, qh)
                dvc = _dot_tn(p, doh)
                dvs[c] = dvc if dvs[c] is None else dvs[c] + dvc
            dq_ref[:, sl] = dq
        for (start, size), dvc in zip(chunks, dvs):
            dv_ref[start:start + size, :] += dvc
        finish()

    res = pl.pallas_call(
        body, name="attn_bwd", grid=(n // t, N_HEADS // 2, nq),
        in_specs=[qspec, kspec, vspec, ospec, ospec, _probs_spec(t, tq)] + c_in, out_specs=[qspec, kspec, vspec] + c_out,
        out_shape=[jax.ShapeDtypeStruct((n, QP_COLS), F32), jax.ShapeDtypeStruct((n, QP_COLS), F32),
                   jax.ShapeDtypeStruct((n, D_ATTN), F32)] + c_shape, scratch_shapes=c_sems,
        compiler_params=pltpu.CompilerParams(dimension_semantics=("arbitrary", "arbitrary", "arbitrary"),
                                             vmem_limit_bytes=VMEM_LIMIT),
    )(q, k, v, do, o, probs, *srcs)
    return res[:3], res[3:]


SCAN_STEPS = 8


def _scan(chains, t):
    seg = t // 8
    rows = lax.broadcasted_iota(jnp.int32, (8, LANES), 0)

    def step(i, carry):
        carry = list(carry)
        for u in range(SCAN_STEPS):
            j = i * SCAN_STEPS + u
            for n, (a_ref, b_ref, h_ref, p_ref, reverse) in enumerate(chains):
                h, p = carry[n]
                idx = pl.ds(seg - 1 - j if reverse else j, 8, stride=seg)
                a = a_ref[idx, :]
                h = a * h + b_ref[idx, :]
                p = a * p
                h_ref[idx, :] = h
                p_ref[idx, :] = p
                carry[n] = (h, p)
        return tuple(carry)

    init = tuple((jnp.zeros((8, LANES), F32), jnp.ones((8, LANES), F32)) for _ in chains)
    ends = lax.fori_loop(0, seg // SCAN_STEPS, step, init)
    for (_, _, h_ref, p_ref, reverse), (b, a) in zip(chains, ends):
        for d in (1, 2, 4):
            if reverse:
                keep = rows < 8 - d
                a_n, b_n = pltpu.roll(a, 8 - d, 0), pltpu.roll(b, 8 - d, 0)
            else:
                keep = rows >= d
                a_n, b_n = pltpu.roll(a, d, 0), pltpu.roll(b, d, 0)
            b = a * jnp.where(keep, b_n, 0.0) + b
            a = a * jnp.where(keep, a_n, 1.0)
        for s in (range(7) if reverse else range(1, 8)):
            sl = slice(s * seg, (s + 1) * seg)
            carry_in = b[s + 1:s + 2, :] if reverse else b[s - 1:s, :]
            h_ref[sl, :] = h_ref[sl, :] + p_ref[sl, :] * carry_in


def _shift_rows(x, s, rows, t):
    if s == 0:
        return x
    rolled = pltpu.roll(x, s % t, 0)
    return jnp.where(rows >= s, rolled, 0.0) if s > 0 else jnp.where(rows < t + s, rolled, 0.0)


def _neg_expm1(x, exp_x):
    series = -x * (1.0 + x * (0.5 + x * (1.0 / 6 + x * (1.0 / 24))))
    return jnp.where(x > -0.1, series, 1.0 - exp_x)


def _sigmoid(x):
    return 0.5 * jnp.tanh(0.5 * x) + 0.5


def _gelu_parts(x):
    k = math.sqrt(2.0 / math.pi)
    th = jnp.tanh(k * (x + 0.044715 * x * x * x))
    g = 0.5 * x * (1.0 + th)
    dg = 0.5 * (1.0 + th) + 0.5 * x * (1.0 - th * th) * k * (1.0 + 3 * 0.044715 * x * x)
    return g, dg


def _lru_gates(xc, gates, lam_ref, valid, d):
    r = _sigmoid(gates[:, (2 * d) * LANES:(2 * d + 1) * LANES])
    i = _sigmoid(gates[:, (2 * d + 1) * LANES:(2 * d + 2) * LANES])
    neg_lam = -lam_ref[d:d + 1, :]
    sp = jnp.maximum(neg_lam, 0.0) + jnp.log1p(jnp.exp(-jnp.abs(neg_lam)))
    log_a = -LRU_C * r * sp
    a = jnp.exp(log_a)
    m = jnp.maximum(_neg_expm1(2.0 * log_a, a * a), 0.0)
    sq = jnp.sqrt(m)
    b = jnp.where(valid, sq * (i * xc), 0.0)
    return r, i, sp, a, m, sq, b


def _conv(xr, cw_ref, cb_ref, rows, t):
    return (cw_ref[0:1, :] * _shift_rows(xr, 2, rows, t) + cw_ref[1:2, :] * _shift_rows(xr, 1, rows, t)
            + cw_ref[2:3, :] * xr + cw_ref[3:4, :] * _shift_rows(xr, -1, rows, t) + cb_ref[...])


def _rnn_specs(t):
    seq = pl.BlockSpec((t, LANES), lambda cb, b: (b, cb))
    cw = pl.BlockSpec((4, LANES), lambda cb, b: (0, cb))
    vec1 = pl.BlockSpec((1, LANES), lambda cb, b: (0, cb))
    vec2 = pl.BlockSpec((2, LANES), lambda cb, b: (0, cb))
    wblk = pl.BlockSpec((1, LANES, 4 * LANES), lambda cb, b: (cb, 0, 0))
    gbias = pl.BlockSpec((1, 1, 4 * LANES), lambda cb, b: (cb, 0, 0))
    return seq, cw, vec1, vec2, wblk, gbias


def _rnn_fwd(xr, xg, conv_w, conv_b, wblk, gbias, lam):
    n = xr.shape[0]
    t = _t_pad()
    seq, cw, vec1, vec2, wspec, gspec = _rnn_specs(t)

    def body(xr_ref, xg_ref, cw_ref, cb_ref, w_ref, gb_ref, lam_ref, o_ref, a_s, b_s, h_s, p_s):
        rows = lax.broadcasted_iota(jnp.int32, (t, LANES), 0)
        valid = rows >= PAD_ROWS
        xc = _conv(xr_ref[...], cw_ref, cb_ref, rows, t)
        gates = _dot(xc.astype(BF16), w_ref[0]) + gb_ref[0]
        for d in range(2):
            _, _, _, a, _, _, b = _lru_gates(xc, gates, lam_ref, valid, d)
            a_s[d] = a
            b_s[d] = b
        _scan([(a_s.at[d], b_s.at[d], h_s.at[d], p_s.at[d], d == 1) for d in range(2)], t)
        g, _ = _gelu_parts(xg_ref[...])
        o_ref[...] = (h_s[0] + h_s[1]) * g

    return pl.pallas_call(
        body, name="rnn_fwd", grid=(D_RNN // LANES, n // t),
        in_specs=[seq, seq, cw, vec1, wspec, gspec, vec2], out_specs=seq,
        out_shape=jax.ShapeDtypeStruct((n, D_RNN), F32),
        scratch_shapes=[pltpu.VMEM((2, t, LANES), F32)] * 4,
        compiler_params=pltpu.CompilerParams(dimension_semantics=("parallel", "parallel"), vmem_limit_bytes=VMEM_LIMIT),
    )(xr, xg, conv_w, conv_b, wblk, gbias, lam)


def _rnn_bwd(xr, xg, do, conv_w, conv_b, wblk, gbias, lam, srcs=(), scatter=()):
    n = xr.shape[0]
    t = _t_pad()
    seq, cw, vec1, vec2, wspec, gspec = _rnn_specs(t)
    nk = len(srcs)
    c_in, c_out, c_shape, c_sems = _exchange_specs(srcs, scatter)

    def body(xr_ref, xg_ref, do_ref, cw_ref, cb_ref, w_ref, gb_ref, lam_ref, *rest):
        dxr_ref, dxg_ref, dcw_ref, dcb_ref, dw_ref, dgb_ref, dlam_ref = rest[nk:nk + 7]
        a_s, b_s, h_s, l_s, p_s, back_s, r_s, i_s, q_s, dg_s = rest[2 * nk + 7 + len(c_sems):]
        finish = _ride(2, *_exchange_fns(rest[:nk], rest[nk + 7:2 * nk + 7], rest[2 * nk + 7:2 * nk + 7 + len(c_sems)],
                                         scatter))
        first = pl.program_id(1) == 0
        rows = lax.broadcasted_iota(jnp.int32, (t, LANES), 0)
        valid = rows >= PAD_ROWS
        xr = xr_ref[...]
        xc = _conv(xr, cw_ref, cb_ref, rows, t)
        xcb = xc.astype(BF16)
        gates = _dot(xcb, w_ref[0]) + gb_ref[0]
        sps = []
        for d in range(2):
            r_s[d], i_s[d], sp, a_s[d], _, q_s[d], b_s[d] = _lru_gates(xc, gates, lam_ref, valid, d)
            sps.append(sp)
        _scan([(a_s.at[d], b_s.at[d], h_s.at[d], p_s.at[d], d == 1) for d in range(2)], t)
        g, dg = _gelu_parts(xg_ref[...])
        do = do_ref[...]
        dxg_ref[...] = do * (h_s[0] + h_s[1]) * dg
        b_s[0] = do * g
        for d in range(2):
            back_s[d] = _shift_rows(a_s[d], -1 if d == 0 else 1, rows, t)
        _scan([(back_s.at[d], b_s.at[0], l_s.at[d], p_s.at[d], d == 0) for d in range(2)], t)
        dxc = jnp.zeros((t, LANES), F32)
        dlams = []
        for d in range(2):
            r, i, sp, a, sq = r_s[d], i_s[d], sps[d], a_s[d], q_s[d]
            lam_t = l_s[d]
            da = lam_t * _shift_rows(h_s[d], 1 if d == 0 else -1, rows, t)
            lam_v = jnp.where(valid, lam_t, 0.0)
            dsq = lam_v * (i * xc)
            di = lam_v * sq * xc
            dxc = dxc + lam_v * sq * i
            dm = jnp.where(sq > 0.0, dsq * 0.5 / jnp.where(sq > 0.0, sq, 1.0), 0.0)
            dla = da * a - 2.0 * dm * a * a
            dr = dla * (-LRU_C) * sp
            dsp = _colsum(dla * (-LRU_C) * r)
            dlams.append(dsp * -jax.nn.sigmoid(-lam_ref[d:d + 1, :]))
            dg_s[:, (2 * d) * LANES:(2 * d + 1) * LANES] = (dr * r * (1.0 - r)).astype(BF16)
            dg_s[:, (2 * d + 1) * LANES:(2 * d + 2) * LANES] = (di * i * (1.0 - i)).astype(BF16)
        dgates = dg_s[...]
        dxc = dxc + _dot_nt(dgates, w_ref[0])
        taps = [_shift_rows(dxc, j - 2, rows, t) for j in range(4)]
        dxr_ref[...] = (cw_ref[0:1, :] * taps[0] + cw_ref[1:2, :] * taps[1] + cw_ref[2:3, :] * taps[2]
                        + cw_ref[3:4, :] * taps[3])
        dcw = jnp.concatenate([_colsum(tap * xr) for tap in taps], axis=0)
        _acc(dcw_ref, first, dcw)
        _acc(dcb_ref, first, _colsum(dxc))
        _acc(dw_ref, first, _dot_tn(xcb, dgates)[None])
        _acc(dgb_ref, first, _colsum(dgates.astype(F32))[None])
        _acc(dlam_ref, first, jnp.concatenate(dlams, axis=0))
        finish()

    res = pl.pallas_call(
        body, name="rnn_bwd", grid=(D_RNN // LANES, n // t),
        in_specs=[seq, seq, seq, cw, vec1, wspec, gspec, vec2] + c_in,
        out_specs=[seq, seq, cw, vec1, wspec, gspec, vec2] + c_out,
        out_shape=[jax.ShapeDtypeStruct((n, D_RNN), F32), jax.ShapeDtypeStruct((n, D_RNN), F32),
                   jax.ShapeDtypeStruct((4, D_RNN), F32), jax.ShapeDtypeStruct((1, D_RNN), F32),
                   jax.ShapeDtypeStruct((D_RNN // LANES, LANES, 4 * LANES), F32),
                   jax.ShapeDtypeStruct((D_RNN // LANES, 1, 4 * LANES), F32), jax.ShapeDtypeStruct((2, D_RNN), F32)]
        + c_shape,
        scratch_shapes=c_sems + [pltpu.VMEM((2, t, LANES), F32)] * 9 + [pltpu.VMEM((t, 4 * LANES), BF16)],
        compiler_params=pltpu.CompilerParams(dimension_semantics=("arbitrary", "arbitrary"), vmem_limit_bytes=VMEM_LIMIT),
    )(xr, xg, do, conv_w, conv_b, wblk, gbias, lam, *srcs)
    return res[:7], res[7:]


def _post(oa, orn, h0, tgt, ga, gr, g2, w_out, w_gate, w_up, w_down):
    n = oa.shape[0]
    tm = _row_tile(n)
    t = _t_pad()

    def body(oa_ref, or_ref, h0_ref, tgt_ref, ga_ref, gr_ref, g2_ref, wo_ref, wg_ref, wu_ref, wd_ref,
             doa_ref, dor_ref, dh1_ref, mix_ref, h1n_ref, act_ref, dgate_ref, dup_ref, dy_ref,
             loss_ref, dga_ref, dgr_ref, dg2_ref, gate_s, up_s):
        first = pl.program_id(0) == 0
        xa, ra = _rms(oa_ref[...], D_ATTN)
        xr, rr = _rms(or_ref[...], D_RNN)
        mix = jnp.concatenate([(xa * ga_ref[...]).astype(BF16), (xr * gr_ref[...]).astype(BF16)], axis=-1)
        mix_ref[...] = mix.T
        h1 = h0_ref[...] + _dot(mix, wo_ref[...])
        x2, r2 = _rms(h1, D_MODEL)
        h1n = (x2 * g2_ref[...]).astype(BF16)
        h1n_ref[...] = h1n
        y = h1
        for cs in range(0, D_FF, FF_CHUNK):
            sl = slice(cs, cs + FF_CHUNK)
            gate = _dot_nt(h1n, wg_ref[sl, :])
            up = _dot_nt(h1n, wu_ref[sl, :])
            gate_s[:, sl] = gate
            up_s[:, sl] = up
            act = (gate * _sigmoid(gate) * up).astype(BF16)
            act_ref[sl, :] = act.T
            y = y + _dot(act, wd_ref[sl, :])
        row = pl.program_id(0) * tm + lax.broadcasted_iota(jnp.int32, (tm, 1), 0)
        for _ in range(1, n // t):
            row = jnp.where(row >= t, row - t, row)
        err = jnp.where(row >= PAD_ROWS + N_META, y - tgt_ref[...], 0.0)
        _acc(loss_ref, first, jnp.full((1, LANES), 0.5 / D_MODEL, F32) * jnp.sum(err * err))
        dy = err * (1.0 / D_MODEL)
        dyb = dy.astype(BF16)
        dy_ref[...] = dyb
        dh1n = jnp.zeros((tm, D_MODEL), F32)
        for cs in range(0, D_FF, FF_CHUNK):
            sl = slice(cs, cs + FF_CHUNK)
            dact = _dot_nt(dyb, wd_ref[sl, :])
            gate, up = gate_s[:, sl], up_s[:, sl]
            sg = _sigmoid(gate)
            dgate = (dact * up * sg * (1.0 + gate * (1.0 - sg))).astype(BF16)
            dup = (dact * gate * sg).astype(BF16)
            dgate_ref[sl, :] = dgate.T
            dup_ref[sl, :] = dup.T
            dh1n = dh1n + _dot(dgate, wg_ref[sl, :]) + _dot(dup, wu_ref[sl, :])
        _acc(dg2_ref, first, _colsum(dh1n * x2))
        dh1 = dy + _rms_bwd(dh1n, x2, r2, g2_ref[...], D_MODEL)
        dh1_ref[...] = dh1
        dmix = _dot_nt(dh1.astype(BF16), wo_ref[...])
        dma, dmr = dmix[:, :D_ATTN], dmix[:, D_ATTN:]
        _acc(dga_ref, first, _colsum(dma * xa))
        _acc(dgr_ref, first, _colsum(dmr * xr))
        doa_ref[...] = _rms_bwd(dma, xa, ra, ga_ref[...], D_ATTN)
        dor_ref[...] = _rms_bwd(dmr, xr, rr, gr_ref[...], D_RNN)

    def row(w):
        return pl.BlockSpec((tm, w), lambda i: (i, 0))

    def acc(w):
        return pl.BlockSpec((1, w), lambda i: (0, 0))

    def col(w):
        return pl.BlockSpec((w, tm), lambda i: (0, i))

    outs = [(D_ATTN, F32, row), (D_RNN, F32, row), (D_MODEL, F32, row), (D_MODEL, BF16, col), (D_MODEL, BF16, row),
            (D_FF, BF16, col), (D_FF, BF16, col), (D_FF, BF16, col), (D_MODEL, BF16, row)]
    accs = [LANES, D_ATTN, D_RNN, D_MODEL]
    return pl.pallas_call(
        body, name="post", grid=(n // tm,),
        in_specs=[row(D_ATTN), row(D_RNN), row(D_MODEL), row(D_MODEL),
                  _const_spec((1, D_ATTN)), _const_spec((1, D_RNN)), _const_spec((1, D_MODEL)),
                  _const_spec((D_MODEL, D_MODEL)), _const_spec((D_FF, D_MODEL)), _const_spec((D_FF, D_MODEL)),
                  _const_spec((D_FF, D_MODEL))],
        out_specs=[spec(w) for w, _, spec in outs] + [acc(w) for w in accs],
        out_shape=[jax.ShapeDtypeStruct((n, w) if spec is row else (w, n), dt) for w, dt, spec in outs]
        + [jax.ShapeDtypeStruct((1, w), F32) for w in accs],
        scratch_shapes=[pltpu.VMEM((tm, D_FF), F32), pltpu.VMEM((tm, D_FF), F32)],
        compiler_params=pltpu.CompilerParams(dimension_semantics=("arbitrary",), vmem_limit_bytes=VMEM_LIMIT),
    )(oa, orn, h0, tgt, ga, gr, g2, w_out, w_gate, w_up, w_down)


def _in_bwd(dp, h0, dh1, ln1_g, w_in_p, srcs=(), scatter=()):
    n = h0.shape[0]
    tm = _row_tile(n)
    nk = len(srcs)
    c_in, c_out, c_shape, c_sems = _exchange_specs(srcs, scatter)

    def body(dp_ref, h0_ref, dh1_ref, g_ref, w_ref, *rest):
        dh0_ref, dg_ref = rest[nk:nk + 2]
        finish = _ride(1, *_exchange_fns(rest[:nk], rest[nk + 2:2 * nk + 2], rest[2 * nk + 2:], scatter))
        dhn = _dot(dp_ref[...], w_ref[...])
        xhat, r = _rms(h0_ref[...], D_MODEL)
        _acc(dg_ref, pl.program_id(0) == 0, _colsum(dhn * xhat))
        dh0_ref[...] = dh1_ref[...] + _rms_bwd(dhn, xhat, r, g_ref[...], D_MODEL)
        finish()

    def row(w):
        return pl.BlockSpec((tm, w), lambda i: (i, 0))

    res = pl.pallas_call(
        body, name="in_bwd", grid=(n // tm,),
        in_specs=[row(P_COLS), row(D_MODEL), row(D_MODEL), _const_spec((1, D_MODEL)), _const_spec((P_COLS, D_MODEL))] + c_in,
        out_specs=[row(D_MODEL), pl.BlockSpec((1, D_MODEL), lambda i: (0, 0))] + c_out,
        out_shape=[jax.ShapeDtypeStruct((n, D_MODEL), F32), jax.ShapeDtypeStruct((1, D_MODEL), F32)] + c_shape,
        scratch_shapes=c_sems,
        compiler_params=pltpu.CompilerParams(dimension_semantics=("arbitrary",), vmem_limit_bytes=VMEM_LIMIT),
    )(dp, h0, dh1, ln1_g, w_in_p, *srcs)
    return res[:2], res[2:]


MAX_TILE = FF_CHUNK


def _pick_tile(width, cap):
    best = LANES
    for mult in range(1, width // LANES + 1):
        cand = mult * LANES
        if width % cand == 0 and cand <= cap:
            best = cand
    return best


def _matmul_tn(name, a, b, srcs=(), scatter=()):
    n, ka = a.shape
    kb = b.shape[1]
    ta, tb = _pick_tile(ka, MAX_TILE), _pick_tile(kb, MAX_TILE)
    tk = n // 4
    nk = len(srcs)
    c_in, c_out, c_shape, c_sems = _exchange_specs(srcs, scatter)

    def body(a_ref, b_ref, *rest):
        o_ref = rest[nk]
        finish = _ride(3, *_exchange_fns(rest[:nk], rest[nk + 1:2 * nk + 1], rest[2 * nk + 1:], scatter))
        _acc(o_ref, pl.program_id(2) == 0, _dot_tn(a_ref[...].astype(BF16), b_ref[...].astype(BF16)))
        finish()

    res = pl.pallas_call(
        body, name=name, grid=(ka // ta, kb // tb, n // tk),
        in_specs=[pl.BlockSpec((tk, ta), lambda i, j, k: (k, i)), pl.BlockSpec((tk, tb), lambda i, j, k: (k, j))] + c_in,
        out_specs=[pl.BlockSpec((ta, tb), lambda i, j, k: (i, j))] + c_out,
        out_shape=[jax.ShapeDtypeStruct((ka, kb), F32)] + c_shape, scratch_shapes=c_sems,
        compiler_params=pltpu.CompilerParams(dimension_semantics=("arbitrary", "arbitrary", "arbitrary"),
                                             vmem_limit_bytes=VMEM_LIMIT),
    )(a, b, *srcs)
    return res[0], res[1:]


def _matmul_shards(name, at, b):
    ka, n = at.shape
    kb = b.shape[1]
    ta, tb = _pick_tile(ka, MAX_TILE), _pick_tile(kb, MAX_TILE)
    tk = n // 2
    width = ka // N_DEV
    per = ta // width

    def body(a_ref, b_ref, o_ref, acc_ref):
        _acc(acc_ref, pl.program_id(2) == 0, _dot(a_ref[...], b_ref[...].astype(BF16)))

        @pl.when(pl.program_id(2) == pl.num_programs(2) - 1)
        def _():
            for s in range(per):
                o_ref[s] = acc_ref[s * width:(s + 1) * width, :].astype(BF16)

    return pl.pallas_call(
        body, name=name, grid=(ka // ta, kb // tb, n // tk),
        in_specs=[pl.BlockSpec((ta, tk), lambda i, j, k: (i, k)), pl.BlockSpec((tk, tb), lambda i, j, k: (k, j))],
        out_specs=pl.BlockSpec((per, width, tb), lambda i, j, k: (i, 0, j)),
        out_shape=jax.ShapeDtypeStruct((N_DEV, width, kb), BF16),
        scratch_shapes=[pltpu.VMEM((ta, tb), F32)],
        compiler_params=pltpu.CompilerParams(dimension_semantics=("parallel", "parallel", "arbitrary"),
                                             vmem_limit_bytes=VMEM_LIMIT),
    )(at, b)


def _adamw_math(g8_ref, w_ref, m_ref, v_ref, g_ref, d_ref, nm_ref, nv_ref):
    g = g8_ref[0].astype(F32)
    for s in range(1, N_DEV):
        g = g + g8_ref[s].astype(F32)
    g_ref[...] = g
    nm = ADAM_B1 * m_ref[...] + (1.0 - ADAM_B1) * g
    nv = ADAM_B2 * v_ref[...] + (1.0 - ADAM_B2) * (g * g)
    nm_ref[...] = nm
    nv_ref[...] = nv
    m_hat = nm / (1.0 - ADAM_B1 ** ADAM_STEP)
    v_hat = nv / (1.0 - ADAM_B2 ** ADAM_STEP)
    d_ref[...] = -ADAM_LR * (m_hat / (jnp.sqrt(v_hat) + ADAM_EPS) + ADAM_WD * w_ref[...])


def _adamw_many(name, items):
    count = len(items)

    def body(*refs):
        ins, outs = refs[:4 * count], refs[4 * count:]
        for i in range(count):
            _adamw_math(*ins[4 * i:4 * i + 4], *outs[4 * i:4 * i + 4])

    flat = [a for item in items for a in item]
    res = pl.pallas_call(
        body, name=name,
        out_shape=[jax.ShapeDtypeStruct(item[1].shape, F32) for item in items for _ in range(4)],
        compiler_params=pltpu.CompilerParams(vmem_limit_bytes=VMEM_LIMIT),
    )(*flat)
    return [tuple(res[4 * i:4 * i + 4]) for i in range(count)]


def _adamw(name, g8, w, m, v):
    rows, cols = w.shape
    tr = rows
    for cand in (256, 176, 128, 64):
        if rows % cand == 0 and rows > cand:
            tr = cand
            break

    def body(*refs):
        _adamw_math(*refs)

    blk = pl.BlockSpec((tr, cols), lambda i: (i, 0))
    return pl.pallas_call(
        body, name=name, grid=(rows // tr,),
        in_specs=[pl.BlockSpec((N_DEV, tr, cols), lambda i: (0, i, 0)), blk, blk, blk],
        out_specs=[blk] * 4, out_shape=[jax.ShapeDtypeStruct((rows, cols), F32)] * 4,
        compiler_params=pltpu.CompilerParams(dimension_semantics=("parallel",), vmem_limit_bytes=VMEM_LIMIT),
    )(g8, w, m, v)


def _exchange_specs(srcs, scatter):
    nk = len(srcs)
    if not nk:
        return [], [], [], []
    any_spec = pl.BlockSpec(memory_space=pl.ANY)
    out_shape = [jax.ShapeDtypeStruct(s.shape if sc else (N_DEV,) + s.shape, s.dtype) for s, sc in zip(srcs, scatter)]
    sems = [pltpu.SemaphoreType.DMA((nk, N_DEV - 1)), pltpu.SemaphoreType.DMA((nk, N_DEV - 1)),
            pltpu.SemaphoreType.DMA((nk,))]
    return [any_spec] * nk, [any_spec] * nk, out_shape, sems


FLIPS = ((0, 0, 1), (1, 0, 0), (0, 1, 0), (1, 1, 0), (1, 0, 1), (0, 1, 1), (1, 1, 1))
N_CHIP_PEERS = 3


def _exchange_fns(src_refs, out_refs, sems, scatter):
    nk = len(src_refs)
    if not nk:
        return (lambda: None), (lambda: None), (lambda: None)
    send_sems, recv_sems, local_sems = sems
    first = 1 + N_CHIP_PEERS

    def plan():
        x, y, c = lax.axis_index("x"), lax.axis_index("y"), lax.axis_index("c")
        me = 4 * x + 2 * y + c
        peers = [(1 - x if fx else x, 1 - y if fy else y, 1 - c if fc else c) for fx, fy, fc in FLIPS]
        pids = [4 * px + 2 * py + pc for px, py, pc in peers]

        def remote(k, j, src, dst, to):
            return pltpu.make_async_remote_copy(src_ref=src, dst_ref=dst, send_sem=send_sems.at[k, j],
                                                recv_sem=recv_sems.at[k, j], device_id=to, device_id_type=MESH)

        def mine(k, dest):
            return src_refs[k].at[dest] if scatter[k] else src_refs[k]

        local = [pltpu.make_async_copy(mine(k, me), out_refs[k].at[me], local_sems.at[k]) for k in range(nk)]
        direct = [remote(k, j, mine(k, pids[j]), out_refs[k].at[me], peers[j])
                  for k in range(nk) for j in range(len(FLIPS) if scatter[k] else first)]
        relays = {(k, j): remote(k, j, out_refs[k].at[pids[j - N_CHIP_PEERS]], out_refs[k].at[pids[j - N_CHIP_PEERS]], peers[0])
                  for k in range(nk) if not scatter[k] for j in range(first, len(FLIPS))}
        arrivals = {(k, j): remote(k, j, out_refs[k].at[pids[j]], out_refs[k].at[pids[j]], peers[j])
                    for k in range(nk) for j in range(len(FLIPS))}
        return local, direct, relays, arrivals

    def start():
        local, direct, _, _ = plan()
        for cp in local + direct:
            cp.start()

    def relay():
        _, _, relays, arrivals = plan()
        for (k, j), cp in relays.items():
            arrivals[k, j - N_CHIP_PEERS].wait_recv()
            cp.start()

    def wait():
        local, direct, relays, arrivals = plan()
        for (k, j), cp in arrivals.items():
            if (k, j + N_CHIP_PEERS) not in relays:
                cp.wait_recv()
        for cp in direct + list(relays.values()):
            cp.wait_send()
        for cp in local:
            cp.wait()

    return start, relay, wait


def _grid_step(rank):
    step, total = 0, 1
    for axis in range(rank):
        step = step * pl.num_programs(axis) + pl.program_id(axis)
        total = total * pl.num_programs(axis)
    return step, total


def _ride(rank, start, relay, wait):
    step, total = _grid_step(rank)
    pl.when(step == 0)(start)
    pl.when(step == (3 * total) // 4)(relay)
    return lambda: pl.when(step == total - 1)(wait)


def _exchange(name, srcs, scatter):
    nk = len(srcs)
    c_in, c_out, c_shape, c_sems = _exchange_specs(srcs, scatter)

    def body(*refs):
        start, relay, wait = _exchange_fns(refs[:nk], refs[nk:2 * nk], refs[2 * nk:], scatter)
        start()
        relay()
        wait()

    return pl.pallas_call(body, name=name, in_specs=c_in, out_specs=c_out, out_shape=c_shape, scratch_shapes=c_sems)(*srcs)


def _cols_from_shards(g):
    return jnp.transpose(g, (1, 0, 2)).reshape(g.shape[1], -1)


def _cols_to_shards(w):
    return jnp.transpose(w.reshape(w.shape[0], N_DEV, -1), (1, 0, 2))


def _prep(x, tgt, srcs, scatter):
    nb = x.shape[0]
    t = _t_pad()
    head = PAD_ROWS + N_META
    nk = len(srcs)
    c_in, c_out, c_shape, c_sems = _exchange_specs(srcs, scatter)

    def body(x_ref, tgt_ref, *rest):
        h0_ref, tp_ref = rest[nk:nk + 2]
        finish = _ride(1, *_exchange_fns(rest[:nk], rest[nk + 2:2 * nk + 2], rest[2 * nk + 2:], scatter))
        lead = pl.program_id(0) == 0

        @pl.when(lead)
        def _():
            h0_ref[...] = jnp.zeros_like(h0_ref)
            tp_ref[...] = jnp.zeros_like(tp_ref)

        @pl.when(jnp.logical_not(lead))
        def _():
            h0_ref[...] = x_ref[...]
            tp_ref[...] = tgt_ref[...]

        finish()

    src = pl.BlockSpec((nb, head, D_MODEL), lambda j: (0, jnp.maximum(j - 1, 0), 0))
    dst = pl.BlockSpec((nb, head, D_MODEL), lambda j: (0, j, 0))
    padded = jax.ShapeDtypeStruct((nb, t, D_MODEL), F32)
    res = pl.pallas_call(
        body, name="prep", grid=(t // head,), in_specs=[src, src] + c_in, out_specs=[dst, dst] + c_out,
        out_shape=[padded, padded] + c_shape, scratch_shapes=c_sems,
        compiler_params=pltpu.CompilerParams(dimension_semantics=("arbitrary",)),
    )(x, tgt, *srcs)
    return res[0], res[1], res[2:]


def _rope_tables(n):
    t = _t_pad()
    pos = np.arange(t, dtype=np.float32) - np.float32(PAD_ROWS)
    half = QK_ROPE // 2
    freqs = (1.0 / (ROPE_THETA ** (np.arange(half, dtype=np.float32) / half))).astype(np.float32)
    ang = pos[:, None] * freqs[None, :]
    cos, sin = np.cos(ang), np.sin(ang)
    z = lambda w: np.zeros((t, w), np.float32)
    c = np.concatenate([np.ones((t, QK_NOPE), np.float32), cos, cos, z(HEAD_PAD - QK_HEAD)], axis=1)
    s1 = np.concatenate([z(QK_NOPE + half), sin, z(HEAD_PAD - QK_HEAD)], axis=1)
    s2 = np.concatenate([z(QK_NOPE), -sin, z(HEAD_PAD - QK_NOPE - half)], axis=1)
    return tuple(jnp.asarray(np.tile(a, (n // t, 1))) for a in (c, s1, s2))


def _block_diag_gates(lru_wa, lru_wi):
    eye = jnp.eye(2, dtype=lru_wa.dtype)

    def bd(w):
        w = w.reshape(2, D_RNN // LANES, 2, RNN_BW, RNN_BW)
        full = w[:, :, :, :, None, :] * eye[None, None, :, None, :, None]
        return full.reshape(2, D_RNN // LANES, LANES, LANES)

    a, i = bd(lru_wa), bd(lru_wi)
    return jnp.concatenate([a[0], i[0], a[1], i[1]], axis=-1)


def _unblock_gates(dw):
    nb = D_RNN // LANES
    parts = dw.reshape(nb, 2, RNN_BW, 4, 2, RNN_BW)
    diag = jnp.stack([parts[:, k, :, :, k, :] for k in range(2)], axis=1)
    diag = jnp.transpose(diag, (3, 0, 1, 2, 4)).reshape(4, 2 * nb, RNN_BW, RNN_BW)
    return jnp.stack([diag[0], diag[2]]), jnp.stack([diag[1], diag[3]])


WEIGHTS = ("meta_tokens", "ln1_g", "w_in", "q_a_norm_g", "w_uq", "kv_a_norm_g", "w_ukv", "q_norm_g", "k_norm_g",
           "conv_w", "conv_b", "lru_wa", "lru_ba", "lru_wi", "lru_bi", "lru_lambda", "attn_out_g", "rnn_out_g",
           "w_out", "ln2_g", "w_gate", "w_up", "w_down")
BIG = ("w_in", "w_uq", "w_ukv", "w_out", "w_gate", "w_up", "w_down")
TRANSPOSED = ("w_in", "w_uq", "w_gate", "w_up")
ROW_SHARDED = ("w_out", "w_down") + TRANSPOSED
REPLICATED = ("ln1_g", "q_a_norm_g", "kv_a_norm_g", "q_norm_g", "k_norm_g", "conv_b", "lru_wa", "lru_wi",
              "attn_out_g", "rnn_out_g", "ln2_g")
WHOLE = REPLICATED + ("loss",)
G_FIRST = ("w_in", "meta_tokens")
G_MID = ("w_uq", "w_ukv", "conv_w", "lru_ba", "lru_bi", "lru_lambda")
LATE = ("w_out", "w_gate", "w_up", "w_down")
G_LAST = ("meta_tokens", "ln1_g")


def _local_step(x, tgt, ex):
    nb = x.shape[0]
    t = _t_pad()
    n = nb * t
    local = ex.local
    h0, tgt_p, got = _prep(x, tgt, *ex.gather_srcs(G_FIRST))
    first = ex.gathered(G_FIRST, got)
    meta, w_in = first["meta_tokens"], first["w_in"]
    h0 = h0.at[:, PAD_ROWS:PAD_ROWS + N_META].set(jnp.broadcast_to(meta[None], (nb, N_META, D_MODEL))).reshape(n, D_MODEL)
    tgt_p = tgt_p.reshape(n, D_MODEL)

    zr = lambda r: jnp.zeros((r, D_MODEL), w_in.dtype)
    w_in_p = jnp.concatenate([w_in[:OFF_CKV], w_in[OFF_KR:], zr(QK_NOPE), w_in[OFF_CKV:OFF_KR], zr(HEAD_PAD - QK_HEAD)],
                             axis=0)
    pad_g = lambda g: jnp.pad(g, ((0, 0), (0, HEAD_PAD - QK_HEAD)))
    qg, kg = pad_g(local["q_norm_g"]), pad_g(local["k_norm_g"])
    rc, rs1, rs2 = _rope_tables(n)
    wblk = _block_diag_gates(local["lru_wa"].reshape(2, -1, RNN_BW, RNN_BW),
                             local["lru_wi"].reshape(2, -1, RNN_BW, RNN_BW)).astype(BF16)
    nblk = D_RNN // LANES

    (hn, cq, ckv, xr, xg, kr), got = _in_proj(h0, local["ln1_g"], w_in_p, *ex.gather_srcs(G_MID))
    w = ex.gathered(G_MID, got)
    w_uq_p = jnp.pad(w["w_uq"].reshape(N_HEADS, QK_HEAD, Q_LORA), ((0, 0), (0, HEAD_PAD - QK_HEAD), (0, 0))
                     ).reshape(QP_COLS, Q_LORA)
    ukv = w["w_ukv"].reshape(KV_LORA, N_HEADS, QK_NOPE + V_HEAD)
    w_uk_p = jnp.pad(ukv[:, :, :QK_NOPE], ((0, 0), (0, 0), (0, HEAD_PAD - QK_NOPE))).reshape(KV_LORA, QP_COLS)
    w_v = ukv[:, :, QK_NOPE:].reshape(KV_LORA, D_ATTN)
    gbias = jnp.stack([w["lru_ba"][0], w["lru_bi"][0], w["lru_ba"][1], w["lru_bi"][1]], axis=0)
    gbias = jnp.transpose(gbias.reshape(4, nblk, LANES), (1, 0, 2)).reshape(nblk, 1, 4 * LANES)

    q, k, v = _qkv_fwd(cq, ckv, kr, local["q_a_norm_g"], local["kv_a_norm_g"], w_uq_p, w_uk_p, w_v, qg, kg, rc, rs1, rs2)
    oa, probs, got = _attn_fwd(q, k, v, *ex.gather_srcs(LATE))
    late = ex.gathered(LATE, got)
    orn = _rnn_fwd(xr, xg, w["conv_w"], local["conv_b"], wblk, gbias, w["lru_lambda"])
    (doa, dor, dh1, mix_t, h1n, act_t, dgate_t, dup_t, dyb, loss, dga, dgr, dg2) = _post(
        oa, orn, h0, tgt_p, local["attn_out_g"], local["rnn_out_g"], local["ln2_g"], late["w_out"], late["w_gate"],
        late["w_up"], late["w_down"])
    wire = {"w_out": _matmul_shards("dw_out", mix_t, dh1), "w_gate": _matmul_shards("dw_gate", dgate_t, h1n),
            "w_up": _matmul_shards("dw_up", dup_t, h1n), "w_down": _matmul_shards("dw_down", act_t, dyb)}
    names = ("w_out", "w_gate")
    (dxr, dxg, dcw, dcb, dwblk, dgb, dlam), got = _rnn_bwd(xr, xg, dor, w["conv_w"], local["conv_b"], wblk, gbias,
                                                           w["lru_lambda"], *ex.scatter_srcs(names, wire))
    summed = ex.scattered(names, wire, got)
    dwa, dwi = _unblock_gates(dwblk)
    dgb = jnp.transpose(dgb.reshape(nblk, 4, LANES), (1, 0, 2)).reshape(4, D_RNN)
    wire = {"w_up": wire["w_up"], "w_down": wire["w_down"], **ex.to_wire({
        "conv_w": dcw, "conv_b": dcb, "lru_wa": dwa.reshape(-1, RNN_BW), "lru_ba": jnp.stack([dgb[0], dgb[2]]),
        "lru_wi": dwi.reshape(-1, RNN_BW), "lru_bi": jnp.stack([dgb[1], dgb[3]]), "lru_lambda": dlam,
        "attn_out_g": dga, "rnn_out_g": dgr, "ln2_g": dg2, "loss": loss})}
    names = tuple(wire)
    (dq_r, dk_r, dv), got = _attn_bwd(q, k, v, doa, oa, probs, *ex.scatter_srcs(names, wire))
    summed.update(ex.scattered(names, wire, got))
    (dp, qa, kva, dqp, dkv, dqg, dkg, dgqa, dgkva) = _qkv_bwd(
        cq, ckv, kr, dq_r, dk_r, dv, dxr, dxg, local["q_a_norm_g"], local["kv_a_norm_g"], w_uq_p, w_uk_p, w_v, qg, kg,
        rc, rs1, rs2)
    dw_uq_p, _ = _matmul_tn("dw_uq", dqp, qa)
    dw_kv, _ = _matmul_tn("dw_ukv", kva, dkv)
    dw_uq = dw_uq_p.reshape(N_HEADS, HEAD_PAD, Q_LORA)[:, :QK_HEAD].reshape(N_HEADS * QK_HEAD, Q_LORA)
    dw_ukv = jnp.concatenate([dw_kv[:, :QP_COLS].reshape(KV_LORA, N_HEADS, HEAD_PAD)[:, :, :QK_NOPE],
                              dw_kv[:, QP_COLS:].reshape(KV_LORA, N_HEADS, V_HEAD)], axis=2).reshape(KV_LORA, -1)
    wire = ex.to_wire({"q_a_norm_g": dgqa, "w_uq": dw_uq, "kv_a_norm_g": dgkva, "w_ukv": dw_ukv,
                       "q_norm_g": dqg[:, :QK_HEAD], "k_norm_g": dkg[:, :QK_HEAD]})
    names = tuple(wire)
    dw_in_p, got = _matmul_tn("dw_in", dp, hn, *ex.scatter_srcs(names, wire))
    summed.update(ex.scattered(names, wire, got))
    kr0 = OFF_CKV + 2 * D_RNN + QK_NOPE
    dw_in = jnp.concatenate([dw_in_p[:OFF_CKV], dw_in_p[kr0:kr0 + QK_ROPE], dw_in_p[OFF_CKV:OFF_CKV + 2 * D_RNN]], axis=0)
    wire = ex.to_wire({"w_in": dw_in})
    (dh0, dg1), got = _in_bwd(dp, h0, dh1, local["ln1_g"], w_in_p, *ex.scatter_srcs(("w_in",), wire))
    summed.update(ex.scattered(("w_in",), wire, got))

    dh0 = dh0.reshape(nb, t, D_MODEL)
    wire = ex.to_wire({"meta_tokens": jnp.sum(dh0[:, PAD_ROWS:PAD_ROWS + N_META], axis=0), "ln1_g": dg1})
    got = ex.run("reduce_last", *ex.scatter_srcs(G_LAST, wire))
    summed.update(ex.scattered(G_LAST, wire, got))
    return dh0[:, PAD_ROWS + N_META:], summed


class _MeshExchange:
    def __init__(self, shards):
        self.local = shards

    @staticmethod
    def run(name, srcs, scatter):
        return _exchange(name, srcs, scatter)

    def gather_srcs(self, names):
        return [self.local[k].astype(BF16) if k in BIG else self.local[k] for k in names], [False] * len(names)

    @staticmethod
    def gathered(names, outs):
        return {k: g.reshape(-1, g.shape[-1]) if k in ROW_SHARDED else _cols_from_shards(g) for k, g in zip(names, outs)}

    @staticmethod
    def to_wire(grads):
        wire = {}
        for k, g in grads.items():
            if k in WHOLE:
                wire[k] = g
            elif k in ROW_SHARDED:
                wire[k] = g.reshape(N_DEV, -1, g.shape[-1]).astype(BF16)
            else:
                wire[k] = _cols_to_shards(g).astype(BF16) if k in BIG else _cols_to_shards(g)
        return wire

    @staticmethod
    def scatter_srcs(names, wire):
        return [wire[k] for k in names], [k not in WHOLE for k in names]

    @staticmethod
    def scattered(names, wire, outs):
        return dict(zip(names, outs))


def kernel(x, meta_tokens, ln1_g, w_in, q_a_norm_g, w_uq, kv_a_norm_g, w_ukv, q_norm_g, k_norm_g, conv_w, conv_b, lru_wa, lru_ba, lru_wi, lru_bi, lru_lambda, attn_out_g, rnn_out_g, w_out, ln2_g, w_gate, w_up, w_down, loss_target, m_meta_tokens, m_ln1_g, m_w_in, m_q_a_norm_g, m_w_uq, m_kv_a_norm_g, m_w_ukv, m_q_norm_g, m_k_norm_g, m_conv_w, m_conv_b, m_lru_wa, m_lru_ba, m_lru_wi, m_lru_bi, m_lru_lambda, m_attn_out_g, m_rnn_out_g, m_w_out, m_ln2_g, m_w_gate, m_w_up, m_w_down, v_meta_tokens, v_ln1_g, v_w_in, v_q_a_norm_g, v_w_uq, v_kv_a_norm_g, v_w_ukv, v_q_norm_g, v_k_norm_g, v_conv_w, v_conv_b, v_lru_wa, v_lru_ba, v_lru_wi, v_lru_bi, v_lru_lambda, v_attn_out_g, v_rnn_out_g, v_w_out, v_ln2_g, v_w_gate, v_w_up, v_w_down):
    given = (meta_tokens, ln1_g, w_in, q_a_norm_g, w_uq, kv_a_norm_g, w_ukv, q_norm_g, k_norm_g, conv_w, conv_b,
             lru_wa, lru_ba, lru_wi, lru_bi, lru_lambda, attn_out_g, rnn_out_g, w_out, ln2_g, w_gate, w_up, w_down)
    moments_m = (m_meta_tokens, m_ln1_g, m_w_in, m_q_a_norm_g, m_w_uq, m_kv_a_norm_g, m_w_ukv, m_q_norm_g, m_k_norm_g,
                 m_conv_w, m_conv_b, m_lru_wa, m_lru_ba, m_lru_wi, m_lru_bi, m_lru_lambda, m_attn_out_g, m_rnn_out_g,
                 m_w_out, m_ln2_g, m_w_gate, m_w_up, m_w_down)
    moments_v = (v_meta_tokens, v_ln1_g, v_w_in, v_q_a_norm_g, v_w_uq, v_kv_a_norm_g, v_w_ukv, v_q_norm_g, v_k_norm_g,
                 v_conv_w, v_conv_b, v_lru_wa, v_lru_ba, v_lru_wi, v_lru_bi, v_lru_lambda, v_attn_out_g, v_rnn_out_g,
                 v_w_out, v_ln2_g, v_w_gate, v_w_up, v_w_down)
    shapes = {k: a.shape for k, a in zip(WEIGHTS, given)}

    def two_d(k, a):
        a = a.reshape(-1, a.shape[-1])
        return a.T if k in TRANSPOSED else a

    w = {k: two_d(k, a) for k, a in zip(WEIGHTS, given)}
    m = {k: two_d(k, a) for k, a in zip(WEIGHTS, moments_m)}
    v = {k: two_d(k, a) for k, a in zip(WEIGHTS, moments_v)}

    grad_x, parts = _local_step(x, loss_target, _MeshExchange(w))

    new = {k: _adamw("adamw_" + k, parts[k], w[k], m[k], v[k]) for k in BIG}
    small = [k for k in WEIGHTS if k not in BIG]
    new.update(zip(small, _adamw_many("adamw_small", [(parts[k], w[k], m[k], v[k]) for k in small])))

    loss = jnp.sum(parts["loss"][:, 0, 0])
    outs = [loss, grad_x]
    for idx in range(4):
        outs += [(new[k][idx].T if k in TRANSPOSED else new[k][idx]).reshape(shapes[k]) for k in WEIGHTS]
    return tuple(outs)
```

```python
import functools
import math

import numpy as np
import jax
import jax.numpy as jnp
from jax import lax
from jax.experimental import pallas as pl
from jax.experimental.pallas import tpu as pltpu

F32 = jnp.float32
BF16 = jnp.bfloat16

D_MODEL = 1024
N_META = 16
SEQ = 2048
N_HEADS = 8
QK_NOPE = 64
QK_ROPE = 32
QK_HEAD = QK_NOPE + QK_ROPE
V_HEAD = 64
D_ATTN = N_HEADS * V_HEAD
Q_LORA = 384
KV_LORA = 256
D_RNN = 512
RNN_BW = 64
D_FF = 2816
EPS = 1e-6
LRU_C = 8.0
ROPE_THETA = 10000.0
OFF_CKV = Q_LORA + KV_LORA
OFF_KR = OFF_CKV + QK_ROPE
IN_COLS = OFF_KR + 2 * D_RNN

ADAM_LR = 0.001
ADAM_B1 = 0.9
ADAM_B2 = 0.999
ADAM_EPS = 1e-08
ADAM_WD = 0.01
ADAM_STEP = 10

N_DEV = 8
LANES = 128
HEAD_PAD = LANES
PAD_ROWS = LANES - N_META
QP_COLS = N_HEADS * HEAD_PAD
P_COLS = OFF_CKV + 2 * D_RNN + LANES
FF_CHUNK = D_FF
VMEM_LIMIT = 56 * 1024 * 1024
MESH = pl.DeviceIdType.MESH


def _t_pad():
    return PAD_ROWS + N_META + SEQ


def _row_tile(n):
    return 256 if n % 256 == 0 else 128


def _wide_row_tile(n):
    quarter = _t_pad() // 4
    return quarter if quarter % 16 == 0 and n % quarter == 0 else _row_tile(n)


def _const_spec(shape):
    nd = len(shape)
    return pl.BlockSpec(shape, lambda *_: (0,) * nd, pipeline_mode=pl.Buffered(1))


def _rms(x, d):
    r = lax.rsqrt(jnp.sum(x * x, axis=-1, keepdims=True) * (1.0 / d) + EPS)
    return x * r, r


def _rms_bwd(dy, xhat, r, g, d):
    dxh = dy * g
    return r * (dxh - xhat * (jnp.sum(dxh * xhat, axis=-1, keepdims=True) * (1.0 / d)))


def _colsum(x):
    return jnp.sum(x, axis=0, keepdims=True)


def _dot(a, b):
    return jnp.dot(a, b, preferred_element_type=F32)


def _dot_nt(a, b):
    return lax.dot_general(a, b, (((1,), (1,)), ((), ())), preferred_element_type=F32)


def _dot_tn(a, b):
    return lax.dot_general(a, b, (((0,), (0,)), ((), ())), preferred_element_type=F32)


def _rope(x, c, s1, s2):
    return x * c + pltpu.roll(x, 16, 1) * s1 + pltpu.roll(x, HEAD_PAD - 16, 1) * s2


def _rope_bwd(dy, c, s1, s2):
    return dy * c + pltpu.roll(dy * s1, HEAD_PAD - 16, 1) + pltpu.roll(dy * s2, 16, 1)


def _acc(ref, first, val):
    @pl.when(first)
    def _():
        ref[...] = val

    @pl.when(jnp.logical_not(first))
    def _():
        ref[...] += val


def _in_proj(h0, ln1_g, w_in_p, srcs=(), scatter=()):
    n = h0.shape[0]
    tm = _wide_row_tile(n)
    nk = len(srcs)
    c_in, c_out, c_shape, c_sems = _exchange_specs(srcs, scatter)

    def body(h_ref, g_ref, w_ref, *rest):
        hn_ref, cq_ref, ckv_ref, xr_ref, xg_ref, kr_ref = rest[nk:nk + 6]
        finish = _ride(1, *_exchange_fns(rest[:nk], rest[nk + 6:2 * nk + 6], rest[2 * nk + 6:], scatter))
        xhat, _ = _rms(h_ref[...], D_MODEL)
        hn = (xhat * g_ref[...]).astype(BF16)
        hn_ref[...] = hn
        p = _dot_nt(hn, w_ref[...])
        cq_ref[...] = p[:, :Q_LORA]
        ckv_ref[...] = p[:, Q_LORA:OFF_CKV]
        xr_ref[...] = p[:, OFF_CKV:OFF_CKV + D_RNN]
        xg_ref[...] = p[:, OFF_CKV + D_RNN:OFF_CKV + 2 * D_RNN]
        kr_ref[...] = p[:, OFF_CKV + 2 * D_RNN:]
        finish()

    def row(w):
        return pl.BlockSpec((tm, w), lambda i: (i, 0))

    widths = (D_MODEL, Q_LORA, KV_LORA, D_RNN, D_RNN, LANES)
    res = pl.pallas_call(
        body, name="in_proj", grid=(n // tm,),
        in_specs=[row(D_MODEL), _const_spec((1, D_MODEL)), _const_spec((P_COLS, D_MODEL))] + c_in,
        out_specs=[row(w) for w in widths] + c_out,
        out_shape=[jax.ShapeDtypeStruct((n, w), BF16 if k == 0 else F32) for k, w in enumerate(widths)] + c_shape,
        scratch_shapes=c_sems,
        compiler_params=pltpu.CompilerParams(dimension_semantics=("arbitrary",), vmem_limit_bytes=VMEM_LIMIT),
    )(h0, ln1_g, w_in_p, *srcs)
    return res[:6], res[6:]


def _qkv_fwd(cq, ckv, kr, gqa, gkva, w_uq_p, w_uk_p, w_v, qg, kg, rc, rs1, rs2):
    n = cq.shape[0]
    tm = _wide_row_tile(n)

    def body(cq_ref, ckv_ref, kr_ref, gqa_ref, gkva_ref, wuq_ref, wuk_ref, wv_ref, qg_ref, kg_ref,
             c_ref, s1_ref, s2_ref, q_ref, k_ref, v_ref):
        xq, _ = _rms(cq_ref[...], Q_LORA)
        qa = (xq * gqa_ref[...]).astype(BF16)
        q = _dot_nt(qa, wuq_ref[...])
        xkv, _ = _rms(ckv_ref[...], KV_LORA)
        kva = (xkv * gkva_ref[...]).astype(BF16)
        kn = _dot(kva, wuk_ref[...])
        v_ref[...] = _dot(kva, wv_ref[...]).astype(BF16)
        krp = kr_ref[...]
        c, s1, s2 = c_ref[...], s1_ref[...], s2_ref[...]
        for h in range(N_HEADS):
            sl = slice(h * HEAD_PAD, (h + 1) * HEAD_PAD)
            qh, _ = _rms(q[:, sl], QK_HEAD)
            q_ref[:, sl] = _rope(qh * qg_ref[...], c, s1, s2).astype(BF16)
            kh, _ = _rms(kn[:, sl] + krp, QK_HEAD)
            k_ref[:, sl] = _rope(kh * kg_ref[...], c, s1, s2).astype(BF16)

    def row(w):
        return pl.BlockSpec((tm, w), lambda i: (i, 0))

    return pl.pallas_call(
        body, name="qkv_fwd", grid=(n // tm,),
        in_specs=[row(Q_LORA), row(KV_LORA), row(LANES), _const_spec((1, Q_LORA)), _const_spec((1, KV_LORA)),
                  _const_spec((QP_COLS, Q_LORA)), _const_spec((KV_LORA, QP_COLS)), _const_spec((KV_LORA, D_ATTN)),
                  _const_spec((1, LANES)), _const_spec((1, LANES)), row(LANES), row(LANES), row(LANES)],
        out_specs=[row(QP_COLS), row(QP_COLS), row(D_ATTN)],
        out_shape=[jax.ShapeDtypeStruct((n, QP_COLS), BF16), jax.ShapeDtypeStruct((n, QP_COLS), BF16),
                   jax.ShapeDtypeStruct((n, D_ATTN), BF16)],
        compiler_params=pltpu.CompilerParams(dimension_semantics=("parallel",), vmem_limit_bytes=VMEM_LIMIT),
    )(cq, ckv, kr, gqa, gkva, w_uq_p, w_uk_p, w_v, qg, kg, rc, rs1, rs2)


def _qkv_bwd(cq, ckv, kr, dq_r, dk_r, dv, dxr, dxg, gqa, gkva, w_uq_p, w_uk_p, w_v, qg, kg, rc, rs1, rs2):
    n = cq.shape[0]
    tm = _wide_row_tile(n)

    def body(cq_ref, ckv_ref, kr_ref, dq_ref, dk_ref, dv_ref, dxr_ref, dxg_ref, gqa_ref, gkva_ref, wuq_ref, wuk_ref,
             wv_ref, qg_ref, kg_ref, c_ref, s1_ref, s2_ref,
             dp_ref, qa_ref, kva_ref, dqp_ref, dkv_ref, dqg_ref, dkg_ref, dgqa_ref, dgkva_ref):
        first = pl.program_id(0) == 0
        dp_ref[:, OFF_CKV:OFF_CKV + D_RNN] = dxr_ref[...].astype(BF16)
        dp_ref[:, OFF_CKV + D_RNN:OFF_CKV + 2 * D_RNN] = dxg_ref[...].astype(BF16)
        xq, rq = _rms(cq_ref[...], Q_LORA)
        qa = (xq * gqa_ref[...]).astype(BF16)
        qa_ref[...] = qa
        q = _dot_nt(qa, wuq_ref[...])
        xkv, rkv = _rms(ckv_ref[...], KV_LORA)
        kva = (xkv * gkva_ref[...]).astype(BF16)
        kva_ref[...] = kva
        kn = _dot(kva, wuk_ref[...])
        krp = kr_ref[...]
        c, s1, s2 = c_ref[...], s1_ref[...], s2_ref[...]
        lane = lax.broadcasted_iota(jnp.int32, (tm, HEAD_PAD), 1)
        rope_lanes = jnp.logical_and(lane >= QK_NOPE, lane < QK_HEAD)
        dqg = jnp.zeros((1, HEAD_PAD), F32)
        dkg = jnp.zeros((1, HEAD_PAD), F32)
        dkr = jnp.zeros((tm, HEAD_PAD), F32)
        for h in range(N_HEADS):
            sl = slice(h * HEAD_PAD, (h + 1) * HEAD_PAD)
            qh, rqh = _rms(q[:, sl], QK_HEAD)
            dy = _rope_bwd(dq_ref[:, sl], c, s1, s2)
            dqg = dqg + _colsum(dy * qh)
            dqp_ref[:, sl] = _rms_bwd(dy, qh, rqh, qg_ref[...], QK_HEAD).astype(BF16)
            kh, rkh = _rms(kn[:, sl] + krp, QK_HEAD)
            dyk = _rope_bwd(dk_ref[:, sl], c, s1, s2)
            dkg = dkg + _colsum(dyk * kh)
            dkh = _rms_bwd(dyk, kh, rkh, kg_ref[...], QK_HEAD)
            dkv_ref[:, sl] = dkh.astype(BF16)
            dkr = dkr + jnp.where(rope_lanes, dkh, 0.0)
        dkv_ref[:, QP_COLS:] = dv_ref[...].astype(BF16)
        dp_ref[:, OFF_CKV + 2 * D_RNN:] = dkr.astype(BF16)
        dqa = _dot(dqp_ref[...], wuq_ref[...])
        dp_ref[:, :Q_LORA] = _rms_bwd(dqa, xq, rq, gqa_ref[...], Q_LORA).astype(BF16)
        dkva = _dot_nt(dkv_ref[:, :QP_COLS], wuk_ref[...]) + _dot_nt(dkv_ref[:, QP_COLS:], wv_ref[...])
        dp_ref[:, Q_LORA:OFF_CKV] = _rms_bwd(dkva, xkv, rkv, gkva_ref[...], KV_LORA).astype(BF16)
        _acc(dqg_ref, first, dqg)
        _acc(dkg_ref, first, dkg)
        _acc(dgqa_ref, first, _colsum(dqa * xq))
        _acc(dgkva_ref, first, _colsum(dkva * xkv))

    def row(w):
        return pl.BlockSpec((tm, w), lambda i: (i, 0))

    def acc(w):
        return pl.BlockSpec((1, w), lambda i: (0, 0))

    return pl.pallas_call(
        body, name="qkv_bwd", grid=(n // tm,),
        in_specs=[row(Q_LORA), row(KV_LORA), row(LANES), row(QP_COLS), row(QP_COLS), row(D_ATTN), row(D_RNN), row(D_RNN),
                  _const_spec((1, Q_LORA)), _const_spec((1, KV_LORA)),
                  _const_spec((QP_COLS, Q_LORA)), _const_spec((KV_LORA, QP_COLS)), _const_spec((KV_LORA, D_ATTN)),
                  _const_spec((1, LANES)), _const_spec((1, LANES)), row(LANES), row(LANES), row(LANES)],
        out_specs=[row(P_COLS), row(Q_LORA), row(KV_LORA), row(QP_COLS),
                   row(QP_COLS + D_ATTN), acc(LANES), acc(LANES), acc(Q_LORA), acc(KV_LORA)],
        out_shape=[jax.ShapeDtypeStruct((n, P_COLS), BF16), jax.ShapeDtypeStruct((n, Q_LORA), BF16),
                   jax.ShapeDtypeStruct((n, KV_LORA), BF16), jax.ShapeDtypeStruct((n, QP_COLS), BF16),
                   jax.ShapeDtypeStruct((n, QP_COLS + D_ATTN), BF16),
                   jax.ShapeDtypeStruct((1, LANES), F32), jax.ShapeDtypeStruct((1, LANES), F32),
                   jax.ShapeDtypeStruct((1, Q_LORA), F32), jax.ShapeDtypeStruct((1, KV_LORA), F32)],
        compiler_params=pltpu.CompilerParams(dimension_semantics=("arbitrary",), vmem_limit_bytes=VMEM_LIMIT),
    )(cq, ckv, kr, dq_r, dk_r, dv, dxr, dxg, gqa, gkva, w_uq_p, w_uk_p, w_v, qg, kg, rc, rs1, rs2)


KEY_CHUNK = 4 * LANES


def _key_chunks(t):
    count = max(t // KEY_CHUNK, 1)
    first = t - KEY_CHUNK * (count - 1)
    return [(0, first)] + [(first + KEY_CHUNK * c, KEY_CHUNK) for c in range(count - 1)]


def _softmax_parts(qh, k_ref, sl, tq, t):
    scores = []
    for start, size in _key_chunks(t):
        s = _dot_nt(qh, k_ref[start:start + size, sl]) * (QK_HEAD ** -0.5)
        if start < PAD_ROWS:
            key = lax.broadcasted_iota(jnp.int32, (tq, size), 1) + start
            s = jnp.where(key >= PAD_ROWS, s, -jnp.inf)
        scores.append(s)
    top = functools.reduce(jnp.maximum, [jnp.max(s, axis=-1, keepdims=True) for s in scores])
    es = [jnp.exp(s - top) for s in scores]
    return es, functools.reduce(jnp.add, [jnp.sum(e, axis=-1, keepdims=True) for e in es])


def _attn_specs(t, tq):
    nq = t // tq
    qspec = pl.BlockSpec((tq, 2 * HEAD_PAD), lambda b, hp, i: (b * nq + i, hp))
    kspec = pl.BlockSpec((t, 2 * HEAD_PAD), lambda b, hp, i: (b, hp))
    vspec = pl.BlockSpec((t, 2 * V_HEAD), lambda b, hp, i: (b, hp))
    ospec = pl.BlockSpec((tq, 2 * V_HEAD), lambda b, hp, i: (b * nq + i, hp))
    return nq, qspec, kspec, vspec, ospec


def _probs_spec(t, tq):
    return pl.BlockSpec((1, 2, tq, t), lambda b, hp, i: (b, hp, i, 0))


def _attn_fwd(q, k, v, srcs=(), scatter=()):
    n = q.shape[0]
    t = _t_pad()
    tq = t // 2
    nq, qspec, kspec, vspec, ospec = _attn_specs(t, tq)
    nk = len(srcs)
    c_in, c_out, c_shape, c_sems = _exchange_specs(srcs, scatter)

    def body(q_ref, k_ref, v_ref, *rest):
        o_ref, p_ref = rest[nk:nk + 2]
        finish = _ride(3, *_exchange_fns(rest[:nk], rest[nk + 2:2 * nk + 2], rest[2 * nk + 2:], scatter))
        lane = lax.broadcasted_iota(jnp.int32, (tq, 2 * V_HEAD), 1)
        outs = []
        for j in range(2):
            sl = slice(j * HEAD_PAD, (j + 1) * HEAD_PAD)
            es, l = _softmax_parts(q_ref[:, sl], k_ref, sl, tq, t)
            inv_l = 1.0 / l
            pv = []
            for e, (start, size) in zip(es, _key_chunks(t)):
                p = (e * inv_l).astype(BF16)
                p_ref[0, j, :, start:start + size] = p
                pv.append(_dot(p, v_ref[start:start + size, :]))
            outs.append(functools.reduce(jnp.add, pv))
        o_ref[...] = jnp.where(lane < V_HEAD, outs[0], outs[1])
        finish()

    res = pl.pallas_call(
        body, name="attn_fwd", grid=(n // t, N_HEADS // 2, nq),
        in_specs=[qspec, kspec, vspec] + c_in, out_specs=[ospec, _probs_spec(t, tq)] + c_out,
        out_shape=[jax.ShapeDtypeStruct((n, D_ATTN), F32), jax.ShapeDtypeStruct((n // t, N_HEADS, t, t), BF16)] + c_shape,
        scratch_shapes=c_sems,
        compiler_params=pltpu.CompilerParams(dimension_semantics=("arbitrary", "arbitrary", "arbitrary"),
                                             vmem_limit_bytes=VMEM_LIMIT),
    )(q, k, v, *srcs)
    return res[0], res[1], res[2:]


def _attn_bwd(q, k, v, do, o, probs, srcs=(), scatter=()):
    n = q.shape[0]
    t = _t_pad()
    tq = t // 2
    nq, qspec, kspec, vspec, ospec = _attn_specs(t, tq)
    nk = len(srcs)
    c_in, c_out, c_shape, c_sems = _exchange_specs(srcs, scatter)

    def body(q_ref, k_ref, v_ref, do_ref, o_ref, p_ref, *rest):
        dq_ref, dk_ref, dv_ref = rest[nk:nk + 3]
        finish = _ride(3, *_exchange_fns(rest[:nk], rest[nk + 3:2 * nk + 3], rest[2 * nk + 3:], scatter))

        @pl.when(pl.program_id(2) == 0)
        def _():
            dk_ref[...] = jnp.zeros_like(dk_ref)
            dv_ref[...] = jnp.zeros_like(dv_ref)

        lane = lax.broadcasted_iota(jnp.int32, (tq, 2 * V_HEAD), 1)
        do = do_ref[...]
        do_o = do * o_ref[...]
        chunks = _key_chunks(t)
        dvs = [None] * len(chunks)
        for j in range(2):
            sl = slice(j * HEAD_PAD, (j + 1) * HEAD_PAD)
            qh = q_ref[:, sl]
            in_head = (lane < V_HEAD) if j == 0 else (lane >= V_HEAD)
            doh = jnp.where(in_head, do, 0.0).astype(BF16)
            delta = jnp.sum(jnp.where(in_head, do_o, 0.0), axis=-1, keepdims=True)
            dq = jnp.zeros((tq, HEAD_PAD), F32)
            for c, (start, size) in enumerate(chunks):
                rows = slice(start, start + size)
                p = p_ref[0, j, :, rows]
                dp = _dot_nt(doh, v_ref[rows, :])
                ds = (p.astype(F32) * (dp - delta) * (QK_HEAD ** -0.5)).astype(BF16)
                dq = dq + _dot(ds, k_ref[rows, sl])
                dk_ref[rows, sl] += _dot_tn(ds, qh)
                dvc = _dot_tn(p, doh)
                dvs[c] = dvc if dvs[c] is None else dvs[c] + dvc
            dq_ref[:, sl] = dq
        for (start, size), dvc in zip(chunks, dvs):
            dv_ref[start:start + size, :] += dvc
        finish()

    res = pl.pallas_call(
        body, name="attn_bwd", grid=(n // t, N_HEADS // 2, nq),
        in_specs=[qspec, kspec, vspec, ospec, ospec, _probs_spec(t, tq)] + c_in, out_specs=[qspec, kspec, vspec] + c_out,
        out_shape=[jax.ShapeDtypeStruct((n, QP_COLS), F32), jax.ShapeDtypeStruct((n, QP_COLS), F32),
                   jax.ShapeDtypeStruct((n, D_ATTN), F32)] + c_shape, scratch_shapes=c_sems,
        compiler_params=pltpu.CompilerParams(dimension_semantics=("arbitrary", "arbitrary", "arbitrary"),
                                             vmem_limit_bytes=VMEM_LIMIT),
    )(q, k, v, do, o, probs, *srcs)
    return res[:3], res[3:]


SCAN_STEPS = 8


def _scan(chains, t):
    seg = t // 8
    rows = lax.broadcasted_iota(jnp.int32, (8, LANES), 0)

    def step(i, carry):
        carry = list(carry)
        for u in range(SCAN_STEPS):
            j = i * SCAN_STEPS + u
            for n, (a_ref, b_ref, h_ref, p_ref, reverse) in enumerate(chains):
                h, p = carry[n]
                idx = pl.ds(seg - 1 - j if reverse else j, 8, stride=seg)
                a = a_ref[idx, :]
                h = a * h + b_ref[idx, :]
                p = a * p
                h_ref[idx, :] = h
                p_ref[idx, :] = p
                carry[n] = (h, p)
        return tuple(carry)

    init = tuple((jnp.zeros((8, LANES), F32), jnp.ones((8, LANES), F32)) for _ in chains)
    ends = lax.fori_loop(0, seg // SCAN_STEPS, step, init)
    for (_, _, h_ref, p_ref, reverse), (b, a) in zip(chains, ends):
        for d in (1, 2, 4):
            if reverse:
                keep = rows < 8 - d
                a_n, b_n = pltpu.roll(a, 8 - d, 0), pltpu.roll(b, 8 - d, 0)
            else:
                keep = rows >= d
                a_n, b_n = pltpu.roll(a, d, 0), pltpu.roll(b, d, 0)
            b = a * jnp.where(keep, b_n, 0.0) + b
            a = a * jnp.where(keep, a_n, 1.0)
        for s in (range(7) if reverse else range(1, 8)):
            sl = slice(s * seg, (s + 1) * seg)
            carry_in = b[s + 1:s + 2, :] if reverse else b[s - 1:s, :]
            h_ref[sl, :] = h_ref[sl, :] + p_ref[sl, :] * carry_in


def _shift_rows(x, s, rows, t):
    if s == 0:
        return x
    rolled = pltpu.roll(x, s % t, 0)
    return jnp.where(rows >= s, rolled, 0.0) if s > 0 else jnp.where(rows < t + s, rolled, 0.0)


def _neg_expm1(x, exp_x):
    series = -x * (1.0 + x * (0.5 + x * (1.0 / 6 + x * (1.0 / 24))))
    return jnp.where(x > -0.1, series, 1.0 - exp_x)


def _sigmoid(x):
    return 0.5 * jnp.tanh(0.5 * x) + 0.5


def _gelu_parts(x):
    k = math.sqrt(2.0 / math.pi)
    th = jnp.tanh(k * (x + 0.044715 * x * x * x))
    g = 0.5 * x * (1.0 + th)
    dg = 0.5 * (1.0 + th) + 0.5 * x * (1.0 - th * th) * k * (1.0 + 3 * 0.044715 * x * x)
    return g, dg


def _lru_gates(xc, gates, lam_ref, valid, d):
    r = _sigmoid(gates[:, (2 * d) * LANES:(2 * d + 1) * LANES])
    i = _sigmoid(gates[:, (2 * d + 1) * LANES:(2 * d + 2) * LANES])
    neg_lam = -lam_ref[d:d + 1, :]
    sp = jnp.maximum(neg_lam, 0.0) + jnp.log1p(jnp.exp(-jnp.abs(neg_lam)))
    log_a = -LRU_C * r * sp
    a = jnp.exp(log_a)
    m = jnp.maximum(_neg_expm1(2.0 * log_a, a * a), 0.0)
    sq = jnp.sqrt(m)
    b = jnp.where(valid, sq * (i * xc), 0.0)
    return r, i, sp, a, m, sq, b


def _conv(xr, cw_ref, cb_ref, rows, t):
    return (cw_ref[0:1, :] * _shift_rows(xr, 2, rows, t) + cw_ref[1:2, :] * _shift_rows(xr, 1, rows, t)
            + cw_ref[2:3, :] * xr + cw_ref[3:4, :] * _shift_rows(xr, -1, rows, t) + cb_ref[...])


def _rnn_specs(t):
    seq = pl.BlockSpec((t, LANES), lambda cb, b: (b, cb))
    cw = pl.BlockSpec((4, LANES), lambda cb, b: (0, cb))
    vec1 = pl.BlockSpec((1, LANES), lambda cb, b: (0, cb))
    vec2 = pl.BlockSpec((2, LANES), lambda cb, b: (0, cb))
    wblk = pl.BlockSpec((1, LANES, 4 * LANES), lambda cb, b: (cb, 0, 0))
    gbias = pl.BlockSpec((1, 1, 4 * LANES), lambda cb, b: (cb, 0, 0))
    return seq, cw, vec1, vec2, wblk, gbias


def _rnn_fwd(xr, xg, conv_w, conv_b, wblk, gbias, lam):
    n = xr.shape[0]
    t = _t_pad()
    seq, cw, vec1, vec2, wspec, gspec = _rnn_specs(t)

    def body(xr_ref, xg_ref, cw_ref, cb_ref, w_ref, gb_ref, lam_ref, o_ref, a_s, b_s, h_s, p_s):
        rows = lax.broadcasted_iota(jnp.int32, (t, LANES), 0)
        valid = rows >= PAD_ROWS
        xc = _conv(xr_ref[...], cw_ref, cb_ref, rows, t)
        gates = _dot(xc.astype(BF16), w_ref[0]) + gb_ref[0]
        for d in range(2):
            _, _, _, a, _, _, b = _lru_gates(xc, gates, lam_ref, valid, d)
            a_s[d] = a
            b_s[d] = b
        _scan([(a_s.at[d], b_s.at[d], h_s.at[d], p_s.at[d], d == 1) for d in range(2)], t)
        g, _ = _gelu_parts(xg_ref[...])
        o_ref[...] = (h_s[0] + h_s[1]) * g

    return pl.pallas_call(
        body, name="rnn_fwd", grid=(D_RNN // LANES, n // t),
        in_specs=[seq, seq, cw, vec1, wspec, gspec, vec2], out_specs=seq,
        out_shape=jax.ShapeDtypeStruct((n, D_RNN), F32),
        scratch_shapes=[pltpu.VMEM((2, t, LANES), F32)] * 4,
        compiler_params=pltpu.CompilerParams(dimension_semantics=("parallel", "parallel"), vmem_limit_bytes=VMEM_LIMIT),
    )(xr, xg, conv_w, conv_b, wblk, gbias, lam)


def _rnn_bwd(xr, xg, do, conv_w, conv_b, wblk, gbias, lam, srcs=(), scatter=()):
    n = xr.shape[0]
    t = _t_pad()
    seq, cw, vec1, vec2, wspec, gspec = _rnn_specs(t)
    nk = len(srcs)
    c_in, c_out, c_shape, c_sems = _exchange_specs(srcs, scatter)

    def body(xr_ref, xg_ref, do_ref, cw_ref, cb_ref, w_ref, gb_ref, lam_ref, *rest):
        dxr_ref, dxg_ref, dcw_ref, dcb_ref, dw_ref, dgb_ref, dlam_ref = rest[nk:nk + 7]
        a_s, b_s, h_s, l_s, p_s, back_s, r_s, i_s, q_s, dg_s = rest[2 * nk + 7 + len(c_sems):]
        finish = _ride(2, *_exchange_fns(rest[:nk], rest[nk + 7:2 * nk + 7], rest[2 * nk + 7:2 * nk + 7 + len(c_sems)],
                                         scatter))
        first = pl.program_id(1) == 0
        rows = lax.broadcasted_iota(jnp.int32, (t, LANES), 0)
        valid = rows >= PAD_ROWS
        xr = xr_ref[...]
        xc = _conv(xr, cw_ref, cb_ref, rows, t)
        xcb = xc.astype(BF16)
        gates = _dot(xcb, w_ref[0]) + gb_ref[0]
        sps = []
        for d in range(2):
            r_s[d], i_s[d], sp, a_s[d], _, q_s[d], b_s[d] = _lru_gates(xc, gates, lam_ref, valid, d)
            sps.append(sp)
        _scan([(a_s.at[d], b_s.at[d], h_s.at[d], p_s.at[d], d == 1) for d in range(2)], t)
        g, dg = _gelu_parts(xg_ref[...])
        do = do_ref[...]
        dxg_ref[...] = do * (h_s[0] + h_s[1]) * dg
        b_s[0] = do * g
        for d in range(2):
            back_s[d] = _shift_rows(a_s[d], -1 if d == 0 else 1, rows, t)
        _scan([(back_s.at[d], b_s.at[0], l_s.at[d], p_s.at[d], d == 0) for d in range(2)], t)
        dxc = jnp.zeros((t, LANES), F32)
        dlams = []
        for d in range(2):
            r, i, sp, a, sq = r_s[d], i_s[d], sps[d], a_s[d], q_s[d]
            lam_t = l_s[d]
            da = lam_t * _shift_rows(h_s[d], 1 if d == 0 else -1, rows, t)
            lam_v = jnp.where(valid, lam_t, 0.0)
            dsq = lam_v * (i * xc)
            di = lam_v * sq * xc
            dxc = dxc + lam_v * sq * i
            dm = jnp.where(sq > 0.0, dsq * 0.5 / jnp.where(sq > 0.0, sq, 1.0), 0.0)
            dla = da * a - 2.0 * dm * a * a
            dr = dla * (-LRU_C) * sp
            dsp = _colsum(dla * (-LRU_C) * r)
            dlams.append(dsp * -jax.nn.sigmoid(-lam_ref[d:d + 1, :]))
            dg_s[:, (2 * d) * LANES:(2 * d + 1) * LANES] = (dr * r * (1.0 - r)).astype(BF16)
            dg_s[:, (2 * d + 1) * LANES:(2 * d + 2) * LANES] = (di * i * (1.0 - i)).astype(BF16)
        dgates = dg_s[...]
        dxc = dxc + _dot_nt(dgates, w_ref[0])
        taps = [_shift_rows(dxc, j - 2, rows, t) for j in range(4)]
        dxr_ref[...] = (cw_ref[0:1, :] * taps[0] + cw_ref[1:2, :] * taps[1] + cw_ref[2:3, :] * taps[2]
                        + cw_ref[3:4, :] * taps[3])
        dcw = jnp.concatenate([_colsum(tap * xr) for tap in taps], axis=0)
        _acc(dcw_ref, first, dcw)
        _acc(dcb_ref, first, _colsum(dxc))
        _acc(dw_ref, first, _dot_tn(xcb, dgates)[None])
        _acc(dgb_ref, first, _colsum(dgates.astype(F32))[None])
        _acc(dlam_ref, first, jnp.concatenate(dlams, axis=0))
        finish()

    res = pl.pallas_call(
        body, name="rnn_bwd", grid=(D_RNN // LANES, n // t),
        in_specs=[seq, seq, seq, cw, vec1, wspec, gspec, vec2] + c_in,
        out_specs=[seq, seq, cw, vec1, wspec, gspec, vec2] + c_out,
        out_shape=[jax.ShapeDtypeStruct((n, D_RNN), F32), jax.ShapeDtypeStruct((n, D_RNN), F32),
                   jax.ShapeDtypeStruct((4, D_RNN), F32), jax.ShapeDtypeStruct((1, D_RNN), F32),
                   jax.ShapeDtypeStruct((D_RNN // LANES, LANES, 4 * LANES), F32),
                   jax.ShapeDtypeStruct((D_RNN // LANES, 1, 4 * LANES), F32), jax.ShapeDtypeStruct((2, D_RNN), F32)]
        + c_shape,
        scratch_shapes=c_sems + [pltpu.VMEM((2, t, LANES), F32)] * 9 + [pltpu.VMEM((t, 4 * LANES), BF16)],
        compiler_params=pltpu.CompilerParams(dimension_semantics=("arbitrary", "arbitrary"), vmem_limit_bytes=VMEM_LIMIT),
    )(xr, xg, do, conv_w, conv_b, wblk, gbias, lam, *srcs)
    return res[:7], res[7:]


def _post(oa, orn, h0, tgt, ga, gr, g2, w_out, w_gate, w_up, w_down):
    n = oa.shape[0]
    tm = _row_tile(n)
    t = _t_pad()

    def body(oa_ref, or_ref, h0_ref, tgt_ref, ga_ref, gr_ref, g2_ref, wo_ref, wg_ref, wu_ref, wd_ref,
             doa_ref, dor_ref, dh1_ref, mix_ref, h1n_ref, act_ref, dgate_ref, dup_ref, dy_ref,
             loss_ref, dga_ref, dgr_ref, dg2_ref, gate_s, up_s):
        first = pl.program_id(0) == 0
        xa, ra = _rms(oa_ref[...], D_ATTN)
        xr, rr = _rms(or_ref[...], D_RNN)
        mix = jnp.concatenate([(xa * ga_ref[...]).astype(BF16), (xr * gr_ref[...]).astype(BF16)], axis=-1)
        mix_ref[...] = mix.T
        h1 = h0_ref[...] + _dot(mix, wo_ref[...])
        x2, r2 = _rms(h1, D_MODEL)
        h1n = (x2 * g2_ref[...]).astype(BF16)
        h1n_ref[...] = h1n
        y = h1
        for cs in range(0, D_FF, FF_CHUNK):
            sl = slice(cs, cs + FF_CHUNK)
            gate = _dot_nt(h1n, wg_ref[sl, :])
            up = _dot_nt(h1n, wu_ref[sl, :])
            gate_s[:, sl] = gate
            up_s[:, sl] = up
            act = (gate * _sigmoid(gate) * up).astype(BF16)
            act_ref[sl, :] = act.T
            y = y + _dot(act, wd_ref[sl, :])
        row = pl.program_id(0) * tm + lax.broadcasted_iota(jnp.int32, (tm, 1), 0)
        for _ in range(1, n // t):
            row = jnp.where(row >= t, row - t, row)
        err = jnp.where(row >= PAD_ROWS + N_META, y - tgt_ref[...], 0.0)
        _acc(loss_ref, first, jnp.full((1, LANES), 0.5 / D_MODEL, F32) * jnp.sum(err * err))
        dy = err * (1.0 / D_MODEL)
        dyb = dy.astype(BF16)
        dy_ref[...] = dyb
        dh1n = jnp.zeros((tm, D_MODEL), F32)
        for cs in range(0, D_FF, FF_CHUNK):
            sl = slice(cs, cs + FF_CHUNK)
            dact = _dot_nt(dyb, wd_ref[sl, :])
            gate, up = gate_s[:, sl], up_s[:, sl]
            sg = _sigmoid(gate)
            dgate = (dact * up * sg * (1.0 + gate * (1.0 - sg))).astype(BF16)
            dup = (dact * gate * sg).astype(BF16)
            dgate_ref[sl, :] = dgate.T
            dup_ref[sl, :] = dup.T
            dh1n = dh1n + _dot(dgate, wg_ref[sl, :]) + _dot(dup, wu_ref[sl, :])
        _acc(dg2_ref, first, _colsum(dh1n * x2))
        dh1 = dy + _rms_bwd(dh1n, x2, r2, g2_ref[...], D_MODEL)
        dh1_ref[...] = dh1
        dmix = _dot_nt(dh1.astype(BF16), wo_ref[...])
        dma, dmr = dmix[:, :D_ATTN], dmix[:, D_ATTN:]
        _acc(dga_ref, first, _colsum(dma * xa))
        _acc(dgr_ref, first, _colsum(dmr * xr))
        doa_ref[...] = _rms_bwd(dma, xa, ra, ga_ref[...], D_ATTN)
        dor_ref[...] = _rms_bwd(dmr, xr, rr, gr_ref[...], D_RNN)

    def row(w):
        return pl.BlockSpec((tm, w), lambda i: (i, 0))

    def acc(w):
        return pl.BlockSpec((1, w), lambda i: (0, 0))

    def col(w):
        return pl.BlockSpec((w, tm), lambda i: (0, i))

    outs = [(D_ATTN, F32, row), (D_RNN, F32, row), (D_MODEL, F32, row), (D_MODEL, BF16, col), (D_MODEL, BF16, row),
            (D_FF, BF16, col), (D_FF, BF16, col), (D_FF, BF16, col), (D_MODEL, BF16, row)]
    accs = [LANES, D_ATTN, D_RNN, D_MODEL]
    return pl.pallas_call(
        body, name="post", grid=(n // tm,),
        in_specs=[row(D_ATTN), row(D_RNN), row(D_MODEL), row(D_MODEL),
                  _const_spec((1, D_ATTN)), _const_spec((1, D_RNN)), _const_spec((1, D_MODEL)),
                  _const_spec((D_MODEL, D_MODEL)), _const_spec((D_FF, D_MODEL)), _const_spec((D_FF, D_MODEL)),
                  _const_spec((D_FF, D_MODEL))],
        out_specs=[spec(w) for w, _, spec in outs] + [acc(w) for w in accs],
        out_shape=[jax.ShapeDtypeStruct((n, w) if spec is row else (w, n), dt) for w, dt, spec in outs]
        + [jax.ShapeDtypeStruct((1, w), F32) for w in accs],
        scratch_shapes=[pltpu.VMEM((tm, D_FF), F32), pltpu.VMEM((tm, D_FF), F32)],
        compiler_params=pltpu.CompilerParams(dimension_semantics=("arbitrary",), vmem_limit_bytes=VMEM_LIMIT),
    )(oa, orn, h0, tgt, ga, gr, g2, w_out, w_gate, w_up, w_down)


def _in_bwd(dp, h0, dh1, ln1_g, w_in_p, srcs=(), scatter=()):
    n = h0.shape[0]
    tm = _row_tile(n)
    nk = len(srcs)
    c_in, c_out, c_shape, c_sems = _exchange_specs(srcs, scatter)

    def body(dp_ref, h0_ref, dh1_ref, g_ref, w_ref, *rest):
        dh0_ref, dg_ref = rest[nk:nk + 2]
        finish = _ride(1, *_exchange_fns(rest[:nk], rest[nk + 2:2 * nk + 2], rest[2 * nk + 2:], scatter))
        dhn = _dot(dp_ref[...], w_ref[...])
        xhat, r = _rms(h0_ref[...], D_MODEL)
        _acc(dg_ref, pl.program_id(0) == 0, _colsum(dhn * xhat))
        dh0_ref[...] = dh1_ref[...] + _rms_bwd(dhn, xhat, r, g_ref[...], D_MODEL)
        finish()

    def row(w):
        return pl.BlockSpec((tm, w), lambda i: (i, 0))

    res = pl.pallas_call(
        body, name="in_bwd", grid=(n // tm,),
        in_specs=[row(P_COLS), row(D_MODEL), row(D_MODEL), _const_spec((1, D_MODEL)), _const_spec((P_COLS, D_MODEL))] + c_in,
        out_specs=[row(D_MODEL), pl.BlockSpec((1, D_MODEL), lambda i: (0, 0))] + c_out,
        out_shape=[jax.ShapeDtypeStruct((n, D_MODEL), F32), jax.ShapeDtypeStruct((1, D_MODEL), F32)] + c_shape,
        scratch_shapes=c_sems,
        compiler_params=pltpu.CompilerParams(dimension_semantics=("arbitrary",), vmem_limit_bytes=VMEM_LIMIT),
    )(dp, h0, dh1, ln1_g, w_in_p, *srcs)
    return res[:2], res[2:]


MAX_TILE = D_FF // 2


def _pick_tile(width, cap):
    best = LANES
    for mult in range(1, width // LANES + 1):
        cand = mult * LANES
        if width % cand == 0 and cand <= cap:
            best = cand
    return best


def _matmul_tn(name, a, b, srcs=(), scatter=()):
    n, ka = a.shape
    kb = b.shape[1]
    ta, tb = _pick_tile(ka, MAX_TILE), _pick_tile(kb, MAX_TILE)
    tk = n // 4
    nk = len(srcs)
    c_in, c_out, c_shape, c_sems = _exchange_specs(srcs, scatter)

    def body(a_ref, b_ref, *rest):
        o_ref = rest[nk]
        finish = _ride(3, *_exchange_fns(rest[:nk], rest[nk + 1:2 * nk + 1], rest[2 * nk + 1:], scatter))
        _acc(o_ref, pl.program_id(2) == 0, _dot_tn(a_ref[...].astype(BF16), b_ref[...].astype(BF16)))
        finish()

    res = pl.pallas_call(
        body, name=name, grid=(ka // ta, kb // tb, n // tk),
        in_specs=[pl.BlockSpec((tk, ta), lambda i, j, k: (k, i)), pl.BlockSpec((tk, tb), lambda i, j, k: (k, j))] + c_in,
        out_specs=[pl.BlockSpec((ta, tb), lambda i, j, k: (i, j))] + c_out,
        out_shape=[jax.ShapeDtypeStruct((ka, kb), F32)] + c_shape, scratch_shapes=c_sems,
        compiler_params=pltpu.CompilerParams(dimension_semantics=("arbitrary", "arbitrary", "arbitrary"),
                                             vmem_limit_bytes=VMEM_LIMIT),
    )(a, b, *srcs)
    return res[0], res[1:]


def _matmul_shards(name, at, b):
    ka, n = at.shape
    kb = b.shape[1]
    ta, tb = _pick_tile(ka, MAX_TILE), _pick_tile(kb, MAX_TILE)
    tk = n // 2
    width = ka // N_DEV
    per = ta // width

    def body(a_ref, b_ref, o_ref, acc_ref):
        _acc(acc_ref, pl.program_id(2) == 0, _dot(a_ref[...], b_ref[...].astype(BF16)))

        @pl.when(pl.program_id(2) == pl.num_programs(2) - 1)
        def _():
            for s in range(per):
                o_ref[s] = acc_ref[s * width:(s + 1) * width, :].astype(BF16)

    return pl.pallas_call(
        body, name=name, grid=(ka // ta, kb // tb, n // tk),
        in_specs=[pl.BlockSpec((ta, tk), lambda i, j, k: (i, k)), pl.BlockSpec((tk, tb), lambda i, j, k: (k, j))],
        out_specs=pl.BlockSpec((per, width, tb), lambda i, j, k: (i, 0, j)),
        out_shape=jax.ShapeDtypeStruct((N_DEV, width, kb), BF16),
        scratch_shapes=[pltpu.VMEM((ta, tb), F32)],
        compiler_params=pltpu.CompilerParams(dimension_semantics=("parallel", "parallel", "arbitrary"),
                                             vmem_limit_bytes=VMEM_LIMIT),
    )(at, b)


def _adamw_math(g8_ref, w_ref, m_ref, v_ref, g_ref, d_ref, nm_ref, nv_ref):
    g = g8_ref[0].astype(F32)
    for s in range(1, N_DEV):
        g = g + g8_ref[s].astype(F32)
    g_ref[...] = g
    nm = ADAM_B1 * m_ref[...] + (1.0 - ADAM_B1) * g
    nv = ADAM_B2 * v_ref[...] + (1.0 - ADAM_B2) * (g * g)
    nm_ref[...] = nm
    nv_ref[...] = nv
    m_hat = nm / (1.0 - ADAM_B1 ** ADAM_STEP)
    v_hat = nv / (1.0 - ADAM_B2 ** ADAM_STEP)
    d_ref[...] = -ADAM_LR * (m_hat / (jnp.sqrt(v_hat) + ADAM_EPS) + ADAM_WD * w_ref[...])


def _adamw_many(name, items):
    count = len(items)

    def body(*refs):
        ins, outs = refs[:4 * count], refs[4 * count:]
        for i in range(count):
            _adamw_math(*ins[4 * i:4 * i + 4], *outs[4 * i:4 * i + 4])

    flat = [a for item in items for a in item]
    res = pl.pallas_call(
        body, name=name,
        out_shape=[jax.ShapeDtypeStruct(item[1].shape, F32) for item in items for _ in range(4)],
        compiler_params=pltpu.CompilerParams(vmem_limit_bytes=VMEM_LIMIT),
    )(*flat)
    return [tuple(res[4 * i:4 * i + 4]) for i in range(count)]


def _adamw(name, g8, w, m, v):
    rows, cols = w.shape
    tr = rows
    for cand in (256, 176, 128, 64):
        if rows % cand == 0 and rows > cand:
            tr = cand
            break

    def body(*refs):
        _adamw_math(*refs)

    blk = pl.BlockSpec((tr, cols), lambda i: (i, 0))
    return pl.pallas_call(
        body, name=name, grid=(rows // tr,),
        in_specs=[pl.BlockSpec((N_DEV, tr, cols), lambda i: (0, i, 0)), blk, blk, blk],
        out_specs=[blk] * 4, out_shape=[jax.ShapeDtypeStruct((rows, cols), F32)] * 4,
        compiler_params=pltpu.CompilerParams(dimension_semantics=("parallel",), vmem_limit_bytes=VMEM_LIMIT),
    )(g8, w, m, v)


def _exchange_specs(srcs, scatter):
    nk = len(srcs)
    if not nk:
        return [], [], [], []
    any_spec = pl.BlockSpec(memory_space=pl.ANY)
    out_shape = [jax.ShapeDtypeStruct(s.shape if sc else (N_DEV,) + s.shape, s.dtype) for s, sc in zip(srcs, scatter)]
    sems = [pltpu.SemaphoreType.DMA((nk, N_DEV - 1)), pltpu.SemaphoreType.DMA((nk, N_DEV - 1)),
            pltpu.SemaphoreType.DMA((nk,))]
    return [any_spec] * nk, [any_spec] * nk, out_shape, sems


FLIPS = ((0, 0, 1), (1, 0, 0), (0, 1, 0), (1, 1, 0), (1, 0, 1), (0, 1, 1), (1, 1, 1))
N_CHIP_PEERS = 3


def _exchange_fns(src_refs, out_refs, sems, scatter):
    nk = len(src_refs)
    if not nk:
        return (lambda: None), (lambda: None), (lambda: None)
    send_sems, recv_sems, local_sems = sems
    first = 1 + N_CHIP_PEERS

    def plan():
        x, y, c = lax.axis_index("x"), lax.axis_index("y"), lax.axis_index("c")
        me = 4 * x + 2 * y + c
        peers = [(1 - x if fx else x, 1 - y if fy else y, 1 - c if fc else c) for fx, fy, fc in FLIPS]
        pids = [4 * px + 2 * py + pc for px, py, pc in peers]

        def remote(k, j, src, dst, to):
            return pltpu.make_async_remote_copy(src_ref=src, dst_ref=dst, send_sem=send_sems.at[k, j],
                                                recv_sem=recv_sems.at[k, j], device_id=to, device_id_type=MESH)

        def mine(k, dest):
            return src_refs[k].at[dest] if scatter[k] else src_refs[k]

        local = [pltpu.make_async_copy(mine(k, me), out_refs[k].at[me], local_sems.at[k]) for k in range(nk)]
        direct = [remote(k, j, mine(k, pids[j]), out_refs[k].at[me], peers[j])
                  for k in range(nk) for j in range(len(FLIPS) if scatter[k] else first)]
        relays = {(k, j): remote(k, j, out_refs[k].at[pids[j - N_CHIP_PEERS]], out_refs[k].at[pids[j - N_CHIP_PEERS]], peers[0])
                  for k in range(nk) if not scatter[k] for j in range(first, len(FLIPS))}
        arrivals = {(k, j): remote(k, j, out_refs[k].at[pids[j]], out_refs[k].at[pids[j]], peers[j])
                    for k in range(nk) for j in range(len(FLIPS))}
        return local, direct, relays, arrivals

    def start():
        local, direct, _, _ = plan()
        for cp in local + direct:
            cp.start()

    def relay():
        _, _, relays, arrivals = plan()
        for (k, j), cp in relays.items():
            arrivals[k, j - N_CHIP_PEERS].wait_recv()
            cp.start()

    def wait():
        local, direct, relays, arrivals = plan()
        for (k, j), cp in arrivals.items():
            if (k, j + N_CHIP_PEERS) not in relays:
                cp.wait_recv()
        for cp in direct + list(relays.values()):
            cp.wait_send()
        for cp in local:
            cp.wait()

    return start, relay, wait


def _grid_step(rank):
    step, total = 0, 1
    for axis in range(rank):
        step = step * pl.num_programs(axis) + pl.program_id(axis)
        total = total * pl.num_programs(axis)
    return step, total


def _ride(rank, start, relay, wait):
    step, total = _grid_step(rank)
    pl.when(step == 0)(start)
    pl.when(step == (3 * total) // 4)(relay)
    return lambda: pl.when(step == total - 1)(wait)


def _exchange(name, srcs, scatter):
    nk = len(srcs)
    c_in, c_out, c_shape, c_sems = _exchange_specs(srcs, scatter)

    def body(*refs):
        start, relay, wait = _exchange_fns(refs[:nk], refs[nk:2 * nk], refs[2 * nk:], scatter)
        start()
        relay()
        wait()

    return pl.pallas_call(body, name=name, in_specs=c_in, out_specs=c_out, out_shape=c_shape, scratch_shapes=c_sems)(*srcs)


def _cols_from_shards(g):
    return jnp.transpose(g, (1, 0, 2)).reshape(g.shape[1], -1)


def _cols_to_shards(w):
    return jnp.transpose(w.reshape(w.shape[0], N_DEV, -1), (1, 0, 2))


def _prep(x, tgt, srcs, scatter):
    nb = x.shape[0]
    t = _t_pad()
    head = PAD_ROWS + N_META
    nk = len(srcs)
    c_in, c_out, c_shape, c_sems = _exchange_specs(srcs, scatter)

    def body(x_ref, tgt_ref, *rest):
        h0_ref, tp_ref = rest[nk:nk + 2]
        finish = _ride(1, *_exchange_fns(rest[:nk], rest[nk + 2:2 * nk + 2], rest[2 * nk + 2:], scatter))
        lead = pl.program_id(0) == 0

        @pl.when(lead)
        def _():
            h0_ref[...] = jnp.zeros_like(h0_ref)
            tp_ref[...] = jnp.zeros_like(tp_ref)

        @pl.when(jnp.logical_not(lead))
        def _():
            h0_ref[...] = x_ref[...]
            tp_ref[...] = tgt_ref[...]

        finish()

    src = pl.BlockSpec((nb, head, D_MODEL), lambda j: (0, jnp.maximum(j - 1, 0), 0))
    dst = pl.BlockSpec((nb, head, D_MODEL), lambda j: (0, j, 0))
    padded = jax.ShapeDtypeStruct((nb, t, D_MODEL), F32)
    res = pl.pallas_call(
        body, name="prep", grid=(t // head,), in_specs=[src, src] + c_in, out_specs=[dst, dst] + c_out,
        out_shape=[padded, padded] + c_shape, scratch_shapes=c_sems,
        compiler_params=pltpu.CompilerParams(dimension_semantics=("arbitrary",)),
    )(x, tgt, *srcs)
    return res[0], res[1], res[2:]


def _rope_tables(n):
    t = _t_pad()
    pos = np.arange(t, dtype=np.float32) - np.float32(PAD_ROWS)
    half = QK_ROPE // 2
    freqs = (1.0 / (ROPE_THETA ** (np.arange(half, dtype=np.float32) / half))).astype(np.float32)
    ang = pos[:, None] * freqs[None, :]
    cos, sin = np.cos(ang), np.sin(ang)
    z = lambda w: np.zeros((t, w), np.float32)
    c = np.concatenate([np.ones((t, QK_NOPE), np.float32), cos, cos, z(HEAD_PAD - QK_HEAD)], axis=1)
    s1 = np.concatenate([z(QK_NOPE + half), sin, z(HEAD_PAD - QK_HEAD)], axis=1)
    s2 = np.concatenate([z(QK_NOPE), -sin, z(HEAD_PAD - QK_NOPE - half)], axis=1)
    return tuple(jnp.asarray(np.tile(a, (n // t, 1))) for a in (c, s1, s2))


def _block_diag_gates(lru_wa, lru_wi):
    eye = jnp.eye(2, dtype=lru_wa.dtype)

    def bd(w):
        w = w.reshape(2, D_RNN // LANES, 2, RNN_BW, RNN_BW)
        full = w[:, :, :, :, None, :] * eye[None, None, :, None, :, None]
        return full.reshape(2, D_RNN // LANES, LANES, LANES)

    a, i = bd(lru_wa), bd(lru_wi)
    return jnp.concatenate([a[0], i[0], a[1], i[1]], axis=-1)


def _unblock_gates(dw):
    nb = D_RNN // LANES
    parts = dw.reshape(nb, 2, RNN_BW, 4, 2, RNN_BW)
    diag = jnp.stack([parts[:, k, :, :, k, :] for k in range(2)], axis=1)
    diag = jnp.transpose(diag, (3, 0, 1, 2, 4)).reshape(4, 2 * nb, RNN_BW, RNN_BW)
    return jnp.stack([diag[0], diag[2]]), jnp.stack([diag[1], diag[3]])


WEIGHTS = ("meta_tokens", "ln1_g", "w_in", "q_a_norm_g", "w_uq", "kv_a_norm_g", "w_ukv", "q_norm_g", "k_norm_g",
           "conv_w", "conv_b", "lru_wa", "lru_ba", "lru_wi", "lru_bi", "lru_lambda", "attn_out_g", "rnn_out_g",
           "w_out", "ln2_g", "w_gate", "w_up", "w_down")
BIG = ("w_in", "w_uq", "w_ukv", "w_out", "w_gate", "w_up", "w_down")
TRANSPOSED = ("w_in", "w_uq", "w_gate", "w_up")
ROW_SHARDED = ("w_out", "w_down") + TRANSPOSED
REPLICATED = ("ln1_g", "q_a_norm_g", "kv_a_norm_g", "q_norm_g", "k_norm_g", "conv_b", "lru_wa", "lru_wi",
              "attn_out_g", "rnn_out_g", "ln2_g")
WHOLE = REPLICATED + ("loss",)
G_FIRST = ("w_in", "meta_tokens")
G_MID = ("w_uq", "w_ukv", "conv_w", "lru_ba", "lru_bi", "lru_lambda")
LATE = ("w_out", "w_gate", "w_up", "w_down")
G_LAST = ("meta_tokens", "ln1_g")


def _local_step(x, tgt, ex):
    nb = x.shape[0]
    t = _t_pad()
    n = nb * t
    local = ex.local
    h0, tgt_p, got = _prep(x, tgt, *ex.gather_srcs(G_FIRST))
    first = ex.gathered(G_FIRST, got)
    meta, w_in = first["meta_tokens"], first["w_in"]
    h0 = h0.at[:, PAD_ROWS:PAD_ROWS + N_META].set(jnp.broadcast_to(meta[None], (nb, N_META, D_MODEL))).reshape(n, D_MODEL)
    tgt_p = tgt_p.reshape(n, D_MODEL)

    zr = lambda r: jnp.zeros((r, D_MODEL), w_in.dtype)
    w_in_p = jnp.concatenate([w_in[:OFF_CKV], w_in[OFF_KR:], zr(QK_NOPE), w_in[OFF_CKV:OFF_KR], zr(HEAD_PAD - QK_HEAD)],
                             axis=0)
    pad_g = lambda g: jnp.pad(g, ((0, 0), (0, HEAD_PAD - QK_HEAD)))
    qg, kg = pad_g(local["q_norm_g"]), pad_g(local["k_norm_g"])
    rc, rs1, rs2 = _rope_tables(n)
    wblk = _block_diag_gates(local["lru_wa"].reshape(2, -1, RNN_BW, RNN_BW),
                             local["lru_wi"].reshape(2, -1, RNN_BW, RNN_BW)).astype(BF16)
    nblk = D_RNN // LANES

    (hn, cq, ckv, xr, xg, kr), got = _in_proj(h0, local["ln1_g"], w_in_p, *ex.gather_srcs(G_MID))
    w = ex.gathered(G_MID, got)
    w_uq_p = jnp.pad(w["w_uq"].reshape(N_HEADS, QK_HEAD, Q_LORA), ((0, 0), (0, HEAD_PAD - QK_HEAD), (0, 0))
                     ).reshape(QP_COLS, Q_LORA)
    ukv = w["w_ukv"].reshape(KV_LORA, N_HEADS, QK_NOPE + V_HEAD)
    w_uk_p = jnp.pad(ukv[:, :, :QK_NOPE], ((0, 0), (0, 0), (0, HEAD_PAD - QK_NOPE))).reshape(KV_LORA, QP_COLS)
    w_v = ukv[:, :, QK_NOPE:].reshape(KV_LORA, D_ATTN)
    gbias = jnp.stack([w["lru_ba"][0], w["lru_bi"][0], w["lru_ba"][1], w["lru_bi"][1]], axis=0)
    gbias = jnp.transpose(gbias.reshape(4, nblk, LANES), (1, 0, 2)).reshape(nblk, 1, 4 * LANES)

    q, k, v = _qkv_fwd(cq, ckv, kr, local["q_a_norm_g"], local["kv_a_norm_g"], w_uq_p, w_uk_p, w_v, qg, kg, rc, rs1, rs2)
    oa, probs, got = _attn_fwd(q, k, v, *ex.gather_srcs(LATE))
    late = ex.gathered(LATE, got)
    orn = _rnn_fwd(xr, xg, w["conv_w"], local["conv_b"], wblk, gbias, w["lru_lambda"])
    (doa, dor, dh1, mix_t, h1n, act_t, dgate_t, dup_t, dyb, loss, dga, dgr, dg2) = _post(
        oa, orn, h0, tgt_p, local["attn_out_g"], local["rnn_out_g"], local["ln2_g"], late["w_out"], late["w_gate"],
        late["w_up"], late["w_down"])
    wire = {"w_out": _matmul_shards("dw_out", mix_t, dh1), "w_gate": _matmul_shards("dw_gate", dgate_t, h1n),
            "w_up": _matmul_shards("dw_up", dup_t, h1n), "w_down": _matmul_shards("dw_down", act_t, dyb)}
    names = ("w_out", "w_gate")
    (dxr, dxg, dcw, dcb, dwblk, dgb, dlam), got = _rnn_bwd(xr, xg, dor, w["conv_w"], local["conv_b"], wblk, gbias,
                                                           w["lru_lambda"], *ex.scatter_srcs(names, wire))
    summed = ex.scattered(names, wire, got)
    dwa, dwi = _unblock_gates(dwblk)
    dgb = jnp.transpose(dgb.reshape(nblk, 4, LANES), (1, 0, 2)).reshape(4, D_RNN)
    wire = {"w_up": wire["w_up"], "w_down": wire["w_down"], **ex.to_wire({
        "conv_w": dcw, "conv_b": dcb, "lru_wa": dwa.reshape(-1, RNN_BW), "lru_ba": jnp.stack([dgb[0], dgb[2]]),
        "lru_wi": dwi.reshape(-1, RNN_BW), "lru_bi": jnp.stack([dgb[1], dgb[3]]), "lru_lambda": dlam,
        "attn_out_g": dga, "rnn_out_g": dgr, "ln2_g": dg2, "loss": loss})}
    names = tuple(wire)
    (dq_r, dk_r, dv), got = _attn_bwd(q, k, v, doa, oa, probs, *ex.scatter_srcs(names, wire))
    summed.update(ex.scattered(names, wire, got))
    (dp, qa, kva, dqp, dkv, dqg, dkg, dgqa, dgkva) = _qkv_bwd(
        cq, ckv, kr, dq_r, dk_r, dv, dxr, dxg, local["q_a_norm_g"], local["kv_a_norm_g"], w_uq_p, w_uk_p, w_v, qg, kg,
        rc, rs1, rs2)
    dw_uq_p, _ = _matmul_tn("dw_uq", dqp, qa)
    dw_kv, _ = _matmul_tn("dw_ukv", kva, dkv)
    dw_uq = dw_uq_p.reshape(N_HEADS, HEAD_PAD, Q_LORA)[:, :QK_HEAD].reshape(N_HEADS * QK_HEAD, Q_LORA)
    dw_ukv = jnp.concatenate([dw_kv[:, :QP_COLS].reshape(KV_LORA, N_HEADS, HEAD_PAD)[:, :, :QK_NOPE],
                              dw_kv[:, QP_COLS:].reshape(KV_LORA, N_HEADS, V_HEAD)], axis=2).reshape(KV_LORA, -1)
    wire = ex.to_wire({"q_a_norm_g": dgqa, "w_uq": dw_uq, "kv_a_norm_g": dgkva, "w_ukv": dw_ukv,
                       "q_norm_g": dqg[:, :QK_HEAD], "k_norm_g": dkg[:, :QK_HEAD]})
    names = tuple(wire)
    dw_in_p, got = _matmul_tn("dw_in", dp, hn, *ex.scatter_srcs(names, wire))
    summed.update(ex.scattered(names, wire, got))
    kr0 = OFF_CKV + 2 * D_RNN + QK_NOPE
    dw_in = jnp.concatenate([dw_in_p[:OFF_CKV], dw_in_p[kr0:kr0 + QK_ROPE], dw_in_p[OFF_CKV:OFF_CKV + 2 * D_RNN]], axis=0)
    wire = ex.to_wire({"w_in": dw_in})
    (dh0, dg1), got = _in_bwd(dp, h0, dh1, local["ln1_g"], w_in_p, *ex.scatter_srcs(("w_in",), wire))
    summed.update(ex.scattered(("w_in",), wire, got))

    dh0 = dh0.reshape(nb, t, D_MODEL)
    wire = ex.to_wire({"meta_tokens": jnp.sum(dh0[:, PAD_ROWS:PAD_ROWS + N_META], axis=0), "ln1_g": dg1})
    got = ex.run("reduce_last", *ex.scatter_srcs(G_LAST, wire))
    summed.update(ex.scattered(G_LAST, wire, got))
    return dh0[:, PAD_ROWS + N_META:], summed


class _MeshExchange:
    def __init__(self, shards):
        self.local = shards

    @staticmethod
    def run(name, srcs, scatter):
        return _exchange(name, srcs, scatter)

    def gather_srcs(self, names):
        return [self.local[k].astype(BF16) if k in BIG else self.local[k] for k in names], [False] * len(names)

    @staticmethod
    def gathered(names, outs):
        return {k: g.reshape(-1, g.shape[-1]) if k in ROW_SHARDED else _cols_from_shards(g) for k, g in zip(names, outs)}

    @staticmethod
    def to_wire(grads):
        wire = {}
        for k, g in grads.items():
            if k in WHOLE:
                wire[k] = g
            elif k in ROW_SHARDED:
                wire[k] = g.reshape(N_DEV, -1, g.shape[-1]).astype(BF16)
            else:
                wire[k] = _cols_to_shards(g).astype(BF16) if k in BIG else _cols_to_shards(g)
        return wire

    @staticmethod
    def scatter_srcs(names, wire):
        return [wire[k] for k in names], [k not in WHOLE for k in names]

    @staticmethod
    def scattered(names, wire, outs):
        return dict(zip(names, outs))


def kernel(x, meta_tokens, ln1_g, w_in, q_a_norm_g, w_uq, kv_a_norm_g, w_ukv, q_norm_g, k_norm_g, conv_w, conv_b, lru_wa, lru_ba, lru_wi, lru_bi, lru_lambda, attn_out_g, rnn_out_g, w_out, ln2_g, w_gate, w_up, w_down, loss_target, m_meta_tokens, m_ln1_g, m_w_in, m_q_a_norm_g, m_w_uq, m_kv_a_norm_g, m_w_ukv, m_q_norm_g, m_k_norm_g, m_conv_w, m_conv_b, m_lru_wa, m_lru_ba, m_lru_wi, m_lru_bi, m_lru_lambda, m_attn_out_g, m_rnn_out_g, m_w_out, m_ln2_g, m_w_gate, m_w_up, m_w_down, v_meta_tokens, v_ln1_g, v_w_in, v_q_a_norm_g, v_w_uq, v_kv_a_norm_g, v_w_ukv, v_q_norm_g, v_k_norm_g, v_conv_w, v_conv_b, v_lru_wa, v_lru_ba, v_lru_wi, v_lru_bi, v_lru_lambda, v_attn_out_g, v_rnn_out_g, v_w_out, v_ln2_g, v_w_gate, v_w_up, v_w_down):
    given = (meta_tokens, ln1_g, w_in, q_a_norm_g, w_uq, kv_a_norm_g, w_ukv, q_norm_g, k_norm_g, conv_w, conv_b,
             lru_wa, lru_ba, lru_wi, lru_bi, lru_lambda, attn_out_g, rnn_out_g, w_out, ln2_g, w_gate, w_up, w_down)
    moments_m = (m_meta_tokens, m_ln1_g, m_w_in, m_q_a_norm_g, m_w_uq, m_kv_a_norm_g, m_w_ukv, m_q_norm_g, m_k_norm_g,
                 m_conv_w, m_conv_b, m_lru_wa, m_lru_ba, m_lru_wi, m_lru_bi, m_lru_lambda, m_attn_out_g, m_rnn_out_g,
                 m_w_out, m_ln2_g, m_w_gate, m_w_up, m_w_down)
    moments_v = (v_meta_tokens, v_ln1_g, v_w_in, v_q_a_norm_g, v_w_uq, v_kv_a_norm_g, v_w_ukv, v_q_norm_g, v_k_norm_g,
                 v_conv_w, v_conv_b, v_lru_wa, v_lru_ba, v_lru_wi, v_lru_bi, v_lru_lambda, v_attn_out_g, v_rnn_out_g,
                 v_w_out, v_ln2_g, v_w_gate, v_w_up, v_w_down)
    shapes = {k: a.shape for k, a in zip(WEIGHTS, given)}

    def two_d(k, a):
        a = a.reshape(-1, a.shape[-1])
        return a.T if k in TRANSPOSED else a

    w = {k: two_d(k, a) for k, a in zip(WEIGHTS, given)}
    m = {k: two_d(k, a) for k, a in zip(WEIGHTS, moments_m)}
    v = {k: two_d(k, a) for k, a in zip(WEIGHTS, moments_v)}

    grad_x, parts = _local_step(x, loss_target, _MeshExchange(w))

    new = {k: _adamw("adamw_" + k, parts[k], w[k], m[k], v[k]) for k in BIG}
    small = [k for k in WEIGHTS if k not in BIG]
    new.update(zip(small, _adamw_many("adamw_small", [(parts[k], w[k], m[k], v[k]) for k in small])))

    loss = jnp.sum(parts["loss"][:, 0, 0])
    outs = [loss, grad_x]
    for idx in range(4):
        outs += [(new[k][idx].T if k in TRANSPOSED else new[k][idx]).reshape(shapes[k]) for k in WEIGHTS]
    return tuple(outs)
```

```python
import functools
import math

import numpy as np
import jax
import jax.numpy as jnp
from jax import lax
from jax.experimental import pallas as pl
from jax.experimental.pallas import tpu as pltpu

F32 = jnp.float32
BF16 = jnp.bfloat16

D_MODEL = 1024
N_META = 16
SEQ = 2048
N_HEADS = 8
QK_NOPE = 64
QK_ROPE = 32
QK_HEAD = QK_NOPE + QK_ROPE
V_HEAD = 64
D_ATTN = N_HEADS * V_HEAD
Q_LORA = 384
KV_LORA = 256
D_RNN = 512
RNN_BW = 64
D_FF = 2816
EPS = 1e-6
LRU_C = 8.0
ROPE_THETA = 10000.0
OFF_CKV = Q_LORA + KV_LORA
OFF_KR = OFF_CKV + QK_ROPE
IN_COLS = OFF_KR + 2 * D_RNN

ADAM_LR = 0.001
ADAM_B1 = 0.9
ADAM_B2 = 0.999
ADAM_EPS = 1e-08
ADAM_WD = 0.01
ADAM_STEP = 10

N_DEV = 8
LANES = 128
HEAD_PAD = LANES
PAD_ROWS = LANES - N_META
QP_COLS = N_HEADS * HEAD_PAD
P_COLS = OFF_CKV + 2 * D_RNN + LANES
FF_CHUNK = D_FF
VMEM_LIMIT = 56 * 1024 * 1024
MESH = pl.DeviceIdType.MESH


def _t_pad():
    return PAD_ROWS + N_META + SEQ


def _row_tile(n):
    return 256 if n % 256 == 0 else 128


def _wide_row_tile(n):
    quarter = _t_pad() // 4
    return quarter if quarter % 16 == 0 and n % quarter == 0 else _row_tile(n)


def _const_spec(shape):
    nd = len(shape)
    return pl.BlockSpec(shape, lambda *_: (0,) * nd, pipeline_mode=pl.Buffered(1))


def _rms(x, d):
    r = lax.rsqrt(jnp.sum(x * x, axis=-1, keepdims=True) * (1.0 / d) + EPS)
    return x * r, r


def _rms_bwd(dy, xhat, r, g, d):
    dxh = dy * g
    return r * (dxh - xhat * (jnp.sum(dxh * xhat, axis=-1, keepdims=True) * (1.0 / d)))


def _colsum(x):
    return jnp.sum(x, axis=0, keepdims=True)


def _dot(a, b):
    return jnp.dot(a, b, preferred_element_type=F32)


def _dot_nt(a, b):
    return lax.dot_general(a, b, (((1,), (1,)), ((), ())), preferred_element_type=F32)


def _dot_tn(a, b):
    return lax.dot_general(a, b, (((0,), (0,)), ((), ())), preferred_element_type=F32)


def _rope(x, c, s1, s2):
    return x * c + pltpu.roll(x, 16, 1) * s1 + pltpu.roll(x, HEAD_PAD - 16, 1) * s2


def _rope_bwd(dy, c, s1, s2):
    return dy * c + pltpu.roll(dy * s1, HEAD_PAD - 16, 1) + pltpu.roll(dy * s2, 16, 1)


def _acc(ref, first, val):
    @pl.when(first)
    def _():
        ref[...] = val

    @pl.when(jnp.logical_not(first))
    def _():
        ref[...] += val


def _in_proj(h0, ln1_g, w_in_p, srcs=(), scatter=()):
    n = h0.shape[0]
    tm = _wide_row_tile(n)
    nk = len(srcs)
    c_in, c_out, c_shape, c_sems = _exchange_specs(srcs, scatter)

    def body(h_ref, g_ref, w_ref, *rest):
        hn_ref, cq_ref, ckv_ref, xr_ref, xg_ref, kr_ref = rest[nk:nk + 6]
        finish = _ride(1, *_exchange_fns(rest[:nk], rest[nk + 6:2 * nk + 6], rest[2 * nk + 6:], scatter))
        xhat, _ = _rms(h_ref[...], D_MODEL)
        hn = (xhat * g_ref[...]).astype(BF16)
        hn_ref[...] = hn
        p = _dot_nt(hn, w_ref[...])
        cq_ref[...] = p[:, :Q_LORA]
        ckv_ref[...] = p[:, Q_LORA:OFF_CKV]
        xr_ref[...] = p[:, OFF_CKV:OFF_CKV + D_RNN]
        xg_ref[...] = p[:, OFF_CKV + D_RNN:OFF_CKV + 2 * D_RNN]
        kr_ref[...] = p[:, OFF_CKV + 2 * D_RNN:]
        finish()

    def row(w):
        return pl.BlockSpec((tm, w), lambda i: (i, 0))

    widths = (D_MODEL, Q_LORA, KV_LORA, D_RNN, D_RNN, LANES)
    res = pl.pallas_call(
        body, name="in_proj", grid=(n // tm,),
        in_specs=[row(D_MODEL), _const_spec((1, D_MODEL)), _const_spec((P_COLS, D_MODEL))] + c_in,
        out_specs=[row(w) for w in widths] + c_out,
        out_shape=[jax.ShapeDtypeStruct((n, w), BF16 if k == 0 else F32) for k, w in enumerate(widths)] + c_shape,
        scratch_shapes=c_sems,
        compiler_params=pltpu.CompilerParams(dimension_semantics=("arbitrary",), vmem_limit_bytes=VMEM_LIMIT),
    )(h0, ln1_g, w_in_p, *srcs)
    return res[:6], res[6:]


def _qkv_fwd(cq, ckv, kr, gqa, gkva, w_uq_p, w_uk_p, w_v, qg, kg, rc, rs1, rs2):
    n = cq.shape[0]
    tm = _wide_row_tile(n)

    def body(cq_ref, ckv_ref, kr_ref, gqa_ref, gkva_ref, wuq_ref, wuk_ref, wv_ref, qg_ref, kg_ref,
             c_ref, s1_ref, s2_ref, q_ref, k_ref, v_ref):
        xq, _ = _rms(cq_ref[...], Q_LORA)
        qa = (xq * gqa_ref[...]).astype(BF16)
        q = _dot_nt(qa, wuq_ref[...])
        xkv, _ = _rms(ckv_ref[...], KV_LORA)
        kva = (xkv * gkva_ref[...]).astype(BF16)
        kn = _dot(kva, wuk_ref[...])
        v_ref[...] = _dot(kva, wv_ref[...]).astype(BF16)
        krp = kr_ref[...]
        c, s1, s2 = c_ref[...], s1_ref[...], s2_ref[...]
        for h in range(N_HEADS):
            sl = slice(h * HEAD_PAD, (h + 1) * HEAD_PAD)
            qh, _ = _rms(q[:, sl], QK_HEAD)
            q_ref[:, sl] = _rope(qh * qg_ref[...], c, s1, s2).astype(BF16)
            kh, _ = _rms(kn[:, sl] + krp, QK_HEAD)
            k_ref[:, sl] = _rope(kh * kg_ref[...], c, s1, s2).astype(BF16)

    def row(w):
        return pl.BlockSpec((tm, w), lambda i: (i, 0))

    return pl.pallas_call(
        body, name="qkv_fwd", grid=(n // tm,),
        in_specs=[row(Q_LORA), row(KV_LORA), row(LANES), _const_spec((1, Q_LORA)), _const_spec((1, KV_LORA)),
                  _const_spec((QP_COLS, Q_LORA)), _const_spec((KV_LORA, QP_COLS)), _const_spec((KV_LORA, D_ATTN)),
                  _const_spec((1, LANES)), _const_spec((1, LANES)), row(LANES), row(LANES), row(LANES)],
        out_specs=[row(QP_COLS), row(QP_COLS), row(D_ATTN)],
        out_shape=[jax.ShapeDtypeStruct((n, QP_COLS), BF16), jax.ShapeDtypeStruct((n, QP_COLS), BF16),
                   jax.ShapeDtypeStruct((n, D_ATTN), BF16)],
        compiler_params=pltpu.CompilerParams(dimension_semantics=("parallel",), vmem_limit_bytes=VMEM_LIMIT),
    )(cq, ckv, kr, gqa, gkva, w_uq_p, w_uk_p, w_v, qg, kg, rc, rs1, rs2)


def _qkv_bwd(cq, ckv, kr, dq_r, dk_r, dv, dxr, dxg, gqa, gkva, w_uq_p, w_uk_p, w_v, qg, kg, rc, rs1, rs2,
             srcs=(), scatter=()):
    n = cq.shape[0]
    tm = _wide_row_tile(n)
    nk = len(srcs)
    c_in, c_out, c_shape, c_sems = _exchange_specs(srcs, scatter)

    def body(cq_ref, ckv_ref, kr_ref, dq_ref, dk_ref, dv_ref, dxr_ref, dxg_ref, gqa_ref, gkva_ref, wuq_ref, wuk_ref,
             wv_ref, qg_ref, kg_ref, c_ref, s1_ref, s2_ref, *rest):
        dp_ref, qa_ref, kva_ref, dqp_ref, dkv_ref, dqg_ref, dkg_ref, dgqa_ref, dgkva_ref = rest[nk:nk + 9]
        finish = _ride(1, *_exchange_fns(rest[:nk], rest[nk + 9:2 * nk + 9], rest[2 * nk + 9:], scatter))
        first = pl.program_id(0) == 0
        dp_ref[:, OFF_CKV:OFF_CKV + D_RNN] = dxr_ref[...].astype(BF16)
        dp_ref[:, OFF_CKV + D_RNN:OFF_CKV + 2 * D_RNN] = dxg_ref[...].astype(BF16)
        xq, rq = _rms(cq_ref[...], Q_LORA)
        qa = (xq * gqa_ref[...]).astype(BF16)
        qa_ref[...] = qa
        q = _dot_nt(qa, wuq_ref[...])
        xkv, rkv = _rms(ckv_ref[...], KV_LORA)
        kva = (xkv * gkva_ref[...]).astype(BF16)
        kva_ref[...] = kva
        kn = _dot(kva, wuk_ref[...])
        krp = kr_ref[...]
        c, s1, s2 = c_ref[...], s1_ref[...], s2_ref[...]
        lane = lax.broadcasted_iota(jnp.int32, (tm, HEAD_PAD), 1)
        rope_lanes = jnp.logical_and(lane >= QK_NOPE, lane < QK_HEAD)
        dqg = jnp.zeros((1, HEAD_PAD), F32)
        dkg = jnp.zeros((1, HEAD_PAD), F32)
        dkr = jnp.zeros((tm, HEAD_PAD), F32)
        for h in range(N_HEADS):
            sl = slice(h * HEAD_PAD, (h + 1) * HEAD_PAD)
            qh, rqh = _rms(q[:, sl], QK_HEAD)
            dy = _rope_bwd(dq_ref[:, sl], c, s1, s2)
            dqg = dqg + _colsum(dy * qh)
            dqp_ref[:, sl] = _rms_bwd(dy, qh, rqh, qg_ref[...], QK_HEAD).astype(BF16)
            kh, rkh = _rms(kn[:, sl] + krp, QK_HEAD)
            dyk = _rope_bwd(dk_ref[:, sl], c, s1, s2)
            dkg = dkg + _colsum(dyk * kh)
            dkh = _rms_bwd(dyk, kh, rkh, kg_ref[...], QK_HEAD)
            dkv_ref[:, sl] = dkh.astype(BF16)
            dkr = dkr + jnp.where(rope_lanes, dkh, 0.0)
        dkv_ref[:, QP_COLS:] = dv_ref[...].astype(BF16)
        dp_ref[:, OFF_CKV + 2 * D_RNN:] = dkr.astype(BF16)
        dqa = _dot(dqp_ref[...], wuq_ref[...])
        dp_ref[:, :Q_LORA] = _rms_bwd(dqa, xq, rq, gqa_ref[...], Q_LORA).astype(BF16)
        dkva = _dot_nt(dkv_ref[:, :QP_COLS], wuk_ref[...]) + _dot_nt(dkv_ref[:, QP_COLS:], wv_ref[...])
        dp_ref[:, Q_LORA:OFF_CKV] = _rms_bwd(dkva, xkv, rkv, gkva_ref[...], KV_LORA).astype(BF16)
        _acc(dqg_ref, first, dqg)
        _acc(dkg_ref, first, dkg)
        _acc(dgqa_ref, first, _colsum(dqa * xq))
        _acc(dgkva_ref, first, _colsum(dkva * xkv))
        finish()

    def row(w):
        return pl.BlockSpec((tm, w), lambda i: (i, 0))

    def acc(w):
        return pl.BlockSpec((1, w), lambda i: (0, 0))

    res = pl.pallas_call(
        body, name="qkv_bwd", grid=(n // tm,),
        in_specs=[row(Q_LORA), row(KV_LORA), row(LANES), row(QP_COLS), row(QP_COLS), row(D_ATTN), row(D_RNN), row(D_RNN),
                  _const_spec((1, Q_LORA)), _const_spec((1, KV_LORA)),
                  _const_spec((QP_COLS, Q_LORA)), _const_spec((KV_LORA, QP_COLS)), _const_spec((KV_LORA, D_ATTN)),
                  _const_spec((1, LANES)), _const_spec((1, LANES)), row(LANES), row(LANES), row(LANES)] + c_in,
        out_specs=[row(P_COLS), row(Q_LORA), row(KV_LORA), row(QP_COLS),
                   row(QP_COLS + D_ATTN), acc(LANES), acc(LANES), acc(Q_LORA), acc(KV_LORA)] + c_out,
        out_shape=[jax.ShapeDtypeStruct((n, P_COLS), BF16), jax.ShapeDtypeStruct((n, Q_LORA), BF16),
                   jax.ShapeDtypeStruct((n, KV_LORA), BF16), jax.ShapeDtypeStruct((n, QP_COLS), BF16),
                   jax.ShapeDtypeStruct((n, QP_COLS + D_ATTN), BF16),
                   jax.ShapeDtypeStruct((1, LANES), F32), jax.ShapeDtypeStruct((1, LANES), F32),
                   jax.ShapeDtypeStruct((1, Q_LORA), F32), jax.ShapeDtypeStruct((1, KV_LORA), F32)] + c_shape,
        scratch_shapes=c_sems,
        compiler_params=pltpu.CompilerParams(dimension_semantics=("arbitrary",), vmem_limit_bytes=VMEM_LIMIT),
    )(cq, ckv, kr, dq_r, dk_r, dv, dxr, dxg, gqa, gkva, w_uq_p, w_uk_p, w_v, qg, kg, rc, rs1, rs2, *srcs)
    return res[:9], res[9:]


KEY_CHUNK = 4 * LANES


def _key_chunks(t):
    count = max(t // KEY_CHUNK, 1)
    first = t - KEY_CHUNK * (count - 1)
    return [(0, first)] + [(first + KEY_CHUNK * c, KEY_CHUNK) for c in range(count - 1)]


def _softmax_parts(qh, k_ref, sl, tq, t):
    scores = []
    for start, size in _key_chunks(t):
        s = _dot_nt(qh, k_ref[start:start + size, sl]) * (QK_HEAD ** -0.5)
        if start < PAD_ROWS:
            key = lax.broadcasted_iota(jnp.int32, (tq, size), 1) + start
            s = jnp.where(key >= PAD_ROWS, s, -jnp.inf)
        scores.append(s)
    top = functools.reduce(jnp.maximum, [jnp.max(s, axis=-1, keepdims=True) for s in scores])
    es = [jnp.exp(s - top) for s in scores]
    return es, functools.reduce(jnp.add, [jnp.sum(e, axis=-1, keepdims=True) for e in es])


def _attn_specs(t, tq):
    nq = t // tq
    qspec = pl.BlockSpec((tq, 2 * HEAD_PAD), lambda b, hp, i: (b * nq + i, hp))
    kspec = pl.BlockSpec((t, 2 * HEAD_PAD), lambda b, hp, i: (b, hp))
    vspec = pl.BlockSpec((t, 2 * V_HEAD), lambda b, hp, i: (b, hp))
    ospec = pl.BlockSpec((tq, 2 * V_HEAD), lambda b, hp, i: (b * nq + i, hp))
    return nq, qspec, kspec, vspec, ospec


def _probs_spec(t, tq):
    return pl.BlockSpec((1, 2, tq, t), lambda b, hp, i: (b, hp, i, 0))


def _attn_fwd(q, k, v, srcs=(), scatter=()):
    n = q.shape[0]
    t = _t_pad()
    tq = t // 2
    nq, qspec, kspec, vspec, ospec = _attn_specs(t, tq)
    nk = len(srcs)
    c_in, c_out, c_shape, c_sems = _exchange_specs(srcs, scatter)

    def body(q_ref, k_ref, v_ref, *rest):
        o_ref, p_ref = rest[nk:nk + 2]
        finish = _ride(3, *_exchange_fns(rest[:nk], rest[nk + 2:2 * nk + 2], rest[2 * nk + 2:], scatter))
        lane = lax.broadcasted_iota(jnp.int32, (tq, 2 * V_HEAD), 1)
        outs = []
        for j in range(2):
            sl = slice(j * HEAD_PAD, (j + 1) * HEAD_PAD)
            es, l = _softmax_parts(q_ref[:, sl], k_ref, sl, tq, t)
            inv_l = 1.0 / l
            pv = []
            for e, (start, size) in zip(es, _key_chunks(t)):
                p = (e * inv_l).astype(BF16)
                p_ref[0, j, :, start:start + size] = p
                pv.append(_dot(p, v_ref[start:start + size, :]))
            outs.append(functools.reduce(jnp.add, pv))
        o_ref[...] = jnp.where(lane < V_HEAD, outs[0], outs[1])
        finish()

    res = pl.pallas_call(
        body, name="attn_fwd", grid=(n // t, N_HEADS // 2, nq),
        in_specs=[qspec, kspec, vspec] + c_in, out_specs=[ospec, _probs_spec(t, tq)] + c_out,
        out_shape=[jax.ShapeDtypeStruct((n, D_ATTN), F32), jax.ShapeDtypeStruct((n // t, N_HEADS, t, t), BF16)] + c_shape,
        scratch_shapes=c_sems,
        compiler_params=pltpu.CompilerParams(dimension_semantics=("arbitrary", "arbitrary", "arbitrary"),
                                             vmem_limit_bytes=VMEM_LIMIT),
    )(q, k, v, *srcs)
    return res[0], res[1], res[2:]


def _attn_bwd(q, k, v, do, o, probs, srcs=(), scatter=()):
    n = q.shape[0]
    t = _t_pad()
    tq = t // 2
    nq, qspec, kspec, vspec, ospec = _attn_specs(t, tq)
    nk = len(srcs)
    c_in, c_out, c_shape, c_sems = _exchange_specs(srcs, scatter)

    def body(q_ref, k_ref, v_ref, do_ref, o_ref, p_ref, *rest):
        dq_ref, dk_ref, dv_ref = rest[nk:nk + 3]
        finish = _ride(3, *_exchange_fns(rest[:nk], rest[nk + 3:2 * nk + 3], rest[2 * nk + 3:], scatter))

        @pl.when(pl.program_id(2) == 0)
        def _():
            dk_ref[...] = jnp.zeros_like(dk_ref)
            dv_ref[...] = jnp.zeros_like(dv_ref)

        lane = lax.broadcasted_iota(jnp.int32, (tq, 2 * V_HEAD), 1)
        do = do_ref[...]
        do_o = do * o_ref[...]
        chunks = _key_chunks(t)
        dvs = [None] * len(chunks)
        for j in range(2):
            sl = slice(j * HEAD_PAD, (j + 1) * HEAD_PAD)
            qh = q_ref[:, sl]
            in_head = (lane < V_HEAD) if j == 0 else (lane >= V_HEAD)
            doh = jnp.where(in_head, do, 0.0).astype(BF16)
            delta = jnp.sum(jnp.where(in_head, do_o, 0.0), axis=-1, keepdims=True)
            dq = jnp.zeros((tq, HEAD_PAD), F32)
            for c, (start, size) in enumerate(chunks):
                rows = slice(start, start + size)
                p = p_ref[0, j, :, rows]
                dp = _dot_nt(doh, v_ref[rows, :])
                ds = (p.astype(F32) * (dp - delta) * (QK_HEAD ** -0.5)).astype(BF16)
                dq = dq + _dot(ds, k_ref[rows, sl])
                dk_ref[rows, sl] += _dot_tn(ds, qh)
                dvc = _dot_tn(p, doh)
                dvs[c] = dvc if dvs[c] is None else dvs[c] + dvc
            dq_ref[:, sl] = dq
        for (start, size), dvc in zip(chunks, dvs):
            dv_ref[start:start + size, :] += dvc
        finish()

    res = pl.pallas_call(
        body, name="attn_bwd", grid=(n // t, N_HEADS // 2, nq),
        in_specs=[qspec, kspec, vspec, ospec, ospec, _probs_spec(t, tq)] + c_in, out_specs=[qspec, kspec, vspec] + c_out,
        out_shape=[jax.ShapeDtypeStruct((n, QP_COLS), F32), jax.ShapeDtypeStruct((n, QP_COLS), F32),
                   jax.ShapeDtypeStruct((n, D_ATTN), F32)] + c_shape, scratch_shapes=c_sems,
        compiler_params=pltpu.CompilerParams(dimension_semantics=("arbitrary", "arbitrary", "arbitrary"),
                                             vmem_limit_bytes=VMEM_LIMIT),
    )(q, k, v, do, o, probs, *srcs)
    return res[:3], res[3:]


SCAN_STEPS = 8


def _scan(chains, t):
    seg = t // 8
    rows = lax.broadcasted_iota(jnp.int32, (8, LANES), 0)

    def step(i, carry):
        carry = list(carry)
        for u in range(SCAN_STEPS):
            j = i * SCAN_STEPS + u
            for n, (a_ref, b_ref, h_ref, p_ref, reverse) in enumerate(chains):
                h, p = carry[n]
                idx = pl.ds(seg - 1 - j if reverse else j, 8, stride=seg)
                a = a_ref[idx, :]
                h = a * h + b_ref[idx, :]
                p = a * p
                h_ref[idx, :] = h
                p_ref[idx, :] = p
                carry[n] = (h, p)
        return tuple(carry)

    init = tuple((jnp.zeros((8, LANES), F32), jnp.ones((8, LANES), F32)) for _ in chains)
    ends = lax.fori_loop(0, seg // SCAN_STEPS, step, init)
    for (_, _, h_ref, p_ref, reverse), (b, a) in zip(chains, ends):
        for d in (1, 2, 4):
            if reverse:
                keep = rows < 8 - d
                a_n, b_n = pltpu.roll(a, 8 - d, 0), pltpu.roll(b, 8 - d, 0)
            else:
                keep = rows >= d
                a_n, b_n = pltpu.roll(a, d, 0), pltpu.roll(b, d, 0)
            b = a * jnp.where(keep, b_n, 0.0) + b
            a = a * jnp.where(keep, a_n, 1.0)
        for s in (range(7) if reverse else range(1, 8)):
            sl = slice(s * seg, (s + 1) * seg)
            carry_in = b[s + 1:s + 2, :] if reverse else b[s - 1:s, :]
            h_ref[sl, :] = h_ref[sl, :] + p_ref[sl, :] * carry_in


def _shift_rows(x, s, rows, t):
    if s == 0:
        return x
    rolled = pltpu.roll(x, s % t, 0)
    return jnp.where(rows >= s, rolled, 0.0) if s > 0 else jnp.where(rows < t + s, rolled, 0.0)


def _neg_expm1(x, exp_x):
    series = -x * (1.0 + x * (0.5 + x * (1.0 / 6 + x * (1.0 / 24))))
    return jnp.where(x > -0.1, series, 1.0 - exp_x)


def _sigmoid(x):
    return 0.5 * jnp.tanh(0.5 * x) + 0.5


def _gelu_parts(x):
    k = math.sqrt(2.0 / math.pi)
    th = jnp.tanh(k * (x + 0.044715 * x * x * x))
    g = 0.5 * x * (1.0 + th)
    dg = 0.5 * (1.0 + th) + 0.5 * x * (1.0 - th * th) * k * (1.0 + 3 * 0.044715 * x * x)
    return g, dg


def _lru_gates(xc, gates, lam_ref, valid, d):
    r = _sigmoid(gates[:, (2 * d) * LANES:(2 * d + 1) * LANES])
    i = _sigmoid(gates[:, (2 * d + 1) * LANES:(2 * d + 2) * LANES])
    neg_lam = -lam_ref[d:d + 1, :]
    sp = jnp.maximum(neg_lam, 0.0) + jnp.log1p(jnp.exp(-jnp.abs(neg_lam)))
    log_a = -LRU_C * r * sp
    a = jnp.exp(log_a)
    m = jnp.maximum(_neg_expm1(2.0 * log_a, a * a), 0.0)
    sq = jnp.sqrt(m)
    b = jnp.where(valid, sq * (i * xc), 0.0)
    return r, i, sp, a, m, sq, b


def _conv(xr, cw_ref, cb_ref, rows, t):
    return (cw_ref[0:1, :] * _shift_rows(xr, 2, rows, t) + cw_ref[1:2, :] * _shift_rows(xr, 1, rows, t)
            + cw_ref[2:3, :] * xr + cw_ref[3:4, :] * _shift_rows(xr, -1, rows, t) + cb_ref[...])


def _rnn_specs(t):
    seq = pl.BlockSpec((t, LANES), lambda cb, b: (b, cb))
    cw = pl.BlockSpec((4, LANES), lambda cb, b: (0, cb))
    vec1 = pl.BlockSpec((1, LANES), lambda cb, b: (0, cb))
    vec2 = pl.BlockSpec((2, LANES), lambda cb, b: (0, cb))
    wblk = pl.BlockSpec((1, LANES, 4 * LANES), lambda cb, b: (cb, 0, 0))
    gbias = pl.BlockSpec((1, 1, 4 * LANES), lambda cb, b: (cb, 0, 0))
    return seq, cw, vec1, vec2, wblk, gbias


def _rnn_fwd(xr, xg, conv_w, conv_b, wblk, gbias, lam):
    n = xr.shape[0]
    t = _t_pad()
    seq, cw, vec1, vec2, wspec, gspec = _rnn_specs(t)

    def body(xr_ref, xg_ref, cw_ref, cb_ref, w_ref, gb_ref, lam_ref, o_ref, a_s, b_s, h_s, p_s):
        rows = lax.broadcasted_iota(jnp.int32, (t, LANES), 0)
        valid = rows >= PAD_ROWS
        xc = _conv(xr_ref[...], cw_ref, cb_ref, rows, t)
        gates = _dot(xc.astype(BF16), w_ref[0]) + gb_ref[0]
        for d in range(2):
            _, _, _, a, _, _, b = _lru_gates(xc, gates, lam_ref, valid, d)
            a_s[d] = a
            b_s[d] = b
        _scan([(a_s.at[d], b_s.at[d], h_s.at[d], p_s.at[d], d == 1) for d in range(2)], t)
        g, _ = _gelu_parts(xg_ref[...])
        o_ref[...] = (h_s[0] + h_s[1]) * g

    return pl.pallas_call(
        body, name="rnn_fwd", grid=(D_RNN // LANES, n // t),
        in_specs=[seq, seq, cw, vec1, wspec, gspec, vec2], out_specs=seq,
        out_shape=jax.ShapeDtypeStruct((n, D_RNN), F32),
        scratch_shapes=[pltpu.VMEM((2, t, LANES), F32)] * 4,
        compiler_params=pltpu.CompilerParams(dimension_semantics=("parallel", "parallel"), vmem_limit_bytes=VMEM_LIMIT),
    )(xr, xg, conv_w, conv_b, wblk, gbias, lam)


def _rnn_bwd(xr, xg, do, conv_w, conv_b, wblk, gbias, lam, srcs=(), scatter=()):
    n = xr.shape[0]
    t = _t_pad()
    seq, cw, vec1, vec2, wspec, gspec = _rnn_specs(t)
    nk = len(srcs)
    c_in, c_out, c_shape, c_sems = _exchange_specs(srcs, scatter)

    def body(xr_ref, xg_ref, do_ref, cw_ref, cb_ref, w_ref, gb_ref, lam_ref, *rest):
        dxr_ref, dxg_ref, dcw_ref, dcb_ref, dw_ref, dgb_ref, dlam_ref = rest[nk:nk + 7]
        a_s, b_s, h_s, l_s, p_s, back_s, r_s, i_s, q_s, dg_s = rest[2 * nk + 7 + len(c_sems):]
        finish = _ride(2, *_exchange_fns(rest[:nk], rest[nk + 7:2 * nk + 7], rest[2 * nk + 7:2 * nk + 7 + len(c_sems)],
                                         scatter))
        first = pl.program_id(1) == 0
        rows = lax.broadcasted_iota(jnp.int32, (t, LANES), 0)
        valid = rows >= PAD_ROWS
        xr = xr_ref[...]
        xc = _conv(xr, cw_ref, cb_ref, rows, t)
        xcb = xc.astype(BF16)
        gates = _dot(xcb, w_ref[0]) + gb_ref[0]
        sps = []
        for d in range(2):
            r_s[d], i_s[d], sp, a_s[d], _, q_s[d], b_s[d] = _lru_gates(xc, gates, lam_ref, valid, d)
            sps.append(sp)
        _scan([(a_s.at[d], b_s.at[d], h_s.at[d], p_s.at[d], d == 1) for d in range(2)], t)
        g, dg = _gelu_parts(xg_ref[...])
        do = do_ref[...]
        dxg_ref[...] = do * (h_s[0] + h_s[1]) * dg
        b_s[0] = do * g
        for d in range(2):
            back_s[d] = _shift_rows(a_s[d], -1 if d == 0 else 1, rows, t)
        _scan([(back_s.at[d], b_s.at[0], l_s.at[d], p_s.at[d], d == 0) for d in range(2)], t)
        dxc = jnp.zeros((t, LANES), F32)
        dlams = []
        for d in range(2):
            r, i, sp, a, sq = r_s[d], i_s[d], sps[d], a_s[d], q_s[d]
            lam_t = l_s[d]
            da = lam_t * _shift_rows(h_s[d], 1 if d == 0 else -1, rows, t)
            lam_v = jnp.where(valid, lam_t, 0.0)
            dsq = lam_v * (i * xc)
            di = lam_v * sq * xc
            dxc = dxc + lam_v * sq * i
            dm = jnp.where(sq > 0.0, dsq * 0.5 / jnp.where(sq > 0.0, sq, 1.0), 0.0)
            dla = da * a - 2.0 * dm * a * a
            dr = dla * (-LRU_C) * sp
            dsp = _colsum(dla * (-LRU_C) * r)
            dlams.append(dsp * -jax.nn.sigmoid(-lam_ref[d:d + 1, :]))
            dg_s[:, (2 * d) * LANES:(2 * d + 1) * LANES] = (dr * r * (1.0 - r)).astype(BF16)
            dg_s[:, (2 * d + 1) * LANES:(2 * d + 2) * LANES] = (di * i * (1.0 - i)).astype(BF16)
        dgates = dg_s[...]
        dxc = dxc + _dot_nt(dgates, w_ref[0])
        taps = [_shift_rows(dxc, j - 2, rows, t) for j in range(4)]
        dxr_ref[...] = (cw_ref[0:1, :] * taps[0] + cw_ref[1:2, :] * taps[1] + cw_ref[2:3, :] * taps[2]
                        + cw_ref[3:4, :] * taps[3])
        dcw = jnp.concatenate([_colsum(tap * xr) for tap in taps], axis=0)
        _acc(dcw_ref, first, dcw)
        _acc(dcb_ref, first, _colsum(dxc))
        _acc(dw_ref, first, _dot_tn(xcb, dgates)[None])
        _acc(dgb_ref, first, _colsum(dgates.astype(F32))[None])
        _acc(dlam_ref, first, jnp.concatenate(dlams, axis=0))
        finish()

    res = pl.pallas_call(
        body, name="rnn_bwd", grid=(D_RNN // LANES, n // t),
        in_specs=[seq, seq, seq, cw, vec1, wspec, gspec, vec2] + c_in,
        out_specs=[seq, seq, cw, vec1, wspec, gspec, vec2] + c_out,
        out_shape=[jax.ShapeDtypeStruct((n, D_RNN), F32), jax.ShapeDtypeStruct((n, D_RNN), F32),
                   jax.ShapeDtypeStruct((4, D_RNN), F32), jax.ShapeDtypeStruct((1, D_RNN), F32),
                   jax.ShapeDtypeStruct((D_RNN // LANES, LANES, 4 * LANES), F32),
                   jax.ShapeDtypeStruct((D_RNN // LANES, 1, 4 * LANES), F32), jax.ShapeDtypeStruct((2, D_RNN), F32)]
        + c_shape,
        scratch_shapes=c_sems + [pltpu.VMEM((2, t, LANES), F32)] * 9 + [pltpu.VMEM((t, 4 * LANES), BF16)],
        compiler_params=pltpu.CompilerParams(dimension_semantics=("arbitrary", "arbitrary"), vmem_limit_bytes=VMEM_LIMIT),
    )(xr, xg, do, conv_w, conv_b, wblk, gbias, lam, *srcs)
    return res[:7], res[7:]


def _post(oa, orn, h0, tgt, ga, gr, g2, w_out, w_gate, w_up, w_down):
    n = oa.shape[0]
    tm = _row_tile(n)
    t = _t_pad()

    def body(oa_ref, or_ref, h0_ref, tgt_ref, ga_ref, gr_ref, g2_ref, wo_ref, wg_ref, wu_ref, wd_ref,
             doa_ref, dor_ref, dh1_ref, mix_ref, h1n_ref, act_ref, dgate_ref, dup_ref, dy_ref,
             loss_ref, dga_ref, dgr_ref, dg2_ref, gate_s, up_s):
        first = pl.program_id(0) == 0
        xa, ra = _rms(oa_ref[...], D_ATTN)
        xr, rr = _rms(or_ref[...], D_RNN)
        mix = jnp.concatenate([(xa * ga_ref[...]).astype(BF16), (xr * gr_ref[...]).astype(BF16)], axis=-1)
        mix_ref[...] = mix.T
        h1 = h0_ref[...] + _dot(mix, wo_ref[...])
        x2, r2 = _rms(h1, D_MODEL)
        h1n = (x2 * g2_ref[...]).astype(BF16)
        h1n_ref[...] = h1n
        y = h1
        for cs in range(0, D_FF, FF_CHUNK):
            sl = slice(cs, cs + FF_CHUNK)
            gate = _dot_nt(h1n, wg_ref[sl, :])
            up = _dot_nt(h1n, wu_ref[sl, :])
            gate_s[:, sl] = gate
            up_s[:, sl] = up
            act = (gate * _sigmoid(gate) * up).astype(BF16)
            act_ref[sl, :] = act.T
            y = y + _dot(act, wd_ref[sl, :])
        row = pl.program_id(0) * tm + lax.broadcasted_iota(jnp.int32, (tm, 1), 0)
        for _ in range(1, n // t):
            row = jnp.where(row >= t, row - t, row)
        err = jnp.where(row >= PAD_ROWS + N_META, y - tgt_ref[...], 0.0)
        _acc(loss_ref, first, jnp.full((1, LANES), 0.5 / D_MODEL, F32) * jnp.sum(err * err))
        dy = err * (1.0 / D_MODEL)
        dyb = dy.astype(BF16)
        dy_ref[...] = dyb
        dh1n = jnp.zeros((tm, D_MODEL), F32)
        for cs in range(0, D_FF, FF_CHUNK):
            sl = slice(cs, cs + FF_CHUNK)
            dact = _dot_nt(dyb, wd_ref[sl, :])
            gate, up = gate_s[:, sl], up_s[:, sl]
            sg = _sigmoid(gate)
            dgate = (dact * up * sg * (1.0 + gate * (1.0 - sg))).astype(BF16)
            dup = (dact * gate * sg).astype(BF16)
            dgate_ref[sl, :] = dgate.T
            dup_ref[sl, :] = dup.T
            dh1n = dh1n + _dot(dgate, wg_ref[sl, :]) + _dot(dup, wu_ref[sl, :])
        _acc(dg2_ref, first, _colsum(dh1n * x2))
        dh1 = dy + _rms_bwd(dh1n, x2, r2, g2_ref[...], D_MODEL)
        dh1_ref[...] = dh1
        dmix = _dot_nt(dh1.astype(BF16), wo_ref[...])
        dma, dmr = dmix[:, :D_ATTN], dmix[:, D_ATTN:]
        _acc(dga_ref, first, _colsum(dma * xa))
        _acc(dgr_ref, first, _colsum(dmr * xr))
        doa_ref[...] = _rms_bwd(dma, xa, ra, ga_ref[...], D_ATTN)
        dor_ref[...] = _rms_bwd(dmr, xr, rr, gr_ref[...], D_RNN)

    def row(w):
        return pl.BlockSpec((tm, w), lambda i: (i, 0))

    def acc(w):
        return pl.BlockSpec((1, w), lambda i: (0, 0))

    def col(w):
        return pl.BlockSpec((w, tm), lambda i: (0, i))

    outs = [(D_ATTN, F32, row), (D_RNN, F32, row), (D_MODEL, F32, row), (D_MODEL, BF16, col), (D_MODEL, BF16, row),
            (D_FF, BF16, col), (D_FF, BF16, col), (D_FF, BF16, col), (D_MODEL, BF16, row)]
    accs = [LANES, D_ATTN, D_RNN, D_MODEL]
    return pl.pallas_call(
        body, name="post", grid=(n // tm,),
        in_specs=[row(D_ATTN), row(D_RNN), row(D_MODEL), row(D_MODEL),
                  _const_spec((1, D_ATTN)), _const_spec((1, D_RNN)), _const_spec((1, D_MODEL)),
                  _const_spec((D_MODEL, D_MODEL)), _const_spec((D_FF, D_MODEL)), _const_spec((D_FF, D_MODEL)),
                  _const_spec((D_FF, D_MODEL))],
        out_specs=[spec(w) for w, _, spec in outs] + [acc(w) for w in accs],
        out_shape=[jax.ShapeDtypeStruct((n, w) if spec is row else (w, n), dt) for w, dt, spec in outs]
        + [jax.ShapeDtypeStruct((1, w), F32) for w in accs],
        scratch_shapes=[pltpu.VMEM((tm, D_FF), F32), pltpu.VMEM((tm, D_FF), F32)],
        compiler_params=pltpu.CompilerParams(dimension_semantics=("arbitrary",), vmem_limit_bytes=VMEM_LIMIT),
    )(oa, orn, h0, tgt, ga, gr, g2, w_out, w_gate, w_up, w_down)


def _in_bwd(dp, h0, dh1, ln1_g, w_in_p, srcs=(), scatter=()):
    n = h0.shape[0]
    tm = _row_tile(n)
    nk = len(srcs)
    c_in, c_out, c_shape, c_sems = _exchange_specs(srcs, scatter)

    def body(dp_ref, h0_ref, dh1_ref, g_ref, w_ref, *rest):
        dh0_ref, dg_ref = rest[nk:nk + 2]
        finish = _ride(1, *_exchange_fns(rest[:nk], rest[nk + 2:2 * nk + 2], rest[2 * nk + 2:], scatter))
        dhn = _dot(dp_ref[...], w_ref[...])
        xhat, r = _rms(h0_ref[...], D_MODEL)
        _acc(dg_ref, pl.program_id(0) == 0, _colsum(dhn * xhat))
        dh0_ref[...] = dh1_ref[...] + _rms_bwd(dhn, xhat, r, g_ref[...], D_MODEL)
        finish()

    def row(w):
        return pl.BlockSpec((tm, w), lambda i: (i, 0))

    res = pl.pallas_call(
        body, name="in_bwd", grid=(n // tm,),
        in_specs=[row(P_COLS), row(D_MODEL), row(D_MODEL), _const_spec((1, D_MODEL)), _const_spec((P_COLS, D_MODEL))] + c_in,
        out_specs=[row(D_MODEL), pl.BlockSpec((1, D_MODEL), lambda i: (0, 0))] + c_out,
        out_shape=[jax.ShapeDtypeStruct((n, D_MODEL), F32), jax.ShapeDtypeStruct((1, D_MODEL), F32)] + c_shape,
        scratch_shapes=c_sems,
        compiler_params=pltpu.CompilerParams(dimension_semantics=("arbitrary",), vmem_limit_bytes=VMEM_LIMIT),
    )(dp, h0, dh1, ln1_g, w_in_p, *srcs)
    return res[:2], res[2:]


MAX_TILE = D_FF // 2


def _pick_tile(width, cap):
    best = LANES
    for mult in range(1, width // LANES + 1):
        cand = mult * LANES
        if width % cand == 0 and cand <= cap:
            best = cand
    return best


def _matmul_tn(name, a, b, srcs=(), scatter=()):
    n, ka = a.shape
    kb = b.shape[1]
    ta, tb = _pick_tile(ka, MAX_TILE), _pick_tile(kb, MAX_TILE)
    tk = n // 4
    nk = len(srcs)
    c_in, c_out, c_shape, c_sems = _exchange_specs(srcs, scatter)

    def body(a_ref, b_ref, *rest):
        o_ref = rest[nk]
        finish = _ride(3, *_exchange_fns(rest[:nk], rest[nk + 1:2 * nk + 1], rest[2 * nk + 1:], scatter))
        _acc(o_ref, pl.program_id(2) == 0, _dot_tn(a_ref[...].astype(BF16), b_ref[...].astype(BF16)))
        finish()

    res = pl.pallas_call(
        body, name=name, grid=(ka // ta, kb // tb, n // tk),
        in_specs=[pl.BlockSpec((tk, ta), lambda i, j, k: (k, i)), pl.BlockSpec((tk, tb), lambda i, j, k: (k, j))] + c_in,
        out_specs=[pl.BlockSpec((ta, tb), lambda i, j, k: (i, j))] + c_out,
        out_shape=[jax.ShapeDtypeStruct((ka, kb), F32)] + c_shape, scratch_shapes=c_sems,
        compiler_params=pltpu.CompilerParams(dimension_semantics=("arbitrary", "arbitrary", "arbitrary"),
                                             vmem_limit_bytes=VMEM_LIMIT),
    )(a, b, *srcs)
    return res[0], res[1:]


def _matmul_shards(name, at, b):
    ka, n = at.shape
    kb = b.shape[1]
    ta, tb = _pick_tile(ka, MAX_TILE), _pick_tile(kb, MAX_TILE)
    tk = n // 2
    width = ka // N_DEV
    per = ta // width

    def body(a_ref, b_ref, o_ref, acc_ref):
        _acc(acc_ref, pl.program_id(2) == 0, _dot(a_ref[...], b_ref[...].astype(BF16)))

        @pl.when(pl.program_id(2) == pl.num_programs(2) - 1)
        def _():
            for s in range(per):
                o_ref[s] = acc_ref[s * width:(s + 1) * width, :].astype(BF16)

    return pl.pallas_call(
        body, name=name, grid=(ka // ta, kb // tb, n // tk),
        in_specs=[pl.BlockSpec((ta, tk), lambda i, j, k: (i, k)), pl.BlockSpec((tk, tb), lambda i, j, k: (k, j))],
        out_specs=pl.BlockSpec((per, width, tb), lambda i, j, k: (i, 0, j)),
        out_shape=jax.ShapeDtypeStruct((N_DEV, width, kb), BF16),
        scratch_shapes=[pltpu.VMEM((ta, tb), F32)],
        compiler_params=pltpu.CompilerParams(dimension_semantics=("parallel", "parallel", "arbitrary"),
                                             vmem_limit_bytes=VMEM_LIMIT),
    )(at, b)


def _adamw_math(g8_ref, w_ref, m_ref, v_ref, g_ref, d_ref, nm_ref, nv_ref):
    g = g8_ref[0].astype(F32)
    for s in range(1, N_DEV):
        g = g + g8_ref[s].astype(F32)
    g_ref[...] = g
    nm = ADAM_B1 * m_ref[...] + (1.0 - ADAM_B1) * g
    nv = ADAM_B2 * v_ref[...] + (1.0 - ADAM_B2) * (g * g)
    nm_ref[...] = nm
    nv_ref[...] = nv
    m_hat = nm / (1.0 - ADAM_B1 ** ADAM_STEP)
    v_hat = nv / (1.0 - ADAM_B2 ** ADAM_STEP)
    d_ref[...] = -ADAM_LR * (m_hat / (jnp.sqrt(v_hat) + ADAM_EPS) + ADAM_WD * w_ref[...])


def _adamw_many(name, items):
    count = len(items)

    def body(*refs):
        ins, outs = refs[:4 * count], refs[4 * count:]
        for i in range(count):
            _adamw_math(*ins[4 * i:4 * i + 4], *outs[4 * i:4 * i + 4])

    flat = [a for item in items for a in item]
    res = pl.pallas_call(
        body, name=name,
        out_shape=[jax.ShapeDtypeStruct(item[1].shape, F32) for item in items for _ in range(4)],
        compiler_params=pltpu.CompilerParams(vmem_limit_bytes=VMEM_LIMIT),
    )(*flat)
    return [tuple(res[4 * i:4 * i + 4]) for i in range(count)]


def _adamw(name, g8, w, m, v):
    rows, cols = w.shape
    tr = rows
    for cand in (256, 176, 128, 64):
        if rows % cand == 0 and rows > cand:
            tr = cand
            break

    def body(*refs):
        _adamw_math(*refs)

    blk = pl.BlockSpec((tr, cols), lambda i: (i, 0))
    return pl.pallas_call(
        body, name=name, grid=(rows // tr,),
        in_specs=[pl.BlockSpec((N_DEV, tr, cols), lambda i: (0, i, 0)), blk, blk, blk],
        out_specs=[blk] * 4, out_shape=[jax.ShapeDtypeStruct((rows, cols), F32)] * 4,
        compiler_params=pltpu.CompilerParams(dimension_semantics=("parallel",), vmem_limit_bytes=VMEM_LIMIT),
    )(g8, w, m, v)


def _exchange_specs(srcs, scatter):
    nk = len(srcs)
    if not nk:
        return [], [], [], []
    any_spec = pl.BlockSpec(memory_space=pl.ANY)
    out_shape = [jax.ShapeDtypeStruct(s.shape if sc else (N_DEV,) + s.shape, s.dtype) for s, sc in zip(srcs, scatter)]
    sems = [pltpu.SemaphoreType.DMA((nk, N_DEV - 1)), pltpu.SemaphoreType.DMA((nk, N_DEV - 1)),
            pltpu.SemaphoreType.DMA((nk,))]
    return [any_spec] * nk, [any_spec] * nk, out_shape, sems


FLIPS = ((0, 0, 1), (1, 0, 0), (0, 1, 0), (1, 1, 0), (1, 0, 1), (0, 1, 1), (1, 1, 1))
N_CHIP_PEERS = 3


def _exchange_fns(src_refs, out_refs, sems, scatter):
    nk = len(src_refs)
    if not nk:
        return (lambda: None), (lambda: None), (lambda: None)
    send_sems, recv_sems, local_sems = sems
    first = 1 + N_CHIP_PEERS

    def plan():
        x, y, c = lax.axis_index("x"), lax.axis_index("y"), lax.axis_index("c")
        me = 4 * x + 2 * y + c
        peers = [(1 - x if fx else x, 1 - y if fy else y, 1 - c if fc else c) for fx, fy, fc in FLIPS]
        pids = [4 * px + 2 * py + pc for px, py, pc in peers]

        def remote(k, j, src, dst, to):
            return pltpu.make_async_remote_copy(src_ref=src, dst_ref=dst, send_sem=send_sems.at[k, j],
                                                recv_sem=recv_sems.at[k, j], device_id=to, device_id_type=MESH)

        def mine(k, dest):
            return src_refs[k].at[dest] if scatter[k] else src_refs[k]

        local = [pltpu.make_async_copy(mine(k, me), out_refs[k].at[me], local_sems.at[k]) for k in range(nk)]
        direct = [remote(k, j, mine(k, pids[j]), out_refs[k].at[me], peers[j])
                  for k in range(nk) for j in range(len(FLIPS) if scatter[k] else first)]
        relays = {(k, j): remote(k, j, out_refs[k].at[pids[j - N_CHIP_PEERS]], out_refs[k].at[pids[j - N_CHIP_PEERS]], peers[0])
                  for k in range(nk) if not scatter[k] for j in range(first, len(FLIPS))}
        arrivals = {(k, j): remote(k, j, out_refs[k].at[pids[j]], out_refs[k].at[pids[j]], peers[j])
                    for k in range(nk) for j in range(len(FLIPS))}
        return local, direct, relays, arrivals

    def start():
        local, direct, _, _ = plan()
        for cp in local + direct:
            cp.start()

    def relay():
        _, _, relays, arrivals = plan()
        for (k, j), cp in relays.items():
            arrivals[k, j - N_CHIP_PEERS].wait_recv()
            cp.start()

    def wait():
        local, direct, relays, arrivals = plan()
        for (k, j), cp in arrivals.items():
            if (k, j + N_CHIP_PEERS) not in relays:
                cp.wait_recv()
        for cp in direct + list(relays.values()):
            cp.wait_send()
        for cp in local:
            cp.wait()

    return start, relay, wait


def _grid_step(rank):
    step, total = 0, 1
    for axis in range(rank):
        step = step * pl.num_programs(axis) + pl.program_id(axis)
        total = total * pl.num_programs(axis)
    return step, total


def _ride(rank, start, relay, wait):
    step, total = _grid_step(rank)
    pl.when(step == 0)(start)
    pl.when(step == (3 * total) // 4)(relay)
    return lambda: pl.when(step == total - 1)(wait)


def _exchange(name, srcs, scatter):
    nk = len(srcs)
    c_in, c_out, c_shape, c_sems = _exchange_specs(srcs, scatter)

    def body(*refs):
        start, relay, wait = _exchange_fns(refs[:nk], refs[nk:2 * nk], refs[2 * nk:], scatter)
        start()
        relay()
        wait()

    return pl.pallas_call(body, name=name, in_specs=c_in, out_specs=c_out, out_shape=c_shape, scratch_shapes=c_sems)(*srcs)


def _cols_from_shards(g):
    return jnp.transpose(g, (1, 0, 2)).reshape(g.shape[1], -1)


def _cols_to_shards(w):
    return jnp.transpose(w.reshape(w.shape[0], N_DEV, -1), (1, 0, 2))


def _prep(x, tgt, srcs, scatter):
    nb = x.shape[0]
    t = _t_pad()
    head = PAD_ROWS + N_META
    nk = len(srcs)
    c_in, c_out, c_shape, c_sems = _exchange_specs(srcs, scatter)

    def body(x_ref, tgt_ref, *rest):
        h0_ref, tp_ref = rest[nk:nk + 2]
        finish = _ride(1, *_exchange_fns(rest[:nk], rest[nk + 2:2 * nk + 2], rest[2 * nk + 2:], scatter))
        lead = pl.program_id(0) == 0

        @pl.when(lead)
        def _():
            h0_ref[...] = jnp.zeros_like(h0_ref)
            tp_ref[...] = jnp.zeros_like(tp_ref)

        @pl.when(jnp.logical_not(lead))
        def _():
            h0_ref[...] = x_ref[...]
            tp_ref[...] = tgt_ref[...]

        finish()

    src = pl.BlockSpec((nb, head, D_MODEL), lambda j: (0, jnp.maximum(j - 1, 0), 0))
    dst = pl.BlockSpec((nb, head, D_MODEL), lambda j: (0, j, 0))
    padded = jax.ShapeDtypeStruct((nb, t, D_MODEL), F32)
    res = pl.pallas_call(
        body, name="prep", grid=(t // head,), in_specs=[src, src] + c_in, out_specs=[dst, dst] + c_out,
        out_shape=[padded, padded] + c_shape, scratch_shapes=c_sems,
        compiler_params=pltpu.CompilerParams(dimension_semantics=("arbitrary",)),
    )(x, tgt, *srcs)
    return res[0], res[1], res[2:]


def _rope_tables(n):
    t = _t_pad()
    pos = np.arange(t, dtype=np.float32) - np.float32(PAD_ROWS)
    half = QK_ROPE // 2
    freqs = (1.0 / (ROPE_THETA ** (np.arange(half, dtype=np.float32) / half))).astype(np.float32)
    ang = pos[:, None] * freqs[None, :]
    cos, sin = np.cos(ang), np.sin(ang)
    z = lambda w: np.zeros((t, w), np.float32)
    c = np.concatenate([np.ones((t, QK_NOPE), np.float32), cos, cos, z(HEAD_PAD - QK_HEAD)], axis=1)
    s1 = np.concatenate([z(QK_NOPE + half), sin, z(HEAD_PAD - QK_HEAD)], axis=1)
    s2 = np.concatenate([z(QK_NOPE), -sin, z(HEAD_PAD - QK_NOPE - half)], axis=1)
    return tuple(jnp.asarray(np.tile(a, (n // t, 1))) for a in (c, s1, s2))


def _block_diag_gates(lru_wa, lru_wi):
    eye = jnp.eye(2, dtype=lru_wa.dtype)

    def bd(w):
        w = w.reshape(2, D_RNN // LANES, 2, RNN_BW, RNN_BW)
        full = w[:, :, :, :, None, :] * eye[None, None, :, None, :, None]
        return full.reshape(2, D_RNN // LANES, LANES, LANES)

    a, i = bd(lru_wa), bd(lru_wi)
    return jnp.concatenate([a[0], i[0], a[1], i[1]], axis=-1)


def _unblock_gates(dw):
    nb = D_RNN // LANES
    parts = dw.reshape(nb, 2, RNN_BW, 4, 2, RNN_BW)
    diag = jnp.stack([parts[:, k, :, :, k, :] for k in range(2)], axis=1)
    diag = jnp.transpose(diag, (3, 0, 1, 2, 4)).reshape(4, 2 * nb, RNN_BW, RNN_BW)
    return jnp.stack([diag[0], diag[2]]), jnp.stack([diag[1], diag[3]])


WEIGHTS = ("meta_tokens", "ln1_g", "w_in", "q_a_norm_g", "w_uq", "kv_a_norm_g", "w_ukv", "q_norm_g", "k_norm_g",
           "conv_w", "conv_b", "lru_wa", "lru_ba", "lru_wi", "lru_bi", "lru_lambda", "attn_out_g", "rnn_out_g",
           "w_out", "ln2_g", "w_gate", "w_up", "w_down")
BIG = ("w_in", "w_uq", "w_ukv", "w_out", "w_gate", "w_up", "w_down")
TRANSPOSED = ("w_in", "w_uq", "w_gate", "w_up")
ROW_SHARDED = ("w_out", "w_down") + TRANSPOSED
REPLICATED = ("ln1_g", "q_a_norm_g", "kv_a_norm_g", "q_norm_g", "k_norm_g", "conv_b", "lru_wa", "lru_wi",
              "attn_out_g", "rnn_out_g", "ln2_g")
WHOLE = REPLICATED + ("loss",)
G_FIRST = ("w_in", "meta_tokens")
G_MID = ("w_uq", "w_ukv", "conv_w", "lru_ba", "lru_bi", "lru_lambda")
LATE = ("w_out", "w_gate", "w_up", "w_down")
G_LAST = ("meta_tokens", "ln1_g")


def _local_step(x, tgt, ex):
    nb = x.shape[0]
    t = _t_pad()
    n = nb * t
    local = ex.local
    h0, tgt_p, got = _prep(x, tgt, *ex.gather_srcs(G_FIRST))
    first = ex.gathered(G_FIRST, got)
    meta, w_in = first["meta_tokens"], first["w_in"]
    h0 = h0.at[:, PAD_ROWS:PAD_ROWS + N_META].set(jnp.broadcast_to(meta[None], (nb, N_META, D_MODEL))).reshape(n, D_MODEL)
    tgt_p = tgt_p.reshape(n, D_MODEL)

    zr = lambda r: jnp.zeros((r, D_MODEL), w_in.dtype)
    w_in_p = jnp.concatenate([w_in[:OFF_CKV], w_in[OFF_KR:], zr(QK_NOPE), w_in[OFF_CKV:OFF_KR], zr(HEAD_PAD - QK_HEAD)],
                             axis=0)
    pad_g = lambda g: jnp.pad(g, ((0, 0), (0, HEAD_PAD - QK_HEAD)))
    qg, kg = pad_g(local["q_norm_g"]), pad_g(local["k_norm_g"])
    rc, rs1, rs2 = _rope_tables(n)
    wblk = _block_diag_gates(local["lru_wa"].reshape(2, -1, RNN_BW, RNN_BW),
                             local["lru_wi"].reshape(2, -1, RNN_BW, RNN_BW)).astype(BF16)
    nblk = D_RNN // LANES

    (hn, cq, ckv, xr, xg, kr), got = _in_proj(h0, local["ln1_g"], w_in_p, *ex.gather_srcs(G_MID))
    w = ex.gathered(G_MID, got)
    w_uq_p = jnp.pad(w["w_uq"].reshape(N_HEADS, QK_HEAD, Q_LORA), ((0, 0), (0, HEAD_PAD - QK_HEAD), (0, 0))
                     ).reshape(QP_COLS, Q_LORA)
    ukv = w["w_ukv"].reshape(KV_LORA, N_HEADS, QK_NOPE + V_HEAD)
    w_uk_p = jnp.pad(ukv[:, :, :QK_NOPE], ((0, 0), (0, 0), (0, HEAD_PAD - QK_NOPE))).reshape(KV_LORA, QP_COLS)
    w_v = ukv[:, :, QK_NOPE:].reshape(KV_LORA, D_ATTN)
    gbias = jnp.stack([w["lru_ba"][0], w["lru_bi"][0], w["lru_ba"][1], w["lru_bi"][1]], axis=0)
    gbias = jnp.transpose(gbias.reshape(4, nblk, LANES), (1, 0, 2)).reshape(nblk, 1, 4 * LANES)

    q, k, v = _qkv_fwd(cq, ckv, kr, local["q_a_norm_g"], local["kv_a_norm_g"], w_uq_p, w_uk_p, w_v, qg, kg, rc, rs1, rs2)
    oa, probs, got = _attn_fwd(q, k, v, *ex.gather_srcs(LATE))
    late = ex.gathered(LATE, got)
    orn = _rnn_fwd(xr, xg, w["conv_w"], local["conv_b"], wblk, gbias, w["lru_lambda"])
    (doa, dor, dh1, mix_t, h1n, act_t, dgate_t, dup_t, dyb, loss, dga, dgr, dg2) = _post(
        oa, orn, h0, tgt_p, local["attn_out_g"], local["rnn_out_g"], local["ln2_g"], late["w_out"], late["w_gate"],
        late["w_up"], late["w_down"])
    wire = {"w_out": _matmul_shards("dw_out", mix_t, dh1), "w_gate": _matmul_shards("dw_gate", dgate_t, h1n),
            "w_up": _matmul_shards("dw_up", dup_t, h1n), "w_down": _matmul_shards("dw_down", act_t, dyb)}
    names = ("w_out", "w_gate")
    (dxr, dxg, dcw, dcb, dwblk, dgb, dlam), got = _rnn_bwd(xr, xg, dor, w["conv_w"], local["conv_b"], wblk, gbias,
                                                           w["lru_lambda"], *ex.scatter_srcs(names, wire))
    summed = ex.scattered(names, wire, got)
    dwa, dwi = _unblock_gates(dwblk)
    dgb = jnp.transpose(dgb.reshape(nblk, 4, LANES), (1, 0, 2)).reshape(4, D_RNN)
    names = ("w_up", "w_down")
    (dq_r, dk_r, dv), got = _attn_bwd(q, k, v, doa, oa, probs, *ex.scatter_srcs(names, wire))
    summed.update(ex.scattered(names, wire, got))
    wire = ex.to_wire({
        "conv_w": dcw, "conv_b": dcb, "lru_wa": dwa.reshape(-1, RNN_BW), "lru_ba": jnp.stack([dgb[0], dgb[2]]),
        "lru_wi": dwi.reshape(-1, RNN_BW), "lru_bi": jnp.stack([dgb[1], dgb[3]]), "lru_lambda": dlam,
        "attn_out_g": dga, "rnn_out_g": dgr, "ln2_g": dg2, "loss": loss})
    names = tuple(wire)
    (dp, qa, kva, dqp, dkv, dqg, dkg, dgqa, dgkva), got = _qkv_bwd(
        cq, ckv, kr, dq_r, dk_r, dv, dxr, dxg, local["q_a_norm_g"], local["kv_a_norm_g"], w_uq_p, w_uk_p, w_v, qg, kg,
        rc, rs1, rs2, *ex.scatter_srcs(names, wire))
    summed.update(ex.scattered(names, wire, got))
    dw_uq_p, _ = _matmul_tn("dw_uq", dqp, qa)
    dw_kv, _ = _matmul_tn("dw_ukv", kva, dkv)
    dw_uq = dw_uq_p.reshape(N_HEADS, HEAD_PAD, Q_LORA)[:, :QK_HEAD].reshape(N_HEADS * QK_HEAD, Q_LORA)
    dw_ukv = jnp.concatenate([dw_kv[:, :QP_COLS].reshape(KV_LORA, N_HEADS, HEAD_PAD)[:, :, :QK_NOPE],
                              dw_kv[:, QP_COLS:].reshape(KV_LORA, N_HEADS, V_HEAD)], axis=2).reshape(KV_LORA, -1)
    wire = ex.to_wire({"q_a_norm_g": dgqa, "w_uq": dw_uq, "kv_a_norm_g": dgkva, "w_ukv": dw_ukv,
                       "q_norm_g": dqg[:, :QK_HEAD], "k_norm_g": dkg[:, :QK_HEAD]})
    names = tuple(wire)
    dw_in_p, got = _matmul_tn("dw_in", dp, hn, *ex.scatter_srcs(names, wire))
    summed.update(ex.scattered(names, wire, got))
    kr0 = OFF_CKV + 2 * D_RNN + QK_NOPE
    dw_in = jnp.concatenate([dw_in_p[:OFF_CKV], dw_in_p[kr0:kr0 + QK_ROPE], dw_in_p[OFF_CKV:OFF_CKV + 2 * D_RNN]], axis=0)
    wire = ex.to_wire({"w_in": dw_in})
    (dh0, dg1), got = _in_bwd(dp, h0, dh1, local["ln1_g"], w_in_p, *ex.scatter_srcs(("w_in",), wire))
    summed.update(ex.scattered(("w_in",), wire, got))

    dh0 = dh0.reshape(nb, t, D_MODEL)
    wire = ex.to_wire({"meta_tokens": jnp.sum(dh0[:, PAD_ROWS:PAD_ROWS + N_META], axis=0), "ln1_g": dg1})
    got = ex.run("reduce_last", *ex.scatter_srcs(G_LAST, wire))
    summed.update(ex.scattered(G_LAST, wire, got))
    return dh0[:, PAD_ROWS + N_META:], summed


class _MeshExchange:
    def __init__(self, shards):
        self.local = shards

    @staticmethod
    def run(name, srcs, scatter):
        return _exchange(name, srcs, scatter)

    def gather_srcs(self, names):
        return [self.local[k].astype(BF16) if k in BIG else self.local[k] for k in names], [False] * len(names)

    @staticmethod
    def gathered(names, outs):
        return {k: g.reshape(-1, g.shape[-1]) if k in ROW_SHARDED else _cols_from_shards(g) for k, g in zip(names, outs)}

    @staticmethod
    def to_wire(grads):
        wire = {}
        for k, g in grads.items():
            if k in WHOLE:
                wire[k] = g
            elif k in ROW_SHARDED:
                wire[k] = g.reshape(N_DEV, -1, g.shape[-1]).astype(BF16)
            else:
                wire[k] = _cols_to_shards(g).astype(BF16) if k in BIG else _cols_to_shards(g)
        return wire

    @staticmethod
    def scatter_srcs(names, wire):
        return [wire[k] for k in names], [k not in WHOLE for k in names]

    @staticmethod
    def scattered(names, wire, outs):
        return dict(zip(names, outs))


def kernel(x, meta_tokens, ln1_g, w_in, q_a_norm_g, w_uq, kv_a_norm_g, w_ukv, q_norm_g, k_norm_g, conv_w, conv_b, lru_wa, lru_ba, lru_wi, lru_bi, lru_lambda, attn_out_g, rnn_out_g, w_out, ln2_g, w_gate, w_up, w_down, loss_target, m_meta_tokens, m_ln1_g, m_w_in, m_q_a_norm_g, m_w_uq, m_kv_a_norm_g, m_w_ukv, m_q_norm_g, m_k_norm_g, m_conv_w, m_conv_b, m_lru_wa, m_lru_ba, m_lru_wi, m_lru_bi, m_lru_lambda, m_attn_out_g, m_rnn_out_g, m_w_out, m_ln2_g, m_w_gate, m_w_up, m_w_down, v_meta_tokens, v_ln1_g, v_w_in, v_q_a_norm_g, v_w_uq, v_kv_a_norm_g, v_w_ukv, v_q_norm_g, v_k_norm_g, v_conv_w, v_conv_b, v_lru_wa, v_lru_ba, v_lru_wi, v_lru_bi, v_lru_lambda, v_attn_out_g, v_rnn_out_g, v_w_out, v_ln2_g, v_w_gate, v_w_up, v_w_down):
    given = (meta_tokens, ln1_g, w_in, q_a_norm_g, w_uq, kv_a_norm_g, w_ukv, q_norm_g, k_norm_g, conv_w, conv_b,
             lru_wa, lru_ba, lru_wi, lru_bi, lru_lambda, attn_out_g, rnn_out_g, w_out, ln2_g, w_gate, w_up, w_down)
    moments_m = (m_meta_tokens, m_ln1_g, m_w_in, m_q_a_norm_g, m_w_uq, m_kv_a_norm_g, m_w_ukv, m_q_norm_g, m_k_norm_g,
                 m_conv_w, m_conv_b, m_lru_wa, m_lru_ba, m_lru_wi, m_lru_bi, m_lru_lambda, m_attn_out_g, m_rnn_out_g,
                 m_w_out, m_ln2_g, m_w_gate, m_w_up, m_w_down)
    moments_v = (v_meta_tokens, v_ln1_g, v_w_in, v_q_a_norm_g, v_w_uq, v_kv_a_norm_g, v_w_ukv, v_q_norm_g, v_k_norm_g,
                 v_conv_w, v_conv_b, v_lru_wa, v_lru_ba, v_lru_wi, v_lru_bi, v_lru_lambda, v_attn_out_g, v_rnn_out_g,
                 v_w_out, v_ln2_g, v_w_gate, v_w_up, v_w_down)
    shapes = {k: a.shape for k, a in zip(WEIGHTS, given)}

    def two_d(k, a):
        a = a.reshape(-1, a.shape[-1])
        return a.T if k in TRANSPOSED else a

    w = {k: two_d(k, a) for k, a in zip(WEIGHTS, given)}
    m = {k: two_d(k, a) for k, a in zip(WEIGHTS, moments_m)}
    v = {k: two_d(k, a) for k, a in zip(WEIGHTS, moments_v)}

    grad_x, parts = _local_step(x, loss_target, _MeshExchange(w))

    tiled = ("w_in", "w_gate", "w_up", "w_down")
    new = {k: _adamw("adamw_" + k, parts[k], w[k], m[k], v[k]) for k in tiled}
    small = [k for k in WEIGHTS if k not in tiled]
    new.update(zip(small, _adamw_many("adamw_small", [(parts[k], w[k], m[k], v[k]) for k in small])))

    loss = jnp.sum(parts["loss"][:, 0, 0])
    outs = [loss, grad_x]
    for idx in range(4):
        outs += [(new[k][idx].T if k in TRANSPOSED else new[k][idx]).reshape(shapes[k]) for k in WEIGHTS]
    return tuple(outs)
```

```python
import functools
import math

import numpy as np
import jax
import jax.numpy as jnp
from jax import lax
from jax.experimental import pallas as pl
from jax.experimental.pallas import tpu as pltpu

F32 = jnp.float32
BF16 = jnp.bfloat16

D_MODEL = 1024
N_META = 16
SEQ = 2048
N_HEADS = 8
QK_NOPE = 64
QK_ROPE = 32
QK_HEAD = QK_NOPE + QK_ROPE
V_HEAD = 64
D_ATTN = N_HEADS * V_HEAD
Q_LORA = 384
KV_LORA = 256
D_RNN = 512
RNN_BW = 64
D_FF = 2816
EPS = 1e-6
LRU_C = 8.0
ROPE_THETA = 10000.0
OFF_CKV = Q_LORA + KV_LORA
OFF_KR = OFF_CKV + QK_ROPE
IN_COLS = OFF_KR + 2 * D_RNN

ADAM_LR = 0.001
ADAM_B1 = 0.9
ADAM_B2 = 0.999
ADAM_EPS = 1e-08
ADAM_WD = 0.01
ADAM_STEP = 10

N_DEV = 8
LANES = 128
HEAD_PAD = LANES
PAD_ROWS = LANES - N_META
QP_COLS = N_HEADS * HEAD_PAD
P_COLS = OFF_CKV + 2 * D_RNN + LANES
FF_CHUNK = D_FF
VMEM_LIMIT = 56 * 1024 * 1024
MESH = pl.DeviceIdType.MESH


def _t_pad():
    return PAD_ROWS + N_META + SEQ


def _row_tile(n):
    return 256 if n % 256 == 0 else 128


def _wide_row_tile(n):
    quarter = _t_pad() // 4
    return quarter if quarter % 16 == 0 and n % quarter == 0 else _row_tile(n)


def _const_spec(shape):
    nd = len(shape)
    return pl.BlockSpec(shape, lambda *_: (0,) * nd, pipeline_mode=pl.Buffered(1))


def _rms(x, d):
    r = lax.rsqrt(jnp.sum(x * x, axis=-1, keepdims=True) * (1.0 / d) + EPS)
    return x * r, r


def _rms_bwd(dy, xhat, r, g, d):
    dxh = dy * g
    return r * (dxh - xhat * (jnp.sum(dxh * xhat, axis=-1, keepdims=True) * (1.0 / d)))


def _colsum(x):
    return jnp.sum(x, axis=0, keepdims=True)


def _dot(a, b):
    return jnp.dot(a, b, preferred_element_type=F32)


def _dot_nt(a, b):
    return lax.dot_general(a, b, (((1,), (1,)), ((), ())), preferred_element_type=F32)


def _dot_tn(a, b):
    return lax.dot_general(a, b, (((0,), (0,)), ((), ())), preferred_element_type=F32)


def _rope(x, c, s1, s2):
    return x * c + pltpu.roll(x, 16, 1) * s1 + pltpu.roll(x, HEAD_PAD - 16, 1) * s2


def _rope_bwd(dy, c, s1, s2):
    return dy * c + pltpu.roll(dy * s1, HEAD_PAD - 16, 1) + pltpu.roll(dy * s2, 16, 1)


def _acc(ref, first, val):
    @pl.when(first)
    def _():
        ref[...] = val

    @pl.when(jnp.logical_not(first))
    def _():
        ref[...] += val


def _in_proj(h0, ln1_g, w_in_p, srcs=(), scatter=()):
    n = h0.shape[0]
    tm = _wide_row_tile(n)
    nk = len(srcs)
    c_in, c_out, c_shape, c_sems = _exchange_specs(srcs, scatter)

    def body(h_ref, g_ref, w_ref, *rest):
        hn_ref, cq_ref, ckv_ref, xr_ref, xg_ref, kr_ref = rest[nk:nk + 6]
        finish = _ride(1, *_exchange_fns(rest[:nk], rest[nk + 6:2 * nk + 6], rest[2 * nk + 6:], scatter))
        xhat, _ = _rms(h_ref[...], D_MODEL)
        hn = (xhat * g_ref[...]).astype(BF16)
        hn_ref[...] = hn
        p = _dot_nt(hn, w_ref[...])
        cq_ref[...] = p[:, :Q_LORA]
        ckv_ref[...] = p[:, Q_LORA:OFF_CKV]
        xr_ref[...] = p[:, OFF_CKV:OFF_CKV + D_RNN]
        xg_ref[...] = p[:, OFF_CKV + D_RNN:OFF_CKV + 2 * D_RNN]
        kr_ref[...] = p[:, OFF_CKV + 2 * D_RNN:]
        finish()

    def row(w):
        return pl.BlockSpec((tm, w), lambda i: (i, 0))

    widths = (D_MODEL, Q_LORA, KV_LORA, D_RNN, D_RNN, LANES)
    res = pl.pallas_call(
        body, name="in_proj", grid=(n // tm,),
        in_specs=[row(D_MODEL), _const_spec((1, D_MODEL)), _const_spec((P_COLS, D_MODEL))] + c_in,
        out_specs=[row(w) for w in widths] + c_out,
        out_shape=[jax.ShapeDtypeStruct((n, w), BF16 if k == 0 else F32) for k, w in enumerate(widths)] + c_shape,
        scratch_shapes=c_sems,
        compiler_params=pltpu.CompilerParams(dimension_semantics=("arbitrary",), vmem_limit_bytes=VMEM_LIMIT),
    )(h0, ln1_g, w_in_p, *srcs)
    return res[:6], res[6:]


def _qkv_fwd(cq, ckv, kr, gqa, gkva, w_uq_p, w_uk_p, w_v, qg, kg, rc, rs1, rs2):
    n = cq.shape[0]
    tm = _wide_row_tile(n)

    def body(cq_ref, ckv_ref, kr_ref, gqa_ref, gkva_ref, wuq_ref, wuk_ref, wv_ref, qg_ref, kg_ref,
             c_ref, s1_ref, s2_ref, q_ref, k_ref, v_ref):
        xq, _ = _rms(cq_ref[...], Q_LORA)
        qa = (xq * gqa_ref[...]).astype(BF16)
        q = _dot_nt(qa, wuq_ref[...])
        xkv, _ = _rms(ckv_ref[...], KV_LORA)
        kva = (xkv * gkva_ref[...]).astype(BF16)
        kn = _dot(kva, wuk_ref[...])
        v_ref[...] = _dot(kva, wv_ref[...]).astype(BF16)
        krp = kr_ref[...]
        c, s1, s2 = c_ref[...], s1_ref[...], s2_ref[...]
        for h in range(N_HEADS):
            sl = slice(h * HEAD_PAD, (h + 1) * HEAD_PAD)
            qh, _ = _rms(q[:, sl], QK_HEAD)
            q_ref[:, sl] = _rope(qh * qg_ref[...], c, s1, s2).astype(BF16)
            kh, _ = _rms(kn[:, sl] + krp, QK_HEAD)
            k_ref[:, sl] = _rope(kh * kg_ref[...], c, s1, s2).astype(BF16)

    def row(w):
        return pl.BlockSpec((tm, w), lambda i: (i, 0))

    return pl.pallas_call(
        body, name="qkv_fwd", grid=(n // tm,),
        in_specs=[row(Q_LORA), row(KV_LORA), row(LANES), _const_spec((1, Q_LORA)), _const_spec((1, KV_LORA)),
                  _const_spec((QP_COLS, Q_LORA)), _const_spec((KV_LORA, QP_COLS)), _const_spec((KV_LORA, D_ATTN)),
                  _const_spec((1, LANES)), _const_spec((1, LANES)), row(LANES), row(LANES), row(LANES)],
        out_specs=[row(QP_COLS), row(QP_COLS), row(D_ATTN)],
        out_shape=[jax.ShapeDtypeStruct((n, QP_COLS), BF16), jax.ShapeDtypeStruct((n, QP_COLS), BF16),
                   jax.ShapeDtypeStruct((n, D_ATTN), BF16)],
        compiler_params=pltpu.CompilerParams(dimension_semantics=("parallel",), vmem_limit_bytes=VMEM_LIMIT),
    )(cq, ckv, kr, gqa, gkva, w_uq_p, w_uk_p, w_v, qg, kg, rc, rs1, rs2)


def _qkv_bwd(cq, ckv, kr, dq_r, dk_r, dv, dxr, dxg, gqa, gkva, w_uq_p, w_uk_p, w_v, qg, kg, rc, rs1, rs2,
             srcs=(), scatter=()):
    n = cq.shape[0]
    tm = _wide_row_tile(n)
    nk = len(srcs)
    c_in, c_out, c_shape, c_sems = _exchange_specs(srcs, scatter)

    def body(cq_ref, ckv_ref, kr_ref, dq_ref, dk_ref, dv_ref, dxr_ref, dxg_ref, gqa_ref, gkva_ref, wuq_ref, wuk_ref,
             wv_ref, qg_ref, kg_ref, c_ref, s1_ref, s2_ref, *rest):
        dp_ref, qa_ref, kva_ref, dqp_ref, dkv_ref, dqg_ref, dkg_ref, dgqa_ref, dgkva_ref = rest[nk:nk + 9]
        finish = _ride(1, *_exchange_fns(rest[:nk], rest[nk + 9:2 * nk + 9], rest[2 * nk + 9:], scatter))
        first = pl.program_id(0) == 0
        dp_ref[:, OFF_CKV:OFF_CKV + D_RNN] = dxr_ref[...].astype(BF16)
        dp_ref[:, OFF_CKV + D_RNN:OFF_CKV + 2 * D_RNN] = dxg_ref[...].astype(BF16)
        xq, rq = _rms(cq_ref[...], Q_LORA)
        qa = (xq * gqa_ref[...]).astype(BF16)
        qa_ref[...] = qa
        q = _dot_nt(qa, wuq_ref[...])
        xkv, rkv = _rms(ckv_ref[...], KV_LORA)
        kva = (xkv * gkva_ref[...]).astype(BF16)
        kva_ref[...] = kva
        kn = _dot(kva, wuk_ref[...])
        krp = kr_ref[...]
        c, s1, s2 = c_ref[...], s1_ref[...], s2_ref[...]
        lane = lax.broadcasted_iota(jnp.int32, (tm, HEAD_PAD), 1)
        rope_lanes = jnp.logical_and(lane >= QK_NOPE, lane < QK_HEAD)
        dqg = jnp.zeros((1, HEAD_PAD), F32)
        dkg = jnp.zeros((1, HEAD_PAD), F32)
        dkr = jnp.zeros((tm, HEAD_PAD), F32)
        for h in range(N_HEADS):
            sl = slice(h * HEAD_PAD, (h + 1) * HEAD_PAD)
            qh, rqh = _rms(q[:, sl], QK_HEAD)
            dy = _rope_bwd(dq_ref[:, sl], c, s1, s2)
            dqg = dqg + _colsum(dy * qh)
            dqp_ref[:, sl] = _rms_bwd(dy, qh, rqh, qg_ref[...], QK_HEAD).astype(BF16)
            kh, rkh = _rms(kn[:, sl] + krp, QK_HEAD)
            dyk = _rope_bwd(dk_ref[:, sl], c, s1, s2)
            dkg = dkg + _colsum(dyk * kh)
            dkh = _rms_bwd(dyk, kh, rkh, kg_ref[...], QK_HEAD)
            dkv_ref[:, sl] = dkh.astype(BF16)
            dkr = dkr + jnp.where(rope_lanes, dkh, 0.0)
        dkv_ref[:, QP_COLS:] = dv_ref[...].astype(BF16)
        dp_ref[:, OFF_CKV + 2 * D_RNN:] = dkr.astype(BF16)
        dqa = _dot(dqp_ref[...], wuq_ref[...])
        dp_ref[:, :Q_LORA] = _rms_bwd(dqa, xq, rq, gqa_ref[...], Q_LORA).astype(BF16)
        dkva = _dot_nt(dkv_ref[:, :QP_COLS], wuk_ref[...]) + _dot_nt(dkv_ref[:, QP_COLS:], wv_ref[...])
        dp_ref[:, Q_LORA:OFF_CKV] = _rms_bwd(dkva, xkv, rkv, gkva_ref[...], KV_LORA).astype(BF16)
        _acc(dqg_ref, first, dqg)
        _acc(dkg_ref, first, dkg)
        _acc(dgqa_ref, first, _colsum(dqa * xq))
        _acc(dgkva_ref, first, _colsum(dkva * xkv))
        finish()

    def row(w):
        return pl.BlockSpec((tm, w), lambda i: (i, 0))

    def acc(w):
        return pl.BlockSpec((1, w), lambda i: (0, 0))

    res = pl.pallas_call(
        body, name="qkv_bwd", grid=(n // tm,),
        in_specs=[row(Q_LORA), row(KV_LORA), row(LANES), row(QP_COLS), row(QP_COLS), row(D_ATTN), row(D_RNN), row(D_RNN),
                  _const_spec((1, Q_LORA)), _const_spec((1, KV_LORA)),
                  _const_spec((QP_COLS, Q_LORA)), _const_spec((KV_LORA, QP_COLS)), _const_spec((KV_LORA, D_ATTN)),
                  _const_spec((1, LANES)), _const_spec((1, LANES)), row(LANES), row(LANES), row(LANES)] + c_in,
        out_specs=[row(P_COLS), row(Q_LORA), row(KV_LORA), row(QP_COLS),
                   row(QP_COLS + D_ATTN), acc(LANES), acc(LANES), acc(Q_LORA), acc(KV_LORA)] + c_out,
        out_shape=[jax.ShapeDtypeStruct((n, P_COLS), BF16), jax.ShapeDtypeStruct((n, Q_LORA), BF16),
                   jax.ShapeDtypeStruct((n, KV_LORA), BF16), jax.ShapeDtypeStruct((n, QP_COLS), BF16),
                   jax.ShapeDtypeStruct((n, QP_COLS + D_ATTN), BF16),
                   jax.ShapeDtypeStruct((1, LANES), F32), jax.ShapeDtypeStruct((1, LANES), F32),
                   jax.ShapeDtypeStruct((1, Q_LORA), F32), jax.ShapeDtypeStruct((1, KV_LORA), F32)] + c_shape,
        scratch_shapes=c_sems,
        compiler_params=pltpu.CompilerParams(dimension_semantics=("arbitrary",), vmem_limit_bytes=VMEM_LIMIT),
    )(cq, ckv, kr, dq_r, dk_r, dv, dxr, dxg, gqa, gkva, w_uq_p, w_uk_p, w_v, qg, kg, rc, rs1, rs2, *srcs)
    return res[:9], res[9:]


KEY_CHUNK = 4 * LANES


def _key_chunks(t):
    count = max(t // KEY_CHUNK, 1)
    first = t - KEY_CHUNK * (count - 1)
    return [(0, first)] + [(first + KEY_CHUNK * c, KEY_CHUNK) for c in range(count - 1)]


def _attn_specs(t, tq):
    nq = t // tq
    qspec = pl.BlockSpec((tq, 2 * HEAD_PAD), lambda b, hp, i: (b * nq + i, hp))
    kspec = pl.BlockSpec((t, 2 * HEAD_PAD), lambda b, hp, i: (b, hp))
    vspec = pl.BlockSpec((t, 2 * V_HEAD), lambda b, hp, i: (b, hp))
    ospec = pl.BlockSpec((tq, 2 * V_HEAD), lambda b, hp, i: (b * nq + i, hp))
    return nq, qspec, kspec, vspec, ospec


def _probs_spec(t, tq):
    return pl.BlockSpec((1, 2, tq, t), lambda b, hp, i: (b, hp, i, 0))


def _attn_fwd(q, k, v, srcs=(), scatter=()):
    n = q.shape[0]
    t = _t_pad()
    tq = t // 2
    nq, qspec, kspec, vspec, ospec = _attn_specs(t, tq)
    nk = len(srcs)
    c_in, c_out, c_shape, c_sems = _exchange_specs(srcs, scatter)

    def body(q_ref, k_ref, v_ref, *rest):
        o_ref, l_ref, p_ref = rest[nk:nk + 3]
        finish = _ride(3, *_exchange_fns(rest[:nk], rest[nk + 3:2 * nk + 3], rest[2 * nk + 3:], scatter))
        lane = lax.broadcasted_iota(jnp.int32, (tq, 2 * V_HEAD), 1)
        outs = []
        sums = []
        for j in range(2):
            sl = slice(j * HEAD_PAD, (j + 1) * HEAD_PAD)
            qh = q_ref[:, sl]

            def scores(start, size):
                s = _dot_nt(qh, k_ref[start:start + size, sl])
                if start < PAD_ROWS:
                    key = lax.broadcasted_iota(jnp.int32, (tq, size), 1) + start
                    s = jnp.where(key >= PAD_ROWS, s, -jnp.inf)
                return s

            top = functools.reduce(jnp.maximum, [jnp.max(scores(*c), axis=-1, keepdims=True) for c in _key_chunks(t)])
            l = jnp.zeros((tq, 1), F32)
            pv = jnp.zeros((tq, 2 * V_HEAD), F32)
            for start, size in _key_chunks(t):
                e = jnp.exp((scores(start, size) - top) * (QK_HEAD ** -0.5))
                l = l + jnp.sum(e, axis=-1, keepdims=True)
                e = e.astype(BF16)
                p_ref[0, j, :, start:start + size] = e
                pv = pv + _dot(e, v_ref[start:start + size, :])
            outs.append(pv / l)
            sums.append(l)
        o_ref[...] = jnp.where(lane < V_HEAD, outs[0], outs[1])
        l_ref[...] = jnp.where(lane < V_HEAD, sums[0], sums[1])
        finish()

    res = pl.pallas_call(
        body, name="attn_fwd", grid=(n // t, N_HEADS // 2, nq),
        in_specs=[qspec, kspec, vspec] + c_in, out_specs=[ospec, ospec, _probs_spec(t, tq)] + c_out,
        out_shape=[jax.ShapeDtypeStruct((n, D_ATTN), F32), jax.ShapeDtypeStruct((n, D_ATTN), F32),
                   jax.ShapeDtypeStruct((n // t, N_HEADS, t, t), BF16)] + c_shape,
        scratch_shapes=c_sems,
        compiler_params=pltpu.CompilerParams(dimension_semantics=("arbitrary", "arbitrary", "arbitrary"),
                                             vmem_limit_bytes=VMEM_LIMIT),
    )(q, k, v, *srcs)
    return res[0], (res[1], res[2]), res[3:]


def _attn_bwd(q, k, v, do, o, probs, srcs=(), scatter=()):
    n = q.shape[0]
    t = _t_pad()
    tq = t // 2
    nq, qspec, kspec, vspec, ospec = _attn_specs(t, tq)
    nk = len(srcs)
    c_in, c_out, c_shape, c_sems = _exchange_specs(srcs, scatter)

    def body(q_ref, k_ref, v_ref, do_ref, o_ref, l_ref, p_ref, *rest):
        dq_ref, dk_ref, dv_ref = rest[nk:nk + 3]
        finish = _ride(3, *_exchange_fns(rest[:nk], rest[nk + 3:2 * nk + 3], rest[2 * nk + 3:], scatter))

        @pl.when(pl.program_id(2) == 0)
        def _():
            dk_ref[...] = jnp.zeros_like(dk_ref)
            dv_ref[...] = jnp.zeros_like(dv_ref)

        lane = lax.broadcasted_iota(jnp.int32, (tq, 2 * V_HEAD), 1)
        do = do_ref[...]
        do_o = do * o_ref[...]
        chunks = _key_chunks(t)
        dvs = [None] * len(chunks)
        for j in range(2):
            sl = slice(j * HEAD_PAD, (j + 1) * HEAD_PAD)
            qh = q_ref[:, sl]
            in_head = (lane < V_HEAD) if j == 0 else (lane >= V_HEAD)
            inv_l = 1.0 / l_ref[:, j * V_HEAD:j * V_HEAD + 1]
            doh = jnp.where(in_head, do, 0.0).astype(BF16)
            doh_n = jnp.where(in_head, do * inv_l, 0.0).astype(BF16)
            delta = jnp.sum(jnp.where(in_head, do_o, 0.0), axis=-1, keepdims=True)
            row_scale = inv_l * (QK_HEAD ** -0.5)
            dq = jnp.zeros((tq, HEAD_PAD), F32)
            for c, (start, size) in enumerate(chunks):
                rows = slice(start, start + size)
                e = p_ref[0, j, :, rows]
                dp = _dot_nt(doh, v_ref[rows, :])
                ds = (e.astype(F32) * (dp - delta) * row_scale).astype(BF16)
                dq = dq + _dot(ds, k_ref[rows, sl])
                dk_ref[rows, sl] += _dot_tn(ds, qh)
                dvc = _dot_tn(e, doh_n)
                dvs[c] = dvc if dvs[c] is None else dvs[c] + dvc
            dq_ref[:, sl] = dq
        for (start, size), dvc in zip(chunks, dvs):
            dv_ref[start:start + size, :] += dvc
        finish()

    res = pl.pallas_call(
        body, name="attn_bwd", grid=(n // t, N_HEADS // 2, nq),
        in_specs=[qspec, kspec, vspec, ospec, ospec, ospec, _probs_spec(t, tq)] + c_in,
        out_specs=[qspec, kspec, vspec] + c_out,
        out_shape=[jax.ShapeDtypeStruct((n, QP_COLS), F32), jax.ShapeDtypeStruct((n, QP_COLS), F32),
                   jax.ShapeDtypeStruct((n, D_ATTN), F32)] + c_shape, scratch_shapes=c_sems,
        compiler_params=pltpu.CompilerParams(dimension_semantics=("arbitrary", "arbitrary", "arbitrary"),
                                             vmem_limit_bytes=VMEM_LIMIT),
    )(q, k, v, do, o, *probs, *srcs)
    return res[:3], res[3:]


SCAN_STEPS = 8


def _scan(chains, t):
    seg = t // 8
    rows = lax.broadcasted_iota(jnp.int32, (8, LANES), 0)

    def step(i, carry):
        carry = list(carry)
        for u in range(SCAN_STEPS):
            j = i * SCAN_STEPS + u
            for n, (a_ref, b_ref, h_ref, p_ref, reverse) in enumerate(chains):
                h, p = carry[n]
                idx = pl.ds(seg - 1 - j if reverse else j, 8, stride=seg)
                a = a_ref[idx, :]
                h = a * h + b_ref[idx, :]
                p = a * p
                h_ref[idx, :] = h
                p_ref[idx, :] = p
                carry[n] = (h, p)
        return tuple(carry)

    init = tuple((jnp.zeros((8, LANES), F32), jnp.ones((8, LANES), F32)) for _ in chains)
    ends = lax.fori_loop(0, seg // SCAN_STEPS, step, init)
    for (_, _, h_ref, p_ref, reverse), (b, a) in zip(chains, ends):
        for d in (1, 2, 4):
            if reverse:
                keep = rows < 8 - d
                a_n, b_n = pltpu.roll(a, 8 - d, 0), pltpu.roll(b, 8 - d, 0)
            else:
                keep = rows >= d
                a_n, b_n = pltpu.roll(a, d, 0), pltpu.roll(b, d, 0)
            b = a * jnp.where(keep, b_n, 0.0) + b
            a = a * jnp.where(keep, a_n, 1.0)
        for s in (range(7) if reverse else range(1, 8)):
            sl = slice(s * seg, (s + 1) * seg)
            carry_in = b[s + 1:s + 2, :] if reverse else b[s - 1:s, :]
            h_ref[sl, :] = h_ref[sl, :] + p_ref[sl, :] * carry_in


def _shift_rows(x, s, rows, t):
    if s == 0:
        return x
    rolled = pltpu.roll(x, s % t, 0)
    return jnp.where(rows >= s, rolled, 0.0) if s > 0 else jnp.where(rows < t + s, rolled, 0.0)


def _neg_expm1(x, exp_x):
    series = -x * (1.0 + x * (0.5 + x * (1.0 / 6 + x * (1.0 / 24))))
    return jnp.where(x > -0.1, series, 1.0 - exp_x)


def _sigmoid(x):
    return 0.5 * jnp.tanh(0.5 * x) + 0.5


def _gelu_parts(x):
    k = math.sqrt(2.0 / math.pi)
    th = jnp.tanh(k * (x + 0.044715 * x * x * x))
    g = 0.5 * x * (1.0 + th)
    dg = 0.5 * (1.0 + th) + 0.5 * x * (1.0 - th * th) * k * (1.0 + 3 * 0.044715 * x * x)
    return g, dg


def _lru_gates(xc, gates, lam_ref, valid, d):
    r = _sigmoid(gates[:, (2 * d) * LANES:(2 * d + 1) * LANES])
    i = _sigmoid(gates[:, (2 * d + 1) * LANES:(2 * d + 2) * LANES])
    neg_lam = -lam_ref[d:d + 1, :]
    sp = jnp.maximum(neg_lam, 0.0) + jnp.log1p(jnp.exp(-jnp.abs(neg_lam)))
    log_a = -LRU_C * r * sp
    a = jnp.exp(log_a)
    m = jnp.maximum(_neg_expm1(2.0 * log_a, a * a), 0.0)
    sq = jnp.sqrt(m)
    b = jnp.where(valid, sq * (i * xc), 0.0)
    return r, i, sp, a, m, sq, b


def _conv(xr, cw_ref, cb_ref, rows, t):
    return (cw_ref[0:1, :] * _shift_rows(xr, 2, rows, t) + cw_ref[1:2, :] * _shift_rows(xr, 1, rows, t)
            + cw_ref[2:3, :] * xr + cw_ref[3:4, :] * _shift_rows(xr, -1, rows, t) + cb_ref[...])


def _rnn_specs(t):
    seq = pl.BlockSpec((t, LANES), lambda cb, b: (b, cb))
    cw = pl.BlockSpec((4, LANES), lambda cb, b: (0, cb))
    vec1 = pl.BlockSpec((1, LANES), lambda cb, b: (0, cb))
    vec2 = pl.BlockSpec((2, LANES), lambda cb, b: (0, cb))
    wblk = pl.BlockSpec((1, LANES, 4 * LANES), lambda cb, b: (cb, 0, 0))
    gbias = pl.BlockSpec((1, 1, 4 * LANES), lambda cb, b: (cb, 0, 0))
    return seq, cw, vec1, vec2, wblk, gbias


def _rnn_fwd(xr, xg, conv_w, conv_b, wblk, gbias, lam):
    n = xr.shape[0]
    t = _t_pad()
    seq, cw, vec1, vec2, wspec, gspec = _rnn_specs(t)

    def body(xr_ref, xg_ref, cw_ref, cb_ref, w_ref, gb_ref, lam_ref, o_ref, a_s, b_s, h_s, p_s):
        rows = lax.broadcasted_iota(jnp.int32, (t, LANES), 0)
        valid = rows >= PAD_ROWS
        xc = _conv(xr_ref[...], cw_ref, cb_ref, rows, t)
        gates = _dot(xc.astype(BF16), w_ref[0]) + gb_ref[0]
        for d in range(2):
            _, _, _, a, _, _, b = _lru_gates(xc, gates, lam_ref, valid, d)
            a_s[d] = a
            b_s[d] = b
        _scan([(a_s.at[d], b_s.at[d], h_s.at[d], p_s.at[d], d == 1) for d in range(2)], t)
        g, _ = _gelu_parts(xg_ref[...])
        o_ref[...] = (h_s[0] + h_s[1]) * g

    return pl.pallas_call(
        body, name="rnn_fwd", grid=(D_RNN // LANES, n // t),
        in_specs=[seq, seq, cw, vec1, wspec, gspec, vec2], out_specs=seq,
        out_shape=jax.ShapeDtypeStruct((n, D_RNN), F32),
        scratch_shapes=[pltpu.VMEM((2, t, LANES), F32)] * 4,
        compiler_params=pltpu.CompilerParams(dimension_semantics=("parallel", "parallel"), vmem_limit_bytes=VMEM_LIMIT),
    )(xr, xg, conv_w, conv_b, wblk, gbias, lam)


def _rnn_bwd(xr, xg, do, conv_w, conv_b, wblk, gbias, lam, srcs=(), scatter=()):
    n = xr.shape[0]
    t = _t_pad()
    seq, cw, vec1, vec2, wspec, gspec = _rnn_specs(t)
    nk = len(srcs)
    c_in, c_out, c_shape, c_sems = _exchange_specs(srcs, scatter)

    def body(xr_ref, xg_ref, do_ref, cw_ref, cb_ref, w_ref, gb_ref, lam_ref, *rest):
        dxr_ref, dxg_ref, dcw_ref, dcb_ref, dw_ref, dgb_ref, dlam_ref = rest[nk:nk + 7]
        a_s, b_s, h_s, l_s, p_s, back_s, r_s, i_s, q_s, dg_s = rest[2 * nk + 7 + len(c_sems):]
        finish = _ride(2, *_exchange_fns(rest[:nk], rest[nk + 7:2 * nk + 7], rest[2 * nk + 7:2 * nk + 7 + len(c_sems)],
                                         scatter))
        first = pl.program_id(1) == 0
        rows = lax.broadcasted_iota(jnp.int32, (t, LANES), 0)
        valid = rows >= PAD_ROWS
        xr = xr_ref[...]
        xc = _conv(xr, cw_ref, cb_ref, rows, t)
        xcb = xc.astype(BF16)
        gates = _dot(xcb, w_ref[0]) + gb_ref[0]
        sps = []
        for d in range(2):
            r_s[d], i_s[d], sp, a_s[d], _, q_s[d], b_s[d] = _lru_gates(xc, gates, lam_ref, valid, d)
            sps.append(sp)
        _scan([(a_s.at[d], b_s.at[d], h_s.at[d], p_s.at[d], d == 1) for d in range(2)], t)
        g, dg = _gelu_parts(xg_ref[...])
        do = do_ref[...]
        dxg_ref[...] = do * (h_s[0] + h_s[1]) * dg
        b_s[0] = do * g
        for d in range(2):
            back_s[d] = _shift_rows(a_s[d], -1 if d == 0 else 1, rows, t)
        _scan([(back_s.at[d], b_s.at[0], l_s.at[d], p_s.at[d], d == 0) for d in range(2)], t)
        dxc = jnp.zeros((t, LANES), F32)
        dlams = []
        for d in range(2):
            r, i, sp, a, sq = r_s[d], i_s[d], sps[d], a_s[d], q_s[d]
            lam_t = l_s[d]
            da = lam_t * _shift_rows(h_s[d], 1 if d == 0 else -1, rows, t)
            lam_v = jnp.where(valid, lam_t, 0.0)
            dsq = lam_v * (i * xc)
            di = lam_v * sq * xc
            dxc = dxc + lam_v * sq * i
            dm = jnp.where(sq > 0.0, dsq * 0.5 / jnp.where(sq > 0.0, sq, 1.0), 0.0)
            dla = da * a - 2.0 * dm * a * a
            dr = dla * (-LRU_C) * sp
            dsp = _colsum(dla * (-LRU_C) * r)
            dlams.append(dsp * -jax.nn.sigmoid(-lam_ref[d:d + 1, :]))
            dg_s[:, (2 * d) * LANES:(2 * d + 1) * LANES] = (dr * r * (1.0 - r)).astype(BF16)
            dg_s[:, (2 * d + 1) * LANES:(2 * d + 2) * LANES] = (di * i * (1.0 - i)).astype(BF16)
        dgates = dg_s[...]
        dxc = dxc + _dot_nt(dgates, w_ref[0])
        taps = [_shift_rows(dxc, j - 2, rows, t) for j in range(4)]
        dxr_ref[...] = (cw_ref[0:1, :] * taps[0] + cw_ref[1:2, :] * taps[1] + cw_ref[2:3, :] * taps[2]
                        + cw_ref[3:4, :] * taps[3])
        dcw = jnp.concatenate([_colsum(tap * xr) for tap in taps], axis=0)
        _acc(dcw_ref, first, dcw)
        _acc(dcb_ref, first, _colsum(dxc))
        _acc(dw_ref, first, _dot_tn(xcb, dgates)[None])
        _acc(dgb_ref, first, _colsum(dgates.astype(F32))[None])
        _acc(dlam_ref, first, jnp.concatenate(dlams, axis=0))
        finish()

    res = pl.pallas_call(
        body, name="rnn_bwd", grid=(D_RNN // LANES, n // t),
        in_specs=[seq, seq, seq, cw, vec1, wspec, gspec, vec2] + c_in,
        out_specs=[seq, seq, cw, vec1, wspec, gspec, vec2] + c_out,
        out_shape=[jax.ShapeDtypeStruct((n, D_RNN), F32), jax.ShapeDtypeStruct((n, D_RNN), F32),
                   jax.ShapeDtypeStruct((4, D_RNN), F32), jax.ShapeDtypeStruct((1, D_RNN), F32),
                   jax.ShapeDtypeStruct((D_RNN // LANES, LANES, 4 * LANES), F32),
                   jax.ShapeDtypeStruct((D_RNN // LANES, 1, 4 * LANES), F32), jax.ShapeDtypeStruct((2, D_RNN), F32)]
        + c_shape,
        scratch_shapes=c_sems + [pltpu.VMEM((2, t, LANES), F32)] * 9 + [pltpu.VMEM((t, 4 * LANES), BF16)],
        compiler_params=pltpu.CompilerParams(dimension_semantics=("arbitrary", "arbitrary"), vmem_limit_bytes=VMEM_LIMIT),
    )(xr, xg, do, conv_w, conv_b, wblk, gbias, lam, *srcs)
    return res[:7], res[7:]


def _post(oa, orn, h0, tgt, ga, gr, g2, w_out, w_gate, w_up, w_down):
    n = oa.shape[0]
    tm = _row_tile(n)
    t = _t_pad()

    def body(oa_ref, or_ref, h0_ref, tgt_ref, ga_ref, gr_ref, g2_ref, wo_ref, wg_ref, wu_ref, wd_ref,
             doa_ref, dor_ref, dh1_ref, mix_ref, h1n_ref, act_ref, dgate_ref, dup_ref, dy_ref,
             loss_ref, dga_ref, dgr_ref, dg2_ref, gate_s, up_s):
        first = pl.program_id(0) == 0
        xa, ra = _rms(oa_ref[...], D_ATTN)
        xr, rr = _rms(or_ref[...], D_RNN)
        mix = jnp.concatenate([(xa * ga_ref[...]).astype(BF16), (xr * gr_ref[...]).astype(BF16)], axis=-1)
        mix_ref[...] = mix.T
        h1 = h0_ref[...] + _dot(mix, wo_ref[...])
        x2, r2 = _rms(h1, D_MODEL)
        h1n = (x2 * g2_ref[...]).astype(BF16)
        h1n_ref[...] = h1n
        y = h1
        for cs in range(0, D_FF, FF_CHUNK):
            sl = slice(cs, cs + FF_CHUNK)
            gate = _dot_nt(h1n, wg_ref[sl, :])
            up = _dot_nt(h1n, wu_ref[sl, :])
            gate_s[:, sl] = gate
            up_s[:, sl] = up
            act = (gate * _sigmoid(gate) * up).astype(BF16)
            act_ref[sl, :] = act.T
            y = y + _dot(act, wd_ref[sl, :])
        row = pl.program_id(0) * tm + lax.broadcasted_iota(jnp.int32, (tm, 1), 0)
        for _ in range(1, n // t):
            row = jnp.where(row >= t, row - t, row)
        err = jnp.where(row >= PAD_ROWS + N_META, y - tgt_ref[...], 0.0)
        _acc(loss_ref, first, jnp.full((1, LANES), 0.5 / D_MODEL, F32) * jnp.sum(err * err))
        dy = err * (1.0 / D_MODEL)
        dyb = dy.astype(BF16)
        dy_ref[...] = dyb
        dh1n = jnp.zeros((tm, D_MODEL), F32)
        for cs in range(0, D_FF, FF_CHUNK):
            sl = slice(cs, cs + FF_CHUNK)
            dact = _dot_nt(dyb, wd_ref[sl, :])
            gate, up = gate_s[:, sl], up_s[:, sl]
            sg = _sigmoid(gate)
            dgate = (dact * up * sg * (1.0 + gate * (1.0 - sg))).astype(BF16)
            dup = (dact * gate * sg).astype(BF16)
            dgate_ref[sl, :] = dgate.T
            dup_ref[sl, :] = dup.T
            dh1n = dh1n + _dot(dgate, wg_ref[sl, :]) + _dot(dup, wu_ref[sl, :])
        _acc(dg2_ref, first, _colsum(dh1n * x2))
        dh1 = dy + _rms_bwd(dh1n, x2, r2, g2_ref[...], D_MODEL)
        dh1_ref[...] = dh1
        dmix = _dot_nt(dh1.astype(BF16), wo_ref[...])
        dma, dmr = dmix[:, :D_ATTN], dmix[:, D_ATTN:]
        _acc(dga_ref, first, _colsum(dma * xa))
        _acc(dgr_ref, first, _colsum(dmr * xr))
        doa_ref[...] = _rms_bwd(dma, xa, ra, ga_ref[...], D_ATTN)
        dor_ref[...] = _rms_bwd(dmr, xr, rr, gr_ref[...], D_RNN)

    def row(w):
        return pl.BlockSpec((tm, w), lambda i: (i, 0))

    def acc(w):
        return pl.BlockSpec((1, w), lambda i: (0, 0))

    def col(w):
        return pl.BlockSpec((w, tm), lambda i: (0, i))

    outs = [(D_ATTN, F32, row), (D_RNN, F32, row), (D_MODEL, F32, row), (D_MODEL, BF16, col), (D_MODEL, BF16, row),
            (D_FF, BF16, col), (D_FF, BF16, col), (D_FF, BF16, col), (D_MODEL, BF16, row)]
    accs = [LANES, D_ATTN, D_RNN, D_MODEL]
    return pl.pallas_call(
        body, name="post", grid=(n // tm,),
        in_specs=[row(D_ATTN), row(D_RNN), row(D_MODEL), row(D_MODEL),
                  _const_spec((1, D_ATTN)), _const_spec((1, D_RNN)), _const_spec((1, D_MODEL)),
                  _const_spec((D_MODEL, D_MODEL)), _const_spec((D_FF, D_MODEL)), _const_spec((D_FF, D_MODEL)),
                  _const_spec((D_FF, D_MODEL))],
        out_specs=[spec(w) for w, _, spec in outs] + [acc(w) for w in accs],
        out_shape=[jax.ShapeDtypeStruct((n, w) if spec is row else (w, n), dt) for w, dt, spec in outs]
        + [jax.ShapeDtypeStruct((1, w), F32) for w in accs],
        scratch_shapes=[pltpu.VMEM((tm, D_FF), F32), pltpu.VMEM((tm, D_FF), F32)],
        compiler_params=pltpu.CompilerParams(dimension_semantics=("arbitrary",), vmem_limit_bytes=VMEM_LIMIT),
    )(oa, orn, h0, tgt, ga, gr, g2, w_out, w_gate, w_up, w_down)


def _in_bwd(dp, h0, dh1, ln1_g, w_in_p, srcs=(), scatter=()):
    n = h0.shape[0]
    tm = _row_tile(n)
    nk = len(srcs)
    c_in, c_out, c_shape, c_sems = _exchange_specs(srcs, scatter)

    def body(dp_ref, h0_ref, dh1_ref, g_ref, w_ref, *rest):
        dh0_ref, dg_ref = rest[nk:nk + 2]
        finish = _ride(1, *_exchange_fns(rest[:nk], rest[nk + 2:2 * nk + 2], rest[2 * nk + 2:], scatter))
        dhn = _dot(dp_ref[...], w_ref[...])
        xhat, r = _rms(h0_ref[...], D_MODEL)
        _acc(dg_ref, pl.program_id(0) == 0, _colsum(dhn * xhat))
        dh0_ref[...] = dh1_ref[...] + _rms_bwd(dhn, xhat, r, g_ref[...], D_MODEL)
        finish()

    def row(w):
        return pl.BlockSpec((tm, w), lambda i: (i, 0))

    res = pl.pallas_call(
        body, name="in_bwd", grid=(n // tm,),
        in_specs=[row(P_COLS), row(D_MODEL), row(D_MODEL), _const_spec((1, D_MODEL)), _const_spec((P_COLS, D_MODEL))] + c_in,
        out_specs=[row(D_MODEL), pl.BlockSpec((1, D_MODEL), lambda i: (0, 0))] + c_out,
        out_shape=[jax.ShapeDtypeStruct((n, D_MODEL), F32), jax.ShapeDtypeStruct((1, D_MODEL), F32)] + c_shape,
        scratch_shapes=c_sems,
        compiler_params=pltpu.CompilerParams(dimension_semantics=("arbitrary",), vmem_limit_bytes=VMEM_LIMIT),
    )(dp, h0, dh1, ln1_g, w_in_p, *srcs)
    return res[:2], res[2:]


MAX_TILE = D_FF // 2


def _pick_tile(width, cap):
    best = LANES
    for mult in range(1, width // LANES + 1):
        cand = mult * LANES
        if width % cand == 0 and cand <= cap:
            best = cand
    return best


def _matmul_tn(name, a, b, srcs=(), scatter=()):
    n, ka = a.shape
    kb = b.shape[1]
    ta, tb = _pick_tile(ka, MAX_TILE), _pick_tile(kb, MAX_TILE)
    tk = n // 4
    nk = len(srcs)
    c_in, c_out, c_shape, c_sems = _exchange_specs(srcs, scatter)

    def body(a_ref, b_ref, *rest):
        o_ref = rest[nk]
        finish = _ride(3, *_exchange_fns(rest[:nk], rest[nk + 1:2 * nk + 1], rest[2 * nk + 1:], scatter))
        _acc(o_ref, pl.program_id(2) == 0, _dot_tn(a_ref[...].astype(BF16), b_ref[...].astype(BF16)))
        finish()

    res = pl.pallas_call(
        body, name=name, grid=(ka // ta, kb // tb, n // tk),
        in_specs=[pl.BlockSpec((tk, ta), lambda i, j, k: (k, i)), pl.BlockSpec((tk, tb), lambda i, j, k: (k, j))] + c_in,
        out_specs=[pl.BlockSpec((ta, tb), lambda i, j, k: (i, j))] + c_out,
        out_shape=[jax.ShapeDtypeStruct((ka, kb), F32)] + c_shape, scratch_shapes=c_sems,
        compiler_params=pltpu.CompilerParams(dimension_semantics=("arbitrary", "arbitrary", "arbitrary"),
                                             vmem_limit_bytes=VMEM_LIMIT),
    )(a, b, *srcs)
    return res[0], res[1:]


def _matmul_shards(name, at, b):
    ka, n = at.shape
    kb = b.shape[1]
    ta, tb = _pick_tile(ka, MAX_TILE), _pick_tile(kb, MAX_TILE)
    tk = n // 2
    width = ka // N_DEV
    per = ta // width

    def body(a_ref, b_ref, o_ref, acc_ref):
        _acc(acc_ref, pl.program_id(2) == 0, _dot(a_ref[...], b_ref[...].astype(BF16)))

        @pl.when(pl.program_id(2) == pl.num_programs(2) - 1)
        def _():
            for s in range(per):
                o_ref[s] = acc_ref[s * width:(s + 1) * width, :].astype(BF16)

    return pl.pallas_call(
        body, name=name, grid=(ka // ta, kb // tb, n // tk),
        in_specs=[pl.BlockSpec((ta, tk), lambda i, j, k: (i, k)), pl.BlockSpec((tk, tb), lambda i, j, k: (k, j))],
        out_specs=pl.BlockSpec((per, width, tb), lambda i, j, k: (i, 0, j)),
        out_shape=jax.ShapeDtypeStruct((N_DEV, width, kb), BF16),
        scratch_shapes=[pltpu.VMEM((ta, tb), F32)],
        compiler_params=pltpu.CompilerParams(dimension_semantics=("parallel", "parallel", "arbitrary"),
                                             vmem_limit_bytes=VMEM_LIMIT),
    )(at, b)


def _adamw_math(g8_ref, w_ref, m_ref, v_ref, g_ref, d_ref, nm_ref, nv_ref):
    g = g8_ref[0].astype(F32)
    for s in range(1, N_DEV):
        g = g + g8_ref[s].astype(F32)
    g_ref[...] = g
    nm = ADAM_B1 * m_ref[...] + (1.0 - ADAM_B1) * g
    nv = ADAM_B2 * v_ref[...] + (1.0 - ADAM_B2) * (g * g)
    nm_ref[...] = nm
    nv_ref[...] = nv
    m_hat = nm / (1.0 - ADAM_B1 ** ADAM_STEP)
    v_hat = nv / (1.0 - ADAM_B2 ** ADAM_STEP)
    d_ref[...] = -ADAM_LR * (m_hat / (jnp.sqrt(v_hat) + ADAM_EPS) + ADAM_WD * w_ref[...])


def _adamw_many(name, items):
    count = len(items)

    def body(*refs):
        ins, outs = refs[:4 * count], refs[4 * count:]
        for i in range(count):
            _adamw_math(*ins[4 * i:4 * i + 4], *outs[4 * i:4 * i + 4])

    flat = [a for item in items for a in item]
    res = pl.pallas_call(
        body, name=name,
        out_shape=[jax.ShapeDtypeStruct(item[1].shape, F32) for item in items for _ in range(4)],
        compiler_params=pltpu.CompilerParams(vmem_limit_bytes=VMEM_LIMIT),
    )(*flat)
    return [tuple(res[4 * i:4 * i + 4]) for i in range(count)]


def _adamw(name, g8, w, m, v):
    rows, cols = w.shape
    tr = rows
    for cand in (256, 176, 128, 64):
        if rows % cand == 0 and rows > cand:
            tr = cand
            break

    def body(*refs):
        _adamw_math(*refs)

    blk = pl.BlockSpec((tr, cols), lambda i: (i, 0))
    return pl.pallas_call(
        body, name=name, grid=(rows // tr,),
        in_specs=[pl.BlockSpec((N_DEV, tr, cols), lambda i: (0, i, 0)), blk, blk, blk],
        out_specs=[blk] * 4, out_shape=[jax.ShapeDtypeStruct((rows, cols), F32)] * 4,
        compiler_params=pltpu.CompilerParams(dimension_semantics=("parallel",), vmem_limit_bytes=VMEM_LIMIT),
    )(g8, w, m, v)


def _exchange_specs(srcs, scatter):
    nk = len(srcs)
    if not nk:
        return [], [], [], []
    any_spec = pl.BlockSpec(memory_space=pl.ANY)
    out_shape = [jax.ShapeDtypeStruct(s.shape if sc else (N_DEV,) + s.shape, s.dtype) for s, sc in zip(srcs, scatter)]
    sems = [pltpu.SemaphoreType.DMA((nk, N_DEV - 1)), pltpu.SemaphoreType.DMA((nk, N_DEV - 1)),
            pltpu.SemaphoreType.DMA((nk,))]
    return [any_spec] * nk, [any_spec] * nk, out_shape, sems


FLIPS = ((0, 0, 1), (1, 0, 0), (0, 1, 0), (1, 1, 0), (1, 0, 1), (0, 1, 1), (1, 1, 1))
N_CHIP_PEERS = 3


def _exchange_fns(src_refs, out_refs, sems, scatter):
    nk = len(src_refs)
    if not nk:
        return (lambda: None), (lambda: None), (lambda: None)
    send_sems, recv_sems, local_sems = sems
    first = 1 + N_CHIP_PEERS

    def plan():
        x, y, c = lax.axis_index("x"), lax.axis_index("y"), lax.axis_index("c")
        me = 4 * x + 2 * y + c
        peers = [(1 - x if fx else x, 1 - y if fy else y, 1 - c if fc else c) for fx, fy, fc in FLIPS]
        pids = [4 * px + 2 * py + pc for px, py, pc in peers]

        def remote(k, j, src, dst, to):
            return pltpu.make_async_remote_copy(src_ref=src, dst_ref=dst, send_sem=send_sems.at[k, j],
                                                recv_sem=recv_sems.at[k, j], device_id=to, device_id_type=MESH)

        def mine(k, dest):
            return src_refs[k].at[dest] if scatter[k] else src_refs[k]

        local = [pltpu.make_async_copy(mine(k, me), out_refs[k].at[me], local_sems.at[k]) for k in range(nk)]
        direct = [remote(k, j, mine(k, pids[j]), out_refs[k].at[me], peers[j])
                  for k in range(nk) for j in range(len(FLIPS) if scatter[k] else first)]
        relays = {(k, j): remote(k, j, out_refs[k].at[pids[j - N_CHIP_PEERS]], out_refs[k].at[pids[j - N_CHIP_PEERS]], peers[0])
                  for k in range(nk) if not scatter[k] for j in range(first, len(FLIPS))}
        arrivals = {(k, j): remote(k, j, out_refs[k].at[pids[j]], out_refs[k].at[pids[j]], peers[j])
                    for k in range(nk) for j in range(len(FLIPS))}
        return local, direct, relays, arrivals

    def start():
        local, direct, _, _ = plan()
        for cp in local + direct:
            cp.start()

    def relay():
        _, _, relays, arrivals = plan()
        for (k, j), cp in relays.items():
            arrivals[k, j - N_CHIP_PEERS].wait_recv()
            cp.start()

    def wait():
        local, direct, relays, arrivals = plan()
        for (k, j), cp in arrivals.items():
            if (k, j + N_CHIP_PEERS) not in relays:
                cp.wait_recv()
        for cp in direct + list(relays.values()):
            cp.wait_send()
        for cp in local:
            cp.wait()

    return start, relay, wait


def _grid_step(rank):
    step, total = 0, 1
    for axis in range(rank):
        step = step * pl.num_programs(axis) + pl.program_id(axis)
        total = total * pl.num_programs(axis)
    return step, total


def _ride(rank, start, relay, wait):
    step, total = _grid_step(rank)
    pl.when(step == 0)(start)
    pl.when(step == (3 * total) // 4)(relay)
    return lambda: pl.when(step == total - 1)(wait)


def _exchange(name, srcs, scatter):
    nk = len(srcs)
    c_in, c_out, c_shape, c_sems = _exchange_specs(srcs, scatter)

    def body(*refs):
        start, relay, wait = _exchange_fns(refs[:nk], refs[nk:2 * nk], refs[2 * nk:], scatter)
        start()
        relay()
        wait()

    return pl.pallas_call(body, name=name, in_specs=c_in, out_specs=c_out, out_shape=c_shape, scratch_shapes=c_sems)(*srcs)


def _cols_from_shards(g):
    return jnp.transpose(g, (1, 0, 2)).reshape(g.shape[1], -1)


def _cols_to_shards(w):
    return jnp.transpose(w.reshape(w.shape[0], N_DEV, -1), (1, 0, 2))


def _prep(x, tgt, srcs, scatter):
    nb = x.shape[0]
    t = _t_pad()
    head = PAD_ROWS + N_META
    nk = len(srcs)
    c_in, c_out, c_shape, c_sems = _exchange_specs(srcs, scatter)

    def body(x_ref, tgt_ref, *rest):
        h0_ref, tp_ref = rest[nk:nk + 2]
        finish = _ride(1, *_exchange_fns(rest[:nk], rest[nk + 2:2 * nk + 2], rest[2 * nk + 2:], scatter))
        lead = pl.program_id(0) == 0

        @pl.when(lead)
        def _():
            h0_ref[...] = jnp.zeros_like(h0_ref)
            tp_ref[...] = jnp.zeros_like(tp_ref)

        @pl.when(jnp.logical_not(lead))
        def _():
            h0_ref[...] = x_ref[...]
            tp_ref[...] = tgt_ref[...]

        finish()

    src = pl.BlockSpec((nb, head, D_MODEL), lambda j: (0, jnp.maximum(j - 1, 0), 0))
    dst = pl.BlockSpec((nb, head, D_MODEL), lambda j: (0, j, 0))
    padded = jax.ShapeDtypeStruct((nb, t, D_MODEL), F32)
    res = pl.pallas_call(
        body, name="prep", grid=(t // head,), in_specs=[src, src] + c_in, out_specs=[dst, dst] + c_out,
        out_shape=[padded, padded] + c_shape, scratch_shapes=c_sems,
        compiler_params=pltpu.CompilerParams(dimension_semantics=("arbitrary",)),
    )(x, tgt, *srcs)
    return res[0], res[1], res[2:]


def _rope_tables(n):
    t = _t_pad()
    pos = np.arange(t, dtype=np.float32) - np.float32(PAD_ROWS)
    half = QK_ROPE // 2
    freqs = (1.0 / (ROPE_THETA ** (np.arange(half, dtype=np.float32) / half))).astype(np.float32)
    ang = pos[:, None] * freqs[None, :]
    cos, sin = np.cos(ang), np.sin(ang)
    z = lambda w: np.zeros((t, w), np.float32)
    c = np.concatenate([np.ones((t, QK_NOPE), np.float32), cos, cos, z(HEAD_PAD - QK_HEAD)], axis=1)
    s1 = np.concatenate([z(QK_NOPE + half), sin, z(HEAD_PAD - QK_HEAD)], axis=1)
    s2 = np.concatenate([z(QK_NOPE), -sin, z(HEAD_PAD - QK_NOPE - half)], axis=1)
    return tuple(jnp.asarray(np.tile(a, (n // t, 1))) for a in (c, s1, s2))


def _block_diag_gates(lru_wa, lru_wi):
    eye = jnp.eye(2, dtype=lru_wa.dtype)

    def bd(w):
        w = w.reshape(2, D_RNN // LANES, 2, RNN_BW, RNN_BW)
        full = w[:, :, :, :, None, :] * eye[None, None, :, None, :, None]
        return full.reshape(2, D_RNN // LANES, LANES, LANES)

    a, i = bd(lru_wa), bd(lru_wi)
    return jnp.concatenate([a[0], i[0], a[1], i[1]], axis=-1)


def _unblock_gates(dw):
    nb = D_RNN // LANES
    parts = dw.reshape(nb, 2, RNN_BW, 4, 2, RNN_BW)
    diag = jnp.stack([parts[:, k, :, :, k, :] for k in range(2)], axis=1)
    diag = jnp.transpose(diag, (3, 0, 1, 2, 4)).reshape(4, 2 * nb, RNN_BW, RNN_BW)
    return jnp.stack([diag[0], diag[2]]), jnp.stack([diag[1], diag[3]])


WEIGHTS = ("meta_tokens", "ln1_g", "w_in", "q_a_norm_g", "w_uq", "kv_a_norm_g", "w_ukv", "q_norm_g", "k_norm_g",
           "conv_w", "conv_b", "lru_wa", "lru_ba", "lru_wi", "lru_bi", "lru_lambda", "attn_out_g", "rnn_out_g",
           "w_out", "ln2_g", "w_gate", "w_up", "w_down")
BIG = ("w_in", "w_uq", "w_ukv", "w_out", "w_gate", "w_up", "w_down")
TRANSPOSED = ("w_in", "w_uq", "w_gate", "w_up")
ROW_SHARDED = ("w_out", "w_down") + TRANSPOSED
REPLICATED = ("ln1_g", "q_a_norm_g", "kv_a_norm_g", "q_norm_g", "k_norm_g", "conv_b", "lru_wa", "lru_wi",
              "attn_out_g", "rnn_out_g", "ln2_g")
WHOLE = REPLICATED + ("loss",)
G_FIRST = ("w_in", "meta_tokens")
G_MID = ("w_uq", "w_ukv", "conv_w", "lru_ba", "lru_bi", "lru_lambda")
LATE = ("w_out", "w_gate", "w_up", "w_down")
G_LAST = ("meta_tokens", "ln1_g")


def _local_step(x, tgt, ex):
    nb = x.shape[0]
    t = _t_pad()
    n = nb * t
    local = ex.local
    h0, tgt_p, got = _prep(x, tgt, *ex.gather_srcs(G_FIRST))
    first = ex.gathered(G_FIRST, got)
    meta, w_in = first["meta_tokens"], first["w_in"]
    h0 = h0.at[:, PAD_ROWS:PAD_ROWS + N_META].set(jnp.broadcast_to(meta[None], (nb, N_META, D_MODEL))).reshape(n, D_MODEL)
    tgt_p = tgt_p.reshape(n, D_MODEL)

    zr = lambda r: jnp.zeros((r, D_MODEL), w_in.dtype)
    w_in_p = jnp.concatenate([w_in[:OFF_CKV], w_in[OFF_KR:], zr(QK_NOPE), w_in[OFF_CKV:OFF_KR], zr(HEAD_PAD - QK_HEAD)],
                             axis=0)
    pad_g = lambda g: jnp.pad(g, ((0, 0), (0, HEAD_PAD - QK_HEAD)))
    qg, kg = pad_g(local["q_norm_g"]), pad_g(local["k_norm_g"])
    rc, rs1, rs2 = _rope_tables(n)
    wblk = _block_diag_gates(local["lru_wa"].reshape(2, -1, RNN_BW, RNN_BW),
                             local["lru_wi"].reshape(2, -1, RNN_BW, RNN_BW)).astype(BF16)
    nblk = D_RNN // LANES

    (hn, cq, ckv, xr, xg, kr), got = _in_proj(h0, local["ln1_g"], w_in_p, *ex.gather_srcs(G_MID))
    w = ex.gathered(G_MID, got)
    w_uq_p = jnp.pad(w["w_uq"].reshape(N_HEADS, QK_HEAD, Q_LORA), ((0, 0), (0, HEAD_PAD - QK_HEAD), (0, 0))
                     ).reshape(QP_COLS, Q_LORA)
    ukv = w["w_ukv"].reshape(KV_LORA, N_HEADS, QK_NOPE + V_HEAD)
    w_uk_p = jnp.pad(ukv[:, :, :QK_NOPE], ((0, 0), (0, 0), (0, HEAD_PAD - QK_NOPE))).reshape(KV_LORA, QP_COLS)
    w_v = ukv[:, :, QK_NOPE:].reshape(KV_LORA, D_ATTN)
    gbias = jnp.stack([w["lru_ba"][0], w["lru_bi"][0], w["lru_ba"][1], w["lru_bi"][1]], axis=0)
    gbias = jnp.transpose(gbias.reshape(4, nblk, LANES), (1, 0, 2)).reshape(nblk, 1, 4 * LANES)

    q, k, v = _qkv_fwd(cq, ckv, kr, local["q_a_norm_g"], local["kv_a_norm_g"], w_uq_p, w_uk_p, w_v, qg, kg, rc, rs1, rs2)
    oa, probs, got = _attn_fwd(q, k, v, *ex.gather_srcs(LATE))
    late = ex.gathered(LATE, got)
    orn = _rnn_fwd(xr, xg, w["conv_w"], local["conv_b"], wblk, gbias, w["lru_lambda"])
    (doa, dor, dh1, mix_t, h1n, act_t, dgate_t, dup_t, dyb, loss, dga, dgr, dg2) = _post(
        oa, orn, h0, tgt_p, local["attn_out_g"], local["rnn_out_g"], local["ln2_g"], late["w_out"], late["w_gate"],
        late["w_up"], late["w_down"])
    wire = {"w_out": _matmul_shards("dw_out", mix_t, dh1), "w_gate": _matmul_shards("dw_gate", dgate_t, h1n),
            "w_up": _matmul_shards("dw_up", dup_t, h1n), "w_down": _matmul_shards("dw_down", act_t, dyb)}
    names = ("w_out", "w_gate")
    (dxr, dxg, dcw, dcb, dwblk, dgb, dlam), got = _rnn_bwd(xr, xg, dor, w["conv_w"], local["conv_b"], wblk, gbias,
                                                           w["lru_lambda"], *ex.scatter_srcs(names, wire))
    summed = ex.scattered(names, wire, got)
    dwa, dwi = _unblock_gates(dwblk)
    dgb = jnp.transpose(dgb.reshape(nblk, 4, LANES), (1, 0, 2)).reshape(4, D_RNN)
    names = ("w_up", "w_down")
    (dq_r, dk_r, dv), got = _attn_bwd(q, k, v, doa, oa, probs, *ex.scatter_srcs(names, wire))
    summed.update(ex.scattered(names, wire, got))
    wire = ex.to_wire({
        "conv_w": dcw, "conv_b": dcb, "lru_wa": dwa.reshape(-1, RNN_BW), "lru_ba": jnp.stack([dgb[0], dgb[2]]),
        "lru_wi": dwi.reshape(-1, RNN_BW), "lru_bi": jnp.stack([dgb[1], dgb[3]]), "lru_lambda": dlam,
        "attn_out_g": dga, "rnn_out_g": dgr, "ln2_g": dg2, "loss": loss})
    names = tuple(wire)
    (dp, qa, kva, dqp, dkv, dqg, dkg, dgqa, dgkva), got = _qkv_bwd(
        cq, ckv, kr, dq_r, dk_r, dv, dxr, dxg, local["q_a_norm_g"], local["kv_a_norm_g"], w_uq_p, w_uk_p, w_v, qg, kg,
        rc, rs1, rs2, *ex.scatter_srcs(names, wire))
    summed.update(ex.scattered(names, wire, got))
    dw_uq_p, _ = _matmul_tn("dw_uq", dqp, qa)
    dw_kv, _ = _matmul_tn("dw_ukv", kva, dkv)
    dw_uq = dw_uq_p.reshape(N_HEADS, HEAD_PAD, Q_LORA)[:, :QK_HEAD].reshape(N_HEADS * QK_HEAD, Q_LORA)
    dw_ukv = jnp.concatenate([dw_kv[:, :QP_COLS].reshape(KV_LORA, N_HEADS, HEAD_PAD)[:, :, :QK_NOPE],
                              dw_kv[:, QP_COLS:].reshape(KV_LORA, N_HEADS, V_HEAD)], axis=2).reshape(KV_LORA, -1)
    wire = ex.to_wire({"q_a_norm_g": dgqa, "w_uq": dw_uq, "kv_a_norm_g": dgkva, "w_ukv": dw_ukv,
                       "q_norm_g": dqg[:, :QK_HEAD], "k_norm_g": dkg[:, :QK_HEAD]})
    names = tuple(wire)
    dw_in_p, got = _matmul_tn("dw_in", dp, hn, *ex.scatter_srcs(names, wire))
    summed.update(ex.scattered(names, wire, got))
    kr0 = OFF_CKV + 2 * D_RNN + QK_NOPE
    dw_in = jnp.concatenate([dw_in_p[:OFF_CKV], dw_in_p[kr0:kr0 + QK_ROPE], dw_in_p[OFF_CKV:OFF_CKV + 2 * D_RNN]], axis=0)
    wire = ex.to_wire({"w_in": dw_in})
    (dh0, dg1), got = _in_bwd(dp, h0, dh1, local["ln1_g"], w_in_p, *ex.scatter_srcs(("w_in",), wire))
    summed.update(ex.scattered(("w_in",), wire, got))

    dh0 = dh0.reshape(nb, t, D_MODEL)
    wire = ex.to_wire({"meta_tokens": jnp.sum(dh0[:, PAD_ROWS:PAD_ROWS + N_META], axis=0), "ln1_g": dg1})
    got = ex.run("reduce_last", *ex.scatter_srcs(G_LAST, wire))
    summed.update(ex.scattered(G_LAST, wire, got))
    return dh0[:, PAD_ROWS + N_META:], summed


class _MeshExchange:
    def __init__(self, shards):
        self.local = shards

    @staticmethod
    def run(name, srcs, scatter):
        return _exchange(name, srcs, scatter)

    def gather_srcs(self, names):
        return [self.local[k].astype(BF16) if k in BIG else self.local[k] for k in names], [False] * len(names)

    @staticmethod
    def gathered(names, outs):
        return {k: g.reshape(-1, g.shape[-1]) if k in ROW_SHARDED else _cols_from_shards(g) for k, g in zip(names, outs)}

    @staticmethod
    def to_wire(grads):
        wire = {}
        for k, g in grads.items():
            if k in WHOLE:
                wire[k] = g
            elif k in ROW_SHARDED:
                wire[k] = g.reshape(N_DEV, -1, g.shape[-1]).astype(BF16)
            else:
                wire[k] = _cols_to_shards(g).astype(BF16) if k in BIG else _cols_to_shards(g)
        return wire

    @staticmethod
    def scatter_srcs(names, wire):
        return [wire[k] for k in names], [k not in WHOLE for k in names]

    @staticmethod
    def scattered(names, wire, outs):
        return dict(zip(names, outs))


def kernel(x, meta_tokens, ln1_g, w_in, q_a_norm_g, w_uq, kv_a_norm_g, w_ukv, q_norm_g, k_norm_g, conv_w, conv_b, lru_wa, lru_ba, lru_wi, lru_bi, lru_lambda, attn_out_g, rnn_out_g, w_out, ln2_g, w_gate, w_up, w_down, loss_target, m_meta_tokens, m_ln1_g, m_w_in, m_q_a_norm_g, m_w_uq, m_kv_a_norm_g, m_w_ukv, m_q_norm_g, m_k_norm_g, m_conv_w, m_conv_b, m_lru_wa, m_lru_ba, m_lru_wi, m_lru_bi, m_lru_lambda, m_attn_out_g, m_rnn_out_g, m_w_out, m_ln2_g, m_w_gate, m_w_up, m_w_down, v_meta_tokens, v_ln1_g, v_w_in, v_q_a_norm_g, v_w_uq, v_kv_a_norm_g, v_w_ukv, v_q_norm_g, v_k_norm_g, v_conv_w, v_conv_b, v_lru_wa, v_lru_ba, v_lru_wi, v_lru_bi, v_lru_lambda, v_attn_out_g, v_rnn_out_g, v_w_out, v_ln2_g, v_w_gate, v_w_up, v_w_down):
    given = (meta_tokens, ln1_g, w_in, q_a_norm_g, w_uq, kv_a_norm_g, w_ukv, q_norm_g, k_norm_g, conv_w, conv_b,
             lru_wa, lru_ba, lru_wi, lru_bi, lru_lambda, attn_out_g, rnn_out_g, w_out, ln2_g, w_gate, w_up, w_down)
    moments_m = (m_meta_tokens, m_ln1_g, m_w_in, m_q_a_norm_g, m_w_uq, m_kv_a_norm_g, m_w_ukv, m_q_norm_g, m_k_norm_g,
                 m_conv_w, m_conv_b, m_lru_wa, m_lru_ba, m_lru_wi, m_lru_bi, m_lru_lambda, m_attn_out_g, m_rnn_out_g,
                 m_w_out, m_ln2_g, m_w_gate, m_w_up, m_w_down)
    moments_v = (v_meta_tokens, v_ln1_g, v_w_in, v_q_a_norm_g, v_w_uq, v_kv_a_norm_g, v_w_ukv, v_q_norm_g, v_k_norm_g,
                 v_conv_w, v_conv_b, v_lru_wa, v_lru_ba, v_lru_wi, v_lru_bi, v_lru_lambda, v_attn_out_g, v_rnn_out_g,
                 v_w_out, v_ln2_g, v_w_gate, v_w_up, v_w_down)
    shapes = {k: a.shape for k, a in zip(WEIGHTS, given)}

    def two_d(k, a):
        a = a.reshape(-1, a.shape[-1])
        return a.T if k in TRANSPOSED else a

    w = {k: two_d(k, a) for k, a in zip(WEIGHTS, given)}
    m = {k: two_d(k, a) for k, a in zip(WEIGHTS, moments_m)}
    v = {k: two_d(k, a) for k, a in zip(WEIGHTS, moments_v)}

    grad_x, parts = _local_step(x, loss_target, _MeshExchange(w))

    tiled = ("w_in", "w_gate", "w_up", "w_down")
    new = {k: _adamw("adamw_" + k, parts[k], w[k], m[k], v[k]) for k in tiled}
    small = [k for k in WEIGHTS if k not in tiled]
    new.update(zip(small, _adamw_many("adamw_small", [(parts[k], w[k], m[k], v[k]) for k in small])))

    loss = jnp.sum(parts["loss"][:, 0, 0])
    outs = [loss, grad_x]
    for idx in range(4):
        outs += [(new[k][idx].T if k in TRANSPOSED else new[k][idx]).reshape(shapes[k]) for k in WEIGHTS]
    return tuple(outs)
```

```python
import functools
import math

import numpy as np
import jax
import jax.numpy as jnp
from jax import lax
from jax.experimental import pallas as pl
from jax.experimental.pallas import tpu as pltpu

F32 = jnp.float32
BF16 = jnp.bfloat16

D_MODEL = 1024
N_META = 16
SEQ = 2048
N_HEADS = 8
QK_NOPE = 64
QK_ROPE = 32
QK_HEAD = QK_NOPE + QK_ROPE
V_HEAD = 64
D_ATTN = N_HEADS * V_HEAD
Q_LORA = 384
KV_LORA = 256
D_RNN = 512
RNN_BW = 64
D_FF = 2816
EPS = 1e-6
LRU_C = 8.0
ROPE_THETA = 10000.0
OFF_CKV = Q_LORA + KV_LORA
OFF_KR = OFF_CKV + QK_ROPE
IN_COLS = OFF_KR + 2 * D_RNN

ADAM_LR = 0.001
ADAM_B1 = 0.9
ADAM_B2 = 0.999
ADAM_EPS = 1e-08
ADAM_WD = 0.01
ADAM_STEP = 10

N_DEV = 8
LANES = 128
HEAD_PAD = LANES
PAD_ROWS = LANES - N_META
QP_COLS = N_HEADS * HEAD_PAD
P_COLS = OFF_CKV + 2 * D_RNN + LANES
FF_CHUNK = D_FF
VMEM_LIMIT = 56 * 1024 * 1024
MESH = pl.DeviceIdType.MESH


def _t_pad():
    return PAD_ROWS + N_META + SEQ


def _row_tile(n):
    return 256 if n % 256 == 0 else 128


def _wide_row_tile(n):
    quarter = _t_pad() // 4
    return quarter if quarter % 16 == 0 and n % quarter == 0 else _row_tile(n)


def _const_spec(shape):
    nd = len(shape)
    return pl.BlockSpec(shape, lambda *_: (0,) * nd, pipeline_mode=pl.Buffered(1))


def _rms(x, d):
    r = lax.rsqrt(jnp.sum(x * x, axis=-1, keepdims=True) * (1.0 / d) + EPS)
    return x * r, r


def _rms_bwd(dy, xhat, r, g, d):
    dxh = dy * g
    return r * (dxh - xhat * (jnp.sum(dxh * xhat, axis=-1, keepdims=True) * (1.0 / d)))


def _colsum(x):
    return jnp.sum(x, axis=0, keepdims=True)


def _dot(a, b):
    return jnp.dot(a, b, preferred_element_type=F32)


def _dot_nt(a, b):
    return lax.dot_general(a, b, (((1,), (1,)), ((), ())), preferred_element_type=F32)


def _dot_tn(a, b):
    return lax.dot_general(a, b, (((0,), (0,)), ((), ())), preferred_element_type=F32)


def _rope(x, c, s1, s2):
    return x * c + pltpu.roll(x, 16, 1) * s1 + pltpu.roll(x, HEAD_PAD - 16, 1) * s2


def _rope_bwd(dy, c, s1, s2):
    return dy * c + pltpu.roll(dy * s1, HEAD_PAD - 16, 1) + pltpu.roll(dy * s2, 16, 1)


def _acc(ref, first, val):
    @pl.when(first)
    def _():
        ref[...] = val

    @pl.when(jnp.logical_not(first))
    def _():
        ref[...] += val


def _in_proj(h0, ln1_g, w_in_p, srcs=(), scatter=()):
    n = h0.shape[0]
    tm = _wide_row_tile(n)
    nk = len(srcs)
    c_in, c_out, c_shape, c_sems = _exchange_specs(srcs, scatter)

    def body(h_ref, g_ref, w_ref, *rest):
        hn_ref, cq_ref, ckv_ref, xr_ref, xg_ref, kr_ref = rest[nk:nk + 6]
        finish = _ride(1, *_exchange_fns(rest[:nk], rest[nk + 6:2 * nk + 6], rest[2 * nk + 6:], scatter))
        xhat, _ = _rms(h_ref[...], D_MODEL)
        hn = (xhat * g_ref[...]).astype(BF16)
        hn_ref[...] = hn
        p = _dot_nt(hn, w_ref[...])
        cq_ref[...] = p[:, :Q_LORA]
        ckv_ref[...] = p[:, Q_LORA:OFF_CKV]
        xr_ref[...] = p[:, OFF_CKV:OFF_CKV + D_RNN]
        xg_ref[...] = p[:, OFF_CKV + D_RNN:OFF_CKV + 2 * D_RNN]
        kr_ref[...] = p[:, OFF_CKV + 2 * D_RNN:]
        finish()

    def row(w):
        return pl.BlockSpec((tm, w), lambda i: (i, 0))

    widths = (D_MODEL, Q_LORA, KV_LORA, D_RNN, D_RNN, LANES)
    res = pl.pallas_call(
        body, name="in_proj", grid=(n // tm,),
        in_specs=[row(D_MODEL), _const_spec((1, D_MODEL)), _const_spec((P_COLS, D_MODEL))] + c_in,
        out_specs=[row(w) for w in widths] + c_out,
        out_shape=[jax.ShapeDtypeStruct((n, w), BF16 if k == 0 else F32) for k, w in enumerate(widths)] + c_shape,
        scratch_shapes=c_sems,
        compiler_params=pltpu.CompilerParams(dimension_semantics=("arbitrary",), vmem_limit_bytes=VMEM_LIMIT),
    )(h0, ln1_g, w_in_p, *srcs)
    return res[:6], res[6:]


def _qkv_fwd(cq, ckv, kr, gqa, gkva, w_uq_p, w_uk_p, w_v, qg, kg, rc, rs1, rs2):
    n = cq.shape[0]
    tm = _wide_row_tile(n)

    def body(cq_ref, ckv_ref, kr_ref, gqa_ref, gkva_ref, wuq_ref, wuk_ref, wv_ref, qg_ref, kg_ref,
             c_ref, s1_ref, s2_ref, q_ref, k_ref, v_ref):
        xq, _ = _rms(cq_ref[...], Q_LORA)
        qa = (xq * gqa_ref[...]).astype(BF16)
        q = _dot_nt(qa, wuq_ref[...])
        xkv, _ = _rms(ckv_ref[...], KV_LORA)
        kva = (xkv * gkva_ref[...]).astype(BF16)
        kn = _dot(kva, wuk_ref[...])
        v_ref[...] = _dot(kva, wv_ref[...]).astype(BF16)
        krp = kr_ref[...]
        c, s1, s2 = c_ref[...], s1_ref[...], s2_ref[...]
        for h in range(N_HEADS):
            sl = slice(h * HEAD_PAD, (h + 1) * HEAD_PAD)
            qh, _ = _rms(q[:, sl], QK_HEAD)
            q_ref[:, sl] = _rope(qh * qg_ref[...], c, s1, s2).astype(BF16)
            kh, _ = _rms(kn[:, sl] + krp, QK_HEAD)
            k_ref[:, sl] = _rope(kh * kg_ref[...], c, s1, s2).astype(BF16)

    def row(w):
        return pl.BlockSpec((tm, w), lambda i: (i, 0))

    return pl.pallas_call(
        body, name="qkv_fwd", grid=(n // tm,),
        in_specs=[row(Q_LORA), row(KV_LORA), row(LANES), _const_spec((1, Q_LORA)), _const_spec((1, KV_LORA)),
                  _const_spec((QP_COLS, Q_LORA)), _const_spec((KV_LORA, QP_COLS)), _const_spec((KV_LORA, D_ATTN)),
                  _const_spec((1, LANES)), _const_spec((1, LANES)), row(LANES), row(LANES), row(LANES)],
        out_specs=[row(QP_COLS), row(QP_COLS), row(D_ATTN)],
        out_shape=[jax.ShapeDtypeStruct((n, QP_COLS), BF16), jax.ShapeDtypeStruct((n, QP_COLS), BF16),
                   jax.ShapeDtypeStruct((n, D_ATTN), BF16)],
        compiler_params=pltpu.CompilerParams(dimension_semantics=("parallel",), vmem_limit_bytes=VMEM_LIMIT),
    )(cq, ckv, kr, gqa, gkva, w_uq_p, w_uk_p, w_v, qg, kg, rc, rs1, rs2)


def _qkv_bwd(cq, ckv, kr, dq_r, dk_r, dv, dxr, dxg, gqa, gkva, w_uq_p, w_uk_p, w_v, qg, kg, rc, rs1, rs2,
             srcs=(), scatter=()):
    n = cq.shape[0]
    tm = _wide_row_tile(n)
    nk = len(srcs)
    c_in, c_out, c_shape, c_sems = _exchange_specs(srcs, scatter)

    def body(cq_ref, ckv_ref, kr_ref, dq_ref, dk_ref, dv_ref, dxr_ref, dxg_ref, gqa_ref, gkva_ref, wuq_ref, wuk_ref,
             wv_ref, qg_ref, kg_ref, c_ref, s1_ref, s2_ref, *rest):
        dp_ref, qa_ref, kva_ref, dqp_ref, dkv_ref, dqg_ref, dkg_ref, dgqa_ref, dgkva_ref = rest[nk:nk + 9]
        finish = _ride(1, *_exchange_fns(rest[:nk], rest[nk + 9:2 * nk + 9], rest[2 * nk + 9:], scatter))
        first = pl.program_id(0) == 0
        dp_ref[:, OFF_CKV:OFF_CKV + D_RNN] = dxr_ref[...].astype(BF16)
        dp_ref[:, OFF_CKV + D_RNN:OFF_CKV + 2 * D_RNN] = dxg_ref[...].astype(BF16)
        xq, rq = _rms(cq_ref[...], Q_LORA)
        qa = (xq * gqa_ref[...]).astype(BF16)
        qa_ref[...] = qa
        q = _dot_nt(qa, wuq_ref[...])
        xkv, rkv = _rms(ckv_ref[...], KV_LORA)
        kva = (xkv * gkva_ref[...]).astype(BF16)
        kva_ref[...] = kva
        kn = _dot(kva, wuk_ref[...])
        krp = kr_ref[...]
        c, s1, s2 = c_ref[...], s1_ref[...], s2_ref[...]
        lane = lax.broadcasted_iota(jnp.int32, (tm, HEAD_PAD), 1)
        rope_lanes = jnp.logical_and(lane >= QK_NOPE, lane < QK_HEAD)
        dqg = jnp.zeros((1, HEAD_PAD), F32)
        dkg = jnp.zeros((1, HEAD_PAD), F32)
        dkr = jnp.zeros((tm, HEAD_PAD), F32)
        for h in range(N_HEADS):
            sl = slice(h * HEAD_PAD, (h + 1) * HEAD_PAD)
            qh, rqh = _rms(q[:, sl], QK_HEAD)
            dy = _rope_bwd(dq_ref[:, sl], c, s1, s2)
            dqg = dqg + _colsum(dy * qh)
            dqp_ref[:, sl] = _rms_bwd(dy, qh, rqh, qg_ref[...], QK_HEAD).astype(BF16)
            kh, rkh = _rms(kn[:, sl] + krp, QK_HEAD)
            dyk = _rope_bwd(dk_ref[:, sl], c, s1, s2)
            dkg = dkg + _colsum(dyk * kh)
            dkh = _rms_bwd(dyk, kh, rkh, kg_ref[...], QK_HEAD)
            dkv_ref[:, sl] = dkh.astype(BF16)
            dkr = dkr + jnp.where(rope_lanes, dkh, 0.0)
        dkv_ref[:, QP_COLS:] = dv_ref[...].astype(BF16)
        dp_ref[:, OFF_CKV + 2 * D_RNN:] = dkr.astype(BF16)
        dqa = _dot(dqp_ref[...], wuq_ref[...])
        dp_ref[:, :Q_LORA] = _rms_bwd(dqa, xq, rq, gqa_ref[...], Q_LORA).astype(BF16)
        dkva = _dot_nt(dkv_ref[:, :QP_COLS], wuk_ref[...]) + _dot_nt(dkv_ref[:, QP_COLS:], wv_ref[...])
        dp_ref[:, Q_LORA:OFF_CKV] = _rms_bwd(dkva, xkv, rkv, gkva_ref[...], KV_LORA).astype(BF16)
        _acc(dqg_ref, first, dqg)
        _acc(dkg_ref, first, dkg)
        _acc(dgqa_ref, first, _colsum(dqa * xq))
        _acc(dgkva_ref, first, _colsum(dkva * xkv))
        finish()

    def row(w):
        return pl.BlockSpec((tm, w), lambda i: (i, 0))

    def acc(w):
        return pl.BlockSpec((1, w), lambda i: (0, 0))

    res = pl.pallas_call(
        body, name="qkv_bwd", grid=(n // tm,),
        in_specs=[row(Q_LORA), row(KV_LORA), row(LANES), row(QP_COLS), row(QP_COLS), row(D_ATTN), row(D_RNN), row(D_RNN),
                  _const_spec((1, Q_LORA)), _const_spec((1, KV_LORA)),
                  _const_spec((QP_COLS, Q_LORA)), _const_spec((KV_LORA, QP_COLS)), _const_spec((KV_LORA, D_ATTN)),
                  _const_spec((1, LANES)), _const_spec((1, LANES)), row(LANES), row(LANES), row(LANES)] + c_in,
        out_specs=[row(P_COLS), row(Q_LORA), row(KV_LORA), row(QP_COLS),
                   row(QP_COLS + D_ATTN), acc(LANES), acc(LANES), acc(Q_LORA), acc(KV_LORA)] + c_out,
        out_shape=[jax.ShapeDtypeStruct((n, P_COLS), BF16), jax.ShapeDtypeStruct((n, Q_LORA), BF16),
                   jax.ShapeDtypeStruct((n, KV_LORA), BF16), jax.ShapeDtypeStruct((n, QP_COLS), BF16),
                   jax.ShapeDtypeStruct((n, QP_COLS + D_ATTN), BF16),
                   jax.ShapeDtypeStruct((1, LANES), F32), jax.ShapeDtypeStruct((1, LANES), F32),
                   jax.ShapeDtypeStruct((1, Q_LORA), F32), jax.ShapeDtypeStruct((1, KV_LORA), F32)] + c_shape,
        scratch_shapes=c_sems,
        compiler_params=pltpu.CompilerParams(dimension_semantics=("arbitrary",), vmem_limit_bytes=VMEM_LIMIT),
    )(cq, ckv, kr, dq_r, dk_r, dv, dxr, dxg, gqa, gkva, w_uq_p, w_uk_p, w_v, qg, kg, rc, rs1, rs2, *srcs)
    return res[:9], res[9:]


KEY_CHUNK = 4 * LANES


def _key_chunks(t):
    count = max(t // KEY_CHUNK, 1)
    first = t - KEY_CHUNK * (count - 1)
    return [(0, first)] + [(first + KEY_CHUNK * c, KEY_CHUNK) for c in range(count - 1)]


def _attn_specs(t, tq):
    nq = t // tq
    qspec = pl.BlockSpec((tq, 2 * HEAD_PAD), lambda b, hp, i: (b * nq + i, hp))
    kspec = pl.BlockSpec((t, 2 * HEAD_PAD), lambda b, hp, i: (b, hp))
    vspec = pl.BlockSpec((t, 2 * V_HEAD), lambda b, hp, i: (b, hp))
    ospec = pl.BlockSpec((tq, 2 * V_HEAD), lambda b, hp, i: (b * nq + i, hp))
    return nq, qspec, kspec, vspec, ospec


def _probs_spec(t, tq):
    return pl.BlockSpec((1, 2, tq, t), lambda b, hp, i: (b, hp, i, 0))


def _attn_fwd(q, k, v, srcs=(), scatter=()):
    n = q.shape[0]
    t = _t_pad()
    tq = t // 2
    nq, qspec, kspec, vspec, ospec = _attn_specs(t, tq)
    nk = len(srcs)
    c_in, c_out, c_shape, c_sems = _exchange_specs(srcs, scatter)

    def body(q_ref, k_ref, v_ref, *rest):
        o_ref, l_ref, p_ref = rest[nk:nk + 3]
        finish = _ride(3, *_exchange_fns(rest[:nk], rest[nk + 3:2 * nk + 3], rest[2 * nk + 3:], scatter))
        lane = lax.broadcasted_iota(jnp.int32, (tq, 2 * V_HEAD), 1)
        outs = []
        sums = []
        for j in range(2):
            sl = slice(j * HEAD_PAD, (j + 1) * HEAD_PAD)
            qh = q_ref[:, sl]

            def scores(start, size):
                s = _dot_nt(qh, k_ref[start:start + size, sl])
                if start < PAD_ROWS:
                    key = lax.broadcasted_iota(jnp.int32, (tq, size), 1) + start
                    s = jnp.where(key >= PAD_ROWS, s, -jnp.inf)
                return s

            top = functools.reduce(jnp.maximum, [jnp.max(scores(*c), axis=-1, keepdims=True) for c in _key_chunks(t)])
            l = jnp.zeros((tq, 1), F32)
            pv = jnp.zeros((tq, 2 * V_HEAD), F32)
            for start, size in _key_chunks(t):
                e = jnp.exp((scores(start, size) - top) * (QK_HEAD ** -0.5))
                l = l + jnp.sum(e, axis=-1, keepdims=True)
                e = e.astype(BF16)
                p_ref[0, j, :, start:start + size] = e
                pv = pv + _dot(e, v_ref[start:start + size, :])
            outs.append(pv / l)
            sums.append(l)
        o_ref[...] = jnp.where(lane < V_HEAD, outs[0], outs[1])
        l_ref[...] = jnp.where(lane < V_HEAD, sums[0], sums[1])
        finish()

    res = pl.pallas_call(
        body, name="attn_fwd", grid=(n // t, N_HEADS // 2, nq),
        in_specs=[qspec, kspec, vspec] + c_in, out_specs=[ospec, ospec, _probs_spec(t, tq)] + c_out,
        out_shape=[jax.ShapeDtypeStruct((n, D_ATTN), F32), jax.ShapeDtypeStruct((n, D_ATTN), F32),
                   jax.ShapeDtypeStruct((n // t, N_HEADS, t, t), BF16)] + c_shape,
        scratch_shapes=c_sems,
        compiler_params=pltpu.CompilerParams(dimension_semantics=("arbitrary", "arbitrary", "arbitrary"),
                                             vmem_limit_bytes=VMEM_LIMIT),
    )(q, k, v, *srcs)
    return res[0], (res[1], res[2]), res[3:]


def _attn_bwd(q, k, v, do, o, probs, srcs=(), scatter=()):
    n = q.shape[0]
    t = _t_pad()
    tq = t // 2
    nq, qspec, kspec, vspec, ospec = _attn_specs(t, tq)
    nk = len(srcs)
    c_in, c_out, c_shape, c_sems = _exchange_specs(srcs, scatter)

    def body(q_ref, k_ref, v_ref, do_ref, o_ref, l_ref, p_ref, *rest):
        dq_ref, dk_ref, dv_ref = rest[nk:nk + 3]
        finish = _ride(3, *_exchange_fns(rest[:nk], rest[nk + 3:2 * nk + 3], rest[2 * nk + 3:], scatter))

        @pl.when(pl.program_id(2) == 0)
        def _():
            dk_ref[...] = jnp.zeros_like(dk_ref)
            dv_ref[...] = jnp.zeros_like(dv_ref)

        lane = lax.broadcasted_iota(jnp.int32, (tq, 2 * V_HEAD), 1)
        do = do_ref[...]
        do_o = do * o_ref[...]
        chunks = _key_chunks(t)
        dvs = [None] * len(chunks)
        for j in range(2):
            sl = slice(j * HEAD_PAD, (j + 1) * HEAD_PAD)
            qh = q_ref[:, sl]
            in_head = (lane < V_HEAD) if j == 0 else (lane >= V_HEAD)
            inv_l = 1.0 / l_ref[:, j * V_HEAD:j * V_HEAD + 1]
            doh = jnp.where(in_head, do, 0.0).astype(BF16)
            doh_n = jnp.where(in_head, do * inv_l, 0.0).astype(BF16)
            delta = jnp.sum(jnp.where(in_head, do_o, 0.0), axis=-1, keepdims=True)
            row_scale = inv_l * (QK_HEAD ** -0.5)
            dq = jnp.zeros((tq, HEAD_PAD), F32)
            for c, (start, size) in enumerate(chunks):
                rows = slice(start, start + size)
                e = p_ref[0, j, :, rows]
                dp = _dot_nt(doh, v_ref[rows, :])
                ds = (e.astype(F32) * (dp - delta) * row_scale).astype(BF16)
                dq = dq + _dot(ds, k_ref[rows, sl])
                dk_ref[rows, sl] += _dot_tn(ds, qh)
                dvc = _dot_tn(e, doh_n)
                dvs[c] = dvc if dvs[c] is None else dvs[c] + dvc
            dq_ref[:, sl] = dq
        for (start, size), dvc in zip(chunks, dvs):
            dv_ref[start:start + size, :] += dvc
        finish()

    res = pl.pallas_call(
        body, name="attn_bwd", grid=(n // t, N_HEADS // 2, nq),
        in_specs=[qspec, kspec, vspec, ospec, ospec, ospec, _probs_spec(t, tq)] + c_in,
        out_specs=[qspec, kspec, vspec] + c_out,
        out_shape=[jax.ShapeDtypeStruct((n, QP_COLS), F32), jax.ShapeDtypeStruct((n, QP_COLS), F32),
                   jax.ShapeDtypeStruct((n, D_ATTN), F32)] + c_shape, scratch_shapes=c_sems,
        compiler_params=pltpu.CompilerParams(dimension_semantics=("arbitrary", "arbitrary", "arbitrary"),
                                             vmem_limit_bytes=VMEM_LIMIT),
    )(q, k, v, do, o, *probs, *srcs)
    return res[:3], res[3:]


SCAN_STEPS = 8


def _scan(chains, t):
    seg = t // 8
    rows = lax.broadcasted_iota(jnp.int32, (8, LANES), 0)

    def step(i, carry):
        carry = list(carry)
        for u in range(SCAN_STEPS):
            j = i * SCAN_STEPS + u
            for n, (a_ref, b_ref, h_ref, p_ref, reverse) in enumerate(chains):
                h, p = carry[n]
                idx = pl.ds(seg - 1 - j if reverse else j, 8, stride=seg)
                a = a_ref[idx, :]
                h = a * h + b_ref[idx, :]
                p = a * p
                h_ref[idx, :] = h
                p_ref[idx, :] = p
                carry[n] = (h, p)
        return tuple(carry)

    init = tuple((jnp.zeros((8, LANES), F32), jnp.ones((8, LANES), F32)) for _ in chains)
    ends = lax.fori_loop(0, seg // SCAN_STEPS, step, init)
    for (_, _, h_ref, p_ref, reverse), (b, a) in zip(chains, ends):
        for d in (1, 2, 4):
            if reverse:
                keep = rows < 8 - d
                a_n, b_n = pltpu.roll(a, 8 - d, 0), pltpu.roll(b, 8 - d, 0)
            else:
                keep = rows >= d
                a_n, b_n = pltpu.roll(a, d, 0), pltpu.roll(b, d, 0)
            b = a * jnp.where(keep, b_n, 0.0) + b
            a = a * jnp.where(keep, a_n, 1.0)
        for s in (range(7) if reverse else range(1, 8)):
            sl = slice(s * seg, (s + 1) * seg)
            carry_in = b[s + 1:s + 2, :] if reverse else b[s - 1:s, :]
            h_ref[sl, :] = h_ref[sl, :] + p_ref[sl, :] * carry_in


def _shift_rows(x, s, rows, t):
    if s == 0:
        return x
    rolled = pltpu.roll(x, s % t, 0)
    return jnp.where(rows >= s, rolled, 0.0) if s > 0 else jnp.where(rows < t + s, rolled, 0.0)


def _neg_expm1(x, exp_x):
    series = -x * (1.0 + x * (0.5 + x * (1.0 / 6 + x * (1.0 / 24))))
    return jnp.where(x > -0.1, series, 1.0 - exp_x)


def _sigmoid(x):
    return 0.5 * jnp.tanh(0.5 * x) + 0.5


def _gelu_parts(x):
    k = math.sqrt(2.0 / math.pi)
    th = jnp.tanh(k * (x + 0.044715 * x * x * x))
    g = 0.5 * x * (1.0 + th)
    dg = 0.5 * (1.0 + th) + 0.5 * x * (1.0 - th * th) * k * (1.0 + 3 * 0.044715 * x * x)
    return g, dg


def _lru_gates(xc, gates, lam_ref, valid, d):
    r = _sigmoid(gates[:, (2 * d) * LANES:(2 * d + 1) * LANES])
    i = _sigmoid(gates[:, (2 * d + 1) * LANES:(2 * d + 2) * LANES])
    neg_lam = -lam_ref[d:d + 1, :]
    sp = jnp.maximum(neg_lam, 0.0) + jnp.log1p(jnp.exp(-jnp.abs(neg_lam)))
    log_a = -LRU_C * r * sp
    a = jnp.exp(log_a)
    m = jnp.maximum(_neg_expm1(2.0 * log_a, a * a), 0.0)
    sq = jnp.sqrt(m)
    b = jnp.where(valid, sq * (i * xc), 0.0)
    return r, i, sp, a, m, sq, b


def _conv(xr, cw_ref, cb_ref, rows, t):
    return (cw_ref[0:1, :] * _shift_rows(xr, 2, rows, t) + cw_ref[1:2, :] * _shift_rows(xr, 1, rows, t)
            + cw_ref[2:3, :] * xr + cw_ref[3:4, :] * _shift_rows(xr, -1, rows, t) + cb_ref[...])


def _rnn_specs(t):
    seq = pl.BlockSpec((t, LANES), lambda cb, b: (b, cb))
    cw = pl.BlockSpec((4, LANES), lambda cb, b: (0, cb))
    vec1 = pl.BlockSpec((1, LANES), lambda cb, b: (0, cb))
    vec2 = pl.BlockSpec((2, LANES), lambda cb, b: (0, cb))
    wblk = pl.BlockSpec((1, LANES, 4 * LANES), lambda cb, b: (cb, 0, 0))
    gbias = pl.BlockSpec((1, 1, 4 * LANES), lambda cb, b: (cb, 0, 0))
    return seq, cw, vec1, vec2, wblk, gbias


def _rnn_fwd(xr, xg, conv_w, conv_b, wblk, gbias, lam):
    n = xr.shape[0]
    t = _t_pad()
    seq, cw, vec1, vec2, wspec, gspec = _rnn_specs(t)

    def body(xr_ref, xg_ref, cw_ref, cb_ref, w_ref, gb_ref, lam_ref, o_ref, a_s, b_s, h_s, p_s):
        rows = lax.broadcasted_iota(jnp.int32, (t, LANES), 0)
        valid = rows >= PAD_ROWS
        xc = _conv(xr_ref[...], cw_ref, cb_ref, rows, t)
        gates = _dot(xc.astype(BF16), w_ref[0]) + gb_ref[0]
        for d in range(2):
            _, _, _, a, _, _, b = _lru_gates(xc, gates, lam_ref, valid, d)
            a_s[d] = a
            b_s[d] = b
        _scan([(a_s.at[d], b_s.at[d], h_s.at[d], p_s.at[d], d == 1) for d in range(2)], t)
        g, _ = _gelu_parts(xg_ref[...])
        o_ref[...] = (h_s[0] + h_s[1]) * g

    return pl.pallas_call(
        body, name="rnn_fwd", grid=(D_RNN // LANES, n // t),
        in_specs=[seq, seq, cw, vec1, wspec, gspec, vec2], out_specs=seq,
        out_shape=jax.ShapeDtypeStruct((n, D_RNN), F32),
        scratch_shapes=[pltpu.VMEM((2, t, LANES), F32)] * 4,
        compiler_params=pltpu.CompilerParams(dimension_semantics=("parallel", "parallel"), vmem_limit_bytes=VMEM_LIMIT),
    )(xr, xg, conv_w, conv_b, wblk, gbias, lam)


def _rnn_bwd(xr, xg, do, conv_w, conv_b, wblk, gbias, lam, srcs=(), scatter=()):
    n = xr.shape[0]
    t = _t_pad()
    seq, cw, vec1, vec2, wspec, gspec = _rnn_specs(t)
    nk = len(srcs)
    c_in, c_out, c_shape, c_sems = _exchange_specs(srcs, scatter)

    def body(xr_ref, xg_ref, do_ref, cw_ref, cb_ref, w_ref, gb_ref, lam_ref, *rest):
        dxr_ref, dxg_ref, dcw_ref, dcb_ref, dw_ref, dgb_ref, dlam_ref = rest[nk:nk + 7]
        a_s, b_s, h_s, l_s, p_s, back_s, r_s, i_s, q_s, dg_s = rest[2 * nk + 7 + len(c_sems):]
        finish = _ride(2, *_exchange_fns(rest[:nk], rest[nk + 7:2 * nk + 7], rest[2 * nk + 7:2 * nk + 7 + len(c_sems)],
                                         scatter))
        first = pl.program_id(1) == 0
        rows = lax.broadcasted_iota(jnp.int32, (t, LANES), 0)
        valid = rows >= PAD_ROWS
        xr = xr_ref[...]
        xc = _conv(xr, cw_ref, cb_ref, rows, t)
        xcb = xc.astype(BF16)
        gates = _dot(xcb, w_ref[0]) + gb_ref[0]
        sps = []
        for d in range(2):
            r_s[d], i_s[d], sp, a_s[d], _, q_s[d], b_s[d] = _lru_gates(xc, gates, lam_ref, valid, d)
            sps.append(sp)
        _scan([(a_s.at[d], b_s.at[d], h_s.at[d], p_s.at[d], d == 1) for d in range(2)], t)
        g, dg = _gelu_parts(xg_ref[...])
        do = do_ref[...]
        dxg_ref[...] = do * (h_s[0] + h_s[1]) * dg
        b_s[0] = do * g
        for d in range(2):
            back_s[d] = _shift_rows(a_s[d], -1 if d == 0 else 1, rows, t)
        _scan([(back_s.at[d], b_s.at[0], l_s.at[d], p_s.at[d], d == 0) for d in range(2)], t)
        dxc = jnp.zeros((t, LANES), F32)
        dlams = []
        for d in range(2):
            r, i, sp, a, sq = r_s[d], i_s[d], sps[d], a_s[d], q_s[d]
            lam_t = l_s[d]
            da = lam_t * _shift_rows(h_s[d], 1 if d == 0 else -1, rows, t)
            lam_v = jnp.where(valid, lam_t, 0.0)
            dsq = lam_v * (i * xc)
            di = lam_v * sq * xc
            dxc = dxc + lam_v * sq * i
            dm = jnp.where(sq > 0.0, dsq * 0.5 / jnp.where(sq > 0.0, sq, 1.0), 0.0)
            dla = da * a - 2.0 * dm * a * a
            dr = dla * (-LRU_C) * sp
            dsp = _colsum(dla * (-LRU_C) * r)
            dlams.append(dsp * -jax.nn.sigmoid(-lam_ref[d:d + 1, :]))
            dg_s[:, (2 * d) * LANES:(2 * d + 1) * LANES] = (dr * r * (1.0 - r)).astype(BF16)
            dg_s[:, (2 * d + 1) * LANES:(2 * d + 2) * LANES] = (di * i * (1.0 - i)).astype(BF16)
        dgates = dg_s[...]
        dxc = dxc + _dot_nt(dgates, w_ref[0])
        taps = [_shift_rows(dxc, j - 2, rows, t) for j in range(4)]
        dxr_ref[...] = (cw_ref[0:1, :] * taps[0] + cw_ref[1:2, :] * taps[1] + cw_ref[2:3, :] * taps[2]
                        + cw_ref[3:4, :] * taps[3])
        dcw = jnp.concatenate([_colsum(tap * xr) for tap in taps], axis=0)
        _acc(dcw_ref, first, dcw)
        _acc(dcb_ref, first, _colsum(dxc))
        _acc(dw_ref, first, _dot_tn(xcb, dgates)[None])
        _acc(dgb_ref, first, _colsum(dgates.astype(F32))[None])
        _acc(dlam_ref, first, jnp.concatenate(dlams, axis=0))
        finish()

    res = pl.pallas_call(
        body, name="rnn_bwd", grid=(D_RNN // LANES, n // t),
        in_specs=[seq, seq, seq, cw, vec1, wspec, gspec, vec2] + c_in,
        out_specs=[seq, seq, cw, vec1, wspec, gspec, vec2] + c_out,
        out_shape=[jax.ShapeDtypeStruct((n, D_RNN), F32), jax.ShapeDtypeStruct((n, D_RNN), F32),
                   jax.ShapeDtypeStruct((4, D_RNN), F32), jax.ShapeDtypeStruct((1, D_RNN), F32),
                   jax.ShapeDtypeStruct((D_RNN // LANES, LANES, 4 * LANES), F32),
                   jax.ShapeDtypeStruct((D_RNN // LANES, 1, 4 * LANES), F32), jax.ShapeDtypeStruct((2, D_RNN), F32)]
        + c_shape,
        scratch_shapes=c_sems + [pltpu.VMEM((2, t, LANES), F32)] * 9 + [pltpu.VMEM((t, 4 * LANES), BF16)],
        compiler_params=pltpu.CompilerParams(dimension_semantics=("arbitrary", "arbitrary"), vmem_limit_bytes=VMEM_LIMIT),
    )(xr, xg, do, conv_w, conv_b, wblk, gbias, lam, *srcs)
    return res[:7], res[7:]


def _post(oa, orn, h0, tgt, ga, gr, g2, w_out, w_gate, w_up, w_down):
    n = oa.shape[0]
    tm = _row_tile(n)
    t = _t_pad()

    def body(oa_ref, or_ref, h0_ref, tgt_ref, ga_ref, gr_ref, g2_ref, wo_ref, wg_ref, wu_ref, wd_ref,
             doa_ref, dor_ref, dh1_ref, mix_ref, h1n_ref, act_ref, dgate_ref, dup_ref, dy_ref,
             loss_ref, dga_ref, dgr_ref, dg2_ref, gate_s, up_s):
        first = pl.program_id(0) == 0
        xa, ra = _rms(oa_ref[...], D_ATTN)
        xr, rr = _rms(or_ref[...], D_RNN)
        mix = jnp.concatenate([(xa * ga_ref[...]).astype(BF16), (xr * gr_ref[...]).astype(BF16)], axis=-1)
        mix_ref[...] = mix.T
        h1 = h0_ref[...] + _dot(mix, wo_ref[...])
        x2, r2 = _rms(h1, D_MODEL)
        h1n = (x2 * g2_ref[...]).astype(BF16)
        h1n_ref[...] = h1n
        y = h1
        for cs in range(0, D_FF, FF_CHUNK):
            sl = slice(cs, cs + FF_CHUNK)
            gate = _dot_nt(h1n, wg_ref[sl, :])
            up = _dot_nt(h1n, wu_ref[sl, :])
            gate_s[:, sl] = gate
            up_s[:, sl] = up
            act = (gate * _sigmoid(gate) * up).astype(BF16)
            act_ref[sl, :] = act.T
            y = y + _dot(act, wd_ref[sl, :])
        row = pl.program_id(0) * tm + lax.broadcasted_iota(jnp.int32, (tm, 1), 0)
        for _ in range(1, n // t):
            row = jnp.where(row >= t, row - t, row)
        err = jnp.where(row >= PAD_ROWS + N_META, y - tgt_ref[...], 0.0)
        _acc(loss_ref, first, jnp.full((1, LANES), 0.5 / D_MODEL, F32) * jnp.sum(err * err))
        dy = err * (1.0 / D_MODEL)
        dyb = dy.astype(BF16)
        dy_ref[...] = dyb
        dh1n = jnp.zeros((tm, D_MODEL), F32)
        for cs in range(0, D_FF, FF_CHUNK):
            sl = slice(cs, cs + FF_CHUNK)
            dact = _dot_nt(dyb, wd_ref[sl, :])
            gate, up = gate_s[:, sl], up_s[:, sl]
            sg = _sigmoid(gate)
            dgate = (dact * up * sg * (1.0 + gate * (1.0 - sg))).astype(BF16)
            dup = (dact * gate * sg).astype(BF16)
            dgate_ref[sl, :] = dgate.T
            dup_ref[sl, :] = dup.T
            dh1n = dh1n + _dot(dgate, wg_ref[sl, :]) + _dot(dup, wu_ref[sl, :])
        _acc(dg2_ref, first, _colsum(dh1n * x2))
        dh1 = dy + _rms_bwd(dh1n, x2, r2, g2_ref[...], D_MODEL)
        dh1_ref[...] = dh1
        dmix = _dot_nt(dh1.astype(BF16), wo_ref[...])
        dma, dmr = dmix[:, :D_ATTN], dmix[:, D_ATTN:]
        _acc(dga_ref, first, _colsum(dma * xa))
        _acc(dgr_ref, first, _colsum(dmr * xr))
        doa_ref[...] = _rms_bwd(dma, xa, ra, ga_ref[...], D_ATTN)
        dor_ref[...] = _rms_bwd(dmr, xr, rr, gr_ref[...], D_RNN)

    def row(w):
        return pl.BlockSpec((tm, w), lambda i: (i, 0))

    def acc(w):
        return pl.BlockSpec((1, w), lambda i: (0, 0))

    def col(w):
        return pl.BlockSpec((w, tm), lambda i: (0, i))

    outs = [(D_ATTN, F32, row), (D_RNN, F32, row), (D_MODEL, F32, row), (D_MODEL, BF16, col), (D_MODEL, BF16, row),
            (D_FF, BF16, col), (D_FF, BF16, col), (D_FF, BF16, col), (D_MODEL, BF16, row)]
    accs = [LANES, D_ATTN, D_RNN, D_MODEL]
    return pl.pallas_call(
        body, name="post", grid=(n // tm,),
        in_specs=[row(D_ATTN), row(D_RNN), row(D_MODEL), row(D_MODEL),
                  _const_spec((1, D_ATTN)), _const_spec((1, D_RNN)), _const_spec((1, D_MODEL)),
                  _const_spec((D_MODEL, D_MODEL)), _const_spec((D_FF, D_MODEL)), _const_spec((D_FF, D_MODEL)),
                  _const_spec((D_FF, D_MODEL))],
        out_specs=[spec(w) for w, _, spec in outs] + [acc(w) for w in accs],
        out_shape=[jax.ShapeDtypeStruct((n, w) if spec is row else (w, n), dt) for w, dt, spec in outs]
        + [jax.ShapeDtypeStruct((1, w), F32) for w in accs],
        scratch_shapes=[pltpu.VMEM((tm, D_FF), F32), pltpu.VMEM((tm, D_FF), F32)],
        compiler_params=pltpu.CompilerParams(dimension_semantics=("arbitrary",), vmem_limit_bytes=VMEM_LIMIT),
    )(oa, orn, h0, tgt, ga, gr, g2, w_out, w_gate, w_up, w_down)


def _in_bwd(dp, h0, dh1, ln1_g, w_in_p, srcs=(), scatter=()):
    n = h0.shape[0]
    tm = _row_tile(n)
    nk = len(srcs)
    c_in, c_out, c_shape, c_sems = _exchange_specs(srcs, scatter)

    def body(dp_ref, h0_ref, dh1_ref, g_ref, w_ref, *rest):
        dh0_ref, dg_ref = rest[nk:nk + 2]
        finish = _ride(1, *_exchange_fns(rest[:nk], rest[nk + 2:2 * nk + 2], rest[2 * nk + 2:], scatter))
        dhn = _dot(dp_ref[...], w_ref[...])
        xhat, r = _rms(h0_ref[...], D_MODEL)
        _acc(dg_ref, pl.program_id(0) == 0, _colsum(dhn * xhat))
        dh0_ref[...] = dh1_ref[...] + _rms_bwd(dhn, xhat, r, g_ref[...], D_MODEL)
        finish()

    def row(w):
        return pl.BlockSpec((tm, w), lambda i: (i, 0))

    res = pl.pallas_call(
        body, name="in_bwd", grid=(n // tm,),
        in_specs=[row(P_COLS), row(D_MODEL), row(D_MODEL), _const_spec((1, D_MODEL)), _const_spec((P_COLS, D_MODEL))] + c_in,
        out_specs=[row(D_MODEL), pl.BlockSpec((1, D_MODEL), lambda i: (0, 0))] + c_out,
        out_shape=[jax.ShapeDtypeStruct((n, D_MODEL), F32), jax.ShapeDtypeStruct((1, D_MODEL), F32)] + c_shape,
        scratch_shapes=c_sems,
        compiler_params=pltpu.CompilerParams(dimension_semantics=("arbitrary",), vmem_limit_bytes=VMEM_LIMIT),
    )(dp, h0, dh1, ln1_g, w_in_p, *srcs)
    return res[:2], res[2:]


MAX_TILE = D_FF // 2


def _pick_tile(width, cap):
    best = LANES
    for mult in range(1, width // LANES + 1):
        cand = mult * LANES
        if width % cand == 0 and cand <= cap:
            best = cand
    return best


def _matmul_tn(name, a, b, srcs=(), scatter=()):
    n, ka = a.shape
    kb = b.shape[1]
    ta, tb = _pick_tile(ka, MAX_TILE), _pick_tile(kb, MAX_TILE)
    tk = n // 4
    nk = len(srcs)
    c_in, c_out, c_shape, c_sems = _exchange_specs(srcs, scatter)

    def body(a_ref, b_ref, *rest):
        o_ref = rest[nk]
        finish = _ride(3, *_exchange_fns(rest[:nk], rest[nk + 1:2 * nk + 1], rest[2 * nk + 1:], scatter))
        _acc(o_ref, pl.program_id(2) == 0, _dot_tn(a_ref[...].astype(BF16), b_ref[...].astype(BF16)))
        finish()

    res = pl.pallas_call(
        body, name=name, grid=(ka // ta, kb // tb, n // tk),
        in_specs=[pl.BlockSpec((tk, ta), lambda i, j, k: (k, i)), pl.BlockSpec((tk, tb), lambda i, j, k: (k, j))] + c_in,
        out_specs=[pl.BlockSpec((ta, tb), lambda i, j, k: (i, j))] + c_out,
        out_shape=[jax.ShapeDtypeStruct((ka, kb), F32)] + c_shape, scratch_shapes=c_sems,
        compiler_params=pltpu.CompilerParams(dimension_semantics=("arbitrary", "arbitrary", "arbitrary"),
                                             vmem_limit_bytes=VMEM_LIMIT),
    )(a, b, *srcs)
    return res[0], res[1:]


def _matmul_shards(name, at, b):
    ka, n = at.shape
    kb = b.shape[1]
    ta, tb = _pick_tile(ka, MAX_TILE), _pick_tile(kb, MAX_TILE)
    width = ka // N_DEV
    per = ta // width

    def body(a_ref, b_ref, o_ref):
        out = _dot(a_ref[...], b_ref[...].astype(BF16))
        for s in range(per):
            o_ref[s] = out[s * width:(s + 1) * width, :].astype(BF16)

    return pl.pallas_call(
        body, name=name, grid=(ka // ta, kb // tb),
        in_specs=[pl.BlockSpec((ta, n), lambda i, j: (i, 0)), pl.BlockSpec((n, tb), lambda i, j: (0, j))],
        out_specs=pl.BlockSpec((per, width, tb), lambda i, j: (i, 0, j)),
        out_shape=jax.ShapeDtypeStruct((N_DEV, width, kb), BF16),
        compiler_params=pltpu.CompilerParams(dimension_semantics=("parallel", "parallel"), vmem_limit_bytes=VMEM_LIMIT),
    )(at, b)


def _adamw_math(g8_ref, w_ref, m_ref, v_ref, g_ref, d_ref, nm_ref, nv_ref):
    g = g8_ref[0].astype(F32)
    for s in range(1, N_DEV):
        g = g + g8_ref[s].astype(F32)
    g_ref[...] = g
    nm = ADAM_B1 * m_ref[...] + (1.0 - ADAM_B1) * g
    nv = ADAM_B2 * v_ref[...] + (1.0 - ADAM_B2) * (g * g)
    nm_ref[...] = nm
    nv_ref[...] = nv
    m_hat = nm / (1.0 - ADAM_B1 ** ADAM_STEP)
    v_hat = nv / (1.0 - ADAM_B2 ** ADAM_STEP)
    d_ref[...] = -ADAM_LR * (m_hat / (jnp.sqrt(v_hat) + ADAM_EPS) + ADAM_WD * w_ref[...])


def _adamw_many(name, items):
    count = len(items)

    def body(*refs):
        ins, outs = refs[:4 * count], refs[4 * count:]
        for i in range(count):
            _adamw_math(*ins[4 * i:4 * i + 4], *outs[4 * i:4 * i + 4])

    flat = [a for item in items for a in item]
    res = pl.pallas_call(
        body, name=name,
        out_shape=[jax.ShapeDtypeStruct(item[1].shape, F32) for item in items for _ in range(4)],
        compiler_params=pltpu.CompilerParams(vmem_limit_bytes=VMEM_LIMIT),
    )(*flat)
    return [tuple(res[4 * i:4 * i + 4]) for i in range(count)]


def _adamw(name, g8, w, m, v):
    rows, cols = w.shape
    tr = rows
    for cand in (256, 176, 128, 64):
        if rows % cand == 0 and rows > cand:
            tr = cand
            break

    def body(*refs):
        _adamw_math(*refs)

    blk = pl.BlockSpec((tr, cols), lambda i: (i, 0))
    return pl.pallas_call(
        body, name=name, grid=(rows // tr,),
        in_specs=[pl.BlockSpec((N_DEV, tr, cols), lambda i: (0, i, 0)), blk, blk, blk],
        out_specs=[blk] * 4, out_shape=[jax.ShapeDtypeStruct((rows, cols), F32)] * 4,
        compiler_params=pltpu.CompilerParams(dimension_semantics=("parallel",), vmem_limit_bytes=VMEM_LIMIT),
    )(g8, w, m, v)


def _exchange_specs(srcs, scatter):
    nk = len(srcs)
    if not nk:
        return [], [], [], []
    any_spec = pl.BlockSpec(memory_space=pl.ANY)
    out_shape = [jax.ShapeDtypeStruct(s.shape if sc else (N_DEV,) + s.shape, s.dtype) for s, sc in zip(srcs, scatter)]
    sems = [pltpu.SemaphoreType.DMA((nk, N_DEV - 1)), pltpu.SemaphoreType.DMA((nk, N_DEV - 1)),
            pltpu.SemaphoreType.DMA((nk,))]
    return [any_spec] * nk, [any_spec] * nk, out_shape, sems


FLIPS = ((0, 0, 1), (1, 0, 0), (0, 1, 0), (1, 1, 0), (1, 0, 1), (0, 1, 1), (1, 1, 1))
N_CHIP_PEERS = 3


def _exchange_fns(src_refs, out_refs, sems, scatter):
    nk = len(src_refs)
    if not nk:
        return (lambda: None), (lambda: None), (lambda: None)
    send_sems, recv_sems, local_sems = sems
    first = 1 + N_CHIP_PEERS

    def plan():
        x, y, c = lax.axis_index("x"), lax.axis_index("y"), lax.axis_index("c")
        me = 4 * x + 2 * y + c
        peers = [(1 - x if fx else x, 1 - y if fy else y, 1 - c if fc else c) for fx, fy, fc in FLIPS]
        pids = [4 * px + 2 * py + pc for px, py, pc in peers]

        def remote(k, j, src, dst, to):
            return pltpu.make_async_remote_copy(src_ref=src, dst_ref=dst, send_sem=send_sems.at[k, j],
                                                recv_sem=recv_sems.at[k, j], device_id=to, device_id_type=MESH)

        def mine(k, dest):
            return src_refs[k].at[dest] if scatter[k] else src_refs[k]

        local = [pltpu.make_async_copy(mine(k, me), out_refs[k].at[me], local_sems.at[k]) for k in range(nk)]
        direct = [remote(k, j, mine(k, pids[j]), out_refs[k].at[me], peers[j])
                  for k in range(nk) for j in range(len(FLIPS) if scatter[k] else first)]
        relays = {(k, j): remote(k, j, out_refs[k].at[pids[j - N_CHIP_PEERS]], out_refs[k].at[pids[j - N_CHIP_PEERS]], peers[0])
                  for k in range(nk) if not scatter[k] for j in range(first, len(FLIPS))}
        arrivals = {(k, j): remote(k, j, out_refs[k].at[pids[j]], out_refs[k].at[pids[j]], peers[j])
                    for k in range(nk) for j in range(len(FLIPS))}
        return local, direct, relays, arrivals

    def start():
        local, direct, _, _ = plan()
        for cp in local + direct:
            cp.start()

    def relay():
        _, _, relays, arrivals = plan()
        for (k, j), cp in relays.items():
            arrivals[k, j - N_CHIP_PEERS].wait_recv()
            cp.start()

    def wait():
        local, direct, relays, arrivals = plan()
        for (k, j), cp in arrivals.items():
            if (k, j + N_CHIP_PEERS) not in relays:
                cp.wait_recv()
        for cp in direct + list(relays.values()):
            cp.wait_send()
        for cp in local:
            cp.wait()

    return start, relay, wait


def _grid_step(rank):
    step, total = 0, 1
    for axis in range(rank):
        step = step * pl.num_programs(axis) + pl.program_id(axis)
        total = total * pl.num_programs(axis)
    return step, total


def _ride(rank, start, relay, wait):
    step, total = _grid_step(rank)
    pl.when(step == 0)(start)
    pl.when(step == (3 * total) // 4)(relay)
    return lambda: pl.when(step == total - 1)(wait)


def _exchange(name, srcs, scatter):
    nk = len(srcs)
    c_in, c_out, c_shape, c_sems = _exchange_specs(srcs, scatter)

    def body(*refs):
        start, relay, wait = _exchange_fns(refs[:nk], refs[nk:2 * nk], refs[2 * nk:], scatter)
        start()
        relay()
        wait()

    return pl.pallas_call(body, name=name, in_specs=c_in, out_specs=c_out, out_shape=c_shape, scratch_shapes=c_sems)(*srcs)


def _cols_from_shards(g):
    return jnp.transpose(g, (1, 0, 2)).reshape(g.shape[1], -1)


def _cols_to_shards(w):
    return jnp.transpose(w.reshape(w.shape[0], N_DEV, -1), (1, 0, 2))


def _prep(x, tgt, srcs, scatter):
    nb = x.shape[0]
    t = _t_pad()
    head = PAD_ROWS + N_META
    nk = len(srcs)
    c_in, c_out, c_shape, c_sems = _exchange_specs(srcs, scatter)

    def body(x_ref, tgt_ref, *rest):
        h0_ref, tp_ref = rest[nk:nk + 2]
        finish = _ride(1, *_exchange_fns(rest[:nk], rest[nk + 2:2 * nk + 2], rest[2 * nk + 2:], scatter))
        lead = pl.program_id(0) == 0

        @pl.when(lead)
        def _():
            h0_ref[...] = jnp.zeros_like(h0_ref)
            tp_ref[...] = jnp.zeros_like(tp_ref)

        @pl.when(jnp.logical_not(lead))
        def _():
            h0_ref[...] = x_ref[...]
            tp_ref[...] = tgt_ref[...]

        finish()

    src = pl.BlockSpec((nb, head, D_MODEL), lambda j: (0, jnp.maximum(j - 1, 0), 0))
    dst = pl.BlockSpec((nb, head, D_MODEL), lambda j: (0, j, 0))
    padded = jax.ShapeDtypeStruct((nb, t, D_MODEL), F32)
    res = pl.pallas_call(
        body, name="prep", grid=(t // head,), in_specs=[src, src] + c_in, out_specs=[dst, dst] + c_out,
        out_shape=[padded, padded] + c_shape, scratch_shapes=c_sems,
        compiler_params=pltpu.CompilerParams(dimension_semantics=("arbitrary",)),
    )(x, tgt, *srcs)
    return res[0], res[1], res[2:]


def _rope_tables(n):
    t = _t_pad()
    pos = np.arange(t, dtype=np.float32) - np.float32(PAD_ROWS)
    half = QK_ROPE // 2
    freqs = (1.0 / (ROPE_THETA ** (np.arange(half, dtype=np.float32) / half))).astype(np.float32)
    ang = pos[:, None] * freqs[None, :]
    cos, sin = np.cos(ang), np.sin(ang)
    z = lambda w: np.zeros((t, w), np.float32)
    c = np.concatenate([np.ones((t, QK_NOPE), np.float32), cos, cos, z(HEAD_PAD - QK_HEAD)], axis=1)
    s1 = np.concatenate([z(QK_NOPE + half), sin, z(HEAD_PAD - QK_HEAD)], axis=1)
    s2 = np.concatenate([z(QK_NOPE), -sin, z(HEAD_PAD - QK_NOPE - half)], axis=1)
    return tuple(jnp.asarray(np.tile(a, (n // t, 1))) for a in (c, s1, s2))


def _block_diag_gates(lru_wa, lru_wi):
    eye = jnp.eye(2, dtype=lru_wa.dtype)

    def bd(w):
        w = w.reshape(2, D_RNN // LANES, 2, RNN_BW, RNN_BW)
        full = w[:, :, :, :, None, :] * eye[None, None, :, None, :, None]
        return full.reshape(2, D_RNN // LANES, LANES, LANES)

    a, i = bd(lru_wa), bd(lru_wi)
    return jnp.concatenate([a[0], i[0], a[1], i[1]], axis=-1)


def _unblock_gates(dw):
    nb = D_RNN // LANES
    parts = dw.reshape(nb, 2, RNN_BW, 4, 2, RNN_BW)
    diag = jnp.stack([parts[:, k, :, :, k, :] for k in range(2)], axis=1)
    diag = jnp.transpose(diag, (3, 0, 1, 2, 4)).reshape(4, 2 * nb, RNN_BW, RNN_BW)
    return jnp.stack([diag[0], diag[2]]), jnp.stack([diag[1], diag[3]])


WEIGHTS = ("meta_tokens", "ln1_g", "w_in", "q_a_norm_g", "w_uq", "kv_a_norm_g", "w_ukv", "q_norm_g", "k_norm_g",
           "conv_w", "conv_b", "lru_wa", "lru_ba", "lru_wi", "lru_bi", "lru_lambda", "attn_out_g", "rnn_out_g",
           "w_out", "ln2_g", "w_gate", "w_up", "w_down")
BIG = ("w_in", "w_uq", "w_ukv", "w_out", "w_gate", "w_up", "w_down")
TRANSPOSED = ("w_in", "w_uq", "w_gate", "w_up")
ROW_SHARDED = ("w_out", "w_down") + TRANSPOSED
REPLICATED = ("ln1_g", "q_a_norm_g", "kv_a_norm_g", "q_norm_g", "k_norm_g", "conv_b", "lru_wa", "lru_wi",
              "attn_out_g", "rnn_out_g", "ln2_g")
WHOLE = REPLICATED + ("loss",)
G_FIRST = ("w_in", "meta_tokens")
G_MID = ("w_uq", "w_ukv", "conv_w", "lru_ba", "lru_bi", "lru_lambda")
LATE = ("w_out", "w_gate", "w_up", "w_down")
G_LAST = ("meta_tokens", "ln1_g")


def _local_step(x, tgt, ex):
    nb = x.shape[0]
    t = _t_pad()
    n = nb * t
    local = ex.local
    h0, tgt_p, got = _prep(x, tgt, *ex.gather_srcs(G_FIRST))
    first = ex.gathered(G_FIRST, got)
    meta, w_in = first["meta_tokens"], first["w_in"]
    h0 = h0.at[:, PAD_ROWS:PAD_ROWS + N_META].set(jnp.broadcast_to(meta[None], (nb, N_META, D_MODEL))).reshape(n, D_MODEL)
    tgt_p = tgt_p.reshape(n, D_MODEL)

    zr = lambda r: jnp.zeros((r, D_MODEL), w_in.dtype)
    w_in_p = jnp.concatenate([w_in[:OFF_CKV], w_in[OFF_KR:], zr(QK_NOPE), w_in[OFF_CKV:OFF_KR], zr(HEAD_PAD - QK_HEAD)],
                             axis=0)
    pad_g = lambda g: jnp.pad(g, ((0, 0), (0, HEAD_PAD - QK_HEAD)))
    qg, kg = pad_g(local["q_norm_g"]), pad_g(local["k_norm_g"])
    rc, rs1, rs2 = _rope_tables(n)
    wblk = _block_diag_gates(local["lru_wa"].reshape(2, -1, RNN_BW, RNN_BW),
                             local["lru_wi"].reshape(2, -1, RNN_BW, RNN_BW)).astype(BF16)
    nblk = D_RNN // LANES

    (hn, cq, ckv, xr, xg, kr), got = _in_proj(h0, local["ln1_g"], w_in_p, *ex.gather_srcs(G_MID))
    w = ex.gathered(G_MID, got)
    w_uq_p = jnp.pad(w["w_uq"].reshape(N_HEADS, QK_HEAD, Q_LORA), ((0, 0), (0, HEAD_PAD - QK_HEAD), (0, 0))
                     ).reshape(QP_COLS, Q_LORA)
    ukv = w["w_ukv"].reshape(KV_LORA, N_HEADS, QK_NOPE + V_HEAD)
    w_uk_p = jnp.pad(ukv[:, :, :QK_NOPE], ((0, 0), (0, 0), (0, HEAD_PAD - QK_NOPE))).reshape(KV_LORA, QP_COLS)
    w_v = ukv[:, :, QK_NOPE:].reshape(KV_LORA, D_ATTN)
    gbias = jnp.stack([w["lru_ba"][0], w["lru_bi"][0], w["lru_ba"][1], w["lru_bi"][1]], axis=0)
    gbias = jnp.transpose(gbias.reshape(4, nblk, LANES), (1, 0, 2)).reshape(nblk, 1, 4 * LANES)

    q, k, v = _qkv_fwd(cq, ckv, kr, local["q_a_norm_g"], local["kv_a_norm_g"], w_uq_p, w_uk_p, w_v, qg, kg, rc, rs1, rs2)
    oa, probs, got = _attn_fwd(q, k, v, *ex.gather_srcs(LATE))
    late = ex.gathered(LATE, got)
    orn = _rnn_fwd(xr, xg, w["conv_w"], local["conv_b"], wblk, gbias, w["lru_lambda"])
    (doa, dor, dh1, mix_t, h1n, act_t, dgate_t, dup_t, dyb, loss, dga, dgr, dg2) = _post(
        oa, orn, h0, tgt_p, local["attn_out_g"], local["rnn_out_g"], local["ln2_g"], late["w_out"], late["w_gate"],
        late["w_up"], late["w_down"])
    wire = {"w_out": _matmul_shards("dw_out", mix_t, dh1), "w_gate": _matmul_shards("dw_gate", dgate_t, h1n),
            "w_up": _matmul_shards("dw_up", dup_t, h1n), "w_down": _matmul_shards("dw_down", act_t, dyb)}
    names = ("w_out", "w_gate")
    (dxr, dxg, dcw, dcb, dwblk, dgb, dlam), got = _rnn_bwd(xr, xg, dor, w["conv_w"], local["conv_b"], wblk, gbias,
                                                           w["lru_lambda"], *ex.scatter_srcs(names, wire))
    summed = ex.scattered(names, wire, got)
    dwa, dwi = _unblock_gates(dwblk)
    dgb = jnp.transpose(dgb.reshape(nblk, 4, LANES), (1, 0, 2)).reshape(4, D_RNN)
    names = ("w_up", "w_down")
    (dq_r, dk_r, dv), got = _attn_bwd(q, k, v, doa, oa, probs, *ex.scatter_srcs(names, wire))
    summed.update(ex.scattered(names, wire, got))
    wire = ex.to_wire({
        "conv_w": dcw, "conv_b": dcb, "lru_wa": dwa.reshape(-1, RNN_BW), "lru_ba": jnp.stack([dgb[0], dgb[2]]),
        "lru_wi": dwi.reshape(-1, RNN_BW), "lru_bi": jnp.stack([dgb[1], dgb[3]]), "lru_lambda": dlam,
        "attn_out_g": dga, "rnn_out_g": dgr, "ln2_g": dg2, "loss": loss})
    names = tuple(wire)
    (dp, qa, kva, dqp, dkv, dqg, dkg, dgqa, dgkva), got = _qkv_bwd(
        cq, ckv, kr, dq_r, dk_r, dv, dxr, dxg, local["q_a_norm_g"], local["kv_a_norm_g"], w_uq_p, w_uk_p, w_v, qg, kg,
        rc, rs1, rs2, *ex.scatter_srcs(names, wire))
    summed.update(ex.scattered(names, wire, got))
    dw_uq_p, _ = _matmul_tn("dw_uq", dqp, qa)
    dw_kv, _ = _matmul_tn("dw_ukv", kva, dkv)
    dw_uq = dw_uq_p.reshape(N_HEADS, HEAD_PAD, Q_LORA)[:, :QK_HEAD].reshape(N_HEADS * QK_HEAD, Q_LORA)
    dw_ukv = jnp.concatenate([dw_kv[:, :QP_COLS].reshape(KV_LORA, N_HEADS, HEAD_PAD)[:, :, :QK_NOPE],
                              dw_kv[:, QP_COLS:].reshape(KV_LORA, N_HEADS, V_HEAD)], axis=2).reshape(KV_LORA, -1)
    wire = ex.to_wire({"q_a_norm_g": dgqa, "w_uq": dw_uq, "kv_a_norm_g": dgkva, "w_ukv": dw_ukv,
                       "q_norm_g": dqg[:, :QK_HEAD], "k_norm_g": dkg[:, :QK_HEAD]})
    names = tuple(wire)
    dw_in_p, got = _matmul_tn("dw_in", dp, hn, *ex.scatter_srcs(names, wire))
    summed.update(ex.scattered(names, wire, got))
    kr0 = OFF_CKV + 2 * D_RNN + QK_NOPE
    dw_in = jnp.concatenate([dw_in_p[:OFF_CKV], dw_in_p[kr0:kr0 + QK_ROPE], dw_in_p[OFF_CKV:OFF_CKV + 2 * D_RNN]], axis=0)
    wire = ex.to_wire({"w_in": dw_in})
    (dh0, dg1), got = _in_bwd(dp, h0, dh1, local["ln1_g"], w_in_p, *ex.scatter_srcs(("w_in",), wire))
    summed.update(ex.scattered(("w_in",), wire, got))

    dh0 = dh0.reshape(nb, t, D_MODEL)
    wire = ex.to_wire({"meta_tokens": jnp.sum(dh0[:, PAD_ROWS:PAD_ROWS + N_META], axis=0), "ln1_g": dg1})
    got = ex.run("reduce_last", *ex.scatter_srcs(G_LAST, wire))
    summed.update(ex.scattered(G_LAST, wire, got))
    return dh0[:, PAD_ROWS + N_META:], summed


class _MeshExchange:
    def __init__(self, shards):
        self.local = shards

    @staticmethod
    def run(name, srcs, scatter):
        return _exchange(name, srcs, scatter)

    def gather_srcs(self, names):
        return [self.local[k].astype(BF16) if k in BIG else self.local[k] for k in names], [False] * len(names)

    @staticmethod
    def gathered(names, outs):
        return {k: g.reshape(-1, g.shape[-1]) if k in ROW_SHARDED else _cols_from_shards(g) for k, g in zip(names, outs)}

    @staticmethod
    def to_wire(grads):
        wire = {}
        for k, g in grads.items():
            if k in WHOLE:
                wire[k] = g
            elif k in ROW_SHARDED:
                wire[k] = g.reshape(N_DEV, -1, g.shape[-1]).astype(BF16)
            else:
                wire[k] = _cols_to_shards(g).astype(BF16) if k in BIG else _cols_to_shards(g)
        return wire

    @staticmethod
    def scatter_srcs(names, wire):
        return [wire[k] for k in names], [k not in WHOLE for k in names]

    @staticmethod
    def scattered(names, wire, outs):
        return dict(zip(names, outs))


def kernel(x, meta_tokens, ln1_g, w_in, q_a_norm_g, w_uq, kv_a_norm_g, w_ukv, q_norm_g, k_norm_g, conv_w, conv_b, lru_wa, lru_ba, lru_wi, lru_bi, lru_lambda, attn_out_g, rnn_out_g, w_out, ln2_g, w_gate, w_up, w_down, loss_target, m_meta_tokens, m_ln1_g, m_w_in, m_q_a_norm_g, m_w_uq, m_kv_a_norm_g, m_w_ukv, m_q_norm_g, m_k_norm_g, m_conv_w, m_conv_b, m_lru_wa, m_lru_ba, m_lru_wi, m_lru_bi, m_lru_lambda, m_attn_out_g, m_rnn_out_g, m_w_out, m_ln2_g, m_w_gate, m_w_up, m_w_down, v_meta_tokens, v_ln1_g, v_w_in, v_q_a_norm_g, v_w_uq, v_kv_a_norm_g, v_w_ukv, v_q_norm_g, v_k_norm_g, v_conv_w, v_conv_b, v_lru_wa, v_lru_ba, v_lru_wi, v_lru_bi, v_lru_lambda, v_attn_out_g, v_rnn_out_g, v_w_out, v_ln2_g, v_w_gate, v_w_up, v_w_down):
    given = (meta_tokens, ln1_g, w_in, q_a_norm_g, w_uq, kv_a_norm_g, w_ukv, q_norm_g, k_norm_g, conv_w, conv_b,
             lru_wa, lru_ba, lru_wi, lru_bi, lru_lambda, attn_out_g, rnn_out_g, w_out, ln2_g, w_gate, w_up, w_down)
    moments_m = (m_meta_tokens, m_ln1_g, m_w_in, m_q_a_norm_g, m_w_uq, m_kv_a_norm_g, m_w_ukv, m_q_norm_g, m_k_norm_g,
                 m_conv_w, m_conv_b, m_lru_wa, m_lru_ba, m_lru_wi, m_lru_bi, m_lru_lambda, m_attn_out_g, m_rnn_out_g,
                 m_w_out, m_ln2_g, m_w_gate, m_w_up, m_w_down)
    moments_v = (v_meta_tokens, v_ln1_g, v_w_in, v_q_a_norm_g, v_w_uq, v_kv_a_norm_g, v_w_ukv, v_q_norm_g, v_k_norm_g,
                 v_conv_w, v_conv_b, v_lru_wa, v_lru_ba, v_lru_wi, v_lru_bi, v_lru_lambda, v_attn_out_g, v_rnn_out_g,
                 v_w_out, v_ln2_g, v_w_gate, v_w_up, v_w_down)
    shapes = {k: a.shape for k, a in zip(WEIGHTS, given)}

    def two_d(k, a):
        a = a.reshape(-1, a.shape[-1])
        return a.T if k in TRANSPOSED else a

    w = {k: two_d(k, a) for k, a in zip(WEIGHTS, given)}
    m = {k: two_d(k, a) for k, a in zip(WEIGHTS, moments_m)}
    v = {k: two_d(k, a) for k, a in zip(WEIGHTS, moments_v)}

    grad_x, parts = _local_step(x, loss_target, _MeshExchange(w))

    tiled = ("w_in", "w_gate", "w_up", "w_down")
    new = {k: _adamw("adamw_" + k, parts[k], w[k], m[k], v[k]) for k in tiled}
    small = [k for k in WEIGHTS if k not in tiled]
    new.update(zip(small, _adamw_many("adamw_small", [(parts[k], w[k], m[k], v[k]) for k in small])))

    loss = jnp.sum(parts["loss"][:, 0, 0])
    outs = [loss, grad_x]
    for idx in range(4):
        outs += [(new[k][idx].T if k in TRANSPOSED else new[k][idx]).reshape(shapes[k]) for k in WEIGHTS]
    return tuple(outs)
```

```python
import functools
import math

import numpy as np
import jax
import jax.numpy as jnp
from jax import lax
from jax.experimental import pallas as pl
from jax.experimental.pallas import tpu as pltpu

F32 = jnp.float32
BF16 = jnp.bfloat16

D_MODEL = 1024
N_META = 16
SEQ = 2048
N_HEADS = 8
QK_NOPE = 64
QK_ROPE = 32
QK_HEAD = QK_NOPE + QK_ROPE
V_HEAD = 64
D_ATTN = N_HEADS * V_HEAD
Q_LORA = 384
KV_LORA = 256
D_RNN = 512
RNN_BW = 64
D_FF = 2816
EPS = 1e-6
LRU_C = 8.0
ROPE_THETA = 10000.0
OFF_CKV = Q_LORA + KV_LORA
OFF_KR = OFF_CKV + QK_ROPE
IN_COLS = OFF_KR + 2 * D_RNN

ADAM_LR = 0.001
ADAM_B1 = 0.9
ADAM_B2 = 0.999
ADAM_EPS = 1e-08
ADAM_WD = 0.01
ADAM_STEP = 10

N_DEV = 8
LANES = 128
HEAD_PAD = LANES
PAD_ROWS = LANES - N_META
QP_COLS = N_HEADS * HEAD_PAD
P_COLS = OFF_CKV + 2 * D_RNN + LANES
FF_CHUNK = D_FF
VMEM_LIMIT = 56 * 1024 * 1024
MESH = pl.DeviceIdType.MESH


def _t_pad():
    return PAD_ROWS + N_META + SEQ


def _row_tile(n):
    return 256 if n % 256 == 0 else 128


def _wide_row_tile(n):
    quarter = _t_pad() // 4
    return quarter if quarter % 16 == 0 and n % quarter == 0 else _row_tile(n)


def _const_spec(shape):
    nd = len(shape)
    return pl.BlockSpec(shape, lambda *_: (0,) * nd, pipeline_mode=pl.Buffered(1))


def _rms(x, d):
    r = lax.rsqrt(jnp.sum(x * x, axis=-1, keepdims=True) * (1.0 / d) + EPS)
    return x * r, r


def _rms_bwd(dy, xhat, r, g, d):
    dxh = dy * g
    return r * (dxh - xhat * (jnp.sum(dxh * xhat, axis=-1, keepdims=True) * (1.0 / d)))


def _colsum(x):
    return jnp.sum(x, axis=0, keepdims=True)


def _dot(a, b):
    return jnp.dot(a, b, preferred_element_type=F32)


def _dot_nt(a, b):
    return lax.dot_general(a, b, (((1,), (1,)), ((), ())), preferred_element_type=F32)


def _dot_tn(a, b):
    return lax.dot_general(a, b, (((0,), (0,)), ((), ())), preferred_element_type=F32)


def _rope(x, c, s1, s2):
    return x * c + pltpu.roll(x, 16, 1) * s1 + pltpu.roll(x, HEAD_PAD - 16, 1) * s2


def _rope_bwd(dy, c, s1, s2):
    return dy * c + pltpu.roll(dy * s1, HEAD_PAD - 16, 1) + pltpu.roll(dy * s2, 16, 1)


def _acc(ref, first, val):
    @pl.when(first)
    def _():
        ref[...] = val

    @pl.when(jnp.logical_not(first))
    def _():
        ref[...] += val


def _in_proj(h0, ln1_g, w_in_p, srcs=(), scatter=()):
    n = h0.shape[0]
    tm = _wide_row_tile(n)
    nk = len(srcs)
    c_in, c_out, c_shape, c_sems = _exchange_specs(srcs, scatter)

    def body(h_ref, g_ref, w_ref, *rest):
        hn_ref, cq_ref, ckv_ref, xr_ref, xg_ref, kr_ref = rest[nk:nk + 6]
        finish = _ride(1, *_exchange_fns(rest[:nk], rest[nk + 6:2 * nk + 6], rest[2 * nk + 6:], scatter))
        xhat, _ = _rms(h_ref[...], D_MODEL)
        hn = (xhat * g_ref[...]).astype(BF16)
        hn_ref[...] = hn
        p = _dot_nt(hn, w_ref[...])
        cq_ref[...] = p[:, :Q_LORA]
        ckv_ref[...] = p[:, Q_LORA:OFF_CKV]
        xr_ref[...] = p[:, OFF_CKV:OFF_CKV + D_RNN]
        xg_ref[...] = p[:, OFF_CKV + D_RNN:OFF_CKV + 2 * D_RNN]
        kr_ref[...] = p[:, OFF_CKV + 2 * D_RNN:]
        finish()

    def row(w):
        return pl.BlockSpec((tm, w), lambda i: (i, 0))

    widths = (D_MODEL, Q_LORA, KV_LORA, D_RNN, D_RNN, LANES)
    res = pl.pallas_call(
        body, name="in_proj", grid=(n // tm,),
        in_specs=[row(D_MODEL), _const_spec((1, D_MODEL)), _const_spec((P_COLS, D_MODEL))] + c_in,
        out_specs=[row(w) for w in widths] + c_out,
        out_shape=[jax.ShapeDtypeStruct((n, w), BF16 if k == 0 else F32) for k, w in enumerate(widths)] + c_shape,
        scratch_shapes=c_sems,
        compiler_params=pltpu.CompilerParams(dimension_semantics=("arbitrary",), vmem_limit_bytes=VMEM_LIMIT),
    )(h0, ln1_g, w_in_p, *srcs)
    return res[:6], res[6:]


def _qkv_fwd(cq, ckv, kr, gqa, gkva, w_uq_p, w_uk_p, w_v, qg, kg, rc, rs1, rs2):
    n = cq.shape[0]
    tm = _wide_row_tile(n)

    def body(cq_ref, ckv_ref, kr_ref, gqa_ref, gkva_ref, wuq_ref, wuk_ref, wv_ref, qg_ref, kg_ref,
             c_ref, s1_ref, s2_ref, q_ref, k_ref, v_ref):
        xq, _ = _rms(cq_ref[...], Q_LORA)
        qa = (xq * gqa_ref[...]).astype(BF16)
        q = _dot_nt(qa, wuq_ref[...])
        xkv, _ = _rms(ckv_ref[...], KV_LORA)
        kva = (xkv * gkva_ref[...]).astype(BF16)
        kn = _dot(kva, wuk_ref[...])
        v_ref[...] = _dot(kva, wv_ref[...]).astype(BF16)
        krp = kr_ref[...]
        c, s1, s2 = c_ref[...], s1_ref[...], s2_ref[...]
        for h in range(N_HEADS):
            sl = slice(h * HEAD_PAD, (h + 1) * HEAD_PAD)
            qh, _ = _rms(q[:, sl], QK_HEAD)
            q_ref[:, sl] = _rope(qh * qg_ref[...], c, s1, s2).astype(BF16)
            kh, _ = _rms(kn[:, sl] + krp, QK_HEAD)
            k_ref[:, sl] = _rope(kh * kg_ref[...], c, s1, s2).astype(BF16)

    def row(w):
        return pl.BlockSpec((tm, w), lambda i: (i, 0))

    return pl.pallas_call(
        body, name="qkv_fwd", grid=(n // tm,),
        in_specs=[row(Q_LORA), row(KV_LORA), row(LANES), _const_spec((1, Q_LORA)), _const_spec((1, KV_LORA)),
                  _const_spec((QP_COLS, Q_LORA)), _const_spec((KV_LORA, QP_COLS)), _const_spec((KV_LORA, D_ATTN)),
                  _const_spec((1, LANES)), _const_spec((1, LANES)), row(LANES), row(LANES), row(LANES)],
        out_specs=[row(QP_COLS), row(QP_COLS), row(D_ATTN)],
        out_shape=[jax.ShapeDtypeStruct((n, QP_COLS), BF16), jax.ShapeDtypeStruct((n, QP_COLS), BF16),
                   jax.ShapeDtypeStruct((n, D_ATTN), BF16)],
        compiler_params=pltpu.CompilerParams(dimension_semantics=("parallel",), vmem_limit_bytes=VMEM_LIMIT),
    )(cq, ckv, kr, gqa, gkva, w_uq_p, w_uk_p, w_v, qg, kg, rc, rs1, rs2)


def _qkv_bwd(cq, ckv, kr, dq_r, dk_r, dv, dxr, dxg, gqa, gkva, w_uq_p, w_uk_p, w_v, qg, kg, rc, rs1, rs2,
             srcs=(), scatter=()):
    n = cq.shape[0]
    tm = _wide_row_tile(n)
    nk = len(srcs)
    c_in, c_out, c_shape, c_sems = _exchange_specs(srcs, scatter)

    def body(cq_ref, ckv_ref, kr_ref, dq_ref, dk_ref, dv_ref, dxr_ref, dxg_ref, gqa_ref, gkva_ref, wuq_ref, wuk_ref,
             wv_ref, qg_ref, kg_ref, c_ref, s1_ref, s2_ref, *rest):
        dp_ref, qa_ref, kva_ref, dqp_ref, dkv_ref, dqg_ref, dkg_ref, dgqa_ref, dgkva_ref = rest[nk:nk + 9]
        finish = _ride(1, *_exchange_fns(rest[:nk], rest[nk + 9:2 * nk + 9], rest[2 * nk + 9:], scatter))
        first = pl.program_id(0) == 0
        dp_ref[:, OFF_CKV:OFF_CKV + D_RNN] = dxr_ref[...].astype(BF16)
        dp_ref[:, OFF_CKV + D_RNN:OFF_CKV + 2 * D_RNN] = dxg_ref[...].astype(BF16)
        xq, rq = _rms(cq_ref[...], Q_LORA)
        qa = (xq * gqa_ref[...]).astype(BF16)
        qa_ref[...] = qa
        q = _dot_nt(qa, wuq_ref[...])
        xkv, rkv = _rms(ckv_ref[...], KV_LORA)
        kva = (xkv * gkva_ref[...]).astype(BF16)
        kva_ref[...] = kva
        kn = _dot(kva, wuk_ref[...])
        krp = kr_ref[...]
        c, s1, s2 = c_ref[...], s1_ref[...], s2_ref[...]
        lane = lax.broadcasted_iota(jnp.int32, (tm, HEAD_PAD), 1)
        rope_lanes = jnp.logical_and(lane >= QK_NOPE, lane < QK_HEAD)
        dqg = jnp.zeros((1, HEAD_PAD), F32)
        dkg = jnp.zeros((1, HEAD_PAD), F32)
        dkr = jnp.zeros((tm, HEAD_PAD), F32)
        for h in range(N_HEADS):
            sl = slice(h * HEAD_PAD, (h + 1) * HEAD_PAD)
            qh, rqh = _rms(q[:, sl], QK_HEAD)
            dy = _rope_bwd(dq_ref[:, sl], c, s1, s2)
            dqg = dqg + _colsum(dy * qh)
            dqp_ref[:, sl] = _rms_bwd(dy, qh, rqh, qg_ref[...], QK_HEAD).astype(BF16)
            kh, rkh = _rms(kn[:, sl] + krp, QK_HEAD)
            dyk = _rope_bwd(dk_ref[:, sl], c, s1, s2)
            dkg = dkg + _colsum(dyk * kh)
            dkh = _rms_bwd(dyk, kh, rkh, kg_ref[...], QK_HEAD)
            dkv_ref[:, sl] = dkh.astype(BF16)
            dkr = dkr + jnp.where(rope_lanes, dkh, 0.0)
        dkv_ref[:, QP_COLS:] = dv_ref[...].astype(BF16)
        dp_ref[:, OFF_CKV + 2 * D_RNN:] = dkr.astype(BF16)
        dqa = _dot(dqp_ref[...], wuq_ref[...])
        dp_ref[:, :Q_LORA] = _rms_bwd(dqa, xq, rq, gqa_ref[...], Q_LORA).astype(BF16)
        dkva = _dot_nt(dkv_ref[:, :QP_COLS], wuk_ref[...]) + _dot_nt(dkv_ref[:, QP_COLS:], wv_ref[...])
        dp_ref[:, Q_LORA:OFF_CKV] = _rms_bwd(dkva, xkv, rkv, gkva_ref[...], KV_LORA).astype(BF16)
        _acc(dqg_ref, first, dqg)
        _acc(dkg_ref, first, dkg)
        _acc(dgqa_ref, first, _colsum(dqa * xq))
        _acc(dgkva_ref, first, _colsum(dkva * xkv))
        finish()

    def row(w):
        return pl.BlockSpec((tm, w), lambda i: (i, 0))

    def acc(w):
        return pl.BlockSpec((1, w), lambda i: (0, 0))

    res = pl.pallas_call(
        body, name="qkv_bwd", grid=(n // tm,),
        in_specs=[row(Q_LORA), row(KV_LORA), row(LANES), row(QP_COLS), row(QP_COLS), row(D_ATTN), row(D_RNN), row(D_RNN),
                  _const_spec((1, Q_LORA)), _const_spec((1, KV_LORA)),
                  _const_spec((QP_COLS, Q_LORA)), _const_spec((KV_LORA, QP_COLS)), _const_spec((KV_LORA, D_ATTN)),
                  _const_spec((1, LANES)), _const_spec((1, LANES)), row(LANES), row(LANES), row(LANES)] + c_in,
        out_specs=[row(P_COLS), row(Q_LORA), row(KV_LORA), row(QP_COLS),
                   row(QP_COLS + D_ATTN), acc(LANES), acc(LANES), acc(Q_LORA), acc(KV_LORA)] + c_out,
        out_shape=[jax.ShapeDtypeStruct((n, P_COLS), BF16), jax.ShapeDtypeStruct((n, Q_LORA), BF16),
                   jax.ShapeDtypeStruct((n, KV_LORA), BF16), jax.ShapeDtypeStruct((n, QP_COLS), BF16),
                   jax.ShapeDtypeStruct((n, QP_COLS + D_ATTN), BF16),
                   jax.ShapeDtypeStruct((1, LANES), F32), jax.ShapeDtypeStruct((1, LANES), F32),
                   jax.ShapeDtypeStruct((1, Q_LORA), F32), jax.ShapeDtypeStruct((1, KV_LORA), F32)] + c_shape,
        scratch_shapes=c_sems,
        compiler_params=pltpu.CompilerParams(dimension_semantics=("arbitrary",), vmem_limit_bytes=VMEM_LIMIT),
    )(cq, ckv, kr, dq_r, dk_r, dv, dxr, dxg, gqa, gkva, w_uq_p, w_uk_p, w_v, qg, kg, rc, rs1, rs2, *srcs)
    return res[:9], res[9:]


KEY_CHUNK = 4 * LANES


def _key_chunks(t):
    count = max(t // KEY_CHUNK, 1)
    first = t - KEY_CHUNK * (count - 1)
    return [(0, first)] + [(first + KEY_CHUNK * c, KEY_CHUNK) for c in range(count - 1)]


def _attn_specs(t, tq):
    nq = t // tq
    qspec = pl.BlockSpec((tq, 2 * HEAD_PAD), lambda b, hp, i: (b * nq + i, hp))
    kspec = pl.BlockSpec((t, 2 * HEAD_PAD), lambda b, hp, i: (b, hp))
    vspec = pl.BlockSpec((t, 2 * V_HEAD), lambda b, hp, i: (b, hp))
    ospec = pl.BlockSpec((tq, 2 * V_HEAD), lambda b, hp, i: (b * nq + i, hp))
    return nq, qspec, kspec, vspec, ospec


def _probs_spec(t, tq):
    return pl.BlockSpec((1, 2, tq, t), lambda b, hp, i: (b, hp, i, 0))


def _attn_fwd(q, k, v, srcs=(), scatter=()):
    n = q.shape[0]
    t = _t_pad()
    tq = t // 2
    nq, qspec, kspec, vspec, ospec = _attn_specs(t, tq)
    nk = len(srcs)
    c_in, c_out, c_shape, c_sems = _exchange_specs(srcs, scatter)

    def body(q_ref, k_ref, v_ref, *rest):
        o_ref, l_ref, p_ref = rest[nk:nk + 3]
        finish = _ride(3, *_exchange_fns(rest[:nk], rest[nk + 3:2 * nk + 3], rest[2 * nk + 3:], scatter))
        lane = lax.broadcasted_iota(jnp.int32, (tq, 2 * V_HEAD), 1)
        outs = []
        sums = []
        for j in range(2):
            sl = slice(j * HEAD_PAD, (j + 1) * HEAD_PAD)
            qh = q_ref[:, sl]

            def scores(start, size):
                s = _dot_nt(qh, k_ref[start:start + size, sl])
                if start < PAD_ROWS:
                    key = lax.broadcasted_iota(jnp.int32, (tq, size), 1) + start
                    s = jnp.where(key >= PAD_ROWS, s, -jnp.inf)
                return s

            top = functools.reduce(jnp.maximum, [jnp.max(scores(*c), axis=-1, keepdims=True) for c in _key_chunks(t)])
            l = jnp.zeros((tq, 1), F32)
            pv = jnp.zeros((tq, 2 * V_HEAD), F32)
            for start, size in _key_chunks(t):
                e = jnp.exp((scores(start, size) - top) * (QK_HEAD ** -0.5))
                l = l + jnp.sum(e, axis=-1, keepdims=True)
                e = e.astype(BF16)
                p_ref[0, j, :, start:start + size] = e
                pv = pv + _dot(e, v_ref[start:start + size, :])
            outs.append(pv / l)
            sums.append(l)
        o_ref[...] = jnp.where(lane < V_HEAD, outs[0], outs[1])
        l_ref[...] = jnp.where(lane < V_HEAD, sums[0], sums[1])
        finish()

    res = pl.pallas_call(
        body, name="attn_fwd", grid=(n // t, N_HEADS // 2, nq),
        in_specs=[qspec, kspec, vspec] + c_in, out_specs=[ospec, ospec, _probs_spec(t, tq)] + c_out,
        out_shape=[jax.ShapeDtypeStruct((n, D_ATTN), F32), jax.ShapeDtypeStruct((n, D_ATTN), F32),
                   jax.ShapeDtypeStruct((n // t, N_HEADS, t, t), BF16)] + c_shape,
        scratch_shapes=c_sems,
        compiler_params=pltpu.CompilerParams(dimension_semantics=("arbitrary", "arbitrary", "arbitrary"),
                                             vmem_limit_bytes=VMEM_LIMIT),
    )(q, k, v, *srcs)
    return res[0], (res[1], res[2]), res[3:]


def _attn_bwd(q, k, v, do, o, probs, srcs=(), scatter=()):
    n = q.shape[0]
    t = _t_pad()
    tq = t // 2
    nq, qspec, kspec, vspec, ospec = _attn_specs(t, tq)
    nk = len(srcs)
    c_in, c_out, c_shape, c_sems = _exchange_specs(srcs, scatter)

    def body(q_ref, k_ref, v_ref, do_ref, o_ref, l_ref, p_ref, *rest):
        dq_ref, dk_ref, dv_ref = rest[nk:nk + 3]
        finish = _ride(3, *_exchange_fns(rest[:nk], rest[nk + 3:2 * nk + 3], rest[2 * nk + 3:], scatter))

        @pl.when(pl.program_id(2) == 0)
        def _():
            dk_ref[...] = jnp.zeros_like(dk_ref)
            dv_ref[...] = jnp.zeros_like(dv_ref)

        lane = lax.broadcasted_iota(jnp.int32, (tq, 2 * V_HEAD), 1)
        do = do_ref[...]
        do_o = do * o_ref[...]
        chunks = _key_chunks(t)
        dvs = [None] * len(chunks)
        for j in range(2):
            sl = slice(j * HEAD_PAD, (j + 1) * HEAD_PAD)
            qh = q_ref[:, sl]
            in_head = (lane < V_HEAD) if j == 0 else (lane >= V_HEAD)
            inv_l = 1.0 / l_ref[:, j * V_HEAD:j * V_HEAD + 1]
            doh = jnp.where(in_head, do, 0.0).astype(BF16)
            doh_n = jnp.where(in_head, do * inv_l, 0.0).astype(BF16)
            delta = jnp.sum(jnp.where(in_head, do_o, 0.0), axis=-1, keepdims=True)
            row_scale = inv_l * (QK_HEAD ** -0.5)
            dq = jnp.zeros((tq, HEAD_PAD), F32)
            for c, (start, size) in enumerate(chunks):
                rows = slice(start, start + size)
                e = p_ref[0, j, :, rows]
                dp = _dot_nt(doh, v_ref[rows, :])
                ds = (e.astype(F32) * (dp - delta) * row_scale).astype(BF16)
                dq = dq + _dot(ds, k_ref[rows, sl])
                dk_ref[rows, sl] += _dot_tn(ds, qh)
                dvc = _dot_tn(e, doh_n)
                dvs[c] = dvc if dvs[c] is None else dvs[c] + dvc
            dq_ref[:, sl] = dq
        for (start, size), dvc in zip(chunks, dvs):
            dv_ref[start:start + size, :] += dvc
        finish()

    res = pl.pallas_call(
        body, name="attn_bwd", grid=(n // t, N_HEADS // 2, nq),
        in_specs=[qspec, kspec, vspec, ospec, ospec, ospec, _probs_spec(t, tq)] + c_in,
        out_specs=[qspec, kspec, vspec] + c_out,
        out_shape=[jax.ShapeDtypeStruct((n, QP_COLS), F32), jax.ShapeDtypeStruct((n, QP_COLS), F32),
                   jax.ShapeDtypeStruct((n, D_ATTN), F32)] + c_shape, scratch_shapes=c_sems,
        compiler_params=pltpu.CompilerParams(dimension_semantics=("arbitrary", "arbitrary", "arbitrary"),
                                             vmem_limit_bytes=VMEM_LIMIT),
    )(q, k, v, do, o, *probs, *srcs)
    return res[:3], res[3:]


SCAN_STEPS = 8


def _scan(chains, t):
    seg = t // 8
    rows = lax.broadcasted_iota(jnp.int32, (8, LANES), 0)

    def step(i, carry):
        carry = list(carry)
        for u in range(SCAN_STEPS):
            j = i * SCAN_STEPS + u
            for n, (a_ref, b_ref, h_ref, p_ref, reverse) in enumerate(chains):
                h, p = carry[n]
                idx = pl.ds(seg - 1 - j if reverse else j, 8, stride=seg)
                a = a_ref[idx, :]
                h = a * h + b_ref[idx, :]
                p = a * p
                h_ref[idx, :] = h
                p_ref[idx, :] = p
                carry[n] = (h, p)
        return tuple(carry)

    init = tuple((jnp.zeros((8, LANES), F32), jnp.ones((8, LANES), F32)) for _ in chains)
    ends = lax.fori_loop(0, seg // SCAN_STEPS, step, init)
    for (_, _, h_ref, p_ref, reverse), (b, a) in zip(chains, ends):
        for d in (1, 2, 4):
            if reverse:
                keep = rows < 8 - d
                a_n, b_n = pltpu.roll(a, 8 - d, 0), pltpu.roll(b, 8 - d, 0)
            else:
                keep = rows >= d
                a_n, b_n = pltpu.roll(a, d, 0), pltpu.roll(b, d, 0)
            b = a * jnp.where(keep, b_n, 0.0) + b
            a = a * jnp.where(keep, a_n, 1.0)
        for s in (range(7) if reverse else range(1, 8)):
            sl = slice(s * seg, (s + 1) * seg)
            carry_in = b[s + 1:s + 2, :] if reverse else b[s - 1:s, :]
            h_ref[sl, :] = h_ref[sl, :] + p_ref[sl, :] * carry_in


def _shift_rows(x, s, rows, t):
    if s == 0:
        return x
    rolled = pltpu.roll(x, s % t, 0)
    return jnp.where(rows >= s, rolled, 0.0) if s > 0 else jnp.where(rows < t + s, rolled, 0.0)


def _neg_expm1(x, exp_x):
    series = -x * (1.0 + x * (0.5 + x * (1.0 / 6 + x * (1.0 / 24))))
    return jnp.where(x > -0.1, series, 1.0 - exp_x)


def _sigmoid(x):
    return 0.5 * jnp.tanh(0.5 * x) + 0.5


def _gelu_parts(x):
    k = math.sqrt(2.0 / math.pi)
    th = jnp.tanh(k * (x + 0.044715 * x * x * x))
    g = 0.5 * x * (1.0 + th)
    dg = 0.5 * (1.0 + th) + 0.5 * x * (1.0 - th * th) * k * (1.0 + 3 * 0.044715 * x * x)
    return g, dg


def _lru_gates(xc, gates, lam_ref, valid, d):
    r = _sigmoid(gates[:, (2 * d) * LANES:(2 * d + 1) * LANES])
    i = _sigmoid(gates[:, (2 * d + 1) * LANES:(2 * d + 2) * LANES])
    neg_lam = -lam_ref[d:d + 1, :]
    sp = jnp.maximum(neg_lam, 0.0) + jnp.log1p(jnp.exp(-jnp.abs(neg_lam)))
    log_a = -LRU_C * r * sp
    a = jnp.exp(log_a)
    m = jnp.maximum(_neg_expm1(2.0 * log_a, a * a), 0.0)
    sq = jnp.sqrt(m)
    b = jnp.where(valid, sq * (i * xc), 0.0)
    return r, i, sp, a, m, sq, b


def _conv(xr, cw_ref, cb_ref, rows, t):
    return (cw_ref[0:1, :] * _shift_rows(xr, 2, rows, t) + cw_ref[1:2, :] * _shift_rows(xr, 1, rows, t)
            + cw_ref[2:3, :] * xr + cw_ref[3:4, :] * _shift_rows(xr, -1, rows, t) + cb_ref[...])


def _rnn_specs(t):
    seq = pl.BlockSpec((t, LANES), lambda cb, b: (b, cb))
    cw = pl.BlockSpec((4, LANES), lambda cb, b: (0, cb))
    vec1 = pl.BlockSpec((1, LANES), lambda cb, b: (0, cb))
    vec2 = pl.BlockSpec((2, LANES), lambda cb, b: (0, cb))
    wblk = pl.BlockSpec((1, LANES, 4 * LANES), lambda cb, b: (cb, 0, 0))
    gbias = pl.BlockSpec((1, 1, 4 * LANES), lambda cb, b: (cb, 0, 0))
    return seq, cw, vec1, vec2, wblk, gbias


def _rnn_fwd(xr, xg, conv_w, conv_b, wblk, gbias, lam):
    n = xr.shape[0]
    t = _t_pad()
    seq, cw, vec1, vec2, wspec, gspec = _rnn_specs(t)

    def body(xr_ref, xg_ref, cw_ref, cb_ref, w_ref, gb_ref, lam_ref, o_ref, a_s, b_s, h_s, p_s):
        rows = lax.broadcasted_iota(jnp.int32, (t, LANES), 0)
        valid = rows >= PAD_ROWS
        xc = _conv(xr_ref[...], cw_ref, cb_ref, rows, t)
        gates = _dot(xc.astype(BF16), w_ref[0]) + gb_ref[0]
        for d in range(2):
            _, _, _, a, _, _, b = _lru_gates(xc, gates, lam_ref, valid, d)
            a_s[d] = a
            b_s[d] = b
        _scan([(a_s.at[d], b_s.at[d], h_s.at[d], p_s.at[d], d == 1) for d in range(2)], t)
        g, _ = _gelu_parts(xg_ref[...])
        o_ref[...] = (h_s[0] + h_s[1]) * g

    return pl.pallas_call(
        body, name="rnn_fwd", grid=(D_RNN // LANES, n // t),
        in_specs=[seq, seq, cw, vec1, wspec, gspec, vec2], out_specs=seq,
        out_shape=jax.ShapeDtypeStruct((n, D_RNN), F32),
        scratch_shapes=[pltpu.VMEM((2, t, LANES), F32)] * 4,
        compiler_params=pltpu.CompilerParams(dimension_semantics=("parallel", "parallel"), vmem_limit_bytes=VMEM_LIMIT),
    )(xr, xg, conv_w, conv_b, wblk, gbias, lam)


def _rnn_bwd(xr, xg, do, conv_w, conv_b, wblk, gbias, lam, srcs=(), scatter=()):
    n = xr.shape[0]
    t = _t_pad()
    seq, cw, vec1, vec2, wspec, gspec = _rnn_specs(t)
    nk = len(srcs)
    c_in, c_out, c_shape, c_sems = _exchange_specs(srcs, scatter)

    def body(xr_ref, xg_ref, do_ref, cw_ref, cb_ref, w_ref, gb_ref, lam_ref, *rest):
        dxr_ref, dxg_ref, dcw_ref, dcb_ref, dw_ref, dgb_ref, dlam_ref = rest[nk:nk + 7]
        a_s, b_s, h_s, l_s, p_s, back_s, r_s, i_s, q_s, dg_s = rest[2 * nk + 7 + len(c_sems):]
        finish = _ride(2, *_exchange_fns(rest[:nk], rest[nk + 7:2 * nk + 7], rest[2 * nk + 7:2 * nk + 7 + len(c_sems)],
                                         scatter))
        first = pl.program_id(1) == 0
        rows = lax.broadcasted_iota(jnp.int32, (t, LANES), 0)
        valid = rows >= PAD_ROWS
        xr = xr_ref[...]
        xc = _conv(xr, cw_ref, cb_ref, rows, t)
        xcb = xc.astype(BF16)
        gates = _dot(xcb, w_ref[0]) + gb_ref[0]
        sps = []
        for d in range(2):
            r_s[d], i_s[d], sp, a_s[d], _, q_s[d], b_s[d] = _lru_gates(xc, gates, lam_ref, valid, d)
            sps.append(sp)
        _scan([(a_s.at[d], b_s.at[d], h_s.at[d], p_s.at[d], d == 1) for d in range(2)], t)
        g, dg = _gelu_parts(xg_ref[...])
        do = do_ref[...]
        dxg_ref[...] = do * (h_s[0] + h_s[1]) * dg
        b_s[0] = do * g
        for d in range(2):
            back_s[d] = _shift_rows(a_s[d], -1 if d == 0 else 1, rows, t)
        _scan([(back_s.at[d], b_s.at[0], l_s.at[d], p_s.at[d], d == 0) for d in range(2)], t)
        dxc = jnp.zeros((t, LANES), F32)
        dlams = []
        for d in range(2):
            r, i, sp, a, sq = r_s[d], i_s[d], sps[d], a_s[d], q_s[d]
            lam_t = l_s[d]
            da = lam_t * _shift_rows(h_s[d], 1 if d == 0 else -1, rows, t)
            lam_v = jnp.where(valid, lam_t, 0.0)
            dsq = lam_v * (i * xc)
            di = lam_v * sq * xc
            dxc = dxc + lam_v * sq * i
            dm = jnp.where(sq > 0.0, dsq * 0.5 / jnp.where(sq > 0.0, sq, 1.0), 0.0)
            dla = da * a - 2.0 * dm * a * a
            dr = dla * (-LRU_C) * sp
            dsp = _colsum(dla * (-LRU_C) * r)
            dlams.append(dsp * -jax.nn.sigmoid(-lam_ref[d:d + 1, :]))
            dg_s[:, (2 * d) * LANES:(2 * d + 1) * LANES] = (dr * r * (1.0 - r)).astype(BF16)
            dg_s[:, (2 * d + 1) * LANES:(2 * d + 2) * LANES] = (di * i * (1.0 - i)).astype(BF16)
        dgates = dg_s[...]
        dxc = dxc + _dot_nt(dgates, w_ref[0])
        taps = [_shift_rows(dxc, j - 2, rows, t) for j in range(4)]
        dxr_ref[...] = (cw_ref[0:1, :] * taps[0] + cw_ref[1:2, :] * taps[1] + cw_ref[2:3, :] * taps[2]
                        + cw_ref[3:4, :] * taps[3])
        dcw = jnp.concatenate([_colsum(tap * xr) for tap in taps], axis=0)
        _acc(dcw_ref, first, dcw)
        _acc(dcb_ref, first, _colsum(dxc))
        _acc(dw_ref, first, _dot_tn(xcb, dgates)[None])
        _acc(dgb_ref, first, _colsum(dgates.astype(F32))[None])
        _acc(dlam_ref, first, jnp.concatenate(dlams, axis=0))
        finish()

    res = pl.pallas_call(
        body, name="rnn_bwd", grid=(D_RNN // LANES, n // t),
        in_specs=[seq, seq, seq, cw, vec1, wspec, gspec, vec2] + c_in,
        out_specs=[seq, seq, cw, vec1, wspec, gspec, vec2] + c_out,
        out_shape=[jax.ShapeDtypeStruct((n, D_RNN), F32), jax.ShapeDtypeStruct((n, D_RNN), F32),
                   jax.ShapeDtypeStruct((4, D_RNN), F32), jax.ShapeDtypeStruct((1, D_RNN), F32),
                   jax.ShapeDtypeStruct((D_RNN // LANES, LANES, 4 * LANES), F32),
                   jax.ShapeDtypeStruct((D_RNN // LANES, 1, 4 * LANES), F32), jax.ShapeDtypeStruct((2, D_RNN), F32)]
        + c_shape,
        scratch_shapes=c_sems + [pltpu.VMEM((2, t, LANES), F32)] * 9 + [pltpu.VMEM((t, 4 * LANES), BF16)],
        compiler_params=pltpu.CompilerParams(dimension_semantics=("arbitrary", "arbitrary"), vmem_limit_bytes=VMEM_LIMIT),
    )(xr, xg, do, conv_w, conv_b, wblk, gbias, lam, *srcs)
    return res[:7], res[7:]


def _post(oa, orn, h0, tgt, ga, gr, g2, w_out, w_gate, w_up, w_down):
    n = oa.shape[0]
    tm = _row_tile(n)
    t = _t_pad()

    def body(oa_ref, or_ref, h0_ref, tgt_ref, ga_ref, gr_ref, g2_ref, wo_ref, wg_ref, wu_ref, wd_ref,
             doa_ref, dor_ref, dh1_ref, mix_ref, h1n_ref, act_ref, dgate_ref, dup_ref, dy_ref,
             loss_ref, dga_ref, dgr_ref, dg2_ref, gate_s, up_s):
        first = pl.program_id(0) == 0
        xa, ra = _rms(oa_ref[...], D_ATTN)
        xr, rr = _rms(or_ref[...], D_RNN)
        mix = jnp.concatenate([(xa * ga_ref[...]).astype(BF16), (xr * gr_ref[...]).astype(BF16)], axis=-1)
        mix_ref[...] = mix.T
        h1 = h0_ref[...] + _dot(mix, wo_ref[...])
        x2, r2 = _rms(h1, D_MODEL)
        h1n = (x2 * g2_ref[...]).astype(BF16)
        h1n_ref[...] = h1n
        y = h1
        for cs in range(0, D_FF, FF_CHUNK):
            sl = slice(cs, cs + FF_CHUNK)
            gate = _dot_nt(h1n, wg_ref[sl, :])
            up = _dot_nt(h1n, wu_ref[sl, :])
            gate_s[:, sl] = gate
            up_s[:, sl] = up
            act = (gate * _sigmoid(gate) * up).astype(BF16)
            act_ref[sl, :] = act.T
            y = y + _dot(act, wd_ref[sl, :])
        row = pl.program_id(0) * tm + lax.broadcasted_iota(jnp.int32, (tm, 1), 0)
        for _ in range(1, n // t):
            row = jnp.where(row >= t, row - t, row)
        err = jnp.where(row >= PAD_ROWS + N_META, y - tgt_ref[...], 0.0)
        _acc(loss_ref, first, jnp.full((1, LANES), 0.5 / D_MODEL, F32) * jnp.sum(err * err))
        dy = err * (1.0 / D_MODEL)
        dyb = dy.astype(BF16)
        dy_ref[...] = dyb
        dh1n = jnp.zeros((tm, D_MODEL), F32)
        for cs in range(0, D_FF, FF_CHUNK):
            sl = slice(cs, cs + FF_CHUNK)
            dact = _dot_nt(dyb, wd_ref[sl, :])
            gate, up = gate_s[:, sl], up_s[:, sl]
            sg = _sigmoid(gate)
            dgate = (dact * up * sg * (1.0 + gate * (1.0 - sg))).astype(BF16)
            dup = (dact * gate * sg).astype(BF16)
            dgate_ref[sl, :] = dgate.T
            dup_ref[sl, :] = dup.T
            dh1n = dh1n + _dot(dgate, wg_ref[sl, :]) + _dot(dup, wu_ref[sl, :])
        _acc(dg2_ref, first, _colsum(dh1n * x2))
        dh1 = dy + _rms_bwd(dh1n, x2, r2, g2_ref[...], D_MODEL)
        dh1_ref[...] = dh1
        dmix = _dot_nt(dh1.astype(BF16), wo_ref[...])
        dma, dmr = dmix[:, :D_ATTN], dmix[:, D_ATTN:]
        _acc(dga_ref, first, _colsum(dma * xa))
        _acc(dgr_ref, first, _colsum(dmr * xr))
        doa_ref[...] = _rms_bwd(dma, xa, ra, ga_ref[...], D_ATTN)
        dor_ref[...] = _rms_bwd(dmr, xr, rr, gr_ref[...], D_RNN)

    def row(w):
        return pl.BlockSpec((tm, w), lambda i: (i, 0))

    def acc(w):
        return pl.BlockSpec((1, w), lambda i: (0, 0))

    def col(w):
        return pl.BlockSpec((w, tm), lambda i: (0, i))

    outs = [(D_ATTN, F32, row), (D_RNN, F32, row), (D_MODEL, F32, row), (D_MODEL, BF16, col), (D_MODEL, BF16, row),
            (D_FF, BF16, col), (D_FF, BF16, col), (D_FF, BF16, col), (D_MODEL, BF16, row)]
    accs = [LANES, D_ATTN, D_RNN, D_MODEL]
    return pl.pallas_call(
        body, name="post", grid=(n // tm,),
        in_specs=[row(D_ATTN), row(D_RNN), row(D_MODEL), row(D_MODEL),
                  _const_spec((1, D_ATTN)), _const_spec((1, D_RNN)), _const_spec((1, D_MODEL)),
                  _const_spec((D_MODEL, D_MODEL)), _const_spec((D_FF, D_MODEL)), _const_spec((D_FF, D_MODEL)),
                  _const_spec((D_FF, D_MODEL))],
        out_specs=[spec(w) for w, _, spec in outs] + [acc(w) for w in accs],
        out_shape=[jax.ShapeDtypeStruct((n, w) if spec is row else (w, n), dt) for w, dt, spec in outs]
        + [jax.ShapeDtypeStruct((1, w), F32) for w in accs],
        scratch_shapes=[pltpu.VMEM((tm, D_FF), F32), pltpu.VMEM((tm, D_FF), F32)],
        compiler_params=pltpu.CompilerParams(dimension_semantics=("arbitrary",), vmem_limit_bytes=VMEM_LIMIT),
    )(oa, orn, h0, tgt, ga, gr, g2, w_out, w_gate, w_up, w_down)


def _in_bwd(dp, h0, dh1, ln1_g, w_in_p, srcs=(), scatter=()):
    n = h0.shape[0]
    tm = _row_tile(n)
    nk = len(srcs)
    c_in, c_out, c_shape, c_sems = _exchange_specs(srcs, scatter)

    def body(dp_ref, h0_ref, dh1_ref, g_ref, w_ref, *rest):
        dh0_ref, dg_ref = rest[nk:nk + 2]
        finish = _ride(1, *_exchange_fns(rest[:nk], rest[nk + 2:2 * nk + 2], rest[2 * nk + 2:], scatter))
        dhn = _dot(dp_ref[...], w_ref[...])
        xhat, r = _rms(h0_ref[...], D_MODEL)
        _acc(dg_ref, pl.program_id(0) == 0, _colsum(dhn * xhat))
        dh0_ref[...] = dh1_ref[...] + _rms_bwd(dhn, xhat, r, g_ref[...], D_MODEL)
        finish()

    def row(w):
        return pl.BlockSpec((tm, w), lambda i: (i, 0))

    res = pl.pallas_call(
        body, name="in_bwd", grid=(n // tm,),
        in_specs=[row(P_COLS), row(D_MODEL), row(D_MODEL), _const_spec((1, D_MODEL)), _const_spec((P_COLS, D_MODEL))] + c_in,
        out_specs=[row(D_MODEL), pl.BlockSpec((1, D_MODEL), lambda i: (0, 0))] + c_out,
        out_shape=[jax.ShapeDtypeStruct((n, D_MODEL), F32), jax.ShapeDtypeStruct((1, D_MODEL), F32)] + c_shape,
        scratch_shapes=c_sems,
        compiler_params=pltpu.CompilerParams(dimension_semantics=("arbitrary",), vmem_limit_bytes=VMEM_LIMIT),
    )(dp, h0, dh1, ln1_g, w_in_p, *srcs)
    return res[:2], res[2:]


MAX_TILE = D_FF // 2


def _pick_tile(width, cap):
    best = LANES
    for mult in range(1, width // LANES + 1):
        cand = mult * LANES
        if width % cand == 0 and cand <= cap:
            best = cand
    return best


def _matmul_tn(name, a, b, srcs=(), scatter=()):
    n, ka = a.shape
    kb = b.shape[1]
    ta, tb = _pick_tile(ka, MAX_TILE), _pick_tile(kb, MAX_TILE)
    tk = n // 2
    nk = len(srcs)
    c_in, c_out, c_shape, c_sems = _exchange_specs(srcs, scatter)

    def body(a_ref, b_ref, *rest):
        o_ref = rest[nk]
        finish = _ride(3, *_exchange_fns(rest[:nk], rest[nk + 1:2 * nk + 1], rest[2 * nk + 1:], scatter))
        _acc(o_ref, pl.program_id(2) == 0, _dot_tn(a_ref[...].astype(BF16), b_ref[...].astype(BF16)))
        finish()

    res = pl.pallas_call(
        body, name=name, grid=(ka // ta, kb // tb, n // tk),
        in_specs=[pl.BlockSpec((tk, ta), lambda i, j, k: (k, i)), pl.BlockSpec((tk, tb), lambda i, j, k: (k, j))] + c_in,
        out_specs=[pl.BlockSpec((ta, tb), lambda i, j, k: (i, j))] + c_out,
        out_shape=[jax.ShapeDtypeStruct((ka, kb), F32)] + c_shape, scratch_shapes=c_sems,
        compiler_params=pltpu.CompilerParams(dimension_semantics=("arbitrary", "arbitrary", "arbitrary"),
                                             vmem_limit_bytes=VMEM_LIMIT),
    )(a, b, *srcs)
    return res[0], res[1:]


def _matmul_shards(name, at, b):
    ka, n = at.shape
    kb = b.shape[1]
    width = ka // N_DEV
    per = 2 if 2 * width >= 4 * LANES else 4
    ta = per * width

    def body(a_ref, b_ref, o_ref):
        out = _dot(a_ref[...], b_ref[...].astype(BF16))
        for s in range(per):
            o_ref[s] = out[s * width:(s + 1) * width, :].astype(BF16)

    return pl.pallas_call(
        body, name=name, grid=(ka // ta,),
        in_specs=[pl.BlockSpec((ta, n), lambda i: (i, 0)), _const_spec((n, kb))],
        out_specs=pl.BlockSpec((per, width, kb), lambda i: (i, 0, 0)),
        out_shape=jax.ShapeDtypeStruct((N_DEV, width, kb), BF16),
        compiler_params=pltpu.CompilerParams(dimension_semantics=("parallel",), vmem_limit_bytes=VMEM_LIMIT),
    )(at, b)


def _adamw_math(g8_ref, w_ref, m_ref, v_ref, g_ref, d_ref, nm_ref, nv_ref):
    g = g8_ref[0].astype(F32)
    for s in range(1, N_DEV):
        g = g + g8_ref[s].astype(F32)
    g_ref[...] = g
    nm = ADAM_B1 * m_ref[...] + (1.0 - ADAM_B1) * g
    nv = ADAM_B2 * v_ref[...] + (1.0 - ADAM_B2) * (g * g)
    nm_ref[...] = nm
    nv_ref[...] = nv
    m_hat = nm / (1.0 - ADAM_B1 ** ADAM_STEP)
    v_hat = nv / (1.0 - ADAM_B2 ** ADAM_STEP)
    d_ref[...] = -ADAM_LR * (m_hat / (jnp.sqrt(v_hat) + ADAM_EPS) + ADAM_WD * w_ref[...])


def _adamw_many(name, items):
    count = len(items)

    def body(*refs):
        ins, outs = refs[:4 * count], refs[4 * count:]
        for i in range(count):
            _adamw_math(*ins[4 * i:4 * i + 4], *outs[4 * i:4 * i + 4])

    flat = [a for item in items for a in item]
    res = pl.pallas_call(
        body, name=name,
        out_shape=[jax.ShapeDtypeStruct(item[1].shape, F32) for item in items for _ in range(4)],
        compiler_params=pltpu.CompilerParams(vmem_limit_bytes=VMEM_LIMIT),
    )(*flat)
    return [tuple(res[4 * i:4 * i + 4]) for i in range(count)]


def _adamw(name, g8, w, m, v):
    rows, cols = w.shape
    tr = rows
    for cand in (256, 176, 128, 64):
        if rows % cand == 0 and rows > cand:
            tr = cand
            break

    def body(*refs):
        _adamw_math(*refs)

    blk = pl.BlockSpec((tr, cols), lambda i: (i, 0))
    return pl.pallas_call(
        body, name=name, grid=(rows // tr,),
        in_specs=[pl.BlockSpec((N_DEV, tr, cols), lambda i: (0, i, 0)), blk, blk, blk],
        out_specs=[blk] * 4, out_shape=[jax.ShapeDtypeStruct((rows, cols), F32)] * 4,
        compiler_params=pltpu.CompilerParams(dimension_semantics=("parallel",), vmem_limit_bytes=VMEM_LIMIT),
    )(g8, w, m, v)


def _exchange_specs(srcs, scatter):
    nk = len(srcs)
    if not nk:
        return [], [], [], []
    any_spec = pl.BlockSpec(memory_space=pl.ANY)
    out_shape = [jax.ShapeDtypeStruct(s.shape if sc else (N_DEV,) + s.shape, s.dtype) for s, sc in zip(srcs, scatter)]
    sems = [pltpu.SemaphoreType.DMA((nk, N_DEV - 1)), pltpu.SemaphoreType.DMA((nk, N_DEV - 1)),
            pltpu.SemaphoreType.DMA((nk,))]
    return [any_spec] * nk, [any_spec] * nk, out_shape, sems


FLIPS = ((0, 0, 1), (1, 0, 0), (0, 1, 0), (1, 1, 0), (1, 0, 1), (0, 1, 1), (1, 1, 1))
N_CHIP_PEERS = 3


def _exchange_fns(src_refs, out_refs, sems, scatter):
    nk = len(src_refs)
    if not nk:
        return (lambda: None), (lambda: None), (lambda: None)
    send_sems, recv_sems, local_sems = sems
    first = 1 + N_CHIP_PEERS

    def plan():
        x, y, c = lax.axis_index("x"), lax.axis_index("y"), lax.axis_index("c")
        me = 4 * x + 2 * y + c
        peers = [(1 - x if fx else x, 1 - y if fy else y, 1 - c if fc else c) for fx, fy, fc in FLIPS]
        pids = [4 * px + 2 * py + pc for px, py, pc in peers]

        def remote(k, j, src, dst, to):
            return pltpu.make_async_remote_copy(src_ref=src, dst_ref=dst, send_sem=send_sems.at[k, j],
                                                recv_sem=recv_sems.at[k, j], device_id=to, device_id_type=MESH)

        def mine(k, dest):
            return src_refs[k].at[dest] if scatter[k] else src_refs[k]

        local = [pltpu.make_async_copy(mine(k, me), out_refs[k].at[me], local_sems.at[k]) for k in range(nk)]
        direct = [remote(k, j, mine(k, pids[j]), out_refs[k].at[me], peers[j])
                  for k in range(nk) for j in range(len(FLIPS) if scatter[k] else first)]
        relays = {(k, j): remote(k, j, out_refs[k].at[pids[j - N_CHIP_PEERS]], out_refs[k].at[pids[j - N_CHIP_PEERS]], peers[0])
                  for k in range(nk) if not scatter[k] for j in range(first, len(FLIPS))}
        arrivals = {(k, j): remote(k, j, out_refs[k].at[pids[j]], out_refs[k].at[pids[j]], peers[j])
                    for k in range(nk) for j in range(len(FLIPS))}
        return local, direct, relays, arrivals

    def start():
        local, direct, _, _ = plan()
        for cp in local + direct:
            cp.start()

    def relay():
        _, _, relays, arrivals = plan()
        for (k, j), cp in relays.items():
            arrivals[k, j - N_CHIP_PEERS].wait_recv()
            cp.start()

    def wait():
        local, direct, relays, arrivals = plan()
        for (k, j), cp in arrivals.items():
            if (k, j + N_CHIP_PEERS) not in relays:
                cp.wait_recv()
        for cp in direct + list(relays.values()):
            cp.wait_send()
        for cp in local:
            cp.wait()

    return start, relay, wait


def _grid_step(rank):
    step, total = 0, 1
    for axis in range(rank):
        step = step * pl.num_programs(axis) + pl.program_id(axis)
        total = total * pl.num_programs(axis)
    return step, total


def _ride(rank, start, relay, wait):
    step, total = _grid_step(rank)
    pl.when(step == 0)(start)
    pl.when(step == (3 * total) // 4)(relay)
    return lambda: pl.when(step == total - 1)(wait)


def _exchange(name, srcs, scatter):
    nk = len(srcs)
    c_in, c_out, c_shape, c_sems = _exchange_specs(srcs, scatter)

    def body(*refs):
        start, relay, wait = _exchange_fns(refs[:nk], refs[nk:2 * nk], refs[2 * nk:], scatter)
        start()
        relay()
        wait()

    return pl.pallas_call(body, name=name, in_specs=c_in, out_specs=c_out, out_shape=c_shape, scratch_shapes=c_sems)(*srcs)


def _cols_from_shards(g):
    return jnp.transpose(g, (1, 0, 2)).reshape(g.shape[1], -1)


def _cols_to_shards(w):
    return jnp.transpose(w.reshape(w.shape[0], N_DEV, -1), (1, 0, 2))


def _prep(x, tgt, srcs, scatter):
    nb = x.shape[0]
    t = _t_pad()
    head = PAD_ROWS + N_META
    nk = len(srcs)
    c_in, c_out, c_shape, c_sems = _exchange_specs(srcs, scatter)

    def body(x_ref, tgt_ref, *rest):
        h0_ref, tp_ref = rest[nk:nk + 2]
        finish = _ride(1, *_exchange_fns(rest[:nk], rest[nk + 2:2 * nk + 2], rest[2 * nk + 2:], scatter))
        lead = pl.program_id(0) == 0

        @pl.when(lead)
        def _():
            h0_ref[...] = jnp.zeros_like(h0_ref)
            tp_ref[...] = jnp.zeros_like(tp_ref)

        @pl.when(jnp.logical_not(lead))
        def _():
            h0_ref[...] = x_ref[...]
            tp_ref[...] = tgt_ref[...]

        finish()

    src = pl.BlockSpec((nb, head, D_MODEL), lambda j: (0, jnp.maximum(j - 1, 0), 0))
    dst = pl.BlockSpec((nb, head, D_MODEL), lambda j: (0, j, 0))
    padded = jax.ShapeDtypeStruct((nb, t, D_MODEL), F32)
    res = pl.pallas_call(
        body, name="prep", grid=(t // head,), in_specs=[src, src] + c_in, out_specs=[dst, dst] + c_out,
        out_shape=[padded, padded] + c_shape, scratch_shapes=c_sems,
        compiler_params=pltpu.CompilerParams(dimension_semantics=("arbitrary",)),
    )(x, tgt, *srcs)
    return res[0], res[1], res[2:]


def _rope_tables(n):
    t = _t_pad()
    pos = np.arange(t, dtype=np.float32) - np.float32(PAD_ROWS)
    half = QK_ROPE // 2
    freqs = (1.0 / (ROPE_THETA ** (np.arange(half, dtype=np.float32) / half))).astype(np.float32)
    ang = pos[:, None] * freqs[None, :]
    cos, sin = np.cos(ang), np.sin(ang)
    z = lambda w: np.zeros((t, w), np.float32)
    c = np.concatenate([np.ones((t, QK_NOPE), np.float32), cos, cos, z(HEAD_PAD - QK_HEAD)], axis=1)
    s1 = np.concatenate([z(QK_NOPE + half), sin, z(HEAD_PAD - QK_HEAD)], axis=1)
    s2 = np.concatenate([z(QK_NOPE), -sin, z(HEAD_PAD - QK_NOPE - half)], axis=1)
    return tuple(jnp.asarray(np.tile(a, (n // t, 1))) for a in (c, s1, s2))


def _block_diag_gates(lru_wa, lru_wi):
    eye = jnp.eye(2, dtype=lru_wa.dtype)

    def bd(w):
        w = w.reshape(2, D_RNN // LANES, 2, RNN_BW, RNN_BW)
        full = w[:, :, :, :, None, :] * eye[None, None, :, None, :, None]
        return full.reshape(2, D_RNN // LANES, LANES, LANES)

    a, i = bd(lru_wa), bd(lru_wi)
    return jnp.concatenate([a[0], i[0], a[1], i[1]], axis=-1)


def _unblock_gates(dw):
    nb = D_RNN // LANES
    parts = dw.reshape(nb, 2, RNN_BW, 4, 2, RNN_BW)
    diag = jnp.stack([parts[:, k, :, :, k, :] for k in range(2)], axis=1)
    diag = jnp.transpose(diag, (3, 0, 1, 2, 4)).reshape(4, 2 * nb, RNN_BW, RNN_BW)
    return jnp.stack([diag[0], diag[2]]), jnp.stack([diag[1], diag[3]])


WEIGHTS = ("meta_tokens", "ln1_g", "w_in", "q_a_norm_g", "w_uq", "kv_a_norm_g", "w_ukv", "q_norm_g", "k_norm_g",
           "conv_w", "conv_b", "lru_wa", "lru_ba", "lru_wi", "lru_bi", "lru_lambda", "attn_out_g", "rnn_out_g",
           "w_out", "ln2_g", "w_gate", "w_up", "w_down")
BIG = ("w_in", "w_uq", "w_ukv", "w_out", "w_gate", "w_up", "w_down")
TRANSPOSED = ("w_in", "w_uq", "w_gate", "w_up")
ROW_SHARDED = ("w_out", "w_down") + TRANSPOSED
REPLICATED = ("ln1_g", "q_a_norm_g", "kv_a_norm_g", "q_norm_g", "k_norm_g", "conv_b", "lru_wa", "lru_wi",
              "attn_out_g", "rnn_out_g", "ln2_g")
WHOLE = REPLICATED + ("loss",)
G_FIRST = ("w_in", "meta_tokens")
G_MID = ("w_uq", "w_ukv", "conv_w", "lru_ba", "lru_bi", "lru_lambda")
LATE = ("w_out", "w_gate", "w_up", "w_down")
G_LAST = ("meta_tokens", "ln1_g")


def _local_step(x, tgt, ex):
    nb = x.shape[0]
    t = _t_pad()
    n = nb * t
    local = ex.local
    h0, tgt_p, got = _prep(x, tgt, *ex.gather_srcs(G_FIRST))
    first = ex.gathered(G_FIRST, got)
    meta, w_in = first["meta_tokens"], first["w_in"]
    h0 = h0.at[:, PAD_ROWS:PAD_ROWS + N_META].set(jnp.broadcast_to(meta[None], (nb, N_META, D_MODEL))).reshape(n, D_MODEL)
    tgt_p = tgt_p.reshape(n, D_MODEL)

    zr = lambda r: jnp.zeros((r, D_MODEL), w_in.dtype)
    w_in_p = jnp.concatenate([w_in[:OFF_CKV], w_in[OFF_KR:], zr(QK_NOPE), w_in[OFF_CKV:OFF_KR], zr(HEAD_PAD - QK_HEAD)],
                             axis=0)
    pad_g = lambda g: jnp.pad(g, ((0, 0), (0, HEAD_PAD - QK_HEAD)))
    qg, kg = pad_g(local["q_norm_g"]), pad_g(local["k_norm_g"])
    rc, rs1, rs2 = _rope_tables(n)
    wblk = _block_diag_gates(local["lru_wa"].reshape(2, -1, RNN_BW, RNN_BW),
                             local["lru_wi"].reshape(2, -1, RNN_BW, RNN_BW)).astype(BF16)
    nblk = D_RNN // LANES

    (hn, cq, ckv, xr, xg, kr), got = _in_proj(h0, local["ln1_g"], w_in_p, *ex.gather_srcs(G_MID))
    w = ex.gathered(G_MID, got)
    w_uq_p = jnp.pad(w["w_uq"].reshape(N_HEADS, QK_HEAD, Q_LORA), ((0, 0), (0, HEAD_PAD - QK_HEAD), (0, 0))
                     ).reshape(QP_COLS, Q_LORA)
    ukv = w["w_ukv"].reshape(KV_LORA, N_HEADS, QK_NOPE + V_HEAD)
    w_uk_p = jnp.pad(ukv[:, :, :QK_NOPE], ((0, 0), (0, 0), (0, HEAD_PAD - QK_NOPE))).reshape(KV_LORA, QP_COLS)
    w_v = ukv[:, :, QK_NOPE:].reshape(KV_LORA, D_ATTN)
    gbias = jnp.stack([w["lru_ba"][0], w["lru_bi"][0], w["lru_ba"][1], w["lru_bi"][1]], axis=0)
    gbias = jnp.transpose(gbias.reshape(4, nblk, LANES), (1, 0, 2)).reshape(nblk, 1, 4 * LANES)

    q, k, v = _qkv_fwd(cq, ckv, kr, local["q_a_norm_g"], local["kv_a_norm_g"], w_uq_p, w_uk_p, w_v, qg, kg, rc, rs1, rs2)
    oa, probs, got = _attn_fwd(q, k, v, *ex.gather_srcs(LATE))
    late = ex.gathered(LATE, got)
    orn = _rnn_fwd(xr, xg, w["conv_w"], local["conv_b"], wblk, gbias, w["lru_lambda"])
    (doa, dor, dh1, mix_t, h1n, act_t, dgate_t, dup_t, dyb, loss, dga, dgr, dg2) = _post(
        oa, orn, h0, tgt_p, local["attn_out_g"], local["rnn_out_g"], local["ln2_g"], late["w_out"], late["w_gate"],
        late["w_up"], late["w_down"])
    wire = {"w_out": _matmul_shards("dw_out", mix_t, dh1), "w_gate": _matmul_shards("dw_gate", dgate_t, h1n),
            "w_up": _matmul_shards("dw_up", dup_t, h1n), "w_down": _matmul_shards("dw_down", act_t, dyb)}
    names = ("w_out", "w_gate")
    (dxr, dxg, dcw, dcb, dwblk, dgb, dlam), got = _rnn_bwd(xr, xg, dor, w["conv_w"], local["conv_b"], wblk, gbias,
                                                           w["lru_lambda"], *ex.scatter_srcs(names, wire))
    summed = ex.scattered(names, wire, got)
    dwa, dwi = _unblock_gates(dwblk)
    dgb = jnp.transpose(dgb.reshape(nblk, 4, LANES), (1, 0, 2)).reshape(4, D_RNN)
    names = ("w_up", "w_down")
    (dq_r, dk_r, dv), got = _attn_bwd(q, k, v, doa, oa, probs, *ex.scatter_srcs(names, wire))
    summed.update(ex.scattered(names, wire, got))
    wire = ex.to_wire({
        "conv_w": dcw, "conv_b": dcb, "lru_wa": dwa.reshape(-1, RNN_BW), "lru_ba": jnp.stack([dgb[0], dgb[2]]),
        "lru_wi": dwi.reshape(-1, RNN_BW), "lru_bi": jnp.stack([dgb[1], dgb[3]]), "lru_lambda": dlam,
        "attn_out_g": dga, "rnn_out_g": dgr, "ln2_g": dg2, "loss": loss})
    names = tuple(wire)
    (dp, qa, kva, dqp, dkv, dqg, dkg, dgqa, dgkva), got = _qkv_bwd(
        cq, ckv, kr, dq_r, dk_r, dv, dxr, dxg, local["q_a_norm_g"], local["kv_a_norm_g"], w_uq_p, w_uk_p, w_v, qg, kg,
        rc, rs1, rs2, *ex.scatter_srcs(names, wire))
    summed.update(ex.scattered(names, wire, got))
    dw_uq_p, _ = _matmul_tn("dw_uq", dqp, qa)
    dw_kv, _ = _matmul_tn("dw_ukv", kva, dkv)
    dw_uq = dw_uq_p.reshape(N_HEADS, HEAD_PAD, Q_LORA)[:, :QK_HEAD].reshape(N_HEADS * QK_HEAD, Q_LORA)
    dw_ukv = jnp.concatenate([dw_kv[:, :QP_COLS].reshape(KV_LORA, N_HEADS, HEAD_PAD)[:, :, :QK_NOPE],
                              dw_kv[:, QP_COLS:].reshape(KV_LORA, N_HEADS, V_HEAD)], axis=2).reshape(KV_LORA, -1)
    wire = ex.to_wire({"q_a_norm_g": dgqa, "w_uq": dw_uq, "kv_a_norm_g": dgkva, "w_ukv": dw_ukv,
                       "q_norm_g": dqg[:, :QK_HEAD], "k_norm_g": dkg[:, :QK_HEAD]})
    names = tuple(wire)
    dw_in_p, got = _matmul_tn("dw_in", dp, hn, *ex.scatter_srcs(names, wire))
    summed.update(ex.scattered(names, wire, got))
    kr0 = OFF_CKV + 2 * D_RNN + QK_NOPE
    dw_in = jnp.concatenate([dw_in_p[:OFF_CKV], dw_in_p[kr0:kr0 + QK_ROPE], dw_in_p[OFF_CKV:OFF_CKV + 2 * D_RNN]], axis=0)
    wire = ex.to_wire({"w_in": dw_in})
    (dh0, dg1), got = _in_bwd(dp, h0, dh1, local["ln1_g"], w_in_p, *ex.scatter_srcs(("w_in",), wire))
    summed.update(ex.scattered(("w_in",), wire, got))

    dh0 = dh0.reshape(nb, t, D_MODEL)
    wire = ex.to_wire({"meta_tokens": jnp.sum(dh0[:, PAD_ROWS:PAD_ROWS + N_META], axis=0), "ln1_g": dg1})
    got = ex.run("reduce_last", *ex.scatter_srcs(G_LAST, wire))
    summed.update(ex.scattered(G_LAST, wire, got))
    return dh0[:, PAD_ROWS + N_META:], summed


class _MeshExchange:
    def __init__(self, shards):
        self.local = shards

    @staticmethod
    def run(name, srcs, scatter):
        return _exchange(name, srcs, scatter)

    def gather_srcs(self, names):
        return [self.local[k].astype(BF16) if k in BIG else self.local[k] for k in names], [False] * len(names)

    @staticmethod
    def gathered(names, outs):
        return {k: g.reshape(-1, g.shape[-1]) if k in ROW_SHARDED else _cols_from_shards(g) for k, g in zip(names, outs)}

    @staticmethod
    def to_wire(grads):
        wire = {}
        for k, g in grads.items():
            if k in WHOLE:
                wire[k] = g
            elif k in ROW_SHARDED:
                wire[k] = g.reshape(N_DEV, -1, g.shape[-1]).astype(BF16)
            else:
                wire[k] = _cols_to_shards(g).astype(BF16) if k in BIG else _cols_to_shards(g)
        return wire

    @staticmethod
    def scatter_srcs(names, wire):
        return [wire[k] for k in names], [k not in WHOLE for k in names]

    @staticmethod
    def scattered(names, wire, outs):
        return dict(zip(names, outs))


def kernel(x, meta_tokens, ln1_g, w_in, q_a_norm_g, w_uq, kv_a_norm_g, w_ukv, q_norm_g, k_norm_g, conv_w, conv_b, lru_wa, lru_ba, lru_wi, lru_bi, lru_lambda, attn_out_g, rnn_out_g, w_out, ln2_g, w_gate, w_up, w_down, loss_target, m_meta_tokens, m_ln1_g, m_w_in, m_q_a_norm_g, m_w_uq, m_kv_a_norm_g, m_w_ukv, m_q_norm_g, m_k_norm_g, m_conv_w, m_conv_b, m_lru_wa, m_lru_ba, m_lru_wi, m_lru_bi, m_lru_lambda, m_attn_out_g, m_rnn_out_g, m_w_out, m_ln2_g, m_w_gate, m_w_up, m_w_down, v_meta_tokens, v_ln1_g, v_w_in, v_q_a_norm_g, v_w_uq, v_kv_a_norm_g, v_w_ukv, v_q_norm_g, v_k_norm_g, v_conv_w, v_conv_b, v_lru_wa, v_lru_ba, v_lru_wi, v_lru_bi, v_lru_lambda, v_attn_out_g, v_rnn_out_g, v_w_out, v_ln2_g, v_w_gate, v_w_up, v_w_down):
    given = (meta_tokens, ln1_g, w_in, q_a_norm_g, w_uq, kv_a_norm_g, w_ukv, q_norm_g, k_norm_g, conv_w, conv_b,
             lru_wa, lru_ba, lru_wi, lru_bi, lru_lambda, attn_out_g, rnn_out_g, w_out, ln2_g, w_gate, w_up, w_down)
    moments_m = (m_meta_tokens, m_ln1_g, m_w_in, m_q_a_norm_g, m_w_uq, m_kv_a_norm_g, m_w_ukv, m_q_norm_g, m_k_norm_g,
                 m_conv_w, m_conv_b, m_lru_wa, m_lru_ba, m_lru_wi, m_lru_bi, m_lru_lambda, m_attn_out_g, m_rnn_out_g,
                 m_w_out, m_ln2_g, m_w_gate, m_w_up, m_w_down)
    moments_v = (v_meta_tokens, v_ln1_g, v_w_in, v_q_a_norm_g, v_w_uq, v_kv_a_norm_g, v_w_ukv, v_q_norm_g, v_k_norm_g,
                 v_conv_w, v_conv_b, v_lru_wa, v_lru_ba, v_lru_wi, v_lru_bi, v_lru_lambda, v_attn_out_g, v_rnn_out_g,
                 v_w_out, v_ln2_g, v_w_gate, v_w_up, v_w_down)
    shapes = {k: a.shape for k, a in zip(WEIGHTS, given)}

    def two_d(k, a):
        a = a.reshape(-1, a.shape[-1])
        return a.T if k in TRANSPOSED else a

    w = {k: two_d(k, a) for k, a in zip(WEIGHTS, given)}
    m = {k: two_d(k, a) for k, a in zip(WEIGHTS, moments_m)}
    v = {k: two_d(k, a) for k, a in zip(WEIGHTS, moments_v)}

    grad_x, parts = _local_step(x, loss_target, _MeshExchange(w))

    tiled = ("w_in", "w_gate", "w_up", "w_down")
    new = {k: _adamw("adamw_" + k, parts[k], w[k], m[k], v[k]) for k in tiled}
    small = [k for k in WEIGHTS if k not in tiled]
    new.update(zip(small, _adamw_many("adamw_small", [(parts[k], w[k], m[k], v[k]) for k in small])))

    loss = jnp.sum(parts["loss"][:, 0, 0])
    outs = [loss, grad_x]
    for idx in range(4):
        outs += [(new[k][idx].T if k in TRANSPOSED else new[k][idx]).reshape(shapes[k]) for k in WEIGHTS]
    return tuple(outs)
```

```python
import functools
import math

import numpy as np
import jax
import jax.numpy as jnp
from jax import lax
from jax.experimental import pallas as pl
from jax.experimental.pallas import tpu as pltpu

F32 = jnp.float32
BF16 = jnp.bfloat16

D_MODEL = 1024
N_META = 16
SEQ = 2048
N_HEADS = 8
QK_NOPE = 64
QK_ROPE = 32
QK_HEAD = QK_NOPE + QK_ROPE
V_HEAD = 64
D_ATTN = N_HEADS * V_HEAD
Q_LORA = 384
KV_LORA = 256
D_RNN = 512
RNN_BW = 64
D_FF = 2816
EPS = 1e-6
LRU_C = 8.0
ROPE_THETA = 10000.0
OFF_CKV = Q_LORA + KV_LORA
OFF_KR = OFF_CKV + QK_ROPE
IN_COLS = OFF_KR + 2 * D_RNN

ADAM_LR = 0.001
ADAM_B1 = 0.9
ADAM_B2 = 0.999
ADAM_EPS = 1e-08
ADAM_WD = 0.01
ADAM_STEP = 10

N_DEV = 8
LANES = 128
HEAD_PAD = LANES
PAD_ROWS = LANES - N_META
QP_COLS = N_HEADS * HEAD_PAD
P_COLS = OFF_CKV + 2 * D_RNN + LANES
FF_CHUNK = D_FF
VMEM_LIMIT = 56 * 1024 * 1024
MESH = pl.DeviceIdType.MESH


def _t_pad():
    return PAD_ROWS + N_META + SEQ


def _row_tile(n):
    return 256 if n % 256 == 0 else 128


def _wide_row_tile(n):
    quarter = _t_pad() // 4
    return quarter if quarter % 16 == 0 and n % quarter == 0 else _row_tile(n)


def _const_spec(shape):
    nd = len(shape)
    return pl.BlockSpec(shape, lambda *_: (0,) * nd, pipeline_mode=pl.Buffered(1))


def _rms(x, d):
    r = lax.rsqrt(jnp.sum(x * x, axis=-1, keepdims=True) * (1.0 / d) + EPS)
    return x * r, r


def _rms_bwd(dy, xhat, r, g, d):
    dxh = dy * g
    return r * (dxh - xhat * (jnp.sum(dxh * xhat, axis=-1, keepdims=True) * (1.0 / d)))


def _colsum(x):
    return jnp.sum(x, axis=0, keepdims=True)


def _dot(a, b):
    return jnp.dot(a, b, preferred_element_type=F32)


def _dot_nt(a, b):
    return lax.dot_general(a, b, (((1,), (1,)), ((), ())), preferred_element_type=F32)


def _dot_tn(a, b):
    return lax.dot_general(a, b, (((0,), (0,)), ((), ())), preferred_element_type=F32)


def _rope(x, c, s1, s2):
    return x * c + pltpu.roll(x, 16, 1) * s1 + pltpu.roll(x, HEAD_PAD - 16, 1) * s2


def _rope_bwd(dy, c, s1, s2):
    return dy * c + pltpu.roll(dy * s1, HEAD_PAD - 16, 1) + pltpu.roll(dy * s2, 16, 1)


def _acc(ref, first, val):
    @pl.when(first)
    def _():
        ref[...] = val

    @pl.when(jnp.logical_not(first))
    def _():
        ref[...] += val


def _in_proj(h0, ln1_g, w_in_p, srcs=(), scatter=()):
    n = h0.shape[0]
    tm = _wide_row_tile(n)
    nk = len(srcs)
    c_in, c_out, c_shape, c_sems = _exchange_specs(srcs, scatter)

    def body(h_ref, g_ref, w_ref, *rest):
        hn_ref, cq_ref, ckv_ref, xr_ref, xg_ref, kr_ref = rest[nk:nk + 6]
        finish = _ride(1, *_exchange_fns(rest[:nk], rest[nk + 6:2 * nk + 6], rest[2 * nk + 6:], scatter))
        xhat, _ = _rms(h_ref[...], D_MODEL)
        hn = (xhat * g_ref[...]).astype(BF16)
        hn_ref[...] = hn
        p = _dot_nt(hn, w_ref[...])
        cq_ref[...] = p[:, :Q_LORA]
        ckv_ref[...] = p[:, Q_LORA:OFF_CKV]
        xr_ref[...] = p[:, OFF_CKV:OFF_CKV + D_RNN]
        xg_ref[...] = p[:, OFF_CKV + D_RNN:OFF_CKV + 2 * D_RNN]
        kr_ref[...] = p[:, OFF_CKV + 2 * D_RNN:]
        finish()

    def row(w):
        return pl.BlockSpec((tm, w), lambda i: (i, 0))

    widths = (D_MODEL, Q_LORA, KV_LORA, D_RNN, D_RNN, LANES)
    res = pl.pallas_call(
        body, name="in_proj", grid=(n // tm,),
        in_specs=[row(D_MODEL), _const_spec((1, D_MODEL)), _const_spec((P_COLS, D_MODEL))] + c_in,
        out_specs=[row(w) for w in widths] + c_out,
        out_shape=[jax.ShapeDtypeStruct((n, w), BF16 if k == 0 else F32) for k, w in enumerate(widths)] + c_shape,
        scratch_shapes=c_sems,
        compiler_params=pltpu.CompilerParams(dimension_semantics=("arbitrary",), vmem_limit_bytes=VMEM_LIMIT),
    )(h0, ln1_g, w_in_p, *srcs)
    return res[:6], res[6:]


def _qkv_fwd(cq, ckv, kr, gqa, gkva, w_uq_p, w_uk_p, w_v, qg, kg, rc, rs1, rs2):
    n = cq.shape[0]
    tm = _wide_row_tile(n)

    def body(cq_ref, ckv_ref, kr_ref, gqa_ref, gkva_ref, wuq_ref, wuk_ref, wv_ref, qg_ref, kg_ref,
             c_ref, s1_ref, s2_ref, q_ref, k_ref, v_ref):
        xq, _ = _rms(cq_ref[...], Q_LORA)
        qa = (xq * gqa_ref[...]).astype(BF16)
        q = _dot_nt(qa, wuq_ref[...])
        xkv, _ = _rms(ckv_ref[...], KV_LORA)
        kva = (xkv * gkva_ref[...]).astype(BF16)
        kn = _dot(kva, wuk_ref[...])
        v_ref[...] = _dot(kva, wv_ref[...]).astype(BF16)
        krp = kr_ref[...]
        c, s1, s2 = c_ref[...], s1_ref[...], s2_ref[...]
        for h in range(N_HEADS):
            sl = slice(h * HEAD_PAD, (h + 1) * HEAD_PAD)
            qh, _ = _rms(q[:, sl], QK_HEAD)
            q_ref[:, sl] = _rope(qh * qg_ref[...], c, s1, s2).astype(BF16)
            kh, _ = _rms(kn[:, sl] + krp, QK_HEAD)
            k_ref[:, sl] = _rope(kh * kg_ref[...], c, s1, s2).astype(BF16)

    def row(w):
        return pl.BlockSpec((tm, w), lambda i: (i, 0))

    return pl.pallas_call(
        body, name="qkv_fwd", grid=(n // tm,),
        in_specs=[row(Q_LORA), row(KV_LORA), row(LANES), _const_spec((1, Q_LORA)), _const_spec((1, KV_LORA)),
                  _const_spec((QP_COLS, Q_LORA)), _const_spec((KV_LORA, QP_COLS)), _const_spec((KV_LORA, D_ATTN)),
                  _const_spec((1, LANES)), _const_spec((1, LANES)), row(LANES), row(LANES), row(LANES)],
        out_specs=[row(QP_COLS), row(QP_COLS), row(D_ATTN)],
        out_shape=[jax.ShapeDtypeStruct((n, QP_COLS), BF16), jax.ShapeDtypeStruct((n, QP_COLS), BF16),
                   jax.ShapeDtypeStruct((n, D_ATTN), BF16)],
        compiler_params=pltpu.CompilerParams(dimension_semantics=("parallel",), vmem_limit_bytes=VMEM_LIMIT),
    )(cq, ckv, kr, gqa, gkva, w_uq_p, w_uk_p, w_v, qg, kg, rc, rs1, rs2)


def _qkv_bwd(cq, ckv, kr, dq_r, dk_r, dv, dxr, dxg, gqa, gkva, w_uq_p, w_uk_p, w_v, qg, kg, rc, rs1, rs2,
             srcs=(), scatter=()):
    n = cq.shape[0]
    tm = _wide_row_tile(n)
    nk = len(srcs)
    c_in, c_out, c_shape, c_sems = _exchange_specs(srcs, scatter)

    def body(cq_ref, ckv_ref, kr_ref, dq_ref, dk_ref, dv_ref, dxr_ref, dxg_ref, gqa_ref, gkva_ref, wuq_ref, wuk_ref,
             wv_ref, qg_ref, kg_ref, c_ref, s1_ref, s2_ref, *rest):
        dp_ref, qa_ref, kva_ref, dqp_ref, dkv_ref, dqg_ref, dkg_ref, dgqa_ref, dgkva_ref = rest[nk:nk + 9]
        finish = _ride(1, *_exchange_fns(rest[:nk], rest[nk + 9:2 * nk + 9], rest[2 * nk + 9:], scatter))
        first = pl.program_id(0) == 0
        dp_ref[:, OFF_CKV:OFF_CKV + D_RNN] = dxr_ref[...].astype(BF16)
        dp_ref[:, OFF_CKV + D_RNN:OFF_CKV + 2 * D_RNN] = dxg_ref[...].astype(BF16)
        xq, rq = _rms(cq_ref[...], Q_LORA)
        qa = (xq * gqa_ref[...]).astype(BF16)
        qa_ref[...] = qa
        q = _dot_nt(qa, wuq_ref[...])
        xkv, rkv = _rms(ckv_ref[...], KV_LORA)
        kva = (xkv * gkva_ref[...]).astype(BF16)
        kva_ref[...] = kva
        kn = _dot(kva, wuk_ref[...])
        krp = kr_ref[...]
        c, s1, s2 = c_ref[...], s1_ref[...], s2_ref[...]
        lane = lax.broadcasted_iota(jnp.int32, (tm, HEAD_PAD), 1)
        rope_lanes = jnp.logical_and(lane >= QK_NOPE, lane < QK_HEAD)
        dqg = jnp.zeros((1, HEAD_PAD), F32)
        dkg = jnp.zeros((1, HEAD_PAD), F32)
        dkr = jnp.zeros((tm, HEAD_PAD), F32)
        for h in range(N_HEADS):
            sl = slice(h * HEAD_PAD, (h + 1) * HEAD_PAD)
            qh, rqh = _rms(q[:, sl], QK_HEAD)
            dy = _rope_bwd(dq_ref[:, sl], c, s1, s2)
            dqg = dqg + _colsum(dy * qh)
            dqp_ref[:, sl] = _rms_bwd(dy, qh, rqh, qg_ref[...], QK_HEAD).astype(BF16)
            kh, rkh = _rms(kn[:, sl] + krp, QK_HEAD)
            dyk = _rope_bwd(dk_ref[:, sl], c, s1, s2)
            dkg = dkg + _colsum(dyk * kh)
            dkh = _rms_bwd(dyk, kh, rkh, kg_ref[...], QK_HEAD)
            dkv_ref[:, sl] = dkh.astype(BF16)
            dkr = dkr + jnp.where(rope_lanes, dkh, 0.0)
        dkv_ref[:, QP_COLS:] = dv_ref[...].astype(BF16)
        dp_ref[:, OFF_CKV + 2 * D_RNN:] = dkr.astype(BF16)
        dqa = _dot(dqp_ref[...], wuq_ref[...])
        dp_ref[:, :Q_LORA] = _rms_bwd(dqa, xq, rq, gqa_ref[...], Q_LORA).astype(BF16)
        dkva = _dot_nt(dkv_ref[:, :QP_COLS], wuk_ref[...]) + _dot_nt(dkv_ref[:, QP_COLS:], wv_ref[...])
        dp_ref[:, Q_LORA:OFF_CKV] = _rms_bwd(dkva, xkv, rkv, gkva_ref[...], KV_LORA).astype(BF16)
        _acc(dqg_ref, first, dqg)
        _acc(dkg_ref, first, dkg)
        _acc(dgqa_ref, first, _colsum(dqa * xq))
        _acc(dgkva_ref, first, _colsum(dkva * xkv))
        finish()

    def row(w):
        return pl.BlockSpec((tm, w), lambda i: (i, 0))

    def acc(w):
        return pl.BlockSpec((1, w), lambda i: (0, 0))

    res = pl.pallas_call(
        body, name="qkv_bwd", grid=(n // tm,),
        in_specs=[row(Q_LORA), row(KV_LORA), row(LANES), row(QP_COLS), row(QP_COLS), row(D_ATTN), row(D_RNN), row(D_RNN),
                  _const_spec((1, Q_LORA)), _const_spec((1, KV_LORA)),
                  _const_spec((QP_COLS, Q_LORA)), _const_spec((KV_LORA, QP_COLS)), _const_spec((KV_LORA, D_ATTN)),
                  _const_spec((1, LANES)), _const_spec((1, LANES)), row(LANES), row(LANES), row(LANES)] + c_in,
        out_specs=[row(P_COLS), row(Q_LORA), row(KV_LORA), row(QP_COLS),
                   row(QP_COLS + D_ATTN), acc(LANES), acc(LANES), acc(Q_LORA), acc(KV_LORA)] + c_out,
        out_shape=[jax.ShapeDtypeStruct((n, P_COLS), BF16), jax.ShapeDtypeStruct((n, Q_LORA), BF16),
                   jax.ShapeDtypeStruct((n, KV_LORA), BF16), jax.ShapeDtypeStruct((n, QP_COLS), BF16),
                   jax.ShapeDtypeStruct((n, QP_COLS + D_ATTN), BF16),
                   jax.ShapeDtypeStruct((1, LANES), F32), jax.ShapeDtypeStruct((1, LANES), F32),
                   jax.ShapeDtypeStruct((1, Q_LORA), F32), jax.ShapeDtypeStruct((1, KV_LORA), F32)] + c_shape,
        scratch_shapes=c_sems,
        compiler_params=pltpu.CompilerParams(dimension_semantics=("arbitrary",), vmem_limit_bytes=VMEM_LIMIT),
    )(cq, ckv, kr, dq_r, dk_r, dv, dxr, dxg, gqa, gkva, w_uq_p, w_uk_p, w_v, qg, kg, rc, rs1, rs2, *srcs)
    return res[:9], res[9:]


KEY_CHUNK = 4 * LANES


def _key_chunks(t):
    count = max(t // KEY_CHUNK, 1)
    first = t - KEY_CHUNK * (count - 1)
    return [(0, first)] + [(first + KEY_CHUNK * c, KEY_CHUNK) for c in range(count - 1)]


def _attn_specs(t, tq):
    nq = t // tq
    qspec = pl.BlockSpec((tq, 2 * HEAD_PAD), lambda b, hp, i: (b * nq + i, hp))
    kspec = pl.BlockSpec((t, 2 * HEAD_PAD), lambda b, hp, i: (b, hp))
    vspec = pl.BlockSpec((t, 2 * V_HEAD), lambda b, hp, i: (b, hp))
    ospec = pl.BlockSpec((tq, 2 * V_HEAD), lambda b, hp, i: (b * nq + i, hp))
    return nq, qspec, kspec, vspec, ospec


def _probs_spec(t, tq):
    return pl.BlockSpec((1, 2, tq, t), lambda b, hp, i: (b, hp, i, 0))


def _attn_fwd(q, k, v, srcs=(), scatter=()):
    n = q.shape[0]
    t = _t_pad()
    tq = t // 2
    nq, qspec, kspec, vspec, ospec = _attn_specs(t, tq)
    nk = len(srcs)
    c_in, c_out, c_shape, c_sems = _exchange_specs(srcs, scatter)

    def body(q_ref, k_ref, v_ref, *rest):
        o_ref, l_ref, p_ref = rest[nk:nk + 3]
        finish = _ride(3, *_exchange_fns(rest[:nk], rest[nk + 3:2 * nk + 3], rest[2 * nk + 3:], scatter))
        lane = lax.broadcasted_iota(jnp.int32, (tq, 2 * V_HEAD), 1)
        outs = []
        sums = []
        for j in range(2):
            sl = slice(j * HEAD_PAD, (j + 1) * HEAD_PAD)
            qh = q_ref[:, sl]

            def scores(start, size):
                s = _dot_nt(qh, k_ref[start:start + size, sl])
                if start < PAD_ROWS:
                    key = lax.broadcasted_iota(jnp.int32, (tq, size), 1) + start
                    s = jnp.where(key >= PAD_ROWS, s, -jnp.inf)
                return s

            top = functools.reduce(jnp.maximum, [jnp.max(scores(*c), axis=-1, keepdims=True) for c in _key_chunks(t)])
            l = jnp.zeros((tq, 1), F32)
            pv = jnp.zeros((tq, 2 * V_HEAD), F32)
            for start, size in _key_chunks(t):
                e = jnp.exp((scores(start, size) - top) * (QK_HEAD ** -0.5))
                l = l + jnp.sum(e, axis=-1, keepdims=True)
                e = e.astype(BF16)
                p_ref[0, j, :, start:start + size] = e
                pv = pv + _dot(e, v_ref[start:start + size, :])
            outs.append(pv / l)
            sums.append(l)
        o_ref[...] = jnp.where(lane < V_HEAD, outs[0], outs[1])
        l_ref[...] = jnp.where(lane < V_HEAD, sums[0], sums[1])
        finish()

    res = pl.pallas_call(
        body, name="attn_fwd", grid=(n // t, N_HEADS // 2, nq),
        in_specs=[qspec, kspec, vspec] + c_in, out_specs=[ospec, ospec, _probs_spec(t, tq)] + c_out,
        out_shape=[jax.ShapeDtypeStruct((n, D_ATTN), F32), jax.ShapeDtypeStruct((n, D_ATTN), F32),
                   jax.ShapeDtypeStruct((n // t, N_HEADS, t, t), BF16)] + c_shape,
        scratch_shapes=c_sems,
        compiler_params=pltpu.CompilerParams(dimension_semantics=("arbitrary", "arbitrary", "arbitrary"),
                                             vmem_limit_bytes=VMEM_LIMIT),
    )(q, k, v, *srcs)
    return res[0], (res[1], res[2]), res[3:]


def _attn_bwd(q, k, v, do, o, probs, srcs=(), scatter=()):
    n = q.shape[0]
    t = _t_pad()
    tq = t // 2
    nq, qspec, kspec, vspec, ospec = _attn_specs(t, tq)
    nk = len(srcs)
    c_in, c_out, c_shape, c_sems = _exchange_specs(srcs, scatter)

    def body(q_ref, k_ref, v_ref, do_ref, o_ref, l_ref, p_ref, *rest):
        dq_ref, dk_ref, dv_ref = rest[nk:nk + 3]
        finish = _ride(3, *_exchange_fns(rest[:nk], rest[nk + 3:2 * nk + 3], rest[2 * nk + 3:], scatter))

        @pl.when(pl.program_id(2) == 0)
        def _():
            dk_ref[...] = jnp.zeros_like(dk_ref)
            dv_ref[...] = jnp.zeros_like(dv_ref)

        lane = lax.broadcasted_iota(jnp.int32, (tq, 2 * V_HEAD), 1)
        do = do_ref[...]
        do_o = do * o_ref[...]
        chunks = _key_chunks(t)
        dvs = [None] * len(chunks)
        for j in range(2):
            sl = slice(j * HEAD_PAD, (j + 1) * HEAD_PAD)
            qh = q_ref[:, sl]
            in_head = (lane < V_HEAD) if j == 0 else (lane >= V_HEAD)
            inv_l = 1.0 / l_ref[:, j * V_HEAD:j * V_HEAD + 1]
            doh = jnp.where(in_head, do, 0.0).astype(BF16)
            doh_n = jnp.where(in_head, do * inv_l, 0.0).astype(BF16)
            delta = jnp.sum(jnp.where(in_head, do_o, 0.0), axis=-1, keepdims=True)
            row_scale = inv_l * (QK_HEAD ** -0.5)
            dq = jnp.zeros((tq, HEAD_PAD), F32)
            for c, (start, size) in enumerate(chunks):
                rows = slice(start, start + size)
                e = p_ref[0, j, :, rows]
                dp = _dot_nt(doh, v_ref[rows, :])
                ds = (e.astype(F32) * (dp - delta) * row_scale).astype(BF16)
                dq = dq + _dot(ds, k_ref[rows, sl])
                dk_ref[rows, sl] += _dot_tn(ds, qh)
                dvc = _dot_tn(e, doh_n)
                dvs[c] = dvc if dvs[c] is None else dvs[c] + dvc
            dq_ref[:, sl] = dq
        for (start, size), dvc in zip(chunks, dvs):
            dv_ref[start:start + size, :] += dvc
        finish()

    res = pl.pallas_call(
        body, name="attn_bwd", grid=(n // t, N_HEADS // 2, nq),
        in_specs=[qspec, kspec, vspec, ospec, ospec, ospec, _probs_spec(t, tq)] + c_in,
        out_specs=[qspec, kspec, vspec] + c_out,
        out_shape=[jax.ShapeDtypeStruct((n, QP_COLS), F32), jax.ShapeDtypeStruct((n, QP_COLS), F32),
                   jax.ShapeDtypeStruct((n, D_ATTN), F32)] + c_shape, scratch_shapes=c_sems,
        compiler_params=pltpu.CompilerParams(dimension_semantics=("arbitrary", "arbitrary", "arbitrary"),
                                             vmem_limit_bytes=VMEM_LIMIT),
    )(q, k, v, do, o, *probs, *srcs)
    return res[:3], res[3:]


SCAN_STEPS = 8


def _scan(chains, t):
    seg = t // 8
    rows = lax.broadcasted_iota(jnp.int32, (8, LANES), 0)

    def step(i, carry):
        carry = list(carry)
        for u in range(SCAN_STEPS):
            j = i * SCAN_STEPS + u
            for n, (a_ref, b_ref, h_ref, p_ref, reverse) in enumerate(chains):
                h, p = carry[n]
                idx = pl.ds(seg - 1 - j if reverse else j, 8, stride=seg)
                a = a_ref[idx, :]
                h = a * h + b_ref[idx, :]
                p = a * p
                h_ref[idx, :] = h
                p_ref[idx, :] = p
                carry[n] = (h, p)
        return tuple(carry)

    init = tuple((jnp.zeros((8, LANES), F32), jnp.ones((8, LANES), F32)) for _ in chains)
    ends = lax.fori_loop(0, seg // SCAN_STEPS, step, init)
    for (_, _, h_ref, p_ref, reverse), (b, a) in zip(chains, ends):
        for d in (1, 2, 4):
            if reverse:
                keep = rows < 8 - d
                a_n, b_n = pltpu.roll(a, 8 - d, 0), pltpu.roll(b, 8 - d, 0)
            else:
                keep = rows >= d
                a_n, b_n = pltpu.roll(a, d, 0), pltpu.roll(b, d, 0)
            b = a * jnp.where(keep, b_n, 0.0) + b
            a = a * jnp.where(keep, a_n, 1.0)
        for s in (range(7) if reverse else range(1, 8)):
            sl = slice(s * seg, (s + 1) * seg)
            carry_in = b[s + 1:s + 2, :] if reverse else b[s - 1:s, :]
            h_ref[sl, :] = h_ref[sl, :] + p_ref[sl, :] * carry_in


def _shift_rows(x, s, rows, t):
    if s == 0:
        return x
    rolled = pltpu.roll(x, s % t, 0)
    return jnp.where(rows >= s, rolled, 0.0) if s > 0 else jnp.where(rows < t + s, rolled, 0.0)


def _neg_expm1(x, exp_x):
    series = -x * (1.0 + x * (0.5 + x * (1.0 / 6 + x * (1.0 / 24))))
    return jnp.where(x > -0.1, series, 1.0 - exp_x)


def _sigmoid(x):
    return 0.5 * jnp.tanh(0.5 * x) + 0.5


def _gelu_parts(x):
    k = math.sqrt(2.0 / math.pi)
    th = jnp.tanh(k * (x + 0.044715 * x * x * x))
    g = 0.5 * x * (1.0 + th)
    dg = 0.5 * (1.0 + th) + 0.5 * x * (1.0 - th * th) * k * (1.0 + 3 * 0.044715 * x * x)
    return g, dg


def _lru_gates(xc, gates, lam_ref, valid, d):
    r = _sigmoid(gates[:, (2 * d) * LANES:(2 * d + 1) * LANES])
    i = _sigmoid(gates[:, (2 * d + 1) * LANES:(2 * d + 2) * LANES])
    neg_lam = -lam_ref[d:d + 1, :]
    sp = jnp.maximum(neg_lam, 0.0) + jnp.log1p(jnp.exp(-jnp.abs(neg_lam)))
    log_a = -LRU_C * r * sp
    a = jnp.exp(log_a)
    m = jnp.maximum(_neg_expm1(2.0 * log_a, a * a), 0.0)
    sq = jnp.sqrt(m)
    b = jnp.where(valid, sq * (i * xc), 0.0)
    return r, i, sp, a, m, sq, b


def _conv(xr, cw_ref, cb_ref, rows, t):
    return (cw_ref[0:1, :] * _shift_rows(xr, 2, rows, t) + cw_ref[1:2, :] * _shift_rows(xr, 1, rows, t)
            + cw_ref[2:3, :] * xr + cw_ref[3:4, :] * _shift_rows(xr, -1, rows, t) + cb_ref[...])


def _rnn_specs(t):
    seq = pl.BlockSpec((t, LANES), lambda cb, b: (b, cb))
    cw = pl.BlockSpec((4, LANES), lambda cb, b: (0, cb))
    vec1 = pl.BlockSpec((1, LANES), lambda cb, b: (0, cb))
    vec2 = pl.BlockSpec((2, LANES), lambda cb, b: (0, cb))
    wblk = pl.BlockSpec((1, LANES, 4 * LANES), lambda cb, b: (cb, 0, 0))
    gbias = pl.BlockSpec((1, 1, 4 * LANES), lambda cb, b: (cb, 0, 0))
    return seq, cw, vec1, vec2, wblk, gbias


def _rnn_fwd(xr, xg, conv_w, conv_b, wblk, gbias, lam):
    n = xr.shape[0]
    t = _t_pad()
    seq, cw, vec1, vec2, wspec, gspec = _rnn_specs(t)

    def body(xr_ref, xg_ref, cw_ref, cb_ref, w_ref, gb_ref, lam_ref, o_ref, a_s, b_s, h_s, p_s):
        rows = lax.broadcasted_iota(jnp.int32, (t, LANES), 0)
        valid = rows >= PAD_ROWS
        xc = _conv(xr_ref[...], cw_ref, cb_ref, rows, t)
        gates = _dot(xc.astype(BF16), w_ref[0]) + gb_ref[0]
        for d in range(2):
            _, _, _, a, _, _, b = _lru_gates(xc, gates, lam_ref, valid, d)
            a_s[d] = a
            b_s[d] = b
        _scan([(a_s.at[d], b_s.at[d], h_s.at[d], p_s.at[d], d == 1) for d in range(2)], t)
        g, _ = _gelu_parts(xg_ref[...])
        o_ref[...] = (h_s[0] + h_s[1]) * g

    return pl.pallas_call(
        body, name="rnn_fwd", grid=(D_RNN // LANES, n // t),
        in_specs=[seq, seq, cw, vec1, wspec, gspec, vec2], out_specs=seq,
        out_shape=jax.ShapeDtypeStruct((n, D_RNN), F32),
        scratch_shapes=[pltpu.VMEM((2, t, LANES), F32)] * 4,
        compiler_params=pltpu.CompilerParams(dimension_semantics=("parallel", "parallel"), vmem_limit_bytes=VMEM_LIMIT),
    )(xr, xg, conv_w, conv_b, wblk, gbias, lam)


def _rnn_bwd(xr, xg, do, conv_w, conv_b, wblk, gbias, lam, srcs=(), scatter=()):
    n = xr.shape[0]
    t = _t_pad()
    seq, cw, vec1, vec2, wspec, gspec = _rnn_specs(t)
    nk = len(srcs)
    c_in, c_out, c_shape, c_sems = _exchange_specs(srcs, scatter)

    def body(xr_ref, xg_ref, do_ref, cw_ref, cb_ref, w_ref, gb_ref, lam_ref, *rest):
        dxr_ref, dxg_ref, dcw_ref, dcb_ref, dw_ref, dgb_ref, dlam_ref = rest[nk:nk + 7]
        a_s, b_s, h_s, l_s, p_s, back_s, r_s, i_s, q_s, dg_s = rest[2 * nk + 7 + len(c_sems):]
        finish = _ride(2, *_exchange_fns(rest[:nk], rest[nk + 7:2 * nk + 7], rest[2 * nk + 7:2 * nk + 7 + len(c_sems)],
                                         scatter))
        first = pl.program_id(1) == 0
        rows = lax.broadcasted_iota(jnp.int32, (t, LANES), 0)
        valid = rows >= PAD_ROWS
        xr = xr_ref[...]
        xc = _conv(xr, cw_ref, cb_ref, rows, t)
        xcb = xc.astype(BF16)
        gates = _dot(xcb, w_ref[0]) + gb_ref[0]
        sps = []
        for d in range(2):
            r_s[d], i_s[d], sp, a_s[d], _, q_s[d], b_s[d] = _lru_gates(xc, gates, lam_ref, valid, d)
            sps.append(sp)
        _scan([(a_s.at[d], b_s.at[d], h_s.at[d], p_s.at[d], d == 1) for d in range(2)], t)
        g, dg = _gelu_parts(xg_ref[...])
        do = do_ref[...]
        dxg_ref[...] = do * (h_s[0] + h_s[1]) * dg
        b_s[0] = do * g
        for d in range(2):
            back_s[d] = _shift_rows(a_s[d], -1 if d == 0 else 1, rows, t)
        _scan([(back_s.at[d], b_s.at[0], l_s.at[d], p_s.at[d], d == 0) for d in range(2)], t)
        dxc = jnp.zeros((t, LANES), F32)
        dlams = []
        for d in range(2):
            r, i, sp, a, sq = r_s[d], i_s[d], sps[d], a_s[d], q_s[d]
            lam_t = l_s[d]
            da = lam_t * _shift_rows(h_s[d], 1 if d == 0 else -1, rows, t)
            lam_v = jnp.where(valid, lam_t, 0.0)
            dsq = lam_v * (i * xc)
            di = lam_v * sq * xc
            dxc = dxc + lam_v * sq * i
            dm = jnp.where(sq > 0.0, dsq * 0.5 / jnp.where(sq > 0.0, sq, 1.0), 0.0)
            dla = da * a - 2.0 * dm * a * a
            dr = dla * (-LRU_C) * sp
            dsp = _colsum(dla * (-LRU_C) * r)
            dlams.append(dsp * -jax.nn.sigmoid(-lam_ref[d:d + 1, :]))
            dg_s[:, (2 * d) * LANES:(2 * d + 1) * LANES] = (dr * r * (1.0 - r)).astype(BF16)
            dg_s[:, (2 * d + 1) * LANES:(2 * d + 2) * LANES] = (di * i * (1.0 - i)).astype(BF16)
        dgates = dg_s[...]
        dxc = dxc + _dot_nt(dgates, w_ref[0])
        taps = [_shift_rows(dxc, j - 2, rows, t) for j in range(4)]
        dxr_ref[...] = (cw_ref[0:1, :] * taps[0] + cw_ref[1:2, :] * taps[1] + cw_ref[2:3, :] * taps[2]
                        + cw_ref[3:4, :] * taps[3])
        dcw = jnp.concatenate([_colsum(tap * xr) for tap in taps], axis=0)
        _acc(dcw_ref, first, dcw)
        _acc(dcb_ref, first, _colsum(dxc))
        _acc(dw_ref, first, _dot_tn(xcb, dgates)[None])
        _acc(dgb_ref, first, _colsum(dgates.astype(F32))[None])
        _acc(dlam_ref, first, jnp.concatenate(dlams, axis=0))
        finish()

    res = pl.pallas_call(
        body, name="rnn_bwd", grid=(D_RNN // LANES, n // t),
        in_specs=[seq, seq, seq, cw, vec1, wspec, gspec, vec2] + c_in,
        out_specs=[seq, seq, cw, vec1, wspec, gspec, vec2] + c_out,
        out_shape=[jax.ShapeDtypeStruct((n, D_RNN), F32), jax.ShapeDtypeStruct((n, D_RNN), F32),
                   jax.ShapeDtypeStruct((4, D_RNN), F32), jax.ShapeDtypeStruct((1, D_RNN), F32),
                   jax.ShapeDtypeStruct((D_RNN // LANES, LANES, 4 * LANES), F32),
                   jax.ShapeDtypeStruct((D_RNN // LANES, 1, 4 * LANES), F32), jax.ShapeDtypeStruct((2, D_RNN), F32)]
        + c_shape,
        scratch_shapes=c_sems + [pltpu.VMEM((2, t, LANES), F32)] * 9 + [pltpu.VMEM((t, 4 * LANES), BF16)],
        compiler_params=pltpu.CompilerParams(dimension_semantics=("arbitrary", "arbitrary"), vmem_limit_bytes=VMEM_LIMIT),
    )(xr, xg, do, conv_w, conv_b, wblk, gbias, lam, *srcs)
    return res[:7], res[7:]


def _post(oa, orn, h0, tgt, ga, gr, g2, w_out, w_gate, w_up, w_down):
    n = oa.shape[0]
    tm = _row_tile(n)
    t = _t_pad()
    head = PAD_ROWS + N_META
    parts = tm // head

    def body(oa_ref, or_ref, h0_ref, *rest):
        tgt_refs = rest[:parts]
        (ga_ref, gr_ref, g2_ref, wo_ref, wg_ref, wu_ref, wd_ref,
         doa_ref, dor_ref, dh1_ref, mix_ref, h1n_ref, act_ref, dgate_ref, dup_ref, dy_ref,
         loss_ref, dga_ref, dgr_ref, dg2_ref, gate_s, up_s) = rest[parts:]
        first = pl.program_id(0) == 0
        xa, ra = _rms(oa_ref[...], D_ATTN)
        xr, rr = _rms(or_ref[...], D_RNN)
        mix = jnp.concatenate([(xa * ga_ref[...]).astype(BF16), (xr * gr_ref[...]).astype(BF16)], axis=-1)
        mix_ref[...] = mix.T
        h1 = h0_ref[...] + _dot(mix, wo_ref[...])
        x2, r2 = _rms(h1, D_MODEL)
        h1n = (x2 * g2_ref[...]).astype(BF16)
        h1n_ref[...] = h1n
        y = h1
        for cs in range(0, D_FF, FF_CHUNK):
            sl = slice(cs, cs + FF_CHUNK)
            gate = _dot_nt(h1n, wg_ref[sl, :])
            up = _dot_nt(h1n, wu_ref[sl, :])
            gate_s[:, sl] = gate
            up_s[:, sl] = up
            act = (gate * _sigmoid(gate) * up).astype(BF16)
            act_ref[sl, :] = act.T
            y = y + _dot(act, wd_ref[sl, :])
        row = pl.program_id(0) * tm + lax.broadcasted_iota(jnp.int32, (tm, 1), 0)
        for _ in range(1, n // t):
            row = jnp.where(row >= t, row - t, row)
        tgt = jnp.concatenate([ref[0] for ref in tgt_refs], axis=0)
        err = jnp.where(row >= PAD_ROWS + N_META, y - tgt, 0.0)
        _acc(loss_ref, first, jnp.full((1, LANES), 0.5 / D_MODEL, F32) * jnp.sum(err * err))
        dy = err * (1.0 / D_MODEL)
        dyb = dy.astype(BF16)
        dy_ref[...] = dyb
        dh1n = jnp.zeros((tm, D_MODEL), F32)
        for cs in range(0, D_FF, FF_CHUNK):
            sl = slice(cs, cs + FF_CHUNK)
            dact = _dot_nt(dyb, wd_ref[sl, :])
            gate, up = gate_s[:, sl], up_s[:, sl]
            sg = _sigmoid(gate)
            dgate = (dact * up * sg * (1.0 + gate * (1.0 - sg))).astype(BF16)
            dup = (dact * gate * sg).astype(BF16)
            dgate_ref[sl, :] = dgate.T
            dup_ref[sl, :] = dup.T
            dh1n = dh1n + _dot(dgate, wg_ref[sl, :]) + _dot(dup, wu_ref[sl, :])
        _acc(dg2_ref, first, _colsum(dh1n * x2))
        dh1 = dy + _rms_bwd(dh1n, x2, r2, g2_ref[...], D_MODEL)
        dh1_ref[...] = dh1
        dmix = _dot_nt(dh1.astype(BF16), wo_ref[...])
        dma, dmr = dmix[:, :D_ATTN], dmix[:, D_ATTN:]
        _acc(dga_ref, first, _colsum(dma * xa))
        _acc(dgr_ref, first, _colsum(dmr * xr))
        doa_ref[...] = _rms_bwd(dma, xa, ra, ga_ref[...], D_ATTN)
        dor_ref[...] = _rms_bwd(dmr, xr, rr, gr_ref[...], D_RNN)

    def row(w):
        return pl.BlockSpec((tm, w), lambda i: (i, 0))

    def acc(w):
        return pl.BlockSpec((1, w), lambda i: (0, 0))

    def col(w):
        return pl.BlockSpec((w, tm), lambda i: (0, i))

    outs = [(D_ATTN, F32, row), (D_RNN, F32, row), (D_MODEL, F32, row), (D_MODEL, BF16, col), (D_MODEL, BF16, row),
            (D_FF, BF16, col), (D_FF, BF16, col), (D_FF, BF16, col), (D_MODEL, BF16, row)]
    accs = [LANES, D_ATTN, D_RNN, D_MODEL]
    per = t // head

    def target_part(p):
        def index(i):
            block = i * parts + p
            return block // per, jnp.maximum(block % per - 1, 0), 0
        return pl.BlockSpec((1, head, D_MODEL), index)

    return pl.pallas_call(
        body, name="post", grid=(n // tm,),
        in_specs=[row(D_ATTN), row(D_RNN), row(D_MODEL)] + [target_part(p) for p in range(parts)] + [
                  _const_spec((1, D_ATTN)), _const_spec((1, D_RNN)), _const_spec((1, D_MODEL)),
                  _const_spec((D_MODEL, D_MODEL)), _const_spec((D_FF, D_MODEL)), _const_spec((D_FF, D_MODEL)),
                  _const_spec((D_FF, D_MODEL))],
        out_specs=[spec(w) for w, _, spec in outs] + [acc(w) for w in accs],
        out_shape=[jax.ShapeDtypeStruct((n, w) if spec is row else (w, n), dt) for w, dt, spec in outs]
        + [jax.ShapeDtypeStruct((1, w), F32) for w in accs],
        scratch_shapes=[pltpu.VMEM((tm, D_FF), F32), pltpu.VMEM((tm, D_FF), F32)],
        compiler_params=pltpu.CompilerParams(dimension_semantics=("arbitrary",), vmem_limit_bytes=VMEM_LIMIT),
    )(oa, orn, h0, *[tgt] * parts, ga, gr, g2, w_out, w_gate, w_up, w_down)


def _in_bwd(dp, h0, dh1, ln1_g, w_in_p, srcs=(), scatter=()):
    n = h0.shape[0]
    tm = _row_tile(n)
    nk = len(srcs)
    c_in, c_out, c_shape, c_sems = _exchange_specs(srcs, scatter)

    def body(dp_ref, h0_ref, dh1_ref, g_ref, w_ref, *rest):
        dh0_ref, dg_ref = rest[nk:nk + 2]
        finish = _ride(1, *_exchange_fns(rest[:nk], rest[nk + 2:2 * nk + 2], rest[2 * nk + 2:], scatter))
        dhn = _dot(dp_ref[...], w_ref[...])
        xhat, r = _rms(h0_ref[...], D_MODEL)
        _acc(dg_ref, pl.program_id(0) == 0, _colsum(dhn * xhat))
        dh0_ref[...] = dh1_ref[...] + _rms_bwd(dhn, xhat, r, g_ref[...], D_MODEL)
        finish()

    def row(w):
        return pl.BlockSpec((tm, w), lambda i: (i, 0))

    res = pl.pallas_call(
        body, name="in_bwd", grid=(n // tm,),
        in_specs=[row(P_COLS), row(D_MODEL), row(D_MODEL), _const_spec((1, D_MODEL)), _const_spec((P_COLS, D_MODEL))] + c_in,
        out_specs=[row(D_MODEL), pl.BlockSpec((1, D_MODEL), lambda i: (0, 0))] + c_out,
        out_shape=[jax.ShapeDtypeStruct((n, D_MODEL), F32), jax.ShapeDtypeStruct((1, D_MODEL), F32)] + c_shape,
        scratch_shapes=c_sems,
        compiler_params=pltpu.CompilerParams(dimension_semantics=("arbitrary",), vmem_limit_bytes=VMEM_LIMIT),
    )(dp, h0, dh1, ln1_g, w_in_p, *srcs)
    return res[:2], res[2:]


MAX_TILE = D_FF // 2


def _pick_tile(width, cap):
    best = LANES
    for mult in range(1, width // LANES + 1):
        cand = mult * LANES
        if width % cand == 0 and cand <= cap:
            best = cand
    return best


def _matmul_tn(name, a, b, srcs=(), scatter=()):
    n, ka = a.shape
    kb = b.shape[1]
    ta, tb = _pick_tile(ka, MAX_TILE), _pick_tile(kb, MAX_TILE)
    tk = n // 2
    nk = len(srcs)
    c_in, c_out, c_shape, c_sems = _exchange_specs(srcs, scatter)

    def body(a_ref, b_ref, *rest):
        o_ref = rest[nk]
        finish = _ride(3, *_exchange_fns(rest[:nk], rest[nk + 1:2 * nk + 1], rest[2 * nk + 1:], scatter))
        _acc(o_ref, pl.program_id(2) == 0, _dot_tn(a_ref[...].astype(BF16), b_ref[...].astype(BF16)))
        finish()

    res = pl.pallas_call(
        body, name=name, grid=(ka // ta, kb // tb, n // tk),
        in_specs=[pl.BlockSpec((tk, ta), lambda i, j, k: (k, i)), pl.BlockSpec((tk, tb), lambda i, j, k: (k, j))] + c_in,
        out_specs=[pl.BlockSpec((ta, tb), lambda i, j, k: (i, j))] + c_out,
        out_shape=[jax.ShapeDtypeStruct((ka, kb), F32)] + c_shape, scratch_shapes=c_sems,
        compiler_params=pltpu.CompilerParams(dimension_semantics=("arbitrary", "arbitrary", "arbitrary"),
                                             vmem_limit_bytes=VMEM_LIMIT),
    )(a, b, *srcs)
    return res[0], res[1:]


def _matmul_shards(name, at, b):
    ka, n = at.shape
    kb = b.shape[1]
    width = ka // N_DEV
    per = 2 if 2 * width >= 4 * LANES else 4
    ta = per * width

    def body(a_ref, b_ref, o_ref):
        out = _dot(a_ref[...], b_ref[...].astype(BF16))
        for s in range(per):
            o_ref[s] = out[s * width:(s + 1) * width, :].astype(BF16)

    return pl.pallas_call(
        body, name=name, grid=(ka // ta,),
        in_specs=[pl.BlockSpec((ta, n), lambda i: (i, 0)), _const_spec((n, kb))],
        out_specs=pl.BlockSpec((per, width, kb), lambda i: (i, 0, 0)),
        out_shape=jax.ShapeDtypeStruct((N_DEV, width, kb), BF16),
        compiler_params=pltpu.CompilerParams(dimension_semantics=("parallel",), vmem_limit_bytes=VMEM_LIMIT),
    )(at, b)


def _adamw_math(g8_ref, w_ref, m_ref, v_ref, g_ref, d_ref, nm_ref, nv_ref):
    g = g8_ref[0].astype(F32)
    for s in range(1, N_DEV):
        g = g + g8_ref[s].astype(F32)
    g_ref[...] = g
    nm = ADAM_B1 * m_ref[...] + (1.0 - ADAM_B1) * g
    nv = ADAM_B2 * v_ref[...] + (1.0 - ADAM_B2) * (g * g)
    nm_ref[...] = nm
    nv_ref[...] = nv
    m_hat = nm / (1.0 - ADAM_B1 ** ADAM_STEP)
    v_hat = nv / (1.0 - ADAM_B2 ** ADAM_STEP)
    d_ref[...] = -ADAM_LR * (m_hat / (jnp.sqrt(v_hat) + ADAM_EPS) + ADAM_WD * w_ref[...])


def _adamw_many(name, items):
    count = len(items)

    def body(*refs):
        ins, outs = refs[:4 * count], refs[4 * count:]
        for i in range(count):
            _adamw_math(*ins[4 * i:4 * i + 4], *outs[4 * i:4 * i + 4])

    flat = [a for item in items for a in item]
    res = pl.pallas_call(
        body, name=name,
        out_shape=[jax.ShapeDtypeStruct(item[1].shape, F32) for item in items for _ in range(4)],
        compiler_params=pltpu.CompilerParams(vmem_limit_bytes=VMEM_LIMIT),
    )(*flat)
    return [tuple(res[4 * i:4 * i + 4]) for i in range(count)]


def _adamw(name, g8, w, m, v):
    rows, cols = w.shape
    tr = rows
    for cand in (256, 176, 128, 64):
        if rows % cand == 0 and rows > cand:
            tr = cand
            break

    def body(*refs):
        _adamw_math(*refs)

    blk = pl.BlockSpec((tr, cols), lambda i: (i, 0))
    return pl.pallas_call(
        body, name=name, grid=(rows // tr,),
        in_specs=[pl.BlockSpec((N_DEV, tr, cols), lambda i: (0, i, 0)), blk, blk, blk],
        out_specs=[blk] * 4, out_shape=[jax.ShapeDtypeStruct((rows, cols), F32)] * 4,
        compiler_params=pltpu.CompilerParams(dimension_semantics=("parallel",), vmem_limit_bytes=VMEM_LIMIT),
    )(g8, w, m, v)


def _exchange_specs(srcs, scatter):
    nk = len(srcs)
    if not nk:
        return [], [], [], []
    any_spec = pl.BlockSpec(memory_space=pl.ANY)
    out_shape = [jax.ShapeDtypeStruct(s.shape if sc else (N_DEV,) + s.shape, s.dtype) for s, sc in zip(srcs, scatter)]
    sems = [pltpu.SemaphoreType.DMA((nk, N_DEV - 1)), pltpu.SemaphoreType.DMA((nk, N_DEV - 1)),
            pltpu.SemaphoreType.DMA((nk,))]
    return [any_spec] * nk, [any_spec] * nk, out_shape, sems


FLIPS = ((0, 0, 1), (1, 0, 0), (0, 1, 0), (1, 1, 0), (1, 0, 1), (0, 1, 1), (1, 1, 1))
N_CHIP_PEERS = 3


def _exchange_fns(src_refs, out_refs, sems, scatter):
    nk = len(src_refs)
    if not nk:
        return (lambda: None), (lambda: None), (lambda: None)
    send_sems, recv_sems, local_sems = sems
    first = 1 + N_CHIP_PEERS

    def plan():
        x, y, c = lax.axis_index("x"), lax.axis_index("y"), lax.axis_index("c")
        me = 4 * x + 2 * y + c
        peers = [(1 - x if fx else x, 1 - y if fy else y, 1 - c if fc else c) for fx, fy, fc in FLIPS]
        pids = [4 * px + 2 * py + pc for px, py, pc in peers]

        def remote(k, j, src, dst, to):
            return pltpu.make_async_remote_copy(src_ref=src, dst_ref=dst, send_sem=send_sems.at[k, j],
                                                recv_sem=recv_sems.at[k, j], device_id=to, device_id_type=MESH)

        def mine(k, dest):
            return src_refs[k].at[dest] if scatter[k] else src_refs[k]

        local = [pltpu.make_async_copy(mine(k, me), out_refs[k].at[me], local_sems.at[k]) for k in range(nk)]
        direct = [remote(k, j, mine(k, pids[j]), out_refs[k].at[me], peers[j])
                  for k in range(nk) for j in range(len(FLIPS) if scatter[k] else first)]
        relays = {(k, j): remote(k, j, out_refs[k].at[pids[j - N_CHIP_PEERS]], out_refs[k].at[pids[j - N_CHIP_PEERS]], peers[0])
                  for k in range(nk) if not scatter[k] for j in range(first, len(FLIPS))}
        arrivals = {(k, j): remote(k, j, out_refs[k].at[pids[j]], out_refs[k].at[pids[j]], peers[j])
                    for k in range(nk) for j in range(len(FLIPS))}
        return local, direct, relays, arrivals

    def start():
        local, direct, _, _ = plan()
        for cp in local + direct:
            cp.start()

    def relay():
        _, _, relays, arrivals = plan()
        for (k, j), cp in relays.items():
            arrivals[k, j - N_CHIP_PEERS].wait_recv()
            cp.start()

    def wait():
        local, direct, relays, arrivals = plan()
        for (k, j), cp in arrivals.items():
            if (k, j + N_CHIP_PEERS) not in relays:
                cp.wait_recv()
        for cp in direct + list(relays.values()):
            cp.wait_send()
        for cp in local:
            cp.wait()

    return start, relay, wait


def _grid_step(rank):
    step, total = 0, 1
    for axis in range(rank):
        step = step * pl.num_programs(axis) + pl.program_id(axis)
        total = total * pl.num_programs(axis)
    return step, total


def _ride(rank, start, relay, wait):
    step, total = _grid_step(rank)
    pl.when(step == 0)(start)
    pl.when(step == (3 * total) // 4)(relay)
    return lambda: pl.when(step == total - 1)(wait)


def _exchange(name, srcs, scatter):
    nk = len(srcs)
    c_in, c_out, c_shape, c_sems = _exchange_specs(srcs, scatter)

    def body(*refs):
        start, relay, wait = _exchange_fns(refs[:nk], refs[nk:2 * nk], refs[2 * nk:], scatter)
        start()
        relay()
        wait()

    return pl.pallas_call(body, name=name, in_specs=c_in, out_specs=c_out, out_shape=c_shape, scratch_shapes=c_sems)(*srcs)


def _cols_from_shards(g):
    return jnp.transpose(g, (1, 0, 2)).reshape(g.shape[1], -1)


def _cols_to_shards(w):
    return jnp.transpose(w.reshape(w.shape[0], N_DEV, -1), (1, 0, 2))


def _prep(x, srcs, scatter):
    nb = x.shape[0]
    t = _t_pad()
    head = PAD_ROWS + N_META
    nk = len(srcs)
    c_in, c_out, c_shape, c_sems = _exchange_specs(srcs, scatter)

    def body(x_ref, *rest):
        h0_ref = rest[nk]
        finish = _ride(1, *_exchange_fns(rest[:nk], rest[nk + 1:2 * nk + 1], rest[2 * nk + 1:], scatter))
        lead = pl.program_id(0) == 0

        @pl.when(lead)
        def _():
            h0_ref[...] = jnp.zeros_like(h0_ref)

        @pl.when(jnp.logical_not(lead))
        def _():
            h0_ref[...] = x_ref[...]

        finish()

    src = pl.BlockSpec((nb, head, D_MODEL), lambda j: (0, jnp.maximum(j - 1, 0), 0))
    dst = pl.BlockSpec((nb, head, D_MODEL), lambda j: (0, j, 0))
    res = pl.pallas_call(
        body, name="prep", grid=(t // head,), in_specs=[src] + c_in, out_specs=[dst] + c_out,
        out_shape=[jax.ShapeDtypeStruct((nb, t, D_MODEL), F32)] + c_shape, scratch_shapes=c_sems,
        compiler_params=pltpu.CompilerParams(dimension_semantics=("arbitrary",)),
    )(x, *srcs)
    return res[0], res[1:]


def _rope_tables(n):
    t = _t_pad()
    pos = np.arange(t, dtype=np.float32) - np.float32(PAD_ROWS)
    half = QK_ROPE // 2
    freqs = (1.0 / (ROPE_THETA ** (np.arange(half, dtype=np.float32) / half))).astype(np.float32)
    ang = pos[:, None] * freqs[None, :]
    cos, sin = np.cos(ang), np.sin(ang)
    z = lambda w: np.zeros((t, w), np.float32)
    c = np.concatenate([np.ones((t, QK_NOPE), np.float32), cos, cos, z(HEAD_PAD - QK_HEAD)], axis=1)
    s1 = np.concatenate([z(QK_NOPE + half), sin, z(HEAD_PAD - QK_HEAD)], axis=1)
    s2 = np.concatenate([z(QK_NOPE), -sin, z(HEAD_PAD - QK_NOPE - half)], axis=1)
    return tuple(jnp.asarray(np.tile(a, (n // t, 1))) for a in (c, s1, s2))


def _block_diag_gates(lru_wa, lru_wi):
    eye = jnp.eye(2, dtype=lru_wa.dtype)

    def bd(w):
        w = w.reshape(2, D_RNN // LANES, 2, RNN_BW, RNN_BW)
        full = w[:, :, :, :, None, :] * eye[None, None, :, None, :, None]
        return full.reshape(2, D_RNN // LANES, LANES, LANES)

    a, i = bd(lru_wa), bd(lru_wi)
    return jnp.concatenate([a[0], i[0], a[1], i[1]], axis=-1)


def _unblock_gates(dw):
    nb = D_RNN // LANES
    parts = dw.reshape(nb, 2, RNN_BW, 4, 2, RNN_BW)
    diag = jnp.stack([parts[:, k, :, :, k, :] for k in range(2)], axis=1)
    diag = jnp.transpose(diag, (3, 0, 1, 2, 4)).reshape(4, 2 * nb, RNN_BW, RNN_BW)
    return jnp.stack([diag[0], diag[2]]), jnp.stack([diag[1], diag[3]])


WEIGHTS = ("meta_tokens", "ln1_g", "w_in", "q_a_norm_g", "w_uq", "kv_a_norm_g", "w_ukv", "q_norm_g", "k_norm_g",
           "conv_w", "conv_b", "lru_wa", "lru_ba", "lru_wi", "lru_bi", "lru_lambda", "attn_out_g", "rnn_out_g",
           "w_out", "ln2_g", "w_gate", "w_up", "w_down")
BIG = ("w_in", "w_uq", "w_ukv", "w_out", "w_gate", "w_up", "w_down")
TRANSPOSED = ("w_in", "w_uq", "w_gate", "w_up")
ROW_SHARDED = ("w_out", "w_down") + TRANSPOSED
REPLICATED = ("ln1_g", "q_a_norm_g", "kv_a_norm_g", "q_norm_g", "k_norm_g", "conv_b", "lru_wa", "lru_wi",
              "attn_out_g", "rnn_out_g", "ln2_g")
WHOLE = REPLICATED + ("loss",)
G_FIRST = ("w_in", "meta_tokens")
G_MID = ("w_uq", "w_ukv", "conv_w", "lru_ba", "lru_bi", "lru_lambda")
LATE = ("w_out", "w_gate", "w_up", "w_down")
G_LAST = ("meta_tokens", "ln1_g")


def _local_step(x, tgt, ex):
    nb = x.shape[0]
    t = _t_pad()
    n = nb * t
    local = ex.local
    h0, got = _prep(x, *ex.gather_srcs(G_FIRST))
    first = ex.gathered(G_FIRST, got)
    meta, w_in = first["meta_tokens"], first["w_in"]
    h0 = h0.at[:, PAD_ROWS:PAD_ROWS + N_META].set(jnp.broadcast_to(meta[None], (nb, N_META, D_MODEL))).reshape(n, D_MODEL)

    zr = lambda r: jnp.zeros((r, D_MODEL), w_in.dtype)
    w_in_p = jnp.concatenate([w_in[:OFF_CKV], w_in[OFF_KR:], zr(QK_NOPE), w_in[OFF_CKV:OFF_KR], zr(HEAD_PAD - QK_HEAD)],
                             axis=0)
    pad_g = lambda g: jnp.pad(g, ((0, 0), (0, HEAD_PAD - QK_HEAD)))
    qg, kg = pad_g(local["q_norm_g"]), pad_g(local["k_norm_g"])
    rc, rs1, rs2 = _rope_tables(n)
    wblk = _block_diag_gates(local["lru_wa"].reshape(2, -1, RNN_BW, RNN_BW),
                             local["lru_wi"].reshape(2, -1, RNN_BW, RNN_BW)).astype(BF16)
    nblk = D_RNN // LANES

    (hn, cq, ckv, xr, xg, kr), got = _in_proj(h0, local["ln1_g"], w_in_p, *ex.gather_srcs(G_MID))
    w = ex.gathered(G_MID, got)
    w_uq_p = jnp.pad(w["w_uq"].reshape(N_HEADS, QK_HEAD, Q_LORA), ((0, 0), (0, HEAD_PAD - QK_HEAD), (0, 0))
                     ).reshape(QP_COLS, Q_LORA)
    ukv = w["w_ukv"].reshape(KV_LORA, N_HEADS, QK_NOPE + V_HEAD)
    w_uk_p = jnp.pad(ukv[:, :, :QK_NOPE], ((0, 0), (0, 0), (0, HEAD_PAD - QK_NOPE))).reshape(KV_LORA, QP_COLS)
    w_v = ukv[:, :, QK_NOPE:].reshape(KV_LORA, D_ATTN)
    gbias = jnp.stack([w["lru_ba"][0], w["lru_bi"][0], w["lru_ba"][1], w["lru_bi"][1]], axis=0)
    gbias = jnp.transpose(gbias.reshape(4, nblk, LANES), (1, 0, 2)).reshape(nblk, 1, 4 * LANES)

    q, k, v = _qkv_fwd(cq, ckv, kr, local["q_a_norm_g"], local["kv_a_norm_g"], w_uq_p, w_uk_p, w_v, qg, kg, rc, rs1, rs2)
    oa, probs, got = _attn_fwd(q, k, v, *ex.gather_srcs(LATE))
    late = ex.gathered(LATE, got)
    orn = _rnn_fwd(xr, xg, w["conv_w"], local["conv_b"], wblk, gbias, w["lru_lambda"])
    (doa, dor, dh1, mix_t, h1n, act_t, dgate_t, dup_t, dyb, loss, dga, dgr, dg2) = _post(
        oa, orn, h0, tgt, local["attn_out_g"], local["rnn_out_g"], local["ln2_g"], late["w_out"], late["w_gate"],
        late["w_up"], late["w_down"])
    wire = {"w_out": _matmul_shards("dw_out", mix_t, dh1), "w_gate": _matmul_shards("dw_gate", dgate_t, h1n),
            "w_up": _matmul_shards("dw_up", dup_t, h1n), "w_down": _matmul_shards("dw_down", act_t, dyb)}
    names = ("w_out", "w_gate")
    (dxr, dxg, dcw, dcb, dwblk, dgb, dlam), got = _rnn_bwd(xr, xg, dor, w["conv_w"], local["conv_b"], wblk, gbias,
                                                           w["lru_lambda"], *ex.scatter_srcs(names, wire))
    summed = ex.scattered(names, wire, got)
    dwa, dwi = _unblock_gates(dwblk)
    dgb = jnp.transpose(dgb.reshape(nblk, 4, LANES), (1, 0, 2)).reshape(4, D_RNN)
    names = ("w_up", "w_down")
    (dq_r, dk_r, dv), got = _attn_bwd(q, k, v, doa, oa, probs, *ex.scatter_srcs(names, wire))
    summed.update(ex.scattered(names, wire, got))
    wire = ex.to_wire({
        "conv_w": dcw, "conv_b": dcb, "lru_wa": dwa.reshape(-1, RNN_BW), "lru_ba": jnp.stack([dgb[0], dgb[2]]),
        "lru_wi": dwi.reshape(-1, RNN_BW), "lru_bi": jnp.stack([dgb[1], dgb[3]]), "lru_lambda": dlam,
        "attn_out_g": dga, "rnn_out_g": dgr, "ln2_g": dg2, "loss": loss})
    names = tuple(wire)
    (dp, qa, kva, dqp, dkv, dqg, dkg, dgqa, dgkva), got = _qkv_bwd(
        cq, ckv, kr, dq_r, dk_r, dv, dxr, dxg, local["q_a_norm_g"], local["kv_a_norm_g"], w_uq_p, w_uk_p, w_v, qg, kg,
        rc, rs1, rs2, *ex.scatter_srcs(names, wire))
    summed.update(ex.scattered(names, wire, got))
    dw_uq_p, _ = _matmul_tn("dw_uq", dqp, qa)
    dw_kv, _ = _matmul_tn("dw_ukv", kva, dkv)
    dw_uq = dw_uq_p.reshape(N_HEADS, HEAD_PAD, Q_LORA)[:, :QK_HEAD].reshape(N_HEADS * QK_HEAD, Q_LORA)
    dw_ukv = jnp.concatenate([dw_kv[:, :QP_COLS].reshape(KV_LORA, N_HEADS, HEAD_PAD)[:, :, :QK_NOPE],
                              dw_kv[:, QP_COLS:].reshape(KV_LORA, N_HEADS, V_HEAD)], axis=2).reshape(KV_LORA, -1)
    wire = ex.to_wire({"q_a_norm_g": dgqa, "w_uq": dw_uq, "kv_a_norm_g": dgkva, "w_ukv": dw_ukv,
                       "q_norm_g": dqg[:, :QK_HEAD], "k_norm_g": dkg[:, :QK_HEAD]})
    names = tuple(wire)
    dw_in_p, got = _matmul_tn("dw_in", dp, hn, *ex.scatter_srcs(names, wire))
    summed.update(ex.scattered(names, wire, got))
    kr0 = OFF_CKV + 2 * D_RNN + QK_NOPE
    dw_in = jnp.concatenate([dw_in_p[:OFF_CKV], dw_in_p[kr0:kr0 + QK_ROPE], dw_in_p[OFF_CKV:OFF_CKV + 2 * D_RNN]], axis=0)
    wire = ex.to_wire({"w_in": dw_in})
    (dh0, dg1), got = _in_bwd(dp, h0, dh1, local["ln1_g"], w_in_p, *ex.scatter_srcs(("w_in",), wire))
    summed.update(ex.scattered(("w_in",), wire, got))

    dh0 = dh0.reshape(nb, t, D_MODEL)
    wire = ex.to_wire({"meta_tokens": jnp.sum(dh0[:, PAD_ROWS:PAD_ROWS + N_META], axis=0), "ln1_g": dg1})
    got = ex.run("reduce_last", *ex.scatter_srcs(G_LAST, wire))
    summed.update(ex.scattered(G_LAST, wire, got))
    return dh0[:, PAD_ROWS + N_META:], summed


class _MeshExchange:
    def __init__(self, shards):
        self.local = shards

    @staticmethod
    def run(name, srcs, scatter):
        return _exchange(name, srcs, scatter)

    def gather_srcs(self, names):
        return [self.local[k].astype(BF16) if k in BIG else self.local[k] for k in names], [False] * len(names)

    @staticmethod
    def gathered(names, outs):
        return {k: g.reshape(-1, g.shape[-1]) if k in ROW_SHARDED else _cols_from_shards(g) for k, g in zip(names, outs)}

    @staticmethod
    def to_wire(grads):
        wire = {}
        for k, g in grads.items():
            if k in WHOLE:
                wire[k] = g
            elif k in ROW_SHARDED:
                wire[k] = g.reshape(N_DEV, -1, g.shape[-1]).astype(BF16)
            else:
                wire[k] = _cols_to_shards(g).astype(BF16) if k in BIG else _cols_to_shards(g)
        return wire

    @staticmethod
    def scatter_srcs(names, wire):
        return [wire[k] for k in names], [k not in WHOLE for k in names]

    @staticmethod
    def scattered(names, wire, outs):
        return dict(zip(names, outs))


def kernel(x, meta_tokens, ln1_g, w_in, q_a_norm_g, w_uq, kv_a_norm_g, w_ukv, q_norm_g, k_norm_g, conv_w, conv_b, lru_wa, lru_ba, lru_wi, lru_bi, lru_lambda, attn_out_g, rnn_out_g, w_out, ln2_g, w_gate, w_up, w_down, loss_target, m_meta_tokens, m_ln1_g, m_w_in, m_q_a_norm_g, m_w_uq, m_kv_a_norm_g, m_w_ukv, m_q_norm_g, m_k_norm_g, m_conv_w, m_conv_b, m_lru_wa, m_lru_ba, m_lru_wi, m_lru_bi, m_lru_lambda, m_attn_out_g, m_rnn_out_g, m_w_out, m_ln2_g, m_w_gate, m_w_up, m_w_down, v_meta_tokens, v_ln1_g, v_w_in, v_q_a_norm_g, v_w_uq, v_kv_a_norm_g, v_w_ukv, v_q_norm_g, v_k_norm_g, v_conv_w, v_conv_b, v_lru_wa, v_lru_ba, v_lru_wi, v_lru_bi, v_lru_lambda, v_attn_out_g, v_rnn_out_g, v_w_out, v_ln2_g, v_w_gate, v_w_up, v_w_down):
    given = (meta_tokens, ln1_g, w_in, q_a_norm_g, w_uq, kv_a_norm_g, w_ukv, q_norm_g, k_norm_g, conv_w, conv_b,
             lru_wa, lru_ba, lru_wi, lru_bi, lru_lambda, attn_out_g, rnn_out_g, w_out, ln2_g, w_gate, w_up, w_down)
    moments_m = (m_meta_tokens, m_ln1_g, m_w_in, m_q_a_norm_g, m_w_uq, m_kv_a_norm_g, m_w_ukv, m_q_norm_g, m_k_norm_g,
                 m_conv_w, m_conv_b, m_lru_wa, m_lru_ba, m_lru_wi, m_lru_bi, m_lru_lambda, m_attn_out_g, m_rnn_out_g,
                 m_w_out, m_ln2_g, m_w_gate, m_w_up, m_w_down)
    moments_v = (v_meta_tokens, v_ln1_g, v_w_in, v_q_a_norm_g, v_w_uq, v_kv_a_norm_g, v_w_ukv, v_q_norm_g, v_k_norm_g,
                 v_conv_w, v_conv_b, v_lru_wa, v_lru_ba, v_lru_wi, v_lru_bi, v_lru_lambda, v_attn_out_g, v_rnn_out_g,
                 v_w_out, v_ln2_g, v_w_gate, v_w_up, v_w_down)
    shapes = {k: a.shape for k, a in zip(WEIGHTS, given)}

    def two_d(k, a):
        a = a.reshape(-1, a.shape[-1])
        return a.T if k in TRANSPOSED else a

    w = {k: two_d(k, a) for k, a in zip(WEIGHTS, given)}
    m = {k: two_d(k, a) for k, a in zip(WEIGHTS, moments_m)}
    v = {k: two_d(k, a) for k, a in zip(WEIGHTS, moments_v)}

    grad_x, parts = _local_step(x, loss_target, _MeshExchange(w))

    tiled = ("w_in", "w_gate", "w_up", "w_down")
    new = {k: _adamw("adamw_" + k, parts[k], w[k], m[k], v[k]) for k in tiled}
    small = [k for k in WEIGHTS if k not in tiled]
    new.update(zip(small, _adamw_many("adamw_small", [(parts[k], w[k], m[k], v[k]) for k in small])))

    loss = jnp.sum(parts["loss"][:, 0, 0])
    outs = [loss, grad_x]
    for idx in range(4):
        outs += [(new[k][idx].T if k in TRANSPOSED else new[k][idx]).reshape(shapes[k]) for k in WEIGHTS]
    return tuple(outs)
```

```python
import functools
import math

import numpy as np
import jax
import jax.numpy as jnp
from jax import lax
from jax.experimental import pallas as pl
from jax.experimental.pallas import tpu as pltpu

F32 = jnp.float32
BF16 = jnp.bfloat16

D_MODEL = 1024
N_META = 16
SEQ = 2048
N_HEADS = 8
QK_NOPE = 64
QK_ROPE = 32
QK_HEAD = QK_NOPE + QK_ROPE
V_HEAD = 64
D_ATTN = N_HEADS * V_HEAD
Q_LORA = 384
KV_LORA = 256
D_RNN = 512
RNN_BW = 64
D_FF = 2816
EPS = 1e-6
LRU_C = 8.0
ROPE_THETA = 10000.0
OFF_CKV = Q_LORA + KV_LORA
OFF_KR = OFF_CKV + QK_ROPE
IN_COLS = OFF_KR + 2 * D_RNN

ADAM_LR = 0.001
ADAM_B1 = 0.9
ADAM_B2 = 0.999
ADAM_EPS = 1e-08
ADAM_WD = 0.01
ADAM_STEP = 10

N_DEV = 8
LANES = 128
HEAD_PAD = LANES
PAD_ROWS = LANES - N_META
QP_COLS = N_HEADS * HEAD_PAD
P_COLS = OFF_CKV + 2 * D_RNN + LANES
FF_CHUNK = D_FF
VMEM_LIMIT = 56 * 1024 * 1024
MESH = pl.DeviceIdType.MESH


def _t_pad():
    return PAD_ROWS + N_META + SEQ


def _row_tile(n):
    return 256 if n % 256 == 0 else 128


def _wide_row_tile(n):
    quarter = _t_pad() // 4
    return quarter if quarter % 16 == 0 and n % quarter == 0 else _row_tile(n)


def _const_spec(shape):
    nd = len(shape)
    return pl.BlockSpec(shape, lambda *_: (0,) * nd, pipeline_mode=pl.Buffered(1))


def _rms(x, d):
    r = lax.rsqrt(jnp.sum(x * x, axis=-1, keepdims=True) * (1.0 / d) + EPS)
    return x * r, r


def _rms_bwd(dy, xhat, r, g, d):
    dxh = dy * g
    return r * (dxh - xhat * (jnp.sum(dxh * xhat, axis=-1, keepdims=True) * (1.0 / d)))


def _colsum(x):
    return jnp.sum(x, axis=0, keepdims=True)


def _dot(a, b):
    return jnp.dot(a, b, preferred_element_type=F32)


def _dot_nt(a, b):
    return lax.dot_general(a, b, (((1,), (1,)), ((), ())), preferred_element_type=F32)


def _dot_tn(a, b):
    return lax.dot_general(a, b, (((0,), (0,)), ((), ())), preferred_element_type=F32)


def _rope(x, c, s1, s2):
    return x * c + pltpu.roll(x, 16, 1) * s1 + pltpu.roll(x, HEAD_PAD - 16, 1) * s2


def _rope_bwd(dy, c, s1, s2):
    return dy * c + pltpu.roll(dy * s1, HEAD_PAD - 16, 1) + pltpu.roll(dy * s2, 16, 1)


def _acc(ref, first, val):
    @pl.when(first)
    def _():
        ref[...] = val

    @pl.when(jnp.logical_not(first))
    def _():
        ref[...] += val


def _in_proj(h0, ln1_g, w_in_p, srcs=(), scatter=()):
    n = h0.shape[0]
    tm = _wide_row_tile(n)
    nk = len(srcs)
    c_in, c_out, c_shape, c_sems = _exchange_specs(srcs, scatter)

    def body(h_ref, g_ref, w_ref, *rest):
        hn_ref, cq_ref, ckv_ref, xr_ref, xg_ref, kr_ref = rest[nk:nk + 6]
        finish = _ride(1, *_exchange_fns(rest[:nk], rest[nk + 6:2 * nk + 6], rest[2 * nk + 6:], scatter))
        xhat, _ = _rms(h_ref[...], D_MODEL)
        hn = (xhat * g_ref[...]).astype(BF16)
        hn_ref[...] = hn
        p = _dot_nt(hn, w_ref[...])
        cq_ref[...] = p[:, :Q_LORA]
        ckv_ref[...] = p[:, Q_LORA:OFF_CKV]
        xr_ref[...] = p[:, OFF_CKV:OFF_CKV + D_RNN]
        xg_ref[...] = p[:, OFF_CKV + D_RNN:OFF_CKV + 2 * D_RNN]
        kr_ref[...] = p[:, OFF_CKV + 2 * D_RNN:]
        finish()

    def row(w):
        return pl.BlockSpec((tm, w), lambda i: (i, 0))

    widths = (D_MODEL, Q_LORA, KV_LORA, D_RNN, D_RNN, LANES)
    res = pl.pallas_call(
        body, name="in_proj", grid=(n // tm,),
        in_specs=[row(D_MODEL), _const_spec((1, D_MODEL)), _const_spec((P_COLS, D_MODEL))] + c_in,
        out_specs=[row(w) for w in widths] + c_out,
        out_shape=[jax.ShapeDtypeStruct((n, w), BF16 if k == 0 else F32) for k, w in enumerate(widths)] + c_shape,
        scratch_shapes=c_sems,
        compiler_params=pltpu.CompilerParams(dimension_semantics=("arbitrary",), vmem_limit_bytes=VMEM_LIMIT),
    )(h0, ln1_g, w_in_p, *srcs)
    return res[:6], res[6:]


def _qkv_fwd(cq, ckv, kr, gqa, gkva, w_uq_p, w_uk_p, w_v, qg, kg, rc, rs1, rs2):
    n = cq.shape[0]
    tm = _wide_row_tile(n)

    def body(cq_ref, ckv_ref, kr_ref, gqa_ref, gkva_ref, wuq_ref, wuk_ref, wv_ref, qg_ref, kg_ref,
             c_ref, s1_ref, s2_ref, q_ref, k_ref, v_ref):
        xq, _ = _rms(cq_ref[...], Q_LORA)
        qa = (xq * gqa_ref[...]).astype(BF16)
        q = _dot_nt(qa, wuq_ref[...])
        xkv, _ = _rms(ckv_ref[...], KV_LORA)
        kva = (xkv * gkva_ref[...]).astype(BF16)
        kn = _dot(kva, wuk_ref[...])
        v_ref[...] = _dot(kva, wv_ref[...]).astype(BF16)
        krp = kr_ref[...]
        c, s1, s2 = c_ref[...], s1_ref[...], s2_ref[...]
        for h in range(N_HEADS):
            sl = slice(h * HEAD_PAD, (h + 1) * HEAD_PAD)
            qh, _ = _rms(q[:, sl], QK_HEAD)
            q_ref[:, sl] = _rope(qh * qg_ref[...], c, s1, s2).astype(BF16)
            kh, _ = _rms(kn[:, sl] + krp, QK_HEAD)
            k_ref[:, sl] = _rope(kh * kg_ref[...], c, s1, s2).astype(BF16)

    def row(w):
        return pl.BlockSpec((tm, w), lambda i: (i, 0))

    return pl.pallas_call(
        body, name="qkv_fwd", grid=(n // tm,),
        in_specs=[row(Q_LORA), row(KV_LORA), row(LANES), _const_spec((1, Q_LORA)), _const_spec((1, KV_LORA)),
                  _const_spec((QP_COLS, Q_LORA)), _const_spec((KV_LORA, QP_COLS)), _const_spec((KV_LORA, D_ATTN)),
                  _const_spec((1, LANES)), _const_spec((1, LANES)), row(LANES), row(LANES), row(LANES)],
        out_specs=[row(QP_COLS), row(QP_COLS), row(D_ATTN)],
        out_shape=[jax.ShapeDtypeStruct((n, QP_COLS), BF16), jax.ShapeDtypeStruct((n, QP_COLS), BF16),
                   jax.ShapeDtypeStruct((n, D_ATTN), BF16)],
        compiler_params=pltpu.CompilerParams(dimension_semantics=("parallel",), vmem_limit_bytes=VMEM_LIMIT),
    )(cq, ckv, kr, gqa, gkva, w_uq_p, w_uk_p, w_v, qg, kg, rc, rs1, rs2)


def _qkv_bwd(cq, ckv, kr, dq_r, dk_r, dv, dxr, dxg, gqa, gkva, w_uq_p, w_uk_p, w_v, qg, kg, rc, rs1, rs2,
             srcs=(), scatter=()):
    n = cq.shape[0]
    tm = _wide_row_tile(n)
    nk = len(srcs)
    c_in, c_out, c_shape, c_sems = _exchange_specs(srcs, scatter)

    def body(cq_ref, ckv_ref, kr_ref, dq_ref, dk_ref, dv_ref, dxr_ref, dxg_ref, gqa_ref, gkva_ref, wuq_ref, wuk_ref,
             wv_ref, qg_ref, kg_ref, c_ref, s1_ref, s2_ref, *rest):
        dp_ref, qa_ref, kva_ref, dqp_ref, dkv_ref, dqg_ref, dkg_ref, dgqa_ref, dgkva_ref = rest[nk:nk + 9]
        finish = _ride(1, *_exchange_fns(rest[:nk], rest[nk + 9:2 * nk + 9], rest[2 * nk + 9:], scatter))
        first = pl.program_id(0) == 0
        dp_ref[:, OFF_CKV:OFF_CKV + D_RNN] = dxr_ref[...].astype(BF16)
        dp_ref[:, OFF_CKV + D_RNN:OFF_CKV + 2 * D_RNN] = dxg_ref[...].astype(BF16)
        xq, rq = _rms(cq_ref[...], Q_LORA)
        qa = (xq * gqa_ref[...]).astype(BF16)
        qa_ref[...] = qa
        q = _dot_nt(qa, wuq_ref[...])
        xkv, rkv = _rms(ckv_ref[...], KV_LORA)
        kva = (xkv * gkva_ref[...]).astype(BF16)
        kva_ref[...] = kva
        kn = _dot(kva, wuk_ref[...])
        krp = kr_ref[...]
        c, s1, s2 = c_ref[...], s1_ref[...], s2_ref[...]
        lane = lax.broadcasted_iota(jnp.int32, (tm, HEAD_PAD), 1)
        rope_lanes = jnp.logical_and(lane >= QK_NOPE, lane < QK_HEAD)
        dqg = jnp.zeros((1, HEAD_PAD), F32)
        dkg = jnp.zeros((1, HEAD_PAD), F32)
        dkr = jnp.zeros((tm, HEAD_PAD), F32)
        for h in range(N_HEADS):
            sl = slice(h * HEAD_PAD, (h + 1) * HEAD_PAD)
            qh, rqh = _rms(q[:, sl], QK_HEAD)
            dy = _rope_bwd(dq_ref[:, sl], c, s1, s2)
            dqg = dqg + _colsum(dy * qh)
            dqp_ref[:, sl] = _rms_bwd(dy, qh, rqh, qg_ref[...], QK_HEAD).astype(BF16)
            kh, rkh = _rms(kn[:, sl] + krp, QK_HEAD)
            dyk = _rope_bwd(dk_ref[:, sl], c, s1, s2)
            dkg = dkg + _colsum(dyk * kh)
            dkh = _rms_bwd(dyk, kh, rkh, kg_ref[...], QK_HEAD)
            dkv_ref[:, sl] = dkh.astype(BF16)
            dkr = dkr + jnp.where(rope_lanes, dkh, 0.0)
        dkv_ref[:, QP_COLS:] = dv_ref[...].astype(BF16)
        dp_ref[:, OFF_CKV + 2 * D_RNN:] = dkr.astype(BF16)
        dqa = _dot(dqp_ref[...], wuq_ref[...])
        dp_ref[:, :Q_LORA] = _rms_bwd(dqa, xq, rq, gqa_ref[...], Q_LORA).astype(BF16)
        dkva = _dot_nt(dkv_ref[:, :QP_COLS], wuk_ref[...]) + _dot_nt(dkv_ref[:, QP_COLS:], wv_ref[...])
        dp_ref[:, Q_LORA:OFF_CKV] = _rms_bwd(dkva, xkv, rkv, gkva_ref[...], KV_LORA).astype(BF16)
        _acc(dqg_ref, first, dqg)
        _acc(dkg_ref, first, dkg)
        _acc(dgqa_ref, first, _colsum(dqa * xq))
        _acc(dgkva_ref, first, _colsum(dkva * xkv))
        finish()

    def row(w):
        return pl.BlockSpec((tm, w), lambda i: (i, 0))

    def acc(w):
        return pl.BlockSpec((1, w), lambda i: (0, 0))

    res = pl.pallas_call(
        body, name="qkv_bwd", grid=(n // tm,),
        in_specs=[row(Q_LORA), row(KV_LORA), row(LANES), row(QP_COLS), row(QP_COLS), row(D_ATTN), row(D_RNN), row(D_RNN),
                  _const_spec((1, Q_LORA)), _const_spec((1, KV_LORA)),
                  _const_spec((QP_COLS, Q_LORA)), _const_spec((KV_LORA, QP_COLS)), _const_spec((KV_LORA, D_ATTN)),
                  _const_spec((1, LANES)), _const_spec((1, LANES)), row(LANES), row(LANES), row(LANES)] + c_in,
        out_specs=[row(P_COLS), row(Q_LORA), row(KV_LORA), row(QP_COLS),
                   row(QP_COLS + D_ATTN), acc(LANES), acc(LANES), acc(Q_LORA), acc(KV_LORA)] + c_out,
        out_shape=[jax.ShapeDtypeStruct((n, P_COLS), BF16), jax.ShapeDtypeStruct((n, Q_LORA), BF16),
                   jax.ShapeDtypeStruct((n, KV_LORA), BF16), jax.ShapeDtypeStruct((n, QP_COLS), BF16),
                   jax.ShapeDtypeStruct((n, QP_COLS + D_ATTN), BF16),
                   jax.ShapeDtypeStruct((1, LANES), F32), jax.ShapeDtypeStruct((1, LANES), F32),
                   jax.ShapeDtypeStruct((1, Q_LORA), F32), jax.ShapeDtypeStruct((1, KV_LORA), F32)] + c_shape,
        scratch_shapes=c_sems,
        compiler_params=pltpu.CompilerParams(dimension_semantics=("arbitrary",), vmem_limit_bytes=VMEM_LIMIT),
    )(cq, ckv, kr, dq_r, dk_r, dv, dxr, dxg, gqa, gkva, w_uq_p, w_uk_p, w_v, qg, kg, rc, rs1, rs2, *srcs)
    return res[:9], res[9:]


KEY_CHUNK = 4 * LANES


def _key_chunks(t):
    count = max(t // KEY_CHUNK, 1)
    first = t - KEY_CHUNK * (count - 1)
    return [(0, first)] + [(first + KEY_CHUNK * c, KEY_CHUNK) for c in range(count - 1)]


def _attn_specs(t, tq):
    nq = t // tq
    qspec = pl.BlockSpec((tq, 2 * HEAD_PAD), lambda b, hp, i: (b * nq + i, hp))
    kspec = pl.BlockSpec((t, 2 * HEAD_PAD), lambda b, hp, i: (b, hp))
    vspec = pl.BlockSpec((t, 2 * V_HEAD), lambda b, hp, i: (b, hp))
    ospec = pl.BlockSpec((tq, 2 * V_HEAD), lambda b, hp, i: (b * nq + i, hp))
    return nq, qspec, kspec, vspec, ospec


def _probs_spec(t, tq):
    return pl.BlockSpec((1, 2, tq, t), lambda b, hp, i: (b, hp, i, 0))


def _attn_fwd(q, k, v, srcs=(), scatter=()):
    n = q.shape[0]
    t = _t_pad()
    tq = t // 2
    nq, qspec, kspec, vspec, ospec = _attn_specs(t, tq)
    nk = len(srcs)
    c_in, c_out, c_shape, c_sems = _exchange_specs(srcs, scatter)

    def body(q_ref, k_ref, v_ref, *rest):
        o_ref, l_ref, p_ref = rest[nk:nk + 3]
        finish = _ride(3, *_exchange_fns(rest[:nk], rest[nk + 3:2 * nk + 3], rest[2 * nk + 3:], scatter))
        lane = lax.broadcasted_iota(jnp.int32, (tq, 2 * V_HEAD), 1)
        outs = []
        sums = []
        for j in range(2):
            sl = slice(j * HEAD_PAD, (j + 1) * HEAD_PAD)
            qh = q_ref[:, sl]

            def scores(start, size):
                s = _dot_nt(qh, k_ref[start:start + size, sl])
                if start < PAD_ROWS:
                    key = lax.broadcasted_iota(jnp.int32, (tq, size), 1) + start
                    s = jnp.where(key >= PAD_ROWS, s, -jnp.inf)
                return s

            top = functools.reduce(jnp.maximum, [jnp.max(scores(*c), axis=-1, keepdims=True) for c in _key_chunks(t)])
            l = jnp.zeros((tq, 1), F32)
            pv = jnp.zeros((tq, 2 * V_HEAD), F32)
            for start, size in _key_chunks(t):
                e = jnp.exp2((scores(start, size) - top) * (QK_HEAD ** -0.5 * math.log2(math.e)))
                l = l + jnp.sum(e, axis=-1, keepdims=True)
                e = e.astype(BF16)
                p_ref[0, j, :, start:start + size] = e
                pv = pv + _dot(e, v_ref[start:start + size, :])
            outs.append(pv / l)
            sums.append(l)
        o_ref[...] = jnp.where(lane < V_HEAD, outs[0], outs[1])
        l_ref[...] = jnp.where(lane < V_HEAD, sums[0], sums[1])
        finish()

    res = pl.pallas_call(
        body, name="attn_fwd", grid=(n // t, N_HEADS // 2, nq),
        in_specs=[qspec, kspec, vspec] + c_in, out_specs=[ospec, ospec, _probs_spec(t, tq)] + c_out,
        out_shape=[jax.ShapeDtypeStruct((n, D_ATTN), F32), jax.ShapeDtypeStruct((n, D_ATTN), F32),
                   jax.ShapeDtypeStruct((n // t, N_HEADS, t, t), BF16)] + c_shape,
        scratch_shapes=c_sems,
        compiler_params=pltpu.CompilerParams(dimension_semantics=("arbitrary", "arbitrary", "arbitrary"),
                                             vmem_limit_bytes=VMEM_LIMIT),
    )(q, k, v, *srcs)
    return res[0], (res[1], res[2]), res[3:]


def _attn_bwd(q, k, v, do, o, probs, srcs=(), scatter=()):
    n = q.shape[0]
    t = _t_pad()
    tq = t // 2
    nq, qspec, kspec, vspec, ospec = _attn_specs(t, tq)
    nk = len(srcs)
    c_in, c_out, c_shape, c_sems = _exchange_specs(srcs, scatter)

    def body(q_ref, k_ref, v_ref, do_ref, o_ref, l_ref, p_ref, *rest):
        dq_ref, dk_ref, dv_ref = rest[nk:nk + 3]
        finish = _ride(3, *_exchange_fns(rest[:nk], rest[nk + 3:2 * nk + 3], rest[2 * nk + 3:], scatter))

        @pl.when(pl.program_id(2) == 0)
        def _():
            dk_ref[...] = jnp.zeros_like(dk_ref)
            dv_ref[...] = jnp.zeros_like(dv_ref)

        lane = lax.broadcasted_iota(jnp.int32, (tq, 2 * V_HEAD), 1)
        do = do_ref[...]
        do_o = do * o_ref[...]
        chunks = _key_chunks(t)
        dvs = [None] * len(chunks)
        for j in range(2):
            sl = slice(j * HEAD_PAD, (j + 1) * HEAD_PAD)
            qh = q_ref[:, sl]
            in_head = (lane < V_HEAD) if j == 0 else (lane >= V_HEAD)
            inv_l = 1.0 / l_ref[:, j * V_HEAD:j * V_HEAD + 1]
            doh = jnp.where(in_head, do, 0.0).astype(BF16)
            doh_n = jnp.where(in_head, do * inv_l, 0.0).astype(BF16)
            delta = jnp.sum(jnp.where(in_head, do_o, 0.0), axis=-1, keepdims=True)
            row_scale = inv_l * (QK_HEAD ** -0.5)
            dq = jnp.zeros((tq, HEAD_PAD), F32)
            for c, (start, size) in enumerate(chunks):
                rows = slice(start, start + size)
                e = p_ref[0, j, :, rows]
                dp = _dot_nt(doh, v_ref[rows, :])
                ds = (e.astype(F32) * (dp - delta) * row_scale).astype(BF16)
                dq = dq + _dot(ds, k_ref[rows, sl])
                dk_ref[rows, sl] += _dot_tn(ds, qh)
                dvc = _dot_tn(e, doh_n)
                dvs[c] = dvc if dvs[c] is None else dvs[c] + dvc
            dq_ref[:, sl] = dq
        for (start, size), dvc in zip(chunks, dvs):
            dv_ref[start:start + size, :] += dvc
        finish()

    res = pl.pallas_call(
        body, name="attn_bwd", grid=(n // t, N_HEADS // 2, nq),
        in_specs=[qspec, kspec, vspec, ospec, ospec, ospec, _probs_spec(t, tq)] + c_in,
        out_specs=[qspec, kspec, vspec] + c_out,
        out_shape=[jax.ShapeDtypeStruct((n, QP_COLS), F32), jax.ShapeDtypeStruct((n, QP_COLS), F32),
                   jax.ShapeDtypeStruct((n, D_ATTN), F32)] + c_shape, scratch_shapes=c_sems,
        compiler_params=pltpu.CompilerParams(dimension_semantics=("arbitrary", "arbitrary", "arbitrary"),
                                             vmem_limit_bytes=VMEM_LIMIT),
    )(q, k, v, do, o, *probs, *srcs)
    return res[:3], res[3:]


SCAN_STEPS = 8


def _scan(chains, t):
    seg = t // 8
    rows = lax.broadcasted_iota(jnp.int32, (8, LANES), 0)

    def step(i, carry):
        carry = list(carry)
        for u in range(SCAN_STEPS):
            j = i * SCAN_STEPS + u
            for n, (a_ref, b_ref, h_ref, p_ref, reverse) in enumerate(chains):
                h, p = carry[n]
                idx = pl.ds(seg - 1 - j if reverse else j, 8, stride=seg)
                a = a_ref[idx, :]
                h = a * h + b_ref[idx, :]
                p = a * p
                h_ref[idx, :] = h
                p_ref[idx, :] = p
                carry[n] = (h, p)
        return tuple(carry)

    init = tuple((jnp.zeros((8, LANES), F32), jnp.ones((8, LANES), F32)) for _ in chains)
    ends = lax.fori_loop(0, seg // SCAN_STEPS, step, init)
    for (_, _, h_ref, p_ref, reverse), (b, a) in zip(chains, ends):
        for d in (1, 2, 4):
            if reverse:
                keep = rows < 8 - d
                a_n, b_n = pltpu.roll(a, 8 - d, 0), pltpu.roll(b, 8 - d, 0)
            else:
                keep = rows >= d
                a_n, b_n = pltpu.roll(a, d, 0), pltpu.roll(b, d, 0)
            b = a * jnp.where(keep, b_n, 0.0) + b
            a = a * jnp.where(keep, a_n, 1.0)
        for s in (range(7) if reverse else range(1, 8)):
            sl = slice(s * seg, (s + 1) * seg)
            carry_in = b[s + 1:s + 2, :] if reverse else b[s - 1:s, :]
            h_ref[sl, :] = h_ref[sl, :] + p_ref[sl, :] * carry_in


def _shift_rows(x, s, rows, t):
    if s == 0:
        return x
    rolled = pltpu.roll(x, s % t, 0)
    return jnp.where(rows >= s, rolled, 0.0) if s > 0 else jnp.where(rows < t + s, rolled, 0.0)


def _neg_expm1_twice(h, exp_2h):
    series = h * (-2.0 + h * (-2.0 + h * (-4.0 / 3 + h * (-2.0 / 3))))
    return jnp.where(h > -0.05, series, 1.0 - exp_2h)


def _sigmoid(x):
    return 0.5 * jnp.tanh(0.5 * x) + 0.5


def _gelu_parts(x):
    k = math.sqrt(2.0 / math.pi)
    th = jnp.tanh(k * (x + 0.044715 * x * x * x))
    g = 0.5 * x * (1.0 + th)
    dg = 0.5 * (1.0 + th) + 0.5 * x * (1.0 - th * th) * k * (1.0 + 3 * 0.044715 * x * x)
    return g, dg


def _lru_gates(xc, gates, lam_ref, valid, d):
    r = _sigmoid(gates[:, (2 * d) * LANES:(2 * d + 1) * LANES])
    i = _sigmoid(gates[:, (2 * d + 1) * LANES:(2 * d + 2) * LANES])
    neg_lam = -lam_ref[d:d + 1, :]
    sp = jnp.maximum(neg_lam, 0.0) + jnp.log1p(jnp.exp(-jnp.abs(neg_lam)))
    log_a = -LRU_C * r * sp
    a = jnp.exp(log_a)
    m = jnp.maximum(_neg_expm1_twice(log_a, a * a), 0.0)
    sq = jnp.sqrt(m)
    b = jnp.where(valid, sq * (i * xc), 0.0)
    return r, i, sp, a, m, sq, b


def _conv(xr, cw_ref, cb_ref, rows, t):
    return (cw_ref[0:1, :] * _shift_rows(xr, 2, rows, t) + cw_ref[1:2, :] * _shift_rows(xr, 1, rows, t)
            + cw_ref[2:3, :] * xr + cw_ref[3:4, :] * _shift_rows(xr, -1, rows, t) + cb_ref[...])


def _rnn_specs(t):
    seq = pl.BlockSpec((t, LANES), lambda cb, b: (b, cb))
    cw = pl.BlockSpec((4, LANES), lambda cb, b: (0, cb))
    vec1 = pl.BlockSpec((1, LANES), lambda cb, b: (0, cb))
    vec2 = pl.BlockSpec((2, LANES), lambda cb, b: (0, cb))
    wblk = pl.BlockSpec((1, LANES, 4 * LANES), lambda cb, b: (cb, 0, 0))
    gbias = pl.BlockSpec((1, 1, 4 * LANES), lambda cb, b: (cb, 0, 0))
    return seq, cw, vec1, vec2, wblk, gbias


def _rnn_fwd(xr, xg, conv_w, conv_b, wblk, gbias, lam):
    n = xr.shape[0]
    t = _t_pad()
    seq, cw, vec1, vec2, wspec, gspec = _rnn_specs(t)

    def body(xr_ref, xg_ref, cw_ref, cb_ref, w_ref, gb_ref, lam_ref, o_ref, a_s, b_s, h_s, p_s):
        rows = lax.broadcasted_iota(jnp.int32, (t, LANES), 0)
        valid = rows >= PAD_ROWS
        xc = _conv(xr_ref[...], cw_ref, cb_ref, rows, t)
        gates = _dot(xc.astype(BF16), w_ref[0]) + gb_ref[0]
        for d in range(2):
            _, _, _, a, _, _, b = _lru_gates(xc, gates, lam_ref, valid, d)
            a_s[d] = a
            b_s[d] = b
        _scan([(a_s.at[d], b_s.at[d], h_s.at[d], p_s.at[d], d == 1) for d in range(2)], t)
        g, _ = _gelu_parts(xg_ref[...])
        o_ref[...] = (h_s[0] + h_s[1]) * g

    return pl.pallas_call(
        body, name="rnn_fwd", grid=(D_RNN // LANES, n // t),
        in_specs=[seq, seq, cw, vec1, wspec, gspec, vec2], out_specs=seq,
        out_shape=jax.ShapeDtypeStruct((n, D_RNN), F32),
        scratch_shapes=[pltpu.VMEM((2, t, LANES), F32)] * 4,
        compiler_params=pltpu.CompilerParams(dimension_semantics=("parallel", "parallel"), vmem_limit_bytes=VMEM_LIMIT),
    )(xr, xg, conv_w, conv_b, wblk, gbias, lam)


def _rnn_bwd(xr, xg, do, conv_w, conv_b, wblk, gbias, lam, srcs=(), scatter=()):
    n = xr.shape[0]
    t = _t_pad()
    seq, cw, vec1, vec2, wspec, gspec = _rnn_specs(t)
    nk = len(srcs)
    c_in, c_out, c_shape, c_sems = _exchange_specs(srcs, scatter)

    def body(xr_ref, xg_ref, do_ref, cw_ref, cb_ref, w_ref, gb_ref, lam_ref, *rest):
        dxr_ref, dxg_ref, dcw_ref, dcb_ref, dw_ref, dgb_ref, dlam_ref = rest[nk:nk + 7]
        a_s, b_s, h_s, l_s, p_s, back_s, r_s, i_s, q_s, dg_s = rest[2 * nk + 7 + len(c_sems):]
        finish = _ride(2, *_exchange_fns(rest[:nk], rest[nk + 7:2 * nk + 7], rest[2 * nk + 7:2 * nk + 7 + len(c_sems)],
                                         scatter))
        first = pl.program_id(1) == 0
        rows = lax.broadcasted_iota(jnp.int32, (t, LANES), 0)
        valid = rows >= PAD_ROWS
        xr = xr_ref[...]
        xc = _conv(xr, cw_ref, cb_ref, rows, t)
        xcb = xc.astype(BF16)
        gates = _dot(xcb, w_ref[0]) + gb_ref[0]
        sps = []
        for d in range(2):
            r_s[d], i_s[d], sp, a_s[d], _, q_s[d], b_s[d] = _lru_gates(xc, gates, lam_ref, valid, d)
            sps.append(sp)
        _scan([(a_s.at[d], b_s.at[d], h_s.at[d], p_s.at[d], d == 1) for d in range(2)], t)
        g, dg = _gelu_parts(xg_ref[...])
        do = do_ref[...]
        dxg_ref[...] = do * (h_s[0] + h_s[1]) * dg
        b_s[0] = do * g
        for d in range(2):
            back_s[d] = _shift_rows(a_s[d], -1 if d == 0 else 1, rows, t)
        _scan([(back_s.at[d], b_s.at[0], l_s.at[d], p_s.at[d], d == 0) for d in range(2)], t)
        dxc = jnp.zeros((t, LANES), F32)
        dlams = []
        for d in range(2):
            r, i, sp, a, sq = r_s[d], i_s[d], sps[d], a_s[d], q_s[d]
            lam_t = l_s[d]
            da = lam_t * _shift_rows(h_s[d], 1 if d == 0 else -1, rows, t)
            lam_v = jnp.where(valid, lam_t, 0.0)
            dsq = lam_v * (i * xc)
            di = lam_v * sq * xc
            dxc = dxc + lam_v * sq * i
            dm = jnp.where(sq > 0.0, dsq * 0.5 / jnp.where(sq > 0.0, sq, 1.0), 0.0)
            dla = da * a - 2.0 * dm * a * a
            dr = dla * (-LRU_C) * sp
            dsp = _colsum(dla * (-LRU_C) * r)
            dlams.append(dsp * -jax.nn.sigmoid(-lam_ref[d:d + 1, :]))
            dg_s[:, (2 * d) * LANES:(2 * d + 1) * LANES] = (dr * r * (1.0 - r)).astype(BF16)
            dg_s[:, (2 * d + 1) * LANES:(2 * d + 2) * LANES] = (di * i * (1.0 - i)).astype(BF16)
        dgates = dg_s[...]
        dxc = dxc + _dot_nt(dgates, w_ref[0])
        taps = [_shift_rows(dxc, j - 2, rows, t) for j in range(4)]
        dxr_ref[...] = (cw_ref[0:1, :] * taps[0] + cw_ref[1:2, :] * taps[1] + cw_ref[2:3, :] * taps[2]
                        + cw_ref[3:4, :] * taps[3])
        dcw = jnp.concatenate([_colsum(tap * xr) for tap in taps], axis=0)
        _acc(dcw_ref, first, dcw)
        _acc(dcb_ref, first, _colsum(dxc))
        _acc(dw_ref, first, _dot_tn(xcb, dgates)[None])
        _acc(dgb_ref, first, _colsum(dgates.astype(F32))[None])
        _acc(dlam_ref, first, jnp.concatenate(dlams, axis=0))
        finish()

    res = pl.pallas_call(
        body, name="rnn_bwd", grid=(D_RNN // LANES, n // t),
        in_specs=[seq, seq, seq, cw, vec1, wspec, gspec, vec2] + c_in,
        out_specs=[seq, seq, cw, vec1, wspec, gspec, vec2] + c_out,
        out_shape=[jax.ShapeDtypeStruct((n, D_RNN), F32), jax.ShapeDtypeStruct((n, D_RNN), F32),
                   jax.ShapeDtypeStruct((4, D_RNN), F32), jax.ShapeDtypeStruct((1, D_RNN), F32),
                   jax.ShapeDtypeStruct((D_RNN // LANES, LANES, 4 * LANES), F32),
                   jax.ShapeDtypeStruct((D_RNN // LANES, 1, 4 * LANES), F32), jax.ShapeDtypeStruct((2, D_RNN), F32)]
        + c_shape,
        scratch_shapes=c_sems + [pltpu.VMEM((2, t, LANES), F32)] * 9 + [pltpu.VMEM((t, 4 * LANES), BF16)],
        compiler_params=pltpu.CompilerParams(dimension_semantics=("arbitrary", "arbitrary"), vmem_limit_bytes=VMEM_LIMIT),
    )(xr, xg, do, conv_w, conv_b, wblk, gbias, lam, *srcs)
    return res[:7], res[7:]


def _post(oa, orn, h0, tgt, ga, gr, g2, w_out, w_gate, w_up, w_down):
    n = oa.shape[0]
    tm = _row_tile(n)
    t = _t_pad()
    head = PAD_ROWS + N_META
    parts = tm // head

    def body(oa_ref, or_ref, h0_ref, *rest):
        tgt_refs = rest[:parts]
        (ga_ref, gr_ref, g2_ref, wo_ref, wg_ref, wu_ref, wd_ref,
         doa_ref, dor_ref, dh1_ref, mix_ref, h1n_ref, act_ref, dgate_ref, dup_ref, dy_ref,
         loss_ref, dga_ref, dgr_ref, dg2_ref, gate_s, up_s) = rest[parts:]
        first = pl.program_id(0) == 0
        xa, ra = _rms(oa_ref[...], D_ATTN)
        xr, rr = _rms(or_ref[...], D_RNN)
        mix = jnp.concatenate([(xa * ga_ref[...]).astype(BF16), (xr * gr_ref[...]).astype(BF16)], axis=-1)
        mix_ref[...] = mix.T
        h1 = h0_ref[...] + _dot(mix, wo_ref[...])
        x2, r2 = _rms(h1, D_MODEL)
        h1n = (x2 * g2_ref[...]).astype(BF16)
        h1n_ref[...] = h1n
        y = h1
        for cs in range(0, D_FF, FF_CHUNK):
            sl = slice(cs, cs + FF_CHUNK)
            gate = _dot_nt(h1n, wg_ref[sl, :])
            up = _dot_nt(h1n, wu_ref[sl, :])
            gate_s[:, sl] = gate
            up_s[:, sl] = up
            act = (gate * _sigmoid(gate) * up).astype(BF16)
            act_ref[sl, :] = act.T
            y = y + _dot(act, wd_ref[sl, :])
        row = pl.program_id(0) * tm + lax.broadcasted_iota(jnp.int32, (tm, 1), 0)
        for _ in range(1, n // t):
            row = jnp.where(row >= t, row - t, row)
        tgt = jnp.concatenate([ref[0] for ref in tgt_refs], axis=0)
        err = jnp.where(row >= PAD_ROWS + N_META, y - tgt, 0.0)
        _acc(loss_ref, first, jnp.full((1, LANES), 0.5 / D_MODEL, F32) * jnp.sum(err * err))
        dy = err * (1.0 / D_MODEL)
        dyb = dy.astype(BF16)
        dy_ref[...] = dyb
        dh1n = jnp.zeros((tm, D_MODEL), F32)
        for cs in range(0, D_FF, FF_CHUNK):
            sl = slice(cs, cs + FF_CHUNK)
            dact = _dot_nt(dyb, wd_ref[sl, :])
            gate, up = gate_s[:, sl], up_s[:, sl]
            sg = _sigmoid(gate)
            dgate = (dact * up * sg * (1.0 + gate * (1.0 - sg))).astype(BF16)
            dup = (dact * gate * sg).astype(BF16)
            dgate_ref[sl, :] = dgate.T
            dup_ref[sl, :] = dup.T
            dh1n = dh1n + _dot(dgate, wg_ref[sl, :]) + _dot(dup, wu_ref[sl, :])
        _acc(dg2_ref, first, _colsum(dh1n * x2))
        dh1 = dy + _rms_bwd(dh1n, x2, r2, g2_ref[...], D_MODEL)
        dh1_ref[...] = dh1
        dmix = _dot_nt(dh1.astype(BF16), wo_ref[...])
        dma, dmr = dmix[:, :D_ATTN], dmix[:, D_ATTN:]
        _acc(dga_ref, first, _colsum(dma * xa))
        _acc(dgr_ref, first, _colsum(dmr * xr))
        doa_ref[...] = _rms_bwd(dma, xa, ra, ga_ref[...], D_ATTN)
        dor_ref[...] = _rms_bwd(dmr, xr, rr, gr_ref[...], D_RNN)

    def row(w):
        return pl.BlockSpec((tm, w), lambda i: (i, 0))

    def acc(w):
        return pl.BlockSpec((1, w), lambda i: (0, 0))

    def col(w):
        return pl.BlockSpec((w, tm), lambda i: (0, i))

    outs = [(D_ATTN, F32, row), (D_RNN, F32, row), (D_MODEL, F32, row), (D_MODEL, BF16, col), (D_MODEL, BF16, row),
            (D_FF, BF16, col), (D_FF, BF16, col), (D_FF, BF16, col), (D_MODEL, BF16, row)]
    accs = [LANES, D_ATTN, D_RNN, D_MODEL]
    per = t // head

    def target_part(p):
        def index(i):
            block = i * parts + p
            return block // per, jnp.maximum(block % per - 1, 0), 0
        return pl.BlockSpec((1, head, D_MODEL), index)

    return pl.pallas_call(
        body, name="post", grid=(n // tm,),
        in_specs=[row(D_ATTN), row(D_RNN), row(D_MODEL)] + [target_part(p) for p in range(parts)] + [
                  _const_spec((1, D_ATTN)), _const_spec((1, D_RNN)), _const_spec((1, D_MODEL)),
                  _const_spec((D_MODEL, D_MODEL)), _const_spec((D_FF, D_MODEL)), _const_spec((D_FF, D_MODEL)),
                  _const_spec((D_FF, D_MODEL))],
        out_specs=[spec(w) for w, _, spec in outs] + [acc(w) for w in accs],
        out_shape=[jax.ShapeDtypeStruct((n, w) if spec is row else (w, n), dt) for w, dt, spec in outs]
        + [jax.ShapeDtypeStruct((1, w), F32) for w in accs],
        scratch_shapes=[pltpu.VMEM((tm, D_FF), F32), pltpu.VMEM((tm, D_FF), F32)],
        compiler_params=pltpu.CompilerParams(dimension_semantics=("arbitrary",), vmem_limit_bytes=VMEM_LIMIT),
    )(oa, orn, h0, *[tgt] * parts, ga, gr, g2, w_out, w_gate, w_up, w_down)


def _in_bwd(dp, h0, dh1, ln1_g, w_in_p, srcs=(), scatter=()):
    n = h0.shape[0]
    tm = _row_tile(n)
    nk = len(srcs)
    c_in, c_out, c_shape, c_sems = _exchange_specs(srcs, scatter)

    def body(dp_ref, h0_ref, dh1_ref, g_ref, w_ref, *rest):
        dh0_ref, dg_ref = rest[nk:nk + 2]
        finish = _ride(1, *_exchange_fns(rest[:nk], rest[nk + 2:2 * nk + 2], rest[2 * nk + 2:], scatter))
        dhn = _dot(dp_ref[...], w_ref[...])
        xhat, r = _rms(h0_ref[...], D_MODEL)
        _acc(dg_ref, pl.program_id(0) == 0, _colsum(dhn * xhat))
        dh0_ref[...] = dh1_ref[...] + _rms_bwd(dhn, xhat, r, g_ref[...], D_MODEL)
        finish()

    def row(w):
        return pl.BlockSpec((tm, w), lambda i: (i, 0))

    res = pl.pallas_call(
        body, name="in_bwd", grid=(n // tm,),
        in_specs=[row(P_COLS), row(D_MODEL), row(D_MODEL), _const_spec((1, D_MODEL)), _const_spec((P_COLS, D_MODEL))] + c_in,
        out_specs=[row(D_MODEL), pl.BlockSpec((1, D_MODEL), lambda i: (0, 0))] + c_out,
        out_shape=[jax.ShapeDtypeStruct((n, D_MODEL), F32), jax.ShapeDtypeStruct((1, D_MODEL), F32)] + c_shape,
        scratch_shapes=c_sems,
        compiler_params=pltpu.CompilerParams(dimension_semantics=("arbitrary",), vmem_limit_bytes=VMEM_LIMIT),
    )(dp, h0, dh1, ln1_g, w_in_p, *srcs)
    return res[:2], res[2:]


MAX_TILE = D_FF // 2


def _pick_tile(width, cap):
    best = LANES
    for mult in range(1, width // LANES + 1):
        cand = mult * LANES
        if width % cand == 0 and cand <= cap:
            best = cand
    return best


def _matmul_tn(name, a, b, srcs=(), scatter=()):
    n, ka = a.shape
    kb = b.shape[1]
    ta, tb = _pick_tile(ka, MAX_TILE), _pick_tile(kb, MAX_TILE)
    tk = n // 2
    nk = len(srcs)
    c_in, c_out, c_shape, c_sems = _exchange_specs(srcs, scatter)

    def body(a_ref, b_ref, *rest):
        o_ref = rest[nk]
        finish = _ride(3, *_exchange_fns(rest[:nk], rest[nk + 1:2 * nk + 1], rest[2 * nk + 1:], scatter))
        _acc(o_ref, pl.program_id(2) == 0, _dot_tn(a_ref[...].astype(BF16), b_ref[...].astype(BF16)))
        finish()

    res = pl.pallas_call(
        body, name=name, grid=(ka // ta, kb // tb, n // tk),
        in_specs=[pl.BlockSpec((tk, ta), lambda i, j, k: (k, i)), pl.BlockSpec((tk, tb), lambda i, j, k: (k, j))] + c_in,
        out_specs=[pl.BlockSpec((ta, tb), lambda i, j, k: (i, j))] + c_out,
        out_shape=[jax.ShapeDtypeStruct((ka, kb), F32)] + c_shape, scratch_shapes=c_sems,
        compiler_params=pltpu.CompilerParams(dimension_semantics=("arbitrary", "arbitrary", "arbitrary"),
                                             vmem_limit_bytes=VMEM_LIMIT),
    )(a, b, *srcs)
    return res[0], res[1:]


def _matmul_shards(name, at, b):
    ka, n = at.shape
    kb = b.shape[1]
    width = ka // N_DEV
    per = 2 if 2 * width >= 4 * LANES else 4
    ta = per * width

    def body(a_ref, b_ref, o_ref):
        out = _dot(a_ref[...], b_ref[...].astype(BF16))
        for s in range(per):
            o_ref[s] = out[s * width:(s + 1) * width, :].astype(BF16)

    return pl.pallas_call(
        body, name=name, grid=(ka // ta,),
        in_specs=[pl.BlockSpec((ta, n), lambda i: (i, 0)), _const_spec((n, kb))],
        out_specs=pl.BlockSpec((per, width, kb), lambda i: (i, 0, 0)),
        out_shape=jax.ShapeDtypeStruct((N_DEV, width, kb), BF16),
        compiler_params=pltpu.CompilerParams(dimension_semantics=("parallel",), vmem_limit_bytes=VMEM_LIMIT),
    )(at, b)


def _adamw_math(g8_ref, w_ref, m_ref, v_ref, g_ref, d_ref, nm_ref, nv_ref):
    g = g8_ref[0].astype(F32)
    for s in range(1, N_DEV):
        g = g + g8_ref[s].astype(F32)
    g_ref[...] = g
    nm = ADAM_B1 * m_ref[...] + (1.0 - ADAM_B1) * g
    nv = ADAM_B2 * v_ref[...] + (1.0 - ADAM_B2) * (g * g)
    nm_ref[...] = nm
    nv_ref[...] = nv
    m_hat = nm / (1.0 - ADAM_B1 ** ADAM_STEP)
    v_hat = nv / (1.0 - ADAM_B2 ** ADAM_STEP)
    d_ref[...] = -ADAM_LR * (m_hat / (jnp.sqrt(v_hat) + ADAM_EPS) + ADAM_WD * w_ref[...])


def _adamw_many(name, items):
    count = len(items)

    def body(*refs):
        ins, outs = refs[:4 * count], refs[4 * count:]
        for i in range(count):
            _adamw_math(*ins[4 * i:4 * i + 4], *outs[4 * i:4 * i + 4])

    flat = [a for item in items for a in item]
    res = pl.pallas_call(
        body, name=name,
        out_shape=[jax.ShapeDtypeStruct(item[1].shape, F32) for item in items for _ in range(4)],
        compiler_params=pltpu.CompilerParams(vmem_limit_bytes=VMEM_LIMIT),
    )(*flat)
    return [tuple(res[4 * i:4 * i + 4]) for i in range(count)]


def _adamw(name, g8, w, m, v):
    rows, cols = w.shape
    tr = rows
    for cand in (256, 176, 128, 64):
        if rows % cand == 0 and rows > cand:
            tr = cand
            break

    def body(*refs):
        _adamw_math(*refs)

    blk = pl.BlockSpec((tr, cols), lambda i: (i, 0))
    return pl.pallas_call(
        body, name=name, grid=(rows // tr,),
        in_specs=[pl.BlockSpec((N_DEV, tr, cols), lambda i: (0, i, 0)), blk, blk, blk],
        out_specs=[blk] * 4, out_shape=[jax.ShapeDtypeStruct((rows, cols), F32)] * 4,
        compiler_params=pltpu.CompilerParams(dimension_semantics=("parallel",), vmem_limit_bytes=VMEM_LIMIT),
    )(g8, w, m, v)


def _exchange_specs(srcs, scatter):
    nk = len(srcs)
    if not nk:
        return [], [], [], []
    any_spec = pl.BlockSpec(memory_space=pl.ANY)
    out_shape = [jax.ShapeDtypeStruct(s.shape if sc else (N_DEV,) + s.shape, s.dtype) for s, sc in zip(srcs, scatter)]
    sems = [pltpu.SemaphoreType.DMA((nk, N_DEV - 1)), pltpu.SemaphoreType.DMA((nk, N_DEV - 1)),
            pltpu.SemaphoreType.DMA((nk,))]
    return [any_spec] * nk, [any_spec] * nk, out_shape, sems


FLIPS = ((0, 0, 1), (1, 0, 0), (0, 1, 0), (1, 1, 0), (1, 0, 1), (0, 1, 1), (1, 1, 1))
N_CHIP_PEERS = 3


def _exchange_fns(src_refs, out_refs, sems, scatter):
    nk = len(src_refs)
    if not nk:
        return (lambda: None), (lambda: None), (lambda: None)
    send_sems, recv_sems, local_sems = sems
    first = 1 + N_CHIP_PEERS

    def plan():
        x, y, c = lax.axis_index("x"), lax.axis_index("y"), lax.axis_index("c")
        me = 4 * x + 2 * y + c
        peers = [(1 - x if fx else x, 1 - y if fy else y, 1 - c if fc else c) for fx, fy, fc in FLIPS]
        pids = [4 * px + 2 * py + pc for px, py, pc in peers]

        def remote(k, j, src, dst, to):
            return pltpu.make_async_remote_copy(src_ref=src, dst_ref=dst, send_sem=send_sems.at[k, j],
                                                recv_sem=recv_sems.at[k, j], device_id=to, device_id_type=MESH)

        def mine(k, dest):
            return src_refs[k].at[dest] if scatter[k] else src_refs[k]

        local = [pltpu.make_async_copy(mine(k, me), out_refs[k].at[me], local_sems.at[k]) for k in range(nk)]
        direct = [remote(k, j, mine(k, pids[j]), out_refs[k].at[me], peers[j])
                  for k in range(nk) for j in range(len(FLIPS) if scatter[k] else first)]
        relays = {(k, j): remote(k, j, out_refs[k].at[pids[j - N_CHIP_PEERS]], out_refs[k].at[pids[j - N_CHIP_PEERS]], peers[0])
                  for k in range(nk) if not scatter[k] for j in range(first, len(FLIPS))}
        arrivals = {(k, j): remote(k, j, out_refs[k].at[pids[j]], out_refs[k].at[pids[j]], peers[j])
                    for k in range(nk) for j in range(len(FLIPS))}
        return local, direct, relays, arrivals

    def start():
        local, direct, _, _ = plan()
        for cp in local + direct:
            cp.start()

    def relay():
        _, _, relays, arrivals = plan()
        for (k, j), cp in relays.items():
            arrivals[k, j - N_CHIP_PEERS].wait_recv()
            cp.start()

    def wait():
        local, direct, relays, arrivals = plan()
        for (k, j), cp in arrivals.items():
            if (k, j + N_CHIP_PEERS) not in relays:
                cp.wait_recv()
        for cp in direct + list(relays.values()):
            cp.wait_send()
        for cp in local:
            cp.wait()

    return start, relay, wait


def _grid_step(rank):
    step, total = 0, 1
    for axis in range(rank):
        step = step * pl.num_programs(axis) + pl.program_id(axis)
        total = total * pl.num_programs(axis)
    return step, total


def _ride(rank, start, relay, wait):
    step, total = _grid_step(rank)
    pl.when(step == 0)(start)
    pl.when(step == (3 * total) // 4)(relay)
    return lambda: pl.when(step == total - 1)(wait)


def _exchange(name, srcs, scatter):
    nk = len(srcs)
    c_in, c_out, c_shape, c_sems = _exchange_specs(srcs, scatter)

    def body(*refs):
        start, relay, wait = _exchange_fns(refs[:nk], refs[nk:2 * nk], refs[2 * nk:], scatter)
        start()
        relay()
        wait()

    return pl.pallas_call(body, name=name, in_specs=c_in, out_specs=c_out, out_shape=c_shape, scratch_shapes=c_sems)(*srcs)


def _cols_from_shards(g):
    return jnp.transpose(g, (1, 0, 2)).reshape(g.shape[1], -1)


def _cols_to_shards(w):
    return jnp.transpose(w.reshape(w.shape[0], N_DEV, -1), (1, 0, 2))


def _prep(x, srcs, scatter):
    nb = x.shape[0]
    t = _t_pad()
    head = PAD_ROWS + N_META
    nk = len(srcs)
    c_in, c_out, c_shape, c_sems = _exchange_specs(srcs, scatter)

    def body(x_ref, *rest):
        h0_ref = rest[nk]
        finish = _ride(1, *_exchange_fns(rest[:nk], rest[nk + 1:2 * nk + 1], rest[2 * nk + 1:], scatter))
        lead = pl.program_id(0) == 0

        @pl.when(lead)
        def _():
            h0_ref[...] = jnp.zeros_like(h0_ref)

        @pl.when(jnp.logical_not(lead))
        def _():
            h0_ref[...] = x_ref[...]

        finish()

    src = pl.BlockSpec((nb, head, D_MODEL), lambda j: (0, jnp.maximum(j - 1, 0), 0))
    dst = pl.BlockSpec((nb, head, D_MODEL), lambda j: (0, j, 0))
    res = pl.pallas_call(
        body, name="prep", grid=(t // head,), in_specs=[src] + c_in, out_specs=[dst] + c_out,
        out_shape=[jax.ShapeDtypeStruct((nb, t, D_MODEL), F32)] + c_shape, scratch_shapes=c_sems,
        compiler_params=pltpu.CompilerParams(dimension_semantics=("arbitrary",)),
    )(x, *srcs)
    return res[0], res[1:]


def _rope_tables(n):
    t = _t_pad()
    pos = np.arange(t, dtype=np.float32) - np.float32(PAD_ROWS)
    half = QK_ROPE // 2
    freqs = (1.0 / (ROPE_THETA ** (np.arange(half, dtype=np.float32) / half))).astype(np.float32)
    ang = pos[:, None] * freqs[None, :]
    cos, sin = np.cos(ang), np.sin(ang)
    z = lambda w: np.zeros((t, w), np.float32)
    c = np.concatenate([np.ones((t, QK_NOPE), np.float32), cos, cos, z(HEAD_PAD - QK_HEAD)], axis=1)
    s1 = np.concatenate([z(QK_NOPE + half), sin, z(HEAD_PAD - QK_HEAD)], axis=1)
    s2 = np.concatenate([z(QK_NOPE), -sin, z(HEAD_PAD - QK_NOPE - half)], axis=1)
    return tuple(jnp.asarray(np.tile(a, (n // t, 1))) for a in (c, s1, s2))


def _block_diag_gates(lru_wa, lru_wi):
    eye = jnp.eye(2, dtype=lru_wa.dtype)

    def bd(w):
        w = w.reshape(2, D_RNN // LANES, 2, RNN_BW, RNN_BW)
        full = w[:, :, :, :, None, :] * eye[None, None, :, None, :, None]
        return full.reshape(2, D_RNN // LANES, LANES, LANES)

    a, i = bd(lru_wa), bd(lru_wi)
    return jnp.concatenate([a[0], i[0], a[1], i[1]], axis=-1)


def _unblock_gates(dw):
    nb = D_RNN // LANES
    parts = dw.reshape(nb, 2, RNN_BW, 4, 2, RNN_BW)
    diag = jnp.stack([parts[:, k, :, :, k, :] for k in range(2)], axis=1)
    diag = jnp.transpose(diag, (3, 0, 1, 2, 4)).reshape(4, 2 * nb, RNN_BW, RNN_BW)
    return jnp.stack([diag[0], diag[2]]), jnp.stack([diag[1], diag[3]])


WEIGHTS = ("meta_tokens", "ln1_g", "w_in", "q_a_norm_g", "w_uq", "kv_a_norm_g", "w_ukv", "q_norm_g", "k_norm_g",
           "conv_w", "conv_b", "lru_wa", "lru_ba", "lru_wi", "lru_bi", "lru_lambda", "attn_out_g", "rnn_out_g",
           "w_out", "ln2_g", "w_gate", "w_up", "w_down")
BIG = ("w_in", "w_uq", "w_ukv", "w_out", "w_gate", "w_up", "w_down")
TRANSPOSED = ("w_in", "w_uq", "w_gate", "w_up")
ROW_SHARDED = ("w_out", "w_down") + TRANSPOSED
REPLICATED = ("ln1_g", "q_a_norm_g", "kv_a_norm_g", "q_norm_g", "k_norm_g", "conv_b", "lru_wa", "lru_wi",
              "attn_out_g", "rnn_out_g", "ln2_g")
WHOLE = REPLICATED + ("loss",)
G_FIRST = ("w_in", "meta_tokens")
G_MID = ("w_uq", "w_ukv", "conv_w", "lru_ba", "lru_bi", "lru_lambda")
LATE = ("w_out", "w_gate", "w_up", "w_down")
G_LAST = ("meta_tokens", "ln1_g")


def _local_step(x, tgt, ex):
    nb = x.shape[0]
    t = _t_pad()
    n = nb * t
    local = ex.local
    h0, got = _prep(x, *ex.gather_srcs(G_FIRST))
    first = ex.gathered(G_FIRST, got)
    meta, w_in = first["meta_tokens"], first["w_in"]
    h0 = h0.at[:, PAD_ROWS:PAD_ROWS + N_META].set(jnp.broadcast_to(meta[None], (nb, N_META, D_MODEL))).reshape(n, D_MODEL)

    zr = lambda r: jnp.zeros((r, D_MODEL), w_in.dtype)
    w_in_p = jnp.concatenate([w_in[:OFF_CKV], w_in[OFF_KR:], zr(QK_NOPE), w_in[OFF_CKV:OFF_KR], zr(HEAD_PAD - QK_HEAD)],
                             axis=0)
    pad_g = lambda g: jnp.pad(g, ((0, 0), (0, HEAD_PAD - QK_HEAD)))
    qg, kg = pad_g(local["q_norm_g"]), pad_g(local["k_norm_g"])
    rc, rs1, rs2 = _rope_tables(n)
    wblk = _block_diag_gates(local["lru_wa"].reshape(2, -1, RNN_BW, RNN_BW),
                             local["lru_wi"].reshape(2, -1, RNN_BW, RNN_BW)).astype(BF16)
    nblk = D_RNN // LANES

    (hn, cq, ckv, xr, xg, kr), got = _in_proj(h0, local["ln1_g"], w_in_p, *ex.gather_srcs(G_MID))
    w = ex.gathered(G_MID, got)
    w_uq_p = jnp.pad(w["w_uq"].reshape(N_HEADS, QK_HEAD, Q_LORA), ((0, 0), (0, HEAD_PAD - QK_HEAD), (0, 0))
                     ).reshape(QP_COLS, Q_LORA)
    ukv = w["w_ukv"].reshape(KV_LORA, N_HEADS, QK_NOPE + V_HEAD)
    w_uk_p = jnp.pad(ukv[:, :, :QK_NOPE], ((0, 0), (0, 0), (0, HEAD_PAD - QK_NOPE))).reshape(KV_LORA, QP_COLS)
    w_v = ukv[:, :, QK_NOPE:].reshape(KV_LORA, D_ATTN)
    gbias = jnp.stack([w["lru_ba"][0], w["lru_bi"][0], w["lru_ba"][1], w["lru_bi"][1]], axis=0)
    gbias = jnp.transpose(gbias.reshape(4, nblk, LANES), (1, 0, 2)).reshape(nblk, 1, 4 * LANES)

    q, k, v = _qkv_fwd(cq, ckv, kr, local["q_a_norm_g"], local["kv_a_norm_g"], w_uq_p, w_uk_p, w_v, qg, kg, rc, rs1, rs2)
    oa, probs, got = _attn_fwd(q, k, v, *ex.gather_srcs(LATE))
    late = ex.gathered(LATE, got)
    orn = _rnn_fwd(xr, xg, w["conv_w"], local["conv_b"], wblk, gbias, w["lru_lambda"])
    (doa, dor, dh1, mix_t, h1n, act_t, dgate_t, dup_t, dyb, loss, dga, dgr, dg2) = _post(
        oa, orn, h0, tgt, local["attn_out_g"], local["rnn_out_g"], local["ln2_g"], late["w_out"], late["w_gate"],
        late["w_up"], late["w_down"])
    wire = {"w_out": _matmul_shards("dw_out", mix_t, dh1), "w_gate": _matmul_shards("dw_gate", dgate_t, h1n),
            "w_up": _matmul_shards("dw_up", dup_t, h1n), "w_down": _matmul_shards("dw_down", act_t, dyb)}
    names = ("w_out", "w_gate")
    (dxr, dxg, dcw, dcb, dwblk, dgb, dlam), got = _rnn_bwd(xr, xg, dor, w["conv_w"], local["conv_b"], wblk, gbias,
                                                           w["lru_lambda"], *ex.scatter_srcs(names, wire))
    summed = ex.scattered(names, wire, got)
    dwa, dwi = _unblock_gates(dwblk)
    dgb = jnp.transpose(dgb.reshape(nblk, 4, LANES), (1, 0, 2)).reshape(4, D_RNN)
    names = ("w_up", "w_down")
    (dq_r, dk_r, dv), got = _attn_bwd(q, k, v, doa, oa, probs, *ex.scatter_srcs(names, wire))
    summed.update(ex.scattered(names, wire, got))
    wire = ex.to_wire({
        "conv_w": dcw, "conv_b": dcb, "lru_wa": dwa.reshape(-1, RNN_BW), "lru_ba": jnp.stack([dgb[0], dgb[2]]),
        "lru_wi": dwi.reshape(-1, RNN_BW), "lru_bi": jnp.stack([dgb[1], dgb[3]]), "lru_lambda": dlam,
        "attn_out_g": dga, "rnn_out_g": dgr, "ln2_g": dg2, "loss": loss})
    names = tuple(wire)
    (dp, qa, kva, dqp, dkv, dqg, dkg, dgqa, dgkva), got = _qkv_bwd(
        cq, ckv, kr, dq_r, dk_r, dv, dxr, dxg, local["q_a_norm_g"], local["kv_a_norm_g"], w_uq_p, w_uk_p, w_v, qg, kg,
        rc, rs1, rs2, *ex.scatter_srcs(names, wire))
    summed.update(ex.scattered(names, wire, got))
    dw_uq_p, _ = _matmul_tn("dw_uq", dqp, qa)
    dw_kv, _ = _matmul_tn("dw_ukv", kva, dkv)
    dw_uq = dw_uq_p.reshape(N_HEADS, HEAD_PAD, Q_LORA)[:, :QK_HEAD].reshape(N_HEADS * QK_HEAD, Q_LORA)
    dw_ukv = jnp.concatenate([dw_kv[:, :QP_COLS].reshape(KV_LORA, N_HEADS, HEAD_PAD)[:, :, :QK_NOPE],
                              dw_kv[:, QP_COLS:].reshape(KV_LORA, N_HEADS, V_HEAD)], axis=2).reshape(KV_LORA, -1)
    wire = ex.to_wire({"q_a_norm_g": dgqa, "w_uq": dw_uq, "kv_a_norm_g": dgkva, "w_ukv": dw_ukv,
                       "q_norm_g": dqg[:, :QK_HEAD], "k_norm_g": dkg[:, :QK_HEAD]})
    names = tuple(wire)
    dw_in_p, got = _matmul_tn("dw_in", dp, hn, *ex.scatter_srcs(names, wire))
    summed.update(ex.scattered(names, wire, got))
    kr0 = OFF_CKV + 2 * D_RNN + QK_NOPE
    dw_in = jnp.concatenate([dw_in_p[:OFF_CKV], dw_in_p[kr0:kr0 + QK_ROPE], dw_in_p[OFF_CKV:OFF_CKV + 2 * D_RNN]], axis=0)
    wire = ex.to_wire({"w_in": dw_in})
    (dh0, dg1), got = _in_bwd(dp, h0, dh1, local["ln1_g"], w_in_p, *ex.scatter_srcs(("w_in",), wire))
    summed.update(ex.scattered(("w_in",), wire, got))

    dh0 = dh0.reshape(nb, t, D_MODEL)
    wire = ex.to_wire({"meta_tokens": jnp.sum(dh0[:, PAD_ROWS:PAD_ROWS + N_META], axis=0), "ln1_g": dg1})
    got = ex.run("reduce_last", *ex.scatter_srcs(G_LAST, wire))
    summed.update(ex.scattered(G_LAST, wire, got))
    return dh0[:, PAD_ROWS + N_META:], summed


class _MeshExchange:
    def __init__(self, shards):
        self.local = shards

    @staticmethod
    def run(name, srcs, scatter):
        return _exchange(name, srcs, scatter)

    def gather_srcs(self, names):
        return [self.local[k].astype(BF16) if k in BIG else self.local[k] for k in names], [False] * len(names)

    @staticmethod
    def gathered(names, outs):
        return {k: g.reshape(-1, g.shape[-1]) if k in ROW_SHARDED else _cols_from_shards(g) for k, g in zip(names, outs)}

    @staticmethod
    def to_wire(grads):
        wire = {}
        for k, g in grads.items():
            if k in WHOLE:
                wire[k] = g
            elif k in ROW_SHARDED:
                wire[k] = g.reshape(N_DEV, -1, g.shape[-1]).astype(BF16)
            else:
                wire[k] = _cols_to_shards(g).astype(BF16) if k in BIG else _cols_to_shards(g)
        return wire

    @staticmethod
    def scatter_srcs(names, wire):
        return [wire[k] for k in names], [k not in WHOLE for k in names]

    @staticmethod
    def scattered(names, wire, outs):
        return dict(zip(names, outs))


def kernel(x, meta_tokens, ln1_g, w_in, q_a_norm_g, w_uq, kv_a_norm_g, w_ukv, q_norm_g, k_norm_g, conv_w, conv_b, lru_wa, lru_ba, lru_wi, lru_bi, lru_lambda, attn_out_g, rnn_out_g, w_out, ln2_g, w_gate, w_up, w_down, loss_target, m_meta_tokens, m_ln1_g, m_w_in, m_q_a_norm_g, m_w_uq, m_kv_a_norm_g, m_w_ukv, m_q_norm_g, m_k_norm_g, m_conv_w, m_conv_b, m_lru_wa, m_lru_ba, m_lru_wi, m_lru_bi, m_lru_lambda, m_attn_out_g, m_rnn_out_g, m_w_out, m_ln2_g, m_w_gate, m_w_up, m_w_down, v_meta_tokens, v_ln1_g, v_w_in, v_q_a_norm_g, v_w_uq, v_kv_a_norm_g, v_w_ukv, v_q_norm_g, v_k_norm_g, v_conv_w, v_conv_b, v_lru_wa, v_lru_ba, v_lru_wi, v_lru_bi, v_lru_lambda, v_attn_out_g, v_rnn_out_g, v_w_out, v_ln2_g, v_w_gate, v_w_up, v_w_down):
    given = (meta_tokens, ln1_g, w_in, q_a_norm_g, w_uq, kv_a_norm_g, w_ukv, q_norm_g, k_norm_g, conv_w, conv_b,
             lru_wa, lru_ba, lru_wi, lru_bi, lru_lambda, attn_out_g, rnn_out_g, w_out, ln2_g, w_gate, w_up, w_down)
    moments_m = (m_meta_tokens, m_ln1_g, m_w_in, m_q_a_norm_g, m_w_uq, m_kv_a_norm_g, m_w_ukv, m_q_norm_g, m_k_norm_g,
                 m_conv_w, m_conv_b, m_lru_wa, m_lru_ba, m_lru_wi, m_lru_bi, m_lru_lambda, m_attn_out_g, m_rnn_out_g,
                 m_w_out, m_ln2_g, m_w_gate, m_w_up, m_w_down)
    moments_v = (v_meta_tokens, v_ln1_g, v_w_in, v_q_a_norm_g, v_w_uq, v_kv_a_norm_g, v_w_ukv, v_q_norm_g, v_k_norm_g,
                 v_conv_w, v_conv_b, v_lru_wa, v_lru_ba, v_lru_wi, v_lru_bi, v_lru_lambda, v_attn_out_g, v_rnn_out_g,
                 v_w_out, v_ln2_g, v_w_gate, v_w_up, v_w_down)
    shapes = {k: a.shape for k, a in zip(WEIGHTS, given)}

    def two_d(k, a):
        a = a.reshape(-1, a.shape[-1])
        return a.T if k in TRANSPOSED else a

    w = {k: two_d(k, a) for k, a in zip(WEIGHTS, given)}
    m = {k: two_d(k, a) for k, a in zip(WEIGHTS, moments_m)}
    v = {k: two_d(k, a) for k, a in zip(WEIGHTS, moments_v)}

    grad_x, parts = _local_step(x, loss_target, _MeshExchange(w))

    tiled = ("w_in", "w_gate", "w_up", "w_down")
    new = {k: _adamw("adamw_" + k, parts[k], w[k], m[k], v[k]) for k in tiled}
    small = [k for k in WEIGHTS if k not in tiled]
    new.update(zip(small, _adamw_many("adamw_small", [(parts[k], w[k], m[k], v[k]) for k in small])))

    loss = jnp.sum(parts["loss"][:, 0, 0])
    outs = [loss, grad_x]
    for idx in range(4):
        outs += [(new[k][idx].T if k in TRANSPOSED else new[k][idx]).reshape(shapes[k]) for k in WEIGHTS]
    return tuple(outs)
```

```python
import functools
import math

import numpy as np
import jax
import jax.numpy as jnp
from jax import lax
from jax.experimental import pallas as pl
from jax.experimental.pallas import tpu as pltpu

F32 = jnp.float32
BF16 = jnp.bfloat16

D_MODEL = 1024
N_META = 16
SEQ = 2048
N_HEADS = 8
QK_NOPE = 64
QK_ROPE = 32
QK_HEAD = QK_NOPE + QK_ROPE
V_HEAD = 64
D_ATTN = N_HEADS * V_HEAD
Q_LORA = 384
KV_LORA = 256
D_RNN = 512
RNN_BW = 64
D_FF = 2816
EPS = 1e-6
LRU_C = 8.0
ROPE_THETA = 10000.0
OFF_CKV = Q_LORA + KV_LORA
OFF_KR = OFF_CKV + QK_ROPE
IN_COLS = OFF_KR + 2 * D_RNN

ADAM_LR = 0.001
ADAM_B1 = 0.9
ADAM_B2 = 0.999
ADAM_EPS = 1e-08
ADAM_WD = 0.01
ADAM_STEP = 10

N_DEV = 8
LANES = 128
HEAD_PAD = LANES
PAD_ROWS = LANES - N_META
QP_COLS = N_HEADS * HEAD_PAD
P_COLS = OFF_CKV + 2 * D_RNN + LANES
FF_CHUNK = D_FF
VMEM_LIMIT = 56 * 1024 * 1024
MESH = pl.DeviceIdType.MESH


def _t_pad():
    return PAD_ROWS + N_META + SEQ


def _row_tile(n):
    return 256 if n % 256 == 0 else 128


def _wide_row_tile(n):
    quarter = _t_pad() // 4
    return quarter if quarter % 16 == 0 and n % quarter == 0 else _row_tile(n)


def _const_spec(shape):
    nd = len(shape)
    return pl.BlockSpec(shape, lambda *_: (0,) * nd, pipeline_mode=pl.Buffered(1))


def _rms(x, d):
    r = lax.rsqrt(jnp.sum(x * x, axis=-1, keepdims=True) * (1.0 / d) + EPS)
    return x * r, r


def _rms_bwd(dy, xhat, r, g, d):
    dxh = dy * g
    return r * (dxh - xhat * (jnp.sum(dxh * xhat, axis=-1, keepdims=True) * (1.0 / d)))


def _colsum(x):
    return jnp.sum(x, axis=0, keepdims=True)


def _dot(a, b):
    return jnp.dot(a, b, preferred_element_type=F32)


def _dot_nt(a, b):
    return lax.dot_general(a, b, (((1,), (1,)), ((), ())), preferred_element_type=F32)


def _dot_tn(a, b):
    return lax.dot_general(a, b, (((0,), (0,)), ((), ())), preferred_element_type=F32)


def _rope(x, c, s1, s2):
    return x * c + pltpu.roll(x, 16, 1) * s1 + pltpu.roll(x, HEAD_PAD - 16, 1) * s2


def _rope_bwd(dy, c, s1, s2):
    return dy * c + pltpu.roll(dy * s1, HEAD_PAD - 16, 1) + pltpu.roll(dy * s2, 16, 1)


def _acc(ref, first, val):
    @pl.when(first)
    def _():
        ref[...] = val

    @pl.when(jnp.logical_not(first))
    def _():
        ref[...] += val


def _in_proj(h0, ln1_g, w_in_p, srcs=(), scatter=()):
    n = h0.shape[0]
    tm = _wide_row_tile(n)
    nk = len(srcs)
    c_in, c_out, c_shape, c_sems = _exchange_specs(srcs, scatter)

    def body(h_ref, g_ref, w_ref, *rest):
        hn_ref, cq_ref, ckv_ref, xr_ref, xg_ref, kr_ref = rest[nk:nk + 6]
        finish = _ride(1, *_exchange_fns(rest[:nk], rest[nk + 6:2 * nk + 6], rest[2 * nk + 6:], scatter))
        xhat, _ = _rms(h_ref[...], D_MODEL)
        hn = (xhat * g_ref[...]).astype(BF16)
        hn_ref[...] = hn
        p = _dot_nt(hn, w_ref[...])
        cq_ref[...] = p[:, :Q_LORA]
        ckv_ref[...] = p[:, Q_LORA:OFF_CKV]
        xr_ref[...] = p[:, OFF_CKV:OFF_CKV + D_RNN]
        xg_ref[...] = p[:, OFF_CKV + D_RNN:OFF_CKV + 2 * D_RNN]
        kr_ref[...] = p[:, OFF_CKV + 2 * D_RNN:]
        finish()

    def row(w):
        return pl.BlockSpec((tm, w), lambda i: (i, 0))

    widths = (D_MODEL, Q_LORA, KV_LORA, D_RNN, D_RNN, LANES)
    res = pl.pallas_call(
        body, name="in_proj", grid=(n // tm,),
        in_specs=[row(D_MODEL), _const_spec((1, D_MODEL)), _const_spec((P_COLS, D_MODEL))] + c_in,
        out_specs=[row(w) for w in widths] + c_out,
        out_shape=[jax.ShapeDtypeStruct((n, w), BF16 if k == 0 else F32) for k, w in enumerate(widths)] + c_shape,
        scratch_shapes=c_sems,
        compiler_params=pltpu.CompilerParams(dimension_semantics=("arbitrary",), vmem_limit_bytes=VMEM_LIMIT),
    )(h0, ln1_g, w_in_p, *srcs)
    return res[:6], res[6:]


def _qkv_fwd(cq, ckv, kr, gqa, gkva, w_uq_p, w_uk_p, w_v, qg, kg, rc, rs1, rs2):
    n = cq.shape[0]
    tm = _wide_row_tile(n)

    def body(cq_ref, ckv_ref, kr_ref, gqa_ref, gkva_ref, wuq_ref, wuk_ref, wv_ref, qg_ref, kg_ref,
             c_ref, s1_ref, s2_ref, q_ref, k_ref, v_ref):
        xq, _ = _rms(cq_ref[...], Q_LORA)
        qa = (xq * gqa_ref[...]).astype(BF16)
        q = _dot_nt(qa, wuq_ref[...])
        xkv, _ = _rms(ckv_ref[...], KV_LORA)
        kva = (xkv * gkva_ref[...]).astype(BF16)
        kn = _dot(kva, wuk_ref[...])
        v_ref[...] = _dot(kva, wv_ref[...]).astype(BF16)
        krp = kr_ref[...]
        c, s1, s2 = c_ref[...], s1_ref[...], s2_ref[...]
        for h in range(N_HEADS):
            sl = slice(h * HEAD_PAD, (h + 1) * HEAD_PAD)
            qh, _ = _rms(q[:, sl], QK_HEAD)
            q_ref[:, sl] = _rope(qh * qg_ref[...], c, s1, s2).astype(BF16)
            kh, _ = _rms(kn[:, sl] + krp, QK_HEAD)
            k_ref[:, sl] = _rope(kh * kg_ref[...], c, s1, s2).astype(BF16)

    def row(w):
        return pl.BlockSpec((tm, w), lambda i: (i, 0))

    return pl.pallas_call(
        body, name="qkv_fwd", grid=(n // tm,),
        in_specs=[row(Q_LORA), row(KV_LORA), row(LANES), _const_spec((1, Q_LORA)), _const_spec((1, KV_LORA)),
                  _const_spec((QP_COLS, Q_LORA)), _const_spec((KV_LORA, QP_COLS)), _const_spec((KV_LORA, D_ATTN)),
                  _const_spec((1, LANES)), _const_spec((1, LANES)), row(LANES), row(LANES), row(LANES)],
        out_specs=[row(QP_COLS), row(QP_COLS), row(D_ATTN)],
        out_shape=[jax.ShapeDtypeStruct((n, QP_COLS), BF16), jax.ShapeDtypeStruct((n, QP_COLS), BF16),
                   jax.ShapeDtypeStruct((n, D_ATTN), BF16)],
        compiler_params=pltpu.CompilerParams(dimension_semantics=("parallel",), vmem_limit_bytes=VMEM_LIMIT),
    )(cq, ckv, kr, gqa, gkva, w_uq_p, w_uk_p, w_v, qg, kg, rc, rs1, rs2)


def _qkv_bwd(cq, ckv, kr, dq_r, dk_r, dv, dxr, dxg, gqa, gkva, w_uq_p, w_uk_p, w_v, qg, kg, rc, rs1, rs2,
             srcs=(), scatter=()):
    n = cq.shape[0]
    tm = _wide_row_tile(n)
    nk = len(srcs)
    c_in, c_out, c_shape, c_sems = _exchange_specs(srcs, scatter)

    def body(cq_ref, ckv_ref, kr_ref, dq_ref, dk_ref, dv_ref, dxr_ref, dxg_ref, gqa_ref, gkva_ref, wuq_ref, wuk_ref,
             wv_ref, qg_ref, kg_ref, c_ref, s1_ref, s2_ref, *rest):
        dp_ref, qa_ref, kva_ref, dqp_ref, dkv_ref, dqg_ref, dkg_ref, dgqa_ref, dgkva_ref = rest[nk:nk + 9]
        finish = _ride(1, *_exchange_fns(rest[:nk], rest[nk + 9:2 * nk + 9], rest[2 * nk + 9:], scatter))
        first = pl.program_id(0) == 0
        dp_ref[:, OFF_CKV:OFF_CKV + D_RNN] = dxr_ref[...].astype(BF16)
        dp_ref[:, OFF_CKV + D_RNN:OFF_CKV + 2 * D_RNN] = dxg_ref[...].astype(BF16)
        xq, rq = _rms(cq_ref[...], Q_LORA)
        qa = (xq * gqa_ref[...]).astype(BF16)
        qa_ref[...] = qa
        q = _dot_nt(qa, wuq_ref[...])
        xkv, rkv = _rms(ckv_ref[...], KV_LORA)
        kva = (xkv * gkva_ref[...]).astype(BF16)
        kva_ref[...] = kva
        kn = _dot(kva, wuk_ref[...])
        krp = kr_ref[...]
        c, s1, s2 = c_ref[...], s1_ref[...], s2_ref[...]
        lane = lax.broadcasted_iota(jnp.int32, (tm, HEAD_PAD), 1)
        rope_lanes = jnp.logical_and(lane >= QK_NOPE, lane < QK_HEAD)
        dqg = jnp.zeros((1, HEAD_PAD), F32)
        dkg = jnp.zeros((1, HEAD_PAD), F32)
        dkr = jnp.zeros((tm, HEAD_PAD), F32)
        for h in range(N_HEADS):
            sl = slice(h * HEAD_PAD, (h + 1) * HEAD_PAD)
            qh, rqh = _rms(q[:, sl], QK_HEAD)
            dy = _rope_bwd(dq_ref[:, sl], c, s1, s2)
            dqg = dqg + _colsum(dy * qh)
            dqp_ref[:, sl] = _rms_bwd(dy, qh, rqh, qg_ref[...], QK_HEAD).astype(BF16)
            kh, rkh = _rms(kn[:, sl] + krp, QK_HEAD)
            dyk = _rope_bwd(dk_ref[:, sl], c, s1, s2)
            dkg = dkg + _colsum(dyk * kh)
            dkh = _rms_bwd(dyk, kh, rkh, kg_ref[...], QK_HEAD)
            dkv_ref[:, sl] = dkh.astype(BF16)
            dkr = dkr + jnp.where(rope_lanes, dkh, 0.0)
        dkv_ref[:, QP_COLS:] = dv_ref[...].astype(BF16)
        dp_ref[:, OFF_CKV + 2 * D_RNN:] = dkr.astype(BF16)
        dqa = _dot(dqp_ref[...], wuq_ref[...])
        dp_ref[:, :Q_LORA] = _rms_bwd(dqa, xq, rq, gqa_ref[...], Q_LORA).astype(BF16)
        dkva = _dot_nt(dkv_ref[:, :QP_COLS], wuk_ref[...]) + _dot_nt(dkv_ref[:, QP_COLS:], wv_ref[...])
        dp_ref[:, Q_LORA:OFF_CKV] = _rms_bwd(dkva, xkv, rkv, gkva_ref[...], KV_LORA).astype(BF16)
        _acc(dqg_ref, first, dqg)
        _acc(dkg_ref, first, dkg)
        _acc(dgqa_ref, first, _colsum(dqa * xq))
        _acc(dgkva_ref, first, _colsum(dkva * xkv))
        finish()

    def row(w):
        return pl.BlockSpec((tm, w), lambda i: (i, 0))

    def acc(w):
        return pl.BlockSpec((1, w), lambda i: (0, 0))

    res = pl.pallas_call(
        body, name="qkv_bwd", grid=(n // tm,),
        in_specs=[row(Q_LORA), row(KV_LORA), row(LANES), row(QP_COLS), row(QP_COLS), row(D_ATTN), row(D_RNN), row(D_RNN),
                  _const_spec((1, Q_LORA)), _const_spec((1, KV_LORA)),
                  _const_spec((QP_COLS, Q_LORA)), _const_spec((KV_LORA, QP_COLS)), _const_spec((KV_LORA, D_ATTN)),
                  _const_spec((1, LANES)), _const_spec((1, LANES)), row(LANES), row(LANES), row(LANES)] + c_in,
        out_specs=[row(P_COLS), row(Q_LORA), row(KV_LORA), row(QP_COLS),
                   row(QP_COLS + D_ATTN), acc(LANES), acc(LANES), acc(Q_LORA), acc(KV_LORA)] + c_out,
        out_shape=[jax.ShapeDtypeStruct((n, P_COLS), BF16), jax.ShapeDtypeStruct((n, Q_LORA), BF16),
                   jax.ShapeDtypeStruct((n, KV_LORA), BF16), jax.ShapeDtypeStruct((n, QP_COLS), BF16),
                   jax.ShapeDtypeStruct((n, QP_COLS + D_ATTN), BF16),
                   jax.ShapeDtypeStruct((1, LANES), F32), jax.ShapeDtypeStruct((1, LANES), F32),
                   jax.ShapeDtypeStruct((1, Q_LORA), F32), jax.ShapeDtypeStruct((1, KV_LORA), F32)] + c_shape,
        scratch_shapes=c_sems,
        compiler_params=pltpu.CompilerParams(dimension_semantics=("arbitrary",), vmem_limit_bytes=VMEM_LIMIT),
    )(cq, ckv, kr, dq_r, dk_r, dv, dxr, dxg, gqa, gkva, w_uq_p, w_uk_p, w_v, qg, kg, rc, rs1, rs2, *srcs)
    return res[:9], res[9:]


KEY_CHUNK = 4 * LANES


def _key_chunks(t):
    count = max(t // KEY_CHUNK, 1)
    first = t - KEY_CHUNK * (count - 1)
    return [(0, first)] + [(first + KEY_CHUNK * c, KEY_CHUNK) for c in range(count - 1)]


def _attn_specs(t, tq):
    nq = t // tq
    qspec = pl.BlockSpec((tq, 2 * HEAD_PAD), lambda b, hp, i: (b * nq + i, hp))
    kspec = pl.BlockSpec((t, 2 * HEAD_PAD), lambda b, hp, i: (b, hp))
    vspec = pl.BlockSpec((t, 2 * V_HEAD), lambda b, hp, i: (b, hp))
    ospec = pl.BlockSpec((tq, 2 * V_HEAD), lambda b, hp, i: (b * nq + i, hp))
    return nq, qspec, kspec, vspec, ospec


def _probs_spec(t, tq):
    return pl.BlockSpec((1, 2, tq, t), lambda b, hp, i: (b, hp, i, 0))


def _attn_fwd(q, k, v, srcs=(), scatter=()):
    n = q.shape[0]
    t = _t_pad()
    tq = t // 2
    nq, qspec, kspec, vspec, ospec = _attn_specs(t, tq)
    nk = len(srcs)
    c_in, c_out, c_shape, c_sems = _exchange_specs(srcs, scatter)

    def body(q_ref, k_ref, v_ref, *rest):
        o_ref, l_ref, p_ref = rest[nk:nk + 3]
        finish = _ride(3, *_exchange_fns(rest[:nk], rest[nk + 3:2 * nk + 3], rest[2 * nk + 3:], scatter))
        lane = lax.broadcasted_iota(jnp.int32, (tq, 2 * V_HEAD), 1)
        outs = []
        sums = []
        for j in range(2):
            sl = slice(j * HEAD_PAD, (j + 1) * HEAD_PAD)
            qh = q_ref[:, sl]

            def scores(start, size):
                s = _dot_nt(qh, k_ref[start:start + size, sl])
                if start < PAD_ROWS:
                    key = lax.broadcasted_iota(jnp.int32, (tq, size), 1) + start
                    s = jnp.where(key >= PAD_ROWS, s, -jnp.inf)
                return s

            top = functools.reduce(jnp.maximum, [jnp.max(scores(*c), axis=-1, keepdims=True) for c in _key_chunks(t)])
            l = jnp.zeros((tq, 1), F32)
            pv = jnp.zeros((tq, 2 * V_HEAD), F32)
            for start, size in _key_chunks(t):
                e = jnp.exp2((scores(start, size) - top) * (QK_HEAD ** -0.5 * math.log2(math.e)))
                l = l + jnp.sum(e, axis=-1, keepdims=True)
                e = e.astype(BF16)
                p_ref[0, j, :, start:start + size] = e
                pv = pv + _dot(e, v_ref[start:start + size, :])
            outs.append(pv / l)
            sums.append(l)
        o_ref[...] = jnp.where(lane < V_HEAD, outs[0], outs[1])
        l_ref[...] = jnp.where(lane < V_HEAD, sums[0], sums[1])
        finish()

    res = pl.pallas_call(
        body, name="attn_fwd", grid=(n // t, N_HEADS // 2, nq),
        in_specs=[qspec, kspec, vspec] + c_in, out_specs=[ospec, ospec, _probs_spec(t, tq)] + c_out,
        out_shape=[jax.ShapeDtypeStruct((n, D_ATTN), F32), jax.ShapeDtypeStruct((n, D_ATTN), F32),
                   jax.ShapeDtypeStruct((n // t, N_HEADS, t, t), BF16)] + c_shape,
        scratch_shapes=c_sems,
        compiler_params=pltpu.CompilerParams(dimension_semantics=("arbitrary", "arbitrary", "arbitrary"),
                                             vmem_limit_bytes=VMEM_LIMIT),
    )(q, k, v, *srcs)
    return res[0], (res[1], res[2]), res[3:]


def _attn_bwd(q, k, v, do, o, probs, srcs=(), scatter=()):
    n = q.shape[0]
    t = _t_pad()
    tq = t // 2
    nq, qspec, kspec, vspec, ospec = _attn_specs(t, tq)
    nk = len(srcs)
    c_in, c_out, c_shape, c_sems = _exchange_specs(srcs, scatter)

    def body(q_ref, k_ref, v_ref, do_ref, o_ref, l_ref, p_ref, *rest):
        dq_ref, dk_ref, dv_ref = rest[nk:nk + 3]
        finish = _ride(3, *_exchange_fns(rest[:nk], rest[nk + 3:2 * nk + 3], rest[2 * nk + 3:], scatter))

        @pl.when(pl.program_id(2) == 0)
        def _():
            dk_ref[...] = jnp.zeros_like(dk_ref)
            dv_ref[...] = jnp.zeros_like(dv_ref)

        lane = lax.broadcasted_iota(jnp.int32, (tq, 2 * V_HEAD), 1)
        do = do_ref[...]
        do_o = do * o_ref[...]
        chunks = _key_chunks(t)
        dvs = [None] * len(chunks)
        for j in range(2):
            sl = slice(j * HEAD_PAD, (j + 1) * HEAD_PAD)
            qh = q_ref[:, sl]
            in_head = (lane < V_HEAD) if j == 0 else (lane >= V_HEAD)
            inv_l = 1.0 / l_ref[:, j * V_HEAD:j * V_HEAD + 1]
            doh = jnp.where(in_head, do, 0.0).astype(BF16)
            doh_n = jnp.where(in_head, do * inv_l, 0.0).astype(BF16)
            delta = jnp.sum(jnp.where(in_head, do_o, 0.0), axis=-1, keepdims=True)
            row_scale = inv_l * (QK_HEAD ** -0.5)
            dq = jnp.zeros((tq, HEAD_PAD), F32)
            for c, (start, size) in enumerate(chunks):
                rows = slice(start, start + size)
                e = p_ref[0, j, :, rows]
                dp = _dot_nt(doh, v_ref[rows, :])
                ds = (e.astype(F32) * (dp - delta) * row_scale).astype(BF16)
                dq = dq + _dot(ds, k_ref[rows, sl])
                dk_ref[rows, sl] += _dot_tn(ds, qh)
                dvc = _dot_tn(e, doh_n)
                dvs[c] = dvc if dvs[c] is None else dvs[c] + dvc
            dq_ref[:, sl] = dq
        for (start, size), dvc in zip(chunks, dvs):
            dv_ref[start:start + size, :] += dvc
        finish()

    res = pl.pallas_call(
        body, name="attn_bwd", grid=(n // t, N_HEADS // 2, nq),
        in_specs=[qspec, kspec, vspec, ospec, ospec, ospec, _probs_spec(t, tq)] + c_in,
        out_specs=[qspec, kspec, vspec] + c_out,
        out_shape=[jax.ShapeDtypeStruct((n, QP_COLS), F32), jax.ShapeDtypeStruct((n, QP_COLS), F32),
                   jax.ShapeDtypeStruct((n, D_ATTN), F32)] + c_shape, scratch_shapes=c_sems,
        compiler_params=pltpu.CompilerParams(dimension_semantics=("arbitrary", "arbitrary", "arbitrary"),
                                             vmem_limit_bytes=VMEM_LIMIT),
    )(q, k, v, do, o, *probs, *srcs)
    return res[:3], res[3:]


SCAN_STEPS = 8


def _scan(chains, t):
    seg = t // 8
    rows = lax.broadcasted_iota(jnp.int32, (8, LANES), 0)

    def step(i, carry):
        carry = list(carry)
        for u in range(SCAN_STEPS):
            j = i * SCAN_STEPS + u
            for n, (a_ref, b_ref, h_ref, p_ref, reverse) in enumerate(chains):
                h, p = carry[n]
                idx = pl.ds(seg - 1 - j if reverse else j, 8, stride=seg)
                a = a_ref[idx, :]
                h = a * h + b_ref[idx, :]
                p = a * p
                h_ref[idx, :] = h
                p_ref[idx, :] = p
                carry[n] = (h, p)
        return tuple(carry)

    init = tuple((jnp.zeros((8, LANES), F32), jnp.ones((8, LANES), F32)) for _ in chains)
    ends = lax.fori_loop(0, seg // SCAN_STEPS, step, init)
    for (_, _, h_ref, p_ref, reverse), (b, a) in zip(chains, ends):
        for d in (1, 2, 4):
            if reverse:
                keep = rows < 8 - d
                a_n, b_n = pltpu.roll(a, 8 - d, 0), pltpu.roll(b, 8 - d, 0)
            else:
                keep = rows >= d
                a_n, b_n = pltpu.roll(a, d, 0), pltpu.roll(b, d, 0)
            b = a * jnp.where(keep, b_n, 0.0) + b
            a = a * jnp.where(keep, a_n, 1.0)
        for s in (range(7) if reverse else range(1, 8)):
            sl = slice(s * seg, (s + 1) * seg)
            carry_in = b[s + 1:s + 2, :] if reverse else b[s - 1:s, :]
            h_ref[sl, :] = h_ref[sl, :] + p_ref[sl, :] * carry_in


def _shift_rows(x, s, rows, t):
    if s == 0:
        return x
    rolled = pltpu.roll(x, s % t, 0)
    return jnp.where(rows >= s, rolled, 0.0) if s > 0 else jnp.where(rows < t + s, rolled, 0.0)


def _neg_expm1_twice(h, exp_2h):
    series = h * (-2.0 + h * (-2.0 + h * (-4.0 / 3 + h * (-2.0 / 3))))
    return jnp.where(h > -0.05, series, 1.0 - exp_2h)


def _sigmoid(x):
    return 0.5 * jnp.tanh(0.5 * x) + 0.5


def _gelu_parts(x):
    k = math.sqrt(2.0 / math.pi)
    th = jnp.tanh(k * (x + 0.044715 * x * x * x))
    g = 0.5 * x * (1.0 + th)
    dg = 0.5 * (1.0 + th) + 0.5 * x * (1.0 - th * th) * k * (1.0 + 3 * 0.044715 * x * x)
    return g, dg


def _lru_gates(xc, gates, lam_ref, valid, d):
    r = _sigmoid(gates[:, (2 * d) * LANES:(2 * d + 1) * LANES])
    i = _sigmoid(gates[:, (2 * d + 1) * LANES:(2 * d + 2) * LANES])
    neg_lam = -lam_ref[d:d + 1, :]
    sp = jnp.maximum(neg_lam, 0.0) + jnp.log1p(jnp.exp(-jnp.abs(neg_lam)))
    log_a = -LRU_C * r * sp
    a = jnp.exp(log_a)
    m = jnp.maximum(_neg_expm1_twice(log_a, a * a), 0.0)
    sq = jnp.sqrt(m)
    b = jnp.where(valid, sq * (i * xc), 0.0)
    return r, i, sp, a, m, sq, b


def _conv(xr, cw_ref, cb_ref, rows, t):
    return (cw_ref[0:1, :] * _shift_rows(xr, 2, rows, t) + cw_ref[1:2, :] * _shift_rows(xr, 1, rows, t)
            + cw_ref[2:3, :] * xr + cw_ref[3:4, :] * _shift_rows(xr, -1, rows, t) + cb_ref[...])


def _rnn_specs(t):
    seq = pl.BlockSpec((t, LANES), lambda cb, b: (b, cb))
    cw = pl.BlockSpec((4, LANES), lambda cb, b: (0, cb))
    vec1 = pl.BlockSpec((1, LANES), lambda cb, b: (0, cb))
    vec2 = pl.BlockSpec((2, LANES), lambda cb, b: (0, cb))
    wblk = pl.BlockSpec((1, LANES, 4 * LANES), lambda cb, b: (cb, 0, 0))
    gbias = pl.BlockSpec((1, 1, 4 * LANES), lambda cb, b: (cb, 0, 0))
    return seq, cw, vec1, vec2, wblk, gbias


def _rnn_fwd(xr, xg, conv_w, conv_b, wblk, gbias, lam):
    n = xr.shape[0]
    t = _t_pad()
    seq, cw, vec1, vec2, wspec, gspec = _rnn_specs(t)
    both = pl.BlockSpec((2, t, LANES), lambda cb, b: (0, b, cb))

    def body(xr_ref, xg_ref, cw_ref, cb_ref, w_ref, gb_ref, lam_ref,
             o_ref, xc_ref, r_ref, i_ref, q_ref, h_ref, a_s, b_s, p_s):
        rows = lax.broadcasted_iota(jnp.int32, (t, LANES), 0)
        valid = rows >= PAD_ROWS
        xc = _conv(xr_ref[...], cw_ref, cb_ref, rows, t)
        xc_ref[...] = xc
        gates = _dot(xc.astype(BF16), w_ref[0]) + gb_ref[0]
        for d in range(2):
            r_ref[d], i_ref[d], _, a_s[d], _, q_ref[d], b_s[d] = _lru_gates(xc, gates, lam_ref, valid, d)
        _scan([(a_s.at[d], b_s.at[d], h_ref.at[d], p_s.at[d], d == 1) for d in range(2)], t)
        g, _ = _gelu_parts(xg_ref[...])
        o_ref[...] = (h_ref[0] + h_ref[1]) * g

    stacked = jax.ShapeDtypeStruct((2, n, D_RNN), F32)
    res = pl.pallas_call(
        body, name="rnn_fwd", grid=(D_RNN // LANES, n // t),
        in_specs=[seq, seq, cw, vec1, wspec, gspec, vec2], out_specs=[seq, seq, both, both, both, both],
        out_shape=[jax.ShapeDtypeStruct((n, D_RNN), F32), jax.ShapeDtypeStruct((n, D_RNN), F32)] + [stacked] * 4,
        scratch_shapes=[pltpu.VMEM((2, t, LANES), F32)] * 3,
        compiler_params=pltpu.CompilerParams(dimension_semantics=("parallel", "parallel"), vmem_limit_bytes=VMEM_LIMIT),
    )(xr, xg, conv_w, conv_b, wblk, gbias, lam)
    return res[0], tuple(res[1:])


def _rnn_bwd(xr, xg, do, saved, conv_w, wblk, lam, srcs=(), scatter=()):
    n = xr.shape[0]
    t = _t_pad()
    seq, cw, vec1, vec2, wspec, gspec = _rnn_specs(t)
    nk = len(srcs)
    c_in, c_out, c_shape, c_sems = _exchange_specs(srcs, scatter)

    def body(xr_ref, xg_ref, do_ref, xc_ref, r_s, i_s, q_s, h_s, cw_ref, w_ref, lam_ref, *rest):
        dxr_ref, dxg_ref, dcw_ref, dcb_ref, dw_ref, dgb_ref, dlam_ref = rest[nk:nk + 7]
        a_s, b_s, l_s, p_s, back_s, dg_s = rest[2 * nk + 7 + len(c_sems):]
        finish = _ride(2, *_exchange_fns(rest[:nk], rest[nk + 7:2 * nk + 7], rest[2 * nk + 7:2 * nk + 7 + len(c_sems)],
                                         scatter))
        first = pl.program_id(1) == 0
        rows = lax.broadcasted_iota(jnp.int32, (t, LANES), 0)
        valid = rows >= PAD_ROWS
        xr = xr_ref[...]
        xc = xc_ref[...]
        xcb = xc.astype(BF16)
        g, dg = _gelu_parts(xg_ref[...])
        do = do_ref[...]
        dxg_ref[...] = do * (h_s[0] + h_s[1]) * dg
        b_s[...] = do * g
        sps = []
        for d in range(2):
            neg_lam = -lam_ref[d:d + 1, :]
            sps.append(jnp.maximum(neg_lam, 0.0) + jnp.log1p(jnp.exp(-jnp.abs(neg_lam))))
            a_s[d] = jnp.exp(-LRU_C * r_s[d] * sps[d])
            back_s[d] = _shift_rows(a_s[d], -1 if d == 0 else 1, rows, t)
        _scan([(back_s.at[d], b_s, l_s.at[d], p_s.at[d], d == 0) for d in range(2)], t)
        dxc = jnp.zeros((t, LANES), F32)
        dlams = []
        for d in range(2):
            r, i, sp, a, sq = r_s[d], i_s[d], sps[d], a_s[d], q_s[d]
            lam_t = l_s[d]
            da = lam_t * _shift_rows(h_s[d], 1 if d == 0 else -1, rows, t)
            lam_v = jnp.where(valid, lam_t, 0.0)
            dsq = lam_v * (i * xc)
            di = lam_v * sq * xc
            dxc = dxc + lam_v * sq * i
            dm = jnp.where(sq > 0.0, dsq * 0.5 / jnp.where(sq > 0.0, sq, 1.0), 0.0)
            dla = da * a - 2.0 * dm * a * a
            dr = dla * (-LRU_C) * sp
            dsp = _colsum(dla * (-LRU_C) * r)
            dlams.append(dsp * -jax.nn.sigmoid(-lam_ref[d:d + 1, :]))
            dg_s[:, (2 * d) * LANES:(2 * d + 1) * LANES] = (dr * r * (1.0 - r)).astype(BF16)
            dg_s[:, (2 * d + 1) * LANES:(2 * d + 2) * LANES] = (di * i * (1.0 - i)).astype(BF16)
        dgates = dg_s[...]
        dxc = dxc + _dot_nt(dgates, w_ref[0])
        taps = [_shift_rows(dxc, j - 2, rows, t) for j in range(4)]
        dxr_ref[...] = (cw_ref[0:1, :] * taps[0] + cw_ref[1:2, :] * taps[1] + cw_ref[2:3, :] * taps[2]
                        + cw_ref[3:4, :] * taps[3])
        dcw = jnp.concatenate([_colsum(tap * xr) for tap in taps], axis=0)
        _acc(dcw_ref, first, dcw)
        _acc(dcb_ref, first, _colsum(dxc))
        _acc(dw_ref, first, _dot_tn(xcb, dgates)[None])
        _acc(dgb_ref, first, _colsum(dgates.astype(F32))[None])
        _acc(dlam_ref, first, jnp.concatenate(dlams, axis=0))
        finish()

    both = pl.BlockSpec((2, t, LANES), lambda cb, b: (0, b, cb))
    pair = pltpu.VMEM((2, t, LANES), F32)
    res = pl.pallas_call(
        body, name="rnn_bwd", grid=(D_RNN // LANES, n // t),
        in_specs=[seq, seq, seq, seq, both, both, both, both, cw, wspec, vec2] + c_in,
        out_specs=[seq, seq, cw, vec1, wspec, gspec, vec2] + c_out,
        out_shape=[jax.ShapeDtypeStruct((n, D_RNN), F32), jax.ShapeDtypeStruct((n, D_RNN), F32),
                   jax.ShapeDtypeStruct((4, D_RNN), F32), jax.ShapeDtypeStruct((1, D_RNN), F32),
                   jax.ShapeDtypeStruct((D_RNN // LANES, LANES, 4 * LANES), F32),
                   jax.ShapeDtypeStruct((D_RNN // LANES, 1, 4 * LANES), F32), jax.ShapeDtypeStruct((2, D_RNN), F32)]
        + c_shape,
        scratch_shapes=c_sems + [pair, pltpu.VMEM((t, LANES), F32), pair, pair, pair, pltpu.VMEM((t, 4 * LANES), BF16)],
        compiler_params=pltpu.CompilerParams(dimension_semantics=("arbitrary", "arbitrary"), vmem_limit_bytes=VMEM_LIMIT),
    )(xr, xg, do, *saved, conv_w, wblk, lam, *srcs)
    return res[:7], res[7:]


def _post(oa, orn, h0, tgt, ga, gr, g2, w_out, w_gate, w_up, w_down):
    n = oa.shape[0]
    tm = _row_tile(n)
    t = _t_pad()
    head = PAD_ROWS + N_META
    parts = tm // head

    def body(oa_ref, or_ref, h0_ref, *rest):
        tgt_refs = rest[:parts]
        (ga_ref, gr_ref, g2_ref, wo_ref, wg_ref, wu_ref, wd_ref,
         doa_ref, dor_ref, dh1_ref, mix_ref, h1n_ref, act_ref, dgate_ref, dup_ref, dy_ref,
         loss_ref, dga_ref, dgr_ref, dg2_ref, gate_s, up_s) = rest[parts:]
        first = pl.program_id(0) == 0
        xa, ra = _rms(oa_ref[...], D_ATTN)
        xr, rr = _rms(or_ref[...], D_RNN)
        mix = jnp.concatenate([(xa * ga_ref[...]).astype(BF16), (xr * gr_ref[...]).astype(BF16)], axis=-1)
        mix_ref[...] = mix.T
        h1 = h0_ref[...] + _dot(mix, wo_ref[...])
        x2, r2 = _rms(h1, D_MODEL)
        h1n = (x2 * g2_ref[...]).astype(BF16)
        h1n_ref[...] = h1n
        y = h1
        for cs in range(0, D_FF, FF_CHUNK):
            sl = slice(cs, cs + FF_CHUNK)
            gate = _dot_nt(h1n, wg_ref[sl, :])
            up = _dot_nt(h1n, wu_ref[sl, :])
            gate_s[:, sl] = gate
            up_s[:, sl] = up
            act = (gate * _sigmoid(gate) * up).astype(BF16)
            act_ref[sl, :] = act.T
            y = y + _dot(act, wd_ref[sl, :])
        row = pl.program_id(0) * tm + lax.broadcasted_iota(jnp.int32, (tm, 1), 0)
        for _ in range(1, n // t):
            row = jnp.where(row >= t, row - t, row)
        tgt = jnp.concatenate([ref[0] for ref in tgt_refs], axis=0)
        err = jnp.where(row >= PAD_ROWS + N_META, y - tgt, 0.0)
        _acc(loss_ref, first, jnp.full((1, LANES), 0.5 / D_MODEL, F32) * jnp.sum(err * err))
        dy = err * (1.0 / D_MODEL)
        dyb = dy.astype(BF16)
        dy_ref[...] = dyb
        dh1n = jnp.zeros((tm, D_MODEL), F32)
        for cs in range(0, D_FF, FF_CHUNK):
            sl = slice(cs, cs + FF_CHUNK)
            dact = _dot_nt(dyb, wd_ref[sl, :])
            gate, up = gate_s[:, sl], up_s[:, sl]
            sg = _sigmoid(gate)
            dgate = (dact * up * sg * (1.0 + gate * (1.0 - sg))).astype(BF16)
            dup = (dact * gate * sg).astype(BF16)
            dgate_ref[sl, :] = dgate.T
            dup_ref[sl, :] = dup.T
            dh1n = dh1n + _dot(dgate, wg_ref[sl, :]) + _dot(dup, wu_ref[sl, :])
        _acc(dg2_ref, first, _colsum(dh1n * x2))
        dh1 = dy + _rms_bwd(dh1n, x2, r2, g2_ref[...], D_MODEL)
        dh1_ref[...] = dh1
        dmix = _dot_nt(dh1.astype(BF16), wo_ref[...])
        dma, dmr = dmix[:, :D_ATTN], dmix[:, D_ATTN:]
        _acc(dga_ref, first, _colsum(dma * xa))
        _acc(dgr_ref, first, _colsum(dmr * xr))
        doa_ref[...] = _rms_bwd(dma, xa, ra, ga_ref[...], D_ATTN)
        dor_ref[...] = _rms_bwd(dmr, xr, rr, gr_ref[...], D_RNN)

    def row(w):
        return pl.BlockSpec((tm, w), lambda i: (i, 0))

    def acc(w):
        return pl.BlockSpec((1, w), lambda i: (0, 0))

    def col(w):
        return pl.BlockSpec((w, tm), lambda i: (0, i))

    outs = [(D_ATTN, F32, row), (D_RNN, F32, row), (D_MODEL, F32, row), (D_MODEL, BF16, col), (D_MODEL, BF16, row),
            (D_FF, BF16, col), (D_FF, BF16, col), (D_FF, BF16, col), (D_MODEL, BF16, row)]
    accs = [LANES, D_ATTN, D_RNN, D_MODEL]
    per = t // head

    def target_part(p):
        def index(i):
            block = i * parts + p
            return block // per, jnp.maximum(block % per - 1, 0), 0
        return pl.BlockSpec((1, head, D_MODEL), index)

    return pl.pallas_call(
        body, name="post", grid=(n // tm,),
        in_specs=[row(D_ATTN), row(D_RNN), row(D_MODEL)] + [target_part(p) for p in range(parts)] + [
                  _const_spec((1, D_ATTN)), _const_spec((1, D_RNN)), _const_spec((1, D_MODEL)),
                  _const_spec((D_MODEL, D_MODEL)), _const_spec((D_FF, D_MODEL)), _const_spec((D_FF, D_MODEL)),
                  _const_spec((D_FF, D_MODEL))],
        out_specs=[spec(w) for w, _, spec in outs] + [acc(w) for w in accs],
        out_shape=[jax.ShapeDtypeStruct((n, w) if spec is row else (w, n), dt) for w, dt, spec in outs]
        + [jax.ShapeDtypeStruct((1, w), F32) for w in accs],
        scratch_shapes=[pltpu.VMEM((tm, D_FF), F32), pltpu.VMEM((tm, D_FF), F32)],
        compiler_params=pltpu.CompilerParams(dimension_semantics=("arbitrary",), vmem_limit_bytes=VMEM_LIMIT),
    )(oa, orn, h0, *[tgt] * parts, ga, gr, g2, w_out, w_gate, w_up, w_down)


def _in_bwd(dp, h0, dh1, ln1_g, w_in_p, srcs=(), scatter=()):
    n = h0.shape[0]
    tm = _row_tile(n)
    nk = len(srcs)
    c_in, c_out, c_shape, c_sems = _exchange_specs(srcs, scatter)

    def body(dp_ref, h0_ref, dh1_ref, g_ref, w_ref, *rest):
        dh0_ref, dg_ref = rest[nk:nk + 2]
        finish = _ride(1, *_exchange_fns(rest[:nk], rest[nk + 2:2 * nk + 2], rest[2 * nk + 2:], scatter))
        dhn = _dot(dp_ref[...], w_ref[...])
        xhat, r = _rms(h0_ref[...], D_MODEL)
        _acc(dg_ref, pl.program_id(0) == 0, _colsum(dhn * xhat))
        dh0_ref[...] = dh1_ref[...] + _rms_bwd(dhn, xhat, r, g_ref[...], D_MODEL)
        finish()

    def row(w):
        return pl.BlockSpec((tm, w), lambda i: (i, 0))

    res = pl.pallas_call(
        body, name="in_bwd", grid=(n // tm,),
        in_specs=[row(P_COLS), row(D_MODEL), row(D_MODEL), _const_spec((1, D_MODEL)), _const_spec((P_COLS, D_MODEL))] + c_in,
        out_specs=[row(D_MODEL), pl.BlockSpec((1, D_MODEL), lambda i: (0, 0))] + c_out,
        out_shape=[jax.ShapeDtypeStruct((n, D_MODEL), F32), jax.ShapeDtypeStruct((1, D_MODEL), F32)] + c_shape,
        scratch_shapes=c_sems,
        compiler_params=pltpu.CompilerParams(dimension_semantics=("arbitrary",), vmem_limit_bytes=VMEM_LIMIT),
    )(dp, h0, dh1, ln1_g, w_in_p, *srcs)
    return res[:2], res[2:]


MAX_TILE = D_FF // 2


def _pick_tile(width, cap):
    best = LANES
    for mult in range(1, width // LANES + 1):
        cand = mult * LANES
        if width % cand == 0 and cand <= cap:
            best = cand
    return best


def _matmul_tn(name, a, b, srcs=(), scatter=()):
    n, ka = a.shape
    kb = b.shape[1]
    ta, tb = _pick_tile(ka, MAX_TILE), _pick_tile(kb, MAX_TILE)
    tk = n // 2
    nk = len(srcs)
    c_in, c_out, c_shape, c_sems = _exchange_specs(srcs, scatter)

    def body(a_ref, b_ref, *rest):
        o_ref = rest[nk]
        finish = _ride(3, *_exchange_fns(rest[:nk], rest[nk + 1:2 * nk + 1], rest[2 * nk + 1:], scatter))
        _acc(o_ref, pl.program_id(2) == 0, _dot_tn(a_ref[...].astype(BF16), b_ref[...].astype(BF16)))
        finish()

    res = pl.pallas_call(
        body, name=name, grid=(ka // ta, kb // tb, n // tk),
        in_specs=[pl.BlockSpec((tk, ta), lambda i, j, k: (k, i)), pl.BlockSpec((tk, tb), lambda i, j, k: (k, j))] + c_in,
        out_specs=[pl.BlockSpec((ta, tb), lambda i, j, k: (i, j))] + c_out,
        out_shape=[jax.ShapeDtypeStruct((ka, kb), F32)] + c_shape, scratch_shapes=c_sems,
        compiler_params=pltpu.CompilerParams(dimension_semantics=("arbitrary", "arbitrary", "arbitrary"),
                                             vmem_limit_bytes=VMEM_LIMIT),
    )(a, b, *srcs)
    return res[0], res[1:]


def _matmul_shards(name, at, b):
    ka, n = at.shape
    kb = b.shape[1]
    width = ka // N_DEV
    per = 2 if 2 * width >= 4 * LANES else 4
    ta = per * width

    def body(a_ref, b_ref, o_ref):
        out = _dot(a_ref[...], b_ref[...].astype(BF16))
        for s in range(per):
            o_ref[s] = out[s * width:(s + 1) * width, :].astype(BF16)

    return pl.pallas_call(
        body, name=name, grid=(ka // ta,),
        in_specs=[pl.BlockSpec((ta, n), lambda i: (i, 0)), _const_spec((n, kb))],
        out_specs=pl.BlockSpec((per, width, kb), lambda i: (i, 0, 0)),
        out_shape=jax.ShapeDtypeStruct((N_DEV, width, kb), BF16),
        compiler_params=pltpu.CompilerParams(dimension_semantics=("parallel",), vmem_limit_bytes=VMEM_LIMIT),
    )(at, b)


def _adamw_math(g8_ref, w_ref, m_ref, v_ref, g_ref, d_ref, nm_ref, nv_ref):
    g = g8_ref[0].astype(F32)
    for s in range(1, N_DEV):
        g = g + g8_ref[s].astype(F32)
    g_ref[...] = g
    nm = ADAM_B1 * m_ref[...] + (1.0 - ADAM_B1) * g
    nv = ADAM_B2 * v_ref[...] + (1.0 - ADAM_B2) * (g * g)
    nm_ref[...] = nm
    nv_ref[...] = nv
    m_hat = nm / (1.0 - ADAM_B1 ** ADAM_STEP)
    v_hat = nv / (1.0 - ADAM_B2 ** ADAM_STEP)
    d_ref[...] = -ADAM_LR * (m_hat / (jnp.sqrt(v_hat) + ADAM_EPS) + ADAM_WD * w_ref[...])


def _adamw_many(name, items):
    count = len(items)

    def body(*refs):
        ins, outs = refs[:4 * count], refs[4 * count:]
        for i in range(count):
            _adamw_math(*ins[4 * i:4 * i + 4], *outs[4 * i:4 * i + 4])

    flat = [a for item in items for a in item]
    res = pl.pallas_call(
        body, name=name,
        out_shape=[jax.ShapeDtypeStruct(item[1].shape, F32) for item in items for _ in range(4)],
        compiler_params=pltpu.CompilerParams(vmem_limit_bytes=VMEM_LIMIT),
    )(*flat)
    return [tuple(res[4 * i:4 * i + 4]) for i in range(count)]


def _adamw(name, g8, w, m, v):
    rows, cols = w.shape
    tr = rows
    for cand in (256, 176, 128, 64):
        if rows % cand == 0 and rows > cand:
            tr = cand
            break

    def body(*refs):
        _adamw_math(*refs)

    blk = pl.BlockSpec((tr, cols), lambda i: (i, 0))
    return pl.pallas_call(
        body, name=name, grid=(rows // tr,),
        in_specs=[pl.BlockSpec((N_DEV, tr, cols), lambda i: (0, i, 0)), blk, blk, blk],
        out_specs=[blk] * 4, out_shape=[jax.ShapeDtypeStruct((rows, cols), F32)] * 4,
        compiler_params=pltpu.CompilerParams(dimension_semantics=("parallel",), vmem_limit_bytes=VMEM_LIMIT),
    )(g8, w, m, v)


def _exchange_specs(srcs, scatter):
    nk = len(srcs)
    if not nk:
        return [], [], [], []
    any_spec = pl.BlockSpec(memory_space=pl.ANY)
    out_shape = [jax.ShapeDtypeStruct(s.shape if sc else (N_DEV,) + s.shape, s.dtype) for s, sc in zip(srcs, scatter)]
    sems = [pltpu.SemaphoreType.DMA((nk, N_DEV - 1)), pltpu.SemaphoreType.DMA((nk, N_DEV - 1)),
            pltpu.SemaphoreType.DMA((nk,))]
    return [any_spec] * nk, [any_spec] * nk, out_shape, sems


FLIPS = ((0, 0, 1), (1, 0, 0), (0, 1, 0), (1, 1, 0), (1, 0, 1), (0, 1, 1), (1, 1, 1))
N_CHIP_PEERS = 3


def _exchange_fns(src_refs, out_refs, sems, scatter):
    nk = len(src_refs)
    if not nk:
        return (lambda: None), (lambda: None), (lambda: None)
    send_sems, recv_sems, local_sems = sems
    first = 1 + N_CHIP_PEERS

    def plan():
        x, y, c = lax.axis_index("x"), lax.axis_index("y"), lax.axis_index("c")
        me = 4 * x + 2 * y + c
        peers = [(1 - x if fx else x, 1 - y if fy else y, 1 - c if fc else c) for fx, fy, fc in FLIPS]
        pids = [4 * px + 2 * py + pc for px, py, pc in peers]

        def remote(k, j, src, dst, to):
            return pltpu.make_async_remote_copy(src_ref=src, dst_ref=dst, send_sem=send_sems.at[k, j],
                                                recv_sem=recv_sems.at[k, j], device_id=to, device_id_type=MESH)

        def mine(k, dest):
            return src_refs[k].at[dest] if scatter[k] else src_refs[k]

        local = [pltpu.make_async_copy(mine(k, me), out_refs[k].at[me], local_sems.at[k]) for k in range(nk)]
        direct = [remote(k, j, mine(k, pids[j]), out_refs[k].at[me], peers[j])
                  for k in range(nk) for j in range(len(FLIPS) if scatter[k] else first)]
        relays = {(k, j): remote(k, j, out_refs[k].at[pids[j - N_CHIP_PEERS]], out_refs[k].at[pids[j - N_CHIP_PEERS]], peers[0])
                  for k in range(nk) if not scatter[k] for j in range(first, len(FLIPS))}
        arrivals = {(k, j): remote(k, j, out_refs[k].at[pids[j]], out_refs[k].at[pids[j]], peers[j])
                    for k in range(nk) for j in range(len(FLIPS))}
        return local, direct, relays, arrivals

    def start():
        local, direct, _, _ = plan()
        for cp in local + direct:
            cp.start()

    def relay():
        _, _, relays, arrivals = plan()
        for (k, j), cp in relays.items():
            arrivals[k, j - N_CHIP_PEERS].wait_recv()
            cp.start()

    def wait():
        local, direct, relays, arrivals = plan()
        for (k, j), cp in arrivals.items():
            if (k, j + N_CHIP_PEERS) not in relays:
                cp.wait_recv()
        for cp in direct + list(relays.values()):
            cp.wait_send()
        for cp in local:
            cp.wait()

    return start, relay, wait


def _grid_step(rank):
    step, total = 0, 1
    for axis in range(rank):
        step = step * pl.num_programs(axis) + pl.program_id(axis)
        total = total * pl.num_programs(axis)
    return step, total


def _ride(rank, start, relay, wait):
    step, total = _grid_step(rank)
    pl.when(step == 0)(start)
    pl.when(step == (3 * total) // 4)(relay)
    return lambda: pl.when(step == total - 1)(wait)


def _exchange(name, srcs, scatter):
    nk = len(srcs)
    c_in, c_out, c_shape, c_sems = _exchange_specs(srcs, scatter)

    def body(*refs):
        start, relay, wait = _exchange_fns(refs[:nk], refs[nk:2 * nk], refs[2 * nk:], scatter)
        start()
        relay()
        wait()

    return pl.pallas_call(body, name=name, in_specs=c_in, out_specs=c_out, out_shape=c_shape, scratch_shapes=c_sems)(*srcs)


def _cols_from_shards(g):
    return jnp.transpose(g, (1, 0, 2)).reshape(g.shape[1], -1)


def _cols_to_shards(w):
    return jnp.transpose(w.reshape(w.shape[0], N_DEV, -1), (1, 0, 2))


def _prep(x, srcs, scatter):
    nb = x.shape[0]
    t = _t_pad()
    head = PAD_ROWS + N_META
    nk = len(srcs)
    c_in, c_out, c_shape, c_sems = _exchange_specs(srcs, scatter)

    def body(x_ref, *rest):
        h0_ref = rest[nk]
        finish = _ride(1, *_exchange_fns(rest[:nk], rest[nk + 1:2 * nk + 1], rest[2 * nk + 1:], scatter))
        lead = pl.program_id(0) == 0

        @pl.when(lead)
        def _():
            h0_ref[...] = jnp.zeros_like(h0_ref)

        @pl.when(jnp.logical_not(lead))
        def _():
            h0_ref[...] = x_ref[...]

        finish()

    src = pl.BlockSpec((nb, head, D_MODEL), lambda j: (0, jnp.maximum(j - 1, 0), 0))
    dst = pl.BlockSpec((nb, head, D_MODEL), lambda j: (0, j, 0))
    res = pl.pallas_call(
        body, name="prep", grid=(t // head,), in_specs=[src] + c_in, out_specs=[dst] + c_out,
        out_shape=[jax.ShapeDtypeStruct((nb, t, D_MODEL), F32)] + c_shape, scratch_shapes=c_sems,
        compiler_params=pltpu.CompilerParams(dimension_semantics=("arbitrary",)),
    )(x, *srcs)
    return res[0], res[1:]


def _rope_tables(n):
    t = _t_pad()
    pos = np.arange(t, dtype=np.float32) - np.float32(PAD_ROWS)
    half = QK_ROPE // 2
    freqs = (1.0 / (ROPE_THETA ** (np.arange(half, dtype=np.float32) / half))).astype(np.float32)
    ang = pos[:, None] * freqs[None, :]
    cos, sin = np.cos(ang), np.sin(ang)
    z = lambda w: np.zeros((t, w), np.float32)
    c = np.concatenate([np.ones((t, QK_NOPE), np.float32), cos, cos, z(HEAD_PAD - QK_HEAD)], axis=1)
    s1 = np.concatenate([z(QK_NOPE + half), sin, z(HEAD_PAD - QK_HEAD)], axis=1)
    s2 = np.concatenate([z(QK_NOPE), -sin, z(HEAD_PAD - QK_NOPE - half)], axis=1)
    return tuple(jnp.asarray(np.tile(a, (n // t, 1))) for a in (c, s1, s2))


def _block_diag_gates(lru_wa, lru_wi):
    eye = jnp.eye(2, dtype=lru_wa.dtype)

    def bd(w):
        w = w.reshape(2, D_RNN // LANES, 2, RNN_BW, RNN_BW)
        full = w[:, :, :, :, None, :] * eye[None, None, :, None, :, None]
        return full.reshape(2, D_RNN // LANES, LANES, LANES)

    a, i = bd(lru_wa), bd(lru_wi)
    return jnp.concatenate([a[0], i[0], a[1], i[1]], axis=-1)


def _unblock_gates(dw):
    nb = D_RNN // LANES
    parts = dw.reshape(nb, 2, RNN_BW, 4, 2, RNN_BW)
    diag = jnp.stack([parts[:, k, :, :, k, :] for k in range(2)], axis=1)
    diag = jnp.transpose(diag, (3, 0, 1, 2, 4)).reshape(4, 2 * nb, RNN_BW, RNN_BW)
    return jnp.stack([diag[0], diag[2]]), jnp.stack([diag[1], diag[3]])


WEIGHTS = ("meta_tokens", "ln1_g", "w_in", "q_a_norm_g", "w_uq", "kv_a_norm_g", "w_ukv", "q_norm_g", "k_norm_g",
           "conv_w", "conv_b", "lru_wa", "lru_ba", "lru_wi", "lru_bi", "lru_lambda", "attn_out_g", "rnn_out_g",
           "w_out", "ln2_g", "w_gate", "w_up", "w_down")
BIG = ("w_in", "w_uq", "w_ukv", "w_out", "w_gate", "w_up", "w_down")
TRANSPOSED = ("w_in", "w_uq", "w_gate", "w_up")
ROW_SHARDED = ("w_out", "w_down") + TRANSPOSED
REPLICATED = ("ln1_g", "q_a_norm_g", "kv_a_norm_g", "q_norm_g", "k_norm_g", "conv_b", "lru_wa", "lru_wi",
              "attn_out_g", "rnn_out_g", "ln2_g")
WHOLE = REPLICATED + ("loss",)
G_FIRST = ("w_in", "meta_tokens")
G_MID = ("w_uq", "w_ukv", "conv_w", "lru_ba", "lru_bi", "lru_lambda")
LATE = ("w_out", "w_gate", "w_up", "w_down")
G_LAST = ("meta_tokens", "ln1_g")


def _local_step(x, tgt, ex):
    nb = x.shape[0]
    t = _t_pad()
    n = nb * t
    local = ex.local
    h0, got = _prep(x, *ex.gather_srcs(G_FIRST))
    first = ex.gathered(G_FIRST, got)
    meta, w_in = first["meta_tokens"], first["w_in"]
    h0 = h0.at[:, PAD_ROWS:PAD_ROWS + N_META].set(jnp.broadcast_to(meta[None], (nb, N_META, D_MODEL))).reshape(n, D_MODEL)

    zr = lambda r: jnp.zeros((r, D_MODEL), w_in.dtype)
    w_in_p = jnp.concatenate([w_in[:OFF_CKV], w_in[OFF_KR:], zr(QK_NOPE), w_in[OFF_CKV:OFF_KR], zr(HEAD_PAD - QK_HEAD)],
                             axis=0)
    pad_g = lambda g: jnp.pad(g, ((0, 0), (0, HEAD_PAD - QK_HEAD)))
    qg, kg = pad_g(local["q_norm_g"]), pad_g(local["k_norm_g"])
    rc, rs1, rs2 = _rope_tables(n)
    wblk = _block_diag_gates(local["lru_wa"].reshape(2, -1, RNN_BW, RNN_BW),
                             local["lru_wi"].reshape(2, -1, RNN_BW, RNN_BW)).astype(BF16)
    nblk = D_RNN // LANES

    (hn, cq, ckv, xr, xg, kr), got = _in_proj(h0, local["ln1_g"], w_in_p, *ex.gather_srcs(G_MID))
    w = ex.gathered(G_MID, got)
    w_uq_p = jnp.pad(w["w_uq"].reshape(N_HEADS, QK_HEAD, Q_LORA), ((0, 0), (0, HEAD_PAD - QK_HEAD), (0, 0))
                     ).reshape(QP_COLS, Q_LORA)
    ukv = w["w_ukv"].reshape(KV_LORA, N_HEADS, QK_NOPE + V_HEAD)
    w_uk_p = jnp.pad(ukv[:, :, :QK_NOPE], ((0, 0), (0, 0), (0, HEAD_PAD - QK_NOPE))).reshape(KV_LORA, QP_COLS)
    w_v = ukv[:, :, QK_NOPE:].reshape(KV_LORA, D_ATTN)
    gbias = jnp.stack([w["lru_ba"][0], w["lru_bi"][0], w["lru_ba"][1], w["lru_bi"][1]], axis=0)
    gbias = jnp.transpose(gbias.reshape(4, nblk, LANES), (1, 0, 2)).reshape(nblk, 1, 4 * LANES)

    q, k, v = _qkv_fwd(cq, ckv, kr, local["q_a_norm_g"], local["kv_a_norm_g"], w_uq_p, w_uk_p, w_v, qg, kg, rc, rs1, rs2)
    oa, probs, got = _attn_fwd(q, k, v, *ex.gather_srcs(LATE))
    late = ex.gathered(LATE, got)
    orn, rnn_saved = _rnn_fwd(xr, xg, w["conv_w"], local["conv_b"], wblk, gbias, w["lru_lambda"])
    (doa, dor, dh1, mix_t, h1n, act_t, dgate_t, dup_t, dyb, loss, dga, dgr, dg2) = _post(
        oa, orn, h0, tgt, local["attn_out_g"], local["rnn_out_g"], local["ln2_g"], late["w_out"], late["w_gate"],
        late["w_up"], late["w_down"])
    wire = {"w_out": _matmul_shards("dw_out", mix_t, dh1), "w_gate": _matmul_shards("dw_gate", dgate_t, h1n),
            "w_up": _matmul_shards("dw_up", dup_t, h1n), "w_down": _matmul_shards("dw_down", act_t, dyb)}
    names = ("w_out", "w_gate")
    (dxr, dxg, dcw, dcb, dwblk, dgb, dlam), got = _rnn_bwd(xr, xg, dor, rnn_saved, w["conv_w"], wblk, w["lru_lambda"],
                                                           *ex.scatter_srcs(names, wire))
    summed = ex.scattered(names, wire, got)
    dwa, dwi = _unblock_gates(dwblk)
    dgb = jnp.transpose(dgb.reshape(nblk, 4, LANES), (1, 0, 2)).reshape(4, D_RNN)
    names = ("w_up", "w_down")
    (dq_r, dk_r, dv), got = _attn_bwd(q, k, v, doa, oa, probs, *ex.scatter_srcs(names, wire))
    summed.update(ex.scattered(names, wire, got))
    wire = ex.to_wire({
        "conv_w": dcw, "conv_b": dcb, "lru_wa": dwa.reshape(-1, RNN_BW), "lru_ba": jnp.stack([dgb[0], dgb[2]]),
        "lru_wi": dwi.reshape(-1, RNN_BW), "lru_bi": jnp.stack([dgb[1], dgb[3]]), "lru_lambda": dlam,
        "attn_out_g": dga, "rnn_out_g": dgr, "ln2_g": dg2, "loss": loss})
    names = tuple(wire)
    (dp, qa, kva, dqp, dkv, dqg, dkg, dgqa, dgkva), got = _qkv_bwd(
        cq, ckv, kr, dq_r, dk_r, dv, dxr, dxg, local["q_a_norm_g"], local["kv_a_norm_g"], w_uq_p, w_uk_p, w_v, qg, kg,
        rc, rs1, rs2, *ex.scatter_srcs(names, wire))
    summed.update(ex.scattered(names, wire, got))
    dw_uq_p, _ = _matmul_tn("dw_uq", dqp, qa)
    dw_kv, _ = _matmul_tn("dw_ukv", kva, dkv)
    dw_uq = dw_uq_p.reshape(N_HEADS, HEAD_PAD, Q_LORA)[:, :QK_HEAD].reshape(N_HEADS * QK_HEAD, Q_LORA)
    dw_ukv = jnp.concatenate([dw_kv[:, :QP_COLS].reshape(KV_LORA, N_HEADS, HEAD_PAD)[:, :, :QK_NOPE],
                              dw_kv[:, QP_COLS:].reshape(KV_LORA, N_HEADS, V_HEAD)], axis=2).reshape(KV_LORA, -1)
    wire = ex.to_wire({"q_a_norm_g": dgqa, "w_uq": dw_uq, "kv_a_norm_g": dgkva, "w_ukv": dw_ukv,
                       "q_norm_g": dqg[:, :QK_HEAD], "k_norm_g": dkg[:, :QK_HEAD]})
    names = tuple(wire)
    dw_in_p, got = _matmul_tn("dw_in", dp, hn, *ex.scatter_srcs(names, wire))
    summed.update(ex.scattered(names, wire, got))
    kr0 = OFF_CKV + 2 * D_RNN + QK_NOPE
    dw_in = jnp.concatenate([dw_in_p[:OFF_CKV], dw_in_p[kr0:kr0 + QK_ROPE], dw_in_p[OFF_CKV:OFF_CKV + 2 * D_RNN]], axis=0)
    wire = ex.to_wire({"w_in": dw_in})
    (dh0, dg1), got = _in_bwd(dp, h0, dh1, local["ln1_g"], w_in_p, *ex.scatter_srcs(("w_in",), wire))
    summed.update(ex.scattered(("w_in",), wire, got))

    dh0 = dh0.reshape(nb, t, D_MODEL)
    wire = ex.to_wire({"meta_tokens": jnp.sum(dh0[:, PAD_ROWS:PAD_ROWS + N_META], axis=0), "ln1_g": dg1})
    got = ex.run("reduce_last", *ex.scatter_srcs(G_LAST, wire))
    summed.update(ex.scattered(G_LAST, wire, got))
    return dh0[:, PAD_ROWS + N_META:], summed


class _MeshExchange:
    def __init__(self, shards):
        self.local = shards

    @staticmethod
    def run(name, srcs, scatter):
        return _exchange(name, srcs, scatter)

    def gather_srcs(self, names):
        return [self.local[k].astype(BF16) if k in BIG else self.local[k] for k in names], [False] * len(names)

    @staticmethod
    def gathered(names, outs):
        return {k: g.reshape(-1, g.shape[-1]) if k in ROW_SHARDED else _cols_from_shards(g) for k, g in zip(names, outs)}

    @staticmethod
    def to_wire(grads):
        wire = {}
        for k, g in grads.items():
            if k in WHOLE:
                wire[k] = g
            elif k in ROW_SHARDED:
                wire[k] = g.reshape(N_DEV, -1, g.shape[-1]).astype(BF16)
            else:
                wire[k] = _cols_to_shards(g).astype(BF16) if k in BIG else _cols_to_shards(g)
        return wire

    @staticmethod
    def scatter_srcs(names, wire):
        return [wire[k] for k in names], [k not in WHOLE for k in names]

    @staticmethod
    def scattered(names, wire, outs):
        return dict(zip(names, outs))


def kernel(x, meta_tokens, ln1_g, w_in, q_a_norm_g, w_uq, kv_a_norm_g, w_ukv, q_norm_g, k_norm_g, conv_w, conv_b, lru_wa, lru_ba, lru_wi, lru_bi, lru_lambda, attn_out_g, rnn_out_g, w_out, ln2_g, w_gate, w_up, w_down, loss_target, m_meta_tokens, m_ln1_g, m_w_in, m_q_a_norm_g, m_w_uq, m_kv_a_norm_g, m_w_ukv, m_q_norm_g, m_k_norm_g, m_conv_w, m_conv_b, m_lru_wa, m_lru_ba, m_lru_wi, m_lru_bi, m_lru_lambda, m_attn_out_g, m_rnn_out_g, m_w_out, m_ln2_g, m_w_gate, m_w_up, m_w_down, v_meta_tokens, v_ln1_g, v_w_in, v_q_a_norm_g, v_w_uq, v_kv_a_norm_g, v_w_ukv, v_q_norm_g, v_k_norm_g, v_conv_w, v_conv_b, v_lru_wa, v_lru_ba, v_lru_wi, v_lru_bi, v_lru_lambda, v_attn_out_g, v_rnn_out_g, v_w_out, v_ln2_g, v_w_gate, v_w_up, v_w_down):
    given = (meta_tokens, ln1_g, w_in, q_a_norm_g, w_uq, kv_a_norm_g, w_ukv, q_norm_g, k_norm_g, conv_w, conv_b,
             lru_wa, lru_ba, lru_wi, lru_bi, lru_lambda, attn_out_g, rnn_out_g, w_out, ln2_g, w_gate, w_up, w_down)
    moments_m = (m_meta_tokens, m_ln1_g, m_w_in, m_q_a_norm_g, m_w_uq, m_kv_a_norm_g, m_w_ukv, m_q_norm_g, m_k_norm_g,
                 m_conv_w, m_conv_b, m_lru_wa, m_lru_ba, m_lru_wi, m_lru_bi, m_lru_lambda, m_attn_out_g, m_rnn_out_g,
                 m_w_out, m_ln2_g, m_w_gate, m_w_up, m_w_down)
    moments_v = (v_meta_tokens, v_ln1_g, v_w_in, v_q_a_norm_g, v_w_uq, v_kv_a_norm_g, v_w_ukv, v_q_norm_g, v_k_norm_g,
                 v_conv_w, v_conv_b, v_lru_wa, v_lru_ba, v_lru_wi, v_lru_bi, v_lru_lambda, v_attn_out_g, v_rnn_out_g,
                 v_w_out, v_ln2_g, v_w_gate, v_w_up, v_w_down)
    shapes = {k: a.shape for k, a in zip(WEIGHTS, given)}

    def two_d(k, a):
        a = a.reshape(-1, a.shape[-1])
        return a.T if k in TRANSPOSED else a

    w = {k: two_d(k, a) for k, a in zip(WEIGHTS, given)}
    m = {k: two_d(k, a) for k, a in zip(WEIGHTS, moments_m)}
    v = {k: two_d(k, a) for k, a in zip(WEIGHTS, moments_v)}

    grad_x, parts = _local_step(x, loss_target, _MeshExchange(w))

    tiled = ("w_in", "w_gate", "w_up", "w_down")
    new = {k: _adamw("adamw_" + k, parts[k], w[k], m[k], v[k]) for k in tiled}
    small = [k for k in WEIGHTS if k not in tiled]
    new.update(zip(small, _adamw_many("adamw_small", [(parts[k], w[k], m[k], v[k]) for k in small])))

    loss = jnp.sum(parts["loss"][:, 0, 0])
    outs = [loss, grad_x]
    for idx in range(4):
        outs += [(new[k][idx].T if k in TRANSPOSED else new[k][idx]).reshape(shapes[k]) for k in WEIGHTS]
    return tuple(outs)
```

```python
import functools
import math

import numpy as np
import jax
import jax.numpy as jnp
from jax import lax
from jax.experimental import pallas as pl
from jax.experimental.pallas import tpu as pltpu

F32 = jnp.float32
BF16 = jnp.bfloat16

D_MODEL = 1024
N_META = 16
SEQ = 2048
N_HEADS = 8
QK_NOPE = 64
QK_ROPE = 32
QK_HEAD = QK_NOPE + QK_ROPE
V_HEAD = 64
D_ATTN = N_HEADS * V_HEAD
Q_LORA = 384
KV_LORA = 256
D_RNN = 512
RNN_BW = 64
D_FF = 2816
EPS = 1e-6
LRU_C = 8.0
ROPE_THETA = 10000.0
OFF_CKV = Q_LORA + KV_LORA
OFF_KR = OFF_CKV + QK_ROPE
IN_COLS = OFF_KR + 2 * D_RNN

ADAM_LR = 0.001
ADAM_B1 = 0.9
ADAM_B2 = 0.999
ADAM_EPS = 1e-08
ADAM_WD = 0.01
ADAM_STEP = 10

N_DEV = 8
LANES = 128
HEAD_PAD = LANES
PAD_ROWS = LANES - N_META
QP_COLS = N_HEADS * HEAD_PAD
P_COLS = OFF_CKV + 2 * D_RNN + LANES
FF_CHUNK = D_FF
VMEM_LIMIT = 56 * 1024 * 1024
MESH = pl.DeviceIdType.MESH


def _t_pad():
    return PAD_ROWS + N_META + SEQ


def _row_tile(n):
    return 256 if n % 256 == 0 else 128


def _wide_row_tile(n):
    quarter = _t_pad() // 4
    return quarter if quarter % 16 == 0 and n % quarter == 0 else _row_tile(n)


def _const_spec(shape):
    nd = len(shape)
    return pl.BlockSpec(shape, lambda *_: (0,) * nd, pipeline_mode=pl.Buffered(1))


def _rms(x, d):
    r = lax.rsqrt(jnp.sum(x * x, axis=-1, keepdims=True) * (1.0 / d) + EPS)
    return x * r, r


def _rms_bwd(dy, xhat, r, g, d):
    dxh = dy * g
    return r * (dxh - xhat * (jnp.sum(dxh * xhat, axis=-1, keepdims=True) * (1.0 / d)))


def _colsum(x):
    return jnp.sum(x, axis=0, keepdims=True)


def _dot(a, b):
    return jnp.dot(a, b, preferred_element_type=F32)


def _dot_nt(a, b):
    return lax.dot_general(a, b, (((1,), (1,)), ((), ())), preferred_element_type=F32)


def _dot_tn(a, b):
    return lax.dot_general(a, b, (((0,), (0,)), ((), ())), preferred_element_type=F32)


def _rope(x, c, s1, s2):
    return x * c + pltpu.roll(x, 16, 1) * s1 + pltpu.roll(x, HEAD_PAD - 16, 1) * s2


def _rope_bwd(dy, c, s1, s2):
    return dy * c + pltpu.roll(dy * s1, HEAD_PAD - 16, 1) + pltpu.roll(dy * s2, 16, 1)


def _acc(ref, first, val):
    @pl.when(first)
    def _():
        ref[...] = val

    @pl.when(jnp.logical_not(first))
    def _():
        ref[...] += val


def _in_proj(h0, ln1_g, w_in_p, srcs=(), scatter=()):
    n = h0.shape[0]
    tm = _wide_row_tile(n)
    nk = len(srcs)
    c_in, c_out, c_shape, c_sems = _exchange_specs(srcs, scatter)

    def body(h_ref, g_ref, w_ref, *rest):
        hn_ref, cq_ref, ckv_ref, xr_ref, xg_ref, kr_ref = rest[nk:nk + 6]
        finish = _ride(1, *_exchange_fns(rest[:nk], rest[nk + 6:2 * nk + 6], rest[2 * nk + 6:], scatter))
        xhat, _ = _rms(h_ref[...], D_MODEL)
        hn = (xhat * g_ref[...]).astype(BF16)
        hn_ref[...] = hn
        p = _dot_nt(hn, w_ref[...])
        cq_ref[...] = p[:, :Q_LORA]
        ckv_ref[...] = p[:, Q_LORA:OFF_CKV]
        xr_ref[...] = p[:, OFF_CKV:OFF_CKV + D_RNN]
        xg_ref[...] = p[:, OFF_CKV + D_RNN:OFF_CKV + 2 * D_RNN]
        kr_ref[...] = p[:, OFF_CKV + 2 * D_RNN:]
        finish()

    def row(w):
        return pl.BlockSpec((tm, w), lambda i: (i, 0))

    widths = (D_MODEL, Q_LORA, KV_LORA, D_RNN, D_RNN, LANES)
    res = pl.pallas_call(
        body, name="in_proj", grid=(n // tm,),
        in_specs=[row(D_MODEL), _const_spec((1, D_MODEL)), _const_spec((P_COLS, D_MODEL))] + c_in,
        out_specs=[row(w) for w in widths] + c_out,
        out_shape=[jax.ShapeDtypeStruct((n, w), BF16 if k == 0 else F32) for k, w in enumerate(widths)] + c_shape,
        scratch_shapes=c_sems,
        compiler_params=pltpu.CompilerParams(dimension_semantics=("arbitrary",), vmem_limit_bytes=VMEM_LIMIT),
    )(h0, ln1_g, w_in_p, *srcs)
    return res[:6], res[6:]


def _qkv_fwd(cq, ckv, kr, gqa, gkva, w_uq_p, w_uk_p, w_v, qg, kg, rc, rs1, rs2):
    n = cq.shape[0]
    tm = _wide_row_tile(n)

    def body(cq_ref, ckv_ref, kr_ref, gqa_ref, gkva_ref, wuq_ref, wuk_ref, wv_ref, qg_ref, kg_ref,
             c_ref, s1_ref, s2_ref, q_ref, k_ref, v_ref):
        xq, _ = _rms(cq_ref[...], Q_LORA)
        qa = (xq * gqa_ref[...]).astype(BF16)
        q = _dot_nt(qa, wuq_ref[...])
        xkv, _ = _rms(ckv_ref[...], KV_LORA)
        kva = (xkv * gkva_ref[...]).astype(BF16)
        kn = _dot(kva, wuk_ref[...])
        v_ref[...] = _dot(kva, wv_ref[...]).astype(BF16)
        krp = kr_ref[...]
        c, s1, s2 = c_ref[...], s1_ref[...], s2_ref[...]
        for h in range(N_HEADS):
            sl = slice(h * HEAD_PAD, (h + 1) * HEAD_PAD)
            qh, _ = _rms(q[:, sl], QK_HEAD)
            q_ref[:, sl] = _rope(qh * qg_ref[...], c, s1, s2).astype(BF16)
            kh, _ = _rms(kn[:, sl] + krp, QK_HEAD)
            k_ref[:, sl] = _rope(kh * kg_ref[...], c, s1, s2).astype(BF16)

    def row(w):
        return pl.BlockSpec((tm, w), lambda i: (i, 0))

    return pl.pallas_call(
        body, name="qkv_fwd", grid=(n // tm,),
        in_specs=[row(Q_LORA), row(KV_LORA), row(LANES), _const_spec((1, Q_LORA)), _const_spec((1, KV_LORA)),
                  _const_spec((QP_COLS, Q_LORA)), _const_spec((KV_LORA, QP_COLS)), _const_spec((KV_LORA, D_ATTN)),
                  _const_spec((1, LANES)), _const_spec((1, LANES)), row(LANES), row(LANES), row(LANES)],
        out_specs=[row(QP_COLS), row(QP_COLS), row(D_ATTN)],
        out_shape=[jax.ShapeDtypeStruct((n, QP_COLS), BF16), jax.ShapeDtypeStruct((n, QP_COLS), BF16),
                   jax.ShapeDtypeStruct((n, D_ATTN), BF16)],
        compiler_params=pltpu.CompilerParams(dimension_semantics=("parallel",), vmem_limit_bytes=VMEM_LIMIT),
    )(cq, ckv, kr, gqa, gkva, w_uq_p, w_uk_p, w_v, qg, kg, rc, rs1, rs2)


def _qkv_bwd(cq, ckv, kr, dq_r, dk_r, dv, dxr, dxg, gqa, gkva, w_uq_p, w_uk_p, w_v, qg, kg, rc, rs1, rs2,
             srcs=(), scatter=()):
    n = cq.shape[0]
    tm = _wide_row_tile(n)
    nk = len(srcs)
    c_in, c_out, c_shape, c_sems = _exchange_specs(srcs, scatter)

    def body(cq_ref, ckv_ref, kr_ref, dq_ref, dk_ref, dv_ref, dxr_ref, dxg_ref, gqa_ref, gkva_ref, wuq_ref, wuk_ref,
             wv_ref, qg_ref, kg_ref, c_ref, s1_ref, s2_ref, *rest):
        dp_ref, qa_ref, kva_ref, dqp_ref, dkv_ref, dqg_ref, dkg_ref, dgqa_ref, dgkva_ref = rest[nk:nk + 9]
        finish = _ride(1, *_exchange_fns(rest[:nk], rest[nk + 9:2 * nk + 9], rest[2 * nk + 9:], scatter))
        first = pl.program_id(0) == 0
        dp_ref[:, OFF_CKV:OFF_CKV + D_RNN] = dxr_ref[...].astype(BF16)
        dp_ref[:, OFF_CKV + D_RNN:OFF_CKV + 2 * D_RNN] = dxg_ref[...].astype(BF16)
        xq, rq = _rms(cq_ref[...], Q_LORA)
        qa = (xq * gqa_ref[...]).astype(BF16)
        qa_ref[...] = qa
        q = _dot_nt(qa, wuq_ref[...])
        xkv, rkv = _rms(ckv_ref[...], KV_LORA)
        kva = (xkv * gkva_ref[...]).astype(BF16)
        kva_ref[...] = kva
        kn = _dot(kva, wuk_ref[...])
        krp = kr_ref[...]
        c, s1, s2 = c_ref[...], s1_ref[...], s2_ref[...]
        lane = lax.broadcasted_iota(jnp.int32, (tm, HEAD_PAD), 1)
        rope_lanes = jnp.logical_and(lane >= QK_NOPE, lane < QK_HEAD)
        dqg = jnp.zeros((1, HEAD_PAD), F32)
        dkg = jnp.zeros((1, HEAD_PAD), F32)
        dkr = jnp.zeros((tm, HEAD_PAD), F32)
        for h in range(N_HEADS):
            sl = slice(h * HEAD_PAD, (h + 1) * HEAD_PAD)
            qh, rqh = _rms(q[:, sl], QK_HEAD)
            dy = _rope_bwd(dq_ref[:, sl], c, s1, s2)
            dqg = dqg + _colsum(dy * qh)
            dqp_ref[:, sl] = _rms_bwd(dy, qh, rqh, qg_ref[...], QK_HEAD).astype(BF16)
            kh, rkh = _rms(kn[:, sl] + krp, QK_HEAD)
            dyk = _rope_bwd(dk_ref[:, sl], c, s1, s2)
            dkg = dkg + _colsum(dyk * kh)
            dkh = _rms_bwd(dyk, kh, rkh, kg_ref[...], QK_HEAD)
            dkv_ref[:, sl] = dkh.astype(BF16)
            dkr = dkr + jnp.where(rope_lanes, dkh, 0.0)
        dkv_ref[:, QP_COLS:] = dv_ref[...].astype(BF16)
        dp_ref[:, OFF_CKV + 2 * D_RNN:] = dkr.astype(BF16)
        dqa = _dot(dqp_ref[...], wuq_ref[...])
        dp_ref[:, :Q_LORA] = _rms_bwd(dqa, xq, rq, gqa_ref[...], Q_LORA).astype(BF16)
        dkva = _dot_nt(dkv_ref[:, :QP_COLS], wuk_ref[...]) + _dot_nt(dkv_ref[:, QP_COLS:], wv_ref[...])
        dp_ref[:, Q_LORA:OFF_CKV] = _rms_bwd(dkva, xkv, rkv, gkva_ref[...], KV_LORA).astype(BF16)
        _acc(dqg_ref, first, dqg)
        _acc(dkg_ref, first, dkg)
        _acc(dgqa_ref, first, _colsum(dqa * xq))
        _acc(dgkva_ref, first, _colsum(dkva * xkv))
        finish()

    def row(w):
        return pl.BlockSpec((tm, w), lambda i: (i, 0))

    def acc(w):
        return pl.BlockSpec((1, w), lambda i: (0, 0))

    res = pl.pallas_call(
        body, name="qkv_bwd", grid=(n // tm,),
        in_specs=[row(Q_LORA), row(KV_LORA), row(LANES), row(QP_COLS), row(QP_COLS), row(D_ATTN), row(D_RNN), row(D_RNN),
                  _const_spec((1, Q_LORA)), _const_spec((1, KV_LORA)),
                  _const_spec((QP_COLS, Q_LORA)), _const_spec((KV_LORA, QP_COLS)), _const_spec((KV_LORA, D_ATTN)),
                  _const_spec((1, LANES)), _const_spec((1, LANES)), row(LANES), row(LANES), row(LANES)] + c_in,
        out_specs=[row(P_COLS), row(Q_LORA), row(KV_LORA), row(QP_COLS),
                   row(QP_COLS + D_ATTN), acc(LANES), acc(LANES), acc(Q_LORA), acc(KV_LORA)] + c_out,
        out_shape=[jax.ShapeDtypeStruct((n, P_COLS), BF16), jax.ShapeDtypeStruct((n, Q_LORA), BF16),
                   jax.ShapeDtypeStruct((n, KV_LORA), BF16), jax.ShapeDtypeStruct((n, QP_COLS), BF16),
                   jax.ShapeDtypeStruct((n, QP_COLS + D_ATTN), BF16),
                   jax.ShapeDtypeStruct((1, LANES), F32), jax.ShapeDtypeStruct((1, LANES), F32),
                   jax.ShapeDtypeStruct((1, Q_LORA), F32), jax.ShapeDtypeStruct((1, KV_LORA), F32)] + c_shape,
        scratch_shapes=c_sems,
        compiler_params=pltpu.CompilerParams(dimension_semantics=("arbitrary",), vmem_limit_bytes=VMEM_LIMIT),
    )(cq, ckv, kr, dq_r, dk_r, dv, dxr, dxg, gqa, gkva, w_uq_p, w_uk_p, w_v, qg, kg, rc, rs1, rs2, *srcs)
    return res[:9], res[9:]


KEY_CHUNK = 4 * LANES


def _key_chunks(t):
    count = max(t // KEY_CHUNK, 1)
    first = t - KEY_CHUNK * (count - 1)
    return [(0, first)] + [(first + KEY_CHUNK * c, KEY_CHUNK) for c in range(count - 1)]


def _attn_specs(t, tq):
    nq = t // tq
    qspec = pl.BlockSpec((tq, 2 * HEAD_PAD), lambda b, hp, i: (b * nq + i, hp))
    kspec = pl.BlockSpec((t, 2 * HEAD_PAD), lambda b, hp, i: (b, hp))
    vspec = pl.BlockSpec((t, 2 * V_HEAD), lambda b, hp, i: (b, hp))
    ospec = pl.BlockSpec((tq, 2 * V_HEAD), lambda b, hp, i: (b * nq + i, hp))
    return nq, qspec, kspec, vspec, ospec


def _probs_spec(t, tq):
    return pl.BlockSpec((1, 2, tq, t), lambda b, hp, i: (b, hp, i, 0))


def _attn_fwd(q, k, v, srcs=(), scatter=()):
    n = q.shape[0]
    t = _t_pad()
    tq = t // 2
    nq, qspec, kspec, vspec, ospec = _attn_specs(t, tq)
    nk = len(srcs)
    c_in, c_out, c_shape, c_sems = _exchange_specs(srcs, scatter)

    def body(q_ref, k_ref, v_ref, *rest):
        o_ref, l_ref, p_ref = rest[nk:nk + 3]
        finish = _ride(3, *_exchange_fns(rest[:nk], rest[nk + 3:2 * nk + 3], rest[2 * nk + 3:], scatter))
        lane = lax.broadcasted_iota(jnp.int32, (tq, 2 * V_HEAD), 1)
        outs = []
        sums = []
        for j in range(2):
            sl = slice(j * HEAD_PAD, (j + 1) * HEAD_PAD)
            qh = q_ref[:, sl]

            def scores(start, size):
                s = _dot_nt(qh, k_ref[start:start + size, sl])
                if start < PAD_ROWS:
                    key = lax.broadcasted_iota(jnp.int32, (tq, size), 1) + start
                    s = jnp.where(key >= PAD_ROWS, s, -jnp.inf)
                return s

            top = functools.reduce(jnp.maximum, [jnp.max(scores(*c), axis=-1, keepdims=True) for c in _key_chunks(t)])
            l = jnp.zeros((tq, 1), F32)
            pv = jnp.zeros((tq, 2 * V_HEAD), F32)
            for start, size in _key_chunks(t):
                e = jnp.exp2((scores(start, size) - top) * (QK_HEAD ** -0.5 * math.log2(math.e)))
                l = l + jnp.sum(e, axis=-1, keepdims=True)
                e = e.astype(BF16)
                p_ref[0, j, :, start:start + size] = e
                pv = pv + _dot(e, v_ref[start:start + size, :])
            outs.append(pv / l)
            sums.append(l)
        o_ref[...] = jnp.where(lane < V_HEAD, outs[0], outs[1])
        l_ref[...] = jnp.where(lane < V_HEAD, sums[0], sums[1])
        finish()

    res = pl.pallas_call(
        body, name="attn_fwd", grid=(n // t, N_HEADS // 2, nq),
        in_specs=[qspec, kspec, vspec] + c_in, out_specs=[ospec, ospec, _probs_spec(t, tq)] + c_out,
        out_shape=[jax.ShapeDtypeStruct((n, D_ATTN), F32), jax.ShapeDtypeStruct((n, D_ATTN), F32),
                   jax.ShapeDtypeStruct((n // t, N_HEADS, t, t), BF16)] + c_shape,
        scratch_shapes=c_sems,
        compiler_params=pltpu.CompilerParams(dimension_semantics=("arbitrary", "arbitrary", "arbitrary"),
                                             vmem_limit_bytes=VMEM_LIMIT),
    )(q, k, v, *srcs)
    return res[0], (res[1], res[2]), res[3:]


def _attn_bwd(q, k, v, do, o, probs, srcs=(), scatter=()):
    n = q.shape[0]
    t = _t_pad()
    tq = t // 2
    nq, qspec, kspec, vspec, ospec = _attn_specs(t, tq)
    nk = len(srcs)
    c_in, c_out, c_shape, c_sems = _exchange_specs(srcs, scatter)

    def body(q_ref, k_ref, v_ref, do_ref, o_ref, l_ref, p_ref, *rest):
        dq_ref, dk_ref, dv_ref = rest[nk:nk + 3]
        finish = _ride(3, *_exchange_fns(rest[:nk], rest[nk + 3:2 * nk + 3], rest[2 * nk + 3:], scatter))

        @pl.when(pl.program_id(2) == 0)
        def _():
            dk_ref[...] = jnp.zeros_like(dk_ref)
            dv_ref[...] = jnp.zeros_like(dv_ref)

        lane = lax.broadcasted_iota(jnp.int32, (tq, 2 * V_HEAD), 1)
        do = do_ref[...]
        do_o = do * o_ref[...]
        chunks = _key_chunks(t)
        dvs = [None] * len(chunks)
        for j in range(2):
            sl = slice(j * HEAD_PAD, (j + 1) * HEAD_PAD)
            qh = q_ref[:, sl]
            in_head = (lane < V_HEAD) if j == 0 else (lane >= V_HEAD)
            inv_l = 1.0 / l_ref[:, j * V_HEAD:j * V_HEAD + 1]
            doh = jnp.where(in_head, do, 0.0).astype(BF16)
            doh_n = jnp.where(in_head, do * inv_l, 0.0).astype(BF16)
            delta = jnp.sum(jnp.where(in_head, do_o, 0.0), axis=-1, keepdims=True)
            row_scale = inv_l * (QK_HEAD ** -0.5)
            dq = jnp.zeros((tq, HEAD_PAD), F32)
            for c, (start, size) in enumerate(chunks):
                rows = slice(start, start + size)
                e = p_ref[0, j, :, rows]
                dp = _dot_nt(doh, v_ref[rows, :])
                ds = (e.astype(F32) * (dp - delta) * row_scale).astype(BF16)
                dq = dq + _dot(ds, k_ref[rows, sl])
                dk_ref[rows, sl] += _dot_tn(ds, qh)
                dvc = _dot_tn(e, doh_n)
                dvs[c] = dvc if dvs[c] is None else dvs[c] + dvc
            dq_ref[:, sl] = dq
        for (start, size), dvc in zip(chunks, dvs):
            dv_ref[start:start + size, :] += dvc
        finish()

    res = pl.pallas_call(
        body, name="attn_bwd", grid=(n // t, N_HEADS // 2, nq),
        in_specs=[qspec, kspec, vspec, ospec, ospec, ospec, _probs_spec(t, tq)] + c_in,
        out_specs=[qspec, kspec, vspec] + c_out,
        out_shape=[jax.ShapeDtypeStruct((n, QP_COLS), F32), jax.ShapeDtypeStruct((n, QP_COLS), F32),
                   jax.ShapeDtypeStruct((n, D_ATTN), F32)] + c_shape, scratch_shapes=c_sems,
        compiler_params=pltpu.CompilerParams(dimension_semantics=("arbitrary", "arbitrary", "arbitrary"),
                                             vmem_limit_bytes=VMEM_LIMIT),
    )(q, k, v, do, o, *probs, *srcs)
    return res[:3], res[3:]


SCAN_STEPS = 8


def _scan(chains, t):
    seg = t // 8
    rows = lax.broadcasted_iota(jnp.int32, (8, LANES), 0)

    def step(i, carry):
        carry = list(carry)
        for u in range(SCAN_STEPS):
            j = i * SCAN_STEPS + u
            for n, (a_ref, b_ref, h_ref, p_ref, reverse) in enumerate(chains):
                h, p = carry[n]
                idx = pl.ds(seg - 1 - j if reverse else j, 8, stride=seg)
                a = a_ref[idx, :]
                h = a * h + b_ref[idx, :]
                p = a * p
                h_ref[idx, :] = h
                p_ref[idx, :] = p
                carry[n] = (h, p)
        return tuple(carry)

    init = tuple((jnp.zeros((8, LANES), F32), jnp.ones((8, LANES), F32)) for _ in chains)
    ends = lax.fori_loop(0, seg // SCAN_STEPS, step, init)
    for (_, _, h_ref, p_ref, reverse), (b, a) in zip(chains, ends):
        for d in (1, 2, 4):
            if reverse:
                keep = rows < 8 - d
                a_n, b_n = pltpu.roll(a, 8 - d, 0), pltpu.roll(b, 8 - d, 0)
            else:
                keep = rows >= d
                a_n, b_n = pltpu.roll(a, d, 0), pltpu.roll(b, d, 0)
            b = a * jnp.where(keep, b_n, 0.0) + b
            a = a * jnp.where(keep, a_n, 1.0)
        for s in (range(7) if reverse else range(1, 8)):
            sl = slice(s * seg, (s + 1) * seg)
            carry_in = b[s + 1:s + 2, :] if reverse else b[s - 1:s, :]
            h_ref[sl, :] = h_ref[sl, :] + p_ref[sl, :] * carry_in


def _shift_rows(x, s, rows, t):
    if s == 0:
        return x
    rolled = pltpu.roll(x, s % t, 0)
    return jnp.where(rows >= s, rolled, 0.0) if s > 0 else jnp.where(rows < t + s, rolled, 0.0)


def _neg_expm1_twice(h, exp_2h):
    series = h * (-2.0 + h * (-2.0 + h * (-4.0 / 3 + h * (-2.0 / 3))))
    return jnp.where(h > -0.05, series, 1.0 - exp_2h)


def _sigmoid(x):
    return 0.5 * jnp.tanh(0.5 * x) + 0.5


def _gelu_parts(x):
    k = math.sqrt(2.0 / math.pi)
    th = jnp.tanh(k * (x + 0.044715 * x * x * x))
    g = 0.5 * x * (1.0 + th)
    dg = 0.5 * (1.0 + th) + 0.5 * x * (1.0 - th * th) * k * (1.0 + 3 * 0.044715 * x * x)
    return g, dg


def _lru_gates(xc, gates, lam_ref, valid, d):
    r = _sigmoid(gates[:, (2 * d) * LANES:(2 * d + 1) * LANES])
    i = _sigmoid(gates[:, (2 * d + 1) * LANES:(2 * d + 2) * LANES])
    neg_lam = -lam_ref[d:d + 1, :]
    sp = jnp.maximum(neg_lam, 0.0) + jnp.log1p(jnp.exp(-jnp.abs(neg_lam)))
    log_a = -LRU_C * r * sp
    a = jnp.exp(log_a)
    m = jnp.maximum(_neg_expm1_twice(log_a, a * a), 0.0)
    sq = jnp.sqrt(m)
    b = jnp.where(valid, sq * (i * xc), 0.0)
    return r, i, sp, a, m, sq, b


def _conv(xr, cw_ref, cb_ref, rows, t):
    return (cw_ref[0:1, :] * _shift_rows(xr, 2, rows, t) + cw_ref[1:2, :] * _shift_rows(xr, 1, rows, t)
            + cw_ref[2:3, :] * xr + cw_ref[3:4, :] * _shift_rows(xr, -1, rows, t) + cb_ref[...])


def _rnn_specs(t):
    seq = pl.BlockSpec((t, LANES), lambda cb, b: (b, cb))
    cw = pl.BlockSpec((4, LANES), lambda cb, b: (0, cb))
    vec1 = pl.BlockSpec((1, LANES), lambda cb, b: (0, cb))
    vec2 = pl.BlockSpec((2, LANES), lambda cb, b: (0, cb))
    wblk = pl.BlockSpec((1, LANES, 4 * LANES), lambda cb, b: (cb, 0, 0))
    gbias = pl.BlockSpec((1, 1, 4 * LANES), lambda cb, b: (cb, 0, 0))
    return seq, cw, vec1, vec2, wblk, gbias


def _rnn_fwd(xr, xg, conv_w, conv_b, wblk, gbias, lam):
    n = xr.shape[0]
    t = _t_pad()
    seq, cw, vec1, vec2, wspec, gspec = _rnn_specs(t)
    both = pl.BlockSpec((2, t, LANES), lambda cb, b: (0, b, cb))

    def body(xr_ref, xg_ref, cw_ref, cb_ref, w_ref, gb_ref, lam_ref,
             o_ref, xc_ref, r_ref, i_ref, q_ref, h_ref, a_s, b_s, p_s):
        rows = lax.broadcasted_iota(jnp.int32, (t, LANES), 0)
        valid = rows >= PAD_ROWS
        xc = _conv(xr_ref[...], cw_ref, cb_ref, rows, t)
        xc_ref[...] = xc
        gates = _dot(xc.astype(BF16), w_ref[0]) + gb_ref[0]
        for d in range(2):
            r_ref[d], i_ref[d], _, a_s[d], _, q_ref[d], b_s[d] = _lru_gates(xc, gates, lam_ref, valid, d)
        _scan([(a_s.at[d], b_s.at[d], h_ref.at[d], p_s.at[d], d == 1) for d in range(2)], t)
        g, _ = _gelu_parts(xg_ref[...])
        o_ref[...] = (h_ref[0] + h_ref[1]) * g

    stacked = jax.ShapeDtypeStruct((2, n, D_RNN), F32)
    res = pl.pallas_call(
        body, name="rnn_fwd", grid=(D_RNN // LANES, n // t),
        in_specs=[seq, seq, cw, vec1, wspec, gspec, vec2], out_specs=[seq, seq, both, both, both, both],
        out_shape=[jax.ShapeDtypeStruct((n, D_RNN), F32), jax.ShapeDtypeStruct((n, D_RNN), F32)] + [stacked] * 4,
        scratch_shapes=[pltpu.VMEM((2, t, LANES), F32)] * 3,
        compiler_params=pltpu.CompilerParams(dimension_semantics=("parallel", "parallel"), vmem_limit_bytes=VMEM_LIMIT),
    )(xr, xg, conv_w, conv_b, wblk, gbias, lam)
    return res[0], tuple(res[1:])


def _rnn_bwd(xr, xg, do, saved, conv_w, wblk, lam, srcs=(), scatter=()):
    n = xr.shape[0]
    t = _t_pad()
    seq, cw, vec1, vec2, wspec, gspec = _rnn_specs(t)
    nk = len(srcs)
    c_in, c_out, c_shape, c_sems = _exchange_specs(srcs, scatter)

    def body(xr_ref, xg_ref, do_ref, xc_ref, r_s, i_s, q_s, h_s, cw_ref, w_ref, lam_ref, *rest):
        dxr_ref, dxg_ref, dcw_ref, dcb_ref, dw_ref, dgb_ref, dlam_ref = rest[nk:nk + 7]
        a_s, b_s, l_s, p_s, back_s, dg_s = rest[2 * nk + 7 + len(c_sems):]
        finish = _ride(2, *_exchange_fns(rest[:nk], rest[nk + 7:2 * nk + 7], rest[2 * nk + 7:2 * nk + 7 + len(c_sems)],
                                         scatter))
        first = pl.program_id(1) == 0
        rows = lax.broadcasted_iota(jnp.int32, (t, LANES), 0)
        valid = rows >= PAD_ROWS
        xr = xr_ref[...]
        xc = xc_ref[...]
        xcb = xc.astype(BF16)
        g, dg = _gelu_parts(xg_ref[...])
        do = do_ref[...]
        dxg_ref[...] = do * (h_s[0] + h_s[1]) * dg
        b_s[...] = do * g
        sps = []
        for d in range(2):
            neg_lam = -lam_ref[d:d + 1, :]
            sps.append(jnp.maximum(neg_lam, 0.0) + jnp.log1p(jnp.exp(-jnp.abs(neg_lam))))
            a_s[d] = jnp.exp(-LRU_C * r_s[d] * sps[d])
            back_s[d] = _shift_rows(a_s[d], -1 if d == 0 else 1, rows, t)
        _scan([(back_s.at[d], b_s, l_s.at[d], p_s.at[d], d == 0) for d in range(2)], t)
        dxc = jnp.zeros((t, LANES), F32)
        dlams = []
        for d in range(2):
            r, i, sp, a, sq = r_s[d], i_s[d], sps[d], a_s[d], q_s[d]
            lam_t = l_s[d]
            da = lam_t * _shift_rows(h_s[d], 1 if d == 0 else -1, rows, t)
            lam_v = jnp.where(valid, lam_t, 0.0)
            dsq = lam_v * (i * xc)
            di = lam_v * sq * xc
            dxc = dxc + lam_v * sq * i
            dm = jnp.where(sq > 0.0, dsq * 0.5 / jnp.where(sq > 0.0, sq, 1.0), 0.0)
            dla = da * a - 2.0 * dm * a * a
            dr = dla * (-LRU_C) * sp
            dsp = _colsum(dla * (-LRU_C) * r)
            dlams.append(dsp * -jax.nn.sigmoid(-lam_ref[d:d + 1, :]))
            dg_s[:, (2 * d) * LANES:(2 * d + 1) * LANES] = (dr * r * (1.0 - r)).astype(BF16)
            dg_s[:, (2 * d + 1) * LANES:(2 * d + 2) * LANES] = (di * i * (1.0 - i)).astype(BF16)
        dgates = dg_s[...]
        dxc = dxc + _dot_nt(dgates, w_ref[0])
        taps = [_shift_rows(dxc, j - 2, rows, t) for j in range(4)]
        dxr_ref[...] = (cw_ref[0:1, :] * taps[0] + cw_ref[1:2, :] * taps[1] + cw_ref[2:3, :] * taps[2]
                        + cw_ref[3:4, :] * taps[3])
        dcw = jnp.concatenate([_colsum(tap * xr) for tap in taps], axis=0)
        _acc(dcw_ref, first, dcw)
        _acc(dcb_ref, first, _colsum(dxc))
        _acc(dw_ref, first, _dot_tn(xcb, dgates)[None])
        _acc(dgb_ref, first, _colsum(dgates.astype(F32))[None])
        _acc(dlam_ref, first, jnp.concatenate(dlams, axis=0))
        finish()

    both = pl.BlockSpec((2, t, LANES), lambda cb, b: (0, b, cb))
    pair = pltpu.VMEM((2, t, LANES), F32)
    res = pl.pallas_call(
        body, name="rnn_bwd", grid=(D_RNN // LANES, n // t),
        in_specs=[seq, seq, seq, seq, both, both, both, both, cw, wspec, vec2] + c_in,
        out_specs=[seq, seq, cw, vec1, wspec, gspec, vec2] + c_out,
        out_shape=[jax.ShapeDtypeStruct((n, D_RNN), F32), jax.ShapeDtypeStruct((n, D_RNN), F32),
                   jax.ShapeDtypeStruct((4, D_RNN), F32), jax.ShapeDtypeStruct((1, D_RNN), F32),
                   jax.ShapeDtypeStruct((D_RNN // LANES, LANES, 4 * LANES), F32),
                   jax.ShapeDtypeStruct((D_RNN // LANES, 1, 4 * LANES), F32), jax.ShapeDtypeStruct((2, D_RNN), F32)]
        + c_shape,
        scratch_shapes=c_sems + [pair, pltpu.VMEM((t, LANES), F32), pair, pair, pair, pltpu.VMEM((t, 4 * LANES), BF16)],
        compiler_params=pltpu.CompilerParams(dimension_semantics=("arbitrary", "arbitrary"), vmem_limit_bytes=VMEM_LIMIT),
    )(xr, xg, do, *saved, conv_w, wblk, lam, *srcs)
    return res[:7], res[7:]


def _post(oa, orn, h0, tgt, ga, gr, g2, w_out, w_gate, w_up, w_down):
    n = oa.shape[0]
    tm = _row_tile(n)
    t = _t_pad()
    head = PAD_ROWS + N_META
    parts = tm // head

    def body(oa_ref, or_ref, h0_ref, *rest):
        tgt_refs = rest[:parts]
        (ga_ref, gr_ref, g2_ref, wo_ref, wg_ref, wu_ref, wd_ref,
         doa_ref, dor_ref, dh1_ref, mix_ref, h1n_ref, act_ref, dgate_ref, dup_ref, dy_ref,
         loss_ref, dga_ref, dgr_ref, dg2_ref, gate_s, up_s) = rest[parts:]
        first = pl.program_id(0) == 0
        xa, ra = _rms(oa_ref[...], D_ATTN)
        xr, rr = _rms(or_ref[...], D_RNN)
        mix = jnp.concatenate([(xa * ga_ref[...]).astype(BF16), (xr * gr_ref[...]).astype(BF16)], axis=-1)
        mix_ref[...] = mix.T
        h1 = h0_ref[...] + _dot(mix, wo_ref[...])
        x2, r2 = _rms(h1, D_MODEL)
        h1n = (x2 * g2_ref[...]).astype(BF16)
        h1n_ref[...] = h1n
        y = h1
        for cs in range(0, D_FF, FF_CHUNK):
            sl = slice(cs, cs + FF_CHUNK)
            gate = _dot_nt(h1n, wg_ref[sl, :])
            up = _dot_nt(h1n, wu_ref[sl, :])
            gate_s[:, sl] = gate
            up_s[:, sl] = up
            act = (gate * _sigmoid(gate) * up).astype(BF16)
            act_ref[sl, :] = act.T
            y = y + _dot(act, wd_ref[sl, :])
        row = pl.program_id(0) * tm + lax.broadcasted_iota(jnp.int32, (tm, 1), 0)
        for _ in range(1, n // t):
            row = jnp.where(row >= t, row - t, row)
        tgt = jnp.concatenate([ref[0] for ref in tgt_refs], axis=0)
        err = jnp.where(row >= PAD_ROWS + N_META, y - tgt, 0.0)
        _acc(loss_ref, first, jnp.full((1, LANES), 0.5 / D_MODEL, F32) * jnp.sum(err * err))
        dy = err * (1.0 / D_MODEL)
        dyb = dy.astype(BF16)
        dy_ref[...] = dyb
        dh1n = jnp.zeros((tm, D_MODEL), F32)
        for cs in range(0, D_FF, FF_CHUNK):
            sl = slice(cs, cs + FF_CHUNK)
            dact = _dot_nt(dyb, wd_ref[sl, :])
            gate, up = gate_s[:, sl], up_s[:, sl]
            sg = _sigmoid(gate)
            dgate = (dact * up * sg * (1.0 + gate * (1.0 - sg))).astype(BF16)
            dup = (dact * gate * sg).astype(BF16)
            dgate_ref[sl, :] = dgate.T
            dup_ref[sl, :] = dup.T
            dh1n = dh1n + _dot(dgate, wg_ref[sl, :]) + _dot(dup, wu_ref[sl, :])
        _acc(dg2_ref, first, _colsum(dh1n * x2))
        dh1 = dy + _rms_bwd(dh1n, x2, r2, g2_ref[...], D_MODEL)
        dh1_ref[...] = dh1
        dmix = _dot_nt(dh1.astype(BF16), wo_ref[...])
        dma, dmr = dmix[:, :D_ATTN], dmix[:, D_ATTN:]
        _acc(dga_ref, first, _colsum(dma * xa))
        _acc(dgr_ref, first, _colsum(dmr * xr))
        doa_ref[...] = _rms_bwd(dma, xa, ra, ga_ref[...], D_ATTN)
        dor_ref[...] = _rms_bwd(dmr, xr, rr, gr_ref[...], D_RNN)

    def row(w):
        return pl.BlockSpec((tm, w), lambda i: (i, 0))

    def acc(w):
        return pl.BlockSpec((1, w), lambda i: (0, 0))

    def col(w):
        return pl.BlockSpec((w, tm), lambda i: (0, i))

    outs = [(D_ATTN, F32, row), (D_RNN, F32, row), (D_MODEL, F32, row), (D_MODEL, BF16, col), (D_MODEL, BF16, row),
            (D_FF, BF16, col), (D_FF, BF16, col), (D_FF, BF16, col), (D_MODEL, BF16, row)]
    accs = [LANES, D_ATTN, D_RNN, D_MODEL]
    per = t // head

    def target_part(p):
        def index(i):
            block = i * parts + p
            return block // per, jnp.maximum(block % per - 1, 0), 0
        return pl.BlockSpec((1, head, D_MODEL), index)

    return pl.pallas_call(
        body, name="post", grid=(n // tm,),
        in_specs=[row(D_ATTN), row(D_RNN), row(D_MODEL)] + [target_part(p) for p in range(parts)] + [
                  _const_spec((1, D_ATTN)), _const_spec((1, D_RNN)), _const_spec((1, D_MODEL)),
                  _const_spec((D_MODEL, D_MODEL)), _const_spec((D_FF, D_MODEL)), _const_spec((D_FF, D_MODEL)),
                  _const_spec((D_FF, D_MODEL))],
        out_specs=[spec(w) for w, _, spec in outs] + [acc(w) for w in accs],
        out_shape=[jax.ShapeDtypeStruct((n, w) if spec is row else (w, n), dt) for w, dt, spec in outs]
        + [jax.ShapeDtypeStruct((1, w), F32) for w in accs],
        scratch_shapes=[pltpu.VMEM((tm, D_FF), F32), pltpu.VMEM((tm, D_FF), F32)],
        compiler_params=pltpu.CompilerParams(dimension_semantics=("arbitrary",), vmem_limit_bytes=VMEM_LIMIT),
    )(oa, orn, h0, *[tgt] * parts, ga, gr, g2, w_out, w_gate, w_up, w_down)


def _in_bwd(dp, h0, dh1, ln1_g, w_in_p, srcs=(), scatter=()):
    n = h0.shape[0]
    tm = _row_tile(n)
    nk = len(srcs)
    c_in, c_out, c_shape, c_sems = _exchange_specs(srcs, scatter)

    def body(dp_ref, h0_ref, dh1_ref, g_ref, w_ref, *rest):
        dh0_ref, dg_ref = rest[nk:nk + 2]
        finish = _ride(1, *_exchange_fns(rest[:nk], rest[nk + 2:2 * nk + 2], rest[2 * nk + 2:], scatter))
        dhn = _dot(dp_ref[...], w_ref[...])
        xhat, r = _rms(h0_ref[...], D_MODEL)
        _acc(dg_ref, pl.program_id(0) == 0, _colsum(dhn * xhat))
        dh0_ref[...] = dh1_ref[...] + _rms_bwd(dhn, xhat, r, g_ref[...], D_MODEL)
        finish()

    def row(w):
        return pl.BlockSpec((tm, w), lambda i: (i, 0))

    res = pl.pallas_call(
        body, name="in_bwd", grid=(n // tm,),
        in_specs=[row(P_COLS), row(D_MODEL), row(D_MODEL), _const_spec((1, D_MODEL)), _const_spec((P_COLS, D_MODEL))] + c_in,
        out_specs=[row(D_MODEL), pl.BlockSpec((1, D_MODEL), lambda i: (0, 0))] + c_out,
        out_shape=[jax.ShapeDtypeStruct((n, D_MODEL), F32), jax.ShapeDtypeStruct((1, D_MODEL), F32)] + c_shape,
        scratch_shapes=c_sems,
        compiler_params=pltpu.CompilerParams(dimension_semantics=("arbitrary",), vmem_limit_bytes=VMEM_LIMIT),
    )(dp, h0, dh1, ln1_g, w_in_p, *srcs)
    return res[:2], res[2:]


MAX_TILE = D_FF // 2


def _pick_tile(width, cap):
    best = LANES
    for mult in range(1, width // LANES + 1):
        cand = mult * LANES
        if width % cand == 0 and cand <= cap:
            best = cand
    return best


def _matmul_tn(name, a, b, srcs=(), scatter=()):
    n, ka = a.shape
    kb = b.shape[1]
    ta, tb = _pick_tile(ka, MAX_TILE), _pick_tile(kb, MAX_TILE)
    tk = n // 2
    nk = len(srcs)
    c_in, c_out, c_shape, c_sems = _exchange_specs(srcs, scatter)

    def body(a_ref, b_ref, *rest):
        o_ref = rest[nk]
        finish = _ride(3, *_exchange_fns(rest[:nk], rest[nk + 1:2 * nk + 1], rest[2 * nk + 1:], scatter))
        _acc(o_ref, pl.program_id(2) == 0, _dot_tn(a_ref[...].astype(BF16), b_ref[...].astype(BF16)))
        finish()

    res = pl.pallas_call(
        body, name=name, grid=(ka // ta, kb // tb, n // tk),
        in_specs=[pl.BlockSpec((tk, ta), lambda i, j, k: (k, i)), pl.BlockSpec((tk, tb), lambda i, j, k: (k, j))] + c_in,
        out_specs=[pl.BlockSpec((ta, tb), lambda i, j, k: (i, j))] + c_out,
        out_shape=[jax.ShapeDtypeStruct((ka, kb), F32)] + c_shape, scratch_shapes=c_sems,
        compiler_params=pltpu.CompilerParams(dimension_semantics=("arbitrary", "arbitrary", "arbitrary"),
                                             vmem_limit_bytes=VMEM_LIMIT),
    )(a, b, *srcs)
    return res[0], res[1:]


def _matmul_shards(name, at, b):
    ka, n = at.shape
    kb = b.shape[1]
    width = ka // N_DEV
    per = 2 if 2 * width >= 4 * LANES else 4
    ta = per * width

    def body(a_ref, b_ref, o_ref):
        out = _dot(a_ref[...], b_ref[...].astype(BF16))
        for s in range(per):
            o_ref[s] = out[s * width:(s + 1) * width, :].astype(BF16)

    return pl.pallas_call(
        body, name=name, grid=(ka // ta,),
        in_specs=[pl.BlockSpec((ta, n), lambda i: (i, 0)), _const_spec((n, kb))],
        out_specs=pl.BlockSpec((per, width, kb), lambda i: (i, 0, 0)),
        out_shape=jax.ShapeDtypeStruct((N_DEV, width, kb), BF16),
        compiler_params=pltpu.CompilerParams(dimension_semantics=("parallel",), vmem_limit_bytes=VMEM_LIMIT),
    )(at, b)


def _adamw_math(g8_ref, w_ref, m_ref, v_ref, g_ref, d_ref, nm_ref, nv_ref):
    g = g8_ref[0].astype(F32)
    for s in range(1, N_DEV):
        g = g + g8_ref[s].astype(F32)
    g_ref[...] = g
    nm = ADAM_B1 * m_ref[...] + (1.0 - ADAM_B1) * g
    nv = ADAM_B2 * v_ref[...] + (1.0 - ADAM_B2) * (g * g)
    nm_ref[...] = nm
    nv_ref[...] = nv
    m_hat = nm / (1.0 - ADAM_B1 ** ADAM_STEP)
    v_hat = nv / (1.0 - ADAM_B2 ** ADAM_STEP)
    d_ref[...] = -ADAM_LR * (m_hat / (jnp.sqrt(v_hat) + ADAM_EPS) + ADAM_WD * w_ref[...])


def _adamw_many(name, items):
    count = len(items)

    def body(*refs):
        ins, outs = refs[:4 * count], refs[4 * count:]
        for i in range(count):
            _adamw_math(*ins[4 * i:4 * i + 4], *outs[4 * i:4 * i + 4])

    flat = [a for item in items for a in item]
    res = pl.pallas_call(
        body, name=name,
        out_shape=[jax.ShapeDtypeStruct(item[1].shape, F32) for item in items for _ in range(4)],
        compiler_params=pltpu.CompilerParams(vmem_limit_bytes=VMEM_LIMIT),
    )(*flat)
    return [tuple(res[4 * i:4 * i + 4]) for i in range(count)]


def _adamw(name, g8, w, m, v):
    rows, cols = w.shape
    tr = rows
    for cand in (256, 176, 128, 64):
        if rows % cand == 0 and rows > cand:
            tr = cand
            break

    def body(*refs):
        _adamw_math(*refs)

    blk = pl.BlockSpec((tr, cols), lambda i: (i, 0))
    return pl.pallas_call(
        body, name=name, grid=(rows // tr,),
        in_specs=[pl.BlockSpec((N_DEV, tr, cols), lambda i: (0, i, 0)), blk, blk, blk],
        out_specs=[blk] * 4, out_shape=[jax.ShapeDtypeStruct((rows, cols), F32)] * 4,
        compiler_params=pltpu.CompilerParams(dimension_semantics=("parallel",), vmem_limit_bytes=VMEM_LIMIT),
    )(g8, w, m, v)


def _exchange_specs(srcs, scatter):
    nk = len(srcs)
    if not nk:
        return [], [], [], []
    any_spec = pl.BlockSpec(memory_space=pl.ANY)
    out_shape = [jax.ShapeDtypeStruct(s.shape if sc else (N_DEV,) + s.shape, s.dtype) for s, sc in zip(srcs, scatter)]
    sems = [pltpu.SemaphoreType.DMA((nk, N_DEV - 1)), pltpu.SemaphoreType.DMA((nk, N_DEV - 1)),
            pltpu.SemaphoreType.DMA((nk,))]
    return [any_spec] * nk, [any_spec] * nk, out_shape, sems


FLIPS = ((0, 0, 1), (1, 0, 0), (0, 1, 0), (1, 1, 0), (1, 0, 1), (0, 1, 1), (1, 1, 1))
N_CHIP_PEERS = 3


def _exchange_fns(src_refs, out_refs, sems, scatter):
    nk = len(src_refs)
    if not nk:
        return (lambda: None), (lambda: None), (lambda: None)
    send_sems, recv_sems, local_sems = sems
    first = 1 + N_CHIP_PEERS

    def plan():
        x, y, c = lax.axis_index("x"), lax.axis_index("y"), lax.axis_index("c")
        me = 4 * x + 2 * y + c
        peers = [(1 - x if fx else x, 1 - y if fy else y, 1 - c if fc else c) for fx, fy, fc in FLIPS]
        pids = [4 * px + 2 * py + pc for px, py, pc in peers]

        def remote(k, j, src, dst, to):
            return pltpu.make_async_remote_copy(src_ref=src, dst_ref=dst, send_sem=send_sems.at[k, j],
                                                recv_sem=recv_sems.at[k, j], device_id=to, device_id_type=MESH)

        def mine(k, dest):
            return src_refs[k].at[dest] if scatter[k] else src_refs[k]

        local = [pltpu.make_async_copy(mine(k, me), out_refs[k].at[me], local_sems.at[k]) for k in range(nk)]
        direct = [remote(k, j, mine(k, pids[j]), out_refs[k].at[me], peers[j])
                  for k in range(nk) for j in range(len(FLIPS) if scatter[k] else first)]
        relays = {(k, j): remote(k, j, out_refs[k].at[pids[j - N_CHIP_PEERS]], out_refs[k].at[pids[j - N_CHIP_PEERS]], peers[0])
                  for k in range(nk) if not scatter[k] for j in range(first, len(FLIPS))}
        arrivals = {(k, j): remote(k, j, out_refs[k].at[pids[j]], out_refs[k].at[pids[j]], peers[j])
                    for k in range(nk) for j in range(len(FLIPS))}
        return local, direct, relays, arrivals

    def start():
        local, direct, _, _ = plan()
        for cp in local + direct:
            cp.start()

    def relay():
        _, _, relays, arrivals = plan()
        for (k, j), cp in relays.items():
            arrivals[k, j - N_CHIP_PEERS].wait_recv()
            cp.start()

    def wait():
        local, direct, relays, arrivals = plan()
        for (k, j), cp in arrivals.items():
            if (k, j + N_CHIP_PEERS) not in relays:
                cp.wait_recv()
        for cp in direct + list(relays.values()):
            cp.wait_send()
        for cp in local:
            cp.wait()

    return start, relay, wait


def _grid_step(rank):
    step, total = 0, 1
    for axis in range(rank):
        step = step * pl.num_programs(axis) + pl.program_id(axis)
        total = total * pl.num_programs(axis)
    return step, total


def _ride(rank, start, relay, wait):
    step, total = _grid_step(rank)
    pl.when(step == 0)(start)
    pl.when(step == (3 * total) // 4)(relay)
    return lambda: pl.when(step == total - 1)(wait)


def _exchange(name, srcs, scatter):
    nk = len(srcs)
    c_in, c_out, c_shape, c_sems = _exchange_specs(srcs, scatter)

    def body(*refs):
        start, relay, wait = _exchange_fns(refs[:nk], refs[nk:2 * nk], refs[2 * nk:], scatter)
        start()
        relay()
        wait()

    return pl.pallas_call(body, name=name, in_specs=c_in, out_specs=c_out, out_shape=c_shape, scratch_shapes=c_sems)(*srcs)


def _cols_from_shards(g):
    return jnp.transpose(g, (1, 0, 2)).reshape(g.shape[1], -1)


def _cols_to_shards(w):
    return jnp.transpose(w.reshape(w.shape[0], N_DEV, -1), (1, 0, 2))


def _prep(x, srcs, scatter):
    nb = x.shape[0]
    t = _t_pad()
    head = PAD_ROWS + N_META
    nk = len(srcs)
    c_in, c_out, c_shape, c_sems = _exchange_specs(srcs, scatter)

    def body(x_ref, *rest):
        h0_ref = rest[nk]
        finish = _ride(1, *_exchange_fns(rest[:nk], rest[nk + 1:2 * nk + 1], rest[2 * nk + 1:], scatter))
        lead = pl.program_id(0) == 0

        @pl.when(lead)
        def _():
            h0_ref[...] = jnp.zeros_like(h0_ref)

        @pl.when(jnp.logical_not(lead))
        def _():
            h0_ref[...] = x_ref[...]

        finish()

    src = pl.BlockSpec((nb, head, D_MODEL), lambda j: (0, jnp.maximum(j - 1, 0), 0))
    dst = pl.BlockSpec((nb, head, D_MODEL), lambda j: (0, j, 0))
    res = pl.pallas_call(
        body, name="prep", grid=(t // head,), in_specs=[src] + c_in, out_specs=[dst] + c_out,
        out_shape=[jax.ShapeDtypeStruct((nb, t, D_MODEL), F32)] + c_shape, scratch_shapes=c_sems,
        compiler_params=pltpu.CompilerParams(dimension_semantics=("arbitrary",)),
    )(x, *srcs)
    return res[0], res[1:]


def _rope_tables(n):
    t = _t_pad()
    pos = np.arange(t, dtype=np.float32) - np.float32(PAD_ROWS)
    half = QK_ROPE // 2
    freqs = (1.0 / (ROPE_THETA ** (np.arange(half, dtype=np.float32) / half))).astype(np.float32)
    ang = pos[:, None] * freqs[None, :]
    cos, sin = np.cos(ang), np.sin(ang)
    z = lambda w: np.zeros((t, w), np.float32)
    c = np.concatenate([np.ones((t, QK_NOPE), np.float32), cos, cos, z(HEAD_PAD - QK_HEAD)], axis=1)
    s1 = np.concatenate([z(QK_NOPE + half), sin, z(HEAD_PAD - QK_HEAD)], axis=1)
    s2 = np.concatenate([z(QK_NOPE), -sin, z(HEAD_PAD - QK_NOPE - half)], axis=1)
    return tuple(jnp.asarray(np.tile(a, (n // t, 1))) for a in (c, s1, s2))


def _block_diag_gates(lru_wa, lru_wi):
    eye = jnp.eye(2, dtype=lru_wa.dtype)

    def bd(w):
        w = w.reshape(2, D_RNN // LANES, 2, RNN_BW, RNN_BW)
        full = w[:, :, :, :, None, :] * eye[None, None, :, None, :, None]
        return full.reshape(2, D_RNN // LANES, LANES, LANES)

    a, i = bd(lru_wa), bd(lru_wi)
    return jnp.concatenate([a[0], i[0], a[1], i[1]], axis=-1)


def _unblock_gates(dw):
    nb = D_RNN // LANES
    parts = dw.reshape(nb, 2, RNN_BW, 4, 2, RNN_BW)
    diag = jnp.stack([parts[:, k, :, :, k, :] for k in range(2)], axis=1)
    diag = jnp.transpose(diag, (3, 0, 1, 2, 4)).reshape(4, 2 * nb, RNN_BW, RNN_BW)
    return jnp.stack([diag[0], diag[2]]), jnp.stack([diag[1], diag[3]])


WEIGHTS = ("meta_tokens", "ln1_g", "w_in", "q_a_norm_g", "w_uq", "kv_a_norm_g", "w_ukv", "q_norm_g", "k_norm_g",
           "conv_w", "conv_b", "lru_wa", "lru_ba", "lru_wi", "lru_bi", "lru_lambda", "attn_out_g", "rnn_out_g",
           "w_out", "ln2_g", "w_gate", "w_up", "w_down")
BIG = ("w_in", "w_uq", "w_ukv", "w_out", "w_gate", "w_up", "w_down")
TRANSPOSED = ("w_in", "w_uq", "w_gate", "w_up")
ROW_SHARDED = ("w_out", "w_down") + TRANSPOSED
REPLICATED = ("ln1_g", "q_a_norm_g", "kv_a_norm_g", "q_norm_g", "k_norm_g", "conv_b", "lru_wa", "lru_wi",
              "attn_out_g", "rnn_out_g", "ln2_g")
WHOLE = REPLICATED + ("loss",)
G_FIRST = ("w_in", "meta_tokens")
G_MID = ("w_uq", "w_ukv", "conv_w", "lru_ba", "lru_bi", "lru_lambda")
LATE = ("w_out", "w_gate", "w_up", "w_down")
G_LAST = ("meta_tokens", "ln1_g")


def _local_step(x, tgt, ex):
    nb = x.shape[0]
    t = _t_pad()
    n = nb * t
    local = ex.local
    h0, got = _prep(x, *ex.gather_srcs(G_FIRST))
    first = ex.gathered(G_FIRST, got)
    meta, w_in = first["meta_tokens"], first["w_in"]
    h0 = h0.at[:, PAD_ROWS:PAD_ROWS + N_META].set(jnp.broadcast_to(meta[None], (nb, N_META, D_MODEL))).reshape(n, D_MODEL)

    zr = lambda r: jnp.zeros((r, D_MODEL), w_in.dtype)
    w_in_p = jnp.concatenate([w_in[:OFF_CKV], w_in[OFF_KR:], zr(QK_NOPE), w_in[OFF_CKV:OFF_KR], zr(HEAD_PAD - QK_HEAD)],
                             axis=0)
    pad_g = lambda g: jnp.pad(g, ((0, 0), (0, HEAD_PAD - QK_HEAD)))
    qg, kg = pad_g(local["q_norm_g"]), pad_g(local["k_norm_g"])
    rc, rs1, rs2 = _rope_tables(n)
    wblk = _block_diag_gates(local["lru_wa"].reshape(2, -1, RNN_BW, RNN_BW),
                             local["lru_wi"].reshape(2, -1, RNN_BW, RNN_BW)).astype(BF16)
    nblk = D_RNN // LANES

    (hn, cq, ckv, xr, xg, kr), got = _in_proj(h0, local["ln1_g"], w_in_p, *ex.gather_srcs(G_MID))
    w = ex.gathered(G_MID, got)
    w_uq_p = jnp.pad(w["w_uq"].reshape(N_HEADS, QK_HEAD, Q_LORA), ((0, 0), (0, HEAD_PAD - QK_HEAD), (0, 0))
                     ).reshape(QP_COLS, Q_LORA)
    ukv = w["w_ukv"].reshape(KV_LORA, N_HEADS, QK_NOPE + V_HEAD)
    w_uk_p = jnp.pad(ukv[:, :, :QK_NOPE], ((0, 0), (0, 0), (0, HEAD_PAD - QK_NOPE))).reshape(KV_LORA, QP_COLS)
    w_v = ukv[:, :, QK_NOPE:].reshape(KV_LORA, D_ATTN)
    gbias = jnp.stack([w["lru_ba"][0], w["lru_bi"][0], w["lru_ba"][1], w["lru_bi"][1]], axis=0)
    gbias = jnp.transpose(gbias.reshape(4, nblk, LANES), (1, 0, 2)).reshape(nblk, 1, 4 * LANES)

    q, k, v = _qkv_fwd(cq, ckv, kr, local["q_a_norm_g"], local["kv_a_norm_g"], w_uq_p, w_uk_p, w_v, qg, kg, rc, rs1, rs2)
    oa, probs, got = _attn_fwd(q, k, v, *ex.gather_srcs(LATE))
    late = ex.gathered(LATE, got)
    orn, rnn_saved = _rnn_fwd(xr, xg, w["conv_w"], local["conv_b"], wblk, gbias, w["lru_lambda"])
    (doa, dor, dh1, mix_t, h1n, act_t, dgate_t, dup_t, dyb, loss, dga, dgr, dg2) = _post(
        oa, orn, h0, tgt, local["attn_out_g"], local["rnn_out_g"], local["ln2_g"], late["w_out"], late["w_gate"],
        late["w_up"], late["w_down"])
    wire = {"w_out": _matmul_shards("dw_out", mix_t, dh1), "w_gate": _matmul_shards("dw_gate", dgate_t, h1n),
            "w_up": _matmul_shards("dw_up", dup_t, h1n), "w_down": _matmul_shards("dw_down", act_t, dyb)}
    names = ("w_gate",)
    (dxr, dxg, dcw, dcb, dwblk, dgb, dlam), got = _rnn_bwd(xr, xg, dor, rnn_saved, w["conv_w"], wblk, w["lru_lambda"],
                                                           *ex.scatter_srcs(names, wire))
    summed = ex.scattered(names, wire, got)
    dwa, dwi = _unblock_gates(dwblk)
    dgb = jnp.transpose(dgb.reshape(nblk, 4, LANES), (1, 0, 2)).reshape(4, D_RNN)
    names = ("w_out", "w_up", "w_down")
    (dq_r, dk_r, dv), got = _attn_bwd(q, k, v, doa, oa, probs, *ex.scatter_srcs(names, wire))
    summed.update(ex.scattered(names, wire, got))
    wire = ex.to_wire({
        "conv_w": dcw, "conv_b": dcb, "lru_wa": dwa.reshape(-1, RNN_BW), "lru_ba": jnp.stack([dgb[0], dgb[2]]),
        "lru_wi": dwi.reshape(-1, RNN_BW), "lru_bi": jnp.stack([dgb[1], dgb[3]]), "lru_lambda": dlam,
        "attn_out_g": dga, "rnn_out_g": dgr, "ln2_g": dg2, "loss": loss})
    names = tuple(wire)
    (dp, qa, kva, dqp, dkv, dqg, dkg, dgqa, dgkva), got = _qkv_bwd(
        cq, ckv, kr, dq_r, dk_r, dv, dxr, dxg, local["q_a_norm_g"], local["kv_a_norm_g"], w_uq_p, w_uk_p, w_v, qg, kg,
        rc, rs1, rs2, *ex.scatter_srcs(names, wire))
    summed.update(ex.scattered(names, wire, got))
    dw_uq_p, _ = _matmul_tn("dw_uq", dqp, qa)
    dw_kv, _ = _matmul_tn("dw_ukv", kva, dkv)
    dw_uq = dw_uq_p.reshape(N_HEADS, HEAD_PAD, Q_LORA)[:, :QK_HEAD].reshape(N_HEADS * QK_HEAD, Q_LORA)
    dw_ukv = jnp.concatenate([dw_kv[:, :QP_COLS].reshape(KV_LORA, N_HEADS, HEAD_PAD)[:, :, :QK_NOPE],
                              dw_kv[:, QP_COLS:].reshape(KV_LORA, N_HEADS, V_HEAD)], axis=2).reshape(KV_LORA, -1)
    wire = ex.to_wire({"q_a_norm_g": dgqa, "w_uq": dw_uq, "kv_a_norm_g": dgkva, "w_ukv": dw_ukv,
                       "q_norm_g": dqg[:, :QK_HEAD], "k_norm_g": dkg[:, :QK_HEAD]})
    names = tuple(wire)
    dw_in_p, got = _matmul_tn("dw_in", dp, hn, *ex.scatter_srcs(names, wire))
    summed.update(ex.scattered(names, wire, got))
    kr0 = OFF_CKV + 2 * D_RNN + QK_NOPE
    dw_in = jnp.concatenate([dw_in_p[:OFF_CKV], dw_in_p[kr0:kr0 + QK_ROPE], dw_in_p[OFF_CKV:OFF_CKV + 2 * D_RNN]], axis=0)
    wire = ex.to_wire({"w_in": dw_in})
    (dh0, dg1), got = _in_bwd(dp, h0, dh1, local["ln1_g"], w_in_p, *ex.scatter_srcs(("w_in",), wire))
    summed.update(ex.scattered(("w_in",), wire, got))

    dh0 = dh0.reshape(nb, t, D_MODEL)
    wire = ex.to_wire({"meta_tokens": jnp.sum(dh0[:, PAD_ROWS:PAD_ROWS + N_META], axis=0), "ln1_g": dg1})
    got = ex.run("reduce_last", *ex.scatter_srcs(G_LAST, wire))
    summed.update(ex.scattered(G_LAST, wire, got))
    return dh0[:, PAD_ROWS + N_META:], summed


class _MeshExchange:
    def __init__(self, shards):
        self.local = shards

    @staticmethod
    def run(name, srcs, scatter):
        return _exchange(name, srcs, scatter)

    def gather_srcs(self, names):
        return [self.local[k].astype(BF16) if k in BIG else self.local[k] for k in names], [False] * len(names)

    @staticmethod
    def gathered(names, outs):
        return {k: g.reshape(-1, g.shape[-1]) if k in ROW_SHARDED else _cols_from_shards(g) for k, g in zip(names, outs)}

    @staticmethod
    def to_wire(grads):
        wire = {}
        for k, g in grads.items():
            if k in WHOLE:
                wire[k] = g
            elif k in ROW_SHARDED:
                wire[k] = g.reshape(N_DEV, -1, g.shape[-1]).astype(BF16)
            else:
                wire[k] = _cols_to_shards(g).astype(BF16) if k in BIG else _cols_to_shards(g)
        return wire

    @staticmethod
    def scatter_srcs(names, wire):
        return [wire[k] for k in names], [k not in WHOLE for k in names]

    @staticmethod
    def scattered(names, wire, outs):
        return dict(zip(names, outs))


def kernel(x, meta_tokens, ln1_g, w_in, q_a_norm_g, w_uq, kv_a_norm_g, w_ukv, q_norm_g, k_norm_g, conv_w, conv_b, lru_wa, lru_ba, lru_wi, lru_bi, lru_lambda, attn_out_g, rnn_out_g, w_out, ln2_g, w_gate, w_up, w_down, loss_target, m_meta_tokens, m_ln1_g, m_w_in, m_q_a_norm_g, m_w_uq, m_kv_a_norm_g, m_w_ukv, m_q_norm_g, m_k_norm_g, m_conv_w, m_conv_b, m_lru_wa, m_lru_ba, m_lru_wi, m_lru_bi, m_lru_lambda, m_attn_out_g, m_rnn_out_g, m_w_out, m_ln2_g, m_w_gate, m_w_up, m_w_down, v_meta_tokens, v_ln1_g, v_w_in, v_q_a_norm_g, v_w_uq, v_kv_a_norm_g, v_w_ukv, v_q_norm_g, v_k_norm_g, v_conv_w, v_conv_b, v_lru_wa, v_lru_ba, v_lru_wi, v_lru_bi, v_lru_lambda, v_attn_out_g, v_rnn_out_g, v_w_out, v_ln2_g, v_w_gate, v_w_up, v_w_down):
    given = (meta_tokens, ln1_g, w_in, q_a_norm_g, w_uq, kv_a_norm_g, w_ukv, q_norm_g, k_norm_g, conv_w, conv_b,
             lru_wa, lru_ba, lru_wi, lru_bi, lru_lambda, attn_out_g, rnn_out_g, w_out, ln2_g, w_gate, w_up, w_down)
    moments_m = (m_meta_tokens, m_ln1_g, m_w_in, m_q_a_norm_g, m_w_uq, m_kv_a_norm_g, m_w_ukv, m_q_norm_g, m_k_norm_g,
                 m_conv_w, m_conv_b, m_lru_wa, m_lru_ba, m_lru_wi, m_lru_bi, m_lru_lambda, m_attn_out_g, m_rnn_out_g,
                 m_w_out, m_ln2_g, m_w_gate, m_w_up, m_w_down)
    moments_v = (v_meta_tokens, v_ln1_g, v_w_in, v_q_a_norm_g, v_w_uq, v_kv_a_norm_g, v_w_ukv, v_q_norm_g, v_k_norm_g,
                 v_conv_w, v_conv_b, v_lru_wa, v_lru_ba, v_lru_wi, v_lru_bi, v_lru_lambda, v_attn_out_g, v_rnn_out_g,
                 v_w_out, v_ln2_g, v_w_gate, v_w_up, v_w_down)
    shapes = {k: a.shape for k, a in zip(WEIGHTS, given)}

    def two_d(k, a):
        a = a.reshape(-1, a.shape[-1])
        return a.T if k in TRANSPOSED else a

    w = {k: two_d(k, a) for k, a in zip(WEIGHTS, given)}
    m = {k: two_d(k, a) for k, a in zip(WEIGHTS, moments_m)}
    v = {k: two_d(k, a) for k, a in zip(WEIGHTS, moments_v)}

    grad_x, parts = _local_step(x, loss_target, _MeshExchange(w))

    tiled = ("w_in", "w_gate", "w_up", "w_down")
    new = {k: _adamw("adamw_" + k, parts[k], w[k], m[k], v[k]) for k in tiled}
    small = [k for k in WEIGHTS if k not in tiled]
    new.update(zip(small, _adamw_many("adamw_small", [(parts[k], w[k], m[k], v[k]) for k in small])))

    loss = jnp.sum(parts["loss"][:, 0, 0])
    outs = [loss, grad_x]
    for idx in range(4):
        outs += [(new[k][idx].T if k in TRANSPOSED else new[k][idx]).reshape(shapes[k]) for k in WEIGHTS]
    return tuple(outs)
```

```python
import functools
import math

import numpy as np
import jax
import jax.numpy as jnp
from jax import lax
from jax.experimental import pallas as pl
from jax.experimental.pallas import tpu as pltpu

F32 = jnp.float32
BF16 = jnp.bfloat16

D_MODEL = 1024
N_META = 16
SEQ = 2048
N_HEADS = 8
QK_NOPE = 64
QK_ROPE = 32
QK_HEAD = QK_NOPE + QK_ROPE
V_HEAD = 64
D_ATTN = N_HEADS * V_HEAD
Q_LORA = 384
KV_LORA = 256
D_RNN = 512
RNN_BW = 64
D_FF = 2816
EPS = 1e-6
LRU_C = 8.0
ROPE_THETA = 10000.0
OFF_CKV = Q_LORA + KV_LORA
OFF_KR = OFF_CKV + QK_ROPE
IN_COLS = OFF_KR + 2 * D_RNN

ADAM_LR = 0.001
ADAM_B1 = 0.9
ADAM_B2 = 0.999
ADAM_EPS = 1e-08
ADAM_WD = 0.01
ADAM_STEP = 10

N_DEV = 8
LANES = 128
HEAD_PAD = LANES
PAD_ROWS = LANES - N_META
QP_COLS = N_HEADS * HEAD_PAD
P_COLS = OFF_CKV + 2 * D_RNN + LANES
FF_CHUNK = D_FF
VMEM_LIMIT = 56 * 1024 * 1024
MESH = pl.DeviceIdType.MESH


def _t_pad():
    return PAD_ROWS + N_META + SEQ


def _row_tile(n):
    return 256 if n % 256 == 0 else 128


def _wide_row_tile(n):
    quarter = _t_pad() // 4
    return quarter if quarter % 16 == 0 and n % quarter == 0 else _row_tile(n)


def _const_spec(shape):
    nd = len(shape)
    return pl.BlockSpec(shape, lambda *_: (0,) * nd, pipeline_mode=pl.Buffered(1))


def _rms(x, d):
    r = lax.rsqrt(jnp.sum(x * x, axis=-1, keepdims=True) * (1.0 / d) + EPS)
    return x * r, r


def _rms_bwd(dy, xhat, r, g, d):
    dxh = dy * g
    return r * (dxh - xhat * (jnp.sum(dxh * xhat, axis=-1, keepdims=True) * (1.0 / d)))


def _colsum(x):
    return jnp.sum(x, axis=0, keepdims=True)


def _dot(a, b):
    return jnp.dot(a, b, preferred_element_type=F32)


def _dot_nt(a, b):
    return lax.dot_general(a, b, (((1,), (1,)), ((), ())), preferred_element_type=F32)


def _dot_tn(a, b):
    return lax.dot_general(a, b, (((0,), (0,)), ((), ())), preferred_element_type=F32)


def _rope(x, c, s1, s2):
    return x * c + pltpu.roll(x, 16, 1) * s1 + pltpu.roll(x, HEAD_PAD - 16, 1) * s2


def _rope_bwd(dy, c, s1, s2):
    return dy * c + pltpu.roll(dy * s1, HEAD_PAD - 16, 1) + pltpu.roll(dy * s2, 16, 1)


def _acc(ref, first, val):
    @pl.when(first)
    def _():
        ref[...] = val

    @pl.when(jnp.logical_not(first))
    def _():
        ref[...] += val


def _in_proj(h0, ln1_g, w_in_p, srcs=(), scatter=()):
    n = h0.shape[0]
    tm = _wide_row_tile(n)
    nk = len(srcs)
    c_in, c_out, c_shape, c_sems = _exchange_specs(srcs, scatter)

    def body(h_ref, g_ref, w_ref, *rest):
        hn_ref, cq_ref, ckv_ref, xr_ref, xg_ref, kr_ref = rest[nk:nk + 6]
        finish = _ride(1, *_exchange_fns(rest[:nk], rest[nk + 6:2 * nk + 6], rest[2 * nk + 6:], scatter))
        xhat, _ = _rms(h_ref[...], D_MODEL)
        hn = (xhat * g_ref[...]).astype(BF16)
        hn_ref[...] = hn
        p = _dot_nt(hn, w_ref[...])
        cq_ref[...] = p[:, :Q_LORA]
        ckv_ref[...] = p[:, Q_LORA:OFF_CKV]
        xr_ref[...] = p[:, OFF_CKV:OFF_CKV + D_RNN]
        xg_ref[...] = p[:, OFF_CKV + D_RNN:OFF_CKV + 2 * D_RNN]
        kr_ref[...] = p[:, OFF_CKV + 2 * D_RNN:]
        finish()

    def row(w):
        return pl.BlockSpec((tm, w), lambda i: (i, 0))

    widths = (D_MODEL, Q_LORA, KV_LORA, D_RNN, D_RNN, LANES)
    res = pl.pallas_call(
        body, name="in_proj", grid=(n // tm,),
        in_specs=[row(D_MODEL), _const_spec((1, D_MODEL)), _const_spec((P_COLS, D_MODEL))] + c_in,
        out_specs=[row(w) for w in widths] + c_out,
        out_shape=[jax.ShapeDtypeStruct((n, w), BF16 if k == 0 else F32) for k, w in enumerate(widths)] + c_shape,
        scratch_shapes=c_sems,
        compiler_params=pltpu.CompilerParams(dimension_semantics=("arbitrary",), vmem_limit_bytes=VMEM_LIMIT),
    )(h0, ln1_g, w_in_p, *srcs)
    return res[:6], res[6:]


def _qkv_fwd(cq, ckv, kr, gqa, gkva, w_uq_p, w_uk_p, w_v, qg, kg, rc, rs1, rs2):
    n = cq.shape[0]
    tm = _wide_row_tile(n)

    def body(cq_ref, ckv_ref, kr_ref, gqa_ref, gkva_ref, wuq_ref, wuk_ref, wv_ref, qg_ref, kg_ref,
             c_ref, s1_ref, s2_ref, q_ref, k_ref, v_ref):
        xq, _ = _rms(cq_ref[...], Q_LORA)
        qa = (xq * gqa_ref[...]).astype(BF16)
        q = _dot_nt(qa, wuq_ref[...])
        xkv, _ = _rms(ckv_ref[...], KV_LORA)
        kva = (xkv * gkva_ref[...]).astype(BF16)
        kn = _dot(kva, wuk_ref[...])
        v_ref[...] = _dot(kva, wv_ref[...]).astype(BF16)
        krp = kr_ref[...]
        c, s1, s2 = c_ref[...], s1_ref[...], s2_ref[...]
        for h in range(N_HEADS):
            sl = slice(h * HEAD_PAD, (h + 1) * HEAD_PAD)
            qh, _ = _rms(q[:, sl], QK_HEAD)
            q_ref[:, sl] = _rope(qh * qg_ref[...], c, s1, s2).astype(BF16)
            kh, _ = _rms(kn[:, sl] + krp, QK_HEAD)
            k_ref[:, sl] = _rope(kh * kg_ref[...], c, s1, s2).astype(BF16)

    def row(w):
        return pl.BlockSpec((tm, w), lambda i: (i, 0))

    return pl.pallas_call(
        body, name="qkv_fwd", grid=(n // tm,),
        in_specs=[row(Q_LORA), row(KV_LORA), row(LANES), _const_spec((1, Q_LORA)), _const_spec((1, KV_LORA)),
                  _const_spec((QP_COLS, Q_LORA)), _const_spec((KV_LORA, QP_COLS)), _const_spec((KV_LORA, D_ATTN)),
                  _const_spec((1, LANES)), _const_spec((1, LANES)), row(LANES), row(LANES), row(LANES)],
        out_specs=[row(QP_COLS), row(QP_COLS), row(D_ATTN)],
        out_shape=[jax.ShapeDtypeStruct((n, QP_COLS), BF16), jax.ShapeDtypeStruct((n, QP_COLS), BF16),
                   jax.ShapeDtypeStruct((n, D_ATTN), BF16)],
        compiler_params=pltpu.CompilerParams(dimension_semantics=("parallel",), vmem_limit_bytes=VMEM_LIMIT),
    )(cq, ckv, kr, gqa, gkva, w_uq_p, w_uk_p, w_v, qg, kg, rc, rs1, rs2)


def _qkv_bwd(cq, ckv, kr, dq_r, dk_r, dv, dxr, dxg, gqa, gkva, w_uq_p, w_uk_p, w_v, qg, kg, rc, rs1, rs2,
             srcs=(), scatter=()):
    n = cq.shape[0]
    tm = _wide_row_tile(n)
    nk = len(srcs)
    c_in, c_out, c_shape, c_sems = _exchange_specs(srcs, scatter)

    def body(cq_ref, ckv_ref, kr_ref, dq_ref, dk_ref, dv_ref, dxr_ref, dxg_ref, gqa_ref, gkva_ref, wuq_ref, wuk_ref,
             wv_ref, qg_ref, kg_ref, c_ref, s1_ref, s2_ref, *rest):
        dp_ref, qa_ref, kva_ref, dqp_ref, dkv_ref, dqg_ref, dkg_ref, dgqa_ref, dgkva_ref = rest[nk:nk + 9]
        finish = _ride(1, *_exchange_fns(rest[:nk], rest[nk + 9:2 * nk + 9], rest[2 * nk + 9:], scatter))
        first = pl.program_id(0) == 0
        dp_ref[:, OFF_CKV:OFF_CKV + D_RNN] = dxr_ref[...].astype(BF16)
        dp_ref[:, OFF_CKV + D_RNN:OFF_CKV + 2 * D_RNN] = dxg_ref[...].astype(BF16)
        xq, rq = _rms(cq_ref[...], Q_LORA)
        qa = (xq * gqa_ref[...]).astype(BF16)
        qa_ref[...] = qa
        q = _dot_nt(qa, wuq_ref[...])
        xkv, rkv = _rms(ckv_ref[...], KV_LORA)
        kva = (xkv * gkva_ref[...]).astype(BF16)
        kva_ref[...] = kva
        kn = _dot(kva, wuk_ref[...])
        krp = kr_ref[...]
        c, s1, s2 = c_ref[...], s1_ref[...], s2_ref[...]
        lane = lax.broadcasted_iota(jnp.int32, (tm, HEAD_PAD), 1)
        rope_lanes = jnp.logical_and(lane >= QK_NOPE, lane < QK_HEAD)
        dqg = jnp.zeros((1, HEAD_PAD), F32)
        dkg = jnp.zeros((1, HEAD_PAD), F32)
        dkr = jnp.zeros((tm, HEAD_PAD), F32)
        for h in range(N_HEADS):
            sl = slice(h * HEAD_PAD, (h + 1) * HEAD_PAD)
            qh, rqh = _rms(q[:, sl], QK_HEAD)
            dy = _rope_bwd(dq_ref[:, sl], c, s1, s2)
            dqg = dqg + _colsum(dy * qh)
            dqp_ref[:, sl] = _rms_bwd(dy, qh, rqh, qg_ref[...], QK_HEAD).astype(BF16)
            kh, rkh = _rms(kn[:, sl] + krp, QK_HEAD)
            dyk = _rope_bwd(dk_ref[:, sl], c, s1, s2)
            dkg = dkg + _colsum(dyk * kh)
            dkh = _rms_bwd(dyk, kh, rkh, kg_ref[...], QK_HEAD)
            dkv_ref[:, sl] = dkh.astype(BF16)
            dkr = dkr + jnp.where(rope_lanes, dkh, 0.0)
        dkv_ref[:, QP_COLS:] = dv_ref[...].astype(BF16)
        dp_ref[:, OFF_CKV + 2 * D_RNN:] = dkr.astype(BF16)
        dqa = _dot(dqp_ref[...], wuq_ref[...])
        dp_ref[:, :Q_LORA] = _rms_bwd(dqa, xq, rq, gqa_ref[...], Q_LORA).astype(BF16)
        dkva = _dot_nt(dkv_ref[:, :QP_COLS], wuk_ref[...]) + _dot_nt(dkv_ref[:, QP_COLS:], wv_ref[...])
        dp_ref[:, Q_LORA:OFF_CKV] = _rms_bwd(dkva, xkv, rkv, gkva_ref[...], KV_LORA).astype(BF16)
        _acc(dqg_ref, first, dqg)
        _acc(dkg_ref, first, dkg)
        _acc(dgqa_ref, first, _colsum(dqa * xq))
        _acc(dgkva_ref, first, _colsum(dkva * xkv))
        finish()

    def row(w):
        return pl.BlockSpec((tm, w), lambda i: (i, 0))

    def acc(w):
        return pl.BlockSpec((1, w), lambda i: (0, 0))

    res = pl.pallas_call(
        body, name="qkv_bwd", grid=(n // tm,),
        in_specs=[row(Q_LORA), row(KV_LORA), row(LANES), row(QP_COLS), row(QP_COLS), row(D_ATTN), row(D_RNN), row(D_RNN),
                  _const_spec((1, Q_LORA)), _const_spec((1, KV_LORA)),
                  _const_spec((QP_COLS, Q_LORA)), _const_spec((KV_LORA, QP_COLS)), _const_spec((KV_LORA, D_ATTN)),
                  _const_spec((1, LANES)), _const_spec((1, LANES)), row(LANES), row(LANES), row(LANES)] + c_in,
        out_specs=[row(P_COLS), row(Q_LORA), row(KV_LORA), row(QP_COLS),
                   row(QP_COLS + D_ATTN), acc(LANES), acc(LANES), acc(Q_LORA), acc(KV_LORA)] + c_out,
        out_shape=[jax.ShapeDtypeStruct((n, P_COLS), BF16), jax.ShapeDtypeStruct((n, Q_LORA), BF16),
                   jax.ShapeDtypeStruct((n, KV_LORA), BF16), jax.ShapeDtypeStruct((n, QP_COLS), BF16),
                   jax.ShapeDtypeStruct((n, QP_COLS + D_ATTN), BF16),
                   jax.ShapeDtypeStruct((1, LANES), F32), jax.ShapeDtypeStruct((1, LANES), F32),
                   jax.ShapeDtypeStruct((1, Q_LORA), F32), jax.ShapeDtypeStruct((1, KV_LORA), F32)] + c_shape,
        scratch_shapes=c_sems,
        compiler_params=pltpu.CompilerParams(dimension_semantics=("arbitrary",), vmem_limit_bytes=VMEM_LIMIT),
    )(cq, ckv, kr, dq_r, dk_r, dv, dxr, dxg, gqa, gkva, w_uq_p, w_uk_p, w_v, qg, kg, rc, rs1, rs2, *srcs)
    return res[:9], res[9:]


KEY_CHUNK = 4 * LANES


def _key_chunks(t):
    count = max(t // KEY_CHUNK, 1)
    first = t - KEY_CHUNK * (count - 1)
    return [(0, first)] + [(first + KEY_CHUNK * c, KEY_CHUNK) for c in range(count - 1)]


def _attn_specs(t, tq):
    nq = t // tq
    qspec = pl.BlockSpec((tq, 2 * HEAD_PAD), lambda b, hp, i: (b * nq + i, hp))
    kspec = pl.BlockSpec((t, 2 * HEAD_PAD), lambda b, hp, i: (b, hp))
    vspec = pl.BlockSpec((t, 2 * V_HEAD), lambda b, hp, i: (b, hp))
    ospec = pl.BlockSpec((tq, 2 * V_HEAD), lambda b, hp, i: (b * nq + i, hp))
    return nq, qspec, kspec, vspec, ospec


def _probs_spec(t, tq):
    return pl.BlockSpec((1, 2, tq, t), lambda b, hp, i: (b, hp, i, 0))


def _attn_fwd(q, k, v, srcs=(), scatter=()):
    n = q.shape[0]
    t = _t_pad()
    tq = t // 2
    nq, qspec, kspec, vspec, ospec = _attn_specs(t, tq)
    nk = len(srcs)
    c_in, c_out, c_shape, c_sems = _exchange_specs(srcs, scatter)

    def body(q_ref, k_ref, v_ref, *rest):
        o_ref, l_ref, p_ref = rest[nk:nk + 3]
        finish = _ride(3, *_exchange_fns(rest[:nk], rest[nk + 3:2 * nk + 3], rest[2 * nk + 3:], scatter))
        lane = lax.broadcasted_iota(jnp.int32, (tq, 2 * V_HEAD), 1)
        outs = []
        sums = []
        for j in range(2):
            sl = slice(j * HEAD_PAD, (j + 1) * HEAD_PAD)
            qh = q_ref[:, sl]

            def scores(start, size):
                s = _dot_nt(qh, k_ref[start:start + size, sl])
                if start < PAD_ROWS:
                    key = lax.broadcasted_iota(jnp.int32, (tq, size), 1) + start
                    s = jnp.where(key >= PAD_ROWS, s, -jnp.inf)
                return s

            top = functools.reduce(jnp.maximum, [jnp.max(scores(*c), axis=-1, keepdims=True) for c in _key_chunks(t)])
            l = jnp.zeros((tq, 1), F32)
            pv = jnp.zeros((tq, 2 * V_HEAD), F32)
            for start, size in _key_chunks(t):
                e = jnp.exp2((scores(start, size) - top) * (QK_HEAD ** -0.5 * math.log2(math.e)))
                l = l + jnp.sum(e, axis=-1, keepdims=True)
                e = e.astype(BF16)
                p_ref[0, j, :, start:start + size] = e
                pv = pv + _dot(e, v_ref[start:start + size, :])
            outs.append(pv / l)
            sums.append(l)
        o_ref[...] = jnp.where(lane < V_HEAD, outs[0], outs[1])
        l_ref[...] = jnp.where(lane < V_HEAD, sums[0], sums[1])
        finish()

    res = pl.pallas_call(
        body, name="attn_fwd", grid=(n // t, N_HEADS // 2, nq),
        in_specs=[qspec, kspec, vspec] + c_in, out_specs=[ospec, ospec, _probs_spec(t, tq)] + c_out,
        out_shape=[jax.ShapeDtypeStruct((n, D_ATTN), F32), jax.ShapeDtypeStruct((n, D_ATTN), F32),
                   jax.ShapeDtypeStruct((n // t, N_HEADS, t, t), BF16)] + c_shape,
        scratch_shapes=c_sems,
        compiler_params=pltpu.CompilerParams(dimension_semantics=("arbitrary", "arbitrary", "arbitrary"),
                                             vmem_limit_bytes=VMEM_LIMIT),
    )(q, k, v, *srcs)
    return res[0], (res[1], res[2]), res[3:]


def _attn_bwd(q, k, v, do, o, probs, srcs=(), scatter=()):
    n = q.shape[0]
    t = _t_pad()
    tq = t // 2
    nq, qspec, kspec, vspec, ospec = _attn_specs(t, tq)
    nk = len(srcs)
    c_in, c_out, c_shape, c_sems = _exchange_specs(srcs, scatter)

    def body(q_ref, k_ref, v_ref, do_ref, o_ref, l_ref, p_ref, *rest):
        dq_ref, dk_ref, dv_ref = rest[nk:nk + 3]
        finish = _ride(3, *_exchange_fns(rest[:nk], rest[nk + 3:2 * nk + 3], rest[2 * nk + 3:], scatter))

        @pl.when(pl.program_id(2) == 0)
        def _():
            dk_ref[...] = jnp.zeros_like(dk_ref)
            dv_ref[...] = jnp.zeros_like(dv_ref)

        lane = lax.broadcasted_iota(jnp.int32, (tq, 2 * V_HEAD), 1)
        do = do_ref[...]
        do_o = do * o_ref[...]
        chunks = _key_chunks(t)
        dvs = [None] * len(chunks)
        for j in range(2):
            sl = slice(j * HEAD_PAD, (j + 1) * HEAD_PAD)
            qh = q_ref[:, sl]
            in_head = (lane < V_HEAD) if j == 0 else (lane >= V_HEAD)
            inv_l = 1.0 / l_ref[:, j * V_HEAD:j * V_HEAD + 1]
            doh = jnp.where(in_head, do, 0.0).astype(BF16)
            doh_n = jnp.where(in_head, do * inv_l, 0.0).astype(BF16)
            delta = jnp.sum(jnp.where(in_head, do_o, 0.0), axis=-1, keepdims=True)
            row_scale = inv_l * (QK_HEAD ** -0.5)
            dq = jnp.zeros((tq, HEAD_PAD), F32)
            for c, (start, size) in enumerate(chunks):
                rows = slice(start, start + size)
                e = p_ref[0, j, :, rows]
                dp = _dot_nt(doh, v_ref[rows, :])
                ds = (e.astype(F32) * (dp - delta) * row_scale).astype(BF16)
                dq = dq + _dot(ds, k_ref[rows, sl])
                dk_ref[rows, sl] += _dot_tn(ds, qh)
                dvc = _dot_tn(e, doh_n)
                dvs[c] = dvc if dvs[c] is None else dvs[c] + dvc
            dq_ref[:, sl] = dq
        for (start, size), dvc in zip(chunks, dvs):
            dv_ref[start:start + size, :] += dvc
        finish()

    res = pl.pallas_call(
        body, name="attn_bwd", grid=(n // t, N_HEADS // 2, nq),
        in_specs=[qspec, kspec, vspec, ospec, ospec, ospec, _probs_spec(t, tq)] + c_in,
        out_specs=[qspec, kspec, vspec] + c_out,
        out_shape=[jax.ShapeDtypeStruct((n, QP_COLS), F32), jax.ShapeDtypeStruct((n, QP_COLS), F32),
                   jax.ShapeDtypeStruct((n, D_ATTN), F32)] + c_shape, scratch_shapes=c_sems,
        compiler_params=pltpu.CompilerParams(dimension_semantics=("arbitrary", "arbitrary", "arbitrary"),
                                             vmem_limit_bytes=VMEM_LIMIT),
    )(q, k, v, do, o, *probs, *srcs)
    return res[:3], res[3:]


SCAN_STEPS = 8


def _scan(chains, t):
    seg = t // 8
    rows = lax.broadcasted_iota(jnp.int32, (8, LANES), 0)

    def step(i, carry):
        carry = list(carry)
        for u in range(SCAN_STEPS):
            j = i * SCAN_STEPS + u
            for n, (a_ref, b_ref, h_ref, p_ref, reverse) in enumerate(chains):
                h, p = carry[n]
                idx = pl.ds(seg - 1 - j if reverse else j, 8, stride=seg)
                a = a_ref[idx, :]
                h = a * h + b_ref[idx, :]
                p = a * p
                h_ref[idx, :] = h
                p_ref[idx, :] = p
                carry[n] = (h, p)
        return tuple(carry)

    init = tuple((jnp.zeros((8, LANES), F32), jnp.ones((8, LANES), F32)) for _ in chains)
    ends = lax.fori_loop(0, seg // SCAN_STEPS, step, init)
    for (_, _, h_ref, p_ref, reverse), (b, a) in zip(chains, ends):
        for d in (1, 2, 4):
            if reverse:
                keep = rows < 8 - d
                a_n, b_n = pltpu.roll(a, 8 - d, 0), pltpu.roll(b, 8 - d, 0)
            else:
                keep = rows >= d
                a_n, b_n = pltpu.roll(a, d, 0), pltpu.roll(b, d, 0)
            b = a * jnp.where(keep, b_n, 0.0) + b
            a = a * jnp.where(keep, a_n, 1.0)
        for s in (range(7) if reverse else range(1, 8)):
            sl = slice(s * seg, (s + 1) * seg)
            carry_in = b[s + 1:s + 2, :] if reverse else b[s - 1:s, :]
            h_ref[sl, :] = h_ref[sl, :] + p_ref[sl, :] * carry_in


def _shift_rows(x, s, rows, t):
    if s == 0:
        return x
    rolled = pltpu.roll(x, s % t, 0)
    return jnp.where(rows >= s, rolled, 0.0) if s > 0 else jnp.where(rows < t + s, rolled, 0.0)


def _neg_expm1_twice(h, exp_2h):
    series = h * (-2.0 + h * (-2.0 + h * (-4.0 / 3 + h * (-2.0 / 3))))
    return jnp.where(h > -0.05, series, 1.0 - exp_2h)


def _sigmoid(x):
    return 0.5 * jnp.tanh(0.5 * x) + 0.5


def _gelu_parts(x):
    k = math.sqrt(2.0 / math.pi)
    th = jnp.tanh(k * (x + 0.044715 * x * x * x))
    g = 0.5 * x * (1.0 + th)
    dg = 0.5 * (1.0 + th) + 0.5 * x * (1.0 - th * th) * k * (1.0 + 3 * 0.044715 * x * x)
    return g, dg


def _lru_gates(xc, gates, lam_ref, valid, d):
    r = _sigmoid(gates[:, (2 * d) * LANES:(2 * d + 1) * LANES])
    i = _sigmoid(gates[:, (2 * d + 1) * LANES:(2 * d + 2) * LANES])
    neg_lam = -lam_ref[d:d + 1, :]
    sp = jnp.maximum(neg_lam, 0.0) + jnp.log1p(jnp.exp(-jnp.abs(neg_lam)))
    log_a = -LRU_C * r * sp
    a = jnp.exp(log_a)
    m = jnp.maximum(_neg_expm1_twice(log_a, a * a), 0.0)
    sq = jnp.sqrt(m)
    b = jnp.where(valid, sq * (i * xc), 0.0)
    return r, i, sp, a, m, sq, b


def _conv(xr, cw_ref, cb_ref, rows, t):
    return (cw_ref[0:1, :] * _shift_rows(xr, 2, rows, t) + cw_ref[1:2, :] * _shift_rows(xr, 1, rows, t)
            + cw_ref[2:3, :] * xr + cw_ref[3:4, :] * _shift_rows(xr, -1, rows, t) + cb_ref[...])


def _rnn_specs(t):
    seq = pl.BlockSpec((t, LANES), lambda cb, b: (b, cb))
    cw = pl.BlockSpec((4, LANES), lambda cb, b: (0, cb))
    vec1 = pl.BlockSpec((1, LANES), lambda cb, b: (0, cb))
    vec2 = pl.BlockSpec((2, LANES), lambda cb, b: (0, cb))
    wblk = pl.BlockSpec((1, LANES, 4 * LANES), lambda cb, b: (cb, 0, 0))
    gbias = pl.BlockSpec((1, 1, 4 * LANES), lambda cb, b: (cb, 0, 0))
    return seq, cw, vec1, vec2, wblk, gbias


def _rnn_fwd(xr, xg, conv_w, conv_b, wblk, gbias, lam):
    n = xr.shape[0]
    t = _t_pad()
    seq, cw, vec1, vec2, wspec, gspec = _rnn_specs(t)
    both = pl.BlockSpec((2, t, LANES), lambda cb, b: (0, b, cb))

    def body(xr_ref, xg_ref, cw_ref, cb_ref, w_ref, gb_ref, lam_ref,
             o_ref, xc_ref, r_ref, i_ref, q_ref, h_ref, a_s, b_s, p_s):
        rows = lax.broadcasted_iota(jnp.int32, (t, LANES), 0)
        valid = rows >= PAD_ROWS
        xc = _conv(xr_ref[...], cw_ref, cb_ref, rows, t)
        xc_ref[...] = xc
        gates = _dot(xc.astype(BF16), w_ref[0]) + gb_ref[0]
        for d in range(2):
            r_ref[d], i_ref[d], _, a_s[d], _, q_ref[d], b_s[d] = _lru_gates(xc, gates, lam_ref, valid, d)
        _scan([(a_s.at[d], b_s.at[d], h_ref.at[d], p_s.at[d], d == 1) for d in range(2)], t)
        g, _ = _gelu_parts(xg_ref[...])
        o_ref[...] = (h_ref[0] + h_ref[1]) * g

    stacked = jax.ShapeDtypeStruct((2, n, D_RNN), F32)
    res = pl.pallas_call(
        body, name="rnn_fwd", grid=(D_RNN // LANES, n // t),
        in_specs=[seq, seq, cw, vec1, wspec, gspec, vec2], out_specs=[seq, seq, both, both, both, both],
        out_shape=[jax.ShapeDtypeStruct((n, D_RNN), F32), jax.ShapeDtypeStruct((n, D_RNN), F32)] + [stacked] * 4,
        scratch_shapes=[pltpu.VMEM((2, t, LANES), F32)] * 3,
        compiler_params=pltpu.CompilerParams(dimension_semantics=("parallel", "parallel"), vmem_limit_bytes=VMEM_LIMIT),
    )(xr, xg, conv_w, conv_b, wblk, gbias, lam)
    return res[0], tuple(res[1:])


def _rnn_bwd(xr, xg, do, saved, conv_w, wblk, lam, srcs=(), scatter=()):
    n = xr.shape[0]
    t = _t_pad()
    seq, cw, vec1, vec2, wspec, gspec = _rnn_specs(t)
    nk = len(srcs)
    c_in, c_out, c_shape, c_sems = _exchange_specs(srcs, scatter)

    def body(xr_ref, xg_ref, do_ref, xc_ref, r_s, i_s, q_s, h_s, cw_ref, w_ref, lam_ref, *rest):
        dxr_ref, dxg_ref, dcw_ref, dcb_ref, dw_ref, dgb_ref, dlam_ref = rest[nk:nk + 7]
        a_s, b_s, l_s, p_s, back_s, dg_s = rest[2 * nk + 7 + len(c_sems):]
        finish = _ride(2, *_exchange_fns(rest[:nk], rest[nk + 7:2 * nk + 7], rest[2 * nk + 7:2 * nk + 7 + len(c_sems)],
                                         scatter))
        first = pl.program_id(1) == 0
        rows = lax.broadcasted_iota(jnp.int32, (t, LANES), 0)
        valid = rows >= PAD_ROWS
        xr = xr_ref[...]
        xc = xc_ref[...]
        xcb = xc.astype(BF16)
        g, dg = _gelu_parts(xg_ref[...])
        do = do_ref[...]
        dxg_ref[...] = do * (h_s[0] + h_s[1]) * dg
        b_s[...] = do * g
        sps = []
        for d in range(2):
            neg_lam = -lam_ref[d:d + 1, :]
            sps.append(jnp.maximum(neg_lam, 0.0) + jnp.log1p(jnp.exp(-jnp.abs(neg_lam))))
            a_s[d] = jnp.exp(-LRU_C * r_s[d] * sps[d])
            back_s[d] = _shift_rows(a_s[d], -1 if d == 0 else 1, rows, t)
        _scan([(back_s.at[d], b_s, l_s.at[d], p_s.at[d], d == 0) for d in range(2)], t)
        dxc = jnp.zeros((t, LANES), F32)
        dlams = []
        for d in range(2):
            r, i, sp, a, sq = r_s[d], i_s[d], sps[d], a_s[d], q_s[d]
            lam_t = l_s[d]
            da = lam_t * _shift_rows(h_s[d], 1 if d == 0 else -1, rows, t)
            lam_v = jnp.where(valid, lam_t, 0.0)
            dsq = lam_v * (i * xc)
            di = lam_v * sq * xc
            dxc = dxc + lam_v * sq * i
            dm = jnp.where(sq > 0.0, dsq * 0.5 / jnp.where(sq > 0.0, sq, 1.0), 0.0)
            dla = da * a - 2.0 * dm * a * a
            dr = dla * (-LRU_C) * sp
            dsp = _colsum(dla * (-LRU_C) * r)
            dlams.append(dsp * -jax.nn.sigmoid(-lam_ref[d:d + 1, :]))
            dg_s[:, (2 * d) * LANES:(2 * d + 1) * LANES] = (dr * r * (1.0 - r)).astype(BF16)
            dg_s[:, (2 * d + 1) * LANES:(2 * d + 2) * LANES] = (di * i * (1.0 - i)).astype(BF16)
        dgates = dg_s[...]
        dxc = dxc + _dot_nt(dgates, w_ref[0])
        taps = [_shift_rows(dxc, j - 2, rows, t) for j in range(4)]
        dxr_ref[...] = (cw_ref[0:1, :] * taps[0] + cw_ref[1:2, :] * taps[1] + cw_ref[2:3, :] * taps[2]
                        + cw_ref[3:4, :] * taps[3])
        dcw = jnp.concatenate([_colsum(tap * xr) for tap in taps], axis=0)
        _acc(dcw_ref, first, dcw)
        _acc(dcb_ref, first, _colsum(dxc))
        _acc(dw_ref, first, _dot_tn(xcb, dgates)[None])
        _acc(dgb_ref, first, _colsum(dgates.astype(F32))[None])
        _acc(dlam_ref, first, jnp.concatenate(dlams, axis=0))
        finish()

    both = pl.BlockSpec((2, t, LANES), lambda cb, b: (0, b, cb))
    pair = pltpu.VMEM((2, t, LANES), F32)
    res = pl.pallas_call(
        body, name="rnn_bwd", grid=(D_RNN // LANES, n // t),
        in_specs=[seq, seq, seq, seq, both, both, both, both, cw, wspec, vec2] + c_in,
        out_specs=[seq, seq, cw, vec1, wspec, gspec, vec2] + c_out,
        out_shape=[jax.ShapeDtypeStruct((n, D_RNN), F32), jax.ShapeDtypeStruct((n, D_RNN), F32),
                   jax.ShapeDtypeStruct((4, D_RNN), F32), jax.ShapeDtypeStruct((1, D_RNN), F32),
                   jax.ShapeDtypeStruct((D_RNN // LANES, LANES, 4 * LANES), F32),
                   jax.ShapeDtypeStruct((D_RNN // LANES, 1, 4 * LANES), F32), jax.ShapeDtypeStruct((2, D_RNN), F32)]
        + c_shape,
        scratch_shapes=c_sems + [pair, pltpu.VMEM((t, LANES), F32), pair, pair, pair, pltpu.VMEM((t, 4 * LANES), BF16)],
        compiler_params=pltpu.CompilerParams(dimension_semantics=("arbitrary", "arbitrary"), vmem_limit_bytes=VMEM_LIMIT),
    )(xr, xg, do, *saved, conv_w, wblk, lam, *srcs)
    return res[:7], res[7:]


def _post(oa, orn, h0, tgt, ga, gr, g2, w_out, w_gate, w_up, w_down):
    n = oa.shape[0]
    tm = _row_tile(n)
    t = _t_pad()
    head = PAD_ROWS + N_META
    parts = tm // head

    def body(oa_ref, or_ref, h0_ref, *rest):
        tgt_refs = rest[:parts]
        (ga_ref, gr_ref, g2_ref, wo_ref, wg_ref, wu_ref, wd_ref,
         doa_ref, dor_ref, dh1_ref, mix_ref, h1n_ref, act_ref, dgate_ref, dup_ref, dy_ref,
         loss_ref, dga_ref, dgr_ref, dg2_ref, gate_s, up_s) = rest[parts:]
        first = pl.program_id(0) == 0
        xa, ra = _rms(oa_ref[...], D_ATTN)
        xr, rr = _rms(or_ref[...], D_RNN)
        mix = jnp.concatenate([(xa * ga_ref[...]).astype(BF16), (xr * gr_ref[...]).astype(BF16)], axis=-1)
        mix_ref[...] = mix.T
        h1 = h0_ref[...] + _dot(mix, wo_ref[...])
        x2, r2 = _rms(h1, D_MODEL)
        h1n = (x2 * g2_ref[...]).astype(BF16)
        h1n_ref[...] = h1n
        y = h1
        for cs in range(0, D_FF, FF_CHUNK):
            sl = slice(cs, cs + FF_CHUNK)
            gate = _dot_nt(h1n, wg_ref[sl, :])
            up = _dot_nt(h1n, wu_ref[sl, :])
            gate_s[:, sl] = gate
            up_s[:, sl] = up
            act = (gate * _sigmoid(gate) * up).astype(BF16)
            act_ref[sl, :] = act.T
            y = y + _dot(act, wd_ref[sl, :])
        row = pl.program_id(0) * tm + lax.broadcasted_iota(jnp.int32, (tm, 1), 0)
        for _ in range(1, n // t):
            row = jnp.where(row >= t, row - t, row)
        tgt = jnp.concatenate([ref[0] for ref in tgt_refs], axis=0)
        err = jnp.where(row >= PAD_ROWS + N_META, y - tgt, 0.0)
        _acc(loss_ref, first, jnp.full((1, LANES), 0.5 / D_MODEL, F32) * jnp.sum(err * err))
        dy = err * (1.0 / D_MODEL)
        dyb = dy.astype(BF16)
        dy_ref[...] = dyb
        dh1n = jnp.zeros((tm, D_MODEL), F32)
        for cs in range(0, D_FF, FF_CHUNK):
            sl = slice(cs, cs + FF_CHUNK)
            dact = _dot_nt(dyb, wd_ref[sl, :])
            gate, up = gate_s[:, sl], up_s[:, sl]
            sg = _sigmoid(gate)
            dgate = (dact * up * sg * (1.0 + gate * (1.0 - sg))).astype(BF16)
            dup = (dact * gate * sg).astype(BF16)
            dgate_ref[sl, :] = dgate.T
            dup_ref[sl, :] = dup.T
            dh1n = dh1n + _dot(dgate, wg_ref[sl, :]) + _dot(dup, wu_ref[sl, :])
        _acc(dg2_ref, first, _colsum(dh1n * x2))
        dh1 = dy + _rms_bwd(dh1n, x2, r2, g2_ref[...], D_MODEL)
        dh1_ref[...] = dh1
        dmix = _dot_nt(dh1.astype(BF16), wo_ref[...])
        dma, dmr = dmix[:, :D_ATTN], dmix[:, D_ATTN:]
        _acc(dga_ref, first, _colsum(dma * xa))
        _acc(dgr_ref, first, _colsum(dmr * xr))
        doa_ref[...] = _rms_bwd(dma, xa, ra, ga_ref[...], D_ATTN)
        dor_ref[...] = _rms_bwd(dmr, xr, rr, gr_ref[...], D_RNN)

    def row(w):
        return pl.BlockSpec((tm, w), lambda i: (i, 0))

    def acc(w):
        return pl.BlockSpec((1, w), lambda i: (0, 0))

    def col(w):
        return pl.BlockSpec((w, tm), lambda i: (0, i))

    outs = [(D_ATTN, F32, row), (D_RNN, F32, row), (D_MODEL, F32, row), (D_MODEL, BF16, col), (D_MODEL, BF16, row),
            (D_FF, BF16, col), (D_FF, BF16, col), (D_FF, BF16, col), (D_MODEL, BF16, row)]
    accs = [LANES, D_ATTN, D_RNN, D_MODEL]
    per = t // head

    def target_part(p):
        def index(i):
            block = i * parts + p
            return block // per, jnp.maximum(block % per - 1, 0), 0
        return pl.BlockSpec((1, head, D_MODEL), index)

    return pl.pallas_call(
        body, name="post", grid=(n // tm,),
        in_specs=[row(D_ATTN), row(D_RNN), row(D_MODEL)] + [target_part(p) for p in range(parts)] + [
                  _const_spec((1, D_ATTN)), _const_spec((1, D_RNN)), _const_spec((1, D_MODEL)),
                  _const_spec((D_MODEL, D_MODEL)), _const_spec((D_FF, D_MODEL)), _const_spec((D_FF, D_MODEL)),
                  _const_spec((D_FF, D_MODEL))],
        out_specs=[spec(w) for w, _, spec in outs] + [acc(w) for w in accs],
        out_shape=[jax.ShapeDtypeStruct((n, w) if spec is row else (w, n), dt) for w, dt, spec in outs]
        + [jax.ShapeDtypeStruct((1, w), F32) for w in accs],
        scratch_shapes=[pltpu.VMEM((tm, D_FF), F32), pltpu.VMEM((tm, D_FF), F32)],
        compiler_params=pltpu.CompilerParams(dimension_semantics=("arbitrary",), vmem_limit_bytes=VMEM_LIMIT),
    )(oa, orn, h0, *[tgt] * parts, ga, gr, g2, w_out, w_gate, w_up, w_down)


def _in_bwd(dp, h0, dh1, ln1_g, w_in_p, srcs=(), scatter=()):
    n = h0.shape[0]
    tm = _row_tile(n)
    nk = len(srcs)
    c_in, c_out, c_shape, c_sems = _exchange_specs(srcs, scatter)

    def body(dp_ref, h0_ref, dh1_ref, g_ref, w_ref, *rest):
        dh0_ref, dg_ref = rest[nk:nk + 2]
        finish = _ride(1, *_exchange_fns(rest[:nk], rest[nk + 2:2 * nk + 2], rest[2 * nk + 2:], scatter))
        dhn = _dot(dp_ref[...], w_ref[...])
        xhat, r = _rms(h0_ref[...], D_MODEL)
        _acc(dg_ref, pl.program_id(0) == 0, _colsum(dhn * xhat))
        dh0_ref[...] = dh1_ref[...] + _rms_bwd(dhn, xhat, r, g_ref[...], D_MODEL)
        finish()

    def row(w):
        return pl.BlockSpec((tm, w), lambda i: (i, 0))

    res = pl.pallas_call(
        body, name="in_bwd", grid=(n // tm,),
        in_specs=[row(P_COLS), row(D_MODEL), row(D_MODEL), _const_spec((1, D_MODEL)), _const_spec((P_COLS, D_MODEL))] + c_in,
        out_specs=[row(D_MODEL), pl.BlockSpec((1, D_MODEL), lambda i: (0, 0))] + c_out,
        out_shape=[jax.ShapeDtypeStruct((n, D_MODEL), F32), jax.ShapeDtypeStruct((1, D_MODEL), F32)] + c_shape,
        scratch_shapes=c_sems,
        compiler_params=pltpu.CompilerParams(dimension_semantics=("arbitrary",), vmem_limit_bytes=VMEM_LIMIT),
    )(dp, h0, dh1, ln1_g, w_in_p, *srcs)
    return res[:2], res[2:]


MAX_TILE = D_FF // 2


def _pick_tile(width, cap):
    best = LANES
    for mult in range(1, width // LANES + 1):
        cand = mult * LANES
        if width % cand == 0 and cand <= cap:
            best = cand
    return best


def _matmul_tn(name, a, b, srcs=(), scatter=()):
    n, ka = a.shape
    kb = b.shape[1]
    ta, tb = _pick_tile(ka, MAX_TILE), _pick_tile(kb, MAX_TILE)
    tk = n // 2
    nk = len(srcs)
    c_in, c_out, c_shape, c_sems = _exchange_specs(srcs, scatter)

    def body(a_ref, b_ref, *rest):
        o_ref = rest[nk]
        finish = _ride(3, *_exchange_fns(rest[:nk], rest[nk + 1:2 * nk + 1], rest[2 * nk + 1:], scatter))
        _acc(o_ref, pl.program_id(2) == 0, _dot_tn(a_ref[...].astype(BF16), b_ref[...].astype(BF16)))
        finish()

    res = pl.pallas_call(
        body, name=name, grid=(ka // ta, kb // tb, n // tk),
        in_specs=[pl.BlockSpec((tk, ta), lambda i, j, k: (k, i)), pl.BlockSpec((tk, tb), lambda i, j, k: (k, j))] + c_in,
        out_specs=[pl.BlockSpec((ta, tb), lambda i, j, k: (i, j))] + c_out,
        out_shape=[jax.ShapeDtypeStruct((ka, kb), F32)] + c_shape, scratch_shapes=c_sems,
        compiler_params=pltpu.CompilerParams(dimension_semantics=("arbitrary", "arbitrary", "arbitrary"),
                                             vmem_limit_bytes=VMEM_LIMIT),
    )(a, b, *srcs)
    return res[0], res[1:]


def _matmul_shards(name, at, b):
    ka, n = at.shape
    kb = b.shape[1]
    width = ka // N_DEV
    per = 2 if 2 * width >= 4 * LANES else 4
    ta = per * width

    def body(a_ref, b_ref, o_ref):
        out = _dot(a_ref[...], b_ref[...].astype(BF16))
        for s in range(per):
            o_ref[s] = out[s * width:(s + 1) * width, :].astype(BF16)

    return pl.pallas_call(
        body, name=name, grid=(ka // ta,),
        in_specs=[pl.BlockSpec((ta, n), lambda i: (i, 0)), _const_spec((n, kb))],
        out_specs=pl.BlockSpec((per, width, kb), lambda i: (i, 0, 0)),
        out_shape=jax.ShapeDtypeStruct((N_DEV, width, kb), BF16),
        compiler_params=pltpu.CompilerParams(dimension_semantics=("parallel",), vmem_limit_bytes=VMEM_LIMIT),
    )(at, b)


def _adamw_math(g8_ref, w_ref, m_ref, v_ref, g_ref, d_ref, nm_ref, nv_ref):
    g = g8_ref[0].astype(F32)
    for s in range(1, N_DEV):
        g = g + g8_ref[s].astype(F32)
    g_ref[...] = g
    nm = ADAM_B1 * m_ref[...] + (1.0 - ADAM_B1) * g
    nv = ADAM_B2 * v_ref[...] + (1.0 - ADAM_B2) * (g * g)
    nm_ref[...] = nm
    nv_ref[...] = nv
    m_hat = nm / (1.0 - ADAM_B1 ** ADAM_STEP)
    v_hat = nv / (1.0 - ADAM_B2 ** ADAM_STEP)
    d_ref[...] = -ADAM_LR * (m_hat / (jnp.sqrt(v_hat) + ADAM_EPS) + ADAM_WD * w_ref[...])


def _adamw_many(name, items):
    count = len(items)

    def body(*refs):
        ins, outs = refs[:4 * count], refs[4 * count:]
        for i in range(count):
            _adamw_math(*ins[4 * i:4 * i + 4], *outs[4 * i:4 * i + 4])

    flat = [a for item in items for a in item]
    res = pl.pallas_call(
        body, name=name,
        out_shape=[jax.ShapeDtypeStruct(item[1].shape, F32) for item in items for _ in range(4)],
        compiler_params=pltpu.CompilerParams(vmem_limit_bytes=VMEM_LIMIT),
    )(*flat)
    return [tuple(res[4 * i:4 * i + 4]) for i in range(count)]


def _adamw(name, g8, w, m, v):
    rows, cols = w.shape
    tr = rows
    for cand in (256, 176, 128, 64):
        if rows % cand == 0 and rows > cand:
            tr = cand
            break

    def body(*refs):
        _adamw_math(*refs)

    blk = pl.BlockSpec((tr, cols), lambda i: (i, 0))
    return pl.pallas_call(
        body, name=name, grid=(rows // tr,),
        in_specs=[pl.BlockSpec((N_DEV, tr, cols), lambda i: (0, i, 0)), blk, blk, blk],
        out_specs=[blk] * 4, out_shape=[jax.ShapeDtypeStruct((rows, cols), F32)] * 4,
        compiler_params=pltpu.CompilerParams(dimension_semantics=("parallel",), vmem_limit_bytes=VMEM_LIMIT),
    )(g8, w, m, v)


def _exchange_specs(srcs, scatter):
    nk = len(srcs)
    if not nk:
        return [], [], [], []
    any_spec = pl.BlockSpec(memory_space=pl.ANY)
    out_shape = [jax.ShapeDtypeStruct(s.shape if sc else (N_DEV,) + s.shape, s.dtype) for s, sc in zip(srcs, scatter)]
    sems = [pltpu.SemaphoreType.DMA((nk, N_DEV - 1)), pltpu.SemaphoreType.DMA((nk, N_DEV - 1)),
            pltpu.SemaphoreType.DMA((nk,))]
    return [any_spec] * nk, [any_spec] * nk, out_shape, sems


FLIPS = ((0, 0, 1), (1, 0, 0), (0, 1, 0), (1, 1, 0), (1, 0, 1), (0, 1, 1), (1, 1, 1))
N_CHIP_PEERS = 3


def _exchange_fns(src_refs, out_refs, sems, scatter):
    nk = len(src_refs)
    if not nk:
        return (lambda: None), (lambda: None), (lambda: None)
    send_sems, recv_sems, local_sems = sems
    first = 1 + N_CHIP_PEERS

    def plan():
        x, y, c = lax.axis_index("x"), lax.axis_index("y"), lax.axis_index("c")
        me = 4 * x + 2 * y + c
        peers = [(1 - x if fx else x, 1 - y if fy else y, 1 - c if fc else c) for fx, fy, fc in FLIPS]
        pids = [4 * px + 2 * py + pc for px, py, pc in peers]

        def remote(k, j, src, dst, to):
            return pltpu.make_async_remote_copy(src_ref=src, dst_ref=dst, send_sem=send_sems.at[k, j],
                                                recv_sem=recv_sems.at[k, j], device_id=to, device_id_type=MESH)

        def mine(k, dest):
            return src_refs[k].at[dest] if scatter[k] else src_refs[k]

        local = [pltpu.make_async_copy(mine(k, me), out_refs[k].at[me], local_sems.at[k]) for k in range(nk)]
        direct = [remote(k, j, mine(k, pids[j]), out_refs[k].at[me], peers[j])
                  for k in range(nk) for j in range(len(FLIPS) if scatter[k] else first)]
        relays = {(k, j): remote(k, j, out_refs[k].at[pids[j - N_CHIP_PEERS]], out_refs[k].at[pids[j - N_CHIP_PEERS]], peers[0])
                  for k in range(nk) if not scatter[k] for j in range(first, len(FLIPS))}
        arrivals = {(k, j): remote(k, j, out_refs[k].at[pids[j]], out_refs[k].at[pids[j]], peers[j])
                    for k in range(nk) for j in range(len(FLIPS))}
        return local, direct, relays, arrivals

    def start():
        local, direct, _, _ = plan()
        for cp in local + direct:
            cp.start()

    def relay():
        _, _, relays, arrivals = plan()
        for (k, j), cp in relays.items():
            arrivals[k, j - N_CHIP_PEERS].wait_recv()
            cp.start()

    def wait():
        local, direct, relays, arrivals = plan()
        for (k, j), cp in arrivals.items():
            if (k, j + N_CHIP_PEERS) not in relays:
                cp.wait_recv()
        for cp in direct + list(relays.values()):
            cp.wait_send()
        for cp in local:
            cp.wait()

    return start, relay, wait


def _grid_step(rank):
    step, total = 0, 1
    for axis in range(rank):
        step = step * pl.num_programs(axis) + pl.program_id(axis)
        total = total * pl.num_programs(axis)
    return step, total


def _ride(rank, start, relay, wait):
    step, total = _grid_step(rank)
    pl.when(step == 0)(start)
    pl.when(step == (3 * total) // 4)(relay)
    return lambda: pl.when(step == total - 1)(wait)


def _exchange(name, srcs, scatter):
    nk = len(srcs)
    c_in, c_out, c_shape, c_sems = _exchange_specs(srcs, scatter)

    def body(*refs):
        start, relay, wait = _exchange_fns(refs[:nk], refs[nk:2 * nk], refs[2 * nk:], scatter)
        start()
        relay()
        wait()

    return pl.pallas_call(body, name=name, in_specs=c_in, out_specs=c_out, out_shape=c_shape, scratch_shapes=c_sems)(*srcs)


def _cols_from_shards(g):
    return jnp.transpose(g, (1, 0, 2)).reshape(g.shape[1], -1)


def _cols_to_shards(w):
    return jnp.transpose(w.reshape(w.shape[0], N_DEV, -1), (1, 0, 2))


def _prep(x, srcs, scatter):
    nb = x.shape[0]
    t = _t_pad()
    head = PAD_ROWS + N_META
    nk = len(srcs)
    c_in, c_out, c_shape, c_sems = _exchange_specs(srcs, scatter)

    def body(x_ref, *rest):
        h0_ref = rest[nk]
        finish = _ride(1, *_exchange_fns(rest[:nk], rest[nk + 1:2 * nk + 1], rest[2 * nk + 1:], scatter))
        lead = pl.program_id(0) == 0

        @pl.when(lead)
        def _():
            h0_ref[...] = jnp.zeros_like(h0_ref)

        @pl.when(jnp.logical_not(lead))
        def _():
            h0_ref[...] = x_ref[...]

        finish()

    src = pl.BlockSpec((nb, head, D_MODEL), lambda j: (0, jnp.maximum(j - 1, 0), 0))
    dst = pl.BlockSpec((nb, head, D_MODEL), lambda j: (0, j, 0))
    res = pl.pallas_call(
        body, name="prep", grid=(t // head,), in_specs=[src] + c_in, out_specs=[dst] + c_out,
        out_shape=[jax.ShapeDtypeStruct((nb, t, D_MODEL), F32)] + c_shape, scratch_shapes=c_sems,
        compiler_params=pltpu.CompilerParams(dimension_semantics=("arbitrary",)),
    )(x, *srcs)
    return res[0], res[1:]


def _rope_tables(n):
    t = _t_pad()
    pos = np.arange(t, dtype=np.float32) - np.float32(PAD_ROWS)
    half = QK_ROPE // 2
    freqs = (1.0 / (ROPE_THETA ** (np.arange(half, dtype=np.float32) / half))).astype(np.float32)
    ang = pos[:, None] * freqs[None, :]
    cos, sin = np.cos(ang), np.sin(ang)
    z = lambda w: np.zeros((t, w), np.float32)
    c = np.concatenate([np.ones((t, QK_NOPE), np.float32), cos, cos, z(HEAD_PAD - QK_HEAD)], axis=1)
    s1 = np.concatenate([z(QK_NOPE + half), sin, z(HEAD_PAD - QK_HEAD)], axis=1)
    s2 = np.concatenate([z(QK_NOPE), -sin, z(HEAD_PAD - QK_NOPE - half)], axis=1)
    return tuple(jnp.asarray(np.tile(a, (n // t, 1))) for a in (c, s1, s2))


def _block_diag_gates(lru_wa, lru_wi):
    eye = jnp.eye(2, dtype=lru_wa.dtype)

    def bd(w):
        w = w.reshape(2, D_RNN // LANES, 2, RNN_BW, RNN_BW)
        full = w[:, :, :, :, None, :] * eye[None, None, :, None, :, None]
        return full.reshape(2, D_RNN // LANES, LANES, LANES)

    a, i = bd(lru_wa), bd(lru_wi)
    return jnp.concatenate([a[0], i[0], a[1], i[1]], axis=-1)


def _unblock_gates(dw):
    nb = D_RNN // LANES
    parts = dw.reshape(nb, 2, RNN_BW, 4, 2, RNN_BW)
    diag = jnp.stack([parts[:, k, :, :, k, :] for k in range(2)], axis=1)
    diag = jnp.transpose(diag, (3, 0, 1, 2, 4)).reshape(4, 2 * nb, RNN_BW, RNN_BW)
    return jnp.stack([diag[0], diag[2]]), jnp.stack([diag[1], diag[3]])


WEIGHTS = ("meta_tokens", "ln1_g", "w_in", "q_a_norm_g", "w_uq", "kv_a_norm_g", "w_ukv", "q_norm_g", "k_norm_g",
           "conv_w", "conv_b", "lru_wa", "lru_ba", "lru_wi", "lru_bi", "lru_lambda", "attn_out_g", "rnn_out_g",
           "w_out", "ln2_g", "w_gate", "w_up", "w_down")
BIG = ("w_in", "w_uq", "w_ukv", "w_out", "w_gate", "w_up", "w_down")
TRANSPOSED = ("w_in", "w_uq", "w_gate", "w_up")
ROW_SHARDED = ("w_out", "w_down") + TRANSPOSED
REPLICATED = ("ln1_g", "q_a_norm_g", "kv_a_norm_g", "q_norm_g", "k_norm_g", "conv_b", "lru_wa", "lru_wi",
              "attn_out_g", "rnn_out_g", "ln2_g")
WHOLE = REPLICATED + ("loss",)
G_FIRST = ("w_in", "meta_tokens")
G_MID = ("w_uq", "w_ukv", "conv_w", "lru_ba", "lru_bi", "lru_lambda")
LATE = ("w_out", "w_gate", "w_up", "w_down")
G_LAST = ("meta_tokens", "ln1_g")


def _local_step(x, tgt, ex):
    nb = x.shape[0]
    t = _t_pad()
    n = nb * t
    local = ex.local
    h0, got = _prep(x, *ex.gather_srcs(G_FIRST))
    first = ex.gathered(G_FIRST, got)
    meta, w_in = first["meta_tokens"], first["w_in"]
    h0 = h0.at[:, PAD_ROWS:PAD_ROWS + N_META].set(jnp.broadcast_to(meta[None], (nb, N_META, D_MODEL))).reshape(n, D_MODEL)

    zr = lambda r: jnp.zeros((r, D_MODEL), w_in.dtype)
    w_in_p = jnp.concatenate([w_in[:OFF_CKV], w_in[OFF_KR:], zr(QK_NOPE), w_in[OFF_CKV:OFF_KR], zr(HEAD_PAD - QK_HEAD)],
                             axis=0)
    pad_g = lambda g: jnp.pad(g, ((0, 0), (0, HEAD_PAD - QK_HEAD)))
    qg, kg = pad_g(local["q_norm_g"]), pad_g(local["k_norm_g"])
    rc, rs1, rs2 = _rope_tables(n)
    wblk = _block_diag_gates(local["lru_wa"].reshape(2, -1, RNN_BW, RNN_BW),
                             local["lru_wi"].reshape(2, -1, RNN_BW, RNN_BW)).astype(BF16)
    nblk = D_RNN // LANES

    (hn, cq, ckv, xr, xg, kr), got = _in_proj(h0, local["ln1_g"], w_in_p, *ex.gather_srcs(G_MID))
    w = ex.gathered(G_MID, got)
    w_uq_p = jnp.pad(w["w_uq"].reshape(N_HEADS, QK_HEAD, Q_LORA), ((0, 0), (0, HEAD_PAD - QK_HEAD), (0, 0))
                     ).reshape(QP_COLS, Q_LORA)
    ukv = w["w_ukv"].reshape(KV_LORA, N_HEADS, QK_NOPE + V_HEAD)
    w_uk_p = jnp.pad(ukv[:, :, :QK_NOPE], ((0, 0), (0, 0), (0, HEAD_PAD - QK_NOPE))).reshape(KV_LORA, QP_COLS)
    w_v = ukv[:, :, QK_NOPE:].reshape(KV_LORA, D_ATTN)
    gbias = jnp.stack([w["lru_ba"][0], w["lru_bi"][0], w["lru_ba"][1], w["lru_bi"][1]], axis=0)
    gbias = jnp.transpose(gbias.reshape(4, nblk, LANES), (1, 0, 2)).reshape(nblk, 1, 4 * LANES)

    q, k, v = _qkv_fwd(cq, ckv, kr, local["q_a_norm_g"], local["kv_a_norm_g"], w_uq_p, w_uk_p, w_v, qg, kg, rc, rs1, rs2)
    oa, probs, got = _attn_fwd(q, k, v, *ex.gather_srcs(LATE))
    late = ex.gathered(LATE, got)
    orn, rnn_saved = _rnn_fwd(xr, xg, w["conv_w"], local["conv_b"], wblk, gbias, w["lru_lambda"])
    (doa, dor, dh1, mix_t, h1n, act_t, dgate_t, dup_t, dyb, loss, dga, dgr, dg2) = _post(
        oa, orn, h0, tgt, local["attn_out_g"], local["rnn_out_g"], local["ln2_g"], late["w_out"], late["w_gate"],
        late["w_up"], late["w_down"])
    wire = {"w_out": _matmul_shards("dw_out", mix_t, dh1), "w_gate": _matmul_shards("dw_gate", dgate_t, h1n),
            "w_up": _matmul_shards("dw_up", dup_t, h1n), "w_down": _matmul_shards("dw_down", act_t, dyb)}
    names = ("w_gate",)
    (dxr, dxg, dcw, dcb, dwblk, dgb, dlam), got = _rnn_bwd(xr, xg, dor, rnn_saved, w["conv_w"], wblk, w["lru_lambda"],
                                                           *ex.scatter_srcs(names, wire))
    summed = ex.scattered(names, wire, got)
    dwa, dwi = _unblock_gates(dwblk)
    dgb = jnp.transpose(dgb.reshape(nblk, 4, LANES), (1, 0, 2)).reshape(4, D_RNN)
    wire = {k: wire[k] for k in ("w_out", "w_up", "w_down")} | ex.to_wire({
        "conv_w": dcw, "conv_b": dcb, "lru_wa": dwa.reshape(-1, RNN_BW), "lru_ba": jnp.stack([dgb[0], dgb[2]]),
        "lru_wi": dwi.reshape(-1, RNN_BW), "lru_bi": jnp.stack([dgb[1], dgb[3]]), "lru_lambda": dlam,
        "attn_out_g": dga, "rnn_out_g": dgr, "ln2_g": dg2, "loss": loss})
    names = tuple(wire)
    (dq_r, dk_r, dv), got = _attn_bwd(q, k, v, doa, oa, probs, *ex.scatter_srcs(names, wire))
    summed.update(ex.scattered(names, wire, got))
    (dp, qa, kva, dqp, dkv, dqg, dkg, dgqa, dgkva), _ = _qkv_bwd(
        cq, ckv, kr, dq_r, dk_r, dv, dxr, dxg, local["q_a_norm_g"], local["kv_a_norm_g"], w_uq_p, w_uk_p, w_v, qg, kg,
        rc, rs1, rs2)
    dw_uq_p, _ = _matmul_tn("dw_uq", dqp, qa)
    dw_kv, _ = _matmul_tn("dw_ukv", kva, dkv)
    dw_uq = dw_uq_p.reshape(N_HEADS, HEAD_PAD, Q_LORA)[:, :QK_HEAD].reshape(N_HEADS * QK_HEAD, Q_LORA)
    dw_ukv = jnp.concatenate([dw_kv[:, :QP_COLS].reshape(KV_LORA, N_HEADS, HEAD_PAD)[:, :, :QK_NOPE],
                              dw_kv[:, QP_COLS:].reshape(KV_LORA, N_HEADS, V_HEAD)], axis=2).reshape(KV_LORA, -1)
    wire = ex.to_wire({"q_a_norm_g": dgqa, "w_uq": dw_uq, "kv_a_norm_g": dgkva, "w_ukv": dw_ukv,
                       "q_norm_g": dqg[:, :QK_HEAD], "k_norm_g": dkg[:, :QK_HEAD]})
    names = tuple(wire)
    dw_in_p, got = _matmul_tn("dw_in", dp, hn, *ex.scatter_srcs(names, wire))
    summed.update(ex.scattered(names, wire, got))
    kr0 = OFF_CKV + 2 * D_RNN + QK_NOPE
    dw_in = jnp.concatenate([dw_in_p[:OFF_CKV], dw_in_p[kr0:kr0 + QK_ROPE], dw_in_p[OFF_CKV:OFF_CKV + 2 * D_RNN]], axis=0)
    wire = ex.to_wire({"w_in": dw_in})
    (dh0, dg1), got = _in_bwd(dp, h0, dh1, local["ln1_g"], w_in_p, *ex.scatter_srcs(("w_in",), wire))
    summed.update(ex.scattered(("w_in",), wire, got))

    dh0 = dh0.reshape(nb, t, D_MODEL)
    wire = ex.to_wire({"meta_tokens": jnp.sum(dh0[:, PAD_ROWS:PAD_ROWS + N_META], axis=0), "ln1_g": dg1})
    got = ex.run("reduce_last", *ex.scatter_srcs(G_LAST, wire))
    summed.update(ex.scattered(G_LAST, wire, got))
    return dh0[:, PAD_ROWS + N_META:], summed


class _MeshExchange:
    def __init__(self, shards):
        self.local = shards

    @staticmethod
    def run(name, srcs, scatter):
        return _exchange(name, srcs, scatter)

    def gather_srcs(self, names):
        return [self.local[k].astype(BF16) if k in BIG else self.local[k] for k in names], [False] * len(names)

    @staticmethod
    def gathered(names, outs):
        return {k: g.reshape(-1, g.shape[-1]) if k in ROW_SHARDED else _cols_from_shards(g) for k, g in zip(names, outs)}

    @staticmethod
    def to_wire(grads):
        wire = {}
        for k, g in grads.items():
            if k in WHOLE:
                wire[k] = g
            elif k in ROW_SHARDED:
                wire[k] = g.reshape(N_DEV, -1, g.shape[-1]).astype(BF16)
            else:
                wire[k] = _cols_to_shards(g).astype(BF16) if k in BIG else _cols_to_shards(g)
        return wire

    @staticmethod
    def scatter_srcs(names, wire):
        return [wire[k] for k in names], [k not in WHOLE for k in names]

    @staticmethod
    def scattered(names, wire, outs):
        return dict(zip(names, outs))


def kernel(x, meta_tokens, ln1_g, w_in, q_a_norm_g, w_uq, kv_a_norm_g, w_ukv, q_norm_g, k_norm_g, conv_w, conv_b, lru_wa, lru_ba, lru_wi, lru_bi, lru_lambda, attn_out_g, rnn_out_g, w_out, ln2_g, w_gate, w_up, w_down, loss_target, m_meta_tokens, m_ln1_g, m_w_in, m_q_a_norm_g, m_w_uq, m_kv_a_norm_g, m_w_ukv, m_q_norm_g, m_k_norm_g, m_conv_w, m_conv_b, m_lru_wa, m_lru_ba, m_lru_wi, m_lru_bi, m_lru_lambda, m_attn_out_g, m_rnn_out_g, m_w_out, m_ln2_g, m_w_gate, m_w_up, m_w_down, v_meta_tokens, v_ln1_g, v_w_in, v_q_a_norm_g, v_w_uq, v_kv_a_norm_g, v_w_ukv, v_q_norm_g, v_k_norm_g, v_conv_w, v_conv_b, v_lru_wa, v_lru_ba, v_lru_wi, v_lru_bi, v_lru_lambda, v_attn_out_g, v_rnn_out_g, v_w_out, v_ln2_g, v_w_gate, v_w_up, v_w_down):
    given = (meta_tokens, ln1_g, w_in, q_a_norm_g, w_uq, kv_a_norm_g, w_ukv, q_norm_g, k_norm_g, conv_w, conv_b,
             lru_wa, lru_ba, lru_wi, lru_bi, lru_lambda, attn_out_g, rnn_out_g, w_out, ln2_g, w_gate, w_up, w_down)
    moments_m = (m_meta_tokens, m_ln1_g, m_w_in, m_q_a_norm_g, m_w_uq, m_kv_a_norm_g, m_w_ukv, m_q_norm_g, m_k_norm_g,
                 m_conv_w, m_conv_b, m_lru_wa, m_lru_ba, m_lru_wi, m_lru_bi, m_lru_lambda, m_attn_out_g, m_rnn_out_g,
                 m_w_out, m_ln2_g, m_w_gate, m_w_up, m_w_down)
    moments_v = (v_meta_tokens, v_ln1_g, v_w_in, v_q_a_norm_g, v_w_uq, v_kv_a_norm_g, v_w_ukv, v_q_norm_g, v_k_norm_g,
                 v_conv_w, v_conv_b, v_lru_wa, v_lru_ba, v_lru_wi, v_lru_bi, v_lru_lambda, v_attn_out_g, v_rnn_out_g,
                 v_w_out, v_ln2_g, v_w_gate, v_w_up, v_w_down)
    shapes = {k: a.shape for k, a in zip(WEIGHTS, given)}

    def two_d(k, a):
        a = a.reshape(-1, a.shape[-1])
        return a.T if k in TRANSPOSED else a

    w = {k: two_d(k, a) for k, a in zip(WEIGHTS, given)}
    m = {k: two_d(k, a) for k, a in zip(WEIGHTS, moments_m)}
    v = {k: two_d(k, a) for k, a in zip(WEIGHTS, moments_v)}

    grad_x, parts = _local_step(x, loss_target, _MeshExchange(w))

    tiled = ("w_in", "w_gate", "w_up", "w_down")
    new = {k: _adamw("adamw_" + k, parts[k], w[k], m[k], v[k]) for k in tiled}
    small = [k for k in WEIGHTS if k not in tiled]
    new.update(zip(small, _adamw_many("adamw_small", [(parts[k], w[k], m[k], v[k]) for k in small])))

    loss = jnp.sum(parts["loss"][:, 0, 0])
    outs = [loss, grad_x]
    for idx in range(4):
        outs += [(new[k][idx].T if k in TRANSPOSED else new[k][idx]).reshape(shapes[k]) for k in WEIGHTS]
    return tuple(outs)
```

```python
import functools
import math

import numpy as np
import jax
import jax.numpy as jnp
from jax import lax
from jax.experimental import pallas as pl
from jax.experimental.pallas import tpu as pltpu

F32 = jnp.float32
BF16 = jnp.bfloat16

D_MODEL = 1024
N_META = 16
SEQ = 2048
N_HEADS = 8
QK_NOPE = 64
QK_ROPE = 32
QK_HEAD = QK_NOPE + QK_ROPE
V_HEAD = 64
D_ATTN = N_HEADS * V_HEAD
Q_LORA = 384
KV_LORA = 256
D_RNN = 512
RNN_BW = 64
D_FF = 2816
EPS = 1e-6
LRU_C = 8.0
ROPE_THETA = 10000.0
OFF_CKV = Q_LORA + KV_LORA
OFF_KR = OFF_CKV + QK_ROPE
IN_COLS = OFF_KR + 2 * D_RNN

ADAM_LR = 0.001
ADAM_B1 = 0.9
ADAM_B2 = 0.999
ADAM_EPS = 1e-08
ADAM_WD = 0.01
ADAM_STEP = 10

N_DEV = 8
LANES = 128
HEAD_PAD = LANES
PAD_ROWS = LANES - N_META
QP_COLS = N_HEADS * HEAD_PAD
P_COLS = OFF_CKV + 2 * D_RNN + LANES
FF_CHUNK = D_FF
VMEM_LIMIT = 56 * 1024 * 1024
MESH = pl.DeviceIdType.MESH


def _t_pad():
    return PAD_ROWS + N_META + SEQ


def _row_tile(n):
    return 256 if n % 256 == 0 else 128


def _wide_row_tile(n):
    quarter = _t_pad() // 4
    return quarter if quarter % 16 == 0 and n % quarter == 0 else _row_tile(n)


def _const_spec(shape):
    nd = len(shape)
    return pl.BlockSpec(shape, lambda *_: (0,) * nd, pipeline_mode=pl.Buffered(1))


def _rms(x, d):
    r = lax.rsqrt(jnp.sum(x * x, axis=-1, keepdims=True) * (1.0 / d) + EPS)
    return x * r, r


def _rms_bwd(dy, xhat, r, g, d):
    dxh = dy * g
    return r * (dxh - xhat * (jnp.sum(dxh * xhat, axis=-1, keepdims=True) * (1.0 / d)))


def _colsum(x):
    return jnp.sum(x, axis=0, keepdims=True)


def _dot(a, b):
    return jnp.dot(a, b, preferred_element_type=F32)


def _dot_nt(a, b):
    return lax.dot_general(a, b, (((1,), (1,)), ((), ())), preferred_element_type=F32)


def _dot_tn(a, b):
    return lax.dot_general(a, b, (((0,), (0,)), ((), ())), preferred_element_type=F32)


def _rope(x, c, s1, s2):
    return x * c + pltpu.roll(x, 16, 1) * s1 + pltpu.roll(x, HEAD_PAD - 16, 1) * s2


def _rope_bwd(dy, c, s1, s2):
    return dy * c + pltpu.roll(dy * s1, HEAD_PAD - 16, 1) + pltpu.roll(dy * s2, 16, 1)


def _acc(ref, first, val):
    @pl.when(first)
    def _():
        ref[...] = val

    @pl.when(jnp.logical_not(first))
    def _():
        ref[...] += val


def _in_proj(h0, ln1_g, w_in_p, srcs=(), scatter=()):
    n = h0.shape[0]
    tm = _wide_row_tile(n)
    nk = len(srcs)
    c_in, c_out, c_shape, c_sems = _exchange_specs(srcs, scatter)

    def body(h_ref, g_ref, w_ref, *rest):
        hn_ref, cq_ref, ckv_ref, xr_ref, xg_ref, kr_ref = rest[nk:nk + 6]
        finish = _ride(1, *_exchange_fns(rest[:nk], rest[nk + 6:2 * nk + 6], rest[2 * nk + 6:], scatter))
        xhat, _ = _rms(h_ref[...], D_MODEL)
        hn = (xhat * g_ref[...]).astype(BF16)
        hn_ref[...] = hn
        p = _dot_nt(hn, w_ref[...])
        cq_ref[...] = p[:, :Q_LORA]
        ckv_ref[...] = p[:, Q_LORA:OFF_CKV]
        xr_ref[...] = p[:, OFF_CKV:OFF_CKV + D_RNN]
        xg_ref[...] = p[:, OFF_CKV + D_RNN:OFF_CKV + 2 * D_RNN]
        kr_ref[...] = p[:, OFF_CKV + 2 * D_RNN:]
        finish()

    def row(w):
        return pl.BlockSpec((tm, w), lambda i: (i, 0))

    widths = (D_MODEL, Q_LORA, KV_LORA, D_RNN, D_RNN, LANES)
    res = pl.pallas_call(
        body, name="in_proj", grid=(n // tm,),
        in_specs=[row(D_MODEL), _const_spec((1, D_MODEL)), _const_spec((P_COLS, D_MODEL))] + c_in,
        out_specs=[row(w) for w in widths] + c_out,
        out_shape=[jax.ShapeDtypeStruct((n, w), BF16 if k == 0 else F32) for k, w in enumerate(widths)] + c_shape,
        scratch_shapes=c_sems,
        compiler_params=pltpu.CompilerParams(dimension_semantics=("arbitrary",), vmem_limit_bytes=VMEM_LIMIT),
    )(h0, ln1_g, w_in_p, *srcs)
    return res[:6], res[6:]


def _qkv_fwd(cq, ckv, kr, gqa, gkva, w_uq_p, w_uk_p, w_v, qg, kg, rc, rs1, rs2):
    n = cq.shape[0]
    tm = _wide_row_tile(n)

    def body(cq_ref, ckv_ref, kr_ref, gqa_ref, gkva_ref, wuq_ref, wuk_ref, wv_ref, qg_ref, kg_ref,
             c_ref, s1_ref, s2_ref, q_ref, k_ref, v_ref):
        xq, _ = _rms(cq_ref[...], Q_LORA)
        qa = (xq * gqa_ref[...]).astype(BF16)
        q = _dot_nt(qa, wuq_ref[...])
        xkv, _ = _rms(ckv_ref[...], KV_LORA)
        kva = (xkv * gkva_ref[...]).astype(BF16)
        kn = _dot(kva, wuk_ref[...])
        v_ref[...] = _dot(kva, wv_ref[...]).astype(BF16)
        krp = kr_ref[...]
        c, s1, s2 = c_ref[...], s1_ref[...], s2_ref[...]
        for h in range(N_HEADS):
            sl = slice(h * HEAD_PAD, (h + 1) * HEAD_PAD)
            qh, _ = _rms(q[:, sl], QK_HEAD)
            q_ref[:, sl] = _rope(qh * qg_ref[...], c, s1, s2).astype(BF16)
            kh, _ = _rms(kn[:, sl] + krp, QK_HEAD)
            k_ref[:, sl] = _rope(kh * kg_ref[...], c, s1, s2).astype(BF16)

    def row(w):
        return pl.BlockSpec((tm, w), lambda i: (i, 0))

    return pl.pallas_call(
        body, name="qkv_fwd", grid=(n // tm,),
        in_specs=[row(Q_LORA), row(KV_LORA), row(LANES), _const_spec((1, Q_LORA)), _const_spec((1, KV_LORA)),
                  _const_spec((QP_COLS, Q_LORA)), _const_spec((KV_LORA, QP_COLS)), _const_spec((KV_LORA, D_ATTN)),
                  _const_spec((1, LANES)), _const_spec((1, LANES)), row(LANES), row(LANES), row(LANES)],
        out_specs=[row(QP_COLS), row(QP_COLS), row(D_ATTN)],
        out_shape=[jax.ShapeDtypeStruct((n, QP_COLS), BF16), jax.ShapeDtypeStruct((n, QP_COLS), BF16),
                   jax.ShapeDtypeStruct((n, D_ATTN), BF16)],
        compiler_params=pltpu.CompilerParams(dimension_semantics=("parallel",), vmem_limit_bytes=VMEM_LIMIT),
    )(cq, ckv, kr, gqa, gkva, w_uq_p, w_uk_p, w_v, qg, kg, rc, rs1, rs2)


def _qkv_bwd(cq, ckv, kr, dq_r, dk_r, dv, dxr, dxg, gqa, gkva, w_uq_p, w_uk_p, w_v, qg, kg, rc, rs1, rs2,
             srcs=(), scatter=()):
    n = cq.shape[0]
    tm = _wide_row_tile(n)
    nk = len(srcs)
    c_in, c_out, c_shape, c_sems = _exchange_specs(srcs, scatter)

    def body(cq_ref, ckv_ref, kr_ref, dq_ref, dk_ref, dv_ref, dxr_ref, dxg_ref, gqa_ref, gkva_ref, wuq_ref, wuk_ref,
             wv_ref, qg_ref, kg_ref, c_ref, s1_ref, s2_ref, *rest):
        dp_ref, qa_ref, kva_ref, dqp_ref, dkv_ref, dqg_ref, dkg_ref, dgqa_ref, dgkva_ref = rest[nk:nk + 9]
        finish = _ride(1, *_exchange_fns(rest[:nk], rest[nk + 9:2 * nk + 9], rest[2 * nk + 9:], scatter))
        first = pl.program_id(0) == 0
        dp_ref[:, OFF_CKV:OFF_CKV + D_RNN] = dxr_ref[...].astype(BF16)
        dp_ref[:, OFF_CKV + D_RNN:OFF_CKV + 2 * D_RNN] = dxg_ref[...].astype(BF16)
        xq, rq = _rms(cq_ref[...], Q_LORA)
        qa = (xq * gqa_ref[...]).astype(BF16)
        qa_ref[...] = qa
        q = _dot_nt(qa, wuq_ref[...])
        xkv, rkv = _rms(ckv_ref[...], KV_LORA)
        kva = (xkv * gkva_ref[...]).astype(BF16)
        kva_ref[...] = kva
        kn = _dot(kva, wuk_ref[...])
        krp = kr_ref[...]
        c, s1, s2 = c_ref[...], s1_ref[...], s2_ref[...]
        lane = lax.broadcasted_iota(jnp.int32, (tm, HEAD_PAD), 1)
        rope_lanes = jnp.logical_and(lane >= QK_NOPE, lane < QK_HEAD)
        dqg = jnp.zeros((1, HEAD_PAD), F32)
        dkg = jnp.zeros((1, HEAD_PAD), F32)
        dkr = jnp.zeros((tm, HEAD_PAD), F32)
        for h in range(N_HEADS):
            sl = slice(h * HEAD_PAD, (h + 1) * HEAD_PAD)
            qh, rqh = _rms(q[:, sl], QK_HEAD)
            dy = _rope_bwd(dq_ref[:, sl], c, s1, s2)
            dqg = dqg + _colsum(dy * qh)
            dqp_ref[:, sl] = _rms_bwd(dy, qh, rqh, qg_ref[...], QK_HEAD).astype(BF16)
            kh, rkh = _rms(kn[:, sl] + krp, QK_HEAD)
            dyk = _rope_bwd(dk_ref[:, sl], c, s1, s2)
            dkg = dkg + _colsum(dyk * kh)
            dkh = _rms_bwd(dyk, kh, rkh, kg_ref[...], QK_HEAD)
            dkv_ref[:, sl] = dkh.astype(BF16)
            dkr = dkr + jnp.where(rope_lanes, dkh, 0.0)
        dkv_ref[:, QP_COLS:] = dv_ref[...].astype(BF16)
        dp_ref[:, OFF_CKV + 2 * D_RNN:] = dkr.astype(BF16)
        dqa = _dot(dqp_ref[...], wuq_ref[...])
        dp_ref[:, :Q_LORA] = _rms_bwd(dqa, xq, rq, gqa_ref[...], Q_LORA).astype(BF16)
        dkva = _dot_nt(dkv_ref[:, :QP_COLS], wuk_ref[...]) + _dot_nt(dkv_ref[:, QP_COLS:], wv_ref[...])
        dp_ref[:, Q_LORA:OFF_CKV] = _rms_bwd(dkva, xkv, rkv, gkva_ref[...], KV_LORA).astype(BF16)
        _acc(dqg_ref, first, dqg)
        _acc(dkg_ref, first, dkg)
        _acc(dgqa_ref, first, _colsum(dqa * xq))
        _acc(dgkva_ref, first, _colsum(dkva * xkv))
        finish()

    def row(w):
        return pl.BlockSpec((tm, w), lambda i: (i, 0))

    def acc(w):
        return pl.BlockSpec((1, w), lambda i: (0, 0))

    res = pl.pallas_call(
        body, name="qkv_bwd", grid=(n // tm,),
        in_specs=[row(Q_LORA), row(KV_LORA), row(LANES), row(QP_COLS), row(QP_COLS), row(D_ATTN), row(D_RNN), row(D_RNN),
                  _const_spec((1, Q_LORA)), _const_spec((1, KV_LORA)),
                  _const_spec((QP_COLS, Q_LORA)), _const_spec((KV_LORA, QP_COLS)), _const_spec((KV_LORA, D_ATTN)),
                  _const_spec((1, LANES)), _const_spec((1, LANES)), row(LANES), row(LANES), row(LANES)] + c_in,
        out_specs=[row(P_COLS), row(Q_LORA), row(KV_LORA), row(QP_COLS),
                   row(QP_COLS + D_ATTN), acc(LANES), acc(LANES), acc(Q_LORA), acc(KV_LORA)] + c_out,
        out_shape=[jax.ShapeDtypeStruct((n, P_COLS), BF16), jax.ShapeDtypeStruct((n, Q_LORA), BF16),
                   jax.ShapeDtypeStruct((n, KV_LORA), BF16), jax.ShapeDtypeStruct((n, QP_COLS), BF16),
                   jax.ShapeDtypeStruct((n, QP_COLS + D_ATTN), BF16),
                   jax.ShapeDtypeStruct((1, LANES), F32), jax.ShapeDtypeStruct((1, LANES), F32),
                   jax.ShapeDtypeStruct((1, Q_LORA), F32), jax.ShapeDtypeStruct((1, KV_LORA), F32)] + c_shape,
        scratch_shapes=c_sems,
        compiler_params=pltpu.CompilerParams(dimension_semantics=("arbitrary",), vmem_limit_bytes=VMEM_LIMIT),
    )(cq, ckv, kr, dq_r, dk_r, dv, dxr, dxg, gqa, gkva, w_uq_p, w_uk_p, w_v, qg, kg, rc, rs1, rs2, *srcs)
    return res[:9], res[9:]


KEY_CHUNK = 4 * LANES


def _key_chunks(t):
    count = max(t // KEY_CHUNK, 1)
    first = t - KEY_CHUNK * (count - 1)
    return [(0, first)] + [(first + KEY_CHUNK * c, KEY_CHUNK) for c in range(count - 1)]


def _attn_specs(t, tq):
    nq = t // tq
    qspec = pl.BlockSpec((tq, 2 * HEAD_PAD), lambda b, hp, i: (b * nq + i, hp))
    kspec = pl.BlockSpec((t, 2 * HEAD_PAD), lambda b, hp, i: (b, hp))
    vspec = pl.BlockSpec((t, 2 * V_HEAD), lambda b, hp, i: (b, hp))
    ospec = pl.BlockSpec((tq, 2 * V_HEAD), lambda b, hp, i: (b * nq + i, hp))
    return nq, qspec, kspec, vspec, ospec


def _probs_spec(t, tq):
    return pl.BlockSpec((1, 2, tq, t), lambda b, hp, i: (b, hp, i, 0))


def _attn_fwd(q, k, v, srcs=(), scatter=()):
    n = q.shape[0]
    t = _t_pad()
    tq = t // 2
    nq, qspec, kspec, vspec, ospec = _attn_specs(t, tq)
    nk = len(srcs)
    c_in, c_out, c_shape, c_sems = _exchange_specs(srcs, scatter)

    def body(q_ref, k_ref, v_ref, *rest):
        o_ref, l_ref, p_ref = rest[nk:nk + 3]
        finish = _ride(3, *_exchange_fns(rest[:nk], rest[nk + 3:2 * nk + 3], rest[2 * nk + 3:], scatter))
        lane = lax.broadcasted_iota(jnp.int32, (tq, 2 * V_HEAD), 1)
        outs = []
        sums = []
        for j in range(2):
            sl = slice(j * HEAD_PAD, (j + 1) * HEAD_PAD)
            qh = q_ref[:, sl]

            def scores(start, size):
                s = _dot_nt(qh, k_ref[start:start + size, sl])
                if start < PAD_ROWS:
                    key = lax.broadcasted_iota(jnp.int32, (tq, size), 1) + start
                    s = jnp.where(key >= PAD_ROWS, s, -jnp.inf)
                return s

            top = functools.reduce(jnp.maximum, [jnp.max(scores(*c), axis=-1, keepdims=True) for c in _key_chunks(t)])
            l = jnp.zeros((tq, 1), F32)
            pv = jnp.zeros((tq, 2 * V_HEAD), F32)
            for start, size in _key_chunks(t):
                e = jnp.exp2((scores(start, size) - top) * (QK_HEAD ** -0.5 * math.log2(math.e)))
                l = l + jnp.sum(e, axis=-1, keepdims=True)
                e = e.astype(BF16)
                p_ref[0, j, :, start:start + size] = e
                pv = pv + _dot(e, v_ref[start:start + size, :])
            outs.append(pv / l)
            sums.append(l)
        o_ref[...] = jnp.where(lane < V_HEAD, outs[0], outs[1])
        l_ref[...] = jnp.where(lane < V_HEAD, sums[0], sums[1])
        finish()

    res = pl.pallas_call(
        body, name="attn_fwd", grid=(n // t, N_HEADS // 2, nq),
        in_specs=[qspec, kspec, vspec] + c_in, out_specs=[ospec, ospec, _probs_spec(t, tq)] + c_out,
        out_shape=[jax.ShapeDtypeStruct((n, D_ATTN), F32), jax.ShapeDtypeStruct((n, D_ATTN), F32),
                   jax.ShapeDtypeStruct((n // t, N_HEADS, t, t), BF16)] + c_shape,
        scratch_shapes=c_sems,
        compiler_params=pltpu.CompilerParams(dimension_semantics=("arbitrary", "arbitrary", "arbitrary"),
                                             vmem_limit_bytes=VMEM_LIMIT),
    )(q, k, v, *srcs)
    return res[0], (res[1], res[2]), res[3:]


def _attn_bwd(q, k, v, do, o, probs, srcs=(), scatter=()):
    n = q.shape[0]
    t = _t_pad()
    tq = t // 2
    nq, qspec, kspec, vspec, ospec = _attn_specs(t, tq)
    nk = len(srcs)
    c_in, c_out, c_shape, c_sems = _exchange_specs(srcs, scatter)

    def body(q_ref, k_ref, v_ref, do_ref, o_ref, l_ref, p_ref, *rest):
        dq_ref, dk_ref, dv_ref = rest[nk:nk + 3]
        finish = _ride(3, *_exchange_fns(rest[:nk], rest[nk + 3:2 * nk + 3], rest[2 * nk + 3:], scatter))

        @pl.when(pl.program_id(2) == 0)
        def _():
            dk_ref[...] = jnp.zeros_like(dk_ref)
            dv_ref[...] = jnp.zeros_like(dv_ref)

        lane = lax.broadcasted_iota(jnp.int32, (tq, 2 * V_HEAD), 1)
        do = do_ref[...]
        do_o = do * o_ref[...]
        chunks = _key_chunks(t)
        dvs = [None] * len(chunks)
        for j in range(2):
            sl = slice(j * HEAD_PAD, (j + 1) * HEAD_PAD)
            qh = q_ref[:, sl]
            in_head = (lane < V_HEAD) if j == 0 else (lane >= V_HEAD)
            inv_l = 1.0 / l_ref[:, j * V_HEAD:j * V_HEAD + 1]
            doh = jnp.where(in_head, do, 0.0).astype(BF16)
            doh_n = jnp.where(in_head, do * inv_l, 0.0).astype(BF16)
            delta = jnp.sum(jnp.where(in_head, do_o, 0.0), axis=-1, keepdims=True)
            row_scale = inv_l * (QK_HEAD ** -0.5)
            dq = jnp.zeros((tq, HEAD_PAD), F32)
            for c, (start, size) in enumerate(chunks):
                rows = slice(start, start + size)
                e = p_ref[0, j, :, rows]
                dp = _dot_nt(doh, v_ref[rows, :])
                ds = (e.astype(F32) * (dp - delta) * row_scale).astype(BF16)
                dq = dq + _dot(ds, k_ref[rows, sl])
                dk_ref[rows, sl] += _dot_tn(ds, qh)
                dvc = _dot_tn(e, doh_n)
                dvs[c] = dvc if dvs[c] is None else dvs[c] + dvc
            dq_ref[:, sl] = dq
        for (start, size), dvc in zip(chunks, dvs):
            dv_ref[start:start + size, :] += dvc
        finish()

    res = pl.pallas_call(
        body, name="attn_bwd", grid=(n // t, N_HEADS // 2, nq),
        in_specs=[qspec, kspec, vspec, ospec, ospec, ospec, _probs_spec(t, tq)] + c_in,
        out_specs=[qspec, kspec, vspec] + c_out,
        out_shape=[jax.ShapeDtypeStruct((n, QP_COLS), F32), jax.ShapeDtypeStruct((n, QP_COLS), F32),
                   jax.ShapeDtypeStruct((n, D_ATTN), F32)] + c_shape, scratch_shapes=c_sems,
        compiler_params=pltpu.CompilerParams(dimension_semantics=("arbitrary", "arbitrary", "arbitrary"),
                                             vmem_limit_bytes=VMEM_LIMIT),
    )(q, k, v, do, o, *probs, *srcs)
    return res[:3], res[3:]


SCAN_STEPS = 8


def _scan(chains, t):
    seg = t // 8
    rows = lax.broadcasted_iota(jnp.int32, (8, LANES), 0)

    def step(i, carry):
        carry = list(carry)
        for u in range(SCAN_STEPS):
            j = i * SCAN_STEPS + u
            for n, (a_ref, b_ref, h_ref, p_ref, reverse) in enumerate(chains):
                h, p = carry[n]
                idx = pl.ds(seg - 1 - j if reverse else j, 8, stride=seg)
                a = a_ref[idx, :]
                h = a * h + b_ref[idx, :]
                p = a * p
                h_ref[idx, :] = h
                p_ref[idx, :] = p
                carry[n] = (h, p)
        return tuple(carry)

    init = tuple((jnp.zeros((8, LANES), F32), jnp.ones((8, LANES), F32)) for _ in chains)
    ends = lax.fori_loop(0, seg // SCAN_STEPS, step, init)
    for (_, _, h_ref, p_ref, reverse), (b, a) in zip(chains, ends):
        for d in (1, 2, 4):
            if reverse:
                keep = rows < 8 - d
                a_n, b_n = pltpu.roll(a, 8 - d, 0), pltpu.roll(b, 8 - d, 0)
            else:
                keep = rows >= d
                a_n, b_n = pltpu.roll(a, d, 0), pltpu.roll(b, d, 0)
            b = a * jnp.where(keep, b_n, 0.0) + b
            a = a * jnp.where(keep, a_n, 1.0)
        for s in (range(7) if reverse else range(1, 8)):
            sl = slice(s * seg, (s + 1) * seg)
            carry_in = b[s + 1:s + 2, :] if reverse else b[s - 1:s, :]
            h_ref[sl, :] = h_ref[sl, :] + p_ref[sl, :] * carry_in


def _shift_rows(x, s, rows, t):
    if s == 0:
        return x
    rolled = pltpu.roll(x, s % t, 0)
    return jnp.where(rows >= s, rolled, 0.0) if s > 0 else jnp.where(rows < t + s, rolled, 0.0)


def _neg_expm1_twice(h, exp_2h):
    series = h * (-2.0 + h * (-2.0 + h * (-4.0 / 3 + h * (-2.0 / 3))))
    return jnp.where(h > -0.05, series, 1.0 - exp_2h)


def _sigmoid(x):
    return 0.5 * jnp.tanh(0.5 * x) + 0.5


def _gelu_parts(x):
    k = math.sqrt(2.0 / math.pi)
    th = jnp.tanh(k * (x + 0.044715 * x * x * x))
    g = 0.5 * x * (1.0 + th)
    dg = 0.5 * (1.0 + th) + 0.5 * x * (1.0 - th * th) * k * (1.0 + 3 * 0.044715 * x * x)
    return g, dg


def _lru_gates(xc, gates, lam_ref, valid, d):
    r = _sigmoid(gates[:, (2 * d) * LANES:(2 * d + 1) * LANES])
    i = _sigmoid(gates[:, (2 * d + 1) * LANES:(2 * d + 2) * LANES])
    neg_lam = -lam_ref[d:d + 1, :]
    sp = jnp.maximum(neg_lam, 0.0) + jnp.log1p(jnp.exp(-jnp.abs(neg_lam)))
    log_a = -LRU_C * r * sp
    a = jnp.exp(log_a)
    m = jnp.maximum(_neg_expm1_twice(log_a, a * a), 0.0)
    sq = jnp.sqrt(m)
    b = jnp.where(valid, sq * (i * xc), 0.0)
    return r, i, sp, a, m, sq, b


def _conv(xr, cw_ref, cb_ref, rows, t):
    return (cw_ref[0:1, :] * _shift_rows(xr, 2, rows, t) + cw_ref[1:2, :] * _shift_rows(xr, 1, rows, t)
            + cw_ref[2:3, :] * xr + cw_ref[3:4, :] * _shift_rows(xr, -1, rows, t) + cb_ref[...])


def _rnn_specs(t):
    seq = pl.BlockSpec((t, LANES), lambda cb, b: (b, cb))
    cw = pl.BlockSpec((4, LANES), lambda cb, b: (0, cb))
    vec1 = pl.BlockSpec((1, LANES), lambda cb, b: (0, cb))
    vec2 = pl.BlockSpec((2, LANES), lambda cb, b: (0, cb))
    wblk = pl.BlockSpec((1, LANES, 4 * LANES), lambda cb, b: (cb, 0, 0))
    gbias = pl.BlockSpec((1, 1, 4 * LANES), lambda cb, b: (cb, 0, 0))
    return seq, cw, vec1, vec2, wblk, gbias


def _rnn_fwd(xr, xg, conv_w, conv_b, wblk, gbias, lam):
    n = xr.shape[0]
    t = _t_pad()
    seq, cw, vec1, vec2, wspec, gspec = _rnn_specs(t)
    both = pl.BlockSpec((2, t, LANES), lambda cb, b: (0, b, cb))

    def body(xr_ref, xg_ref, cw_ref, cb_ref, w_ref, gb_ref, lam_ref,
             o_ref, xc_ref, r_ref, i_ref, q_ref, h_ref, a_s, b_s, p_s):
        rows = lax.broadcasted_iota(jnp.int32, (t, LANES), 0)
        valid = rows >= PAD_ROWS
        xc = _conv(xr_ref[...], cw_ref, cb_ref, rows, t)
        xc_ref[...] = xc
        gates = _dot(xc.astype(BF16), w_ref[0]) + gb_ref[0]
        for d in range(2):
            r_ref[d], i_ref[d], _, a_s[d], _, q_ref[d], b_s[d] = _lru_gates(xc, gates, lam_ref, valid, d)
        _scan([(a_s.at[d], b_s.at[d], h_ref.at[d], p_s.at[d], d == 1) for d in range(2)], t)
        g, _ = _gelu_parts(xg_ref[...])
        o_ref[...] = (h_ref[0] + h_ref[1]) * g

    stacked = jax.ShapeDtypeStruct((2, n, D_RNN), F32)
    res = pl.pallas_call(
        body, name="rnn_fwd", grid=(D_RNN // LANES, n // t),
        in_specs=[seq, seq, cw, vec1, wspec, gspec, vec2], out_specs=[seq, seq, both, both, both, both],
        out_shape=[jax.ShapeDtypeStruct((n, D_RNN), F32), jax.ShapeDtypeStruct((n, D_RNN), F32)] + [stacked] * 4,
        scratch_shapes=[pltpu.VMEM((2, t, LANES), F32)] * 3,
        compiler_params=pltpu.CompilerParams(dimension_semantics=("parallel", "parallel"), vmem_limit_bytes=VMEM_LIMIT),
    )(xr, xg, conv_w, conv_b, wblk, gbias, lam)
    return res[0], tuple(res[1:])


def _rnn_bwd(xr, xg, do, saved, conv_w, wblk, lam, srcs=(), scatter=()):
    n = xr.shape[0]
    t = _t_pad()
    seq, cw, vec1, vec2, wspec, gspec = _rnn_specs(t)
    nk = len(srcs)
    c_in, c_out, c_shape, c_sems = _exchange_specs(srcs, scatter)

    def body(xr_ref, xg_ref, do_ref, xc_ref, r_s, i_s, q_s, h_s, cw_ref, w_ref, lam_ref, *rest):
        dxr_ref, dxg_ref, dcw_ref, dcb_ref, dw_ref, dgb_ref, dlam_ref = rest[nk:nk + 7]
        a_s, b_s, l_s, p_s, back_s, dg_s = rest[2 * nk + 7 + len(c_sems):]
        finish = _ride(2, *_exchange_fns(rest[:nk], rest[nk + 7:2 * nk + 7], rest[2 * nk + 7:2 * nk + 7 + len(c_sems)],
                                         scatter))
        first = pl.program_id(1) == 0
        rows = lax.broadcasted_iota(jnp.int32, (t, LANES), 0)
        valid = rows >= PAD_ROWS
        xr = xr_ref[...]
        xc = xc_ref[...]
        xcb = xc.astype(BF16)
        g, dg = _gelu_parts(xg_ref[...])
        do = do_ref[...]
        dxg_ref[...] = do * (h_s[0] + h_s[1]) * dg
        b_s[...] = do * g
        sps = []
        for d in range(2):
            neg_lam = -lam_ref[d:d + 1, :]
            sps.append(jnp.maximum(neg_lam, 0.0) + jnp.log1p(jnp.exp(-jnp.abs(neg_lam))))
            a_s[d] = jnp.exp(-LRU_C * r_s[d] * sps[d])
            back_s[d] = _shift_rows(a_s[d], -1 if d == 0 else 1, rows, t)
        _scan([(back_s.at[d], b_s, l_s.at[d], p_s.at[d], d == 0) for d in range(2)], t)
        dxc = jnp.zeros((t, LANES), F32)
        dlams = []
        for d in range(2):
            r, i, sp, a, sq = r_s[d], i_s[d], sps[d], a_s[d], q_s[d]
            lam_t = l_s[d]
            da = lam_t * _shift_rows(h_s[d], 1 if d == 0 else -1, rows, t)
            lam_v = jnp.where(valid, lam_t, 0.0)
            dsq = lam_v * (i * xc)
            di = lam_v * sq * xc
            dxc = dxc + lam_v * sq * i
            dm = jnp.where(sq > 0.0, dsq * 0.5 / jnp.where(sq > 0.0, sq, 1.0), 0.0)
            dla = da * a - 2.0 * dm * a * a
            dr = dla * (-LRU_C) * sp
            dsp = _colsum(dla * (-LRU_C) * r)
            dlams.append(dsp * -jax.nn.sigmoid(-lam_ref[d:d + 1, :]))
            dg_s[:, (2 * d) * LANES:(2 * d + 1) * LANES] = (dr * r * (1.0 - r)).astype(BF16)
            dg_s[:, (2 * d + 1) * LANES:(2 * d + 2) * LANES] = (di * i * (1.0 - i)).astype(BF16)
        dgates = dg_s[...]
        dxc = dxc + _dot_nt(dgates, w_ref[0])
        taps = [_shift_rows(dxc, j - 2, rows, t) for j in range(4)]
        dxr_ref[...] = (cw_ref[0:1, :] * taps[0] + cw_ref[1:2, :] * taps[1] + cw_ref[2:3, :] * taps[2]
                        + cw_ref[3:4, :] * taps[3])
        dcw = jnp.concatenate([_colsum(tap * xr) for tap in taps], axis=0)
        _acc(dcw_ref, first, dcw)
        _acc(dcb_ref, first, _colsum(dxc))
        _acc(dw_ref, first, _dot_tn(xcb, dgates)[None])
        _acc(dgb_ref, first, _colsum(dgates.astype(F32))[None])
        _acc(dlam_ref, first, jnp.concatenate(dlams, axis=0))
        finish()

    both = pl.BlockSpec((2, t, LANES), lambda cb, b: (0, b, cb))
    pair = pltpu.VMEM((2, t, LANES), F32)
    res = pl.pallas_call(
        body, name="rnn_bwd", grid=(D_RNN // LANES, n // t),
        in_specs=[seq, seq, seq, seq, both, both, both, both, cw, wspec, vec2] + c_in,
        out_specs=[seq, seq, cw, vec1, wspec, gspec, vec2] + c_out,
        out_shape=[jax.ShapeDtypeStruct((n, D_RNN), F32), jax.ShapeDtypeStruct((n, D_RNN), F32),
                   jax.ShapeDtypeStruct((4, D_RNN), F32), jax.ShapeDtypeStruct((1, D_RNN), F32),
                   jax.ShapeDtypeStruct((D_RNN // LANES, LANES, 4 * LANES), F32),
                   jax.ShapeDtypeStruct((D_RNN // LANES, 1, 4 * LANES), F32), jax.ShapeDtypeStruct((2, D_RNN), F32)]
        + c_shape,
        scratch_shapes=c_sems + [pair, pltpu.VMEM((t, LANES), F32), pair, pair, pair, pltpu.VMEM((t, 4 * LANES), BF16)],
        compiler_params=pltpu.CompilerParams(dimension_semantics=("arbitrary", "arbitrary"), vmem_limit_bytes=VMEM_LIMIT),
    )(xr, xg, do, *saved, conv_w, wblk, lam, *srcs)
    return res[:7], res[7:]


def _post(oa, orn, h0, tgt, ga, gr, g2, w_out, w_gate, w_up, w_down):
    n = oa.shape[0]
    tm = _row_tile(n)
    t = _t_pad()
    head = PAD_ROWS + N_META
    parts = tm // head

    def body(oa_ref, or_ref, h0_ref, *rest):
        tgt_refs = rest[:parts]
        (ga_ref, gr_ref, g2_ref, wo_ref, wg_ref, wu_ref, wd_ref,
         doa_ref, dor_ref, dh1_ref, mix_ref, h1n_ref, act_ref, dgate_ref, dup_ref, dy_ref,
         loss_ref, dga_ref, dgr_ref, dg2_ref, gate_s, up_s) = rest[parts:]
        first = pl.program_id(0) == 0
        xa, ra = _rms(oa_ref[...], D_ATTN)
        xr, rr = _rms(or_ref[...], D_RNN)
        mix = jnp.concatenate([(xa * ga_ref[...]).astype(BF16), (xr * gr_ref[...]).astype(BF16)], axis=-1)
        mix_ref[...] = mix.T
        h1 = h0_ref[...] + _dot(mix, wo_ref[...])
        x2, r2 = _rms(h1, D_MODEL)
        h1n = (x2 * g2_ref[...]).astype(BF16)
        h1n_ref[...] = h1n
        y = h1
        for cs in range(0, D_FF, FF_CHUNK):
            sl = slice(cs, cs + FF_CHUNK)
            gate = _dot_nt(h1n, wg_ref[sl, :])
            up = _dot_nt(h1n, wu_ref[sl, :])
            gate_s[:, sl] = gate
            up_s[:, sl] = up
            act = (gate * _sigmoid(gate) * up).astype(BF16)
            act_ref[sl, :] = act.T
            y = y + _dot(act, wd_ref[sl, :])
        row = pl.program_id(0) * tm + lax.broadcasted_iota(jnp.int32, (tm, 1), 0)
        for _ in range(1, n // t):
            row = jnp.where(row >= t, row - t, row)
        tgt = jnp.concatenate([ref[0] for ref in tgt_refs], axis=0)
        err = jnp.where(row >= PAD_ROWS + N_META, y - tgt, 0.0)
        _acc(loss_ref, first, jnp.full((1, LANES), 0.5 / D_MODEL, F32) * jnp.sum(err * err))
        dy = err * (1.0 / D_MODEL)
        dyb = dy.astype(BF16)
        dy_ref[...] = dyb
        dh1n = jnp.zeros((tm, D_MODEL), F32)
        for cs in range(0, D_FF, FF_CHUNK):
            sl = slice(cs, cs + FF_CHUNK)
            dact = _dot_nt(dyb, wd_ref[sl, :])
            gate, up = gate_s[:, sl], up_s[:, sl]
            sg = _sigmoid(gate)
            dgate = (dact * up * sg * (1.0 + gate * (1.0 - sg))).astype(BF16)
            dup = (dact * gate * sg).astype(BF16)
            dgate_ref[sl, :] = dgate.T
            dup_ref[sl, :] = dup.T
            dh1n = dh1n + _dot(dgate, wg_ref[sl, :]) + _dot(dup, wu_ref[sl, :])
        _acc(dg2_ref, first, _colsum(dh1n * x2))
        dh1 = dy + _rms_bwd(dh1n, x2, r2, g2_ref[...], D_MODEL)
        dh1_ref[...] = dh1
        dmix = _dot_nt(dh1.astype(BF16), wo_ref[...])
        dma, dmr = dmix[:, :D_ATTN], dmix[:, D_ATTN:]
        _acc(dga_ref, first, _colsum(dma * xa))
        _acc(dgr_ref, first, _colsum(dmr * xr))
        doa_ref[...] = _rms_bwd(dma, xa, ra, ga_ref[...], D_ATTN)
        dor_ref[...] = _rms_bwd(dmr, xr, rr, gr_ref[...], D_RNN)

    def row(w):
        return pl.BlockSpec((tm, w), lambda i: (i, 0))

    def acc(w):
        return pl.BlockSpec((1, w), lambda i: (0, 0))

    def col(w):
        return pl.BlockSpec((w, tm), lambda i: (0, i))

    outs = [(D_ATTN, F32, row), (D_RNN, F32, row), (D_MODEL, F32, row), (D_MODEL, BF16, col), (D_MODEL, BF16, row),
            (D_FF, BF16, col), (D_FF, BF16, col), (D_FF, BF16, col), (D_MODEL, BF16, row)]
    accs = [LANES, D_ATTN, D_RNN, D_MODEL]
    per = t // head

    def target_part(p):
        def index(i):
            block = i * parts + p
            return block // per, jnp.maximum(block % per - 1, 0), 0
        return pl.BlockSpec((1, head, D_MODEL), index)

    return pl.pallas_call(
        body, name="post", grid=(n // tm,),
        in_specs=[row(D_ATTN), row(D_RNN), row(D_MODEL)] + [target_part(p) for p in range(parts)] + [
                  _const_spec((1, D_ATTN)), _const_spec((1, D_RNN)), _const_spec((1, D_MODEL)),
                  _const_spec((D_MODEL, D_MODEL)), _const_spec((D_FF, D_MODEL)), _const_spec((D_FF, D_MODEL)),
                  _const_spec((D_FF, D_MODEL))],
        out_specs=[spec(w) for w, _, spec in outs] + [acc(w) for w in accs],
        out_shape=[jax.ShapeDtypeStruct((n, w) if spec is row else (w, n), dt) for w, dt, spec in outs]
        + [jax.ShapeDtypeStruct((1, w), F32) for w in accs],
        scratch_shapes=[pltpu.VMEM((tm, D_FF), F32), pltpu.VMEM((tm, D_FF), F32)],
        compiler_params=pltpu.CompilerParams(dimension_semantics=("arbitrary",), vmem_limit_bytes=VMEM_LIMIT),
    )(oa, orn, h0, *[tgt] * parts, ga, gr, g2, w_out, w_gate, w_up, w_down)


def _in_bwd(dp, h0, dh1, ln1_g, w_in_p, srcs=(), scatter=()):
    n = h0.shape[0]
    tm = _row_tile(n)
    nk = len(srcs)
    c_in, c_out, c_shape, c_sems = _exchange_specs(srcs, scatter)

    def body(dp_ref, h0_ref, dh1_ref, g_ref, w_ref, *rest):
        dh0_ref, dg_ref = rest[nk:nk + 2]
        finish = _ride(1, *_exchange_fns(rest[:nk], rest[nk + 2:2 * nk + 2], rest[2 * nk + 2:], scatter))
        dhn = _dot(dp_ref[...], w_ref[...])
        xhat, r = _rms(h0_ref[...], D_MODEL)
        _acc(dg_ref, pl.program_id(0) == 0, _colsum(dhn * xhat))
        dh0_ref[...] = dh1_ref[...] + _rms_bwd(dhn, xhat, r, g_ref[...], D_MODEL)
        finish()

    def row(w):
        return pl.BlockSpec((tm, w), lambda i: (i, 0))

    res = pl.pallas_call(
        body, name="in_bwd", grid=(n // tm,),
        in_specs=[row(P_COLS), row(D_MODEL), row(D_MODEL), _const_spec((1, D_MODEL)), _const_spec((P_COLS, D_MODEL))] + c_in,
        out_specs=[row(D_MODEL), pl.BlockSpec((1, D_MODEL), lambda i: (0, 0))] + c_out,
        out_shape=[jax.ShapeDtypeStruct((n, D_MODEL), F32), jax.ShapeDtypeStruct((1, D_MODEL), F32)] + c_shape,
        scratch_shapes=c_sems,
        compiler_params=pltpu.CompilerParams(dimension_semantics=("arbitrary",), vmem_limit_bytes=VMEM_LIMIT),
    )(dp, h0, dh1, ln1_g, w_in_p, *srcs)
    return res[:2], res[2:]


MAX_TILE = D_FF // 2


def _pick_tile(width, cap):
    best = LANES
    for mult in range(1, width // LANES + 1):
        cand = mult * LANES
        if width % cand == 0 and cand <= cap:
            best = cand
    return best


def _matmul_tn(name, a, b, srcs=(), scatter=()):
    n, ka = a.shape
    kb = b.shape[1]
    ta, tb = _pick_tile(ka, MAX_TILE), _pick_tile(kb, MAX_TILE)
    tk = n // 2
    nk = len(srcs)
    c_in, c_out, c_shape, c_sems = _exchange_specs(srcs, scatter)

    def body(a_ref, b_ref, *rest):
        o_ref = rest[nk]
        finish = _ride(3, *_exchange_fns(rest[:nk], rest[nk + 1:2 * nk + 1], rest[2 * nk + 1:], scatter))
        _acc(o_ref, pl.program_id(2) == 0, _dot_tn(a_ref[...].astype(BF16), b_ref[...].astype(BF16)))
        finish()

    res = pl.pallas_call(
        body, name=name, grid=(ka // ta, kb // tb, n // tk),
        in_specs=[pl.BlockSpec((tk, ta), lambda i, j, k: (k, i)), pl.BlockSpec((tk, tb), lambda i, j, k: (k, j))] + c_in,
        out_specs=[pl.BlockSpec((ta, tb), lambda i, j, k: (i, j))] + c_out,
        out_shape=[jax.ShapeDtypeStruct((ka, kb), F32)] + c_shape, scratch_shapes=c_sems,
        compiler_params=pltpu.CompilerParams(dimension_semantics=("arbitrary", "arbitrary", "arbitrary"),
                                             vmem_limit_bytes=VMEM_LIMIT),
    )(a, b, *srcs)
    return res[0], res[1:]


def _matmul_shards(name, ats, b):
    count = len(ats)
    ka, n = ats[0].shape
    kb = b.shape[1]
    width = ka // N_DEV
    per = 2 if 2 * width >= 4 * LANES else 4
    ta = per * width
    steps = ka // ta

    def body(*refs):
        b_ref = refs[count]
        for c in range(count):
            @pl.when(pl.program_id(0) // steps == c)
            def _():
                out = _dot(refs[c][...], b_ref[...].astype(BF16))
                for s in range(per):
                    refs[count + 1 + c][s] = out[s * width:(s + 1) * width, :].astype(BF16)

    def block_of(c):
        return lambda i: (jnp.clip(i - c * steps, 0, steps - 1), 0)

    def shards_of(c):
        return lambda i: (jnp.clip(i - c * steps, 0, steps - 1), 0, 0)

    return pl.pallas_call(
        body, name=name, grid=(count * steps,),
        in_specs=[pl.BlockSpec((ta, n), block_of(c)) for c in range(count)] + [_const_spec((n, kb))],
        out_specs=[pl.BlockSpec((per, width, kb), shards_of(c)) for c in range(count)],
        out_shape=[jax.ShapeDtypeStruct((N_DEV, width, kb), BF16)] * count,
        compiler_params=pltpu.CompilerParams(dimension_semantics=("arbitrary",), vmem_limit_bytes=VMEM_LIMIT),
    )(*ats, b)


def _adamw_math(g8_ref, w_ref, m_ref, v_ref, g_ref, d_ref, nm_ref, nv_ref):
    g = g8_ref[0].astype(F32)
    for s in range(1, N_DEV):
        g = g + g8_ref[s].astype(F32)
    g_ref[...] = g
    nm = ADAM_B1 * m_ref[...] + (1.0 - ADAM_B1) * g
    nv = ADAM_B2 * v_ref[...] + (1.0 - ADAM_B2) * (g * g)
    nm_ref[...] = nm
    nv_ref[...] = nv
    m_hat = nm / (1.0 - ADAM_B1 ** ADAM_STEP)
    v_hat = nv / (1.0 - ADAM_B2 ** ADAM_STEP)
    d_ref[...] = -ADAM_LR * (m_hat / (jnp.sqrt(v_hat) + ADAM_EPS) + ADAM_WD * w_ref[...])


def _adamw_many(name, items):
    count = len(items)

    def body(*refs):
        ins, outs = refs[:4 * count], refs[4 * count:]
        for i in range(count):
            _adamw_math(*ins[4 * i:4 * i + 4], *outs[4 * i:4 * i + 4])

    flat = [a for item in items for a in item]
    res = pl.pallas_call(
        body, name=name,
        out_shape=[jax.ShapeDtypeStruct(item[1].shape, F32) for item in items for _ in range(4)],
        compiler_params=pltpu.CompilerParams(vmem_limit_bytes=VMEM_LIMIT),
    )(*flat)
    return [tuple(res[4 * i:4 * i + 4]) for i in range(count)]


def _adamw(name, g8, w, m, v):
    rows, cols = w.shape
    tr = rows
    for cand in (256, 176, 128, 64):
        if rows % cand == 0 and rows > cand:
            tr = cand
            break

    def body(*refs):
        _adamw_math(*refs)

    blk = pl.BlockSpec((tr, cols), lambda i: (i, 0))
    return pl.pallas_call(
        body, name=name, grid=(rows // tr,),
        in_specs=[pl.BlockSpec((N_DEV, tr, cols), lambda i: (0, i, 0)), blk, blk, blk],
        out_specs=[blk] * 4, out_shape=[jax.ShapeDtypeStruct((rows, cols), F32)] * 4,
        compiler_params=pltpu.CompilerParams(dimension_semantics=("parallel",), vmem_limit_bytes=VMEM_LIMIT),
    )(g8, w, m, v)


def _exchange_specs(srcs, scatter):
    nk = len(srcs)
    if not nk:
        return [], [], [], []
    any_spec = pl.BlockSpec(memory_space=pl.ANY)
    out_shape = [jax.ShapeDtypeStruct(s.shape if sc else (N_DEV,) + s.shape, s.dtype) for s, sc in zip(srcs, scatter)]
    sems = [pltpu.SemaphoreType.DMA((nk, N_DEV - 1)), pltpu.SemaphoreType.DMA((nk, N_DEV - 1)),
            pltpu.SemaphoreType.DMA((nk,))]
    return [any_spec] * nk, [any_spec] * nk, out_shape, sems


FLIPS = ((0, 0, 1), (1, 0, 0), (0, 1, 0), (1, 1, 0), (1, 0, 1), (0, 1, 1), (1, 1, 1))
N_CHIP_PEERS = 3


def _exchange_fns(src_refs, out_refs, sems, scatter):
    nk = len(src_refs)
    if not nk:
        return (lambda: None), (lambda: None), (lambda: None)
    send_sems, recv_sems, local_sems = sems
    first = 1 + N_CHIP_PEERS

    def plan():
        x, y, c = lax.axis_index("x"), lax.axis_index("y"), lax.axis_index("c")
        me = 4 * x + 2 * y + c
        peers = [(1 - x if fx else x, 1 - y if fy else y, 1 - c if fc else c) for fx, fy, fc in FLIPS]
        pids = [4 * px + 2 * py + pc for px, py, pc in peers]

        def remote(k, j, src, dst, to):
            return pltpu.make_async_remote_copy(src_ref=src, dst_ref=dst, send_sem=send_sems.at[k, j],
                                                recv_sem=recv_sems.at[k, j], device_id=to, device_id_type=MESH)

        def mine(k, dest):
            return src_refs[k].at[dest] if scatter[k] else src_refs[k]

        local = [pltpu.make_async_copy(mine(k, me), out_refs[k].at[me], local_sems.at[k]) for k in range(nk)]
        direct = [remote(k, j, mine(k, pids[j]), out_refs[k].at[me], peers[j])
                  for k in range(nk) for j in range(len(FLIPS) if scatter[k] else first)]
        relays = {(k, j): remote(k, j, out_refs[k].at[pids[j - N_CHIP_PEERS]], out_refs[k].at[pids[j - N_CHIP_PEERS]], peers[0])
                  for k in range(nk) if not scatter[k] for j in range(first, len(FLIPS))}
        arrivals = {(k, j): remote(k, j, out_refs[k].at[pids[j]], out_refs[k].at[pids[j]], peers[j])
                    for k in range(nk) for j in range(len(FLIPS))}
        return local, direct, relays, arrivals

    def start():
        local, direct, _, _ = plan()
        for cp in local + direct:
            cp.start()

    def relay():
        _, _, relays, arrivals = plan()
        for (k, j), cp in relays.items():
            arrivals[k, j - N_CHIP_PEERS].wait_recv()
            cp.start()

    def wait():
        local, direct, relays, arrivals = plan()
        for (k, j), cp in arrivals.items():
            if (k, j + N_CHIP_PEERS) not in relays:
                cp.wait_recv()
        for cp in direct + list(relays.values()):
            cp.wait_send()
        for cp in local:
            cp.wait()

    return start, relay, wait


def _grid_step(rank):
    step, total = 0, 1
    for axis in range(rank):
        step = step * pl.num_programs(axis) + pl.program_id(axis)
        total = total * pl.num_programs(axis)
    return step, total


def _ride(rank, start, relay, wait):
    step, total = _grid_step(rank)
    pl.when(step == 0)(start)
    pl.when(step == (3 * total) // 4)(relay)
    return lambda: pl.when(step == total - 1)(wait)


def _exchange(name, srcs, scatter):
    nk = len(srcs)
    c_in, c_out, c_shape, c_sems = _exchange_specs(srcs, scatter)

    def body(*refs):
        start, relay, wait = _exchange_fns(refs[:nk], refs[nk:2 * nk], refs[2 * nk:], scatter)
        start()
        relay()
        wait()

    return pl.pallas_call(body, name=name, in_specs=c_in, out_specs=c_out, out_shape=c_shape, scratch_shapes=c_sems)(*srcs)


def _cols_from_shards(g):
    return jnp.transpose(g, (1, 0, 2)).reshape(g.shape[1], -1)


def _cols_to_shards(w):
    return jnp.transpose(w.reshape(w.shape[0], N_DEV, -1), (1, 0, 2))


def _prep(x, srcs, scatter):
    nb = x.shape[0]
    t = _t_pad()
    head = PAD_ROWS + N_META
    nk = len(srcs)
    c_in, c_out, c_shape, c_sems = _exchange_specs(srcs, scatter)

    def body(x_ref, *rest):
        h0_ref = rest[nk]
        finish = _ride(1, *_exchange_fns(rest[:nk], rest[nk + 1:2 * nk + 1], rest[2 * nk + 1:], scatter))
        lead = pl.program_id(0) == 0

        @pl.when(lead)
        def _():
            h0_ref[...] = jnp.zeros_like(h0_ref)

        @pl.when(jnp.logical_not(lead))
        def _():
            h0_ref[...] = x_ref[...]

        finish()

    src = pl.BlockSpec((nb, head, D_MODEL), lambda j: (0, jnp.maximum(j - 1, 0), 0))
    dst = pl.BlockSpec((nb, head, D_MODEL), lambda j: (0, j, 0))
    res = pl.pallas_call(
        body, name="prep", grid=(t // head,), in_specs=[src] + c_in, out_specs=[dst] + c_out,
        out_shape=[jax.ShapeDtypeStruct((nb, t, D_MODEL), F32)] + c_shape, scratch_shapes=c_sems,
        compiler_params=pltpu.CompilerParams(dimension_semantics=("arbitrary",)),
    )(x, *srcs)
    return res[0], res[1:]


def _rope_tables(n):
    t = _t_pad()
    pos = np.arange(t, dtype=np.float32) - np.float32(PAD_ROWS)
    half = QK_ROPE // 2
    freqs = (1.0 / (ROPE_THETA ** (np.arange(half, dtype=np.float32) / half))).astype(np.float32)
    ang = pos[:, None] * freqs[None, :]
    cos, sin = np.cos(ang), np.sin(ang)
    z = lambda w: np.zeros((t, w), np.float32)
    c = np.concatenate([np.ones((t, QK_NOPE), np.float32), cos, cos, z(HEAD_PAD - QK_HEAD)], axis=1)
    s1 = np.concatenate([z(QK_NOPE + half), sin, z(HEAD_PAD - QK_HEAD)], axis=1)
    s2 = np.concatenate([z(QK_NOPE), -sin, z(HEAD_PAD - QK_NOPE - half)], axis=1)
    return tuple(jnp.asarray(np.tile(a, (n // t, 1))) for a in (c, s1, s2))


def _block_diag_gates(lru_wa, lru_wi):
    eye = jnp.eye(2, dtype=lru_wa.dtype)

    def bd(w):
        w = w.reshape(2, D_RNN // LANES, 2, RNN_BW, RNN_BW)
        full = w[:, :, :, :, None, :] * eye[None, None, :, None, :, None]
        return full.reshape(2, D_RNN // LANES, LANES, LANES)

    a, i = bd(lru_wa), bd(lru_wi)
    return jnp.concatenate([a[0], i[0], a[1], i[1]], axis=-1)


def _unblock_gates(dw):
    nb = D_RNN // LANES
    parts = dw.reshape(nb, 2, RNN_BW, 4, 2, RNN_BW)
    diag = jnp.stack([parts[:, k, :, :, k, :] for k in range(2)], axis=1)
    diag = jnp.transpose(diag, (3, 0, 1, 2, 4)).reshape(4, 2 * nb, RNN_BW, RNN_BW)
    return jnp.stack([diag[0], diag[2]]), jnp.stack([diag[1], diag[3]])


WEIGHTS = ("meta_tokens", "ln1_g", "w_in", "q_a_norm_g", "w_uq", "kv_a_norm_g", "w_ukv", "q_norm_g", "k_norm_g",
           "conv_w", "conv_b", "lru_wa", "lru_ba", "lru_wi", "lru_bi", "lru_lambda", "attn_out_g", "rnn_out_g",
           "w_out", "ln2_g", "w_gate", "w_up", "w_down")
BIG = ("w_in", "w_uq", "w_ukv", "w_out", "w_gate", "w_up", "w_down")
TRANSPOSED = ("w_in", "w_uq", "w_gate", "w_up")
ROW_SHARDED = ("w_out", "w_down") + TRANSPOSED
REPLICATED = ("ln1_g", "q_a_norm_g", "kv_a_norm_g", "q_norm_g", "k_norm_g", "conv_b", "lru_wa", "lru_wi",
              "attn_out_g", "rnn_out_g", "ln2_g")
WHOLE = REPLICATED + ("loss",)
G_FIRST = ("w_in", "meta_tokens")
G_MID = ("w_uq", "w_ukv", "conv_w", "lru_ba", "lru_bi", "lru_lambda")
LATE = ("w_out", "w_gate", "w_up", "w_down")
G_LAST = ("meta_tokens", "ln1_g")


def _local_step(x, tgt, ex):
    nb = x.shape[0]
    t = _t_pad()
    n = nb * t
    local = ex.local
    h0, got = _prep(x, *ex.gather_srcs(G_FIRST))
    first = ex.gathered(G_FIRST, got)
    meta, w_in = first["meta_tokens"], first["w_in"]
    h0 = h0.at[:, PAD_ROWS:PAD_ROWS + N_META].set(jnp.broadcast_to(meta[None], (nb, N_META, D_MODEL))).reshape(n, D_MODEL)

    zr = lambda r: jnp.zeros((r, D_MODEL), w_in.dtype)
    w_in_p = jnp.concatenate([w_in[:OFF_CKV], w_in[OFF_KR:], zr(QK_NOPE), w_in[OFF_CKV:OFF_KR], zr(HEAD_PAD - QK_HEAD)],
                             axis=0)
    pad_g = lambda g: jnp.pad(g, ((0, 0), (0, HEAD_PAD - QK_HEAD)))
    qg, kg = pad_g(local["q_norm_g"]), pad_g(local["k_norm_g"])
    rc, rs1, rs2 = _rope_tables(n)
    wblk = _block_diag_gates(local["lru_wa"].reshape(2, -1, RNN_BW, RNN_BW),
                             local["lru_wi"].reshape(2, -1, RNN_BW, RNN_BW)).astype(BF16)
    nblk = D_RNN // LANES

    (hn, cq, ckv, xr, xg, kr), got = _in_proj(h0, local["ln1_g"], w_in_p, *ex.gather_srcs(G_MID))
    w = ex.gathered(G_MID, got)
    w_uq_p = jnp.pad(w["w_uq"].reshape(N_HEADS, QK_HEAD, Q_LORA), ((0, 0), (0, HEAD_PAD - QK_HEAD), (0, 0))
                     ).reshape(QP_COLS, Q_LORA)
    ukv = w["w_ukv"].reshape(KV_LORA, N_HEADS, QK_NOPE + V_HEAD)
    w_uk_p = jnp.pad(ukv[:, :, :QK_NOPE], ((0, 0), (0, 0), (0, HEAD_PAD - QK_NOPE))).reshape(KV_LORA, QP_COLS)
    w_v = ukv[:, :, QK_NOPE:].reshape(KV_LORA, D_ATTN)
    gbias = jnp.stack([w["lru_ba"][0], w["lru_bi"][0], w["lru_ba"][1], w["lru_bi"][1]], axis=0)
    gbias = jnp.transpose(gbias.reshape(4, nblk, LANES), (1, 0, 2)).reshape(nblk, 1, 4 * LANES)

    q, k, v = _qkv_fwd(cq, ckv, kr, local["q_a_norm_g"], local["kv_a_norm_g"], w_uq_p, w_uk_p, w_v, qg, kg, rc, rs1, rs2)
    oa, probs, got = _attn_fwd(q, k, v, *ex.gather_srcs(LATE))
    late = ex.gathered(LATE, got)
    orn, rnn_saved = _rnn_fwd(xr, xg, w["conv_w"], local["conv_b"], wblk, gbias, w["lru_lambda"])
    (doa, dor, dh1, mix_t, h1n, act_t, dgate_t, dup_t, dyb, loss, dga, dgr, dg2) = _post(
        oa, orn, h0, tgt, local["attn_out_g"], local["rnn_out_g"], local["ln2_g"], late["w_out"], late["w_gate"],
        late["w_up"], late["w_down"])
    dw_gate, dw_up = _matmul_shards("dw_gate_up", [dgate_t, dup_t], h1n)
    wire = {"w_out": _matmul_shards("dw_out", [mix_t], dh1)[0], "w_gate": dw_gate, "w_up": dw_up,
            "w_down": _matmul_shards("dw_down", [act_t], dyb)[0]}
    names = ("w_gate",)
    (dxr, dxg, dcw, dcb, dwblk, dgb, dlam), got = _rnn_bwd(xr, xg, dor, rnn_saved, w["conv_w"], wblk, w["lru_lambda"],
                                                           *ex.scatter_srcs(names, wire))
    summed = ex.scattered(names, wire, got)
    dwa, dwi = _unblock_gates(dwblk)
    dgb = jnp.transpose(dgb.reshape(nblk, 4, LANES), (1, 0, 2)).reshape(4, D_RNN)
    names = ("w_out", "w_up", "w_down")
    (dq_r, dk_r, dv), got = _attn_bwd(q, k, v, doa, oa, probs, *ex.scatter_srcs(names, wire))
    summed.update(ex.scattered(names, wire, got))
    wire = ex.to_wire({
        "conv_w": dcw, "conv_b": dcb, "lru_wa": dwa.reshape(-1, RNN_BW), "lru_ba": jnp.stack([dgb[0], dgb[2]]),
        "lru_wi": dwi.reshape(-1, RNN_BW), "lru_bi": jnp.stack([dgb[1], dgb[3]]), "lru_lambda": dlam,
        "attn_out_g": dga, "rnn_out_g": dgr, "ln2_g": dg2, "loss": loss})
    names = tuple(wire)
    (dp, qa, kva, dqp, dkv, dqg, dkg, dgqa, dgkva), got = _qkv_bwd(
        cq, ckv, kr, dq_r, dk_r, dv, dxr, dxg, local["q_a_norm_g"], local["kv_a_norm_g"], w_uq_p, w_uk_p, w_v, qg, kg,
        rc, rs1, rs2, *ex.scatter_srcs(names, wire))
    summed.update(ex.scattered(names, wire, got))
    dw_uq_p, _ = _matmul_tn("dw_uq", dqp, qa)
    dw_kv, _ = _matmul_tn("dw_ukv", kva, dkv)
    dw_uq = dw_uq_p.reshape(N_HEADS, HEAD_PAD, Q_LORA)[:, :QK_HEAD].reshape(N_HEADS * QK_HEAD, Q_LORA)
    dw_ukv = jnp.concatenate([dw_kv[:, :QP_COLS].reshape(KV_LORA, N_HEADS, HEAD_PAD)[:, :, :QK_NOPE],
                              dw_kv[:, QP_COLS:].reshape(KV_LORA, N_HEADS, V_HEAD)], axis=2).reshape(KV_LORA, -1)
    wire = ex.to_wire({"q_a_norm_g": dgqa, "w_uq": dw_uq, "kv_a_norm_g": dgkva, "w_ukv": dw_ukv,
                       "q_norm_g": dqg[:, :QK_HEAD], "k_norm_g": dkg[:, :QK_HEAD]})
    names = tuple(wire)
    dw_in_p, got = _matmul_tn("dw_in", dp, hn, *ex.scatter_srcs(names, wire))
    summed.update(ex.scattered(names, wire, got))
    kr0 = OFF_CKV + 2 * D_RNN + QK_NOPE
    dw_in = jnp.concatenate([dw_in_p[:OFF_CKV], dw_in_p[kr0:kr0 + QK_ROPE], dw_in_p[OFF_CKV:OFF_CKV + 2 * D_RNN]], axis=0)
    wire = ex.to_wire({"w_in": dw_in})
    (dh0, dg1), got = _in_bwd(dp, h0, dh1, local["ln1_g"], w_in_p, *ex.scatter_srcs(("w_in",), wire))
    summed.update(ex.scattered(("w_in",), wire, got))

    dh0 = dh0.reshape(nb, t, D_MODEL)
    wire = ex.to_wire({"meta_tokens": jnp.sum(dh0[:, PAD_ROWS:PAD_ROWS + N_META], axis=0), "ln1_g": dg1})
    got = ex.run("reduce_last", *ex.scatter_srcs(G_LAST, wire))
    summed.update(ex.scattered(G_LAST, wire, got))
    return dh0[:, PAD_ROWS + N_META:], summed


class _MeshExchange:
    def __init__(self, shards):
        self.local = shards

    @staticmethod
    def run(name, srcs, scatter):
        return _exchange(name, srcs, scatter)

    def gather_srcs(self, names):
        return [self.local[k].astype(BF16) if k in BIG else self.local[k] for k in names], [False] * len(names)

    @staticmethod
    def gathered(names, outs):
        return {k: g.reshape(-1, g.shape[-1]) if k in ROW_SHARDED else _cols_from_shards(g) for k, g in zip(names, outs)}

    @staticmethod
    def to_wire(grads):
        wire = {}
        for k, g in grads.items():
            if k in WHOLE:
                wire[k] = g
            elif k in ROW_SHARDED:
                wire[k] = g.reshape(N_DEV, -1, g.shape[-1]).astype(BF16)
            else:
                wire[k] = _cols_to_shards(g).astype(BF16) if k in BIG else _cols_to_shards(g)
        return wire

    @staticmethod
    def scatter_srcs(names, wire):
        return [wire[k] for k in names], [k not in WHOLE for k in names]

    @staticmethod
    def scattered(names, wire, outs):
        return dict(zip(names, outs))


def kernel(x, meta_tokens, ln1_g, w_in, q_a_norm_g, w_uq, kv_a_norm_g, w_ukv, q_norm_g, k_norm_g, conv_w, conv_b, lru_wa, lru_ba, lru_wi, lru_bi, lru_lambda, attn_out_g, rnn_out_g, w_out, ln2_g, w_gate, w_up, w_down, loss_target, m_meta_tokens, m_ln1_g, m_w_in, m_q_a_norm_g, m_w_uq, m_kv_a_norm_g, m_w_ukv, m_q_norm_g, m_k_norm_g, m_conv_w, m_conv_b, m_lru_wa, m_lru_ba, m_lru_wi, m_lru_bi, m_lru_lambda, m_attn_out_g, m_rnn_out_g, m_w_out, m_ln2_g, m_w_gate, m_w_up, m_w_down, v_meta_tokens, v_ln1_g, v_w_in, v_q_a_norm_g, v_w_uq, v_kv_a_norm_g, v_w_ukv, v_q_norm_g, v_k_norm_g, v_conv_w, v_conv_b, v_lru_wa, v_lru_ba, v_lru_wi, v_lru_bi, v_lru_lambda, v_attn_out_g, v_rnn_out_g, v_w_out, v_ln2_g, v_w_gate, v_w_up, v_w_down):
    given = (meta_tokens, ln1_g, w_in, q_a_norm_g, w_uq, kv_a_norm_g, w_ukv, q_norm_g, k_norm_g, conv_w, conv_b,
             lru_wa, lru_ba, lru_wi, lru_bi, lru_lambda, attn_out_g, rnn_out_g, w_out, ln2_g, w_gate, w_up, w_down)
    moments_m = (m_meta_tokens, m_ln1_g, m_w_in, m_q_a_norm_g, m_w_uq, m_kv_a_norm_g, m_w_ukv, m_q_norm_g, m_k_norm_g,
                 m_conv_w, m_conv_b, m_lru_wa, m_lru_ba, m_lru_wi, m_lru_bi, m_lru_lambda, m_attn_out_g, m_rnn_out_g,
                 m_w_out, m_ln2_g, m_w_gate, m_w_up, m_w_down)
    moments_v = (v_meta_tokens, v_ln1_g, v_w_in, v_q_a_norm_g, v_w_uq, v_kv_a_norm_g, v_w_ukv, v_q_norm_g, v_k_norm_g,
                 v_conv_w, v_conv_b, v_lru_wa, v_lru_ba, v_lru_wi, v_lru_bi, v_lru_lambda, v_attn_out_g, v_rnn_out_g,
                 v_w_out, v_ln2_g, v_w_gate, v_w_up, v_w_down)
    shapes = {k: a.shape for k, a in zip(WEIGHTS, given)}

    def two_d(k, a):
        a = a.reshape(-1, a.shape[-1])
        return a.T if k in TRANSPOSED else a

    w = {k: two_d(k, a) for k, a in zip(WEIGHTS, given)}
    m = {k: two_d(k, a) for k, a in zip(WEIGHTS, moments_m)}
    v = {k: two_d(k, a) for k, a in zip(WEIGHTS, moments_v)}

    grad_x, parts = _local_step(x, loss_target, _MeshExchange(w))

    tiled = ("w_in", "w_gate", "w_up", "w_down")
    new = {k: _adamw("adamw_" + k, parts[k], w[k], m[k], v[k]) for k in tiled}
    small = [k for k in WEIGHTS if k not in tiled]
    new.update(zip(small, _adamw_many("adamw_small", [(parts[k], w[k], m[k], v[k]) for k in small])))

    loss = jnp.sum(parts["loss"][:, 0, 0])
    outs = [loss, grad_x]
    for idx in range(4):
        outs += [(new[k][idx].T if k in TRANSPOSED else new[k][idx]).reshape(shapes[k]) for k in WEIGHTS]
    return tuple(outs)
```

```python
import functools
import math

import numpy as np
import jax
import jax.numpy as jnp
from jax import lax
from jax.experimental import pallas as pl
from jax.experimental.pallas import tpu as pltpu

F32 = jnp.float32
BF16 = jnp.bfloat16

D_MODEL = 1024
N_META = 16
SEQ = 2048
N_HEADS = 8
QK_NOPE = 64
QK_ROPE = 32
QK_HEAD = QK_NOPE + QK_ROPE
V_HEAD = 64
D_ATTN = N_HEADS * V_HEAD
Q_LORA = 384
KV_LORA = 256
D_RNN = 512
RNN_BW = 64
D_FF = 2816
EPS = 1e-6
LRU_C = 8.0
ROPE_THETA = 10000.0
OFF_CKV = Q_LORA + KV_LORA
OFF_KR = OFF_CKV + QK_ROPE
IN_COLS = OFF_KR + 2 * D_RNN

ADAM_LR = 0.001
ADAM_B1 = 0.9
ADAM_B2 = 0.999
ADAM_EPS = 1e-08
ADAM_WD = 0.01
ADAM_STEP = 10

N_DEV = 8
LANES = 128
HEAD_PAD = LANES
PAD_ROWS = LANES - N_META
QP_COLS = N_HEADS * HEAD_PAD
P_COLS = OFF_CKV + 2 * D_RNN + LANES
FF_CHUNK = D_FF
VMEM_LIMIT = 56 * 1024 * 1024
MESH = pl.DeviceIdType.MESH


def _t_pad():
    return PAD_ROWS + N_META + SEQ


def _row_tile(n):
    return 256 if n % 256 == 0 else 128


def _wide_row_tile(n):
    quarter = _t_pad() // 4
    return quarter if quarter % 16 == 0 and n % quarter == 0 else _row_tile(n)


def _const_spec(shape):
    nd = len(shape)
    return pl.BlockSpec(shape, lambda *_: (0,) * nd, pipeline_mode=pl.Buffered(1))


def _rms(x, d):
    r = lax.rsqrt(jnp.sum(x * x, axis=-1, keepdims=True) * (1.0 / d) + EPS)
    return x * r, r


def _rms_bwd(dy, xhat, r, g, d):
    dxh = dy * g
    return r * (dxh - xhat * (jnp.sum(dxh * xhat, axis=-1, keepdims=True) * (1.0 / d)))


def _colsum(x):
    return jnp.sum(x, axis=0, keepdims=True)


def _dot(a, b):
    return jnp.dot(a, b, preferred_element_type=F32)


def _dot_nt(a, b):
    return lax.dot_general(a, b, (((1,), (1,)), ((), ())), preferred_element_type=F32)


def _dot_tn(a, b):
    return lax.dot_general(a, b, (((0,), (0,)), ((), ())), preferred_element_type=F32)


def _rope(x, c, s1, s2):
    return x * c + pltpu.roll(x, 16, 1) * s1 + pltpu.roll(x, HEAD_PAD - 16, 1) * s2


def _rope_bwd(dy, c, s1, s2):
    return dy * c + pltpu.roll(dy * s1, HEAD_PAD - 16, 1) + pltpu.roll(dy * s2, 16, 1)


def _acc(ref, first, val):
    @pl.when(first)
    def _():
        ref[...] = val

    @pl.when(jnp.logical_not(first))
    def _():
        ref[...] += val


def _in_proj(h0, ln1_g, w_in_p, srcs=(), scatter=()):
    n = h0.shape[0]
    tm = _wide_row_tile(n)
    nk = len(srcs)
    c_in, c_out, c_shape, c_sems = _exchange_specs(srcs, scatter)

    def body(h_ref, g_ref, w_ref, *rest):
        hn_ref, cq_ref, ckv_ref, xr_ref, xg_ref, kr_ref = rest[nk:nk + 6]
        finish = _ride(1, *_exchange_fns(rest[:nk], rest[nk + 6:2 * nk + 6], rest[2 * nk + 6:], scatter))
        xhat, _ = _rms(h_ref[...], D_MODEL)
        hn = (xhat * g_ref[...]).astype(BF16)
        hn_ref[...] = hn
        p = _dot_nt(hn, w_ref[...])
        cq_ref[...] = p[:, :Q_LORA]
        ckv_ref[...] = p[:, Q_LORA:OFF_CKV]
        xr_ref[...] = p[:, OFF_CKV:OFF_CKV + D_RNN]
        xg_ref[...] = p[:, OFF_CKV + D_RNN:OFF_CKV + 2 * D_RNN]
        kr_ref[...] = p[:, OFF_CKV + 2 * D_RNN:]
        finish()

    def row(w):
        return pl.BlockSpec((tm, w), lambda i: (i, 0))

    widths = (D_MODEL, Q_LORA, KV_LORA, D_RNN, D_RNN, LANES)
    res = pl.pallas_call(
        body, name="in_proj", grid=(n // tm,),
        in_specs=[row(D_MODEL), _const_spec((1, D_MODEL)), _const_spec((P_COLS, D_MODEL))] + c_in,
        out_specs=[row(w) for w in widths] + c_out,
        out_shape=[jax.ShapeDtypeStruct((n, w), BF16 if k == 0 else F32) for k, w in enumerate(widths)] + c_shape,
        scratch_shapes=c_sems,
        compiler_params=pltpu.CompilerParams(dimension_semantics=("arbitrary",), vmem_limit_bytes=VMEM_LIMIT),
    )(h0, ln1_g, w_in_p, *srcs)
    return res[:6], res[6:]


def _qkv_fwd(cq, ckv, kr, gqa, gkva, w_uq_p, w_uk_p, w_v, qg, kg, rc, rs1, rs2):
    n = cq.shape[0]
    tm = _wide_row_tile(n)

    def body(cq_ref, ckv_ref, kr_ref, gqa_ref, gkva_ref, wuq_ref, wuk_ref, wv_ref, qg_ref, kg_ref,
             c_ref, s1_ref, s2_ref, q_ref, k_ref, v_ref):
        xq, _ = _rms(cq_ref[...], Q_LORA)
        qa = (xq * gqa_ref[...]).astype(BF16)
        q = _dot_nt(qa, wuq_ref[...])
        xkv, _ = _rms(ckv_ref[...], KV_LORA)
        kva = (xkv * gkva_ref[...]).astype(BF16)
        kn = _dot(kva, wuk_ref[...])
        v_ref[...] = _dot(kva, wv_ref[...]).astype(BF16)
        krp = kr_ref[...]
        c, s1, s2 = c_ref[...], s1_ref[...], s2_ref[...]
        for h in range(N_HEADS):
            sl = slice(h * HEAD_PAD, (h + 1) * HEAD_PAD)
            qh, _ = _rms(q[:, sl], QK_HEAD)
            q_ref[:, sl] = _rope(qh * qg_ref[...], c, s1, s2).astype(BF16)
            kh, _ = _rms(kn[:, sl] + krp, QK_HEAD)
            k_ref[:, sl] = _rope(kh * kg_ref[...], c, s1, s2).astype(BF16)

    def row(w):
        return pl.BlockSpec((tm, w), lambda i: (i, 0))

    return pl.pallas_call(
        body, name="qkv_fwd", grid=(n // tm,),
        in_specs=[row(Q_LORA), row(KV_LORA), row(LANES), _const_spec((1, Q_LORA)), _const_spec((1, KV_LORA)),
                  _const_spec((QP_COLS, Q_LORA)), _const_spec((KV_LORA, QP_COLS)), _const_spec((KV_LORA, D_ATTN)),
                  _const_spec((1, LANES)), _const_spec((1, LANES)), row(LANES), row(LANES), row(LANES)],
        out_specs=[row(QP_COLS), row(QP_COLS), row(D_ATTN)],
        out_shape=[jax.ShapeDtypeStruct((n, QP_COLS), BF16), jax.ShapeDtypeStruct((n, QP_COLS), BF16),
                   jax.ShapeDtypeStruct((n, D_ATTN), BF16)],
        compiler_params=pltpu.CompilerParams(dimension_semantics=("parallel",), vmem_limit_bytes=VMEM_LIMIT),
    )(cq, ckv, kr, gqa, gkva, w_uq_p, w_uk_p, w_v, qg, kg, rc, rs1, rs2)


def _qkv_bwd(cq, ckv, kr, dq_r, dk_r, dv, dxr, dxg, gqa, gkva, w_uq_p, w_uk_p, w_v, qg, kg, rc, rs1, rs2,
             srcs=(), scatter=()):
    n = cq.shape[0]
    tm = _wide_row_tile(n)
    nk = len(srcs)
    c_in, c_out, c_shape, c_sems = _exchange_specs(srcs, scatter)

    def body(cq_ref, ckv_ref, kr_ref, dq_ref, dk_ref, dv_ref, dxr_ref, dxg_ref, gqa_ref, gkva_ref, wuq_ref, wuk_ref,
             wv_ref, qg_ref, kg_ref, c_ref, s1_ref, s2_ref, *rest):
        dp_ref, qa_ref, kva_ref, dqp_ref, dkv_ref, dqg_ref, dkg_ref, dgqa_ref, dgkva_ref = rest[nk:nk + 9]
        finish = _ride(1, *_exchange_fns(rest[:nk], rest[nk + 9:2 * nk + 9], rest[2 * nk + 9:], scatter))
        first = pl.program_id(0) == 0
        dp_ref[:, OFF_CKV:OFF_CKV + D_RNN] = dxr_ref[...].astype(BF16)
        dp_ref[:, OFF_CKV + D_RNN:OFF_CKV + 2 * D_RNN] = dxg_ref[...].astype(BF16)
        xq, rq = _rms(cq_ref[...], Q_LORA)
        qa = (xq * gqa_ref[...]).astype(BF16)
        qa_ref[...] = qa
        q = _dot_nt(qa, wuq_ref[...])
        xkv, rkv = _rms(ckv_ref[...], KV_LORA)
        kva = (xkv * gkva_ref[...]).astype(BF16)
        kva_ref[...] = kva
        kn = _dot(kva, wuk_ref[...])
        krp = kr_ref[...]
        c, s1, s2 = c_ref[...], s1_ref[...], s2_ref[...]
        lane = lax.broadcasted_iota(jnp.int32, (tm, HEAD_PAD), 1)
        rope_lanes = jnp.logical_and(lane >= QK_NOPE, lane < QK_HEAD)
        dqg = jnp.zeros((1, HEAD_PAD), F32)
        dkg = jnp.zeros((1, HEAD_PAD), F32)
        dkr = jnp.zeros((tm, HEAD_PAD), F32)
        for h in range(N_HEADS):
            sl = slice(h * HEAD_PAD, (h + 1) * HEAD_PAD)
            qh, rqh = _rms(q[:, sl], QK_HEAD)
            dy = _rope_bwd(dq_ref[:, sl], c, s1, s2)
            dqg = dqg + _colsum(dy * qh)
            dqp_ref[:, sl] = _rms_bwd(dy, qh, rqh, qg_ref[...], QK_HEAD).astype(BF16)
            kh, rkh = _rms(kn[:, sl] + krp, QK_HEAD)
            dyk = _rope_bwd(dk_ref[:, sl], c, s1, s2)
            dkg = dkg + _colsum(dyk * kh)
            dkh = _rms_bwd(dyk, kh, rkh, kg_ref[...], QK_HEAD)
            dkv_ref[:, sl] = dkh.astype(BF16)
            dkr = dkr + jnp.where(rope_lanes, dkh, 0.0)
        dkv_ref[:, QP_COLS:] = dv_ref[...].astype(BF16)
        dp_ref[:, OFF_CKV + 2 * D_RNN:] = dkr.astype(BF16)
        dqa = _dot(dqp_ref[...], wuq_ref[...])
        dp_ref[:, :Q_LORA] = _rms_bwd(dqa, xq, rq, gqa_ref[...], Q_LORA).astype(BF16)
        dkva = _dot_nt(dkv_ref[:, :QP_COLS], wuk_ref[...]) + _dot_nt(dkv_ref[:, QP_COLS:], wv_ref[...])
        dp_ref[:, Q_LORA:OFF_CKV] = _rms_bwd(dkva, xkv, rkv, gkva_ref[...], KV_LORA).astype(BF16)
        _acc(dqg_ref, first, dqg)
        _acc(dkg_ref, first, dkg)
        _acc(dgqa_ref, first, _colsum(dqa * xq))
        _acc(dgkva_ref, first, _colsum(dkva * xkv))
        finish()

    def row(w):
        return pl.BlockSpec((tm, w), lambda i: (i, 0))

    def acc(w):
        return pl.BlockSpec((1, w), lambda i: (0, 0))

    res = pl.pallas_call(
        body, name="qkv_bwd", grid=(n // tm,),
        in_specs=[row(Q_LORA), row(KV_LORA), row(LANES), row(QP_COLS), row(QP_COLS), row(D_ATTN), row(D_RNN), row(D_RNN),
                  _const_spec((1, Q_LORA)), _const_spec((1, KV_LORA)),
                  _const_spec((QP_COLS, Q_LORA)), _const_spec((KV_LORA, QP_COLS)), _const_spec((KV_LORA, D_ATTN)),
                  _const_spec((1, LANES)), _const_spec((1, LANES)), row(LANES), row(LANES), row(LANES)] + c_in,
        out_specs=[row(P_COLS), row(Q_LORA), row(KV_LORA), row(QP_COLS),
                   row(QP_COLS + D_ATTN), acc(LANES), acc(LANES), acc(Q_LORA), acc(KV_LORA)] + c_out,
        out_shape=[jax.ShapeDtypeStruct((n, P_COLS), BF16), jax.ShapeDtypeStruct((n, Q_LORA), BF16),
                   jax.ShapeDtypeStruct((n, KV_LORA), BF16), jax.ShapeDtypeStruct((n, QP_COLS), BF16),
                   jax.ShapeDtypeStruct((n, QP_COLS + D_ATTN), BF16),
                   jax.ShapeDtypeStruct((1, LANES), F32), jax.ShapeDtypeStruct((1, LANES), F32),
                   jax.ShapeDtypeStruct((1, Q_LORA), F32), jax.ShapeDtypeStruct((1, KV_LORA), F32)] + c_shape,
        scratch_shapes=c_sems,
        compiler_params=pltpu.CompilerParams(dimension_semantics=("arbitrary",), vmem_limit_bytes=VMEM_LIMIT),
    )(cq, ckv, kr, dq_r, dk_r, dv, dxr, dxg, gqa, gkva, w_uq_p, w_uk_p, w_v, qg, kg, rc, rs1, rs2, *srcs)
    return res[:9], res[9:]


FWD_KEY_CHUNK = 2 * LANES
BWD_KEY_CHUNK = 4 * LANES


def _key_chunks(t, chunk):
    count = max(t // chunk, 1)
    first = t - chunk * (count - 1)
    return [(0, first)] + [(first + chunk * c, chunk) for c in range(count - 1)]


def _attn_specs(t, tq):
    nq = t // tq
    qspec = pl.BlockSpec((tq, 2 * HEAD_PAD), lambda b, hp, i: (b * nq + i, hp))
    kspec = pl.BlockSpec((t, 2 * HEAD_PAD), lambda b, hp, i: (b, hp))
    vspec = pl.BlockSpec((t, 2 * V_HEAD), lambda b, hp, i: (b, hp))
    ospec = pl.BlockSpec((tq, 2 * V_HEAD), lambda b, hp, i: (b * nq + i, hp))
    return nq, qspec, kspec, vspec, ospec


def _probs_spec(t, tq):
    return pl.BlockSpec((1, 2, tq, t), lambda b, hp, i: (b, hp, i, 0))


def _attn_fwd(q, k, v, srcs=(), scatter=()):
    n = q.shape[0]
    t = _t_pad()
    tq = t // 2
    nq, qspec, kspec, vspec, ospec = _attn_specs(t, tq)
    nk = len(srcs)
    c_in, c_out, c_shape, c_sems = _exchange_specs(srcs, scatter)

    def body(q_ref, k_ref, v_ref, *rest):
        o_ref, l_ref, p_ref = rest[nk:nk + 3]
        finish = _ride(3, *_exchange_fns(rest[:nk], rest[nk + 3:2 * nk + 3], rest[2 * nk + 3:], scatter))
        lane = lax.broadcasted_iota(jnp.int32, (tq, 2 * V_HEAD), 1)
        outs = []
        sums = []
        for j in range(2):
            sl = slice(j * HEAD_PAD, (j + 1) * HEAD_PAD)
            qh = q_ref[:, sl]

            def scores(start, size):
                s = _dot_nt(qh, k_ref[start:start + size, sl])
                if start < PAD_ROWS:
                    key = lax.broadcasted_iota(jnp.int32, (tq, size), 1) + start
                    s = jnp.where(key >= PAD_ROWS, s, -jnp.inf)
                return s

            chunks = _key_chunks(t, FWD_KEY_CHUNK)
            top = functools.reduce(jnp.maximum, [jnp.max(scores(*c), axis=-1, keepdims=True) for c in chunks])
            l = jnp.zeros((tq, 1), F32)
            pv = jnp.zeros((tq, 2 * V_HEAD), F32)
            for start, size in chunks:
                e = jnp.exp2((scores(start, size) - top) * (QK_HEAD ** -0.5 * math.log2(math.e)))
                l = l + jnp.sum(e, axis=-1, keepdims=True)
                e = e.astype(BF16)
                p_ref[0, j, :, start:start + size] = e
                pv = pv + _dot(e, v_ref[start:start + size, :])
            outs.append(pv / l)
            sums.append(l)
        o_ref[...] = jnp.where(lane < V_HEAD, outs[0], outs[1])
        l_ref[...] = jnp.where(lane < V_HEAD, sums[0], sums[1])
        finish()

    res = pl.pallas_call(
        body, name="attn_fwd", grid=(n // t, N_HEADS // 2, nq),
        in_specs=[qspec, kspec, vspec] + c_in, out_specs=[ospec, ospec, _probs_spec(t, tq)] + c_out,
        out_shape=[jax.ShapeDtypeStruct((n, D_ATTN), F32), jax.ShapeDtypeStruct((n, D_ATTN), F32),
                   jax.ShapeDtypeStruct((n // t, N_HEADS, t, t), BF16)] + c_shape,
        scratch_shapes=c_sems,
        compiler_params=pltpu.CompilerParams(dimension_semantics=("arbitrary", "arbitrary", "arbitrary"),
                                             vmem_limit_bytes=VMEM_LIMIT),
    )(q, k, v, *srcs)
    return res[0], (res[1], res[2]), res[3:]


def _attn_bwd(q, k, v, do, o, probs, srcs=(), scatter=()):
    n = q.shape[0]
    t = _t_pad()
    tq = t // 2
    nq, qspec, kspec, vspec, ospec = _attn_specs(t, tq)
    nk = len(srcs)
    c_in, c_out, c_shape, c_sems = _exchange_specs(srcs, scatter)

    def body(q_ref, k_ref, v_ref, do_ref, o_ref, l_ref, p_ref, *rest):
        dq_ref, dk_ref, dv_ref = rest[nk:nk + 3]
        finish = _ride(3, *_exchange_fns(rest[:nk], rest[nk + 3:2 * nk + 3], rest[2 * nk + 3:], scatter))

        @pl.when(pl.program_id(2) == 0)
        def _():
            dk_ref[...] = jnp.zeros_like(dk_ref)
            dv_ref[...] = jnp.zeros_like(dv_ref)

        lane = lax.broadcasted_iota(jnp.int32, (tq, 2 * V_HEAD), 1)
        do = do_ref[...]
        do_o = do * o_ref[...]
        chunks = _key_chunks(t, BWD_KEY_CHUNK)
        dvs = [None] * len(chunks)
        for j in range(2):
            sl = slice(j * HEAD_PAD, (j + 1) * HEAD_PAD)
            qh = q_ref[:, sl]
            in_head = (lane < V_HEAD) if j == 0 else (lane >= V_HEAD)
            inv_l = 1.0 / l_ref[:, j * V_HEAD:j * V_HEAD + 1]
            doh = jnp.where(in_head, do, 0.0).astype(BF16)
            doh_n = jnp.where(in_head, do * inv_l, 0.0).astype(BF16)
            delta = jnp.sum(jnp.where(in_head, do_o, 0.0), axis=-1, keepdims=True)
            row_scale = inv_l * (QK_HEAD ** -0.5)
            dq = jnp.zeros((tq, HEAD_PAD), F32)
            for c, (start, size) in enumerate(chunks):
                rows = slice(start, start + size)
                e = p_ref[0, j, :, rows]
                dp = _dot_nt(doh, v_ref[rows, :])
                ds = (e.astype(F32) * (dp - delta) * row_scale).astype(BF16)
                dq = dq + _dot(ds, k_ref[rows, sl])
                dk_ref[rows, sl] += _dot_tn(ds, qh)
                dvc = _dot_tn(e, doh_n)
                dvs[c] = dvc if dvs[c] is None else dvs[c] + dvc
            dq_ref[:, sl] = dq
        for (start, size), dvc in zip(chunks, dvs):
            dv_ref[start:start + size, :] += dvc
        finish()

    res = pl.pallas_call(
        body, name="attn_bwd", grid=(n // t, N_HEADS // 2, nq),
        in_specs=[qspec, kspec, vspec, ospec, ospec, ospec, _probs_spec(t, tq)] + c_in,
        out_specs=[qspec, kspec, vspec] + c_out,
        out_shape=[jax.ShapeDtypeStruct((n, QP_COLS), F32), jax.ShapeDtypeStruct((n, QP_COLS), F32),
                   jax.ShapeDtypeStruct((n, D_ATTN), F32)] + c_shape, scratch_shapes=c_sems,
        compiler_params=pltpu.CompilerParams(dimension_semantics=("arbitrary", "arbitrary", "arbitrary"),
                                             vmem_limit_bytes=VMEM_LIMIT),
    )(q, k, v, do, o, *probs, *srcs)
    return res[:3], res[3:]


SCAN_STEPS = 8


def _scan(chains, t):
    seg = t // 8
    rows = lax.broadcasted_iota(jnp.int32, (8, LANES), 0)

    def step(i, carry):
        carry = list(carry)
        for u in range(SCAN_STEPS):
            j = i * SCAN_STEPS + u
            for n, (a_ref, b_ref, h_ref, p_ref, reverse) in enumerate(chains):
                h, p = carry[n]
                idx = pl.ds(seg - 1 - j if reverse else j, 8, stride=seg)
                a = a_ref[idx, :]
                h = a * h + b_ref[idx, :]
                p = a * p
                h_ref[idx, :] = h
                p_ref[idx, :] = p
                carry[n] = (h, p)
        return tuple(carry)

    init = tuple((jnp.zeros((8, LANES), F32), jnp.ones((8, LANES), F32)) for _ in chains)
    ends = lax.fori_loop(0, seg // SCAN_STEPS, step, init)
    for (_, _, h_ref, p_ref, reverse), (b, a) in zip(chains, ends):
        for d in (1, 2, 4):
            if reverse:
                keep = rows < 8 - d
                a_n, b_n = pltpu.roll(a, 8 - d, 0), pltpu.roll(b, 8 - d, 0)
            else:
                keep = rows >= d
                a_n, b_n = pltpu.roll(a, d, 0), pltpu.roll(b, d, 0)
            b = a * jnp.where(keep, b_n, 0.0) + b
            a = a * jnp.where(keep, a_n, 1.0)
        for s in (range(7) if reverse else range(1, 8)):
            sl = slice(s * seg, (s + 1) * seg)
            carry_in = b[s + 1:s + 2, :] if reverse else b[s - 1:s, :]
            h_ref[sl, :] = h_ref[sl, :] + p_ref[sl, :] * carry_in


def _shift_rows(x, s, rows, t):
    if s == 0:
        return x
    rolled = pltpu.roll(x, s % t, 0)
    return jnp.where(rows >= s, rolled, 0.0) if s > 0 else jnp.where(rows < t + s, rolled, 0.0)


def _neg_expm1_twice(h, exp_2h):
    series = h * (-2.0 + h * (-2.0 + h * (-4.0 / 3 + h * (-2.0 / 3))))
    return jnp.where(h > -0.05, series, 1.0 - exp_2h)


def _sigmoid(x):
    return 0.5 * jnp.tanh(0.5 * x) + 0.5


def _gelu_parts(x):
    k = math.sqrt(2.0 / math.pi)
    th = jnp.tanh(k * (x + 0.044715 * x * x * x))
    g = 0.5 * x * (1.0 + th)
    dg = 0.5 * (1.0 + th) + 0.5 * x * (1.0 - th * th) * k * (1.0 + 3 * 0.044715 * x * x)
    return g, dg


def _lru_gates(xc, gates, lam_ref, valid, d):
    r = _sigmoid(gates[:, (2 * d) * LANES:(2 * d + 1) * LANES])
    i = _sigmoid(gates[:, (2 * d + 1) * LANES:(2 * d + 2) * LANES])
    neg_lam = -lam_ref[d:d + 1, :]
    sp = jnp.maximum(neg_lam, 0.0) + jnp.log1p(jnp.exp(-jnp.abs(neg_lam)))
    log_a = -LRU_C * r * sp
    a = jnp.exp(log_a)
    m = jnp.maximum(_neg_expm1_twice(log_a, a * a), 0.0)
    sq = jnp.sqrt(m)
    b = jnp.where(valid, sq * (i * xc), 0.0)
    return r, i, sp, a, m, sq, b


def _conv(xr, cw_ref, cb_ref, rows, t):
    return (cw_ref[0:1, :] * _shift_rows(xr, 2, rows, t) + cw_ref[1:2, :] * _shift_rows(xr, 1, rows, t)
            + cw_ref[2:3, :] * xr + cw_ref[3:4, :] * _shift_rows(xr, -1, rows, t) + cb_ref[...])


def _rnn_specs(t):
    seq = pl.BlockSpec((t, LANES), lambda cb, b: (b, cb))
    cw = pl.BlockSpec((4, LANES), lambda cb, b: (0, cb))
    vec1 = pl.BlockSpec((1, LANES), lambda cb, b: (0, cb))
    vec2 = pl.BlockSpec((2, LANES), lambda cb, b: (0, cb))
    wblk = pl.BlockSpec((1, LANES, 4 * LANES), lambda cb, b: (cb, 0, 0))
    gbias = pl.BlockSpec((1, 1, 4 * LANES), lambda cb, b: (cb, 0, 0))
    return seq, cw, vec1, vec2, wblk, gbias


def _rnn_fwd(xr, xg, conv_w, conv_b, wblk, gbias, lam):
    n = xr.shape[0]
    t = _t_pad()
    seq, cw, vec1, vec2, wspec, gspec = _rnn_specs(t)
    both = pl.BlockSpec((2, t, LANES), lambda cb, b: (0, b, cb))

    def body(xr_ref, xg_ref, cw_ref, cb_ref, w_ref, gb_ref, lam_ref,
             o_ref, xc_ref, r_ref, i_ref, q_ref, h_ref, a_s, b_s, p_s):
        rows = lax.broadcasted_iota(jnp.int32, (t, LANES), 0)
        valid = rows >= PAD_ROWS
        xc = _conv(xr_ref[...], cw_ref, cb_ref, rows, t)
        xc_ref[...] = xc
        gates = _dot(xc.astype(BF16), w_ref[0]) + gb_ref[0]
        for d in range(2):
            r_ref[d], i_ref[d], _, a_s[d], _, q_ref[d], b_s[d] = _lru_gates(xc, gates, lam_ref, valid, d)
        _scan([(a_s.at[d], b_s.at[d], h_ref.at[d], p_s.at[d], d == 1) for d in range(2)], t)
        g, _ = _gelu_parts(xg_ref[...])
        o_ref[...] = (h_ref[0] + h_ref[1]) * g

    stacked = jax.ShapeDtypeStruct((2, n, D_RNN), F32)
    res = pl.pallas_call(
        body, name="rnn_fwd", grid=(D_RNN // LANES, n // t),
        in_specs=[seq, seq, cw, vec1, wspec, gspec, vec2], out_specs=[seq, seq, both, both, both, both, both],
        out_shape=[jax.ShapeDtypeStruct((n, D_RNN), F32), jax.ShapeDtypeStruct((n, D_RNN), F32)] + [stacked] * 5,
        scratch_shapes=[pltpu.VMEM((2, t, LANES), F32)] * 2,
        compiler_params=pltpu.CompilerParams(dimension_semantics=("parallel", "parallel"), vmem_limit_bytes=VMEM_LIMIT),
    )(xr, xg, conv_w, conv_b, wblk, gbias, lam)
    return res[0], tuple(res[1:])


def _rnn_bwd(xr, xg, do, saved, conv_w, wblk, lam, srcs=(), scatter=()):
    n = xr.shape[0]
    t = _t_pad()
    seq, cw, vec1, vec2, wspec, gspec = _rnn_specs(t)
    nk = len(srcs)
    c_in, c_out, c_shape, c_sems = _exchange_specs(srcs, scatter)

    def body(xr_ref, xg_ref, do_ref, xc_ref, r_s, i_s, q_s, h_s, a_s, cw_ref, w_ref, lam_ref, *rest):
        dxr_ref, dxg_ref, dcw_ref, dcb_ref, dw_ref, dgb_ref, dlam_ref = rest[nk:nk + 7]
        b_s, l_s, p_s, back_s, dg_s = rest[2 * nk + 7 + len(c_sems):]
        finish = _ride(2, *_exchange_fns(rest[:nk], rest[nk + 7:2 * nk + 7], rest[2 * nk + 7:2 * nk + 7 + len(c_sems)],
                                         scatter))
        first = pl.program_id(1) == 0
        rows = lax.broadcasted_iota(jnp.int32, (t, LANES), 0)
        valid = rows >= PAD_ROWS
        xr = xr_ref[...]
        xc = xc_ref[...]
        xcb = xc.astype(BF16)
        g, dg = _gelu_parts(xg_ref[...])
        do = do_ref[...]
        dxg_ref[...] = do * (h_s[0] + h_s[1]) * dg
        b_s[...] = do * g
        sps = []
        for d in range(2):
            neg_lam = -lam_ref[d:d + 1, :]
            sps.append(jnp.maximum(neg_lam, 0.0) + jnp.log1p(jnp.exp(-jnp.abs(neg_lam))))
            back_s[d] = _shift_rows(a_s[d], -1 if d == 0 else 1, rows, t)
        _scan([(back_s.at[d], b_s, l_s.at[d], p_s.at[d], d == 0) for d in range(2)], t)
        dxc = jnp.zeros((t, LANES), F32)
        dlams = []
        for d in range(2):
            r, i, sp, a, sq = r_s[d], i_s[d], sps[d], a_s[d], q_s[d]
            lam_t = l_s[d]
            da = lam_t * _shift_rows(h_s[d], 1 if d == 0 else -1, rows, t)
            lam_v = jnp.where(valid, lam_t, 0.0)
            dsq = lam_v * (i * xc)
            di = lam_v * sq * xc
            dxc = dxc + lam_v * sq * i
            dm = jnp.where(sq > 0.0, dsq * 0.5 / jnp.where(sq > 0.0, sq, 1.0), 0.0)
            dla = da * a - 2.0 * dm * a * a
            dr = dla * (-LRU_C) * sp
            dsp = _colsum(dla * (-LRU_C) * r)
            dlams.append(dsp * -jax.nn.sigmoid(-lam_ref[d:d + 1, :]))
            dg_s[:, (2 * d) * LANES:(2 * d + 1) * LANES] = (dr * r * (1.0 - r)).astype(BF16)
            dg_s[:, (2 * d + 1) * LANES:(2 * d + 2) * LANES] = (di * i * (1.0 - i)).astype(BF16)
        dgates = dg_s[...]
        dxc = dxc + _dot_nt(dgates, w_ref[0])
        taps = [_shift_rows(dxc, j - 2, rows, t) for j in range(4)]
        dxr_ref[...] = (cw_ref[0:1, :] * taps[0] + cw_ref[1:2, :] * taps[1] + cw_ref[2:3, :] * taps[2]
                        + cw_ref[3:4, :] * taps[3])
        dcw = jnp.concatenate([_colsum(tap * xr) for tap in taps], axis=0)
        _acc(dcw_ref, first, dcw)
        _acc(dcb_ref, first, _colsum(dxc))
        _acc(dw_ref, first, _dot_tn(xcb, dgates)[None])
        _acc(dgb_ref, first, _colsum(dgates.astype(F32))[None])
        _acc(dlam_ref, first, jnp.concatenate(dlams, axis=0))
        finish()

    both = pl.BlockSpec((2, t, LANES), lambda cb, b: (0, b, cb))
    pair = pltpu.VMEM((2, t, LANES), F32)
    res = pl.pallas_call(
        body, name="rnn_bwd", grid=(D_RNN // LANES, n // t),
        in_specs=[seq, seq, seq, seq, both, both, both, both, both, cw, wspec, vec2] + c_in,
        out_specs=[seq, seq, cw, vec1, wspec, gspec, vec2] + c_out,
        out_shape=[jax.ShapeDtypeStruct((n, D_RNN), F32), jax.ShapeDtypeStruct((n, D_RNN), F32),
                   jax.ShapeDtypeStruct((4, D_RNN), F32), jax.ShapeDtypeStruct((1, D_RNN), F32),
                   jax.ShapeDtypeStruct((D_RNN // LANES, LANES, 4 * LANES), F32),
                   jax.ShapeDtypeStruct((D_RNN // LANES, 1, 4 * LANES), F32), jax.ShapeDtypeStruct((2, D_RNN), F32)]
        + c_shape,
        scratch_shapes=c_sems + [pltpu.VMEM((t, LANES), F32), pair, pair, pair, pltpu.VMEM((t, 4 * LANES), BF16)],
        compiler_params=pltpu.CompilerParams(dimension_semantics=("arbitrary", "arbitrary"), vmem_limit_bytes=VMEM_LIMIT),
    )(xr, xg, do, *saved, conv_w, wblk, lam, *srcs)
    return res[:7], res[7:]


def _post(oa, orn, h0, tgt, ga, gr, g2, w_out, w_gate, w_up, w_down):
    n = oa.shape[0]
    tm = _row_tile(n)
    t = _t_pad()
    head = PAD_ROWS + N_META
    parts = tm // head

    def body(oa_ref, or_ref, h0_ref, *rest):
        tgt_refs = rest[:parts]
        (ga_ref, gr_ref, g2_ref, wo_ref, wg_ref, wu_ref, wd_ref,
         doa_ref, dor_ref, dh1_ref, mix_ref, h1n_ref, act_ref, dgate_ref, dup_ref, dy_ref,
         loss_ref, dga_ref, dgr_ref, dg2_ref, gate_s, up_s) = rest[parts:]
        first = pl.program_id(0) == 0
        xa, ra = _rms(oa_ref[...], D_ATTN)
        xr, rr = _rms(or_ref[...], D_RNN)
        mix = jnp.concatenate([(xa * ga_ref[...]).astype(BF16), (xr * gr_ref[...]).astype(BF16)], axis=-1)
        mix_ref[...] = mix.T
        h1 = h0_ref[...] + _dot(mix, wo_ref[...])
        x2, r2 = _rms(h1, D_MODEL)
        h1n = (x2 * g2_ref[...]).astype(BF16)
        h1n_ref[...] = h1n
        y = h1
        for cs in range(0, D_FF, FF_CHUNK):
            sl = slice(cs, cs + FF_CHUNK)
            gate = _dot_nt(h1n, wg_ref[sl, :])
            up = _dot_nt(h1n, wu_ref[sl, :])
            gate_s[:, sl] = gate
            up_s[:, sl] = up
            act = (gate * _sigmoid(gate) * up).astype(BF16)
            act_ref[sl, :] = act.T
            y = y + _dot(act, wd_ref[sl, :])
        row = pl.program_id(0) * tm + lax.broadcasted_iota(jnp.int32, (tm, 1), 0)
        for _ in range(1, n // t):
            row = jnp.where(row >= t, row - t, row)
        tgt = jnp.concatenate([ref[0] for ref in tgt_refs], axis=0)
        err = jnp.where(row >= PAD_ROWS + N_META, y - tgt, 0.0)
        _acc(loss_ref, first, jnp.full((1, LANES), 0.5 / D_MODEL, F32) * jnp.sum(err * err))
        dy = err * (1.0 / D_MODEL)
        dyb = dy.astype(BF16)
        dy_ref[...] = dyb
        dh1n = jnp.zeros((tm, D_MODEL), F32)
        for cs in range(0, D_FF, FF_CHUNK):
            sl = slice(cs, cs + FF_CHUNK)
            dact = _dot_nt(dyb, wd_ref[sl, :])
            gate, up = gate_s[:, sl], up_s[:, sl]
            sg = _sigmoid(gate)
            dgate = (dact * up * sg * (1.0 + gate * (1.0 - sg))).astype(BF16)
            dup = (dact * gate * sg).astype(BF16)
            dgate_ref[sl, :] = dgate.T
            dup_ref[sl, :] = dup.T
            dh1n = dh1n + _dot(dgate, wg_ref[sl, :]) + _dot(dup, wu_ref[sl, :])
        _acc(dg2_ref, first, _colsum(dh1n * x2))
        dh1 = dy + _rms_bwd(dh1n, x2, r2, g2_ref[...], D_MODEL)
        dh1_ref[...] = dh1
        dmix = _dot_nt(dh1.astype(BF16), wo_ref[...])
        dma, dmr = dmix[:, :D_ATTN], dmix[:, D_ATTN:]
        _acc(dga_ref, first, _colsum(dma * xa))
        _acc(dgr_ref, first, _colsum(dmr * xr))
        doa_ref[...] = _rms_bwd(dma, xa, ra, ga_ref[...], D_ATTN)
        dor_ref[...] = _rms_bwd(dmr, xr, rr, gr_ref[...], D_RNN)

    def row(w):
        return pl.BlockSpec((tm, w), lambda i: (i, 0))

    def acc(w):
        return pl.BlockSpec((1, w), lambda i: (0, 0))

    def col(w):
        return pl.BlockSpec((w, tm), lambda i: (0, i))

    outs = [(D_ATTN, F32, row), (D_RNN, F32, row), (D_MODEL, F32, row), (D_MODEL, BF16, col), (D_MODEL, BF16, row),
            (D_FF, BF16, col), (D_FF, BF16, col), (D_FF, BF16, col), (D_MODEL, BF16, row)]
    accs = [LANES, D_ATTN, D_RNN, D_MODEL]
    per = t // head

    def target_part(p):
        def index(i):
            block = i * parts + p
            return block // per, jnp.maximum(block % per - 1, 0), 0
        return pl.BlockSpec((1, head, D_MODEL), index)

    return pl.pallas_call(
        body, name="post", grid=(n // tm,),
        in_specs=[row(D_ATTN), row(D_RNN), row(D_MODEL)] + [target_part(p) for p in range(parts)] + [
                  _const_spec((1, D_ATTN)), _const_spec((1, D_RNN)), _const_spec((1, D_MODEL)),
                  _const_spec((D_MODEL, D_MODEL)), _const_spec((D_FF, D_MODEL)), _const_spec((D_FF, D_MODEL)),
                  _const_spec((D_FF, D_MODEL))],
        out_specs=[spec(w) for w, _, spec in outs] + [acc(w) for w in accs],
        out_shape=[jax.ShapeDtypeStruct((n, w) if spec is row else (w, n), dt) for w, dt, spec in outs]
        + [jax.ShapeDtypeStruct((1, w), F32) for w in accs],
        scratch_shapes=[pltpu.VMEM((tm, D_FF), F32), pltpu.VMEM((tm, D_FF), F32)],
        compiler_params=pltpu.CompilerParams(dimension_semantics=("arbitrary",), vmem_limit_bytes=VMEM_LIMIT),
    )(oa, orn, h0, *[tgt] * parts, ga, gr, g2, w_out, w_gate, w_up, w_down)


def _in_bwd(dp, h0, dh1, ln1_g, w_in_p, srcs=(), scatter=()):
    n = h0.shape[0]
    tm = _row_tile(n)
    nk = len(srcs)
    c_in, c_out, c_shape, c_sems = _exchange_specs(srcs, scatter)

    def body(dp_ref, h0_ref, dh1_ref, g_ref, w_ref, *rest):
        dh0_ref, dg_ref = rest[nk:nk + 2]
        finish = _ride(1, *_exchange_fns(rest[:nk], rest[nk + 2:2 * nk + 2], rest[2 * nk + 2:], scatter))
        dhn = _dot(dp_ref[...], w_ref[...])
        xhat, r = _rms(h0_ref[...], D_MODEL)
        _acc(dg_ref, pl.program_id(0) == 0, _colsum(dhn * xhat))
        dh0_ref[...] = dh1_ref[...] + _rms_bwd(dhn, xhat, r, g_ref[...], D_MODEL)
        finish()

    def row(w):
        return pl.BlockSpec((tm, w), lambda i: (i, 0))

    res = pl.pallas_call(
        body, name="in_bwd", grid=(n // tm,),
        in_specs=[row(P_COLS), row(D_MODEL), row(D_MODEL), _const_spec((1, D_MODEL)), _const_spec((P_COLS, D_MODEL))] + c_in,
        out_specs=[row(D_MODEL), pl.BlockSpec((1, D_MODEL), lambda i: (0, 0))] + c_out,
        out_shape=[jax.ShapeDtypeStruct((n, D_MODEL), F32), jax.ShapeDtypeStruct((1, D_MODEL), F32)] + c_shape,
        scratch_shapes=c_sems,
        compiler_params=pltpu.CompilerParams(dimension_semantics=("arbitrary",), vmem_limit_bytes=VMEM_LIMIT),
    )(dp, h0, dh1, ln1_g, w_in_p, *srcs)
    return res[:2], res[2:]


MAX_TILE = D_FF // 2


def _pick_tile(width, cap):
    best = LANES
    for mult in range(1, width // LANES + 1):
        cand = mult * LANES
        if width % cand == 0 and cand <= cap:
            best = cand
    return best


def _matmul_tn(name, a, b, srcs=(), scatter=()):
    n, ka = a.shape
    kb = b.shape[1]
    ta, tb = _pick_tile(ka, MAX_TILE), _pick_tile(kb, MAX_TILE)
    tk = n // 2
    nk = len(srcs)
    c_in, c_out, c_shape, c_sems = _exchange_specs(srcs, scatter)

    def body(a_ref, b_ref, *rest):
        o_ref = rest[nk]
        finish = _ride(3, *_exchange_fns(rest[:nk], rest[nk + 1:2 * nk + 1], rest[2 * nk + 1:], scatter))
        _acc(o_ref, pl.program_id(2) == 0, _dot_tn(a_ref[...].astype(BF16), b_ref[...].astype(BF16)))
        finish()

    res = pl.pallas_call(
        body, name=name, grid=(ka // ta, kb // tb, n // tk),
        in_specs=[pl.BlockSpec((tk, ta), lambda i, j, k: (k, i)), pl.BlockSpec((tk, tb), lambda i, j, k: (k, j))] + c_in,
        out_specs=[pl.BlockSpec((ta, tb), lambda i, j, k: (i, j))] + c_out,
        out_shape=[jax.ShapeDtypeStruct((ka, kb), F32)] + c_shape, scratch_shapes=c_sems,
        compiler_params=pltpu.CompilerParams(dimension_semantics=("arbitrary", "arbitrary", "arbitrary"),
                                             vmem_limit_bytes=VMEM_LIMIT),
    )(a, b, *srcs)
    return res[0], res[1:]


def _matmul_shards(name, ats, b):
    count = len(ats)
    ka, n = ats[0].shape
    kb = b.shape[1]
    width = ka // N_DEV
    per = 2 if 2 * width >= 4 * LANES else 4
    ta = per * width
    steps = ka // ta

    def body(*refs):
        b_ref = refs[count]
        for c in range(count):
            @pl.when(pl.program_id(0) // steps == c)
            def _():
                out = _dot(refs[c][...], b_ref[...].astype(BF16))
                for s in range(per):
                    refs[count + 1 + c][s] = out[s * width:(s + 1) * width, :].astype(BF16)

    def block_of(c):
        return lambda i: (jnp.clip(i - c * steps, 0, steps - 1), 0)

    def shards_of(c):
        return lambda i: (jnp.clip(i - c * steps, 0, steps - 1), 0, 0)

    return pl.pallas_call(
        body, name=name, grid=(count * steps,),
        in_specs=[pl.BlockSpec((ta, n), block_of(c)) for c in range(count)] + [_const_spec((n, kb))],
        out_specs=[pl.BlockSpec((per, width, kb), shards_of(c)) for c in range(count)],
        out_shape=[jax.ShapeDtypeStruct((N_DEV, width, kb), BF16)] * count,
        compiler_params=pltpu.CompilerParams(dimension_semantics=("arbitrary",), vmem_limit_bytes=VMEM_LIMIT),
    )(*ats, b)


def _adamw_math(g8_ref, w_ref, m_ref, v_ref, g_ref, d_ref, nm_ref, nv_ref):
    g = g8_ref[0].astype(F32)
    for s in range(1, N_DEV):
        g = g + g8_ref[s].astype(F32)
    g_ref[...] = g
    nm = ADAM_B1 * m_ref[...] + (1.0 - ADAM_B1) * g
    nv = ADAM_B2 * v_ref[...] + (1.0 - ADAM_B2) * (g * g)
    nm_ref[...] = nm
    nv_ref[...] = nv
    m_hat = nm / (1.0 - ADAM_B1 ** ADAM_STEP)
    v_hat = nv / (1.0 - ADAM_B2 ** ADAM_STEP)
    d_ref[...] = -ADAM_LR * (m_hat / (jnp.sqrt(v_hat) + ADAM_EPS) + ADAM_WD * w_ref[...])


def _adamw_many(name, items):
    count = len(items)

    def body(*refs):
        ins, outs = refs[:4 * count], refs[4 * count:]
        for i in range(count):
            _adamw_math(*ins[4 * i:4 * i + 4], *outs[4 * i:4 * i + 4])

    flat = [a for item in items for a in item]
    res = pl.pallas_call(
        body, name=name,
        out_shape=[jax.ShapeDtypeStruct(item[1].shape, F32) for item in items for _ in range(4)],
        compiler_params=pltpu.CompilerParams(vmem_limit_bytes=VMEM_LIMIT),
    )(*flat)
    return [tuple(res[4 * i:4 * i + 4]) for i in range(count)]


def _adamw(name, g8, w, m, v):
    rows, cols = w.shape
    tr = rows
    for cand in (256, 176, 128, 64):
        if rows % cand == 0 and rows > cand:
            tr = cand
            break

    def body(*refs):
        _adamw_math(*refs)

    blk = pl.BlockSpec((tr, cols), lambda i: (i, 0))
    return pl.pallas_call(
        body, name=name, grid=(rows // tr,),
        in_specs=[pl.BlockSpec((N_DEV, tr, cols), lambda i: (0, i, 0)), blk, blk, blk],
        out_specs=[blk] * 4, out_shape=[jax.ShapeDtypeStruct((rows, cols), F32)] * 4,
        compiler_params=pltpu.CompilerParams(dimension_semantics=("parallel",), vmem_limit_bytes=VMEM_LIMIT),
    )(g8, w, m, v)


def _exchange_specs(srcs, scatter):
    nk = len(srcs)
    if not nk:
        return [], [], [], []
    any_spec = pl.BlockSpec(memory_space=pl.ANY)
    out_shape = [jax.ShapeDtypeStruct(s.shape if sc else (N_DEV,) + s.shape, s.dtype) for s, sc in zip(srcs, scatter)]
    sems = [pltpu.SemaphoreType.DMA((nk, N_DEV - 1)), pltpu.SemaphoreType.DMA((nk, N_DEV - 1)),
            pltpu.SemaphoreType.DMA((nk,))]
    return [any_spec] * nk, [any_spec] * nk, out_shape, sems


FLIPS = ((0, 0, 1), (1, 0, 0), (0, 1, 0), (1, 1, 0), (1, 0, 1), (0, 1, 1), (1, 1, 1))
N_CHIP_PEERS = 3


def _exchange_fns(src_refs, out_refs, sems, scatter):
    nk = len(src_refs)
    if not nk:
        return (lambda: None), (lambda: None), (lambda: None)
    send_sems, recv_sems, local_sems = sems
    first = 1 + N_CHIP_PEERS

    def plan():
        x, y, c = lax.axis_index("x"), lax.axis_index("y"), lax.axis_index("c")
        me = 4 * x + 2 * y + c
        peers = [(1 - x if fx else x, 1 - y if fy else y, 1 - c if fc else c) for fx, fy, fc in FLIPS]
        pids = [4 * px + 2 * py + pc for px, py, pc in peers]

        def remote(k, j, src, dst, to):
            return pltpu.make_async_remote_copy(src_ref=src, dst_ref=dst, send_sem=send_sems.at[k, j],
                                                recv_sem=recv_sems.at[k, j], device_id=to, device_id_type=MESH)

        def mine(k, dest):
            return src_refs[k].at[dest] if scatter[k] else src_refs[k]

        local = [pltpu.make_async_copy(mine(k, me), out_refs[k].at[me], local_sems.at[k]) for k in range(nk)]
        direct = [remote(k, j, mine(k, pids[j]), out_refs[k].at[me], peers[j])
                  for k in range(nk) for j in range(len(FLIPS) if scatter[k] else first)]
        relays = {(k, j): remote(k, j, out_refs[k].at[pids[j - N_CHIP_PEERS]], out_refs[k].at[pids[j - N_CHIP_PEERS]], peers[0])
                  for k in range(nk) if not scatter[k] for j in range(first, len(FLIPS))}
        arrivals = {(k, j): remote(k, j, out_refs[k].at[pids[j]], out_refs[k].at[pids[j]], peers[j])
                    for k in range(nk) for j in range(len(FLIPS))}
        return local, direct, relays, arrivals

    def start():
        local, direct, _, _ = plan()
        for cp in local + direct:
            cp.start()

    def relay():
        _, _, relays, arrivals = plan()
        for (k, j), cp in relays.items():
            arrivals[k, j - N_CHIP_PEERS].wait_recv()
            cp.start()

    def wait():
        local, direct, relays, arrivals = plan()
        for (k, j), cp in arrivals.items():
            if (k, j + N_CHIP_PEERS) not in relays:
                cp.wait_recv()
        for cp in direct + list(relays.values()):
            cp.wait_send()
        for cp in local:
            cp.wait()

    return start, relay, wait


def _grid_step(rank):
    step, total = 0, 1
    for axis in range(rank):
        step = step * pl.num_programs(axis) + pl.program_id(axis)
        total = total * pl.num_programs(axis)
    return step, total


def _ride(rank, start, relay, wait):
    step, total = _grid_step(rank)
    pl.when(step == 0)(start)
    pl.when(step == (3 * total) // 4)(relay)
    return lambda: pl.when(step == total - 1)(wait)


def _exchange(name, srcs, scatter):
    nk = len(srcs)
    c_in, c_out, c_shape, c_sems = _exchange_specs(srcs, scatter)

    def body(*refs):
        start, relay, wait = _exchange_fns(refs[:nk], refs[nk:2 * nk], refs[2 * nk:], scatter)
        start()
        relay()
        wait()

    return pl.pallas_call(body, name=name, in_specs=c_in, out_specs=c_out, out_shape=c_shape, scratch_shapes=c_sems)(*srcs)


def _cols_from_shards(g):
    return jnp.transpose(g, (1, 0, 2)).reshape(g.shape[1], -1)


def _cols_to_shards(w):
    return jnp.transpose(w.reshape(w.shape[0], N_DEV, -1), (1, 0, 2))


def _prep(x, srcs, scatter):
    nb = x.shape[0]
    t = _t_pad()
    head = PAD_ROWS + N_META
    nk = len(srcs)
    c_in, c_out, c_shape, c_sems = _exchange_specs(srcs, scatter)

    def body(x_ref, *rest):
        h0_ref = rest[nk]
        finish = _ride(1, *_exchange_fns(rest[:nk], rest[nk + 1:2 * nk + 1], rest[2 * nk + 1:], scatter))
        lead = pl.program_id(0) == 0

        @pl.when(lead)
        def _():
            h0_ref[...] = jnp.zeros_like(h0_ref)

        @pl.when(jnp.logical_not(lead))
        def _():
            h0_ref[...] = x_ref[...]

        finish()

    src = pl.BlockSpec((nb, head, D_MODEL), lambda j: (0, jnp.maximum(j - 1, 0), 0))
    dst = pl.BlockSpec((nb, head, D_MODEL), lambda j: (0, j, 0))
    res = pl.pallas_call(
        body, name="prep", grid=(t // head,), in_specs=[src] + c_in, out_specs=[dst] + c_out,
        out_shape=[jax.ShapeDtypeStruct((nb, t, D_MODEL), F32)] + c_shape, scratch_shapes=c_sems,
        compiler_params=pltpu.CompilerParams(dimension_semantics=("arbitrary",)),
    )(x, *srcs)
    return res[0], res[1:]


def _rope_tables(n):
    t = _t_pad()
    pos = np.arange(t, dtype=np.float32) - np.float32(PAD_ROWS)
    half = QK_ROPE // 2
    freqs = (1.0 / (ROPE_THETA ** (np.arange(half, dtype=np.float32) / half))).astype(np.float32)
    ang = pos[:, None] * freqs[None, :]
    cos, sin = np.cos(ang), np.sin(ang)
    z = lambda w: np.zeros((t, w), np.float32)
    c = np.concatenate([np.ones((t, QK_NOPE), np.float32), cos, cos, z(HEAD_PAD - QK_HEAD)], axis=1)
    s1 = np.concatenate([z(QK_NOPE + half), sin, z(HEAD_PAD - QK_HEAD)], axis=1)
    s2 = np.concatenate([z(QK_NOPE), -sin, z(HEAD_PAD - QK_NOPE - half)], axis=1)
    return tuple(jnp.asarray(np.tile(a, (n // t, 1))) for a in (c, s1, s2))


def _block_diag_gates(lru_wa, lru_wi):
    eye = jnp.eye(2, dtype=lru_wa.dtype)

    def bd(w):
        w = w.reshape(2, D_RNN // LANES, 2, RNN_BW, RNN_BW)
        full = w[:, :, :, :, None, :] * eye[None, None, :, None, :, None]
        return full.reshape(2, D_RNN // LANES, LANES, LANES)

    a, i = bd(lru_wa), bd(lru_wi)
    return jnp.concatenate([a[0], i[0], a[1], i[1]], axis=-1)


def _unblock_gates(dw):
    nb = D_RNN // LANES
    parts = dw.reshape(nb, 2, RNN_BW, 4, 2, RNN_BW)
    diag = jnp.stack([parts[:, k, :, :, k, :] for k in range(2)], axis=1)
    diag = jnp.transpose(diag, (3, 0, 1, 2, 4)).reshape(4, 2 * nb, RNN_BW, RNN_BW)
    return jnp.stack([diag[0], diag[2]]), jnp.stack([diag[1], diag[3]])


WEIGHTS = ("meta_tokens", "ln1_g", "w_in", "q_a_norm_g", "w_uq", "kv_a_norm_g", "w_ukv", "q_norm_g", "k_norm_g",
           "conv_w", "conv_b", "lru_wa", "lru_ba", "lru_wi", "lru_bi", "lru_lambda", "attn_out_g", "rnn_out_g",
           "w_out", "ln2_g", "w_gate", "w_up", "w_down")
BIG = ("w_in", "w_uq", "w_ukv", "w_out", "w_gate", "w_up", "w_down")
TRANSPOSED = ("w_in", "w_uq", "w_gate", "w_up")
ROW_SHARDED = ("w_out", "w_down") + TRANSPOSED
REPLICATED = ("ln1_g", "q_a_norm_g", "kv_a_norm_g", "q_norm_g", "k_norm_g", "conv_b", "lru_wa", "lru_wi",
              "attn_out_g", "rnn_out_g", "ln2_g")
WHOLE = REPLICATED + ("loss",)
G_FIRST = ("w_in", "meta_tokens")
G_MID = ("w_uq", "w_ukv", "conv_w", "lru_ba", "lru_bi", "lru_lambda")
LATE = ("w_out", "w_gate", "w_up", "w_down")
G_LAST = ("meta_tokens", "ln1_g")


def _local_step(x, tgt, ex):
    nb = x.shape[0]
    t = _t_pad()
    n = nb * t
    local = ex.local
    h0, got = _prep(x, *ex.gather_srcs(G_FIRST))
    first = ex.gathered(G_FIRST, got)
    meta, w_in = first["meta_tokens"], first["w_in"]
    h0 = h0.at[:, PAD_ROWS:PAD_ROWS + N_META].set(jnp.broadcast_to(meta[None], (nb, N_META, D_MODEL))).reshape(n, D_MODEL)

    zr = lambda r: jnp.zeros((r, D_MODEL), w_in.dtype)
    w_in_p = jnp.concatenate([w_in[:OFF_CKV], w_in[OFF_KR:], zr(QK_NOPE), w_in[OFF_CKV:OFF_KR], zr(HEAD_PAD - QK_HEAD)],
                             axis=0)
    pad_g = lambda g: jnp.pad(g, ((0, 0), (0, HEAD_PAD - QK_HEAD)))
    qg, kg = pad_g(local["q_norm_g"]), pad_g(local["k_norm_g"])
    rc, rs1, rs2 = _rope_tables(n)
    wblk = _block_diag_gates(local["lru_wa"].reshape(2, -1, RNN_BW, RNN_BW),
                             local["lru_wi"].reshape(2, -1, RNN_BW, RNN_BW)).astype(BF16)
    nblk = D_RNN // LANES

    (hn, cq, ckv, xr, xg, kr), got = _in_proj(h0, local["ln1_g"], w_in_p, *ex.gather_srcs(G_MID))
    w = ex.gathered(G_MID, got)
    w_uq_p = jnp.pad(w["w_uq"].reshape(N_HEADS, QK_HEAD, Q_LORA), ((0, 0), (0, HEAD_PAD - QK_HEAD), (0, 0))
                     ).reshape(QP_COLS, Q_LORA)
    ukv = w["w_ukv"].reshape(KV_LORA, N_HEADS, QK_NOPE + V_HEAD)
    w_uk_p = jnp.pad(ukv[:, :, :QK_NOPE], ((0, 0), (0, 0), (0, HEAD_PAD - QK_NOPE))).reshape(KV_LORA, QP_COLS)
    w_v = ukv[:, :, QK_NOPE:].reshape(KV_LORA, D_ATTN)
    gbias = jnp.stack([w["lru_ba"][0], w["lru_bi"][0], w["lru_ba"][1], w["lru_bi"][1]], axis=0)
    gbias = jnp.transpose(gbias.reshape(4, nblk, LANES), (1, 0, 2)).reshape(nblk, 1, 4 * LANES)

    q, k, v = _qkv_fwd(cq, ckv, kr, local["q_a_norm_g"], local["kv_a_norm_g"], w_uq_p, w_uk_p, w_v, qg, kg, rc, rs1, rs2)
    oa, probs, got = _attn_fwd(q, k, v, *ex.gather_srcs(LATE))
    late = ex.gathered(LATE, got)
    orn, rnn_saved = _rnn_fwd(xr, xg, w["conv_w"], local["conv_b"], wblk, gbias, w["lru_lambda"])
    (doa, dor, dh1, mix_t, h1n, act_t, dgate_t, dup_t, dyb, loss, dga, dgr, dg2) = _post(
        oa, orn, h0, tgt, local["attn_out_g"], local["rnn_out_g"], local["ln2_g"], late["w_out"], late["w_gate"],
        late["w_up"], late["w_down"])
    dw_gate, dw_up = _matmul_shards("dw_gate_up", [dgate_t, dup_t], h1n)
    wire = {"w_out": _matmul_shards("dw_out", [mix_t], dh1)[0], "w_gate": dw_gate, "w_up": dw_up,
            "w_down": _matmul_shards("dw_down", [act_t], dyb)[0]}
    names = ("w_gate",)
    (dxr, dxg, dcw, dcb, dwblk, dgb, dlam), got = _rnn_bwd(xr, xg, dor, rnn_saved, w["conv_w"], wblk, w["lru_lambda"],
                                                           *ex.scatter_srcs(names, wire))
    summed = ex.scattered(names, wire, got)
    dwa, dwi = _unblock_gates(dwblk)
    dgb = jnp.transpose(dgb.reshape(nblk, 4, LANES), (1, 0, 2)).reshape(4, D_RNN)
    names = ("w_out", "w_up", "w_down")
    (dq_r, dk_r, dv), got = _attn_bwd(q, k, v, doa, oa, probs, *ex.scatter_srcs(names, wire))
    summed.update(ex.scattered(names, wire, got))
    wire = ex.to_wire({
        "conv_w": dcw, "conv_b": dcb, "lru_wa": dwa.reshape(-1, RNN_BW), "lru_ba": jnp.stack([dgb[0], dgb[2]]),
        "lru_wi": dwi.reshape(-1, RNN_BW), "lru_bi": jnp.stack([dgb[1], dgb[3]]), "lru_lambda": dlam,
        "attn_out_g": dga, "rnn_out_g": dgr, "ln2_g": dg2, "loss": loss})
    names = tuple(wire)
    (dp, qa, kva, dqp, dkv, dqg, dkg, dgqa, dgkva), got = _qkv_bwd(
        cq, ckv, kr, dq_r, dk_r, dv, dxr, dxg, local["q_a_norm_g"], local["kv_a_norm_g"], w_uq_p, w_uk_p, w_v, qg, kg,
        rc, rs1, rs2, *ex.scatter_srcs(names, wire))
    summed.update(ex.scattered(names, wire, got))
    dw_uq_p, _ = _matmul_tn("dw_uq", dqp, qa)
    dw_kv, _ = _matmul_tn("dw_ukv", kva, dkv)
    dw_uq = dw_uq_p.reshape(N_HEADS, HEAD_PAD, Q_LORA)[:, :QK_HEAD].reshape(N_HEADS * QK_HEAD, Q_LORA)
    dw_ukv = jnp.concatenate([dw_kv[:, :QP_COLS].reshape(KV_LORA, N_HEADS, HEAD_PAD)[:, :, :QK_NOPE],
                              dw_kv[:, QP_COLS:].reshape(KV_LORA, N_HEADS, V_HEAD)], axis=2).reshape(KV_LORA, -1)
    wire = ex.to_wire({"q_a_norm_g": dgqa, "w_uq": dw_uq, "kv_a_norm_g": dgkva, "w_ukv": dw_ukv,
                       "q_norm_g": dqg[:, :QK_HEAD], "k_norm_g": dkg[:, :QK_HEAD]})
    names = tuple(wire)
    dw_in_p, got = _matmul_tn("dw_in", dp, hn, *ex.scatter_srcs(names, wire))
    summed.update(ex.scattered(names, wire, got))
    kr0 = OFF_CKV + 2 * D_RNN + QK_NOPE
    dw_in = jnp.concatenate([dw_in_p[:OFF_CKV], dw_in_p[kr0:kr0 + QK_ROPE], dw_in_p[OFF_CKV:OFF_CKV + 2 * D_RNN]], axis=0)
    wire = ex.to_wire({"w_in": dw_in})
    (dh0, dg1), got = _in_bwd(dp, h0, dh1, local["ln1_g"], w_in_p, *ex.scatter_srcs(("w_in",), wire))
    summed.update(ex.scattered(("w_in",), wire, got))

    dh0 = dh0.reshape(nb, t, D_MODEL)
    wire = ex.to_wire({"meta_tokens": jnp.sum(dh0[:, PAD_ROWS:PAD_ROWS + N_META], axis=0), "ln1_g": dg1})
    got = ex.run("reduce_last", *ex.scatter_srcs(G_LAST, wire))
    summed.update(ex.scattered(G_LAST, wire, got))
    return dh0[:, PAD_ROWS + N_META:], summed


class _MeshExchange:
    def __init__(self, shards):
        self.local = shards

    @staticmethod
    def run(name, srcs, scatter):
        return _exchange(name, srcs, scatter)

    def gather_srcs(self, names):
        return [self.local[k].astype(BF16) if k in BIG else self.local[k] for k in names], [False] * len(names)

    @staticmethod
    def gathered(names, outs):
        return {k: g.reshape(-1, g.shape[-1]) if k in ROW_SHARDED else _cols_from_shards(g) for k, g in zip(names, outs)}

    @staticmethod
    def to_wire(grads):
        wire = {}
        for k, g in grads.items():
            if k in WHOLE:
                wire[k] = g
            elif k in ROW_SHARDED:
                wire[k] = g.reshape(N_DEV, -1, g.shape[-1]).astype(BF16)
            else:
                wire[k] = _cols_to_shards(g).astype(BF16) if k in BIG else _cols_to_shards(g)
        return wire

    @staticmethod
    def scatter_srcs(names, wire):
        return [wire[k] for k in names], [k not in WHOLE for k in names]

    @staticmethod
    def scattered(names, wire, outs):
        return dict(zip(names, outs))


def kernel(x, meta_tokens, ln1_g, w_in, q_a_norm_g, w_uq, kv_a_norm_g, w_ukv, q_norm_g, k_norm_g, conv_w, conv_b, lru_wa, lru_ba, lru_wi, lru_bi, lru_lambda, attn_out_g, rnn_out_g, w_out, ln2_g, w_gate, w_up, w_down, loss_target, m_meta_tokens, m_ln1_g, m_w_in, m_q_a_norm_g, m_w_uq, m_kv_a_norm_g, m_w_ukv, m_q_norm_g, m_k_norm_g, m_conv_w, m_conv_b, m_lru_wa, m_lru_ba, m_lru_wi, m_lru_bi, m_lru_lambda, m_attn_out_g, m_rnn_out_g, m_w_out, m_ln2_g, m_w_gate, m_w_up, m_w_down, v_meta_tokens, v_ln1_g, v_w_in, v_q_a_norm_g, v_w_uq, v_kv_a_norm_g, v_w_ukv, v_q_norm_g, v_k_norm_g, v_conv_w, v_conv_b, v_lru_wa, v_lru_ba, v_lru_wi, v_lru_bi, v_lru_lambda, v_attn_out_g, v_rnn_out_g, v_w_out, v_ln2_g, v_w_gate, v_w_up, v_w_down):
    given = (meta_tokens, ln1_g, w_in, q_a_norm_g, w_uq, kv_a_norm_g, w_ukv, q_norm_g, k_norm_g, conv_w, conv_b,
             lru_wa, lru_ba, lru_wi, lru_bi, lru_lambda, attn_out_g, rnn_out_g, w_out, ln2_g, w_gate, w_up, w_down)
    moments_m = (m_meta_tokens, m_ln1_g, m_w_in, m_q_a_norm_g, m_w_uq, m_kv_a_norm_g, m_w_ukv, m_q_norm_g, m_k_norm_g,
                 m_conv_w, m_conv_b, m_lru_wa, m_lru_ba, m_lru_wi, m_lru_bi, m_lru_lambda, m_attn_out_g, m_rnn_out_g,
                 m_w_out, m_ln2_g, m_w_gate, m_w_up, m_w_down)
    moments_v = (v_meta_tokens, v_ln1_g, v_w_in, v_q_a_norm_g, v_w_uq, v_kv_a_norm_g, v_w_ukv, v_q_norm_g, v_k_norm_g,
                 v_conv_w, v_conv_b, v_lru_wa, v_lru_ba, v_lru_wi, v_lru_bi, v_lru_lambda, v_attn_out_g, v_rnn_out_g,
                 v_w_out, v_ln2_g, v_w_gate, v_w_up, v_w_down)
    shapes = {k: a.shape for k, a in zip(WEIGHTS, given)}

    def two_d(k, a):
        a = a.reshape(-1, a.shape[-1])
        return a.T if k in TRANSPOSED else a

    w = {k: two_d(k, a) for k, a in zip(WEIGHTS, given)}
    m = {k: two_d(k, a) for k, a in zip(WEIGHTS, moments_m)}
    v = {k: two_d(k, a) for k, a in zip(WEIGHTS, moments_v)}

    grad_x, parts = _local_step(x, loss_target, _MeshExchange(w))

    tiled = ("w_in", "w_gate", "w_up", "w_down")
    new = {k: _adamw("adamw_" + k, parts[k], w[k], m[k], v[k]) for k in tiled}
    small = [k for k in WEIGHTS if k not in tiled]
    new.update(zip(small, _adamw_many("adamw_small", [(parts[k], w[k], m[k], v[k]) for k in small])))

    loss = jnp.sum(parts["loss"][:, 0, 0])
    outs = [loss, grad_x]
    for idx in range(4):
        outs += [(new[k][idx].T if k in TRANSPOSED else new[k][idx]).reshape(shapes[k]) for k in WEIGHTS]
    return tuple(outs)
```

```python
import functools
import math

import numpy as np
import jax
import jax.numpy as jnp
from jax import lax
from jax.experimental import pallas as pl
from jax.experimental.pallas import tpu as pltpu

F32 = jnp.float32
BF16 = jnp.bfloat16

D_MODEL = 1024
N_META = 16
SEQ = 2048
N_HEADS = 8
QK_NOPE = 64
QK_ROPE = 32
QK_HEAD = QK_NOPE + QK_ROPE
V_HEAD = 64
D_ATTN = N_HEADS * V_HEAD
Q_LORA = 384
KV_LORA = 256
D_RNN = 512
RNN_BW = 64
D_FF = 2816
EPS = 1e-6
LRU_C = 8.0
ROPE_THETA = 10000.0
OFF_CKV = Q_LORA + KV_LORA
OFF_KR = OFF_CKV + QK_ROPE
IN_COLS = OFF_KR + 2 * D_RNN

ADAM_LR = 0.001
ADAM_B1 = 0.9
ADAM_B2 = 0.999
ADAM_EPS = 1e-08
ADAM_WD = 0.01
ADAM_STEP = 10

N_DEV = 8
LANES = 128
HEAD_PAD = LANES
PAD_ROWS = LANES - N_META
QP_COLS = N_HEADS * HEAD_PAD
P_COLS = OFF_CKV + 2 * D_RNN + LANES
FF_CHUNK = D_FF
VMEM_LIMIT = 56 * 1024 * 1024
MESH = pl.DeviceIdType.MESH


def _t_pad():
    return PAD_ROWS + N_META + SEQ


def _row_tile(n):
    return 256 if n % 256 == 0 else 128


def _wide_row_tile(n):
    quarter = _t_pad() // 4
    return quarter if quarter % 16 == 0 and n % quarter == 0 else _row_tile(n)


def _const_spec(shape):
    nd = len(shape)
    return pl.BlockSpec(shape, lambda *_: (0,) * nd, pipeline_mode=pl.Buffered(1))


def _rms(x, d):
    r = lax.rsqrt(jnp.sum(x * x, axis=-1, keepdims=True) * (1.0 / d) + EPS)
    return x * r, r


def _rms_bwd(dy, xhat, r, g, d):
    dxh = dy * g
    return r * (dxh - xhat * (jnp.sum(dxh * xhat, axis=-1, keepdims=True) * (1.0 / d)))


def _colsum(x):
    return jnp.sum(x, axis=0, keepdims=True)


def _dot(a, b):
    return jnp.dot(a, b, preferred_element_type=F32)


def _dot_nt(a, b):
    return lax.dot_general(a, b, (((1,), (1,)), ((), ())), preferred_element_type=F32)


def _dot_tn(a, b):
    return lax.dot_general(a, b, (((0,), (0,)), ((), ())), preferred_element_type=F32)


def _rope(x, c, s1, s2):
    return x * c + pltpu.roll(x, 16, 1) * s1 + pltpu.roll(x, HEAD_PAD - 16, 1) * s2


def _rope_bwd(dy, c, s1, s2):
    return dy * c + pltpu.roll(dy * s1, HEAD_PAD - 16, 1) + pltpu.roll(dy * s2, 16, 1)


def _acc(ref, first, val):
    @pl.when(first)
    def _():
        ref[...] = val

    @pl.when(jnp.logical_not(first))
    def _():
        ref[...] += val


def _in_proj(h0, ln1_g, w_in_p, srcs=(), scatter=()):
    n = h0.shape[0]
    tm = _wide_row_tile(n)
    nk = len(srcs)
    c_in, c_out, c_shape, c_sems = _exchange_specs(srcs, scatter)

    def body(h_ref, g_ref, w_ref, *rest):
        hn_ref, cq_ref, ckv_ref, xr_ref, xg_ref, kr_ref = rest[nk:nk + 6]
        finish = _ride(1, *_exchange_fns(rest[:nk], rest[nk + 6:2 * nk + 6], rest[2 * nk + 6:], scatter))
        xhat, _ = _rms(h_ref[...], D_MODEL)
        hn = (xhat * g_ref[...]).astype(BF16)
        hn_ref[...] = hn
        p = _dot_nt(hn, w_ref[...])
        cq_ref[...] = p[:, :Q_LORA]
        ckv_ref[...] = p[:, Q_LORA:OFF_CKV]
        xr_ref[...] = p[:, OFF_CKV:OFF_CKV + D_RNN]
        xg_ref[...] = p[:, OFF_CKV + D_RNN:OFF_CKV + 2 * D_RNN]
        kr_ref[...] = p[:, OFF_CKV + 2 * D_RNN:]
        finish()

    def row(w):
        return pl.BlockSpec((tm, w), lambda i: (i, 0))

    widths = (D_MODEL, Q_LORA, KV_LORA, D_RNN, D_RNN, LANES)
    res = pl.pallas_call(
        body, name="in_proj", grid=(n // tm,),
        in_specs=[row(D_MODEL), _const_spec((1, D_MODEL)), _const_spec((P_COLS, D_MODEL))] + c_in,
        out_specs=[row(w) for w in widths] + c_out,
        out_shape=[jax.ShapeDtypeStruct((n, w), BF16 if k == 0 else F32) for k, w in enumerate(widths)] + c_shape,
        scratch_shapes=c_sems,
        compiler_params=pltpu.CompilerParams(dimension_semantics=("arbitrary",), vmem_limit_bytes=VMEM_LIMIT),
    )(h0, ln1_g, w_in_p, *srcs)
    return res[:6], res[6:]


def _qkv_fwd(cq, ckv, kr, gqa, gkva, w_uq_p, w_uk_p, w_v, qg, kg, rc, rs1, rs2):
    n = cq.shape[0]
    tm = _wide_row_tile(n)

    def body(cq_ref, ckv_ref, kr_ref, gqa_ref, gkva_ref, wuq_ref, wuk_ref, wv_ref, qg_ref, kg_ref,
             c_ref, s1_ref, s2_ref, q_ref, k_ref, v_ref):
        xq, _ = _rms(cq_ref[...], Q_LORA)
        qa = (xq * gqa_ref[...]).astype(BF16)
        q = _dot_nt(qa, wuq_ref[...])
        xkv, _ = _rms(ckv_ref[...], KV_LORA)
        kva = (xkv * gkva_ref[...]).astype(BF16)
        kn = _dot(kva, wuk_ref[...])
        v_ref[...] = _dot(kva, wv_ref[...]).astype(BF16)
        krp = kr_ref[...]
        c, s1, s2 = c_ref[...], s1_ref[...], s2_ref[...]
        for h in range(N_HEADS):
            sl = slice(h * HEAD_PAD, (h + 1) * HEAD_PAD)
            qh, _ = _rms(q[:, sl], QK_HEAD)
            q_ref[:, sl] = _rope(qh * qg_ref[...], c, s1, s2).astype(BF16)
            kh, _ = _rms(kn[:, sl] + krp, QK_HEAD)
            k_ref[:, sl] = _rope(kh * kg_ref[...], c, s1, s2).astype(BF16)

    def row(w):
        return pl.BlockSpec((tm, w), lambda i: (i, 0))

    return pl.pallas_call(
        body, name="qkv_fwd", grid=(n // tm,),
        in_specs=[row(Q_LORA), row(KV_LORA), row(LANES), _const_spec((1, Q_LORA)), _const_spec((1, KV_LORA)),
                  _const_spec((QP_COLS, Q_LORA)), _const_spec((KV_LORA, QP_COLS)), _const_spec((KV_LORA, D_ATTN)),
                  _const_spec((1, LANES)), _const_spec((1, LANES)), row(LANES), row(LANES), row(LANES)],
        out_specs=[row(QP_COLS), row(QP_COLS), row(D_ATTN)],
        out_shape=[jax.ShapeDtypeStruct((n, QP_COLS), BF16), jax.ShapeDtypeStruct((n, QP_COLS), BF16),
                   jax.ShapeDtypeStruct((n, D_ATTN), BF16)],
        compiler_params=pltpu.CompilerParams(dimension_semantics=("parallel",), vmem_limit_bytes=VMEM_LIMIT),
    )(cq, ckv, kr, gqa, gkva, w_uq_p, w_uk_p, w_v, qg, kg, rc, rs1, rs2)


def _qkv_bwd(cq, ckv, kr, dq_r, dk_r, dv, dxr, dxg, gqa, gkva, w_uq_p, w_uk_p, w_v, qg, kg, rc, rs1, rs2,
             srcs=(), scatter=()):
    n = cq.shape[0]
    tm = _wide_row_tile(n)
    nk = len(srcs)
    c_in, c_out, c_shape, c_sems = _exchange_specs(srcs, scatter)

    def body(cq_ref, ckv_ref, kr_ref, dq_ref, dk_ref, dv_ref, dxr_ref, dxg_ref, gqa_ref, gkva_ref, wuq_ref, wuk_ref,
             wv_ref, qg_ref, kg_ref, c_ref, s1_ref, s2_ref, *rest):
        dp_ref, qa_ref, kva_ref, dqp_ref, dkv_ref, dqg_ref, dkg_ref, dgqa_ref, dgkva_ref = rest[nk:nk + 9]
        finish = _ride(1, *_exchange_fns(rest[:nk], rest[nk + 9:2 * nk + 9], rest[2 * nk + 9:], scatter))
        first = pl.program_id(0) == 0
        dp_ref[:, OFF_CKV:OFF_CKV + D_RNN] = dxr_ref[...].astype(BF16)
        dp_ref[:, OFF_CKV + D_RNN:OFF_CKV + 2 * D_RNN] = dxg_ref[...].astype(BF16)
        xq, rq = _rms(cq_ref[...], Q_LORA)
        qa = (xq * gqa_ref[...]).astype(BF16)
        qa_ref[...] = qa
        q = _dot_nt(qa, wuq_ref[...])
        xkv, rkv = _rms(ckv_ref[...], KV_LORA)
        kva = (xkv * gkva_ref[...]).astype(BF16)
        kva_ref[...] = kva
        kn = _dot(kva, wuk_ref[...])
        krp = kr_ref[...]
        c, s1, s2 = c_ref[...], s1_ref[...], s2_ref[...]
        lane = lax.broadcasted_iota(jnp.int32, (tm, HEAD_PAD), 1)
        rope_lanes = jnp.logical_and(lane >= QK_NOPE, lane < QK_HEAD)
        dqg = jnp.zeros((1, HEAD_PAD), F32)
        dkg = jnp.zeros((1, HEAD_PAD), F32)
        dkr = jnp.zeros((tm, HEAD_PAD), F32)
        for h in range(N_HEADS):
            sl = slice(h * HEAD_PAD, (h + 1) * HEAD_PAD)
            qh, rqh = _rms(q[:, sl], QK_HEAD)
            dy = _rope_bwd(dq_ref[:, sl], c, s1, s2)
            dqg = dqg + _colsum(dy * qh)
            dqp_ref[:, sl] = _rms_bwd(dy, qh, rqh, qg_ref[...], QK_HEAD).astype(BF16)
            kh, rkh = _rms(kn[:, sl] + krp, QK_HEAD)
            dyk = _rope_bwd(dk_ref[:, sl], c, s1, s2)
            dkg = dkg + _colsum(dyk * kh)
            dkh = _rms_bwd(dyk, kh, rkh, kg_ref[...], QK_HEAD)
            dkv_ref[:, sl] = dkh.astype(BF16)
            dkr = dkr + jnp.where(rope_lanes, dkh, 0.0)
        dkv_ref[:, QP_COLS:] = dv_ref[...].astype(BF16)
        dp_ref[:, OFF_CKV + 2 * D_RNN:] = dkr.astype(BF16)
        dqa = _dot(dqp_ref[...], wuq_ref[...])
        dp_ref[:, :Q_LORA] = _rms_bwd(dqa, xq, rq, gqa_ref[...], Q_LORA).astype(BF16)
        dkva = _dot_nt(dkv_ref[:, :QP_COLS], wuk_ref[...]) + _dot_nt(dkv_ref[:, QP_COLS:], wv_ref[...])
        dp_ref[:, Q_LORA:OFF_CKV] = _rms_bwd(dkva, xkv, rkv, gkva_ref[...], KV_LORA).astype(BF16)
        _acc(dqg_ref, first, dqg)
        _acc(dkg_ref, first, dkg)
        _acc(dgqa_ref, first, _colsum(dqa * xq))
        _acc(dgkva_ref, first, _colsum(dkva * xkv))
        finish()

    def row(w):
        return pl.BlockSpec((tm, w), lambda i: (i, 0))

    def acc(w):
        return pl.BlockSpec((1, w), lambda i: (0, 0))

    res = pl.pallas_call(
        body, name="qkv_bwd", grid=(n // tm,),
        in_specs=[row(Q_LORA), row(KV_LORA), row(LANES), row(QP_COLS), row(QP_COLS), row(D_ATTN), row(D_RNN), row(D_RNN),
                  _const_spec((1, Q_LORA)), _const_spec((1, KV_LORA)),
                  _const_spec((QP_COLS, Q_LORA)), _const_spec((KV_LORA, QP_COLS)), _const_spec((KV_LORA, D_ATTN)),
                  _const_spec((1, LANES)), _const_spec((1, LANES)), row(LANES), row(LANES), row(LANES)] + c_in,
        out_specs=[row(P_COLS), row(Q_LORA), row(KV_LORA), row(QP_COLS),
                   row(QP_COLS + D_ATTN), acc(LANES), acc(LANES), acc(Q_LORA), acc(KV_LORA)] + c_out,
        out_shape=[jax.ShapeDtypeStruct((n, P_COLS), BF16), jax.ShapeDtypeStruct((n, Q_LORA), BF16),
                   jax.ShapeDtypeStruct((n, KV_LORA), BF16), jax.ShapeDtypeStruct((n, QP_COLS), BF16),
                   jax.ShapeDtypeStruct((n, QP_COLS + D_ATTN), BF16),
                   jax.ShapeDtypeStruct((1, LANES), F32), jax.ShapeDtypeStruct((1, LANES), F32),
                   jax.ShapeDtypeStruct((1, Q_LORA), F32), jax.ShapeDtypeStruct((1, KV_LORA), F32)] + c_shape,
        scratch_shapes=c_sems,
        compiler_params=pltpu.CompilerParams(dimension_semantics=("arbitrary",), vmem_limit_bytes=VMEM_LIMIT),
    )(cq, ckv, kr, dq_r, dk_r, dv, dxr, dxg, gqa, gkva, w_uq_p, w_uk_p, w_v, qg, kg, rc, rs1, rs2, *srcs)
    return res[:9], res[9:]


FWD_KEY_CHUNK = 2 * LANES
BWD_KEY_CHUNK = 4 * LANES


def _key_chunks(t, chunk):
    count = max(t // chunk, 1)
    first = t - chunk * (count - 1)
    return [(0, first)] + [(first + chunk * c, chunk) for c in range(count - 1)]


def _attn_specs(t, tq):
    nq = t // tq
    qspec = pl.BlockSpec((tq, 2 * HEAD_PAD), lambda b, hp, i: (b * nq + i, hp))
    kspec = pl.BlockSpec((t, 2 * HEAD_PAD), lambda b, hp, i: (b, hp))
    vspec = pl.BlockSpec((t, 2 * V_HEAD), lambda b, hp, i: (b, hp))
    ospec = pl.BlockSpec((tq, 2 * V_HEAD), lambda b, hp, i: (b * nq + i, hp))
    return nq, qspec, kspec, vspec, ospec


def _probs_spec(t, tq):
    return pl.BlockSpec((1, 2, tq, t), lambda b, hp, i: (b, hp, i, 0))


def _attn_fwd(q, k, v, srcs=(), scatter=()):
    n = q.shape[0]
    t = _t_pad()
    tq = t // 2
    nq, qspec, kspec, vspec, ospec = _attn_specs(t, tq)
    nk = len(srcs)
    c_in, c_out, c_shape, c_sems = _exchange_specs(srcs, scatter)

    def body(q_ref, k_ref, v_ref, *rest):
        o_ref, l_ref, p_ref = rest[nk:nk + 3]
        finish = _ride(3, *_exchange_fns(rest[:nk], rest[nk + 3:2 * nk + 3], rest[2 * nk + 3:], scatter))
        lane = lax.broadcasted_iota(jnp.int32, (tq, 2 * V_HEAD), 1)
        outs = []
        sums = []
        for j in range(2):
            sl = slice(j * HEAD_PAD, (j + 1) * HEAD_PAD)
            qh = q_ref[:, sl]

            def scores(start, size):
                s = _dot_nt(qh, k_ref[start:start + size, sl])
                if start < PAD_ROWS:
                    key = lax.broadcasted_iota(jnp.int32, (tq, size), 1) + start
                    s = jnp.where(key >= PAD_ROWS, s, -jnp.inf)
                return s

            chunks = _key_chunks(t, FWD_KEY_CHUNK)

            def lane_folds(x):
                return [x[:, c:c + LANES] for c in range(0, x.shape[1], LANES)]

            top = functools.reduce(jnp.maximum, [blk for c in chunks for blk in lane_folds(scores(*c))])
            top = jnp.max(top, axis=-1, keepdims=True)
            l = jnp.zeros((tq, LANES), F32)
            pv = jnp.zeros((tq, 2 * V_HEAD), F32)
            for start, size in chunks:
                e = jnp.exp2((scores(start, size) - top) * (QK_HEAD ** -0.5 * math.log2(math.e)))
                l = functools.reduce(jnp.add, lane_folds(e), l)
                e = e.astype(BF16)
                p_ref[0, j, :, start:start + size] = e
                pv = pv + _dot(e, v_ref[start:start + size, :])
            l = jnp.sum(l, axis=-1, keepdims=True)
            outs.append(pv / l)
            sums.append(l)
        o_ref[...] = jnp.where(lane < V_HEAD, outs[0], outs[1])
        l_ref[...] = jnp.where(lane < V_HEAD, sums[0], sums[1])
        finish()

    res = pl.pallas_call(
        body, name="attn_fwd", grid=(n // t, N_HEADS // 2, nq),
        in_specs=[qspec, kspec, vspec] + c_in, out_specs=[ospec, ospec, _probs_spec(t, tq)] + c_out,
        out_shape=[jax.ShapeDtypeStruct((n, D_ATTN), F32), jax.ShapeDtypeStruct((n, D_ATTN), F32),
                   jax.ShapeDtypeStruct((n // t, N_HEADS, t, t), BF16)] + c_shape,
        scratch_shapes=c_sems,
        compiler_params=pltpu.CompilerParams(dimension_semantics=("arbitrary", "arbitrary", "arbitrary"),
                                             vmem_limit_bytes=VMEM_LIMIT),
    )(q, k, v, *srcs)
    return res[0], (res[1], res[2]), res[3:]


def _attn_bwd(q, k, v, do, o, probs, srcs=(), scatter=()):
    n = q.shape[0]
    t = _t_pad()
    tq = t // 2
    nq, qspec, kspec, vspec, ospec = _attn_specs(t, tq)
    nk = len(srcs)
    c_in, c_out, c_shape, c_sems = _exchange_specs(srcs, scatter)

    def body(q_ref, k_ref, v_ref, do_ref, o_ref, l_ref, p_ref, *rest):
        dq_ref, dk_ref, dv_ref = rest[nk:nk + 3]
        finish = _ride(3, *_exchange_fns(rest[:nk], rest[nk + 3:2 * nk + 3], rest[2 * nk + 3:], scatter))

        @pl.when(pl.program_id(2) == 0)
        def _():
            dk_ref[...] = jnp.zeros_like(dk_ref)
            dv_ref[...] = jnp.zeros_like(dv_ref)

        lane = lax.broadcasted_iota(jnp.int32, (tq, 2 * V_HEAD), 1)
        do = do_ref[...]
        do_o = do * o_ref[...]
        chunks = _key_chunks(t, BWD_KEY_CHUNK)
        dvs = [None] * len(chunks)
        for j in range(2):
            sl = slice(j * HEAD_PAD, (j + 1) * HEAD_PAD)
            qh = q_ref[:, sl]
            in_head = (lane < V_HEAD) if j == 0 else (lane >= V_HEAD)
            inv_l = 1.0 / l_ref[:, j * V_HEAD:j * V_HEAD + 1]
            doh = jnp.where(in_head, do, 0.0).astype(BF16)
            doh_n = jnp.where(in_head, do * inv_l, 0.0).astype(BF16)
            delta = jnp.sum(jnp.where(in_head, do_o, 0.0), axis=-1, keepdims=True)
            row_scale = inv_l * (QK_HEAD ** -0.5)
            dq = jnp.zeros((tq, HEAD_PAD), F32)
            for c, (start, size) in enumerate(chunks):
                rows = slice(start, start + size)
                e = p_ref[0, j, :, rows]
                dp = _dot_nt(doh, v_ref[rows, :])
                ds = (e.astype(F32) * (dp - delta) * row_scale).astype(BF16)
                dq = dq + _dot(ds, k_ref[rows, sl])
                dk_ref[rows, sl] += _dot_tn(ds, qh)
                dvc = _dot_tn(e, doh_n)
                dvs[c] = dvc if dvs[c] is None else dvs[c] + dvc
            dq_ref[:, sl] = dq
        for (start, size), dvc in zip(chunks, dvs):
            dv_ref[start:start + size, :] += dvc
        finish()

    res = pl.pallas_call(
        body, name="attn_bwd", grid=(n // t, N_HEADS // 2, nq),
        in_specs=[qspec, kspec, vspec, ospec, ospec, ospec, _probs_spec(t, tq)] + c_in,
        out_specs=[qspec, kspec, vspec] + c_out,
        out_shape=[jax.ShapeDtypeStruct((n, QP_COLS), F32), jax.ShapeDtypeStruct((n, QP_COLS), F32),
                   jax.ShapeDtypeStruct((n, D_ATTN), F32)] + c_shape, scratch_shapes=c_sems,
        compiler_params=pltpu.CompilerParams(dimension_semantics=("arbitrary", "arbitrary", "arbitrary"),
                                             vmem_limit_bytes=VMEM_LIMIT),
    )(q, k, v, do, o, *probs, *srcs)
    return res[:3], res[3:]


SCAN_STEPS = 8


def _scan(chains, t):
    seg = t // 8
    rows = lax.broadcasted_iota(jnp.int32, (8, LANES), 0)

    def step(i, carry):
        carry = list(carry)
        for u in range(SCAN_STEPS):
            j = i * SCAN_STEPS + u
            for n, (a_ref, b_ref, h_ref, p_ref, reverse) in enumerate(chains):
                h, p = carry[n]
                idx = pl.ds(seg - 1 - j if reverse else j, 8, stride=seg)
                a = a_ref[idx, :]
                h = a * h + b_ref[idx, :]
                p = a * p
                h_ref[idx, :] = h
                p_ref[idx, :] = p
                carry[n] = (h, p)
        return tuple(carry)

    init = tuple((jnp.zeros((8, LANES), F32), jnp.ones((8, LANES), F32)) for _ in chains)
    ends = lax.fori_loop(0, seg // SCAN_STEPS, step, init)
    for (_, _, h_ref, p_ref, reverse), (b, a) in zip(chains, ends):
        for d in (1, 2, 4):
            if reverse:
                keep = rows < 8 - d
                a_n, b_n = pltpu.roll(a, 8 - d, 0), pltpu.roll(b, 8 - d, 0)
            else:
                keep = rows >= d
                a_n, b_n = pltpu.roll(a, d, 0), pltpu.roll(b, d, 0)
            b = a * jnp.where(keep, b_n, 0.0) + b
            a = a * jnp.where(keep, a_n, 1.0)
        for s in (range(7) if reverse else range(1, 8)):
            sl = slice(s * seg, (s + 1) * seg)
            carry_in = b[s + 1:s + 2, :] if reverse else b[s - 1:s, :]
            h_ref[sl, :] = h_ref[sl, :] + p_ref[sl, :] * carry_in


def _shift_rows(x, s, rows, t):
    if s == 0:
        return x
    rolled = pltpu.roll(x, s % t, 0)
    return jnp.where(rows >= s, rolled, 0.0) if s > 0 else jnp.where(rows < t + s, rolled, 0.0)


def _neg_expm1_twice(h, exp_2h):
    series = h * (-2.0 + h * (-2.0 + h * (-4.0 / 3 + h * (-2.0 / 3))))
    return jnp.where(h > -0.05, series, 1.0 - exp_2h)


def _sigmoid(x):
    return 0.5 * jnp.tanh(0.5 * x) + 0.5


def _gelu_parts(x):
    k = math.sqrt(2.0 / math.pi)
    th = jnp.tanh(k * (x + 0.044715 * x * x * x))
    g = 0.5 * x * (1.0 + th)
    dg = 0.5 * (1.0 + th) + 0.5 * x * (1.0 - th * th) * k * (1.0 + 3 * 0.044715 * x * x)
    return g, dg


def _lru_gates(xc, gates, lam_ref, valid, d):
    r = _sigmoid(gates[:, (2 * d) * LANES:(2 * d + 1) * LANES])
    i = _sigmoid(gates[:, (2 * d + 1) * LANES:(2 * d + 2) * LANES])
    neg_lam = -lam_ref[d:d + 1, :]
    sp = jnp.maximum(neg_lam, 0.0) + jnp.log1p(jnp.exp(-jnp.abs(neg_lam)))
    log_a = -LRU_C * r * sp
    a = jnp.exp(log_a)
    m = jnp.maximum(_neg_expm1_twice(log_a, a * a), 0.0)
    sq = jnp.sqrt(m)
    b = jnp.where(valid, sq * (i * xc), 0.0)
    return r, i, sp, a, m, sq, b


def _conv(xr, cw_ref, cb_ref, rows, t):
    return (cw_ref[0:1, :] * _shift_rows(xr, 2, rows, t) + cw_ref[1:2, :] * _shift_rows(xr, 1, rows, t)
            + cw_ref[2:3, :] * xr + cw_ref[3:4, :] * _shift_rows(xr, -1, rows, t) + cb_ref[...])


def _rnn_specs(t):
    seq = pl.BlockSpec((t, LANES), lambda cb, b: (b, cb))
    cw = pl.BlockSpec((4, LANES), lambda cb, b: (0, cb))
    vec1 = pl.BlockSpec((1, LANES), lambda cb, b: (0, cb))
    vec2 = pl.BlockSpec((2, LANES), lambda cb, b: (0, cb))
    wblk = pl.BlockSpec((1, LANES, 4 * LANES), lambda cb, b: (cb, 0, 0))
    gbias = pl.BlockSpec((1, 1, 4 * LANES), lambda cb, b: (cb, 0, 0))
    return seq, cw, vec1, vec2, wblk, gbias


def _rnn_fwd(xr, xg, conv_w, conv_b, wblk, gbias, lam):
    n = xr.shape[0]
    t = _t_pad()
    seq, cw, vec1, vec2, wspec, gspec = _rnn_specs(t)
    both = pl.BlockSpec((2, t, LANES), lambda cb, b: (0, b, cb))

    def body(xr_ref, xg_ref, cw_ref, cb_ref, w_ref, gb_ref, lam_ref,
             o_ref, xc_ref, r_ref, i_ref, q_ref, h_ref, a_s, b_s, p_s):
        rows = lax.broadcasted_iota(jnp.int32, (t, LANES), 0)
        valid = rows >= PAD_ROWS
        xc = _conv(xr_ref[...], cw_ref, cb_ref, rows, t)
        xc_ref[...] = xc
        gates = _dot(xc.astype(BF16), w_ref[0]) + gb_ref[0]
        for d in range(2):
            r_ref[d], i_ref[d], _, a_s[d], _, q_ref[d], b_s[d] = _lru_gates(xc, gates, lam_ref, valid, d)
        _scan([(a_s.at[d], b_s.at[d], h_ref.at[d], p_s.at[d], d == 1) for d in range(2)], t)
        g, _ = _gelu_parts(xg_ref[...])
        o_ref[...] = (h_ref[0] + h_ref[1]) * g

    stacked = jax.ShapeDtypeStruct((2, n, D_RNN), F32)
    res = pl.pallas_call(
        body, name="rnn_fwd", grid=(D_RNN // LANES, n // t),
        in_specs=[seq, seq, cw, vec1, wspec, gspec, vec2], out_specs=[seq, seq, both, both, both, both, both],
        out_shape=[jax.ShapeDtypeStruct((n, D_RNN), F32), jax.ShapeDtypeStruct((n, D_RNN), F32)] + [stacked] * 5,
        scratch_shapes=[pltpu.VMEM((2, t, LANES), F32)] * 2,
        compiler_params=pltpu.CompilerParams(dimension_semantics=("parallel", "parallel"), vmem_limit_bytes=VMEM_LIMIT),
    )(xr, xg, conv_w, conv_b, wblk, gbias, lam)
    return res[0], tuple(res[1:])


def _rnn_bwd(xr, xg, do, saved, conv_w, wblk, lam, srcs=(), scatter=()):
    n = xr.shape[0]
    t = _t_pad()
    seq, cw, vec1, vec2, wspec, gspec = _rnn_specs(t)
    nk = len(srcs)
    c_in, c_out, c_shape, c_sems = _exchange_specs(srcs, scatter)

    def body(xr_ref, xg_ref, do_ref, xc_ref, r_s, i_s, q_s, h_s, a_s, cw_ref, w_ref, lam_ref, *rest):
        dxr_ref, dxg_ref, dcw_ref, dcb_ref, dw_ref, dgb_ref, dlam_ref = rest[nk:nk + 7]
        b_s, l_s, p_s, back_s, dg_s = rest[2 * nk + 7 + len(c_sems):]
        finish = _ride(2, *_exchange_fns(rest[:nk], rest[nk + 7:2 * nk + 7], rest[2 * nk + 7:2 * nk + 7 + len(c_sems)],
                                         scatter))
        first = pl.program_id(1) == 0
        rows = lax.broadcasted_iota(jnp.int32, (t, LANES), 0)
        valid = rows >= PAD_ROWS
        xr = xr_ref[...]
        xc = xc_ref[...]
        xcb = xc.astype(BF16)
        g, dg = _gelu_parts(xg_ref[...])
        do = do_ref[...]
        dxg_ref[...] = do * (h_s[0] + h_s[1]) * dg
        b_s[...] = do * g
        sps = []
        for d in range(2):
            neg_lam = -lam_ref[d:d + 1, :]
            sps.append(jnp.maximum(neg_lam, 0.0) + jnp.log1p(jnp.exp(-jnp.abs(neg_lam))))
            back_s[d] = _shift_rows(a_s[d], -1 if d == 0 else 1, rows, t)
        _scan([(back_s.at[d], b_s, l_s.at[d], p_s.at[d], d == 0) for d in range(2)], t)
        dxc = jnp.zeros((t, LANES), F32)
        dlams = []
        for d in range(2):
            r, i, sp, a, sq = r_s[d], i_s[d], sps[d], a_s[d], q_s[d]
            lam_t = l_s[d]
            da = lam_t * _shift_rows(h_s[d], 1 if d == 0 else -1, rows, t)
            lam_v = jnp.where(valid, lam_t, 0.0)
            dsq = lam_v * (i * xc)
            di = lam_v * sq * xc
            dxc = dxc + lam_v * sq * i
            dm = jnp.where(sq > 0.0, dsq * 0.5 / jnp.where(sq > 0.0, sq, 1.0), 0.0)
            dla = da * a - 2.0 * dm * a * a
            dr = dla * (-LRU_C) * sp
            dsp = _colsum(dla * (-LRU_C) * r)
            dlams.append(dsp * -jax.nn.sigmoid(-lam_ref[d:d + 1, :]))
            dg_s[:, (2 * d) * LANES:(2 * d + 1) * LANES] = (dr * r * (1.0 - r)).astype(BF16)
            dg_s[:, (2 * d + 1) * LANES:(2 * d + 2) * LANES] = (di * i * (1.0 - i)).astype(BF16)
        dgates = dg_s[...]
        dxc = dxc + _dot_nt(dgates, w_ref[0])
        taps = [_shift_rows(dxc, j - 2, rows, t) for j in range(4)]
        dxr_ref[...] = (cw_ref[0:1, :] * taps[0] + cw_ref[1:2, :] * taps[1] + cw_ref[2:3, :] * taps[2]
                        + cw_ref[3:4, :] * taps[3])
        dcw = jnp.concatenate([_colsum(tap * xr) for tap in taps], axis=0)
        _acc(dcw_ref, first, dcw)
        _acc(dcb_ref, first, _colsum(dxc))
        _acc(dw_ref, first, _dot_tn(xcb, dgates)[None])
        _acc(dgb_ref, first, _colsum(dgates.astype(F32))[None])
        _acc(dlam_ref, first, jnp.concatenate(dlams, axis=0))
        finish()

    both = pl.BlockSpec((2, t, LANES), lambda cb, b: (0, b, cb))
    pair = pltpu.VMEM((2, t, LANES), F32)
    res = pl.pallas_call(
        body, name="rnn_bwd", grid=(D_RNN // LANES, n // t),
        in_specs=[seq, seq, seq, seq, both, both, both, both, both, cw, wspec, vec2] + c_in,
        out_specs=[seq, seq, cw, vec1, wspec, gspec, vec2] + c_out,
        out_shape=[jax.ShapeDtypeStruct((n, D_RNN), F32), jax.ShapeDtypeStruct((n, D_RNN), F32),
                   jax.ShapeDtypeStruct((4, D_RNN), F32), jax.ShapeDtypeStruct((1, D_RNN), F32),
                   jax.ShapeDtypeStruct((D_RNN // LANES, LANES, 4 * LANES), F32),
                   jax.ShapeDtypeStruct((D_RNN // LANES, 1, 4 * LANES), F32), jax.ShapeDtypeStruct((2, D_RNN), F32)]
        + c_shape,
        scratch_shapes=c_sems + [pltpu.VMEM((t, LANES), F32), pair, pair, pair, pltpu.VMEM((t, 4 * LANES), BF16)],
        compiler_params=pltpu.CompilerParams(dimension_semantics=("arbitrary", "arbitrary"), vmem_limit_bytes=VMEM_LIMIT),
    )(xr, xg, do, *saved, conv_w, wblk, lam, *srcs)
    return res[:7], res[7:]


def _post(oa, orn, h0, tgt, ga, gr, g2, w_out, w_gate, w_up, w_down):
    n = oa.shape[0]
    tm = _row_tile(n)
    t = _t_pad()
    head = PAD_ROWS + N_META
    parts = tm // head

    def body(oa_ref, or_ref, h0_ref, *rest):
        tgt_refs = rest[:parts]
        (ga_ref, gr_ref, g2_ref, wo_ref, wg_ref, wu_ref, wd_ref,
         doa_ref, dor_ref, dh1_ref, mix_ref, h1n_ref, act_ref, dgate_ref, dup_ref, dy_ref,
         loss_ref, dga_ref, dgr_ref, dg2_ref, gate_s, up_s) = rest[parts:]
        first = pl.program_id(0) == 0
        xa, ra = _rms(oa_ref[...], D_ATTN)
        xr, rr = _rms(or_ref[...], D_RNN)
        mix = jnp.concatenate([(xa * ga_ref[...]).astype(BF16), (xr * gr_ref[...]).astype(BF16)], axis=-1)
        mix_ref[...] = mix.T
        h1 = h0_ref[...] + _dot(mix, wo_ref[...])
        x2, r2 = _rms(h1, D_MODEL)
        h1n = (x2 * g2_ref[...]).astype(BF16)
        h1n_ref[...] = h1n
        y = h1
        for cs in range(0, D_FF, FF_CHUNK):
            sl = slice(cs, cs + FF_CHUNK)
            gate = _dot_nt(h1n, wg_ref[sl, :])
            up = _dot_nt(h1n, wu_ref[sl, :])
            gate_s[:, sl] = gate
            up_s[:, sl] = up
            act = (gate * _sigmoid(gate) * up).astype(BF16)
            act_ref[sl, :] = act.T
            y = y + _dot(act, wd_ref[sl, :])
        row = pl.program_id(0) * tm + lax.broadcasted_iota(jnp.int32, (tm, 1), 0)
        for _ in range(1, n // t):
            row = jnp.where(row >= t, row - t, row)
        tgt = jnp.concatenate([ref[0] for ref in tgt_refs], axis=0)
        err = jnp.where(row >= PAD_ROWS + N_META, y - tgt, 0.0)
        _acc(loss_ref, first, jnp.full((1, LANES), 0.5 / D_MODEL, F32) * jnp.sum(err * err))
        dy = err * (1.0 / D_MODEL)
        dyb = dy.astype(BF16)
        dy_ref[...] = dyb
        dh1n = jnp.zeros((tm, D_MODEL), F32)
        for cs in range(0, D_FF, FF_CHUNK):
            sl = slice(cs, cs + FF_CHUNK)
            dact = _dot_nt(dyb, wd_ref[sl, :])
            gate, up = gate_s[:, sl], up_s[:, sl]
            sg = _sigmoid(gate)
            dgate = (dact * up * sg * (1.0 + gate * (1.0 - sg))).astype(BF16)
            dup = (dact * gate * sg).astype(BF16)
            dgate_ref[sl, :] = dgate.T
            dup_ref[sl, :] = dup.T
            dh1n = dh1n + _dot(dgate, wg_ref[sl, :]) + _dot(dup, wu_ref[sl, :])
        _acc(dg2_ref, first, _colsum(dh1n * x2))
        dh1 = dy + _rms_bwd(dh1n, x2, r2, g2_ref[...], D_MODEL)
        dh1_ref[...] = dh1
        dmix = _dot_nt(dh1.astype(BF16), wo_ref[...])
        dma, dmr = dmix[:, :D_ATTN], dmix[:, D_ATTN:]
        _acc(dga_ref, first, _colsum(dma * xa))
        _acc(dgr_ref, first, _colsum(dmr * xr))
        doa_ref[...] = _rms_bwd(dma, xa, ra, ga_ref[...], D_ATTN)
        dor_ref[...] = _rms_bwd(dmr, xr, rr, gr_ref[...], D_RNN)

    def row(w):
        return pl.BlockSpec((tm, w), lambda i: (i, 0))

    def acc(w):
        return pl.BlockSpec((1, w), lambda i: (0, 0))

    def col(w):
        return pl.BlockSpec((w, tm), lambda i: (0, i))

    outs = [(D_ATTN, F32, row), (D_RNN, F32, row), (D_MODEL, F32, row), (D_MODEL, BF16, col), (D_MODEL, BF16, row),
            (D_FF, BF16, col), (D_FF, BF16, col), (D_FF, BF16, col), (D_MODEL, BF16, row)]
    accs = [LANES, D_ATTN, D_RNN, D_MODEL]
    per = t // head

    def target_part(p):
        def index(i):
            block = i * parts + p
            return block // per, jnp.maximum(block % per - 1, 0), 0
        return pl.BlockSpec((1, head, D_MODEL), index)

    return pl.pallas_call(
        body, name="post", grid=(n // tm,),
        in_specs=[row(D_ATTN), row(D_RNN), row(D_MODEL)] + [target_part(p) for p in range(parts)] + [
                  _const_spec((1, D_ATTN)), _const_spec((1, D_RNN)), _const_spec((1, D_MODEL)),
                  _const_spec((D_MODEL, D_MODEL)), _const_spec((D_FF, D_MODEL)), _const_spec((D_FF, D_MODEL)),
                  _const_spec((D_FF, D_MODEL))],
        out_specs=[spec(w) for w, _, spec in outs] + [acc(w) for w in accs],
        out_shape=[jax.ShapeDtypeStruct((n, w) if spec is row else (w, n), dt) for w, dt, spec in outs]
        + [jax.ShapeDtypeStruct((1, w), F32) for w in accs],
        scratch_shapes=[pltpu.VMEM((tm, D_FF), F32), pltpu.VMEM((tm, D_FF), F32)],
        compiler_params=pltpu.CompilerParams(dimension_semantics=("arbitrary",), vmem_limit_bytes=VMEM_LIMIT),
    )(oa, orn, h0, *[tgt] * parts, ga, gr, g2, w_out, w_gate, w_up, w_down)


def _in_bwd(dp, h0, dh1, ln1_g, w_in_p, srcs=(), scatter=()):
    n = h0.shape[0]
    tm = _row_tile(n)
    nk = len(srcs)
    c_in, c_out, c_shape, c_sems = _exchange_specs(srcs, scatter)

    def body(dp_ref, h0_ref, dh1_ref, g_ref, w_ref, *rest):
        dh0_ref, dg_ref = rest[nk:nk + 2]
        finish = _ride(1, *_exchange_fns(rest[:nk], rest[nk + 2:2 * nk + 2], rest[2 * nk + 2:], scatter))
        dhn = _dot(dp_ref[...], w_ref[...])
        xhat, r = _rms(h0_ref[...], D_MODEL)
        _acc(dg_ref, pl.program_id(0) == 0, _colsum(dhn * xhat))
        dh0_ref[...] = dh1_ref[...] + _rms_bwd(dhn, xhat, r, g_ref[...], D_MODEL)
        finish()

    def row(w):
        return pl.BlockSpec((tm, w), lambda i: (i, 0))

    res = pl.pallas_call(
        body, name="in_bwd", grid=(n // tm,),
        in_specs=[row(P_COLS), row(D_MODEL), row(D_MODEL), _const_spec((1, D_MODEL)), _const_spec((P_COLS, D_MODEL))] + c_in,
        out_specs=[row(D_MODEL), pl.BlockSpec((1, D_MODEL), lambda i: (0, 0))] + c_out,
        out_shape=[jax.ShapeDtypeStruct((n, D_MODEL), F32), jax.ShapeDtypeStruct((1, D_MODEL), F32)] + c_shape,
        scratch_shapes=c_sems,
        compiler_params=pltpu.CompilerParams(dimension_semantics=("arbitrary",), vmem_limit_bytes=VMEM_LIMIT),
    )(dp, h0, dh1, ln1_g, w_in_p, *srcs)
    return res[:2], res[2:]


MAX_TILE = D_FF // 2


def _pick_tile(width, cap):
    best = LANES
    for mult in range(1, width // LANES + 1):
        cand = mult * LANES
        if width % cand == 0 and cand <= cap:
            best = cand
    return best


def _matmul_tn(name, a, b, srcs=(), scatter=()):
    n, ka = a.shape
    kb = b.shape[1]
    ta, tb = _pick_tile(ka, MAX_TILE), _pick_tile(kb, MAX_TILE)
    tk = n // 2
    nk = len(srcs)
    c_in, c_out, c_shape, c_sems = _exchange_specs(srcs, scatter)

    def body(a_ref, b_ref, *rest):
        o_ref = rest[nk]
        finish = _ride(3, *_exchange_fns(rest[:nk], rest[nk + 1:2 * nk + 1], rest[2 * nk + 1:], scatter))
        _acc(o_ref, pl.program_id(2) == 0, _dot_tn(a_ref[...].astype(BF16), b_ref[...].astype(BF16)))
        finish()

    res = pl.pallas_call(
        body, name=name, grid=(ka // ta, kb // tb, n // tk),
        in_specs=[pl.BlockSpec((tk, ta), lambda i, j, k: (k, i)), pl.BlockSpec((tk, tb), lambda i, j, k: (k, j))] + c_in,
        out_specs=[pl.BlockSpec((ta, tb), lambda i, j, k: (i, j))] + c_out,
        out_shape=[jax.ShapeDtypeStruct((ka, kb), F32)] + c_shape, scratch_shapes=c_sems,
        compiler_params=pltpu.CompilerParams(dimension_semantics=("arbitrary", "arbitrary", "arbitrary"),
                                             vmem_limit_bytes=VMEM_LIMIT),
    )(a, b, *srcs)
    return res[0], res[1:]


def _matmul_shards(name, ats, b):
    count = len(ats)
    ka, n = ats[0].shape
    kb = b.shape[1]
    width = ka // N_DEV
    per = 2 if 2 * width >= 4 * LANES else 4
    ta = per * width
    steps = ka // ta

    def body(*refs):
        b_ref = refs[count]
        for c in range(count):
            @pl.when(pl.program_id(0) // steps == c)
            def _():
                out = _dot(refs[c][...], b_ref[...].astype(BF16))
                for s in range(per):
                    refs[count + 1 + c][s] = out[s * width:(s + 1) * width, :].astype(BF16)

    def block_of(c):
        return lambda i: (jnp.clip(i - c * steps, 0, steps - 1), 0)

    def shards_of(c):
        return lambda i: (jnp.clip(i - c * steps, 0, steps - 1), 0, 0)

    return pl.pallas_call(
        body, name=name, grid=(count * steps,),
        in_specs=[pl.BlockSpec((ta, n), block_of(c)) for c in range(count)] + [_const_spec((n, kb))],
        out_specs=[pl.BlockSpec((per, width, kb), shards_of(c)) for c in range(count)],
        out_shape=[jax.ShapeDtypeStruct((N_DEV, width, kb), BF16)] * count,
        compiler_params=pltpu.CompilerParams(dimension_semantics=("arbitrary",), vmem_limit_bytes=VMEM_LIMIT),
    )(*ats, b)


def _adamw_math(g8_ref, w_ref, m_ref, v_ref, g_ref, d_ref, nm_ref, nv_ref):
    g = g8_ref[0].astype(F32)
    for s in range(1, N_DEV):
        g = g + g8_ref[s].astype(F32)
    g_ref[...] = g
    nm = ADAM_B1 * m_ref[...] + (1.0 - ADAM_B1) * g
    nv = ADAM_B2 * v_ref[...] + (1.0 - ADAM_B2) * (g * g)
    nm_ref[...] = nm
    nv_ref[...] = nv
    m_hat = nm / (1.0 - ADAM_B1 ** ADAM_STEP)
    v_hat = nv / (1.0 - ADAM_B2 ** ADAM_STEP)
    d_ref[...] = -ADAM_LR * (m_hat / (jnp.sqrt(v_hat) + ADAM_EPS) + ADAM_WD * w_ref[...])


def _adamw_many(name, items):
    count = len(items)

    def body(*refs):
        ins, outs = refs[:4 * count], refs[4 * count:]
        for i in range(count):
            _adamw_math(*ins[4 * i:4 * i + 4], *outs[4 * i:4 * i + 4])

    flat = [a for item in items for a in item]
    res = pl.pallas_call(
        body, name=name,
        out_shape=[jax.ShapeDtypeStruct(item[1].shape, F32) for item in items for _ in range(4)],
        compiler_params=pltpu.CompilerParams(vmem_limit_bytes=VMEM_LIMIT),
    )(*flat)
    return [tuple(res[4 * i:4 * i + 4]) for i in range(count)]


def _adamw(name, g8, w, m, v):
    rows, cols = w.shape
    tr = rows
    for cand in (256, 176, 128, 64):
        if rows % cand == 0 and rows > cand:
            tr = cand
            break

    def body(*refs):
        _adamw_math(*refs)

    blk = pl.BlockSpec((tr, cols), lambda i: (i, 0))
    return pl.pallas_call(
        body, name=name, grid=(rows // tr,),
        in_specs=[pl.BlockSpec((N_DEV, tr, cols), lambda i: (0, i, 0)), blk, blk, blk],
        out_specs=[blk] * 4, out_shape=[jax.ShapeDtypeStruct((rows, cols), F32)] * 4,
        compiler_params=pltpu.CompilerParams(dimension_semantics=("parallel",), vmem_limit_bytes=VMEM_LIMIT),
    )(g8, w, m, v)


def _exchange_specs(srcs, scatter):
    nk = len(srcs)
    if not nk:
        return [], [], [], []
    any_spec = pl.BlockSpec(memory_space=pl.ANY)
    out_shape = [jax.ShapeDtypeStruct(s.shape if sc else (N_DEV,) + s.shape, s.dtype) for s, sc in zip(srcs, scatter)]
    sems = [pltpu.SemaphoreType.DMA((nk, N_DEV - 1)), pltpu.SemaphoreType.DMA((nk, N_DEV - 1)),
            pltpu.SemaphoreType.DMA((nk,))]
    return [any_spec] * nk, [any_spec] * nk, out_shape, sems


FLIPS = ((0, 0, 1), (1, 0, 0), (0, 1, 0), (1, 1, 0), (1, 0, 1), (0, 1, 1), (1, 1, 1))
N_CHIP_PEERS = 3


def _exchange_fns(src_refs, out_refs, sems, scatter):
    nk = len(src_refs)
    if not nk:
        return (lambda: None), (lambda: None), (lambda: None)
    send_sems, recv_sems, local_sems = sems
    first = 1 + N_CHIP_PEERS

    def plan():
        x, y, c = lax.axis_index("x"), lax.axis_index("y"), lax.axis_index("c")
        me = 4 * x + 2 * y + c
        peers = [(1 - x if fx else x, 1 - y if fy else y, 1 - c if fc else c) for fx, fy, fc in FLIPS]
        pids = [4 * px + 2 * py + pc for px, py, pc in peers]

        def remote(k, j, src, dst, to):
            return pltpu.make_async_remote_copy(src_ref=src, dst_ref=dst, send_sem=send_sems.at[k, j],
                                                recv_sem=recv_sems.at[k, j], device_id=to, device_id_type=MESH)

        def mine(k, dest):
            return src_refs[k].at[dest] if scatter[k] else src_refs[k]

        local = [pltpu.make_async_copy(mine(k, me), out_refs[k].at[me], local_sems.at[k]) for k in range(nk)]
        direct = [remote(k, j, mine(k, pids[j]), out_refs[k].at[me], peers[j])
                  for k in range(nk) for j in range(len(FLIPS) if scatter[k] else first)]
        relays = {(k, j): remote(k, j, out_refs[k].at[pids[j - N_CHIP_PEERS]], out_refs[k].at[pids[j - N_CHIP_PEERS]], peers[0])
                  for k in range(nk) if not scatter[k] for j in range(first, len(FLIPS))}
        arrivals = {(k, j): remote(k, j, out_refs[k].at[pids[j]], out_refs[k].at[pids[j]], peers[j])
                    for k in range(nk) for j in range(len(FLIPS))}
        return local, direct, relays, arrivals

    def start():
        local, direct, _, _ = plan()
        for cp in local + direct:
            cp.start()

    def relay():
        _, _, relays, arrivals = plan()
        for (k, j), cp in relays.items():
            arrivals[k, j - N_CHIP_PEERS].wait_recv()
            cp.start()

    def wait():
        local, direct, relays, arrivals = plan()
        for (k, j), cp in arrivals.items():
            if (k, j + N_CHIP_PEERS) not in relays:
                cp.wait_recv()
        for cp in direct + list(relays.values()):
            cp.wait_send()
        for cp in local:
            cp.wait()

    return start, relay, wait


def _grid_step(rank):
    step, total = 0, 1
    for axis in range(rank):
        step = step * pl.num_programs(axis) + pl.program_id(axis)
        total = total * pl.num_programs(axis)
    return step, total


def _ride(rank, start, relay, wait):
    step, total = _grid_step(rank)
    pl.when(step == 0)(start)
    pl.when(step == (3 * total) // 4)(relay)
    return lambda: pl.when(step == total - 1)(wait)


def _exchange(name, srcs, scatter):
    nk = len(srcs)
    c_in, c_out, c_shape, c_sems = _exchange_specs(srcs, scatter)

    def body(*refs):
        start, relay, wait = _exchange_fns(refs[:nk], refs[nk:2 * nk], refs[2 * nk:], scatter)
        start()
        relay()
        wait()

    return pl.pallas_call(body, name=name, in_specs=c_in, out_specs=c_out, out_shape=c_shape, scratch_shapes=c_sems)(*srcs)


def _cols_from_shards(g):
    return jnp.transpose(g, (1, 0, 2)).reshape(g.shape[1], -1)


def _cols_to_shards(w):
    return jnp.transpose(w.reshape(w.shape[0], N_DEV, -1), (1, 0, 2))


def _prep(x, srcs, scatter):
    nb = x.shape[0]
    t = _t_pad()
    head = PAD_ROWS + N_META
    nk = len(srcs)
    c_in, c_out, c_shape, c_sems = _exchange_specs(srcs, scatter)

    def body(x_ref, *rest):
        h0_ref = rest[nk]
        finish = _ride(1, *_exchange_fns(rest[:nk], rest[nk + 1:2 * nk + 1], rest[2 * nk + 1:], scatter))
        lead = pl.program_id(0) == 0

        @pl.when(lead)
        def _():
            h0_ref[...] = jnp.zeros_like(h0_ref)

        @pl.when(jnp.logical_not(lead))
        def _():
            h0_ref[...] = x_ref[...]

        finish()

    src = pl.BlockSpec((nb, head, D_MODEL), lambda j: (0, jnp.maximum(j - 1, 0), 0))
    dst = pl.BlockSpec((nb, head, D_MODEL), lambda j: (0, j, 0))
    res = pl.pallas_call(
        body, name="prep", grid=(t // head,), in_specs=[src] + c_in, out_specs=[dst] + c_out,
        out_shape=[jax.ShapeDtypeStruct((nb, t, D_MODEL), F32)] + c_shape, scratch_shapes=c_sems,
        compiler_params=pltpu.CompilerParams(dimension_semantics=("arbitrary",)),
    )(x, *srcs)
    return res[0], res[1:]


def _rope_tables(n):
    t = _t_pad()
    pos = np.arange(t, dtype=np.float32) - np.float32(PAD_ROWS)
    half = QK_ROPE // 2
    freqs = (1.0 / (ROPE_THETA ** (np.arange(half, dtype=np.float32) / half))).astype(np.float32)
    ang = pos[:, None] * freqs[None, :]
    cos, sin = np.cos(ang), np.sin(ang)
    z = lambda w: np.zeros((t, w), np.float32)
    c = np.concatenate([np.ones((t, QK_NOPE), np.float32), cos, cos, z(HEAD_PAD - QK_HEAD)], axis=1)
    s1 = np.concatenate([z(QK_NOPE + half), sin, z(HEAD_PAD - QK_HEAD)], axis=1)
    s2 = np.concatenate([z(QK_NOPE), -sin, z(HEAD_PAD - QK_NOPE - half)], axis=1)
    return tuple(jnp.asarray(np.tile(a, (n // t, 1))) for a in (c, s1, s2))


def _block_diag_gates(lru_wa, lru_wi):
    eye = jnp.eye(2, dtype=lru_wa.dtype)

    def bd(w):
        w = w.reshape(2, D_RNN // LANES, 2, RNN_BW, RNN_BW)
        full = w[:, :, :, :, None, :] * eye[None, None, :, None, :, None]
        return full.reshape(2, D_RNN // LANES, LANES, LANES)

    a, i = bd(lru_wa), bd(lru_wi)
    return jnp.concatenate([a[0], i[0], a[1], i[1]], axis=-1)


def _unblock_gates(dw):
    nb = D_RNN // LANES
    parts = dw.reshape(nb, 2, RNN_BW, 4, 2, RNN_BW)
    diag = jnp.stack([parts[:, k, :, :, k, :] for k in range(2)], axis=1)
    diag = jnp.transpose(diag, (3, 0, 1, 2, 4)).reshape(4, 2 * nb, RNN_BW, RNN_BW)
    return jnp.stack([diag[0], diag[2]]), jnp.stack([diag[1], diag[3]])


WEIGHTS = ("meta_tokens", "ln1_g", "w_in", "q_a_norm_g", "w_uq", "kv_a_norm_g", "w_ukv", "q_norm_g", "k_norm_g",
           "conv_w", "conv_b", "lru_wa", "lru_ba", "lru_wi", "lru_bi", "lru_lambda", "attn_out_g", "rnn_out_g",
           "w_out", "ln2_g", "w_gate", "w_up", "w_down")
BIG = ("w_in", "w_uq", "w_ukv", "w_out", "w_gate", "w_up", "w_down")
TRANSPOSED = ("w_in", "w_uq", "w_gate", "w_up")
ROW_SHARDED = ("w_out", "w_down") + TRANSPOSED
REPLICATED = ("ln1_g", "q_a_norm_g", "kv_a_norm_g", "q_norm_g", "k_norm_g", "conv_b", "lru_wa", "lru_wi",
              "attn_out_g", "rnn_out_g", "ln2_g")
WHOLE = REPLICATED + ("loss",)
G_FIRST = ("w_in", "meta_tokens")
G_MID = ("w_uq", "w_ukv", "conv_w", "lru_ba", "lru_bi", "lru_lambda")
LATE = ("w_out", "w_gate", "w_up", "w_down")
G_LAST = ("meta_tokens", "ln1_g")


def _local_step(x, tgt, ex):
    nb = x.shape[0]
    t = _t_pad()
    n = nb * t
    local = ex.local
    h0, got = _prep(x, *ex.gather_srcs(G_FIRST))
    first = ex.gathered(G_FIRST, got)
    meta, w_in = first["meta_tokens"], first["w_in"]
    h0 = h0.at[:, PAD_ROWS:PAD_ROWS + N_META].set(jnp.broadcast_to(meta[None], (nb, N_META, D_MODEL))).reshape(n, D_MODEL)

    zr = lambda r: jnp.zeros((r, D_MODEL), w_in.dtype)
    w_in_p = jnp.concatenate([w_in[:OFF_CKV], w_in[OFF_KR:], zr(QK_NOPE), w_in[OFF_CKV:OFF_KR], zr(HEAD_PAD - QK_HEAD)],
                             axis=0)
    pad_g = lambda g: jnp.pad(g, ((0, 0), (0, HEAD_PAD - QK_HEAD)))
    qg, kg = pad_g(local["q_norm_g"]), pad_g(local["k_norm_g"])
    rc, rs1, rs2 = _rope_tables(n)
    wblk = _block_diag_gates(local["lru_wa"].reshape(2, -1, RNN_BW, RNN_BW),
                             local["lru_wi"].reshape(2, -1, RNN_BW, RNN_BW)).astype(BF16)
    nblk = D_RNN // LANES

    (hn, cq, ckv, xr, xg, kr), got = _in_proj(h0, local["ln1_g"], w_in_p, *ex.gather_srcs(G_MID))
    w = ex.gathered(G_MID, got)
    w_uq_p = jnp.pad(w["w_uq"].reshape(N_HEADS, QK_HEAD, Q_LORA), ((0, 0), (0, HEAD_PAD - QK_HEAD), (0, 0))
                     ).reshape(QP_COLS, Q_LORA)
    ukv = w["w_ukv"].reshape(KV_LORA, N_HEADS, QK_NOPE + V_HEAD)
    w_uk_p = jnp.pad(ukv[:, :, :QK_NOPE], ((0, 0), (0, 0), (0, HEAD_PAD - QK_NOPE))).reshape(KV_LORA, QP_COLS)
    w_v = ukv[:, :, QK_NOPE:].reshape(KV_LORA, D_ATTN)
    gbias = jnp.stack([w["lru_ba"][0], w["lru_bi"][0], w["lru_ba"][1], w["lru_bi"][1]], axis=0)
    gbias = jnp.transpose(gbias.reshape(4, nblk, LANES), (1, 0, 2)).reshape(nblk, 1, 4 * LANES)

    q, k, v = _qkv_fwd(cq, ckv, kr, local["q_a_norm_g"], local["kv_a_norm_g"], w_uq_p, w_uk_p, w_v, qg, kg, rc, rs1, rs2)
    oa, probs, got = _attn_fwd(q, k, v, *ex.gather_srcs(LATE))
    late = ex.gathered(LATE, got)
    orn, rnn_saved = _rnn_fwd(xr, xg, w["conv_w"], local["conv_b"], wblk, gbias, w["lru_lambda"])
    (doa, dor, dh1, mix_t, h1n, act_t, dgate_t, dup_t, dyb, loss, dga, dgr, dg2) = _post(
        oa, orn, h0, tgt, local["attn_out_g"], local["rnn_out_g"], local["ln2_g"], late["w_out"], late["w_gate"],
        late["w_up"], late["w_down"])
    dw_gate, dw_up = _matmul_shards("dw_gate_up", [dgate_t, dup_t], h1n)
    wire = {"w_out": _matmul_shards("dw_out", [mix_t], dh1)[0], "w_gate": dw_gate, "w_up": dw_up,
            "w_down": _matmul_shards("dw_down", [act_t], dyb)[0]}
    names = ("w_gate",)
    (dxr, dxg, dcw, dcb, dwblk, dgb, dlam), got = _rnn_bwd(xr, xg, dor, rnn_saved, w["conv_w"], wblk, w["lru_lambda"],
                                                           *ex.scatter_srcs(names, wire))
    summed = ex.scattered(names, wire, got)
    dwa, dwi = _unblock_gates(dwblk)
    dgb = jnp.transpose(dgb.reshape(nblk, 4, LANES), (1, 0, 2)).reshape(4, D_RNN)
    names = ("w_out", "w_up", "w_down")
    (dq_r, dk_r, dv), got = _attn_bwd(q, k, v, doa, oa, probs, *ex.scatter_srcs(names, wire))
    summed.update(ex.scattered(names, wire, got))
    wire = ex.to_wire({
        "conv_w": dcw, "conv_b": dcb, "lru_wa": dwa.reshape(-1, RNN_BW), "lru_ba": jnp.stack([dgb[0], dgb[2]]),
        "lru_wi": dwi.reshape(-1, RNN_BW), "lru_bi": jnp.stack([dgb[1], dgb[3]]), "lru_lambda": dlam,
        "attn_out_g": dga, "rnn_out_g": dgr, "ln2_g": dg2, "loss": loss})
    names = tuple(wire)
    (dp, qa, kva, dqp, dkv, dqg, dkg, dgqa, dgkva), got = _qkv_bwd(
        cq, ckv, kr, dq_r, dk_r, dv, dxr, dxg, local["q_a_norm_g"], local["kv_a_norm_g"], w_uq_p, w_uk_p, w_v, qg, kg,
        rc, rs1, rs2, *ex.scatter_srcs(names, wire))
    summed.update(ex.scattered(names, wire, got))
    dw_uq_p, _ = _matmul_tn("dw_uq", dqp, qa)
    dw_kv, _ = _matmul_tn("dw_ukv", kva, dkv)
    dw_uq = dw_uq_p.reshape(N_HEADS, HEAD_PAD, Q_LORA)[:, :QK_HEAD].reshape(N_HEADS * QK_HEAD, Q_LORA)
    dw_ukv = jnp.concatenate([dw_kv[:, :QP_COLS].reshape(KV_LORA, N_HEADS, HEAD_PAD)[:, :, :QK_NOPE],
                              dw_kv[:, QP_COLS:].reshape(KV_LORA, N_HEADS, V_HEAD)], axis=2).reshape(KV_LORA, -1)
    wire = ex.to_wire({"q_a_norm_g": dgqa, "w_uq": dw_uq, "kv_a_norm_g": dgkva, "w_ukv": dw_ukv,
                       "q_norm_g": dqg[:, :QK_HEAD], "k_norm_g": dkg[:, :QK_HEAD]})
    names = tuple(wire)
    dw_in_p, got = _matmul_tn("dw_in", dp, hn, *ex.scatter_srcs(names, wire))
    summed.update(ex.scattered(names, wire, got))
    kr0 = OFF_CKV + 2 * D_RNN + QK_NOPE
    dw_in = jnp.concatenate([dw_in_p[:OFF_CKV], dw_in_p[kr0:kr0 + QK_ROPE], dw_in_p[OFF_CKV:OFF_CKV + 2 * D_RNN]], axis=0)
    wire = ex.to_wire({"w_in": dw_in})
    (dh0, dg1), got = _in_bwd(dp, h0, dh1, local["ln1_g"], w_in_p, *ex.scatter_srcs(("w_in",), wire))
    summed.update(ex.scattered(("w_in",), wire, got))

    dh0 = dh0.reshape(nb, t, D_MODEL)
    wire = ex.to_wire({"meta_tokens": jnp.sum(dh0[:, PAD_ROWS:PAD_ROWS + N_META], axis=0), "ln1_g": dg1})
    got = ex.run("reduce_last", *ex.scatter_srcs(G_LAST, wire))
    summed.update(ex.scattered(G_LAST, wire, got))
    return dh0[:, PAD_ROWS + N_META:], summed


class _MeshExchange:
    def __init__(self, shards):
        self.local = shards

    @staticmethod
    def run(name, srcs, scatter):
        return _exchange(name, srcs, scatter)

    def gather_srcs(self, names):
        return [self.local[k].astype(BF16) if k in BIG else self.local[k] for k in names], [False] * len(names)

    @staticmethod
    def gathered(names, outs):
        return {k: g.reshape(-1, g.shape[-1]) if k in ROW_SHARDED else _cols_from_shards(g) for k, g in zip(names, outs)}

    @staticmethod
    def to_wire(grads):
        wire = {}
        for k, g in grads.items():
            if k in WHOLE:
                wire[k] = g
            elif k in ROW_SHARDED:
                wire[k] = g.reshape(N_DEV, -1, g.shape[-1]).astype(BF16)
            else:
                wire[k] = _cols_to_shards(g).astype(BF16) if k in BIG else _cols_to_shards(g)
        return wire

    @staticmethod
    def scatter_srcs(names, wire):
        return [wire[k] for k in names], [k not in WHOLE for k in names]

    @staticmethod
    def scattered(names, wire, outs):
        return dict(zip(names, outs))


def kernel(x, meta_tokens, ln1_g, w_in, q_a_norm_g, w_uq, kv_a_norm_g, w_ukv, q_norm_g, k_norm_g, conv_w, conv_b, lru_wa, lru_ba, lru_wi, lru_bi, lru_lambda, attn_out_g, rnn_out_g, w_out, ln2_g, w_gate, w_up, w_down, loss_target, m_meta_tokens, m_ln1_g, m_w_in, m_q_a_norm_g, m_w_uq, m_kv_a_norm_g, m_w_ukv, m_q_norm_g, m_k_norm_g, m_conv_w, m_conv_b, m_lru_wa, m_lru_ba, m_lru_wi, m_lru_bi, m_lru_lambda, m_attn_out_g, m_rnn_out_g, m_w_out, m_ln2_g, m_w_gate, m_w_up, m_w_down, v_meta_tokens, v_ln1_g, v_w_in, v_q_a_norm_g, v_w_uq, v_kv_a_norm_g, v_w_ukv, v_q_norm_g, v_k_norm_g, v_conv_w, v_conv_b, v_lru_wa, v_lru_ba, v_lru_wi, v_lru_bi, v_lru_lambda, v_attn_out_g, v_rnn_out_g, v_w_out, v_ln2_g, v_w_gate, v_w_up, v_w_down):
    given = (meta_tokens, ln1_g, w_in, q_a_norm_g, w_uq, kv_a_norm_g, w_ukv, q_norm_g, k_norm_g, conv_w, conv_b,
             lru_wa, lru_ba, lru_wi, lru_bi, lru_lambda, attn_out_g, rnn_out_g, w_out, ln2_g, w_gate, w_up, w_down)
    moments_m = (m_meta_tokens, m_ln1_g, m_w_in, m_q_a_norm_g, m_w_uq, m_kv_a_norm_g, m_w_ukv, m_q_norm_g, m_k_norm_g,
                 m_conv_w, m_conv_b, m_lru_wa, m_lru_ba, m_lru_wi, m_lru_bi, m_lru_lambda, m_attn_out_g, m_rnn_out_g,
                 m_w_out, m_ln2_g, m_w_gate, m_w_up, m_w_down)
    moments_v = (v_meta_tokens, v_ln1_g, v_w_in, v_q_a_norm_g, v_w_uq, v_kv_a_norm_g, v_w_ukv, v_q_norm_g, v_k_norm_g,
                 v_conv_w, v_conv_b, v_lru_wa, v_lru_ba, v_lru_wi, v_lru_bi, v_lru_lambda, v_attn_out_g, v_rnn_out_g,
                 v_w_out, v_ln2_g, v_w_gate, v_w_up, v_w_down)
    shapes = {k: a.shape for k, a in zip(WEIGHTS, given)}

    def two_d(k, a):
        a = a.reshape(-1, a.shape[-1])
        return a.T if k in TRANSPOSED else a

    w = {k: two_d(k, a) for k, a in zip(WEIGHTS, given)}
    m = {k: two_d(k, a) for k, a in zip(WEIGHTS, moments_m)}
    v = {k: two_d(k, a) for k, a in zip(WEIGHTS, moments_v)}

    grad_x, parts = _local_step(x, loss_target, _MeshExchange(w))

    tiled = ("w_in", "w_gate", "w_up", "w_down")
    new = {k: _adamw("adamw_" + k, parts[k], w[k], m[k], v[k]) for k in tiled}
    small = [k for k in WEIGHTS if k not in tiled]
    new.update(zip(small, _adamw_many("adamw_small", [(parts[k], w[k], m[k], v[k]) for k in small])))

    loss = jnp.sum(parts["loss"][:, 0, 0])
    outs = [loss, grad_x]
    for idx in range(4):
        outs += [(new[k][idx].T if k in TRANSPOSED else new[k][idx]).reshape(shapes[k]) for k in WEIGHTS]
    return tuple(outs)
```

```python
import functools
import math

import numpy as np
import jax
import jax.numpy as jnp
from jax import lax
from jax.experimental import pallas as pl
from jax.experimental.pallas import tpu as pltpu

F32 = jnp.float32
BF16 = jnp.bfloat16

D_MODEL = 1024
N_META = 16
SEQ = 2048
N_HEADS = 8
QK_NOPE = 64
QK_ROPE = 32
QK_HEAD = QK_NOPE + QK_ROPE
V_HEAD = 64
D_ATTN = N_HEADS * V_HEAD
Q_LORA = 384
KV_LORA = 256
D_RNN = 512
RNN_BW = 64
D_FF = 2816
EPS = 1e-6
LRU_C = 8.0
ROPE_THETA = 10000.0
OFF_CKV = Q_LORA + KV_LORA
OFF_KR = OFF_CKV + QK_ROPE
IN_COLS = OFF_KR + 2 * D_RNN

ADAM_LR = 0.001
ADAM_B1 = 0.9
ADAM_B2 = 0.999
ADAM_EPS = 1e-08
ADAM_WD = 0.01
ADAM_STEP = 10

N_DEV = 8
LANES = 128
HEAD_PAD = LANES
PAD_ROWS = LANES - N_META
QP_COLS = N_HEADS * HEAD_PAD
P_COLS = OFF_CKV + 2 * D_RNN + LANES
FF_CHUNK = D_FF
VMEM_LIMIT = 56 * 1024 * 1024
MESH = pl.DeviceIdType.MESH


def _t_pad():
    return PAD_ROWS + N_META + SEQ


def _row_tile(n):
    return 256 if n % 256 == 0 else 128


def _wide_row_tile(n):
    quarter = _t_pad() // 4
    return quarter if quarter % 16 == 0 and n % quarter == 0 else _row_tile(n)


def _const_spec(shape):
    nd = len(shape)
    return pl.BlockSpec(shape, lambda *_: (0,) * nd, pipeline_mode=pl.Buffered(1))


def _rms(x, d):
    r = lax.rsqrt(jnp.sum(x * x, axis=-1, keepdims=True) * (1.0 / d) + EPS)
    return x * r, r


def _rms_bwd(dy, xhat, r, g, d):
    dxh = dy * g
    return r * (dxh - xhat * (jnp.sum(dxh * xhat, axis=-1, keepdims=True) * (1.0 / d)))


def _colsum(x):
    return jnp.sum(x, axis=0, keepdims=True)


def _dot(a, b):
    return jnp.dot(a, b, preferred_element_type=F32)


def _dot_nt(a, b):
    return lax.dot_general(a, b, (((1,), (1,)), ((), ())), preferred_element_type=F32)


def _dot_tn(a, b):
    return lax.dot_general(a, b, (((0,), (0,)), ((), ())), preferred_element_type=F32)


def _rope(x, c, s1, s2):
    return x * c + pltpu.roll(x, 16, 1) * s1 + pltpu.roll(x, HEAD_PAD - 16, 1) * s2


def _rope_bwd(dy, c, s1, s2):
    return dy * c + pltpu.roll(dy * s1, HEAD_PAD - 16, 1) + pltpu.roll(dy * s2, 16, 1)


def _acc(ref, first, val):
    @pl.when(first)
    def _():
        ref[...] = val

    @pl.when(jnp.logical_not(first))
    def _():
        ref[...] += val


def _in_proj(h0, ln1_g, w_in_p, srcs=(), scatter=()):
    n = h0.shape[0]
    tm = _wide_row_tile(n)
    nk = len(srcs)
    c_in, c_out, c_shape, c_sems = _exchange_specs(srcs, scatter)

    def body(h_ref, g_ref, w_ref, *rest):
        hn_ref, cq_ref, ckv_ref, xr_ref, xg_ref, kr_ref = rest[nk:nk + 6]
        finish = _ride(1, *_exchange_fns(rest[:nk], rest[nk + 6:2 * nk + 6], rest[2 * nk + 6:], scatter))
        xhat, _ = _rms(h_ref[...], D_MODEL)
        hn = (xhat * g_ref[...]).astype(BF16)
        hn_ref[...] = hn
        p = _dot_nt(hn, w_ref[...])
        cq_ref[...] = p[:, :Q_LORA]
        ckv_ref[...] = p[:, Q_LORA:OFF_CKV]
        xr_ref[...] = p[:, OFF_CKV:OFF_CKV + D_RNN]
        xg_ref[...] = p[:, OFF_CKV + D_RNN:OFF_CKV + 2 * D_RNN]
        kr_ref[...] = p[:, OFF_CKV + 2 * D_RNN:]
        finish()

    def row(w):
        return pl.BlockSpec((tm, w), lambda i: (i, 0))

    widths = (D_MODEL, Q_LORA, KV_LORA, D_RNN, D_RNN, LANES)
    res = pl.pallas_call(
        body, name="in_proj", grid=(n // tm,),
        in_specs=[row(D_MODEL), _const_spec((1, D_MODEL)), _const_spec((P_COLS, D_MODEL))] + c_in,
        out_specs=[row(w) for w in widths] + c_out,
        out_shape=[jax.ShapeDtypeStruct((n, w), BF16 if k == 0 else F32) for k, w in enumerate(widths)] + c_shape,
        scratch_shapes=c_sems,
        compiler_params=pltpu.CompilerParams(dimension_semantics=("arbitrary",), vmem_limit_bytes=VMEM_LIMIT),
    )(h0, ln1_g, w_in_p, *srcs)
    return res[:6], res[6:]


def _qkv_fwd(cq, ckv, kr, gqa, gkva, w_uq_p, w_uk_p, w_v, qg, kg, rc, rs1, rs2):
    n = cq.shape[0]
    tm = _wide_row_tile(n)

    def body(cq_ref, ckv_ref, kr_ref, gqa_ref, gkva_ref, wuq_ref, wuk_ref, wv_ref, qg_ref, kg_ref,
             c_ref, s1_ref, s2_ref, q_ref, k_ref, v_ref):
        xq, _ = _rms(cq_ref[...], Q_LORA)
        qa = (xq * gqa_ref[...]).astype(BF16)
        q = _dot_nt(qa, wuq_ref[...])
        xkv, _ = _rms(ckv_ref[...], KV_LORA)
        kva = (xkv * gkva_ref[...]).astype(BF16)
        kn = _dot(kva, wuk_ref[...])
        v_ref[...] = _dot(kva, wv_ref[...]).astype(BF16)
        krp = kr_ref[...]
        c, s1, s2 = c_ref[...], s1_ref[...], s2_ref[...]
        for h in range(N_HEADS):
            sl = slice(h * HEAD_PAD, (h + 1) * HEAD_PAD)
            qh, _ = _rms(q[:, sl], QK_HEAD)
            q_ref[:, sl] = _rope(qh * qg_ref[...], c, s1, s2).astype(BF16)
            kh, _ = _rms(kn[:, sl] + krp, QK_HEAD)
            k_ref[:, sl] = _rope(kh * kg_ref[...], c, s1, s2).astype(BF16)

    def row(w):
        return pl.BlockSpec((tm, w), lambda i: (i, 0))

    return pl.pallas_call(
        body, name="qkv_fwd", grid=(n // tm,),
        in_specs=[row(Q_LORA), row(KV_LORA), row(LANES), _const_spec((1, Q_LORA)), _const_spec((1, KV_LORA)),
                  _const_spec((QP_COLS, Q_LORA)), _const_spec((KV_LORA, QP_COLS)), _const_spec((KV_LORA, D_ATTN)),
                  _const_spec((1, LANES)), _const_spec((1, LANES)), row(LANES), row(LANES), row(LANES)],
        out_specs=[row(QP_COLS), row(QP_COLS), row(D_ATTN)],
        out_shape=[jax.ShapeDtypeStruct((n, QP_COLS), BF16), jax.ShapeDtypeStruct((n, QP_COLS), BF16),
                   jax.ShapeDtypeStruct((n, D_ATTN), BF16)],
        compiler_params=pltpu.CompilerParams(dimension_semantics=("parallel",), vmem_limit_bytes=VMEM_LIMIT),
    )(cq, ckv, kr, gqa, gkva, w_uq_p, w_uk_p, w_v, qg, kg, rc, rs1, rs2)


def _qkv_bwd(cq, ckv, kr, dq_r, dk_r, dv, dxr, dxg, gqa, gkva, w_uq_p, w_uk_p, w_v, qg, kg, rc, rs1, rs2,
             srcs=(), scatter=()):
    n = cq.shape[0]
    tm = _wide_row_tile(n)
    nk = len(srcs)
    c_in, c_out, c_shape, c_sems = _exchange_specs(srcs, scatter)

    def body(cq_ref, ckv_ref, kr_ref, dq_ref, dk_ref, dv_ref, dxr_ref, dxg_ref, gqa_ref, gkva_ref, wuq_ref, wuk_ref,
             wv_ref, qg_ref, kg_ref, c_ref, s1_ref, s2_ref, *rest):
        dp_ref, qa_ref, kva_ref, dqp_ref, dkv_ref, dqg_ref, dkg_ref, dgqa_ref, dgkva_ref = rest[nk:nk + 9]
        finish = _ride(1, *_exchange_fns(rest[:nk], rest[nk + 9:2 * nk + 9], rest[2 * nk + 9:], scatter))
        first = pl.program_id(0) == 0
        dp_ref[:, OFF_CKV:OFF_CKV + D_RNN] = dxr_ref[...].astype(BF16)
        dp_ref[:, OFF_CKV + D_RNN:OFF_CKV + 2 * D_RNN] = dxg_ref[...].astype(BF16)
        xq, rq = _rms(cq_ref[...], Q_LORA)
        qa = (xq * gqa_ref[...]).astype(BF16)
        qa_ref[...] = qa
        q = _dot_nt(qa, wuq_ref[...])
        xkv, rkv = _rms(ckv_ref[...], KV_LORA)
        kva = (xkv * gkva_ref[...]).astype(BF16)
        kva_ref[...] = kva
        kn = _dot(kva, wuk_ref[...])
        krp = kr_ref[...]
        c, s1, s2 = c_ref[...], s1_ref[...], s2_ref[...]
        lane = lax.broadcasted_iota(jnp.int32, (tm, HEAD_PAD), 1)
        rope_lanes = jnp.logical_and(lane >= QK_NOPE, lane < QK_HEAD)
        dqg = jnp.zeros((1, HEAD_PAD), F32)
        dkg = jnp.zeros((1, HEAD_PAD), F32)
        dkr = jnp.zeros((tm, HEAD_PAD), F32)
        for h in range(N_HEADS):
            sl = slice(h * HEAD_PAD, (h + 1) * HEAD_PAD)
            qh, rqh = _rms(q[:, sl], QK_HEAD)
            dy = _rope_bwd(dq_ref[:, sl], c, s1, s2)
            dqg = dqg + _colsum(dy * qh)
            dqp_ref[:, sl] = _rms_bwd(dy, qh, rqh, qg_ref[...], QK_HEAD).astype(BF16)
            kh, rkh = _rms(kn[:, sl] + krp, QK_HEAD)
            dyk = _rope_bwd(dk_ref[:, sl], c, s1, s2)
            dkg = dkg + _colsum(dyk * kh)
            dkh = _rms_bwd(dyk, kh, rkh, kg_ref[...], QK_HEAD)
            dkv_ref[:, sl] = dkh.astype(BF16)
            dkr = dkr + jnp.where(rope_lanes, dkh, 0.0)
        dkv_ref[:, QP_COLS:] = dv_ref[...].astype(BF16)
        dp_ref[:, OFF_CKV + 2 * D_RNN:] = dkr.astype(BF16)
        dqa = _dot(dqp_ref[...], wuq_ref[...])
        dp_ref[:, :Q_LORA] = _rms_bwd(dqa, xq, rq, gqa_ref[...], Q_LORA).astype(BF16)
        dkva = _dot_nt(dkv_ref[:, :QP_COLS], wuk_ref[...]) + _dot_nt(dkv_ref[:, QP_COLS:], wv_ref[...])
        dp_ref[:, Q_LORA:OFF_CKV] = _rms_bwd(dkva, xkv, rkv, gkva_ref[...], KV_LORA).astype(BF16)
        _acc(dqg_ref, first, dqg)
        _acc(dkg_ref, first, dkg)
        _acc(dgqa_ref, first, _colsum(dqa * xq))
        _acc(dgkva_ref, first, _colsum(dkva * xkv))
        finish()

    def row(w):
        return pl.BlockSpec((tm, w), lambda i: (i, 0))

    def acc(w):
        return pl.BlockSpec((1, w), lambda i: (0, 0))

    res = pl.pallas_call(
        body, name="qkv_bwd", grid=(n // tm,),
        in_specs=[row(Q_LORA), row(KV_LORA), row(LANES), row(QP_COLS), row(QP_COLS), row(D_ATTN), row(D_RNN), row(D_RNN),
                  _const_spec((1, Q_LORA)), _const_spec((1, KV_LORA)),
                  _const_spec((QP_COLS, Q_LORA)), _const_spec((KV_LORA, QP_COLS)), _const_spec((KV_LORA, D_ATTN)),
                  _const_spec((1, LANES)), _const_spec((1, LANES)), row(LANES), row(LANES), row(LANES)] + c_in,
        out_specs=[row(P_COLS), row(Q_LORA), row(KV_LORA), row(QP_COLS),
                   row(QP_COLS + D_ATTN), acc(LANES), acc(LANES), acc(Q_LORA), acc(KV_LORA)] + c_out,
        out_shape=[jax.ShapeDtypeStruct((n, P_COLS), BF16), jax.ShapeDtypeStruct((n, Q_LORA), BF16),
                   jax.ShapeDtypeStruct((n, KV_LORA), BF16), jax.ShapeDtypeStruct((n, QP_COLS), BF16),
                   jax.ShapeDtypeStruct((n, QP_COLS + D_ATTN), BF16),
                   jax.ShapeDtypeStruct((1, LANES), F32), jax.ShapeDtypeStruct((1, LANES), F32),
                   jax.ShapeDtypeStruct((1, Q_LORA), F32), jax.ShapeDtypeStruct((1, KV_LORA), F32)] + c_shape,
        scratch_shapes=c_sems,
        compiler_params=pltpu.CompilerParams(dimension_semantics=("arbitrary",), vmem_limit_bytes=VMEM_LIMIT),
    )(cq, ckv, kr, dq_r, dk_r, dv, dxr, dxg, gqa, gkva, w_uq_p, w_uk_p, w_v, qg, kg, rc, rs1, rs2, *srcs)
    return res[:9], res[9:]


FWD_KEY_CHUNK = 2 * LANES
BWD_KEY_CHUNK = 4 * LANES


def _key_chunks(t, chunk):
    count = max(t // chunk, 1)
    first = t - chunk * (count - 1)
    return [(0, first)] + [(first + chunk * c, chunk) for c in range(count - 1)]


def _attn_specs(t, tq):
    nq = t // tq
    qspec = pl.BlockSpec((tq, 2 * HEAD_PAD), lambda b, hp, i: (b * nq + i, hp))
    kspec = pl.BlockSpec((t, 2 * HEAD_PAD), lambda b, hp, i: (b, hp))
    vspec = pl.BlockSpec((t, 2 * V_HEAD), lambda b, hp, i: (b, hp))
    ospec = pl.BlockSpec((tq, 2 * V_HEAD), lambda b, hp, i: (b * nq + i, hp))
    return nq, qspec, kspec, vspec, ospec


def _probs_spec(t, tq):
    return pl.BlockSpec((1, 2, tq, t), lambda b, hp, i: (b, hp, i, 0))


def _attn_fwd(q, k, v, srcs=(), scatter=()):
    n = q.shape[0]
    t = _t_pad()
    tq = t // 2
    nq, qspec, kspec, vspec, ospec = _attn_specs(t, tq)
    nk = len(srcs)
    c_in, c_out, c_shape, c_sems = _exchange_specs(srcs, scatter)

    def body(q_ref, k_ref, v_ref, *rest):
        o_ref, l_ref, p_ref = rest[nk:nk + 3]
        finish = _ride(3, *_exchange_fns(rest[:nk], rest[nk + 3:2 * nk + 3], rest[2 * nk + 3:], scatter))
        lane = lax.broadcasted_iota(jnp.int32, (tq, 2 * V_HEAD), 1)
        outs = []
        sums = []
        for j in range(2):
            sl = slice(j * HEAD_PAD, (j + 1) * HEAD_PAD)
            qh = q_ref[:, sl]

            def scores(start, size):
                s = _dot_nt(qh, k_ref[start:start + size, sl])
                if start < PAD_ROWS:
                    key = lax.broadcasted_iota(jnp.int32, (tq, size), 1) + start
                    s = jnp.where(key >= PAD_ROWS, s, -jnp.inf)
                return s

            chunks = _key_chunks(t, FWD_KEY_CHUNK)

            def lane_folds(x):
                return [x[:, c:c + LANES] for c in range(0, x.shape[1], LANES)]

            top = functools.reduce(jnp.maximum, [blk for c in chunks for blk in lane_folds(scores(*c))])
            top = jnp.max(top, axis=-1, keepdims=True)
            l = jnp.zeros((tq, LANES), F32)
            pv = jnp.zeros((tq, 2 * V_HEAD), F32)
            for start, size in chunks:
                e = jnp.exp2((scores(start, size) - top) * (QK_HEAD ** -0.5 * math.log2(math.e)))
                l = functools.reduce(jnp.add, lane_folds(e), l)
                e = e.astype(BF16)
                p_ref[0, j, :, start:start + size] = e
                pv = pv + _dot(e, v_ref[start:start + size, :])
            l = jnp.sum(l, axis=-1, keepdims=True)
            outs.append(pv / l)
            sums.append(l)
        o_ref[...] = jnp.where(lane < V_HEAD, outs[0], outs[1])
        l_ref[...] = jnp.where(lane < V_HEAD, sums[0], sums[1])
        finish()

    res = pl.pallas_call(
        body, name="attn_fwd", grid=(n // t, N_HEADS // 2, nq),
        in_specs=[qspec, kspec, vspec] + c_in, out_specs=[ospec, ospec, _probs_spec(t, tq)] + c_out,
        out_shape=[jax.ShapeDtypeStruct((n, D_ATTN), F32), jax.ShapeDtypeStruct((n, D_ATTN), F32),
                   jax.ShapeDtypeStruct((n // t, N_HEADS, t, t), BF16)] + c_shape,
        scratch_shapes=c_sems,
        compiler_params=pltpu.CompilerParams(dimension_semantics=("arbitrary", "arbitrary", "arbitrary"),
                                             vmem_limit_bytes=VMEM_LIMIT),
    )(q, k, v, *srcs)
    return res[0], (res[1], res[2]), res[3:]


def _attn_bwd(q, k, v, do, o, probs, srcs=(), scatter=()):
    n = q.shape[0]
    t = _t_pad()
    tq = t // 2
    nq, qspec, kspec, vspec, ospec = _attn_specs(t, tq)
    nk = len(srcs)
    c_in, c_out, c_shape, c_sems = _exchange_specs(srcs, scatter)

    def body(q_ref, k_ref, v_ref, do_ref, o_ref, l_ref, p_ref, *rest):
        dq_ref, dk_ref, dv_ref = rest[nk:nk + 3]
        finish = _ride(3, *_exchange_fns(rest[:nk], rest[nk + 3:2 * nk + 3], rest[2 * nk + 3:], scatter))

        @pl.when(pl.program_id(2) == 0)
        def _():
            dk_ref[...] = jnp.zeros_like(dk_ref)
            dv_ref[...] = jnp.zeros_like(dv_ref)

        lane = lax.broadcasted_iota(jnp.int32, (tq, 2 * V_HEAD), 1)
        do = do_ref[...]
        do_o = do * o_ref[...]
        chunks = _key_chunks(t, BWD_KEY_CHUNK)
        dvs = [None] * len(chunks)
        for j in range(2):
            sl = slice(j * HEAD_PAD, (j + 1) * HEAD_PAD)
            qh = q_ref[:, sl]
            in_head = (lane < V_HEAD) if j == 0 else (lane >= V_HEAD)
            inv_l = 1.0 / l_ref[:, j * V_HEAD:j * V_HEAD + 1]
            doh = jnp.where(in_head, do, 0.0).astype(BF16)
            doh_n = jnp.where(in_head, do * inv_l, 0.0).astype(BF16)
            delta = jnp.sum(jnp.where(in_head, do_o, 0.0), axis=-1, keepdims=True)
            row_scale = inv_l * (QK_HEAD ** -0.5)
            dq = jnp.zeros((tq, HEAD_PAD), F32)
            for c, (start, size) in enumerate(chunks):
                rows = slice(start, start + size)
                e = p_ref[0, j, :, rows]
                dp = _dot_nt(doh, v_ref[rows, :])
                ds = (e.astype(F32) * (dp - delta) * row_scale).astype(BF16)
                dq = dq + _dot(ds, k_ref[rows, sl])
                dk_ref[rows, sl] += _dot_tn(ds, qh)
                dvc = _dot_tn(e, doh_n)
                dvs[c] = dvc if dvs[c] is None else dvs[c] + dvc
            dq_ref[:, sl] = dq
        for (start, size), dvc in zip(chunks, dvs):
            dv_ref[start:start + size, :] += dvc
        finish()

    res = pl.pallas_call(
        body, name="attn_bwd", grid=(n // t, N_HEADS // 2, nq),
        in_specs=[qspec, kspec, vspec, ospec, ospec, ospec, _probs_spec(t, tq)] + c_in,
        out_specs=[qspec, kspec, vspec] + c_out,
        out_shape=[jax.ShapeDtypeStruct((n, QP_COLS), F32), jax.ShapeDtypeStruct((n, QP_COLS), F32),
                   jax.ShapeDtypeStruct((n, D_ATTN), F32)] + c_shape, scratch_shapes=c_sems,
        compiler_params=pltpu.CompilerParams(dimension_semantics=("arbitrary", "arbitrary", "arbitrary"),
                                             vmem_limit_bytes=VMEM_LIMIT),
    )(q, k, v, do, o, *probs, *srcs)
    return res[:3], res[3:]


SCAN_STEPS = 8


def _scan(chains, t):
    seg = t // 8
    rows = lax.broadcasted_iota(jnp.int32, (8, LANES), 0)

    def step(i, carry):
        carry = list(carry)
        for u in range(SCAN_STEPS):
            j = i * SCAN_STEPS + u
            for n, (a_ref, b_ref, h_ref, p_ref, reverse) in enumerate(chains):
                h, p = carry[n]
                idx = pl.ds(seg - 1 - j if reverse else j, 8, stride=seg)
                a = a_ref[idx, :]
                h = a * h + b_ref[idx, :]
                p = a * p
                h_ref[idx, :] = h
                p_ref[idx, :] = p
                carry[n] = (h, p)
        return tuple(carry)

    init = tuple((jnp.zeros((8, LANES), F32), jnp.ones((8, LANES), F32)) for _ in chains)
    ends = lax.fori_loop(0, seg // SCAN_STEPS, step, init)
    for (_, _, h_ref, p_ref, reverse), (b, a) in zip(chains, ends):
        for d in (1, 2, 4):
            if reverse:
                keep = rows < 8 - d
                a_n, b_n = pltpu.roll(a, 8 - d, 0), pltpu.roll(b, 8 - d, 0)
            else:
                keep = rows >= d
                a_n, b_n = pltpu.roll(a, d, 0), pltpu.roll(b, d, 0)
            b = a * jnp.where(keep, b_n, 0.0) + b
            a = a * jnp.where(keep, a_n, 1.0)
        for s in (range(7) if reverse else range(1, 8)):
            sl = slice(s * seg, (s + 1) * seg)
            carry_in = b[s + 1:s + 2, :] if reverse else b[s - 1:s, :]
            h_ref[sl, :] = h_ref[sl, :] + p_ref[sl, :] * carry_in


def _shift_rows(x, s, rows, t):
    if s == 0:
        return x
    rolled = pltpu.roll(x, s % t, 0)
    return jnp.where(rows >= s, rolled, 0.0) if s > 0 else jnp.where(rows < t + s, rolled, 0.0)


def _neg_expm1_twice(h, exp_2h):
    series = h * (-2.0 + h * (-2.0 + h * (-4.0 / 3 + h * (-2.0 / 3))))
    return jnp.where(h > -0.05, series, 1.0 - exp_2h)


def _sigmoid(x):
    return 0.5 * jnp.tanh(0.5 * x) + 0.5


def _gelu_parts(x):
    k = math.sqrt(2.0 / math.pi)
    th = jnp.tanh(k * (x + 0.044715 * x * x * x))
    g = 0.5 * x * (1.0 + th)
    dg = 0.5 * (1.0 + th) + 0.5 * x * (1.0 - th * th) * k * (1.0 + 3 * 0.044715 * x * x)
    return g, dg


def _lru_gates(xc, gates, lam_ref, valid, d):
    r = _sigmoid(gates[:, (2 * d) * LANES:(2 * d + 1) * LANES])
    i = _sigmoid(gates[:, (2 * d + 1) * LANES:(2 * d + 2) * LANES])
    neg_lam = -lam_ref[d:d + 1, :]
    sp = jnp.maximum(neg_lam, 0.0) + jnp.log1p(jnp.exp(-jnp.abs(neg_lam)))
    log_a = -LRU_C * r * sp
    a = jnp.exp(log_a)
    m = jnp.maximum(_neg_expm1_twice(log_a, a * a), 0.0)
    sq = jnp.sqrt(m)
    b = jnp.where(valid, sq * (i * xc), 0.0)
    return r, i, sp, a, m, sq, b


def _conv(xr, cw_ref, cb_ref, rows, t):
    return (cw_ref[0:1, :] * _shift_rows(xr, 2, rows, t) + cw_ref[1:2, :] * _shift_rows(xr, 1, rows, t)
            + cw_ref[2:3, :] * xr + cw_ref[3:4, :] * _shift_rows(xr, -1, rows, t) + cb_ref[...])


def _rnn_specs(t):
    seq = pl.BlockSpec((t, LANES), lambda cb, b: (b, cb))
    cw = pl.BlockSpec((4, LANES), lambda cb, b: (0, cb))
    vec1 = pl.BlockSpec((1, LANES), lambda cb, b: (0, cb))
    vec2 = pl.BlockSpec((2, LANES), lambda cb, b: (0, cb))
    wblk = pl.BlockSpec((1, LANES, 4 * LANES), lambda cb, b: (cb, 0, 0))
    gbias = pl.BlockSpec((1, 1, 4 * LANES), lambda cb, b: (cb, 0, 0))
    return seq, cw, vec1, vec2, wblk, gbias


def _rnn_fwd(xr, xg, conv_w, conv_b, wblk, gbias, lam):
    n = xr.shape[0]
    t = _t_pad()
    seq, cw, vec1, vec2, wspec, gspec = _rnn_specs(t)
    both = pl.BlockSpec((2, t, LANES), lambda cb, b: (0, b, cb))

    def body(xr_ref, xg_ref, cw_ref, cb_ref, w_ref, gb_ref, lam_ref,
             o_ref, xc_ref, r_ref, i_ref, q_ref, h_ref, a_s, b_s, p_s):
        rows = lax.broadcasted_iota(jnp.int32, (t, LANES), 0)
        valid = rows >= PAD_ROWS
        xc = _conv(xr_ref[...], cw_ref, cb_ref, rows, t)
        xc_ref[...] = xc
        gates = _dot(xc.astype(BF16), w_ref[0]) + gb_ref[0]
        for d in range(2):
            r_ref[d], i_ref[d], _, a_s[d], _, q_ref[d], b_s[d] = _lru_gates(xc, gates, lam_ref, valid, d)
        _scan([(a_s.at[d], b_s.at[d], h_ref.at[d], p_s.at[d], d == 1) for d in range(2)], t)
        g, _ = _gelu_parts(xg_ref[...])
        o_ref[...] = (h_ref[0] + h_ref[1]) * g

    stacked = jax.ShapeDtypeStruct((2, n, D_RNN), F32)
    res = pl.pallas_call(
        body, name="rnn_fwd", grid=(D_RNN // LANES, n // t),
        in_specs=[seq, seq, cw, vec1, wspec, gspec, vec2], out_specs=[seq, seq, both, both, both, both, both],
        out_shape=[jax.ShapeDtypeStruct((n, D_RNN), F32), jax.ShapeDtypeStruct((n, D_RNN), F32)] + [stacked] * 5,
        scratch_shapes=[pltpu.VMEM((2, t, LANES), F32)] * 2,
        compiler_params=pltpu.CompilerParams(dimension_semantics=("parallel", "parallel"), vmem_limit_bytes=VMEM_LIMIT),
    )(xr, xg, conv_w, conv_b, wblk, gbias, lam)
    return res[0], tuple(res[1:])


def _rnn_bwd(xr, xg, do, saved, conv_w, wblk, lam, srcs=(), scatter=()):
    n = xr.shape[0]
    t = _t_pad()
    seq, cw, vec1, vec2, wspec, gspec = _rnn_specs(t)
    nk = len(srcs)
    c_in, c_out, c_shape, c_sems = _exchange_specs(srcs, scatter)

    def body(xr_ref, xg_ref, do_ref, xc_ref, r_s, i_s, q_s, h_s, a_s, cw_ref, w_ref, lam_ref, *rest):
        dxr_ref, dxg_ref, dcw_ref, dcb_ref, dw_ref, dgb_ref, dlam_ref = rest[nk:nk + 7]
        b_s, l_s, p_s, back_s, dg_s = rest[2 * nk + 7 + len(c_sems):]
        finish = _ride(2, *_exchange_fns(rest[:nk], rest[nk + 7:2 * nk + 7], rest[2 * nk + 7:2 * nk + 7 + len(c_sems)],
                                         scatter))
        first = pl.program_id(1) == 0
        rows = lax.broadcasted_iota(jnp.int32, (t, LANES), 0)
        valid = rows >= PAD_ROWS
        xr = xr_ref[...]
        xc = xc_ref[...]
        xcb = xc.astype(BF16)
        g, dg = _gelu_parts(xg_ref[...])
        do = do_ref[...]
        dxg_ref[...] = do * (h_s[0] + h_s[1]) * dg
        b_s[...] = do * g
        sps = []
        for d in range(2):
            neg_lam = -lam_ref[d:d + 1, :]
            sps.append(jnp.maximum(neg_lam, 0.0) + jnp.log1p(jnp.exp(-jnp.abs(neg_lam))))
            back_s[d] = _shift_rows(a_s[d], -1 if d == 0 else 1, rows, t)
        _scan([(back_s.at[d], b_s, l_s.at[d], p_s.at[d], d == 0) for d in range(2)], t)
        dxc = jnp.zeros((t, LANES), F32)
        dlams = []
        for d in range(2):
            r, i, sp, a, sq = r_s[d], i_s[d], sps[d], a_s[d], q_s[d]
            lam_t = l_s[d]
            da = lam_t * _shift_rows(h_s[d], 1 if d == 0 else -1, rows, t)
            lam_v = jnp.where(valid, lam_t, 0.0)
            dsq = lam_v * (i * xc)
            di = lam_v * sq * xc
            dxc = dxc + lam_v * sq * i
            dm = jnp.where(sq > 0.0, dsq * 0.5 / jnp.where(sq > 0.0, sq, 1.0), 0.0)
            dla = da * a - 2.0 * dm * a * a
            dr = dla * (-LRU_C) * sp
            dsp = _colsum(dla * (-LRU_C) * r)
            dlams.append(dsp * -jax.nn.sigmoid(-lam_ref[d:d + 1, :]))
            dg_s[:, (2 * d) * LANES:(2 * d + 1) * LANES] = (dr * r * (1.0 - r)).astype(BF16)
            dg_s[:, (2 * d + 1) * LANES:(2 * d + 2) * LANES] = (di * i * (1.0 - i)).astype(BF16)
        dgates = dg_s[...]
        dxc = dxc + _dot_nt(dgates, w_ref[0])
        taps = [_shift_rows(dxc, j - 2, rows, t) for j in range(4)]
        dxr_ref[...] = (cw_ref[0:1, :] * taps[0] + cw_ref[1:2, :] * taps[1] + cw_ref[2:3, :] * taps[2]
                        + cw_ref[3:4, :] * taps[3])
        dcw = jnp.concatenate([_colsum(tap * xr) for tap in taps], axis=0)
        _acc(dcw_ref, first, dcw)
        _acc(dcb_ref, first, _colsum(dxc))
        _acc(dw_ref, first, _dot_tn(xcb, dgates)[None])
        _acc(dgb_ref, first, _colsum(dgates.astype(F32))[None])
        _acc(dlam_ref, first, jnp.concatenate(dlams, axis=0))
        finish()

    both = pl.BlockSpec((2, t, LANES), lambda cb, b: (0, b, cb))
    pair = pltpu.VMEM((2, t, LANES), F32)
    res = pl.pallas_call(
        body, name="rnn_bwd", grid=(D_RNN // LANES, n // t),
        in_specs=[seq, seq, seq, seq, both, both, both, both, both, cw, wspec, vec2] + c_in,
        out_specs=[seq, seq, cw, vec1, wspec, gspec, vec2] + c_out,
        out_shape=[jax.ShapeDtypeStruct((n, D_RNN), F32), jax.ShapeDtypeStruct((n, D_RNN), F32),
                   jax.ShapeDtypeStruct((4, D_RNN), F32), jax.ShapeDtypeStruct((1, D_RNN), F32),
                   jax.ShapeDtypeStruct((D_RNN // LANES, LANES, 4 * LANES), F32),
                   jax.ShapeDtypeStruct((D_RNN // LANES, 1, 4 * LANES), F32), jax.ShapeDtypeStruct((2, D_RNN), F32)]
        + c_shape,
        scratch_shapes=c_sems + [pltpu.VMEM((t, LANES), F32), pair, pair, pair, pltpu.VMEM((t, 4 * LANES), BF16)],
        compiler_params=pltpu.CompilerParams(dimension_semantics=("arbitrary", "arbitrary"), vmem_limit_bytes=VMEM_LIMIT),
    )(xr, xg, do, *saved, conv_w, wblk, lam, *srcs)
    return res[:7], res[7:]


def _post(oa, orn, h0, tgt, ga, gr, g2, w_out, w_gate, w_up, w_down):
    n = oa.shape[0]
    tm = _row_tile(n)
    t = _t_pad()
    head = PAD_ROWS + N_META
    parts = tm // head

    def body(oa_ref, or_ref, h0_ref, *rest):
        tgt_refs = rest[:parts]
        (ga_ref, gr_ref, g2_ref, wo_ref, wg_ref, wu_ref, wd_ref,
         doa_ref, dor_ref, dh1_ref, mix_ref, h1n_ref, act_ref, dgate_ref, dup_ref, dy_ref,
         loss_ref, dga_ref, dgr_ref, dg2_ref, gate_s, up_s) = rest[parts:]
        first = pl.program_id(0) == 0
        xa, ra = _rms(oa_ref[...], D_ATTN)
        xr, rr = _rms(or_ref[...], D_RNN)
        mix = jnp.concatenate([(xa * ga_ref[...]).astype(BF16), (xr * gr_ref[...]).astype(BF16)], axis=-1)
        mix_ref[...] = mix.T
        h1 = h0_ref[...] + _dot(mix, wo_ref[...])
        x2, r2 = _rms(h1, D_MODEL)
        h1n = (x2 * g2_ref[...]).astype(BF16)
        h1n_ref[...] = h1n
        y = h1
        for cs in range(0, D_FF, FF_CHUNK):
            sl = slice(cs, cs + FF_CHUNK)
            gate = _dot_nt(h1n, wg_ref[sl, :])
            up = _dot_nt(h1n, wu_ref[sl, :])
            gate_s[:, sl] = gate
            up_s[:, sl] = up
            act = (gate * _sigmoid(gate) * up).astype(BF16)
            act_ref[sl, :] = act.T
            y = y + _dot(act, wd_ref[sl, :])
        row = pl.program_id(0) * tm + lax.broadcasted_iota(jnp.int32, (tm, 1), 0)
        for _ in range(1, n // t):
            row = jnp.where(row >= t, row - t, row)
        tgt = jnp.concatenate([ref[0] for ref in tgt_refs], axis=0)
        err = jnp.where(row >= PAD_ROWS + N_META, y - tgt, 0.0)
        _acc(loss_ref, first, jnp.full((1, LANES), 0.5 / D_MODEL, F32) * jnp.sum(err * err))
        dy = err * (1.0 / D_MODEL)
        dyb = dy.astype(BF16)
        dy_ref[...] = dyb
        dh1n = jnp.zeros((tm, D_MODEL), F32)
        for cs in range(0, D_FF, FF_CHUNK):
            sl = slice(cs, cs + FF_CHUNK)
            dact = _dot_nt(dyb, wd_ref[sl, :])
            gate, up = gate_s[:, sl], up_s[:, sl]
            sg = _sigmoid(gate)
            dgate = (dact * up * sg * (1.0 + gate * (1.0 - sg))).astype(BF16)
            dup = (dact * gate * sg).astype(BF16)
            dgate_ref[sl, :] = dgate.T
            dup_ref[sl, :] = dup.T
            dh1n = dh1n + _dot(dgate, wg_ref[sl, :]) + _dot(dup, wu_ref[sl, :])
        _acc(dg2_ref, first, _colsum(dh1n * x2))
        dh1 = dy + _rms_bwd(dh1n, x2, r2, g2_ref[...], D_MODEL)
        dh1_ref[...] = dh1
        dmix = _dot_nt(dh1.astype(BF16), wo_ref[...])
        dma, dmr = dmix[:, :D_ATTN], dmix[:, D_ATTN:]
        _acc(dga_ref, first, _colsum(dma * xa))
        _acc(dgr_ref, first, _colsum(dmr * xr))
        doa_ref[...] = _rms_bwd(dma, xa, ra, ga_ref[...], D_ATTN)
        dor_ref[...] = _rms_bwd(dmr, xr, rr, gr_ref[...], D_RNN)

    def row(w):
        return pl.BlockSpec((tm, w), lambda i: (i, 0))

    def acc(w):
        return pl.BlockSpec((1, w), lambda i: (0, 0))

    def col(w):
        return pl.BlockSpec((w, tm), lambda i: (0, i))

    outs = [(D_ATTN, F32, row), (D_RNN, F32, row), (D_MODEL, F32, row), (D_MODEL, BF16, col), (D_MODEL, BF16, row),
            (D_FF, BF16, col), (D_FF, BF16, col), (D_FF, BF16, col), (D_MODEL, BF16, row)]
    accs = [LANES, D_ATTN, D_RNN, D_MODEL]
    per = t // head

    def target_part(p):
        def index(i):
            block = i * parts + p
            return block // per, jnp.maximum(block % per - 1, 0), 0
        return pl.BlockSpec((1, head, D_MODEL), index)

    return pl.pallas_call(
        body, name="post", grid=(n // tm,),
        in_specs=[row(D_ATTN), row(D_RNN), row(D_MODEL)] + [target_part(p) for p in range(parts)] + [
                  _const_spec((1, D_ATTN)), _const_spec((1, D_RNN)), _const_spec((1, D_MODEL)),
                  _const_spec((D_MODEL, D_MODEL)), _const_spec((D_FF, D_MODEL)), _const_spec((D_FF, D_MODEL)),
                  _const_spec((D_FF, D_MODEL))],
        out_specs=[spec(w) for w, _, spec in outs] + [acc(w) for w in accs],
        out_shape=[jax.ShapeDtypeStruct((n, w) if spec is row else (w, n), dt) for w, dt, spec in outs]
        + [jax.ShapeDtypeStruct((1, w), F32) for w in accs],
        scratch_shapes=[pltpu.VMEM((tm, D_FF), F32), pltpu.VMEM((tm, D_FF), F32)],
        compiler_params=pltpu.CompilerParams(dimension_semantics=("arbitrary",), vmem_limit_bytes=VMEM_LIMIT),
    )(oa, orn, h0, *[tgt] * parts, ga, gr, g2, w_out, w_gate, w_up, w_down)


def _in_bwd(dp, h0, dh1, ln1_g, w_in_p, srcs=(), scatter=()):
    n = h0.shape[0]
    tm = _row_tile(n)
    nk = len(srcs)
    c_in, c_out, c_shape, c_sems = _exchange_specs(srcs, scatter)

    def body(dp_ref, h0_ref, dh1_ref, g_ref, w_ref, *rest):
        dh0_ref, dg_ref = rest[nk:nk + 2]
        finish = _ride(1, *_exchange_fns(rest[:nk], rest[nk + 2:2 * nk + 2], rest[2 * nk + 2:], scatter))
        dhn = _dot(dp_ref[...], w_ref[...])
        xhat, r = _rms(h0_ref[...], D_MODEL)
        _acc(dg_ref, pl.program_id(0) == 0, _colsum(dhn * xhat))
        dh0_ref[...] = dh1_ref[...] + _rms_bwd(dhn, xhat, r, g_ref[...], D_MODEL)
        finish()

    def row(w):
        return pl.BlockSpec((tm, w), lambda i: (i, 0))

    res = pl.pallas_call(
        body, name="in_bwd", grid=(n // tm,),
        in_specs=[row(P_COLS), row(D_MODEL), row(D_MODEL), _const_spec((1, D_MODEL)), _const_spec((P_COLS, D_MODEL))] + c_in,
        out_specs=[row(D_MODEL), pl.BlockSpec((1, D_MODEL), lambda i: (0, 0))] + c_out,
        out_shape=[jax.ShapeDtypeStruct((n, D_MODEL), F32), jax.ShapeDtypeStruct((1, D_MODEL), F32)] + c_shape,
        scratch_shapes=c_sems,
        compiler_params=pltpu.CompilerParams(dimension_semantics=("arbitrary",), vmem_limit_bytes=VMEM_LIMIT),
    )(dp, h0, dh1, ln1_g, w_in_p, *srcs)
    return res[:2], res[2:]


MAX_TILE = D_FF // 2


def _pick_tile(width, cap):
    best = LANES
    for mult in range(1, width // LANES + 1):
        cand = mult * LANES
        if width % cand == 0 and cand <= cap:
            best = cand
    return best


def _matmul_tn(name, a, b, srcs=(), scatter=()):
    n, ka = a.shape
    kb = b.shape[1]
    ta, tb = _pick_tile(ka, MAX_TILE), _pick_tile(kb, MAX_TILE)
    tk = n // 2
    nk = len(srcs)
    c_in, c_out, c_shape, c_sems = _exchange_specs(srcs, scatter)

    def body(a_ref, b_ref, *rest):
        o_ref = rest[nk]
        finish = _ride(3, *_exchange_fns(rest[:nk], rest[nk + 1:2 * nk + 1], rest[2 * nk + 1:], scatter))
        _acc(o_ref, pl.program_id(2) == 0, _dot_tn(a_ref[...].astype(BF16), b_ref[...].astype(BF16)))
        finish()

    res = pl.pallas_call(
        body, name=name, grid=(ka // ta, kb // tb, n // tk),
        in_specs=[pl.BlockSpec((tk, ta), lambda i, j, k: (k, i)), pl.BlockSpec((tk, tb), lambda i, j, k: (k, j))] + c_in,
        out_specs=[pl.BlockSpec((ta, tb), lambda i, j, k: (i, j))] + c_out,
        out_shape=[jax.ShapeDtypeStruct((ka, kb), F32)] + c_shape, scratch_shapes=c_sems,
        compiler_params=pltpu.CompilerParams(dimension_semantics=("arbitrary", "arbitrary", "arbitrary"),
                                             vmem_limit_bytes=VMEM_LIMIT),
    )(a, b, *srcs)
    return res[0], res[1:]


def _matmul_shards(name, ats, b):
    count = len(ats)
    ka, n = ats[0].shape
    kb = b.shape[1]
    width = ka // N_DEV
    per = 2 if 2 * width >= 4 * LANES else 4
    ta = per * width
    steps = ka // ta

    def body(*refs):
        b_ref = refs[count]
        for c in range(count):
            @pl.when(pl.program_id(0) // steps == c)
            def _():
                out = _dot(refs[c][...], b_ref[...].astype(BF16))
                for s in range(per):
                    refs[count + 1 + c][s] = out[s * width:(s + 1) * width, :].astype(BF16)

    def block_of(c):
        return lambda i: (jnp.clip(i - c * steps, 0, steps - 1), 0)

    def shards_of(c):
        return lambda i: (jnp.clip(i - c * steps, 0, steps - 1), 0, 0)

    return pl.pallas_call(
        body, name=name, grid=(count * steps,),
        in_specs=[pl.BlockSpec((ta, n), block_of(c)) for c in range(count)] + [_const_spec((n, kb))],
        out_specs=[pl.BlockSpec((per, width, kb), shards_of(c)) for c in range(count)],
        out_shape=[jax.ShapeDtypeStruct((N_DEV, width, kb), BF16)] * count,
        compiler_params=pltpu.CompilerParams(dimension_semantics=("arbitrary",), vmem_limit_bytes=VMEM_LIMIT),
    )(*ats, b)


def _adamw_math(g8_ref, w_ref, m_ref, v_ref, g_ref, d_ref, nm_ref, nv_ref):
    g = g8_ref[0].astype(F32)
    for s in range(1, N_DEV):
        g = g + g8_ref[s].astype(F32)
    g_ref[...] = g
    nm = ADAM_B1 * m_ref[...] + (1.0 - ADAM_B1) * g
    nv = ADAM_B2 * v_ref[...] + (1.0 - ADAM_B2) * (g * g)
    nm_ref[...] = nm
    nv_ref[...] = nv
    m_hat = nm / (1.0 - ADAM_B1 ** ADAM_STEP)
    v_hat = nv / (1.0 - ADAM_B2 ** ADAM_STEP)
    d_ref[...] = -ADAM_LR * (m_hat / (jnp.sqrt(v_hat) + ADAM_EPS) + ADAM_WD * w_ref[...])


def _adamw_many(name, items):
    count = len(items)

    def body(*refs):
        ins, outs = refs[:4 * count], refs[4 * count:]
        for i in range(count):
            _adamw_math(*ins[4 * i:4 * i + 4], *outs[4 * i:4 * i + 4])

    flat = [a for item in items for a in item]
    res = pl.pallas_call(
        body, name=name,
        out_shape=[jax.ShapeDtypeStruct(item[1].shape, F32) for item in items for _ in range(4)],
        compiler_params=pltpu.CompilerParams(vmem_limit_bytes=VMEM_LIMIT),
    )(*flat)
    return [tuple(res[4 * i:4 * i + 4]) for i in range(count)]


def _adamw(name, g8, w, m, v, srcs=(), scatter=()):
    rows, cols = w.shape
    tr = rows
    for cand in (256, 176, 128, 64):
        if rows % cand == 0 and rows > cand:
            tr = cand
            break
    nk = len(srcs)
    c_in, c_out, c_shape, c_sems = _exchange_specs(srcs, scatter)

    def body(g8_ref, w_ref, m_ref, v_ref, *rest):
        outs = rest[nk:nk + 4]
        finish = _ride(1, *_exchange_fns(rest[:nk], rest[nk + 4:2 * nk + 4], rest[2 * nk + 4:], scatter))
        _adamw_math(g8_ref, w_ref, m_ref, v_ref, *outs)
        finish()

    blk = pl.BlockSpec((tr, cols), lambda i: (i, 0))
    res = pl.pallas_call(
        body, name=name, grid=(rows // tr,),
        in_specs=[pl.BlockSpec((N_DEV, tr, cols), lambda i: (0, i, 0)), blk, blk, blk] + c_in,
        out_specs=[blk] * 4 + c_out, out_shape=[jax.ShapeDtypeStruct((rows, cols), F32)] * 4 + c_shape,
        scratch_shapes=c_sems,
        compiler_params=pltpu.CompilerParams(dimension_semantics=("arbitrary" if nk else "parallel",),
                                             vmem_limit_bytes=VMEM_LIMIT),
    )(g8, w, m, v, *srcs)
    return tuple(res[:4]), res[4:]


def _exchange_specs(srcs, scatter):
    nk = len(srcs)
    if not nk:
        return [], [], [], []
    any_spec = pl.BlockSpec(memory_space=pl.ANY)
    out_shape = [jax.ShapeDtypeStruct(s.shape if sc else (N_DEV,) + s.shape, s.dtype) for s, sc in zip(srcs, scatter)]
    sems = [pltpu.SemaphoreType.DMA((nk, N_DEV - 1)), pltpu.SemaphoreType.DMA((nk, N_DEV - 1)),
            pltpu.SemaphoreType.DMA((nk,))]
    return [any_spec] * nk, [any_spec] * nk, out_shape, sems


FLIPS = ((0, 0, 1), (1, 0, 0), (0, 1, 0), (1, 1, 0), (1, 0, 1), (0, 1, 1), (1, 1, 1))
N_CHIP_PEERS = 3


def _exchange_fns(src_refs, out_refs, sems, scatter):
    nk = len(src_refs)
    if not nk:
        return (lambda: None), (lambda: None), (lambda: None)
    send_sems, recv_sems, local_sems = sems
    first = 1 + N_CHIP_PEERS

    def plan():
        x, y, c = lax.axis_index("x"), lax.axis_index("y"), lax.axis_index("c")
        me = 4 * x + 2 * y + c
        peers = [(1 - x if fx else x, 1 - y if fy else y, 1 - c if fc else c) for fx, fy, fc in FLIPS]
        pids = [4 * px + 2 * py + pc for px, py, pc in peers]

        def remote(k, j, src, dst, to):
            return pltpu.make_async_remote_copy(src_ref=src, dst_ref=dst, send_sem=send_sems.at[k, j],
                                                recv_sem=recv_sems.at[k, j], device_id=to, device_id_type=MESH)

        def mine(k, dest):
            return src_refs[k].at[dest] if scatter[k] else src_refs[k]

        local = [pltpu.make_async_copy(mine(k, me), out_refs[k].at[me], local_sems.at[k]) for k in range(nk)]
        direct = [remote(k, j, mine(k, pids[j]), out_refs[k].at[me], peers[j])
                  for k in range(nk) for j in range(len(FLIPS) if scatter[k] else first)]
        relays = {(k, j): remote(k, j, out_refs[k].at[pids[j - N_CHIP_PEERS]], out_refs[k].at[pids[j - N_CHIP_PEERS]], peers[0])
                  for k in range(nk) if not scatter[k] for j in range(first, len(FLIPS))}
        arrivals = {(k, j): remote(k, j, out_refs[k].at[pids[j]], out_refs[k].at[pids[j]], peers[j])
                    for k in range(nk) for j in range(len(FLIPS))}
        return local, direct, relays, arrivals

    def start():
        local, direct, _, _ = plan()
        for cp in local + direct:
            cp.start()

    def relay():
        _, _, relays, arrivals = plan()
        for (k, j), cp in relays.items():
            arrivals[k, j - N_CHIP_PEERS].wait_recv()
            cp.start()

    def wait():
        local, direct, relays, arrivals = plan()
        for (k, j), cp in arrivals.items():
            if (k, j + N_CHIP_PEERS) not in relays:
                cp.wait_recv()
        for cp in direct + list(relays.values()):
            cp.wait_send()
        for cp in local:
            cp.wait()

    return start, relay, wait


def _grid_step(rank):
    step, total = 0, 1
    for axis in range(rank):
        step = step * pl.num_programs(axis) + pl.program_id(axis)
        total = total * pl.num_programs(axis)
    return step, total


def _ride(rank, start, relay, wait):
    step, total = _grid_step(rank)
    pl.when(step == 0)(start)
    pl.when(step == (3 * total) // 4)(relay)
    return lambda: pl.when(step == total - 1)(wait)


def _exchange(name, srcs, scatter):
    nk = len(srcs)
    c_in, c_out, c_shape, c_sems = _exchange_specs(srcs, scatter)

    def body(*refs):
        start, relay, wait = _exchange_fns(refs[:nk], refs[nk:2 * nk], refs[2 * nk:], scatter)
        start()
        relay()
        wait()

    return pl.pallas_call(body, name=name, in_specs=c_in, out_specs=c_out, out_shape=c_shape, scratch_shapes=c_sems)(*srcs)


def _cols_from_shards(g):
    return jnp.transpose(g, (1, 0, 2)).reshape(g.shape[1], -1)


def _cols_to_shards(w):
    return jnp.transpose(w.reshape(w.shape[0], N_DEV, -1), (1, 0, 2))


def _prep(x, srcs, scatter):
    nb = x.shape[0]
    t = _t_pad()
    head = PAD_ROWS + N_META
    nk = len(srcs)
    c_in, c_out, c_shape, c_sems = _exchange_specs(srcs, scatter)

    def body(x_ref, *rest):
        h0_ref = rest[nk]
        finish = _ride(1, *_exchange_fns(rest[:nk], rest[nk + 1:2 * nk + 1], rest[2 * nk + 1:], scatter))
        lead = pl.program_id(0) == 0

        @pl.when(lead)
        def _():
            h0_ref[...] = jnp.zeros_like(h0_ref)

        @pl.when(jnp.logical_not(lead))
        def _():
            h0_ref[...] = x_ref[...]

        finish()

    src = pl.BlockSpec((nb, head, D_MODEL), lambda j: (0, jnp.maximum(j - 1, 0), 0))
    dst = pl.BlockSpec((nb, head, D_MODEL), lambda j: (0, j, 0))
    res = pl.pallas_call(
        body, name="prep", grid=(t // head,), in_specs=[src] + c_in, out_specs=[dst] + c_out,
        out_shape=[jax.ShapeDtypeStruct((nb, t, D_MODEL), F32)] + c_shape, scratch_shapes=c_sems,
        compiler_params=pltpu.CompilerParams(dimension_semantics=("arbitrary",)),
    )(x, *srcs)
    return res[0], res[1:]


def _rope_tables(n):
    t = _t_pad()
    pos = np.arange(t, dtype=np.float32) - np.float32(PAD_ROWS)
    half = QK_ROPE // 2
    freqs = (1.0 / (ROPE_THETA ** (np.arange(half, dtype=np.float32) / half))).astype(np.float32)
    ang = pos[:, None] * freqs[None, :]
    cos, sin = np.cos(ang), np.sin(ang)
    z = lambda w: np.zeros((t, w), np.float32)
    c = np.concatenate([np.ones((t, QK_NOPE), np.float32), cos, cos, z(HEAD_PAD - QK_HEAD)], axis=1)
    s1 = np.concatenate([z(QK_NOPE + half), sin, z(HEAD_PAD - QK_HEAD)], axis=1)
    s2 = np.concatenate([z(QK_NOPE), -sin, z(HEAD_PAD - QK_NOPE - half)], axis=1)
    return tuple(jnp.asarray(np.tile(a, (n // t, 1))) for a in (c, s1, s2))


def _block_diag_gates(lru_wa, lru_wi):
    eye = jnp.eye(2, dtype=lru_wa.dtype)

    def bd(w):
        w = w.reshape(2, D_RNN // LANES, 2, RNN_BW, RNN_BW)
        full = w[:, :, :, :, None, :] * eye[None, None, :, None, :, None]
        return full.reshape(2, D_RNN // LANES, LANES, LANES)

    a, i = bd(lru_wa), bd(lru_wi)
    return jnp.concatenate([a[0], i[0], a[1], i[1]], axis=-1)


def _unblock_gates(dw):
    nb = D_RNN // LANES
    parts = dw.reshape(nb, 2, RNN_BW, 4, 2, RNN_BW)
    diag = jnp.stack([parts[:, k, :, :, k, :] for k in range(2)], axis=1)
    diag = jnp.transpose(diag, (3, 0, 1, 2, 4)).reshape(4, 2 * nb, RNN_BW, RNN_BW)
    return jnp.stack([diag[0], diag[2]]), jnp.stack([diag[1], diag[3]])


WEIGHTS = ("meta_tokens", "ln1_g", "w_in", "q_a_norm_g", "w_uq", "kv_a_norm_g", "w_ukv", "q_norm_g", "k_norm_g",
           "conv_w", "conv_b", "lru_wa", "lru_ba", "lru_wi", "lru_bi", "lru_lambda", "attn_out_g", "rnn_out_g",
           "w_out", "ln2_g", "w_gate", "w_up", "w_down")
BIG = ("w_in", "w_uq", "w_ukv", "w_out", "w_gate", "w_up", "w_down")
TRANSPOSED = ("w_in", "w_uq", "w_gate", "w_up")
ROW_SHARDED = ("w_out", "w_down") + TRANSPOSED
REPLICATED = ("ln1_g", "q_a_norm_g", "kv_a_norm_g", "q_norm_g", "k_norm_g", "conv_b", "lru_wa", "lru_wi",
              "attn_out_g", "rnn_out_g", "ln2_g")
WHOLE = REPLICATED + ("loss",)
G_FIRST = ("w_in", "meta_tokens")
G_MID = ("w_uq", "w_ukv", "conv_w", "lru_ba", "lru_bi", "lru_lambda")
LATE = ("w_out", "w_gate", "w_up", "w_down")
G_LAST = ("meta_tokens", "ln1_g")


def _local_step(x, tgt, ex):
    nb = x.shape[0]
    t = _t_pad()
    n = nb * t
    local = ex.local
    h0, got = _prep(x, *ex.gather_srcs(G_FIRST))
    first = ex.gathered(G_FIRST, got)
    meta, w_in = first["meta_tokens"], first["w_in"]
    h0 = h0.at[:, PAD_ROWS:PAD_ROWS + N_META].set(jnp.broadcast_to(meta[None], (nb, N_META, D_MODEL))).reshape(n, D_MODEL)

    zr = lambda r: jnp.zeros((r, D_MODEL), w_in.dtype)
    w_in_p = jnp.concatenate([w_in[:OFF_CKV], w_in[OFF_KR:], zr(QK_NOPE), w_in[OFF_CKV:OFF_KR], zr(HEAD_PAD - QK_HEAD)],
                             axis=0)
    pad_g = lambda g: jnp.pad(g, ((0, 0), (0, HEAD_PAD - QK_HEAD)))
    qg, kg = pad_g(local["q_norm_g"]), pad_g(local["k_norm_g"])
    rc, rs1, rs2 = _rope_tables(n)
    wblk = _block_diag_gates(local["lru_wa"].reshape(2, -1, RNN_BW, RNN_BW),
                             local["lru_wi"].reshape(2, -1, RNN_BW, RNN_BW)).astype(BF16)
    nblk = D_RNN // LANES

    (hn, cq, ckv, xr, xg, kr), got = _in_proj(h0, local["ln1_g"], w_in_p, *ex.gather_srcs(G_MID))
    w = ex.gathered(G_MID, got)
    w_uq_p = jnp.pad(w["w_uq"].reshape(N_HEADS, QK_HEAD, Q_LORA), ((0, 0), (0, HEAD_PAD - QK_HEAD), (0, 0))
                     ).reshape(QP_COLS, Q_LORA)
    ukv = w["w_ukv"].reshape(KV_LORA, N_HEADS, QK_NOPE + V_HEAD)
    w_uk_p = jnp.pad(ukv[:, :, :QK_NOPE], ((0, 0), (0, 0), (0, HEAD_PAD - QK_NOPE))).reshape(KV_LORA, QP_COLS)
    w_v = ukv[:, :, QK_NOPE:].reshape(KV_LORA, D_ATTN)
    gbias = jnp.stack([w["lru_ba"][0], w["lru_bi"][0], w["lru_ba"][1], w["lru_bi"][1]], axis=0)
    gbias = jnp.transpose(gbias.reshape(4, nblk, LANES), (1, 0, 2)).reshape(nblk, 1, 4 * LANES)

    q, k, v = _qkv_fwd(cq, ckv, kr, local["q_a_norm_g"], local["kv_a_norm_g"], w_uq_p, w_uk_p, w_v, qg, kg, rc, rs1, rs2)
    oa, probs, got = _attn_fwd(q, k, v, *ex.gather_srcs(LATE))
    late = ex.gathered(LATE, got)
    orn, rnn_saved = _rnn_fwd(xr, xg, w["conv_w"], local["conv_b"], wblk, gbias, w["lru_lambda"])
    (doa, dor, dh1, mix_t, h1n, act_t, dgate_t, dup_t, dyb, loss, dga, dgr, dg2) = _post(
        oa, orn, h0, tgt, local["attn_out_g"], local["rnn_out_g"], local["ln2_g"], late["w_out"], late["w_gate"],
        late["w_up"], late["w_down"])
    dw_gate, dw_up = _matmul_shards("dw_gate_up", [dgate_t, dup_t], h1n)
    wire = {"w_out": _matmul_shards("dw_out", [mix_t], dh1)[0], "w_gate": dw_gate, "w_up": dw_up,
            "w_down": _matmul_shards("dw_down", [act_t], dyb)[0]}
    names = ("w_gate",)
    (dxr, dxg, dcw, dcb, dwblk, dgb, dlam), got = _rnn_bwd(xr, xg, dor, rnn_saved, w["conv_w"], wblk, w["lru_lambda"],
                                                           *ex.scatter_srcs(names, wire))
    summed = ex.scattered(names, wire, got)
    dwa, dwi = _unblock_gates(dwblk)
    dgb = jnp.transpose(dgb.reshape(nblk, 4, LANES), (1, 0, 2)).reshape(4, D_RNN)
    names = ("w_out", "w_up", "w_down")
    (dq_r, dk_r, dv), got = _attn_bwd(q, k, v, doa, oa, probs, *ex.scatter_srcs(names, wire))
    summed.update(ex.scattered(names, wire, got))
    wire = ex.to_wire({
        "conv_w": dcw, "conv_b": dcb, "lru_wa": dwa.reshape(-1, RNN_BW), "lru_ba": jnp.stack([dgb[0], dgb[2]]),
        "lru_wi": dwi.reshape(-1, RNN_BW), "lru_bi": jnp.stack([dgb[1], dgb[3]]), "lru_lambda": dlam,
        "attn_out_g": dga, "rnn_out_g": dgr, "ln2_g": dg2, "loss": loss})
    names = tuple(wire)
    (dp, qa, kva, dqp, dkv, dqg, dkg, dgqa, dgkva), got = _qkv_bwd(
        cq, ckv, kr, dq_r, dk_r, dv, dxr, dxg, local["q_a_norm_g"], local["kv_a_norm_g"], w_uq_p, w_uk_p, w_v, qg, kg,
        rc, rs1, rs2, *ex.scatter_srcs(names, wire))
    summed.update(ex.scattered(names, wire, got))
    dw_uq_p, _ = _matmul_tn("dw_uq", dqp, qa)
    dw_kv, _ = _matmul_tn("dw_ukv", kva, dkv)
    dw_uq = dw_uq_p.reshape(N_HEADS, HEAD_PAD, Q_LORA)[:, :QK_HEAD].reshape(N_HEADS * QK_HEAD, Q_LORA)
    dw_ukv = jnp.concatenate([dw_kv[:, :QP_COLS].reshape(KV_LORA, N_HEADS, HEAD_PAD)[:, :, :QK_NOPE],
                              dw_kv[:, QP_COLS:].reshape(KV_LORA, N_HEADS, V_HEAD)], axis=2).reshape(KV_LORA, -1)
    wire = ex.to_wire({"q_a_norm_g": dgqa, "w_uq": dw_uq, "kv_a_norm_g": dgkva, "w_ukv": dw_ukv,
                       "q_norm_g": dqg[:, :QK_HEAD], "k_norm_g": dkg[:, :QK_HEAD]})
    names = tuple(wire)
    dw_in_p, got = _matmul_tn("dw_in", dp, hn, *ex.scatter_srcs(names, wire))
    summed.update(ex.scattered(names, wire, got))
    kr0 = OFF_CKV + 2 * D_RNN + QK_NOPE
    dw_in = jnp.concatenate([dw_in_p[:OFF_CKV], dw_in_p[kr0:kr0 + QK_ROPE], dw_in_p[OFF_CKV:OFF_CKV + 2 * D_RNN]], axis=0)
    wire = ex.to_wire({"w_in": dw_in})
    (dh0, dg1), got = _in_bwd(dp, h0, dh1, local["ln1_g"], w_in_p, *ex.scatter_srcs(("w_in",), wire))
    summed.update(ex.scattered(("w_in",), wire, got))

    dh0 = dh0.reshape(nb, t, D_MODEL)
    wire = ex.to_wire({"meta_tokens": jnp.sum(dh0[:, PAD_ROWS:PAD_ROWS + N_META], axis=0), "ln1_g": dg1})
    return dh0[:, PAD_ROWS + N_META:], summed, wire


class _MeshExchange:
    def __init__(self, shards):
        self.local = shards

    @staticmethod
    def run(name, srcs, scatter):
        return _exchange(name, srcs, scatter)

    def gather_srcs(self, names):
        return [self.local[k].astype(BF16) if k in BIG else self.local[k] for k in names], [False] * len(names)

    @staticmethod
    def gathered(names, outs):
        return {k: g.reshape(-1, g.shape[-1]) if k in ROW_SHARDED else _cols_from_shards(g) for k, g in zip(names, outs)}

    @staticmethod
    def to_wire(grads):
        wire = {}
        for k, g in grads.items():
            if k in WHOLE:
                wire[k] = g
            elif k in ROW_SHARDED:
                wire[k] = g.reshape(N_DEV, -1, g.shape[-1]).astype(BF16)
            else:
                wire[k] = _cols_to_shards(g).astype(BF16) if k in BIG else _cols_to_shards(g)
        return wire

    @staticmethod
    def scatter_srcs(names, wire):
        return [wire[k] for k in names], [k not in WHOLE for k in names]

    @staticmethod
    def scattered(names, wire, outs):
        return dict(zip(names, outs))


def kernel(x, meta_tokens, ln1_g, w_in, q_a_norm_g, w_uq, kv_a_norm_g, w_ukv, q_norm_g, k_norm_g, conv_w, conv_b, lru_wa, lru_ba, lru_wi, lru_bi, lru_lambda, attn_out_g, rnn_out_g, w_out, ln2_g, w_gate, w_up, w_down, loss_target, m_meta_tokens, m_ln1_g, m_w_in, m_q_a_norm_g, m_w_uq, m_kv_a_norm_g, m_w_ukv, m_q_norm_g, m_k_norm_g, m_conv_w, m_conv_b, m_lru_wa, m_lru_ba, m_lru_wi, m_lru_bi, m_lru_lambda, m_attn_out_g, m_rnn_out_g, m_w_out, m_ln2_g, m_w_gate, m_w_up, m_w_down, v_meta_tokens, v_ln1_g, v_w_in, v_q_a_norm_g, v_w_uq, v_kv_a_norm_g, v_w_ukv, v_q_norm_g, v_k_norm_g, v_conv_w, v_conv_b, v_lru_wa, v_lru_ba, v_lru_wi, v_lru_bi, v_lru_lambda, v_attn_out_g, v_rnn_out_g, v_w_out, v_ln2_g, v_w_gate, v_w_up, v_w_down):
    given = (meta_tokens, ln1_g, w_in, q_a_norm_g, w_uq, kv_a_norm_g, w_ukv, q_norm_g, k_norm_g, conv_w, conv_b,
             lru_wa, lru_ba, lru_wi, lru_bi, lru_lambda, attn_out_g, rnn_out_g, w_out, ln2_g, w_gate, w_up, w_down)
    moments_m = (m_meta_tokens, m_ln1_g, m_w_in, m_q_a_norm_g, m_w_uq, m_kv_a_norm_g, m_w_ukv, m_q_norm_g, m_k_norm_g,
                 m_conv_w, m_conv_b, m_lru_wa, m_lru_ba, m_lru_wi, m_lru_bi, m_lru_lambda, m_attn_out_g, m_rnn_out_g,
                 m_w_out, m_ln2_g, m_w_gate, m_w_up, m_w_down)
    moments_v = (v_meta_tokens, v_ln1_g, v_w_in, v_q_a_norm_g, v_w_uq, v_kv_a_norm_g, v_w_ukv, v_q_norm_g, v_k_norm_g,
                 v_conv_w, v_conv_b, v_lru_wa, v_lru_ba, v_lru_wi, v_lru_bi, v_lru_lambda, v_attn_out_g, v_rnn_out_g,
                 v_w_out, v_ln2_g, v_w_gate, v_w_up, v_w_down)
    shapes = {k: a.shape for k, a in zip(WEIGHTS, given)}

    def two_d(k, a):
        a = a.reshape(-1, a.shape[-1])
        return a.T if k in TRANSPOSED else a

    w = {k: two_d(k, a) for k, a in zip(WEIGHTS, given)}
    m = {k: two_d(k, a) for k, a in zip(WEIGHTS, moments_m)}
    v = {k: two_d(k, a) for k, a in zip(WEIGHTS, moments_v)}

    ex = _MeshExchange(w)
    grad_x, parts, wire = _local_step(x, loss_target, ex)

    tiled = ("w_in", "w_gate", "w_up", "w_down")
    new = {}
    for k in tiled:
        hosted = ex.scatter_srcs(G_LAST, wire) if k == "w_gate" else ((), ())
        new[k], got = _adamw("adamw_" + k, parts[k], w[k], m[k], v[k], *hosted)
        if k == "w_gate":
            parts.update(ex.scattered(G_LAST, wire, got))
    small = [k for k in WEIGHTS if k not in tiled]
    new.update(zip(small, _adamw_many("adamw_small", [(parts[k], w[k], m[k], v[k]) for k in small])))

    loss = jnp.sum(parts["loss"][:, 0, 0])
    outs = [loss, grad_x]
    for idx in range(4):
        outs += [(new[k][idx].T if k in TRANSPOSED else new[k][idx]).reshape(shapes[k]) for k in WEIGHTS]
    return tuple(outs)
```

```python
import functools
import math

import numpy as np
import jax
import jax.numpy as jnp
from jax import lax
from jax.experimental import pallas as pl
from jax.experimental.pallas import tpu as pltpu

F32 = jnp.float32
BF16 = jnp.bfloat16

D_MODEL = 1024
N_META = 16
SEQ = 2048
N_HEADS = 8
QK_NOPE = 64
QK_ROPE = 32
QK_HEAD = QK_NOPE + QK_ROPE
V_HEAD = 64
D_ATTN = N_HEADS * V_HEAD
Q_LORA = 384
KV_LORA = 256
D_RNN = 512
RNN_BW = 64
D_FF = 2816
EPS = 1e-6
LRU_C = 8.0
ROPE_THETA = 10000.0
OFF_CKV = Q_LORA + KV_LORA
OFF_KR = OFF_CKV + QK_ROPE
IN_COLS = OFF_KR + 2 * D_RNN

ADAM_LR = 0.001
ADAM_B1 = 0.9
ADAM_B2 = 0.999
ADAM_EPS = 1e-08
ADAM_WD = 0.01
ADAM_STEP = 10

N_DEV = 8
LANES = 128
HEAD_PAD = LANES
PAD_ROWS = LANES - N_META
QP_COLS = N_HEADS * HEAD_PAD
P_COLS = OFF_CKV + 2 * D_RNN + LANES
FF_CHUNK = D_FF
VMEM_LIMIT = 56 * 1024 * 1024
MESH = pl.DeviceIdType.MESH


def _t_pad():
    return PAD_ROWS + N_META + SEQ


def _row_tile(n):
    return 256 if n % 256 == 0 else 128


def _wide_row_tile(n):
    quarter = _t_pad() // 4
    return quarter if quarter % 16 == 0 and n % quarter == 0 else _row_tile(n)


def _const_spec(shape):
    nd = len(shape)
    return pl.BlockSpec(shape, lambda *_: (0,) * nd, pipeline_mode=pl.Buffered(1))


def _rms(x, d):
    r = lax.rsqrt(jnp.sum(x * x, axis=-1, keepdims=True) * (1.0 / d) + EPS)
    return x * r, r


def _rms_bwd(dy, xhat, r, g, d):
    dxh = dy * g
    return r * (dxh - xhat * (jnp.sum(dxh * xhat, axis=-1, keepdims=True) * (1.0 / d)))


def _colsum(x):
    return jnp.sum(x, axis=0, keepdims=True)


def _dot(a, b):
    return jnp.dot(a, b, preferred_element_type=F32)


def _dot_nt(a, b):
    return lax.dot_general(a, b, (((1,), (1,)), ((), ())), preferred_element_type=F32)


def _dot_tn(a, b):
    return lax.dot_general(a, b, (((0,), (0,)), ((), ())), preferred_element_type=F32)


def _rope(x, c, s1, s2):
    return x * c + pltpu.roll(x, 16, 1) * s1 + pltpu.roll(x, HEAD_PAD - 16, 1) * s2


def _rope_bwd(dy, c, s1, s2):
    return dy * c + pltpu.roll(dy * s1, HEAD_PAD - 16, 1) + pltpu.roll(dy * s2, 16, 1)


def _acc(ref, first, val):
    @pl.when(first)
    def _():
        ref[...] = val

    @pl.when(jnp.logical_not(first))
    def _():
        ref[...] += val


def _in_proj(h0, ln1_g, w_in_p, srcs=(), scatter=()):
    n = h0.shape[0]
    tm = _wide_row_tile(n)
    nk = len(srcs)
    c_in, c_out, c_shape, c_sems = _exchange_specs(srcs, scatter)

    def body(h_ref, g_ref, w_ref, *rest):
        hn_ref, cq_ref, ckv_ref, xr_ref, xg_ref, kr_ref = rest[nk:nk + 6]
        finish = _ride(1, *_exchange_fns(rest[:nk], rest[nk + 6:2 * nk + 6], rest[2 * nk + 6:], scatter))
        xhat, _ = _rms(h_ref[...], D_MODEL)
        hn = (xhat * g_ref[...]).astype(BF16)
        hn_ref[...] = hn
        p = _dot_nt(hn, w_ref[...])
        cq_ref[...] = p[:, :Q_LORA]
        ckv_ref[...] = p[:, Q_LORA:OFF_CKV]
        xr_ref[...] = p[:, OFF_CKV:OFF_CKV + D_RNN]
        xg_ref[...] = p[:, OFF_CKV + D_RNN:OFF_CKV + 2 * D_RNN]
        kr_ref[...] = p[:, OFF_CKV + 2 * D_RNN:]
        finish()

    def row(w):
        return pl.BlockSpec((tm, w), lambda i: (i, 0))

    widths = (D_MODEL, Q_LORA, KV_LORA, D_RNN, D_RNN, LANES)
    res = pl.pallas_call(
        body, name="in_proj", grid=(n // tm,),
        in_specs=[row(D_MODEL), _const_spec((1, D_MODEL)), _const_spec((P_COLS, D_MODEL))] + c_in,
        out_specs=[row(w) for w in widths] + c_out,
        out_shape=[jax.ShapeDtypeStruct((n, w), BF16 if k == 0 else F32) for k, w in enumerate(widths)] + c_shape,
        scratch_shapes=c_sems,
        compiler_params=pltpu.CompilerParams(dimension_semantics=("arbitrary",), vmem_limit_bytes=VMEM_LIMIT),
    )(h0, ln1_g, w_in_p, *srcs)
    return res[:6], res[6:]


def _qkv_fwd(cq, ckv, kr, gqa, gkva, w_uq_p, w_uk_p, w_v, qg, kg, rc, rs1, rs2):
    n = cq.shape[0]
    tm = _wide_row_tile(n)

    def body(cq_ref, ckv_ref, kr_ref, gqa_ref, gkva_ref, wuq_ref, wuk_ref, wv_ref, qg_ref, kg_ref,
             c_ref, s1_ref, s2_ref, q_ref, k_ref, v_ref):
        xq, _ = _rms(cq_ref[...], Q_LORA)
        qa = (xq * gqa_ref[...]).astype(BF16)
        q = _dot_nt(qa, wuq_ref[...])
        xkv, _ = _rms(ckv_ref[...], KV_LORA)
        kva = (xkv * gkva_ref[...]).astype(BF16)
        kn = _dot(kva, wuk_ref[...])
        v_ref[...] = _dot(kva, wv_ref[...]).astype(BF16)
        krp = kr_ref[...]
        c, s1, s2 = c_ref[...], s1_ref[...], s2_ref[...]
        for h in range(N_HEADS):
            sl = slice(h * HEAD_PAD, (h + 1) * HEAD_PAD)
            qh, _ = _rms(q[:, sl], QK_HEAD)
            q_ref[:, sl] = _rope(qh * qg_ref[...], c, s1, s2).astype(BF16)
            kh, _ = _rms(kn[:, sl] + krp, QK_HEAD)
            k_ref[:, sl] = _rope(kh * kg_ref[...], c, s1, s2).astype(BF16)

    def row(w):
        return pl.BlockSpec((tm, w), lambda i: (i, 0))

    return pl.pallas_call(
        body, name="qkv_fwd", grid=(n // tm,),
        in_specs=[row(Q_LORA), row(KV_LORA), row(LANES), _const_spec((1, Q_LORA)), _const_spec((1, KV_LORA)),
                  _const_spec((QP_COLS, Q_LORA)), _const_spec((KV_LORA, QP_COLS)), _const_spec((KV_LORA, D_ATTN)),
                  _const_spec((1, LANES)), _const_spec((1, LANES)), row(LANES), row(LANES), row(LANES)],
        out_specs=[row(QP_COLS), row(QP_COLS), row(D_ATTN)],
        out_shape=[jax.ShapeDtypeStruct((n, QP_COLS), BF16), jax.ShapeDtypeStruct((n, QP_COLS), BF16),
                   jax.ShapeDtypeStruct((n, D_ATTN), BF16)],
        compiler_params=pltpu.CompilerParams(dimension_semantics=("parallel",), vmem_limit_bytes=VMEM_LIMIT),
    )(cq, ckv, kr, gqa, gkva, w_uq_p, w_uk_p, w_v, qg, kg, rc, rs1, rs2)


def _qkv_bwd(cq, ckv, kr, dq_r, dk_r, dv, dxr, dxg, gqa, gkva, w_uq_p, w_uk_p, w_v, qg, kg, rc, rs1, rs2,
             srcs=(), scatter=()):
    n = cq.shape[0]
    tm = _wide_row_tile(n)
    nk = len(srcs)
    c_in, c_out, c_shape, c_sems = _exchange_specs(srcs, scatter)

    def body(cq_ref, ckv_ref, kr_ref, dq_ref, dk_ref, dv_ref, dxr_ref, dxg_ref, gqa_ref, gkva_ref, wuq_ref, wuk_ref,
             wv_ref, qg_ref, kg_ref, c_ref, s1_ref, s2_ref, *rest):
        dp_ref, qa_ref, kva_ref, dqp_ref, dkv_ref, dqg_ref, dkg_ref, dgqa_ref, dgkva_ref = rest[nk:nk + 9]
        finish = _ride(1, *_exchange_fns(rest[:nk], rest[nk + 9:2 * nk + 9], rest[2 * nk + 9:], scatter))
        first = pl.program_id(0) == 0
        dp_ref[:, OFF_CKV:OFF_CKV + D_RNN] = dxr_ref[...].astype(BF16)
        dp_ref[:, OFF_CKV + D_RNN:OFF_CKV + 2 * D_RNN] = dxg_ref[...].astype(BF16)
        xq, rq = _rms(cq_ref[...], Q_LORA)
        qa = (xq * gqa_ref[...]).astype(BF16)
        qa_ref[...] = qa
        q = _dot_nt(qa, wuq_ref[...])
        xkv, rkv = _rms(ckv_ref[...], KV_LORA)
        kva = (xkv * gkva_ref[...]).astype(BF16)
        kva_ref[...] = kva
        kn = _dot(kva, wuk_ref[...])
        krp = kr_ref[...]
        c, s1, s2 = c_ref[...], s1_ref[...], s2_ref[...]
        lane = lax.broadcasted_iota(jnp.int32, (tm, HEAD_PAD), 1)
        rope_lanes = jnp.logical_and(lane >= QK_NOPE, lane < QK_HEAD)
        dqg = jnp.zeros((1, HEAD_PAD), F32)
        dkg = jnp.zeros((1, HEAD_PAD), F32)
        dkr = jnp.zeros((tm, HEAD_PAD), F32)
        for h in range(N_HEADS):
            sl = slice(h * HEAD_PAD, (h + 1) * HEAD_PAD)
            qh, rqh = _rms(q[:, sl], QK_HEAD)
            dy = _rope_bwd(dq_ref[:, sl], c, s1, s2)
            dqg = dqg + _colsum(dy * qh)
            dqp_ref[:, sl] = _rms_bwd(dy, qh, rqh, qg_ref[...], QK_HEAD).astype(BF16)
            kh, rkh = _rms(kn[:, sl] + krp, QK_HEAD)
            dyk = _rope_bwd(dk_ref[:, sl], c, s1, s2)
            dkg = dkg + _colsum(dyk * kh)
            dkh = _rms_bwd(dyk, kh, rkh, kg_ref[...], QK_HEAD)
            dkv_ref[:, sl] = dkh.astype(BF16)
            dkr = dkr + jnp.where(rope_lanes, dkh, 0.0)
        dkv_ref[:, QP_COLS:] = dv_ref[...].astype(BF16)
        dp_ref[:, OFF_CKV + 2 * D_RNN:] = dkr.astype(BF16)
        dqa = _dot(dqp_ref[...], wuq_ref[...])
        dp_ref[:, :Q_LORA] = _rms_bwd(dqa, xq, rq, gqa_ref[...], Q_LORA).astype(BF16)
        dkva = _dot_nt(dkv_ref[:, :QP_COLS], wuk_ref[...]) + _dot_nt(dkv_ref[:, QP_COLS:], wv_ref[...])
        dp_ref[:, Q_LORA:OFF_CKV] = _rms_bwd(dkva, xkv, rkv, gkva_ref[...], KV_LORA).astype(BF16)
        _acc(dqg_ref, first, dqg)
        _acc(dkg_ref, first, dkg)
        _acc(dgqa_ref, first, _colsum(dqa * xq))
        _acc(dgkva_ref, first, _colsum(dkva * xkv))
        finish()

    def row(w):
        return pl.BlockSpec((tm, w), lambda i: (i, 0))

    def acc(w):
        return pl.BlockSpec((1, w), lambda i: (0, 0))

    res = pl.pallas_call(
        body, name="qkv_bwd", grid=(n // tm,),
        in_specs=[row(Q_LORA), row(KV_LORA), row(LANES), row(QP_COLS), row(QP_COLS), row(D_ATTN), row(D_RNN), row(D_RNN),
                  _const_spec((1, Q_LORA)), _const_spec((1, KV_LORA)),
                  _const_spec((QP_COLS, Q_LORA)), _const_spec((KV_LORA, QP_COLS)), _const_spec((KV_LORA, D_ATTN)),
                  _const_spec((1, LANES)), _const_spec((1, LANES)), row(LANES), row(LANES), row(LANES)] + c_in,
        out_specs=[row(P_COLS), row(Q_LORA), row(KV_LORA), row(QP_COLS),
                   row(QP_COLS + D_ATTN), acc(LANES), acc(LANES), acc(Q_LORA), acc(KV_LORA)] + c_out,
        out_shape=[jax.ShapeDtypeStruct((n, P_COLS), BF16), jax.ShapeDtypeStruct((n, Q_LORA), BF16),
                   jax.ShapeDtypeStruct((n, KV_LORA), BF16), jax.ShapeDtypeStruct((n, QP_COLS), BF16),
                   jax.ShapeDtypeStruct((n, QP_COLS + D_ATTN), BF16),
                   jax.ShapeDtypeStruct((1, LANES), F32), jax.ShapeDtypeStruct((1, LANES), F32),
                   jax.ShapeDtypeStruct((1, Q_LORA), F32), jax.ShapeDtypeStruct((1, KV_LORA), F32)] + c_shape,
        scratch_shapes=c_sems,
        compiler_params=pltpu.CompilerParams(dimension_semantics=("arbitrary",), vmem_limit_bytes=VMEM_LIMIT),
    )(cq, ckv, kr, dq_r, dk_r, dv, dxr, dxg, gqa, gkva, w_uq_p, w_uk_p, w_v, qg, kg, rc, rs1, rs2, *srcs)
    return res[:9], res[9:]


FWD_KEY_CHUNK = 2 * LANES
BWD_KEY_CHUNK = 4 * LANES


def _key_chunks(t, chunk):
    count = max(t // chunk, 1)
    first = t - chunk * (count - 1)
    return [(0, first)] + [(first + chunk * c, chunk) for c in range(count - 1)]


def _attn_specs(t, tq):
    nq = t // tq
    qspec = pl.BlockSpec((tq, 2 * HEAD_PAD), lambda b, hp, i: (b * nq + i, hp))
    kspec = pl.BlockSpec((t, 2 * HEAD_PAD), lambda b, hp, i: (b, hp))
    vspec = pl.BlockSpec((t, 2 * V_HEAD), lambda b, hp, i: (b, hp))
    ospec = pl.BlockSpec((tq, 2 * V_HEAD), lambda b, hp, i: (b * nq + i, hp))
    return nq, qspec, kspec, vspec, ospec


def _probs_spec(t, tq):
    return pl.BlockSpec((1, 2, tq, t), lambda b, hp, i: (b, hp, i, 0))


def _attn_fwd(q, k, v, srcs=(), scatter=()):
    n = q.shape[0]
    t = _t_pad()
    tq = t // 2
    nq, qspec, kspec, vspec, ospec = _attn_specs(t, tq)
    nk = len(srcs)
    c_in, c_out, c_shape, c_sems = _exchange_specs(srcs, scatter)

    def body(q_ref, k_ref, v_ref, *rest):
        o_ref, l_ref, p_ref = rest[nk:nk + 3]
        finish = _ride(3, *_exchange_fns(rest[:nk], rest[nk + 3:2 * nk + 3], rest[2 * nk + 3:], scatter))
        lane = lax.broadcasted_iota(jnp.int32, (tq, 2 * V_HEAD), 1)
        outs = []
        sums = []
        for j in range(2):
            sl = slice(j * HEAD_PAD, (j + 1) * HEAD_PAD)
            qh = q_ref[:, sl]

            def scores(start, size):
                s = _dot_nt(qh, k_ref[start:start + size, sl])
                if start < PAD_ROWS:
                    key = lax.broadcasted_iota(jnp.int32, (tq, size), 1) + start
                    s = jnp.where(key >= PAD_ROWS, s, -jnp.inf)
                return s

            chunks = _key_chunks(t, FWD_KEY_CHUNK)

            def lane_folds(x):
                return [x[:, c:c + LANES] for c in range(0, x.shape[1], LANES)]

            top = functools.reduce(jnp.maximum, [blk for c in chunks for blk in lane_folds(scores(*c))])
            top = jnp.max(top, axis=-1, keepdims=True)
            l = jnp.zeros((tq, LANES), F32)
            pv = jnp.zeros((tq, 2 * V_HEAD), F32)
            for start, size in chunks:
                e = jnp.exp2((scores(start, size) - top) * (QK_HEAD ** -0.5 * math.log2(math.e)))
                l = functools.reduce(jnp.add, lane_folds(e), l)
                e = e.astype(BF16)
                p_ref[0, j, :, start:start + size] = e
                pv = pv + _dot(e, v_ref[start:start + size, :])
            l = jnp.sum(l, axis=-1, keepdims=True)
            outs.append(pv / l)
            sums.append(l)
        o_ref[...] = jnp.where(lane < V_HEAD, outs[0], outs[1])
        l_ref[...] = jnp.where(lane < V_HEAD, sums[0], sums[1])
        finish()

    res = pl.pallas_call(
        body, name="attn_fwd", grid=(n // t, N_HEADS // 2, nq),
        in_specs=[qspec, kspec, vspec] + c_in, out_specs=[ospec, ospec, _probs_spec(t, tq)] + c_out,
        out_shape=[jax.ShapeDtypeStruct((n, D_ATTN), F32), jax.ShapeDtypeStruct((n, D_ATTN), F32),
                   jax.ShapeDtypeStruct((n // t, N_HEADS, t, t), BF16)] + c_shape,
        scratch_shapes=c_sems,
        compiler_params=pltpu.CompilerParams(dimension_semantics=("arbitrary", "arbitrary", "arbitrary"),
                                             vmem_limit_bytes=VMEM_LIMIT),
    )(q, k, v, *srcs)
    return res[0], (res[1], res[2]), res[3:]


def _attn_bwd(q, k, v, do, o, probs, srcs=(), scatter=()):
    n = q.shape[0]
    t = _t_pad()
    tq = t // 2
    nq, qspec, kspec, vspec, ospec = _attn_specs(t, tq)
    nk = len(srcs)
    c_in, c_out, c_shape, c_sems = _exchange_specs(srcs, scatter)

    def body(q_ref, k_ref, v_ref, do_ref, o_ref, l_ref, p_ref, *rest):
        dq_ref, dk_ref, dv_ref = rest[nk:nk + 3]
        finish = _ride(3, *_exchange_fns(rest[:nk], rest[nk + 3:2 * nk + 3], rest[2 * nk + 3:], scatter))

        @pl.when(pl.program_id(2) == 0)
        def _():
            dk_ref[...] = jnp.zeros_like(dk_ref)
            dv_ref[...] = jnp.zeros_like(dv_ref)

        lane = lax.broadcasted_iota(jnp.int32, (tq, 2 * V_HEAD), 1)
        do = do_ref[...]
        do_o = do * o_ref[...]
        chunks = _key_chunks(t, BWD_KEY_CHUNK)
        dvs = [None] * len(chunks)
        for j in range(2):
            sl = slice(j * HEAD_PAD, (j + 1) * HEAD_PAD)
            qh = q_ref[:, sl]
            in_head = (lane < V_HEAD) if j == 0 else (lane >= V_HEAD)
            inv_l = 1.0 / l_ref[:, j * V_HEAD:j * V_HEAD + 1]
            doh = jnp.where(in_head, do, 0.0).astype(BF16)
            doh_n = jnp.where(in_head, do * inv_l, 0.0).astype(BF16)
            delta = jnp.sum(jnp.where(in_head, do_o, 0.0), axis=-1, keepdims=True)
            row_scale = inv_l * (QK_HEAD ** -0.5)
            dq = jnp.zeros((tq, HEAD_PAD), F32)
            for c, (start, size) in enumerate(chunks):
                rows = slice(start, start + size)
                e = p_ref[0, j, :, rows]
                dp = _dot_nt(doh, v_ref[rows, :])
                ds = (e.astype(F32) * (dp - delta) * row_scale).astype(BF16)
                dq = dq + _dot(ds, k_ref[rows, sl])
                dk_ref[rows, sl] += _dot_tn(ds, qh)
                dvc = _dot_tn(e, doh_n)
                dvs[c] = dvc if dvs[c] is None else dvs[c] + dvc
            dq_ref[:, sl] = dq
        for (start, size), dvc in zip(chunks, dvs):
            dv_ref[start:start + size, :] += dvc
        finish()

    res = pl.pallas_call(
        body, name="attn_bwd", grid=(n // t, N_HEADS // 2, nq),
        in_specs=[qspec, kspec, vspec, ospec, ospec, ospec, _probs_spec(t, tq)] + c_in,
        out_specs=[qspec, kspec, vspec] + c_out,
        out_shape=[jax.ShapeDtypeStruct((n, QP_COLS), F32), jax.ShapeDtypeStruct((n, QP_COLS), F32),
                   jax.ShapeDtypeStruct((n, D_ATTN), F32)] + c_shape, scratch_shapes=c_sems,
        compiler_params=pltpu.CompilerParams(dimension_semantics=("arbitrary", "arbitrary", "arbitrary"),
                                             vmem_limit_bytes=VMEM_LIMIT),
    )(q, k, v, do, o, *probs, *srcs)
    return res[:3], res[3:]


SCAN_STEPS = 8


def _scan(chains, t):
    seg = t // 8
    rows = lax.broadcasted_iota(jnp.int32, (8, LANES), 0)

    def step(i, carry):
        carry = list(carry)
        for u in range(SCAN_STEPS):
            j = i * SCAN_STEPS + u
            for n, (a_ref, b_ref, h_ref, p_ref, reverse) in enumerate(chains):
                h, p = carry[n]
                idx = pl.ds(seg - 1 - j if reverse else j, 8, stride=seg)
                a = a_ref[idx, :]
                h = a * h + b_ref[idx, :]
                p = a * p
                h_ref[idx, :] = h
                p_ref[idx, :] = p
                carry[n] = (h, p)
        return tuple(carry)

    init = tuple((jnp.zeros((8, LANES), F32), jnp.ones((8, LANES), F32)) for _ in chains)
    ends = lax.fori_loop(0, seg // SCAN_STEPS, step, init)
    for (_, _, h_ref, p_ref, reverse), (b, a) in zip(chains, ends):
        for d in (1, 2, 4):
            if reverse:
                keep = rows < 8 - d
                a_n, b_n = pltpu.roll(a, 8 - d, 0), pltpu.roll(b, 8 - d, 0)
            else:
                keep = rows >= d
                a_n, b_n = pltpu.roll(a, d, 0), pltpu.roll(b, d, 0)
            b = a * jnp.where(keep, b_n, 0.0) + b
            a = a * jnp.where(keep, a_n, 1.0)
        for s in (range(7) if reverse else range(1, 8)):
            sl = slice(s * seg, (s + 1) * seg)
            carry_in = b[s + 1:s + 2, :] if reverse else b[s - 1:s, :]
            h_ref[sl, :] = h_ref[sl, :] + p_ref[sl, :] * carry_in


def _shift_rows(x, s, rows, t):
    if s == 0:
        return x
    rolled = pltpu.roll(x, s % t, 0)
    return jnp.where(rows >= s, rolled, 0.0) if s > 0 else jnp.where(rows < t + s, rolled, 0.0)


def _neg_expm1_twice(h, exp_2h):
    series = h * (-2.0 + h * (-2.0 + h * (-4.0 / 3 + h * (-2.0 / 3))))
    return jnp.where(h > -0.05, series, 1.0 - exp_2h)


def _sigmoid(x):
    return 0.5 * jnp.tanh(0.5 * x) + 0.5


def _gelu_parts(x):
    k = math.sqrt(2.0 / math.pi)
    th = jnp.tanh(k * (x + 0.044715 * x * x * x))
    g = 0.5 * x * (1.0 + th)
    dg = 0.5 * (1.0 + th) + 0.5 * x * (1.0 - th * th) * k * (1.0 + 3 * 0.044715 * x * x)
    return g, dg


def _lru_gates(xc, gates, lam_ref, valid, d):
    r = _sigmoid(gates[:, (2 * d) * LANES:(2 * d + 1) * LANES])
    i = _sigmoid(gates[:, (2 * d + 1) * LANES:(2 * d + 2) * LANES])
    neg_lam = -lam_ref[d:d + 1, :]
    sp = jnp.maximum(neg_lam, 0.0) + jnp.log1p(jnp.exp(-jnp.abs(neg_lam)))
    log_a = -LRU_C * r * sp
    a = jnp.exp(log_a)
    m = jnp.maximum(_neg_expm1_twice(log_a, a * a), 0.0)
    sq = jnp.sqrt(m)
    b = jnp.where(valid, sq * (i * xc), 0.0)
    return r, i, sp, a, m, sq, b


def _conv(xr, cw_ref, cb_ref, rows, t):
    return (cw_ref[0:1, :] * _shift_rows(xr, 2, rows, t) + cw_ref[1:2, :] * _shift_rows(xr, 1, rows, t)
            + cw_ref[2:3, :] * xr + cw_ref[3:4, :] * _shift_rows(xr, -1, rows, t) + cb_ref[...])


def _rnn_specs(t):
    seq = pl.BlockSpec((t, LANES), lambda cb, b: (b, cb))
    cw = pl.BlockSpec((4, LANES), lambda cb, b: (0, cb))
    vec1 = pl.BlockSpec((1, LANES), lambda cb, b: (0, cb))
    vec2 = pl.BlockSpec((2, LANES), lambda cb, b: (0, cb))
    wblk = pl.BlockSpec((1, LANES, 4 * LANES), lambda cb, b: (cb, 0, 0))
    gbias = pl.BlockSpec((1, 1, 4 * LANES), lambda cb, b: (cb, 0, 0))
    return seq, cw, vec1, vec2, wblk, gbias


def _rnn_fwd(xr, xg, conv_w, conv_b, wblk, gbias, lam):
    n = xr.shape[0]
    t = _t_pad()
    seq, cw, vec1, vec2, wspec, gspec = _rnn_specs(t)
    both = pl.BlockSpec((2, t, LANES), lambda cb, b: (0, b, cb))

    def body(xr_ref, xg_ref, cw_ref, cb_ref, w_ref, gb_ref, lam_ref,
             o_ref, xc_ref, r_ref, i_ref, q_ref, h_ref, a_s, b_s, p_s):
        rows = lax.broadcasted_iota(jnp.int32, (t, LANES), 0)
        valid = rows >= PAD_ROWS
        xc = _conv(xr_ref[...], cw_ref, cb_ref, rows, t)
        xc_ref[...] = xc
        gates = _dot(xc.astype(BF16), w_ref[0]) + gb_ref[0]
        for d in range(2):
            r_ref[d], i_ref[d], _, a_s[d], _, q_ref[d], b_s[d] = _lru_gates(xc, gates, lam_ref, valid, d)
        _scan([(a_s.at[d], b_s.at[d], h_ref.at[d], p_s.at[d], d == 1) for d in range(2)], t)
        g, _ = _gelu_parts(xg_ref[...])
        o_ref[...] = (h_ref[0] + h_ref[1]) * g

    stacked = jax.ShapeDtypeStruct((2, n, D_RNN), F32)
    res = pl.pallas_call(
        body, name="rnn_fwd", grid=(D_RNN // LANES, n // t),
        in_specs=[seq, seq, cw, vec1, wspec, gspec, vec2], out_specs=[seq, seq, both, both, both, both, both],
        out_shape=[jax.ShapeDtypeStruct((n, D_RNN), F32), jax.ShapeDtypeStruct((n, D_RNN), F32)] + [stacked] * 5,
        scratch_shapes=[pltpu.VMEM((2, t, LANES), F32)] * 2,
        compiler_params=pltpu.CompilerParams(dimension_semantics=("parallel", "parallel"), vmem_limit_bytes=VMEM_LIMIT),
    )(xr, xg, conv_w, conv_b, wblk, gbias, lam)
    return res[0], tuple(res[1:])


def _rnn_bwd(xr, xg, do, saved, conv_w, wblk, lam, srcs=(), scatter=()):
    n = xr.shape[0]
    t = _t_pad()
    seq, cw, vec1, vec2, wspec, gspec = _rnn_specs(t)
    nk = len(srcs)
    c_in, c_out, c_shape, c_sems = _exchange_specs(srcs, scatter)

    def body(xr_ref, xg_ref, do_ref, xc_ref, r_s, i_s, q_s, h_s, a_s, cw_ref, w_ref, lam_ref, *rest):
        dxr_ref, dxg_ref, dcw_ref, dcb_ref, dw_ref, dgb_ref, dlam_ref = rest[nk:nk + 7]
        b_s, l_s, p_s, back_s, dg_s = rest[2 * nk + 7 + len(c_sems):]
        finish = _ride(2, *_exchange_fns(rest[:nk], rest[nk + 7:2 * nk + 7], rest[2 * nk + 7:2 * nk + 7 + len(c_sems)],
                                         scatter))
        first = pl.program_id(1) == 0
        rows = lax.broadcasted_iota(jnp.int32, (t, LANES), 0)
        valid = rows >= PAD_ROWS
        xr = xr_ref[...]
        xc = xc_ref[...]
        xcb = xc.astype(BF16)
        g, dg = _gelu_parts(xg_ref[...])
        do = do_ref[...]
        dxg_ref[...] = do * (h_s[0] + h_s[1]) * dg
        b_s[...] = do * g
        sps = []
        for d in range(2):
            neg_lam = -lam_ref[d:d + 1, :]
            sps.append(jnp.maximum(neg_lam, 0.0) + jnp.log1p(jnp.exp(-jnp.abs(neg_lam))))
            back_s[d] = _shift_rows(a_s[d], -1 if d == 0 else 1, rows, t)
        _scan([(back_s.at[d], b_s, l_s.at[d], p_s.at[d], d == 0) for d in range(2)], t)
        dxc = jnp.zeros((t, LANES), F32)
        dlams = []
        for d in range(2):
            r, i, sp, a, sq = r_s[d], i_s[d], sps[d], a_s[d], q_s[d]
            lam_t = l_s[d]
            da = lam_t * _shift_rows(h_s[d], 1 if d == 0 else -1, rows, t)
            lam_v = jnp.where(valid, lam_t, 0.0)
            dsq = lam_v * (i * xc)
            di = lam_v * sq * xc
            dxc = dxc + lam_v * sq * i
            dm = jnp.where(sq > 0.0, dsq * 0.5 / jnp.where(sq > 0.0, sq, 1.0), 0.0)
            dla = da * a - 2.0 * dm * a * a
            dr = dla * (-LRU_C) * sp
            dsp = _colsum(dla * (-LRU_C) * r)
            dlams.append(dsp * -jax.nn.sigmoid(-lam_ref[d:d + 1, :]))
            dg_s[:, (2 * d) * LANES:(2 * d + 1) * LANES] = (dr * r * (1.0 - r)).astype(BF16)
            dg_s[:, (2 * d + 1) * LANES:(2 * d + 2) * LANES] = (di * i * (1.0 - i)).astype(BF16)
        dgates = dg_s[...]
        dxc = dxc + _dot_nt(dgates, w_ref[0])
        taps = [_shift_rows(dxc, j - 2, rows, t) for j in range(4)]
        dxr_ref[...] = (cw_ref[0:1, :] * taps[0] + cw_ref[1:2, :] * taps[1] + cw_ref[2:3, :] * taps[2]
                        + cw_ref[3:4, :] * taps[3])
        dcw = jnp.concatenate([_colsum(tap * xr) for tap in taps], axis=0)
        _acc(dcw_ref, first, dcw)
        _acc(dcb_ref, first, _colsum(dxc))
        _acc(dw_ref, first, _dot_tn(xcb, dgates)[None])
        _acc(dgb_ref, first, _colsum(dgates.astype(F32))[None])
        _acc(dlam_ref, first, jnp.concatenate(dlams, axis=0))
        finish()

    both = pl.BlockSpec((2, t, LANES), lambda cb, b: (0, b, cb))
    pair = pltpu.VMEM((2, t, LANES), F32)
    res = pl.pallas_call(
        body, name="rnn_bwd", grid=(D_RNN // LANES, n // t),
        in_specs=[seq, seq, seq, seq, both, both, both, both, both, cw, wspec, vec2] + c_in,
        out_specs=[seq, seq, cw, vec1, wspec, gspec, vec2] + c_out,
        out_shape=[jax.ShapeDtypeStruct((n, D_RNN), F32), jax.ShapeDtypeStruct((n, D_RNN), F32),
                   jax.ShapeDtypeStruct((4, D_RNN), F32), jax.ShapeDtypeStruct((1, D_RNN), F32),
                   jax.ShapeDtypeStruct((D_RNN // LANES, LANES, 4 * LANES), F32),
                   jax.ShapeDtypeStruct((D_RNN // LANES, 1, 4 * LANES), F32), jax.ShapeDtypeStruct((2, D_RNN), F32)]
        + c_shape,
        scratch_shapes=c_sems + [pltpu.VMEM((t, LANES), F32), pair, pair, pair, pltpu.VMEM((t, 4 * LANES), BF16)],
        compiler_params=pltpu.CompilerParams(dimension_semantics=("arbitrary", "arbitrary"), vmem_limit_bytes=VMEM_LIMIT),
    )(xr, xg, do, *saved, conv_w, wblk, lam, *srcs)
    return res[:7], res[7:]


def _post(oa, orn, h0, tgt, ga, gr, g2, w_out, w_gate, w_up, w_down):
    n = oa.shape[0]
    tm = _row_tile(n)
    t = _t_pad()
    head = PAD_ROWS + N_META
    parts = tm // head

    def body(oa_ref, or_ref, h0_ref, *rest):
        tgt_refs = rest[:parts]
        (ga_ref, gr_ref, g2_ref, wo_ref, wg_ref, wu_ref, wd_ref,
         doa_ref, dor_ref, dh1_ref, mix_ref, h1n_ref, act_ref, dgate_ref, dup_ref, dy_ref,
         loss_ref, dga_ref, dgr_ref, dg2_ref, gate_s, up_s) = rest[parts:]
        first = pl.program_id(0) == 0
        xa, ra = _rms(oa_ref[...], D_ATTN)
        xr, rr = _rms(or_ref[...], D_RNN)
        mix = jnp.concatenate([(xa * ga_ref[...]).astype(BF16), (xr * gr_ref[...]).astype(BF16)], axis=-1)
        mix_ref[...] = mix.T
        h1 = h0_ref[...] + _dot(mix, wo_ref[...])
        x2, r2 = _rms(h1, D_MODEL)
        h1n = (x2 * g2_ref[...]).astype(BF16)
        h1n_ref[...] = h1n
        y = h1
        for cs in range(0, D_FF, FF_CHUNK):
            sl = slice(cs, cs + FF_CHUNK)
            gate = _dot_nt(h1n, wg_ref[sl, :])
            up = _dot_nt(h1n, wu_ref[sl, :])
            gate_s[:, sl] = gate
            up_s[:, sl] = up
            act = (gate * _sigmoid(gate) * up).astype(BF16)
            act_ref[sl, :] = act.T
            y = y + _dot(act, wd_ref[sl, :])
        row = pl.program_id(0) * tm + lax.broadcasted_iota(jnp.int32, (tm, 1), 0)
        for _ in range(1, n // t):
            row = jnp.where(row >= t, row - t, row)
        tgt = jnp.concatenate([ref[0] for ref in tgt_refs], axis=0)
        err = jnp.where(row >= PAD_ROWS + N_META, y - tgt, 0.0)
        _acc(loss_ref, first, jnp.full((1, LANES), 0.5 / D_MODEL, F32) * jnp.sum(err * err))
        dy = err * (1.0 / D_MODEL)
        dyb = dy.astype(BF16)
        dy_ref[...] = dyb
        dh1n = jnp.zeros((tm, D_MODEL), F32)
        for cs in range(0, D_FF, FF_CHUNK):
            sl = slice(cs, cs + FF_CHUNK)
            dact = _dot_nt(dyb, wd_ref[sl, :])
            gate, up = gate_s[:, sl], up_s[:, sl]
            sg = _sigmoid(gate)
            dgate = (dact * up * sg * (1.0 + gate * (1.0 - sg))).astype(BF16)
            dup = (dact * gate * sg).astype(BF16)
            dgate_ref[sl, :] = dgate.T
            dup_ref[sl, :] = dup.T
            dh1n = dh1n + _dot(dgate, wg_ref[sl, :]) + _dot(dup, wu_ref[sl, :])
        _acc(dg2_ref, first, _colsum(dh1n * x2))
        dh1 = dy + _rms_bwd(dh1n, x2, r2, g2_ref[...], D_MODEL)
        dh1_ref[...] = dh1
        dmix = _dot_nt(dh1.astype(BF16), wo_ref[...])
        dma, dmr = dmix[:, :D_ATTN], dmix[:, D_ATTN:]
        _acc(dga_ref, first, _colsum(dma * xa))
        _acc(dgr_ref, first, _colsum(dmr * xr))
        doa_ref[...] = _rms_bwd(dma, xa, ra, ga_ref[...], D_ATTN)
        dor_ref[...] = _rms_bwd(dmr, xr, rr, gr_ref[...], D_RNN)

    def row(w):
        return pl.BlockSpec((tm, w), lambda i: (i, 0))

    def acc(w):
        return pl.BlockSpec((1, w), lambda i: (0, 0))

    def col(w):
        return pl.BlockSpec((w, tm), lambda i: (0, i))

    outs = [(D_ATTN, F32, row), (D_RNN, F32, row), (D_MODEL, F32, row), (D_MODEL, BF16, col), (D_MODEL, BF16, row),
            (D_FF, BF16, col), (D_FF, BF16, col), (D_FF, BF16, col), (D_MODEL, BF16, row)]
    accs = [LANES, D_ATTN, D_RNN, D_MODEL]
    per = t // head

    def target_part(p):
        def index(i):
            block = i * parts + p
            return block // per, jnp.maximum(block % per - 1, 0), 0
        return pl.BlockSpec((1, head, D_MODEL), index)

    return pl.pallas_call(
        body, name="post", grid=(n // tm,),
        in_specs=[row(D_ATTN), row(D_RNN), row(D_MODEL)] + [target_part(p) for p in range(parts)] + [
                  _const_spec((1, D_ATTN)), _const_spec((1, D_RNN)), _const_spec((1, D_MODEL)),
                  _const_spec((D_MODEL, D_MODEL)), _const_spec((D_FF, D_MODEL)), _const_spec((D_FF, D_MODEL)),
                  _const_spec((D_FF, D_MODEL))],
        out_specs=[spec(w) for w, _, spec in outs] + [acc(w) for w in accs],
        out_shape=[jax.ShapeDtypeStruct((n, w) if spec is row else (w, n), dt) for w, dt, spec in outs]
        + [jax.ShapeDtypeStruct((1, w), F32) for w in accs],
        scratch_shapes=[pltpu.VMEM((tm, D_FF), F32), pltpu.VMEM((tm, D_FF), F32)],
        compiler_params=pltpu.CompilerParams(dimension_semantics=("arbitrary",), vmem_limit_bytes=VMEM_LIMIT),
    )(oa, orn, h0, *[tgt] * parts, ga, gr, g2, w_out, w_gate, w_up, w_down)


def _in_bwd(dp, h0, dh1, ln1_g, w_in_p, srcs=(), scatter=()):
    n = h0.shape[0]
    tm = _row_tile(n)
    nk = len(srcs)
    c_in, c_out, c_shape, c_sems = _exchange_specs(srcs, scatter)

    def body(dp_ref, h0_ref, dh1_ref, g_ref, w_ref, *rest):
        dh0_ref, dg_ref = rest[nk:nk + 2]
        finish = _ride(1, *_exchange_fns(rest[:nk], rest[nk + 2:2 * nk + 2], rest[2 * nk + 2:], scatter))
        dhn = _dot(dp_ref[...], w_ref[...])
        xhat, r = _rms(h0_ref[...], D_MODEL)
        _acc(dg_ref, pl.program_id(0) == 0, _colsum(dhn * xhat))
        dh0_ref[...] = dh1_ref[...] + _rms_bwd(dhn, xhat, r, g_ref[...], D_MODEL)
        finish()

    def row(w):
        return pl.BlockSpec((tm, w), lambda i: (i, 0))

    res = pl.pallas_call(
        body, name="in_bwd", grid=(n // tm,),
        in_specs=[row(P_COLS), row(D_MODEL), row(D_MODEL), _const_spec((1, D_MODEL)), _const_spec((P_COLS, D_MODEL))] + c_in,
        out_specs=[row(D_MODEL), pl.BlockSpec((1, D_MODEL), lambda i: (0, 0))] + c_out,
        out_shape=[jax.ShapeDtypeStruct((n, D_MODEL), F32), jax.ShapeDtypeStruct((1, D_MODEL), F32)] + c_shape,
        scratch_shapes=c_sems,
        compiler_params=pltpu.CompilerParams(dimension_semantics=("arbitrary",), vmem_limit_bytes=VMEM_LIMIT),
    )(dp, h0, dh1, ln1_g, w_in_p, *srcs)
    return res[:2], res[2:]


MAX_TILE = D_FF // 2


def _pick_tile(width, cap):
    best = LANES
    for mult in range(1, width // LANES + 1):
        cand = mult * LANES
        if width % cand == 0 and cand <= cap:
            best = cand
    return best


def _matmul_tn(name, a, b, srcs=(), scatter=()):
    n, ka = a.shape
    kb = b.shape[1]
    ta, tb = _pick_tile(ka, MAX_TILE), _pick_tile(kb, MAX_TILE)
    tk = n // 2
    nk = len(srcs)
    c_in, c_out, c_shape, c_sems = _exchange_specs(srcs, scatter)

    def body(a_ref, b_ref, *rest):
        o_ref = rest[nk]
        finish = _ride(3, *_exchange_fns(rest[:nk], rest[nk + 1:2 * nk + 1], rest[2 * nk + 1:], scatter))
        _acc(o_ref, pl.program_id(2) == 0, _dot_tn(a_ref[...].astype(BF16), b_ref[...].astype(BF16)))
        finish()

    res = pl.pallas_call(
        body, name=name, grid=(ka // ta, kb // tb, n // tk),
        in_specs=[pl.BlockSpec((tk, ta), lambda i, j, k: (k, i)), pl.BlockSpec((tk, tb), lambda i, j, k: (k, j))] + c_in,
        out_specs=[pl.BlockSpec((ta, tb), lambda i, j, k: (i, j))] + c_out,
        out_shape=[jax.ShapeDtypeStruct((ka, kb), F32)] + c_shape, scratch_shapes=c_sems,
        compiler_params=pltpu.CompilerParams(dimension_semantics=("arbitrary", "arbitrary", "arbitrary"),
                                             vmem_limit_bytes=VMEM_LIMIT),
    )(a, b, *srcs)
    return res[0], res[1:]


def _matmul_shards(name, ats, b):
    count = len(ats)
    ka, n = ats[0].shape
    kb = b.shape[1]
    width = ka // N_DEV
    per = 2 if 2 * width >= 4 * LANES else 4
    ta = per * width
    steps = ka // ta

    def body(*refs):
        b_ref = refs[count]
        for c in range(count):
            @pl.when(pl.program_id(0) // steps == c)
            def _():
                out = _dot(refs[c][...], b_ref[...].astype(BF16))
                for s in range(per):
                    refs[count + 1 + c][s] = out[s * width:(s + 1) * width, :].astype(BF16)

    def block_of(c):
        return lambda i: (jnp.clip(i - c * steps, 0, steps - 1), 0)

    def shards_of(c):
        return lambda i: (jnp.clip(i - c * steps, 0, steps - 1), 0, 0)

    return pl.pallas_call(
        body, name=name, grid=(count * steps,),
        in_specs=[pl.BlockSpec((ta, n), block_of(c)) for c in range(count)] + [_const_spec((n, kb))],
        out_specs=[pl.BlockSpec((per, width, kb), shards_of(c)) for c in range(count)],
        out_shape=[jax.ShapeDtypeStruct((N_DEV, width, kb), BF16)] * count,
        compiler_params=pltpu.CompilerParams(dimension_semantics=("arbitrary",), vmem_limit_bytes=VMEM_LIMIT),
    )(*ats, b)


def _adamw_math(g8_ref, w_ref, m_ref, v_ref, g_ref, d_ref, nm_ref, nv_ref):
    g = g8_ref[0].astype(F32)
    for s in range(1, N_DEV):
        g = g + g8_ref[s].astype(F32)
    g_ref[...] = g
    nm = ADAM_B1 * m_ref[...] + (1.0 - ADAM_B1) * g
    nv = ADAM_B2 * v_ref[...] + (1.0 - ADAM_B2) * (g * g)
    nm_ref[...] = nm
    nv_ref[...] = nv
    m_hat = nm / (1.0 - ADAM_B1 ** ADAM_STEP)
    v_hat = nv / (1.0 - ADAM_B2 ** ADAM_STEP)
    d_ref[...] = -ADAM_LR * (m_hat / (jnp.sqrt(v_hat) + ADAM_EPS) + ADAM_WD * w_ref[...])


def _adamw_many(name, items):
    count = len(items)

    def body(*refs):
        ins, outs = refs[:4 * count], refs[4 * count:]
        for i in range(count):
            _adamw_math(*ins[4 * i:4 * i + 4], *outs[4 * i:4 * i + 4])

    flat = [a for item in items for a in item]
    res = pl.pallas_call(
        body, name=name,
        out_shape=[jax.ShapeDtypeStruct(item[1].shape, F32) for item in items for _ in range(4)],
        compiler_params=pltpu.CompilerParams(vmem_limit_bytes=VMEM_LIMIT),
    )(*flat)
    return [tuple(res[4 * i:4 * i + 4]) for i in range(count)]


def _adamw_group(name, items, srcs=(), scatter=()):
    count = len(items)
    rows, cols = items[0][1].shape
    tr = rows
    for cand in (176, 128, 64):
        if rows % cand == 0 and rows > cand:
            tr = cand
            break
    steps = rows // tr
    nk = len(srcs)
    c_in, c_out, c_shape, c_sems = _exchange_specs(srcs, scatter)

    def body(*refs):
        ins, rest = refs[:4 * count], refs[4 * count:]
        outs = rest[nk:nk + 4 * count]
        finish = _ride(1, *_exchange_fns(rest[:nk], rest[nk + 4 * count:2 * nk + 4 * count],
                                         rest[2 * nk + 4 * count:], scatter))
        for c in range(count):
            @pl.when(pl.program_id(0) // steps == c)
            def _():
                _adamw_math(*ins[4 * c:4 * c + 4], *outs[4 * c:4 * c + 4])
        finish()

    def blk(c):
        return pl.BlockSpec((tr, cols), lambda i: (jnp.clip(i - c * steps, 0, steps - 1), 0))

    def blk8(c):
        return pl.BlockSpec((N_DEV, tr, cols), lambda i: (0, jnp.clip(i - c * steps, 0, steps - 1), 0))

    res = pl.pallas_call(
        body, name=name, grid=(count * steps,),
        in_specs=[s for c in range(count) for s in (blk8(c), blk(c), blk(c), blk(c))] + c_in,
        out_specs=[blk(c) for c in range(count) for _ in range(4)] + c_out,
        out_shape=[jax.ShapeDtypeStruct((rows, cols), F32)] * (4 * count) + c_shape, scratch_shapes=c_sems,
        compiler_params=pltpu.CompilerParams(dimension_semantics=("arbitrary",), vmem_limit_bytes=VMEM_LIMIT),
    )(*[a for item in items for a in item], *srcs)
    return [tuple(res[4 * c:4 * c + 4]) for c in range(count)], res[4 * count:]


def _adamw(name, g8, w, m, v, srcs=(), scatter=()):
    rows, cols = w.shape
    tr = rows
    for cand in (256, 176, 128, 64):
        if rows % cand == 0 and rows > cand:
            tr = cand
            break
    nk = len(srcs)
    c_in, c_out, c_shape, c_sems = _exchange_specs(srcs, scatter)

    def body(g8_ref, w_ref, m_ref, v_ref, *rest):
        outs = rest[nk:nk + 4]
        finish = _ride(1, *_exchange_fns(rest[:nk], rest[nk + 4:2 * nk + 4], rest[2 * nk + 4:], scatter))
        _adamw_math(g8_ref, w_ref, m_ref, v_ref, *outs)
        finish()

    blk = pl.BlockSpec((tr, cols), lambda i: (i, 0))
    res = pl.pallas_call(
        body, name=name, grid=(rows // tr,),
        in_specs=[pl.BlockSpec((N_DEV, tr, cols), lambda i: (0, i, 0)), blk, blk, blk] + c_in,
        out_specs=[blk] * 4 + c_out, out_shape=[jax.ShapeDtypeStruct((rows, cols), F32)] * 4 + c_shape,
        scratch_shapes=c_sems,
        compiler_params=pltpu.CompilerParams(dimension_semantics=("arbitrary" if nk else "parallel",),
                                             vmem_limit_bytes=VMEM_LIMIT),
    )(g8, w, m, v, *srcs)
    return tuple(res[:4]), res[4:]


def _exchange_specs(srcs, scatter):
    nk = len(srcs)
    if not nk:
        return [], [], [], []
    any_spec = pl.BlockSpec(memory_space=pl.ANY)
    out_shape = [jax.ShapeDtypeStruct(s.shape if sc else (N_DEV,) + s.shape, s.dtype) for s, sc in zip(srcs, scatter)]
    sems = [pltpu.SemaphoreType.DMA((nk, N_DEV - 1)), pltpu.SemaphoreType.DMA((nk, N_DEV - 1)),
            pltpu.SemaphoreType.DMA((nk,))]
    return [any_spec] * nk, [any_spec] * nk, out_shape, sems


FLIPS = ((0, 0, 1), (1, 0, 0), (0, 1, 0), (1, 1, 0), (1, 0, 1), (0, 1, 1), (1, 1, 1))
N_CHIP_PEERS = 3


def _exchange_fns(src_refs, out_refs, sems, scatter):
    nk = len(src_refs)
    if not nk:
        return (lambda: None), (lambda: None), (lambda: None)
    send_sems, recv_sems, local_sems = sems
    first = 1 + N_CHIP_PEERS

    def plan():
        x, y, c = lax.axis_index("x"), lax.axis_index("y"), lax.axis_index("c")
        me = 4 * x + 2 * y + c
        peers = [(1 - x if fx else x, 1 - y if fy else y, 1 - c if fc else c) for fx, fy, fc in FLIPS]
        pids = [4 * px + 2 * py + pc for px, py, pc in peers]

        def remote(k, j, src, dst, to):
            return pltpu.make_async_remote_copy(src_ref=src, dst_ref=dst, send_sem=send_sems.at[k, j],
                                                recv_sem=recv_sems.at[k, j], device_id=to, device_id_type=MESH)

        def mine(k, dest):
            return src_refs[k].at[dest] if scatter[k] else src_refs[k]

        local = [pltpu.make_async_copy(mine(k, me), out_refs[k].at[me], local_sems.at[k]) for k in range(nk)]
        direct = [remote(k, j, mine(k, pids[j]), out_refs[k].at[me], peers[j])
                  for k in range(nk) for j in range(len(FLIPS) if scatter[k] else first)]
        relays = {(k, j): remote(k, j, out_refs[k].at[pids[j - N_CHIP_PEERS]], out_refs[k].at[pids[j - N_CHIP_PEERS]], peers[0])
                  for k in range(nk) if not scatter[k] for j in range(first, len(FLIPS))}
        arrivals = {(k, j): remote(k, j, out_refs[k].at[pids[j]], out_refs[k].at[pids[j]], peers[j])
                    for k in range(nk) for j in range(len(FLIPS))}
        return local, direct, relays, arrivals

    def start():
        local, direct, _, _ = plan()
        for cp in local + direct:
            cp.start()

    def relay():
        _, _, relays, arrivals = plan()
        for (k, j), cp in relays.items():
            arrivals[k, j - N_CHIP_PEERS].wait_recv()
            cp.start()

    def wait():
        local, direct, relays, arrivals = plan()
        for (k, j), cp in arrivals.items():
            if (k, j + N_CHIP_PEERS) not in relays:
                cp.wait_recv()
        for cp in direct + list(relays.values()):
            cp.wait_send()
        for cp in local:
            cp.wait()

    return start, relay, wait


def _grid_step(rank):
    step, total = 0, 1
    for axis in range(rank):
        step = step * pl.num_programs(axis) + pl.program_id(axis)
        total = total * pl.num_programs(axis)
    return step, total


def _ride(rank, start, relay, wait):
    step, total = _grid_step(rank)
    pl.when(step == 0)(start)
    pl.when(step == (3 * total) // 4)(relay)
    return lambda: pl.when(step == total - 1)(wait)


def _exchange(name, srcs, scatter):
    nk = len(srcs)
    c_in, c_out, c_shape, c_sems = _exchange_specs(srcs, scatter)

    def body(*refs):
        start, relay, wait = _exchange_fns(refs[:nk], refs[nk:2 * nk], refs[2 * nk:], scatter)
        start()
        relay()
        wait()

    return pl.pallas_call(body, name=name, in_specs=c_in, out_specs=c_out, out_shape=c_shape, scratch_shapes=c_sems)(*srcs)


def _cols_from_shards(g):
    return jnp.transpose(g, (1, 0, 2)).reshape(g.shape[1], -1)


def _cols_to_shards(w):
    return jnp.transpose(w.reshape(w.shape[0], N_DEV, -1), (1, 0, 2))


def _prep(x, srcs, scatter):
    nb = x.shape[0]
    t = _t_pad()
    head = PAD_ROWS + N_META
    nk = len(srcs)
    c_in, c_out, c_shape, c_sems = _exchange_specs(srcs, scatter)

    def body(x_ref, *rest):
        h0_ref = rest[nk]
        finish = _ride(1, *_exchange_fns(rest[:nk], rest[nk + 1:2 * nk + 1], rest[2 * nk + 1:], scatter))
        lead = pl.program_id(0) == 0

        @pl.when(lead)
        def _():
            h0_ref[...] = jnp.zeros_like(h0_ref)

        @pl.when(jnp.logical_not(lead))
        def _():
            h0_ref[...] = x_ref[...]

        finish()

    src = pl.BlockSpec((nb, head, D_MODEL), lambda j: (0, jnp.maximum(j - 1, 0), 0))
    dst = pl.BlockSpec((nb, head, D_MODEL), lambda j: (0, j, 0))
    res = pl.pallas_call(
        body, name="prep", grid=(t // head,), in_specs=[src] + c_in, out_specs=[dst] + c_out,
        out_shape=[jax.ShapeDtypeStruct((nb, t, D_MODEL), F32)] + c_shape, scratch_shapes=c_sems,
        compiler_params=pltpu.CompilerParams(dimension_semantics=("arbitrary",)),
    )(x, *srcs)
    return res[0], res[1:]


def _rope_tables(n):
    t = _t_pad()
    pos = np.arange(t, dtype=np.float32) - np.float32(PAD_ROWS)
    half = QK_ROPE // 2
    freqs = (1.0 / (ROPE_THETA ** (np.arange(half, dtype=np.float32) / half))).astype(np.float32)
    ang = pos[:, None] * freqs[None, :]
    cos, sin = np.cos(ang), np.sin(ang)
    z = lambda w: np.zeros((t, w), np.float32)
    c = np.concatenate([np.ones((t, QK_NOPE), np.float32), cos, cos, z(HEAD_PAD - QK_HEAD)], axis=1)
    s1 = np.concatenate([z(QK_NOPE + half), sin, z(HEAD_PAD - QK_HEAD)], axis=1)
    s2 = np.concatenate([z(QK_NOPE), -sin, z(HEAD_PAD - QK_NOPE - half)], axis=1)
    return tuple(jnp.asarray(np.tile(a, (n // t, 1))) for a in (c, s1, s2))


def _block_diag_gates(lru_wa, lru_wi):
    eye = jnp.eye(2, dtype=lru_wa.dtype)

    def bd(w):
        w = w.reshape(2, D_RNN // LANES, 2, RNN_BW, RNN_BW)
        full = w[:, :, :, :, None, :] * eye[None, None, :, None, :, None]
        return full.reshape(2, D_RNN // LANES, LANES, LANES)

    a, i = bd(lru_wa), bd(lru_wi)
    return jnp.concatenate([a[0], i[0], a[1], i[1]], axis=-1)


def _unblock_gates(dw):
    nb = D_RNN // LANES
    parts = dw.reshape(nb, 2, RNN_BW, 4, 2, RNN_BW)
    diag = jnp.stack([parts[:, k, :, :, k, :] for k in range(2)], axis=1)
    diag = jnp.transpose(diag, (3, 0, 1, 2, 4)).reshape(4, 2 * nb, RNN_BW, RNN_BW)
    return jnp.stack([diag[0], diag[2]]), jnp.stack([diag[1], diag[3]])


WEIGHTS = ("meta_tokens", "ln1_g", "w_in", "q_a_norm_g", "w_uq", "kv_a_norm_g", "w_ukv", "q_norm_g", "k_norm_g",
           "conv_w", "conv_b", "lru_wa", "lru_ba", "lru_wi", "lru_bi", "lru_lambda", "attn_out_g", "rnn_out_g",
           "w_out", "ln2_g", "w_gate", "w_up", "w_down")
BIG = ("w_in", "w_uq", "w_ukv", "w_out", "w_gate", "w_up", "w_down")
TRANSPOSED = ("w_in", "w_uq", "w_gate", "w_up")
ROW_SHARDED = ("w_out", "w_down") + TRANSPOSED
REPLICATED = ("ln1_g", "q_a_norm_g", "kv_a_norm_g", "q_norm_g", "k_norm_g", "conv_b", "lru_wa", "lru_wi",
              "attn_out_g", "rnn_out_g", "ln2_g")
WHOLE = REPLICATED + ("loss",)
G_FIRST = ("w_in", "meta_tokens")
G_MID = ("w_uq", "w_ukv", "conv_w", "lru_ba", "lru_bi", "lru_lambda")
LATE = ("w_out", "w_gate", "w_up", "w_down")
G_LAST = ("meta_tokens", "ln1_g")


def _local_step(x, tgt, ex):
    nb = x.shape[0]
    t = _t_pad()
    n = nb * t
    local = ex.local
    h0, got = _prep(x, *ex.gather_srcs(G_FIRST))
    first = ex.gathered(G_FIRST, got)
    meta, w_in = first["meta_tokens"], first["w_in"]
    h0 = h0.at[:, PAD_ROWS:PAD_ROWS + N_META].set(jnp.broadcast_to(meta[None], (nb, N_META, D_MODEL))).reshape(n, D_MODEL)

    zr = lambda r: jnp.zeros((r, D_MODEL), w_in.dtype)
    w_in_p = jnp.concatenate([w_in[:OFF_CKV], w_in[OFF_KR:], zr(QK_NOPE), w_in[OFF_CKV:OFF_KR], zr(HEAD_PAD - QK_HEAD)],
                             axis=0)
    pad_g = lambda g: jnp.pad(g, ((0, 0), (0, HEAD_PAD - QK_HEAD)))
    qg, kg = pad_g(local["q_norm_g"]), pad_g(local["k_norm_g"])
    rc, rs1, rs2 = _rope_tables(n)
    wblk = _block_diag_gates(local["lru_wa"].reshape(2, -1, RNN_BW, RNN_BW),
                             local["lru_wi"].reshape(2, -1, RNN_BW, RNN_BW)).astype(BF16)
    nblk = D_RNN // LANES

    (hn, cq, ckv, xr, xg, kr), got = _in_proj(h0, local["ln1_g"], w_in_p, *ex.gather_srcs(G_MID))
    w = ex.gathered(G_MID, got)
    w_uq_p = jnp.pad(w["w_uq"].reshape(N_HEADS, QK_HEAD, Q_LORA), ((0, 0), (0, HEAD_PAD - QK_HEAD), (0, 0))
                     ).reshape(QP_COLS, Q_LORA)
    ukv = w["w_ukv"].reshape(KV_LORA, N_HEADS, QK_NOPE + V_HEAD)
    w_uk_p = jnp.pad(ukv[:, :, :QK_NOPE], ((0, 0), (0, 0), (0, HEAD_PAD - QK_NOPE))).reshape(KV_LORA, QP_COLS)
    w_v = ukv[:, :, QK_NOPE:].reshape(KV_LORA, D_ATTN)
    gbias = jnp.stack([w["lru_ba"][0], w["lru_bi"][0], w["lru_ba"][1], w["lru_bi"][1]], axis=0)
    gbias = jnp.transpose(gbias.reshape(4, nblk, LANES), (1, 0, 2)).reshape(nblk, 1, 4 * LANES)

    q, k, v = _qkv_fwd(cq, ckv, kr, local["q_a_norm_g"], local["kv_a_norm_g"], w_uq_p, w_uk_p, w_v, qg, kg, rc, rs1, rs2)
    oa, probs, got = _attn_fwd(q, k, v, *ex.gather_srcs(LATE))
    late = ex.gathered(LATE, got)
    orn, rnn_saved = _rnn_fwd(xr, xg, w["conv_w"], local["conv_b"], wblk, gbias, w["lru_lambda"])
    (doa, dor, dh1, mix_t, h1n, act_t, dgate_t, dup_t, dyb, loss, dga, dgr, dg2) = _post(
        oa, orn, h0, tgt, local["attn_out_g"], local["rnn_out_g"], local["ln2_g"], late["w_out"], late["w_gate"],
        late["w_up"], late["w_down"])
    dw_gate, dw_up = _matmul_shards("dw_gate_up", [dgate_t, dup_t], h1n)
    wire = {"w_out": _matmul_shards("dw_out", [mix_t], dh1)[0], "w_gate": dw_gate, "w_up": dw_up,
            "w_down": _matmul_shards("dw_down", [act_t], dyb)[0]}
    names = ("w_gate",)
    (dxr, dxg, dcw, dcb, dwblk, dgb, dlam), got = _rnn_bwd(xr, xg, dor, rnn_saved, w["conv_w"], wblk, w["lru_lambda"],
                                                           *ex.scatter_srcs(names, wire))
    summed = ex.scattered(names, wire, got)
    dwa, dwi = _unblock_gates(dwblk)
    dgb = jnp.transpose(dgb.reshape(nblk, 4, LANES), (1, 0, 2)).reshape(4, D_RNN)
    names = ("w_out", "w_up", "w_down")
    (dq_r, dk_r, dv), got = _attn_bwd(q, k, v, doa, oa, probs, *ex.scatter_srcs(names, wire))
    summed.update(ex.scattered(names, wire, got))
    wire = ex.to_wire({
        "conv_w": dcw, "conv_b": dcb, "lru_wa": dwa.reshape(-1, RNN_BW), "lru_ba": jnp.stack([dgb[0], dgb[2]]),
        "lru_wi": dwi.reshape(-1, RNN_BW), "lru_bi": jnp.stack([dgb[1], dgb[3]]), "lru_lambda": dlam,
        "attn_out_g": dga, "rnn_out_g": dgr, "ln2_g": dg2, "loss": loss})
    names = tuple(wire)
    (dp, qa, kva, dqp, dkv, dqg, dkg, dgqa, dgkva), got = _qkv_bwd(
        cq, ckv, kr, dq_r, dk_r, dv, dxr, dxg, local["q_a_norm_g"], local["kv_a_norm_g"], w_uq_p, w_uk_p, w_v, qg, kg,
        rc, rs1, rs2, *ex.scatter_srcs(names, wire))
    summed.update(ex.scattered(names, wire, got))
    dw_uq_p, _ = _matmul_tn("dw_uq", dqp, qa)
    dw_kv, _ = _matmul_tn("dw_ukv", kva, dkv)
    dw_uq = dw_uq_p.reshape(N_HEADS, HEAD_PAD, Q_LORA)[:, :QK_HEAD].reshape(N_HEADS * QK_HEAD, Q_LORA)
    dw_ukv = jnp.concatenate([dw_kv[:, :QP_COLS].reshape(KV_LORA, N_HEADS, HEAD_PAD)[:, :, :QK_NOPE],
                              dw_kv[:, QP_COLS:].reshape(KV_LORA, N_HEADS, V_HEAD)], axis=2).reshape(KV_LORA, -1)
    wire = ex.to_wire({"q_a_norm_g": dgqa, "w_uq": dw_uq, "kv_a_norm_g": dgkva, "w_ukv": dw_ukv,
                       "q_norm_g": dqg[:, :QK_HEAD], "k_norm_g": dkg[:, :QK_HEAD]})
    names = tuple(wire)
    dw_in_p, got = _matmul_tn("dw_in", dp, hn, *ex.scatter_srcs(names, wire))
    summed.update(ex.scattered(names, wire, got))
    kr0 = OFF_CKV + 2 * D_RNN + QK_NOPE
    dw_in = jnp.concatenate([dw_in_p[:OFF_CKV], dw_in_p[kr0:kr0 + QK_ROPE], dw_in_p[OFF_CKV:OFF_CKV + 2 * D_RNN]], axis=0)
    wire = ex.to_wire({"w_in": dw_in})
    (dh0, dg1), got = _in_bwd(dp, h0, dh1, local["ln1_g"], w_in_p, *ex.scatter_srcs(("w_in",), wire))
    summed.update(ex.scattered(("w_in",), wire, got))

    dh0 = dh0.reshape(nb, t, D_MODEL)
    wire = ex.to_wire({"meta_tokens": jnp.sum(dh0[:, PAD_ROWS:PAD_ROWS + N_META], axis=0), "ln1_g": dg1})
    return dh0[:, PAD_ROWS + N_META:], summed, wire


class _MeshExchange:
    def __init__(self, shards):
        self.local = shards

    @staticmethod
    def run(name, srcs, scatter):
        return _exchange(name, srcs, scatter)

    def gather_srcs(self, names):
        return [self.local[k].astype(BF16) if k in BIG else self.local[k] for k in names], [False] * len(names)

    @staticmethod
    def gathered(names, outs):
        return {k: g.reshape(-1, g.shape[-1]) if k in ROW_SHARDED else _cols_from_shards(g) for k, g in zip(names, outs)}

    @staticmethod
    def to_wire(grads):
        wire = {}
        for k, g in grads.items():
            if k in WHOLE:
                wire[k] = g
            elif k in ROW_SHARDED:
                wire[k] = g.reshape(N_DEV, -1, g.shape[-1]).astype(BF16)
            else:
                wire[k] = _cols_to_shards(g).astype(BF16) if k in BIG else _cols_to_shards(g)
        return wire

    @staticmethod
    def scatter_srcs(names, wire):
        return [wire[k] for k in names], [k not in WHOLE for k in names]

    @staticmethod
    def scattered(names, wire, outs):
        return dict(zip(names, outs))


def kernel(x, meta_tokens, ln1_g, w_in, q_a_norm_g, w_uq, kv_a_norm_g, w_ukv, q_norm_g, k_norm_g, conv_w, conv_b, lru_wa, lru_ba, lru_wi, lru_bi, lru_lambda, attn_out_g, rnn_out_g, w_out, ln2_g, w_gate, w_up, w_down, loss_target, m_meta_tokens, m_ln1_g, m_w_in, m_q_a_norm_g, m_w_uq, m_kv_a_norm_g, m_w_ukv, m_q_norm_g, m_k_norm_g, m_conv_w, m_conv_b, m_lru_wa, m_lru_ba, m_lru_wi, m_lru_bi, m_lru_lambda, m_attn_out_g, m_rnn_out_g, m_w_out, m_ln2_g, m_w_gate, m_w_up, m_w_down, v_meta_tokens, v_ln1_g, v_w_in, v_q_a_norm_g, v_w_uq, v_kv_a_norm_g, v_w_ukv, v_q_norm_g, v_k_norm_g, v_conv_w, v_conv_b, v_lru_wa, v_lru_ba, v_lru_wi, v_lru_bi, v_lru_lambda, v_attn_out_g, v_rnn_out_g, v_w_out, v_ln2_g, v_w_gate, v_w_up, v_w_down):
    given = (meta_tokens, ln1_g, w_in, q_a_norm_g, w_uq, kv_a_norm_g, w_ukv, q_norm_g, k_norm_g, conv_w, conv_b,
             lru_wa, lru_ba, lru_wi, lru_bi, lru_lambda, attn_out_g, rnn_out_g, w_out, ln2_g, w_gate, w_up, w_down)
    moments_m = (m_meta_tokens, m_ln1_g, m_w_in, m_q_a_norm_g, m_w_uq, m_kv_a_norm_g, m_w_ukv, m_q_norm_g, m_k_norm_g,
                 m_conv_w, m_conv_b, m_lru_wa, m_lru_ba, m_lru_wi, m_lru_bi, m_lru_lambda, m_attn_out_g, m_rnn_out_g,
                 m_w_out, m_ln2_g, m_w_gate, m_w_up, m_w_down)
    moments_v = (v_meta_tokens, v_ln1_g, v_w_in, v_q_a_norm_g, v_w_uq, v_kv_a_norm_g, v_w_ukv, v_q_norm_g, v_k_norm_g,
                 v_conv_w, v_conv_b, v_lru_wa, v_lru_ba, v_lru_wi, v_lru_bi, v_lru_lambda, v_attn_out_g, v_rnn_out_g,
                 v_w_out, v_ln2_g, v_w_gate, v_w_up, v_w_down)
    shapes = {k: a.shape for k, a in zip(WEIGHTS, given)}

    def two_d(k, a):
        a = a.reshape(-1, a.shape[-1])
        return a.T if k in TRANSPOSED else a

    w = {k: two_d(k, a) for k, a in zip(WEIGHTS, given)}
    m = {k: two_d(k, a) for k, a in zip(WEIGHTS, moments_m)}
    v = {k: two_d(k, a) for k, a in zip(WEIGHTS, moments_v)}

    ex = _MeshExchange(w)
    grad_x, parts, wire = _local_step(x, loss_target, ex)

    tiled = ("w_in", "w_gate", "w_up", "w_down")
    mlp = ("w_gate", "w_up", "w_down")
    res, got = _adamw_group("adamw_mlp", [(parts[k], w[k], m[k], v[k]) for k in mlp], *ex.scatter_srcs(G_LAST, wire))
    parts.update(ex.scattered(G_LAST, wire, got))
    new = dict(zip(mlp, res))
    new["w_in"], _ = _adamw("adamw_w_in", parts["w_in"], w["w_in"], m["w_in"], v["w_in"])
    small = [k for k in WEIGHTS if k not in tiled]
    new.update(zip(small, _adamw_many("adamw_small", [(parts[k], w[k], m[k], v[k]) for k in small])))

    loss = jnp.sum(parts["loss"][:, 0, 0])
    outs = [loss, grad_x]
    for idx in range(4):
        outs += [(new[k][idx].T if k in TRANSPOSED else new[k][idx]).reshape(shapes[k]) for k in WEIGHTS]
    return tuple(outs)
```
